```python
import math
import jax, jax.numpy as jnp
from jax import lax
import numpy as np

D_MODEL = 1024
BATCH = 8
SEQ = 4096
DEPTH = 1

CHUNK = 64
MIX_W = D_MODEL
FOX_W = MIX_W // 2
LRU_W = MIX_W - FOX_W
FOX_HEADS = 8
FOX_HD = FOX_W // FOX_HEADS
LRU_BLOCKS = 8
LRU_BW = LRU_W // LRU_BLOCKS
LRU_C = 8.0
CONV_K = 4
D_FF = 4 * D_MODEL
Q_BLOCK = 128
LN_EPS = 1e-5
DN_ALPHA = (2.0 * DEPTH) ** 0.25
DN_BETA = (8.0 * DEPTH) ** -0.25

Q_OFF = 0
K_OFF = Q_OFF + FOX_W
V_OFF = K_OFF + FOX_W
LX_OFF = V_OFF + FOX_W
LG_OFF = LX_OFF + LRU_W
FG_OFF = LG_OFF + LRU_W
IN_COLS = FG_OFF + FOX_HEADS

kernel_name = "fox_rglru_macaron_deepnorm_block"


def layer_norm(x, g, b):
    xf = x.astype(jnp.float32)
    mu = jnp.mean(xf, axis=-1, keepdims=True)
    var = jnp.mean(jnp.square(xf - mu), axis=-1, keepdims=True)
    y = (xf - mu) * lax.rsqrt(var + LN_EPS) * g.astype(jnp.float32) + b.astype(jnp.float32)
    return y.astype(x.dtype)


def swiglu(x, w_gate, w_up, w_down):
    return (jax.nn.silu(x @ w_gate) * (x @ w_up)) @ w_down


def forgetting_attention(q, k, v, fg_logit):
    seq = q.shape[1]
    scale = 1.0 / math.sqrt(FOX_HD)
    cum = jnp.cumsum(jax.nn.log_sigmoid(fg_logit.astype(jnp.float32)), axis=1)
    cum = cum.transpose(0, 2, 1)
    qh = q.transpose(0, 2, 1, 3)
    kh = k.transpose(0, 2, 1, 3)
    vh = v.transpose(0, 2, 1, 3)
    outs = []
    for i in range(seq // Q_BLOCK):
        q0, q1 = i * Q_BLOCK, (i + 1) * Q_BLOCK
        s = jnp.einsum('bhqd,bhkd->bhqk', qh[:, :, q0:q1], kh[:, :, :q1],
                       preferred_element_type=jnp.float32) * scale
        s = s + cum[:, :, q0:q1, None] - cum[:, :, None, :q1]
        mask = jnp.arange(q0, q1)[:, None] >= jnp.arange(q1)[None, :]
        s = jnp.where(mask, s, -1e30)
        p = jax.nn.softmax(s, axis=-1).astype(vh.dtype)
        outs.append(jnp.einsum('bhqk,bhkd->bqhd', p, vh[:, :, :q1]))
    return jnp.concatenate(outs, axis=1)


def causal_depthwise_conv(u, w, b):
    y = lax.conv_general_dilated(u, w[:, None, :], window_strides=(1,), padding=[(CONV_K - 1, 0)],
                                 dimension_numbers=('NWC', 'WIO', 'NWC'),
                                 feature_group_count=u.shape[-1])
    return y + b


def _lin_rec_combine(c1, c2):
    a1, b1 = c1
    a2, b2 = c2
    return a1 * a2, a2 * b1 + b2


def rg_lru(u, wa, ba, wx, bx, lam):
    bsz, seq, width = u.shape
    ub = u.reshape(bsz, seq, LRU_BLOCKS, LRU_BW)
    r = jax.nn.sigmoid(jnp.einsum('bshi,hij->bshj', ub, wa) + ba).reshape(bsz, seq, width)
    gi = jax.nn.sigmoid(jnp.einsum('bshi,hij->bshj', ub, wx) + bx).reshape(bsz, seq, width)
    log_a = -LRU_C * r.astype(jnp.float32) * jax.nn.softplus(-lam.astype(jnp.float32))
    a = jnp.exp(log_a)
    bterm = jnp.sqrt(-jnp.expm1(2.0 * log_a)) * (gi * u).astype(jnp.float32)
    _, h = lax.associative_scan(_lin_rec_combine, (a, bterm), axis=1)
    return h.astype(u.dtype)


def _fwd_setup_inputs(seed: int = 0) -> dict:
    key = jax.random.key(seed)
    ks = iter(jax.random.split(key, 32))
    f32 = jnp.float32

    def nrm(shape, scale):
        return jax.random.normal(next(ks), shape, f32) * scale

    d_in, d_ff = D_MODEL ** -0.5, D_FF ** -0.5
    x = jax.random.normal(next(ks), (BATCH, SEQ, D_MODEL), f32)
    w_in = nrm((DEPTH, D_MODEL, IN_COLS), d_in)
    w_in = w_in.at[:, :, V_OFF:V_OFF + FOX_W].multiply(DN_BETA)
    a0 = jax.random.uniform(next(ks), (DEPTH, LRU_W), f32, 0.9, 0.999)
    p = a0 ** (1.0 / LRU_C)
    lru_lambda = jnp.log(p) - jnp.log1p(-p)
    return {
        "x": x,
        "ffn1_w_gate": nrm((DEPTH, D_MODEL, D_FF), d_in),
        "ffn1_w_up": nrm((DEPTH, D_MODEL, D_FF), d_in),
        "ffn1_w_down": nrm((DEPTH, D_FF, D_MODEL), d_ff * DN_BETA),
        "ln1_g": 1.0 + nrm((DEPTH, D_MODEL), 0.02),
        "ln1_b": nrm((DEPTH, D_MODEL), 0.02),
        "w_in": w_in,
        "b_forget": 3.0 + nrm((DEPTH, FOX_HEADS), 0.1),
        "conv_w": nrm((DEPTH, CONV_K, LRU_W), CONV_K ** -0.5),
        "conv_b": nrm((DEPTH, LRU_W), 0.02),
        "rg_wa": nrm((DEPTH, LRU_BLOCKS, LRU_BW, LRU_BW), LRU_BW ** -0.5),
        "rg_ba": nrm((DEPTH, LRU_BLOCKS, LRU_BW), 0.02),
        "rg_wx": nrm((DEPTH, LRU_BLOCKS, LRU_BW, LRU_BW), LRU_BW ** -0.5),
        "rg_bx": nrm((DEPTH, LRU_BLOCKS, LRU_BW), 0.02),
        "lru_lambda": lru_lambda,
        "w_out": nrm((DEPTH, MIX_W, D_MODEL), (MIX_W ** -0.5) * DN_BETA),
        "ln2_g": 1.0 + nrm((DEPTH, D_MODEL), 0.02),
        "ln2_b": nrm((DEPTH, D_MODEL), 0.02),
        "ffn2_w_gate": nrm((DEPTH, D_MODEL, D_FF), d_in),
        "ffn2_w_up": nrm((DEPTH, D_MODEL, D_FF), d_in),
        "ffn2_w_down": nrm((DEPTH, D_FF, D_MODEL), d_ff * DN_BETA),
        "ln3_g": 1.0 + nrm((DEPTH, D_MODEL), 0.02),
        "ln3_b": nrm((DEPTH, D_MODEL), 0.02),
    }


def _fwd_reference(x, ffn1_w_gate, ffn1_w_up, ffn1_w_down, ln1_g, ln1_b, w_in, b_forget,
              conv_w, conv_b, rg_wa, rg_ba, rg_wx, rg_bx, lru_lambda, w_out,
              ln2_g, ln2_b, ffn2_w_gate, ffn2_w_up, ffn2_w_down, ln3_g, ln3_b):
    bsz, seq, _ = x.shape
    for l in range(DEPTH):
        x = layer_norm(DN_ALPHA * x + 0.5 * swiglu(x, ffn1_w_gate[l], ffn1_w_up[l], ffn1_w_down[l]),
                       ln1_g[l], ln1_b[l])
        z = x @ w_in[l]
        q = z[..., Q_OFF:Q_OFF + FOX_W].reshape(bsz, seq, FOX_HEADS, FOX_HD)
        k = z[..., K_OFF:K_OFF + FOX_W].reshape(bsz, seq, FOX_HEADS, FOX_HD)
        v = z[..., V_OFF:V_OFF + FOX_W].reshape(bsz, seq, FOX_HEADS, FOX_HD)
        fg = z[..., FG_OFF:FG_OFF + FOX_HEADS] + b_forget[l]
        fox = forgetting_attention(q, k, v, fg).reshape(bsz, seq, FOX_W)
        u = causal_depthwise_conv(z[..., LX_OFF:LX_OFF + LRU_W], conv_w[l], conv_b[l])
        rec = rg_lru(u, rg_wa[l], rg_ba[l], rg_wx[l], rg_bx[l], lru_lambda[l])
        lru = jax.nn.gelu(z[..., LG_OFF:LG_OFF + LRU_W]) * rec
        mix = jnp.concatenate([fox, lru], axis=-1) @ w_out[l]
        x = layer_norm(DN_ALPHA * x + mix, ln2_g[l], ln2_b[l])
        x = layer_norm(DN_ALPHA * x + 0.5 * swiglu(x, ffn2_w_gate[l], ffn2_w_up[l], ffn2_w_down[l]),
                       ln3_g[l], ln3_b[l])
    return x


import jax as _jax
import jax.numpy as _jnp

TWIN_FORMAT = 'train_step'
FWD_PARAMS = ['x', 'ffn1_w_gate', 'ffn1_w_up', 'ffn1_w_down', 'ln1_g', 'ln1_b', 'w_in', 'b_forget', 'conv_w', 'conv_b', 'rg_wa', 'rg_ba', 'rg_wx', 'rg_bx', 'lru_lambda', 'w_out', 'ln2_g', 'ln2_b', 'ffn2_w_gate', 'ffn2_w_up', 'ffn2_w_down', 'ln3_g', 'ln3_b']
TWIN_WEIGHTS = ['ffn1_w_gate', 'ffn1_w_up', 'ffn1_w_down', 'ln1_g', 'ln1_b', 'w_in', 'b_forget', 'conv_w', 'conv_b', 'rg_wa', 'rg_ba', 'rg_wx', 'rg_bx', 'lru_lambda', 'w_out', 'ln2_g', 'ln2_b', 'ffn2_w_gate', 'ffn2_w_up', 'ffn2_w_down', 'ln3_g', 'ln3_b']
TWIN_DIFF_INPUT = 'x'
TWIN_INPUTS = ['x', 'ffn1_w_gate', 'ffn1_w_up', 'ffn1_w_down', 'ln1_g', 'ln1_b', 'w_in', 'b_forget', 'conv_w', 'conv_b', 'rg_wa', 'rg_ba', 'rg_wx', 'rg_bx', 'lru_lambda', 'w_out', 'ln2_g', 'ln2_b', 'ffn2_w_gate', 'ffn2_w_up', 'ffn2_w_down', 'ln3_g', 'ln3_b', 'loss_target', 'm_ffn1_w_gate', 'm_ffn1_w_up', 'm_ffn1_w_down', 'm_ln1_g', 'm_ln1_b', 'm_w_in', 'm_b_forget', 'm_conv_w', 'm_conv_b', 'm_rg_wa', 'm_rg_ba', 'm_rg_wx', 'm_rg_bx', 'm_lru_lambda', 'm_w_out', 'm_ln2_g', 'm_ln2_b', 'm_ffn2_w_gate', 'm_ffn2_w_up', 'm_ffn2_w_down', 'm_ln3_g', 'm_ln3_b', 'v_ffn1_w_gate', 'v_ffn1_w_up', 'v_ffn1_w_down', 'v_ln1_g', 'v_ln1_b', 'v_w_in', 'v_b_forget', 'v_conv_w', 'v_conv_b', 'v_rg_wa', 'v_rg_ba', 'v_rg_wx', 'v_rg_bx', 'v_lru_lambda', 'v_w_out', 'v_ln2_g', 'v_ln2_b', 'v_ffn2_w_gate', 'v_ffn2_w_up', 'v_ffn2_w_down', 'v_ln3_g', 'v_ln3_b']
TWIN_OUTPUTS = ['loss', 'grad_x', 'grad_ffn1_w_gate', 'grad_ffn1_w_up', 'grad_ffn1_w_down', 'grad_ln1_g', 'grad_ln1_b', 'grad_w_in', 'grad_b_forget', 'grad_conv_w', 'grad_conv_b', 'grad_rg_wa', 'grad_rg_ba', 'grad_rg_wx', 'grad_rg_bx', 'grad_lru_lambda', 'grad_w_out', 'grad_ln2_g', 'grad_ln2_b', 'grad_ffn2_w_gate', 'grad_ffn2_w_up', 'grad_ffn2_w_down', 'grad_ln3_g', 'grad_ln3_b', 'delta_ffn1_w_gate', 'delta_ffn1_w_up', 'delta_ffn1_w_down', 'delta_ln1_g', 'delta_ln1_b', 'delta_w_in', 'delta_b_forget', 'delta_conv_w', 'delta_conv_b', 'delta_rg_wa', 'delta_rg_ba', 'delta_rg_wx', 'delta_rg_bx', 'delta_lru_lambda', 'delta_w_out', 'delta_ln2_g', 'delta_ln2_b', 'delta_ffn2_w_gate', 'delta_ffn2_w_up', 'delta_ffn2_w_down', 'delta_ln3_g', 'delta_ln3_b', 'new_m_ffn1_w_gate', 'new_m_ffn1_w_up', 'new_m_ffn1_w_down', 'new_m_ln1_g', 'new_m_ln1_b', 'new_m_w_in', 'new_m_b_forget', 'new_m_conv_w', 'new_m_conv_b', 'new_m_rg_wa', 'new_m_rg_ba', 'new_m_rg_wx', 'new_m_rg_bx', 'new_m_lru_lambda', 'new_m_w_out', 'new_m_ln2_g', 'new_m_ln2_b', 'new_m_ffn2_w_gate', 'new_m_ffn2_w_up', 'new_m_ffn2_w_down', 'new_m_ln3_g', 'new_m_ln3_b', 'new_v_ffn1_w_gate', 'new_v_ffn1_w_up', 'new_v_ffn1_w_down', 'new_v_ln1_g', 'new_v_ln1_b', 'new_v_w_in', 'new_v_b_forget', 'new_v_conv_w', 'new_v_conv_b', 'new_v_rg_wa', 'new_v_rg_ba', 'new_v_rg_wx', 'new_v_rg_bx', 'new_v_lru_lambda', 'new_v_w_out', 'new_v_ln2_g', 'new_v_ln2_b', 'new_v_ffn2_w_gate', 'new_v_ffn2_w_up', 'new_v_ffn2_w_down', 'new_v_ln3_g', 'new_v_ln3_b']
TWIN_LEAF_KINDS = {'loss': 'loss', 'grad_x': 'grad_x', 'grad_ffn1_w_gate': 'grad_w', 'grad_ffn1_w_up': 'grad_w', 'grad_ffn1_w_down': 'grad_w', 'grad_ln1_g': 'grad_w', 'grad_ln1_b': 'grad_w', 'grad_w_in': 'grad_w', 'grad_b_forget': 'grad_w', 'grad_conv_w': 'grad_w', 'grad_conv_b': 'grad_w', 'grad_rg_wa': 'grad_w', 'grad_rg_ba': 'grad_w', 'grad_rg_wx': 'grad_w', 'grad_rg_bx': 'grad_w', 'grad_lru_lambda': 'grad_w', 'grad_w_out': 'grad_w', 'grad_ln2_g': 'grad_w', 'grad_ln2_b': 'grad_w', 'grad_ffn2_w_gate': 'grad_w', 'grad_ffn2_w_up': 'grad_w', 'grad_ffn2_w_down': 'grad_w', 'grad_ln3_g': 'grad_w', 'grad_ln3_b': 'grad_w', 'delta_ffn1_w_gate': 'delta_w', 'delta_ffn1_w_up': 'delta_w', 'delta_ffn1_w_down': 'delta_w', 'delta_ln1_g': 'delta_w', 'delta_ln1_b': 'delta_w', 'delta_w_in': 'delta_w', 'delta_b_forget': 'delta_w', 'delta_conv_w': 'delta_w', 'delta_conv_b': 'delta_w', 'delta_rg_wa': 'delta_w', 'delta_rg_ba': 'delta_w', 'delta_rg_wx': 'delta_w', 'delta_rg_bx': 'delta_w', 'delta_lru_lambda': 'delta_w', 'delta_w_out': 'delta_w', 'delta_ln2_g': 'delta_w', 'delta_ln2_b': 'delta_w', 'delta_ffn2_w_gate': 'delta_w', 'delta_ffn2_w_up': 'delta_w', 'delta_ffn2_w_down': 'delta_w', 'delta_ln3_g': 'delta_w', 'delta_ln3_b': 'delta_w', 'new_m_ffn1_w_gate': 'new_m', 'new_m_ffn1_w_up': 'new_m', 'new_m_ffn1_w_down': 'new_m', 'new_m_ln1_g': 'new_m', 'new_m_ln1_b': 'new_m', 'new_m_w_in': 'new_m', 'new_m_b_forget': 'new_m', 'new_m_conv_w': 'new_m', 'new_m_conv_b': 'new_m', 'new_m_rg_wa': 'new_m', 'new_m_rg_ba': 'new_m', 'new_m_rg_wx': 'new_m', 'new_m_rg_bx': 'new_m', 'new_m_lru_lambda': 'new_m', 'new_m_w_out': 'new_m', 'new_m_ln2_g': 'new_m', 'new_m_ln2_b': 'new_m', 'new_m_ffn2_w_gate': 'new_m', 'new_m_ffn2_w_up': 'new_m', 'new_m_ffn2_w_down': 'new_m', 'new_m_ln3_g': 'new_m', 'new_m_ln3_b': 'new_m', 'new_v_ffn1_w_gate': 'new_v', 'new_v_ffn1_w_up': 'new_v', 'new_v_ffn1_w_down': 'new_v', 'new_v_ln1_g': 'new_v', 'new_v_ln1_b': 'new_v', 'new_v_w_in': 'new_v', 'new_v_b_forget': 'new_v', 'new_v_conv_w': 'new_v', 'new_v_conv_b': 'new_v', 'new_v_rg_wa': 'new_v', 'new_v_rg_ba': 'new_v', 'new_v_rg_wx': 'new_v', 'new_v_rg_bx': 'new_v', 'new_v_lru_lambda': 'new_v', 'new_v_w_out': 'new_v', 'new_v_ln2_g': 'new_v', 'new_v_ln2_b': 'new_v', 'new_v_ffn2_w_gate': 'new_v', 'new_v_ffn2_w_up': 'new_v', 'new_v_ffn2_w_down': 'new_v', 'new_v_ln3_g': 'new_v', 'new_v_ln3_b': 'new_v'}


def _forward(args):
    return _fwd_reference(*[args[k] for k in FWD_PARAMS])


def _output_shape():
    out = _jax.eval_shape(lambda: _forward(_fwd_setup_inputs(0)))
    return out.shape, out.dtype

N_MICROBATCH = 1
ADAM_LR = 0.001
ADAM_B1 = 0.9
ADAM_B2 = 0.999
ADAM_EPS = 1e-08
ADAM_WD = 0.01
ADAM_STEP = 10
PER_EXAMPLE_BATCH_AXIS = {'x': 0, 'loss_target': 0}
SHARED_INPUTS = []
_WEIGHT_DTYPES = {'ffn1_w_gate': _jnp.float32, 'ffn1_w_up': _jnp.float32, 'ffn1_w_down': _jnp.float32, 'ln1_g': _jnp.float32, 'ln1_b': _jnp.float32, 'w_in': _jnp.float32, 'b_forget': _jnp.float32, 'conv_w': _jnp.float32, 'conv_b': _jnp.float32, 'rg_wa': _jnp.float32, 'rg_ba': _jnp.float32, 'rg_wx': _jnp.float32, 'rg_bx': _jnp.float32, 'lru_lambda': _jnp.float32, 'w_out': _jnp.float32, 'ln2_g': _jnp.float32, 'ln2_b': _jnp.float32, 'ffn2_w_gate': _jnp.float32, 'ffn2_w_up': _jnp.float32, 'ffn2_w_down': _jnp.float32, 'ln3_g': _jnp.float32, 'ln3_b': _jnp.float32}
MOMENT_SCALE = {'ffn1_w_gate': 1.384486e-02, 'ffn1_w_up': 1.343928e-02, 'ffn1_w_down': 4.519867e-02, 'ln1_g': 9.881288e-01, 'ln1_b': 6.475922e-01, 'w_in': 3.180859e-02, 'b_forget': 5.310360e-02, 'conv_w': 5.057978e-02, 'conv_b': 5.323688e-01, 'rg_wa': 1.760342e-02, 'rg_ba': 1.731445e-02, 'rg_wx': 3.219249e-02, 'rg_bx': 1.365626e-02, 'lru_lambda': 3.206210e-02, 'w_out': 6.716455e-02, 'ln2_g': 1.055128e+00, 'ln2_b': 4.540827e-01, 'ffn2_w_gate': 1.351676e-02, 'ffn2_w_up': 1.307578e-02, 'ffn2_w_down': 4.400092e-02, 'ln3_g': 3.205627e+01, 'ln3_b': 1.674518e+00}


def _to_microbatches(a, axis):
    t = _jnp.moveaxis(a, axis, 0)
    t = t.reshape((N_MICROBATCH, t.shape[0] // N_MICROBATCH) + t.shape[1:])
    return _jnp.moveaxis(t, 1, axis + 1)


def setup_inputs(seed: int = 0) -> dict:
    inp = _fwd_setup_inputs(seed)
    key = _jax.random.fold_in(_jax.random.key(seed), 7919)
    shape, _ = _output_shape()
    out = dict(inp)
    out["loss_target"] = _jax.random.normal(_jax.random.fold_in(key, 0), shape, _jnp.float32)
    for i, name in enumerate(TWIN_WEIGHTS):
        w = inp[name].astype(_jnp.float32)
        if MOMENT_SCALE is None:
            s = _jnp.sqrt(_jnp.mean(_jnp.square(w)) + 1e-30)
        else:
            s = MOMENT_SCALE[name]
        km, kv = _jax.random.split(_jax.random.fold_in(key, i + 1))
        out[name] = w
        out["m_" + name] = s * _jax.random.normal(km, w.shape, _jnp.float32)
        out["v_" + name] = (s * s) * _jax.random.uniform(kv, w.shape, _jnp.float32, 0.5, 1.5)
    if N_MICROBATCH > 1:
        for name, axis in PER_EXAMPLE_BATCH_AXIS.items():
            out[name] = _to_microbatches(out[name], axis)
    return {'x': out['x'], 'ffn1_w_gate': out['ffn1_w_gate'], 'ffn1_w_up': out['ffn1_w_up'], 'ffn1_w_down': out['ffn1_w_down'], 'ln1_g': out['ln1_g'], 'ln1_b': out['ln1_b'], 'w_in': out['w_in'], 'b_forget': out['b_forget'], 'conv_w': out['conv_w'], 'conv_b': out['conv_b'], 'rg_wa': out['rg_wa'], 'rg_ba': out['rg_ba'], 'rg_wx': out['rg_wx'], 'rg_bx': out['rg_bx'], 'lru_lambda': out['lru_lambda'], 'w_out': out['w_out'], 'ln2_g': out['ln2_g'], 'ln2_b': out['ln2_b'], 'ffn2_w_gate': out['ffn2_w_gate'], 'ffn2_w_up': out['ffn2_w_up'], 'ffn2_w_down': out['ffn2_w_down'], 'ln3_g': out['ln3_g'], 'ln3_b': out['ln3_b'], 'loss_target': out['loss_target'], 'm_ffn1_w_gate': out['m_ffn1_w_gate'], 'm_ffn1_w_up': out['m_ffn1_w_up'], 'm_ffn1_w_down': out['m_ffn1_w_down'], 'm_ln1_g': out['m_ln1_g'], 'm_ln1_b': out['m_ln1_b'], 'm_w_in': out['m_w_in'], 'm_b_forget': out['m_b_forget'], 'm_conv_w': out['m_conv_w'], 'm_conv_b': out['m_conv_b'], 'm_rg_wa': out['m_rg_wa'], 'm_rg_ba': out['m_rg_ba'], 'm_rg_wx': out['m_rg_wx'], 'm_rg_bx': out['m_rg_bx'], 'm_lru_lambda': out['m_lru_lambda'], 'm_w_out': out['m_w_out'], 'm_ln2_g': out['m_ln2_g'], 'm_ln2_b': out['m_ln2_b'], 'm_ffn2_w_gate': out['m_ffn2_w_gate'], 'm_ffn2_w_up': out['m_ffn2_w_up'], 'm_ffn2_w_down': out['m_ffn2_w_down'], 'm_ln3_g': out['m_ln3_g'], 'm_ln3_b': out['m_ln3_b'], 'v_ffn1_w_gate': out['v_ffn1_w_gate'], 'v_ffn1_w_up': out['v_ffn1_w_up'], 'v_ffn1_w_down': out['v_ffn1_w_down'], 'v_ln1_g': out['v_ln1_g'], 'v_ln1_b': out['v_ln1_b'], 'v_w_in': out['v_w_in'], 'v_b_forget': out['v_b_forget'], 'v_conv_w': out['v_conv_w'], 'v_conv_b': out['v_conv_b'], 'v_rg_wa': out['v_rg_wa'], 'v_rg_ba': out['v_rg_ba'], 'v_rg_wx': out['v_rg_wx'], 'v_rg_bx': out['v_rg_bx'], 'v_lru_lambda': out['v_lru_lambda'], 'v_w_out': out['v_w_out'], 'v_ln2_g': out['v_ln2_g'], 'v_ln2_b': out['v_ln2_b'], 'v_ffn2_w_gate': out['v_ffn2_w_gate'], 'v_ffn2_w_up': out['v_ffn2_w_up'], 'v_ffn2_w_down': out['v_ffn2_w_down'], 'v_ln3_g': out['v_ln3_g'], 'v_ln3_b': out['v_ln3_b']}


def _loss(weights, diff, rest, loss_target):
    with _jax.named_scope("forward"):
        args = {**rest, TWIN_DIFF_INPUT: diff, **{k: w.astype(_WEIGHT_DTYPES[k]) for k, w in weights.items()}}
        y = _forward(args)
    with _jax.named_scope("loss_head"):
        err = _jnp.square(y.astype(_jnp.float32) - loss_target)
        return 0.5 * _jnp.sum(_jnp.mean(err, axis=-1)) if err.ndim else 0.5 * err


def _adamw(w, g, m, v):
    m = ADAM_B1 * m + (1.0 - ADAM_B1) * g
    v = ADAM_B2 * v + (1.0 - ADAM_B2) * _jnp.square(g)
    m_hat = m / (1.0 - ADAM_B1 ** ADAM_STEP)
    v_hat = v / (1.0 - ADAM_B2 ** ADAM_STEP)
    delta = -ADAM_LR * (m_hat / (_jnp.sqrt(v_hat) + ADAM_EPS) + ADAM_WD * w)
    return delta, m, v


def reference(x, ffn1_w_gate, ffn1_w_up, ffn1_w_down, ln1_g, ln1_b, w_in, b_forget, conv_w, conv_b, rg_wa, rg_ba, rg_wx, rg_bx, lru_lambda, w_out, ln2_g, ln2_b, ffn2_w_gate, ffn2_w_up, ffn2_w_down, ln3_g, ln3_b, loss_target, m_ffn1_w_gate, m_ffn1_w_up, m_ffn1_w_down, m_ln1_g, m_ln1_b, m_w_in, m_b_forget, m_conv_w, m_conv_b, m_rg_wa, m_rg_ba, m_rg_wx, m_rg_bx, m_lru_lambda, m_w_out, m_ln2_g, m_ln2_b, m_ffn2_w_gate, m_ffn2_w_up, m_ffn2_w_down, m_ln3_g, m_ln3_b, v_ffn1_w_gate, v_ffn1_w_up, v_ffn1_w_down, v_ln1_g, v_ln1_b, v_w_in, v_b_forget, v_conv_w, v_conv_b, v_rg_wa, v_rg_ba, v_rg_wx, v_rg_bx, v_lru_lambda, v_w_out, v_ln2_g, v_ln2_b, v_ffn2_w_gate, v_ffn2_w_up, v_ffn2_w_down, v_ln3_g, v_ln3_b):
    given = dict(x=x, ffn1_w_gate=ffn1_w_gate, ffn1_w_up=ffn1_w_up, ffn1_w_down=ffn1_w_down, ln1_g=ln1_g, ln1_b=ln1_b, w_in=w_in, b_forget=b_forget, conv_w=conv_w, conv_b=conv_b, rg_wa=rg_wa, rg_ba=rg_ba, rg_wx=rg_wx, rg_bx=rg_bx, lru_lambda=lru_lambda, w_out=w_out, ln2_g=ln2_g, ln2_b=ln2_b, ffn2_w_gate=ffn2_w_gate, ffn2_w_up=ffn2_w_up, ffn2_w_down=ffn2_w_down, ln3_g=ln3_g, ln3_b=ln3_b, loss_target=loss_target, m_ffn1_w_gate=m_ffn1_w_gate, m_ffn1_w_up=m_ffn1_w_up, m_ffn1_w_down=m_ffn1_w_down, m_ln1_g=m_ln1_g, m_ln1_b=m_ln1_b, m_w_in=m_w_in, m_b_forget=m_b_forget, m_conv_w=m_conv_w, m_conv_b=m_conv_b, m_rg_wa=m_rg_wa, m_rg_ba=m_rg_ba, m_rg_wx=m_rg_wx, m_rg_bx=m_rg_bx, m_lru_lambda=m_lru_lambda, m_w_out=m_w_out, m_ln2_g=m_ln2_g, m_ln2_b=m_ln2_b, m_ffn2_w_gate=m_ffn2_w_gate, m_ffn2_w_up=m_ffn2_w_up, m_ffn2_w_down=m_ffn2_w_down, m_ln3_g=m_ln3_g, m_ln3_b=m_ln3_b, v_ffn1_w_gate=v_ffn1_w_gate, v_ffn1_w_up=v_ffn1_w_up, v_ffn1_w_down=v_ffn1_w_down, v_ln1_g=v_ln1_g, v_ln1_b=v_ln1_b, v_w_in=v_w_in, v_b_forget=v_b_forget, v_conv_w=v_conv_w, v_conv_b=v_conv_b, v_rg_wa=v_rg_wa, v_rg_ba=v_rg_ba, v_rg_wx=v_rg_wx, v_rg_bx=v_rg_bx, v_lru_lambda=v_lru_lambda, v_w_out=v_w_out, v_ln2_g=v_ln2_g, v_ln2_b=v_ln2_b, v_ffn2_w_gate=v_ffn2_w_gate, v_ffn2_w_up=v_ffn2_w_up, v_ffn2_w_down=v_ffn2_w_down, v_ln3_g=v_ln3_g, v_ln3_b=v_ln3_b)
    weights = {n: given[n] for n in TWIN_WEIGHTS}
    shared = {n: given[n] for n in SHARED_INPUTS}
    per_example = {n: given[n] for n in ['x']}
    grad_fn = _jax.value_and_grad(_loss, argnums=(0, 1))

    def one_microbatch(ex, loss_target):
        ex = dict(ex)
        diff = ex.pop(TWIN_DIFF_INPUT)
        return grad_fn(weights, diff, {**shared, **ex}, loss_target)

    if N_MICROBATCH == 1:
        loss, (grad_w, grad_x) = one_microbatch(per_example, given["loss_target"])
    else:
        def body(carry, xs):
            loss_sum, grad_sum = carry
            l_k, (gw_k, gx_k) = one_microbatch(xs[0], xs[1])
            with _jax.named_scope("update"):
                return (loss_sum + l_k, _jax.tree.map(_jnp.add, grad_sum, gw_k)), gx_k

        init = (_jnp.zeros((), _jnp.float32), _jax.tree.map(_jnp.zeros_like, weights))
        (loss, grad_w), grad_x = _jax.lax.scan(body, init, (per_example, given["loss_target"]))
    with _jax.named_scope("update"):
        delta_w, new_m, new_v = {}, {}, {}
        for n in TWIN_WEIGHTS:
            delta_w[n], new_m[n], new_v[n] = _adamw(weights[n], grad_w[n], given["m_" + n], given["v_" + n])
    return (loss, grad_x, *[grad_w[n] for n in TWIN_WEIGHTS], *[delta_w[n] for n in TWIN_WEIGHTS],
            *[new_m[n] for n in TWIN_WEIGHTS], *[new_v[n] for n in TWIN_WEIGHTS])
```

```python
import math

import jax
import jax.numpy as jnp
from jax import lax
from jax.experimental import pallas as pl
from jax.experimental.pallas import tpu as pltpu

f32 = jnp.float32
bf16 = jnp.bfloat16

N_DEV = 8
D_MODEL = 1024
D_FF = 4096
FF_TILE = D_FF // N_DEV
FOX_W = 512
LRU_W = 512
HEADS = 8
HEAD_D = 64
IN_COLS = 2568
IN_SHARD = IN_COLS // N_DEV
LANES = 128
LN_EPS = 1e-5
ALPHA = 2.0 ** 0.25
ATT_SCALE = 1.0 / math.sqrt(HEAD_D)
LRU_C = 8.0
NEG_BIG = -1e30

ADAM_LR = 0.001
ADAM_B1 = 0.9
ADAM_B2 = 0.999
ADAM_EPS = 1e-08
ADAM_WD = 0.01
ADAM_STEP = 10

VMEM_LIMIT = 56 * 1024 * 1024
MESH_T = pl.DeviceIdType.MESH


def _params(sem, **kw):
    return pltpu.CompilerParams(dimension_semantics=sem, vmem_limit_bytes=VMEM_LIMIT, **kw)


def _sigmoid(x):
    return 1.0 / (1.0 + jnp.exp(-x))


def _softplus(x):
    return jnp.maximum(x, 0.0) + jnp.log(1.0 + jnp.exp(-jnp.abs(x)))


def _one_minus_exp(x):
    series = -x * (1.0 + x * (0.5 + x * (1.0 / 6 + x * (1.0 / 24 + x * (1.0 / 120 + x * (1.0 / 720))))))
    return jnp.where(x > -0.125, series, 1.0 - jnp.exp(x))


_GELU_C = math.sqrt(2.0 / math.pi)


def _gelu_and_grad(x):
    inner = _GELU_C * (x + 0.044715 * x * x * x)
    t = jnp.tanh(inner)
    g = 0.5 * x * (1.0 + t)
    dg = 0.5 * (1.0 + t) + 0.5 * x * (1.0 - t * t) * _GELU_C * (1.0 + 3 * 0.044715 * x * x)
    return g, dg


def _ln_fwd_tile(pre):
    mu = jnp.mean(pre, axis=-1, keepdims=True)
    xc = pre - mu
    var = jnp.mean(xc * xc, axis=-1, keepdims=True)
    rstd = lax.rsqrt(var + LN_EPS)
    return xc * rstd, rstd


def _ln_bwd_tile(dy, xhat, rstd, g):
    dyg = dy * g
    m1 = jnp.mean(dyg, axis=-1, keepdims=True)
    m2 = jnp.mean(dyg * xhat, axis=-1, keepdims=True)
    dpre = rstd * (dyg - m1 - xhat * m2)
    return dpre, jnp.sum(dy * xhat, axis=0, keepdims=True), jnp.sum(dy, axis=0, keepdims=True)


_NT = (((1,), (1,)), ((), ()))
_TN = (((0,), (0,)), ((), ()))


def _ffn_fwd(xhat, g_in, b_in, wg, wu, wd, *, tm, name):
    t = xhat.shape[0]
    nj = N_DEV

    def body(x_ref, g_ref, b_ref, wg_ref, wu_ref, wd_ref, xo_ref, rstd_ref, hg_ref, hu_ref, xb, acc):
        j = pl.program_id(1)

        @pl.when(j == 0)
        def _():
            xb[...] = (x_ref[...] * g_ref[...] + b_ref[...]).astype(bf16)
            acc[...] = jnp.zeros_like(acc)

        hg = jnp.dot(xb[...], wg_ref[...], preferred_element_type=f32)
        hu = jnp.dot(xb[...], wu_ref[...], preferred_element_type=f32)
        hg_ref[...] = hg.astype(bf16)
        hu_ref[...] = hu.astype(bf16)
        a = hg * _sigmoid(hg) * hu
        acc[...] += jnp.dot(a.astype(bf16), wd_ref[...], preferred_element_type=f32)

        @pl.when(j == nj - 1)
        def _():
            x = x_ref[...] * g_ref[...] + b_ref[...]
            xo, rstd = _ln_fwd_tile(ALPHA * x + 0.5 * acc[...])
            xo_ref[...] = xo
            rstd_ref[...] = rstd

    row = pl.BlockSpec((1, D_MODEL), lambda i, j: (0, 0))
    return pl.pallas_call(
        body, name=name, grid=(t // tm, nj),
        in_specs=[pl.BlockSpec((tm, D_MODEL), lambda i, j: (i, 0)), row, row,
                  pl.BlockSpec((None, D_MODEL, FF_TILE), lambda i, j: (j, 0, 0)),
                  pl.BlockSpec((None, D_MODEL, FF_TILE), lambda i, j: (j, 0, 0)),
                  pl.BlockSpec((None, FF_TILE, D_MODEL), lambda i, j: (j, 0, 0))],
        out_specs=[pl.BlockSpec((tm, D_MODEL), lambda i, j: (i, 0)),
                   pl.BlockSpec((tm, 1), lambda i, j: (i, 0)),
                   pl.BlockSpec((tm, FF_TILE), lambda i, j: (i, j)),
                   pl.BlockSpec((tm, FF_TILE), lambda i, j: (i, j))],
        out_shape=[jax.ShapeDtypeStruct((t, D_MODEL), f32), jax.ShapeDtypeStruct((t, 1), f32),
                   jax.ShapeDtypeStruct((t, D_FF), bf16), jax.ShapeDtypeStruct((t, D_FF), bf16)],
        scratch_shapes=[pltpu.VMEM((tm, D_MODEL), bf16), pltpu.VMEM((tm, D_MODEL), f32)],
        compiler_params=_params(("arbitrary", "arbitrary")),
    )(xhat, g_in, b_in, wg, wu, wd)


def _ffn_bwd(dpre, hg, hu, wg, wu, wd, ln_in, *, tm, name):
    t = dpre.shape[0]
    nj = N_DEV
    with_ln = ln_in is not None

    def body(*refs):
        if with_ln:
            (dp_ref, hg_ref, hu_ref, wg_ref, wu_ref, wd_ref, xh_ref, rs_ref, g_ref,
             dx_ref, gg_ref, gb_ref, dhg_ref, dhu_ref, a_ref, dfb, acc) = refs
        else:
            (dp_ref, hg_ref, hu_ref, wg_ref, wu_ref, wd_ref,
             dx_ref, dhg_ref, dhu_ref, a_ref, dfb, acc) = refs
        i = pl.program_id(0)
        j = pl.program_id(1)

        @pl.when(j == 0)
        def _():
            dfb[...] = (0.5 * dp_ref[...]).astype(bf16)
            acc[...] = jnp.zeros_like(acc)

        da = lax.dot_general(dfb[...], wd_ref[...], _NT, preferred_element_type=f32)
        hgv = hg_ref[...].astype(f32)
        huv = hu_ref[...].astype(f32)
        sg = _sigmoid(hgv)
        silu = hgv * sg
        a_ref[...] = (silu * huv).astype(bf16)
        dhu = (da * silu).astype(bf16)
        dhg = (da * huv * (sg * (1.0 + hgv * (1.0 - sg)))).astype(bf16)
        dhg_ref[...] = dhg
        dhu_ref[...] = dhu
        acc[...] += (lax.dot_general(dhg, wg_ref[...], _NT, preferred_element_type=f32)
                     + lax.dot_general(dhu, wu_ref[...], _NT, preferred_element_type=f32))

        @pl.when(j == nj - 1)
        def _():
            dx = ALPHA * dp_ref[...] + acc[...]
            if with_ln:
                dprev, gg, gb = _ln_bwd_tile(dx, xh_ref[...], rs_ref[...], g_ref[...])
                dx_ref[...] = dprev

                @pl.when(i == 0)
                def _():
                    gg_ref[...] = gg
                    gb_ref[...] = gb

                @pl.when(i > 0)
                def _():
                    gg_ref[...] += gg
                    gb_ref[...] += gb
            else:
                dx_ref[...] = dx

    tok = pl.BlockSpec((tm, D_MODEL), lambda i, j: (i, 0))
    row = pl.BlockSpec((1, D_MODEL), lambda i, j: (0, 0))
    hid = pl.BlockSpec((tm, FF_TILE), lambda i, j: (i, j))
    in_specs = [tok, hid, hid,
                pl.BlockSpec((None, D_MODEL, FF_TILE), lambda i, j: (j, 0, 0)),
                pl.BlockSpec((None, D_MODEL, FF_TILE), lambda i, j: (j, 0, 0)),
                pl.BlockSpec((None, FF_TILE, D_MODEL), lambda i, j: (j, 0, 0))]
    args = [dpre, hg, hu, wg, wu, wd]
    out_specs = [tok]
    out_shape = [jax.ShapeDtypeStruct((t, D_MODEL), f32)]
    if with_ln:
        in_specs += [tok, pl.BlockSpec((tm, 1), lambda i, j: (i, 0)), row]
        args += list(ln_in)
        out_specs += [row, row]
        out_shape += [jax.ShapeDtypeStruct((1, D_MODEL), f32)] * 2
    out_specs += [hid, hid, hid]
    out_shape += [jax.ShapeDtypeStruct((t, D_FF), bf16)] * 3
    return pl.pallas_call(
        body, name=name, grid=(t // tm, nj), in_specs=in_specs, out_specs=out_specs, out_shape=out_shape,
        scratch_shapes=[pltpu.VMEM((tm, D_MODEL), bf16), pltpu.VMEM((tm, D_MODEL), f32)],
        compiler_params=_params(("arbitrary", "arbitrary")),
    )(*args)


def _mm(a, b, *, mode, out_dtype, tm, tn, tk, name, affine=None, a_cols=None, b_cols=None,
        b_blocked=False, out_blocked=False, out_scale=None):
    if mode == "nn":
        m_full, k_full = a.shape
        m_dim, k_dim = (m_full, a_cols[1]) if a_cols else (m_full, k_full)
    else:
        k_dim, m_full = a.shape
        m_dim = a_cols[1] if a_cols else m_full
    a_off = a_cols[0] if a_cols else 0
    if b_blocked:
        n_dim = b.shape[0] * b.shape[2]
        assert b.shape[2] == tn
    else:
        n_dim = b_cols[1] if b_cols else b.shape[1]
    b_off = b_cols[0] if b_cols else 0
    assert m_dim % tm == 0 and n_dim % tn == 0 and k_dim % tk == 0, (name, m_dim, n_dim, k_dim)
    nk = k_dim // tk

    def body(*refs):
        if affine is not None:
            a_ref, g_ref, s_ref, b_ref, o_ref, acc = refs
        else:
            a_ref, b_ref, o_ref, acc = refs
        k = pl.program_id(2)

        @pl.when(k == 0)
        def _():
            acc[...] = jnp.zeros_like(acc)

        av = a_ref[...]
        if affine is not None:
            av = av * g_ref[...] + s_ref[...]
        av = av.astype(bf16)
        bv = b_ref[...].astype(bf16)
        if mode == "nn":
            acc[...] += jnp.dot(av, bv, preferred_element_type=f32)
        else:
            acc[...] += lax.dot_general(av, bv, _TN, preferred_element_type=f32)

        @pl.when(k == nk - 1)
        def _():
            res = acc[...] if out_scale is None else acc[...] * out_scale
            o_ref[...] = res.astype(out_dtype)

    if mode == "nn":
        a_spec = pl.BlockSpec((tm, tk), lambda i, j, k: (i, k + a_off))
        aff_spec = pl.BlockSpec((1, tk), lambda i, j, k: (0, k + a_off))
    else:
        a_spec = pl.BlockSpec((tk, tm), lambda i, j, k: (k, i + a_off))
        aff_spec = pl.BlockSpec((1, tm), lambda i, j, k: (0, i + a_off))
    if b_blocked:
        b_spec = pl.BlockSpec((None, tk, tn), lambda i, j, k: (j, k, 0))
    else:
        b_spec = pl.BlockSpec((tk, tn), lambda i, j, k: (k, j + b_off))
    if out_blocked:
        o_spec = pl.BlockSpec((None, tm, tn), lambda i, j, k: (j, i, 0))
        o_shape = jax.ShapeDtypeStruct((n_dim // tn, m_dim, tn), out_dtype)
    else:
        o_spec = pl.BlockSpec((tm, tn), lambda i, j, k: (i, j))
        o_shape = jax.ShapeDtypeStruct((m_dim, n_dim), out_dtype)
    in_specs = [a_spec] + ([aff_spec, aff_spec] if affine is not None else []) + [b_spec]
    args = [a] + (list(affine) if affine is not None else []) + [b]
    return pl.pallas_call(
        body, name=name, grid=(m_dim // tm, n_dim // tn, nk), in_specs=in_specs, out_specs=o_spec,
        out_shape=o_shape, scratch_shapes=[pltpu.VMEM((tm, tn), f32)],
        compiler_params=_params(("arbitrary", "arbitrary", "arbitrary")),
    )(*args)


def _mmln(pairs, *, tm, name, resid=None, resid_scale=1.0, epi=None, ln=None, n_out=D_MODEL):
    t = pairs[0][0].shape[0]
    n_pairs = len(pairs)
    n_resid = 0 if resid is None else len(resid) - 1

    def body(*refs):
        pos = 0
        val = None
        for p in range(n_pairs):
            a_ref, b_ref = refs[pos], refs[pos + 1]
            pos += 2
            av = a_ref[...].astype(bf16)
            bv = b_ref[...].astype(bf16)
            if pairs[p][6] == "nn":
                term = jnp.dot(av, bv, preferred_element_type=f32)
            else:
                term = lax.dot_general(av, bv, _NT, preferred_element_type=f32)
            val = term if val is None else val + term
        if resid is not None:
            if resid[0] == "plain":
                r = refs[pos][...]
            else:
                r = refs[pos][...] * refs[pos + 1][...] + refs[pos + 2][...]
            pos += n_resid
            val = val + resid_scale * r
        if epi is None:
            o_ref = refs[pos]
            o_ref[...] = val.astype(o_ref.dtype)
        elif epi == "ln_fwd":
            xo, rstd = _ln_fwd_tile(val)
            refs[pos][...] = xo
            refs[pos + 1][...] = rstd
        else:
            xh_ref, rs_ref, g_ref, dx_ref, gg_ref, gb_ref = refs[pos:pos + 6]
            dprev, gg, gb = _ln_bwd_tile(val, xh_ref[...], rs_ref[...], g_ref[...])
            dx_ref[...] = dprev
            i = pl.program_id(0)

            @pl.when(i == 0)
            def _():
                gg_ref[...] = gg
                gb_ref[...] = gb

            @pl.when(i > 0)
            def _():
                gg_ref[...] += gg
                gb_ref[...] += gb

    in_specs, args = [], []
    for (a, acb, aw, b, bcb, bw, mode) in pairs:
        in_specs.append(pl.BlockSpec((tm, aw), lambda i, acb=acb: (i, acb)))
        args.append(a)
        if mode == "nn":
            in_specs.append(pl.BlockSpec((aw, n_out), lambda i, bcb=bcb: (bcb, 0)))
        else:
            in_specs.append(pl.BlockSpec((n_out, bw), lambda i, bcb=bcb: (0, bcb)))
        args.append(b)
    tok = pl.BlockSpec((tm, n_out), lambda i: (i, 0))
    row = pl.BlockSpec((1, n_out), lambda i: (0, 0))
    col = pl.BlockSpec((tm, 1), lambda i: (i, 0))
    if resid is not None:
        in_specs += [tok] if resid[0] == "plain" else [tok, row, row]
        args += list(resid[1:])
    if epi is None:
        out_specs, out_shape = tok, jax.ShapeDtypeStruct((t, n_out), f32)
    elif epi == "ln_fwd":
        out_specs = [tok, col]
        out_shape = [jax.ShapeDtypeStruct((t, n_out), f32), jax.ShapeDtypeStruct((t, 1), f32)]
    else:
        in_specs += [tok, col, row]
        args += list(ln)
        out_specs = [tok, row, row]
        out_shape = [jax.ShapeDtypeStruct((t, n_out), f32)] + [jax.ShapeDtypeStruct((1, n_out), f32)] * 2
    return pl.pallas_call(
        body, name=name, grid=(t // tm,), in_specs=in_specs, out_specs=out_specs, out_shape=out_shape,
        compiler_params=_params(("arbitrary",)),
    )(*args)


def _loss_bwd(xhat, rstd, g, b, target, *, tm, name):
    t = xhat.shape[0]

    def body(xh_ref, rs_ref, g_ref, b_ref, tg_ref, dx_ref, sq_ref, gg_ref, gb_ref):
        i = pl.program_id(0)
        xh = xh_ref[...]
        diff = xh * g_ref[...] + b_ref[...] - tg_ref[...]
        sq = jnp.sum(diff * diff, axis=0, keepdims=True)
        dprev, gg, gb = _ln_bwd_tile(diff * (1.0 / D_MODEL), xh, rs_ref[...], g_ref[...])
        dx_ref[...] = dprev

        @pl.when(i == 0)
        def _():
            sq_ref[...] = sq
            gg_ref[...] = gg
            gb_ref[...] = gb

        @pl.when(i > 0)
        def _():
            sq_ref[...] += sq
            gg_ref[...] += gg
            gb_ref[...] += gb

    tok = pl.BlockSpec((tm, D_MODEL), lambda i: (i, 0))
    row = pl.BlockSpec((1, D_MODEL), lambda i: (0, 0))
    return pl.pallas_call(
        body, name=name, grid=(t // tm,),
        in_specs=[tok, pl.BlockSpec((tm, 1), lambda i: (i, 0)), row, row, tok],
        out_specs=[tok, row, row, row],
        out_shape=[jax.ShapeDtypeStruct((t, D_MODEL), f32)] + [jax.ShapeDtypeStruct((1, D_MODEL), f32)] * 3,
        compiler_params=_params(("arbitrary",)),
    )(xhat, rstd, g, b, target)


CUM_TILE = 256


def _tri(n, lower):
    r = lax.broadcasted_iota(jnp.int32, (n, n), 0)
    c = lax.broadcasted_iota(jnp.int32, (n, n), 1)
    return jnp.where((r >= c) if lower else (r <= c), 1.0, 0.0).astype(f32)


def _cum_fwd(zfg, bfg, *, name):
    t = zfg.shape[0]

    def body(z_ref, b_ref, o_ref, carry):
        @pl.when(pl.program_id(0) == 0)
        def _():
            carry[...] = jnp.zeros_like(carry)

        ls = -_softplus(-(z_ref[...] + b_ref[...]))
        c = jnp.dot(_tri(CUM_TILE, True), ls, preferred_element_type=f32,
                    precision=lax.Precision.HIGHEST) + carry[...]
        o_ref[...] = c
        carry[...] = c[CUM_TILE - 1:CUM_TILE, :]

    blk = pl.BlockSpec((CUM_TILE, LANES), lambda i: (i, 0))
    return pl.pallas_call(
        body, name=name, grid=(t // CUM_TILE,),
        in_specs=[blk, pl.BlockSpec((1, LANES), lambda i: (0, 0))], out_specs=blk,
        out_shape=jax.ShapeDtypeStruct((t, LANES), f32), scratch_shapes=[pltpu.VMEM((1, LANES), f32)],
        compiler_params=_params(("arbitrary",)),
    )(zfg, bfg)


def _cum_bwd(dcum_q, dcum_k, zfg, bfg, *, name):
    t = zfg.shape[0]
    n = t // CUM_TILE

    def body(d_ref, d2_ref, z_ref, b_ref, o_ref, s_ref, carry):
        i = pl.program_id(0)

        @pl.when(i == 0)
        def _():
            carry[...] = jnp.zeros_like(carry)

        dls = jnp.dot(_tri(CUM_TILE, False), d_ref[...] + d2_ref[...], preferred_element_type=f32,
                      precision=lax.Precision.HIGHEST) + carry[...]
        carry[...] = dls[0:1, :]
        lane = lax.broadcasted_iota(jnp.int32, (CUM_TILE, LANES), 1)
        dfg = jnp.where(lane < HEADS, dls * _sigmoid(-(z_ref[...] + b_ref[...])), 0.0)
        o_ref[...] = dfg
        tot = jnp.sum(dfg, axis=0, keepdims=True)

        @pl.when(i == 0)
        def _():
            s_ref[...] = tot

        @pl.when(i > 0)
        def _():
            s_ref[...] += tot

    blk = pl.BlockSpec((CUM_TILE, LANES), lambda i: (n - 1 - i, 0))
    row = pl.BlockSpec((1, LANES), lambda i: (0, 0))
    return pl.pallas_call(
        body, name=name, grid=(n,), in_specs=[blk, blk, blk, row], out_specs=[blk, row],
        out_shape=[jax.ShapeDtypeStruct((t, LANES), f32), jax.ShapeDtypeStruct((1, LANES), f32)],
        scratch_shapes=[pltpu.VMEM((1, LANES), f32)],
        compiler_params=_params(("arbitrary",)),
    )(dcum_q, dcum_k, zfg, bfg)


ATT_TILE = 512


def _causal(i, j, transposed):
    r = lax.broadcasted_iota(jnp.int32, (ATT_TILE, ATT_TILE), 0)
    c = lax.broadcasted_iota(jnp.int32, (ATT_TILE, ATT_TILE), 1)
    if transposed:
        return (c + i * ATT_TILE) >= (r + j * ATT_TILE)
    return (r + i * ATT_TILE) >= (c + j * ATT_TILE)


def _attn_fwd(qkv, cum, cum_t, *, name):
    t = qkv.shape[0]
    n = t // ATT_TILE
    tq = ATT_TILE

    def body(q_ref, k_ref, v_ref, cq_ref, ck_ref, o_ref, lse_ref, acc, m_s, l_s):
        i = pl.program_id(0)
        j = pl.program_id(1)

        @pl.when(j == 0)
        def _():
            acc[...] = jnp.zeros_like(acc)
            m_s[...] = jnp.full_like(m_s, NEG_BIG)
            l_s[...] = jnp.zeros_like(l_s)

        @pl.when(j <= i)
        def _():
            mask = _causal(i, j, False)
            for h in range(HEADS):
                hs = slice(HEAD_D * h, HEAD_D * (h + 1))
                s = lax.dot_general(q_ref[:, hs], k_ref[:, hs], _NT, preferred_element_type=f32) * ATT_SCALE
                s = s + cq_ref[:, h:h + 1] - ck_ref[h:h + 1, :]
                s = jnp.where(mask, s, NEG_BIG)
                m_old = m_s[:, h:h + 1]
                m_new = jnp.maximum(m_old, jnp.max(s, axis=-1, keepdims=True))
                corr = jnp.exp(m_old - m_new)
                p = jnp.exp(s - m_new)
                l_s[:, h:h + 1] = corr * l_s[:, h:h + 1] + jnp.sum(p, axis=-1, keepdims=True)
                acc[:, hs] = corr * acc[:, hs] + jnp.dot(p.astype(bf16), v_ref[:, hs], preferred_element_type=f32)
                m_s[:, h:h + 1] = m_new

        @pl.when(j == i)
        def _():
            lse_ref[...] = jnp.zeros_like(lse_ref)
            for h in range(HEADS):
                hs = slice(HEAD_D * h, HEAD_D * (h + 1))
                l = l_s[:, h:h + 1]
                o_ref[:, hs] = acc[:, hs] / l
                lse_ref[:, h:h + 1] = m_s[:, h:h + 1] + jnp.log(l)

    return pl.pallas_call(
        body, name=name, grid=(n, n),
        in_specs=[pl.BlockSpec((tq, FOX_W), lambda i, j: (i, 0)),
                  pl.BlockSpec((tq, FOX_W), lambda i, j: (jnp.minimum(i, j), 1)),
                  pl.BlockSpec((tq, FOX_W), lambda i, j: (jnp.minimum(i, j), 2)),
                  pl.BlockSpec((tq, LANES), lambda i, j: (i, 0)),
                  pl.BlockSpec((HEADS, tq), lambda i, j: (0, jnp.minimum(i, j)))],
        out_specs=[pl.BlockSpec((tq, FOX_W), lambda i, j: (i, 0)), pl.BlockSpec((tq, LANES), lambda i, j: (i, 0))],
        out_shape=[jax.ShapeDtypeStruct((t, FOX_W), f32), jax.ShapeDtypeStruct((t, LANES), f32)],
        scratch_shapes=[pltpu.VMEM((tq, FOX_W), f32), pltpu.VMEM((tq, LANES), f32), pltpu.VMEM((tq, LANES), f32)],
        compiler_params=_params(("arbitrary", "arbitrary")),
    )(qkv, qkv, qkv, cum, cum_t)


def _attn_delta(dmix, o, *, tm, name):
    t = o.shape[0]

    def body(do_ref, o_ref, d_ref):
        r = lax.broadcasted_iota(jnp.int32, (FOX_W, LANES), 0)
        c = lax.broadcasted_iota(jnp.int32, (FOX_W, LANES), 1)
        pick = jnp.where(r // HEAD_D == c, 1.0, 0.0).astype(f32)
        d_ref[...] = jnp.dot(do_ref[...] * o_ref[...], pick, preferred_element_type=f32,
                             precision=lax.Precision.HIGHEST)

    blk = pl.BlockSpec((tm, FOX_W), lambda i: (i, 0))
    return pl.pallas_call(
        body, name=name, grid=(t // tm,), in_specs=[blk, blk],
        out_specs=pl.BlockSpec((tm, LANES), lambda i: (i, 0)),
        out_shape=jax.ShapeDtypeStruct((t, LANES), f32), compiler_params=_params(("arbitrary",)),
    )(dmix, o)


def _attn_dq(qkv, dmix, cum, cum_t, lse, delta, *, name):
    t = qkv.shape[0]
    n = t // ATT_TILE
    tq = ATT_TILE

    def body(q_ref, k_ref, v_ref, do_ref, cq_ref, ck_ref, lse_ref, dl_ref, dq_ref, dc_ref, acc, dc_acc):
        i = pl.program_id(0)
        j = pl.program_id(1)

        @pl.when(j == 0)
        def _():
            acc[...] = jnp.zeros_like(acc)
            dc_acc[...] = jnp.zeros_like(dc_acc)

        @pl.when(j <= i)
        def _():
            mask = _causal(i, j, False)
            for h in range(HEADS):
                hs = slice(HEAD_D * h, HEAD_D * (h + 1))
                kh = k_ref[:, hs]
                s = lax.dot_general(q_ref[:, hs], kh, _NT, preferred_element_type=f32) * ATT_SCALE
                s = s + cq_ref[:, h:h + 1] - ck_ref[h:h + 1, :]
                s = jnp.where(mask, s, NEG_BIG)
                p = jnp.exp(s - lse_ref[:, h:h + 1])
                dp = lax.dot_general(do_ref[:, hs].astype(bf16), v_ref[:, hs], _NT, preferred_element_type=f32)
                ds = p * (dp - dl_ref[:, h:h + 1])
                acc[:, hs] += jnp.dot((ds * ATT_SCALE).astype(bf16), kh, preferred_element_type=f32)
                dc_acc[:, h:h + 1] += jnp.sum(ds, axis=-1, keepdims=True)

        @pl.when(j == i)
        def _():
            dq_ref[...] = acc[...].astype(bf16)
            dc_ref[...] = dc_acc[...]

    col = pl.BlockSpec((tq, LANES), lambda i, j: (i, 0))
    return pl.pallas_call(
        body, name=name, grid=(n, n),
        in_specs=[pl.BlockSpec((tq, FOX_W), lambda i, j: (i, 0)),
                  pl.BlockSpec((tq, FOX_W), lambda i, j: (jnp.minimum(i, j), 1)),
                  pl.BlockSpec((tq, FOX_W), lambda i, j: (jnp.minimum(i, j), 2)),
                  pl.BlockSpec((tq, FOX_W), lambda i, j: (i, 0)),
                  col, pl.BlockSpec((HEADS, tq), lambda i, j: (0, jnp.minimum(i, j))), col, col],
        out_specs=[pl.BlockSpec((tq, FOX_W), lambda i, j: (i, 0)), col],
        out_shape=[jax.ShapeDtypeStruct((t, FOX_W), bf16), jax.ShapeDtypeStruct((t, LANES), f32)],
        scratch_shapes=[pltpu.VMEM((tq, FOX_W), f32), pltpu.VMEM((tq, LANES), f32)],
        compiler_params=_params(("arbitrary", "arbitrary")),
    )(qkv, qkv, qkv, dmix, cum, cum_t, lse, delta)


def _attn_dkv(qkv, dmix, cum, cum_t, lse_t, delta_t, *, name):
    t = qkv.shape[0]
    n = t // ATT_TILE
    tk = ATT_TILE

    def body(q_ref, k_ref, v_ref, do_ref, cq_ref, ck_ref, lse_ref, dl_ref, dk_ref, dv_ref, dc_ref, dk_acc, dv_acc, dc_acc):
        j = pl.program_id(0)
        i = pl.program_id(1)

        @pl.when(i == 0)
        def _():
            dk_acc[...] = jnp.zeros_like(dk_acc)
            dv_acc[...] = jnp.zeros_like(dv_acc)
            dc_acc[...] = jnp.zeros_like(dc_acc)

        @pl.when(i >= j)
        def _():
            mask = _causal(i, j, True)
            for h in range(HEADS):
                hs = slice(HEAD_D * h, HEAD_D * (h + 1))
                qh = q_ref[:, hs]
                doh = do_ref[:, hs].astype(bf16)
                s_t = lax.dot_general(k_ref[:, hs], qh, _NT, preferred_element_type=f32) * ATT_SCALE
                s_t = s_t + cq_ref[h:h + 1, :] - ck_ref[:, h:h + 1]
                s_t = jnp.where(mask, s_t, NEG_BIG)
                p_t = jnp.exp(s_t - lse_ref[h:h + 1, :])
                dv_acc[:, hs] += jnp.dot(p_t.astype(bf16), doh, preferred_element_type=f32)
                dp_t = lax.dot_general(v_ref[:, hs], doh, _NT, preferred_element_type=f32)
                ds_t = p_t * (dp_t - dl_ref[h:h + 1, :])
                dk_acc[:, hs] += jnp.dot((ds_t * ATT_SCALE).astype(bf16), qh, preferred_element_type=f32)
                dc_acc[:, h:h + 1] -= jnp.sum(ds_t, axis=-1, keepdims=True)

        @pl.when(i == n - 1)
        def _():
            dk_ref[...] = dk_acc[...].astype(bf16)
            dv_ref[...] = dv_acc[...].astype(bf16)
            dc_ref[...] = dc_acc[...]

    rowq = pl.BlockSpec((HEADS, tk), lambda j, i: (0, jnp.maximum(i, j)))
    return pl.pallas_call(
        body, name=name, grid=(n, n),
        in_specs=[pl.BlockSpec((tk, FOX_W), lambda j, i: (jnp.maximum(i, j), 0)),
                  pl.BlockSpec((tk, FOX_W), lambda j, i: (j, 1)),
                  pl.BlockSpec((tk, FOX_W), lambda j, i: (j, 2)),
                  pl.BlockSpec((tk, FOX_W), lambda j, i: (jnp.maximum(i, j), 0)),
                  rowq, pl.BlockSpec((tk, LANES), lambda j, i: (j, 0)), rowq, rowq],
        out_specs=[pl.BlockSpec((tk, FOX_W), lambda j, i: (j, 0)), pl.BlockSpec((tk, FOX_W), lambda j, i: (j, 0)),
                   pl.BlockSpec((tk, LANES), lambda j, i: (j, 0))],
        out_shape=[jax.ShapeDtypeStruct((t, FOX_W), bf16), jax.ShapeDtypeStruct((t, FOX_W), bf16),
                   jax.ShapeDtypeStruct((t, LANES), f32)],
        scratch_shapes=[pltpu.VMEM((tk, FOX_W), f32), pltpu.VMEM((tk, FOX_W), f32), pltpu.VMEM((tk, LANES), f32)],
        compiler_params=_params(("arbitrary", "arbitrary")),
    )(qkv, qkv, qkv, dmix, cum_t, cum, lse_t, delta_t)


LRU_CHUNK = 64
SUB = 8


def _row_ids(n):
    return lax.broadcasted_iota(jnp.int32, (n, LANES), 0)


def _shift_rows_down(ext, s):
    return pltpu.roll(ext, s, axis=0)[SUB:, :]


def _shift_rows_up(ext, s, n):
    return pltpu.roll(ext, ext.shape[0] - s, axis=0)[:n, :]


def _lru_gates(u, wa_ref, ba_ref, wx_ref, bx_ref, sp):
    ub = u.astype(bf16)
    r = _sigmoid(jnp.dot(ub, wa_ref[...], preferred_element_type=f32) + ba_ref[...])
    gi = _sigmoid(jnp.dot(ub, wx_ref[...], preferred_element_type=f32) + bx_ref[...])
    log_a = -LRU_C * r * sp
    a = jnp.exp(log_a)
    s = jnp.sqrt(_one_minus_exp(2.0 * log_a))
    return r, gi, a, s


def _conv_window(lx_ref, r0, ci):
    cur = lx_ref[pl.ds(r0, LRU_CHUNK), :]
    p0 = pl.multiple_of(jnp.maximum(r0 - SUB, 0), SUB)
    prev = jnp.where(ci > 0, lx_ref[pl.ds(p0, SUB), :], 0.0)
    return cur, jnp.concatenate([prev, cur], axis=0)


def _lru_fwd(zl, conv_w, conv_b, wa, ba, wx, bx, lam, *, name):
    t = zl.shape[0]
    n_chunk = t // LRU_CHUNK

    def body(lx_ref, lg_ref, cw_ref, cb_ref, wa_ref, ba_ref, wx_ref, bx_ref, lam_ref, u_ref, h_ref, y_ref):
        sp = _softplus(-lam_ref[...])
        rows = _row_ids(SUB)

        def chunk(ci, hc):
            r0 = pl.multiple_of(ci * LRU_CHUNK, LRU_CHUNK)
            cur, ext = _conv_window(lx_ref, r0, ci)
            u = cb_ref[...] + cw_ref[3:4, :] * cur
            for k in range(3):
                u = u + cw_ref[k:k + 1, :] * _shift_rows_down(ext, 3 - k)
            r, gi, a, s = _lru_gates(u, wa_ref, ba_ref, wx_ref, bx_ref, sp)
            b = s * (gi * u)
            tiles = []
            for q in range(LRU_CHUNK // SUB):
                ta = a[SUB * q:SUB * (q + 1), :]
                tb = b[SUB * q:SUB * (q + 1), :]
                for d in (1, 2, 4):
                    a_sh = jnp.where(rows >= d, pltpu.roll(ta, d, axis=0), 1.0)
                    b_sh = jnp.where(rows >= d, pltpu.roll(tb, d, axis=0), 0.0)
                    tb = ta * b_sh + tb
                    ta = ta * a_sh
                hq = tb + ta * hc
                hc = hq[SUB - 1:SUB, :]
                tiles.append(hq)
            h = jnp.concatenate(tiles, axis=0)
            u_ref[pl.ds(r0, LRU_CHUNK), :] = u
            h_ref[pl.ds(r0, LRU_CHUNK), :] = h
            gel, _ = _gelu_and_grad(lg_ref[pl.ds(r0, LRU_CHUNK), :])
            y_ref[pl.ds(r0, LRU_CHUNK), :] = gel * h
            return hc

        lax.fori_loop(0, n_chunk, chunk, jnp.zeros((1, LANES), f32))

    seq = lambda cb: pl.BlockSpec((t, LANES), lambda c, cb=cb: (0, c + cb))
    rowc = pl.BlockSpec((1, LANES), lambda c: (0, c))
    diag = pl.BlockSpec((LANES, LANES), lambda c: (c, c))
    out = jax.ShapeDtypeStruct((t, LRU_W), f32)
    return pl.pallas_call(
        body, name=name, grid=(LRU_W // LANES,),
        in_specs=[seq(0), seq(4), pl.BlockSpec((4, LANES), lambda c: (0, c)), rowc, diag, rowc, diag, rowc, rowc],
        out_specs=[seq(0)] * 3, out_shape=[out] * 3,
        compiler_params=_params(("arbitrary",)),
    )(zl, zl, conv_w, conv_b, wa, ba, wx, bx, lam)


def _lru_bwd(dmix, zl, u_all, h_all, conv_w, wa, ba, wx, bx, lam, *, name):
    t = zl.shape[0]
    n_chunk = t // LRU_CHUNK

    def body(dy_ref, lx_ref, lg_ref, u_ref, h_ref, cw_ref, wa_ref, ba_ref, wx_ref, bx_ref, lam_ref,
             dlx_ref, dlg_ref, dcw_ref, dcb_ref, dba_ref, dbx_ref, dlam_ref, dwa_ref, dwx_ref, dpr_s, dpx_s):
        lam_v = lam_ref[...]
        sp = _softplus(-lam_v)
        rows = _row_ids(SUB)
        rows_c = _row_ids(LRU_CHUNK)
        zero_row = jnp.zeros((1, LANES), f32)

        def chunk(step, carry):
            dh_c, a_next0, du_next, dsp, dba, dbx, dcb, dw0, dw1, dw2, dw3 = carry
            ci = n_chunk - 1 - step
            r0 = pl.multiple_of(ci * LRU_CHUNK, LRU_CHUNK)
            sl = pl.ds(r0, LRU_CHUNK)
            u = u_ref[sl, :]
            r, gi, a, s = _lru_gates(u, wa_ref, ba_ref, wx_ref, bx_ref, sp)
            h = h_ref[sl, :]
            p0 = pl.multiple_of(jnp.maximum(r0 - SUB, 0), SUB)
            h_before = jnp.where(ci > 0, h_ref[pl.ds(p0, SUB), :], 0.0)[SUB - 1:SUB, :]
            h_prev = jnp.where(rows_c == 0, h_before, pltpu.roll(h, 1, axis=0))
            gel, dgel = _gelu_and_grad(lg_ref[sl, :])
            dy = dy_ref[sl, :]
            dlg_ref[sl, :] = (dy * h * dgel).astype(bf16)
            g_in = dy * gel
            a_next = jnp.where(rows_c == LRU_CHUNK - 1, a_next0, pltpu.roll(a, LRU_CHUNK - 1, axis=0))
            tiles = [None] * (LRU_CHUNK // SUB)
            for q in reversed(range(LRU_CHUNK // SUB)):
                ta = a_next[SUB * q:SUB * (q + 1), :]
                tb = g_in[SUB * q:SUB * (q + 1), :]
                for d in (1, 2, 4):
                    a_sh = jnp.where(rows < SUB - d, pltpu.roll(ta, SUB - d, axis=0), 1.0)
                    b_sh = jnp.where(rows < SUB - d, pltpu.roll(tb, SUB - d, axis=0), 0.0)
                    tb = ta * b_sh + tb
                    ta = ta * a_sh
                dhq = tb + ta * dh_c
                dh_c = dhq[0:1, :]
                tiles[q] = dhq
            dh = jnp.concatenate(tiles, axis=0)
            da = dh * h_prev
            ds = dh * gi * u
            dgi = dh * s * u
            du = dh * s * gi
            dlog_a = da * a - ds * (a * a) / s
            dr = dlog_a * (-LRU_C * sp)
            dsp = dsp + jnp.sum(dlog_a * (-LRU_C * r), axis=0, keepdims=True)
            dpr = dr * r * (1.0 - r)
            dpx = dgi * gi * (1.0 - gi)
            dprb = dpr.astype(bf16)
            dpxb = dpx.astype(bf16)
            dpr_s[sl, :] = dprb
            dpx_s[sl, :] = dpxb
            du = du + (lax.dot_general(dprb, wa_ref[...], _NT, preferred_element_type=f32)
                       + lax.dot_general(dpxb, wx_ref[...], _NT, preferred_element_type=f32))
            dba = dba + jnp.sum(dpr, axis=0, keepdims=True)
            dbx = dbx + jnp.sum(dpx, axis=0, keepdims=True)
            dcb = dcb + jnp.sum(du, axis=0, keepdims=True)
            du_ext = jnp.concatenate([du, du_next], axis=0)
            dlx = cw_ref[3:4, :] * du
            for k in range(3):
                dlx = dlx + cw_ref[k:k + 1, :] * _shift_rows_up(du_ext, 3 - k, LRU_CHUNK)
            dlx_ref[sl, :] = dlx.astype(bf16)
            cur, ext = _conv_window(lx_ref, r0, ci)
            dws = [dw0, dw1, dw2, dw3 + jnp.sum(du * cur, axis=0, keepdims=True)]
            for k in range(3):
                dws[k] = dws[k] + jnp.sum(du * _shift_rows_down(ext, 3 - k), axis=0, keepdims=True)
            return (dh_c, a[0:1, :], du[0:SUB, :], dsp, dba, dbx, dcb, dws[0], dws[1], dws[2], dws[3])

        init = (zero_row, zero_row, jnp.zeros((SUB, LANES), f32)) + (zero_row,) * 8
        out = lax.fori_loop(0, n_chunk, chunk, init)
        _, _, _, dsp, dba, dbx, dcb, dw0, dw1, dw2, dw3 = out
        dlam_ref[...] = dsp * (-_sigmoid(-lam_v))
        dba_ref[...] = dba
        dbx_ref[...] = dbx
        dcb_ref[...] = dcb
        dcw_ref[...] = jnp.concatenate([dw0, dw1, dw2, dw3], axis=0)
        ub = u_ref[...].astype(bf16)
        dwa_ref[...] = lax.dot_general(ub, dpr_s[...], _TN, preferred_element_type=f32)
        dwx_ref[...] = lax.dot_general(ub, dpx_s[...], _TN, preferred_element_type=f32)

    seq = lambda cb: pl.BlockSpec((t, LANES), lambda c, cb=cb: (0, c + cb))
    rowc = pl.BlockSpec((1, LANES), lambda c: (0, c))
    diag = pl.BlockSpec((LANES, LANES), lambda c: (c, c))
    gate_out = pl.BlockSpec((None, LANES, LANES), lambda c: (c, 0, 0))
    row_shape = jax.ShapeDtypeStruct((1, LRU_W), f32)
    return pl.pallas_call(
        body, name=name, grid=(LRU_W // LANES,),
        in_specs=[seq(4), seq(0), seq(4), seq(0), seq(0), pl.BlockSpec((4, LANES), lambda c: (0, c)),
                  diag, rowc, diag, rowc, rowc],
        out_specs=[seq(0), seq(0), pl.BlockSpec((4, LANES), lambda c: (0, c)), rowc, rowc, rowc, rowc,
                   gate_out, gate_out],
        out_shape=[jax.ShapeDtypeStruct((t, LRU_W), bf16)] * 2
        + [jax.ShapeDtypeStruct((4, LRU_W), f32)] + [row_shape] * 4
        + [jax.ShapeDtypeStruct((LRU_W // LANES, LANES, LANES), f32)] * 2,
        scratch_shapes=[pltpu.VMEM((t, LANES), bf16), pltpu.VMEM((t, LANES), bf16)],
        compiler_params=_params(("arbitrary",)),
    )(dmix, zl, zl, u_all, h_all, conv_w, wa, ba, wx, bx, lam)


def _block_diag(w):
    eye = jnp.eye(HEADS, dtype=w.dtype)
    return jnp.einsum("hij,hk->hikj", w, eye).reshape(LRU_W, LRU_W)


def _diag_blocks(dw):
    top = dw[:, :HEAD_D, :HEAD_D]
    bot = dw[:, HEAD_D:, HEAD_D:]
    return jnp.stack([top, bot], axis=1).reshape(HEADS, HEAD_D, HEAD_D)


def _local_step(x, target, wts, small, *, tm=512):
    t = x.shape[0]
    ones = jnp.ones((1, D_MODEL), f32)
    zeros = jnp.zeros((1, D_MODEL), f32)
    ln1 = (small["ln1_g"], small["ln1_b"])
    ln2 = (small["ln2_g"], small["ln2_b"])
    ln3 = (small["ln3_g"], small["ln3_b"])

    xh1, rs1, hg1, hu1 = _ffn_fwd(x, ones, zeros, wts["wg1"], wts["wu1"], wts["wd1"], tm=tm, name="ffn1_fwd")
    w_in = wts["w_in"]
    qkv = _mm(xh1, w_in, mode="nn", out_dtype=bf16, tm=tm, tn=512, tk=D_MODEL, name="qkv_fwd",
              affine=ln1, b_cols=(0, 1536))
    zl = _mm(xh1, w_in, mode="nn", out_dtype=f32, tm=tm, tn=512, tk=D_MODEL, name="zl_fwd",
             affine=ln1, b_cols=(3, 1024))
    zfg = _mm(xh1, w_in, mode="nn", out_dtype=f32, tm=tm, tn=LANES, tk=D_MODEL, name="zfg_fwd",
              affine=ln1, b_cols=(20, LANES))
    bfg = jnp.pad(small["b_forget"], ((0, 0), (0, LANES - HEADS)))
    cum = _cum_fwd(zfg, bfg, name="cum_fwd")
    cum_t = cum[:, :HEADS].T
    o, lse = _attn_fwd(qkv, cum, cum_t, name="attn_fwd")
    wa_bd = _block_diag(small["rg_wa"]).astype(bf16)
    wx_bd = _block_diag(small["rg_wx"]).astype(bf16)
    ba = small["rg_ba"].reshape(1, LRU_W)
    bx = small["rg_bx"].reshape(1, LRU_W)
    u, h, lru = _lru_fwd(zl, wts["conv_w"], small["conv_b"], wa_bd, ba, wx_bd, bx, small["lru_lambda"],
                         name="lru_fwd")
    w_out = wts["w_out"]
    xh2, rs2 = _mmln([(o, 0, FOX_W, w_out, 0, D_MODEL, "nn"), (lru, 0, LRU_W, w_out, 1, D_MODEL, "nn")],
                     tm=tm, name="mix_fwd", resid=("affine", xh1) + ln1, resid_scale=ALPHA, epi="ln_fwd")
    xh3, rs3, hg2, hu2 = _ffn_fwd(xh2, ln2[0], ln2[1], wts["wg2"], wts["wu2"], wts["wd2"], tm=tm, name="ffn2_fwd")

    dpre3, sq_rows, g_ln3g, g_ln3b = _loss_bwd(xh3, rs3, ln3[0], ln3[1], target, tm=tm, name="loss_bwd")
    dpre2, g_ln2g, g_ln2b, dhg2, dhu2, a2 = _ffn_bwd(dpre3, hg2, hu2, wts["wg2"], wts["wu2"], wts["wd2"],
                                                     (xh2, rs2, ln2[0]), tm=tm, name="ffn2_bwd")
    wgrad = dict(mode="tn", out_dtype=bf16, tm=D_MODEL, tn=FF_TILE, tk=512)
    g_wg2 = _mm(xh2, dhg2, name="g_wg2", affine=ln2, out_blocked=True, **wgrad)
    g_wu2 = _mm(xh2, dhu2, name="g_wu2", affine=ln2, out_blocked=True, **wgrad)
    g_wd2 = _mm(a2, dpre3, mode="tn", out_dtype=bf16, tm=512, tn=D_MODEL, tk=512, name="g_wd2", out_scale=0.5)

    dmix = _mmln([(dpre2, 0, D_MODEL, w_out, 0, D_MODEL, "nt")], tm=tm, name="dmix_bwd")
    g_wout_a = _mm(o, dpre2, mode="tn", out_dtype=bf16, tm=512, tn=D_MODEL, tk=512, name="g_wout_fox")
    g_wout_b = _mm(lru, dpre2, mode="tn", out_dtype=bf16, tm=512, tn=D_MODEL, tk=512, name="g_wout_lru")
    dlx, dlg, g_cw, g_cb, g_ba, g_bx, g_lam, g_wa4, g_wx4 = _lru_bwd(
        dmix, zl, u, h, wts["conv_w"], wa_bd, ba, wx_bd, bx, small["lru_lambda"], name="lru_bwd")
    delta = _attn_delta(dmix, o, tm=tm, name="attn_delta")
    dq, dcum_q = _attn_dq(qkv, dmix, cum, cum_t, lse, delta, name="attn_dq")
    dk, dv, dcum_k = _attn_dkv(qkv, dmix, cum, cum_t, lse[:, :HEADS].T, delta[:, :HEADS].T, name="attn_dkv")
    dfg, g_bf = _cum_bwd(dcum_q, dcum_k, zfg, bfg, name="cum_bwd")

    dz = [(dq, 0, 512), (dk, 1, 512), (dv, 2, 512), (dlx, 3, 512), (dlg, 4, 512), (dfg, 20, LANES)]
    dpre1, g_ln1g, g_ln1b = _mmln(
        [(arr, 0, w, w_in, cb, w, "nt") for (arr, cb, w) in dz],
        tm=tm, name="dx1_bwd", resid=("plain", dpre2), resid_scale=ALPHA, epi="ln_bwd", ln=(xh1, rs1, ln1[0]))
    g_win = [_mm(xh1, arr, mode="tn", out_dtype=bf16, tm=D_MODEL, tn=w, tk=512, name=f"g_win{n}", affine=ln1)
             for n, (arr, cb, w) in enumerate(dz)]
    grad_x, dhg1, dhu1, a1 = _ffn_bwd(dpre1, hg1, hu1, wts["wg1"], wts["wu1"], wts["wd1"], None,
                                      tm=tm, name="ffn1_bwd")
    g_wg1 = _mm(x, dhg1, name="g_wg1", out_blocked=True, **wgrad)
    g_wu1 = _mm(x, dhu1, name="g_wu1", out_blocked=True, **wgrad)
    g_wd1 = _mm(a1, dpre1, mode="tn", out_dtype=bf16, tm=512, tn=D_MODEL, tk=512, name="g_wd1", out_scale=0.5)

    g_win_full = jnp.concatenate([g[:, :w] for g, (_, _, w) in zip(g_win, dz)], axis=1)[:, :IN_COLS]
    big = {
        "ffn1_w_gate": g_wg1, "ffn1_w_up": g_wu1, "ffn1_w_down": g_wd1.reshape(N_DEV, FF_TILE, D_MODEL),
        "w_in": g_win_full.reshape(D_MODEL, N_DEV, IN_SHARD).transpose(1, 0, 2),
        "w_out": jnp.concatenate([g_wout_a, g_wout_b], axis=0).reshape(N_DEV, D_MODEL // N_DEV, D_MODEL),
        "ffn2_w_gate": g_wg2, "ffn2_w_up": g_wu2, "ffn2_w_down": g_wd2.reshape(N_DEV, FF_TILE, D_MODEL),
    }
    small_g = {
        "ln1_g": g_ln1g, "ln1_b": g_ln1b, "b_forget": g_bf[:, :HEADS], "conv_w": g_cw, "conv_b": g_cb,
        "rg_wa": _diag_blocks(g_wa4), "rg_ba": g_ba.reshape(HEADS, HEAD_D),
        "rg_wx": _diag_blocks(g_wx4), "rg_bx": g_bx.reshape(HEADS, HEAD_D), "lru_lambda": g_lam,
        "ln2_g": g_ln2g, "ln2_b": g_ln2b, "ln3_g": g_ln3g, "ln3_b": g_ln3b,
    }
    return sq_rows, grad_x, big, small_g


def _exchange(arrs, *, gather, name):
    n = len(arrs)
    n_peer = N_DEV - 1

    def body(*refs):
        ins, outs = refs[:n], refs[n:2 * n]
        send_sems, recv_sems, local_sems = refs[2 * n:]
        x, y, c = lax.axis_index("x"), lax.axis_index("y"), lax.axis_index("c")
        me = 4 * x + 2 * y + c
        copies = []
        for k in range(n):
            for d in range(1, N_DEV):
                px = 1 - x if d & 4 else x
                py = 1 - y if d & 2 else y
                pc = 1 - c if d & 1 else c
                src = ins[k] if gather else ins[k].at[4 * px + 2 * py + pc]
                cp = pltpu.make_async_remote_copy(
                    src_ref=src, dst_ref=outs[k].at[me],
                    send_sem=send_sems.at[k * n_peer + d - 1], recv_sem=recv_sems.at[k * n_peer + d - 1],
                    device_id=(px, py, pc), device_id_type=MESH_T)
                cp.start()
                copies.append(cp)
            own = pltpu.make_async_copy(ins[k] if gather else ins[k].at[me], outs[k].at[me], local_sems.at[k])
            own.start()
            copies.append(own)
        for cp in copies:
            cp.wait()

    hbm = pl.BlockSpec(memory_space=pl.ANY)
    out_shape = [jax.ShapeDtypeStruct(((N_DEV,) + a.shape) if gather else a.shape, a.dtype) for a in arrs]
    return pl.pallas_call(
        body, name=name, in_specs=[hbm] * n, out_specs=[hbm] * n, out_shape=out_shape,
        scratch_shapes=[pltpu.SemaphoreType.DMA((n * n_peer,)), pltpu.SemaphoreType.DMA((n * n_peer,)),
                        pltpu.SemaphoreType.DMA((n,))],
        compiler_params=pltpu.CompilerParams(has_side_effects=True),
    )(*arrs)


def _adam_math(w, g, m, v):
    m2 = ADAM_B1 * m + (1.0 - ADAM_B1) * g
    v2 = ADAM_B2 * v + (1.0 - ADAM_B2) * (g * g)
    m_hat = m2 / (1.0 - ADAM_B1 ** ADAM_STEP)
    v_hat = v2 / (1.0 - ADAM_B2 ** ADAM_STEP)
    delta = -ADAM_LR * (m_hat / (jnp.sqrt(v_hat) + ADAM_EPS) + ADAM_WD * w)
    return delta, m2, v2


def _adamw_big(parts, w, m, v, *, tr, name):
    r, c = w.shape

    def body(p_ref, w_ref, m_ref, v_ref, g_ref, d_ref, m2_ref, v2_ref):
        g = p_ref[0].astype(f32)
        for q in range(1, N_DEV):
            g = g + p_ref[q].astype(f32)
        d, m2, v2 = _adam_math(w_ref[...], g, m_ref[...], v_ref[...])
        g_ref[...] = g
        d_ref[...] = d
        m2_ref[...] = m2
        v2_ref[...] = v2

    blk = pl.BlockSpec((tr, c), lambda i: (i, 0))
    return pl.pallas_call(
        body, name=name, grid=(r // tr,),
        in_specs=[pl.BlockSpec((N_DEV, tr, c), lambda i: (0, i, 0)), blk, blk, blk],
        out_specs=[blk] * 4, out_shape=[jax.ShapeDtypeStruct((r, c), f32)] * 4,
        compiler_params=_params(("arbitrary",)),
    )(parts, w, m, v)


def _adamw_small(items, *, name):
    n = len(items)

    def body(*refs):
        ins, outs = refs[:4 * n], refs[4 * n:]
        for k in range(n):
            g, w, m, v = (ins[4 * k + q][...] for q in range(4))
            d, m2, v2 = _adam_math(w, g, m, v)
            outs[3 * k][...] = d
            outs[3 * k + 1][...] = m2
            outs[3 * k + 2][...] = v2

    vm = pl.BlockSpec(memory_space=pltpu.VMEM)
    flat = [a for item in items for a in item]
    out_shape = [jax.ShapeDtypeStruct(item[1].shape, f32) for item in items for _ in range(3)]
    return pl.pallas_call(
        body, name=name, in_specs=[vm] * (4 * n), out_specs=[vm] * (3 * n), out_shape=out_shape,
    )(*flat)


def _sum_parts(parts, *, name):
    def body(p_ref, o_ref):
        acc = p_ref[0]
        for q in range(1, N_DEV):
            acc = acc + p_ref[q]
        o_ref[...] = acc

    vm = pl.BlockSpec(memory_space=pltpu.VMEM)
    return pl.pallas_call(
        body, name=name, in_specs=[vm], out_specs=vm, out_shape=jax.ShapeDtypeStruct(parts.shape[1:], f32),
    )(parts)


WEIGHTS = ["ffn1_w_gate", "ffn1_w_up", "ffn1_w_down", "ln1_g", "ln1_b", "w_in", "b_forget", "conv_w", "conv_b",
           "rg_wa", "rg_ba", "rg_wx", "rg_bx", "lru_lambda", "w_out", "ln2_g", "ln2_b",
           "ffn2_w_gate", "ffn2_w_up", "ffn2_w_down", "ln3_g", "ln3_b"]
BIG = ["ffn1_w_gate", "ffn1_w_up", "ffn1_w_down", "w_in", "w_out", "ffn2_w_gate", "ffn2_w_up", "ffn2_w_down"]
PACKED = ["ln1_g", "ln1_b", "ln2_g", "ln2_b", "ln3_g", "ln3_b", "conv_b", "rg_ba", "rg_bx", "lru_lambda",
          "conv_w", "rg_wa", "rg_wx", "b_forget"]
PACK_ROWS = 600


def _two_d(a):
    return a.reshape((-1, a.shape[-1]))


def kernel(x, ffn1_w_gate, ffn1_w_up, ffn1_w_down, ln1_g, ln1_b, w_in, b_forget, conv_w, conv_b, rg_wa, rg_ba, rg_wx, rg_bx, lru_lambda, w_out, ln2_g, ln2_b, ffn2_w_gate, ffn2_w_up, ffn2_w_down, ln3_g, ln3_b, loss_target, m_ffn1_w_gate, m_ffn1_w_up, m_ffn1_w_down, m_ln1_g, m_ln1_b, m_w_in, m_b_forget, m_conv_w, m_conv_b, m_rg_wa, m_rg_ba, m_rg_wx, m_rg_bx, m_lru_lambda, m_w_out, m_ln2_g, m_ln2_b, m_ffn2_w_gate, m_ffn2_w_up, m_ffn2_w_down, m_ln3_g, m_ln3_b, v_ffn1_w_gate, v_ffn1_w_up, v_ffn1_w_down, v_ln1_g, v_ln1_b, v_w_in, v_b_forget, v_conv_w, v_conv_b, v_rg_wa, v_rg_ba, v_rg_wx, v_rg_bx, v_lru_lambda, v_w_out, v_ln2_g, v_ln2_b, v_ffn2_w_gate, v_ffn2_w_up, v_ffn2_w_down, v_ln3_g, v_ln3_b):
    w_args = (ffn1_w_gate, ffn1_w_up, ffn1_w_down, ln1_g, ln1_b, w_in, b_forget, conv_w, conv_b, rg_wa, rg_ba, rg_wx, rg_bx, lru_lambda, w_out, ln2_g, ln2_b, ffn2_w_gate, ffn2_w_up, ffn2_w_down, ln3_g, ln3_b)
    m_args = (m_ffn1_w_gate, m_ffn1_w_up, m_ffn1_w_down, m_ln1_g, m_ln1_b, m_w_in, m_b_forget, m_conv_w, m_conv_b, m_rg_wa, m_rg_ba, m_rg_wx, m_rg_bx, m_lru_lambda, m_w_out, m_ln2_g, m_ln2_b, m_ffn2_w_gate, m_ffn2_w_up, m_ffn2_w_down, m_ln3_g, m_ln3_b)
    v_args = (v_ffn1_w_gate, v_ffn1_w_up, v_ffn1_w_down, v_ln1_g, v_ln1_b, v_w_in, v_b_forget, v_conv_w, v_conv_b, v_rg_wa, v_rg_ba, v_rg_wx, v_rg_bx, v_lru_lambda, v_w_out, v_ln2_g, v_ln2_b, v_ffn2_w_gate, v_ffn2_w_up, v_ffn2_w_down, v_ln3_g, v_ln3_b)
    w = dict(zip(WEIGHTS, w_args))
    m = dict(zip(WEIGHTS, m_args))
    v = dict(zip(WEIGHTS, v_args))
    me = 4 * lax.axis_index("x") + 2 * lax.axis_index("y") + lax.axis_index("c")

    sent = [_two_d(w[n]).astype(bf16) for n in BIG] + [_two_d(w["conv_w"])]
    got = _exchange(sent, gather=True, name="gather_weights")
    full = dict(zip(BIG, got[:len(BIG)]))
    w_in_full = full["w_in"].transpose(1, 0, 2).reshape(D_MODEL, IN_COLS)
    wts = {
        "wg1": full["ffn1_w_gate"], "wu1": full["ffn1_w_up"], "wd1": full["ffn1_w_down"],
        "wg2": full["ffn2_w_gate"], "wu2": full["ffn2_w_up"], "wd2": full["ffn2_w_down"],
        "w_in": jnp.pad(w_in_full, ((0, 0), (0, 21 * LANES - IN_COLS))),
        "w_out": full["w_out"].reshape(D_MODEL, D_MODEL),
        "conv_w": got[len(BIG)].transpose(1, 0, 2).reshape(4, LRU_W),
    }
    small = {n: w[n] for n in ("ln1_g", "ln1_b", "ln2_g", "ln2_b", "ln3_g", "ln3_b", "b_forget", "conv_b",
                               "lru_lambda")}
    small.update({n: w[n][0] for n in ("rg_wa", "rg_ba", "rg_wx", "rg_bx")})

    sq_rows, grad_x, big_g, small_g = _local_step(x[0], loss_target[0], wts, small)
    loss = lax.psum(0.5 * jnp.sum(sq_rows) / D_MODEL, ("x", "y", "c"))

    parts = dict(zip(BIG, _exchange([big_g[n] for n in BIG], gather=False, name="scatter_grads")))

    pieces = [small_g[n].reshape(-1) for n in PACKED]
    packed = jnp.concatenate(pieces + [jnp.zeros((PACK_ROWS * LANES - sum(p.shape[0] for p in pieces),), f32)])
    (all_packed,) = _exchange([packed.reshape(PACK_ROWS, LANES)], gather=True, name="gather_small_grads")
    total = _sum_parts(all_packed, name="sum_small_grads").reshape(-1)
    grads, off = {}, 0
    for n, p in zip(PACKED, pieces):
        grads[n] = total[off:off + p.shape[0]].reshape(small_g[n].shape)
        off += p.shape[0]
    grads["conv_w"] = lax.dynamic_slice_in_dim(grads["conv_w"], me * (LRU_W // N_DEV), LRU_W // N_DEV, axis=1)

    delta, new_m, new_v = {}, {}, {}
    for n in BIG:
        w2 = _two_d(w[n])
        g, d, m2, v2 = _adamw_big(parts[n], w2, _two_d(m[n]), _two_d(v[n]), tr=128, name="adamw_" + n)
        grads[n], delta[n], new_m[n], new_v[n] = g, d, m2, v2
    small_names = [n for n in WEIGHTS if n not in BIG]
    outs = _adamw_small([(_two_d(grads[n]), _two_d(w[n]), _two_d(m[n]), _two_d(v[n])) for n in small_names],
                        name="adamw_small")
    for k, n in enumerate(small_names):
        delta[n], new_m[n], new_v[n] = outs[3 * k], outs[3 * k + 1], outs[3 * k + 2]

    def shaped(d):
        return [d[n].reshape(w[n].shape) for n in WEIGHTS]

    return (loss, grad_x[None], *shaped(grads), *shaped(delta), *shaped(new_m), *shaped(new_v))
```

```python
import functools
import math

import jax
import jax.numpy as jnp
from jax import lax
from jax.experimental import pallas as pl
from jax.experimental.pallas import tpu as pltpu

f32 = jnp.float32
bf16 = jnp.bfloat16

N_DEV = 8
D_MODEL = 1024
D_FF = 4096
FF_TILE = D_FF // N_DEV
FOX_W = 512
LRU_W = 512
HEADS = 8
HEAD_D = 64
IN_COLS = 2568
IN_SHARD = IN_COLS // N_DEV
LANES = 128
LN_EPS = 1e-5
ALPHA = 2.0 ** 0.25
ATT_SCALE = 1.0 / math.sqrt(HEAD_D)
LRU_C = 8.0
NEG_BIG = -1e30

ADAM_LR = 0.001
ADAM_B1 = 0.9
ADAM_B2 = 0.999
ADAM_EPS = 1e-08
ADAM_WD = 0.01
ADAM_STEP = 10

VMEM_LIMIT = 56 * 1024 * 1024
MESH_T = pl.DeviceIdType.MESH


def _params(sem, **kw):
    return pltpu.CompilerParams(dimension_semantics=sem, vmem_limit_bytes=VMEM_LIMIT, **kw)


def _sigmoid(x):
    return 1.0 / (1.0 + jnp.exp(-x))


def _softplus(x):
    return jnp.maximum(x, 0.0) + jnp.log(1.0 + jnp.exp(-jnp.abs(x)))


def _one_minus_exp(x):
    series = -x * (1.0 + x * (0.5 + x * (1.0 / 6 + x * (1.0 / 24 + x * (1.0 / 120 + x * (1.0 / 720))))))
    return jnp.where(x > -0.125, series, 1.0 - jnp.exp(x))


_GELU_C = math.sqrt(2.0 / math.pi)


def _gelu_and_grad(x):
    inner = _GELU_C * (x + 0.044715 * x * x * x)
    t = jnp.tanh(inner)
    g = 0.5 * x * (1.0 + t)
    dg = 0.5 * (1.0 + t) + 0.5 * x * (1.0 - t * t) * _GELU_C * (1.0 + 3 * 0.044715 * x * x)
    return g, dg


def _ln_fwd_tile(pre):
    mu = jnp.mean(pre, axis=-1, keepdims=True)
    xc = pre - mu
    var = jnp.mean(xc * xc, axis=-1, keepdims=True)
    rstd = lax.rsqrt(var + LN_EPS)
    return xc * rstd, rstd


def _ln_bwd_tile(dy, xhat, rstd, g):
    dyg = dy * g
    m1 = jnp.mean(dyg, axis=-1, keepdims=True)
    m2 = jnp.mean(dyg * xhat, axis=-1, keepdims=True)
    dpre = rstd * (dyg - m1 - xhat * m2)
    return dpre, jnp.sum(dy * xhat, axis=0, keepdims=True), jnp.sum(dy, axis=0, keepdims=True)


_NT = (((1,), (1,)), ((), ()))
_TN = (((0,), (0,)), ((), ()))


class _Exchange:
    def __init__(self, arrs, gather):
        self.arrs, self.gather, self.n = list(arrs), gather, len(arrs)

    def out_shape(self):
        return [jax.ShapeDtypeStruct(((N_DEV,) + a.shape) if self.gather else a.shape, a.dtype) for a in self.arrs]

    def scratch(self):
        n_remote = self.n * (N_DEV - 1)
        return [pltpu.SemaphoreType.DMA((n_remote,)), pltpu.SemaphoreType.DMA((n_remote,)),
                pltpu.SemaphoreType.DMA((self.n,))]

    def copies(self, ins, outs, sems):
        send_sems, recv_sems, local_sems = sems
        x, y, c = lax.axis_index("x"), lax.axis_index("y"), lax.axis_index("c")
        me = 4 * x + 2 * y + c
        out = []
        for k in range(self.n):
            for d in range(1, N_DEV):
                px = 1 - x if d & 4 else x
                py = 1 - y if d & 2 else y
                pc = 1 - c if d & 1 else c
                src = ins[k] if self.gather else ins[k].at[4 * px + 2 * py + pc]
                sem = k * (N_DEV - 1) + d - 1
                out.append(pltpu.make_async_remote_copy(
                    src_ref=src, dst_ref=outs[k].at[me], send_sem=send_sems.at[sem], recv_sem=recv_sems.at[sem],
                    device_id=(px, py, pc), device_id_type=MESH_T))
            out.append(pltpu.make_async_copy(ins[k] if self.gather else ins[k].at[me], outs[k].at[me],
                                             local_sems.at[k]))
        return out

    def start(self, ins, outs, sems):
        for cp in self.copies(ins, outs, sems):
            cp.start()

    def wait(self, ins, outs, sems):
        for cp in self.copies(ins, outs, sems):
            cp.wait()


def _hosted_call(host, body, *, name, grid, in_specs, out_specs, out_shape, scratch_shapes=(), compiler_params):
    out_specs = list(out_specs) if isinstance(out_specs, (list, tuple)) else [out_specs]
    out_shape = list(out_shape) if isinstance(out_shape, (list, tuple)) else [out_shape]
    if host is None:
        return pl.pallas_call(body, name=name, grid=grid, in_specs=in_specs, out_specs=out_specs,
                              out_shape=out_shape, scratch_shapes=list(scratch_shapes),
                              compiler_params=compiler_params)
    n_in, n_out, n_scr, k = len(in_specs), len(out_shape), len(scratch_shapes), host.n

    def wrapped(*refs):
        ins, h_in = refs[:n_in], refs[n_in:n_in + k]
        outs, h_out = refs[n_in + k:n_in + k + n_out], refs[n_in + k + n_out:n_in + 2 * k + n_out]
        scr, sems = refs[n_in + 2 * k + n_out:n_in + 2 * k + n_out + n_scr], refs[n_in + 2 * k + n_out + n_scr:]
        ids = [pl.program_id(a) for a in range(len(grid))]
        first = functools.reduce(jnp.logical_and, [i == 0 for i in ids])
        last = functools.reduce(jnp.logical_and, [i == g - 1 for i, g in zip(ids, grid)])

        @pl.when(first)
        def _():
            host.start(h_in, h_out, sems)

        body(*ins, *outs, *scr)

        @pl.when(last)
        def _():
            host.wait(h_in, h_out, sems)

    hbm = pl.BlockSpec(memory_space=pl.ANY)
    call = pl.pallas_call(
        wrapped, name=name, grid=grid, in_specs=list(in_specs) + [hbm] * k, out_specs=out_specs + [hbm] * k,
        out_shape=out_shape + host.out_shape(), scratch_shapes=list(scratch_shapes) + host.scratch(),
        compiler_params=compiler_params)
    return lambda *args: call(*args, *host.arrs)


def _exchange(arrs, *, gather, name):
    host = _Exchange(arrs, gather)

    def body(*refs):
        ins, outs, sems = refs[:host.n], refs[host.n:2 * host.n], refs[2 * host.n:]
        host.start(ins, outs, sems)
        host.wait(ins, outs, sems)

    hbm = pl.BlockSpec(memory_space=pl.ANY)
    return pl.pallas_call(
        body, name=name, in_specs=[hbm] * host.n, out_specs=[hbm] * host.n, out_shape=host.out_shape(),
        scratch_shapes=host.scratch(), compiler_params=pltpu.CompilerParams(has_side_effects=True),
    )(*arrs)


def _ffn_fwd(xhat, g_in, b_in, wg, wu, wd, *, tm, name, host=None):
    t = xhat.shape[0]
    nj = N_DEV

    def body(x_ref, g_ref, b_ref, wg_ref, wu_ref, wd_ref, xo_ref, rstd_ref, hg_ref, hu_ref, xb, acc):
        j = pl.program_id(1)

        @pl.when(j == 0)
        def _():
            xb[...] = (x_ref[...] * g_ref[...] + b_ref[...]).astype(bf16)
            acc[...] = jnp.zeros_like(acc)

        hg = jnp.dot(xb[...], wg_ref[...], preferred_element_type=f32)
        hu = jnp.dot(xb[...], wu_ref[...], preferred_element_type=f32)
        hg_ref[...] = hg.astype(bf16)
        hu_ref[...] = hu.astype(bf16)
        a = hg * _sigmoid(hg) * hu
        acc[...] += jnp.dot(a.astype(bf16), wd_ref[...], preferred_element_type=f32)

        @pl.when(j == nj - 1)
        def _():
            x = x_ref[...] * g_ref[...] + b_ref[...]
            xo, rstd = _ln_fwd_tile(ALPHA * x + 0.5 * acc[...])
            xo_ref[...] = xo
            rstd_ref[...] = rstd

    row = pl.BlockSpec((1, D_MODEL), lambda i, j: (0, 0))
    return _hosted_call(
        host, body, name=name, grid=(t // tm, nj),
        in_specs=[pl.BlockSpec((tm, D_MODEL), lambda i, j: (i, 0)), row, row,
                  pl.BlockSpec((None, D_MODEL, FF_TILE), lambda i, j: (j, 0, 0)),
                  pl.BlockSpec((None, D_MODEL, FF_TILE), lambda i, j: (j, 0, 0)),
                  pl.BlockSpec((None, FF_TILE, D_MODEL), lambda i, j: (j, 0, 0))],
        out_specs=[pl.BlockSpec((tm, D_MODEL), lambda i, j: (i, 0)),
                   pl.BlockSpec((tm, 1), lambda i, j: (i, 0)),
                   pl.BlockSpec((tm, FF_TILE), lambda i, j: (i, j)),
                   pl.BlockSpec((tm, FF_TILE), lambda i, j: (i, j))],
        out_shape=[jax.ShapeDtypeStruct((t, D_MODEL), f32), jax.ShapeDtypeStruct((t, 1), f32),
                   jax.ShapeDtypeStruct((t, D_FF), bf16), jax.ShapeDtypeStruct((t, D_FF), bf16)],
        scratch_shapes=[pltpu.VMEM((tm, D_MODEL), bf16), pltpu.VMEM((tm, D_MODEL), f32)],
        compiler_params=_params(("arbitrary", "arbitrary")),
    )(xhat, g_in, b_in, wg, wu, wd)


def _ffn_bwd(dpre, hg, hu, wg, wu, wd, ln_in, *, tm, name, host=None):
    t = dpre.shape[0]
    nj = N_DEV
    with_ln = ln_in is not None

    def body(*refs):
        if with_ln:
            (dp_ref, hg_ref, hu_ref, wg_ref, wu_ref, wd_ref, xh_ref, rs_ref, g_ref,
             dx_ref, gg_ref, gb_ref, dhg_ref, dhu_ref, a_ref, dfb, acc) = refs
        else:
            (dp_ref, hg_ref, hu_ref, wg_ref, wu_ref, wd_ref,
             dx_ref, dhg_ref, dhu_ref, a_ref, dfb, acc) = refs
        i = pl.program_id(0)
        j = pl.program_id(1)

        @pl.when(j == 0)
        def _():
            dfb[...] = (0.5 * dp_ref[...]).astype(bf16)
            acc[...] = jnp.zeros_like(acc)

        da = lax.dot_general(dfb[...], wd_ref[...], _NT, preferred_element_type=f32)
        hgv = hg_ref[...].astype(f32)
        huv = hu_ref[...].astype(f32)
        sg = _sigmoid(hgv)
        silu = hgv * sg
        a_ref[...] = (silu * huv).astype(bf16)
        dhu = (da * silu).astype(bf16)
        dhg = (da * huv * (sg * (1.0 + hgv * (1.0 - sg)))).astype(bf16)
        dhg_ref[...] = dhg
        dhu_ref[...] = dhu
        acc[...] += (lax.dot_general(dhg, wg_ref[...], _NT, preferred_element_type=f32)
                     + lax.dot_general(dhu, wu_ref[...], _NT, preferred_element_type=f32))

        @pl.when(j == nj - 1)
        def _():
            dx = ALPHA * dp_ref[...] + acc[...]
            if with_ln:
                dprev, gg, gb = _ln_bwd_tile(dx, xh_ref[...], rs_ref[...], g_ref[...])
                dx_ref[...] = dprev

                @pl.when(i == 0)
                def _():
                    gg_ref[...] = gg
                    gb_ref[...] = gb

                @pl.when(i > 0)
                def _():
                    gg_ref[...] += gg
                    gb_ref[...] += gb
            else:
                dx_ref[...] = dx

    tok = pl.BlockSpec((tm, D_MODEL), lambda i, j: (i, 0))
    row = pl.BlockSpec((1, D_MODEL), lambda i, j: (0, 0))
    hid = pl.BlockSpec((tm, FF_TILE), lambda i, j: (i, j))
    in_specs = [tok, hid, hid,
                pl.BlockSpec((None, D_MODEL, FF_TILE), lambda i, j: (j, 0, 0)),
                pl.BlockSpec((None, D_MODEL, FF_TILE), lambda i, j: (j, 0, 0)),
                pl.BlockSpec((None, FF_TILE, D_MODEL), lambda i, j: (j, 0, 0))]
    args = [dpre, hg, hu, wg, wu, wd]
    out_specs = [tok]
    out_shape = [jax.ShapeDtypeStruct((t, D_MODEL), f32)]
    if with_ln:
        in_specs += [tok, pl.BlockSpec((tm, 1), lambda i, j: (i, 0)), row]
        args += list(ln_in)
        out_specs += [row, row]
        out_shape += [jax.ShapeDtypeStruct((1, D_MODEL), f32)] * 2
    out_specs += [hid, hid, hid]
    out_shape += [jax.ShapeDtypeStruct((t, D_FF), bf16)] * 3
    return _hosted_call(
        host, body, name=name, grid=(t // tm, nj), in_specs=in_specs, out_specs=out_specs, out_shape=out_shape,
        scratch_shapes=[pltpu.VMEM((tm, D_MODEL), bf16), pltpu.VMEM((tm, D_MODEL), f32)],
        compiler_params=_params(("arbitrary", "arbitrary")),
    )(*args)


def _mm(a, b, *, mode, out_dtype, tm, tn, tk, name, affine=None, a_cols=None, b_cols=None,
        b_blocked=False, out_blocked=False, out_scale=None):
    if mode == "nn":
        m_full, k_full = a.shape
        m_dim, k_dim = (m_full, a_cols[1]) if a_cols else (m_full, k_full)
    else:
        k_dim, m_full = a.shape
        m_dim = a_cols[1] if a_cols else m_full
    a_off = a_cols[0] if a_cols else 0
    if b_blocked:
        n_dim = b.shape[0] * b.shape[2]
        assert b.shape[2] == tn
    else:
        n_dim = b_cols[1] if b_cols else b.shape[1]
    b_off = b_cols[0] if b_cols else 0
    assert m_dim % tm == 0 and n_dim % tn == 0 and k_dim % tk == 0, (name, m_dim, n_dim, k_dim)
    nk = k_dim // tk

    def body(*refs):
        if affine is not None:
            a_ref, g_ref, s_ref, b_ref, o_ref, acc = refs
        else:
            a_ref, b_ref, o_ref, acc = refs
        k = pl.program_id(2)

        @pl.when(k == 0)
        def _():
            acc[...] = jnp.zeros_like(acc)

        av = a_ref[...]
        if affine is not None:
            av = av * g_ref[...] + s_ref[...]
        av = av.astype(bf16)
        bv = b_ref[...].astype(bf16)
        if mode == "nn":
            acc[...] += jnp.dot(av, bv, preferred_element_type=f32)
        else:
            acc[...] += lax.dot_general(av, bv, _TN, preferred_element_type=f32)

        @pl.when(k == nk - 1)
        def _():
            res = acc[...] if out_scale is None else acc[...] * out_scale
            o_ref[...] = res.astype(out_dtype)

    if mode == "nn":
        a_spec = pl.BlockSpec((tm, tk), lambda i, j, k: (i, k + a_off))
        aff_spec = pl.BlockSpec((1, tk), lambda i, j, k: (0, k + a_off))
    else:
        a_spec = pl.BlockSpec((tk, tm), lambda i, j, k: (k, i + a_off))
        aff_spec = pl.BlockSpec((1, tm), lambda i, j, k: (0, i + a_off))
    if b_blocked:
        b_spec = pl.BlockSpec((None, tk, tn), lambda i, j, k: (j, k, 0))
    else:
        b_spec = pl.BlockSpec((tk, tn), lambda i, j, k: (k, j + b_off))
    if out_blocked:
        o_spec = pl.BlockSpec((None, tm, tn), lambda i, j, k: (j, i, 0))
        o_shape = jax.ShapeDtypeStruct((n_dim // tn, m_dim, tn), out_dtype)
    else:
        o_spec = pl.BlockSpec((tm, tn), lambda i, j, k: (i, j))
        o_shape = jax.ShapeDtypeStruct((m_dim, n_dim), out_dtype)
    in_specs = [a_spec] + ([aff_spec, aff_spec] if affine is not None else []) + [b_spec]
    args = [a] + (list(affine) if affine is not None else []) + [b]
    return pl.pallas_call(
        body, name=name, grid=(m_dim // tm, n_dim // tn, nk), in_specs=in_specs, out_specs=o_spec,
        out_shape=o_shape, scratch_shapes=[pltpu.VMEM((tm, tn), f32)],
        compiler_params=_params(("arbitrary", "arbitrary", "arbitrary")),
    )(*args)


def _mmln(pairs, *, tm, name, resid=None, resid_scale=1.0, epi=None, ln=None, n_out=D_MODEL):
    t = pairs[0][0].shape[0]
    n_pairs = len(pairs)
    n_resid = 0 if resid is None else len(resid) - 1

    def body(*refs):
        pos = 0
        val = None
        for p in range(n_pairs):
            a_ref, b_ref = refs[pos], refs[pos + 1]
            pos += 2
            av = a_ref[...].astype(bf16)
            bv = b_ref[...].astype(bf16)
            if pairs[p][6] == "nn":
                term = jnp.dot(av, bv, preferred_element_type=f32)
            else:
                term = lax.dot_general(av, bv, _NT, preferred_element_type=f32)
            val = term if val is None else val + term
        if resid is not None:
            if resid[0] == "plain":
                r = refs[pos][...]
            else:
                r = refs[pos][...] * refs[pos + 1][...] + refs[pos + 2][...]
            pos += n_resid
            val = val + resid_scale * r
        if epi is None:
            o_ref = refs[pos]
            o_ref[...] = val.astype(o_ref.dtype)
        elif epi == "ln_fwd":
            xo, rstd = _ln_fwd_tile(val)
            refs[pos][...] = xo
            refs[pos + 1][...] = rstd
        else:
            xh_ref, rs_ref, g_ref, dx_ref, gg_ref, gb_ref = refs[pos:pos + 6]
            dprev, gg, gb = _ln_bwd_tile(val, xh_ref[...], rs_ref[...], g_ref[...])
            dx_ref[...] = dprev
            i = pl.program_id(0)

            @pl.when(i == 0)
            def _():
                gg_ref[...] = gg
                gb_ref[...] = gb

            @pl.when(i > 0)
            def _():
                gg_ref[...] += gg
                gb_ref[...] += gb

    in_specs, args = [], []
    for (a, acb, aw, b, bcb, bw, mode) in pairs:
        in_specs.append(pl.BlockSpec((tm, aw), lambda i, acb=acb: (i, acb)))
        args.append(a)
        if mode == "nn":
            in_specs.append(pl.BlockSpec((aw, n_out), lambda i, bcb=bcb: (bcb, 0)))
        else:
            in_specs.append(pl.BlockSpec((n_out, bw), lambda i, bcb=bcb: (0, bcb)))
        args.append(b)
    tok = pl.BlockSpec((tm, n_out), lambda i: (i, 0))
    row = pl.BlockSpec((1, n_out), lambda i: (0, 0))
    col = pl.BlockSpec((tm, 1), lambda i: (i, 0))
    if resid is not None:
        in_specs += [tok] if resid[0] == "plain" else [tok, row, row]
        args += list(resid[1:])
    if epi is None:
        out_specs, out_shape = tok, jax.ShapeDtypeStruct((t, n_out), f32)
    elif epi == "ln_fwd":
        out_specs = [tok, col]
        out_shape = [jax.ShapeDtypeStruct((t, n_out), f32), jax.ShapeDtypeStruct((t, 1), f32)]
    else:
        in_specs += [tok, col, row]
        args += list(ln)
        out_specs = [tok, row, row]
        out_shape = [jax.ShapeDtypeStruct((t, n_out), f32)] + [jax.ShapeDtypeStruct((1, n_out), f32)] * 2
    return pl.pallas_call(
        body, name=name, grid=(t // tm,), in_specs=in_specs, out_specs=out_specs, out_shape=out_shape,
        compiler_params=_params(("arbitrary",)),
    )(*args)


def _loss_bwd(xhat, rstd, g, b, target, *, tm, name):
    t = xhat.shape[0]

    def body(xh_ref, rs_ref, g_ref, b_ref, tg_ref, dx_ref, sq_ref, gg_ref, gb_ref):
        i = pl.program_id(0)
        xh = xh_ref[...]
        diff = xh * g_ref[...] + b_ref[...] - tg_ref[...]
        sq = jnp.sum(diff * diff, axis=0, keepdims=True)
        dprev, gg, gb = _ln_bwd_tile(diff * (1.0 / D_MODEL), xh, rs_ref[...], g_ref[...])
        dx_ref[...] = dprev

        @pl.when(i == 0)
        def _():
            sq_ref[...] = sq
            gg_ref[...] = gg
            gb_ref[...] = gb

        @pl.when(i > 0)
        def _():
            sq_ref[...] += sq
            gg_ref[...] += gg
            gb_ref[...] += gb

    tok = pl.BlockSpec((tm, D_MODEL), lambda i: (i, 0))
    row = pl.BlockSpec((1, D_MODEL), lambda i: (0, 0))
    return pl.pallas_call(
        body, name=name, grid=(t // tm,),
        in_specs=[tok, pl.BlockSpec((tm, 1), lambda i: (i, 0)), row, row, tok],
        out_specs=[tok, row, row, row],
        out_shape=[jax.ShapeDtypeStruct((t, D_MODEL), f32)] + [jax.ShapeDtypeStruct((1, D_MODEL), f32)] * 3,
        compiler_params=_params(("arbitrary",)),
    )(xhat, rstd, g, b, target)


CUM_TILE = 256


def _tri(n, lower):
    r = lax.broadcasted_iota(jnp.int32, (n, n), 0)
    c = lax.broadcasted_iota(jnp.int32, (n, n), 1)
    return jnp.where((r >= c) if lower else (r <= c), 1.0, 0.0).astype(f32)


def _cum_fwd(zfg, bfg, *, name):
    t = zfg.shape[0]

    def body(z_ref, b_ref, o_ref, carry):
        @pl.when(pl.program_id(0) == 0)
        def _():
            carry[...] = jnp.zeros_like(carry)

        ls = -_softplus(-(z_ref[...] + b_ref[...]))
        c = jnp.dot(_tri(CUM_TILE, True), ls, preferred_element_type=f32,
                    precision=lax.Precision.HIGHEST) + carry[...]
        o_ref[...] = c
        carry[...] = c[CUM_TILE - 1:CUM_TILE, :]

    blk = pl.BlockSpec((CUM_TILE, LANES), lambda i: (i, 0))
    return pl.pallas_call(
        body, name=name, grid=(t // CUM_TILE,),
        in_specs=[blk, pl.BlockSpec((1, LANES), lambda i: (0, 0))], out_specs=blk,
        out_shape=jax.ShapeDtypeStruct((t, LANES), f32), scratch_shapes=[pltpu.VMEM((1, LANES), f32)],
        compiler_params=_params(("arbitrary",)),
    )(zfg, bfg)


def _cum_bwd(dcum_q, dcum_k, zfg, bfg, *, name):
    t = zfg.shape[0]
    n = t // CUM_TILE

    def body(d_ref, d2_ref, z_ref, b_ref, o_ref, s_ref, carry):
        i = pl.program_id(0)

        @pl.when(i == 0)
        def _():
            carry[...] = jnp.zeros_like(carry)

        dls = jnp.dot(_tri(CUM_TILE, False), d_ref[...] + d2_ref[...], preferred_element_type=f32,
                      precision=lax.Precision.HIGHEST) + carry[...]
        carry[...] = dls[0:1, :]
        lane = lax.broadcasted_iota(jnp.int32, (CUM_TILE, LANES), 1)
        dfg = jnp.where(lane < HEADS, dls * _sigmoid(-(z_ref[...] + b_ref[...])), 0.0)
        o_ref[...] = dfg
        tot = jnp.sum(dfg, axis=0, keepdims=True)

        @pl.when(i == 0)
        def _():
            s_ref[...] = tot

        @pl.when(i > 0)
        def _():
            s_ref[...] += tot

    blk = pl.BlockSpec((CUM_TILE, LANES), lambda i: (n - 1 - i, 0))
    row = pl.BlockSpec((1, LANES), lambda i: (0, 0))
    return pl.pallas_call(
        body, name=name, grid=(n,), in_specs=[blk, blk, blk, row], out_specs=[blk, row],
        out_shape=[jax.ShapeDtypeStruct((t, LANES), f32), jax.ShapeDtypeStruct((1, LANES), f32)],
        scratch_shapes=[pltpu.VMEM((1, LANES), f32)],
        compiler_params=_params(("arbitrary",)),
    )(dcum_q, dcum_k, zfg, bfg)


ATT_TILE = 512


def _causal(i, j, transposed):
    r = lax.broadcasted_iota(jnp.int32, (ATT_TILE, ATT_TILE), 0)
    c = lax.broadcasted_iota(jnp.int32, (ATT_TILE, ATT_TILE), 1)
    if transposed:
        return (c + i * ATT_TILE) >= (r + j * ATT_TILE)
    return (r + i * ATT_TILE) >= (c + j * ATT_TILE)


def _attn_fwd(qkv, cum, cum_t, *, name, host=None):
    t = qkv.shape[0]
    n = t // ATT_TILE
    tq = ATT_TILE

    def body(q_ref, k_ref, v_ref, cq_ref, ck_ref, o_ref, lse_ref, acc, m_s, l_s):
        i = pl.program_id(0)
        j = pl.program_id(1)

        @pl.when(j == 0)
        def _():
            acc[...] = jnp.zeros_like(acc)
            m_s[...] = jnp.full_like(m_s, NEG_BIG)
            l_s[...] = jnp.zeros_like(l_s)

        @pl.when(j <= i)
        def _():
            mask = _causal(i, j, False)
            for h in range(HEADS):
                hs = slice(HEAD_D * h, HEAD_D * (h + 1))
                s = lax.dot_general(q_ref[:, hs], k_ref[:, hs], _NT, preferred_element_type=f32) * ATT_SCALE
                s = s + cq_ref[:, h:h + 1] - ck_ref[h:h + 1, :]
                s = jnp.where(mask, s, NEG_BIG)
                m_old = m_s[:, h:h + 1]
                m_new = jnp.maximum(m_old, jnp.max(s, axis=-1, keepdims=True))
                corr = jnp.exp(m_old - m_new)
                p = jnp.exp(s - m_new)
                l_s[:, h:h + 1] = corr * l_s[:, h:h + 1] + jnp.sum(p, axis=-1, keepdims=True)
                acc[:, hs] = corr * acc[:, hs] + jnp.dot(p.astype(bf16), v_ref[:, hs], preferred_element_type=f32)
                m_s[:, h:h + 1] = m_new

        @pl.when(j == i)
        def _():
            lse_ref[...] = jnp.zeros_like(lse_ref)
            for h in range(HEADS):
                hs = slice(HEAD_D * h, HEAD_D * (h + 1))
                l = l_s[:, h:h + 1]
                o_ref[:, hs] = acc[:, hs] / l
                lse_ref[:, h:h + 1] = m_s[:, h:h + 1] + jnp.log(l)

    return _hosted_call(
        host, body, name=name, grid=(n, n),
        in_specs=[pl.BlockSpec((tq, FOX_W), lambda i, j: (i, 0)),
                  pl.BlockSpec((tq, FOX_W), lambda i, j: (jnp.minimum(i, j), 1)),
                  pl.BlockSpec((tq, FOX_W), lambda i, j: (jnp.minimum(i, j), 2)),
                  pl.BlockSpec((tq, LANES), lambda i, j: (i, 0)),
                  pl.BlockSpec((HEADS, tq), lambda i, j: (0, jnp.minimum(i, j)))],
        out_specs=[pl.BlockSpec((tq, FOX_W), lambda i, j: (i, 0)), pl.BlockSpec((tq, LANES), lambda i, j: (i, 0))],
        out_shape=[jax.ShapeDtypeStruct((t, FOX_W), f32), jax.ShapeDtypeStruct((t, LANES), f32)],
        scratch_shapes=[pltpu.VMEM((tq, FOX_W), f32), pltpu.VMEM((tq, LANES), f32), pltpu.VMEM((tq, LANES), f32)],
        compiler_params=_params(("arbitrary", "arbitrary")),
    )(qkv, qkv, qkv, cum, cum_t)


def _attn_delta(dmix, o, *, tm, name):
    t = o.shape[0]

    def body(do_ref, o_ref, d_ref):
        r = lax.broadcasted_iota(jnp.int32, (FOX_W, LANES), 0)
        c = lax.broadcasted_iota(jnp.int32, (FOX_W, LANES), 1)
        pick = jnp.where(r // HEAD_D == c, 1.0, 0.0).astype(f32)
        d_ref[...] = jnp.dot(do_ref[...] * o_ref[...], pick, preferred_element_type=f32,
                             precision=lax.Precision.HIGHEST)

    blk = pl.BlockSpec((tm, FOX_W), lambda i: (i, 0))
    return pl.pallas_call(
        body, name=name, grid=(t // tm,), in_specs=[blk, blk],
        out_specs=pl.BlockSpec((tm, LANES), lambda i: (i, 0)),
        out_shape=jax.ShapeDtypeStruct((t, LANES), f32), compiler_params=_params(("arbitrary",)),
    )(dmix, o)


def _attn_dq(qkv, dmix, cum, cum_t, lse, delta, *, name, host=None):
    t = qkv.shape[0]
    n = t // ATT_TILE
    tq = ATT_TILE

    def body(q_ref, k_ref, v_ref, do_ref, cq_ref, ck_ref, lse_ref, dl_ref, dq_ref, dc_ref, acc, dc_acc):
        i = pl.program_id(0)
        j = pl.program_id(1)

        @pl.when(j == 0)
        def _():
            acc[...] = jnp.zeros_like(acc)
            dc_acc[...] = jnp.zeros_like(dc_acc)

        @pl.when(j <= i)
        def _():
            mask = _causal(i, j, False)
            for h in range(HEADS):
                hs = slice(HEAD_D * h, HEAD_D * (h + 1))
                kh = k_ref[:, hs]
                s = lax.dot_general(q_ref[:, hs], kh, _NT, preferred_element_type=f32) * ATT_SCALE
                s = s + cq_ref[:, h:h + 1] - ck_ref[h:h + 1, :]
                s = jnp.where(mask, s, NEG_BIG)
                p = jnp.exp(s - lse_ref[:, h:h + 1])
                dp = lax.dot_general(do_ref[:, hs].astype(bf16), v_ref[:, hs], _NT, preferred_element_type=f32)
                ds = p * (dp - dl_ref[:, h:h + 1])
                acc[:, hs] += jnp.dot((ds * ATT_SCALE).astype(bf16), kh, preferred_element_type=f32)
                dc_acc[:, h:h + 1] += jnp.sum(ds, axis=-1, keepdims=True)

        @pl.when(j == i)
        def _():
            dq_ref[...] = acc[...].astype(bf16)
            dc_ref[...] = dc_acc[...]

    col = pl.BlockSpec((tq, LANES), lambda i, j: (i, 0))
    return _hosted_call(
        host, body, name=name, grid=(n, n),
        in_specs=[pl.BlockSpec((tq, FOX_W), lambda i, j: (i, 0)),
                  pl.BlockSpec((tq, FOX_W), lambda i, j: (jnp.minimum(i, j), 1)),
                  pl.BlockSpec((tq, FOX_W), lambda i, j: (jnp.minimum(i, j), 2)),
                  pl.BlockSpec((tq, FOX_W), lambda i, j: (i, 0)),
                  col, pl.BlockSpec((HEADS, tq), lambda i, j: (0, jnp.minimum(i, j))), col, col],
        out_specs=[pl.BlockSpec((tq, FOX_W), lambda i, j: (i, 0)), col],
        out_shape=[jax.ShapeDtypeStruct((t, FOX_W), bf16), jax.ShapeDtypeStruct((t, LANES), f32)],
        scratch_shapes=[pltpu.VMEM((tq, FOX_W), f32), pltpu.VMEM((tq, LANES), f32)],
        compiler_params=_params(("arbitrary", "arbitrary")),
    )(qkv, qkv, qkv, dmix, cum, cum_t, lse, delta)


def _attn_dkv(qkv, dmix, cum, cum_t, lse_t, delta_t, *, name):
    t = qkv.shape[0]
    n = t // ATT_TILE
    tk = ATT_TILE

    def body(q_ref, k_ref, v_ref, do_ref, cq_ref, ck_ref, lse_ref, dl_ref, dk_ref, dv_ref, dc_ref, dk_acc, dv_acc, dc_acc):
        j = pl.program_id(0)
        i = pl.program_id(1)

        @pl.when(i == 0)
        def _():
            dk_acc[...] = jnp.zeros_like(dk_acc)
            dv_acc[...] = jnp.zeros_like(dv_acc)
            dc_acc[...] = jnp.zeros_like(dc_acc)

        @pl.when(i >= j)
        def _():
            mask = _causal(i, j, True)
            for h in range(HEADS):
                hs = slice(HEAD_D * h, HEAD_D * (h + 1))
                qh = q_ref[:, hs]
                doh = do_ref[:, hs].astype(bf16)
                s_t = lax.dot_general(k_ref[:, hs], qh, _NT, preferred_element_type=f32) * ATT_SCALE
                s_t = s_t + cq_ref[h:h + 1, :] - ck_ref[:, h:h + 1]
                s_t = jnp.where(mask, s_t, NEG_BIG)
                p_t = jnp.exp(s_t - lse_ref[h:h + 1, :])
                dv_acc[:, hs] += jnp.dot(p_t.astype(bf16), doh, preferred_element_type=f32)
                dp_t = lax.dot_general(v_ref[:, hs], doh, _NT, preferred_element_type=f32)
                ds_t = p_t * (dp_t - dl_ref[h:h + 1, :])
                dk_acc[:, hs] += jnp.dot((ds_t * ATT_SCALE).astype(bf16), qh, preferred_element_type=f32)
                dc_acc[:, h:h + 1] -= jnp.sum(ds_t, axis=-1, keepdims=True)

        @pl.when(i == n - 1)
        def _():
            dk_ref[...] = dk_acc[...].astype(bf16)
            dv_ref[...] = dv_acc[...].astype(bf16)
            dc_ref[...] = dc_acc[...]

    rowq = pl.BlockSpec((HEADS, tk), lambda j, i: (0, jnp.maximum(i, j)))
    return pl.pallas_call(
        body, name=name, grid=(n, n),
        in_specs=[pl.BlockSpec((tk, FOX_W), lambda j, i: (jnp.maximum(i, j), 0)),
                  pl.BlockSpec((tk, FOX_W), lambda j, i: (j, 1)),
                  pl.BlockSpec((tk, FOX_W), lambda j, i: (j, 2)),
                  pl.BlockSpec((tk, FOX_W), lambda j, i: (jnp.maximum(i, j), 0)),
                  rowq, pl.BlockSpec((tk, LANES), lambda j, i: (j, 0)), rowq, rowq],
        out_specs=[pl.BlockSpec((tk, FOX_W), lambda j, i: (j, 0)), pl.BlockSpec((tk, FOX_W), lambda j, i: (j, 0)),
                   pl.BlockSpec((tk, LANES), lambda j, i: (j, 0))],
        out_shape=[jax.ShapeDtypeStruct((t, FOX_W), bf16), jax.ShapeDtypeStruct((t, FOX_W), bf16),
                   jax.ShapeDtypeStruct((t, LANES), f32)],
        scratch_shapes=[pltpu.VMEM((tk, FOX_W), f32), pltpu.VMEM((tk, FOX_W), f32), pltpu.VMEM((tk, LANES), f32)],
        compiler_params=_params(("arbitrary", "arbitrary")),
    )(qkv, qkv, qkv, dmix, cum_t, cum, lse_t, delta_t)


LRU_CHUNK = 64
SUB = 8


def _row_ids(n):
    return lax.broadcasted_iota(jnp.int32, (n, LANES), 0)


def _shift_rows_down(ext, s):
    return pltpu.roll(ext, s, axis=0)[SUB:, :]


def _shift_rows_up(ext, s, n):
    return pltpu.roll(ext, ext.shape[0] - s, axis=0)[:n, :]


def _lru_gates(u, wa_ref, ba_ref, wx_ref, bx_ref, sp):
    ub = u.astype(bf16)
    r = _sigmoid(jnp.dot(ub, wa_ref[...], preferred_element_type=f32) + ba_ref[...])
    gi = _sigmoid(jnp.dot(ub, wx_ref[...], preferred_element_type=f32) + bx_ref[...])
    log_a = -LRU_C * r * sp
    a = jnp.exp(log_a)
    s = jnp.sqrt(_one_minus_exp(2.0 * log_a))
    return r, gi, a, s


def _conv_window(lx_ref, r0, ci):
    cur = lx_ref[pl.ds(r0, LRU_CHUNK), :]
    p0 = pl.multiple_of(jnp.maximum(r0 - SUB, 0), SUB)
    prev = jnp.where(ci > 0, lx_ref[pl.ds(p0, SUB), :], 0.0)
    return cur, jnp.concatenate([prev, cur], axis=0)


def _lru_fwd(zl, conv_w, conv_b, wa, ba, wx, bx, lam, *, name):
    t = zl.shape[0]
    n_chunk = t // LRU_CHUNK

    def body(lx_ref, lg_ref, cw_ref, cb_ref, wa_ref, ba_ref, wx_ref, bx_ref, lam_ref, u_ref, h_ref, y_ref):
        sp = _softplus(-lam_ref[...])
        rows = _row_ids(SUB)

        def chunk(ci, hc):
            r0 = pl.multiple_of(ci * LRU_CHUNK, LRU_CHUNK)
            cur, ext = _conv_window(lx_ref, r0, ci)
            u = cb_ref[...] + cw_ref[3:4, :] * cur
            for k in range(3):
                u = u + cw_ref[k:k + 1, :] * _shift_rows_down(ext, 3 - k)
            r, gi, a, s = _lru_gates(u, wa_ref, ba_ref, wx_ref, bx_ref, sp)
            b = s * (gi * u)
            tiles = []
            for q in range(LRU_CHUNK // SUB):
                ta = a[SUB * q:SUB * (q + 1), :]
                tb = b[SUB * q:SUB * (q + 1), :]
                for d in (1, 2, 4):
                    a_sh = jnp.where(rows >= d, pltpu.roll(ta, d, axis=0), 1.0)
                    b_sh = jnp.where(rows >= d, pltpu.roll(tb, d, axis=0), 0.0)
                    tb = ta * b_sh + tb
                    ta = ta * a_sh
                hq = tb + ta * hc
                hc = hq[SUB - 1:SUB, :]
                tiles.append(hq)
            h = jnp.concatenate(tiles, axis=0)
            u_ref[pl.ds(r0, LRU_CHUNK), :] = u
            h_ref[pl.ds(r0, LRU_CHUNK), :] = h
            gel, _ = _gelu_and_grad(lg_ref[pl.ds(r0, LRU_CHUNK), :])
            y_ref[pl.ds(r0, LRU_CHUNK), :] = gel * h
            return hc

        lax.fori_loop(0, n_chunk, chunk, jnp.zeros((1, LANES), f32))

    seq = lambda cb: pl.BlockSpec((t, LANES), lambda c, cb=cb: (0, c + cb))
    rowc = pl.BlockSpec((1, LANES), lambda c: (0, c))
    diag = pl.BlockSpec((LANES, LANES), lambda c: (c, c))
    out = jax.ShapeDtypeStruct((t, LRU_W), f32)
    return pl.pallas_call(
        body, name=name, grid=(LRU_W // LANES,),
        in_specs=[seq(0), seq(4), pl.BlockSpec((4, LANES), lambda c: (0, c)), rowc, diag, rowc, diag, rowc, rowc],
        out_specs=[seq(0)] * 3, out_shape=[out] * 3,
        compiler_params=_params(("arbitrary",)),
    )(zl, zl, conv_w, conv_b, wa, ba, wx, bx, lam)


def _lru_bwd(dmix, zl, u_all, h_all, conv_w, wa, ba, wx, bx, lam, *, name):
    t = zl.shape[0]
    n_chunk = t // LRU_CHUNK

    def body(dy_ref, lx_ref, lg_ref, u_ref, h_ref, cw_ref, wa_ref, ba_ref, wx_ref, bx_ref, lam_ref,
             dlx_ref, dlg_ref, dcw_ref, dcb_ref, dba_ref, dbx_ref, dlam_ref, dwa_ref, dwx_ref, dpr_s, dpx_s):
        lam_v = lam_ref[...]
        sp = _softplus(-lam_v)
        rows = _row_ids(SUB)
        rows_c = _row_ids(LRU_CHUNK)
        zero_row = jnp.zeros((1, LANES), f32)

        def chunk(step, carry):
            dh_c, a_next0, du_next, dsp, dba, dbx, dcb, dw0, dw1, dw2, dw3 = carry
            ci = n_chunk - 1 - step
            r0 = pl.multiple_of(ci * LRU_CHUNK, LRU_CHUNK)
            sl = pl.ds(r0, LRU_CHUNK)
            u = u_ref[sl, :]
            r, gi, a, s = _lru_gates(u, wa_ref, ba_ref, wx_ref, bx_ref, sp)
            h = h_ref[sl, :]
            p0 = pl.multiple_of(jnp.maximum(r0 - SUB, 0), SUB)
            h_before = jnp.where(ci > 0, h_ref[pl.ds(p0, SUB), :], 0.0)[SUB - 1:SUB, :]
            h_prev = jnp.where(rows_c == 0, h_before, pltpu.roll(h, 1, axis=0))
            gel, dgel = _gelu_and_grad(lg_ref[sl, :])
            dy = dy_ref[sl, :]
            dlg_ref[sl, :] = (dy * h * dgel).astype(bf16)
            g_in = dy * gel
            a_next = jnp.where(rows_c == LRU_CHUNK - 1, a_next0, pltpu.roll(a, LRU_CHUNK - 1, axis=0))
            tiles = [None] * (LRU_CHUNK // SUB)
            for q in reversed(range(LRU_CHUNK // SUB)):
                ta = a_next[SUB * q:SUB * (q + 1), :]
                tb = g_in[SUB * q:SUB * (q + 1), :]
                for d in (1, 2, 4):
                    a_sh = jnp.where(rows < SUB - d, pltpu.roll(ta, SUB - d, axis=0), 1.0)
                    b_sh = jnp.where(rows < SUB - d, pltpu.roll(tb, SUB - d, axis=0), 0.0)
                    tb = ta * b_sh + tb
                    ta = ta * a_sh
                dhq = tb + ta * dh_c
                dh_c = dhq[0:1, :]
                tiles[q] = dhq
            dh = jnp.concatenate(tiles, axis=0)
            da = dh * h_prev
            ds = dh * gi * u
            dgi = dh * s * u
            du = dh * s * gi
            dlog_a = da * a - ds * (a * a) / s
            dr = dlog_a * (-LRU_C * sp)
            dsp = dsp + jnp.sum(dlog_a * (-LRU_C * r), axis=0, keepdims=True)
            dpr = dr * r * (1.0 - r)
            dpx = dgi * gi * (1.0 - gi)
            dprb = dpr.astype(bf16)
            dpxb = dpx.astype(bf16)
            dpr_s[sl, :] = dprb
            dpx_s[sl, :] = dpxb
            du = du + (lax.dot_general(dprb, wa_ref[...], _NT, preferred_element_type=f32)
                       + lax.dot_general(dpxb, wx_ref[...], _NT, preferred_element_type=f32))
            dba = dba + jnp.sum(dpr, axis=0, keepdims=True)
            dbx = dbx + jnp.sum(dpx, axis=0, keepdims=True)
            dcb = dcb + jnp.sum(du, axis=0, keepdims=True)
            du_ext = jnp.concatenate([du, du_next], axis=0)
            dlx = cw_ref[3:4, :] * du
            for k in range(3):
                dlx = dlx + cw_ref[k:k + 1, :] * _shift_rows_up(du_ext, 3 - k, LRU_CHUNK)
            dlx_ref[sl, :] = dlx.astype(bf16)
            cur, ext = _conv_window(lx_ref, r0, ci)
            dws = [dw0, dw1, dw2, dw3 + jnp.sum(du * cur, axis=0, keepdims=True)]
            for k in range(3):
                dws[k] = dws[k] + jnp.sum(du * _shift_rows_down(ext, 3 - k), axis=0, keepdims=True)
            return (dh_c, a[0:1, :], du[0:SUB, :], dsp, dba, dbx, dcb, dws[0], dws[1], dws[2], dws[3])

        init = (zero_row, zero_row, jnp.zeros((SUB, LANES), f32)) + (zero_row,) * 8
        out = lax.fori_loop(0, n_chunk, chunk, init)
        _, _, _, dsp, dba, dbx, dcb, dw0, dw1, dw2, dw3 = out
        dlam_ref[...] = dsp * (-_sigmoid(-lam_v))
        dba_ref[...] = dba
        dbx_ref[...] = dbx
        dcb_ref[...] = dcb
        dcw_ref[...] = jnp.concatenate([dw0, dw1, dw2, dw3], axis=0)
        ub = u_ref[...].astype(bf16)
        dwa_ref[...] = lax.dot_general(ub, dpr_s[...], _TN, preferred_element_type=f32)
        dwx_ref[...] = lax.dot_general(ub, dpx_s[...], _TN, preferred_element_type=f32)

    seq = lambda cb: pl.BlockSpec((t, LANES), lambda c, cb=cb: (0, c + cb))
    rowc = pl.BlockSpec((1, LANES), lambda c: (0, c))
    diag = pl.BlockSpec((LANES, LANES), lambda c: (c, c))
    gate_out = pl.BlockSpec((None, LANES, LANES), lambda c: (c, 0, 0))
    row_shape = jax.ShapeDtypeStruct((1, LRU_W), f32)
    return pl.pallas_call(
        body, name=name, grid=(LRU_W // LANES,),
        in_specs=[seq(4), seq(0), seq(4), seq(0), seq(0), pl.BlockSpec((4, LANES), lambda c: (0, c)),
                  diag, rowc, diag, rowc, rowc],
        out_specs=[seq(0), seq(0), pl.BlockSpec((4, LANES), lambda c: (0, c)), rowc, rowc, rowc, rowc,
                   gate_out, gate_out],
        out_shape=[jax.ShapeDtypeStruct((t, LRU_W), bf16)] * 2
        + [jax.ShapeDtypeStruct((4, LRU_W), f32)] + [row_shape] * 4
        + [jax.ShapeDtypeStruct((LRU_W // LANES, LANES, LANES), f32)] * 2,
        scratch_shapes=[pltpu.VMEM((t, LANES), bf16), pltpu.VMEM((t, LANES), bf16)],
        compiler_params=_params(("arbitrary",)),
    )(dmix, zl, zl, u_all, h_all, conv_w, wa, ba, wx, bx, lam)


def _block_diag(w):
    eye = jnp.eye(HEADS, dtype=w.dtype)
    return jnp.einsum("hij,hk->hikj", w, eye).reshape(LRU_W, LRU_W)


def _diag_blocks(dw):
    top = dw[:, :HEAD_D, :HEAD_D]
    bot = dw[:, HEAD_D:, HEAD_D:]
    return jnp.stack([top, bot], axis=1).reshape(HEADS, HEAD_D, HEAD_D)


def _local_step(x, target, sent, small, *, tm=512):
    t = x.shape[0]
    ones = jnp.ones((1, D_MODEL), f32)
    zeros = jnp.zeros((1, D_MODEL), f32)
    ln1 = (small["ln1_g"], small["ln1_b"])
    ln2 = (small["ln2_g"], small["ln2_b"])
    ln3 = (small["ln3_g"], small["ln3_b"])

    wg1, wu1, wd1 = _exchange([sent["ffn1_w_gate"], sent["ffn1_w_up"], sent["ffn1_w_down"]], gather=True,
                              name="gather_ffn1")
    xh1, rs1, hg1, hu1, w_in_g, w_out_g, conv_w_g = _ffn_fwd(
        x, ones, zeros, wg1, wu1, wd1, tm=tm, name="ffn1_fwd",
        host=_Exchange([sent["w_in"], sent["w_out"], sent["conv_w"]], gather=True))
    w_in = jnp.pad(w_in_g.transpose(1, 0, 2).reshape(D_MODEL, IN_COLS), ((0, 0), (0, 21 * LANES - IN_COLS)))
    w_out = w_out_g.reshape(D_MODEL, D_MODEL)
    conv_w = conv_w_g.transpose(1, 0, 2).reshape(4, LRU_W)
    qkv = _mm(xh1, w_in, mode="nn", out_dtype=bf16, tm=tm, tn=512, tk=D_MODEL, name="qkv_fwd",
              affine=ln1, b_cols=(0, 1536))
    zl = _mm(xh1, w_in, mode="nn", out_dtype=f32, tm=tm, tn=512, tk=D_MODEL, name="zl_fwd",
             affine=ln1, b_cols=(3, 1024))
    zfg = _mm(xh1, w_in, mode="nn", out_dtype=f32, tm=tm, tn=LANES, tk=D_MODEL, name="zfg_fwd",
              affine=ln1, b_cols=(20, LANES))
    bfg = jnp.pad(small["b_forget"], ((0, 0), (0, LANES - HEADS)))
    cum = _cum_fwd(zfg, bfg, name="cum_fwd")
    cum_t = cum[:, :HEADS].T
    o, lse, wg2, wu2, wd2 = _attn_fwd(
        qkv, cum, cum_t, name="attn_fwd",
        host=_Exchange([sent["ffn2_w_gate"], sent["ffn2_w_up"], sent["ffn2_w_down"]], gather=True))
    wa_bd = _block_diag(small["rg_wa"]).astype(bf16)
    wx_bd = _block_diag(small["rg_wx"]).astype(bf16)
    ba = small["rg_ba"].reshape(1, LRU_W)
    bx = small["rg_bx"].reshape(1, LRU_W)
    u, h, lru = _lru_fwd(zl, conv_w, small["conv_b"], wa_bd, ba, wx_bd, bx, small["lru_lambda"],
                         name="lru_fwd")
    xh2, rs2 = _mmln([(o, 0, FOX_W, w_out, 0, D_MODEL, "nn"), (lru, 0, LRU_W, w_out, 1, D_MODEL, "nn")],
                     tm=tm, name="mix_fwd", resid=("affine", xh1) + ln1, resid_scale=ALPHA, epi="ln_fwd")
    xh3, rs3, hg2, hu2 = _ffn_fwd(xh2, ln2[0], ln2[1], wg2, wu2, wd2, tm=tm, name="ffn2_fwd")

    dpre3, sq_rows, g_ln3g, g_ln3b = _loss_bwd(xh3, rs3, ln3[0], ln3[1], target, tm=tm, name="loss_bwd")
    dpre2, g_ln2g, g_ln2b, dhg2, dhu2, a2 = _ffn_bwd(dpre3, hg2, hu2, wg2, wu2, wd2,
                                                     (xh2, rs2, ln2[0]), tm=tm, name="ffn2_bwd")
    wgrad = dict(mode="tn", out_dtype=bf16, tm=D_MODEL, tn=FF_TILE, tk=512)
    g_wg2 = _mm(xh2, dhg2, name="g_wg2", affine=ln2, out_blocked=True, **wgrad)
    g_wu2 = _mm(xh2, dhu2, name="g_wu2", affine=ln2, out_blocked=True, **wgrad)
    g_wd2 = _mm(a2, dpre3, mode="tn", out_dtype=bf16, tm=512, tn=D_MODEL, tk=512, name="g_wd2", out_scale=0.5)

    dmix = _mmln([(dpre2, 0, D_MODEL, w_out, 0, D_MODEL, "nt")], tm=tm, name="dmix_bwd")
    g_wout_a = _mm(o, dpre2, mode="tn", out_dtype=bf16, tm=512, tn=D_MODEL, tk=512, name="g_wout_fox")
    g_wout_b = _mm(lru, dpre2, mode="tn", out_dtype=bf16, tm=512, tn=D_MODEL, tk=512, name="g_wout_lru")
    dlx, dlg, g_cw, g_cb, g_ba, g_bx, g_lam, g_wa4, g_wx4 = _lru_bwd(
        dmix, zl, u, h, conv_w, wa_bd, ba, wx_bd, bx, small["lru_lambda"], name="lru_bwd")
    delta = _attn_delta(dmix, o, tm=tm, name="attn_delta")
    dq, dcum_q, p_wg2, p_wu2, p_wd2 = _attn_dq(
        qkv, dmix, cum, cum_t, lse, delta, name="attn_dq",
        host=_Exchange([g_wg2, g_wu2, g_wd2.reshape(N_DEV, FF_TILE, D_MODEL)], gather=False))
    dk, dv, dcum_k = _attn_dkv(qkv, dmix, cum, cum_t, lse[:, :HEADS].T, delta[:, :HEADS].T, name="attn_dkv")
    dfg, g_bf = _cum_bwd(dcum_q, dcum_k, zfg, bfg, name="cum_bwd")

    dz = [(dq, 0, 512), (dk, 1, 512), (dv, 2, 512), (dlx, 3, 512), (dlg, 4, 512), (dfg, 20, LANES)]
    dpre1, g_ln1g, g_ln1b = _mmln(
        [(arr, 0, w, w_in, cb, w, "nt") for (arr, cb, w) in dz],
        tm=tm, name="dx1_bwd", resid=("plain", dpre2), resid_scale=ALPHA, epi="ln_bwd", ln=(xh1, rs1, ln1[0]))
    g_win = [_mm(xh1, arr, mode="tn", out_dtype=bf16, tm=D_MODEL, tn=w, tk=512, name=f"g_win{n}", affine=ln1)
             for n, (arr, cb, w) in enumerate(dz)]
    g_win_full = jnp.concatenate([g[:, :w] for g, (_, _, w) in zip(g_win, dz)], axis=1)[:, :IN_COLS]
    g_win_blocked = g_win_full.reshape(D_MODEL, N_DEV, IN_SHARD).transpose(1, 0, 2)
    g_wout_blocked = jnp.concatenate([g_wout_a, g_wout_b], axis=0).reshape(N_DEV, D_MODEL // N_DEV, D_MODEL)
    grad_x, dhg1, dhu1, a1, p_win, p_wout = _ffn_bwd(
        dpre1, hg1, hu1, wg1, wu1, wd1, None, tm=tm, name="ffn1_bwd",
        host=_Exchange([g_win_blocked, g_wout_blocked], gather=False))
    g_wg1 = _mm(x, dhg1, name="g_wg1", out_blocked=True, **wgrad)
    g_wu1 = _mm(x, dhu1, name="g_wu1", out_blocked=True, **wgrad)
    g_wd1 = _mm(a1, dpre1, mode="tn", out_dtype=bf16, tm=512, tn=D_MODEL, tk=512, name="g_wd1", out_scale=0.5)

    p_wg1, p_wu1, p_wd1 = _exchange([g_wg1, g_wu1, g_wd1.reshape(N_DEV, FF_TILE, D_MODEL)], gather=False,
                                    name="scatter_ffn1")
    parts = {
        "ffn1_w_gate": p_wg1, "ffn1_w_up": p_wu1, "ffn1_w_down": p_wd1, "w_in": p_win, "w_out": p_wout,
        "ffn2_w_gate": p_wg2, "ffn2_w_up": p_wu2, "ffn2_w_down": p_wd2,
    }
    small_g = {
        "ln1_g": g_ln1g, "ln1_b": g_ln1b, "b_forget": g_bf[:, :HEADS], "conv_w": g_cw, "conv_b": g_cb,
        "rg_wa": _diag_blocks(g_wa4), "rg_ba": g_ba.reshape(HEADS, HEAD_D),
        "rg_wx": _diag_blocks(g_wx4), "rg_bx": g_bx.reshape(HEADS, HEAD_D), "lru_lambda": g_lam,
        "ln2_g": g_ln2g, "ln2_b": g_ln2b, "ln3_g": g_ln3g, "ln3_b": g_ln3b,
    }
    return sq_rows, grad_x, parts, small_g


def _adam_math(w, g, m, v):
    m2 = ADAM_B1 * m + (1.0 - ADAM_B1) * g
    v2 = ADAM_B2 * v + (1.0 - ADAM_B2) * (g * g)
    m_hat = m2 / (1.0 - ADAM_B1 ** ADAM_STEP)
    v_hat = v2 / (1.0 - ADAM_B2 ** ADAM_STEP)
    delta = -ADAM_LR * (m_hat / (jnp.sqrt(v_hat) + ADAM_EPS) + ADAM_WD * w)
    return delta, m2, v2


def _adamw_big(parts, w, m, v, *, tr, name):
    r, c = w.shape

    def body(p_ref, w_ref, m_ref, v_ref, g_ref, d_ref, m2_ref, v2_ref):
        g = p_ref[0].astype(f32)
        for q in range(1, N_DEV):
            g = g + p_ref[q].astype(f32)
        d, m2, v2 = _adam_math(w_ref[...], g, m_ref[...], v_ref[...])
        g_ref[...] = g
        d_ref[...] = d
        m2_ref[...] = m2
        v2_ref[...] = v2

    blk = pl.BlockSpec((tr, c), lambda i: (i, 0))
    return pl.pallas_call(
        body, name=name, grid=(r // tr,),
        in_specs=[pl.BlockSpec((N_DEV, tr, c), lambda i: (0, i, 0)), blk, blk, blk],
        out_specs=[blk] * 4, out_shape=[jax.ShapeDtypeStruct((r, c), f32)] * 4,
        compiler_params=_params(("arbitrary",)),
    )(parts, w, m, v)


def _adamw_small(items, *, name):
    n = len(items)

    def body(*refs):
        ins, outs = refs[:4 * n], refs[4 * n:]
        for k in range(n):
            g, w, m, v = (ins[4 * k + q][...] for q in range(4))
            d, m2, v2 = _adam_math(w, g, m, v)
            outs[3 * k][...] = d
            outs[3 * k + 1][...] = m2
            outs[3 * k + 2][...] = v2

    vm = pl.BlockSpec(memory_space=pltpu.VMEM)
    flat = [a for item in items for a in item]
    out_shape = [jax.ShapeDtypeStruct(item[1].shape, f32) for item in items for _ in range(3)]
    return pl.pallas_call(
        body, name=name, in_specs=[vm] * (4 * n), out_specs=[vm] * (3 * n), out_shape=out_shape,
    )(*flat)


def _sum_parts(parts, *, name):
    def body(p_ref, o_ref):
        acc = p_ref[0]
        for q in range(1, N_DEV):
            acc = acc + p_ref[q]
        o_ref[...] = acc

    vm = pl.BlockSpec(memory_space=pltpu.VMEM)
    return pl.pallas_call(
        body, name=name, in_specs=[vm], out_specs=vm, out_shape=jax.ShapeDtypeStruct(parts.shape[1:], f32),
    )(parts)


WEIGHTS = ["ffn1_w_gate", "ffn1_w_up", "ffn1_w_down", "ln1_g", "ln1_b", "w_in", "b_forget", "conv_w", "conv_b",
           "rg_wa", "rg_ba", "rg_wx", "rg_bx", "lru_lambda", "w_out", "ln2_g", "ln2_b",
           "ffn2_w_gate", "ffn2_w_up", "ffn2_w_down", "ln3_g", "ln3_b"]
BIG = ["ffn1_w_gate", "ffn1_w_up", "ffn1_w_down", "w_in", "w_out", "ffn2_w_gate", "ffn2_w_up", "ffn2_w_down"]
PACKED = ["ln1_g", "ln1_b", "ln2_g", "ln2_b", "ln3_g", "ln3_b", "conv_b", "rg_ba", "rg_bx", "lru_lambda",
          "conv_w", "rg_wa", "rg_wx", "b_forget"]
PACK_ROWS = 600


def _two_d(a):
    return a.reshape((-1, a.shape[-1]))


def kernel(x, ffn1_w_gate, ffn1_w_up, ffn1_w_down, ln1_g, ln1_b, w_in, b_forget, conv_w, conv_b, rg_wa, rg_ba, rg_wx, rg_bx, lru_lambda, w_out, ln2_g, ln2_b, ffn2_w_gate, ffn2_w_up, ffn2_w_down, ln3_g, ln3_b, loss_target, m_ffn1_w_gate, m_ffn1_w_up, m_ffn1_w_down, m_ln1_g, m_ln1_b, m_w_in, m_b_forget, m_conv_w, m_conv_b, m_rg_wa, m_rg_ba, m_rg_wx, m_rg_bx, m_lru_lambda, m_w_out, m_ln2_g, m_ln2_b, m_ffn2_w_gate, m_ffn2_w_up, m_ffn2_w_down, m_ln3_g, m_ln3_b, v_ffn1_w_gate, v_ffn1_w_up, v_ffn1_w_down, v_ln1_g, v_ln1_b, v_w_in, v_b_forget, v_conv_w, v_conv_b, v_rg_wa, v_rg_ba, v_rg_wx, v_rg_bx, v_lru_lambda, v_w_out, v_ln2_g, v_ln2_b, v_ffn2_w_gate, v_ffn2_w_up, v_ffn2_w_down, v_ln3_g, v_ln3_b):
    w_args = (ffn1_w_gate, ffn1_w_up, ffn1_w_down, ln1_g, ln1_b, w_in, b_forget, conv_w, conv_b, rg_wa, rg_ba, rg_wx, rg_bx, lru_lambda, w_out, ln2_g, ln2_b, ffn2_w_gate, ffn2_w_up, ffn2_w_down, ln3_g, ln3_b)
    m_args = (m_ffn1_w_gate, m_ffn1_w_up, m_ffn1_w_down, m_ln1_g, m_ln1_b, m_w_in, m_b_forget, m_conv_w, m_conv_b, m_rg_wa, m_rg_ba, m_rg_wx, m_rg_bx, m_lru_lambda, m_w_out, m_ln2_g, m_ln2_b, m_ffn2_w_gate, m_ffn2_w_up, m_ffn2_w_down, m_ln3_g, m_ln3_b)
    v_args = (v_ffn1_w_gate, v_ffn1_w_up, v_ffn1_w_down, v_ln1_g, v_ln1_b, v_w_in, v_b_forget, v_conv_w, v_conv_b, v_rg_wa, v_rg_ba, v_rg_wx, v_rg_bx, v_lru_lambda, v_w_out, v_ln2_g, v_ln2_b, v_ffn2_w_gate, v_ffn2_w_up, v_ffn2_w_down, v_ln3_g, v_ln3_b)
    w = dict(zip(WEIGHTS, w_args))
    m = dict(zip(WEIGHTS, m_args))
    v = dict(zip(WEIGHTS, v_args))
    me = 4 * lax.axis_index("x") + 2 * lax.axis_index("y") + lax.axis_index("c")

    sent = {n: _two_d(w[n]).astype(bf16) for n in BIG}
    sent["conv_w"] = _two_d(w["conv_w"])
    small = {n: w[n] for n in ("ln1_g", "ln1_b", "ln2_g", "ln2_b", "ln3_g", "ln3_b", "b_forget", "conv_b",
                               "lru_lambda")}
    small.update({n: w[n][0] for n in ("rg_wa", "rg_ba", "rg_wx", "rg_bx")})

    sq_rows, grad_x, parts, small_g = _local_step(x[0], loss_target[0], sent, small)
    loss = lax.psum(0.5 * jnp.sum(sq_rows) / D_MODEL, ("x", "y", "c"))

    pieces = [small_g[n].reshape(-1) for n in PACKED]
    packed = jnp.concatenate(pieces + [jnp.zeros((PACK_ROWS * LANES - sum(p.shape[0] for p in pieces),), f32)])
    (all_packed,) = _exchange([packed.reshape(PACK_ROWS, LANES)], gather=True, name="gather_small_grads")
    total = _sum_parts(all_packed, name="sum_small_grads").reshape(-1)
    grads, off = {}, 0
    for n, p in zip(PACKED, pieces):
        grads[n] = total[off:off + p.shape[0]].reshape(small_g[n].shape)
        off += p.shape[0]
    grads["conv_w"] = lax.dynamic_slice_in_dim(grads["conv_w"], me * (LRU_W // N_DEV), LRU_W // N_DEV, axis=1)

    delta, new_m, new_v = {}, {}, {}
    for n in BIG:
        w2 = _two_d(w[n])
        g, d, m2, v2 = _adamw_big(parts[n], w2, _two_d(m[n]), _two_d(v[n]), tr=128, name="adamw_" + n)
        grads[n], delta[n], new_m[n], new_v[n] = g, d, m2, v2
    small_names = [n for n in WEIGHTS if n not in BIG]
    outs = _adamw_small([(_two_d(grads[n]), _two_d(w[n]), _two_d(m[n]), _two_d(v[n])) for n in small_names],
                        name="adamw_small")
    for k, n in enumerate(small_names):
        delta[n], new_m[n], new_v[n] = outs[3 * k], outs[3 * k + 1], outs[3 * k + 2]

    def shaped(d):
        return [d[n].reshape(w[n].shape) for n in WEIGHTS]

    return (loss, grad_x[None], *shaped(grads), *shaped(delta), *shaped(new_m), *shaped(new_v))
```

```python
import functools
import math

import jax
import jax.numpy as jnp
from jax import lax
from jax.experimental import pallas as pl
from jax.experimental.pallas import tpu as pltpu

f32 = jnp.float32
bf16 = jnp.bfloat16

N_DEV = 8
D_MODEL = 1024
D_FF = 4096
FF_TILE = D_FF // N_DEV
FOX_W = 512
LRU_W = 512
HEADS = 8
HEAD_D = 64
IN_COLS = 2568
IN_SHARD = IN_COLS // N_DEV
LANES = 128
LN_EPS = 1e-5
ALPHA = 2.0 ** 0.25
ATT_SCALE = 1.0 / math.sqrt(HEAD_D)
LRU_C = 8.0
NEG_BIG = -1e30

ADAM_LR = 0.001
ADAM_B1 = 0.9
ADAM_B2 = 0.999
ADAM_EPS = 1e-08
ADAM_WD = 0.01
ADAM_STEP = 10

VMEM_LIMIT = 56 * 1024 * 1024
MESH_T = pl.DeviceIdType.MESH


def _params(sem, **kw):
    return pltpu.CompilerParams(dimension_semantics=sem, vmem_limit_bytes=VMEM_LIMIT, **kw)


def _sigmoid(x):
    return 1.0 / (1.0 + jnp.exp(-x))


def _softplus(x):
    return jnp.maximum(x, 0.0) + jnp.log(1.0 + jnp.exp(-jnp.abs(x)))


def _one_minus_exp(x):
    series = -x * (1.0 + x * (0.5 + x * (1.0 / 6 + x * (1.0 / 24 + x * (1.0 / 120 + x * (1.0 / 720))))))
    return jnp.where(x > -0.125, series, 1.0 - jnp.exp(x))


_GELU_C = math.sqrt(2.0 / math.pi)


def _gelu_and_grad(x):
    inner = _GELU_C * (x + 0.044715 * x * x * x)
    t = jnp.tanh(inner)
    g = 0.5 * x * (1.0 + t)
    dg = 0.5 * (1.0 + t) + 0.5 * x * (1.0 - t * t) * _GELU_C * (1.0 + 3 * 0.044715 * x * x)
    return g, dg


def _ln_fwd_tile(pre):
    mu = jnp.mean(pre, axis=-1, keepdims=True)
    xc = pre - mu
    var = jnp.mean(xc * xc, axis=-1, keepdims=True)
    rstd = lax.rsqrt(var + LN_EPS)
    return xc * rstd, rstd


def _ln_bwd_tile(dy, xhat, rstd, g):
    dyg = dy * g
    m1 = jnp.mean(dyg, axis=-1, keepdims=True)
    m2 = jnp.mean(dyg * xhat, axis=-1, keepdims=True)
    dpre = rstd * (dyg - m1 - xhat * m2)
    return dpre, jnp.sum(dy * xhat, axis=0, keepdims=True), jnp.sum(dy, axis=0, keepdims=True)


_NT = (((1,), (1,)), ((), ()))
_TN = (((0,), (0,)), ((), ()))


class _Exchange:
    def __init__(self, arrs, gather):
        self.arrs, self.gather, self.n = list(arrs), gather, len(arrs)

    def out_shape(self):
        return [jax.ShapeDtypeStruct(((N_DEV,) + a.shape) if self.gather else a.shape, a.dtype) for a in self.arrs]

    def scratch(self):
        n_remote = self.n * (N_DEV - 1)
        return [pltpu.SemaphoreType.DMA((n_remote,)), pltpu.SemaphoreType.DMA((n_remote,)),
                pltpu.SemaphoreType.DMA((self.n,))]

    def copies(self, ins, outs, sems):
        send_sems, recv_sems, local_sems = sems
        x, y, c = lax.axis_index("x"), lax.axis_index("y"), lax.axis_index("c")
        me = 4 * x + 2 * y + c
        out = []
        for k in range(self.n):
            for d in range(1, N_DEV):
                px = 1 - x if d & 4 else x
                py = 1 - y if d & 2 else y
                pc = 1 - c if d & 1 else c
                src = ins[k] if self.gather else ins[k].at[4 * px + 2 * py + pc]
                sem = k * (N_DEV - 1) + d - 1
                out.append(pltpu.make_async_remote_copy(
                    src_ref=src, dst_ref=outs[k].at[me], send_sem=send_sems.at[sem], recv_sem=recv_sems.at[sem],
                    device_id=(px, py, pc), device_id_type=MESH_T))
            out.append(pltpu.make_async_copy(ins[k] if self.gather else ins[k].at[me], outs[k].at[me],
                                             local_sems.at[k]))
        return out

    def start(self, ins, outs, sems):
        for cp in self.copies(ins, outs, sems):
            cp.start()

    def wait(self, ins, outs, sems):
        for cp in self.copies(ins, outs, sems):
            cp.wait()


def _hosted_call(host, body, *, name, grid, in_specs, out_specs, out_shape, scratch_shapes=(), compiler_params):
    out_specs = list(out_specs) if isinstance(out_specs, (list, tuple)) else [out_specs]
    out_shape = list(out_shape) if isinstance(out_shape, (list, tuple)) else [out_shape]
    if host is None:
        return pl.pallas_call(body, name=name, grid=grid, in_specs=in_specs, out_specs=out_specs,
                              out_shape=out_shape, scratch_shapes=list(scratch_shapes),
                              compiler_params=compiler_params)
    n_in, n_out, n_scr, k = len(in_specs), len(out_shape), len(scratch_shapes), host.n

    def wrapped(*refs):
        ins, h_in = refs[:n_in], refs[n_in:n_in + k]
        outs, h_out = refs[n_in + k:n_in + k + n_out], refs[n_in + k + n_out:n_in + 2 * k + n_out]
        scr, sems = refs[n_in + 2 * k + n_out:n_in + 2 * k + n_out + n_scr], refs[n_in + 2 * k + n_out + n_scr:]
        ids = [pl.program_id(a) for a in range(len(grid))]
        first = functools.reduce(jnp.logical_and, [i == 0 for i in ids])
        last = functools.reduce(jnp.logical_and, [i == g - 1 for i, g in zip(ids, grid)])

        @pl.when(first)
        def _():
            host.start(h_in, h_out, sems)

        body(*ins, *outs, *scr)

        @pl.when(last)
        def _():
            host.wait(h_in, h_out, sems)

    hbm = pl.BlockSpec(memory_space=pl.ANY)
    call = pl.pallas_call(
        wrapped, name=name, grid=grid, in_specs=list(in_specs) + [hbm] * k, out_specs=out_specs + [hbm] * k,
        out_shape=out_shape + host.out_shape(), scratch_shapes=list(scratch_shapes) + host.scratch(),
        compiler_params=compiler_params)
    return lambda *args: call(*args, *host.arrs)


def _exchange(arrs, *, gather, name):
    host = _Exchange(arrs, gather)

    def body(*refs):
        ins, outs, sems = refs[:host.n], refs[host.n:2 * host.n], refs[2 * host.n:]
        host.start(ins, outs, sems)
        host.wait(ins, outs, sems)

    hbm = pl.BlockSpec(memory_space=pl.ANY)
    return pl.pallas_call(
        body, name=name, in_specs=[hbm] * host.n, out_specs=[hbm] * host.n, out_shape=host.out_shape(),
        scratch_shapes=host.scratch(), compiler_params=pltpu.CompilerParams(has_side_effects=True),
    )(*arrs)


def _ffn_fwd(xhat, g_in, b_in, wg, wu, wd, *, tm, name, host=None):
    t = xhat.shape[0]
    nj = N_DEV

    def body(x_ref, g_ref, b_ref, wg_ref, wu_ref, wd_ref, xo_ref, rstd_ref, hg_ref, hu_ref, xb, acc):
        j = pl.program_id(1)

        @pl.when(j == 0)
        def _():
            xb[...] = (x_ref[...] * g_ref[...] + b_ref[...]).astype(bf16)
            acc[...] = jnp.zeros_like(acc)

        hg = jnp.dot(xb[...], wg_ref[...], preferred_element_type=f32)
        hu = jnp.dot(xb[...], wu_ref[...], preferred_element_type=f32)
        hg_ref[...] = hg.astype(bf16)
        hu_ref[...] = hu.astype(bf16)
        a = hg * _sigmoid(hg) * hu
        acc[...] += jnp.dot(a.astype(bf16), wd_ref[...], preferred_element_type=f32)

        @pl.when(j == nj - 1)
        def _():
            x = x_ref[...] * g_ref[...] + b_ref[...]
            xo, rstd = _ln_fwd_tile(ALPHA * x + 0.5 * acc[...])
            xo_ref[...] = xo
            rstd_ref[...] = rstd

    row = pl.BlockSpec((1, D_MODEL), lambda i, j: (0, 0))
    return _hosted_call(
        host, body, name=name, grid=(t // tm, nj),
        in_specs=[pl.BlockSpec((tm, D_MODEL), lambda i, j: (i, 0)), row, row,
                  pl.BlockSpec((None, D_MODEL, FF_TILE), lambda i, j: (j, 0, 0)),
                  pl.BlockSpec((None, D_MODEL, FF_TILE), lambda i, j: (j, 0, 0)),
                  pl.BlockSpec((None, FF_TILE, D_MODEL), lambda i, j: (j, 0, 0))],
        out_specs=[pl.BlockSpec((tm, D_MODEL), lambda i, j: (i, 0)),
                   pl.BlockSpec((tm, 1), lambda i, j: (i, 0)),
                   pl.BlockSpec((tm, FF_TILE), lambda i, j: (i, j)),
                   pl.BlockSpec((tm, FF_TILE), lambda i, j: (i, j))],
        out_shape=[jax.ShapeDtypeStruct((t, D_MODEL), f32), jax.ShapeDtypeStruct((t, 1), f32),
                   jax.ShapeDtypeStruct((t, D_FF), bf16), jax.ShapeDtypeStruct((t, D_FF), bf16)],
        scratch_shapes=[pltpu.VMEM((tm, D_MODEL), bf16), pltpu.VMEM((tm, D_MODEL), f32)],
        compiler_params=_params(("arbitrary", "arbitrary")),
    )(xhat, g_in, b_in, wg, wu, wd)


def _ffn_bwd(dpre, hg, hu, wg, wu, wd, ln_in, *, tm, name, host=None):
    t = dpre.shape[0]
    nj = N_DEV
    with_ln = ln_in is not None

    def body(*refs):
        if with_ln:
            (dp_ref, hg_ref, hu_ref, wg_ref, wu_ref, wd_ref, xh_ref, rs_ref, g_ref,
             dx_ref, gg_ref, gb_ref, dhg_ref, dhu_ref, a_ref, dfb, acc) = refs
        else:
            (dp_ref, hg_ref, hu_ref, wg_ref, wu_ref, wd_ref,
             dx_ref, dhg_ref, dhu_ref, a_ref, dfb, acc) = refs
        i = pl.program_id(0)
        j = pl.program_id(1)

        @pl.when(j == 0)
        def _():
            dfb[...] = (0.5 * dp_ref[...]).astype(bf16)
            acc[...] = jnp.zeros_like(acc)

        da = lax.dot_general(dfb[...], wd_ref[...], _NT, preferred_element_type=f32)
        hgv = hg_ref[...].astype(f32)
        huv = hu_ref[...].astype(f32)
        sg = _sigmoid(hgv)
        silu = hgv * sg
        a_ref[...] = (silu * huv).astype(bf16)
        dhu = (da * silu).astype(bf16)
        dhg = (da * huv * (sg * (1.0 + hgv * (1.0 - sg)))).astype(bf16)
        dhg_ref[...] = dhg
        dhu_ref[...] = dhu
        acc[...] += (lax.dot_general(dhg, wg_ref[...], _NT, preferred_element_type=f32)
                     + lax.dot_general(dhu, wu_ref[...], _NT, preferred_element_type=f32))

        @pl.when(j == nj - 1)
        def _():
            dx = ALPHA * dp_ref[...] + acc[...]
            if with_ln:
                dprev, gg, gb = _ln_bwd_tile(dx, xh_ref[...], rs_ref[...], g_ref[...])
                dx_ref[...] = dprev

                @pl.when(i == 0)
                def _():
                    gg_ref[...] = gg
                    gb_ref[...] = gb

                @pl.when(i > 0)
                def _():
                    gg_ref[...] += gg
                    gb_ref[...] += gb
            else:
                dx_ref[...] = dx

    tok = pl.BlockSpec((tm, D_MODEL), lambda i, j: (i, 0))
    row = pl.BlockSpec((1, D_MODEL), lambda i, j: (0, 0))
    hid = pl.BlockSpec((tm, FF_TILE), lambda i, j: (i, j))
    in_specs = [tok, hid, hid,
                pl.BlockSpec((None, D_MODEL, FF_TILE), lambda i, j: (j, 0, 0)),
                pl.BlockSpec((None, D_MODEL, FF_TILE), lambda i, j: (j, 0, 0)),
                pl.BlockSpec((None, FF_TILE, D_MODEL), lambda i, j: (j, 0, 0))]
    args = [dpre, hg, hu, wg, wu, wd]
    out_specs = [tok]
    out_shape = [jax.ShapeDtypeStruct((t, D_MODEL), f32)]
    if with_ln:
        in_specs += [tok, pl.BlockSpec((tm, 1), lambda i, j: (i, 0)), row]
        args += list(ln_in)
        out_specs += [row, row]
        out_shape += [jax.ShapeDtypeStruct((1, D_MODEL), f32)] * 2
    out_specs += [hid, hid, hid]
    out_shape += [jax.ShapeDtypeStruct((t, D_FF), bf16)] * 3
    return _hosted_call(
        host, body, name=name, grid=(t // tm, nj), in_specs=in_specs, out_specs=out_specs, out_shape=out_shape,
        scratch_shapes=[pltpu.VMEM((tm, D_MODEL), bf16), pltpu.VMEM((tm, D_MODEL), f32)],
        compiler_params=_params(("arbitrary", "arbitrary")),
    )(*args)


def _mm(a, b, *, mode, out_dtype, tm, tn, tk, name, affine=None, a_cols=None, b_cols=None,
        b_blocked=False, out_blocked=False, out_scale=None):
    if mode == "nn":
        m_full, k_full = a.shape
        m_dim, k_dim = (m_full, a_cols[1]) if a_cols else (m_full, k_full)
    else:
        k_dim, m_full = a.shape
        m_dim = a_cols[1] if a_cols else m_full
    a_off = a_cols[0] if a_cols else 0
    if b_blocked:
        n_dim = b.shape[0] * b.shape[2]
        assert b.shape[2] == tn
    else:
        n_dim = b_cols[1] if b_cols else b.shape[1]
    b_off = b_cols[0] if b_cols else 0
    assert m_dim % tm == 0 and n_dim % tn == 0 and k_dim % tk == 0, (name, m_dim, n_dim, k_dim)
    nk = k_dim // tk

    def body(*refs):
        if affine is not None:
            a_ref, g_ref, s_ref, b_ref, o_ref, acc = refs
        else:
            a_ref, b_ref, o_ref, acc = refs
        k = pl.program_id(2)

        @pl.when(k == 0)
        def _():
            acc[...] = jnp.zeros_like(acc)

        av = a_ref[...]
        if affine is not None:
            av = av * g_ref[...] + s_ref[...]
        av = av.astype(bf16)
        bv = b_ref[...].astype(bf16)
        if mode == "nn":
            acc[...] += jnp.dot(av, bv, preferred_element_type=f32)
        else:
            acc[...] += lax.dot_general(av, bv, _TN, preferred_element_type=f32)

        @pl.when(k == nk - 1)
        def _():
            res = acc[...] if out_scale is None else acc[...] * out_scale
            o_ref[...] = res.astype(out_dtype)

    if mode == "nn":
        a_spec = pl.BlockSpec((tm, tk), lambda i, j, k: (i, k + a_off))
        aff_spec = pl.BlockSpec((1, tk), lambda i, j, k: (0, k + a_off))
    else:
        a_spec = pl.BlockSpec((tk, tm), lambda i, j, k: (k, i + a_off))
        aff_spec = pl.BlockSpec((1, tm), lambda i, j, k: (0, i + a_off))
    if b_blocked:
        b_spec = pl.BlockSpec((None, tk, tn), lambda i, j, k: (j, k, 0))
    else:
        b_spec = pl.BlockSpec((tk, tn), lambda i, j, k: (k, j + b_off))
    if out_blocked:
        o_spec = pl.BlockSpec((None, tm, tn), lambda i, j, k: (j, i, 0))
        o_shape = jax.ShapeDtypeStruct((n_dim // tn, m_dim, tn), out_dtype)
    else:
        o_spec = pl.BlockSpec((tm, tn), lambda i, j, k: (i, j))
        o_shape = jax.ShapeDtypeStruct((m_dim, n_dim), out_dtype)
    in_specs = [a_spec] + ([aff_spec, aff_spec] if affine is not None else []) + [b_spec]
    args = [a] + (list(affine) if affine is not None else []) + [b]
    return pl.pallas_call(
        body, name=name, grid=(m_dim // tm, n_dim // tn, nk), in_specs=in_specs, out_specs=o_spec,
        out_shape=o_shape, scratch_shapes=[pltpu.VMEM((tm, tn), f32)],
        compiler_params=_params(("arbitrary", "arbitrary", "arbitrary")),
    )(*args)


def _mm_tn(a, b, *, out_dtype, tm, mb, tn, nb, tk, name, affine=None, out_blocked=False, out_scale=None, host=None):
    k_dim, m_dim = a.shape
    n_dim = b.shape[1]
    assert m_dim % (mb * tm) == 0 and n_dim % (nb * tn) == 0 and k_dim % tk == 0, (name, m_dim, n_dim, k_dim)
    nk = k_dim // tk

    def body(*refs):
        if affine is not None:
            a_ref, g_ref, s_ref, b_ref, o_ref, acc = refs
        else:
            a_ref, b_ref, o_ref, acc = refs
        k = pl.program_id(2)

        @pl.when(k == 0)
        def _():
            acc[...] = jnp.zeros_like(acc)

        av = a_ref[...]
        if affine is not None:
            av = av * g_ref[...] + s_ref[...]
        av = av.astype(bf16)
        bv = b_ref[...].astype(bf16)
        for im in range(mb):
            a_t = av[:, im * tm:(im + 1) * tm].T
            for jn in range(nb):
                acc[im * nb + jn] += jnp.dot(a_t, bv[:, jn * tn:(jn + 1) * tn], preferred_element_type=f32)

        @pl.when(k == nk - 1)
        def _():
            for im in range(mb):
                for jn in range(nb):
                    res = acc[im * nb + jn]
                    if out_scale is not None:
                        res = res * out_scale
                    if out_blocked:
                        o_ref[jn, im * tm:(im + 1) * tm, :] = res.astype(out_dtype)
                    else:
                        o_ref[im * tm:(im + 1) * tm, jn * tn:(jn + 1) * tn] = res.astype(out_dtype)

    a_spec = pl.BlockSpec((tk, mb * tm), lambda i, j, k: (k, i))
    aff_spec = pl.BlockSpec((1, mb * tm), lambda i, j, k: (0, i))
    b_spec = pl.BlockSpec((tk, nb * tn), lambda i, j, k: (k, j))
    if out_blocked:
        o_spec = pl.BlockSpec((nb, mb * tm, tn), lambda i, j, k: (j, i, 0))
        o_shape = jax.ShapeDtypeStruct((n_dim // tn, m_dim, tn), out_dtype)
    else:
        o_spec = pl.BlockSpec((mb * tm, nb * tn), lambda i, j, k: (i, j))
        o_shape = jax.ShapeDtypeStruct((m_dim, n_dim), out_dtype)
    in_specs = [a_spec] + ([aff_spec, aff_spec] if affine is not None else []) + [b_spec]
    args = [a] + (list(affine) if affine is not None else []) + [b]
    res = _hosted_call(
        host, body, name=name, grid=(m_dim // (mb * tm), n_dim // (nb * tn), nk), in_specs=in_specs,
        out_specs=o_spec, out_shape=o_shape, scratch_shapes=[pltpu.VMEM((mb * nb, tm, tn), f32)],
        compiler_params=_params(("arbitrary", "arbitrary", "arbitrary")),
    )(*args)
    return res[0] if host is None else res


def _mmln(pairs, *, tm, name, resid=None, resid_scale=1.0, epi=None, ln=None, n_out=D_MODEL):
    t = pairs[0][0].shape[0]
    n_pairs = len(pairs)
    n_resid = 0 if resid is None else len(resid) - 1

    def body(*refs):
        pos = 0
        val = None
        for p in range(n_pairs):
            a_ref, b_ref = refs[pos], refs[pos + 1]
            pos += 2
            av = a_ref[...].astype(bf16)
            bv = b_ref[...].astype(bf16)
            if pairs[p][6] == "nn":
                term = jnp.dot(av, bv, preferred_element_type=f32)
            else:
                term = lax.dot_general(av, bv, _NT, preferred_element_type=f32)
            val = term if val is None else val + term
        if resid is not None:
            if resid[0] == "plain":
                r = refs[pos][...]
            else:
                r = refs[pos][...] * refs[pos + 1][...] + refs[pos + 2][...]
            pos += n_resid
            val = val + resid_scale * r
        if epi is None:
            o_ref = refs[pos]
            o_ref[...] = val.astype(o_ref.dtype)
        elif epi == "ln_fwd":
            xo, rstd = _ln_fwd_tile(val)
            refs[pos][...] = xo
            refs[pos + 1][...] = rstd
        else:
            xh_ref, rs_ref, g_ref, dx_ref, gg_ref, gb_ref = refs[pos:pos + 6]
            dprev, gg, gb = _ln_bwd_tile(val, xh_ref[...], rs_ref[...], g_ref[...])
            dx_ref[...] = dprev
            i = pl.program_id(0)

            @pl.when(i == 0)
            def _():
                gg_ref[...] = gg
                gb_ref[...] = gb

            @pl.when(i > 0)
            def _():
                gg_ref[...] += gg
                gb_ref[...] += gb

    in_specs, args = [], []
    for (a, acb, aw, b, bcb, bw, mode) in pairs:
        in_specs.append(pl.BlockSpec((tm, aw), lambda i, acb=acb: (i, acb)))
        args.append(a)
        if mode == "nn":
            in_specs.append(pl.BlockSpec((aw, n_out), lambda i, bcb=bcb: (bcb, 0)))
        else:
            in_specs.append(pl.BlockSpec((n_out, bw), lambda i, bcb=bcb: (0, bcb)))
        args.append(b)
    tok = pl.BlockSpec((tm, n_out), lambda i: (i, 0))
    row = pl.BlockSpec((1, n_out), lambda i: (0, 0))
    col = pl.BlockSpec((tm, 1), lambda i: (i, 0))
    if resid is not None:
        in_specs += [tok] if resid[0] == "plain" else [tok, row, row]
        args += list(resid[1:])
    if epi is None:
        out_specs, out_shape = tok, jax.ShapeDtypeStruct((t, n_out), f32)
    elif epi == "ln_fwd":
        out_specs = [tok, col]
        out_shape = [jax.ShapeDtypeStruct((t, n_out), f32), jax.ShapeDtypeStruct((t, 1), f32)]
    else:
        in_specs += [tok, col, row]
        args += list(ln)
        out_specs = [tok, row, row]
        out_shape = [jax.ShapeDtypeStruct((t, n_out), f32)] + [jax.ShapeDtypeStruct((1, n_out), f32)] * 2
    return pl.pallas_call(
        body, name=name, grid=(t // tm,), in_specs=in_specs, out_specs=out_specs, out_shape=out_shape,
        compiler_params=_params(("arbitrary",)),
    )(*args)


def _loss_bwd(xhat, rstd, g, b, target, *, tm, name):
    t = xhat.shape[0]

    def body(xh_ref, rs_ref, g_ref, b_ref, tg_ref, dx_ref, sq_ref, gg_ref, gb_ref):
        i = pl.program_id(0)
        xh = xh_ref[...]
        diff = xh * g_ref[...] + b_ref[...] - tg_ref[...]
        sq = jnp.sum(diff * diff, axis=0, keepdims=True)
        dprev, gg, gb = _ln_bwd_tile(diff * (1.0 / D_MODEL), xh, rs_ref[...], g_ref[...])
        dx_ref[...] = dprev

        @pl.when(i == 0)
        def _():
            sq_ref[...] = sq
            gg_ref[...] = gg
            gb_ref[...] = gb

        @pl.when(i > 0)
        def _():
            sq_ref[...] += sq
            gg_ref[...] += gg
            gb_ref[...] += gb

    tok = pl.BlockSpec((tm, D_MODEL), lambda i: (i, 0))
    row = pl.BlockSpec((1, D_MODEL), lambda i: (0, 0))
    return pl.pallas_call(
        body, name=name, grid=(t // tm,),
        in_specs=[tok, pl.BlockSpec((tm, 1), lambda i: (i, 0)), row, row, tok],
        out_specs=[tok, row, row, row],
        out_shape=[jax.ShapeDtypeStruct((t, D_MODEL), f32)] + [jax.ShapeDtypeStruct((1, D_MODEL), f32)] * 3,
        compiler_params=_params(("arbitrary",)),
    )(xhat, rstd, g, b, target)


CUM_TILE = 256


def _tri(n, lower):
    r = lax.broadcasted_iota(jnp.int32, (n, n), 0)
    c = lax.broadcasted_iota(jnp.int32, (n, n), 1)
    return jnp.where((r >= c) if lower else (r <= c), 1.0, 0.0).astype(f32)


def _cum_fwd(zfg, bfg, *, name):
    t = zfg.shape[0]

    def body(z_ref, b_ref, o_ref, carry):
        @pl.when(pl.program_id(0) == 0)
        def _():
            carry[...] = jnp.zeros_like(carry)

        ls = -_softplus(-(z_ref[...] + b_ref[...]))
        c = jnp.dot(_tri(CUM_TILE, True), ls, preferred_element_type=f32,
                    precision=lax.Precision.HIGHEST) + carry[...]
        o_ref[...] = c
        carry[...] = c[CUM_TILE - 1:CUM_TILE, :]

    blk = pl.BlockSpec((CUM_TILE, LANES), lambda i: (i, 0))
    return pl.pallas_call(
        body, name=name, grid=(t // CUM_TILE,),
        in_specs=[blk, pl.BlockSpec((1, LANES), lambda i: (0, 0))], out_specs=blk,
        out_shape=jax.ShapeDtypeStruct((t, LANES), f32), scratch_shapes=[pltpu.VMEM((1, LANES), f32)],
        compiler_params=_params(("arbitrary",)),
    )(zfg, bfg)


def _cum_bwd(dcum_q, dcum_k, zfg, bfg, *, name):
    t = zfg.shape[0]
    n = t // CUM_TILE

    def body(d_ref, d2_ref, z_ref, b_ref, o_ref, s_ref, carry):
        i = pl.program_id(0)

        @pl.when(i == 0)
        def _():
            carry[...] = jnp.zeros_like(carry)

        dls = jnp.dot(_tri(CUM_TILE, False), d_ref[...] + d2_ref[...], preferred_element_type=f32,
                      precision=lax.Precision.HIGHEST) + carry[...]
        carry[...] = dls[0:1, :]
        lane = lax.broadcasted_iota(jnp.int32, (CUM_TILE, LANES), 1)
        dfg = jnp.where(lane < HEADS, dls * _sigmoid(-(z_ref[...] + b_ref[...])), 0.0)
        o_ref[...] = dfg
        tot = jnp.sum(dfg, axis=0, keepdims=True)

        @pl.when(i == 0)
        def _():
            s_ref[...] = tot

        @pl.when(i > 0)
        def _():
            s_ref[...] += tot

    blk = pl.BlockSpec((CUM_TILE, LANES), lambda i: (n - 1 - i, 0))
    row = pl.BlockSpec((1, LANES), lambda i: (0, 0))
    return pl.pallas_call(
        body, name=name, grid=(n,), in_specs=[blk, blk, blk, row], out_specs=[blk, row],
        out_shape=[jax.ShapeDtypeStruct((t, LANES), f32), jax.ShapeDtypeStruct((1, LANES), f32)],
        scratch_shapes=[pltpu.VMEM((1, LANES), f32)],
        compiler_params=_params(("arbitrary",)),
    )(dcum_q, dcum_k, zfg, bfg)


ATT_TILE = 512


def _causal(i, j, transposed):
    r = lax.broadcasted_iota(jnp.int32, (ATT_TILE, ATT_TILE), 0)
    c = lax.broadcasted_iota(jnp.int32, (ATT_TILE, ATT_TILE), 1)
    if transposed:
        return (c + i * ATT_TILE) >= (r + j * ATT_TILE)
    return (r + i * ATT_TILE) >= (c + j * ATT_TILE)


def _attn_fwd(qkv, cum, cum_t, *, name, host=None):
    t = qkv.shape[0]
    n = t // ATT_TILE
    tq = ATT_TILE

    def body(q_ref, k_ref, v_ref, cq_ref, ck_ref, o_ref, lse_ref, acc, m_s, l_s):
        i = pl.program_id(0)
        j = pl.program_id(1)

        @pl.when(j == 0)
        def _():
            acc[...] = jnp.zeros_like(acc)
            m_s[...] = jnp.full_like(m_s, NEG_BIG)
            l_s[...] = jnp.zeros_like(l_s)

        def block(masked):
            mask = _causal(i, j, False) if masked else None
            for h in range(HEADS):
                hs = slice(HEAD_D * h, HEAD_D * (h + 1))
                s = lax.dot_general(q_ref[:, hs] * ATT_SCALE, k_ref[:, hs], _NT, preferred_element_type=f32)
                s = s + cq_ref[:, h:h + 1] - ck_ref[h:h + 1, :]
                if masked:
                    s = jnp.where(mask, s, NEG_BIG)
                m_old = m_s[:, h:h + 1]
                m_new = jnp.maximum(m_old, jnp.max(s, axis=-1, keepdims=True))
                corr = jnp.exp(m_old - m_new)
                p = jnp.exp(s - m_new)
                l_s[:, h:h + 1] = corr * l_s[:, h:h + 1] + jnp.sum(p, axis=-1, keepdims=True)
                acc[:, hs] = corr * acc[:, hs] + jnp.dot(p.astype(bf16), v_ref[:, hs], preferred_element_type=f32)
                m_s[:, h:h + 1] = m_new

        @pl.when(j < i)
        def _():
            block(False)

        @pl.when(j == i)
        def _():
            block(True)
            lse_ref[...] = jnp.zeros_like(lse_ref)
            for h in range(HEADS):
                hs = slice(HEAD_D * h, HEAD_D * (h + 1))
                l = l_s[:, h:h + 1]
                o_ref[:, hs] = acc[:, hs] / l
                lse_ref[:, h:h + 1] = m_s[:, h:h + 1] + jnp.log(l)

    return _hosted_call(
        host, body, name=name, grid=(n, n),
        in_specs=[pl.BlockSpec((tq, FOX_W), lambda i, j: (i, 0)),
                  pl.BlockSpec((tq, FOX_W), lambda i, j: (jnp.minimum(i, j), 1)),
                  pl.BlockSpec((tq, FOX_W), lambda i, j: (jnp.minimum(i, j), 2)),
                  pl.BlockSpec((tq, LANES), lambda i, j: (i, 0)),
                  pl.BlockSpec((HEADS, tq), lambda i, j: (0, jnp.minimum(i, j)))],
        out_specs=[pl.BlockSpec((tq, FOX_W), lambda i, j: (i, 0)), pl.BlockSpec((tq, LANES), lambda i, j: (i, 0))],
        out_shape=[jax.ShapeDtypeStruct((t, FOX_W), f32), jax.ShapeDtypeStruct((t, LANES), f32)],
        scratch_shapes=[pltpu.VMEM((tq, FOX_W), f32), pltpu.VMEM((tq, LANES), f32), pltpu.VMEM((tq, LANES), f32)],
        compiler_params=_params(("arbitrary", "arbitrary")),
    )(qkv, qkv, qkv, cum, cum_t)


def _attn_delta(dmix, o, *, tm, name):
    t = o.shape[0]

    def body(do_ref, o_ref, d_ref):
        r = lax.broadcasted_iota(jnp.int32, (FOX_W, LANES), 0)
        c = lax.broadcasted_iota(jnp.int32, (FOX_W, LANES), 1)
        pick = jnp.where(r // HEAD_D == c, 1.0, 0.0).astype(f32)
        d_ref[...] = jnp.dot(do_ref[...] * o_ref[...], pick, preferred_element_type=f32,
                             precision=lax.Precision.HIGHEST)

    blk = pl.BlockSpec((tm, FOX_W), lambda i: (i, 0))
    return pl.pallas_call(
        body, name=name, grid=(t // tm,), in_specs=[blk, blk],
        out_specs=pl.BlockSpec((tm, LANES), lambda i: (i, 0)),
        out_shape=jax.ShapeDtypeStruct((t, LANES), f32), compiler_params=_params(("arbitrary",)),
    )(dmix, o)


def _attn_dq(qkv, dmix, cum, cum_t, lse, delta, *, name, host=None):
    t = qkv.shape[0]
    n = t // ATT_TILE
    tq = ATT_TILE

    def body(q_ref, k_ref, v_ref, do_ref, cq_ref, ck_ref, lse_ref, dl_ref, dq_ref, dc_ref, acc, dc_acc):
        i = pl.program_id(0)
        j = pl.program_id(1)

        @pl.when(j == 0)
        def _():
            acc[...] = jnp.zeros_like(acc)
            dc_acc[...] = jnp.zeros_like(dc_acc)

        def block(masked):
            mask = _causal(i, j, False) if masked else None
            for h in range(HEADS):
                hs = slice(HEAD_D * h, HEAD_D * (h + 1))
                kh = k_ref[:, hs]
                s = lax.dot_general(q_ref[:, hs] * ATT_SCALE, kh, _NT, preferred_element_type=f32)
                s = s + cq_ref[:, h:h + 1] - ck_ref[h:h + 1, :]
                if masked:
                    s = jnp.where(mask, s, NEG_BIG)
                p = jnp.exp(s - lse_ref[:, h:h + 1])
                dp = lax.dot_general(do_ref[:, hs].astype(bf16), v_ref[:, hs], _NT, preferred_element_type=f32)
                ds = p * (dp - dl_ref[:, h:h + 1])
                acc[:, hs] += jnp.dot(ds.astype(bf16), kh, preferred_element_type=f32)
                dc_acc[:, h:h + 1] += jnp.sum(ds, axis=-1, keepdims=True)

        @pl.when(j < i)
        def _():
            block(False)

        @pl.when(j == i)
        def _():
            block(True)
            dq_ref[...] = (acc[...] * ATT_SCALE).astype(bf16)
            dc_ref[...] = dc_acc[...]

    col = pl.BlockSpec((tq, LANES), lambda i, j: (i, 0))
    return _hosted_call(
        host, body, name=name, grid=(n, n),
        in_specs=[pl.BlockSpec((tq, FOX_W), lambda i, j: (i, 0)),
                  pl.BlockSpec((tq, FOX_W), lambda i, j: (jnp.minimum(i, j), 1)),
                  pl.BlockSpec((tq, FOX_W), lambda i, j: (jnp.minimum(i, j), 2)),
                  pl.BlockSpec((tq, FOX_W), lambda i, j: (i, 0)),
                  col, pl.BlockSpec((HEADS, tq), lambda i, j: (0, jnp.minimum(i, j))), col, col],
        out_specs=[pl.BlockSpec((tq, FOX_W), lambda i, j: (i, 0)), col],
        out_shape=[jax.ShapeDtypeStruct((t, FOX_W), bf16), jax.ShapeDtypeStruct((t, LANES), f32)],
        scratch_shapes=[pltpu.VMEM((tq, FOX_W), f32), pltpu.VMEM((tq, LANES), f32)],
        compiler_params=_params(("arbitrary", "arbitrary")),
    )(qkv, qkv, qkv, dmix, cum, cum_t, lse, delta)


def _attn_dkv(qkv, dmix, cum, cum_t, lse_t, delta_t, *, name):
    t = qkv.shape[0]
    n = t // ATT_TILE
    tk = ATT_TILE

    def body(q_ref, k_ref, v_ref, do_ref, cq_ref, ck_ref, lse_ref, dl_ref, dk_ref, dv_ref, dc_ref, dk_acc, dv_acc, dc_acc):
        j = pl.program_id(0)
        i = pl.program_id(1)

        @pl.when(i == 0)
        def _():
            dk_acc[...] = jnp.zeros_like(dk_acc)
            dv_acc[...] = jnp.zeros_like(dv_acc)
            dc_acc[...] = jnp.zeros_like(dc_acc)

        def block(masked):
            mask = _causal(i, j, True) if masked else None
            for h in range(HEADS):
                hs = slice(HEAD_D * h, HEAD_D * (h + 1))
                qh = q_ref[:, hs]
                doh = do_ref[:, hs].astype(bf16)
                s_t = lax.dot_general(k_ref[:, hs] * ATT_SCALE, qh, _NT, preferred_element_type=f32)
                s_t = s_t + cq_ref[h:h + 1, :] - ck_ref[:, h:h + 1]
                if masked:
                    s_t = jnp.where(mask, s_t, NEG_BIG)
                p_t = jnp.exp(s_t - lse_ref[h:h + 1, :])
                dv_acc[:, hs] += jnp.dot(p_t.astype(bf16), doh, preferred_element_type=f32)
                dp_t = lax.dot_general(v_ref[:, hs], doh, _NT, preferred_element_type=f32)
                ds_t = p_t * (dp_t - dl_ref[h:h + 1, :])
                dk_acc[:, hs] += jnp.dot(ds_t.astype(bf16), qh, preferred_element_type=f32)
                dc_acc[:, h:h + 1] -= jnp.sum(ds_t, axis=-1, keepdims=True)

        @pl.when(i > j)
        def _():
            block(False)

        @pl.when(i == j)
        def _():
            block(True)

        @pl.when(i == n - 1)
        def _():
            dk_ref[...] = (dk_acc[...] * ATT_SCALE).astype(bf16)
            dv_ref[...] = dv_acc[...].astype(bf16)
            dc_ref[...] = dc_acc[...]

    rowq = pl.BlockSpec((HEADS, tk), lambda j, i: (0, jnp.maximum(i, j)))
    return pl.pallas_call(
        body, name=name, grid=(n, n),
        in_specs=[pl.BlockSpec((tk, FOX_W), lambda j, i: (jnp.maximum(i, j), 0)),
                  pl.BlockSpec((tk, FOX_W), lambda j, i: (j, 1)),
                  pl.BlockSpec((tk, FOX_W), lambda j, i: (j, 2)),
                  pl.BlockSpec((tk, FOX_W), lambda j, i: (jnp.maximum(i, j), 0)),
                  rowq, pl.BlockSpec((tk, LANES), lambda j, i: (j, 0)), rowq, rowq],
        out_specs=[pl.BlockSpec((tk, FOX_W), lambda j, i: (j, 0)), pl.BlockSpec((tk, FOX_W), lambda j, i: (j, 0)),
                   pl.BlockSpec((tk, LANES), lambda j, i: (j, 0))],
        out_shape=[jax.ShapeDtypeStruct((t, FOX_W), bf16), jax.ShapeDtypeStruct((t, FOX_W), bf16),
                   jax.ShapeDtypeStruct((t, LANES), f32)],
        scratch_shapes=[pltpu.VMEM((tk, FOX_W), f32), pltpu.VMEM((tk, FOX_W), f32), pltpu.VMEM((tk, LANES), f32)],
        compiler_params=_params(("arbitrary", "arbitrary")),
    )(qkv, qkv, qkv, dmix, cum_t, cum, lse_t, delta_t)


LRU_CHUNK = 64
SUB = 8


def _row_ids(n):
    return lax.broadcasted_iota(jnp.int32, (n, LANES), 0)


def _shift_rows_down(ext, s):
    return pltpu.roll(ext, s, axis=0)[SUB:, :]


def _shift_rows_up(ext, s, n):
    return pltpu.roll(ext, ext.shape[0] - s, axis=0)[:n, :]


def _lru_gates(u, wa_ref, ba_ref, wx_ref, bx_ref, sp):
    ub = u.astype(bf16)
    r = _sigmoid(jnp.dot(ub, wa_ref[...], preferred_element_type=f32) + ba_ref[...])
    gi = _sigmoid(jnp.dot(ub, wx_ref[...], preferred_element_type=f32) + bx_ref[...])
    log_a = -LRU_C * r * sp
    a = jnp.exp(log_a)
    s = jnp.sqrt(_one_minus_exp(2.0 * log_a))
    return r, gi, a, s


def _conv_window(lx_ref, r0, ci):
    cur = lx_ref[pl.ds(r0, LRU_CHUNK), :]
    p0 = pl.multiple_of(jnp.maximum(r0 - SUB, 0), SUB)
    prev = jnp.where(ci > 0, lx_ref[pl.ds(p0, SUB), :], 0.0)
    return cur, jnp.concatenate([prev, cur], axis=0)


def _lru_fwd(zl, conv_w, conv_b, wa, ba, wx, bx, lam, *, name):
    t = zl.shape[0]
    n_chunk = t // LRU_CHUNK

    def body(lx_ref, lg_ref, cw_ref, cb_ref, wa_ref, ba_ref, wx_ref, bx_ref, lam_ref, u_ref, h_ref, y_ref):
        sp = _softplus(-lam_ref[...])
        rows = _row_ids(SUB)

        def chunk(ci, hc):
            r0 = pl.multiple_of(ci * LRU_CHUNK, LRU_CHUNK)
            cur, ext = _conv_window(lx_ref, r0, ci)
            u = cb_ref[...] + cw_ref[3:4, :] * cur
            for k in range(3):
                u = u + cw_ref[k:k + 1, :] * _shift_rows_down(ext, 3 - k)
            r, gi, a, s = _lru_gates(u, wa_ref, ba_ref, wx_ref, bx_ref, sp)
            b = s * (gi * u)
            tiles = []
            for q in range(LRU_CHUNK // SUB):
                ta = a[SUB * q:SUB * (q + 1), :]
                tb = b[SUB * q:SUB * (q + 1), :]
                for d in (1, 2, 4):
                    a_sh = jnp.where(rows >= d, pltpu.roll(ta, d, axis=0), 1.0)
                    b_sh = jnp.where(rows >= d, pltpu.roll(tb, d, axis=0), 0.0)
                    tb = ta * b_sh + tb
                    ta = ta * a_sh
                hq = tb + ta * hc
                hc = hq[SUB - 1:SUB, :]
                tiles.append(hq)
            h = jnp.concatenate(tiles, axis=0)
            u_ref[pl.ds(r0, LRU_CHUNK), :] = u
            h_ref[pl.ds(r0, LRU_CHUNK), :] = h
            gel, _ = _gelu_and_grad(lg_ref[pl.ds(r0, LRU_CHUNK), :])
            y_ref[pl.ds(r0, LRU_CHUNK), :] = gel * h
            return hc

        lax.fori_loop(0, n_chunk, chunk, jnp.zeros((1, LANES), f32))

    seq = lambda cb: pl.BlockSpec((t, LANES), lambda c, cb=cb: (0, c + cb))
    rowc = pl.BlockSpec((1, LANES), lambda c: (0, c))
    diag = pl.BlockSpec((LANES, LANES), lambda c: (c, c))
    out = jax.ShapeDtypeStruct((t, LRU_W), f32)
    return pl.pallas_call(
        body, name=name, grid=(LRU_W // LANES,),
        in_specs=[seq(0), seq(4), pl.BlockSpec((4, LANES), lambda c: (0, c)), rowc, diag, rowc, diag, rowc, rowc],
        out_specs=[seq(0)] * 3, out_shape=[out] * 3,
        compiler_params=_params(("arbitrary",)),
    )(zl, zl, conv_w, conv_b, wa, ba, wx, bx, lam)


def _lru_bwd(dmix, zl, u_all, h_all, conv_w, wa, ba, wx, bx, lam, *, name):
    t = zl.shape[0]
    n_chunk = t // LRU_CHUNK

    def body(dy_ref, lx_ref, lg_ref, u_ref, h_ref, cw_ref, wa_ref, ba_ref, wx_ref, bx_ref, lam_ref,
             dlx_ref, dlg_ref, dcw_ref, dcb_ref, dba_ref, dbx_ref, dlam_ref, dwa_ref, dwx_ref, dpr_s, dpx_s):
        lam_v = lam_ref[...]
        sp = _softplus(-lam_v)
        rows = _row_ids(SUB)
        rows_c = _row_ids(LRU_CHUNK)
        zero_row = jnp.zeros((1, LANES), f32)

        def chunk(step, carry):
            dh_c, a_next0, du_next, dsp, dba, dbx, dcb, dw0, dw1, dw2, dw3 = carry
            ci = n_chunk - 1 - step
            r0 = pl.multiple_of(ci * LRU_CHUNK, LRU_CHUNK)
            sl = pl.ds(r0, LRU_CHUNK)
            u = u_ref[sl, :]
            r, gi, a, s = _lru_gates(u, wa_ref, ba_ref, wx_ref, bx_ref, sp)
            h = h_ref[sl, :]
            p0 = pl.multiple_of(jnp.maximum(r0 - SUB, 0), SUB)
            h_before = jnp.where(ci > 0, h_ref[pl.ds(p0, SUB), :], 0.0)[SUB - 1:SUB, :]
            h_prev = jnp.where(rows_c == 0, h_before, pltpu.roll(h, 1, axis=0))
            gel, dgel = _gelu_and_grad(lg_ref[sl, :])
            dy = dy_ref[sl, :]
            dlg_ref[sl, :] = (dy * h * dgel).astype(bf16)
            g_in = dy * gel
            a_next = jnp.where(rows_c == LRU_CHUNK - 1, a_next0, pltpu.roll(a, LRU_CHUNK - 1, axis=0))
            tiles = [None] * (LRU_CHUNK // SUB)
            for q in reversed(range(LRU_CHUNK // SUB)):
                ta = a_next[SUB * q:SUB * (q + 1), :]
                tb = g_in[SUB * q:SUB * (q + 1), :]
                for d in (1, 2, 4):
                    a_sh = jnp.where(rows < SUB - d, pltpu.roll(ta, SUB - d, axis=0), 1.0)
                    b_sh = jnp.where(rows < SUB - d, pltpu.roll(tb, SUB - d, axis=0), 0.0)
                    tb = ta * b_sh + tb
                    ta = ta * a_sh
                dhq = tb + ta * dh_c
                dh_c = dhq[0:1, :]
                tiles[q] = dhq
            dh = jnp.concatenate(tiles, axis=0)
            da = dh * h_prev
            ds = dh * gi * u
            dgi = dh * s * u
            du = dh * s * gi
            dlog_a = da * a - ds * (a * a) / s
            dr = dlog_a * (-LRU_C * sp)
            dsp = dsp + jnp.sum(dlog_a * (-LRU_C * r), axis=0, keepdims=True)
            dpr = dr * r * (1.0 - r)
            dpx = dgi * gi * (1.0 - gi)
            dprb = dpr.astype(bf16)
            dpxb = dpx.astype(bf16)
            dpr_s[sl, :] = dprb
            dpx_s[sl, :] = dpxb
            du = du + (lax.dot_general(dprb, wa_ref[...], _NT, preferred_element_type=f32)
                       + lax.dot_general(dpxb, wx_ref[...], _NT, preferred_element_type=f32))
            dba = dba + jnp.sum(dpr, axis=0, keepdims=True)
            dbx = dbx + jnp.sum(dpx, axis=0, keepdims=True)
            dcb = dcb + jnp.sum(du, axis=0, keepdims=True)
            du_ext = jnp.concatenate([du, du_next], axis=0)
            dlx = cw_ref[3:4, :] * du
            for k in range(3):
                dlx = dlx + cw_ref[k:k + 1, :] * _shift_rows_up(du_ext, 3 - k, LRU_CHUNK)
            dlx_ref[sl, :] = dlx.astype(bf16)
            cur, ext = _conv_window(lx_ref, r0, ci)
            dws = [dw0, dw1, dw2, dw3 + jnp.sum(du * cur, axis=0, keepdims=True)]
            for k in range(3):
                dws[k] = dws[k] + jnp.sum(du * _shift_rows_down(ext, 3 - k), axis=0, keepdims=True)
            return (dh_c, a[0:1, :], du[0:SUB, :], dsp, dba, dbx, dcb, dws[0], dws[1], dws[2], dws[3])

        init = (zero_row, zero_row, jnp.zeros((SUB, LANES), f32)) + (zero_row,) * 8
        out = lax.fori_loop(0, n_chunk, chunk, init)
        _, _, _, dsp, dba, dbx, dcb, dw0, dw1, dw2, dw3 = out
        dlam_ref[...] = dsp * (-_sigmoid(-lam_v))
        dba_ref[...] = dba
        dbx_ref[...] = dbx
        dcb_ref[...] = dcb
        dcw_ref[...] = jnp.concatenate([dw0, dw1, dw2, dw3], axis=0)
        ub = u_ref[...].astype(bf16)
        dwa_ref[...] = lax.dot_general(ub, dpr_s[...], _TN, preferred_element_type=f32)
        dwx_ref[...] = lax.dot_general(ub, dpx_s[...], _TN, preferred_element_type=f32)

    seq = lambda cb: pl.BlockSpec((t, LANES), lambda c, cb=cb: (0, c + cb))
    rowc = pl.BlockSpec((1, LANES), lambda c: (0, c))
    diag = pl.BlockSpec((LANES, LANES), lambda c: (c, c))
    gate_out = pl.BlockSpec((None, LANES, LANES), lambda c: (c, 0, 0))
    row_shape = jax.ShapeDtypeStruct((1, LRU_W), f32)
    return pl.pallas_call(
        body, name=name, grid=(LRU_W // LANES,),
        in_specs=[seq(4), seq(0), seq(4), seq(0), seq(0), pl.BlockSpec((4, LANES), lambda c: (0, c)),
                  diag, rowc, diag, rowc, rowc],
        out_specs=[seq(0), seq(0), pl.BlockSpec((4, LANES), lambda c: (0, c)), rowc, rowc, rowc, rowc,
                   gate_out, gate_out],
        out_shape=[jax.ShapeDtypeStruct((t, LRU_W), bf16)] * 2
        + [jax.ShapeDtypeStruct((4, LRU_W), f32)] + [row_shape] * 4
        + [jax.ShapeDtypeStruct((LRU_W // LANES, LANES, LANES), f32)] * 2,
        scratch_shapes=[pltpu.VMEM((t, LANES), bf16), pltpu.VMEM((t, LANES), bf16)],
        compiler_params=_params(("arbitrary",)),
    )(dmix, zl, zl, u_all, h_all, conv_w, wa, ba, wx, bx, lam)


def _block_diag(w):
    eye = jnp.eye(HEADS, dtype=w.dtype)
    return jnp.einsum("hij,hk->hikj", w, eye).reshape(LRU_W, LRU_W)


def _diag_blocks(dw):
    top = dw[:, :HEAD_D, :HEAD_D]
    bot = dw[:, HEAD_D:, HEAD_D:]
    return jnp.stack([top, bot], axis=1).reshape(HEADS, HEAD_D, HEAD_D)


def _local_step(x, target, sent, small, *, tm=512):
    t = x.shape[0]
    ones = jnp.ones((1, D_MODEL), f32)
    zeros = jnp.zeros((1, D_MODEL), f32)
    ln1 = (small["ln1_g"], small["ln1_b"])
    ln2 = (small["ln2_g"], small["ln2_b"])
    ln3 = (small["ln3_g"], small["ln3_b"])

    wg1, wu1, wd1 = _exchange([sent["ffn1_w_gate"], sent["ffn1_w_up"], sent["ffn1_w_down"]], gather=True,
                              name="gather_ffn1")
    xh1, rs1, hg1, hu1, w_in_g, w_out_g, conv_w_g = _ffn_fwd(
        x, ones, zeros, wg1, wu1, wd1, tm=tm, name="ffn1_fwd",
        host=_Exchange([sent["w_in"], sent["w_out"], sent["conv_w"]], gather=True))
    w_in = jnp.pad(w_in_g.transpose(1, 0, 2).reshape(D_MODEL, IN_COLS), ((0, 0), (0, 21 * LANES - IN_COLS)))
    w_out = w_out_g.reshape(D_MODEL, D_MODEL)
    conv_w = conv_w_g.transpose(1, 0, 2).reshape(4, LRU_W)
    qkv = _mm(xh1, w_in, mode="nn", out_dtype=bf16, tm=tm, tn=512, tk=D_MODEL, name="qkv_fwd",
              affine=ln1, b_cols=(0, 1536))
    zl = _mm(xh1, w_in, mode="nn", out_dtype=f32, tm=tm, tn=512, tk=D_MODEL, name="zl_fwd",
             affine=ln1, b_cols=(3, 1024))
    zfg = _mm(xh1, w_in, mode="nn", out_dtype=f32, tm=tm, tn=LANES, tk=D_MODEL, name="zfg_fwd",
              affine=ln1, b_cols=(20, LANES))
    bfg = jnp.pad(small["b_forget"], ((0, 0), (0, LANES - HEADS)))
    cum = _cum_fwd(zfg, bfg, name="cum_fwd")
    cum_t = cum[:, :HEADS].T
    o, lse, wg2, wu2, wd2 = _attn_fwd(
        qkv, cum, cum_t, name="attn_fwd",
        host=_Exchange([sent["ffn2_w_gate"], sent["ffn2_w_up"], sent["ffn2_w_down"]], gather=True))
    wa_bd = _block_diag(small["rg_wa"]).astype(bf16)
    wx_bd = _block_diag(small["rg_wx"]).astype(bf16)
    ba = small["rg_ba"].reshape(1, LRU_W)
    bx = small["rg_bx"].reshape(1, LRU_W)
    u, h, lru = _lru_fwd(zl, conv_w, small["conv_b"], wa_bd, ba, wx_bd, bx, small["lru_lambda"],
                         name="lru_fwd")
    xh2, rs2 = _mmln([(o, 0, FOX_W, w_out, 0, D_MODEL, "nn"), (lru, 0, LRU_W, w_out, 1, D_MODEL, "nn")],
                     tm=tm, name="mix_fwd", resid=("affine", xh1) + ln1, resid_scale=ALPHA, epi="ln_fwd")
    xh3, rs3, hg2, hu2 = _ffn_fwd(xh2, ln2[0], ln2[1], wg2, wu2, wd2, tm=tm, name="ffn2_fwd")

    dpre3, sq_rows, g_ln3g, g_ln3b = _loss_bwd(xh3, rs3, ln3[0], ln3[1], target, tm=tm, name="loss_bwd")
    dpre2, g_ln2g, g_ln2b, dhg2, dhu2, a2 = _ffn_bwd(dpre3, hg2, hu2, wg2, wu2, wd2,
                                                     (xh2, rs2, ln2[0]), tm=tm, name="ffn2_bwd")
    wgrad = dict(out_dtype=bf16, tm=D_MODEL, mb=1, tn=FF_TILE, nb=4, tk=512, out_blocked=True)
    wdgrad = dict(out_dtype=bf16, tm=512, mb=4, tn=D_MODEL, nb=1, tk=512, out_scale=0.5)
    g_wg2 = _mm_tn(xh2, dhg2, name="g_wg2", affine=ln2, **wgrad)
    g_wu2 = _mm_tn(xh2, dhu2, name="g_wu2", affine=ln2, **wgrad)
    g_wd2 = _mm_tn(a2, dpre3, name="g_wd2", **wdgrad)

    dmix = _mmln([(dpre2, 0, D_MODEL, w_out, 0, D_MODEL, "nt")], tm=tm, name="dmix_bwd")
    g_wout_a = _mm(o, dpre2, mode="tn", out_dtype=bf16, tm=512, tn=D_MODEL, tk=512, name="g_wout_fox")
    g_wout_b = _mm(lru, dpre2, mode="tn", out_dtype=bf16, tm=512, tn=D_MODEL, tk=512, name="g_wout_lru")
    dlx, dlg, g_cw, g_cb, g_ba, g_bx, g_lam, g_wa4, g_wx4 = _lru_bwd(
        dmix, zl, u, h, conv_w, wa_bd, ba, wx_bd, bx, small["lru_lambda"], name="lru_bwd")
    delta = _attn_delta(dmix, o, tm=tm, name="attn_delta")
    dq, dcum_q, p_wg2, p_wu2, p_wd2 = _attn_dq(
        qkv, dmix, cum, cum_t, lse, delta, name="attn_dq",
        host=_Exchange([g_wg2, g_wu2, g_wd2.reshape(N_DEV, FF_TILE, D_MODEL)], gather=False))
    dk, dv, dcum_k = _attn_dkv(qkv, dmix, cum, cum_t, lse[:, :HEADS].T, delta[:, :HEADS].T, name="attn_dkv")
    dfg, g_bf = _cum_bwd(dcum_q, dcum_k, zfg, bfg, name="cum_bwd")

    dz = [(dq, 0, 512), (dk, 1, 512), (dv, 2, 512), (dlx, 3, 512), (dlg, 4, 512), (dfg, 20, LANES)]
    dpre1, g_ln1g, g_ln1b = _mmln(
        [(arr, 0, w, w_in, cb, w, "nt") for (arr, cb, w) in dz],
        tm=tm, name="dx1_bwd", resid=("plain", dpre2), resid_scale=ALPHA, epi="ln_bwd", ln=(xh1, rs1, ln1[0]))
    g_win = [_mm(xh1, arr, mode="tn", out_dtype=bf16, tm=D_MODEL, tn=w, tk=512, name=f"g_win{n}", affine=ln1)
             for n, (arr, cb, w) in enumerate(dz)]
    g_win_full = jnp.concatenate([g[:, :w] for g, (_, _, w) in zip(g_win, dz)], axis=1)[:, :IN_COLS]
    g_win_blocked = g_win_full.reshape(D_MODEL, N_DEV, IN_SHARD).transpose(1, 0, 2)
    g_wout_blocked = jnp.concatenate([g_wout_a, g_wout_b], axis=0).reshape(N_DEV, D_MODEL // N_DEV, D_MODEL)
    grad_x, dhg1, dhu1, a1, p_win, p_wout = _ffn_bwd(
        dpre1, hg1, hu1, wg1, wu1, wd1, None, tm=tm, name="ffn1_bwd",
        host=_Exchange([g_win_blocked, g_wout_blocked], gather=False))
    g_wg1 = _mm_tn(x, dhg1, name="g_wg1", **wgrad)
    g_wu1, p_wg1 = _mm_tn(x, dhu1, name="g_wu1", host=_Exchange([g_wg1], gather=False), **wgrad)
    g_wd1, p_wu1 = _mm_tn(a1, dpre1, name="g_wd1", host=_Exchange([g_wu1], gather=False), **wdgrad)

    (p_wd1,) = _exchange([g_wd1.reshape(N_DEV, FF_TILE, D_MODEL)], gather=False, name="scatter_ffn1_down")
    parts = {
        "ffn1_w_gate": p_wg1, "ffn1_w_up": p_wu1, "ffn1_w_down": p_wd1, "w_in": p_win, "w_out": p_wout,
        "ffn2_w_gate": p_wg2, "ffn2_w_up": p_wu2, "ffn2_w_down": p_wd2,
    }
    small_g = {
        "ln1_g": g_ln1g, "ln1_b": g_ln1b, "b_forget": g_bf[:, :HEADS], "conv_w": g_cw, "conv_b": g_cb,
        "rg_wa": _diag_blocks(g_wa4), "rg_ba": g_ba.reshape(HEADS, HEAD_D),
        "rg_wx": _diag_blocks(g_wx4), "rg_bx": g_bx.reshape(HEADS, HEAD_D), "lru_lambda": g_lam,
        "ln2_g": g_ln2g, "ln2_b": g_ln2b, "ln3_g": g_ln3g, "ln3_b": g_ln3b,
    }
    return sq_rows, grad_x, parts, small_g


def _adam_math(w, g, m, v):
    m2 = ADAM_B1 * m + (1.0 - ADAM_B1) * g
    v2 = ADAM_B2 * v + (1.0 - ADAM_B2) * (g * g)
    m_hat = m2 / (1.0 - ADAM_B1 ** ADAM_STEP)
    v_hat = v2 / (1.0 - ADAM_B2 ** ADAM_STEP)
    delta = -ADAM_LR * (m_hat / (jnp.sqrt(v_hat) + ADAM_EPS) + ADAM_WD * w)
    return delta, m2, v2


def _adamw_big(parts, w, m, v, *, tr, name):
    r, c = w.shape

    def body(p_ref, w_ref, m_ref, v_ref, g_ref, d_ref, m2_ref, v2_ref):
        g = p_ref[0].astype(f32)
        for q in range(1, N_DEV):
            g = g + p_ref[q].astype(f32)
        d, m2, v2 = _adam_math(w_ref[...], g, m_ref[...], v_ref[...])
        g_ref[...] = g
        d_ref[...] = d
        m2_ref[...] = m2
        v2_ref[...] = v2

    blk = pl.BlockSpec((tr, c), lambda i: (i, 0))
    return pl.pallas_call(
        body, name=name, grid=(r // tr,),
        in_specs=[pl.BlockSpec((N_DEV, tr, c), lambda i: (0, i, 0)), blk, blk, blk],
        out_specs=[blk] * 4, out_shape=[jax.ShapeDtypeStruct((r, c), f32)] * 4,
        compiler_params=_params(("arbitrary",)),
    )(parts, w, m, v)


def _adamw_small(items, *, name):
    n = len(items)

    def body(*refs):
        ins, outs = refs[:4 * n], refs[4 * n:]
        for k in range(n):
            g, w, m, v = (ins[4 * k + q][...] for q in range(4))
            d, m2, v2 = _adam_math(w, g, m, v)
            outs[3 * k][...] = d
            outs[3 * k + 1][...] = m2
            outs[3 * k + 2][...] = v2

    vm = pl.BlockSpec(memory_space=pltpu.VMEM)
    flat = [a for item in items for a in item]
    out_shape = [jax.ShapeDtypeStruct(item[1].shape, f32) for item in items for _ in range(3)]
    return pl.pallas_call(
        body, name=name, in_specs=[vm] * (4 * n), out_specs=[vm] * (3 * n), out_shape=out_shape,
    )(*flat)


def _sum_parts(parts, *, name):
    def body(p_ref, o_ref):
        acc = p_ref[0]
        for q in range(1, N_DEV):
            acc = acc + p_ref[q]
        o_ref[...] = acc

    vm = pl.BlockSpec(memory_space=pltpu.VMEM)
    return pl.pallas_call(
        body, name=name, in_specs=[vm], out_specs=vm, out_shape=jax.ShapeDtypeStruct(parts.shape[1:], f32),
    )(parts)


WEIGHTS = ["ffn1_w_gate", "ffn1_w_up", "ffn1_w_down", "ln1_g", "ln1_b", "w_in", "b_forget", "conv_w", "conv_b",
           "rg_wa", "rg_ba", "rg_wx", "rg_bx", "lru_lambda", "w_out", "ln2_g", "ln2_b",
           "ffn2_w_gate", "ffn2_w_up", "ffn2_w_down", "ln3_g", "ln3_b"]
BIG = ["ffn1_w_gate", "ffn1_w_up", "ffn1_w_down", "w_in", "w_out", "ffn2_w_gate", "ffn2_w_up", "ffn2_w_down"]
PACKED = ["ln1_g", "ln1_b", "ln2_g", "ln2_b", "ln3_g", "ln3_b", "conv_b", "rg_ba", "rg_bx", "lru_lambda",
          "conv_w", "rg_wa", "rg_wx", "b_forget"]
PACK_ROWS = 600


def _two_d(a):
    return a.reshape((-1, a.shape[-1]))


def kernel(x, ffn1_w_gate, ffn1_w_up, ffn1_w_down, ln1_g, ln1_b, w_in, b_forget, conv_w, conv_b, rg_wa, rg_ba, rg_wx, rg_bx, lru_lambda, w_out, ln2_g, ln2_b, ffn2_w_gate, ffn2_w_up, ffn2_w_down, ln3_g, ln3_b, loss_target, m_ffn1_w_gate, m_ffn1_w_up, m_ffn1_w_down, m_ln1_g, m_ln1_b, m_w_in, m_b_forget, m_conv_w, m_conv_b, m_rg_wa, m_rg_ba, m_rg_wx, m_rg_bx, m_lru_lambda, m_w_out, m_ln2_g, m_ln2_b, m_ffn2_w_gate, m_ffn2_w_up, m_ffn2_w_down, m_ln3_g, m_ln3_b, v_ffn1_w_gate, v_ffn1_w_up, v_ffn1_w_down, v_ln1_g, v_ln1_b, v_w_in, v_b_forget, v_conv_w, v_conv_b, v_rg_wa, v_rg_ba, v_rg_wx, v_rg_bx, v_lru_lambda, v_w_out, v_ln2_g, v_ln2_b, v_ffn2_w_gate, v_ffn2_w_up, v_ffn2_w_down, v_ln3_g, v_ln3_b):
    w_args = (ffn1_w_gate, ffn1_w_up, ffn1_w_down, ln1_g, ln1_b, w_in, b_forget, conv_w, conv_b, rg_wa, rg_ba, rg_wx, rg_bx, lru_lambda, w_out, ln2_g, ln2_b, ffn2_w_gate, ffn2_w_up, ffn2_w_down, ln3_g, ln3_b)
    m_args = (m_ffn1_w_gate, m_ffn1_w_up, m_ffn1_w_down, m_ln1_g, m_ln1_b, m_w_in, m_b_forget, m_conv_w, m_conv_b, m_rg_wa, m_rg_ba, m_rg_wx, m_rg_bx, m_lru_lambda, m_w_out, m_ln2_g, m_ln2_b, m_ffn2_w_gate, m_ffn2_w_up, m_ffn2_w_down, m_ln3_g, m_ln3_b)
    v_args = (v_ffn1_w_gate, v_ffn1_w_up, v_ffn1_w_down, v_ln1_g, v_ln1_b, v_w_in, v_b_forget, v_conv_w, v_conv_b, v_rg_wa, v_rg_ba, v_rg_wx, v_rg_bx, v_lru_lambda, v_w_out, v_ln2_g, v_ln2_b, v_ffn2_w_gate, v_ffn2_w_up, v_ffn2_w_down, v_ln3_g, v_ln3_b)
    w = dict(zip(WEIGHTS, w_args))
    m = dict(zip(WEIGHTS, m_args))
    v = dict(zip(WEIGHTS, v_args))
    me = 4 * lax.axis_index("x") + 2 * lax.axis_index("y") + lax.axis_index("c")

    sent = {n: _two_d(w[n]).astype(bf16) for n in BIG}
    sent["conv_w"] = _two_d(w["conv_w"])
    small = {n: w[n] for n in ("ln1_g", "ln1_b", "ln2_g", "ln2_b", "ln3_g", "ln3_b", "b_forget", "conv_b",
                               "lru_lambda")}
    small.update({n: w[n][0] for n in ("rg_wa", "rg_ba", "rg_wx", "rg_bx")})

    sq_rows, grad_x, parts, small_g = _local_step(x[0], loss_target[0], sent, small)
    loss = lax.psum(0.5 * jnp.sum(sq_rows) / D_MODEL, ("x", "y", "c"))

    pieces = [small_g[n].reshape(-1) for n in PACKED]
    packed = jnp.concatenate(pieces + [jnp.zeros((PACK_ROWS * LANES - sum(p.shape[0] for p in pieces),), f32)])
    (all_packed,) = _exchange([packed.reshape(PACK_ROWS, LANES)], gather=True, name="gather_small_grads")
    total = _sum_parts(all_packed, name="sum_small_grads").reshape(-1)
    grads, off = {}, 0
    for n, p in zip(PACKED, pieces):
        grads[n] = total[off:off + p.shape[0]].reshape(small_g[n].shape)
        off += p.shape[0]
    grads["conv_w"] = lax.dynamic_slice_in_dim(grads["conv_w"], me * (LRU_W // N_DEV), LRU_W // N_DEV, axis=1)

    delta, new_m, new_v = {}, {}, {}
    for n in BIG:
        w2 = _two_d(w[n])
        g, d, m2, v2 = _adamw_big(parts[n], w2, _two_d(m[n]), _two_d(v[n]), tr=128, name="adamw_" + n)
        grads[n], delta[n], new_m[n], new_v[n] = g, d, m2, v2
    small_names = [n for n in WEIGHTS if n not in BIG]
    outs = _adamw_small([(_two_d(grads[n]), _two_d(w[n]), _two_d(m[n]), _two_d(v[n])) for n in small_names],
                        name="adamw_small")
    for k, n in enumerate(small_names):
        delta[n], new_m[n], new_v[n] = outs[3 * k], outs[3 * k + 1], outs[3 * k + 2]

    def shaped(d):
        return [d[n].reshape(w[n].shape) for n in WEIGHTS]

    return (loss, grad_x[None], *shaped(grads), *shaped(delta), *shaped(new_m), *shaped(new_v))
```

```python
import functools
import math

import jax
import jax.numpy as jnp
from jax import lax
from jax.experimental import pallas as pl
from jax.experimental.pallas import tpu as pltpu

f32 = jnp.float32
bf16 = jnp.bfloat16

N_DEV = 8
D_MODEL = 1024
D_FF = 4096
FF_TILE = D_FF // N_DEV
FOX_W = 512
LRU_W = 512
HEADS = 8
HEAD_D = 64
IN_COLS = 2568
IN_SHARD = IN_COLS // N_DEV
LANES = 128
LN_EPS = 1e-5
ALPHA = 2.0 ** 0.25
ATT_SCALE = 1.0 / math.sqrt(HEAD_D)
LRU_C = 8.0
NEG_BIG = -1e30

ADAM_LR = 0.001
ADAM_B1 = 0.9
ADAM_B2 = 0.999
ADAM_EPS = 1e-08
ADAM_WD = 0.01
ADAM_STEP = 10

VMEM_LIMIT = 56 * 1024 * 1024
MESH_T = pl.DeviceIdType.MESH


def _params(sem, **kw):
    return pltpu.CompilerParams(dimension_semantics=sem, vmem_limit_bytes=VMEM_LIMIT, **kw)


def _sigmoid(x):
    return 1.0 / (1.0 + jnp.exp(-x))


def _softplus(x):
    return jnp.maximum(x, 0.0) + jnp.log(1.0 + jnp.exp(-jnp.abs(x)))


def _one_minus_exp(x):
    series = -x * (1.0 + x * (0.5 + x * (1.0 / 6 + x * (1.0 / 24 + x * (1.0 / 120 + x * (1.0 / 720))))))
    return jnp.where(x > -0.125, series, 1.0 - jnp.exp(x))


_GELU_C = math.sqrt(2.0 / math.pi)


def _gelu_and_grad(x):
    inner = _GELU_C * (x + 0.044715 * x * x * x)
    t = jnp.tanh(inner)
    g = 0.5 * x * (1.0 + t)
    dg = 0.5 * (1.0 + t) + 0.5 * x * (1.0 - t * t) * _GELU_C * (1.0 + 3 * 0.044715 * x * x)
    return g, dg


def _ln_fwd_tile(pre):
    mu = jnp.mean(pre, axis=-1, keepdims=True)
    xc = pre - mu
    var = jnp.mean(xc * xc, axis=-1, keepdims=True)
    rstd = lax.rsqrt(var + LN_EPS)
    return xc * rstd, rstd


def _ln_bwd_tile(dy, xhat, rstd, g):
    dyg = dy * g
    m1 = jnp.mean(dyg, axis=-1, keepdims=True)
    m2 = jnp.mean(dyg * xhat, axis=-1, keepdims=True)
    dpre = rstd * (dyg - m1 - xhat * m2)
    return dpre, jnp.sum(dy * xhat, axis=0, keepdims=True), jnp.sum(dy, axis=0, keepdims=True)


_NT = (((1,), (1,)), ((), ()))
_TN = (((0,), (0,)), ((), ()))


class _Exchange:
    def __init__(self, arrs, gather):
        self.arrs, self.gather, self.n = list(arrs), gather, len(arrs)

    def out_shape(self):
        return [jax.ShapeDtypeStruct(((N_DEV,) + a.shape) if self.gather else a.shape, a.dtype) for a in self.arrs]

    def scratch(self):
        n_remote = self.n * (N_DEV - 1)
        return [pltpu.SemaphoreType.DMA((n_remote,)), pltpu.SemaphoreType.DMA((n_remote,)),
                pltpu.SemaphoreType.DMA((self.n,))]

    def copies(self, ins, outs, sems):
        send_sems, recv_sems, local_sems = sems
        x, y, c = lax.axis_index("x"), lax.axis_index("y"), lax.axis_index("c")
        me = 4 * x + 2 * y + c
        out = []
        for k in range(self.n):
            for d in range(1, N_DEV):
                px = 1 - x if d & 4 else x
                py = 1 - y if d & 2 else y
                pc = 1 - c if d & 1 else c
                src = ins[k] if self.gather else ins[k].at[4 * px + 2 * py + pc]
                sem = k * (N_DEV - 1) + d - 1
                out.append(pltpu.make_async_remote_copy(
                    src_ref=src, dst_ref=outs[k].at[me], send_sem=send_sems.at[sem], recv_sem=recv_sems.at[sem],
                    device_id=(px, py, pc), device_id_type=MESH_T))
            out.append(pltpu.make_async_copy(ins[k] if self.gather else ins[k].at[me], outs[k].at[me],
                                             local_sems.at[k]))
        return out

    def start(self, ins, outs, sems):
        for cp in self.copies(ins, outs, sems):
            cp.start()

    def wait(self, ins, outs, sems):
        for cp in self.copies(ins, outs, sems):
            cp.wait()


def _hosted_call(host, body, *, name, grid, in_specs, out_specs, out_shape, scratch_shapes=(), compiler_params):
    out_specs = list(out_specs) if isinstance(out_specs, (list, tuple)) else [out_specs]
    out_shape = list(out_shape) if isinstance(out_shape, (list, tuple)) else [out_shape]
    if host is None:
        return pl.pallas_call(body, name=name, grid=grid, in_specs=in_specs, out_specs=out_specs,
                              out_shape=out_shape, scratch_shapes=list(scratch_shapes),
                              compiler_params=compiler_params)
    n_in, n_out, n_scr, k = len(in_specs), len(out_shape), len(scratch_shapes), host.n

    def wrapped(*refs):
        ins, h_in = refs[:n_in], refs[n_in:n_in + k]
        outs, h_out = refs[n_in + k:n_in + k + n_out], refs[n_in + k + n_out:n_in + 2 * k + n_out]
        scr, sems = refs[n_in + 2 * k + n_out:n_in + 2 * k + n_out + n_scr], refs[n_in + 2 * k + n_out + n_scr:]
        ids = [pl.program_id(a) for a in range(len(grid))]
        first = functools.reduce(jnp.logical_and, [i == 0 for i in ids])
        last = functools.reduce(jnp.logical_and, [i == g - 1 for i, g in zip(ids, grid)])

        @pl.when(first)
        def _():
            host.start(h_in, h_out, sems)

        body(*ins, *outs, *scr)

        @pl.when(last)
        def _():
            host.wait(h_in, h_out, sems)

    hbm = pl.BlockSpec(memory_space=pl.ANY)
    call = pl.pallas_call(
        wrapped, name=name, grid=grid, in_specs=list(in_specs) + [hbm] * k, out_specs=out_specs + [hbm] * k,
        out_shape=out_shape + host.out_shape(), scratch_shapes=list(scratch_shapes) + host.scratch(),
        compiler_params=compiler_params)
    return lambda *args: call(*args, *host.arrs)


def _exchange(arrs, *, gather, name):
    host = _Exchange(arrs, gather)

    def body(*refs):
        ins, outs, sems = refs[:host.n], refs[host.n:2 * host.n], refs[2 * host.n:]
        host.start(ins, outs, sems)
        host.wait(ins, outs, sems)

    hbm = pl.BlockSpec(memory_space=pl.ANY)
    return pl.pallas_call(
        body, name=name, in_specs=[hbm] * host.n, out_specs=[hbm] * host.n, out_shape=host.out_shape(),
        scratch_shapes=host.scratch(), compiler_params=pltpu.CompilerParams(has_side_effects=True),
    )(*arrs)


def _ffn_fwd(xhat, g_in, b_in, wg, wu, wd, *, tm, name, host=None):
    t = xhat.shape[0]
    nj = N_DEV

    def body(x_ref, g_ref, b_ref, wg_ref, wu_ref, wd_ref, xo_ref, rstd_ref, hg_ref, hu_ref, xb, acc):
        j = pl.program_id(1)

        @pl.when(j == 0)
        def _():
            xb[...] = (x_ref[...] * g_ref[...] + b_ref[...]).astype(bf16)
            acc[...] = jnp.zeros_like(acc)

        hg = jnp.dot(xb[...], wg_ref[...], preferred_element_type=f32)
        hu = jnp.dot(xb[...], wu_ref[...], preferred_element_type=f32)
        hg_ref[...] = hg.astype(bf16)
        hu_ref[...] = hu.astype(bf16)
        a = hg * _sigmoid(hg) * hu
        acc[...] += jnp.dot(a.astype(bf16), wd_ref[...], preferred_element_type=f32)

        @pl.when(j == nj - 1)
        def _():
            x = x_ref[...] * g_ref[...] + b_ref[...]
            xo, rstd = _ln_fwd_tile(ALPHA * x + 0.5 * acc[...])
            xo_ref[...] = xo
            rstd_ref[...] = rstd

    row = pl.BlockSpec((1, D_MODEL), lambda i, j: (0, 0))
    return _hosted_call(
        host, body, name=name, grid=(t // tm, nj),
        in_specs=[pl.BlockSpec((tm, D_MODEL), lambda i, j: (i, 0)), row, row,
                  pl.BlockSpec((None, D_MODEL, FF_TILE), lambda i, j: (j, 0, 0)),
                  pl.BlockSpec((None, D_MODEL, FF_TILE), lambda i, j: (j, 0, 0)),
                  pl.BlockSpec((None, FF_TILE, D_MODEL), lambda i, j: (j, 0, 0))],
        out_specs=[pl.BlockSpec((tm, D_MODEL), lambda i, j: (i, 0)),
                   pl.BlockSpec((tm, 1), lambda i, j: (i, 0)),
                   pl.BlockSpec((tm, FF_TILE), lambda i, j: (i, j)),
                   pl.BlockSpec((tm, FF_TILE), lambda i, j: (i, j))],
        out_shape=[jax.ShapeDtypeStruct((t, D_MODEL), f32), jax.ShapeDtypeStruct((t, 1), f32),
                   jax.ShapeDtypeStruct((t, D_FF), bf16), jax.ShapeDtypeStruct((t, D_FF), bf16)],
        scratch_shapes=[pltpu.VMEM((tm, D_MODEL), bf16), pltpu.VMEM((tm, D_MODEL), f32)],
        compiler_params=_params(("arbitrary", "arbitrary")),
    )(xhat, g_in, b_in, wg, wu, wd)


def _ffn_bwd(dpre, hg, hu, wg, wu, wd, ln_in, *, tm, name, host=None):
    t = dpre.shape[0]
    nj = N_DEV
    with_ln = ln_in is not None

    def body(*refs):
        if with_ln:
            (dp_ref, hg_ref, hu_ref, wg_ref, wu_ref, wd_ref, xh_ref, rs_ref, g_ref,
             dx_ref, gg_ref, gb_ref, dhg_ref, dhu_ref, a_ref, dfb, acc) = refs
        else:
            (dp_ref, hg_ref, hu_ref, wg_ref, wu_ref, wd_ref,
             dx_ref, dhg_ref, dhu_ref, a_ref, dfb, acc) = refs
        i = pl.program_id(0)
        j = pl.program_id(1)

        @pl.when(j == 0)
        def _():
            dfb[...] = (0.5 * dp_ref[...]).astype(bf16)
            acc[...] = jnp.zeros_like(acc)

        da = lax.dot_general(dfb[...], wd_ref[...], _NT, preferred_element_type=f32)
        hgv = hg_ref[...].astype(f32)
        huv = hu_ref[...].astype(f32)
        sg = _sigmoid(hgv)
        silu = hgv * sg
        a_ref[...] = (silu * huv).astype(bf16)
        dhu = (da * silu).astype(bf16)
        dhg = (da * huv * (sg * (1.0 + hgv * (1.0 - sg)))).astype(bf16)
        dhg_ref[...] = dhg
        dhu_ref[...] = dhu
        acc[...] += (lax.dot_general(dhg, wg_ref[...], _NT, preferred_element_type=f32)
                     + lax.dot_general(dhu, wu_ref[...], _NT, preferred_element_type=f32))

        @pl.when(j == nj - 1)
        def _():
            dx = ALPHA * dp_ref[...] + acc[...]
            if with_ln:
                dprev, gg, gb = _ln_bwd_tile(dx, xh_ref[...], rs_ref[...], g_ref[...])
                dx_ref[...] = dprev

                @pl.when(i == 0)
                def _():
                    gg_ref[...] = gg
                    gb_ref[...] = gb

                @pl.when(i > 0)
                def _():
                    gg_ref[...] += gg
                    gb_ref[...] += gb
            else:
                dx_ref[...] = dx

    tok = pl.BlockSpec((tm, D_MODEL), lambda i, j: (i, 0))
    row = pl.BlockSpec((1, D_MODEL), lambda i, j: (0, 0))
    hid = pl.BlockSpec((tm, FF_TILE), lambda i, j: (i, j))
    in_specs = [tok, hid, hid,
                pl.BlockSpec((None, D_MODEL, FF_TILE), lambda i, j: (j, 0, 0)),
                pl.BlockSpec((None, D_MODEL, FF_TILE), lambda i, j: (j, 0, 0)),
                pl.BlockSpec((None, FF_TILE, D_MODEL), lambda i, j: (j, 0, 0))]
    args = [dpre, hg, hu, wg, wu, wd]
    out_specs = [tok]
    out_shape = [jax.ShapeDtypeStruct((t, D_MODEL), f32)]
    if with_ln:
        in_specs += [tok, pl.BlockSpec((tm, 1), lambda i, j: (i, 0)), row]
        args += list(ln_in)
        out_specs += [row, row]
        out_shape += [jax.ShapeDtypeStruct((1, D_MODEL), f32)] * 2
    out_specs += [hid, hid, hid]
    out_shape += [jax.ShapeDtypeStruct((t, D_FF), bf16)] * 3
    return _hosted_call(
        host, body, name=name, grid=(t // tm, nj), in_specs=in_specs, out_specs=out_specs, out_shape=out_shape,
        scratch_shapes=[pltpu.VMEM((tm, D_MODEL), bf16), pltpu.VMEM((tm, D_MODEL), f32)],
        compiler_params=_params(("arbitrary", "arbitrary")),
    )(*args)


def _ffn_bwd_act(dpre, hg, hu, wd, *, tm, name, host=None):
    t = dpre.shape[0]

    def body(dp_ref, hg_ref, hu_ref, wd_ref, dhg_ref, dhu_ref, a_ref, dfb):
        @pl.when(pl.program_id(1) == 0)
        def _():
            dfb[...] = (0.5 * dp_ref[...]).astype(bf16)

        da = lax.dot_general(dfb[...], wd_ref[...], _NT, preferred_element_type=f32)
        hgv = hg_ref[...].astype(f32)
        huv = hu_ref[...].astype(f32)
        sg = _sigmoid(hgv)
        silu = hgv * sg
        a_ref[...] = (silu * huv).astype(bf16)
        dhu_ref[...] = (da * silu).astype(bf16)
        dhg_ref[...] = (da * huv * (sg * (1.0 + hgv * (1.0 - sg)))).astype(bf16)

    hid = pl.BlockSpec((tm, FF_TILE), lambda i, j: (i, j))
    return _hosted_call(
        host, body, name=name, grid=(t // tm, N_DEV),
        in_specs=[pl.BlockSpec((tm, D_MODEL), lambda i, j: (i, 0)), hid, hid,
                  pl.BlockSpec((None, FF_TILE, D_MODEL), lambda i, j: (j, 0, 0))],
        out_specs=[hid, hid, hid], out_shape=[jax.ShapeDtypeStruct((t, D_FF), bf16)] * 3,
        scratch_shapes=[pltpu.VMEM((tm, D_MODEL), bf16)],
        compiler_params=_params(("arbitrary", "arbitrary")),
    )(dpre, hg, hu, wd)


def _ffn_bwd_dx(dpre, dhg, dhu, wg, wu, *, tm, name, host=None):
    t = dpre.shape[0]
    nj = N_DEV

    def body(dp_ref, dhg_ref, dhu_ref, wg_ref, wu_ref, dx_ref, acc):
        j = pl.program_id(1)

        @pl.when(j == 0)
        def _():
            acc[...] = jnp.zeros_like(acc)

        acc[...] += (lax.dot_general(dhg_ref[...], wg_ref[...], _NT, preferred_element_type=f32)
                     + lax.dot_general(dhu_ref[...], wu_ref[...], _NT, preferred_element_type=f32))

        @pl.when(j == nj - 1)
        def _():
            dx_ref[...] = ALPHA * dp_ref[...] + acc[...]

    tok = pl.BlockSpec((tm, D_MODEL), lambda i, j: (i, 0))
    hid = pl.BlockSpec((tm, FF_TILE), lambda i, j: (i, j))
    wspec = pl.BlockSpec((None, D_MODEL, FF_TILE), lambda i, j: (j, 0, 0))
    return _hosted_call(
        host, body, name=name, grid=(t // tm, nj), in_specs=[tok, hid, hid, wspec, wspec],
        out_specs=[tok], out_shape=[jax.ShapeDtypeStruct((t, D_MODEL), f32)],
        scratch_shapes=[pltpu.VMEM((tm, D_MODEL), f32)],
        compiler_params=_params(("arbitrary", "arbitrary")),
    )(dpre, dhg, dhu, wg, wu)


def _mm(a, b, *, mode, out_dtype, tm, tn, tk, name, affine=None, a_cols=None, b_cols=None,
        b_blocked=False, out_blocked=False, out_scale=None):
    if mode == "nn":
        m_full, k_full = a.shape
        m_dim, k_dim = (m_full, a_cols[1]) if a_cols else (m_full, k_full)
    else:
        k_dim, m_full = a.shape
        m_dim = a_cols[1] if a_cols else m_full
    a_off = a_cols[0] if a_cols else 0
    if b_blocked:
        n_dim = b.shape[0] * b.shape[2]
        assert b.shape[2] == tn
    else:
        n_dim = b_cols[1] if b_cols else b.shape[1]
    b_off = b_cols[0] if b_cols else 0
    assert m_dim % tm == 0 and n_dim % tn == 0 and k_dim % tk == 0, (name, m_dim, n_dim, k_dim)
    nk = k_dim // tk

    def body(*refs):
        if affine is not None:
            a_ref, g_ref, s_ref, b_ref, o_ref, acc = refs
        else:
            a_ref, b_ref, o_ref, acc = refs
        k = pl.program_id(2)

        @pl.when(k == 0)
        def _():
            acc[...] = jnp.zeros_like(acc)

        av = a_ref[...]
        if affine is not None:
            av = av * g_ref[...] + s_ref[...]
        av = av.astype(bf16)
        bv = b_ref[...].astype(bf16)
        if mode == "nn":
            acc[...] += jnp.dot(av, bv, preferred_element_type=f32)
        else:
            acc[...] += lax.dot_general(av, bv, _TN, preferred_element_type=f32)

        @pl.when(k == nk - 1)
        def _():
            res = acc[...] if out_scale is None else acc[...] * out_scale
            o_ref[...] = res.astype(out_dtype)

    if mode == "nn":
        a_spec = pl.BlockSpec((tm, tk), lambda i, j, k: (i, k + a_off))
        aff_spec = pl.BlockSpec((1, tk), lambda i, j, k: (0, k + a_off))
    else:
        a_spec = pl.BlockSpec((tk, tm), lambda i, j, k: (k, i + a_off))
        aff_spec = pl.BlockSpec((1, tm), lambda i, j, k: (0, i + a_off))
    if b_blocked:
        b_spec = pl.BlockSpec((None, tk, tn), lambda i, j, k: (j, k, 0))
    else:
        b_spec = pl.BlockSpec((tk, tn), lambda i, j, k: (k, j + b_off))
    if out_blocked:
        o_spec = pl.BlockSpec((None, tm, tn), lambda i, j, k: (j, i, 0))
        o_shape = jax.ShapeDtypeStruct((n_dim // tn, m_dim, tn), out_dtype)
    else:
        o_spec = pl.BlockSpec((tm, tn), lambda i, j, k: (i, j))
        o_shape = jax.ShapeDtypeStruct((m_dim, n_dim), out_dtype)
    in_specs = [a_spec] + ([aff_spec, aff_spec] if affine is not None else []) + [b_spec]
    args = [a] + (list(affine) if affine is not None else []) + [b]
    return pl.pallas_call(
        body, name=name, grid=(m_dim // tm, n_dim // tn, nk), in_specs=in_specs, out_specs=o_spec,
        out_shape=o_shape, scratch_shapes=[pltpu.VMEM((tm, tn), f32)],
        compiler_params=_params(("arbitrary", "arbitrary", "arbitrary")),
    )(*args)


def _mm_tn(a, b, *, out_dtype, tm, mb, tn, nb, tk, name, affine=None, out_blocked=False, out_scale=None, host=None):
    k_dim, m_dim = a.shape
    n_dim = b.shape[1]
    assert m_dim % (mb * tm) == 0 and n_dim % (nb * tn) == 0 and k_dim % tk == 0, (name, m_dim, n_dim, k_dim)
    nk = k_dim // tk

    def body(*refs):
        if affine is not None:
            a_ref, g_ref, s_ref, b_ref, o_ref, acc = refs
        else:
            a_ref, b_ref, o_ref, acc = refs
        k = pl.program_id(2)

        @pl.when(k == 0)
        def _():
            acc[...] = jnp.zeros_like(acc)

        av = a_ref[...]
        if affine is not None:
            av = av * g_ref[...] + s_ref[...]
        av = av.astype(bf16)
        bv = b_ref[...].astype(bf16)
        for im in range(mb):
            a_t = av[:, im * tm:(im + 1) * tm].T
            for jn in range(nb):
                acc[im * nb + jn] += jnp.dot(a_t, bv[:, jn * tn:(jn + 1) * tn], preferred_element_type=f32)

        @pl.when(k == nk - 1)
        def _():
            for im in range(mb):
                for jn in range(nb):
                    res = acc[im * nb + jn]
                    if out_scale is not None:
                        res = res * out_scale
                    if out_blocked:
                        o_ref[jn, im * tm:(im + 1) * tm, :] = res.astype(out_dtype)
                    else:
                        o_ref[im * tm:(im + 1) * tm, jn * tn:(jn + 1) * tn] = res.astype(out_dtype)

    a_spec = pl.BlockSpec((tk, mb * tm), lambda i, j, k: (k, i))
    aff_spec = pl.BlockSpec((1, mb * tm), lambda i, j, k: (0, i))
    b_spec = pl.BlockSpec((tk, nb * tn), lambda i, j, k: (k, j))
    if out_blocked:
        o_spec = pl.BlockSpec((nb, mb * tm, tn), lambda i, j, k: (j, i, 0))
        o_shape = jax.ShapeDtypeStruct((n_dim // tn, m_dim, tn), out_dtype)
    else:
        o_spec = pl.BlockSpec((mb * tm, nb * tn), lambda i, j, k: (i, j))
        o_shape = jax.ShapeDtypeStruct((m_dim, n_dim), out_dtype)
    in_specs = [a_spec] + ([aff_spec, aff_spec] if affine is not None else []) + [b_spec]
    args = [a] + (list(affine) if affine is not None else []) + [b]
    res = _hosted_call(
        host, body, name=name, grid=(m_dim // (mb * tm), n_dim // (nb * tn), nk), in_specs=in_specs,
        out_specs=o_spec, out_shape=o_shape, scratch_shapes=[pltpu.VMEM((mb * nb, tm, tn), f32)],
        compiler_params=_params(("arbitrary", "arbitrary", "arbitrary")),
    )(*args)
    return res[0] if host is None else res


def _mmln(pairs, *, tm, name, resid=None, resid_scale=1.0, epi=None, ln=None, n_out=D_MODEL):
    t = pairs[0][0].shape[0]
    n_pairs = len(pairs)
    n_resid = 0 if resid is None else len(resid) - 1

    def body(*refs):
        pos = 0
        val = None
        for p in range(n_pairs):
            a_ref, b_ref = refs[pos], refs[pos + 1]
            pos += 2
            av = a_ref[...].astype(bf16)
            bv = b_ref[...].astype(bf16)
            if pairs[p][6] == "nn":
                term = jnp.dot(av, bv, preferred_element_type=f32)
            else:
                term = lax.dot_general(av, bv, _NT, preferred_element_type=f32)
            val = term if val is None else val + term
        if resid is not None:
            if resid[0] == "plain":
                r = refs[pos][...]
            else:
                r = refs[pos][...] * refs[pos + 1][...] + refs[pos + 2][...]
            pos += n_resid
            val = val + resid_scale * r
        if epi is None:
            o_ref = refs[pos]
            o_ref[...] = val.astype(o_ref.dtype)
        elif epi == "ln_fwd":
            xo, rstd = _ln_fwd_tile(val)
            refs[pos][...] = xo
            refs[pos + 1][...] = rstd
        else:
            xh_ref, rs_ref, g_ref, dx_ref, gg_ref, gb_ref = refs[pos:pos + 6]
            dprev, gg, gb = _ln_bwd_tile(val, xh_ref[...], rs_ref[...], g_ref[...])
            dx_ref[...] = dprev
            i = pl.program_id(0)

            @pl.when(i == 0)
            def _():
                gg_ref[...] = gg
                gb_ref[...] = gb

            @pl.when(i > 0)
            def _():
                gg_ref[...] += gg
                gb_ref[...] += gb

    in_specs, args = [], []
    for (a, acb, aw, b, bcb, bw, mode) in pairs:
        in_specs.append(pl.BlockSpec((tm, aw), lambda i, acb=acb: (i, acb)))
        args.append(a)
        if mode == "nn":
            in_specs.append(pl.BlockSpec((aw, n_out), lambda i, bcb=bcb: (bcb, 0)))
        else:
            in_specs.append(pl.BlockSpec((n_out, bw), lambda i, bcb=bcb: (0, bcb)))
        args.append(b)
    tok = pl.BlockSpec((tm, n_out), lambda i: (i, 0))
    row = pl.BlockSpec((1, n_out), lambda i: (0, 0))
    col = pl.BlockSpec((tm, 1), lambda i: (i, 0))
    if resid is not None:
        in_specs += [tok] if resid[0] == "plain" else [tok, row, row]
        args += list(resid[1:])
    if epi is None:
        out_specs, out_shape = tok, jax.ShapeDtypeStruct((t, n_out), f32)
    elif epi == "ln_fwd":
        out_specs = [tok, col]
        out_shape = [jax.ShapeDtypeStruct((t, n_out), f32), jax.ShapeDtypeStruct((t, 1), f32)]
    else:
        in_specs += [tok, col, row]
        args += list(ln)
        out_specs = [tok, row, row]
        out_shape = [jax.ShapeDtypeStruct((t, n_out), f32)] + [jax.ShapeDtypeStruct((1, n_out), f32)] * 2
    return pl.pallas_call(
        body, name=name, grid=(t // tm,), in_specs=in_specs, out_specs=out_specs, out_shape=out_shape,
        compiler_params=_params(("arbitrary",)),
    )(*args)


def _loss_bwd(xhat, rstd, g, b, target, *, tm, name):
    t = xhat.shape[0]

    def body(xh_ref, rs_ref, g_ref, b_ref, tg_ref, dx_ref, sq_ref, gg_ref, gb_ref):
        i = pl.program_id(0)
        xh = xh_ref[...]
        diff = xh * g_ref[...] + b_ref[...] - tg_ref[...]
        sq = jnp.sum(diff * diff, axis=0, keepdims=True)
        dprev, gg, gb = _ln_bwd_tile(diff * (1.0 / D_MODEL), xh, rs_ref[...], g_ref[...])
        dx_ref[...] = dprev

        @pl.when(i == 0)
        def _():
            sq_ref[...] = sq
            gg_ref[...] = gg
            gb_ref[...] = gb

        @pl.when(i > 0)
        def _():
            sq_ref[...] += sq
            gg_ref[...] += gg
            gb_ref[...] += gb

    tok = pl.BlockSpec((tm, D_MODEL), lambda i: (i, 0))
    row = pl.BlockSpec((1, D_MODEL), lambda i: (0, 0))
    return pl.pallas_call(
        body, name=name, grid=(t // tm,),
        in_specs=[tok, pl.BlockSpec((tm, 1), lambda i: (i, 0)), row, row, tok],
        out_specs=[tok, row, row, row],
        out_shape=[jax.ShapeDtypeStruct((t, D_MODEL), f32)] + [jax.ShapeDtypeStruct((1, D_MODEL), f32)] * 3,
        compiler_params=_params(("arbitrary",)),
    )(xhat, rstd, g, b, target)


CUM_TILE = 256


def _tri(n, lower):
    r = lax.broadcasted_iota(jnp.int32, (n, n), 0)
    c = lax.broadcasted_iota(jnp.int32, (n, n), 1)
    return jnp.where((r >= c) if lower else (r <= c), 1.0, 0.0).astype(f32)


def _cum_fwd(zfg, bfg, *, name):
    t = zfg.shape[0]

    def body(z_ref, b_ref, o_ref, carry):
        @pl.when(pl.program_id(0) == 0)
        def _():
            carry[...] = jnp.zeros_like(carry)

        ls = -_softplus(-(z_ref[...] + b_ref[...]))
        c = jnp.dot(_tri(CUM_TILE, True), ls, preferred_element_type=f32,
                    precision=lax.Precision.HIGHEST) + carry[...]
        o_ref[...] = c
        carry[...] = c[CUM_TILE - 1:CUM_TILE, :]

    blk = pl.BlockSpec((CUM_TILE, LANES), lambda i: (i, 0))
    return pl.pallas_call(
        body, name=name, grid=(t // CUM_TILE,),
        in_specs=[blk, pl.BlockSpec((1, LANES), lambda i: (0, 0))], out_specs=blk,
        out_shape=jax.ShapeDtypeStruct((t, LANES), f32), scratch_shapes=[pltpu.VMEM((1, LANES), f32)],
        compiler_params=_params(("arbitrary",)),
    )(zfg, bfg)


def _cum_bwd(dcum_q, dcum_k, zfg, bfg, *, name):
    t = zfg.shape[0]
    n = t // CUM_TILE

    def body(d_ref, d2_ref, z_ref, b_ref, o_ref, s_ref, carry):
        i = pl.program_id(0)

        @pl.when(i == 0)
        def _():
            carry[...] = jnp.zeros_like(carry)

        dls = jnp.dot(_tri(CUM_TILE, False), d_ref[...] + d2_ref[...], preferred_element_type=f32,
                      precision=lax.Precision.HIGHEST) + carry[...]
        carry[...] = dls[0:1, :]
        lane = lax.broadcasted_iota(jnp.int32, (CUM_TILE, LANES), 1)
        dfg = jnp.where(lane < HEADS, dls * _sigmoid(-(z_ref[...] + b_ref[...])), 0.0)
        o_ref[...] = dfg
        tot = jnp.sum(dfg, axis=0, keepdims=True)

        @pl.when(i == 0)
        def _():
            s_ref[...] = tot

        @pl.when(i > 0)
        def _():
            s_ref[...] += tot

    blk = pl.BlockSpec((CUM_TILE, LANES), lambda i: (n - 1 - i, 0))
    row = pl.BlockSpec((1, LANES), lambda i: (0, 0))
    return pl.pallas_call(
        body, name=name, grid=(n,), in_specs=[blk, blk, blk, row], out_specs=[blk, row],
        out_shape=[jax.ShapeDtypeStruct((t, LANES), f32), jax.ShapeDtypeStruct((1, LANES), f32)],
        scratch_shapes=[pltpu.VMEM((1, LANES), f32)],
        compiler_params=_params(("arbitrary",)),
    )(dcum_q, dcum_k, zfg, bfg)


ATT_TILE = 512


ATT_ROWS = 32


def _causal_rows(r, transposed):
    rr = lax.broadcasted_iota(jnp.int32, (ATT_ROWS, ATT_TILE), 0) + r * ATT_ROWS
    cc = lax.broadcasted_iota(jnp.int32, (ATT_ROWS, ATT_TILE), 1)
    return (cc >= rr) if transposed else (rr >= cc)


def _causal(i, j, transposed):
    r = lax.broadcasted_iota(jnp.int32, (ATT_TILE, ATT_TILE), 0)
    c = lax.broadcasted_iota(jnp.int32, (ATT_TILE, ATT_TILE), 1)
    if transposed:
        return (c + i * ATT_TILE) >= (r + j * ATT_TILE)
    return (r + i * ATT_TILE) >= (c + j * ATT_TILE)


def _attn_fwd(qkv, cum, cum_t, *, name, host=None):
    t = qkv.shape[0]
    n = t // ATT_TILE
    tq = ATT_TILE

    def body(q_ref, k_ref, v_ref, cq_ref, ck_ref, o_ref, lse_ref, acc, m_s, l_s, c_s, s_s, p_s):
        i = pl.program_id(0)
        j = pl.program_id(1)

        @pl.when(j == 0)
        def _():
            acc[...] = jnp.zeros_like(acc)
            m_s[...] = jnp.full_like(m_s, NEG_BIG)
            l_s[...] = jnp.zeros_like(l_s)

        def block(masked):
            for h in range(HEADS):
                hs = slice(HEAD_D * h, HEAD_D * (h + 1))
                s_s[...] = lax.dot_general(q_ref[:, hs] * ATT_SCALE, k_ref[:, hs], _NT, preferred_element_type=f32)
                ck = ck_ref[h:h + 1, :]

                def rows_chunk(r, carry):
                    rows = pl.ds(pl.multiple_of(r * ATT_ROWS, ATT_ROWS), ATT_ROWS)
                    s = s_s[rows, :] + (cq_ref[rows, h:h + 1] - ck)
                    if masked:
                        s = jnp.where(_causal_rows(r, False), s, NEG_BIG)
                    m_old = m_s[rows, h:h + 1]
                    m_new = jnp.maximum(m_old, jnp.max(s, axis=-1, keepdims=True))
                    corr = jnp.exp(m_old - m_new)
                    p = jnp.exp(s - m_new)
                    l_s[rows, h:h + 1] = corr * l_s[rows, h:h + 1] + jnp.sum(p, axis=-1, keepdims=True)
                    m_s[rows, h:h + 1] = m_new
                    c_s[rows, h:h + 1] = corr
                    p_s[rows, :] = p.astype(bf16)
                    return carry

                lax.fori_loop(0, tq // ATT_ROWS, rows_chunk, 0, unroll=4)
                acc[:, hs] = c_s[:, h:h + 1] * acc[:, hs] + jnp.dot(p_s[...], v_ref[:, hs],
                                                                   preferred_element_type=f32)

        @pl.when(j < i)
        def _():
            block(False)

        @pl.when(j == i)
        def _():
            block(True)
            lse_ref[...] = jnp.zeros_like(lse_ref)
            for h in range(HEADS):
                hs = slice(HEAD_D * h, HEAD_D * (h + 1))
                l = l_s[:, h:h + 1]
                o_ref[:, hs] = acc[:, hs] / l
                lse_ref[:, h:h + 1] = m_s[:, h:h + 1] + jnp.log(l)

    return _hosted_call(
        host, body, name=name, grid=(n, n),
        in_specs=[pl.BlockSpec((tq, FOX_W), lambda i, j: (i, 0)),
                  pl.BlockSpec((tq, FOX_W), lambda i, j: (jnp.minimum(i, j), 1)),
                  pl.BlockSpec((tq, FOX_W), lambda i, j: (jnp.minimum(i, j), 2)),
                  pl.BlockSpec((tq, LANES), lambda i, j: (i, 0)),
                  pl.BlockSpec((HEADS, tq), lambda i, j: (0, jnp.minimum(i, j)))],
        out_specs=[pl.BlockSpec((tq, FOX_W), lambda i, j: (i, 0)), pl.BlockSpec((tq, LANES), lambda i, j: (i, 0))],
        out_shape=[jax.ShapeDtypeStruct((t, FOX_W), f32), jax.ShapeDtypeStruct((t, LANES), f32)],
        scratch_shapes=[pltpu.VMEM((tq, FOX_W), f32), pltpu.VMEM((tq, LANES), f32), pltpu.VMEM((tq, LANES), f32),
                        pltpu.VMEM((tq, LANES), f32), pltpu.VMEM((tq, tq), f32), pltpu.VMEM((tq, tq), bf16)],
        compiler_params=_params(("arbitrary", "arbitrary")),
    )(qkv, qkv, qkv, cum, cum_t)


def _attn_delta(dmix, o, *, tm, name):
    t = o.shape[0]

    def body(do_ref, o_ref, d_ref):
        r = lax.broadcasted_iota(jnp.int32, (FOX_W, LANES), 0)
        c = lax.broadcasted_iota(jnp.int32, (FOX_W, LANES), 1)
        pick = jnp.where(r // HEAD_D == c, 1.0, 0.0).astype(f32)
        d_ref[...] = jnp.dot(do_ref[...] * o_ref[...], pick, preferred_element_type=f32,
                             precision=lax.Precision.HIGHEST)

    blk = pl.BlockSpec((tm, FOX_W), lambda i: (i, 0))
    return pl.pallas_call(
        body, name=name, grid=(t // tm,), in_specs=[blk, blk],
        out_specs=pl.BlockSpec((tm, LANES), lambda i: (i, 0)),
        out_shape=jax.ShapeDtypeStruct((t, LANES), f32), compiler_params=_params(("arbitrary",)),
    )(dmix, o)


def _attn_dq(qkv, dmix, cum, cum_t, lse, delta, *, name, host=None):
    t = qkv.shape[0]
    n = t // ATT_TILE
    tq = ATT_TILE

    def body(q_ref, k_ref, v_ref, do_ref, cq_ref, ck_ref, lse_ref, dl_ref, dq_ref, dc_ref, acc, dc_acc):
        i = pl.program_id(0)
        j = pl.program_id(1)

        @pl.when(j == 0)
        def _():
            acc[...] = jnp.zeros_like(acc)
            dc_acc[...] = jnp.zeros_like(dc_acc)

        def block(masked):
            mask = _causal(i, j, False) if masked else None
            for h in range(HEADS):
                hs = slice(HEAD_D * h, HEAD_D * (h + 1))
                kh = k_ref[:, hs]
                s = lax.dot_general(q_ref[:, hs] * ATT_SCALE, kh, _NT, preferred_element_type=f32)
                s = s + cq_ref[:, h:h + 1] - ck_ref[h:h + 1, :]
                if masked:
                    s = jnp.where(mask, s, NEG_BIG)
                p = jnp.exp(s - lse_ref[:, h:h + 1])
                dp = lax.dot_general(do_ref[:, hs].astype(bf16), v_ref[:, hs], _NT, preferred_element_type=f32)
                ds = p * (dp - dl_ref[:, h:h + 1])
                acc[:, hs] += jnp.dot(ds.astype(bf16), kh, preferred_element_type=f32)
                dc_acc[:, h:h + 1] += jnp.sum(ds, axis=-1, keepdims=True)

        @pl.when(j < i)
        def _():
            block(False)

        @pl.when(j == i)
        def _():
            block(True)
            dq_ref[...] = (acc[...] * ATT_SCALE).astype(bf16)
            dc_ref[...] = dc_acc[...]

    col = pl.BlockSpec((tq, LANES), lambda i, j: (i, 0))
    return _hosted_call(
        host, body, name=name, grid=(n, n),
        in_specs=[pl.BlockSpec((tq, FOX_W), lambda i, j: (i, 0)),
                  pl.BlockSpec((tq, FOX_W), lambda i, j: (jnp.minimum(i, j), 1)),
                  pl.BlockSpec((tq, FOX_W), lambda i, j: (jnp.minimum(i, j), 2)),
                  pl.BlockSpec((tq, FOX_W), lambda i, j: (i, 0)),
                  col, pl.BlockSpec((HEADS, tq), lambda i, j: (0, jnp.minimum(i, j))), col, col],
        out_specs=[pl.BlockSpec((tq, FOX_W), lambda i, j: (i, 0)), col],
        out_shape=[jax.ShapeDtypeStruct((t, FOX_W), bf16), jax.ShapeDtypeStruct((t, LANES), f32)],
        scratch_shapes=[pltpu.VMEM((tq, FOX_W), f32), pltpu.VMEM((tq, LANES), f32)],
        compiler_params=_params(("arbitrary", "arbitrary")),
    )(qkv, qkv, qkv, dmix, cum, cum_t, lse, delta)


def _attn_dkv(qkv, dmix, cum, cum_t, lse_t, delta_t, *, name):
    t = qkv.shape[0]
    n = t // ATT_TILE
    tk = ATT_TILE

    def body(q_ref, k_ref, v_ref, do_ref, cq_ref, ck_ref, lse_ref, dl_ref, dk_ref, dv_ref, dc_ref, dk_acc, dv_acc, dc_acc):
        j = pl.program_id(0)
        i = pl.program_id(1)

        @pl.when(i == 0)
        def _():
            dk_acc[...] = jnp.zeros_like(dk_acc)
            dv_acc[...] = jnp.zeros_like(dv_acc)
            dc_acc[...] = jnp.zeros_like(dc_acc)

        def block(masked):
            mask = _causal(i, j, True) if masked else None
            for h in range(HEADS):
                hs = slice(HEAD_D * h, HEAD_D * (h + 1))
                qh = q_ref[:, hs]
                doh = do_ref[:, hs].astype(bf16)
                s_t = lax.dot_general(k_ref[:, hs] * ATT_SCALE, qh, _NT, preferred_element_type=f32)
                s_t = s_t + cq_ref[h:h + 1, :] - ck_ref[:, h:h + 1]
                if masked:
                    s_t = jnp.where(mask, s_t, NEG_BIG)
                p_t = jnp.exp(s_t - lse_ref[h:h + 1, :])
                dv_acc[:, hs] += jnp.dot(p_t.astype(bf16), doh, preferred_element_type=f32)
                dp_t = lax.dot_general(v_ref[:, hs], doh, _NT, preferred_element_type=f32)
                ds_t = p_t * (dp_t - dl_ref[h:h + 1, :])
                dk_acc[:, hs] += jnp.dot(ds_t.astype(bf16), qh, preferred_element_type=f32)
                dc_acc[:, h:h + 1] -= jnp.sum(ds_t, axis=-1, keepdims=True)

        @pl.when(i > j)
        def _():
            block(False)

        @pl.when(i == j)
        def _():
            block(True)

        @pl.when(i == n - 1)
        def _():
            dk_ref[...] = (dk_acc[...] * ATT_SCALE).astype(bf16)
            dv_ref[...] = dv_acc[...].astype(bf16)
            dc_ref[...] = dc_acc[...]

    rowq = pl.BlockSpec((HEADS, tk), lambda j, i: (0, jnp.maximum(i, j)))
    return pl.pallas_call(
        body, name=name, grid=(n, n),
        in_specs=[pl.BlockSpec((tk, FOX_W), lambda j, i: (jnp.maximum(i, j), 0)),
                  pl.BlockSpec((tk, FOX_W), lambda j, i: (j, 1)),
                  pl.BlockSpec((tk, FOX_W), lambda j, i: (j, 2)),
                  pl.BlockSpec((tk, FOX_W), lambda j, i: (jnp.maximum(i, j), 0)),
                  rowq, pl.BlockSpec((tk, LANES), lambda j, i: (j, 0)), rowq, rowq],
        out_specs=[pl.BlockSpec((tk, FOX_W), lambda j, i: (j, 0)), pl.BlockSpec((tk, FOX_W), lambda j, i: (j, 0)),
                   pl.BlockSpec((tk, LANES), lambda j, i: (j, 0))],
        out_shape=[jax.ShapeDtypeStruct((t, FOX_W), bf16), jax.ShapeDtypeStruct((t, FOX_W), bf16),
                   jax.ShapeDtypeStruct((t, LANES), f32)],
        scratch_shapes=[pltpu.VMEM((tk, FOX_W), f32), pltpu.VMEM((tk, FOX_W), f32), pltpu.VMEM((tk, LANES), f32)],
        compiler_params=_params(("arbitrary", "arbitrary")),
    )(qkv, qkv, qkv, dmix, cum_t, cum, lse_t, delta_t)


ATT_W = HEADS * LANES


def _data_lane(h):
    return HEAD_D * (h % 2)


def _extra_lane(h):
    return HEAD_D - _data_lane(h)


def _split3(x):
    hi = x.astype(bf16)
    rest = x - hi.astype(f32)
    mid = rest.astype(bf16)
    lo = (rest - mid.astype(f32)).astype(bf16)
    return hi, mid, lo


def _augment(pair, h, first, second, fill=0.0):
    rows = pair.shape[0]
    lane = lax.broadcasted_iota(jnp.int32, (rows, LANES), 1)
    base = _extra_lane(h)
    own = (lane < HEAD_D) if h % 2 == 0 else (lane >= HEAD_D)
    out = jnp.where(own, pair, jnp.full((rows, LANES), fill, bf16))
    for off, src in ((0, first), (3, second)):
        for q in range(3):
            val = src[q] if isinstance(src, tuple) else jnp.full((rows, 1), src, bf16)
            out = jnp.where(lane == base + off + q, val, out)
    return out


def _attn_prep_fwd(qkv, cum, *, tm, name):
    t = qkv.shape[0]

    def body(q_ref, k_ref, v_ref, c_ref, qa_ref, ka_ref, va_ref):
        for h in range(HEADS):
            pair = slice(LANES * (h // 2), LANES * (h // 2 + 1))
            hs = slice(LANES * h, LANES * (h + 1))
            c3 = _split3(c_ref[:, h:h + 1])
            qa_ref[:, hs] = _augment(q_ref[:, pair] * ATT_SCALE, h, c3, 1.0)
            ka_ref[:, hs] = _augment(k_ref[:, pair], h, 1.0, tuple(-p for p in c3))
            va_ref[:, hs] = _augment(v_ref[:, pair], h, 1.0, 1.0, fill=1.0)

    wide = pl.BlockSpec((tm, ATT_W), lambda i: (i, 0))
    out = jax.ShapeDtypeStruct((t, ATT_W), bf16)
    return pl.pallas_call(
        body, name=name, grid=(t // tm,),
        in_specs=[pl.BlockSpec((tm, FOX_W), lambda i: (i, 0)), pl.BlockSpec((tm, FOX_W), lambda i: (i, 1)),
                  pl.BlockSpec((tm, FOX_W), lambda i: (i, 2)), pl.BlockSpec((tm, LANES), lambda i: (i, 0))],
        out_specs=[wide] * 3, out_shape=[out] * 3, compiler_params=_params(("arbitrary",)),
    )(qkv, qkv, qkv, cum)


def _attn_prep_bwd(qkv, cum, lse, dmix, o, *, tm, name):
    t = qkv.shape[0]

    def body(q_ref, c_ref, l_ref, do_ref, o_ref, qa_ref, da_ref):
        for h in range(HEADS):
            pair = slice(LANES * (h // 2), LANES * (h // 2 + 1))
            src = slice(HEAD_D * h, HEAD_D * (h + 1))
            hs = slice(LANES * h, LANES * (h + 1))
            delta = jnp.sum(do_ref[:, src] * o_ref[:, src], axis=-1, keepdims=True)
            qa_ref[:, hs] = _augment(q_ref[:, pair] * ATT_SCALE, h,
                                     _split3(c_ref[:, h:h + 1] - l_ref[:, h:h + 1]), 1.0)
            da_ref[:, hs] = _augment(do_ref[:, pair].astype(bf16), h, tuple(-p for p in _split3(delta)), 0.0)

    wide = pl.BlockSpec((tm, ATT_W), lambda i: (i, 0))
    half = pl.BlockSpec((tm, FOX_W), lambda i: (i, 0))
    col = pl.BlockSpec((tm, LANES), lambda i: (i, 0))
    out = jax.ShapeDtypeStruct((t, ATT_W), bf16)
    return pl.pallas_call(
        body, name=name, grid=(t // tm,), in_specs=[half, col, col, half, half],
        out_specs=[wide] * 2, out_shape=[out] * 2, compiler_params=_params(("arbitrary",)),
    )(qkv, cum, lse, dmix, o)


def _attn_fwd2(q_aug, k_aug, v_aug, *, name, host=None):
    t = q_aug.shape[0]
    n = t // ATT_TILE
    tq = ATT_TILE

    def body(q_ref, k_ref, v_ref, o_ref, lse_ref, acc, m_s):
        i = pl.program_id(0)
        j = pl.program_id(1)

        @pl.when(j == 0)
        def _():
            acc[...] = jnp.zeros_like(acc)
            m_s[...] = jnp.full_like(m_s, NEG_BIG)

        def block(masked):
            mask = _causal(i, j, False) if masked else None
            for h in range(HEADS):
                hs = slice(LANES * h, LANES * (h + 1))
                s = lax.dot_general(q_ref[:, hs], k_ref[:, hs], _NT, preferred_element_type=f32)
                if masked:
                    s = jnp.where(mask, s, NEG_BIG)
                m_old = m_s[:, h:h + 1]
                m_new = jnp.maximum(m_old, jnp.max(s, axis=-1, keepdims=True))
                p = jnp.exp(s - m_new).astype(bf16)
                acc[h] = jnp.exp(m_old - m_new) * acc[h] + jnp.dot(p, v_ref[:, hs], preferred_element_type=f32)
                m_s[:, h:h + 1] = m_new

        @pl.when(j < i)
        def _():
            block(False)

        @pl.when(j == i)
        def _():
            block(True)
            lse_ref[...] = jnp.zeros_like(lse_ref)
            for h in range(HEADS):
                a = acc[h]
                l = a[:, _extra_lane(h):_extra_lane(h) + 1]
                o_ref[:, HEAD_D * h:HEAD_D * (h + 1)] = a[:, _data_lane(h):_data_lane(h) + HEAD_D] / l
                lse_ref[:, h:h + 1] = m_s[:, h:h + 1] + jnp.log(l)

    kv = pl.BlockSpec((tq, ATT_W), lambda i, j: (jnp.minimum(i, j), 0))
    return _hosted_call(
        host, body, name=name, grid=(n, n),
        in_specs=[pl.BlockSpec((tq, ATT_W), lambda i, j: (i, 0)), kv, kv],
        out_specs=[pl.BlockSpec((tq, FOX_W), lambda i, j: (i, 0)), pl.BlockSpec((tq, LANES), lambda i, j: (i, 0))],
        out_shape=[jax.ShapeDtypeStruct((t, FOX_W), f32), jax.ShapeDtypeStruct((t, LANES), f32)],
        scratch_shapes=[pltpu.VMEM((HEADS, tq, LANES), f32), pltpu.VMEM((tq, LANES), f32)],
        compiler_params=_params(("arbitrary", "arbitrary")),
    )(q_aug, k_aug, v_aug)


def _attn_dq2(qb_aug, k_aug, v_aug, do_aug, *, name, host=None):
    t = qb_aug.shape[0]
    n = t // ATT_TILE
    tq = ATT_TILE

    def body(q_ref, k_ref, v_ref, do_ref, dq_ref, dc_ref, acc):
        i = pl.program_id(0)
        j = pl.program_id(1)

        @pl.when(j == 0)
        def _():
            acc[...] = jnp.zeros_like(acc)

        def block(masked):
            mask = _causal(i, j, False) if masked else None
            for h in range(HEADS):
                hs = slice(LANES * h, LANES * (h + 1))
                kh = k_ref[:, hs]
                s = lax.dot_general(q_ref[:, hs], kh, _NT, preferred_element_type=f32)
                if masked:
                    s = jnp.where(mask, s, NEG_BIG)
                dp = lax.dot_general(do_ref[:, hs], v_ref[:, hs], _NT, preferred_element_type=f32)
                ds = (jnp.exp(s) * dp).astype(bf16)
                acc[h] += jnp.dot(ds, kh, preferred_element_type=f32)

        @pl.when(j < i)
        def _():
            block(False)

        @pl.when(j == i)
        def _():
            block(True)
            dc_ref[...] = jnp.zeros_like(dc_ref)
            for h in range(HEADS):
                a = acc[h]
                dq_ref[:, HEAD_D * h:HEAD_D * (h + 1)] = (
                    a[:, _data_lane(h):_data_lane(h) + HEAD_D] * ATT_SCALE).astype(bf16)
                dc_ref[:, h:h + 1] = a[:, _extra_lane(h):_extra_lane(h) + 1]

    own = pl.BlockSpec((tq, ATT_W), lambda i, j: (i, 0))
    kv = pl.BlockSpec((tq, ATT_W), lambda i, j: (jnp.minimum(i, j), 0))
    return _hosted_call(
        host, body, name=name, grid=(n, n), in_specs=[own, kv, kv, own],
        out_specs=[pl.BlockSpec((tq, FOX_W), lambda i, j: (i, 0)), pl.BlockSpec((tq, LANES), lambda i, j: (i, 0))],
        out_shape=[jax.ShapeDtypeStruct((t, FOX_W), bf16), jax.ShapeDtypeStruct((t, LANES), f32)],
        scratch_shapes=[pltpu.VMEM((HEADS, tq, LANES), f32)],
        compiler_params=_params(("arbitrary", "arbitrary")),
    )(qb_aug, k_aug, v_aug, do_aug)


def _attn_dkv2(qb_aug, k_aug, v_aug, do_aug, *, name, host=None):
    t = qb_aug.shape[0]
    n = t // ATT_TILE
    tk = ATT_TILE

    def body(q_ref, k_ref, v_ref, do_ref, dk_ref, dv_ref, dc_ref, dk_acc, dv_acc):
        j = pl.program_id(0)
        i = pl.program_id(1)

        @pl.when(i == 0)
        def _():
            dk_acc[...] = jnp.zeros_like(dk_acc)
            dv_acc[...] = jnp.zeros_like(dv_acc)

        def block(masked):
            mask = _causal(i, j, True) if masked else None
            for h in range(HEADS):
                hs = slice(LANES * h, LANES * (h + 1))
                qh = q_ref[:, hs]
                doh = do_ref[:, hs]
                s_t = lax.dot_general(k_ref[:, hs], qh, _NT, preferred_element_type=f32)
                if masked:
                    s_t = jnp.where(mask, s_t, NEG_BIG)
                p_t = jnp.exp(s_t)
                dv_acc[h] += jnp.dot(p_t.astype(bf16), doh, preferred_element_type=f32)
                dp_t = lax.dot_general(v_ref[:, hs], doh, _NT, preferred_element_type=f32)
                dk_acc[h] += jnp.dot((p_t * dp_t).astype(bf16), qh, preferred_element_type=f32)

        @pl.when(i > j)
        def _():
            block(False)

        @pl.when(i == j)
        def _():
            block(True)

        @pl.when(i == n - 1)
        def _():
            dc_ref[...] = jnp.zeros_like(dc_ref)
            for h in range(HEADS):
                a = dk_acc[h]
                cols = slice(_data_lane(h), _data_lane(h) + HEAD_D)
                dk_ref[:, HEAD_D * h:HEAD_D * (h + 1)] = a[:, cols].astype(bf16)
                dv_ref[:, HEAD_D * h:HEAD_D * (h + 1)] = dv_acc[h][:, cols].astype(bf16)
                dc_ref[:, h:h + 1] = -a[:, _extra_lane(h) + 3:_extra_lane(h) + 4]

    own = pl.BlockSpec((tk, ATT_W), lambda j, i: (j, 0))
    qs = pl.BlockSpec((tk, ATT_W), lambda j, i: (jnp.maximum(i, j), 0))
    half = pl.BlockSpec((tk, FOX_W), lambda j, i: (j, 0))
    return _hosted_call(
        host, body, name=name, grid=(n, n), in_specs=[qs, own, own, qs],
        out_specs=[half, half, pl.BlockSpec((tk, LANES), lambda j, i: (j, 0))],
        out_shape=[jax.ShapeDtypeStruct((t, FOX_W), bf16), jax.ShapeDtypeStruct((t, FOX_W), bf16),
                   jax.ShapeDtypeStruct((t, LANES), f32)],
        scratch_shapes=[pltpu.VMEM((HEADS, tk, LANES), f32), pltpu.VMEM((HEADS, tk, LANES), f32)],
        compiler_params=_params(("arbitrary", "arbitrary")),
    )(qb_aug, k_aug, v_aug, do_aug)


LRU_CHUNK = 64
SUB = 8


def _row_ids(n):
    return lax.broadcasted_iota(jnp.int32, (n, LANES), 0)


def _shift_rows_down(ext, s):
    return pltpu.roll(ext, s, axis=0)[SUB:, :]


def _shift_rows_up(ext, s, n):
    return pltpu.roll(ext, ext.shape[0] - s, axis=0)[:n, :]


def _lru_gates(u, wa_ref, ba_ref, wx_ref, bx_ref, sp):
    ub = u.astype(bf16)
    r = _sigmoid(jnp.dot(ub, wa_ref[...], preferred_element_type=f32) + ba_ref[...])
    gi = _sigmoid(jnp.dot(ub, wx_ref[...], preferred_element_type=f32) + bx_ref[...])
    log_a = -LRU_C * r * sp
    a = jnp.exp(log_a)
    s = jnp.sqrt(_one_minus_exp(2.0 * log_a))
    return r, gi, a, s


def _conv_window(lx_ref, r0, ci):
    cur = lx_ref[pl.ds(r0, LRU_CHUNK), :]
    p0 = pl.multiple_of(jnp.maximum(r0 - SUB, 0), SUB)
    prev = jnp.where(ci > 0, lx_ref[pl.ds(p0, SUB), :], 0.0)
    return cur, jnp.concatenate([prev, cur], axis=0)


def _lru_fwd(zl, conv_w, conv_b, wa, ba, wx, bx, lam, *, name):
    t = zl.shape[0]
    n_chunk = t // LRU_CHUNK

    def body(lx_ref, lg_ref, cw_ref, cb_ref, wa_ref, ba_ref, wx_ref, bx_ref, lam_ref, u_ref, h_ref, y_ref):
        sp = _softplus(-lam_ref[...])
        rows = _row_ids(SUB)

        def chunk(ci, hc):
            r0 = pl.multiple_of(ci * LRU_CHUNK, LRU_CHUNK)
            cur, ext = _conv_window(lx_ref, r0, ci)
            u = cb_ref[...] + cw_ref[3:4, :] * cur
            for k in range(3):
                u = u + cw_ref[k:k + 1, :] * _shift_rows_down(ext, 3 - k)
            r, gi, a, s = _lru_gates(u, wa_ref, ba_ref, wx_ref, bx_ref, sp)
            b = s * (gi * u)
            tiles = []
            for q in range(LRU_CHUNK // SUB):
                ta = a[SUB * q:SUB * (q + 1), :]
                tb = b[SUB * q:SUB * (q + 1), :]
                for d in (1, 2, 4):
                    a_sh = jnp.where(rows >= d, pltpu.roll(ta, d, axis=0), 1.0)
                    b_sh = jnp.where(rows >= d, pltpu.roll(tb, d, axis=0), 0.0)
                    tb = ta * b_sh + tb
                    ta = ta * a_sh
                hq = tb + ta * hc
                hc = hq[SUB - 1:SUB, :]
                tiles.append(hq)
            h = jnp.concatenate(tiles, axis=0)
            u_ref[pl.ds(r0, LRU_CHUNK), :] = u
            h_ref[pl.ds(r0, LRU_CHUNK), :] = h
            gel, _ = _gelu_and_grad(lg_ref[pl.ds(r0, LRU_CHUNK), :])
            y_ref[pl.ds(r0, LRU_CHUNK), :] = gel * h
            return hc

        lax.fori_loop(0, n_chunk, chunk, jnp.zeros((1, LANES), f32))

    seq = lambda cb: pl.BlockSpec((t, LANES), lambda c, cb=cb: (0, c + cb))
    rowc = pl.BlockSpec((1, LANES), lambda c: (0, c))
    diag = pl.BlockSpec((LANES, LANES), lambda c: (c, c))
    out = jax.ShapeDtypeStruct((t, LRU_W), f32)
    return pl.pallas_call(
        body, name=name, grid=(LRU_W // LANES,),
        in_specs=[seq(0), seq(4), pl.BlockSpec((4, LANES), lambda c: (0, c)), rowc, diag, rowc, diag, rowc, rowc],
        out_specs=[seq(0)] * 3, out_shape=[out] * 3,
        compiler_params=_params(("arbitrary",)),
    )(zl, zl, conv_w, conv_b, wa, ba, wx, bx, lam)


def _lru_bwd(dmix, zl, u_all, h_all, conv_w, wa, ba, wx, bx, lam, *, name, host=None):
    t = zl.shape[0]
    n_chunk = t // LRU_CHUNK

    def body(dy_ref, lx_ref, lg_ref, u_ref, h_ref, cw_ref, wa_ref, ba_ref, wx_ref, bx_ref, lam_ref,
             dlx_ref, dlg_ref, dcw_ref, dcb_ref, dba_ref, dbx_ref, dlam_ref, dwa_ref, dwx_ref, dpr_s, dpx_s):
        lam_v = lam_ref[...]
        sp = _softplus(-lam_v)
        rows = _row_ids(SUB)
        rows_c = _row_ids(LRU_CHUNK)
        zero_row = jnp.zeros((1, LANES), f32)

        def chunk(step, carry):
            dh_c, a_next0, du_next, dsp, dba, dbx, dcb, dw0, dw1, dw2, dw3 = carry
            ci = n_chunk - 1 - step
            r0 = pl.multiple_of(ci * LRU_CHUNK, LRU_CHUNK)
            sl = pl.ds(r0, LRU_CHUNK)
            u = u_ref[sl, :]
            r, gi, a, s = _lru_gates(u, wa_ref, ba_ref, wx_ref, bx_ref, sp)
            h = h_ref[sl, :]
            p0 = pl.multiple_of(jnp.maximum(r0 - SUB, 0), SUB)
            h_before = jnp.where(ci > 0, h_ref[pl.ds(p0, SUB), :], 0.0)[SUB - 1:SUB, :]
            h_prev = jnp.where(rows_c == 0, h_before, pltpu.roll(h, 1, axis=0))
            gel, dgel = _gelu_and_grad(lg_ref[sl, :])
            dy = dy_ref[sl, :]
            dlg_ref[sl, :] = (dy * h * dgel).astype(bf16)
            g_in = dy * gel
            a_next = jnp.where(rows_c == LRU_CHUNK - 1, a_next0, pltpu.roll(a, LRU_CHUNK - 1, axis=0))
            tiles = [None] * (LRU_CHUNK // SUB)
            for q in reversed(range(LRU_CHUNK // SUB)):
                ta = a_next[SUB * q:SUB * (q + 1), :]
                tb = g_in[SUB * q:SUB * (q + 1), :]
                for d in (1, 2, 4):
                    a_sh = jnp.where(rows < SUB - d, pltpu.roll(ta, SUB - d, axis=0), 1.0)
                    b_sh = jnp.where(rows < SUB - d, pltpu.roll(tb, SUB - d, axis=0), 0.0)
                    tb = ta * b_sh + tb
                    ta = ta * a_sh
                dhq = tb + ta * dh_c
                dh_c = dhq[0:1, :]
                tiles[q] = dhq
            dh = jnp.concatenate(tiles, axis=0)
            da = dh * h_prev
            ds = dh * gi * u
            dgi = dh * s * u
            du = dh * s * gi
            dlog_a = da * a - ds * (a * a) / s
            dr = dlog_a * (-LRU_C * sp)
            dsp = dsp + jnp.sum(dlog_a * (-LRU_C * r), axis=0, keepdims=True)
            dpr = dr * r * (1.0 - r)
            dpx = dgi * gi * (1.0 - gi)
            dprb = dpr.astype(bf16)
            dpxb = dpx.astype(bf16)
            dpr_s[sl, :] = dprb
            dpx_s[sl, :] = dpxb
            du = du + (lax.dot_general(dprb, wa_ref[...], _NT, preferred_element_type=f32)
                       + lax.dot_general(dpxb, wx_ref[...], _NT, preferred_element_type=f32))
            dba = dba + jnp.sum(dpr, axis=0, keepdims=True)
            dbx = dbx + jnp.sum(dpx, axis=0, keepdims=True)
            dcb = dcb + jnp.sum(du, axis=0, keepdims=True)
            du_ext = jnp.concatenate([du, du_next], axis=0)
            dlx = cw_ref[3:4, :] * du
            for k in range(3):
                dlx = dlx + cw_ref[k:k + 1, :] * _shift_rows_up(du_ext, 3 - k, LRU_CHUNK)
            dlx_ref[sl, :] = dlx.astype(bf16)
            cur, ext = _conv_window(lx_ref, r0, ci)
            dws = [dw0, dw1, dw2, dw3 + jnp.sum(du * cur, axis=0, keepdims=True)]
            for k in range(3):
                dws[k] = dws[k] + jnp.sum(du * _shift_rows_down(ext, 3 - k), axis=0, keepdims=True)
            return (dh_c, a[0:1, :], du[0:SUB, :], dsp, dba, dbx, dcb, dws[0], dws[1], dws[2], dws[3])

        init = (zero_row, zero_row, jnp.zeros((SUB, LANES), f32)) + (zero_row,) * 8
        out = lax.fori_loop(0, n_chunk, chunk, init)
        _, _, _, dsp, dba, dbx, dcb, dw0, dw1, dw2, dw3 = out
        dlam_ref[...] = dsp * (-_sigmoid(-lam_v))
        dba_ref[...] = dba
        dbx_ref[...] = dbx
        dcb_ref[...] = dcb
        dcw_ref[...] = jnp.concatenate([dw0, dw1, dw2, dw3], axis=0)
        ub = u_ref[...].astype(bf16)
        dwa_ref[...] = lax.dot_general(ub, dpr_s[...], _TN, preferred_element_type=f32)
        dwx_ref[...] = lax.dot_general(ub, dpx_s[...], _TN, preferred_element_type=f32)

    seq = lambda cb: pl.BlockSpec((t, LANES), lambda c, cb=cb: (0, c + cb))
    rowc = pl.BlockSpec((1, LANES), lambda c: (0, c))
    diag = pl.BlockSpec((LANES, LANES), lambda c: (c, c))
    gate_out = pl.BlockSpec((None, LANES, LANES), lambda c: (c, 0, 0))
    row_shape = jax.ShapeDtypeStruct((1, LRU_W), f32)
    return _hosted_call(
        host, body, name=name, grid=(LRU_W // LANES,),
        in_specs=[seq(4), seq(0), seq(4), seq(0), seq(0), pl.BlockSpec((4, LANES), lambda c: (0, c)),
                  diag, rowc, diag, rowc, rowc],
        out_specs=[seq(0), seq(0), pl.BlockSpec((4, LANES), lambda c: (0, c)), rowc, rowc, rowc, rowc,
                   gate_out, gate_out],
        out_shape=[jax.ShapeDtypeStruct((t, LRU_W), bf16)] * 2
        + [jax.ShapeDtypeStruct((4, LRU_W), f32)] + [row_shape] * 4
        + [jax.ShapeDtypeStruct((LRU_W // LANES, LANES, LANES), f32)] * 2,
        scratch_shapes=[pltpu.VMEM((t, LANES), bf16), pltpu.VMEM((t, LANES), bf16)],
        compiler_params=_params(("arbitrary",)),
    )(dmix, zl, zl, u_all, h_all, conv_w, wa, ba, wx, bx, lam)


def _block_diag(w):
    eye = jnp.eye(HEADS, dtype=w.dtype)
    return jnp.einsum("hij,hk->hikj", w, eye).reshape(LRU_W, LRU_W)


def _diag_blocks(dw):
    top = dw[:, :HEAD_D, :HEAD_D]
    bot = dw[:, HEAD_D:, HEAD_D:]
    return jnp.stack([top, bot], axis=1).reshape(HEADS, HEAD_D, HEAD_D)


def _local_step(x, target, sent, small, *, tm=512):
    t = x.shape[0]
    ones = jnp.ones((1, D_MODEL), f32)
    zeros = jnp.zeros((1, D_MODEL), f32)
    ln1 = (small["ln1_g"], small["ln1_b"])
    ln2 = (small["ln2_g"], small["ln2_b"])
    ln3 = (small["ln3_g"], small["ln3_b"])

    wg1, wu1, wd1 = _exchange([sent["ffn1_w_gate"], sent["ffn1_w_up"], sent["ffn1_w_down"]], gather=True,
                              name="gather_ffn1")
    xh1, rs1, hg1, hu1, w_in_g, w_out_g, conv_w_g = _ffn_fwd(
        x, ones, zeros, wg1, wu1, wd1, tm=tm, name="ffn1_fwd",
        host=_Exchange([sent["w_in"], sent["w_out"], sent["conv_w"]], gather=True))
    w_in = jnp.pad(w_in_g.transpose(1, 0, 2).reshape(D_MODEL, IN_COLS), ((0, 0), (0, 21 * LANES - IN_COLS)))
    w_out = w_out_g.reshape(D_MODEL, D_MODEL)
    conv_w = conv_w_g.transpose(1, 0, 2).reshape(4, LRU_W)
    qkv = _mm(xh1, w_in, mode="nn", out_dtype=bf16, tm=tm, tn=512, tk=D_MODEL, name="qkv_fwd",
              affine=ln1, b_cols=(0, 1536))
    zl = _mm(xh1, w_in, mode="nn", out_dtype=f32, tm=tm, tn=512, tk=D_MODEL, name="zl_fwd",
             affine=ln1, b_cols=(3, 1024))
    zfg = _mm(xh1, w_in, mode="nn", out_dtype=f32, tm=tm, tn=LANES, tk=D_MODEL, name="zfg_fwd",
              affine=ln1, b_cols=(20, LANES))
    bfg = jnp.pad(small["b_forget"], ((0, 0), (0, LANES - HEADS)))
    cum = _cum_fwd(zfg, bfg, name="cum_fwd")
    q_aug, k_aug, v_aug = _attn_prep_fwd(qkv, cum, tm=tm, name="attn_prep_fwd")
    o, lse, wg2, wu2, wd2 = _attn_fwd2(
        q_aug, k_aug, v_aug, name="attn_fwd",
        host=_Exchange([sent["ffn2_w_gate"], sent["ffn2_w_up"], sent["ffn2_w_down"]], gather=True))
    wa_bd = _block_diag(small["rg_wa"]).astype(bf16)
    wx_bd = _block_diag(small["rg_wx"]).astype(bf16)
    ba = small["rg_ba"].reshape(1, LRU_W)
    bx = small["rg_bx"].reshape(1, LRU_W)
    u, h, lru = _lru_fwd(zl, conv_w, small["conv_b"], wa_bd, ba, wx_bd, bx, small["lru_lambda"],
                         name="lru_fwd")
    xh2, rs2 = _mmln([(o, 0, FOX_W, w_out, 0, D_MODEL, "nn"), (lru, 0, LRU_W, w_out, 1, D_MODEL, "nn")],
                     tm=tm, name="mix_fwd", resid=("affine", xh1) + ln1, resid_scale=ALPHA, epi="ln_fwd")
    xh3, rs3, hg2, hu2 = _ffn_fwd(xh2, ln2[0], ln2[1], wg2, wu2, wd2, tm=tm, name="ffn2_fwd")

    dpre3, sq_rows, g_ln3g, g_ln3b = _loss_bwd(xh3, rs3, ln3[0], ln3[1], target, tm=tm, name="loss_bwd")
    dpre2, g_ln2g, g_ln2b, dhg2, dhu2, a2 = _ffn_bwd(dpre3, hg2, hu2, wg2, wu2, wd2,
                                                     (xh2, rs2, ln2[0]), tm=tm, name="ffn2_bwd")
    wgrad = dict(out_dtype=bf16, tm=D_MODEL, mb=1, tn=FF_TILE, nb=4, tk=512, out_blocked=True)
    wdgrad = dict(out_dtype=bf16, tm=512, mb=4, tn=D_MODEL, nb=1, tk=512, out_scale=0.5)
    g_wg2 = _mm_tn(xh2, dhg2, name="g_wg2", affine=ln2, **wgrad)
    g_wu2 = _mm_tn(xh2, dhu2, name="g_wu2", affine=ln2, **wgrad)
    g_wd2 = _mm_tn(a2, dpre3, name="g_wd2", **wdgrad)

    dmix = _mmln([(dpre2, 0, D_MODEL, w_out, 0, D_MODEL, "nt")], tm=tm, name="dmix_bwd")
    g_wout_a = _mm(o, dpre2, mode="tn", out_dtype=bf16, tm=512, tn=D_MODEL, tk=512, name="g_wout_fox")
    g_wout_b = _mm(lru, dpre2, mode="tn", out_dtype=bf16, tm=512, tn=D_MODEL, tk=512, name="g_wout_lru")
    dlx, dlg, g_cw, g_cb, g_ba, g_bx, g_lam, g_wa4, g_wx4, *p_wg2 = _lru_bwd(
        dmix, zl, u, h, conv_w, wa_bd, ba, wx_bd, bx, small["lru_lambda"], name="lru_bwd",
        host=_Exchange([g_wg2], gather=False))
    p_wg2 = p_wg2[0]
    qb_aug, do_aug = _attn_prep_bwd(qkv, cum, lse, dmix, o, tm=tm, name="attn_prep_bwd")
    dq, dcum_q, p_wu2 = _attn_dq2(qb_aug, k_aug, v_aug, do_aug, name="attn_dq",
                                  host=_Exchange([g_wu2], gather=False))
    g_wout_blocked = jnp.concatenate([g_wout_a, g_wout_b], axis=0).reshape(N_DEV, D_MODEL // N_DEV, D_MODEL)
    dk, dv, dcum_k, p_wd2, p_wout = _attn_dkv2(
        qb_aug, k_aug, v_aug, do_aug, name="attn_dkv",
        host=_Exchange([g_wd2.reshape(N_DEV, FF_TILE, D_MODEL), g_wout_blocked], gather=False))
    dfg, g_bf = _cum_bwd(dcum_q, dcum_k, zfg, bfg, name="cum_bwd")

    dz = [(dq, 0, 512), (dk, 1, 512), (dv, 2, 512), (dlx, 3, 512), (dlg, 4, 512), (dfg, 20, LANES)]
    dpre1, g_ln1g, g_ln1b = _mmln(
        [(arr, 0, w, w_in, cb, w, "nt") for (arr, cb, w) in dz],
        tm=tm, name="dx1_bwd", resid=("plain", dpre2), resid_scale=ALPHA, epi="ln_bwd", ln=(xh1, rs1, ln1[0]))
    g_win = [_mm(xh1, arr, mode="tn", out_dtype=bf16, tm=D_MODEL, tn=w, tk=512, name=f"g_win{n}", affine=ln1)
             for n, (arr, cb, w) in enumerate(dz)]
    g_win_full = jnp.concatenate([g[:, :w] for g, (_, _, w) in zip(g_win, dz)], axis=1)[:, :IN_COLS]
    g_win_blocked = g_win_full.reshape(D_MODEL, N_DEV, IN_SHARD).transpose(1, 0, 2)
    dhg1, dhu1, a1, p_win = _ffn_bwd_act(dpre1, hg1, hu1, wd1, tm=tm, name="ffn1_bwd_act",
                                         host=_Exchange([g_win_blocked], gather=False))
    g_wg1 = _mm_tn(x, dhg1, name="g_wg1", **wgrad)
    g_wu1, p_wg1 = _mm_tn(x, dhu1, name="g_wu1", host=_Exchange([g_wg1], gather=False), **wgrad)
    g_wd1, p_wu1 = _mm_tn(a1, dpre1, name="g_wd1", host=_Exchange([g_wu1], gather=False), **wdgrad)
    grad_x, p_wd1 = _ffn_bwd_dx(dpre1, dhg1, dhu1, wg1, wu1, tm=tm, name="ffn1_bwd_dx",
                                host=_Exchange([g_wd1.reshape(N_DEV, FF_TILE, D_MODEL)], gather=False))
    parts = {
        "ffn1_w_gate": p_wg1, "ffn1_w_up": p_wu1, "ffn1_w_down": p_wd1, "w_in": p_win, "w_out": p_wout,
        "ffn2_w_gate": p_wg2, "ffn2_w_up": p_wu2, "ffn2_w_down": p_wd2,
    }
    small_g = {
        "ln1_g": g_ln1g, "ln1_b": g_ln1b, "b_forget": g_bf[:, :HEADS], "conv_w": g_cw, "conv_b": g_cb,
        "rg_wa": _diag_blocks(g_wa4), "rg_ba": g_ba.reshape(HEADS, HEAD_D),
        "rg_wx": _diag_blocks(g_wx4), "rg_bx": g_bx.reshape(HEADS, HEAD_D), "lru_lambda": g_lam,
        "ln2_g": g_ln2g, "ln2_b": g_ln2b, "ln3_g": g_ln3g, "ln3_b": g_ln3b,
    }
    return sq_rows, grad_x, parts, small_g


def _adam_math(w, g, m, v):
    m2 = ADAM_B1 * m + (1.0 - ADAM_B1) * g
    v2 = ADAM_B2 * v + (1.0 - ADAM_B2) * (g * g)
    m_hat = m2 / (1.0 - ADAM_B1 ** ADAM_STEP)
    v_hat = v2 / (1.0 - ADAM_B2 ** ADAM_STEP)
    delta = -ADAM_LR * (m_hat / (jnp.sqrt(v_hat) + ADAM_EPS) + ADAM_WD * w)
    return delta, m2, v2


def _adamw_big(parts, w, m, v, *, tr, name):
    r, c = w.shape

    def body(p_ref, w_ref, m_ref, v_ref, g_ref, d_ref, m2_ref, v2_ref):
        g = p_ref[0].astype(f32)
        for q in range(1, N_DEV):
            g = g + p_ref[q].astype(f32)
        d, m2, v2 = _adam_math(w_ref[...], g, m_ref[...], v_ref[...])
        g_ref[...] = g
        d_ref[...] = d
        m2_ref[...] = m2
        v2_ref[...] = v2

    blk = pl.BlockSpec((tr, c), lambda i: (i, 0))
    return pl.pallas_call(
        body, name=name, grid=(r // tr,),
        in_specs=[pl.BlockSpec((N_DEV, tr, c), lambda i: (0, i, 0)), blk, blk, blk],
        out_specs=[blk] * 4, out_shape=[jax.ShapeDtypeStruct((r, c), f32)] * 4,
        compiler_params=_params(("arbitrary",)),
    )(parts, w, m, v)


def _adamw_small(items, *, name):
    n = len(items)

    def body(*refs):
        ins, outs = refs[:4 * n], refs[4 * n:]
        for k in range(n):
            g, w, m, v = (ins[4 * k + q][...] for q in range(4))
            d, m2, v2 = _adam_math(w, g, m, v)
            outs[3 * k][...] = d
            outs[3 * k + 1][...] = m2
            outs[3 * k + 2][...] = v2

    vm = pl.BlockSpec(memory_space=pltpu.VMEM)
    flat = [a for item in items for a in item]
    out_shape = [jax.ShapeDtypeStruct(item[1].shape, f32) for item in items for _ in range(3)]
    return pl.pallas_call(
        body, name=name, in_specs=[vm] * (4 * n), out_specs=[vm] * (3 * n), out_shape=out_shape,
    )(*flat)


def _sum_parts(parts, *, name):
    def body(p_ref, o_ref):
        acc = p_ref[0]
        for q in range(1, N_DEV):
            acc = acc + p_ref[q]
        o_ref[...] = acc

    vm = pl.BlockSpec(memory_space=pltpu.VMEM)
    return pl.pallas_call(
        body, name=name, in_specs=[vm], out_specs=vm, out_shape=jax.ShapeDtypeStruct(parts.shape[1:], f32),
    )(parts)


WEIGHTS = ["ffn1_w_gate", "ffn1_w_up", "ffn1_w_down", "ln1_g", "ln1_b", "w_in", "b_forget", "conv_w", "conv_b",
           "rg_wa", "rg_ba", "rg_wx", "rg_bx", "lru_lambda", "w_out", "ln2_g", "ln2_b",
           "ffn2_w_gate", "ffn2_w_up", "ffn2_w_down", "ln3_g", "ln3_b"]
BIG = ["ffn1_w_gate", "ffn1_w_up", "ffn1_w_down", "w_in", "w_out", "ffn2_w_gate", "ffn2_w_up", "ffn2_w_down"]
PACKED = ["ln1_g", "ln1_b", "ln2_g", "ln2_b", "ln3_g", "ln3_b", "conv_b", "rg_ba", "rg_bx", "lru_lambda",
          "conv_w", "rg_wa", "rg_wx", "b_forget"]
PACK_ROWS = 600


def _two_d(a):
    return a.reshape((-1, a.shape[-1]))


def kernel(x, ffn1_w_gate, ffn1_w_up, ffn1_w_down, ln1_g, ln1_b, w_in, b_forget, conv_w, conv_b, rg_wa, rg_ba, rg_wx, rg_bx, lru_lambda, w_out, ln2_g, ln2_b, ffn2_w_gate, ffn2_w_up, ffn2_w_down, ln3_g, ln3_b, loss_target, m_ffn1_w_gate, m_ffn1_w_up, m_ffn1_w_down, m_ln1_g, m_ln1_b, m_w_in, m_b_forget, m_conv_w, m_conv_b, m_rg_wa, m_rg_ba, m_rg_wx, m_rg_bx, m_lru_lambda, m_w_out, m_ln2_g, m_ln2_b, m_ffn2_w_gate, m_ffn2_w_up, m_ffn2_w_down, m_ln3_g, m_ln3_b, v_ffn1_w_gate, v_ffn1_w_up, v_ffn1_w_down, v_ln1_g, v_ln1_b, v_w_in, v_b_forget, v_conv_w, v_conv_b, v_rg_wa, v_rg_ba, v_rg_wx, v_rg_bx, v_lru_lambda, v_w_out, v_ln2_g, v_ln2_b, v_ffn2_w_gate, v_ffn2_w_up, v_ffn2_w_down, v_ln3_g, v_ln3_b):
    w_args = (ffn1_w_gate, ffn1_w_up, ffn1_w_down, ln1_g, ln1_b, w_in, b_forget, conv_w, conv_b, rg_wa, rg_ba, rg_wx, rg_bx, lru_lambda, w_out, ln2_g, ln2_b, ffn2_w_gate, ffn2_w_up, ffn2_w_down, ln3_g, ln3_b)
    m_args = (m_ffn1_w_gate, m_ffn1_w_up, m_ffn1_w_down, m_ln1_g, m_ln1_b, m_w_in, m_b_forget, m_conv_w, m_conv_b, m_rg_wa, m_rg_ba, m_rg_wx, m_rg_bx, m_lru_lambda, m_w_out, m_ln2_g, m_ln2_b, m_ffn2_w_gate, m_ffn2_w_up, m_ffn2_w_down, m_ln3_g, m_ln3_b)
    v_args = (v_ffn1_w_gate, v_ffn1_w_up, v_ffn1_w_down, v_ln1_g, v_ln1_b, v_w_in, v_b_forget, v_conv_w, v_conv_b, v_rg_wa, v_rg_ba, v_rg_wx, v_rg_bx, v_lru_lambda, v_w_out, v_ln2_g, v_ln2_b, v_ffn2_w_gate, v_ffn2_w_up, v_ffn2_w_down, v_ln3_g, v_ln3_b)
    w = dict(zip(WEIGHTS, w_args))
    m = dict(zip(WEIGHTS, m_args))
    v = dict(zip(WEIGHTS, v_args))
    me = 4 * lax.axis_index("x") + 2 * lax.axis_index("y") + lax.axis_index("c")

    sent = {n: _two_d(w[n]).astype(bf16) for n in BIG}
    sent["conv_w"] = _two_d(w["conv_w"])
    small = {n: w[n] for n in ("ln1_g", "ln1_b", "ln2_g", "ln2_b", "ln3_g", "ln3_b", "b_forget", "conv_b",
                               "lru_lambda")}
    small.update({n: w[n][0] for n in ("rg_wa", "rg_ba", "rg_wx", "rg_bx")})

    sq_rows, grad_x, parts, small_g = _local_step(x[0], loss_target[0], sent, small)
    loss = lax.psum(0.5 * jnp.sum(sq_rows) / D_MODEL, ("x", "y", "c"))

    pieces = [small_g[n].reshape(-1) for n in PACKED]
    packed = jnp.concatenate(pieces + [jnp.zeros((PACK_ROWS * LANES - sum(p.shape[0] for p in pieces),), f32)])
    (all_packed,) = _exchange([packed.reshape(PACK_ROWS, LANES)], gather=True, name="gather_small_grads")
    total = _sum_parts(all_packed, name="sum_small_grads").reshape(-1)
    grads, off = {}, 0
    for n, p in zip(PACKED, pieces):
        grads[n] = total[off:off + p.shape[0]].reshape(small_g[n].shape)
        off += p.shape[0]
    grads["conv_w"] = lax.dynamic_slice_in_dim(grads["conv_w"], me * (LRU_W // N_DEV), LRU_W // N_DEV, axis=1)

    delta, new_m, new_v = {}, {}, {}
    for n in BIG:
        w2 = _two_d(w[n])
        g, d, m2, v2 = _adamw_big(parts[n], w2, _two_d(m[n]), _two_d(v[n]), tr=128, name="adamw_" + n)
        grads[n], delta[n], new_m[n], new_v[n] = g, d, m2, v2
    small_names = [n for n in WEIGHTS if n not in BIG]
    outs = _adamw_small([(_two_d(grads[n]), _two_d(w[n]), _two_d(m[n]), _two_d(v[n])) for n in small_names],
                        name="adamw_small")
    for k, n in enumerate(small_names):
        delta[n], new_m[n], new_v[n] = outs[3 * k], outs[3 * k + 1], outs[3 * k + 2]

    def shaped(d):
        return [d[n].reshape(w[n].shape) for n in WEIGHTS]

    return (loss, grad_x[None], *shaped(grads), *shaped(delta), *shaped(new_m), *shaped(new_v))
```

```python
import functools
import math

import jax
import jax.numpy as jnp
from jax import lax
from jax.experimental import pallas as pl
from jax.experimental.pallas import tpu as pltpu

f32 = jnp.float32
bf16 = jnp.bfloat16

N_DEV = 8
D_MODEL = 1024
D_FF = 4096
FF_TILE = D_FF // N_DEV
FOX_W = 512
LRU_W = 512
HEADS = 8
HEAD_D = 64
IN_COLS = 2568
IN_SHARD = IN_COLS // N_DEV
LANES = 128
LN_EPS = 1e-5
ALPHA = 2.0 ** 0.25
ATT_SCALE = 1.0 / math.sqrt(HEAD_D)
LRU_C = 8.0
NEG_BIG = -1e30

ADAM_LR = 0.001
ADAM_B1 = 0.9
ADAM_B2 = 0.999
ADAM_EPS = 1e-08
ADAM_WD = 0.01
ADAM_STEP = 10

VMEM_LIMIT = 56 * 1024 * 1024
MESH_T = pl.DeviceIdType.MESH


def _params(sem, **kw):
    return pltpu.CompilerParams(dimension_semantics=sem, vmem_limit_bytes=VMEM_LIMIT, **kw)


def _sigmoid(x):
    return 1.0 / (1.0 + jnp.exp(-x))


def _softplus(x):
    return jnp.maximum(x, 0.0) + jnp.log(1.0 + jnp.exp(-jnp.abs(x)))


def _one_minus_exp(x):
    series = -x * (1.0 + x * (0.5 + x * (1.0 / 6 + x * (1.0 / 24 + x * (1.0 / 120 + x * (1.0 / 720))))))
    return jnp.where(x > -0.125, series, 1.0 - jnp.exp(x))


_GELU_C = math.sqrt(2.0 / math.pi)


def _gelu_and_grad(x):
    inner = _GELU_C * (x + 0.044715 * x * x * x)
    t = jnp.tanh(inner)
    g = 0.5 * x * (1.0 + t)
    dg = 0.5 * (1.0 + t) + 0.5 * x * (1.0 - t * t) * _GELU_C * (1.0 + 3 * 0.044715 * x * x)
    return g, dg


def _ln_fwd_tile(pre):
    mu = jnp.mean(pre, axis=-1, keepdims=True)
    xc = pre - mu
    var = jnp.mean(xc * xc, axis=-1, keepdims=True)
    rstd = lax.rsqrt(var + LN_EPS)
    return xc * rstd, rstd


def _ln_bwd_tile(dy, xhat, rstd, g):
    dyg = dy * g
    m1 = jnp.mean(dyg, axis=-1, keepdims=True)
    m2 = jnp.mean(dyg * xhat, axis=-1, keepdims=True)
    dpre = rstd * (dyg - m1 - xhat * m2)
    return dpre, jnp.sum(dy * xhat, axis=0, keepdims=True), jnp.sum(dy, axis=0, keepdims=True)


_NT = (((1,), (1,)), ((), ()))
_TN = (((0,), (0,)), ((), ()))


class _Exchange:
    def __init__(self, arrs, gather):
        self.arrs, self.gather, self.n = list(arrs), gather, len(arrs)

    def out_shape(self):
        return [jax.ShapeDtypeStruct(((N_DEV,) + a.shape) if self.gather else a.shape, a.dtype) for a in self.arrs]

    def scratch(self):
        n_remote = self.n * (N_DEV - 1)
        return [pltpu.SemaphoreType.DMA((n_remote,)), pltpu.SemaphoreType.DMA((n_remote,)),
                pltpu.SemaphoreType.DMA((self.n,))]

    def copies(self, ins, outs, sems):
        send_sems, recv_sems, local_sems = sems
        x, y, c = lax.axis_index("x"), lax.axis_index("y"), lax.axis_index("c")
        me = 4 * x + 2 * y + c
        out = []
        for k in range(self.n):
            for d in range(1, N_DEV):
                px = 1 - x if d & 4 else x
                py = 1 - y if d & 2 else y
                pc = 1 - c if d & 1 else c
                sem = k * (N_DEV - 1) + d - 1
                out.append(pltpu.make_async_remote_copy(
                    src_ref=ins[k].at[4 * px + 2 * py + pc], dst_ref=outs[k].at[me],
                    send_sem=send_sems.at[sem], recv_sem=recv_sems.at[sem],
                    device_id=(px, py, pc), device_id_type=MESH_T))
            out.append(pltpu.make_async_copy(ins[k].at[me], outs[k].at[me], local_sems.at[k]))
        return out

    def gather_copies(self, ins, outs, sems):
        send_sems, recv_sems, local_sems = sems
        x, y, c = lax.axis_index("x"), lax.axis_index("y"), lax.axis_index("c")
        sibling = (x, y, 1 - c)
        chips = [(1 - x, y), (x, 1 - y), (1 - x, 1 - y)]
        out = []
        for k in range(self.n):
            def copy(s, block, to, src=None, k=k):
                rows = outs[k].at[4 * block[0] + 2 * block[1] + block[2]]
                sem = k * (N_DEV - 1) + s
                return pltpu.make_async_remote_copy(
                    src_ref=rows if src is None else src, dst_ref=rows, send_sem=send_sems.at[sem],
                    recv_sem=recv_sems.at[sem], device_id=to, device_id_type=MESH_T)

            first = [copy(0, (x, y, c), sibling, src=ins[k])]
            first += [copy(1 + q, (x, y, c), (*chip, c), src=ins[k]) for q, chip in enumerate(chips)]
            passed = [copy(4 + q, (*chip, c), sibling) for q, chip in enumerate(chips)]
            own = pltpu.make_async_copy(ins[k], outs[k].at[4 * x + 2 * y + c], local_sems.at[k])
            out.append((first, passed, own, copy))
        return out, sibling, chips, (x, y, c)

    def start(self, ins, outs, sems):
        if not self.gather:
            for cp in self.copies(ins, outs, sems):
                cp.start()
            return
        per_array, _, _, _ = self.gather_copies(ins, outs, sems)
        for first, _, own, _ in per_array:
            own.start()
            for cp in first:
                cp.start()

    def wait(self, ins, outs, sems):
        if not self.gather:
            for cp in self.copies(ins, outs, sems):
                cp.wait()
            return
        per_array, sibling, chips, (x, y, c) = self.gather_copies(ins, outs, sems)
        for first, passed, own, copy in per_array:
            for q, chip in enumerate(chips):
                copy(1 + q, (*chip, c), (x, y, c)).wait_recv()
                passed[q].start()
        for first, passed, own, copy in per_array:
            copy(0, sibling, (x, y, c)).wait_recv()
            for q, chip in enumerate(chips):
                copy(4 + q, (*chip, 1 - c), (x, y, c)).wait_recv()
            for cp in first + passed:
                cp.wait_send()
            own.wait()


def _hosted_call(host, body, *, name, grid, in_specs, out_specs, out_shape, scratch_shapes=(), compiler_params):
    out_specs = list(out_specs) if isinstance(out_specs, (list, tuple)) else [out_specs]
    out_shape = list(out_shape) if isinstance(out_shape, (list, tuple)) else [out_shape]
    if host is None:
        return pl.pallas_call(body, name=name, grid=grid, in_specs=in_specs, out_specs=out_specs,
                              out_shape=out_shape, scratch_shapes=list(scratch_shapes),
                              compiler_params=compiler_params)
    n_in, n_out, n_scr, k = len(in_specs), len(out_shape), len(scratch_shapes), host.n

    def wrapped(*refs):
        ins, h_in = refs[:n_in], refs[n_in:n_in + k]
        outs, h_out = refs[n_in + k:n_in + k + n_out], refs[n_in + k + n_out:n_in + 2 * k + n_out]
        scr, sems = refs[n_in + 2 * k + n_out:n_in + 2 * k + n_out + n_scr], refs[n_in + 2 * k + n_out + n_scr:]
        ids = [pl.program_id(a) for a in range(len(grid))]
        first = functools.reduce(jnp.logical_and, [i == 0 for i in ids])
        last = functools.reduce(jnp.logical_and, [i == g - 1 for i, g in zip(ids, grid)])

        @pl.when(first)
        def _():
            host.start(h_in, h_out, sems)

        body(*ins, *outs, *scr)

        @pl.when(last)
        def _():
            host.wait(h_in, h_out, sems)

    hbm = pl.BlockSpec(memory_space=pl.ANY)
    call = pl.pallas_call(
        wrapped, name=name, grid=grid, in_specs=list(in_specs) + [hbm] * k, out_specs=out_specs + [hbm] * k,
        out_shape=out_shape + host.out_shape(), scratch_shapes=list(scratch_shapes) + host.scratch(),
        compiler_params=compiler_params)
    return lambda *args: call(*args, *host.arrs)


def _exchange(arrs, *, gather, name):
    host = _Exchange(arrs, gather)

    def body(*refs):
        ins, outs, sems = refs[:host.n], refs[host.n:2 * host.n], refs[2 * host.n:]
        host.start(ins, outs, sems)
        host.wait(ins, outs, sems)

    hbm = pl.BlockSpec(memory_space=pl.ANY)
    return pl.pallas_call(
        body, name=name, in_specs=[hbm] * host.n, out_specs=[hbm] * host.n, out_shape=host.out_shape(),
        scratch_shapes=host.scratch(), compiler_params=pltpu.CompilerParams(has_side_effects=True),
    )(*arrs)


def _ffn_fwd(xhat, g_in, b_in, wg, wu, wd, *, tm, name, host=None):
    t = xhat.shape[0]
    nj = N_DEV

    def body(x_ref, g_ref, b_ref, wg_ref, wu_ref, wd_ref, xo_ref, rstd_ref, hg_ref, hu_ref, xb, acc):
        j = pl.program_id(1)

        @pl.when(j == 0)
        def _():
            xb[...] = (x_ref[...] * g_ref[...] + b_ref[...]).astype(bf16)
            acc[...] = jnp.zeros_like(acc)

        hg = jnp.dot(xb[...], wg_ref[...], preferred_element_type=f32)
        hu = jnp.dot(xb[...], wu_ref[...], preferred_element_type=f32)
        hg_ref[...] = hg.astype(bf16)
        hu_ref[...] = hu.astype(bf16)
        a = hg * _sigmoid(hg) * hu
        acc[...] += jnp.dot(a.astype(bf16), wd_ref[...], preferred_element_type=f32)

        @pl.when(j == nj - 1)
        def _():
            x = x_ref[...] * g_ref[...] + b_ref[...]
            xo, rstd = _ln_fwd_tile(ALPHA * x + 0.5 * acc[...])
            xo_ref[...] = xo
            rstd_ref[...] = rstd

    row = pl.BlockSpec((1, D_MODEL), lambda i, j: (0, 0))
    return _hosted_call(
        host, body, name=name, grid=(t // tm, nj),
        in_specs=[pl.BlockSpec((tm, D_MODEL), lambda i, j: (i, 0)), row, row,
                  pl.BlockSpec((None, D_MODEL, FF_TILE), lambda i, j: (j, 0, 0)),
                  pl.BlockSpec((None, D_MODEL, FF_TILE), lambda i, j: (j, 0, 0)),
                  pl.BlockSpec((None, FF_TILE, D_MODEL), lambda i, j: (j, 0, 0))],
        out_specs=[pl.BlockSpec((tm, D_MODEL), lambda i, j: (i, 0)),
                   pl.BlockSpec((tm, 1), lambda i, j: (i, 0)),
                   pl.BlockSpec((tm, FF_TILE), lambda i, j: (i, j)),
                   pl.BlockSpec((tm, FF_TILE), lambda i, j: (i, j))],
        out_shape=[jax.ShapeDtypeStruct((t, D_MODEL), f32), jax.ShapeDtypeStruct((t, 1), f32),
                   jax.ShapeDtypeStruct((t, D_FF), bf16), jax.ShapeDtypeStruct((t, D_FF), bf16)],
        scratch_shapes=[pltpu.VMEM((tm, D_MODEL), bf16), pltpu.VMEM((tm, D_MODEL), f32)],
        compiler_params=_params(("arbitrary", "arbitrary")),
    )(xhat, g_in, b_in, wg, wu, wd)


def _ffn_bwd(dpre, hg, hu, wg, wu, wd, ln_in, *, tm, name, host=None):
    t = dpre.shape[0]
    nj = N_DEV
    with_ln = ln_in is not None

    def body(*refs):
        if with_ln:
            (dp_ref, hg_ref, hu_ref, wg_ref, wu_ref, wd_ref, xh_ref, rs_ref, g_ref,
             dx_ref, gg_ref, gb_ref, dhg_ref, dhu_ref, a_ref, dfb, acc) = refs
        else:
            (dp_ref, hg_ref, hu_ref, wg_ref, wu_ref, wd_ref,
             dx_ref, dhg_ref, dhu_ref, a_ref, dfb, acc) = refs
        i = pl.program_id(0)
        j = pl.program_id(1)

        @pl.when(j == 0)
        def _():
            dfb[...] = (0.5 * dp_ref[...]).astype(bf16)
            acc[...] = jnp.zeros_like(acc)

        da = lax.dot_general(dfb[...], wd_ref[...], _NT, preferred_element_type=f32)
        hgv = hg_ref[...].astype(f32)
        huv = hu_ref[...].astype(f32)
        sg = _sigmoid(hgv)
        silu = hgv * sg
        a_ref[...] = (silu * huv).astype(bf16)
        dhu = (da * silu).astype(bf16)
        dhg = (da * huv * (sg * (1.0 + hgv * (1.0 - sg)))).astype(bf16)
        dhg_ref[...] = dhg
        dhu_ref[...] = dhu
        acc[...] += (lax.dot_general(dhg, wg_ref[...], _NT, preferred_element_type=f32)
                     + lax.dot_general(dhu, wu_ref[...], _NT, preferred_element_type=f32))

        @pl.when(j == nj - 1)
        def _():
            dx = ALPHA * dp_ref[...] + acc[...]
            if with_ln:
                dprev, gg, gb = _ln_bwd_tile(dx, xh_ref[...], rs_ref[...], g_ref[...])
                dx_ref[...] = dprev

                @pl.when(i == 0)
                def _():
                    gg_ref[...] = gg
                    gb_ref[...] = gb

                @pl.when(i > 0)
                def _():
                    gg_ref[...] += gg
                    gb_ref[...] += gb
            else:
                dx_ref[...] = dx

    tok = pl.BlockSpec((tm, D_MODEL), lambda i, j: (i, 0))
    row = pl.BlockSpec((1, D_MODEL), lambda i, j: (0, 0))
    hid = pl.BlockSpec((tm, FF_TILE), lambda i, j: (i, j))
    in_specs = [tok, hid, hid,
                pl.BlockSpec((None, D_MODEL, FF_TILE), lambda i, j: (j, 0, 0)),
                pl.BlockSpec((None, D_MODEL, FF_TILE), lambda i, j: (j, 0, 0)),
                pl.BlockSpec((None, FF_TILE, D_MODEL), lambda i, j: (j, 0, 0))]
    args = [dpre, hg, hu, wg, wu, wd]
    out_specs = [tok]
    out_shape = [jax.ShapeDtypeStruct((t, D_MODEL), f32)]
    if with_ln:
        in_specs += [tok, pl.BlockSpec((tm, 1), lambda i, j: (i, 0)), row]
        args += list(ln_in)
        out_specs += [row, row]
        out_shape += [jax.ShapeDtypeStruct((1, D_MODEL), f32)] * 2
    out_specs += [hid, hid, hid]
    out_shape += [jax.ShapeDtypeStruct((t, D_FF), bf16)] * 3
    return _hosted_call(
        host, body, name=name, grid=(t // tm, nj), in_specs=in_specs, out_specs=out_specs, out_shape=out_shape,
        scratch_shapes=[pltpu.VMEM((tm, D_MODEL), bf16), pltpu.VMEM((tm, D_MODEL), f32)],
        compiler_params=_params(("arbitrary", "arbitrary")),
    )(*args)


def _ffn_bwd_act(dpre, hg, hu, wd, *, tm, name, host=None):
    t = dpre.shape[0]

    def body(dp_ref, hg_ref, hu_ref, wd_ref, dhg_ref, dhu_ref, a_ref, dfb):
        @pl.when(pl.program_id(1) == 0)
        def _():
            dfb[...] = (0.5 * dp_ref[...]).astype(bf16)

        da = lax.dot_general(dfb[...], wd_ref[...], _NT, preferred_element_type=f32)
        hgv = hg_ref[...].astype(f32)
        huv = hu_ref[...].astype(f32)
        sg = _sigmoid(hgv)
        silu = hgv * sg
        a_ref[...] = (silu * huv).astype(bf16)
        dhu_ref[...] = (da * silu).astype(bf16)
        dhg_ref[...] = (da * huv * (sg * (1.0 + hgv * (1.0 - sg)))).astype(bf16)

    hid = pl.BlockSpec((tm, FF_TILE), lambda i, j: (i, j))
    return _hosted_call(
        host, body, name=name, grid=(t // tm, N_DEV),
        in_specs=[pl.BlockSpec((tm, D_MODEL), lambda i, j: (i, 0)), hid, hid,
                  pl.BlockSpec((None, FF_TILE, D_MODEL), lambda i, j: (j, 0, 0))],
        out_specs=[hid, hid, hid], out_shape=[jax.ShapeDtypeStruct((t, D_FF), bf16)] * 3,
        scratch_shapes=[pltpu.VMEM((tm, D_MODEL), bf16)],
        compiler_params=_params(("arbitrary", "arbitrary")),
    )(dpre, hg, hu, wd)


def _ffn_bwd_dx(dpre, dhg, dhu, wg, wu, *, tm, name, host=None):
    t = dpre.shape[0]
    nj = N_DEV

    def body(dp_ref, dhg_ref, dhu_ref, wg_ref, wu_ref, dx_ref, acc):
        j = pl.program_id(1)

        @pl.when(j == 0)
        def _():
            acc[...] = jnp.zeros_like(acc)

        acc[...] += (lax.dot_general(dhg_ref[...], wg_ref[...], _NT, preferred_element_type=f32)
                     + lax.dot_general(dhu_ref[...], wu_ref[...], _NT, preferred_element_type=f32))

        @pl.when(j == nj - 1)
        def _():
            dx_ref[...] = ALPHA * dp_ref[...] + acc[...]

    tok = pl.BlockSpec((tm, D_MODEL), lambda i, j: (i, 0))
    hid = pl.BlockSpec((tm, FF_TILE), lambda i, j: (i, j))
    wspec = pl.BlockSpec((None, D_MODEL, FF_TILE), lambda i, j: (j, 0, 0))
    return _hosted_call(
        host, body, name=name, grid=(t // tm, nj), in_specs=[tok, hid, hid, wspec, wspec],
        out_specs=[tok], out_shape=[jax.ShapeDtypeStruct((t, D_MODEL), f32)],
        scratch_shapes=[pltpu.VMEM((tm, D_MODEL), f32)],
        compiler_params=_params(("arbitrary", "arbitrary")),
    )(dpre, dhg, dhu, wg, wu)


def _mm(a, b, *, mode, out_dtype, tm, tn, tk, name, affine=None, a_cols=None, b_cols=None,
        b_blocked=False, out_blocked=False, out_scale=None):
    if mode == "nn":
        m_full, k_full = a.shape
        m_dim, k_dim = (m_full, a_cols[1]) if a_cols else (m_full, k_full)
    else:
        k_dim, m_full = a.shape
        m_dim = a_cols[1] if a_cols else m_full
    a_off = a_cols[0] if a_cols else 0
    if b_blocked:
        n_dim = b.shape[0] * b.shape[2]
        assert b.shape[2] == tn
    else:
        n_dim = b_cols[1] if b_cols else b.shape[1]
    b_off = b_cols[0] if b_cols else 0
    assert m_dim % tm == 0 and n_dim % tn == 0 and k_dim % tk == 0, (name, m_dim, n_dim, k_dim)
    nk = k_dim // tk

    def body(*refs):
        if affine is not None:
            a_ref, g_ref, s_ref, b_ref, o_ref, acc = refs
        else:
            a_ref, b_ref, o_ref, acc = refs
        k = pl.program_id(2)

        @pl.when(k == 0)
        def _():
            acc[...] = jnp.zeros_like(acc)

        av = a_ref[...]
        if affine is not None:
            av = av * g_ref[...] + s_ref[...]
        av = av.astype(bf16)
        bv = b_ref[...].astype(bf16)
        if mode == "nn":
            acc[...] += jnp.dot(av, bv, preferred_element_type=f32)
        else:
            acc[...] += lax.dot_general(av, bv, _TN, preferred_element_type=f32)

        @pl.when(k == nk - 1)
        def _():
            res = acc[...] if out_scale is None else acc[...] * out_scale
            o_ref[...] = res.astype(out_dtype)

    if mode == "nn":
        a_spec = pl.BlockSpec((tm, tk), lambda i, j, k: (i, k + a_off))
        aff_spec = pl.BlockSpec((1, tk), lambda i, j, k: (0, k + a_off))
    else:
        a_spec = pl.BlockSpec((tk, tm), lambda i, j, k: (k, i + a_off))
        aff_spec = pl.BlockSpec((1, tm), lambda i, j, k: (0, i + a_off))
    if b_blocked:
        b_spec = pl.BlockSpec((None, tk, tn), lambda i, j, k: (j, k, 0))
    else:
        b_spec = pl.BlockSpec((tk, tn), lambda i, j, k: (k, j + b_off))
    if out_blocked:
        o_spec = pl.BlockSpec((None, tm, tn), lambda i, j, k: (j, i, 0))
        o_shape = jax.ShapeDtypeStruct((n_dim // tn, m_dim, tn), out_dtype)
    else:
        o_spec = pl.BlockSpec((tm, tn), lambda i, j, k: (i, j))
        o_shape = jax.ShapeDtypeStruct((m_dim, n_dim), out_dtype)
    in_specs = [a_spec] + ([aff_spec, aff_spec] if affine is not None else []) + [b_spec]
    args = [a] + (list(affine) if affine is not None else []) + [b]
    return pl.pallas_call(
        body, name=name, grid=(m_dim // tm, n_dim // tn, nk), in_specs=in_specs, out_specs=o_spec,
        out_shape=o_shape, scratch_shapes=[pltpu.VMEM((tm, tn), f32)],
        compiler_params=_params(("arbitrary", "arbitrary", "arbitrary")),
    )(*args)


def _mm_tn(a, b, *, out_dtype, tm, mb, tn, nb, tk, name, affine=None, out_blocked=False, out_scale=None, host=None):
    k_dim, m_dim = a.shape
    n_dim = b.shape[1]
    assert m_dim % (mb * tm) == 0 and n_dim % (nb * tn) == 0 and k_dim % tk == 0, (name, m_dim, n_dim, k_dim)
    nk = k_dim // tk

    def body(*refs):
        if affine is not None:
            a_ref, g_ref, s_ref, b_ref, o_ref, acc = refs
        else:
            a_ref, b_ref, o_ref, acc = refs
        k = pl.program_id(2)

        @pl.when(k == 0)
        def _():
            acc[...] = jnp.zeros_like(acc)

        av = a_ref[...]
        if affine is not None:
            av = av * g_ref[...] + s_ref[...]
        av = av.astype(bf16)
        bv = b_ref[...].astype(bf16)
        for im in range(mb):
            a_t = av[:, im * tm:(im + 1) * tm].T
            for jn in range(nb):
                acc[im * nb + jn] += jnp.dot(a_t, bv[:, jn * tn:(jn + 1) * tn], preferred_element_type=f32)

        @pl.when(k == nk - 1)
        def _():
            for im in range(mb):
                for jn in range(nb):
                    res = acc[im * nb + jn]
                    if out_scale is not None:
                        res = res * out_scale
                    if out_blocked:
                        o_ref[jn, im * tm:(im + 1) * tm, :] = res.astype(out_dtype)
                    else:
                        o_ref[im * tm:(im + 1) * tm, jn * tn:(jn + 1) * tn] = res.astype(out_dtype)

    a_spec = pl.BlockSpec((tk, mb * tm), lambda i, j, k: (k, i))
    aff_spec = pl.BlockSpec((1, mb * tm), lambda i, j, k: (0, i))
    b_spec = pl.BlockSpec((tk, nb * tn), lambda i, j, k: (k, j))
    if out_blocked:
        o_spec = pl.BlockSpec((nb, mb * tm, tn), lambda i, j, k: (j, i, 0))
        o_shape = jax.ShapeDtypeStruct((n_dim // tn, m_dim, tn), out_dtype)
    else:
        o_spec = pl.BlockSpec((mb * tm, nb * tn), lambda i, j, k: (i, j))
        o_shape = jax.ShapeDtypeStruct((m_dim, n_dim), out_dtype)
    in_specs = [a_spec] + ([aff_spec, aff_spec] if affine is not None else []) + [b_spec]
    args = [a] + (list(affine) if affine is not None else []) + [b]
    res = _hosted_call(
        host, body, name=name, grid=(m_dim // (mb * tm), n_dim // (nb * tn), nk), in_specs=in_specs,
        out_specs=o_spec, out_shape=o_shape, scratch_shapes=[pltpu.VMEM((mb * nb, tm, tn), f32)],
        compiler_params=_params(("arbitrary", "arbitrary", "arbitrary")),
    )(*args)
    return res[0] if host is None else res


def _mmln(pairs, *, tm, name, resid=None, resid_scale=1.0, epi=None, ln=None, n_out=D_MODEL):
    t = pairs[0][0].shape[0]
    n_pairs = len(pairs)
    n_resid = 0 if resid is None else len(resid) - 1

    def body(*refs):
        pos = 0
        val = None
        for p in range(n_pairs):
            a_ref, b_ref = refs[pos], refs[pos + 1]
            pos += 2
            av = a_ref[...].astype(bf16)
            bv = b_ref[...].astype(bf16)
            if pairs[p][6] == "nn":
                term = jnp.dot(av, bv, preferred_element_type=f32)
            else:
                term = lax.dot_general(av, bv, _NT, preferred_element_type=f32)
            val = term if val is None else val + term
        if resid is not None:
            if resid[0] == "plain":
                r = refs[pos][...]
            else:
                r = refs[pos][...] * refs[pos + 1][...] + refs[pos + 2][...]
            pos += n_resid
            val = val + resid_scale * r
        if epi is None:
            o_ref = refs[pos]
            o_ref[...] = val.astype(o_ref.dtype)
        elif epi == "ln_fwd":
            xo, rstd = _ln_fwd_tile(val)
            refs[pos][...] = xo
            refs[pos + 1][...] = rstd
        else:
            xh_ref, rs_ref, g_ref, dx_ref, gg_ref, gb_ref = refs[pos:pos + 6]
            dprev, gg, gb = _ln_bwd_tile(val, xh_ref[...], rs_ref[...], g_ref[...])
            dx_ref[...] = dprev
            i = pl.program_id(0)

            @pl.when(i == 0)
            def _():
                gg_ref[...] = gg
                gb_ref[...] = gb

            @pl.when(i > 0)
            def _():
                gg_ref[...] += gg
                gb_ref[...] += gb

    in_specs, args = [], []
    for (a, acb, aw, b, bcb, bw, mode) in pairs:
        in_specs.append(pl.BlockSpec((tm, aw), lambda i, acb=acb: (i, acb)))
        args.append(a)
        if mode == "nn":
            in_specs.append(pl.BlockSpec((aw, n_out), lambda i, bcb=bcb: (bcb, 0)))
        else:
            in_specs.append(pl.BlockSpec((n_out, bw), lambda i, bcb=bcb: (0, bcb)))
        args.append(b)
    tok = pl.BlockSpec((tm, n_out), lambda i: (i, 0))
    row = pl.BlockSpec((1, n_out), lambda i: (0, 0))
    col = pl.BlockSpec((tm, 1), lambda i: (i, 0))
    if resid is not None:
        in_specs += [tok] if resid[0] == "plain" else [tok, row, row]
        args += list(resid[1:])
    if epi is None:
        out_specs, out_shape = tok, jax.ShapeDtypeStruct((t, n_out), f32)
    elif epi == "ln_fwd":
        out_specs = [tok, col]
        out_shape = [jax.ShapeDtypeStruct((t, n_out), f32), jax.ShapeDtypeStruct((t, 1), f32)]
    else:
        in_specs += [tok, col, row]
        args += list(ln)
        out_specs = [tok, row, row]
        out_shape = [jax.ShapeDtypeStruct((t, n_out), f32)] + [jax.ShapeDtypeStruct((1, n_out), f32)] * 2
    return pl.pallas_call(
        body, name=name, grid=(t // tm,), in_specs=in_specs, out_specs=out_specs, out_shape=out_shape,
        compiler_params=_params(("arbitrary",)),
    )(*args)


def _loss_bwd(xhat, rstd, g, b, target, *, tm, name):
    t = xhat.shape[0]

    def body(xh_ref, rs_ref, g_ref, b_ref, tg_ref, dx_ref, sq_ref, gg_ref, gb_ref):
        i = pl.program_id(0)
        xh = xh_ref[...]
        diff = xh * g_ref[...] + b_ref[...] - tg_ref[...]
        sq = jnp.sum(diff * diff, axis=0, keepdims=True)
        dprev, gg, gb = _ln_bwd_tile(diff * (1.0 / D_MODEL), xh, rs_ref[...], g_ref[...])
        dx_ref[...] = dprev

        @pl.when(i == 0)
        def _():
            sq_ref[...] = sq
            gg_ref[...] = gg
            gb_ref[...] = gb

        @pl.when(i > 0)
        def _():
            sq_ref[...] += sq
            gg_ref[...] += gg
            gb_ref[...] += gb

    tok = pl.BlockSpec((tm, D_MODEL), lambda i: (i, 0))
    row = pl.BlockSpec((1, D_MODEL), lambda i: (0, 0))
    return pl.pallas_call(
        body, name=name, grid=(t // tm,),
        in_specs=[tok, pl.BlockSpec((tm, 1), lambda i: (i, 0)), row, row, tok],
        out_specs=[tok, row, row, row],
        out_shape=[jax.ShapeDtypeStruct((t, D_MODEL), f32)] + [jax.ShapeDtypeStruct((1, D_MODEL), f32)] * 3,
        compiler_params=_params(("arbitrary",)),
    )(xhat, rstd, g, b, target)


CUM_TILE = 256


def _tri(n, lower):
    r = lax.broadcasted_iota(jnp.int32, (n, n), 0)
    c = lax.broadcasted_iota(jnp.int32, (n, n), 1)
    return jnp.where((r >= c) if lower else (r <= c), 1.0, 0.0).astype(f32)


def _cum_fwd(zfg, bfg, *, name):
    t = zfg.shape[0]

    def body(z_ref, b_ref, o_ref, carry):
        @pl.when(pl.program_id(0) == 0)
        def _():
            carry[...] = jnp.zeros_like(carry)

        ls = -_softplus(-(z_ref[...] + b_ref[...]))
        c = jnp.dot(_tri(CUM_TILE, True), ls, preferred_element_type=f32,
                    precision=lax.Precision.HIGHEST) + carry[...]
        o_ref[...] = c
        carry[...] = c[CUM_TILE - 1:CUM_TILE, :]

    blk = pl.BlockSpec((CUM_TILE, LANES), lambda i: (i, 0))
    return pl.pallas_call(
        body, name=name, grid=(t // CUM_TILE,),
        in_specs=[blk, pl.BlockSpec((1, LANES), lambda i: (0, 0))], out_specs=blk,
        out_shape=jax.ShapeDtypeStruct((t, LANES), f32), scratch_shapes=[pltpu.VMEM((1, LANES), f32)],
        compiler_params=_params(("arbitrary",)),
    )(zfg, bfg)


def _cum_bwd(dcum_q, dcum_k, zfg, bfg, *, name):
    t = zfg.shape[0]
    n = t // CUM_TILE

    def body(d_ref, d2_ref, z_ref, b_ref, o_ref, s_ref, carry):
        i = pl.program_id(0)

        @pl.when(i == 0)
        def _():
            carry[...] = jnp.zeros_like(carry)

        dls = jnp.dot(_tri(CUM_TILE, False), d_ref[...] + d2_ref[...], preferred_element_type=f32,
                      precision=lax.Precision.HIGHEST) + carry[...]
        carry[...] = dls[0:1, :]
        lane = lax.broadcasted_iota(jnp.int32, (CUM_TILE, LANES), 1)
        dfg = jnp.where(lane < HEADS, dls * _sigmoid(-(z_ref[...] + b_ref[...])), 0.0)
        o_ref[...] = dfg
        tot = jnp.sum(dfg, axis=0, keepdims=True)

        @pl.when(i == 0)
        def _():
            s_ref[...] = tot

        @pl.when(i > 0)
        def _():
            s_ref[...] += tot

    blk = pl.BlockSpec((CUM_TILE, LANES), lambda i: (n - 1 - i, 0))
    row = pl.BlockSpec((1, LANES), lambda i: (0, 0))
    return pl.pallas_call(
        body, name=name, grid=(n,), in_specs=[blk, blk, blk, row], out_specs=[blk, row],
        out_shape=[jax.ShapeDtypeStruct((t, LANES), f32), jax.ShapeDtypeStruct((1, LANES), f32)],
        scratch_shapes=[pltpu.VMEM((1, LANES), f32)],
        compiler_params=_params(("arbitrary",)),
    )(dcum_q, dcum_k, zfg, bfg)


ATT_TILE = 512


ATT_ROWS = 32


def _causal_rows(r, transposed):
    rr = lax.broadcasted_iota(jnp.int32, (ATT_ROWS, ATT_TILE), 0) + r * ATT_ROWS
    cc = lax.broadcasted_iota(jnp.int32, (ATT_ROWS, ATT_TILE), 1)
    return (cc >= rr) if transposed else (rr >= cc)


def _causal(i, j, transposed):
    r = lax.broadcasted_iota(jnp.int32, (ATT_TILE, ATT_TILE), 0)
    c = lax.broadcasted_iota(jnp.int32, (ATT_TILE, ATT_TILE), 1)
    if transposed:
        return (c + i * ATT_TILE) >= (r + j * ATT_TILE)
    return (r + i * ATT_TILE) >= (c + j * ATT_TILE)


def _attn_fwd(qkv, cum, cum_t, *, name, host=None):
    t = qkv.shape[0]
    n = t // ATT_TILE
    tq = ATT_TILE

    def body(q_ref, k_ref, v_ref, cq_ref, ck_ref, o_ref, lse_ref, acc, m_s, l_s, c_s, s_s, p_s):
        i = pl.program_id(0)
        j = pl.program_id(1)

        @pl.when(j == 0)
        def _():
            acc[...] = jnp.zeros_like(acc)
            m_s[...] = jnp.full_like(m_s, NEG_BIG)
            l_s[...] = jnp.zeros_like(l_s)

        def block(masked):
            for h in range(HEADS):
                hs = slice(HEAD_D * h, HEAD_D * (h + 1))
                s_s[...] = lax.dot_general(q_ref[:, hs] * ATT_SCALE, k_ref[:, hs], _NT, preferred_element_type=f32)
                ck = ck_ref[h:h + 1, :]

                def rows_chunk(r, carry):
                    rows = pl.ds(pl.multiple_of(r * ATT_ROWS, ATT_ROWS), ATT_ROWS)
                    s = s_s[rows, :] + (cq_ref[rows, h:h + 1] - ck)
                    if masked:
                        s = jnp.where(_causal_rows(r, False), s, NEG_BIG)
                    m_old = m_s[rows, h:h + 1]
                    m_new = jnp.maximum(m_old, jnp.max(s, axis=-1, keepdims=True))
                    corr = jnp.exp(m_old - m_new)
                    p = jnp.exp(s - m_new)
                    l_s[rows, h:h + 1] = corr * l_s[rows, h:h + 1] + jnp.sum(p, axis=-1, keepdims=True)
                    m_s[rows, h:h + 1] = m_new
                    c_s[rows, h:h + 1] = corr
                    p_s[rows, :] = p.astype(bf16)
                    return carry

                lax.fori_loop(0, tq // ATT_ROWS, rows_chunk, 0, unroll=4)
                acc[:, hs] = c_s[:, h:h + 1] * acc[:, hs] + jnp.dot(p_s[...], v_ref[:, hs],
                                                                   preferred_element_type=f32)

        @pl.when(j < i)
        def _():
            block(False)

        @pl.when(j == i)
        def _():
            block(True)
            lse_ref[...] = jnp.zeros_like(lse_ref)
            for h in range(HEADS):
                hs = slice(HEAD_D * h, HEAD_D * (h + 1))
                l = l_s[:, h:h + 1]
                o_ref[:, hs] = acc[:, hs] / l
                lse_ref[:, h:h + 1] = m_s[:, h:h + 1] + jnp.log(l)

    return _hosted_call(
        host, body, name=name, grid=(n, n),
        in_specs=[pl.BlockSpec((tq, FOX_W), lambda i, j: (i, 0)),
                  pl.BlockSpec((tq, FOX_W), lambda i, j: (jnp.minimum(i, j), 1)),
                  pl.BlockSpec((tq, FOX_W), lambda i, j: (jnp.minimum(i, j), 2)),
                  pl.BlockSpec((tq, LANES), lambda i, j: (i, 0)),
                  pl.BlockSpec((HEADS, tq), lambda i, j: (0, jnp.minimum(i, j)))],
        out_specs=[pl.BlockSpec((tq, FOX_W), lambda i, j: (i, 0)), pl.BlockSpec((tq, LANES), lambda i, j: (i, 0))],
        out_shape=[jax.ShapeDtypeStruct((t, FOX_W), f32), jax.ShapeDtypeStruct((t, LANES), f32)],
        scratch_shapes=[pltpu.VMEM((tq, FOX_W), f32), pltpu.VMEM((tq, LANES), f32), pltpu.VMEM((tq, LANES), f32),
                        pltpu.VMEM((tq, LANES), f32), pltpu.VMEM((tq, tq), f32), pltpu.VMEM((tq, tq), bf16)],
        compiler_params=_params(("arbitrary", "arbitrary")),
    )(qkv, qkv, qkv, cum, cum_t)


def _attn_delta(dmix, o, *, tm, name):
    t = o.shape[0]

    def body(do_ref, o_ref, d_ref):
        r = lax.broadcasted_iota(jnp.int32, (FOX_W, LANES), 0)
        c = lax.broadcasted_iota(jnp.int32, (FOX_W, LANES), 1)
        pick = jnp.where(r // HEAD_D == c, 1.0, 0.0).astype(f32)
        d_ref[...] = jnp.dot(do_ref[...] * o_ref[...], pick, preferred_element_type=f32,
                             precision=lax.Precision.HIGHEST)

    blk = pl.BlockSpec((tm, FOX_W), lambda i: (i, 0))
    return pl.pallas_call(
        body, name=name, grid=(t // tm,), in_specs=[blk, blk],
        out_specs=pl.BlockSpec((tm, LANES), lambda i: (i, 0)),
        out_shape=jax.ShapeDtypeStruct((t, LANES), f32), compiler_params=_params(("arbitrary",)),
    )(dmix, o)


def _attn_dq(qkv, dmix, cum, cum_t, lse, delta, *, name, host=None):
    t = qkv.shape[0]
    n = t // ATT_TILE
    tq = ATT_TILE

    def body(q_ref, k_ref, v_ref, do_ref, cq_ref, ck_ref, lse_ref, dl_ref, dq_ref, dc_ref, acc, dc_acc):
        i = pl.program_id(0)
        j = pl.program_id(1)

        @pl.when(j == 0)
        def _():
            acc[...] = jnp.zeros_like(acc)
            dc_acc[...] = jnp.zeros_like(dc_acc)

        def block(masked):
            mask = _causal(i, j, False) if masked else None
            for h in range(HEADS):
                hs = slice(HEAD_D * h, HEAD_D * (h + 1))
                kh = k_ref[:, hs]
                s = lax.dot_general(q_ref[:, hs] * ATT_SCALE, kh, _NT, preferred_element_type=f32)
                s = s + cq_ref[:, h:h + 1] - ck_ref[h:h + 1, :]
                if masked:
                    s = jnp.where(mask, s, NEG_BIG)
                p = jnp.exp(s - lse_ref[:, h:h + 1])
                dp = lax.dot_general(do_ref[:, hs].astype(bf16), v_ref[:, hs], _NT, preferred_element_type=f32)
                ds = p * (dp - dl_ref[:, h:h + 1])
                acc[:, hs] += jnp.dot(ds.astype(bf16), kh, preferred_element_type=f32)
                dc_acc[:, h:h + 1] += jnp.sum(ds, axis=-1, keepdims=True)

        @pl.when(j < i)
        def _():
            block(False)

        @pl.when(j == i)
        def _():
            block(True)
            dq_ref[...] = (acc[...] * ATT_SCALE).astype(bf16)
            dc_ref[...] = dc_acc[...]

    col = pl.BlockSpec((tq, LANES), lambda i, j: (i, 0))
    return _hosted_call(
        host, body, name=name, grid=(n, n),
        in_specs=[pl.BlockSpec((tq, FOX_W), lambda i, j: (i, 0)),
                  pl.BlockSpec((tq, FOX_W), lambda i, j: (jnp.minimum(i, j), 1)),
                  pl.BlockSpec((tq, FOX_W), lambda i, j: (jnp.minimum(i, j), 2)),
                  pl.BlockSpec((tq, FOX_W), lambda i, j: (i, 0)),
                  col, pl.BlockSpec((HEADS, tq), lambda i, j: (0, jnp.minimum(i, j))), col, col],
        out_specs=[pl.BlockSpec((tq, FOX_W), lambda i, j: (i, 0)), col],
        out_shape=[jax.ShapeDtypeStruct((t, FOX_W), bf16), jax.ShapeDtypeStruct((t, LANES), f32)],
        scratch_shapes=[pltpu.VMEM((tq, FOX_W), f32), pltpu.VMEM((tq, LANES), f32)],
        compiler_params=_params(("arbitrary", "arbitrary")),
    )(qkv, qkv, qkv, dmix, cum, cum_t, lse, delta)


def _attn_dkv(qkv, dmix, cum, cum_t, lse_t, delta_t, *, name):
    t = qkv.shape[0]
    n = t // ATT_TILE
    tk = ATT_TILE

    def body(q_ref, k_ref, v_ref, do_ref, cq_ref, ck_ref, lse_ref, dl_ref, dk_ref, dv_ref, dc_ref, dk_acc, dv_acc, dc_acc):
        j = pl.program_id(0)
        i = pl.program_id(1)

        @pl.when(i == 0)
        def _():
            dk_acc[...] = jnp.zeros_like(dk_acc)
            dv_acc[...] = jnp.zeros_like(dv_acc)
            dc_acc[...] = jnp.zeros_like(dc_acc)

        def block(masked):
            mask = _causal(i, j, True) if masked else None
            for h in range(HEADS):
                hs = slice(HEAD_D * h, HEAD_D * (h + 1))
                qh = q_ref[:, hs]
                doh = do_ref[:, hs].astype(bf16)
                s_t = lax.dot_general(k_ref[:, hs] * ATT_SCALE, qh, _NT, preferred_element_type=f32)
                s_t = s_t + cq_ref[h:h + 1, :] - ck_ref[:, h:h + 1]
                if masked:
                    s_t = jnp.where(mask, s_t, NEG_BIG)
                p_t = jnp.exp(s_t - lse_ref[h:h + 1, :])
                dv_acc[:, hs] += jnp.dot(p_t.astype(bf16), doh, preferred_element_type=f32)
                dp_t = lax.dot_general(v_ref[:, hs], doh, _NT, preferred_element_type=f32)
                ds_t = p_t * (dp_t - dl_ref[h:h + 1, :])
                dk_acc[:, hs] += jnp.dot(ds_t.astype(bf16), qh, preferred_element_type=f32)
                dc_acc[:, h:h + 1] -= jnp.sum(ds_t, axis=-1, keepdims=True)

        @pl.when(i > j)
        def _():
            block(False)

        @pl.when(i == j)
        def _():
            block(True)

        @pl.when(i == n - 1)
        def _():
            dk_ref[...] = (dk_acc[...] * ATT_SCALE).astype(bf16)
            dv_ref[...] = dv_acc[...].astype(bf16)
            dc_ref[...] = dc_acc[...]

    rowq = pl.BlockSpec((HEADS, tk), lambda j, i: (0, jnp.maximum(i, j)))
    return pl.pallas_call(
        body, name=name, grid=(n, n),
        in_specs=[pl.BlockSpec((tk, FOX_W), lambda j, i: (jnp.maximum(i, j), 0)),
                  pl.BlockSpec((tk, FOX_W), lambda j, i: (j, 1)),
                  pl.BlockSpec((tk, FOX_W), lambda j, i: (j, 2)),
                  pl.BlockSpec((tk, FOX_W), lambda j, i: (jnp.maximum(i, j), 0)),
                  rowq, pl.BlockSpec((tk, LANES), lambda j, i: (j, 0)), rowq, rowq],
        out_specs=[pl.BlockSpec((tk, FOX_W), lambda j, i: (j, 0)), pl.BlockSpec((tk, FOX_W), lambda j, i: (j, 0)),
                   pl.BlockSpec((tk, LANES), lambda j, i: (j, 0))],
        out_shape=[jax.ShapeDtypeStruct((t, FOX_W), bf16), jax.ShapeDtypeStruct((t, FOX_W), bf16),
                   jax.ShapeDtypeStruct((t, LANES), f32)],
        scratch_shapes=[pltpu.VMEM((tk, FOX_W), f32), pltpu.VMEM((tk, FOX_W), f32), pltpu.VMEM((tk, LANES), f32)],
        compiler_params=_params(("arbitrary", "arbitrary")),
    )(qkv, qkv, qkv, dmix, cum_t, cum, lse_t, delta_t)


ATT_W = HEADS * LANES


def _data_lane(h):
    return HEAD_D * (h % 2)


def _extra_lane(h):
    return HEAD_D - _data_lane(h)


def _split3(x):
    hi = x.astype(bf16)
    rest = x - hi.astype(f32)
    mid = rest.astype(bf16)
    lo = (rest - mid.astype(f32)).astype(bf16)
    return hi, mid, lo


def _augment(pair, h, first, second, fill=0.0):
    rows = pair.shape[0]
    lane = lax.broadcasted_iota(jnp.int32, (rows, LANES), 1)
    base = _extra_lane(h)
    own = (lane < HEAD_D) if h % 2 == 0 else (lane >= HEAD_D)
    out = jnp.where(own, pair, jnp.full((rows, LANES), fill, bf16))
    for off, src in ((0, first), (3, second)):
        for q in range(3):
            val = src[q] if isinstance(src, tuple) else jnp.full((rows, 1), src, bf16)
            out = jnp.where(lane == base + off + q, val, out)
    return out


def _attn_prep_fwd(qkv, cum, *, tm, name):
    t = qkv.shape[0]

    def body(q_ref, k_ref, v_ref, c_ref, qa_ref, ka_ref, va_ref):
        for h in range(HEADS):
            pair = slice(LANES * (h // 2), LANES * (h // 2 + 1))
            hs = slice(LANES * h, LANES * (h + 1))
            c3 = _split3(c_ref[:, h:h + 1])
            qa_ref[:, hs] = _augment(q_ref[:, pair] * ATT_SCALE, h, c3, 1.0)
            ka_ref[:, hs] = _augment(k_ref[:, pair], h, 1.0, tuple(-p for p in c3))
            va_ref[:, hs] = _augment(v_ref[:, pair], h, 1.0, 1.0, fill=1.0)

    wide = pl.BlockSpec((tm, ATT_W), lambda i: (i, 0))
    out = jax.ShapeDtypeStruct((t, ATT_W), bf16)
    return pl.pallas_call(
        body, name=name, grid=(t // tm,),
        in_specs=[pl.BlockSpec((tm, FOX_W), lambda i: (i, 0)), pl.BlockSpec((tm, FOX_W), lambda i: (i, 1)),
                  pl.BlockSpec((tm, FOX_W), lambda i: (i, 2)), pl.BlockSpec((tm, LANES), lambda i: (i, 0))],
        out_specs=[wide] * 3, out_shape=[out] * 3, compiler_params=_params(("arbitrary",)),
    )(qkv, qkv, qkv, cum)


def _attn_prep_bwd(qkv, cum, lse, dmix, o, *, tm, name):
    t = qkv.shape[0]

    def body(q_ref, c_ref, l_ref, do_ref, o_ref, qa_ref, da_ref):
        for h in range(HEADS):
            pair = slice(LANES * (h // 2), LANES * (h // 2 + 1))
            src = slice(HEAD_D * h, HEAD_D * (h + 1))
            hs = slice(LANES * h, LANES * (h + 1))
            delta = jnp.sum(do_ref[:, src] * o_ref[:, src], axis=-1, keepdims=True)
            qa_ref[:, hs] = _augment(q_ref[:, pair] * ATT_SCALE, h,
                                     _split3(c_ref[:, h:h + 1] - l_ref[:, h:h + 1]), 1.0)
            da_ref[:, hs] = _augment(do_ref[:, pair].astype(bf16), h, tuple(-p for p in _split3(delta)), 0.0)

    wide = pl.BlockSpec((tm, ATT_W), lambda i: (i, 0))
    half = pl.BlockSpec((tm, FOX_W), lambda i: (i, 0))
    col = pl.BlockSpec((tm, LANES), lambda i: (i, 0))
    out = jax.ShapeDtypeStruct((t, ATT_W), bf16)
    return pl.pallas_call(
        body, name=name, grid=(t // tm,), in_specs=[half, col, col, half, half],
        out_specs=[wide] * 2, out_shape=[out] * 2, compiler_params=_params(("arbitrary",)),
    )(qkv, cum, lse, dmix, o)


def _attn_fwd2(q_aug, k_aug, v_aug, *, name, host=None):
    t = q_aug.shape[0]
    n = t // ATT_TILE
    tq = ATT_TILE

    def body(q_ref, k_ref, v_ref, o_ref, lse_ref, acc, m_s):
        i = pl.program_id(0)
        j = pl.program_id(1)

        @pl.when(j == 0)
        def _():
            acc[...] = jnp.zeros_like(acc)
            m_s[...] = jnp.full_like(m_s, NEG_BIG)

        def block(masked):
            mask = _causal(i, j, False) if masked else None
            for h in range(HEADS):
                hs = slice(LANES * h, LANES * (h + 1))
                s = lax.dot_general(q_ref[:, hs], k_ref[:, hs], _NT, preferred_element_type=f32)
                if masked:
                    s = jnp.where(mask, s, NEG_BIG)
                m_old = m_s[:, h:h + 1]
                m_new = jnp.maximum(m_old, jnp.max(s, axis=-1, keepdims=True))
                p = jnp.exp(s - m_new).astype(bf16)
                acc[h] = jnp.exp(m_old - m_new) * acc[h] + jnp.dot(p, v_ref[:, hs], preferred_element_type=f32)
                m_s[:, h:h + 1] = m_new

        @pl.when(j < i)
        def _():
            block(False)

        @pl.when(j == i)
        def _():
            block(True)
            lse_ref[...] = jnp.zeros_like(lse_ref)
            for h in range(HEADS):
                a = acc[h]
                l = a[:, _extra_lane(h):_extra_lane(h) + 1]
                o_ref[:, HEAD_D * h:HEAD_D * (h + 1)] = a[:, _data_lane(h):_data_lane(h) + HEAD_D] / l
                lse_ref[:, h:h + 1] = m_s[:, h:h + 1] + jnp.log(l)

    kv = pl.BlockSpec((tq, ATT_W), lambda i, j: (jnp.minimum(i, j), 0))
    return _hosted_call(
        host, body, name=name, grid=(n, n),
        in_specs=[pl.BlockSpec((tq, ATT_W), lambda i, j: (i, 0)), kv, kv],
        out_specs=[pl.BlockSpec((tq, FOX_W), lambda i, j: (i, 0)), pl.BlockSpec((tq, LANES), lambda i, j: (i, 0))],
        out_shape=[jax.ShapeDtypeStruct((t, FOX_W), f32), jax.ShapeDtypeStruct((t, LANES), f32)],
        scratch_shapes=[pltpu.VMEM((HEADS, tq, LANES), f32), pltpu.VMEM((tq, LANES), f32)],
        compiler_params=_params(("arbitrary", "arbitrary")),
    )(q_aug, k_aug, v_aug)


def _attn_dq2(qb_aug, k_aug, v_aug, do_aug, *, name, host=None):
    t = qb_aug.shape[0]
    n = t // ATT_TILE
    tq = ATT_TILE

    def body(q_ref, k_ref, v_ref, do_ref, dq_ref, dc_ref, acc):
        i = pl.program_id(0)
        j = pl.program_id(1)

        @pl.when(j == 0)
        def _():
            acc[...] = jnp.zeros_like(acc)

        def block(masked):
            mask = _causal(i, j, False) if masked else None
            for h in range(HEADS):
                hs = slice(LANES * h, LANES * (h + 1))
                kh = k_ref[:, hs]
                s = lax.dot_general(q_ref[:, hs], kh, _NT, preferred_element_type=f32)
                if masked:
                    s = jnp.where(mask, s, NEG_BIG)
                dp = lax.dot_general(do_ref[:, hs], v_ref[:, hs], _NT, preferred_element_type=f32)
                ds = (jnp.exp(s) * dp).astype(bf16)
                acc[h] += jnp.dot(ds, kh, preferred_element_type=f32)

        @pl.when(j < i)
        def _():
            block(False)

        @pl.when(j == i)
        def _():
            block(True)
            dc_ref[...] = jnp.zeros_like(dc_ref)
            for h in range(HEADS):
                a = acc[h]
                dq_ref[:, HEAD_D * h:HEAD_D * (h + 1)] = (
                    a[:, _data_lane(h):_data_lane(h) + HEAD_D] * ATT_SCALE).astype(bf16)
                dc_ref[:, h:h + 1] = a[:, _extra_lane(h):_extra_lane(h) + 1]

    own = pl.BlockSpec((tq, ATT_W), lambda i, j: (i, 0))
    kv = pl.BlockSpec((tq, ATT_W), lambda i, j: (jnp.minimum(i, j), 0))
    return _hosted_call(
        host, body, name=name, grid=(n, n), in_specs=[own, kv, kv, own],
        out_specs=[pl.BlockSpec((tq, FOX_W), lambda i, j: (i, 0)), pl.BlockSpec((tq, LANES), lambda i, j: (i, 0))],
        out_shape=[jax.ShapeDtypeStruct((t, FOX_W), bf16), jax.ShapeDtypeStruct((t, LANES), f32)],
        scratch_shapes=[pltpu.VMEM((HEADS, tq, LANES), f32)],
        compiler_params=_params(("arbitrary", "arbitrary")),
    )(qb_aug, k_aug, v_aug, do_aug)


def _attn_dkv2(qb_aug, k_aug, v_aug, do_aug, *, name, host=None):
    t = qb_aug.shape[0]
    n = t // ATT_TILE
    tk = ATT_TILE

    def body(q_ref, k_ref, v_ref, do_ref, dk_ref, dv_ref, dc_ref, dk_acc, dv_acc):
        j = pl.program_id(0)
        i = pl.program_id(1)

        @pl.when(i == 0)
        def _():
            dk_acc[...] = jnp.zeros_like(dk_acc)
            dv_acc[...] = jnp.zeros_like(dv_acc)

        def block(masked):
            mask = _causal(i, j, True) if masked else None
            for h in range(HEADS):
                hs = slice(LANES * h, LANES * (h + 1))
                qh = q_ref[:, hs]
                doh = do_ref[:, hs]
                s_t = lax.dot_general(k_ref[:, hs], qh, _NT, preferred_element_type=f32)
                if masked:
                    s_t = jnp.where(mask, s_t, NEG_BIG)
                p_t = jnp.exp(s_t)
                dv_acc[h] += jnp.dot(p_t.astype(bf16), doh, preferred_element_type=f32)
                dp_t = lax.dot_general(v_ref[:, hs], doh, _NT, preferred_element_type=f32)
                dk_acc[h] += jnp.dot((p_t * dp_t).astype(bf16), qh, preferred_element_type=f32)

        @pl.when(i > j)
        def _():
            block(False)

        @pl.when(i == j)
        def _():
            block(True)

        @pl.when(i == n - 1)
        def _():
            dc_ref[...] = jnp.zeros_like(dc_ref)
            for h in range(HEADS):
                a = dk_acc[h]
                cols = slice(_data_lane(h), _data_lane(h) + HEAD_D)
                dk_ref[:, HEAD_D * h:HEAD_D * (h + 1)] = a[:, cols].astype(bf16)
                dv_ref[:, HEAD_D * h:HEAD_D * (h + 1)] = dv_acc[h][:, cols].astype(bf16)
                dc_ref[:, h:h + 1] = -a[:, _extra_lane(h) + 3:_extra_lane(h) + 4]

    own = pl.BlockSpec((tk, ATT_W), lambda j, i: (j, 0))
    qs = pl.BlockSpec((tk, ATT_W), lambda j, i: (jnp.maximum(i, j), 0))
    half = pl.BlockSpec((tk, FOX_W), lambda j, i: (j, 0))
    return _hosted_call(
        host, body, name=name, grid=(n, n), in_specs=[qs, own, own, qs],
        out_specs=[half, half, pl.BlockSpec((tk, LANES), lambda j, i: (j, 0))],
        out_shape=[jax.ShapeDtypeStruct((t, FOX_W), bf16), jax.ShapeDtypeStruct((t, FOX_W), bf16),
                   jax.ShapeDtypeStruct((t, LANES), f32)],
        scratch_shapes=[pltpu.VMEM((HEADS, tk, LANES), f32), pltpu.VMEM((HEADS, tk, LANES), f32)],
        compiler_params=_params(("arbitrary", "arbitrary")),
    )(qb_aug, k_aug, v_aug, do_aug)


LRU_CHUNK = 64
SUB = 8


def _row_ids(n):
    return lax.broadcasted_iota(jnp.int32, (n, LANES), 0)


def _shift_rows_down(ext, s):
    return pltpu.roll(ext, s, axis=0)[SUB:, :]


def _shift_rows_up(ext, s, n):
    return pltpu.roll(ext, ext.shape[0] - s, axis=0)[:n, :]


def _lru_gates(u, wa_ref, ba_ref, wx_ref, bx_ref, sp):
    ub = u.astype(bf16)
    r = _sigmoid(jnp.dot(ub, wa_ref[...], preferred_element_type=f32) + ba_ref[...])
    gi = _sigmoid(jnp.dot(ub, wx_ref[...], preferred_element_type=f32) + bx_ref[...])
    log_a = -LRU_C * r * sp
    a = jnp.exp(log_a)
    s = jnp.sqrt(_one_minus_exp(2.0 * log_a))
    return r, gi, a, s


def _conv_window(lx_ref, r0, ci):
    cur = lx_ref[pl.ds(r0, LRU_CHUNK), :]
    p0 = pl.multiple_of(jnp.maximum(r0 - SUB, 0), SUB)
    prev = jnp.where(ci > 0, lx_ref[pl.ds(p0, SUB), :], 0.0)
    return cur, jnp.concatenate([prev, cur], axis=0)


def _lru_fwd(zl, conv_w, conv_b, wa, ba, wx, bx, lam, *, name):
    t = zl.shape[0]
    n_chunk = t // LRU_CHUNK

    def body(lx_ref, lg_ref, cw_ref, cb_ref, wa_ref, ba_ref, wx_ref, bx_ref, lam_ref, u_ref, h_ref, y_ref):
        sp = _softplus(-lam_ref[...])
        rows = _row_ids(SUB)

        def chunk(ci, hc):
            r0 = pl.multiple_of(ci * LRU_CHUNK, LRU_CHUNK)
            cur, ext = _conv_window(lx_ref, r0, ci)
            u = cb_ref[...] + cw_ref[3:4, :] * cur
            for k in range(3):
                u = u + cw_ref[k:k + 1, :] * _shift_rows_down(ext, 3 - k)
            r, gi, a, s = _lru_gates(u, wa_ref, ba_ref, wx_ref, bx_ref, sp)
            b = s * (gi * u)
            tiles = []
            for q in range(LRU_CHUNK // SUB):
                ta = a[SUB * q:SUB * (q + 1), :]
                tb = b[SUB * q:SUB * (q + 1), :]
                for d in (1, 2, 4):
                    a_sh = jnp.where(rows >= d, pltpu.roll(ta, d, axis=0), 1.0)
                    b_sh = jnp.where(rows >= d, pltpu.roll(tb, d, axis=0), 0.0)
                    tb = ta * b_sh + tb
                    ta = ta * a_sh
                hq = tb + ta * hc
                hc = hq[SUB - 1:SUB, :]
                tiles.append(hq)
            h = jnp.concatenate(tiles, axis=0)
            u_ref[pl.ds(r0, LRU_CHUNK), :] = u
            h_ref[pl.ds(r0, LRU_CHUNK), :] = h
            gel, _ = _gelu_and_grad(lg_ref[pl.ds(r0, LRU_CHUNK), :])
            y_ref[pl.ds(r0, LRU_CHUNK), :] = gel * h
            return hc

        lax.fori_loop(0, n_chunk, chunk, jnp.zeros((1, LANES), f32))

    seq = lambda cb: pl.BlockSpec((t, LANES), lambda c, cb=cb: (0, c + cb))
    rowc = pl.BlockSpec((1, LANES), lambda c: (0, c))
    diag = pl.BlockSpec((LANES, LANES), lambda c: (c, c))
    out = jax.ShapeDtypeStruct((t, LRU_W), f32)
    return pl.pallas_call(
        body, name=name, grid=(LRU_W // LANES,),
        in_specs=[seq(0), seq(4), pl.BlockSpec((4, LANES), lambda c: (0, c)), rowc, diag, rowc, diag, rowc, rowc],
        out_specs=[seq(0)] * 3, out_shape=[out] * 3,
        compiler_params=_params(("arbitrary",)),
    )(zl, zl, conv_w, conv_b, wa, ba, wx, bx, lam)


def _lru_bwd(dmix, zl, u_all, h_all, conv_w, wa, ba, wx, bx, lam, *, name, host=None):
    t = zl.shape[0]
    n_chunk = t // LRU_CHUNK

    def body(dy_ref, lx_ref, lg_ref, u_ref, h_ref, cw_ref, wa_ref, ba_ref, wx_ref, bx_ref, lam_ref,
             dlx_ref, dlg_ref, dcw_ref, dcb_ref, dba_ref, dbx_ref, dlam_ref, dwa_ref, dwx_ref, dpr_s, dpx_s):
        lam_v = lam_ref[...]
        sp = _softplus(-lam_v)
        rows = _row_ids(SUB)
        rows_c = _row_ids(LRU_CHUNK)
        zero_row = jnp.zeros((1, LANES), f32)

        def chunk(step, carry):
            dh_c, a_next0, du_next, dsp, dba, dbx, dcb, dw0, dw1, dw2, dw3 = carry
            ci = n_chunk - 1 - step
            r0 = pl.multiple_of(ci * LRU_CHUNK, LRU_CHUNK)
            sl = pl.ds(r0, LRU_CHUNK)
            u = u_ref[sl, :]
            r, gi, a, s = _lru_gates(u, wa_ref, ba_ref, wx_ref, bx_ref, sp)
            h = h_ref[sl, :]
            p0 = pl.multiple_of(jnp.maximum(r0 - SUB, 0), SUB)
            h_before = jnp.where(ci > 0, h_ref[pl.ds(p0, SUB), :], 0.0)[SUB - 1:SUB, :]
            h_prev = jnp.where(rows_c == 0, h_before, pltpu.roll(h, 1, axis=0))
            gel, dgel = _gelu_and_grad(lg_ref[sl, :])
            dy = dy_ref[sl, :]
            dlg_ref[sl, :] = (dy * h * dgel).astype(bf16)
            g_in = dy * gel
            a_next = jnp.where(rows_c == LRU_CHUNK - 1, a_next0, pltpu.roll(a, LRU_CHUNK - 1, axis=0))
            tiles = [None] * (LRU_CHUNK // SUB)
            for q in reversed(range(LRU_CHUNK // SUB)):
                ta = a_next[SUB * q:SUB * (q + 1), :]
                tb = g_in[SUB * q:SUB * (q + 1), :]
                for d in (1, 2, 4):
                    a_sh = jnp.where(rows < SUB - d, pltpu.roll(ta, SUB - d, axis=0), 1.0)
                    b_sh = jnp.where(rows < SUB - d, pltpu.roll(tb, SUB - d, axis=0), 0.0)
                    tb = ta * b_sh + tb
                    ta = ta * a_sh
                dhq = tb + ta * dh_c
                dh_c = dhq[0:1, :]
                tiles[q] = dhq
            dh = jnp.concatenate(tiles, axis=0)
            da = dh * h_prev
            ds = dh * gi * u
            dgi = dh * s * u
            du = dh * s * gi
            dlog_a = da * a - ds * (a * a) / s
            dr = dlog_a * (-LRU_C * sp)
            dsp = dsp + jnp.sum(dlog_a * (-LRU_C * r), axis=0, keepdims=True)
            dpr = dr * r * (1.0 - r)
            dpx = dgi * gi * (1.0 - gi)
            dprb = dpr.astype(bf16)
            dpxb = dpx.astype(bf16)
            dpr_s[sl, :] = dprb
            dpx_s[sl, :] = dpxb
            du = du + (lax.dot_general(dprb, wa_ref[...], _NT, preferred_element_type=f32)
                       + lax.dot_general(dpxb, wx_ref[...], _NT, preferred_element_type=f32))
            dba = dba + jnp.sum(dpr, axis=0, keepdims=True)
            dbx = dbx + jnp.sum(dpx, axis=0, keepdims=True)
            dcb = dcb + jnp.sum(du, axis=0, keepdims=True)
            du_ext = jnp.concatenate([du, du_next], axis=0)
            dlx = cw_ref[3:4, :] * du
            for k in range(3):
                dlx = dlx + cw_ref[k:k + 1, :] * _shift_rows_up(du_ext, 3 - k, LRU_CHUNK)
            dlx_ref[sl, :] = dlx.astype(bf16)
            cur, ext = _conv_window(lx_ref, r0, ci)
            dws = [dw0, dw1, dw2, dw3 + jnp.sum(du * cur, axis=0, keepdims=True)]
            for k in range(3):
                dws[k] = dws[k] + jnp.sum(du * _shift_rows_down(ext, 3 - k), axis=0, keepdims=True)
            return (dh_c, a[0:1, :], du[0:SUB, :], dsp, dba, dbx, dcb, dws[0], dws[1], dws[2], dws[3])

        init = (zero_row, zero_row, jnp.zeros((SUB, LANES), f32)) + (zero_row,) * 8
        out = lax.fori_loop(0, n_chunk, chunk, init)
        _, _, _, dsp, dba, dbx, dcb, dw0, dw1, dw2, dw3 = out
        dlam_ref[...] = dsp * (-_sigmoid(-lam_v))
        dba_ref[...] = dba
        dbx_ref[...] = dbx
        dcb_ref[...] = dcb
        dcw_ref[...] = jnp.concatenate([dw0, dw1, dw2, dw3], axis=0)
        ub = u_ref[...].astype(bf16)
        dwa_ref[...] = lax.dot_general(ub, dpr_s[...], _TN, preferred_element_type=f32)
        dwx_ref[...] = lax.dot_general(ub, dpx_s[...], _TN, preferred_element_type=f32)

    seq = lambda cb: pl.BlockSpec((t, LANES), lambda c, cb=cb: (0, c + cb))
    rowc = pl.BlockSpec((1, LANES), lambda c: (0, c))
    diag = pl.BlockSpec((LANES, LANES), lambda c: (c, c))
    gate_out = pl.BlockSpec((None, LANES, LANES), lambda c: (c, 0, 0))
    row_shape = jax.ShapeDtypeStruct((1, LRU_W), f32)
    return _hosted_call(
        host, body, name=name, grid=(LRU_W // LANES,),
        in_specs=[seq(4), seq(0), seq(4), seq(0), seq(0), pl.BlockSpec((4, LANES), lambda c: (0, c)),
                  diag, rowc, diag, rowc, rowc],
        out_specs=[seq(0), seq(0), pl.BlockSpec((4, LANES), lambda c: (0, c)), rowc, rowc, rowc, rowc,
                   gate_out, gate_out],
        out_shape=[jax.ShapeDtypeStruct((t, LRU_W), bf16)] * 2
        + [jax.ShapeDtypeStruct((4, LRU_W), f32)] + [row_shape] * 4
        + [jax.ShapeDtypeStruct((LRU_W // LANES, LANES, LANES), f32)] * 2,
        scratch_shapes=[pltpu.VMEM((t, LANES), bf16), pltpu.VMEM((t, LANES), bf16)],
        compiler_params=_params(("arbitrary",)),
    )(dmix, zl, zl, u_all, h_all, conv_w, wa, ba, wx, bx, lam)


def _block_diag(w):
    eye = jnp.eye(HEADS, dtype=w.dtype)
    return jnp.einsum("hij,hk->hikj", w, eye).reshape(LRU_W, LRU_W)


def _diag_blocks(dw):
    top = dw[:, :HEAD_D, :HEAD_D]
    bot = dw[:, HEAD_D:, HEAD_D:]
    return jnp.stack([top, bot], axis=1).reshape(HEADS, HEAD_D, HEAD_D)


def _local_step(x, target, sent, small, *, tm=512):
    t = x.shape[0]
    ones = jnp.ones((1, D_MODEL), f32)
    zeros = jnp.zeros((1, D_MODEL), f32)
    ln1 = (small["ln1_g"], small["ln1_b"])
    ln2 = (small["ln2_g"], small["ln2_b"])
    ln3 = (small["ln3_g"], small["ln3_b"])

    wg1, wu1, wd1 = _exchange([sent["ffn1_w_gate"], sent["ffn1_w_up"], sent["ffn1_w_down"]], gather=True,
                              name="gather_ffn1")
    xh1, rs1, hg1, hu1, w_in_g, w_out_g, conv_w_g = _ffn_fwd(
        x, ones, zeros, wg1, wu1, wd1, tm=tm, name="ffn1_fwd",
        host=_Exchange([sent["w_in"], sent["w_out"], sent["conv_w"]], gather=True))
    w_in = jnp.pad(w_in_g.reshape(N_DEV, D_MODEL, IN_SHARD).transpose(1, 0, 2).reshape(D_MODEL, IN_COLS),
                   ((0, 0), (0, 21 * LANES - IN_COLS)))
    w_out = w_out_g.reshape(D_MODEL, D_MODEL)
    conv_w = conv_w_g.transpose(1, 0, 2).reshape(4, LRU_W)
    qkv = _mm(xh1, w_in, mode="nn", out_dtype=bf16, tm=tm, tn=512, tk=D_MODEL, name="qkv_fwd",
              affine=ln1, b_cols=(0, 1536))
    zl = _mm(xh1, w_in, mode="nn", out_dtype=f32, tm=tm, tn=512, tk=D_MODEL, name="zl_fwd",
             affine=ln1, b_cols=(3, 1024))
    zfg = _mm(xh1, w_in, mode="nn", out_dtype=f32, tm=tm, tn=LANES, tk=D_MODEL, name="zfg_fwd",
              affine=ln1, b_cols=(20, LANES))
    bfg = jnp.pad(small["b_forget"], ((0, 0), (0, LANES - HEADS)))
    cum = _cum_fwd(zfg, bfg, name="cum_fwd")
    q_aug, k_aug, v_aug = _attn_prep_fwd(qkv, cum, tm=tm, name="attn_prep_fwd")
    o, lse, wg2, wu2, wd2 = _attn_fwd2(
        q_aug, k_aug, v_aug, name="attn_fwd",
        host=_Exchange([sent["ffn2_w_gate"], sent["ffn2_w_up"], sent["ffn2_w_down"]], gather=True))
    wa_bd = _block_diag(small["rg_wa"]).astype(bf16)
    wx_bd = _block_diag(small["rg_wx"]).astype(bf16)
    ba = small["rg_ba"].reshape(1, LRU_W)
    bx = small["rg_bx"].reshape(1, LRU_W)
    u, h, lru = _lru_fwd(zl, conv_w, small["conv_b"], wa_bd, ba, wx_bd, bx, small["lru_lambda"],
                         name="lru_fwd")
    xh2, rs2 = _mmln([(o, 0, FOX_W, w_out, 0, D_MODEL, "nn"), (lru, 0, LRU_W, w_out, 1, D_MODEL, "nn")],
                     tm=tm, name="mix_fwd", resid=("affine", xh1) + ln1, resid_scale=ALPHA, epi="ln_fwd")
    xh3, rs3, hg2, hu2 = _ffn_fwd(xh2, ln2[0], ln2[1], wg2, wu2, wd2, tm=tm, name="ffn2_fwd")

    dpre3, sq_rows, g_ln3g, g_ln3b = _loss_bwd(xh3, rs3, ln3[0], ln3[1], target, tm=tm, name="loss_bwd")
    dpre2, g_ln2g, g_ln2b, dhg2, dhu2, a2 = _ffn_bwd(dpre3, hg2, hu2, wg2, wu2, wd2,
                                                     (xh2, rs2, ln2[0]), tm=tm, name="ffn2_bwd")
    wgrad = dict(out_dtype=bf16, tm=D_MODEL, mb=1, tn=FF_TILE, nb=4, tk=512, out_blocked=True)
    wdgrad = dict(out_dtype=bf16, tm=512, mb=4, tn=D_MODEL, nb=1, tk=512, out_scale=0.5)
    g_wg2 = _mm_tn(xh2, dhg2, name="g_wg2", affine=ln2, **wgrad)
    g_wu2 = _mm_tn(xh2, dhu2, name="g_wu2", affine=ln2, **wgrad)
    g_wd2 = _mm_tn(a2, dpre3, name="g_wd2", **wdgrad)

    dmix = _mmln([(dpre2, 0, D_MODEL, w_out, 0, D_MODEL, "nt")], tm=tm, name="dmix_bwd")
    g_wout_a = _mm(o, dpre2, mode="tn", out_dtype=bf16, tm=512, tn=D_MODEL, tk=512, name="g_wout_fox")
    g_wout_b = _mm(lru, dpre2, mode="tn", out_dtype=bf16, tm=512, tn=D_MODEL, tk=512, name="g_wout_lru")
    dlx, dlg, g_cw, g_cb, g_ba, g_bx, g_lam, g_wa4, g_wx4, *p_wg2 = _lru_bwd(
        dmix, zl, u, h, conv_w, wa_bd, ba, wx_bd, bx, small["lru_lambda"], name="lru_bwd",
        host=_Exchange([g_wg2], gather=False))
    p_wg2 = p_wg2[0]
    qb_aug, do_aug = _attn_prep_bwd(qkv, cum, lse, dmix, o, tm=tm, name="attn_prep_bwd")
    dq, dcum_q, p_wu2 = _attn_dq2(qb_aug, k_aug, v_aug, do_aug, name="attn_dq",
                                  host=_Exchange([g_wu2], gather=False))
    g_wout_blocked = jnp.concatenate([g_wout_a, g_wout_b], axis=0).reshape(N_DEV, D_MODEL // N_DEV, D_MODEL)
    dk, dv, dcum_k, p_wd2, p_wout = _attn_dkv2(
        qb_aug, k_aug, v_aug, do_aug, name="attn_dkv",
        host=_Exchange([g_wd2.reshape(N_DEV, FF_TILE, D_MODEL), g_wout_blocked], gather=False))
    dfg, g_bf = _cum_bwd(dcum_q, dcum_k, zfg, bfg, name="cum_bwd")

    dz = [(dq, 0, 512), (dk, 1, 512), (dv, 2, 512), (dlx, 3, 512), (dlg, 4, 512), (dfg, 20, LANES)]
    dpre1, g_ln1g, g_ln1b = _mmln(
        [(arr, 0, w, w_in, cb, w, "nt") for (arr, cb, w) in dz],
        tm=tm, name="dx1_bwd", resid=("plain", dpre2), resid_scale=ALPHA, epi="ln_bwd", ln=(xh1, rs1, ln1[0]))
    g_win = [_mm(xh1, arr, mode="tn", out_dtype=bf16, tm=D_MODEL, tn=w, tk=512, name=f"g_win{n}", affine=ln1)
             for n, (arr, cb, w) in enumerate(dz)]
    g_win_full = jnp.concatenate([g[:, :w] for g, (_, _, w) in zip(g_win, dz)], axis=1)[:, :IN_COLS]
    g_win_blocked = g_win_full.reshape(D_MODEL, N_DEV, IN_SHARD).transpose(1, 0, 2).reshape(N_DEV, -1, LANES)
    dhg1, dhu1, a1, p_win = _ffn_bwd_act(dpre1, hg1, hu1, wd1, tm=tm, name="ffn1_bwd_act",
                                         host=_Exchange([g_win_blocked], gather=False))
    small_g = {
        "ln1_g": g_ln1g, "ln1_b": g_ln1b, "b_forget": g_bf[:, :HEADS], "conv_w": g_cw, "conv_b": g_cb,
        "rg_wa": _diag_blocks(g_wa4), "rg_ba": g_ba.reshape(HEADS, HEAD_D),
        "rg_wx": _diag_blocks(g_wx4), "rg_bx": g_bx.reshape(HEADS, HEAD_D), "lru_lambda": g_lam,
        "ln2_g": g_ln2g, "ln2_b": g_ln2b, "ln3_g": g_ln3g, "ln3_b": g_ln3b,
    }
    pieces = [small_g[n].reshape(-1) for n in PACKED]
    packed = jnp.concatenate(pieces + [jnp.zeros((PACK_ROWS * LANES - sum(p.shape[0] for p in pieces),), f32)])
    g_wg1, all_packed = _mm_tn(x, dhg1, name="g_wg1",
                               host=_Exchange([packed.reshape(PACK_ROWS, LANES)], gather=True), **wgrad)
    g_wu1, p_wg1 = _mm_tn(x, dhu1, name="g_wu1", host=_Exchange([g_wg1], gather=False), **wgrad)
    g_wd1, p_wu1 = _mm_tn(a1, dpre1, name="g_wd1", host=_Exchange([g_wu1], gather=False), **wdgrad)
    grad_x, p_wd1 = _ffn_bwd_dx(dpre1, dhg1, dhu1, wg1, wu1, tm=tm, name="ffn1_bwd_dx",
                                host=_Exchange([g_wd1.reshape(N_DEV, FF_TILE, D_MODEL)], gather=False))
    parts = {
        "ffn1_w_gate": p_wg1, "ffn1_w_up": p_wu1, "ffn1_w_down": p_wd1, "w_in": p_win, "w_out": p_wout,
        "ffn2_w_gate": p_wg2, "ffn2_w_up": p_wu2, "ffn2_w_down": p_wd2,
    }
    return sq_rows, grad_x, parts, all_packed, {n: small_g[n].shape for n in PACKED}


def _adam_math(w, g, m, v):
    m2 = ADAM_B1 * m + (1.0 - ADAM_B1) * g
    v2 = ADAM_B2 * v + (1.0 - ADAM_B2) * (g * g)
    m_hat = m2 / (1.0 - ADAM_B1 ** ADAM_STEP)
    v_hat = v2 / (1.0 - ADAM_B2 ** ADAM_STEP)
    delta = -ADAM_LR * (m_hat / (jnp.sqrt(v_hat) + ADAM_EPS) + ADAM_WD * w)
    return delta, m2, v2


ADAM_TILE_ELEMS = 128 * 1024


def _adamw_big(parts, w, m, v, *, name):
    r, c = w.shape
    tr = max(d for d in range(8, r + 1, 8) if r % d == 0 and d * c <= ADAM_TILE_ELEMS)

    def body(p_ref, w_ref, m_ref, v_ref, g_ref, d_ref, m2_ref, v2_ref):
        g = p_ref[0].astype(f32)
        for q in range(1, N_DEV):
            g = g + p_ref[q].astype(f32)
        d, m2, v2 = _adam_math(w_ref[...], g, m_ref[...], v_ref[...])
        g_ref[...] = g
        d_ref[...] = d
        m2_ref[...] = m2
        v2_ref[...] = v2

    blk = pl.BlockSpec((tr, c), lambda i: (i, 0))
    return pl.pallas_call(
        body, name=name, grid=(r // tr,),
        in_specs=[pl.BlockSpec((N_DEV, tr, c), lambda i: (0, i, 0)), blk, blk, blk],
        out_specs=[blk] * 4, out_shape=[jax.ShapeDtypeStruct((r, c), f32)] * 4,
        compiler_params=_params(("arbitrary",)),
    )(parts, w, m, v)


def _adamw_small(items, *, name):
    n = len(items)

    def body(*refs):
        ins, outs = refs[:4 * n], refs[4 * n:]
        for k in range(n):
            g, w, m, v = (ins[4 * k + q][...] for q in range(4))
            d, m2, v2 = _adam_math(w, g, m, v)
            outs[3 * k][...] = d
            outs[3 * k + 1][...] = m2
            outs[3 * k + 2][...] = v2

    vm = pl.BlockSpec(memory_space=pltpu.VMEM)
    flat = [a for item in items for a in item]
    out_shape = [jax.ShapeDtypeStruct(item[1].shape, f32) for item in items for _ in range(3)]
    return pl.pallas_call(
        body, name=name, in_specs=[vm] * (4 * n), out_specs=[vm] * (3 * n), out_shape=out_shape,
    )(*flat)


def _sum_parts(parts, *, name):
    def body(p_ref, o_ref):
        acc = p_ref[0]
        for q in range(1, N_DEV):
            acc = acc + p_ref[q]
        o_ref[...] = acc

    vm = pl.BlockSpec(memory_space=pltpu.VMEM)
    return pl.pallas_call(
        body, name=name, in_specs=[vm], out_specs=vm, out_shape=jax.ShapeDtypeStruct(parts.shape[1:], f32),
    )(parts)


WEIGHTS = ["ffn1_w_gate", "ffn1_w_up", "ffn1_w_down", "ln1_g", "ln1_b", "w_in", "b_forget", "conv_w", "conv_b",
           "rg_wa", "rg_ba", "rg_wx", "rg_bx", "lru_lambda", "w_out", "ln2_g", "ln2_b",
           "ffn2_w_gate", "ffn2_w_up", "ffn2_w_down", "ln3_g", "ln3_b"]
BIG = ["ffn1_w_gate", "ffn1_w_up", "ffn1_w_down", "w_in", "w_out", "ffn2_w_gate", "ffn2_w_up", "ffn2_w_down"]
PACKED = ["ln1_g", "ln1_b", "ln2_g", "ln2_b", "ln3_g", "ln3_b", "conv_b", "rg_ba", "rg_bx", "lru_lambda",
          "conv_w", "rg_wa", "rg_wx", "b_forget"]
PACK_ROWS = 600


def _two_d(a):
    return a.reshape((-1, a.shape[-1]))


def _transport(a):
    a = _two_d(a)
    return a.reshape(-1, LANES) if a.shape[-1] == IN_SHARD else a


def kernel(x, ffn1_w_gate, ffn1_w_up, ffn1_w_down, ln1_g, ln1_b, w_in, b_forget, conv_w, conv_b, rg_wa, rg_ba, rg_wx, rg_bx, lru_lambda, w_out, ln2_g, ln2_b, ffn2_w_gate, ffn2_w_up, ffn2_w_down, ln3_g, ln3_b, loss_target, m_ffn1_w_gate, m_ffn1_w_up, m_ffn1_w_down, m_ln1_g, m_ln1_b, m_w_in, m_b_forget, m_conv_w, m_conv_b, m_rg_wa, m_rg_ba, m_rg_wx, m_rg_bx, m_lru_lambda, m_w_out, m_ln2_g, m_ln2_b, m_ffn2_w_gate, m_ffn2_w_up, m_ffn2_w_down, m_ln3_g, m_ln3_b, v_ffn1_w_gate, v_ffn1_w_up, v_ffn1_w_down, v_ln1_g, v_ln1_b, v_w_in, v_b_forget, v_conv_w, v_conv_b, v_rg_wa, v_rg_ba, v_rg_wx, v_rg_bx, v_lru_lambda, v_w_out, v_ln2_g, v_ln2_b, v_ffn2_w_gate, v_ffn2_w_up, v_ffn2_w_down, v_ln3_g, v_ln3_b):
    w_args = (ffn1_w_gate, ffn1_w_up, ffn1_w_down, ln1_g, ln1_b, w_in, b_forget, conv_w, conv_b, rg_wa, rg_ba, rg_wx, rg_bx, lru_lambda, w_out, ln2_g, ln2_b, ffn2_w_gate, ffn2_w_up, ffn2_w_down, ln3_g, ln3_b)
    m_args = (m_ffn1_w_gate, m_ffn1_w_up, m_ffn1_w_down, m_ln1_g, m_ln1_b, m_w_in, m_b_forget, m_conv_w, m_conv_b, m_rg_wa, m_rg_ba, m_rg_wx, m_rg_bx, m_lru_lambda, m_w_out, m_ln2_g, m_ln2_b, m_ffn2_w_gate, m_ffn2_w_up, m_ffn2_w_down, m_ln3_g, m_ln3_b)
    v_args = (v_ffn1_w_gate, v_ffn1_w_up, v_ffn1_w_down, v_ln1_g, v_ln1_b, v_w_in, v_b_forget, v_conv_w, v_conv_b, v_rg_wa, v_rg_ba, v_rg_wx, v_rg_bx, v_lru_lambda, v_w_out, v_ln2_g, v_ln2_b, v_ffn2_w_gate, v_ffn2_w_up, v_ffn2_w_down, v_ln3_g, v_ln3_b)
    w = dict(zip(WEIGHTS, w_args))
    m = dict(zip(WEIGHTS, m_args))
    v = dict(zip(WEIGHTS, v_args))
    me = 4 * lax.axis_index("x") + 2 * lax.axis_index("y") + lax.axis_index("c")

    sent = {n: _transport(w[n]).astype(bf16) for n in BIG}
    sent["conv_w"] = _two_d(w["conv_w"])
    small = {n: w[n] for n in ("ln1_g", "ln1_b", "ln2_g", "ln2_b", "ln3_g", "ln3_b", "b_forget", "conv_b",
                               "lru_lambda")}
    small.update({n: w[n][0] for n in ("rg_wa", "rg_ba", "rg_wx", "rg_bx")})

    sq_rows, grad_x, parts, all_packed, small_shapes = _local_step(x[0], loss_target[0], sent, small)
    loss = lax.psum(0.5 * jnp.sum(sq_rows) / D_MODEL, ("x", "y", "c"))

    total = _sum_parts(all_packed, name="sum_small_grads").reshape(-1)
    grads, off = {}, 0
    for n in PACKED:
        size = math.prod(small_shapes[n])
        grads[n] = total[off:off + size].reshape(small_shapes[n])
        off += size
    grads["conv_w"] = lax.dynamic_slice_in_dim(grads["conv_w"], me * (LRU_W // N_DEV), LRU_W // N_DEV, axis=1)

    delta, new_m, new_v = {}, {}, {}
    for n in BIG:
        g, d, m2, v2 = _adamw_big(parts[n], _transport(w[n]), _transport(m[n]), _transport(v[n]),
                                  name="adamw_" + n)
        grads[n], delta[n], new_m[n], new_v[n] = g, d, m2, v2
    small_names = [n for n in WEIGHTS if n not in BIG]
    outs = _adamw_small([(_two_d(grads[n]), _two_d(w[n]), _two_d(m[n]), _two_d(v[n])) for n in small_names],
                        name="adamw_small")
    for k, n in enumerate(small_names):
        delta[n], new_m[n], new_v[n] = outs[3 * k], outs[3 * k + 1], outs[3 * k + 2]

    def shaped(d):
        return [d[n].reshape(w[n].shape) for n in WEIGHTS]

    return (loss, grad_x[None], *shaped(grads), *shaped(delta), *shaped(new_m), *shaped(new_v))
```

```python
import functools
import math

import jax
import jax.numpy as jnp
from jax import lax
from jax.experimental import pallas as pl
from jax.experimental.pallas import tpu as pltpu

f32 = jnp.float32
bf16 = jnp.bfloat16

N_DEV = 8
D_MODEL = 1024
D_FF = 4096
FF_TILE = D_FF // N_DEV
FOX_W = 512
LRU_W = 512
HEADS = 8
HEAD_D = 64
IN_COLS = 2568
IN_SHARD = IN_COLS // N_DEV
LANES = 128
LN_EPS = 1e-5
ALPHA = 2.0 ** 0.25
ATT_SCALE = 1.0 / math.sqrt(HEAD_D)
LRU_C = 8.0
NEG_BIG = -1e30

ADAM_LR = 0.001
ADAM_B1 = 0.9
ADAM_B2 = 0.999
ADAM_EPS = 1e-08
ADAM_WD = 0.01
ADAM_STEP = 10

VMEM_LIMIT = 56 * 1024 * 1024
MESH_T = pl.DeviceIdType.MESH


def _params(sem, **kw):
    return pltpu.CompilerParams(dimension_semantics=sem, vmem_limit_bytes=VMEM_LIMIT, **kw)


def _sigmoid(x):
    return 1.0 / (1.0 + jnp.exp(-x))


def _sigmoid_tanh(x):
    return 0.5 * jnp.tanh(0.5 * x) + 0.5


def _softplus(x):
    return jnp.maximum(x, 0.0) + jnp.log(1.0 + jnp.exp(-jnp.abs(x)))


def _one_minus_exp(x):
    series = -x * (1.0 + x * (0.5 + x * (1.0 / 6 + x * (1.0 / 24 + x * (1.0 / 120 + x * (1.0 / 720))))))
    return jnp.where(x > -0.125, series, 1.0 - jnp.exp(x))


_GELU_C = math.sqrt(2.0 / math.pi)


def _gelu_and_grad(x):
    inner = _GELU_C * (x + 0.044715 * x * x * x)
    t = jnp.tanh(inner)
    g = 0.5 * x * (1.0 + t)
    dg = 0.5 * (1.0 + t) + 0.5 * x * (1.0 - t * t) * _GELU_C * (1.0 + 3 * 0.044715 * x * x)
    return g, dg


def _ln_fwd_tile(pre):
    mu = jnp.mean(pre, axis=-1, keepdims=True)
    xc = pre - mu
    var = jnp.mean(xc * xc, axis=-1, keepdims=True)
    rstd = lax.rsqrt(var + LN_EPS)
    return xc * rstd, rstd


def _ln_bwd_tile(dy, xhat, rstd, g):
    dyg = dy * g
    m1 = jnp.mean(dyg, axis=-1, keepdims=True)
    m2 = jnp.mean(dyg * xhat, axis=-1, keepdims=True)
    dpre = rstd * (dyg - m1 - xhat * m2)
    return dpre, jnp.sum(dy * xhat, axis=0, keepdims=True), jnp.sum(dy, axis=0, keepdims=True)


_NT = (((1,), (1,)), ((), ()))
_TN = (((0,), (0,)), ((), ()))


class _Exchange:
    def __init__(self, arrs, gather):
        self.arrs, self.gather, self.n = list(arrs), gather, len(arrs)

    def out_shape(self):
        return [jax.ShapeDtypeStruct(((N_DEV,) + a.shape) if self.gather else a.shape, a.dtype) for a in self.arrs]

    def scratch(self):
        n_remote = self.n * (N_DEV - 1)
        return [pltpu.SemaphoreType.DMA((n_remote,)), pltpu.SemaphoreType.DMA((n_remote,)),
                pltpu.SemaphoreType.DMA((self.n,))]

    def copies(self, ins, outs, sems):
        send_sems, recv_sems, local_sems = sems
        x, y, c = lax.axis_index("x"), lax.axis_index("y"), lax.axis_index("c")
        me = 4 * x + 2 * y + c
        out = []
        for k in range(self.n):
            for d in range(1, N_DEV):
                px = 1 - x if d & 4 else x
                py = 1 - y if d & 2 else y
                pc = 1 - c if d & 1 else c
                sem = k * (N_DEV - 1) + d - 1
                out.append(pltpu.make_async_remote_copy(
                    src_ref=ins[k].at[4 * px + 2 * py + pc], dst_ref=outs[k].at[me],
                    send_sem=send_sems.at[sem], recv_sem=recv_sems.at[sem],
                    device_id=(px, py, pc), device_id_type=MESH_T))
            out.append(pltpu.make_async_copy(ins[k].at[me], outs[k].at[me], local_sems.at[k]))
        return out

    def gather_copies(self, ins, outs, sems):
        send_sems, recv_sems, local_sems = sems
        x, y, c = lax.axis_index("x"), lax.axis_index("y"), lax.axis_index("c")
        sibling = (x, y, 1 - c)
        chips = [(1 - x, y), (x, 1 - y), (1 - x, 1 - y)]
        out = []
        for k in range(self.n):
            def copy(s, block, to, src=None, k=k):
                rows = outs[k].at[4 * block[0] + 2 * block[1] + block[2]]
                sem = k * (N_DEV - 1) + s
                return pltpu.make_async_remote_copy(
                    src_ref=rows if src is None else src, dst_ref=rows, send_sem=send_sems.at[sem],
                    recv_sem=recv_sems.at[sem], device_id=to, device_id_type=MESH_T)

            first = [copy(0, (x, y, c), sibling, src=ins[k])]
            first += [copy(1 + q, (x, y, c), (*chip, c), src=ins[k]) for q, chip in enumerate(chips)]
            passed = [copy(4 + q, (*chip, c), sibling) for q, chip in enumerate(chips)]
            own = pltpu.make_async_copy(ins[k], outs[k].at[4 * x + 2 * y + c], local_sems.at[k])
            out.append((first, passed, own, copy))
        return out, sibling, chips, (x, y, c)

    def start(self, ins, outs, sems):
        if not self.gather:
            for cp in self.copies(ins, outs, sems):
                cp.start()
            return
        per_array, _, _, _ = self.gather_copies(ins, outs, sems)
        for first, _, own, _ in per_array:
            own.start()
            for cp in first:
                cp.start()

    def wait(self, ins, outs, sems):
        if not self.gather:
            for cp in self.copies(ins, outs, sems):
                cp.wait()
            return
        per_array, sibling, chips, (x, y, c) = self.gather_copies(ins, outs, sems)
        for first, passed, own, copy in per_array:
            for q, chip in enumerate(chips):
                copy(1 + q, (*chip, c), (x, y, c)).wait_recv()
                passed[q].start()
        for first, passed, own, copy in per_array:
            copy(0, sibling, (x, y, c)).wait_recv()
            for q, chip in enumerate(chips):
                copy(4 + q, (*chip, 1 - c), (x, y, c)).wait_recv()
            for cp in first + passed:
                cp.wait_send()
            own.wait()


def _hosted_call(host, body, *, name, grid, in_specs, out_specs, out_shape, scratch_shapes=(), compiler_params):
    out_specs = list(out_specs) if isinstance(out_specs, (list, tuple)) else [out_specs]
    out_shape = list(out_shape) if isinstance(out_shape, (list, tuple)) else [out_shape]
    if host is None:
        return pl.pallas_call(body, name=name, grid=grid, in_specs=in_specs, out_specs=out_specs,
                              out_shape=out_shape, scratch_shapes=list(scratch_shapes),
                              compiler_params=compiler_params)
    n_in, n_out, n_scr, k = len(in_specs), len(out_shape), len(scratch_shapes), host.n

    def wrapped(*refs):
        ins, h_in = refs[:n_in], refs[n_in:n_in + k]
        outs, h_out = refs[n_in + k:n_in + k + n_out], refs[n_in + k + n_out:n_in + 2 * k + n_out]
        scr, sems = refs[n_in + 2 * k + n_out:n_in + 2 * k + n_out + n_scr], refs[n_in + 2 * k + n_out + n_scr:]
        ids = [pl.program_id(a) for a in range(len(grid))]
        first = functools.reduce(jnp.logical_and, [i == 0 for i in ids])
        last = functools.reduce(jnp.logical_and, [i == g - 1 for i, g in zip(ids, grid)])

        @pl.when(first)
        def _():
            host.start(h_in, h_out, sems)

        body(*ins, *outs, *scr)

        @pl.when(last)
        def _():
            host.wait(h_in, h_out, sems)

    hbm = pl.BlockSpec(memory_space=pl.ANY)
    call = pl.pallas_call(
        wrapped, name=name, grid=grid, in_specs=list(in_specs) + [hbm] * k, out_specs=out_specs + [hbm] * k,
        out_shape=out_shape + host.out_shape(), scratch_shapes=list(scratch_shapes) + host.scratch(),
        compiler_params=compiler_params)
    return lambda *args: call(*args, *host.arrs)


def _exchange(arrs, *, gather, name):
    host = _Exchange(arrs, gather)

    def body(*refs):
        ins, outs, sems = refs[:host.n], refs[host.n:2 * host.n], refs[2 * host.n:]
        host.start(ins, outs, sems)
        host.wait(ins, outs, sems)

    hbm = pl.BlockSpec(memory_space=pl.ANY)
    return pl.pallas_call(
        body, name=name, in_specs=[hbm] * host.n, out_specs=[hbm] * host.n, out_shape=host.out_shape(),
        scratch_shapes=host.scratch(), compiler_params=pltpu.CompilerParams(has_side_effects=True),
    )(*arrs)


def _ffn_fwd(xhat, g_in, b_in, wg, wu, wd, *, tm, name, host=None):
    t = xhat.shape[0]
    nj = N_DEV

    def body(x_ref, g_ref, b_ref, wg_ref, wu_ref, wd_ref, xo_ref, rstd_ref, hg_ref, hu_ref, xb, acc):
        j = pl.program_id(1)

        @pl.when(j == 0)
        def _():
            xb[...] = (x_ref[...] * g_ref[...] + b_ref[...]).astype(bf16)
            acc[...] = jnp.zeros_like(acc)

        hg = jnp.dot(xb[...], wg_ref[...], preferred_element_type=f32)
        hu = jnp.dot(xb[...], wu_ref[...], preferred_element_type=f32)
        hg_ref[...] = hg.astype(bf16)
        hu_ref[...] = hu.astype(bf16)
        a = hg * _sigmoid_tanh(hg) * hu
        acc[...] += jnp.dot(a.astype(bf16), wd_ref[...], preferred_element_type=f32)

        @pl.when(j == nj - 1)
        def _():
            x = x_ref[...] * g_ref[...] + b_ref[...]
            xo, rstd = _ln_fwd_tile(ALPHA * x + 0.5 * acc[...])
            xo_ref[...] = xo
            rstd_ref[...] = rstd

    row = pl.BlockSpec((1, D_MODEL), lambda i, j: (0, 0))
    return _hosted_call(
        host, body, name=name, grid=(t // tm, nj),
        in_specs=[pl.BlockSpec((tm, D_MODEL), lambda i, j: (i, 0)), row, row,
                  pl.BlockSpec((None, D_MODEL, FF_TILE), lambda i, j: (j, 0, 0)),
                  pl.BlockSpec((None, D_MODEL, FF_TILE), lambda i, j: (j, 0, 0)),
                  pl.BlockSpec((None, FF_TILE, D_MODEL), lambda i, j: (j, 0, 0))],
        out_specs=[pl.BlockSpec((tm, D_MODEL), lambda i, j: (i, 0)),
                   pl.BlockSpec((tm, 1), lambda i, j: (i, 0)),
                   pl.BlockSpec((tm, FF_TILE), lambda i, j: (i, j)),
                   pl.BlockSpec((tm, FF_TILE), lambda i, j: (i, j))],
        out_shape=[jax.ShapeDtypeStruct((t, D_MODEL), f32), jax.ShapeDtypeStruct((t, 1), f32),
                   jax.ShapeDtypeStruct((t, D_FF), bf16), jax.ShapeDtypeStruct((t, D_FF), bf16)],
        scratch_shapes=[pltpu.VMEM((tm, D_MODEL), bf16), pltpu.VMEM((tm, D_MODEL), f32)],
        compiler_params=_params(("arbitrary", "arbitrary")),
    )(xhat, g_in, b_in, wg, wu, wd)


def _ffn_bwd(dpre, hg, hu, wg, wu, wd, ln_in, *, tm, name, host=None):
    t = dpre.shape[0]
    nj = N_DEV
    with_ln = ln_in is not None

    def body(*refs):
        if with_ln:
            (dp_ref, hg_ref, hu_ref, wg_ref, wu_ref, wd_ref, xh_ref, rs_ref, g_ref,
             dx_ref, gg_ref, gb_ref, dhg_ref, dhu_ref, a_ref, dfb, acc) = refs
        else:
            (dp_ref, hg_ref, hu_ref, wg_ref, wu_ref, wd_ref,
             dx_ref, dhg_ref, dhu_ref, a_ref, dfb, acc) = refs
        i = pl.program_id(0)
        j = pl.program_id(1)

        @pl.when(j == 0)
        def _():
            dfb[...] = (0.5 * dp_ref[...]).astype(bf16)
            acc[...] = jnp.zeros_like(acc)

        da = lax.dot_general(dfb[...], wd_ref[...], _NT, preferred_element_type=f32)
        hgv = hg_ref[...].astype(f32)
        huv = hu_ref[...].astype(f32)
        sg = _sigmoid_tanh(hgv)
        silu = hgv * sg
        a_ref[...] = (silu * huv).astype(bf16)
        dhu = (da * silu).astype(bf16)
        dhg = (da * huv * (sg * (1.0 + hgv * (1.0 - sg)))).astype(bf16)
        dhg_ref[...] = dhg
        dhu_ref[...] = dhu
        acc[...] += (lax.dot_general(dhg, wg_ref[...], _NT, preferred_element_type=f32)
                     + lax.dot_general(dhu, wu_ref[...], _NT, preferred_element_type=f32))

        @pl.when(j == nj - 1)
        def _():
            dx = ALPHA * dp_ref[...] + acc[...]
            if with_ln:
                dprev, gg, gb = _ln_bwd_tile(dx, xh_ref[...], rs_ref[...], g_ref[...])
                dx_ref[...] = dprev

                @pl.when(i == 0)
                def _():
                    gg_ref[...] = gg
                    gb_ref[...] = gb

                @pl.when(i > 0)
                def _():
                    gg_ref[...] += gg
                    gb_ref[...] += gb
            else:
                dx_ref[...] = dx

    tok = pl.BlockSpec((tm, D_MODEL), lambda i, j: (i, 0))
    row = pl.BlockSpec((1, D_MODEL), lambda i, j: (0, 0))
    hid = pl.BlockSpec((tm, FF_TILE), lambda i, j: (i, j))
    in_specs = [tok, hid, hid,
                pl.BlockSpec((None, D_MODEL, FF_TILE), lambda i, j: (j, 0, 0)),
                pl.BlockSpec((None, D_MODEL, FF_TILE), lambda i, j: (j, 0, 0)),
                pl.BlockSpec((None, FF_TILE, D_MODEL), lambda i, j: (j, 0, 0))]
    args = [dpre, hg, hu, wg, wu, wd]
    out_specs = [tok]
    out_shape = [jax.ShapeDtypeStruct((t, D_MODEL), f32)]
    if with_ln:
        in_specs += [tok, pl.BlockSpec((tm, 1), lambda i, j: (i, 0)), row]
        args += list(ln_in)
        out_specs += [row, row]
        out_shape += [jax.ShapeDtypeStruct((1, D_MODEL), f32)] * 2
    out_specs += [hid, hid, hid]
    out_shape += [jax.ShapeDtypeStruct((t, D_FF), bf16)] * 3
    return _hosted_call(
        host, body, name=name, grid=(t // tm, nj), in_specs=in_specs, out_specs=out_specs, out_shape=out_shape,
        scratch_shapes=[pltpu.VMEM((tm, D_MODEL), bf16), pltpu.VMEM((tm, D_MODEL), f32)],
        compiler_params=_params(("arbitrary", "arbitrary")),
    )(*args)


def _ffn_bwd_act(dpre, hg, hu, wd, *, tm, name, host=None):
    t = dpre.shape[0]

    def body(dp_ref, hg_ref, hu_ref, wd_ref, dhg_ref, dhu_ref, a_ref, dfb):
        @pl.when(pl.program_id(1) == 0)
        def _():
            dfb[...] = (0.5 * dp_ref[...]).astype(bf16)

        da = lax.dot_general(dfb[...], wd_ref[...], _NT, preferred_element_type=f32)
        hgv = hg_ref[...].astype(f32)
        huv = hu_ref[...].astype(f32)
        sg = _sigmoid_tanh(hgv)
        silu = hgv * sg
        a_ref[...] = (silu * huv).astype(bf16)
        dhu_ref[...] = (da * silu).astype(bf16)
        dhg_ref[...] = (da * huv * (sg * (1.0 + hgv * (1.0 - sg)))).astype(bf16)

    hid = pl.BlockSpec((tm, FF_TILE), lambda i, j: (i, j))
    return _hosted_call(
        host, body, name=name, grid=(t // tm, N_DEV),
        in_specs=[pl.BlockSpec((tm, D_MODEL), lambda i, j: (i, 0)), hid, hid,
                  pl.BlockSpec((None, FF_TILE, D_MODEL), lambda i, j: (j, 0, 0))],
        out_specs=[hid, hid, hid], out_shape=[jax.ShapeDtypeStruct((t, D_FF), bf16)] * 3,
        scratch_shapes=[pltpu.VMEM((tm, D_MODEL), bf16)],
        compiler_params=_params(("arbitrary", "arbitrary")),
    )(dpre, hg, hu, wd)


def _ffn_bwd_dx(dpre, dhg, dhu, wg, wu, *, tm, name, host=None):
    t = dpre.shape[0]
    nj = N_DEV

    def body(dp_ref, dhg_ref, dhu_ref, wg_ref, wu_ref, dx_ref, acc):
        j = pl.program_id(1)

        @pl.when(j == 0)
        def _():
            acc[...] = jnp.zeros_like(acc)

        acc[...] += (lax.dot_general(dhg_ref[...], wg_ref[...], _NT, preferred_element_type=f32)
                     + lax.dot_general(dhu_ref[...], wu_ref[...], _NT, preferred_element_type=f32))

        @pl.when(j == nj - 1)
        def _():
            dx_ref[...] = ALPHA * dp_ref[...] + acc[...]

    tok = pl.BlockSpec((tm, D_MODEL), lambda i, j: (i, 0))
    hid = pl.BlockSpec((tm, FF_TILE), lambda i, j: (i, j))
    wspec = pl.BlockSpec((None, D_MODEL, FF_TILE), lambda i, j: (j, 0, 0))
    return _hosted_call(
        host, body, name=name, grid=(t // tm, nj), in_specs=[tok, hid, hid, wspec, wspec],
        out_specs=[tok], out_shape=[jax.ShapeDtypeStruct((t, D_MODEL), f32)],
        scratch_shapes=[pltpu.VMEM((tm, D_MODEL), f32)],
        compiler_params=_params(("arbitrary", "arbitrary")),
    )(dpre, dhg, dhu, wg, wu)


def _mm(a, b, *, mode, out_dtype, tm, tn, tk, name, affine=None, a_cols=None, b_cols=None,
        b_blocked=False, out_blocked=False, out_scale=None):
    if mode == "nn":
        m_full, k_full = a.shape
        m_dim, k_dim = (m_full, a_cols[1]) if a_cols else (m_full, k_full)
    else:
        k_dim, m_full = a.shape
        m_dim = a_cols[1] if a_cols else m_full
    a_off = a_cols[0] if a_cols else 0
    if b_blocked:
        n_dim = b.shape[0] * b.shape[2]
        assert b.shape[2] == tn
    else:
        n_dim = b_cols[1] if b_cols else b.shape[1]
    b_off = b_cols[0] if b_cols else 0
    assert m_dim % tm == 0 and n_dim % tn == 0 and k_dim % tk == 0, (name, m_dim, n_dim, k_dim)
    nk = k_dim // tk

    def body(*refs):
        if affine is not None:
            a_ref, g_ref, s_ref, b_ref, o_ref, acc = refs
        else:
            a_ref, b_ref, o_ref, acc = refs
        k = pl.program_id(2)

        @pl.when(k == 0)
        def _():
            acc[...] = jnp.zeros_like(acc)

        av = a_ref[...]
        if affine is not None:
            av = av * g_ref[...] + s_ref[...]
        av = av.astype(bf16)
        bv = b_ref[...].astype(bf16)
        if mode == "nn":
            acc[...] += jnp.dot(av, bv, preferred_element_type=f32)
        else:
            acc[...] += lax.dot_general(av, bv, _TN, preferred_element_type=f32)

        @pl.when(k == nk - 1)
        def _():
            res = acc[...] if out_scale is None else acc[...] * out_scale
            o_ref[...] = res.astype(out_dtype)

    if mode == "nn":
        a_spec = pl.BlockSpec((tm, tk), lambda i, j, k: (i, k + a_off))
        aff_spec = pl.BlockSpec((1, tk), lambda i, j, k: (0, k + a_off))
    else:
        a_spec = pl.BlockSpec((tk, tm), lambda i, j, k: (k, i + a_off))
        aff_spec = pl.BlockSpec((1, tm), lambda i, j, k: (0, i + a_off))
    if b_blocked:
        b_spec = pl.BlockSpec((None, tk, tn), lambda i, j, k: (j, k, 0))
    else:
        b_spec = pl.BlockSpec((tk, tn), lambda i, j, k: (k, j + b_off))
    if out_blocked:
        o_spec = pl.BlockSpec((None, tm, tn), lambda i, j, k: (j, i, 0))
        o_shape = jax.ShapeDtypeStruct((n_dim // tn, m_dim, tn), out_dtype)
    else:
        o_spec = pl.BlockSpec((tm, tn), lambda i, j, k: (i, j))
        o_shape = jax.ShapeDtypeStruct((m_dim, n_dim), out_dtype)
    in_specs = [a_spec] + ([aff_spec, aff_spec] if affine is not None else []) + [b_spec]
    args = [a] + (list(affine) if affine is not None else []) + [b]
    return pl.pallas_call(
        body, name=name, grid=(m_dim // tm, n_dim // tn, nk), in_specs=in_specs, out_specs=o_spec,
        out_shape=o_shape, scratch_shapes=[pltpu.VMEM((tm, tn), f32)],
        compiler_params=_params(("arbitrary", "arbitrary", "arbitrary")),
    )(*args)


def _mm_tn(a, b, *, out_dtype, tm, mb, tn, nb, tk, name, affine=None, out_blocked=False, out_scale=None, host=None):
    k_dim, m_dim = a.shape
    n_dim = b.shape[1]
    assert m_dim % (mb * tm) == 0 and n_dim % (nb * tn) == 0 and k_dim % tk == 0, (name, m_dim, n_dim, k_dim)
    nk = k_dim // tk

    def body(*refs):
        if affine is not None:
            a_ref, g_ref, s_ref, b_ref, o_ref, acc = refs
        else:
            a_ref, b_ref, o_ref, acc = refs
        k = pl.program_id(2)

        @pl.when(k == 0)
        def _():
            acc[...] = jnp.zeros_like(acc)

        av = a_ref[...]
        if affine is not None:
            av = av * g_ref[...] + s_ref[...]
        av = av.astype(bf16)
        bv = b_ref[...].astype(bf16)
        for im in range(mb):
            a_t = av[:, im * tm:(im + 1) * tm].T
            for jn in range(nb):
                acc[im * nb + jn] += jnp.dot(a_t, bv[:, jn * tn:(jn + 1) * tn], preferred_element_type=f32)

        @pl.when(k == nk - 1)
        def _():
            for im in range(mb):
                for jn in range(nb):
                    res = acc[im * nb + jn]
                    if out_scale is not None:
                        res = res * out_scale
                    if out_blocked:
                        o_ref[jn, im * tm:(im + 1) * tm, :] = res.astype(out_dtype)
                    else:
                        o_ref[im * tm:(im + 1) * tm, jn * tn:(jn + 1) * tn] = res.astype(out_dtype)

    a_spec = pl.BlockSpec((tk, mb * tm), lambda i, j, k: (k, i))
    aff_spec = pl.BlockSpec((1, mb * tm), lambda i, j, k: (0, i))
    b_spec = pl.BlockSpec((tk, nb * tn), lambda i, j, k: (k, j))
    if out_blocked:
        o_spec = pl.BlockSpec((nb, mb * tm, tn), lambda i, j, k: (j, i, 0))
        o_shape = jax.ShapeDtypeStruct((n_dim // tn, m_dim, tn), out_dtype)
    else:
        o_spec = pl.BlockSpec((mb * tm, nb * tn), lambda i, j, k: (i, j))
        o_shape = jax.ShapeDtypeStruct((m_dim, n_dim), out_dtype)
    in_specs = [a_spec] + ([aff_spec, aff_spec] if affine is not None else []) + [b_spec]
    args = [a] + (list(affine) if affine is not None else []) + [b]
    res = _hosted_call(
        host, body, name=name, grid=(m_dim // (mb * tm), n_dim // (nb * tn), nk), in_specs=in_specs,
        out_specs=o_spec, out_shape=o_shape, scratch_shapes=[pltpu.VMEM((mb * nb, tm, tn), f32)],
        compiler_params=_params(("arbitrary", "arbitrary", "arbitrary")),
    )(*args)
    return res[0] if host is None else res


def _mmln(pairs, *, tm, name, resid=None, resid_scale=1.0, epi=None, ln=None, n_out=D_MODEL):
    t = pairs[0][0].shape[0]
    n_pairs = len(pairs)
    n_resid = 0 if resid is None else len(resid) - 1

    def body(*refs):
        pos = 0
        val = None
        for p in range(n_pairs):
            a_ref, b_ref = refs[pos], refs[pos + 1]
            pos += 2
            av = a_ref[...].astype(bf16)
            bv = b_ref[...].astype(bf16)
            if pairs[p][6] == "nn":
                term = jnp.dot(av, bv, preferred_element_type=f32)
            else:
                term = lax.dot_general(av, bv, _NT, preferred_element_type=f32)
            val = term if val is None else val + term
        if resid is not None:
            if resid[0] == "plain":
                r = refs[pos][...]
            else:
                r = refs[pos][...] * refs[pos + 1][...] + refs[pos + 2][...]
            pos += n_resid
            val = val + resid_scale * r
        if epi is None:
            o_ref = refs[pos]
            o_ref[...] = val.astype(o_ref.dtype)
        elif epi == "ln_fwd":
            xo, rstd = _ln_fwd_tile(val)
            refs[pos][...] = xo
            refs[pos + 1][...] = rstd
        else:
            xh_ref, rs_ref, g_ref, dx_ref, gg_ref, gb_ref = refs[pos:pos + 6]
            dprev, gg, gb = _ln_bwd_tile(val, xh_ref[...], rs_ref[...], g_ref[...])
            dx_ref[...] = dprev
            i = pl.program_id(0)

            @pl.when(i == 0)
            def _():
                gg_ref[...] = gg
                gb_ref[...] = gb

            @pl.when(i > 0)
            def _():
                gg_ref[...] += gg
                gb_ref[...] += gb

    in_specs, args = [], []
    for (a, acb, aw, b, bcb, bw, mode) in pairs:
        in_specs.append(pl.BlockSpec((tm, aw), lambda i, acb=acb: (i, acb)))
        args.append(a)
        if mode == "nn":
            in_specs.append(pl.BlockSpec((aw, n_out), lambda i, bcb=bcb: (bcb, 0)))
        else:
            in_specs.append(pl.BlockSpec((n_out, bw), lambda i, bcb=bcb: (0, bcb)))
        args.append(b)
    tok = pl.BlockSpec((tm, n_out), lambda i: (i, 0))
    row = pl.BlockSpec((1, n_out), lambda i: (0, 0))
    col = pl.BlockSpec((tm, 1), lambda i: (i, 0))
    if resid is not None:
        in_specs += [tok] if resid[0] == "plain" else [tok, row, row]
        args += list(resid[1:])
    if epi is None:
        out_specs, out_shape = tok, jax.ShapeDtypeStruct((t, n_out), f32)
    elif epi == "ln_fwd":
        out_specs = [tok, col]
        out_shape = [jax.ShapeDtypeStruct((t, n_out), f32), jax.ShapeDtypeStruct((t, 1), f32)]
    else:
        in_specs += [tok, col, row]
        args += list(ln)
        out_specs = [tok, row, row]
        out_shape = [jax.ShapeDtypeStruct((t, n_out), f32)] + [jax.ShapeDtypeStruct((1, n_out), f32)] * 2
    return pl.pallas_call(
        body, name=name, grid=(t // tm,), in_specs=in_specs, out_specs=out_specs, out_shape=out_shape,
        compiler_params=_params(("arbitrary",)),
    )(*args)


def _loss_bwd(xhat, rstd, g, b, target, *, tm, name):
    t = xhat.shape[0]

    def body(xh_ref, rs_ref, g_ref, b_ref, tg_ref, dx_ref, sq_ref, gg_ref, gb_ref):
        i = pl.program_id(0)
        xh = xh_ref[...]
        diff = xh * g_ref[...] + b_ref[...] - tg_ref[...]
        sq = jnp.sum(diff * diff, axis=0, keepdims=True)
        dprev, gg, gb = _ln_bwd_tile(diff * (1.0 / D_MODEL), xh, rs_ref[...], g_ref[...])
        dx_ref[...] = dprev

        @pl.when(i == 0)
        def _():
            sq_ref[...] = sq
            gg_ref[...] = gg
            gb_ref[...] = gb

        @pl.when(i > 0)
        def _():
            sq_ref[...] += sq
            gg_ref[...] += gg
            gb_ref[...] += gb

    tok = pl.BlockSpec((tm, D_MODEL), lambda i: (i, 0))
    row = pl.BlockSpec((1, D_MODEL), lambda i: (0, 0))
    return pl.pallas_call(
        body, name=name, grid=(t // tm,),
        in_specs=[tok, pl.BlockSpec((tm, 1), lambda i: (i, 0)), row, row, tok],
        out_specs=[tok, row, row, row],
        out_shape=[jax.ShapeDtypeStruct((t, D_MODEL), f32)] + [jax.ShapeDtypeStruct((1, D_MODEL), f32)] * 3,
        compiler_params=_params(("arbitrary",)),
    )(xhat, rstd, g, b, target)


CUM_TILE = 256


def _tri(n, lower):
    r = lax.broadcasted_iota(jnp.int32, (n, n), 0)
    c = lax.broadcasted_iota(jnp.int32, (n, n), 1)
    return jnp.where((r >= c) if lower else (r <= c), 1.0, 0.0).astype(f32)


def _cum_fwd(zfg, bfg, *, name):
    t = zfg.shape[0]

    def body(z_ref, b_ref, o_ref, carry):
        @pl.when(pl.program_id(0) == 0)
        def _():
            carry[...] = jnp.zeros_like(carry)

        ls = -_softplus(-(z_ref[...] + b_ref[...]))
        c = jnp.dot(_tri(CUM_TILE, True), ls, preferred_element_type=f32,
                    precision=lax.Precision.HIGHEST) + carry[...]
        o_ref[...] = c
        carry[...] = c[CUM_TILE - 1:CUM_TILE, :]

    blk = pl.BlockSpec((CUM_TILE, LANES), lambda i: (i, 0))
    return pl.pallas_call(
        body, name=name, grid=(t // CUM_TILE,),
        in_specs=[blk, pl.BlockSpec((1, LANES), lambda i: (0, 0))], out_specs=blk,
        out_shape=jax.ShapeDtypeStruct((t, LANES), f32), scratch_shapes=[pltpu.VMEM((1, LANES), f32)],
        compiler_params=_params(("arbitrary",)),
    )(zfg, bfg)


def _cum_bwd(dcum_q, dcum_k, zfg, bfg, *, name):
    t = zfg.shape[0]
    n = t // CUM_TILE

    def body(d_ref, d2_ref, z_ref, b_ref, o_ref, s_ref, carry):
        i = pl.program_id(0)

        @pl.when(i == 0)
        def _():
            carry[...] = jnp.zeros_like(carry)

        dls = jnp.dot(_tri(CUM_TILE, False), d_ref[...] + d2_ref[...], preferred_element_type=f32,
                      precision=lax.Precision.HIGHEST) + carry[...]
        carry[...] = dls[0:1, :]
        lane = lax.broadcasted_iota(jnp.int32, (CUM_TILE, LANES), 1)
        dfg = jnp.where(lane < HEADS, dls * _sigmoid(-(z_ref[...] + b_ref[...])), 0.0)
        o_ref[...] = dfg
        tot = jnp.sum(dfg, axis=0, keepdims=True)

        @pl.when(i == 0)
        def _():
            s_ref[...] = tot

        @pl.when(i > 0)
        def _():
            s_ref[...] += tot

    blk = pl.BlockSpec((CUM_TILE, LANES), lambda i: (n - 1 - i, 0))
    row = pl.BlockSpec((1, LANES), lambda i: (0, 0))
    return pl.pallas_call(
        body, name=name, grid=(n,), in_specs=[blk, blk, blk, row], out_specs=[blk, row],
        out_shape=[jax.ShapeDtypeStruct((t, LANES), f32), jax.ShapeDtypeStruct((1, LANES), f32)],
        scratch_shapes=[pltpu.VMEM((1, LANES), f32)],
        compiler_params=_params(("arbitrary",)),
    )(dcum_q, dcum_k, zfg, bfg)


ATT_TILE = 512


ATT_ROWS = 32


def _causal_rows(r, transposed):
    rr = lax.broadcasted_iota(jnp.int32, (ATT_ROWS, ATT_TILE), 0) + r * ATT_ROWS
    cc = lax.broadcasted_iota(jnp.int32, (ATT_ROWS, ATT_TILE), 1)
    return (cc >= rr) if transposed else (rr >= cc)


def _causal(i, j, transposed):
    r = lax.broadcasted_iota(jnp.int32, (ATT_TILE, ATT_TILE), 0)
    c = lax.broadcasted_iota(jnp.int32, (ATT_TILE, ATT_TILE), 1)
    if transposed:
        return (c + i * ATT_TILE) >= (r + j * ATT_TILE)
    return (r + i * ATT_TILE) >= (c + j * ATT_TILE)


def _attn_fwd(qkv, cum, cum_t, *, name, host=None):
    t = qkv.shape[0]
    n = t // ATT_TILE
    tq = ATT_TILE

    def body(q_ref, k_ref, v_ref, cq_ref, ck_ref, o_ref, lse_ref, acc, m_s, l_s, c_s, s_s, p_s):
        i = pl.program_id(0)
        j = pl.program_id(1)

        @pl.when(j == 0)
        def _():
            acc[...] = jnp.zeros_like(acc)
            m_s[...] = jnp.full_like(m_s, NEG_BIG)
            l_s[...] = jnp.zeros_like(l_s)

        def block(masked):
            for h in range(HEADS):
                hs = slice(HEAD_D * h, HEAD_D * (h + 1))
                s_s[...] = lax.dot_general(q_ref[:, hs] * ATT_SCALE, k_ref[:, hs], _NT, preferred_element_type=f32)
                ck = ck_ref[h:h + 1, :]

                def rows_chunk(r, carry):
                    rows = pl.ds(pl.multiple_of(r * ATT_ROWS, ATT_ROWS), ATT_ROWS)
                    s = s_s[rows, :] + (cq_ref[rows, h:h + 1] - ck)
                    if masked:
                        s = jnp.where(_causal_rows(r, False), s, NEG_BIG)
                    m_old = m_s[rows, h:h + 1]
                    m_new = jnp.maximum(m_old, jnp.max(s, axis=-1, keepdims=True))
                    corr = jnp.exp(m_old - m_new)
                    p = jnp.exp(s - m_new)
                    l_s[rows, h:h + 1] = corr * l_s[rows, h:h + 1] + jnp.sum(p, axis=-1, keepdims=True)
                    m_s[rows, h:h + 1] = m_new
                    c_s[rows, h:h + 1] = corr
                    p_s[rows, :] = p.astype(bf16)
                    return carry

                lax.fori_loop(0, tq // ATT_ROWS, rows_chunk, 0, unroll=4)
                acc[:, hs] = c_s[:, h:h + 1] * acc[:, hs] + jnp.dot(p_s[...], v_ref[:, hs],
                                                                   preferred_element_type=f32)

        @pl.when(j < i)
        def _():
            block(False)

        @pl.when(j == i)
        def _():
            block(True)
            lse_ref[...] = jnp.zeros_like(lse_ref)
            for h in range(HEADS):
                hs = slice(HEAD_D * h, HEAD_D * (h + 1))
                l = l_s[:, h:h + 1]
                o_ref[:, hs] = acc[:, hs] / l
                lse_ref[:, h:h + 1] = m_s[:, h:h + 1] + jnp.log(l)

    return _hosted_call(
        host, body, name=name, grid=(n, n),
        in_specs=[pl.BlockSpec((tq, FOX_W), lambda i, j: (i, 0)),
                  pl.BlockSpec((tq, FOX_W), lambda i, j: (jnp.minimum(i, j), 1)),
                  pl.BlockSpec((tq, FOX_W), lambda i, j: (jnp.minimum(i, j), 2)),
                  pl.BlockSpec((tq, LANES), lambda i, j: (i, 0)),
                  pl.BlockSpec((HEADS, tq), lambda i, j: (0, jnp.minimum(i, j)))],
        out_specs=[pl.BlockSpec((tq, FOX_W), lambda i, j: (i, 0)), pl.BlockSpec((tq, LANES), lambda i, j: (i, 0))],
        out_shape=[jax.ShapeDtypeStruct((t, FOX_W), f32), jax.ShapeDtypeStruct((t, LANES), f32)],
        scratch_shapes=[pltpu.VMEM((tq, FOX_W), f32), pltpu.VMEM((tq, LANES), f32), pltpu.VMEM((tq, LANES), f32),
                        pltpu.VMEM((tq, LANES), f32), pltpu.VMEM((tq, tq), f32), pltpu.VMEM((tq, tq), bf16)],
        compiler_params=_params(("arbitrary", "arbitrary")),
    )(qkv, qkv, qkv, cum, cum_t)


def _attn_delta(dmix, o, *, tm, name):
    t = o.shape[0]

    def body(do_ref, o_ref, d_ref):
        r = lax.broadcasted_iota(jnp.int32, (FOX_W, LANES), 0)
        c = lax.broadcasted_iota(jnp.int32, (FOX_W, LANES), 1)
        pick = jnp.where(r // HEAD_D == c, 1.0, 0.0).astype(f32)
        d_ref[...] = jnp.dot(do_ref[...] * o_ref[...], pick, preferred_element_type=f32,
                             precision=lax.Precision.HIGHEST)

    blk = pl.BlockSpec((tm, FOX_W), lambda i: (i, 0))
    return pl.pallas_call(
        body, name=name, grid=(t // tm,), in_specs=[blk, blk],
        out_specs=pl.BlockSpec((tm, LANES), lambda i: (i, 0)),
        out_shape=jax.ShapeDtypeStruct((t, LANES), f32), compiler_params=_params(("arbitrary",)),
    )(dmix, o)


def _attn_dq(qkv, dmix, cum, cum_t, lse, delta, *, name, host=None):
    t = qkv.shape[0]
    n = t // ATT_TILE
    tq = ATT_TILE

    def body(q_ref, k_ref, v_ref, do_ref, cq_ref, ck_ref, lse_ref, dl_ref, dq_ref, dc_ref, acc, dc_acc):
        i = pl.program_id(0)
        j = pl.program_id(1)

        @pl.when(j == 0)
        def _():
            acc[...] = jnp.zeros_like(acc)
            dc_acc[...] = jnp.zeros_like(dc_acc)

        def block(masked):
            mask = _causal(i, j, False) if masked else None
            for h in range(HEADS):
                hs = slice(HEAD_D * h, HEAD_D * (h + 1))
                kh = k_ref[:, hs]
                s = lax.dot_general(q_ref[:, hs] * ATT_SCALE, kh, _NT, preferred_element_type=f32)
                s = s + cq_ref[:, h:h + 1] - ck_ref[h:h + 1, :]
                if masked:
                    s = jnp.where(mask, s, NEG_BIG)
                p = jnp.exp(s - lse_ref[:, h:h + 1])
                dp = lax.dot_general(do_ref[:, hs].astype(bf16), v_ref[:, hs], _NT, preferred_element_type=f32)
                ds = p * (dp - dl_ref[:, h:h + 1])
                acc[:, hs] += jnp.dot(ds.astype(bf16), kh, preferred_element_type=f32)
                dc_acc[:, h:h + 1] += jnp.sum(ds, axis=-1, keepdims=True)

        @pl.when(j < i)
        def _():
            block(False)

        @pl.when(j == i)
        def _():
            block(True)
            dq_ref[...] = (acc[...] * ATT_SCALE).astype(bf16)
            dc_ref[...] = dc_acc[...]

    col = pl.BlockSpec((tq, LANES), lambda i, j: (i, 0))
    return _hosted_call(
        host, body, name=name, grid=(n, n),
        in_specs=[pl.BlockSpec((tq, FOX_W), lambda i, j: (i, 0)),
                  pl.BlockSpec((tq, FOX_W), lambda i, j: (jnp.minimum(i, j), 1)),
                  pl.BlockSpec((tq, FOX_W), lambda i, j: (jnp.minimum(i, j), 2)),
                  pl.BlockSpec((tq, FOX_W), lambda i, j: (i, 0)),
                  col, pl.BlockSpec((HEADS, tq), lambda i, j: (0, jnp.minimum(i, j))), col, col],
        out_specs=[pl.BlockSpec((tq, FOX_W), lambda i, j: (i, 0)), col],
        out_shape=[jax.ShapeDtypeStruct((t, FOX_W), bf16), jax.ShapeDtypeStruct((t, LANES), f32)],
        scratch_shapes=[pltpu.VMEM((tq, FOX_W), f32), pltpu.VMEM((tq, LANES), f32)],
        compiler_params=_params(("arbitrary", "arbitrary")),
    )(qkv, qkv, qkv, dmix, cum, cum_t, lse, delta)


def _attn_dkv(qkv, dmix, cum, cum_t, lse_t, delta_t, *, name):
    t = qkv.shape[0]
    n = t // ATT_TILE
    tk = ATT_TILE

    def body(q_ref, k_ref, v_ref, do_ref, cq_ref, ck_ref, lse_ref, dl_ref, dk_ref, dv_ref, dc_ref, dk_acc, dv_acc, dc_acc):
        j = pl.program_id(0)
        i = pl.program_id(1)

        @pl.when(i == 0)
        def _():
            dk_acc[...] = jnp.zeros_like(dk_acc)
            dv_acc[...] = jnp.zeros_like(dv_acc)
            dc_acc[...] = jnp.zeros_like(dc_acc)

        def block(masked):
            mask = _causal(i, j, True) if masked else None
            for h in range(HEADS):
                hs = slice(HEAD_D * h, HEAD_D * (h + 1))
                qh = q_ref[:, hs]
                doh = do_ref[:, hs].astype(bf16)
                s_t = lax.dot_general(k_ref[:, hs] * ATT_SCALE, qh, _NT, preferred_element_type=f32)
                s_t = s_t + cq_ref[h:h + 1, :] - ck_ref[:, h:h + 1]
                if masked:
                    s_t = jnp.where(mask, s_t, NEG_BIG)
                p_t = jnp.exp(s_t - lse_ref[h:h + 1, :])
                dv_acc[:, hs] += jnp.dot(p_t.astype(bf16), doh, preferred_element_type=f32)
                dp_t = lax.dot_general(v_ref[:, hs], doh, _NT, preferred_element_type=f32)
                ds_t = p_t * (dp_t - dl_ref[h:h + 1, :])
                dk_acc[:, hs] += jnp.dot(ds_t.astype(bf16), qh, preferred_element_type=f32)
                dc_acc[:, h:h + 1] -= jnp.sum(ds_t, axis=-1, keepdims=True)

        @pl.when(i > j)
        def _():
            block(False)

        @pl.when(i == j)
        def _():
            block(True)

        @pl.when(i == n - 1)
        def _():
            dk_ref[...] = (dk_acc[...] * ATT_SCALE).astype(bf16)
            dv_ref[...] = dv_acc[...].astype(bf16)
            dc_ref[...] = dc_acc[...]

    rowq = pl.BlockSpec((HEADS, tk), lambda j, i: (0, jnp.maximum(i, j)))
    return pl.pallas_call(
        body, name=name, grid=(n, n),
        in_specs=[pl.BlockSpec((tk, FOX_W), lambda j, i: (jnp.maximum(i, j), 0)),
                  pl.BlockSpec((tk, FOX_W), lambda j, i: (j, 1)),
                  pl.BlockSpec((tk, FOX_W), lambda j, i: (j, 2)),
                  pl.BlockSpec((tk, FOX_W), lambda j, i: (jnp.maximum(i, j), 0)),
                  rowq, pl.BlockSpec((tk, LANES), lambda j, i: (j, 0)), rowq, rowq],
        out_specs=[pl.BlockSpec((tk, FOX_W), lambda j, i: (j, 0)), pl.BlockSpec((tk, FOX_W), lambda j, i: (j, 0)),
                   pl.BlockSpec((tk, LANES), lambda j, i: (j, 0))],
        out_shape=[jax.ShapeDtypeStruct((t, FOX_W), bf16), jax.ShapeDtypeStruct((t, FOX_W), bf16),
                   jax.ShapeDtypeStruct((t, LANES), f32)],
        scratch_shapes=[pltpu.VMEM((tk, FOX_W), f32), pltpu.VMEM((tk, FOX_W), f32), pltpu.VMEM((tk, LANES), f32)],
        compiler_params=_params(("arbitrary", "arbitrary")),
    )(qkv, qkv, qkv, dmix, cum_t, cum, lse_t, delta_t)


ATT_W = HEADS * LANES


def _data_lane(h):
    return HEAD_D * (h % 2)


def _extra_lane(h):
    return HEAD_D - _data_lane(h)


def _split3(x):
    hi = x.astype(bf16)
    rest = x - hi.astype(f32)
    mid = rest.astype(bf16)
    lo = (rest - mid.astype(f32)).astype(bf16)
    return hi, mid, lo


def _augment(pair, h, first, second, fill=0.0):
    rows = pair.shape[0]
    lane = lax.broadcasted_iota(jnp.int32, (rows, LANES), 1)
    base = _extra_lane(h)
    own = (lane < HEAD_D) if h % 2 == 0 else (lane >= HEAD_D)
    out = jnp.where(own, pair, jnp.full((rows, LANES), fill, bf16))
    for off, src in ((0, first), (3, second)):
        for q in range(3):
            val = src[q] if isinstance(src, tuple) else jnp.full((rows, 1), src, bf16)
            out = jnp.where(lane == base + off + q, val, out)
    return out


def _attn_prep_fwd(qkv, cum, *, tm, name):
    t = qkv.shape[0]

    def body(q_ref, k_ref, v_ref, c_ref, qa_ref, ka_ref, va_ref):
        for h in range(HEADS):
            pair = slice(LANES * (h // 2), LANES * (h // 2 + 1))
            hs = slice(LANES * h, LANES * (h + 1))
            c3 = _split3(c_ref[:, h:h + 1])
            qa_ref[:, hs] = _augment(q_ref[:, pair] * ATT_SCALE, h, c3, 1.0)
            ka_ref[:, hs] = _augment(k_ref[:, pair], h, 1.0, tuple(-p for p in c3))
            va_ref[:, hs] = _augment(v_ref[:, pair], h, 1.0, 1.0, fill=1.0)

    wide = pl.BlockSpec((tm, ATT_W), lambda i: (i, 0))
    out = jax.ShapeDtypeStruct((t, ATT_W), bf16)
    return pl.pallas_call(
        body, name=name, grid=(t // tm,),
        in_specs=[pl.BlockSpec((tm, FOX_W), lambda i: (i, 0)), pl.BlockSpec((tm, FOX_W), lambda i: (i, 1)),
                  pl.BlockSpec((tm, FOX_W), lambda i: (i, 2)), pl.BlockSpec((tm, LANES), lambda i: (i, 0))],
        out_specs=[wide] * 3, out_shape=[out] * 3, compiler_params=_params(("arbitrary",)),
    )(qkv, qkv, qkv, cum)


def _attn_prep_bwd(qkv, cum, lse, dmix, o, *, tm, name):
    t = qkv.shape[0]

    def body(q_ref, c_ref, l_ref, do_ref, o_ref, qa_ref, da_ref):
        for h in range(HEADS):
            pair = slice(LANES * (h // 2), LANES * (h // 2 + 1))
            src = slice(HEAD_D * h, HEAD_D * (h + 1))
            hs = slice(LANES * h, LANES * (h + 1))
            delta = jnp.sum(do_ref[:, src] * o_ref[:, src], axis=-1, keepdims=True)
            qa_ref[:, hs] = _augment(q_ref[:, pair] * ATT_SCALE, h,
                                     _split3(c_ref[:, h:h + 1] - l_ref[:, h:h + 1]), 1.0)
            da_ref[:, hs] = _augment(do_ref[:, pair].astype(bf16), h, tuple(-p for p in _split3(delta)), 0.0)

    wide = pl.BlockSpec((tm, ATT_W), lambda i: (i, 0))
    half = pl.BlockSpec((tm, FOX_W), lambda i: (i, 0))
    col = pl.BlockSpec((tm, LANES), lambda i: (i, 0))
    out = jax.ShapeDtypeStruct((t, ATT_W), bf16)
    return pl.pallas_call(
        body, name=name, grid=(t // tm,), in_specs=[half, col, col, half, half],
        out_specs=[wide] * 2, out_shape=[out] * 2, compiler_params=_params(("arbitrary",)),
    )(qkv, cum, lse, dmix, o)


def _attn_fwd2(q_aug, k_aug, v_aug, *, name, host=None):
    t = q_aug.shape[0]
    n = t // ATT_TILE
    tq = ATT_TILE

    def body(q_ref, k_ref, v_ref, o_ref, lse_ref, acc, m_s):
        i = pl.program_id(0)
        j = pl.program_id(1)

        @pl.when(j == 0)
        def _():
            acc[...] = jnp.zeros_like(acc)
            m_s[...] = jnp.full_like(m_s, NEG_BIG)

        def block(masked):
            mask = _causal(i, j, False) if masked else None
            for h in range(HEADS):
                hs = slice(LANES * h, LANES * (h + 1))
                s = lax.dot_general(q_ref[:, hs], k_ref[:, hs], _NT, preferred_element_type=f32)
                if masked:
                    s = jnp.where(mask, s, NEG_BIG)
                m_old = m_s[:, h:h + 1]
                m_new = jnp.maximum(m_old, jnp.max(s, axis=-1, keepdims=True))
                p = jnp.exp(s - m_new).astype(bf16)
                acc[h] = jnp.exp(m_old - m_new) * acc[h] + jnp.dot(p, v_ref[:, hs], preferred_element_type=f32)
                m_s[:, h:h + 1] = m_new

        @pl.when(j < i)
        def _():
            block(False)

        @pl.when(j == i)
        def _():
            block(True)
            lse_ref[...] = jnp.zeros_like(lse_ref)
            for h in range(HEADS):
                a = acc[h]
                l = a[:, _extra_lane(h):_extra_lane(h) + 1]
                o_ref[:, HEAD_D * h:HEAD_D * (h + 1)] = a[:, _data_lane(h):_data_lane(h) + HEAD_D] / l
                lse_ref[:, h:h + 1] = m_s[:, h:h + 1] + jnp.log(l)

    kv = pl.BlockSpec((tq, ATT_W), lambda i, j: (jnp.minimum(i, j), 0))
    return _hosted_call(
        host, body, name=name, grid=(n, n),
        in_specs=[pl.BlockSpec((tq, ATT_W), lambda i, j: (i, 0)), kv, kv],
        out_specs=[pl.BlockSpec((tq, FOX_W), lambda i, j: (i, 0)), pl.BlockSpec((tq, LANES), lambda i, j: (i, 0))],
        out_shape=[jax.ShapeDtypeStruct((t, FOX_W), f32), jax.ShapeDtypeStruct((t, LANES), f32)],
        scratch_shapes=[pltpu.VMEM((HEADS, tq, LANES), f32), pltpu.VMEM((tq, LANES), f32)],
        compiler_params=_params(("arbitrary", "arbitrary")),
    )(q_aug, k_aug, v_aug)


def _attn_dq2(qb_aug, k_aug, v_aug, do_aug, *, name, host=None):
    t = qb_aug.shape[0]
    n = t // ATT_TILE
    tq = ATT_TILE

    def body(q_ref, k_ref, v_ref, do_ref, dq_ref, dc_ref, acc):
        i = pl.program_id(0)
        j = pl.program_id(1)

        @pl.when(j == 0)
        def _():
            acc[...] = jnp.zeros_like(acc)

        def block(masked):
            mask = _causal(i, j, False) if masked else None
            for h in range(HEADS):
                hs = slice(LANES * h, LANES * (h + 1))
                kh = k_ref[:, hs]
                s = lax.dot_general(q_ref[:, hs], kh, _NT, preferred_element_type=f32)
                if masked:
                    s = jnp.where(mask, s, NEG_BIG)
                dp = lax.dot_general(do_ref[:, hs], v_ref[:, hs], _NT, preferred_element_type=f32)
                ds = (jnp.exp(s) * dp).astype(bf16)
                acc[h] += jnp.dot(ds, kh, preferred_element_type=f32)

        @pl.when(j < i)
        def _():
            block(False)

        @pl.when(j == i)
        def _():
            block(True)
            dc_ref[...] = jnp.zeros_like(dc_ref)
            for h in range(HEADS):
                a = acc[h]
                dq_ref[:, HEAD_D * h:HEAD_D * (h + 1)] = (
                    a[:, _data_lane(h):_data_lane(h) + HEAD_D] * ATT_SCALE).astype(bf16)
                dc_ref[:, h:h + 1] = a[:, _extra_lane(h):_extra_lane(h) + 1]

    own = pl.BlockSpec((tq, ATT_W), lambda i, j: (i, 0))
    kv = pl.BlockSpec((tq, ATT_W), lambda i, j: (jnp.minimum(i, j), 0))
    return _hosted_call(
        host, body, name=name, grid=(n, n), in_specs=[own, kv, kv, own],
        out_specs=[pl.BlockSpec((tq, FOX_W), lambda i, j: (i, 0)), pl.BlockSpec((tq, LANES), lambda i, j: (i, 0))],
        out_shape=[jax.ShapeDtypeStruct((t, FOX_W), bf16), jax.ShapeDtypeStruct((t, LANES), f32)],
        scratch_shapes=[pltpu.VMEM((HEADS, tq, LANES), f32)],
        compiler_params=_params(("arbitrary", "arbitrary")),
    )(qb_aug, k_aug, v_aug, do_aug)


def _attn_dkv2(qb_aug, k_aug, v_aug, do_aug, *, name, host=None):
    t = qb_aug.shape[0]
    n = t // ATT_TILE
    tk = ATT_TILE

    def body(q_ref, k_ref, v_ref, do_ref, dk_ref, dv_ref, dc_ref, dk_acc, dv_acc):
        j = pl.program_id(0)
        i = pl.program_id(1)

        @pl.when(i == 0)
        def _():
            dk_acc[...] = jnp.zeros_like(dk_acc)
            dv_acc[...] = jnp.zeros_like(dv_acc)

        def block(masked):
            mask = _causal(i, j, True) if masked else None
            for h in range(HEADS):
                hs = slice(LANES * h, LANES * (h + 1))
                qh = q_ref[:, hs]
                doh = do_ref[:, hs]
                s_t = lax.dot_general(k_ref[:, hs], qh, _NT, preferred_element_type=f32)
                if masked:
                    s_t = jnp.where(mask, s_t, NEG_BIG)
                p_t = jnp.exp(s_t)
                dv_acc[h] += jnp.dot(p_t.astype(bf16), doh, preferred_element_type=f32)
                dp_t = lax.dot_general(v_ref[:, hs], doh, _NT, preferred_element_type=f32)
                dk_acc[h] += jnp.dot((p_t * dp_t).astype(bf16), qh, preferred_element_type=f32)

        @pl.when(i > j)
        def _():
            block(False)

        @pl.when(i == j)
        def _():
            block(True)

        @pl.when(i == n - 1)
        def _():
            dc_ref[...] = jnp.zeros_like(dc_ref)
            for h in range(HEADS):
                a = dk_acc[h]
                cols = slice(_data_lane(h), _data_lane(h) + HEAD_D)
                dk_ref[:, HEAD_D * h:HEAD_D * (h + 1)] = a[:, cols].astype(bf16)
                dv_ref[:, HEAD_D * h:HEAD_D * (h + 1)] = dv_acc[h][:, cols].astype(bf16)
                dc_ref[:, h:h + 1] = -a[:, _extra_lane(h) + 3:_extra_lane(h) + 4]

    own = pl.BlockSpec((tk, ATT_W), lambda j, i: (j, 0))
    qs = pl.BlockSpec((tk, ATT_W), lambda j, i: (jnp.maximum(i, j), 0))
    half = pl.BlockSpec((tk, FOX_W), lambda j, i: (j, 0))
    return _hosted_call(
        host, body, name=name, grid=(n, n), in_specs=[qs, own, own, qs],
        out_specs=[half, half, pl.BlockSpec((tk, LANES), lambda j, i: (j, 0))],
        out_shape=[jax.ShapeDtypeStruct((t, FOX_W), bf16), jax.ShapeDtypeStruct((t, FOX_W), bf16),
                   jax.ShapeDtypeStruct((t, LANES), f32)],
        scratch_shapes=[pltpu.VMEM((HEADS, tk, LANES), f32), pltpu.VMEM((HEADS, tk, LANES), f32)],
        compiler_params=_params(("arbitrary", "arbitrary")),
    )(qb_aug, k_aug, v_aug, do_aug)


LRU_CHUNK = 64
SUB = 8


def _row_ids(n):
    return lax.broadcasted_iota(jnp.int32, (n, LANES), 0)


def _shift_rows_down(ext, s):
    return pltpu.roll(ext, s, axis=0)[SUB:, :]


def _shift_rows_up(ext, s, n):
    return pltpu.roll(ext, ext.shape[0] - s, axis=0)[:n, :]


def _lru_gates(u, wa_ref, ba_ref, wx_ref, bx_ref, sp):
    ub = u.astype(bf16)
    r = _sigmoid(jnp.dot(ub, wa_ref[...], preferred_element_type=f32) + ba_ref[...])
    gi = _sigmoid(jnp.dot(ub, wx_ref[...], preferred_element_type=f32) + bx_ref[...])
    log_a = -LRU_C * r * sp
    a = jnp.exp(log_a)
    s = jnp.sqrt(_one_minus_exp(2.0 * log_a))
    return r, gi, a, s


def _conv_window(lx_ref, r0, ci):
    cur = lx_ref[pl.ds(r0, LRU_CHUNK), :]
    p0 = pl.multiple_of(jnp.maximum(r0 - SUB, 0), SUB)
    prev = jnp.where(ci > 0, lx_ref[pl.ds(p0, SUB), :], 0.0)
    return cur, jnp.concatenate([prev, cur], axis=0)


def _lru_fwd(zl, conv_w, conv_b, wa, ba, wx, bx, lam, *, name, host=None):
    t = zl.shape[0]
    n_chunk = t // LRU_CHUNK

    def body(lx_ref, lg_ref, cw_ref, cb_ref, wa_ref, ba_ref, wx_ref, bx_ref, lam_ref, u_ref, h_ref, y_ref):
        sp = _softplus(-lam_ref[...])
        rows = _row_ids(SUB)

        def chunk(ci, hc):
            r0 = pl.multiple_of(ci * LRU_CHUNK, LRU_CHUNK)
            cur, ext = _conv_window(lx_ref, r0, ci)
            u = cb_ref[...] + cw_ref[3:4, :] * cur
            for k in range(3):
                u = u + cw_ref[k:k + 1, :] * _shift_rows_down(ext, 3 - k)
            r, gi, a, s = _lru_gates(u, wa_ref, ba_ref, wx_ref, bx_ref, sp)
            b = s * (gi * u)
            tiles = []
            for q in range(LRU_CHUNK // SUB):
                ta = a[SUB * q:SUB * (q + 1), :]
                tb = b[SUB * q:SUB * (q + 1), :]
                for d in (1, 2, 4):
                    a_sh = jnp.where(rows >= d, pltpu.roll(ta, d, axis=0), 1.0)
                    b_sh = jnp.where(rows >= d, pltpu.roll(tb, d, axis=0), 0.0)
                    tb = ta * b_sh + tb
                    ta = ta * a_sh
                hq = tb + ta * hc
                hc = hq[SUB - 1:SUB, :]
                tiles.append(hq)
            h = jnp.concatenate(tiles, axis=0)
            u_ref[pl.ds(r0, LRU_CHUNK), :] = u
            h_ref[pl.ds(r0, LRU_CHUNK), :] = h
            gel, _ = _gelu_and_grad(lg_ref[pl.ds(r0, LRU_CHUNK), :])
            y_ref[pl.ds(r0, LRU_CHUNK), :] = gel * h
            return hc

        lax.fori_loop(0, n_chunk, chunk, jnp.zeros((1, LANES), f32))

    seq = lambda cb: pl.BlockSpec((t, LANES), lambda c, cb=cb: (0, c + cb))
    rowc = pl.BlockSpec((1, LANES), lambda c: (0, c))
    diag = pl.BlockSpec((LANES, LANES), lambda c: (c, c))
    out = jax.ShapeDtypeStruct((t, LRU_W), f32)
    return _hosted_call(
        host, body, name=name, grid=(LRU_W // LANES,),
        in_specs=[seq(0), seq(4), pl.BlockSpec((4, LANES), lambda c: (0, c)), rowc, diag, rowc, diag, rowc, rowc],
        out_specs=[seq(0)] * 3, out_shape=[out] * 3,
        compiler_params=_params(("arbitrary",)),
    )(zl, zl, conv_w, conv_b, wa, ba, wx, bx, lam)


def _lru_bwd(dmix, zl, u_all, h_all, conv_w, wa, ba, wx, bx, lam, *, name, host=None):
    t = zl.shape[0]
    n_chunk = t // LRU_CHUNK

    def body(dy_ref, lx_ref, lg_ref, u_ref, h_ref, cw_ref, wa_ref, ba_ref, wx_ref, bx_ref, lam_ref,
             dlx_ref, dlg_ref, dcw_ref, dcb_ref, dba_ref, dbx_ref, dlam_ref, dwa_ref, dwx_ref, dpr_s, dpx_s):
        lam_v = lam_ref[...]
        sp = _softplus(-lam_v)
        rows = _row_ids(SUB)
        rows_c = _row_ids(LRU_CHUNK)
        zero_row = jnp.zeros((1, LANES), f32)

        def chunk(step, carry):
            dh_c, a_next0, du_next, dsp, dba, dbx, dcb, dw0, dw1, dw2, dw3 = carry
            ci = n_chunk - 1 - step
            r0 = pl.multiple_of(ci * LRU_CHUNK, LRU_CHUNK)
            sl = pl.ds(r0, LRU_CHUNK)
            u = u_ref[sl, :]
            r, gi, a, s = _lru_gates(u, wa_ref, ba_ref, wx_ref, bx_ref, sp)
            h = h_ref[sl, :]
            p0 = pl.multiple_of(jnp.maximum(r0 - SUB, 0), SUB)
            h_before = jnp.where(ci > 0, h_ref[pl.ds(p0, SUB), :], 0.0)[SUB - 1:SUB, :]
            h_prev = jnp.where(rows_c == 0, h_before, pltpu.roll(h, 1, axis=0))
            gel, dgel = _gelu_and_grad(lg_ref[sl, :])
            dy = dy_ref[sl, :]
            dlg_ref[sl, :] = (dy * h * dgel).astype(bf16)
            g_in = dy * gel
            a_next = jnp.where(rows_c == LRU_CHUNK - 1, a_next0, pltpu.roll(a, LRU_CHUNK - 1, axis=0))
            tiles = [None] * (LRU_CHUNK // SUB)
            for q in reversed(range(LRU_CHUNK // SUB)):
                ta = a_next[SUB * q:SUB * (q + 1), :]
                tb = g_in[SUB * q:SUB * (q + 1), :]
                for d in (1, 2, 4):
                    a_sh = jnp.where(rows < SUB - d, pltpu.roll(ta, SUB - d, axis=0), 1.0)
                    b_sh = jnp.where(rows < SUB - d, pltpu.roll(tb, SUB - d, axis=0), 0.0)
                    tb = ta * b_sh + tb
                    ta = ta * a_sh
                dhq = tb + ta * dh_c
                dh_c = dhq[0:1, :]
                tiles[q] = dhq
            dh = jnp.concatenate(tiles, axis=0)
            da = dh * h_prev
            ds = dh * gi * u
            dgi = dh * s * u
            du = dh * s * gi
            dlog_a = da * a - ds * (a * a) / s
            dr = dlog_a * (-LRU_C * sp)
            dsp = dsp + jnp.sum(dlog_a * (-LRU_C * r), axis=0, keepdims=True)
            dpr = dr * r * (1.0 - r)
            dpx = dgi * gi * (1.0 - gi)
            dprb = dpr.astype(bf16)
            dpxb = dpx.astype(bf16)
            dpr_s[sl, :] = dprb
            dpx_s[sl, :] = dpxb
            du = du + (lax.dot_general(dprb, wa_ref[...], _NT, preferred_element_type=f32)
                       + lax.dot_general(dpxb, wx_ref[...], _NT, preferred_element_type=f32))
            dba = dba + jnp.sum(dpr, axis=0, keepdims=True)
            dbx = dbx + jnp.sum(dpx, axis=0, keepdims=True)
            dcb = dcb + jnp.sum(du, axis=0, keepdims=True)
            du_ext = jnp.concatenate([du, du_next], axis=0)
            dlx = cw_ref[3:4, :] * du
            for k in range(3):
                dlx = dlx + cw_ref[k:k + 1, :] * _shift_rows_up(du_ext, 3 - k, LRU_CHUNK)
            dlx_ref[sl, :] = dlx.astype(bf16)
            cur, ext = _conv_window(lx_ref, r0, ci)
            dws = [dw0, dw1, dw2, dw3 + jnp.sum(du * cur, axis=0, keepdims=True)]
            for k in range(3):
                dws[k] = dws[k] + jnp.sum(du * _shift_rows_down(ext, 3 - k), axis=0, keepdims=True)
            return (dh_c, a[0:1, :], du[0:SUB, :], dsp, dba, dbx, dcb, dws[0], dws[1], dws[2], dws[3])

        init = (zero_row, zero_row, jnp.zeros((SUB, LANES), f32)) + (zero_row,) * 8
        out = lax.fori_loop(0, n_chunk, chunk, init)
        _, _, _, dsp, dba, dbx, dcb, dw0, dw1, dw2, dw3 = out
        dlam_ref[...] = dsp * (-_sigmoid(-lam_v))
        dba_ref[...] = dba
        dbx_ref[...] = dbx
        dcb_ref[...] = dcb
        dcw_ref[...] = jnp.concatenate([dw0, dw1, dw2, dw3], axis=0)
        ub = u_ref[...].astype(bf16)
        dwa_ref[...] = lax.dot_general(ub, dpr_s[...], _TN, preferred_element_type=f32)
        dwx_ref[...] = lax.dot_general(ub, dpx_s[...], _TN, preferred_element_type=f32)

    seq = lambda cb: pl.BlockSpec((t, LANES), lambda c, cb=cb: (0, c + cb))
    rowc = pl.BlockSpec((1, LANES), lambda c: (0, c))
    diag = pl.BlockSpec((LANES, LANES), lambda c: (c, c))
    gate_out = pl.BlockSpec((None, LANES, LANES), lambda c: (c, 0, 0))
    row_shape = jax.ShapeDtypeStruct((1, LRU_W), f32)
    return _hosted_call(
        host, body, name=name, grid=(LRU_W // LANES,),
        in_specs=[seq(4), seq(0), seq(4), seq(0), seq(0), pl.BlockSpec((4, LANES), lambda c: (0, c)),
                  diag, rowc, diag, rowc, rowc],
        out_specs=[seq(0), seq(0), pl.BlockSpec((4, LANES), lambda c: (0, c)), rowc, rowc, rowc, rowc,
                   gate_out, gate_out],
        out_shape=[jax.ShapeDtypeStruct((t, LRU_W), bf16)] * 2
        + [jax.ShapeDtypeStruct((4, LRU_W), f32)] + [row_shape] * 4
        + [jax.ShapeDtypeStruct((LRU_W // LANES, LANES, LANES), f32)] * 2,
        scratch_shapes=[pltpu.VMEM((t, LANES), bf16), pltpu.VMEM((t, LANES), bf16)],
        compiler_params=_params(("arbitrary",)),
    )(dmix, zl, zl, u_all, h_all, conv_w, wa, ba, wx, bx, lam)


def _block_diag(w):
    eye = jnp.eye(HEADS, dtype=w.dtype)
    return jnp.einsum("hij,hk->hikj", w, eye).reshape(LRU_W, LRU_W)


def _diag_blocks(dw):
    top = dw[:, :HEAD_D, :HEAD_D]
    bot = dw[:, HEAD_D:, HEAD_D:]
    return jnp.stack([top, bot], axis=1).reshape(HEADS, HEAD_D, HEAD_D)


def _local_step(x, target, sent, small, *, tm=512):
    t = x.shape[0]
    ones = jnp.ones((1, D_MODEL), f32)
    zeros = jnp.zeros((1, D_MODEL), f32)
    ln1 = (small["ln1_g"], small["ln1_b"])
    ln2 = (small["ln2_g"], small["ln2_b"])
    ln3 = (small["ln3_g"], small["ln3_b"])

    wg1, wu1, wd1 = _exchange([sent["ffn1_w_gate"], sent["ffn1_w_up"], sent["ffn1_w_down"]], gather=True,
                              name="gather_ffn1")
    xh1, rs1, hg1, hu1, w_in_g, w_out_g, conv_w_g = _ffn_fwd(
        x, ones, zeros, wg1, wu1, wd1, tm=tm, name="ffn1_fwd",
        host=_Exchange([sent["w_in"], sent["w_out"], sent["conv_w"]], gather=True))
    w_in = jnp.pad(w_in_g.transpose(1, 0, 2).reshape(D_MODEL, IN_COLS), ((0, 0), (0, 21 * LANES - IN_COLS)))
    w_out = w_out_g.reshape(D_MODEL, D_MODEL)
    conv_w = conv_w_g.transpose(1, 0, 2).reshape(4, LRU_W)
    qkv = _mm(xh1, w_in, mode="nn", out_dtype=bf16, tm=tm, tn=512, tk=D_MODEL, name="qkv_fwd",
              affine=ln1, b_cols=(0, 1536))
    zl = _mm(xh1, w_in, mode="nn", out_dtype=f32, tm=tm, tn=512, tk=D_MODEL, name="zl_fwd",
             affine=ln1, b_cols=(3, 1024))
    zfg = _mm(xh1, w_in, mode="nn", out_dtype=f32, tm=tm, tn=LANES, tk=D_MODEL, name="zfg_fwd",
              affine=ln1, b_cols=(20, LANES))
    bfg = jnp.pad(small["b_forget"], ((0, 0), (0, LANES - HEADS)))
    cum = _cum_fwd(zfg, bfg, name="cum_fwd")
    q_aug, k_aug, v_aug = _attn_prep_fwd(qkv, cum, tm=tm, name="attn_prep_fwd")
    o, lse, wg2, wu2 = _attn_fwd2(q_aug, k_aug, v_aug, name="attn_fwd",
                                  host=_Exchange([sent["ffn2_w_gate"], sent["ffn2_w_up"]], gather=True))
    wa_bd = _block_diag(small["rg_wa"]).astype(bf16)
    wx_bd = _block_diag(small["rg_wx"]).astype(bf16)
    ba = small["rg_ba"].reshape(1, LRU_W)
    bx = small["rg_bx"].reshape(1, LRU_W)
    u, h, lru, wd2 = _lru_fwd(zl, conv_w, small["conv_b"], wa_bd, ba, wx_bd, bx, small["lru_lambda"],
                              name="lru_fwd", host=_Exchange([sent["ffn2_w_down"]], gather=True))
    xh2, rs2 = _mmln([(o, 0, FOX_W, w_out, 0, D_MODEL, "nn"), (lru, 0, LRU_W, w_out, 1, D_MODEL, "nn")],
                     tm=tm, name="mix_fwd", resid=("affine", xh1) + ln1, resid_scale=ALPHA, epi="ln_fwd")
    xh3, rs3, hg2, hu2 = _ffn_fwd(xh2, ln2[0], ln2[1], wg2, wu2, wd2, tm=tm, name="ffn2_fwd")

    dpre3, sq_rows, g_ln3g, g_ln3b = _loss_bwd(xh3, rs3, ln3[0], ln3[1], target, tm=tm, name="loss_bwd")
    dpre2, g_ln2g, g_ln2b, dhg2, dhu2, a2 = _ffn_bwd(dpre3, hg2, hu2, wg2, wu2, wd2,
                                                     (xh2, rs2, ln2[0]), tm=tm, name="ffn2_bwd")
    wgrad = dict(out_dtype=bf16, tm=D_MODEL, mb=1, tn=FF_TILE, nb=4, tk=512, out_blocked=True)
    wdgrad = dict(out_dtype=bf16, tm=512, mb=4, tn=D_MODEL, nb=1, tk=512, out_scale=0.5)
    g_wg2 = _mm_tn(xh2, dhg2, name="g_wg2", affine=ln2, **wgrad)
    g_wu2 = _mm_tn(xh2, dhu2, name="g_wu2", affine=ln2, **wgrad)
    g_wd2 = _mm_tn(a2, dpre3, name="g_wd2", **wdgrad)

    dmix = _mmln([(dpre2, 0, D_MODEL, w_out, 0, D_MODEL, "nt")], tm=tm, name="dmix_bwd")
    g_wout_a = _mm(o, dpre2, mode="tn", out_dtype=bf16, tm=512, tn=D_MODEL, tk=512, name="g_wout_fox")
    g_wout_b = _mm(lru, dpre2, mode="tn", out_dtype=bf16, tm=512, tn=D_MODEL, tk=512, name="g_wout_lru")
    dlx, dlg, g_cw, g_cb, g_ba, g_bx, g_lam, g_wa4, g_wx4, *p_wg2 = _lru_bwd(
        dmix, zl, u, h, conv_w, wa_bd, ba, wx_bd, bx, small["lru_lambda"], name="lru_bwd",
        host=_Exchange([g_wg2], gather=False))
    p_wg2 = p_wg2[0]
    qb_aug, do_aug = _attn_prep_bwd(qkv, cum, lse, dmix, o, tm=tm, name="attn_prep_bwd")
    dq, dcum_q, p_wu2 = _attn_dq2(qb_aug, k_aug, v_aug, do_aug, name="attn_dq",
                                  host=_Exchange([g_wu2], gather=False))
    g_wout_blocked = jnp.concatenate([g_wout_a, g_wout_b], axis=0).reshape(N_DEV, D_MODEL // N_DEV, D_MODEL)
    dk, dv, dcum_k, p_wd2, p_wout = _attn_dkv2(
        qb_aug, k_aug, v_aug, do_aug, name="attn_dkv",
        host=_Exchange([g_wd2.reshape(N_DEV, FF_TILE, D_MODEL), g_wout_blocked], gather=False))
    dfg, g_bf = _cum_bwd(dcum_q, dcum_k, zfg, bfg, name="cum_bwd")

    dz = [(dq, 0, 512), (dk, 1, 512), (dv, 2, 512), (dlx, 3, 512), (dlg, 4, 512), (dfg, 20, LANES)]
    dpre1, g_ln1g, g_ln1b = _mmln(
        [(arr, 0, w, w_in, cb, w, "nt") for (arr, cb, w) in dz],
        tm=tm, name="dx1_bwd", resid=("plain", dpre2), resid_scale=ALPHA, epi="ln_bwd", ln=(xh1, rs1, ln1[0]))
    g_win = [_mm(xh1, arr, mode="tn", out_dtype=bf16, tm=D_MODEL, tn=w, tk=512, name=f"g_win{n}", affine=ln1)
             for n, (arr, cb, w) in enumerate(dz)]
    g_win_full = jnp.concatenate([g[:, :w] for g, (_, _, w) in zip(g_win, dz)], axis=1)[:, :IN_COLS]
    g_win_blocked = g_win_full.reshape(D_MODEL, N_DEV, IN_SHARD).transpose(1, 0, 2)
    dhg1, dhu1, a1, p_win = _ffn_bwd_act(dpre1, hg1, hu1, wd1, tm=tm, name="ffn1_bwd_act",
                                         host=_Exchange([g_win_blocked], gather=False))
    small_g = {
        "ln1_g": g_ln1g, "ln1_b": g_ln1b, "b_forget": g_bf[:, :HEADS], "conv_w": g_cw, "conv_b": g_cb,
        "rg_wa": _diag_blocks(g_wa4), "rg_ba": g_ba.reshape(HEADS, HEAD_D),
        "rg_wx": _diag_blocks(g_wx4), "rg_bx": g_bx.reshape(HEADS, HEAD_D), "lru_lambda": g_lam,
        "ln2_g": g_ln2g, "ln2_b": g_ln2b, "ln3_g": g_ln3g, "ln3_b": g_ln3b,
    }
    pieces = [small_g[n].reshape(-1) for n in PACKED]
    packed = jnp.concatenate(pieces + [jnp.zeros((PACK_ROWS * LANES - sum(p.shape[0] for p in pieces),), f32)])
    g_wg1, all_packed = _mm_tn(x, dhg1, name="g_wg1",
                               host=_Exchange([packed.reshape(PACK_ROWS, LANES)], gather=True), **wgrad)
    g_wu1, p_wg1 = _mm_tn(x, dhu1, name="g_wu1", host=_Exchange([g_wg1], gather=False), **wgrad)
    g_wd1, p_wu1 = _mm_tn(a1, dpre1, name="g_wd1", host=_Exchange([g_wu1], gather=False), **wdgrad)
    grad_x, p_wd1 = _ffn_bwd_dx(dpre1, dhg1, dhu1, wg1, wu1, tm=tm, name="ffn1_bwd_dx",
                                host=_Exchange([g_wd1.reshape(N_DEV, FF_TILE, D_MODEL)], gather=False))
    parts = {
        "ffn1_w_gate": p_wg1, "ffn1_w_up": p_wu1, "ffn1_w_down": p_wd1, "w_in": p_win, "w_out": p_wout,
        "ffn2_w_gate": p_wg2, "ffn2_w_up": p_wu2, "ffn2_w_down": p_wd2,
    }
    return sq_rows, grad_x, parts, all_packed, {n: small_g[n].shape for n in PACKED}


def _adam_math(w, g, m, v):
    m2 = ADAM_B1 * m + (1.0 - ADAM_B1) * g
    v2 = ADAM_B2 * v + (1.0 - ADAM_B2) * (g * g)
    m_hat = m2 / (1.0 - ADAM_B1 ** ADAM_STEP)
    v_hat = v2 / (1.0 - ADAM_B2 ** ADAM_STEP)
    delta = -ADAM_LR * (m_hat / (jnp.sqrt(v_hat) + ADAM_EPS) + ADAM_WD * w)
    return delta, m2, v2


ADAM_TILE_ELEMS = 128 * 1024


def _adamw_big(parts, w, m, v, *, name):
    r, c = w.shape
    tr = max(d for d in range(8, r + 1, 8) if r % d == 0 and d * c <= ADAM_TILE_ELEMS)

    def body(p_ref, w_ref, m_ref, v_ref, g_ref, d_ref, m2_ref, v2_ref):
        g = p_ref[0].astype(f32)
        for q in range(1, N_DEV):
            g = g + p_ref[q].astype(f32)
        d, m2, v2 = _adam_math(w_ref[...], g, m_ref[...], v_ref[...])
        g_ref[...] = g
        d_ref[...] = d
        m2_ref[...] = m2
        v2_ref[...] = v2

    blk = pl.BlockSpec((tr, c), lambda i: (i, 0))
    return pl.pallas_call(
        body, name=name, grid=(r // tr,),
        in_specs=[pl.BlockSpec((N_DEV, tr, c), lambda i: (0, i, 0)), blk, blk, blk],
        out_specs=[blk] * 4, out_shape=[jax.ShapeDtypeStruct((r, c), f32)] * 4,
        compiler_params=_params(("arbitrary",)),
    )(parts, w, m, v)


def _adamw_small(items, *, name):
    n = len(items)

    def body(*refs):
        ins, outs = refs[:4 * n], refs[4 * n:]
        for k in range(n):
            g, w, m, v = (ins[4 * k + q][...] for q in range(4))
            d, m2, v2 = _adam_math(w, g, m, v)
            outs[3 * k][...] = d
            outs[3 * k + 1][...] = m2
            outs[3 * k + 2][...] = v2

    vm = pl.BlockSpec(memory_space=pltpu.VMEM)
    flat = [a for item in items for a in item]
    out_shape = [jax.ShapeDtypeStruct(item[1].shape, f32) for item in items for _ in range(3)]
    return pl.pallas_call(
        body, name=name, in_specs=[vm] * (4 * n), out_specs=[vm] * (3 * n), out_shape=out_shape,
    )(*flat)


def _sum_parts(parts, *, name):
    def body(p_ref, o_ref):
        acc = p_ref[0]
        for q in range(1, N_DEV):
            acc = acc + p_ref[q]
        o_ref[...] = acc

    vm = pl.BlockSpec(memory_space=pltpu.VMEM)
    return pl.pallas_call(
        body, name=name, in_specs=[vm], out_specs=vm, out_shape=jax.ShapeDtypeStruct(parts.shape[1:], f32),
    )(parts)


WEIGHTS = ["ffn1_w_gate", "ffn1_w_up", "ffn1_w_down", "ln1_g", "ln1_b", "w_in", "b_forget", "conv_w", "conv_b",
           "rg_wa", "rg_ba", "rg_wx", "rg_bx", "lru_lambda", "w_out", "ln2_g", "ln2_b",
           "ffn2_w_gate", "ffn2_w_up", "ffn2_w_down", "ln3_g", "ln3_b"]
BIG = ["ffn1_w_gate", "ffn1_w_up", "ffn1_w_down", "w_in", "w_out", "ffn2_w_gate", "ffn2_w_up", "ffn2_w_down"]
PACKED = ["ln1_g", "ln1_b", "ln2_g", "ln2_b", "ln3_g", "ln3_b", "conv_b", "rg_ba", "rg_bx", "lru_lambda",
          "conv_w", "rg_wa", "rg_wx", "b_forget"]
PACK_ROWS = 600


def _two_d(a):
    return a.reshape((-1, a.shape[-1]))


def _transport(a):
    return _two_d(a)


def kernel(x, ffn1_w_gate, ffn1_w_up, ffn1_w_down, ln1_g, ln1_b, w_in, b_forget, conv_w, conv_b, rg_wa, rg_ba, rg_wx, rg_bx, lru_lambda, w_out, ln2_g, ln2_b, ffn2_w_gate, ffn2_w_up, ffn2_w_down, ln3_g, ln3_b, loss_target, m_ffn1_w_gate, m_ffn1_w_up, m_ffn1_w_down, m_ln1_g, m_ln1_b, m_w_in, m_b_forget, m_conv_w, m_conv_b, m_rg_wa, m_rg_ba, m_rg_wx, m_rg_bx, m_lru_lambda, m_w_out, m_ln2_g, m_ln2_b, m_ffn2_w_gate, m_ffn2_w_up, m_ffn2_w_down, m_ln3_g, m_ln3_b, v_ffn1_w_gate, v_ffn1_w_up, v_ffn1_w_down, v_ln1_g, v_ln1_b, v_w_in, v_b_forget, v_conv_w, v_conv_b, v_rg_wa, v_rg_ba, v_rg_wx, v_rg_bx, v_lru_lambda, v_w_out, v_ln2_g, v_ln2_b, v_ffn2_w_gate, v_ffn2_w_up, v_ffn2_w_down, v_ln3_g, v_ln3_b):
    w_args = (ffn1_w_gate, ffn1_w_up, ffn1_w_down, ln1_g, ln1_b, w_in, b_forget, conv_w, conv_b, rg_wa, rg_ba, rg_wx, rg_bx, lru_lambda, w_out, ln2_g, ln2_b, ffn2_w_gate, ffn2_w_up, ffn2_w_down, ln3_g, ln3_b)
    m_args = (m_ffn1_w_gate, m_ffn1_w_up, m_ffn1_w_down, m_ln1_g, m_ln1_b, m_w_in, m_b_forget, m_conv_w, m_conv_b, m_rg_wa, m_rg_ba, m_rg_wx, m_rg_bx, m_lru_lambda, m_w_out, m_ln2_g, m_ln2_b, m_ffn2_w_gate, m_ffn2_w_up, m_ffn2_w_down, m_ln3_g, m_ln3_b)
    v_args = (v_ffn1_w_gate, v_ffn1_w_up, v_ffn1_w_down, v_ln1_g, v_ln1_b, v_w_in, v_b_forget, v_conv_w, v_conv_b, v_rg_wa, v_rg_ba, v_rg_wx, v_rg_bx, v_lru_lambda, v_w_out, v_ln2_g, v_ln2_b, v_ffn2_w_gate, v_ffn2_w_up, v_ffn2_w_down, v_ln3_g, v_ln3_b)
    w = dict(zip(WEIGHTS, w_args))
    m = dict(zip(WEIGHTS, m_args))
    v = dict(zip(WEIGHTS, v_args))
    me = 4 * lax.axis_index("x") + 2 * lax.axis_index("y") + lax.axis_index("c")

    sent = {n: _transport(w[n]).astype(bf16) for n in BIG}
    sent["conv_w"] = _two_d(w["conv_w"])
    small = {n: w[n] for n in ("ln1_g", "ln1_b", "ln2_g", "ln2_b", "ln3_g", "ln3_b", "b_forget", "conv_b",
                               "lru_lambda")}
    small.update({n: w[n][0] for n in ("rg_wa", "rg_ba", "rg_wx", "rg_bx")})

    sq_rows, grad_x, parts, all_packed, small_shapes = _local_step(x[0], loss_target[0], sent, small)
    loss = lax.psum(0.5 * jnp.sum(sq_rows) / D_MODEL, ("x", "y", "c"))

    total = _sum_parts(all_packed, name="sum_small_grads").reshape(-1)
    grads, off = {}, 0
    for n in PACKED:
        size = math.prod(small_shapes[n])
        grads[n] = total[off:off + size].reshape(small_shapes[n])
        off += size
    grads["conv_w"] = lax.dynamic_slice_in_dim(grads["conv_w"], me * (LRU_W // N_DEV), LRU_W // N_DEV, axis=1)

    delta, new_m, new_v = {}, {}, {}
    for n in BIG:
        g, d, m2, v2 = _adamw_big(parts[n], _transport(w[n]), _transport(m[n]), _transport(v[n]),
                                  name="adamw_" + n)
        grads[n], delta[n], new_m[n], new_v[n] = g, d, m2, v2
    small_names = [n for n in WEIGHTS if n not in BIG]
    outs = _adamw_small([(_two_d(grads[n]), _two_d(w[n]), _two_d(m[n]), _two_d(v[n])) for n in small_names],
                        name="adamw_small")
    for k, n in enumerate(small_names):
        delta[n], new_m[n], new_v[n] = outs[3 * k], outs[3 * k + 1], outs[3 * k + 2]

    def shaped(d):
        return [d[n].reshape(w[n].shape) for n in WEIGHTS]

    return (loss, grad_x[None], *shaped(grads), *shaped(delta), *shaped(new_m), *shaped(new_v))
```

```python
import functools
import math

import jax
import jax.numpy as jnp
from jax import lax
from jax.experimental import pallas as pl
from jax.experimental.pallas import tpu as pltpu

f32 = jnp.float32
bf16 = jnp.bfloat16

N_DEV = 8
D_MODEL = 1024
D_FF = 4096
FF_TILE = D_FF // N_DEV
FOX_W = 512
LRU_W = 512
HEADS = 8
HEAD_D = 64
IN_COLS = 2568
IN_SHARD = IN_COLS // N_DEV
LANES = 128
LN_EPS = 1e-5
ALPHA = 2.0 ** 0.25
ATT_SCALE = 1.0 / math.sqrt(HEAD_D)
LRU_C = 8.0
NEG_BIG = -1e30

ADAM_LR = 0.001
ADAM_B1 = 0.9
ADAM_B2 = 0.999
ADAM_EPS = 1e-08
ADAM_WD = 0.01
ADAM_STEP = 10

VMEM_LIMIT = 56 * 1024 * 1024
MESH_T = pl.DeviceIdType.MESH


def _params(sem, **kw):
    return pltpu.CompilerParams(dimension_semantics=sem, vmem_limit_bytes=VMEM_LIMIT, **kw)


def _sigmoid(x):
    return 1.0 / (1.0 + jnp.exp(-x))


def _sigmoid_tanh(x):
    return 0.5 * jnp.tanh(0.5 * x) + 0.5


def _softplus(x):
    return jnp.maximum(x, 0.0) + jnp.log(1.0 + jnp.exp(-jnp.abs(x)))


def _one_minus_exp(x):
    series = -x * (1.0 + x * (0.5 + x * (1.0 / 6 + x * (1.0 / 24 + x * (1.0 / 120 + x * (1.0 / 720))))))
    return jnp.where(x > -0.125, series, 1.0 - jnp.exp(x))


_GELU_C = math.sqrt(2.0 / math.pi)


def _gelu_and_grad(x):
    inner = _GELU_C * (x + 0.044715 * x * x * x)
    t = jnp.tanh(inner)
    g = 0.5 * x * (1.0 + t)
    dg = 0.5 * (1.0 + t) + 0.5 * x * (1.0 - t * t) * _GELU_C * (1.0 + 3 * 0.044715 * x * x)
    return g, dg


def _ln_fwd_tile(pre):
    mu = jnp.mean(pre, axis=-1, keepdims=True)
    xc = pre - mu
    var = jnp.mean(xc * xc, axis=-1, keepdims=True)
    rstd = lax.rsqrt(var + LN_EPS)
    return xc * rstd, rstd


def _ln_bwd_tile(dy, xhat, rstd, g):
    dyg = dy * g
    m1 = jnp.mean(dyg, axis=-1, keepdims=True)
    m2 = jnp.mean(dyg * xhat, axis=-1, keepdims=True)
    dpre = rstd * (dyg - m1 - xhat * m2)
    return dpre, jnp.sum(dy * xhat, axis=0, keepdims=True), jnp.sum(dy, axis=0, keepdims=True)


_NT = (((1,), (1,)), ((), ()))
_TN = (((0,), (0,)), ((), ()))


class _Exchange:
    def __init__(self, arrs, gather, chips=False):
        self.arrs, self.gather, self.n, self.chips = list(arrs), gather, len(arrs), chips

    def out_shape(self):
        return [jax.ShapeDtypeStruct(((N_DEV,) + a.shape) if self.gather else a.shape, a.dtype) for a in self.arrs]

    def scratch(self):
        n_remote = self.n * (N_DEV - 1)
        return [pltpu.SemaphoreType.DMA((n_remote,)), pltpu.SemaphoreType.DMA((n_remote,)),
                pltpu.SemaphoreType.DMA((self.n,))]

    def copies(self, ins, outs, sems):
        send_sems, recv_sems, local_sems = sems
        x, y, c = lax.axis_index("x"), lax.axis_index("y"), lax.axis_index("c")
        me = 2 * x + y if self.chips else 4 * x + 2 * y + c
        out = []
        for k in range(self.n):
            for d in (range(2, N_DEV, 2) if self.chips else range(1, N_DEV)):
                px = 1 - x if d & 4 else x
                py = 1 - y if d & 2 else y
                pc = 1 - c if d & 1 else c
                sem = k * (N_DEV - 1) + d - 1
                out.append(pltpu.make_async_remote_copy(
                    src_ref=ins[k].at[2 * px + py if self.chips else 4 * px + 2 * py + pc], dst_ref=outs[k].at[me],
                    send_sem=send_sems.at[sem], recv_sem=recv_sems.at[sem],
                    device_id=(px, py, pc), device_id_type=MESH_T))
            out.append(pltpu.make_async_copy(ins[k].at[me], outs[k].at[me], local_sems.at[k]))
        return out

    def gather_copies(self, ins, outs, sems):
        send_sems, recv_sems, local_sems = sems
        x, y, c = lax.axis_index("x"), lax.axis_index("y"), lax.axis_index("c")
        sibling = (x, y, 1 - c)
        chips = [(1 - x, y), (x, 1 - y), (1 - x, 1 - y)]
        out = []
        for k in range(self.n):
            def copy(s, block, to, src=None, k=k):
                rows = outs[k].at[4 * block[0] + 2 * block[1] + block[2]]
                sem = k * (N_DEV - 1) + s
                return pltpu.make_async_remote_copy(
                    src_ref=rows if src is None else src, dst_ref=rows, send_sem=send_sems.at[sem],
                    recv_sem=recv_sems.at[sem], device_id=to, device_id_type=MESH_T)

            first = [copy(0, (x, y, c), sibling, src=ins[k])]
            first += [copy(1 + q, (x, y, c), (*chip, c), src=ins[k]) for q, chip in enumerate(chips)]
            passed = [copy(4 + q, (*chip, c), sibling) for q, chip in enumerate(chips)]
            own = pltpu.make_async_copy(ins[k], outs[k].at[4 * x + 2 * y + c], local_sems.at[k])
            out.append((first, passed, own, copy))
        return out, sibling, chips, (x, y, c)

    def start(self, ins, outs, sems):
        if not self.gather:
            for cp in self.copies(ins, outs, sems):
                cp.start()
            return
        per_array, _, _, _ = self.gather_copies(ins, outs, sems)
        for first, _, own, _ in per_array:
            own.start()
            for cp in first:
                cp.start()

    def wait(self, ins, outs, sems):
        if not self.gather:
            for cp in self.copies(ins, outs, sems):
                cp.wait()
            return
        per_array, sibling, chips, (x, y, c) = self.gather_copies(ins, outs, sems)
        for first, passed, own, copy in per_array:
            for q, chip in enumerate(chips):
                copy(1 + q, (*chip, c), (x, y, c)).wait_recv()
                passed[q].start()
        for first, passed, own, copy in per_array:
            copy(0, sibling, (x, y, c)).wait_recv()
            for q, chip in enumerate(chips):
                copy(4 + q, (*chip, 1 - c), (x, y, c)).wait_recv()
            for cp in first + passed:
                cp.wait_send()
            own.wait()


def _hosted_call(host, body, *, name, grid, in_specs, out_specs, out_shape, scratch_shapes=(), compiler_params):
    out_specs = list(out_specs) if isinstance(out_specs, (list, tuple)) else [out_specs]
    out_shape = list(out_shape) if isinstance(out_shape, (list, tuple)) else [out_shape]
    if host is None:
        return pl.pallas_call(body, name=name, grid=grid, in_specs=in_specs, out_specs=out_specs,
                              out_shape=out_shape, scratch_shapes=list(scratch_shapes),
                              compiler_params=compiler_params)
    n_in, n_out, n_scr, k = len(in_specs), len(out_shape), len(scratch_shapes), host.n

    def wrapped(*refs):
        ins, h_in = refs[:n_in], refs[n_in:n_in + k]
        outs, h_out = refs[n_in + k:n_in + k + n_out], refs[n_in + k + n_out:n_in + 2 * k + n_out]
        scr, sems = refs[n_in + 2 * k + n_out:n_in + 2 * k + n_out + n_scr], refs[n_in + 2 * k + n_out + n_scr:]
        ids = [pl.program_id(a) for a in range(len(grid))]
        first = functools.reduce(jnp.logical_and, [i == 0 for i in ids])
        last = functools.reduce(jnp.logical_and, [i == g - 1 for i, g in zip(ids, grid)])

        @pl.when(first)
        def _():
            host.start(h_in, h_out, sems)

        body(*ins, *outs, *scr)

        @pl.when(last)
        def _():
            host.wait(h_in, h_out, sems)

    hbm = pl.BlockSpec(memory_space=pl.ANY)
    call = pl.pallas_call(
        wrapped, name=name, grid=grid, in_specs=list(in_specs) + [hbm] * k, out_specs=out_specs + [hbm] * k,
        out_shape=out_shape + host.out_shape(), scratch_shapes=list(scratch_shapes) + host.scratch(),
        compiler_params=compiler_params)
    return lambda *args: call(*args, *host.arrs)


def _exchange(arrs, *, gather, name):
    host = _Exchange(arrs, gather)

    def body(*refs):
        ins, outs, sems = refs[:host.n], refs[host.n:2 * host.n], refs[2 * host.n:]
        host.start(ins, outs, sems)
        host.wait(ins, outs, sems)

    hbm = pl.BlockSpec(memory_space=pl.ANY)
    return pl.pallas_call(
        body, name=name, in_specs=[hbm] * host.n, out_specs=[hbm] * host.n, out_shape=host.out_shape(),
        scratch_shapes=host.scratch(), compiler_params=pltpu.CompilerParams(has_side_effects=True),
    )(*arrs)


def _ffn_fwd(xhat, g_in, b_in, wg, wu, wd, *, tm, name, host=None):
    t = xhat.shape[0]
    nj = N_DEV

    def body(x_ref, g_ref, b_ref, wg_ref, wu_ref, wd_ref, xo_ref, rstd_ref, hg_ref, hu_ref, xb, acc):
        j = pl.program_id(1)

        @pl.when(j == 0)
        def _():
            xb[...] = (x_ref[...] * g_ref[...] + b_ref[...]).astype(bf16)
            acc[...] = jnp.zeros_like(acc)

        hg = jnp.dot(xb[...], wg_ref[...], preferred_element_type=f32)
        hu = jnp.dot(xb[...], wu_ref[...], preferred_element_type=f32)
        hg_ref[...] = hg.astype(bf16)
        hu_ref[...] = hu.astype(bf16)
        a = hg * _sigmoid_tanh(hg) * hu
        acc[...] += jnp.dot(a.astype(bf16), wd_ref[...], preferred_element_type=f32)

        @pl.when(j == nj - 1)
        def _():
            x = x_ref[...] * g_ref[...] + b_ref[...]
            xo, rstd = _ln_fwd_tile(ALPHA * x + 0.5 * acc[...])
            xo_ref[...] = xo
            rstd_ref[...] = rstd

    row = pl.BlockSpec((1, D_MODEL), lambda i, j: (0, 0))
    return _hosted_call(
        host, body, name=name, grid=(t // tm, nj),
        in_specs=[pl.BlockSpec((tm, D_MODEL), lambda i, j: (i, 0)), row, row,
                  pl.BlockSpec((None, D_MODEL, FF_TILE), lambda i, j: (j, 0, 0)),
                  pl.BlockSpec((None, D_MODEL, FF_TILE), lambda i, j: (j, 0, 0)),
                  pl.BlockSpec((None, FF_TILE, D_MODEL), lambda i, j: (j, 0, 0))],
        out_specs=[pl.BlockSpec((tm, D_MODEL), lambda i, j: (i, 0)),
                   pl.BlockSpec((tm, 1), lambda i, j: (i, 0)),
                   pl.BlockSpec((tm, FF_TILE), lambda i, j: (i, j)),
                   pl.BlockSpec((tm, FF_TILE), lambda i, j: (i, j))],
        out_shape=[jax.ShapeDtypeStruct((t, D_MODEL), f32), jax.ShapeDtypeStruct((t, 1), f32),
                   jax.ShapeDtypeStruct((t, D_FF), bf16), jax.ShapeDtypeStruct((t, D_FF), bf16)],
        scratch_shapes=[pltpu.VMEM((tm, D_MODEL), bf16), pltpu.VMEM((tm, D_MODEL), f32)],
        compiler_params=_params(("arbitrary", "arbitrary")),
    )(xhat, g_in, b_in, wg, wu, wd)


def _ffn_bwd(dpre, hg, hu, wg, wu, wd, ln_in, *, tm, name, host=None):
    t = dpre.shape[0]
    nj = N_DEV
    with_ln = ln_in is not None

    def body(*refs):
        if with_ln:
            (dp_ref, hg_ref, hu_ref, wg_ref, wu_ref, wd_ref, xh_ref, rs_ref, g_ref,
             dx_ref, gg_ref, gb_ref, dhg_ref, dhu_ref, a_ref, dfb, acc) = refs
        else:
            (dp_ref, hg_ref, hu_ref, wg_ref, wu_ref, wd_ref,
             dx_ref, dhg_ref, dhu_ref, a_ref, dfb, acc) = refs
        i = pl.program_id(0)
        j = pl.program_id(1)

        @pl.when(j == 0)
        def _():
            dfb[...] = (0.5 * dp_ref[...]).astype(bf16)
            acc[...] = jnp.zeros_like(acc)

        da = lax.dot_general(dfb[...], wd_ref[...], _NT, preferred_element_type=f32)
        hgv = hg_ref[...].astype(f32)
        huv = hu_ref[...].astype(f32)
        sg = _sigmoid_tanh(hgv)
        silu = hgv * sg
        a_ref[...] = (silu * huv).astype(bf16)
        dhu = (da * silu).astype(bf16)
        dhg = (da * huv * (sg * (1.0 + hgv * (1.0 - sg)))).astype(bf16)
        dhg_ref[...] = dhg
        dhu_ref[...] = dhu
        acc[...] += (lax.dot_general(dhg, wg_ref[...], _NT, preferred_element_type=f32)
                     + lax.dot_general(dhu, wu_ref[...], _NT, preferred_element_type=f32))

        @pl.when(j == nj - 1)
        def _():
            dx = ALPHA * dp_ref[...] + acc[...]
            if with_ln:
                dprev, gg, gb = _ln_bwd_tile(dx, xh_ref[...], rs_ref[...], g_ref[...])
                dx_ref[...] = dprev

                @pl.when(i == 0)
                def _():
                    gg_ref[...] = gg
                    gb_ref[...] = gb

                @pl.when(i > 0)
                def _():
                    gg_ref[...] += gg
                    gb_ref[...] += gb
            else:
                dx_ref[...] = dx

    tok = pl.BlockSpec((tm, D_MODEL), lambda i, j: (i, 0))
    row = pl.BlockSpec((1, D_MODEL), lambda i, j: (0, 0))
    hid = pl.BlockSpec((tm, FF_TILE), lambda i, j: (i, j))
    in_specs = [tok, hid, hid,
                pl.BlockSpec((None, D_MODEL, FF_TILE), lambda i, j: (j, 0, 0)),
                pl.BlockSpec((None, D_MODEL, FF_TILE), lambda i, j: (j, 0, 0)),
                pl.BlockSpec((None, FF_TILE, D_MODEL), lambda i, j: (j, 0, 0))]
    args = [dpre, hg, hu, wg, wu, wd]
    out_specs = [tok]
    out_shape = [jax.ShapeDtypeStruct((t, D_MODEL), f32)]
    if with_ln:
        in_specs += [tok, pl.BlockSpec((tm, 1), lambda i, j: (i, 0)), row]
        args += list(ln_in)
        out_specs += [row, row]
        out_shape += [jax.ShapeDtypeStruct((1, D_MODEL), f32)] * 2
    out_specs += [hid, hid, hid]
    out_shape += [jax.ShapeDtypeStruct((t, D_FF), bf16)] * 3
    return _hosted_call(
        host, body, name=name, grid=(t // tm, nj), in_specs=in_specs, out_specs=out_specs, out_shape=out_shape,
        scratch_shapes=[pltpu.VMEM((tm, D_MODEL), bf16), pltpu.VMEM((tm, D_MODEL), f32)],
        compiler_params=_params(("arbitrary", "arbitrary")),
    )(*args)


def _ffn_bwd_act(dpre, hg, hu, wd, *, tm, name, host=None):
    t = dpre.shape[0]

    def body(dp_ref, hg_ref, hu_ref, wd_ref, dhg_ref, dhu_ref, a_ref, dfb):
        @pl.when(pl.program_id(1) == 0)
        def _():
            dfb[...] = (0.5 * dp_ref[...]).astype(bf16)

        da = lax.dot_general(dfb[...], wd_ref[...], _NT, preferred_element_type=f32)
        hgv = hg_ref[...].astype(f32)
        huv = hu_ref[...].astype(f32)
        sg = _sigmoid_tanh(hgv)
        silu = hgv * sg
        a_ref[...] = (silu * huv).astype(bf16)
        dhu_ref[...] = (da * silu).astype(bf16)
        dhg_ref[...] = (da * huv * (sg * (1.0 + hgv * (1.0 - sg)))).astype(bf16)

    hid = pl.BlockSpec((tm, FF_TILE), lambda i, j: (i, j))
    return _hosted_call(
        host, body, name=name, grid=(t // tm, N_DEV),
        in_specs=[pl.BlockSpec((tm, D_MODEL), lambda i, j: (i, 0)), hid, hid,
                  pl.BlockSpec((None, FF_TILE, D_MODEL), lambda i, j: (j, 0, 0))],
        out_specs=[hid, hid, hid], out_shape=[jax.ShapeDtypeStruct((t, D_FF), bf16)] * 3,
        scratch_shapes=[pltpu.VMEM((tm, D_MODEL), bf16)],
        compiler_params=_params(("arbitrary", "arbitrary")),
    )(dpre, hg, hu, wd)


def _ffn_bwd_dx(dpre, dhg, dhu, wg, wu, *, tm, name, host=None):
    t = dpre.shape[0]
    nj = N_DEV

    def body(dp_ref, dhg_ref, dhu_ref, wg_ref, wu_ref, dx_ref, acc):
        j = pl.program_id(1)

        @pl.when(j == 0)
        def _():
            acc[...] = jnp.zeros_like(acc)

        acc[...] += (lax.dot_general(dhg_ref[...], wg_ref[...], _NT, preferred_element_type=f32)
                     + lax.dot_general(dhu_ref[...], wu_ref[...], _NT, preferred_element_type=f32))

        @pl.when(j == nj - 1)
        def _():
            dx_ref[...] = ALPHA * dp_ref[...] + acc[...]

    tok = pl.BlockSpec((tm, D_MODEL), lambda i, j: (i, 0))
    hid = pl.BlockSpec((tm, FF_TILE), lambda i, j: (i, j))
    wspec = pl.BlockSpec((None, D_MODEL, FF_TILE), lambda i, j: (j, 0, 0))
    return _hosted_call(
        host, body, name=name, grid=(t // tm, nj), in_specs=[tok, hid, hid, wspec, wspec],
        out_specs=[tok], out_shape=[jax.ShapeDtypeStruct((t, D_MODEL), f32)],
        scratch_shapes=[pltpu.VMEM((tm, D_MODEL), f32)],
        compiler_params=_params(("arbitrary", "arbitrary")),
    )(dpre, dhg, dhu, wg, wu)


def _mm(a, b, *, mode, out_dtype, tm, tn, tk, name, affine=None, a_cols=None, b_cols=None,
        b_blocked=False, out_blocked=False, out_scale=None):
    if mode == "nn":
        m_full, k_full = a.shape
        m_dim, k_dim = (m_full, a_cols[1]) if a_cols else (m_full, k_full)
    else:
        k_dim, m_full = a.shape
        m_dim = a_cols[1] if a_cols else m_full
    a_off = a_cols[0] if a_cols else 0
    if b_blocked:
        n_dim = b.shape[0] * b.shape[2]
        assert b.shape[2] == tn
    else:
        n_dim = b_cols[1] if b_cols else b.shape[1]
    b_off = b_cols[0] if b_cols else 0
    assert m_dim % tm == 0 and n_dim % tn == 0 and k_dim % tk == 0, (name, m_dim, n_dim, k_dim)
    nk = k_dim // tk

    def body(*refs):
        if affine is not None:
            a_ref, g_ref, s_ref, b_ref, o_ref, acc = refs
        else:
            a_ref, b_ref, o_ref, acc = refs
        k = pl.program_id(2)

        @pl.when(k == 0)
        def _():
            acc[...] = jnp.zeros_like(acc)

        av = a_ref[...]
        if affine is not None:
            av = av * g_ref[...] + s_ref[...]
        av = av.astype(bf16)
        bv = b_ref[...].astype(bf16)
        if mode == "nn":
            acc[...] += jnp.dot(av, bv, preferred_element_type=f32)
        else:
            acc[...] += lax.dot_general(av, bv, _TN, preferred_element_type=f32)

        @pl.when(k == nk - 1)
        def _():
            res = acc[...] if out_scale is None else acc[...] * out_scale
            o_ref[...] = res.astype(out_dtype)

    if mode == "nn":
        a_spec = pl.BlockSpec((tm, tk), lambda i, j, k: (i, k + a_off))
        aff_spec = pl.BlockSpec((1, tk), lambda i, j, k: (0, k + a_off))
    else:
        a_spec = pl.BlockSpec((tk, tm), lambda i, j, k: (k, i + a_off))
        aff_spec = pl.BlockSpec((1, tm), lambda i, j, k: (0, i + a_off))
    if b_blocked:
        b_spec = pl.BlockSpec((None, tk, tn), lambda i, j, k: (j, k, 0))
    else:
        b_spec = pl.BlockSpec((tk, tn), lambda i, j, k: (k, j + b_off))
    if out_blocked:
        o_spec = pl.BlockSpec((None, tm, tn), lambda i, j, k: (j, i, 0))
        o_shape = jax.ShapeDtypeStruct((n_dim // tn, m_dim, tn), out_dtype)
    else:
        o_spec = pl.BlockSpec((tm, tn), lambda i, j, k: (i, j))
        o_shape = jax.ShapeDtypeStruct((m_dim, n_dim), out_dtype)
    in_specs = [a_spec] + ([aff_spec, aff_spec] if affine is not None else []) + [b_spec]
    args = [a] + (list(affine) if affine is not None else []) + [b]
    return pl.pallas_call(
        body, name=name, grid=(m_dim // tm, n_dim // tn, nk), in_specs=in_specs, out_specs=o_spec,
        out_shape=o_shape, scratch_shapes=[pltpu.VMEM((tm, tn), f32)],
        compiler_params=_params(("arbitrary", "arbitrary", "arbitrary")),
    )(*args)


def _mm_tn(a, b, *, out_dtype, tm, mb, tn, nb, tk, name, affine=None, out_blocked=False, out_scale=None,
           pair=False, host=None):
    k_dim, m_dim = a.shape
    n_dim = b.shape[1]
    assert m_dim % (mb * tm) == 0 and n_dim % (nb * tn) == 0 and k_dim % tk == 0, (name, m_dim, n_dim, k_dim)
    nk = k_dim // tk
    grid = (m_dim // (mb * tm), n_dim // (nb * tn), nk)
    if pair:
        assert mb * nb == 4 and grid[0] * grid[1] == 2 and out_dtype == bf16, name

    def body(*refs):
        if pair:
            refs, (acc, send_buf, recv_buf, send_sems, recv_sems) = refs[:-5], refs[-5:]
        else:
            refs, acc = refs[:-1], refs[-1]
        if affine is not None:
            a_ref, g_ref, s_ref, b_ref, o_ref = refs
        else:
            a_ref, b_ref, o_ref = refs
        k = pl.program_id(2)

        @pl.when(k == 0)
        def _():
            acc[...] = jnp.zeros_like(acc)

        av = a_ref[...]
        if affine is not None:
            av = av * g_ref[...] + s_ref[...]
        av = av.astype(bf16)
        bv = b_ref[...].astype(bf16)
        for im in range(mb):
            a_t = av[:, im * tm:(im + 1) * tm].T
            for jn in range(nb):
                acc[im * nb + jn] += jnp.dot(a_t, bv[:, jn * tn:(jn + 1) * tn], preferred_element_type=f32)

        def scaled(v):
            return v if out_scale is None else v * out_scale

        @pl.when(k == nk - 1)
        def _():
            if pair:
                x, y, c = lax.axis_index("x"), lax.axis_index("y"), lax.axis_index("c")
                window = pl.program_id(0) + pl.program_id(1)
                swaps = []
                for cc in range(2):
                    send_buf[cc] = scaled(acc[2 * cc + 1 - c]).astype(bf16)
                    swaps.append(pltpu.make_async_remote_copy(
                        src_ref=send_buf.at[cc], dst_ref=recv_buf.at[window, cc],
                        send_sem=send_sems.at[2 * window + cc], recv_sem=recv_sems.at[2 * window + cc],
                        device_id=(x, y, 1 - c), device_id_type=MESH_T))
                    swaps[cc].start()
                for cc in range(2):
                    swaps[cc].wait_recv()
                    o_ref[cc] = (scaled(acc[2 * cc + c]) + recv_buf[window, cc].astype(f32)).astype(bf16)
                for cc in range(2):
                    swaps[cc].wait_send()
                return
            for im in range(mb):
                for jn in range(nb):
                    res = scaled(acc[im * nb + jn])
                    if out_blocked:
                        o_ref[jn, im * tm:(im + 1) * tm, :] = res.astype(out_dtype)
                    else:
                        o_ref[im * tm:(im + 1) * tm, jn * tn:(jn + 1) * tn] = res.astype(out_dtype)

    a_spec = pl.BlockSpec((tk, mb * tm), lambda i, j, k: (k, i))
    aff_spec = pl.BlockSpec((1, mb * tm), lambda i, j, k: (0, i))
    b_spec = pl.BlockSpec((tk, nb * tn), lambda i, j, k: (k, j))
    scratch = [pltpu.VMEM((mb * nb, tm, tn), f32)]
    if pair:
        o_spec = pl.BlockSpec((2, tm, tn), lambda i, j, k: (i + j, 0, 0))
        o_shape = jax.ShapeDtypeStruct((4, tm, tn), out_dtype)
        scratch += [pltpu.VMEM((2, tm, tn), bf16), pltpu.VMEM((2, 2, tm, tn), bf16),
                    pltpu.SemaphoreType.DMA((4,)), pltpu.SemaphoreType.DMA((4,))]
    elif out_blocked:
        o_spec = pl.BlockSpec((nb, mb * tm, tn), lambda i, j, k: (j, i, 0))
        o_shape = jax.ShapeDtypeStruct((n_dim // tn, m_dim, tn), out_dtype)
    else:
        o_spec = pl.BlockSpec((mb * tm, nb * tn), lambda i, j, k: (i, j))
        o_shape = jax.ShapeDtypeStruct((m_dim, n_dim), out_dtype)
    in_specs = [a_spec] + ([aff_spec, aff_spec] if affine is not None else []) + [b_spec]
    args = [a] + (list(affine) if affine is not None else []) + [b]
    res = _hosted_call(
        host, body, name=name, grid=grid, in_specs=in_specs, out_specs=o_spec, out_shape=o_shape,
        scratch_shapes=scratch, compiler_params=_params(("arbitrary", "arbitrary", "arbitrary")),
    )(*args)
    return res[0] if host is None else res


def _mmln(pairs, *, tm, name, resid=None, resid_scale=1.0, epi=None, ln=None, n_out=D_MODEL):
    t = pairs[0][0].shape[0]
    n_pairs = len(pairs)
    n_resid = 0 if resid is None else len(resid) - 1

    def body(*refs):
        pos = 0
        val = None
        for p in range(n_pairs):
            a_ref, b_ref = refs[pos], refs[pos + 1]
            pos += 2
            av = a_ref[...].astype(bf16)
            bv = b_ref[...].astype(bf16)
            if pairs[p][6] == "nn":
                term = jnp.dot(av, bv, preferred_element_type=f32)
            else:
                term = lax.dot_general(av, bv, _NT, preferred_element_type=f32)
            val = term if val is None else val + term
        if resid is not None:
            if resid[0] == "plain":
                r = refs[pos][...]
            else:
                r = refs[pos][...] * refs[pos + 1][...] + refs[pos + 2][...]
            pos += n_resid
            val = val + resid_scale * r
        if epi is None:
            o_ref = refs[pos]
            o_ref[...] = val.astype(o_ref.dtype)
        elif epi == "ln_fwd":
            xo, rstd = _ln_fwd_tile(val)
            refs[pos][...] = xo
            refs[pos + 1][...] = rstd
        else:
            xh_ref, rs_ref, g_ref, dx_ref, gg_ref, gb_ref = refs[pos:pos + 6]
            dprev, gg, gb = _ln_bwd_tile(val, xh_ref[...], rs_ref[...], g_ref[...])
            dx_ref[...] = dprev
            i = pl.program_id(0)

            @pl.when(i == 0)
            def _():
                gg_ref[...] = gg
                gb_ref[...] = gb

            @pl.when(i > 0)
            def _():
                gg_ref[...] += gg
                gb_ref[...] += gb

    in_specs, args = [], []
    for (a, acb, aw, b, bcb, bw, mode) in pairs:
        in_specs.append(pl.BlockSpec((tm, aw), lambda i, acb=acb: (i, acb)))
        args.append(a)
        if mode == "nn":
            in_specs.append(pl.BlockSpec((aw, n_out), lambda i, bcb=bcb: (bcb, 0)))
        else:
            in_specs.append(pl.BlockSpec((n_out, bw), lambda i, bcb=bcb: (0, bcb)))
        args.append(b)
    tok = pl.BlockSpec((tm, n_out), lambda i: (i, 0))
    row = pl.BlockSpec((1, n_out), lambda i: (0, 0))
    col = pl.BlockSpec((tm, 1), lambda i: (i, 0))
    if resid is not None:
        in_specs += [tok] if resid[0] == "plain" else [tok, row, row]
        args += list(resid[1:])
    if epi is None:
        out_specs, out_shape = tok, jax.ShapeDtypeStruct((t, n_out), f32)
    elif epi == "ln_fwd":
        out_specs = [tok, col]
        out_shape = [jax.ShapeDtypeStruct((t, n_out), f32), jax.ShapeDtypeStruct((t, 1), f32)]
    else:
        in_specs += [tok, col, row]
        args += list(ln)
        out_specs = [tok, row, row]
        out_shape = [jax.ShapeDtypeStruct((t, n_out), f32)] + [jax.ShapeDtypeStruct((1, n_out), f32)] * 2
    return pl.pallas_call(
        body, name=name, grid=(t // tm,), in_specs=in_specs, out_specs=out_specs, out_shape=out_shape,
        compiler_params=_params(("arbitrary",)),
    )(*args)


def _loss_bwd(xhat, rstd, g, b, target, *, tm, name):
    t = xhat.shape[0]

    def body(xh_ref, rs_ref, g_ref, b_ref, tg_ref, dx_ref, sq_ref, gg_ref, gb_ref):
        i = pl.program_id(0)
        xh = xh_ref[...]
        diff = xh * g_ref[...] + b_ref[...] - tg_ref[...]
        sq = jnp.sum(diff * diff, axis=0, keepdims=True)
        dprev, gg, gb = _ln_bwd_tile(diff * (1.0 / D_MODEL), xh, rs_ref[...], g_ref[...])
        dx_ref[...] = dprev

        @pl.when(i == 0)
        def _():
            sq_ref[...] = sq
            gg_ref[...] = gg
            gb_ref[...] = gb

        @pl.when(i > 0)
        def _():
            sq_ref[...] += sq
            gg_ref[...] += gg
            gb_ref[...] += gb

    tok = pl.BlockSpec((tm, D_MODEL), lambda i: (i, 0))
    row = pl.BlockSpec((1, D_MODEL), lambda i: (0, 0))
    return pl.pallas_call(
        body, name=name, grid=(t // tm,),
        in_specs=[tok, pl.BlockSpec((tm, 1), lambda i: (i, 0)), row, row, tok],
        out_specs=[tok, row, row, row],
        out_shape=[jax.ShapeDtypeStruct((t, D_MODEL), f32)] + [jax.ShapeDtypeStruct((1, D_MODEL), f32)] * 3,
        compiler_params=_params(("arbitrary",)),
    )(xhat, rstd, g, b, target)


CUM_TILE = 256


def _tri(n, lower):
    r = lax.broadcasted_iota(jnp.int32, (n, n), 0)
    c = lax.broadcasted_iota(jnp.int32, (n, n), 1)
    return jnp.where((r >= c) if lower else (r <= c), 1.0, 0.0).astype(f32)


def _cum_fwd(zfg, bfg, *, name):
    t = zfg.shape[0]

    def body(z_ref, b_ref, o_ref, carry):
        @pl.when(pl.program_id(0) == 0)
        def _():
            carry[...] = jnp.zeros_like(carry)

        ls = -_softplus(-(z_ref[...] + b_ref[...]))
        c = jnp.dot(_tri(CUM_TILE, True), ls, preferred_element_type=f32,
                    precision=lax.Precision.HIGHEST) + carry[...]
        o_ref[...] = c
        carry[...] = c[CUM_TILE - 1:CUM_TILE, :]

    blk = pl.BlockSpec((CUM_TILE, LANES), lambda i: (i, 0))
    return pl.pallas_call(
        body, name=name, grid=(t // CUM_TILE,),
        in_specs=[blk, pl.BlockSpec((1, LANES), lambda i: (0, 0))], out_specs=blk,
        out_shape=jax.ShapeDtypeStruct((t, LANES), f32), scratch_shapes=[pltpu.VMEM((1, LANES), f32)],
        compiler_params=_params(("arbitrary",)),
    )(zfg, bfg)


def _cum_bwd(dcum_q, dcum_k, zfg, bfg, *, name):
    t = zfg.shape[0]
    n = t // CUM_TILE

    def body(d_ref, d2_ref, z_ref, b_ref, o_ref, s_ref, carry):
        i = pl.program_id(0)

        @pl.when(i == 0)
        def _():
            carry[...] = jnp.zeros_like(carry)

        dls = jnp.dot(_tri(CUM_TILE, False), d_ref[...] + d2_ref[...], preferred_element_type=f32,
                      precision=lax.Precision.HIGHEST) + carry[...]
        carry[...] = dls[0:1, :]
        lane = lax.broadcasted_iota(jnp.int32, (CUM_TILE, LANES), 1)
        dfg = jnp.where(lane < HEADS, dls * _sigmoid(-(z_ref[...] + b_ref[...])), 0.0)
        o_ref[...] = dfg
        tot = jnp.sum(dfg, axis=0, keepdims=True)

        @pl.when(i == 0)
        def _():
            s_ref[...] = tot

        @pl.when(i > 0)
        def _():
            s_ref[...] += tot

    blk = pl.BlockSpec((CUM_TILE, LANES), lambda i: (n - 1 - i, 0))
    row = pl.BlockSpec((1, LANES), lambda i: (0, 0))
    return pl.pallas_call(
        body, name=name, grid=(n,), in_specs=[blk, blk, blk, row], out_specs=[blk, row],
        out_shape=[jax.ShapeDtypeStruct((t, LANES), f32), jax.ShapeDtypeStruct((1, LANES), f32)],
        scratch_shapes=[pltpu.VMEM((1, LANES), f32)],
        compiler_params=_params(("arbitrary",)),
    )(dcum_q, dcum_k, zfg, bfg)


ATT_TILE = 512


ATT_ROWS = 32


def _causal_rows(r, transposed):
    rr = lax.broadcasted_iota(jnp.int32, (ATT_ROWS, ATT_TILE), 0) + r * ATT_ROWS
    cc = lax.broadcasted_iota(jnp.int32, (ATT_ROWS, ATT_TILE), 1)
    return (cc >= rr) if transposed else (rr >= cc)


def _causal(i, j, transposed):
    r = lax.broadcasted_iota(jnp.int32, (ATT_TILE, ATT_TILE), 0)
    c = lax.broadcasted_iota(jnp.int32, (ATT_TILE, ATT_TILE), 1)
    if transposed:
        return (c + i * ATT_TILE) >= (r + j * ATT_TILE)
    return (r + i * ATT_TILE) >= (c + j * ATT_TILE)


def _attn_fwd(qkv, cum, cum_t, *, name, host=None):
    t = qkv.shape[0]
    n = t // ATT_TILE
    tq = ATT_TILE

    def body(q_ref, k_ref, v_ref, cq_ref, ck_ref, o_ref, lse_ref, acc, m_s, l_s, c_s, s_s, p_s):
        i = pl.program_id(0)
        j = pl.program_id(1)

        @pl.when(j == 0)
        def _():
            acc[...] = jnp.zeros_like(acc)
            m_s[...] = jnp.full_like(m_s, NEG_BIG)
            l_s[...] = jnp.zeros_like(l_s)

        def block(masked):
            for h in range(HEADS):
                hs = slice(HEAD_D * h, HEAD_D * (h + 1))
                s_s[...] = lax.dot_general(q_ref[:, hs] * ATT_SCALE, k_ref[:, hs], _NT, preferred_element_type=f32)
                ck = ck_ref[h:h + 1, :]

                def rows_chunk(r, carry):
                    rows = pl.ds(pl.multiple_of(r * ATT_ROWS, ATT_ROWS), ATT_ROWS)
                    s = s_s[rows, :] + (cq_ref[rows, h:h + 1] - ck)
                    if masked:
                        s = jnp.where(_causal_rows(r, False), s, NEG_BIG)
                    m_old = m_s[rows, h:h + 1]
                    m_new = jnp.maximum(m_old, jnp.max(s, axis=-1, keepdims=True))
                    corr = jnp.exp(m_old - m_new)
                    p = jnp.exp(s - m_new)
                    l_s[rows, h:h + 1] = corr * l_s[rows, h:h + 1] + jnp.sum(p, axis=-1, keepdims=True)
                    m_s[rows, h:h + 1] = m_new
                    c_s[rows, h:h + 1] = corr
                    p_s[rows, :] = p.astype(bf16)
                    return carry

                lax.fori_loop(0, tq // ATT_ROWS, rows_chunk, 0, unroll=4)
                acc[:, hs] = c_s[:, h:h + 1] * acc[:, hs] + jnp.dot(p_s[...], v_ref[:, hs],
                                                                   preferred_element_type=f32)

        @pl.when(j < i)
        def _():
            block(False)

        @pl.when(j == i)
        def _():
            block(True)
            lse_ref[...] = jnp.zeros_like(lse_ref)
            for h in range(HEADS):
                hs = slice(HEAD_D * h, HEAD_D * (h + 1))
                l = l_s[:, h:h + 1]
                o_ref[:, hs] = acc[:, hs] / l
                lse_ref[:, h:h + 1] = m_s[:, h:h + 1] + jnp.log(l)

    return _hosted_call(
        host, body, name=name, grid=(n, n),
        in_specs=[pl.BlockSpec((tq, FOX_W), lambda i, j: (i, 0)),
                  pl.BlockSpec((tq, FOX_W), lambda i, j: (jnp.minimum(i, j), 1)),
                  pl.BlockSpec((tq, FOX_W), lambda i, j: (jnp.minimum(i, j), 2)),
                  pl.BlockSpec((tq, LANES), lambda i, j: (i, 0)),
                  pl.BlockSpec((HEADS, tq), lambda i, j: (0, jnp.minimum(i, j)))],
        out_specs=[pl.BlockSpec((tq, FOX_W), lambda i, j: (i, 0)), pl.BlockSpec((tq, LANES), lambda i, j: (i, 0))],
        out_shape=[jax.ShapeDtypeStruct((t, FOX_W), f32), jax.ShapeDtypeStruct((t, LANES), f32)],
        scratch_shapes=[pltpu.VMEM((tq, FOX_W), f32), pltpu.VMEM((tq, LANES), f32), pltpu.VMEM((tq, LANES), f32),
                        pltpu.VMEM((tq, LANES), f32), pltpu.VMEM((tq, tq), f32), pltpu.VMEM((tq, tq), bf16)],
        compiler_params=_params(("arbitrary", "arbitrary")),
    )(qkv, qkv, qkv, cum, cum_t)


def _attn_delta(dmix, o, *, tm, name):
    t = o.shape[0]

    def body(do_ref, o_ref, d_ref):
        r = lax.broadcasted_iota(jnp.int32, (FOX_W, LANES), 0)
        c = lax.broadcasted_iota(jnp.int32, (FOX_W, LANES), 1)
        pick = jnp.where(r // HEAD_D == c, 1.0, 0.0).astype(f32)
        d_ref[...] = jnp.dot(do_ref[...] * o_ref[...], pick, preferred_element_type=f32,
                             precision=lax.Precision.HIGHEST)

    blk = pl.BlockSpec((tm, FOX_W), lambda i: (i, 0))
    return pl.pallas_call(
        body, name=name, grid=(t // tm,), in_specs=[blk, blk],
        out_specs=pl.BlockSpec((tm, LANES), lambda i: (i, 0)),
        out_shape=jax.ShapeDtypeStruct((t, LANES), f32), compiler_params=_params(("arbitrary",)),
    )(dmix, o)


def _attn_dq(qkv, dmix, cum, cum_t, lse, delta, *, name, host=None):
    t = qkv.shape[0]
    n = t // ATT_TILE
    tq = ATT_TILE

    def body(q_ref, k_ref, v_ref, do_ref, cq_ref, ck_ref, lse_ref, dl_ref, dq_ref, dc_ref, acc, dc_acc):
        i = pl.program_id(0)
        j = pl.program_id(1)

        @pl.when(j == 0)
        def _():
            acc[...] = jnp.zeros_like(acc)
            dc_acc[...] = jnp.zeros_like(dc_acc)

        def block(masked):
            mask = _causal(i, j, False) if masked else None
            for h in range(HEADS):
                hs = slice(HEAD_D * h, HEAD_D * (h + 1))
                kh = k_ref[:, hs]
                s = lax.dot_general(q_ref[:, hs] * ATT_SCALE, kh, _NT, preferred_element_type=f32)
                s = s + cq_ref[:, h:h + 1] - ck_ref[h:h + 1, :]
                if masked:
                    s = jnp.where(mask, s, NEG_BIG)
                p = jnp.exp(s - lse_ref[:, h:h + 1])
                dp = lax.dot_general(do_ref[:, hs].astype(bf16), v_ref[:, hs], _NT, preferred_element_type=f32)
                ds = p * (dp - dl_ref[:, h:h + 1])
                acc[:, hs] += jnp.dot(ds.astype(bf16), kh, preferred_element_type=f32)
                dc_acc[:, h:h + 1] += jnp.sum(ds, axis=-1, keepdims=True)

        @pl.when(j < i)
        def _():
            block(False)

        @pl.when(j == i)
        def _():
            block(True)
            dq_ref[...] = (acc[...] * ATT_SCALE).astype(bf16)
            dc_ref[...] = dc_acc[...]

    col = pl.BlockSpec((tq, LANES), lambda i, j: (i, 0))
    return _hosted_call(
        host, body, name=name, grid=(n, n),
        in_specs=[pl.BlockSpec((tq, FOX_W), lambda i, j: (i, 0)),
                  pl.BlockSpec((tq, FOX_W), lambda i, j: (jnp.minimum(i, j), 1)),
                  pl.BlockSpec((tq, FOX_W), lambda i, j: (jnp.minimum(i, j), 2)),
                  pl.BlockSpec((tq, FOX_W), lambda i, j: (i, 0)),
                  col, pl.BlockSpec((HEADS, tq), lambda i, j: (0, jnp.minimum(i, j))), col, col],
        out_specs=[pl.BlockSpec((tq, FOX_W), lambda i, j: (i, 0)), col],
        out_shape=[jax.ShapeDtypeStruct((t, FOX_W), bf16), jax.ShapeDtypeStruct((t, LANES), f32)],
        scratch_shapes=[pltpu.VMEM((tq, FOX_W), f32), pltpu.VMEM((tq, LANES), f32)],
        compiler_params=_params(("arbitrary", "arbitrary")),
    )(qkv, qkv, qkv, dmix, cum, cum_t, lse, delta)


def _attn_dkv(qkv, dmix, cum, cum_t, lse_t, delta_t, *, name):
    t = qkv.shape[0]
    n = t // ATT_TILE
    tk = ATT_TILE

    def body(q_ref, k_ref, v_ref, do_ref, cq_ref, ck_ref, lse_ref, dl_ref, dk_ref, dv_ref, dc_ref, dk_acc, dv_acc, dc_acc):
        j = pl.program_id(0)
        i = pl.program_id(1)

        @pl.when(i == 0)
        def _():
            dk_acc[...] = jnp.zeros_like(dk_acc)
            dv_acc[...] = jnp.zeros_like(dv_acc)
            dc_acc[...] = jnp.zeros_like(dc_acc)

        def block(masked):
            mask = _causal(i, j, True) if masked else None
            for h in range(HEADS):
                hs = slice(HEAD_D * h, HEAD_D * (h + 1))
                qh = q_ref[:, hs]
                doh = do_ref[:, hs].astype(bf16)
                s_t = lax.dot_general(k_ref[:, hs] * ATT_SCALE, qh, _NT, preferred_element_type=f32)
                s_t = s_t + cq_ref[h:h + 1, :] - ck_ref[:, h:h + 1]
                if masked:
                    s_t = jnp.where(mask, s_t, NEG_BIG)
                p_t = jnp.exp(s_t - lse_ref[h:h + 1, :])
                dv_acc[:, hs] += jnp.dot(p_t.astype(bf16), doh, preferred_element_type=f32)
                dp_t = lax.dot_general(v_ref[:, hs], doh, _NT, preferred_element_type=f32)
                ds_t = p_t * (dp_t - dl_ref[h:h + 1, :])
                dk_acc[:, hs] += jnp.dot(ds_t.astype(bf16), qh, preferred_element_type=f32)
                dc_acc[:, h:h + 1] -= jnp.sum(ds_t, axis=-1, keepdims=True)

        @pl.when(i > j)
        def _():
            block(False)

        @pl.when(i == j)
        def _():
            block(True)

        @pl.when(i == n - 1)
        def _():
            dk_ref[...] = (dk_acc[...] * ATT_SCALE).astype(bf16)
            dv_ref[...] = dv_acc[...].astype(bf16)
            dc_ref[...] = dc_acc[...]

    rowq = pl.BlockSpec((HEADS, tk), lambda j, i: (0, jnp.maximum(i, j)))
    return pl.pallas_call(
        body, name=name, grid=(n, n),
        in_specs=[pl.BlockSpec((tk, FOX_W), lambda j, i: (jnp.maximum(i, j), 0)),
                  pl.BlockSpec((tk, FOX_W), lambda j, i: (j, 1)),
                  pl.BlockSpec((tk, FOX_W), lambda j, i: (j, 2)),
                  pl.BlockSpec((tk, FOX_W), lambda j, i: (jnp.maximum(i, j), 0)),
                  rowq, pl.BlockSpec((tk, LANES), lambda j, i: (j, 0)), rowq, rowq],
        out_specs=[pl.BlockSpec((tk, FOX_W), lambda j, i: (j, 0)), pl.BlockSpec((tk, FOX_W), lambda j, i: (j, 0)),
                   pl.BlockSpec((tk, LANES), lambda j, i: (j, 0))],
        out_shape=[jax.ShapeDtypeStruct((t, FOX_W), bf16), jax.ShapeDtypeStruct((t, FOX_W), bf16),
                   jax.ShapeDtypeStruct((t, LANES), f32)],
        scratch_shapes=[pltpu.VMEM((tk, FOX_W), f32), pltpu.VMEM((tk, FOX_W), f32), pltpu.VMEM((tk, LANES), f32)],
        compiler_params=_params(("arbitrary", "arbitrary")),
    )(qkv, qkv, qkv, dmix, cum_t, cum, lse_t, delta_t)


ATT_W = HEADS * LANES


def _data_lane(h):
    return HEAD_D * (h % 2)


def _extra_lane(h):
    return HEAD_D - _data_lane(h)


def _split3(x):
    hi = x.astype(bf16)
    rest = x - hi.astype(f32)
    mid = rest.astype(bf16)
    lo = (rest - mid.astype(f32)).astype(bf16)
    return hi, mid, lo


def _augment(pair, h, first, second, fill=0.0):
    rows = pair.shape[0]
    lane = lax.broadcasted_iota(jnp.int32, (rows, LANES), 1)
    base = _extra_lane(h)
    own = (lane < HEAD_D) if h % 2 == 0 else (lane >= HEAD_D)
    out = jnp.where(own, pair, jnp.full((rows, LANES), fill, bf16))
    for off, src in ((0, first), (3, second)):
        for q in range(3):
            val = src[q] if isinstance(src, tuple) else jnp.full((rows, 1), src, bf16)
            out = jnp.where(lane == base + off + q, val, out)
    return out


def _attn_prep_fwd(qkv, cum, *, tm, name):
    t = qkv.shape[0]

    def body(q_ref, k_ref, v_ref, c_ref, qa_ref, ka_ref, va_ref):
        for h in range(HEADS):
            pair = slice(LANES * (h // 2), LANES * (h // 2 + 1))
            hs = slice(LANES * h, LANES * (h + 1))
            c3 = _split3(c_ref[:, h:h + 1])
            qa_ref[:, hs] = _augment(q_ref[:, pair] * ATT_SCALE, h, c3, 1.0)
            ka_ref[:, hs] = _augment(k_ref[:, pair], h, 1.0, tuple(-p for p in c3))
            va_ref[:, hs] = _augment(v_ref[:, pair], h, 1.0, 1.0, fill=1.0)

    wide = pl.BlockSpec((tm, ATT_W), lambda i: (i, 0))
    out = jax.ShapeDtypeStruct((t, ATT_W), bf16)
    return pl.pallas_call(
        body, name=name, grid=(t // tm,),
        in_specs=[pl.BlockSpec((tm, FOX_W), lambda i: (i, 0)), pl.BlockSpec((tm, FOX_W), lambda i: (i, 1)),
                  pl.BlockSpec((tm, FOX_W), lambda i: (i, 2)), pl.BlockSpec((tm, LANES), lambda i: (i, 0))],
        out_specs=[wide] * 3, out_shape=[out] * 3, compiler_params=_params(("arbitrary",)),
    )(qkv, qkv, qkv, cum)


def _attn_prep_bwd(qkv, cum, lse, dmix, o, *, tm, name):
    t = qkv.shape[0]

    def body(q_ref, c_ref, l_ref, do_ref, o_ref, qa_ref, da_ref):
        for h in range(HEADS):
            pair = slice(LANES * (h // 2), LANES * (h // 2 + 1))
            src = slice(HEAD_D * h, HEAD_D * (h + 1))
            hs = slice(LANES * h, LANES * (h + 1))
            delta = jnp.sum(do_ref[:, src] * o_ref[:, src], axis=-1, keepdims=True)
            qa_ref[:, hs] = _augment(q_ref[:, pair] * ATT_SCALE, h,
                                     _split3(c_ref[:, h:h + 1] - l_ref[:, h:h + 1]), 1.0)
            da_ref[:, hs] = _augment(do_ref[:, pair].astype(bf16), h, tuple(-p for p in _split3(delta)), 0.0)

    wide = pl.BlockSpec((tm, ATT_W), lambda i: (i, 0))
    half = pl.BlockSpec((tm, FOX_W), lambda i: (i, 0))
    col = pl.BlockSpec((tm, LANES), lambda i: (i, 0))
    out = jax.ShapeDtypeStruct((t, ATT_W), bf16)
    return pl.pallas_call(
        body, name=name, grid=(t // tm,), in_specs=[half, col, col, half, half],
        out_specs=[wide] * 2, out_shape=[out] * 2, compiler_params=_params(("arbitrary",)),
    )(qkv, cum, lse, dmix, o)


def _attn_fwd2(q_aug, k_aug, v_aug, *, name, host=None):
    t = q_aug.shape[0]
    n = t // ATT_TILE
    tq = ATT_TILE

    def body(q_ref, k_ref, v_ref, o_ref, lse_ref, acc, m_s):
        i = pl.program_id(0)
        j = pl.program_id(1)

        @pl.when(j == 0)
        def _():
            acc[...] = jnp.zeros_like(acc)
            m_s[...] = jnp.full_like(m_s, NEG_BIG)

        def block(masked):
            mask = _causal(i, j, False) if masked else None
            for h in range(HEADS):
                hs = slice(LANES * h, LANES * (h + 1))
                s = lax.dot_general(q_ref[:, hs], k_ref[:, hs], _NT, preferred_element_type=f32)
                if masked:
                    s = jnp.where(mask, s, NEG_BIG)
                m_old = m_s[:, h:h + 1]
                m_new = jnp.maximum(m_old, jnp.max(s, axis=-1, keepdims=True))
                p = jnp.exp(s - m_new).astype(bf16)
                acc[h] = jnp.exp(m_old - m_new) * acc[h] + jnp.dot(p, v_ref[:, hs], preferred_element_type=f32)
                m_s[:, h:h + 1] = m_new

        @pl.when(j < i)
        def _():
            block(False)

        @pl.when(j == i)
        def _():
            block(True)
            lse_ref[...] = jnp.zeros_like(lse_ref)
            for h in range(HEADS):
                a = acc[h]
                l = a[:, _extra_lane(h):_extra_lane(h) + 1]
                o_ref[:, HEAD_D * h:HEAD_D * (h + 1)] = a[:, _data_lane(h):_data_lane(h) + HEAD_D] / l
                lse_ref[:, h:h + 1] = m_s[:, h:h + 1] + jnp.log(l)

    kv = pl.BlockSpec((tq, ATT_W), lambda i, j: (jnp.minimum(i, j), 0))
    return _hosted_call(
        host, body, name=name, grid=(n, n),
        in_specs=[pl.BlockSpec((tq, ATT_W), lambda i, j: (i, 0)), kv, kv],
        out_specs=[pl.BlockSpec((tq, FOX_W), lambda i, j: (i, 0)), pl.BlockSpec((tq, LANES), lambda i, j: (i, 0))],
        out_shape=[jax.ShapeDtypeStruct((t, FOX_W), f32), jax.ShapeDtypeStruct((t, LANES), f32)],
        scratch_shapes=[pltpu.VMEM((HEADS, tq, LANES), f32), pltpu.VMEM((tq, LANES), f32)],
        compiler_params=_params(("arbitrary", "arbitrary")),
    )(q_aug, k_aug, v_aug)


def _attn_dq2(qb_aug, k_aug, v_aug, do_aug, *, name, host=None):
    t = qb_aug.shape[0]
    n = t // ATT_TILE
    tq = ATT_TILE

    def body(q_ref, k_ref, v_ref, do_ref, dq_ref, dc_ref, acc):
        i = pl.program_id(0)
        j = pl.program_id(1)

        @pl.when(j == 0)
        def _():
            acc[...] = jnp.zeros_like(acc)

        def block(masked):
            mask = _causal(i, j, False) if masked else None
            for h in range(HEADS):
                hs = slice(LANES * h, LANES * (h + 1))
                kh = k_ref[:, hs]
                s = lax.dot_general(q_ref[:, hs], kh, _NT, preferred_element_type=f32)
                if masked:
                    s = jnp.where(mask, s, NEG_BIG)
                dp = lax.dot_general(do_ref[:, hs], v_ref[:, hs], _NT, preferred_element_type=f32)
                ds = (jnp.exp(s) * dp).astype(bf16)
                acc[h] += jnp.dot(ds, kh, preferred_element_type=f32)

        @pl.when(j < i)
        def _():
            block(False)

        @pl.when(j == i)
        def _():
            block(True)
            dc_ref[...] = jnp.zeros_like(dc_ref)
            for h in range(HEADS):
                a = acc[h]
                dq_ref[:, HEAD_D * h:HEAD_D * (h + 1)] = (
                    a[:, _data_lane(h):_data_lane(h) + HEAD_D] * ATT_SCALE).astype(bf16)
                dc_ref[:, h:h + 1] = a[:, _extra_lane(h):_extra_lane(h) + 1]

    own = pl.BlockSpec((tq, ATT_W), lambda i, j: (i, 0))
    kv = pl.BlockSpec((tq, ATT_W), lambda i, j: (jnp.minimum(i, j), 0))
    return _hosted_call(
        host, body, name=name, grid=(n, n), in_specs=[own, kv, kv, own],
        out_specs=[pl.BlockSpec((tq, FOX_W), lambda i, j: (i, 0)), pl.BlockSpec((tq, LANES), lambda i, j: (i, 0))],
        out_shape=[jax.ShapeDtypeStruct((t, FOX_W), bf16), jax.ShapeDtypeStruct((t, LANES), f32)],
        scratch_shapes=[pltpu.VMEM((HEADS, tq, LANES), f32)],
        compiler_params=_params(("arbitrary", "arbitrary")),
    )(qb_aug, k_aug, v_aug, do_aug)


def _attn_dkv2(qb_aug, k_aug, v_aug, do_aug, *, name, host=None):
    t = qb_aug.shape[0]
    n = t // ATT_TILE
    tk = ATT_TILE

    def body(q_ref, k_ref, v_ref, do_ref, dk_ref, dv_ref, dc_ref, dk_acc, dv_acc):
        j = pl.program_id(0)
        i = pl.program_id(1)

        @pl.when(i == 0)
        def _():
            dk_acc[...] = jnp.zeros_like(dk_acc)
            dv_acc[...] = jnp.zeros_like(dv_acc)

        def block(masked):
            mask = _causal(i, j, True) if masked else None
            for h in range(HEADS):
                hs = slice(LANES * h, LANES * (h + 1))
                qh = q_ref[:, hs]
                doh = do_ref[:, hs]
                s_t = lax.dot_general(k_ref[:, hs], qh, _NT, preferred_element_type=f32)
                if masked:
                    s_t = jnp.where(mask, s_t, NEG_BIG)
                p_t = jnp.exp(s_t)
                dv_acc[h] += jnp.dot(p_t.astype(bf16), doh, preferred_element_type=f32)
                dp_t = lax.dot_general(v_ref[:, hs], doh, _NT, preferred_element_type=f32)
                dk_acc[h] += jnp.dot((p_t * dp_t).astype(bf16), qh, preferred_element_type=f32)

        @pl.when(i > j)
        def _():
            block(False)

        @pl.when(i == j)
        def _():
            block(True)

        @pl.when(i == n - 1)
        def _():
            dc_ref[...] = jnp.zeros_like(dc_ref)
            for h in range(HEADS):
                a = dk_acc[h]
                cols = slice(_data_lane(h), _data_lane(h) + HEAD_D)
                dk_ref[:, HEAD_D * h:HEAD_D * (h + 1)] = a[:, cols].astype(bf16)
                dv_ref[:, HEAD_D * h:HEAD_D * (h + 1)] = dv_acc[h][:, cols].astype(bf16)
                dc_ref[:, h:h + 1] = -a[:, _extra_lane(h) + 3:_extra_lane(h) + 4]

    own = pl.BlockSpec((tk, ATT_W), lambda j, i: (j, 0))
    qs = pl.BlockSpec((tk, ATT_W), lambda j, i: (jnp.maximum(i, j), 0))
    half = pl.BlockSpec((tk, FOX_W), lambda j, i: (j, 0))
    return _hosted_call(
        host, body, name=name, grid=(n, n), in_specs=[qs, own, own, qs],
        out_specs=[half, half, pl.BlockSpec((tk, LANES), lambda j, i: (j, 0))],
        out_shape=[jax.ShapeDtypeStruct((t, FOX_W), bf16), jax.ShapeDtypeStruct((t, FOX_W), bf16),
                   jax.ShapeDtypeStruct((t, LANES), f32)],
        scratch_shapes=[pltpu.VMEM((HEADS, tk, LANES), f32), pltpu.VMEM((HEADS, tk, LANES), f32)],
        compiler_params=_params(("arbitrary", "arbitrary")),
    )(qb_aug, k_aug, v_aug, do_aug)


LRU_CHUNK = 64
SUB = 8


def _row_ids(n):
    return lax.broadcasted_iota(jnp.int32, (n, LANES), 0)


def _shift_rows_down(ext, s):
    return pltpu.roll(ext, s, axis=0)[SUB:, :]


def _shift_rows_up(ext, s, n):
    return pltpu.roll(ext, ext.shape[0] - s, axis=0)[:n, :]


def _lru_gates(u, wa_ref, ba_ref, wx_ref, bx_ref, sp):
    ub = u.astype(bf16)
    r = _sigmoid(jnp.dot(ub, wa_ref[...], preferred_element_type=f32) + ba_ref[...])
    gi = _sigmoid(jnp.dot(ub, wx_ref[...], preferred_element_type=f32) + bx_ref[...])
    log_a = -LRU_C * r * sp
    a = jnp.exp(log_a)
    s = jnp.sqrt(_one_minus_exp(2.0 * log_a))
    return r, gi, a, s


def _conv_window(lx_ref, r0, ci):
    cur = lx_ref[pl.ds(r0, LRU_CHUNK), :]
    p0 = pl.multiple_of(jnp.maximum(r0 - SUB, 0), SUB)
    prev = jnp.where(ci > 0, lx_ref[pl.ds(p0, SUB), :], 0.0)
    return cur, jnp.concatenate([prev, cur], axis=0)


def _lru_fwd(zl, conv_w, conv_b, wa, ba, wx, bx, lam, *, name, host=None):
    t = zl.shape[0]
    n_chunk = t // LRU_CHUNK

    def body(lx_ref, lg_ref, cw_ref, cb_ref, wa_ref, ba_ref, wx_ref, bx_ref, lam_ref, u_ref, h_ref, y_ref):
        sp = _softplus(-lam_ref[...])
        rows = _row_ids(SUB)

        def chunk(ci, hc):
            r0 = pl.multiple_of(ci * LRU_CHUNK, LRU_CHUNK)
            cur, ext = _conv_window(lx_ref, r0, ci)
            u = cb_ref[...] + cw_ref[3:4, :] * cur
            for k in range(3):
                u = u + cw_ref[k:k + 1, :] * _shift_rows_down(ext, 3 - k)
            r, gi, a, s = _lru_gates(u, wa_ref, ba_ref, wx_ref, bx_ref, sp)
            b = s * (gi * u)
            tiles = []
            for q in range(LRU_CHUNK // SUB):
                ta = a[SUB * q:SUB * (q + 1), :]
                tb = b[SUB * q:SUB * (q + 1), :]
                for d in (1, 2, 4):
                    a_sh = jnp.where(rows >= d, pltpu.roll(ta, d, axis=0), 1.0)
                    b_sh = jnp.where(rows >= d, pltpu.roll(tb, d, axis=0), 0.0)
                    tb = ta * b_sh + tb
                    ta = ta * a_sh
                hq = tb + ta * hc
                hc = hq[SUB - 1:SUB, :]
                tiles.append(hq)
            h = jnp.concatenate(tiles, axis=0)
            u_ref[pl.ds(r0, LRU_CHUNK), :] = u
            h_ref[pl.ds(r0, LRU_CHUNK), :] = h
            gel, _ = _gelu_and_grad(lg_ref[pl.ds(r0, LRU_CHUNK), :])
            y_ref[pl.ds(r0, LRU_CHUNK), :] = gel * h
            return hc

        lax.fori_loop(0, n_chunk, chunk, jnp.zeros((1, LANES), f32))

    seq = lambda cb: pl.BlockSpec((t, LANES), lambda c, cb=cb: (0, c + cb))
    rowc = pl.BlockSpec((1, LANES), lambda c: (0, c))
    diag = pl.BlockSpec((LANES, LANES), lambda c: (c, c))
    out = jax.ShapeDtypeStruct((t, LRU_W), f32)
    return _hosted_call(
        host, body, name=name, grid=(LRU_W // LANES,),
        in_specs=[seq(0), seq(4), pl.BlockSpec((4, LANES), lambda c: (0, c)), rowc, diag, rowc, diag, rowc, rowc],
        out_specs=[seq(0)] * 3, out_shape=[out] * 3,
        compiler_params=_params(("arbitrary",)),
    )(zl, zl, conv_w, conv_b, wa, ba, wx, bx, lam)


def _lru_bwd(dmix, zl, u_all, h_all, conv_w, wa, ba, wx, bx, lam, *, name, host=None):
    t = zl.shape[0]
    n_chunk = t // LRU_CHUNK

    def body(dy_ref, lx_ref, lg_ref, u_ref, h_ref, cw_ref, wa_ref, ba_ref, wx_ref, bx_ref, lam_ref,
             dlx_ref, dlg_ref, dcw_ref, dcb_ref, dba_ref, dbx_ref, dlam_ref, dwa_ref, dwx_ref, dpr_s, dpx_s):
        lam_v = lam_ref[...]
        sp = _softplus(-lam_v)
        rows = _row_ids(SUB)
        rows_c = _row_ids(LRU_CHUNK)
        zero_row = jnp.zeros((1, LANES), f32)

        def chunk(step, carry):
            dh_c, a_next0, du_next, dsp, dba, dbx, dcb, dw0, dw1, dw2, dw3 = carry
            ci = n_chunk - 1 - step
            r0 = pl.multiple_of(ci * LRU_CHUNK, LRU_CHUNK)
            sl = pl.ds(r0, LRU_CHUNK)
            u = u_ref[sl, :]
            r, gi, a, s = _lru_gates(u, wa_ref, ba_ref, wx_ref, bx_ref, sp)
            h = h_ref[sl, :]
            p0 = pl.multiple_of(jnp.maximum(r0 - SUB, 0), SUB)
            h_before = jnp.where(ci > 0, h_ref[pl.ds(p0, SUB), :], 0.0)[SUB - 1:SUB, :]
            h_prev = jnp.where(rows_c == 0, h_before, pltpu.roll(h, 1, axis=0))
            gel, dgel = _gelu_and_grad(lg_ref[sl, :])
            dy = dy_ref[sl, :]
            dlg_ref[sl, :] = (dy * h * dgel).astype(bf16)
            g_in = dy * gel
            a_next = jnp.where(rows_c == LRU_CHUNK - 1, a_next0, pltpu.roll(a, LRU_CHUNK - 1, axis=0))
            tiles = [None] * (LRU_CHUNK // SUB)
            for q in reversed(range(LRU_CHUNK // SUB)):
                ta = a_next[SUB * q:SUB * (q + 1), :]
                tb = g_in[SUB * q:SUB * (q + 1), :]
                for d in (1, 2, 4):
                    a_sh = jnp.where(rows < SUB - d, pltpu.roll(ta, SUB - d, axis=0), 1.0)
                    b_sh = jnp.where(rows < SUB - d, pltpu.roll(tb, SUB - d, axis=0), 0.0)
                    tb = ta * b_sh + tb
                    ta = ta * a_sh
                dhq = tb + ta * dh_c
                dh_c = dhq[0:1, :]
                tiles[q] = dhq
            dh = jnp.concatenate(tiles, axis=0)
            da = dh * h_prev
            ds = dh * gi * u
            dgi = dh * s * u
            du = dh * s * gi
            dlog_a = da * a - ds * (a * a) / s
            dr = dlog_a * (-LRU_C * sp)
            dsp = dsp + jnp.sum(dlog_a * (-LRU_C * r), axis=0, keepdims=True)
            dpr = dr * r * (1.0 - r)
            dpx = dgi * gi * (1.0 - gi)
            dprb = dpr.astype(bf16)
            dpxb = dpx.astype(bf16)
            dpr_s[sl, :] = dprb
            dpx_s[sl, :] = dpxb
            du = du + (lax.dot_general(dprb, wa_ref[...], _NT, preferred_element_type=f32)
                       + lax.dot_general(dpxb, wx_ref[...], _NT, preferred_element_type=f32))
            dba = dba + jnp.sum(dpr, axis=0, keepdims=True)
            dbx = dbx + jnp.sum(dpx, axis=0, keepdims=True)
            dcb = dcb + jnp.sum(du, axis=0, keepdims=True)
            du_ext = jnp.concatenate([du, du_next], axis=0)
            dlx = cw_ref[3:4, :] * du
            for k in range(3):
                dlx = dlx + cw_ref[k:k + 1, :] * _shift_rows_up(du_ext, 3 - k, LRU_CHUNK)
            dlx_ref[sl, :] = dlx.astype(bf16)
            cur, ext = _conv_window(lx_ref, r0, ci)
            dws = [dw0, dw1, dw2, dw3 + jnp.sum(du * cur, axis=0, keepdims=True)]
            for k in range(3):
                dws[k] = dws[k] + jnp.sum(du * _shift_rows_down(ext, 3 - k), axis=0, keepdims=True)
            return (dh_c, a[0:1, :], du[0:SUB, :], dsp, dba, dbx, dcb, dws[0], dws[1], dws[2], dws[3])

        init = (zero_row, zero_row, jnp.zeros((SUB, LANES), f32)) + (zero_row,) * 8
        out = lax.fori_loop(0, n_chunk, chunk, init)
        _, _, _, dsp, dba, dbx, dcb, dw0, dw1, dw2, dw3 = out
        dlam_ref[...] = dsp * (-_sigmoid(-lam_v))
        dba_ref[...] = dba
        dbx_ref[...] = dbx
        dcb_ref[...] = dcb
        dcw_ref[...] = jnp.concatenate([dw0, dw1, dw2, dw3], axis=0)
        ub = u_ref[...].astype(bf16)
        dwa_ref[...] = lax.dot_general(ub, dpr_s[...], _TN, preferred_element_type=f32)
        dwx_ref[...] = lax.dot_general(ub, dpx_s[...], _TN, preferred_element_type=f32)

    seq = lambda cb: pl.BlockSpec((t, LANES), lambda c, cb=cb: (0, c + cb))
    rowc = pl.BlockSpec((1, LANES), lambda c: (0, c))
    diag = pl.BlockSpec((LANES, LANES), lambda c: (c, c))
    gate_out = pl.BlockSpec((None, LANES, LANES), lambda c: (c, 0, 0))
    row_shape = jax.ShapeDtypeStruct((1, LRU_W), f32)
    return _hosted_call(
        host, body, name=name, grid=(LRU_W // LANES,),
        in_specs=[seq(4), seq(0), seq(4), seq(0), seq(0), pl.BlockSpec((4, LANES), lambda c: (0, c)),
                  diag, rowc, diag, rowc, rowc],
        out_specs=[seq(0), seq(0), pl.BlockSpec((4, LANES), lambda c: (0, c)), rowc, rowc, rowc, rowc,
                   gate_out, gate_out],
        out_shape=[jax.ShapeDtypeStruct((t, LRU_W), bf16)] * 2
        + [jax.ShapeDtypeStruct((4, LRU_W), f32)] + [row_shape] * 4
        + [jax.ShapeDtypeStruct((LRU_W // LANES, LANES, LANES), f32)] * 2,
        scratch_shapes=[pltpu.VMEM((t, LANES), bf16), pltpu.VMEM((t, LANES), bf16)],
        compiler_params=_params(("arbitrary",)),
    )(dmix, zl, zl, u_all, h_all, conv_w, wa, ba, wx, bx, lam)


def _block_diag(w):
    eye = jnp.eye(HEADS, dtype=w.dtype)
    return jnp.einsum("hij,hk->hikj", w, eye).reshape(LRU_W, LRU_W)


def _diag_blocks(dw):
    top = dw[:, :HEAD_D, :HEAD_D]
    bot = dw[:, HEAD_D:, HEAD_D:]
    return jnp.stack([top, bot], axis=1).reshape(HEADS, HEAD_D, HEAD_D)


def _local_step(x, target, sent, small, *, tm=512, tm_ffn=1024):
    t = x.shape[0]
    ones = jnp.ones((1, D_MODEL), f32)
    zeros = jnp.zeros((1, D_MODEL), f32)
    ln1 = (small["ln1_g"], small["ln1_b"])
    ln2 = (small["ln2_g"], small["ln2_b"])
    ln3 = (small["ln3_g"], small["ln3_b"])

    wg1, wu1, wd1 = _exchange([sent["ffn1_w_gate"], sent["ffn1_w_up"], sent["ffn1_w_down"]], gather=True,
                              name="gather_ffn1")
    xh1, rs1, hg1, hu1, w_in_g, w_out_g, conv_w_g = _ffn_fwd(
        x, ones, zeros, wg1, wu1, wd1, tm=tm_ffn, name="ffn1_fwd",
        host=_Exchange([sent["w_in"], sent["w_out"], sent["conv_w"]], gather=True))
    w_in = jnp.pad(w_in_g.transpose(1, 0, 2).reshape(D_MODEL, IN_COLS), ((0, 0), (0, 21 * LANES - IN_COLS)))
    w_out = w_out_g.reshape(D_MODEL, D_MODEL)
    conv_w = conv_w_g.transpose(1, 0, 2).reshape(4, LRU_W)
    qkv = _mm(xh1, w_in, mode="nn", out_dtype=bf16, tm=tm, tn=512, tk=D_MODEL, name="qkv_fwd",
              affine=ln1, b_cols=(0, 1536))
    zl = _mm(xh1, w_in, mode="nn", out_dtype=f32, tm=tm, tn=512, tk=D_MODEL, name="zl_fwd",
             affine=ln1, b_cols=(3, 1024))
    zfg = _mm(xh1, w_in, mode="nn", out_dtype=f32, tm=tm, tn=LANES, tk=D_MODEL, name="zfg_fwd",
              affine=ln1, b_cols=(20, LANES))
    bfg = jnp.pad(small["b_forget"], ((0, 0), (0, LANES - HEADS)))
    cum = _cum_fwd(zfg, bfg, name="cum_fwd")
    q_aug, k_aug, v_aug = _attn_prep_fwd(qkv, cum, tm=tm, name="attn_prep_fwd")
    o, lse, wg2, wu2 = _attn_fwd2(q_aug, k_aug, v_aug, name="attn_fwd",
                                  host=_Exchange([sent["ffn2_w_gate"], sent["ffn2_w_up"]], gather=True))
    wa_bd = _block_diag(small["rg_wa"]).astype(bf16)
    wx_bd = _block_diag(small["rg_wx"]).astype(bf16)
    ba = small["rg_ba"].reshape(1, LRU_W)
    bx = small["rg_bx"].reshape(1, LRU_W)
    u, h, lru, wd2 = _lru_fwd(zl, conv_w, small["conv_b"], wa_bd, ba, wx_bd, bx, small["lru_lambda"],
                              name="lru_fwd", host=_Exchange([sent["ffn2_w_down"]], gather=True))
    xh2, rs2 = _mmln([(o, 0, FOX_W, w_out, 0, D_MODEL, "nn"), (lru, 0, LRU_W, w_out, 1, D_MODEL, "nn")],
                     tm=tm, name="mix_fwd", resid=("affine", xh1) + ln1, resid_scale=ALPHA, epi="ln_fwd")
    xh3, rs3, hg2, hu2 = _ffn_fwd(xh2, ln2[0], ln2[1], wg2, wu2, wd2, tm=tm_ffn, name="ffn2_fwd")

    dpre3, sq_rows, g_ln3g, g_ln3b = _loss_bwd(xh3, rs3, ln3[0], ln3[1], target, tm=tm, name="loss_bwd")
    dpre2, g_ln2g, g_ln2b, dhg2, dhu2, a2 = _ffn_bwd(dpre3, hg2, hu2, wg2, wu2, wd2,
                                                     (xh2, rs2, ln2[0]), tm=tm, name="ffn2_bwd")
    wgrad = dict(out_dtype=bf16, tm=D_MODEL, mb=1, tn=FF_TILE, nb=4, tk=512, pair=True)
    wdgrad = dict(out_dtype=bf16, tm=512, mb=4, tn=D_MODEL, nb=1, tk=512, out_scale=0.5, pair=True)
    between_chips = functools.partial(_Exchange, gather=False, chips=True)
    g_wg2 = _mm_tn(xh2, dhg2, name="g_wg2", affine=ln2, **wgrad)
    g_wu2 = _mm_tn(xh2, dhu2, name="g_wu2", affine=ln2, **wgrad)
    g_wd2 = _mm_tn(a2, dpre3, name="g_wd2", **wdgrad)

    dmix = _mmln([(dpre2, 0, D_MODEL, w_out, 0, D_MODEL, "nt")], tm=tm, name="dmix_bwd")
    g_wout_a = _mm(o, dpre2, mode="tn", out_dtype=bf16, tm=512, tn=D_MODEL, tk=512, name="g_wout_fox")
    g_wout_b = _mm(lru, dpre2, mode="tn", out_dtype=bf16, tm=512, tn=D_MODEL, tk=512, name="g_wout_lru")
    dlx, dlg, g_cw, g_cb, g_ba, g_bx, g_lam, g_wa4, g_wx4, *p_wg2 = _lru_bwd(
        dmix, zl, u, h, conv_w, wa_bd, ba, wx_bd, bx, small["lru_lambda"], name="lru_bwd",
        host=between_chips([g_wg2]))
    p_wg2 = p_wg2[0]
    qb_aug, do_aug = _attn_prep_bwd(qkv, cum, lse, dmix, o, tm=tm, name="attn_prep_bwd")
    dq, dcum_q, p_wu2 = _attn_dq2(qb_aug, k_aug, v_aug, do_aug, name="attn_dq",
                                  host=between_chips([g_wu2]))
    g_wout_blocked = jnp.concatenate([g_wout_a, g_wout_b], axis=0).reshape(N_DEV, D_MODEL // N_DEV, D_MODEL)
    dk, dv, dcum_k, p_wd2 = _attn_dkv2(qb_aug, k_aug, v_aug, do_aug, name="attn_dkv", host=between_chips([g_wd2]))
    dfg, g_bf = _cum_bwd(dcum_q, dcum_k, zfg, bfg, name="cum_bwd")

    dz = [(dq, 0, 512), (dk, 1, 512), (dv, 2, 512), (dlx, 3, 512), (dlg, 4, 512), (dfg, 20, LANES)]
    dpre1, g_ln1g, g_ln1b = _mmln(
        [(arr, 0, w, w_in, cb, w, "nt") for (arr, cb, w) in dz],
        tm=tm, name="dx1_bwd", resid=("plain", dpre2), resid_scale=ALPHA, epi="ln_bwd", ln=(xh1, rs1, ln1[0]))
    g_win = [_mm(xh1, arr, mode="tn", out_dtype=bf16, tm=D_MODEL, tn=w, tk=512, name=f"g_win{n}", affine=ln1)
             for n, (arr, cb, w) in enumerate(dz)]
    g_win_full = jnp.concatenate([g[:, :w] for g, (_, _, w) in zip(g_win, dz)], axis=1)[:, :IN_COLS]
    g_win_blocked = g_win_full.reshape(D_MODEL, N_DEV, IN_SHARD).transpose(1, 0, 2)
    dhg1, dhu1, a1, p_win, p_wout = _ffn_bwd_act(dpre1, hg1, hu1, wd1, tm=tm_ffn, name="ffn1_bwd_act",
                                                 host=_Exchange([g_win_blocked, g_wout_blocked], gather=False))
    small_g = {
        "ln1_g": g_ln1g, "ln1_b": g_ln1b, "b_forget": g_bf[:, :HEADS], "conv_w": g_cw, "conv_b": g_cb,
        "rg_wa": _diag_blocks(g_wa4), "rg_ba": g_ba.reshape(HEADS, HEAD_D),
        "rg_wx": _diag_blocks(g_wx4), "rg_bx": g_bx.reshape(HEADS, HEAD_D), "lru_lambda": g_lam,
        "ln2_g": g_ln2g, "ln2_b": g_ln2b, "ln3_g": g_ln3g, "ln3_b": g_ln3b,
    }
    pieces = [small_g[n].reshape(-1) for n in PACKED]
    packed = jnp.concatenate(pieces + [jnp.zeros((PACK_ROWS * LANES - sum(p.shape[0] for p in pieces),), f32)])
    g_wg1, all_packed = _mm_tn(x, dhg1, name="g_wg1",
                               host=_Exchange([packed.reshape(PACK_ROWS, LANES)], gather=True), **wgrad)
    g_wu1, p_wg1 = _mm_tn(x, dhu1, name="g_wu1", host=between_chips([g_wg1]), **wgrad)
    g_wd1, p_wu1 = _mm_tn(a1, dpre1, name="g_wd1", host=between_chips([g_wu1]), **wdgrad)
    grad_x, p_wd1 = _ffn_bwd_dx(dpre1, dhg1, dhu1, wg1, wu1, tm=tm_ffn, name="ffn1_bwd_dx",
                                host=between_chips([g_wd1]))
    parts = {
        "ffn1_w_gate": p_wg1, "ffn1_w_up": p_wu1, "ffn1_w_down": p_wd1, "w_in": p_win, "w_out": p_wout,
        "ffn2_w_gate": p_wg2, "ffn2_w_up": p_wu2, "ffn2_w_down": p_wd2,
    }
    return sq_rows, grad_x, parts, all_packed, {n: small_g[n].shape for n in PACKED}


def _adam_math(w, g, m, v):
    m2 = ADAM_B1 * m + (1.0 - ADAM_B1) * g
    v2 = ADAM_B2 * v + (1.0 - ADAM_B2) * (g * g)
    m_hat = m2 / (1.0 - ADAM_B1 ** ADAM_STEP)
    v_hat = v2 / (1.0 - ADAM_B2 ** ADAM_STEP)
    delta = -ADAM_LR * (m_hat / (jnp.sqrt(v_hat) + ADAM_EPS) + ADAM_WD * w)
    return delta, m2, v2


ADAM_TILE_ELEMS = 128 * 1024


def _adamw_big(parts, w, m, v, *, name):
    r, c = w.shape
    n_parts = parts.shape[0]
    tr = max(d for d in range(8, r + 1, 8) if r % d == 0 and d * c <= ADAM_TILE_ELEMS)

    def body(p_ref, w_ref, m_ref, v_ref, g_ref, d_ref, m2_ref, v2_ref):
        g = p_ref[0].astype(f32)
        for q in range(1, n_parts):
            g = g + p_ref[q].astype(f32)
        d, m2, v2 = _adam_math(w_ref[...], g, m_ref[...], v_ref[...])
        g_ref[...] = g
        d_ref[...] = d
        m2_ref[...] = m2
        v2_ref[...] = v2

    blk = pl.BlockSpec((tr, c), lambda i: (i, 0))
    return pl.pallas_call(
        body, name=name, grid=(r // tr,),
        in_specs=[pl.BlockSpec((n_parts, tr, c), lambda i: (0, i, 0)), blk, blk, blk],
        out_specs=[blk] * 4, out_shape=[jax.ShapeDtypeStruct((r, c), f32)] * 4,
        compiler_params=_params(("arbitrary",)),
    )(parts, w, m, v)


def _adamw_small(items, *, name):
    n = len(items)

    def body(*refs):
        ins, outs = refs[:4 * n], refs[4 * n:]
        for k in range(n):
            g, w, m, v = (ins[4 * k + q][...] for q in range(4))
            d, m2, v2 = _adam_math(w, g, m, v)
            outs[3 * k][...] = d
            outs[3 * k + 1][...] = m2
            outs[3 * k + 2][...] = v2

    vm = pl.BlockSpec(memory_space=pltpu.VMEM)
    flat = [a for item in items for a in item]
    out_shape = [jax.ShapeDtypeStruct(item[1].shape, f32) for item in items for _ in range(3)]
    return pl.pallas_call(
        body, name=name, in_specs=[vm] * (4 * n), out_specs=[vm] * (3 * n), out_shape=out_shape,
    )(*flat)


def _sum_parts(parts, *, name):
    def body(p_ref, o_ref):
        acc = p_ref[0]
        for q in range(1, N_DEV):
            acc = acc + p_ref[q]
        o_ref[...] = acc

    vm = pl.BlockSpec(memory_space=pltpu.VMEM)
    return pl.pallas_call(
        body, name=name, in_specs=[vm], out_specs=vm, out_shape=jax.ShapeDtypeStruct(parts.shape[1:], f32),
    )(parts)


WEIGHTS = ["ffn1_w_gate", "ffn1_w_up", "ffn1_w_down", "ln1_g", "ln1_b", "w_in", "b_forget", "conv_w", "conv_b",
           "rg_wa", "rg_ba", "rg_wx", "rg_bx", "lru_lambda", "w_out", "ln2_g", "ln2_b",
           "ffn2_w_gate", "ffn2_w_up", "ffn2_w_down", "ln3_g", "ln3_b"]
BIG = ["ffn1_w_gate", "ffn1_w_up", "ffn1_w_down", "w_in", "w_out", "ffn2_w_gate", "ffn2_w_up", "ffn2_w_down"]
PACKED = ["ln1_g", "ln1_b", "ln2_g", "ln2_b", "ln3_g", "ln3_b", "conv_b", "rg_ba", "rg_bx", "lru_lambda",
          "conv_w", "rg_wa", "rg_wx", "b_forget"]
PACK_ROWS = 600


def _two_d(a):
    return a.reshape((-1, a.shape[-1]))


def _transport(a):
    return _two_d(a)


def kernel(x, ffn1_w_gate, ffn1_w_up, ffn1_w_down, ln1_g, ln1_b, w_in, b_forget, conv_w, conv_b, rg_wa, rg_ba, rg_wx, rg_bx, lru_lambda, w_out, ln2_g, ln2_b, ffn2_w_gate, ffn2_w_up, ffn2_w_down, ln3_g, ln3_b, loss_target, m_ffn1_w_gate, m_ffn1_w_up, m_ffn1_w_down, m_ln1_g, m_ln1_b, m_w_in, m_b_forget, m_conv_w, m_conv_b, m_rg_wa, m_rg_ba, m_rg_wx, m_rg_bx, m_lru_lambda, m_w_out, m_ln2_g, m_ln2_b, m_ffn2_w_gate, m_ffn2_w_up, m_ffn2_w_down, m_ln3_g, m_ln3_b, v_ffn1_w_gate, v_ffn1_w_up, v_ffn1_w_down, v_ln1_g, v_ln1_b, v_w_in, v_b_forget, v_conv_w, v_conv_b, v_rg_wa, v_rg_ba, v_rg_wx, v_rg_bx, v_lru_lambda, v_w_out, v_ln2_g, v_ln2_b, v_ffn2_w_gate, v_ffn2_w_up, v_ffn2_w_down, v_ln3_g, v_ln3_b):
    w_args = (ffn1_w_gate, ffn1_w_up, ffn1_w_down, ln1_g, ln1_b, w_in, b_forget, conv_w, conv_b, rg_wa, rg_ba, rg_wx, rg_bx, lru_lambda, w_out, ln2_g, ln2_b, ffn2_w_gate, ffn2_w_up, ffn2_w_down, ln3_g, ln3_b)
    m_args = (m_ffn1_w_gate, m_ffn1_w_up, m_ffn1_w_down, m_ln1_g, m_ln1_b, m_w_in, m_b_forget, m_conv_w, m_conv_b, m_rg_wa, m_rg_ba, m_rg_wx, m_rg_bx, m_lru_lambda, m_w_out, m_ln2_g, m_ln2_b, m_ffn2_w_gate, m_ffn2_w_up, m_ffn2_w_down, m_ln3_g, m_ln3_b)
    v_args = (v_ffn1_w_gate, v_ffn1_w_up, v_ffn1_w_down, v_ln1_g, v_ln1_b, v_w_in, v_b_forget, v_conv_w, v_conv_b, v_rg_wa, v_rg_ba, v_rg_wx, v_rg_bx, v_lru_lambda, v_w_out, v_ln2_g, v_ln2_b, v_ffn2_w_gate, v_ffn2_w_up, v_ffn2_w_down, v_ln3_g, v_ln3_b)
    w = dict(zip(WEIGHTS, w_args))
    m = dict(zip(WEIGHTS, m_args))
    v = dict(zip(WEIGHTS, v_args))
    me = 4 * lax.axis_index("x") + 2 * lax.axis_index("y") + lax.axis_index("c")

    sent = {n: _transport(w[n]).astype(bf16) for n in BIG}
    sent["conv_w"] = _two_d(w["conv_w"])
    small = {n: w[n] for n in ("ln1_g", "ln1_b", "ln2_g", "ln2_b", "ln3_g", "ln3_b", "b_forget", "conv_b",
                               "lru_lambda")}
    small.update({n: w[n][0] for n in ("rg_wa", "rg_ba", "rg_wx", "rg_bx")})

    sq_rows, grad_x, parts, all_packed, small_shapes = _local_step(x[0], loss_target[0], sent, small)
    loss = lax.psum(0.5 * jnp.sum(sq_rows) / D_MODEL, ("x", "y", "c"))

    total = _sum_parts(all_packed, name="sum_small_grads").reshape(-1)
    grads, off = {}, 0
    for n in PACKED:
        size = math.prod(small_shapes[n])
        grads[n] = total[off:off + size].reshape(small_shapes[n])
        off += size
    grads["conv_w"] = lax.dynamic_slice_in_dim(grads["conv_w"], me * (LRU_W // N_DEV), LRU_W // N_DEV, axis=1)

    delta, new_m, new_v = {}, {}, {}
    for n in BIG:
        g, d, m2, v2 = _adamw_big(parts[n], _transport(w[n]), _transport(m[n]), _transport(v[n]),
                                  name="adamw_" + n)
        grads[n], delta[n], new_m[n], new_v[n] = g, d, m2, v2
    small_names = [n for n in WEIGHTS if n not in BIG]
    outs = _adamw_small([(_two_d(grads[n]), _two_d(w[n]), _two_d(m[n]), _two_d(v[n])) for n in small_names],
                        name="adamw_small")
    for k, n in enumerate(small_names):
        delta[n], new_m[n], new_v[n] = outs[3 * k], outs[3 * k + 1], outs[3 * k + 2]

    def shaped(d):
        return [d[n].reshape(w[n].shape) for n in WEIGHTS]

    return (loss, grad_x[None], *shaped(grads), *shaped(delta), *shaped(new_m), *shaped(new_v))
```

```python
import functools
import math

import jax
import jax.numpy as jnp
from jax import lax
from jax.experimental import pallas as pl
from jax.experimental.pallas import tpu as pltpu

f32 = jnp.float32
bf16 = jnp.bfloat16

N_DEV = 8
D_MODEL = 1024
D_FF = 4096
FF_TILE = D_FF // N_DEV
FOX_W = 512
LRU_W = 512
HEADS = 8
HEAD_D = 64
IN_COLS = 2568
IN_SHARD = IN_COLS // N_DEV
LANES = 128
LN_EPS = 1e-5
ALPHA = 2.0 ** 0.25
ATT_SCALE = 1.0 / math.sqrt(HEAD_D)
LRU_C = 8.0
NEG_BIG = -1e30

ADAM_LR = 0.001
ADAM_B1 = 0.9
ADAM_B2 = 0.999
ADAM_EPS = 1e-08
ADAM_WD = 0.01
ADAM_STEP = 10

VMEM_LIMIT = 56 * 1024 * 1024
MESH_T = pl.DeviceIdType.MESH


def _params(sem, **kw):
    return pltpu.CompilerParams(dimension_semantics=sem, vmem_limit_bytes=VMEM_LIMIT, **kw)


def _sigmoid(x):
    return 1.0 / (1.0 + jnp.exp(-x))


def _sigmoid_tanh(x):
    return 0.5 * jnp.tanh(0.5 * x) + 0.5


def _softplus(x):
    return jnp.maximum(x, 0.0) + jnp.log(1.0 + jnp.exp(-jnp.abs(x)))


def _one_minus_exp(x):
    series = -x * (1.0 + x * (0.5 + x * (1.0 / 6 + x * (1.0 / 24 + x * (1.0 / 120 + x * (1.0 / 720))))))
    return jnp.where(x > -0.125, series, 1.0 - jnp.exp(x))


_GELU_C = math.sqrt(2.0 / math.pi)


def _gelu_and_grad(x):
    inner = _GELU_C * (x + 0.044715 * x * x * x)
    t = jnp.tanh(inner)
    g = 0.5 * x * (1.0 + t)
    dg = 0.5 * (1.0 + t) + 0.5 * x * (1.0 - t * t) * _GELU_C * (1.0 + 3 * 0.044715 * x * x)
    return g, dg


def _ln_fwd_tile(pre):
    mu = jnp.mean(pre, axis=-1, keepdims=True)
    xc = pre - mu
    var = jnp.mean(xc * xc, axis=-1, keepdims=True)
    rstd = lax.rsqrt(var + LN_EPS)
    return xc * rstd, rstd


def _ln_bwd_tile(dy, xhat, rstd, g):
    dyg = dy * g
    m1 = jnp.mean(dyg, axis=-1, keepdims=True)
    m2 = jnp.mean(dyg * xhat, axis=-1, keepdims=True)
    dpre = rstd * (dyg - m1 - xhat * m2)
    return dpre, jnp.sum(dy * xhat, axis=0, keepdims=True), jnp.sum(dy, axis=0, keepdims=True)


_NT = (((1,), (1,)), ((), ()))
_TN = (((0,), (0,)), ((), ()))


class _Exchange:
    def __init__(self, arrs, gather, chips=False):
        self.arrs, self.gather, self.n, self.chips = list(arrs), gather, len(arrs), chips

    def out_shape(self):
        return [jax.ShapeDtypeStruct(((N_DEV,) + a.shape) if self.gather else a.shape, a.dtype) for a in self.arrs]

    def scratch(self):
        n_remote = self.n * (N_DEV - 1)
        return [pltpu.SemaphoreType.DMA((n_remote,)), pltpu.SemaphoreType.DMA((n_remote,)),
                pltpu.SemaphoreType.DMA((self.n,))]

    def copies(self, ins, outs, sems):
        send_sems, recv_sems, local_sems = sems
        x, y, c = lax.axis_index("x"), lax.axis_index("y"), lax.axis_index("c")
        me = 2 * x + y if self.chips else 4 * x + 2 * y + c
        out = []
        for k in range(self.n):
            for d in (range(2, N_DEV, 2) if self.chips else range(1, N_DEV)):
                px = 1 - x if d & 4 else x
                py = 1 - y if d & 2 else y
                pc = 1 - c if d & 1 else c
                sem = k * (N_DEV - 1) + d - 1
                out.append(pltpu.make_async_remote_copy(
                    src_ref=ins[k].at[2 * px + py if self.chips else 4 * px + 2 * py + pc], dst_ref=outs[k].at[me],
                    send_sem=send_sems.at[sem], recv_sem=recv_sems.at[sem],
                    device_id=(px, py, pc), device_id_type=MESH_T))
            out.append(pltpu.make_async_copy(ins[k].at[me], outs[k].at[me], local_sems.at[k]))
        return out

    def gather_copies(self, ins, outs, sems):
        send_sems, recv_sems, local_sems = sems
        x, y, c = lax.axis_index("x"), lax.axis_index("y"), lax.axis_index("c")
        sibling = (x, y, 1 - c)
        chips = [(1 - x, y), (x, 1 - y), (1 - x, 1 - y)]
        out = []
        for k in range(self.n):
            def copy(s, block, to, src=None, k=k):
                rows = outs[k].at[4 * block[0] + 2 * block[1] + block[2]]
                sem = k * (N_DEV - 1) + s
                return pltpu.make_async_remote_copy(
                    src_ref=rows if src is None else src, dst_ref=rows, send_sem=send_sems.at[sem],
                    recv_sem=recv_sems.at[sem], device_id=to, device_id_type=MESH_T)

            first = [copy(0, (x, y, c), sibling, src=ins[k])]
            first += [copy(1 + q, (x, y, c), (*chip, c), src=ins[k]) for q, chip in enumerate(chips)]
            passed = [copy(4 + q, (*chip, c), sibling) for q, chip in enumerate(chips)]
            own = pltpu.make_async_copy(ins[k], outs[k].at[4 * x + 2 * y + c], local_sems.at[k])
            out.append((first, passed, own, copy))
        return out, sibling, chips, (x, y, c)

    def start(self, ins, outs, sems):
        if not self.gather:
            for cp in self.copies(ins, outs, sems):
                cp.start()
            return
        per_array, _, _, _ = self.gather_copies(ins, outs, sems)
        for first, _, own, _ in per_array:
            own.start()
            for cp in first:
                cp.start()

    def wait(self, ins, outs, sems):
        if not self.gather:
            for cp in self.copies(ins, outs, sems):
                cp.wait()
            return
        per_array, sibling, chips, (x, y, c) = self.gather_copies(ins, outs, sems)
        for first, passed, own, copy in per_array:
            for q, chip in enumerate(chips):
                copy(1 + q, (*chip, c), (x, y, c)).wait_recv()
                passed[q].start()
        for first, passed, own, copy in per_array:
            copy(0, sibling, (x, y, c)).wait_recv()
            for q, chip in enumerate(chips):
                copy(4 + q, (*chip, 1 - c), (x, y, c)).wait_recv()
            for cp in first + passed:
                cp.wait_send()
            own.wait()


def _hosted_call(host, body, *, name, grid, in_specs, out_specs, out_shape, scratch_shapes=(), compiler_params):
    out_specs = list(out_specs) if isinstance(out_specs, (list, tuple)) else [out_specs]
    out_shape = list(out_shape) if isinstance(out_shape, (list, tuple)) else [out_shape]
    if host is None:
        return pl.pallas_call(body, name=name, grid=grid, in_specs=in_specs, out_specs=out_specs,
                              out_shape=out_shape, scratch_shapes=list(scratch_shapes),
                              compiler_params=compiler_params)
    n_in, n_out, n_scr, k = len(in_specs), len(out_shape), len(scratch_shapes), host.n

    def wrapped(*refs):
        ins, h_in = refs[:n_in], refs[n_in:n_in + k]
        outs, h_out = refs[n_in + k:n_in + k + n_out], refs[n_in + k + n_out:n_in + 2 * k + n_out]
        scr, sems = refs[n_in + 2 * k + n_out:n_in + 2 * k + n_out + n_scr], refs[n_in + 2 * k + n_out + n_scr:]
        ids = [pl.program_id(a) for a in range(len(grid))]
        first = functools.reduce(jnp.logical_and, [i == 0 for i in ids])
        last = functools.reduce(jnp.logical_and, [i == g - 1 for i, g in zip(ids, grid)])

        @pl.when(first)
        def _():
            host.start(h_in, h_out, sems)

        body(*ins, *outs, *scr)

        @pl.when(last)
        def _():
            host.wait(h_in, h_out, sems)

    hbm = pl.BlockSpec(memory_space=pl.ANY)
    call = pl.pallas_call(
        wrapped, name=name, grid=grid, in_specs=list(in_specs) + [hbm] * k, out_specs=out_specs + [hbm] * k,
        out_shape=out_shape + host.out_shape(), scratch_shapes=list(scratch_shapes) + host.scratch(),
        compiler_params=compiler_params)
    return lambda *args: call(*args, *host.arrs)


def _exchange(arrs, *, gather, name):
    host = _Exchange(arrs, gather)

    def body(*refs):
        ins, outs, sems = refs[:host.n], refs[host.n:2 * host.n], refs[2 * host.n:]
        host.start(ins, outs, sems)
        host.wait(ins, outs, sems)

    hbm = pl.BlockSpec(memory_space=pl.ANY)
    return pl.pallas_call(
        body, name=name, in_specs=[hbm] * host.n, out_specs=[hbm] * host.n, out_shape=host.out_shape(),
        scratch_shapes=host.scratch(), compiler_params=pltpu.CompilerParams(has_side_effects=True),
    )(*arrs)


def _ffn_fwd(xhat, g_in, b_in, wg, wu, wd, *, tm, name, host=None):
    t = xhat.shape[0]
    nj = N_DEV

    def body(x_ref, g_ref, b_ref, wg_ref, wu_ref, wd_ref, xo_ref, rstd_ref, hg_ref, hu_ref, xb, acc):
        j = pl.program_id(1)

        @pl.when(j == 0)
        def _():
            xb[...] = (x_ref[...] * g_ref[...] + b_ref[...]).astype(bf16)
            acc[...] = jnp.zeros_like(acc)

        hg = jnp.dot(xb[...], wg_ref[...], preferred_element_type=f32)
        hu = jnp.dot(xb[...], wu_ref[...], preferred_element_type=f32)
        hg_ref[...] = hg.astype(bf16)
        hu_ref[...] = hu.astype(bf16)
        a = hg * _sigmoid_tanh(hg) * hu
        acc[...] += jnp.dot(a.astype(bf16), wd_ref[...], preferred_element_type=f32)

        @pl.when(j == nj - 1)
        def _():
            x = x_ref[...] * g_ref[...] + b_ref[...]
            xo, rstd = _ln_fwd_tile(ALPHA * x + 0.5 * acc[...])
            xo_ref[...] = xo
            rstd_ref[...] = rstd

    row = pl.BlockSpec((1, D_MODEL), lambda i, j: (0, 0))
    return _hosted_call(
        host, body, name=name, grid=(t // tm, nj),
        in_specs=[pl.BlockSpec((tm, D_MODEL), lambda i, j: (i, 0)), row, row,
                  pl.BlockSpec((None, D_MODEL, FF_TILE), lambda i, j: (j, 0, 0)),
                  pl.BlockSpec((None, D_MODEL, FF_TILE), lambda i, j: (j, 0, 0)),
                  pl.BlockSpec((None, FF_TILE, D_MODEL), lambda i, j: (j, 0, 0))],
        out_specs=[pl.BlockSpec((tm, D_MODEL), lambda i, j: (i, 0)),
                   pl.BlockSpec((tm, 1), lambda i, j: (i, 0)),
                   pl.BlockSpec((tm, FF_TILE), lambda i, j: (i, j)),
                   pl.BlockSpec((tm, FF_TILE), lambda i, j: (i, j))],
        out_shape=[jax.ShapeDtypeStruct((t, D_MODEL), f32), jax.ShapeDtypeStruct((t, 1), f32),
                   jax.ShapeDtypeStruct((t, D_FF), bf16), jax.ShapeDtypeStruct((t, D_FF), bf16)],
        scratch_shapes=[pltpu.VMEM((tm, D_MODEL), bf16), pltpu.VMEM((tm, D_MODEL), f32)],
        compiler_params=_params(("arbitrary", "arbitrary")),
    )(xhat, g_in, b_in, wg, wu, wd)


def _ffn_bwd(dpre, hg, hu, wg, wu, wd, ln_in, *, tm, name, host=None):
    t = dpre.shape[0]
    nj = N_DEV
    with_ln = ln_in is not None

    def body(*refs):
        if with_ln:
            (dp_ref, hg_ref, hu_ref, wg_ref, wu_ref, wd_ref, xh_ref, rs_ref, g_ref,
             dx_ref, gg_ref, gb_ref, dhg_ref, dhu_ref, a_ref, dfb, acc) = refs
        else:
            (dp_ref, hg_ref, hu_ref, wg_ref, wu_ref, wd_ref,
             dx_ref, dhg_ref, dhu_ref, a_ref, dfb, acc) = refs
        i = pl.program_id(0)
        j = pl.program_id(1)

        @pl.when(j == 0)
        def _():
            dfb[...] = (0.5 * dp_ref[...]).astype(bf16)
            acc[...] = jnp.zeros_like(acc)

        da = lax.dot_general(dfb[...], wd_ref[...], _NT, preferred_element_type=f32)
        hgv = hg_ref[...].astype(f32)
        huv = hu_ref[...].astype(f32)
        sg = _sigmoid_tanh(hgv)
        silu = hgv * sg
        a_ref[...] = (silu * huv).astype(bf16)
        dhu = (da * silu).astype(bf16)
        dhg = (da * huv * (sg * (1.0 + hgv * (1.0 - sg)))).astype(bf16)
        dhg_ref[...] = dhg
        dhu_ref[...] = dhu
        acc[...] += (lax.dot_general(dhg, wg_ref[...], _NT, preferred_element_type=f32)
                     + lax.dot_general(dhu, wu_ref[...], _NT, preferred_element_type=f32))

        @pl.when(j == nj - 1)
        def _():
            dx = ALPHA * dp_ref[...] + acc[...]
            if with_ln:
                dprev, gg, gb = _ln_bwd_tile(dx, xh_ref[...], rs_ref[...], g_ref[...])
                dx_ref[...] = dprev

                @pl.when(i == 0)
                def _():
                    gg_ref[...] = gg
                    gb_ref[...] = gb

                @pl.when(i > 0)
                def _():
                    gg_ref[...] += gg
                    gb_ref[...] += gb
            else:
                dx_ref[...] = dx

    tok = pl.BlockSpec((tm, D_MODEL), lambda i, j: (i, 0))
    row = pl.BlockSpec((1, D_MODEL), lambda i, j: (0, 0))
    hid = pl.BlockSpec((tm, FF_TILE), lambda i, j: (i, j))
    in_specs = [tok, hid, hid,
                pl.BlockSpec((None, D_MODEL, FF_TILE), lambda i, j: (j, 0, 0)),
                pl.BlockSpec((None, D_MODEL, FF_TILE), lambda i, j: (j, 0, 0)),
                pl.BlockSpec((None, FF_TILE, D_MODEL), lambda i, j: (j, 0, 0))]
    args = [dpre, hg, hu, wg, wu, wd]
    out_specs = [tok]
    out_shape = [jax.ShapeDtypeStruct((t, D_MODEL), f32)]
    if with_ln:
        in_specs += [tok, pl.BlockSpec((tm, 1), lambda i, j: (i, 0)), row]
        args += list(ln_in)
        out_specs += [row, row]
        out_shape += [jax.ShapeDtypeStruct((1, D_MODEL), f32)] * 2
    out_specs += [hid, hid, hid]
    out_shape += [jax.ShapeDtypeStruct((t, D_FF), bf16)] * 3
    return _hosted_call(
        host, body, name=name, grid=(t // tm, nj), in_specs=in_specs, out_specs=out_specs, out_shape=out_shape,
        scratch_shapes=[pltpu.VMEM((tm, D_MODEL), bf16), pltpu.VMEM((tm, D_MODEL), f32)],
        compiler_params=_params(("arbitrary", "arbitrary")),
    )(*args)


def _ffn_bwd_act(dpre, hg, hu, wd, *, tm, name, host=None):
    t = dpre.shape[0]

    def body(dp_ref, hg_ref, hu_ref, wd_ref, dhg_ref, dhu_ref, a_ref, dfb):
        @pl.when(pl.program_id(1) == 0)
        def _():
            dfb[...] = (0.5 * dp_ref[...]).astype(bf16)

        da = lax.dot_general(dfb[...], wd_ref[...], _NT, preferred_element_type=f32)
        hgv = hg_ref[...].astype(f32)
        huv = hu_ref[...].astype(f32)
        sg = _sigmoid_tanh(hgv)
        silu = hgv * sg
        a_ref[...] = (silu * huv).astype(bf16)
        dhu_ref[...] = (da * silu).astype(bf16)
        dhg_ref[...] = (da * huv * (sg * (1.0 + hgv * (1.0 - sg)))).astype(bf16)

    hid = pl.BlockSpec((tm, FF_TILE), lambda i, j: (i, j))
    return _hosted_call(
        host, body, name=name, grid=(t // tm, N_DEV),
        in_specs=[pl.BlockSpec((tm, D_MODEL), lambda i, j: (i, 0)), hid, hid,
                  pl.BlockSpec((None, FF_TILE, D_MODEL), lambda i, j: (j, 0, 0))],
        out_specs=[hid, hid, hid], out_shape=[jax.ShapeDtypeStruct((t, D_FF), bf16)] * 3,
        scratch_shapes=[pltpu.VMEM((tm, D_MODEL), bf16)],
        compiler_params=_params(("arbitrary", "arbitrary")),
    )(dpre, hg, hu, wd)


def _ffn_bwd_dx(dpre, dhg, dhu, wg, wu, *, tm, name, host=None):
    t = dpre.shape[0]
    nj = N_DEV

    def body(dp_ref, dhg_ref, dhu_ref, wg_ref, wu_ref, dx_ref, acc):
        j = pl.program_id(1)

        @pl.when(j == 0)
        def _():
            acc[...] = jnp.zeros_like(acc)

        acc[...] += (lax.dot_general(dhg_ref[...], wg_ref[...], _NT, preferred_element_type=f32)
                     + lax.dot_general(dhu_ref[...], wu_ref[...], _NT, preferred_element_type=f32))

        @pl.when(j == nj - 1)
        def _():
            dx_ref[...] = ALPHA * dp_ref[...] + acc[...]

    tok = pl.BlockSpec((tm, D_MODEL), lambda i, j: (i, 0))
    hid = pl.BlockSpec((tm, FF_TILE), lambda i, j: (i, j))
    wspec = pl.BlockSpec((None, D_MODEL, FF_TILE), lambda i, j: (j, 0, 0))
    return _hosted_call(
        host, body, name=name, grid=(t // tm, nj), in_specs=[tok, hid, hid, wspec, wspec],
        out_specs=[tok], out_shape=[jax.ShapeDtypeStruct((t, D_MODEL), f32)],
        scratch_shapes=[pltpu.VMEM((tm, D_MODEL), f32)],
        compiler_params=_params(("arbitrary", "arbitrary")),
    )(dpre, dhg, dhu, wg, wu)


def _mm(a, b, *, mode, out_dtype, tm, tn, tk, name, affine=None, a_cols=None, b_cols=None,
        b_blocked=False, out_blocked=False, out_scale=None):
    if mode == "nn":
        m_full, k_full = a.shape
        m_dim, k_dim = (m_full, a_cols[1]) if a_cols else (m_full, k_full)
    else:
        k_dim, m_full = a.shape
        m_dim = a_cols[1] if a_cols else m_full
    a_off = a_cols[0] if a_cols else 0
    if b_blocked:
        n_dim = b.shape[0] * b.shape[2]
        assert b.shape[2] == tn
    else:
        n_dim = b_cols[1] if b_cols else b.shape[1]
    b_off = b_cols[0] if b_cols else 0
    assert m_dim % tm == 0 and n_dim % tn == 0 and k_dim % tk == 0, (name, m_dim, n_dim, k_dim)
    nk = k_dim // tk

    def body(*refs):
        if affine is not None:
            a_ref, g_ref, s_ref, b_ref, o_ref, acc = refs
        else:
            a_ref, b_ref, o_ref, acc = refs
        k = pl.program_id(2)

        @pl.when(k == 0)
        def _():
            acc[...] = jnp.zeros_like(acc)

        av = a_ref[...]
        if affine is not None:
            av = av * g_ref[...] + s_ref[...]
        av = av.astype(bf16)
        bv = b_ref[...].astype(bf16)
        if mode == "nn":
            acc[...] += jnp.dot(av, bv, preferred_element_type=f32)
        else:
            acc[...] += lax.dot_general(av, bv, _TN, preferred_element_type=f32)

        @pl.when(k == nk - 1)
        def _():
            res = acc[...] if out_scale is None else acc[...] * out_scale
            o_ref[...] = res.astype(out_dtype)

    if mode == "nn":
        a_spec = pl.BlockSpec((tm, tk), lambda i, j, k: (i, k + a_off))
        aff_spec = pl.BlockSpec((1, tk), lambda i, j, k: (0, k + a_off))
    else:
        a_spec = pl.BlockSpec((tk, tm), lambda i, j, k: (k, i + a_off))
        aff_spec = pl.BlockSpec((1, tm), lambda i, j, k: (0, i + a_off))
    if b_blocked:
        b_spec = pl.BlockSpec((None, tk, tn), lambda i, j, k: (j, k, 0))
    else:
        b_spec = pl.BlockSpec((tk, tn), lambda i, j, k: (k, j + b_off))
    if out_blocked:
        o_spec = pl.BlockSpec((None, tm, tn), lambda i, j, k: (j, i, 0))
        o_shape = jax.ShapeDtypeStruct((n_dim // tn, m_dim, tn), out_dtype)
    else:
        o_spec = pl.BlockSpec((tm, tn), lambda i, j, k: (i, j))
        o_shape = jax.ShapeDtypeStruct((m_dim, n_dim), out_dtype)
    in_specs = [a_spec] + ([aff_spec, aff_spec] if affine is not None else []) + [b_spec]
    args = [a] + (list(affine) if affine is not None else []) + [b]
    return pl.pallas_call(
        body, name=name, grid=(m_dim // tm, n_dim // tn, nk), in_specs=in_specs, out_specs=o_spec,
        out_shape=o_shape, scratch_shapes=[pltpu.VMEM((tm, tn), f32)],
        compiler_params=_params(("arbitrary", "arbitrary", "arbitrary")),
    )(*args)


def _mm_tn(a, b, *, out_dtype, tm, mb, tn, nb, tk, name, affine=None, out_blocked=False, out_scale=None,
           pair=False, host=None):
    k_dim, m_dim = a.shape
    multi_b = isinstance(b, (list, tuple))
    b_list = list(b) if multi_b else [b]
    n_dim = nb * tn if multi_b else b.shape[1]
    assert m_dim % (mb * tm) == 0 and n_dim % (nb * tn) == 0 and k_dim % tk == 0, (name, m_dim, n_dim, k_dim)
    nk = k_dim // tk
    grid = (m_dim // (mb * tm), n_dim // (nb * tn), nk)
    if pair:
        assert mb * nb == 4 and grid[0] * grid[1] == 2 and out_dtype == bf16, name

    def body(*refs):
        if pair:
            refs, (acc, send_buf, recv_buf, send_sems, recv_sems) = refs[:-5], refs[-5:]
        else:
            refs, acc = refs[:-1], refs[-1]
        a_ref, o_ref = refs[0], refs[-1]
        if affine is not None:
            g_ref, s_ref = refs[1:3]
        b_refs = refs[3 if affine is not None else 1:-1]
        k = pl.program_id(2)

        @pl.when(k == 0)
        def _():
            acc[...] = jnp.zeros_like(acc)

        av = a_ref[...]
        if affine is not None:
            av = av * g_ref[...] + s_ref[...]
        av = av.astype(bf16)
        if multi_b:
            pieces = [r[...].astype(bf16) for r in b_refs]
        else:
            bv = b_refs[0][...].astype(bf16)
            pieces = [bv[:, jn * tn:(jn + 1) * tn] for jn in range(nb)]
        for im in range(mb):
            a_t = av[:, im * tm:(im + 1) * tm].T
            for jn in range(nb):
                acc[im * nb + jn] += jnp.dot(a_t, pieces[jn], preferred_element_type=f32)

        def scaled(v):
            return v if out_scale is None else v * out_scale

        @pl.when(k == nk - 1)
        def _():
            if pair:
                x, y, c = lax.axis_index("x"), lax.axis_index("y"), lax.axis_index("c")
                window = pl.program_id(0) + pl.program_id(1)
                swaps = []
                for cc in range(2):
                    send_buf[cc] = scaled(acc[2 * cc + 1 - c]).astype(bf16)
                    swaps.append(pltpu.make_async_remote_copy(
                        src_ref=send_buf.at[cc], dst_ref=recv_buf.at[window, cc],
                        send_sem=send_sems.at[2 * window + cc], recv_sem=recv_sems.at[2 * window + cc],
                        device_id=(x, y, 1 - c), device_id_type=MESH_T))
                    swaps[cc].start()
                for cc in range(2):
                    swaps[cc].wait_recv()
                    o_ref[cc] = (scaled(acc[2 * cc + c]) + recv_buf[window, cc].astype(f32)).astype(bf16)
                for cc in range(2):
                    swaps[cc].wait_send()
                return
            for im in range(mb):
                for jn in range(nb):
                    res = scaled(acc[im * nb + jn])
                    if out_blocked:
                        o_ref[jn, im * tm:(im + 1) * tm, :] = res.astype(out_dtype)
                    else:
                        o_ref[im * tm:(im + 1) * tm, jn * tn:(jn + 1) * tn] = res.astype(out_dtype)

    a_spec = pl.BlockSpec((tk, mb * tm), lambda i, j, k: (k, i))
    aff_spec = pl.BlockSpec((1, mb * tm), lambda i, j, k: (0, i))
    if multi_b:
        b_specs = [pl.BlockSpec((tk, tn), lambda i, j, k: (k, 0))] * nb
    else:
        b_specs = [pl.BlockSpec((tk, nb * tn), lambda i, j, k: (k, j))]
    scratch = [pltpu.VMEM((mb * nb, tm, tn), f32)]
    if pair:
        o_spec = pl.BlockSpec((2, tm, tn), lambda i, j, k: (i + j, 0, 0))
        o_shape = jax.ShapeDtypeStruct((4, tm, tn), out_dtype)
        scratch += [pltpu.VMEM((2, tm, tn), bf16), pltpu.VMEM((2, 2, tm, tn), bf16),
                    pltpu.SemaphoreType.DMA((4,)), pltpu.SemaphoreType.DMA((4,))]
    elif out_blocked:
        o_spec = pl.BlockSpec((nb, mb * tm, tn), lambda i, j, k: (j, i, 0))
        o_shape = jax.ShapeDtypeStruct((n_dim // tn, m_dim, tn), out_dtype)
    else:
        o_spec = pl.BlockSpec((mb * tm, nb * tn), lambda i, j, k: (i, j))
        o_shape = jax.ShapeDtypeStruct((m_dim, n_dim), out_dtype)
    in_specs = [a_spec] + ([aff_spec, aff_spec] if affine is not None else []) + b_specs
    args = [a] + (list(affine) if affine is not None else []) + b_list
    res = _hosted_call(
        host, body, name=name, grid=grid, in_specs=in_specs, out_specs=o_spec, out_shape=o_shape,
        scratch_shapes=scratch, compiler_params=_params(("arbitrary", "arbitrary", "arbitrary")),
    )(*args)
    return res[0] if host is None else res


def _in_proj(xhat, g, b, w_in, *, tm, name):
    t = xhat.shape[0]
    n_qkv, n_l = 3 * FOX_W, 2 * LRU_W

    def body(x_ref, g_ref, b_ref, w_ref, qkv_ref, zl_ref, zfg_ref):
        xb = (x_ref[...] * g_ref[...] + b_ref[...]).astype(bf16)
        qkv_ref[...] = jnp.dot(xb, w_ref[:, :n_qkv], preferred_element_type=f32).astype(bf16)
        zl_ref[...] = jnp.dot(xb, w_ref[:, n_qkv:n_qkv + n_l], preferred_element_type=f32)
        zfg_ref[...] = jnp.dot(xb, w_ref[:, n_qkv + n_l:], preferred_element_type=f32)

    row = pl.BlockSpec((1, D_MODEL), lambda i: (0, 0))
    return pl.pallas_call(
        body, name=name, grid=(t // tm,),
        in_specs=[pl.BlockSpec((tm, D_MODEL), lambda i: (i, 0)), row, row,
                  pl.BlockSpec(w_in.shape, lambda i: (0, 0))],
        out_specs=[pl.BlockSpec((tm, n_qkv), lambda i: (i, 0)), pl.BlockSpec((tm, n_l), lambda i: (i, 0)),
                   pl.BlockSpec((tm, LANES), lambda i: (i, 0))],
        out_shape=[jax.ShapeDtypeStruct((t, n_qkv), bf16), jax.ShapeDtypeStruct((t, n_l), f32),
                   jax.ShapeDtypeStruct((t, LANES), f32)],
        compiler_params=_params(("arbitrary",)),
    )(xhat, g, b, w_in)


def _mmln(pairs, *, tm, name, resid=None, resid_scale=1.0, epi=None, ln=None, n_out=D_MODEL):
    t = pairs[0][0].shape[0]
    n_pairs = len(pairs)
    n_resid = 0 if resid is None else len(resid) - 1

    def body(*refs):
        pos = 0
        val = None
        for p in range(n_pairs):
            a_ref, b_ref = refs[pos], refs[pos + 1]
            pos += 2
            av = a_ref[...].astype(bf16)
            bv = b_ref[...].astype(bf16)
            if pairs[p][6] == "nn":
                term = jnp.dot(av, bv, preferred_element_type=f32)
            else:
                term = lax.dot_general(av, bv, _NT, preferred_element_type=f32)
            val = term if val is None else val + term
        if resid is not None:
            if resid[0] == "plain":
                r = refs[pos][...]
            else:
                r = refs[pos][...] * refs[pos + 1][...] + refs[pos + 2][...]
            pos += n_resid
            val = val + resid_scale * r
        if epi is None:
            o_ref = refs[pos]
            o_ref[...] = val.astype(o_ref.dtype)
        elif epi == "ln_fwd":
            xo, rstd = _ln_fwd_tile(val)
            refs[pos][...] = xo
            refs[pos + 1][...] = rstd
        else:
            xh_ref, rs_ref, g_ref, dx_ref, gg_ref, gb_ref = refs[pos:pos + 6]
            dprev, gg, gb = _ln_bwd_tile(val, xh_ref[...], rs_ref[...], g_ref[...])
            dx_ref[...] = dprev
            i = pl.program_id(0)

            @pl.when(i == 0)
            def _():
                gg_ref[...] = gg
                gb_ref[...] = gb

            @pl.when(i > 0)
            def _():
                gg_ref[...] += gg
                gb_ref[...] += gb

    in_specs, args = [], []
    for (a, acb, aw, b, bcb, bw, mode) in pairs:
        in_specs.append(pl.BlockSpec((tm, aw), lambda i, acb=acb: (i, acb)))
        args.append(a)
        if mode == "nn":
            in_specs.append(pl.BlockSpec((aw, n_out), lambda i, bcb=bcb: (bcb, 0)))
        else:
            in_specs.append(pl.BlockSpec((n_out, bw), lambda i, bcb=bcb: (0, bcb)))
        args.append(b)
    tok = pl.BlockSpec((tm, n_out), lambda i: (i, 0))
    row = pl.BlockSpec((1, n_out), lambda i: (0, 0))
    col = pl.BlockSpec((tm, 1), lambda i: (i, 0))
    if resid is not None:
        in_specs += [tok] if resid[0] == "plain" else [tok, row, row]
        args += list(resid[1:])
    if epi is None:
        out_specs, out_shape = tok, jax.ShapeDtypeStruct((t, n_out), f32)
    elif epi == "ln_fwd":
        out_specs = [tok, col]
        out_shape = [jax.ShapeDtypeStruct((t, n_out), f32), jax.ShapeDtypeStruct((t, 1), f32)]
    else:
        in_specs += [tok, col, row]
        args += list(ln)
        out_specs = [tok, row, row]
        out_shape = [jax.ShapeDtypeStruct((t, n_out), f32)] + [jax.ShapeDtypeStruct((1, n_out), f32)] * 2
    return pl.pallas_call(
        body, name=name, grid=(t // tm,), in_specs=in_specs, out_specs=out_specs, out_shape=out_shape,
        compiler_params=_params(("arbitrary",)),
    )(*args)


def _loss_bwd(xhat, rstd, g, b, target, *, tm, name):
    t = xhat.shape[0]

    def body(xh_ref, rs_ref, g_ref, b_ref, tg_ref, dx_ref, sq_ref, gg_ref, gb_ref):
        i = pl.program_id(0)
        xh = xh_ref[...]
        diff = xh * g_ref[...] + b_ref[...] - tg_ref[...]
        sq = jnp.sum(diff * diff, axis=0, keepdims=True)
        dprev, gg, gb = _ln_bwd_tile(diff * (1.0 / D_MODEL), xh, rs_ref[...], g_ref[...])
        dx_ref[...] = dprev

        @pl.when(i == 0)
        def _():
            sq_ref[...] = sq
            gg_ref[...] = gg
            gb_ref[...] = gb

        @pl.when(i > 0)
        def _():
            sq_ref[...] += sq
            gg_ref[...] += gg
            gb_ref[...] += gb

    tok = pl.BlockSpec((tm, D_MODEL), lambda i: (i, 0))
    row = pl.BlockSpec((1, D_MODEL), lambda i: (0, 0))
    return pl.pallas_call(
        body, name=name, grid=(t // tm,),
        in_specs=[tok, pl.BlockSpec((tm, 1), lambda i: (i, 0)), row, row, tok],
        out_specs=[tok, row, row, row],
        out_shape=[jax.ShapeDtypeStruct((t, D_MODEL), f32)] + [jax.ShapeDtypeStruct((1, D_MODEL), f32)] * 3,
        compiler_params=_params(("arbitrary",)),
    )(xhat, rstd, g, b, target)


CUM_TILE = 256


def _tri(n, lower):
    r = lax.broadcasted_iota(jnp.int32, (n, n), 0)
    c = lax.broadcasted_iota(jnp.int32, (n, n), 1)
    return jnp.where((r >= c) if lower else (r <= c), 1.0, 0.0).astype(f32)


def _cum_fwd(zfg, bfg, *, name):
    t = zfg.shape[0]

    def body(z_ref, b_ref, o_ref, carry):
        @pl.when(pl.program_id(0) == 0)
        def _():
            carry[...] = jnp.zeros_like(carry)

        ls = -_softplus(-(z_ref[...] + b_ref[...]))
        c = jnp.dot(_tri(CUM_TILE, True), ls, preferred_element_type=f32,
                    precision=lax.Precision.HIGHEST) + carry[...]
        o_ref[...] = c
        carry[...] = c[CUM_TILE - 1:CUM_TILE, :]

    blk = pl.BlockSpec((CUM_TILE, LANES), lambda i: (i, 0))
    return pl.pallas_call(
        body, name=name, grid=(t // CUM_TILE,),
        in_specs=[blk, pl.BlockSpec((1, LANES), lambda i: (0, 0))], out_specs=blk,
        out_shape=jax.ShapeDtypeStruct((t, LANES), f32), scratch_shapes=[pltpu.VMEM((1, LANES), f32)],
        compiler_params=_params(("arbitrary",)),
    )(zfg, bfg)


def _cum_bwd(dcum_q, dcum_k, zfg, bfg, *, name):
    t = zfg.shape[0]
    n = t // CUM_TILE

    def body(d_ref, d2_ref, z_ref, b_ref, o_ref, s_ref, carry):
        i = pl.program_id(0)

        @pl.when(i == 0)
        def _():
            carry[...] = jnp.zeros_like(carry)

        dls = jnp.dot(_tri(CUM_TILE, False), d_ref[...] + d2_ref[...], preferred_element_type=f32,
                      precision=lax.Precision.HIGHEST) + carry[...]
        carry[...] = dls[0:1, :]
        lane = lax.broadcasted_iota(jnp.int32, (CUM_TILE, LANES), 1)
        dfg = jnp.where(lane < HEADS, dls * _sigmoid(-(z_ref[...] + b_ref[...])), 0.0)
        o_ref[...] = dfg
        tot = jnp.sum(dfg, axis=0, keepdims=True)

        @pl.when(i == 0)
        def _():
            s_ref[...] = tot

        @pl.when(i > 0)
        def _():
            s_ref[...] += tot

    blk = pl.BlockSpec((CUM_TILE, LANES), lambda i: (n - 1 - i, 0))
    row = pl.BlockSpec((1, LANES), lambda i: (0, 0))
    return pl.pallas_call(
        body, name=name, grid=(n,), in_specs=[blk, blk, blk, row], out_specs=[blk, row],
        out_shape=[jax.ShapeDtypeStruct((t, LANES), f32), jax.ShapeDtypeStruct((1, LANES), f32)],
        scratch_shapes=[pltpu.VMEM((1, LANES), f32)],
        compiler_params=_params(("arbitrary",)),
    )(dcum_q, dcum_k, zfg, bfg)


ATT_TILE = 512


ATT_ROWS = 32


def _causal_rows(r, transposed):
    rr = lax.broadcasted_iota(jnp.int32, (ATT_ROWS, ATT_TILE), 0) + r * ATT_ROWS
    cc = lax.broadcasted_iota(jnp.int32, (ATT_ROWS, ATT_TILE), 1)
    return (cc >= rr) if transposed else (rr >= cc)


def _causal(i, j, transposed):
    r = lax.broadcasted_iota(jnp.int32, (ATT_TILE, ATT_TILE), 0)
    c = lax.broadcasted_iota(jnp.int32, (ATT_TILE, ATT_TILE), 1)
    if transposed:
        return (c + i * ATT_TILE) >= (r + j * ATT_TILE)
    return (r + i * ATT_TILE) >= (c + j * ATT_TILE)


def _attn_fwd(qkv, cum, cum_t, *, name, host=None):
    t = qkv.shape[0]
    n = t // ATT_TILE
    tq = ATT_TILE

    def body(q_ref, k_ref, v_ref, cq_ref, ck_ref, o_ref, lse_ref, acc, m_s, l_s, c_s, s_s, p_s):
        i = pl.program_id(0)
        j = pl.program_id(1)

        @pl.when(j == 0)
        def _():
            acc[...] = jnp.zeros_like(acc)
            m_s[...] = jnp.full_like(m_s, NEG_BIG)
            l_s[...] = jnp.zeros_like(l_s)

        def block(masked):
            for h in range(HEADS):
                hs = slice(HEAD_D * h, HEAD_D * (h + 1))
                s_s[...] = lax.dot_general(q_ref[:, hs] * ATT_SCALE, k_ref[:, hs], _NT, preferred_element_type=f32)
                ck = ck_ref[h:h + 1, :]

                def rows_chunk(r, carry):
                    rows = pl.ds(pl.multiple_of(r * ATT_ROWS, ATT_ROWS), ATT_ROWS)
                    s = s_s[rows, :] + (cq_ref[rows, h:h + 1] - ck)
                    if masked:
                        s = jnp.where(_causal_rows(r, False), s, NEG_BIG)
                    m_old = m_s[rows, h:h + 1]
                    m_new = jnp.maximum(m_old, jnp.max(s, axis=-1, keepdims=True))
                    corr = jnp.exp(m_old - m_new)
                    p = jnp.exp(s - m_new)
                    l_s[rows, h:h + 1] = corr * l_s[rows, h:h + 1] + jnp.sum(p, axis=-1, keepdims=True)
                    m_s[rows, h:h + 1] = m_new
                    c_s[rows, h:h + 1] = corr
                    p_s[rows, :] = p.astype(bf16)
                    return carry

                lax.fori_loop(0, tq // ATT_ROWS, rows_chunk, 0, unroll=4)
                acc[:, hs] = c_s[:, h:h + 1] * acc[:, hs] + jnp.dot(p_s[...], v_ref[:, hs],
                                                                   preferred_element_type=f32)

        @pl.when(j < i)
        def _():
            block(False)

        @pl.when(j == i)
        def _():
            block(True)
            lse_ref[...] = jnp.zeros_like(lse_ref)
            for h in range(HEADS):
                hs = slice(HEAD_D * h, HEAD_D * (h + 1))
                l = l_s[:, h:h + 1]
                o_ref[:, hs] = acc[:, hs] / l
                lse_ref[:, h:h + 1] = m_s[:, h:h + 1] + jnp.log(l)

    return _hosted_call(
        host, body, name=name, grid=(n, n),
        in_specs=[pl.BlockSpec((tq, FOX_W), lambda i, j: (i, 0)),
                  pl.BlockSpec((tq, FOX_W), lambda i, j: (jnp.minimum(i, j), 1)),
                  pl.BlockSpec((tq, FOX_W), lambda i, j: (jnp.minimum(i, j), 2)),
                  pl.BlockSpec((tq, LANES), lambda i, j: (i, 0)),
                  pl.BlockSpec((HEADS, tq), lambda i, j: (0, jnp.minimum(i, j)))],
        out_specs=[pl.BlockSpec((tq, FOX_W), lambda i, j: (i, 0)), pl.BlockSpec((tq, LANES), lambda i, j: (i, 0))],
        out_shape=[jax.ShapeDtypeStruct((t, FOX_W), f32), jax.ShapeDtypeStruct((t, LANES), f32)],
        scratch_shapes=[pltpu.VMEM((tq, FOX_W), f32), pltpu.VMEM((tq, LANES), f32), pltpu.VMEM((tq, LANES), f32),
                        pltpu.VMEM((tq, LANES), f32), pltpu.VMEM((tq, tq), f32), pltpu.VMEM((tq, tq), bf16)],
        compiler_params=_params(("arbitrary", "arbitrary")),
    )(qkv, qkv, qkv, cum, cum_t)


def _attn_delta(dmix, o, *, tm, name):
    t = o.shape[0]

    def body(do_ref, o_ref, d_ref):
        r = lax.broadcasted_iota(jnp.int32, (FOX_W, LANES), 0)
        c = lax.broadcasted_iota(jnp.int32, (FOX_W, LANES), 1)
        pick = jnp.where(r // HEAD_D == c, 1.0, 0.0).astype(f32)
        d_ref[...] = jnp.dot(do_ref[...] * o_ref[...], pick, preferred_element_type=f32,
                             precision=lax.Precision.HIGHEST)

    blk = pl.BlockSpec((tm, FOX_W), lambda i: (i, 0))
    return pl.pallas_call(
        body, name=name, grid=(t // tm,), in_specs=[blk, blk],
        out_specs=pl.BlockSpec((tm, LANES), lambda i: (i, 0)),
        out_shape=jax.ShapeDtypeStruct((t, LANES), f32), compiler_params=_params(("arbitrary",)),
    )(dmix, o)


def _attn_dq(qkv, dmix, cum, cum_t, lse, delta, *, name, host=None):
    t = qkv.shape[0]
    n = t // ATT_TILE
    tq = ATT_TILE

    def body(q_ref, k_ref, v_ref, do_ref, cq_ref, ck_ref, lse_ref, dl_ref, dq_ref, dc_ref, acc, dc_acc):
        i = pl.program_id(0)
        j = pl.program_id(1)

        @pl.when(j == 0)
        def _():
            acc[...] = jnp.zeros_like(acc)
            dc_acc[...] = jnp.zeros_like(dc_acc)

        def block(masked):
            mask = _causal(i, j, False) if masked else None
            for h in range(HEADS):
                hs = slice(HEAD_D * h, HEAD_D * (h + 1))
                kh = k_ref[:, hs]
                s = lax.dot_general(q_ref[:, hs] * ATT_SCALE, kh, _NT, preferred_element_type=f32)
                s = s + cq_ref[:, h:h + 1] - ck_ref[h:h + 1, :]
                if masked:
                    s = jnp.where(mask, s, NEG_BIG)
                p = jnp.exp(s - lse_ref[:, h:h + 1])
                dp = lax.dot_general(do_ref[:, hs].astype(bf16), v_ref[:, hs], _NT, preferred_element_type=f32)
                ds = p * (dp - dl_ref[:, h:h + 1])
                acc[:, hs] += jnp.dot(ds.astype(bf16), kh, preferred_element_type=f32)
                dc_acc[:, h:h + 1] += jnp.sum(ds, axis=-1, keepdims=True)

        @pl.when(j < i)
        def _():
            block(False)

        @pl.when(j == i)
        def _():
            block(True)
            dq_ref[...] = (acc[...] * ATT_SCALE).astype(bf16)
            dc_ref[...] = dc_acc[...]

    col = pl.BlockSpec((tq, LANES), lambda i, j: (i, 0))
    return _hosted_call(
        host, body, name=name, grid=(n, n),
        in_specs=[pl.BlockSpec((tq, FOX_W), lambda i, j: (i, 0)),
                  pl.BlockSpec((tq, FOX_W), lambda i, j: (jnp.minimum(i, j), 1)),
                  pl.BlockSpec((tq, FOX_W), lambda i, j: (jnp.minimum(i, j), 2)),
                  pl.BlockSpec((tq, FOX_W), lambda i, j: (i, 0)),
                  col, pl.BlockSpec((HEADS, tq), lambda i, j: (0, jnp.minimum(i, j))), col, col],
        out_specs=[pl.BlockSpec((tq, FOX_W), lambda i, j: (i, 0)), col],
        out_shape=[jax.ShapeDtypeStruct((t, FOX_W), bf16), jax.ShapeDtypeStruct((t, LANES), f32)],
        scratch_shapes=[pltpu.VMEM((tq, FOX_W), f32), pltpu.VMEM((tq, LANES), f32)],
        compiler_params=_params(("arbitrary", "arbitrary")),
    )(qkv, qkv, qkv, dmix, cum, cum_t, lse, delta)


def _attn_dkv(qkv, dmix, cum, cum_t, lse_t, delta_t, *, name):
    t = qkv.shape[0]
    n = t // ATT_TILE
    tk = ATT_TILE

    def body(q_ref, k_ref, v_ref, do_ref, cq_ref, ck_ref, lse_ref, dl_ref, dk_ref, dv_ref, dc_ref, dk_acc, dv_acc, dc_acc):
        j = pl.program_id(0)
        i = pl.program_id(1)

        @pl.when(i == 0)
        def _():
            dk_acc[...] = jnp.zeros_like(dk_acc)
            dv_acc[...] = jnp.zeros_like(dv_acc)
            dc_acc[...] = jnp.zeros_like(dc_acc)

        def block(masked):
            mask = _causal(i, j, True) if masked else None
            for h in range(HEADS):
                hs = slice(HEAD_D * h, HEAD_D * (h + 1))
                qh = q_ref[:, hs]
                doh = do_ref[:, hs].astype(bf16)
                s_t = lax.dot_general(k_ref[:, hs] * ATT_SCALE, qh, _NT, preferred_element_type=f32)
                s_t = s_t + cq_ref[h:h + 1, :] - ck_ref[:, h:h + 1]
                if masked:
                    s_t = jnp.where(mask, s_t, NEG_BIG)
                p_t = jnp.exp(s_t - lse_ref[h:h + 1, :])
                dv_acc[:, hs] += jnp.dot(p_t.astype(bf16), doh, preferred_element_type=f32)
                dp_t = lax.dot_general(v_ref[:, hs], doh, _NT, preferred_element_type=f32)
                ds_t = p_t * (dp_t - dl_ref[h:h + 1, :])
                dk_acc[:, hs] += jnp.dot(ds_t.astype(bf16), qh, preferred_element_type=f32)
                dc_acc[:, h:h + 1] -= jnp.sum(ds_t, axis=-1, keepdims=True)

        @pl.when(i > j)
        def _():
            block(False)

        @pl.when(i == j)
        def _():
            block(True)

        @pl.when(i == n - 1)
        def _():
            dk_ref[...] = (dk_acc[...] * ATT_SCALE).astype(bf16)
            dv_ref[...] = dv_acc[...].astype(bf16)
            dc_ref[...] = dc_acc[...]

    rowq = pl.BlockSpec((HEADS, tk), lambda j, i: (0, jnp.maximum(i, j)))
    return pl.pallas_call(
        body, name=name, grid=(n, n),
        in_specs=[pl.BlockSpec((tk, FOX_W), lambda j, i: (jnp.maximum(i, j), 0)),
                  pl.BlockSpec((tk, FOX_W), lambda j, i: (j, 1)),
                  pl.BlockSpec((tk, FOX_W), lambda j, i: (j, 2)),
                  pl.BlockSpec((tk, FOX_W), lambda j, i: (jnp.maximum(i, j), 0)),
                  rowq, pl.BlockSpec((tk, LANES), lambda j, i: (j, 0)), rowq, rowq],
        out_specs=[pl.BlockSpec((tk, FOX_W), lambda j, i: (j, 0)), pl.BlockSpec((tk, FOX_W), lambda j, i: (j, 0)),
                   pl.BlockSpec((tk, LANES), lambda j, i: (j, 0))],
        out_shape=[jax.ShapeDtypeStruct((t, FOX_W), bf16), jax.ShapeDtypeStruct((t, FOX_W), bf16),
                   jax.ShapeDtypeStruct((t, LANES), f32)],
        scratch_shapes=[pltpu.VMEM((tk, FOX_W), f32), pltpu.VMEM((tk, FOX_W), f32), pltpu.VMEM((tk, LANES), f32)],
        compiler_params=_params(("arbitrary", "arbitrary")),
    )(qkv, qkv, qkv, dmix, cum_t, cum, lse_t, delta_t)


ATT_W = HEADS * LANES


def _data_lane(h):
    return HEAD_D * (h % 2)


def _extra_lane(h):
    return HEAD_D - _data_lane(h)


def _split3(x):
    hi = x.astype(bf16)
    rest = x - hi.astype(f32)
    mid = rest.astype(bf16)
    lo = (rest - mid.astype(f32)).astype(bf16)
    return hi, mid, lo


def _augment(pair, h, first, second, fill=0.0):
    rows = pair.shape[0]
    lane = lax.broadcasted_iota(jnp.int32, (rows, LANES), 1)
    base = _extra_lane(h)
    own = (lane < HEAD_D) if h % 2 == 0 else (lane >= HEAD_D)
    out = jnp.where(own, pair, jnp.full((rows, LANES), fill, bf16))
    for off, src in ((0, first), (3, second)):
        for q in range(3):
            val = src[q] if isinstance(src, tuple) else jnp.full((rows, 1), src, bf16)
            out = jnp.where(lane == base + off + q, val, out)
    return out


def _attn_prep_fwd(qkv, cum, *, tm, name):
    t = qkv.shape[0]

    def body(q_ref, k_ref, v_ref, c_ref, qa_ref, ka_ref, va_ref):
        for h in range(HEADS):
            pair = slice(LANES * (h // 2), LANES * (h // 2 + 1))
            hs = slice(LANES * h, LANES * (h + 1))
            c3 = _split3(c_ref[:, h:h + 1])
            qa_ref[:, hs] = _augment(q_ref[:, pair] * ATT_SCALE, h, c3, 1.0)
            ka_ref[:, hs] = _augment(k_ref[:, pair], h, 1.0, tuple(-p for p in c3))
            va_ref[:, hs] = _augment(v_ref[:, pair], h, 1.0, 1.0, fill=1.0)

    wide = pl.BlockSpec((tm, ATT_W), lambda i: (i, 0))
    out = jax.ShapeDtypeStruct((t, ATT_W), bf16)
    return pl.pallas_call(
        body, name=name, grid=(t // tm,),
        in_specs=[pl.BlockSpec((tm, FOX_W), lambda i: (i, 0)), pl.BlockSpec((tm, FOX_W), lambda i: (i, 1)),
                  pl.BlockSpec((tm, FOX_W), lambda i: (i, 2)), pl.BlockSpec((tm, LANES), lambda i: (i, 0))],
        out_specs=[wide] * 3, out_shape=[out] * 3, compiler_params=_params(("arbitrary",)),
    )(qkv, qkv, qkv, cum)


def _attn_prep_bwd(qkv, cum, lse, dmix, o, *, tm, name):
    t = qkv.shape[0]

    def body(q_ref, c_ref, l_ref, do_ref, o_ref, qa_ref, da_ref):
        for h in range(HEADS):
            pair = slice(LANES * (h // 2), LANES * (h // 2 + 1))
            src = slice(HEAD_D * h, HEAD_D * (h + 1))
            hs = slice(LANES * h, LANES * (h + 1))
            delta = jnp.sum(do_ref[:, src] * o_ref[:, src], axis=-1, keepdims=True)
            qa_ref[:, hs] = _augment(q_ref[:, pair] * ATT_SCALE, h,
                                     _split3(c_ref[:, h:h + 1] - l_ref[:, h:h + 1]), 1.0)
            da_ref[:, hs] = _augment(do_ref[:, pair].astype(bf16), h, tuple(-p for p in _split3(delta)), 0.0)

    wide = pl.BlockSpec((tm, ATT_W), lambda i: (i, 0))
    half = pl.BlockSpec((tm, FOX_W), lambda i: (i, 0))
    col = pl.BlockSpec((tm, LANES), lambda i: (i, 0))
    out = jax.ShapeDtypeStruct((t, ATT_W), bf16)
    return pl.pallas_call(
        body, name=name, grid=(t // tm,), in_specs=[half, col, col, half, half],
        out_specs=[wide] * 2, out_shape=[out] * 2, compiler_params=_params(("arbitrary",)),
    )(qkv, cum, lse, dmix, o)


def _attn_fwd2(q_aug, k_aug, v_aug, *, name, host=None):
    t = q_aug.shape[0]
    n = t // ATT_TILE
    tq = ATT_TILE

    def body(q_ref, k_ref, v_ref, o_ref, lse_ref, acc, m_s):
        i = pl.program_id(0)
        j = pl.program_id(1)

        @pl.when(j == 0)
        def _():
            acc[...] = jnp.zeros_like(acc)
            m_s[...] = jnp.full_like(m_s, NEG_BIG)

        def block(masked):
            mask = _causal(i, j, False) if masked else None
            for h in range(HEADS):
                hs = slice(LANES * h, LANES * (h + 1))
                s = lax.dot_general(q_ref[:, hs], k_ref[:, hs], _NT, preferred_element_type=f32)
                if masked:
                    s = jnp.where(mask, s, NEG_BIG)
                blocks = [s[:, LANES * b:LANES * (b + 1)] for b in range(tq // LANES)]
                m_old = m_s[h]
                m_new = jnp.maximum(m_old, jnp.broadcast_to(
                    jnp.max(functools.reduce(jnp.maximum, blocks), axis=-1, keepdims=True), (tq, LANES)))
                p = jnp.concatenate([jnp.exp(b - m_new) for b in blocks], axis=1).astype(bf16)
                acc[h] = jnp.exp(m_old - m_new) * acc[h] + jnp.dot(p, v_ref[:, hs], preferred_element_type=f32)
                m_s[h] = m_new

        @pl.when(j < i)
        def _():
            block(False)

        @pl.when(j == i)
        def _():
            block(True)
            lse_ref[...] = jnp.zeros_like(lse_ref)
            for h in range(HEADS):
                a = acc[h]
                l = a[:, _extra_lane(h):_extra_lane(h) + 1]
                o_ref[:, HEAD_D * h:HEAD_D * (h + 1)] = a[:, _data_lane(h):_data_lane(h) + HEAD_D] / l
                lse_ref[:, h:h + 1] = m_s[h][:, 0:1] + jnp.log(l)

    kv = pl.BlockSpec((tq, ATT_W), lambda i, j: (jnp.minimum(i, j), 0))
    return _hosted_call(
        host, body, name=name, grid=(n, n),
        in_specs=[pl.BlockSpec((tq, ATT_W), lambda i, j: (i, 0)), kv, kv],
        out_specs=[pl.BlockSpec((tq, FOX_W), lambda i, j: (i, 0)), pl.BlockSpec((tq, LANES), lambda i, j: (i, 0))],
        out_shape=[jax.ShapeDtypeStruct((t, FOX_W), f32), jax.ShapeDtypeStruct((t, LANES), f32)],
        scratch_shapes=[pltpu.VMEM((HEADS, tq, LANES), f32), pltpu.VMEM((HEADS, tq, LANES), f32)],
        compiler_params=_params(("arbitrary", "arbitrary")),
    )(q_aug, k_aug, v_aug)


def _attn_dq2(qb_aug, k_aug, v_aug, do_aug, *, name, host=None):
    t = qb_aug.shape[0]
    n = t // ATT_TILE
    tq = ATT_TILE

    def body(q_ref, k_ref, v_ref, do_ref, dq_ref, dc_ref, acc):
        i = pl.program_id(0)
        j = pl.program_id(1)

        @pl.when(j == 0)
        def _():
            acc[...] = jnp.zeros_like(acc)

        def block(masked):
            mask = _causal(i, j, False) if masked else None
            for h in range(HEADS):
                hs = slice(LANES * h, LANES * (h + 1))
                kh = k_ref[:, hs]
                s = lax.dot_general(q_ref[:, hs], kh, _NT, preferred_element_type=f32)
                if masked:
                    s = jnp.where(mask, s, NEG_BIG)
                dp = lax.dot_general(do_ref[:, hs], v_ref[:, hs], _NT, preferred_element_type=f32)
                ds = (jnp.exp(s) * dp).astype(bf16)
                acc[h] += jnp.dot(ds, kh, preferred_element_type=f32)

        @pl.when(j < i)
        def _():
            block(False)

        @pl.when(j == i)
        def _():
            block(True)
            dc_ref[...] = jnp.zeros_like(dc_ref)
            for h in range(HEADS):
                a = acc[h]
                dq_ref[:, HEAD_D * h:HEAD_D * (h + 1)] = (
                    a[:, _data_lane(h):_data_lane(h) + HEAD_D] * ATT_SCALE).astype(bf16)
                dc_ref[:, h:h + 1] = a[:, _extra_lane(h):_extra_lane(h) + 1]

    own = pl.BlockSpec((tq, ATT_W), lambda i, j: (i, 0))
    kv = pl.BlockSpec((tq, ATT_W), lambda i, j: (jnp.minimum(i, j), 0))
    return _hosted_call(
        host, body, name=name, grid=(n, n), in_specs=[own, kv, kv, own],
        out_specs=[pl.BlockSpec((tq, FOX_W), lambda i, j: (i, 0)), pl.BlockSpec((tq, LANES), lambda i, j: (i, 0))],
        out_shape=[jax.ShapeDtypeStruct((t, FOX_W), bf16), jax.ShapeDtypeStruct((t, LANES), f32)],
        scratch_shapes=[pltpu.VMEM((HEADS, tq, LANES), f32)],
        compiler_params=_params(("arbitrary", "arbitrary")),
    )(qb_aug, k_aug, v_aug, do_aug)


def _attn_dkv2(qb_aug, k_aug, v_aug, do_aug, *, name, host=None):
    t = qb_aug.shape[0]
    n = t // ATT_TILE
    tk = ATT_TILE

    def body(q_ref, k_ref, v_ref, do_ref, dk_ref, dv_ref, dc_ref, dk_acc, dv_acc):
        j = pl.program_id(0)
        i = pl.program_id(1)

        @pl.when(i == 0)
        def _():
            dk_acc[...] = jnp.zeros_like(dk_acc)
            dv_acc[...] = jnp.zeros_like(dv_acc)

        def block(masked):
            mask = _causal(i, j, True) if masked else None
            for h in range(HEADS):
                hs = slice(LANES * h, LANES * (h + 1))
                qh = q_ref[:, hs]
                doh = do_ref[:, hs]
                s_t = lax.dot_general(k_ref[:, hs], qh, _NT, preferred_element_type=f32)
                if masked:
                    s_t = jnp.where(mask, s_t, NEG_BIG)
                p_t = jnp.exp(s_t)
                dv_acc[h] += jnp.dot(p_t.astype(bf16), doh, preferred_element_type=f32)
                dp_t = lax.dot_general(v_ref[:, hs], doh, _NT, preferred_element_type=f32)
                dk_acc[h] += jnp.dot((p_t * dp_t).astype(bf16), qh, preferred_element_type=f32)

        @pl.when(i > j)
        def _():
            block(False)

        @pl.when(i == j)
        def _():
            block(True)

        @pl.when(i == n - 1)
        def _():
            dc_ref[...] = jnp.zeros_like(dc_ref)
            for h in range(HEADS):
                a = dk_acc[h]
                cols = slice(_data_lane(h), _data_lane(h) + HEAD_D)
                dk_ref[:, HEAD_D * h:HEAD_D * (h + 1)] = a[:, cols].astype(bf16)
                dv_ref[:, HEAD_D * h:HEAD_D * (h + 1)] = dv_acc[h][:, cols].astype(bf16)
                dc_ref[:, h:h + 1] = -a[:, _extra_lane(h) + 3:_extra_lane(h) + 4]

    own = pl.BlockSpec((tk, ATT_W), lambda j, i: (j, 0))
    qs = pl.BlockSpec((tk, ATT_W), lambda j, i: (jnp.maximum(i, j), 0))
    half = pl.BlockSpec((tk, FOX_W), lambda j, i: (j, 0))
    return _hosted_call(
        host, body, name=name, grid=(n, n), in_specs=[qs, own, own, qs],
        out_specs=[half, half, pl.BlockSpec((tk, LANES), lambda j, i: (j, 0))],
        out_shape=[jax.ShapeDtypeStruct((t, FOX_W), bf16), jax.ShapeDtypeStruct((t, FOX_W), bf16),
                   jax.ShapeDtypeStruct((t, LANES), f32)],
        scratch_shapes=[pltpu.VMEM((HEADS, tk, LANES), f32), pltpu.VMEM((HEADS, tk, LANES), f32)],
        compiler_params=_params(("arbitrary", "arbitrary")),
    )(qb_aug, k_aug, v_aug, do_aug)


LRU_CHUNK = 64
SUB = 8


def _row_ids(n):
    return lax.broadcasted_iota(jnp.int32, (n, LANES), 0)


def _shift_rows_down(ext, s):
    return pltpu.roll(ext, s, axis=0)[SUB:, :]


def _shift_rows_up(ext, s, n):
    return pltpu.roll(ext, ext.shape[0] - s, axis=0)[:n, :]


def _lru_gates(u, wa_ref, ba_ref, wx_ref, bx_ref, sp):
    ub = u.astype(bf16)
    r = _sigmoid(jnp.dot(ub, wa_ref[...], preferred_element_type=f32) + ba_ref[...])
    gi = _sigmoid(jnp.dot(ub, wx_ref[...], preferred_element_type=f32) + bx_ref[...])
    log_a = -LRU_C * r * sp
    a = jnp.exp(log_a)
    s = jnp.sqrt(_one_minus_exp(2.0 * log_a))
    return r, gi, a, s


def _conv_window(lx_ref, r0, ci):
    cur = lx_ref[pl.ds(r0, LRU_CHUNK), :]
    p0 = pl.multiple_of(jnp.maximum(r0 - SUB, 0), SUB)
    prev = jnp.where(ci > 0, lx_ref[pl.ds(p0, SUB), :], 0.0)
    return cur, jnp.concatenate([prev, cur], axis=0)


def _lru_fwd(zl, conv_w, conv_b, wa, ba, wx, bx, lam, *, name, host=None):
    t = zl.shape[0]
    n_chunk = t // LRU_CHUNK

    def body(lx_ref, lg_ref, cw_ref, cb_ref, wa_ref, ba_ref, wx_ref, bx_ref, lam_ref, u_ref, h_ref, y_ref):
        sp = _softplus(-lam_ref[...])
        rows = _row_ids(SUB)

        def chunk(ci, hc):
            r0 = pl.multiple_of(ci * LRU_CHUNK, LRU_CHUNK)
            cur, ext = _conv_window(lx_ref, r0, ci)
            u = cb_ref[...] + cw_ref[3:4, :] * cur
            for k in range(3):
                u = u + cw_ref[k:k + 1, :] * _shift_rows_down(ext, 3 - k)
            r, gi, a, s = _lru_gates(u, wa_ref, ba_ref, wx_ref, bx_ref, sp)
            b = s * (gi * u)
            tiles = []
            for q in range(LRU_CHUNK // SUB):
                ta = a[SUB * q:SUB * (q + 1), :]
                tb = b[SUB * q:SUB * (q + 1), :]
                for d in (1, 2, 4):
                    a_sh = jnp.where(rows >= d, pltpu.roll(ta, d, axis=0), 1.0)
                    b_sh = jnp.where(rows >= d, pltpu.roll(tb, d, axis=0), 0.0)
                    tb = ta * b_sh + tb
                    ta = ta * a_sh
                hq = tb + ta * hc
                hc = hq[SUB - 1:SUB, :]
                tiles.append(hq)
            h = jnp.concatenate(tiles, axis=0)
            u_ref[pl.ds(r0, LRU_CHUNK), :] = u
            h_ref[pl.ds(r0, LRU_CHUNK), :] = h
            gel, _ = _gelu_and_grad(lg_ref[pl.ds(r0, LRU_CHUNK), :])
            y_ref[pl.ds(r0, LRU_CHUNK), :] = gel * h
            return hc

        lax.fori_loop(0, n_chunk, chunk, jnp.zeros((1, LANES), f32))

    seq = lambda cb: pl.BlockSpec((t, LANES), lambda c, cb=cb: (0, c + cb))
    rowc = pl.BlockSpec((1, LANES), lambda c: (0, c))
    diag = pl.BlockSpec((LANES, LANES), lambda c: (c, c))
    out = jax.ShapeDtypeStruct((t, LRU_W), f32)
    return _hosted_call(
        host, body, name=name, grid=(LRU_W // LANES,),
        in_specs=[seq(0), seq(4), pl.BlockSpec((4, LANES), lambda c: (0, c)), rowc, diag, rowc, diag, rowc, rowc],
        out_specs=[seq(0)] * 3, out_shape=[out] * 3,
        compiler_params=_params(("arbitrary",)),
    )(zl, zl, conv_w, conv_b, wa, ba, wx, bx, lam)


def _lru_bwd(dmix, zl, u_all, h_all, conv_w, wa, ba, wx, bx, lam, *, name, host=None):
    t = zl.shape[0]
    n_chunk = t // LRU_CHUNK

    def body(dy_ref, lx_ref, lg_ref, u_ref, h_ref, cw_ref, wa_ref, ba_ref, wx_ref, bx_ref, lam_ref,
             dlx_ref, dlg_ref, dcw_ref, dcb_ref, dba_ref, dbx_ref, dlam_ref, dwa_ref, dwx_ref, dpr_s, dpx_s):
        lam_v = lam_ref[...]
        sp = _softplus(-lam_v)
        rows = _row_ids(SUB)
        rows_c = _row_ids(LRU_CHUNK)
        zero_row = jnp.zeros((1, LANES), f32)

        def chunk(step, carry):
            dh_c, a_next0, du_next, dsp, dba, dbx, dcb, dw0, dw1, dw2, dw3 = carry
            ci = n_chunk - 1 - step
            r0 = pl.multiple_of(ci * LRU_CHUNK, LRU_CHUNK)
            sl = pl.ds(r0, LRU_CHUNK)
            u = u_ref[sl, :]
            r, gi, a, s = _lru_gates(u, wa_ref, ba_ref, wx_ref, bx_ref, sp)
            h = h_ref[sl, :]
            p0 = pl.multiple_of(jnp.maximum(r0 - SUB, 0), SUB)
            h_before = jnp.where(ci > 0, h_ref[pl.ds(p0, SUB), :], 0.0)[SUB - 1:SUB, :]
            h_prev = jnp.where(rows_c == 0, h_before, pltpu.roll(h, 1, axis=0))
            gel, dgel = _gelu_and_grad(lg_ref[sl, :])
            dy = dy_ref[sl, :]
            dlg_ref[sl, :] = (dy * h * dgel).astype(bf16)
            g_in = dy * gel
            a_next = jnp.where(rows_c == LRU_CHUNK - 1, a_next0, pltpu.roll(a, LRU_CHUNK - 1, axis=0))
            tiles = [None] * (LRU_CHUNK // SUB)
            for q in reversed(range(LRU_CHUNK // SUB)):
                ta = a_next[SUB * q:SUB * (q + 1), :]
                tb = g_in[SUB * q:SUB * (q + 1), :]
                for d in (1, 2, 4):
                    a_sh = jnp.where(rows < SUB - d, pltpu.roll(ta, SUB - d, axis=0), 1.0)
                    b_sh = jnp.where(rows < SUB - d, pltpu.roll(tb, SUB - d, axis=0), 0.0)
                    tb = ta * b_sh + tb
                    ta = ta * a_sh
                dhq = tb + ta * dh_c
                dh_c = dhq[0:1, :]
                tiles[q] = dhq
            dh = jnp.concatenate(tiles, axis=0)
            da = dh * h_prev
            ds = dh * gi * u
            dgi = dh * s * u
            du = dh * s * gi
            dlog_a = da * a - ds * (a * a) / s
            dr = dlog_a * (-LRU_C * sp)
            dsp = dsp + jnp.sum(dlog_a * (-LRU_C * r), axis=0, keepdims=True)
            dpr = dr * r * (1.0 - r)
            dpx = dgi * gi * (1.0 - gi)
            dprb = dpr.astype(bf16)
            dpxb = dpx.astype(bf16)
            dpr_s[sl, :] = dprb
            dpx_s[sl, :] = dpxb
            du = du + (lax.dot_general(dprb, wa_ref[...], _NT, preferred_element_type=f32)
                       + lax.dot_general(dpxb, wx_ref[...], _NT, preferred_element_type=f32))
            dba = dba + jnp.sum(dpr, axis=0, keepdims=True)
            dbx = dbx + jnp.sum(dpx, axis=0, keepdims=True)
            dcb = dcb + jnp.sum(du, axis=0, keepdims=True)
            du_ext = jnp.concatenate([du, du_next], axis=0)
            dlx = cw_ref[3:4, :] * du
            for k in range(3):
                dlx = dlx + cw_ref[k:k + 1, :] * _shift_rows_up(du_ext, 3 - k, LRU_CHUNK)
            dlx_ref[sl, :] = dlx.astype(bf16)
            cur, ext = _conv_window(lx_ref, r0, ci)
            dws = [dw0, dw1, dw2, dw3 + jnp.sum(du * cur, axis=0, keepdims=True)]
            for k in range(3):
                dws[k] = dws[k] + jnp.sum(du * _shift_rows_down(ext, 3 - k), axis=0, keepdims=True)
            return (dh_c, a[0:1, :], du[0:SUB, :], dsp, dba, dbx, dcb, dws[0], dws[1], dws[2], dws[3])

        init = (zero_row, zero_row, jnp.zeros((SUB, LANES), f32)) + (zero_row,) * 8
        out = lax.fori_loop(0, n_chunk, chunk, init)
        _, _, _, dsp, dba, dbx, dcb, dw0, dw1, dw2, dw3 = out
        dlam_ref[...] = dsp * (-_sigmoid(-lam_v))
        dba_ref[...] = dba
        dbx_ref[...] = dbx
        dcb_ref[...] = dcb
        dcw_ref[...] = jnp.concatenate([dw0, dw1, dw2, dw3], axis=0)
        ub = u_ref[...].astype(bf16)
        dwa_ref[...] = lax.dot_general(ub, dpr_s[...], _TN, preferred_element_type=f32)
        dwx_ref[...] = lax.dot_general(ub, dpx_s[...], _TN, preferred_element_type=f32)

    seq = lambda cb: pl.BlockSpec((t, LANES), lambda c, cb=cb: (0, c + cb))
    rowc = pl.BlockSpec((1, LANES), lambda c: (0, c))
    diag = pl.BlockSpec((LANES, LANES), lambda c: (c, c))
    gate_out = pl.BlockSpec((None, LANES, LANES), lambda c: (c, 0, 0))
    row_shape = jax.ShapeDtypeStruct((1, LRU_W), f32)
    return _hosted_call(
        host, body, name=name, grid=(LRU_W // LANES,),
        in_specs=[seq(4), seq(0), seq(4), seq(0), seq(0), pl.BlockSpec((4, LANES), lambda c: (0, c)),
                  diag, rowc, diag, rowc, rowc],
        out_specs=[seq(0), seq(0), pl.BlockSpec((4, LANES), lambda c: (0, c)), rowc, rowc, rowc, rowc,
                   gate_out, gate_out],
        out_shape=[jax.ShapeDtypeStruct((t, LRU_W), bf16)] * 2
        + [jax.ShapeDtypeStruct((4, LRU_W), f32)] + [row_shape] * 4
        + [jax.ShapeDtypeStruct((LRU_W // LANES, LANES, LANES), f32)] * 2,
        scratch_shapes=[pltpu.VMEM((t, LANES), bf16), pltpu.VMEM((t, LANES), bf16)],
        compiler_params=_params(("arbitrary",)),
    )(dmix, zl, zl, u_all, h_all, conv_w, wa, ba, wx, bx, lam)


def _block_diag(w):
    eye = jnp.eye(HEADS, dtype=w.dtype)
    return jnp.einsum("hij,hk->hikj", w, eye).reshape(LRU_W, LRU_W)


def _diag_blocks(dw):
    top = dw[:, :HEAD_D, :HEAD_D]
    bot = dw[:, HEAD_D:, HEAD_D:]
    return jnp.stack([top, bot], axis=1).reshape(HEADS, HEAD_D, HEAD_D)


def _local_step(x, target, sent, small, *, tm=512, tm_ffn=1024):
    t = x.shape[0]
    ones = jnp.ones((1, D_MODEL), f32)
    zeros = jnp.zeros((1, D_MODEL), f32)
    ln1 = (small["ln1_g"], small["ln1_b"])
    ln2 = (small["ln2_g"], small["ln2_b"])
    ln3 = (small["ln3_g"], small["ln3_b"])

    wg1, wu1, wd1 = _exchange([sent["ffn1_w_gate"], sent["ffn1_w_up"], sent["ffn1_w_down"]], gather=True,
                              name="gather_ffn1")
    xh1, rs1, hg1, hu1, w_in_g, w_out_g, conv_w_g = _ffn_fwd(
        x, ones, zeros, wg1, wu1, wd1, tm=tm_ffn, name="ffn1_fwd",
        host=_Exchange([sent["w_in"], sent["w_out"], sent["conv_w"]], gather=True))
    w_in = jnp.pad(w_in_g.transpose(1, 0, 2).reshape(D_MODEL, IN_COLS), ((0, 0), (0, 21 * LANES - IN_COLS)))
    w_out = w_out_g.reshape(D_MODEL, D_MODEL)
    conv_w = conv_w_g.transpose(1, 0, 2).reshape(4, LRU_W)
    qkv, zl, zfg = _in_proj(xh1, ln1[0], ln1[1], w_in, tm=tm, name="in_proj")
    bfg = jnp.pad(small["b_forget"], ((0, 0), (0, LANES - HEADS)))
    cum = _cum_fwd(zfg, bfg, name="cum_fwd")
    q_aug, k_aug, v_aug = _attn_prep_fwd(qkv, cum, tm=tm, name="attn_prep_fwd")
    o, lse, wg2, wu2 = _attn_fwd2(q_aug, k_aug, v_aug, name="attn_fwd",
                                  host=_Exchange([sent["ffn2_w_gate"], sent["ffn2_w_up"]], gather=True))
    wa_bd = _block_diag(small["rg_wa"]).astype(bf16)
    wx_bd = _block_diag(small["rg_wx"]).astype(bf16)
    ba = small["rg_ba"].reshape(1, LRU_W)
    bx = small["rg_bx"].reshape(1, LRU_W)
    u, h, lru, wd2 = _lru_fwd(zl, conv_w, small["conv_b"], wa_bd, ba, wx_bd, bx, small["lru_lambda"],
                              name="lru_fwd", host=_Exchange([sent["ffn2_w_down"]], gather=True))
    xh2, rs2 = _mmln([(o, 0, FOX_W, w_out, 0, D_MODEL, "nn"), (lru, 0, LRU_W, w_out, 1, D_MODEL, "nn")],
                     tm=tm, name="mix_fwd", resid=("affine", xh1) + ln1, resid_scale=ALPHA, epi="ln_fwd")
    xh3, rs3, hg2, hu2 = _ffn_fwd(xh2, ln2[0], ln2[1], wg2, wu2, wd2, tm=tm_ffn, name="ffn2_fwd")

    dpre3, sq_rows, g_ln3g, g_ln3b = _loss_bwd(xh3, rs3, ln3[0], ln3[1], target, tm=tm, name="loss_bwd")
    dpre2, g_ln2g, g_ln2b, dhg2, dhu2, a2 = _ffn_bwd(dpre3, hg2, hu2, wg2, wu2, wd2,
                                                     (xh2, rs2, ln2[0]), tm=tm, name="ffn2_bwd")
    wgrad = dict(out_dtype=bf16, tm=D_MODEL, mb=1, tn=FF_TILE, nb=4, tk=512, pair=True)
    wdgrad = dict(out_dtype=bf16, tm=512, mb=4, tn=D_MODEL, nb=1, tk=512, out_scale=0.5, pair=True)
    between_chips = functools.partial(_Exchange, gather=False, chips=True)
    g_wg2 = _mm_tn(xh2, dhg2, name="g_wg2", affine=ln2, **wgrad)
    g_wu2 = _mm_tn(xh2, dhu2, name="g_wu2", affine=ln2, **wgrad)
    g_wd2 = _mm_tn(a2, dpre3, name="g_wd2", **wdgrad)

    dmix = _mmln([(dpre2, 0, D_MODEL, w_out, 0, D_MODEL, "nt")], tm=tm, name="dmix_bwd")
    g_wout_a = _mm(o, dpre2, mode="tn", out_dtype=bf16, tm=512, tn=D_MODEL, tk=512, name="g_wout_fox")
    g_wout_b = _mm(lru, dpre2, mode="tn", out_dtype=bf16, tm=512, tn=D_MODEL, tk=512, name="g_wout_lru")
    dlx, dlg, g_cw, g_cb, g_ba, g_bx, g_lam, g_wa4, g_wx4, *p_wg2 = _lru_bwd(
        dmix, zl, u, h, conv_w, wa_bd, ba, wx_bd, bx, small["lru_lambda"], name="lru_bwd",
        host=between_chips([g_wg2]))
    p_wg2 = p_wg2[0]
    qb_aug, do_aug = _attn_prep_bwd(qkv, cum, lse, dmix, o, tm=tm, name="attn_prep_bwd")
    dq, dcum_q, p_wu2 = _attn_dq2(qb_aug, k_aug, v_aug, do_aug, name="attn_dq",
                                  host=between_chips([g_wu2]))
    g_wout_blocked = jnp.concatenate([g_wout_a, g_wout_b], axis=0).reshape(N_DEV, D_MODEL // N_DEV, D_MODEL)
    dk, dv, dcum_k, p_wd2 = _attn_dkv2(qb_aug, k_aug, v_aug, do_aug, name="attn_dkv", host=between_chips([g_wd2]))
    dfg, g_bf = _cum_bwd(dcum_q, dcum_k, zfg, bfg, name="cum_bwd")

    dz = [(dq, 0, 512), (dk, 1, 512), (dv, 2, 512), (dlx, 3, 512), (dlg, 4, 512), (dfg, 20, LANES)]
    dpre1, g_ln1g, g_ln1b = _mmln(
        [(arr, 0, w, w_in, cb, w, "nt") for (arr, cb, w) in dz],
        tm=tm, name="dx1_bwd", resid=("plain", dpre2), resid_scale=ALPHA, epi="ln_bwd", ln=(xh1, rs1, ln1[0]))
    g_win_main = _mm_tn(xh1, [arr for arr, _, _ in dz[:5]], out_dtype=bf16, tm=D_MODEL, mb=1, tn=512, nb=5, tk=512,
                        name="g_win", affine=ln1, out_blocked=True)
    g_win = [g_win_main[n] for n in range(5)] + [
        _mm(xh1, dfg, mode="tn", out_dtype=bf16, tm=D_MODEL, tn=LANES, tk=512, name="g_win_fg", affine=ln1)]
    g_win_full = jnp.concatenate([g[:, :w] for g, (_, _, w) in zip(g_win, dz)], axis=1)[:, :IN_COLS]
    g_win_blocked = g_win_full.reshape(D_MODEL, N_DEV, IN_SHARD).transpose(1, 0, 2)
    dhg1, dhu1, a1, p_win, p_wout = _ffn_bwd_act(dpre1, hg1, hu1, wd1, tm=tm_ffn, name="ffn1_bwd_act",
                                                 host=_Exchange([g_win_blocked, g_wout_blocked], gather=False))
    small_g = {
        "ln1_g": g_ln1g, "ln1_b": g_ln1b, "b_forget": g_bf[:, :HEADS], "conv_w": g_cw, "conv_b": g_cb,
        "rg_wa": _diag_blocks(g_wa4), "rg_ba": g_ba.reshape(HEADS, HEAD_D),
        "rg_wx": _diag_blocks(g_wx4), "rg_bx": g_bx.reshape(HEADS, HEAD_D), "lru_lambda": g_lam,
        "ln2_g": g_ln2g, "ln2_b": g_ln2b, "ln3_g": g_ln3g, "ln3_b": g_ln3b,
    }
    small_g["loss"] = (0.5 / D_MODEL) * jnp.sum(sq_rows, keepdims=True)
    pieces = [small_g[n].reshape(-1) for n in PACKED]
    packed = jnp.concatenate(pieces + [jnp.zeros((PACK_ROWS * LANES - sum(p.shape[0] for p in pieces),), f32)])
    g_wg1, all_packed = _mm_tn(x, dhg1, name="g_wg1",
                               host=_Exchange([packed.reshape(PACK_ROWS, LANES)], gather=True), **wgrad)
    g_wu1, p_wg1 = _mm_tn(x, dhu1, name="g_wu1", host=between_chips([g_wg1]), **wgrad)
    g_wd1, p_wu1 = _mm_tn(a1, dpre1, name="g_wd1", host=between_chips([g_wu1]), **wdgrad)
    grad_x, p_wd1 = _ffn_bwd_dx(dpre1, dhg1, dhu1, wg1, wu1, tm=tm_ffn, name="ffn1_bwd_dx",
                                host=between_chips([g_wd1]))
    parts = {
        "ffn1_w_gate": p_wg1, "ffn1_w_up": p_wu1, "ffn1_w_down": p_wd1, "w_in": p_win, "w_out": p_wout,
        "ffn2_w_gate": p_wg2, "ffn2_w_up": p_wu2, "ffn2_w_down": p_wd2,
    }
    return sq_rows, grad_x, parts, all_packed, {n: small_g[n].shape for n in PACKED}


def _adam_math(w, g, m, v):
    m2 = ADAM_B1 * m + (1.0 - ADAM_B1) * g
    v2 = ADAM_B2 * v + (1.0 - ADAM_B2) * (g * g)
    m_hat = m2 / (1.0 - ADAM_B1 ** ADAM_STEP)
    v_hat = v2 / (1.0 - ADAM_B2 ** ADAM_STEP)
    delta = -ADAM_LR * (m_hat / (jnp.sqrt(v_hat) + ADAM_EPS) + ADAM_WD * w)
    return delta, m2, v2


ADAM_TILE_ELEMS = 128 * 1024


def _adamw_big(parts, w, m, v, *, name):
    r, c = w.shape
    n_parts = parts.shape[0]
    tr = max(d for d in range(8, r + 1, 8) if r % d == 0 and d * c <= ADAM_TILE_ELEMS)

    def body(p_ref, w_ref, m_ref, v_ref, g_ref, d_ref, m2_ref, v2_ref):
        g = p_ref[0].astype(f32)
        for q in range(1, n_parts):
            g = g + p_ref[q].astype(f32)
        d, m2, v2 = _adam_math(w_ref[...], g, m_ref[...], v_ref[...])
        g_ref[...] = g
        d_ref[...] = d
        m2_ref[...] = m2
        v2_ref[...] = v2

    blk = pl.BlockSpec((tr, c), lambda i: (i, 0))
    return pl.pallas_call(
        body, name=name, grid=(r // tr,),
        in_specs=[pl.BlockSpec((n_parts, tr, c), lambda i: (0, i, 0)), blk, blk, blk],
        out_specs=[blk] * 4, out_shape=[jax.ShapeDtypeStruct((r, c), f32)] * 4,
        compiler_params=_params(("arbitrary",)),
    )(parts, w, m, v)


def _adamw_small(items, *, name):
    n = len(items)

    def body(*refs):
        ins, outs = refs[:4 * n], refs[4 * n:]
        for k in range(n):
            g, w, m, v = (ins[4 * k + q][...] for q in range(4))
            d, m2, v2 = _adam_math(w, g, m, v)
            outs[3 * k][...] = d
            outs[3 * k + 1][...] = m2
            outs[3 * k + 2][...] = v2

    vm = pl.BlockSpec(memory_space=pltpu.VMEM)
    flat = [a for item in items for a in item]
    out_shape = [jax.ShapeDtypeStruct(item[1].shape, f32) for item in items for _ in range(3)]
    return pl.pallas_call(
        body, name=name, in_specs=[vm] * (4 * n), out_specs=[vm] * (3 * n), out_shape=out_shape,
    )(*flat)


def _sum_parts(parts, *, name):
    def body(p_ref, o_ref):
        acc = p_ref[0]
        for q in range(1, N_DEV):
            acc = acc + p_ref[q]
        o_ref[...] = acc

    vm = pl.BlockSpec(memory_space=pltpu.VMEM)
    return pl.pallas_call(
        body, name=name, in_specs=[vm], out_specs=vm, out_shape=jax.ShapeDtypeStruct(parts.shape[1:], f32),
    )(parts)


WEIGHTS = ["ffn1_w_gate", "ffn1_w_up", "ffn1_w_down", "ln1_g", "ln1_b", "w_in", "b_forget", "conv_w", "conv_b",
           "rg_wa", "rg_ba", "rg_wx", "rg_bx", "lru_lambda", "w_out", "ln2_g", "ln2_b",
           "ffn2_w_gate", "ffn2_w_up", "ffn2_w_down", "ln3_g", "ln3_b"]
BIG = ["ffn1_w_gate", "ffn1_w_up", "ffn1_w_down", "w_in", "w_out", "ffn2_w_gate", "ffn2_w_up", "ffn2_w_down"]
PACKED = ["ln1_g", "ln1_b", "ln2_g", "ln2_b", "ln3_g", "ln3_b", "conv_b", "rg_ba", "rg_bx", "lru_lambda",
          "conv_w", "rg_wa", "rg_wx", "b_forget", "loss"]
PACK_ROWS = 600


def _two_d(a):
    return a.reshape((-1, a.shape[-1]))


def _transport(a):
    return _two_d(a)


def kernel(x, ffn1_w_gate, ffn1_w_up, ffn1_w_down, ln1_g, ln1_b, w_in, b_forget, conv_w, conv_b, rg_wa, rg_ba, rg_wx, rg_bx, lru_lambda, w_out, ln2_g, ln2_b, ffn2_w_gate, ffn2_w_up, ffn2_w_down, ln3_g, ln3_b, loss_target, m_ffn1_w_gate, m_ffn1_w_up, m_ffn1_w_down, m_ln1_g, m_ln1_b, m_w_in, m_b_forget, m_conv_w, m_conv_b, m_rg_wa, m_rg_ba, m_rg_wx, m_rg_bx, m_lru_lambda, m_w_out, m_ln2_g, m_ln2_b, m_ffn2_w_gate, m_ffn2_w_up, m_ffn2_w_down, m_ln3_g, m_ln3_b, v_ffn1_w_gate, v_ffn1_w_up, v_ffn1_w_down, v_ln1_g, v_ln1_b, v_w_in, v_b_forget, v_conv_w, v_conv_b, v_rg_wa, v_rg_ba, v_rg_wx, v_rg_bx, v_lru_lambda, v_w_out, v_ln2_g, v_ln2_b, v_ffn2_w_gate, v_ffn2_w_up, v_ffn2_w_down, v_ln3_g, v_ln3_b):
    w_args = (ffn1_w_gate, ffn1_w_up, ffn1_w_down, ln1_g, ln1_b, w_in, b_forget, conv_w, conv_b, rg_wa, rg_ba, rg_wx, rg_bx, lru_lambda, w_out, ln2_g, ln2_b, ffn2_w_gate, ffn2_w_up, ffn2_w_down, ln3_g, ln3_b)
    m_args = (m_ffn1_w_gate, m_ffn1_w_up, m_ffn1_w_down, m_ln1_g, m_ln1_b, m_w_in, m_b_forget, m_conv_w, m_conv_b, m_rg_wa, m_rg_ba, m_rg_wx, m_rg_bx, m_lru_lambda, m_w_out, m_ln2_g, m_ln2_b, m_ffn2_w_gate, m_ffn2_w_up, m_ffn2_w_down, m_ln3_g, m_ln3_b)
    v_args = (v_ffn1_w_gate, v_ffn1_w_up, v_ffn1_w_down, v_ln1_g, v_ln1_b, v_w_in, v_b_forget, v_conv_w, v_conv_b, v_rg_wa, v_rg_ba, v_rg_wx, v_rg_bx, v_lru_lambda, v_w_out, v_ln2_g, v_ln2_b, v_ffn2_w_gate, v_ffn2_w_up, v_ffn2_w_down, v_ln3_g, v_ln3_b)
    w = dict(zip(WEIGHTS, w_args))
    m = dict(zip(WEIGHTS, m_args))
    v = dict(zip(WEIGHTS, v_args))
    me = 4 * lax.axis_index("x") + 2 * lax.axis_index("y") + lax.axis_index("c")

    sent = {n: _transport(w[n]).astype(bf16) for n in BIG}
    sent["conv_w"] = _two_d(w["conv_w"])
    small = {n: w[n] for n in ("ln1_g", "ln1_b", "ln2_g", "ln2_b", "ln3_g", "ln3_b", "b_forget", "conv_b",
                               "lru_lambda")}
    small.update({n: w[n][0] for n in ("rg_wa", "rg_ba", "rg_wx", "rg_bx")})

    sq_rows, grad_x, parts, all_packed, small_shapes = _local_step(x[0], loss_target[0], sent, small)

    total = _sum_parts(all_packed, name="sum_small_grads").reshape(-1)
    grads, off = {}, 0
    for n in PACKED:
        size = math.prod(small_shapes[n])
        grads[n] = total[off:off + size].reshape(small_shapes[n])
        off += size
    loss = grads.pop("loss").reshape(())
    grads["conv_w"] = lax.dynamic_slice_in_dim(grads["conv_w"], me * (LRU_W // N_DEV), LRU_W // N_DEV, axis=1)

    delta, new_m, new_v = {}, {}, {}
    for n in BIG:
        g, d, m2, v2 = _adamw_big(parts[n], _transport(w[n]), _transport(m[n]), _transport(v[n]),
                                  name="adamw_" + n)
        grads[n], delta[n], new_m[n], new_v[n] = g, d, m2, v2
    small_names = [n for n in WEIGHTS if n not in BIG]
    outs = _adamw_small([(_two_d(grads[n]), _two_d(w[n]), _two_d(m[n]), _two_d(v[n])) for n in small_names],
                        name="adamw_small")
    for k, n in enumerate(small_names):
        delta[n], new_m[n], new_v[n] = outs[3 * k], outs[3 * k + 1], outs[3 * k + 2]

    def shaped(d):
        return [d[n].reshape(w[n].shape) for n in WEIGHTS]

    return (loss, grad_x[None], *shaped(grads), *shaped(delta), *shaped(new_m), *shaped(new_v))
```

```python
import functools
import math

import jax
import jax.numpy as jnp
from jax import lax
from jax.experimental import pallas as pl
from jax.experimental.pallas import tpu as pltpu

f32 = jnp.float32
bf16 = jnp.bfloat16

N_DEV = 8
D_MODEL = 1024
D_FF = 4096
FF_TILE = D_FF // N_DEV
FOX_W = 512
LRU_W = 512
HEADS = 8
HEAD_D = 64
IN_COLS = 2568
IN_SHARD = IN_COLS // N_DEV
LANES = 128
LN_EPS = 1e-5
ALPHA = 2.0 ** 0.25
ATT_SCALE = 1.0 / math.sqrt(HEAD_D)
LRU_C = 8.0
NEG_BIG = -1e30

ADAM_LR = 0.001
ADAM_B1 = 0.9
ADAM_B2 = 0.999
ADAM_EPS = 1e-08
ADAM_WD = 0.01
ADAM_STEP = 10

VMEM_LIMIT = 56 * 1024 * 1024
MESH_T = pl.DeviceIdType.MESH


def _params(sem, **kw):
    return pltpu.CompilerParams(dimension_semantics=sem, vmem_limit_bytes=VMEM_LIMIT, **kw)


def _sigmoid(x):
    return 1.0 / (1.0 + jnp.exp(-x))


def _sigmoid_tanh(x):
    return 0.5 * jnp.tanh(0.5 * x) + 0.5


def _softplus(x):
    return jnp.maximum(x, 0.0) + jnp.log(1.0 + jnp.exp(-jnp.abs(x)))


def _one_minus_exp(x):
    series = -x * (1.0 + x * (0.5 + x * (1.0 / 6 + x * (1.0 / 24 + x * (1.0 / 120 + x * (1.0 / 720))))))
    return jnp.where(x > -0.125, series, 1.0 - jnp.exp(x))


_GELU_C = math.sqrt(2.0 / math.pi)


def _gelu_and_grad(x):
    inner = _GELU_C * (x + 0.044715 * x * x * x)
    t = jnp.tanh(inner)
    g = 0.5 * x * (1.0 + t)
    dg = 0.5 * (1.0 + t) + 0.5 * x * (1.0 - t * t) * _GELU_C * (1.0 + 3 * 0.044715 * x * x)
    return g, dg


def _ln_fwd_tile(pre):
    mu = jnp.mean(pre, axis=-1, keepdims=True)
    xc = pre - mu
    var = jnp.mean(xc * xc, axis=-1, keepdims=True)
    rstd = lax.rsqrt(var + LN_EPS)
    return xc * rstd, rstd


def _ln_bwd_tile(dy, xhat, rstd, g):
    dyg = dy * g
    m1 = jnp.mean(dyg, axis=-1, keepdims=True)
    m2 = jnp.mean(dyg * xhat, axis=-1, keepdims=True)
    dpre = rstd * (dyg - m1 - xhat * m2)
    return dpre, jnp.sum(dy * xhat, axis=0, keepdims=True), jnp.sum(dy, axis=0, keepdims=True)


_NT = (((1,), (1,)), ((), ()))
_TN = (((0,), (0,)), ((), ()))


class _Exchange:
    def __init__(self, arrs, gather, chips=False):
        self.arrs, self.gather, self.n, self.chips = list(arrs), gather, len(arrs), chips

    def out_shape(self):
        return [jax.ShapeDtypeStruct(((N_DEV,) + a.shape) if self.gather else a.shape, a.dtype) for a in self.arrs]

    def scratch(self):
        n_remote = self.n * (N_DEV - 1)
        return [pltpu.SemaphoreType.DMA((n_remote,)), pltpu.SemaphoreType.DMA((n_remote,)),
                pltpu.SemaphoreType.DMA((self.n,))]

    def copies(self, ins, outs, sems):
        send_sems, recv_sems, local_sems = sems
        x, y, c = lax.axis_index("x"), lax.axis_index("y"), lax.axis_index("c")
        me = 2 * x + y if self.chips else 4 * x + 2 * y + c
        out = []
        for k in range(self.n):
            for d in (range(2, N_DEV, 2) if self.chips else range(1, N_DEV)):
                px = 1 - x if d & 4 else x
                py = 1 - y if d & 2 else y
                pc = 1 - c if d & 1 else c
                sem = k * (N_DEV - 1) + d - 1
                out.append(pltpu.make_async_remote_copy(
                    src_ref=ins[k].at[2 * px + py if self.chips else 4 * px + 2 * py + pc], dst_ref=outs[k].at[me],
                    send_sem=send_sems.at[sem], recv_sem=recv_sems.at[sem],
                    device_id=(px, py, pc), device_id_type=MESH_T))
            out.append(pltpu.make_async_copy(ins[k].at[me], outs[k].at[me], local_sems.at[k]))
        return out

    def gather_copies(self, ins, outs, sems):
        send_sems, recv_sems, local_sems = sems
        x, y, c = lax.axis_index("x"), lax.axis_index("y"), lax.axis_index("c")
        sibling = (x, y, 1 - c)
        chips = [(1 - x, y), (x, 1 - y), (1 - x, 1 - y)]
        out = []
        for k in range(self.n):
            def copy(s, block, to, src=None, k=k):
                rows = outs[k].at[4 * block[0] + 2 * block[1] + block[2]]
                sem = k * (N_DEV - 1) + s
                return pltpu.make_async_remote_copy(
                    src_ref=rows if src is None else src, dst_ref=rows, send_sem=send_sems.at[sem],
                    recv_sem=recv_sems.at[sem], device_id=to, device_id_type=MESH_T)

            first = [copy(0, (x, y, c), sibling, src=ins[k])]
            first += [copy(1 + q, (x, y, c), (*chip, c), src=ins[k]) for q, chip in enumerate(chips)]
            passed = [copy(4 + q, (*chip, c), sibling) for q, chip in enumerate(chips)]
            own = pltpu.make_async_copy(ins[k], outs[k].at[4 * x + 2 * y + c], local_sems.at[k])
            out.append((first, passed, own, copy))
        return out, sibling, chips, (x, y, c)

    def start(self, ins, outs, sems):
        if not self.gather:
            for cp in self.copies(ins, outs, sems):
                cp.start()
            return
        per_array, _, _, _ = self.gather_copies(ins, outs, sems)
        for first, _, own, _ in per_array:
            own.start()
            for cp in first:
                cp.start()

    def wait(self, ins, outs, sems):
        if not self.gather:
            for cp in self.copies(ins, outs, sems):
                cp.wait()
            return
        per_array, sibling, chips, (x, y, c) = self.gather_copies(ins, outs, sems)
        for first, passed, own, copy in per_array:
            for q, chip in enumerate(chips):
                copy(1 + q, (*chip, c), (x, y, c)).wait_recv()
                passed[q].start()
        for first, passed, own, copy in per_array:
            copy(0, sibling, (x, y, c)).wait_recv()
            for q, chip in enumerate(chips):
                copy(4 + q, (*chip, 1 - c), (x, y, c)).wait_recv()
            for cp in first + passed:
                cp.wait_send()
            own.wait()


def _hosted_call(host, body, *, name, grid, in_specs, out_specs, out_shape, scratch_shapes=(), compiler_params):
    out_specs = list(out_specs) if isinstance(out_specs, (list, tuple)) else [out_specs]
    out_shape = list(out_shape) if isinstance(out_shape, (list, tuple)) else [out_shape]
    if host is None:
        return pl.pallas_call(body, name=name, grid=grid, in_specs=in_specs, out_specs=out_specs,
                              out_shape=out_shape, scratch_shapes=list(scratch_shapes),
                              compiler_params=compiler_params)
    n_in, n_out, n_scr, k = len(in_specs), len(out_shape), len(scratch_shapes), host.n

    def wrapped(*refs):
        ins, h_in = refs[:n_in], refs[n_in:n_in + k]
        outs, h_out = refs[n_in + k:n_in + k + n_out], refs[n_in + k + n_out:n_in + 2 * k + n_out]
        scr, sems = refs[n_in + 2 * k + n_out:n_in + 2 * k + n_out + n_scr], refs[n_in + 2 * k + n_out + n_scr:]
        ids = [pl.program_id(a) for a in range(len(grid))]
        first = functools.reduce(jnp.logical_and, [i == 0 for i in ids])
        last = functools.reduce(jnp.logical_and, [i == g - 1 for i, g in zip(ids, grid)])

        @pl.when(first)
        def _():
            host.start(h_in, h_out, sems)

        body(*ins, *outs, *scr)

        @pl.when(last)
        def _():
            host.wait(h_in, h_out, sems)

    hbm = pl.BlockSpec(memory_space=pl.ANY)
    call = pl.pallas_call(
        wrapped, name=name, grid=grid, in_specs=list(in_specs) + [hbm] * k, out_specs=out_specs + [hbm] * k,
        out_shape=out_shape + host.out_shape(), scratch_shapes=list(scratch_shapes) + host.scratch(),
        compiler_params=compiler_params)
    return lambda *args: call(*args, *host.arrs)


def _exchange(arrs, *, gather, name):
    host = _Exchange(arrs, gather)

    def body(*refs):
        ins, outs, sems = refs[:host.n], refs[host.n:2 * host.n], refs[2 * host.n:]
        host.start(ins, outs, sems)
        host.wait(ins, outs, sems)

    hbm = pl.BlockSpec(memory_space=pl.ANY)
    return pl.pallas_call(
        body, name=name, in_specs=[hbm] * host.n, out_specs=[hbm] * host.n, out_shape=host.out_shape(),
        scratch_shapes=host.scratch(), compiler_params=pltpu.CompilerParams(has_side_effects=True),
    )(*arrs)


def _ffn_fwd(xhat, g_in, b_in, wg, wu, wd, *, tm, name, host=None):
    t = xhat.shape[0]
    nj = N_DEV

    def body(x_ref, g_ref, b_ref, wg_ref, wu_ref, wd_ref, xo_ref, rstd_ref, hg_ref, hu_ref, xb, acc):
        j = pl.program_id(1)

        @pl.when(j == 0)
        def _():
            xb[...] = (x_ref[...] * g_ref[...] + b_ref[...]).astype(bf16)
            acc[...] = jnp.zeros_like(acc)

        hg = jnp.dot(xb[...], wg_ref[...], preferred_element_type=f32)
        hu = jnp.dot(xb[...], wu_ref[...], preferred_element_type=f32)
        hg_ref[...] = hg.astype(bf16)
        hu_ref[...] = hu.astype(bf16)
        a = hg * _sigmoid_tanh(hg) * hu
        acc[...] += jnp.dot(a.astype(bf16), wd_ref[...], preferred_element_type=f32)

        @pl.when(j == nj - 1)
        def _():
            x = x_ref[...] * g_ref[...] + b_ref[...]
            xo, rstd = _ln_fwd_tile(ALPHA * x + 0.5 * acc[...])
            xo_ref[...] = xo
            rstd_ref[...] = rstd

    row = pl.BlockSpec((1, D_MODEL), lambda i, j: (0, 0))
    return _hosted_call(
        host, body, name=name, grid=(t // tm, nj),
        in_specs=[pl.BlockSpec((tm, D_MODEL), lambda i, j: (i, 0)), row, row,
                  pl.BlockSpec((None, D_MODEL, FF_TILE), lambda i, j: (j, 0, 0)),
                  pl.BlockSpec((None, D_MODEL, FF_TILE), lambda i, j: (j, 0, 0)),
                  pl.BlockSpec((None, FF_TILE, D_MODEL), lambda i, j: (j, 0, 0))],
        out_specs=[pl.BlockSpec((tm, D_MODEL), lambda i, j: (i, 0)),
                   pl.BlockSpec((tm, 1), lambda i, j: (i, 0)),
                   pl.BlockSpec((tm, FF_TILE), lambda i, j: (i, j)),
                   pl.BlockSpec((tm, FF_TILE), lambda i, j: (i, j))],
        out_shape=[jax.ShapeDtypeStruct((t, D_MODEL), f32), jax.ShapeDtypeStruct((t, 1), f32),
                   jax.ShapeDtypeStruct((t, D_FF), bf16), jax.ShapeDtypeStruct((t, D_FF), bf16)],
        scratch_shapes=[pltpu.VMEM((tm, D_MODEL), bf16), pltpu.VMEM((tm, D_MODEL), f32)],
        compiler_params=_params(("arbitrary", "arbitrary")),
    )(xhat, g_in, b_in, wg, wu, wd)


def _ffn_bwd(dpre, hg, hu, wg, wu, wd, ln_in, *, tm, name, host=None):
    t = dpre.shape[0]
    nj = N_DEV
    with_ln = ln_in is not None

    def body(*refs):
        if with_ln:
            (dp_ref, hg_ref, hu_ref, wg_ref, wu_ref, wd_ref, xh_ref, rs_ref, g_ref,
             dx_ref, gg_ref, gb_ref, dhg_ref, dhu_ref, a_ref, dfb, acc) = refs
        else:
            (dp_ref, hg_ref, hu_ref, wg_ref, wu_ref, wd_ref,
             dx_ref, dhg_ref, dhu_ref, a_ref, dfb, acc) = refs
        i = pl.program_id(0)
        j = pl.program_id(1)

        @pl.when(j == 0)
        def _():
            dfb[...] = (0.5 * dp_ref[...]).astype(bf16)
            acc[...] = jnp.zeros_like(acc)

        da = lax.dot_general(dfb[...], wd_ref[...], _NT, preferred_element_type=f32)
        hgv = hg_ref[...].astype(f32)
        huv = hu_ref[...].astype(f32)
        sg = _sigmoid_tanh(hgv)
        silu = hgv * sg
        a_ref[...] = (silu * huv).astype(bf16)
        dhu = (da * silu).astype(bf16)
        dhg = (da * huv * (sg * (1.0 + hgv * (1.0 - sg)))).astype(bf16)
        dhg_ref[...] = dhg
        dhu_ref[...] = dhu
        acc[...] += (lax.dot_general(dhg, wg_ref[...], _NT, preferred_element_type=f32)
                     + lax.dot_general(dhu, wu_ref[...], _NT, preferred_element_type=f32))

        @pl.when(j == nj - 1)
        def _():
            dx = ALPHA * dp_ref[...] + acc[...]
            if with_ln:
                dprev, gg, gb = _ln_bwd_tile(dx, xh_ref[...], rs_ref[...], g_ref[...])
                dx_ref[...] = dprev

                @pl.when(i == 0)
                def _():
                    gg_ref[...] = gg
                    gb_ref[...] = gb

                @pl.when(i > 0)
                def _():
                    gg_ref[...] += gg
                    gb_ref[...] += gb
            else:
                dx_ref[...] = dx

    tok = pl.BlockSpec((tm, D_MODEL), lambda i, j: (i, 0), pipeline_mode=pl.Buffered(1))
    row = pl.BlockSpec((1, D_MODEL), lambda i, j: (0, 0))
    hid = pl.BlockSpec((tm, FF_TILE), lambda i, j: (i, j))
    in_specs = [tok, hid, hid,
                pl.BlockSpec((None, D_MODEL, FF_TILE), lambda i, j: (j, 0, 0)),
                pl.BlockSpec((None, D_MODEL, FF_TILE), lambda i, j: (j, 0, 0)),
                pl.BlockSpec((None, FF_TILE, D_MODEL), lambda i, j: (j, 0, 0))]
    args = [dpre, hg, hu, wg, wu, wd]
    out_specs = [tok]
    out_shape = [jax.ShapeDtypeStruct((t, D_MODEL), f32)]
    if with_ln:
        in_specs += [tok, pl.BlockSpec((tm, 1), lambda i, j: (i, 0)), row]
        args += list(ln_in)
        out_specs += [row, row]
        out_shape += [jax.ShapeDtypeStruct((1, D_MODEL), f32)] * 2
    out_specs += [hid, hid, hid]
    out_shape += [jax.ShapeDtypeStruct((t, D_FF), bf16)] * 3
    return _hosted_call(
        host, body, name=name, grid=(t // tm, nj), in_specs=in_specs, out_specs=out_specs, out_shape=out_shape,
        scratch_shapes=[pltpu.VMEM((tm, D_MODEL), bf16), pltpu.VMEM((tm, D_MODEL), f32)],
        compiler_params=_params(("arbitrary", "arbitrary")),
    )(*args)


def _ffn_bwd_act(dpre, hg, hu, wd, *, tm, name, host=None):
    t = dpre.shape[0]

    def body(dp_ref, hg_ref, hu_ref, wd_ref, dhg_ref, dhu_ref, a_ref, dfb):
        @pl.when(pl.program_id(1) == 0)
        def _():
            dfb[...] = (0.5 * dp_ref[...]).astype(bf16)

        da = lax.dot_general(dfb[...], wd_ref[...], _NT, preferred_element_type=f32)
        hgv = hg_ref[...].astype(f32)
        huv = hu_ref[...].astype(f32)
        sg = _sigmoid_tanh(hgv)
        silu = hgv * sg
        a_ref[...] = (silu * huv).astype(bf16)
        dhu_ref[...] = (da * silu).astype(bf16)
        dhg_ref[...] = (da * huv * (sg * (1.0 + hgv * (1.0 - sg)))).astype(bf16)

    hid = pl.BlockSpec((tm, FF_TILE), lambda i, j: (i, j))
    return _hosted_call(
        host, body, name=name, grid=(t // tm, N_DEV),
        in_specs=[pl.BlockSpec((tm, D_MODEL), lambda i, j: (i, 0)), hid, hid,
                  pl.BlockSpec((None, FF_TILE, D_MODEL), lambda i, j: (j, 0, 0))],
        out_specs=[hid, hid, hid], out_shape=[jax.ShapeDtypeStruct((t, D_FF), bf16)] * 3,
        scratch_shapes=[pltpu.VMEM((tm, D_MODEL), bf16)],
        compiler_params=_params(("arbitrary", "arbitrary")),
    )(dpre, hg, hu, wd)


def _ffn_bwd_dx(dpre, dhg, dhu, wg, wu, *, tm, name, host=None):
    t = dpre.shape[0]
    nj = N_DEV

    def body(dp_ref, dhg_ref, dhu_ref, wg_ref, wu_ref, dx_ref, acc):
        j = pl.program_id(1)

        @pl.when(j == 0)
        def _():
            acc[...] = jnp.zeros_like(acc)

        acc[...] += (lax.dot_general(dhg_ref[...], wg_ref[...], _NT, preferred_element_type=f32)
                     + lax.dot_general(dhu_ref[...], wu_ref[...], _NT, preferred_element_type=f32))

        @pl.when(j == nj - 1)
        def _():
            dx_ref[...] = ALPHA * dp_ref[...] + acc[...]

    tok = pl.BlockSpec((tm, D_MODEL), lambda i, j: (i, 0))
    hid = pl.BlockSpec((tm, FF_TILE), lambda i, j: (i, j))
    wspec = pl.BlockSpec((None, D_MODEL, FF_TILE), lambda i, j: (j, 0, 0))
    return _hosted_call(
        host, body, name=name, grid=(t // tm, nj), in_specs=[tok, hid, hid, wspec, wspec],
        out_specs=[tok], out_shape=[jax.ShapeDtypeStruct((t, D_MODEL), f32)],
        scratch_shapes=[pltpu.VMEM((tm, D_MODEL), f32)],
        compiler_params=_params(("arbitrary", "arbitrary")),
    )(dpre, dhg, dhu, wg, wu)


def _mm(a, b, *, mode, out_dtype, tm, tn, tk, name, affine=None, a_cols=None, b_cols=None,
        b_blocked=False, out_blocked=False, out_scale=None):
    if mode == "nn":
        m_full, k_full = a.shape
        m_dim, k_dim = (m_full, a_cols[1]) if a_cols else (m_full, k_full)
    else:
        k_dim, m_full = a.shape
        m_dim = a_cols[1] if a_cols else m_full
    a_off = a_cols[0] if a_cols else 0
    if b_blocked:
        n_dim = b.shape[0] * b.shape[2]
        assert b.shape[2] == tn
    else:
        n_dim = b_cols[1] if b_cols else b.shape[1]
    b_off = b_cols[0] if b_cols else 0
    assert m_dim % tm == 0 and n_dim % tn == 0 and k_dim % tk == 0, (name, m_dim, n_dim, k_dim)
    nk = k_dim // tk

    def body(*refs):
        if affine is not None:
            a_ref, g_ref, s_ref, b_ref, o_ref, acc = refs
        else:
            a_ref, b_ref, o_ref, acc = refs
        k = pl.program_id(2)

        @pl.when(k == 0)
        def _():
            acc[...] = jnp.zeros_like(acc)

        av = a_ref[...]
        if affine is not None:
            av = av * g_ref[...] + s_ref[...]
        av = av.astype(bf16)
        bv = b_ref[...].astype(bf16)
        if mode == "nn":
            acc[...] += jnp.dot(av, bv, preferred_element_type=f32)
        else:
            acc[...] += lax.dot_general(av, bv, _TN, preferred_element_type=f32)

        @pl.when(k == nk - 1)
        def _():
            res = acc[...] if out_scale is None else acc[...] * out_scale
            o_ref[...] = res.astype(out_dtype)

    if mode == "nn":
        a_spec = pl.BlockSpec((tm, tk), lambda i, j, k: (i, k + a_off))
        aff_spec = pl.BlockSpec((1, tk), lambda i, j, k: (0, k + a_off))
    else:
        a_spec = pl.BlockSpec((tk, tm), lambda i, j, k: (k, i + a_off))
        aff_spec = pl.BlockSpec((1, tm), lambda i, j, k: (0, i + a_off))
    if b_blocked:
        b_spec = pl.BlockSpec((None, tk, tn), lambda i, j, k: (j, k, 0))
    else:
        b_spec = pl.BlockSpec((tk, tn), lambda i, j, k: (k, j + b_off))
    if out_blocked:
        o_spec = pl.BlockSpec((None, tm, tn), lambda i, j, k: (j, i, 0))
        o_shape = jax.ShapeDtypeStruct((n_dim // tn, m_dim, tn), out_dtype)
    else:
        o_spec = pl.BlockSpec((tm, tn), lambda i, j, k: (i, j))
        o_shape = jax.ShapeDtypeStruct((m_dim, n_dim), out_dtype)
    in_specs = [a_spec] + ([aff_spec, aff_spec] if affine is not None else []) + [b_spec]
    args = [a] + (list(affine) if affine is not None else []) + [b]
    return pl.pallas_call(
        body, name=name, grid=(m_dim // tm, n_dim // tn, nk), in_specs=in_specs, out_specs=o_spec,
        out_shape=o_shape, scratch_shapes=[pltpu.VMEM((tm, tn), f32)],
        compiler_params=_params(("arbitrary", "arbitrary", "arbitrary")),
    )(*args)


def _mm_tn(a, b, *, out_dtype, tm, mb, tn, nb, tk, name, affine=None, out_blocked=False, out_scale=None,
           pair=False, host=None):
    k_dim, m_dim = a.shape
    multi_b = isinstance(b, (list, tuple))
    b_list = list(b) if multi_b else [b]
    n_dim = nb * tn if multi_b else b.shape[1]
    assert m_dim % (mb * tm) == 0 and n_dim % (nb * tn) == 0 and k_dim % tk == 0, (name, m_dim, n_dim, k_dim)
    nk = k_dim // tk
    grid = (m_dim // (mb * tm), n_dim // (nb * tn), nk)
    if pair:
        assert mb * nb == 4 and grid[0] * grid[1] == 2 and out_dtype == bf16, name

    def body(*refs):
        if pair:
            refs, (acc, send_buf, recv_buf, send_sems, recv_sems) = refs[:-5], refs[-5:]
        else:
            refs, acc = refs[:-1], refs[-1]
        a_ref, o_ref = refs[0], refs[-1]
        if affine is not None:
            g_ref, s_ref = refs[1:3]
        b_refs = refs[3 if affine is not None else 1:-1]
        k = pl.program_id(2)

        @pl.when(k == 0)
        def _():
            acc[...] = jnp.zeros_like(acc)

        av = a_ref[...]
        if affine is not None:
            av = av * g_ref[...] + s_ref[...]
        av = av.astype(bf16)
        if multi_b:
            pieces = [r[...].astype(bf16) for r in b_refs]
        else:
            bv = b_refs[0][...].astype(bf16)
            pieces = [bv[:, jn * tn:(jn + 1) * tn] for jn in range(nb)]
        for im in range(mb):
            a_t = av[:, im * tm:(im + 1) * tm].T
            for jn in range(nb):
                acc[im * nb + jn] += jnp.dot(a_t, pieces[jn], preferred_element_type=f32)

        def scaled(v):
            return v if out_scale is None else v * out_scale

        @pl.when(k == nk - 1)
        def _():
            if pair:
                x, y, c = lax.axis_index("x"), lax.axis_index("y"), lax.axis_index("c")
                window = pl.program_id(0) + pl.program_id(1)
                swaps = []
                for cc in range(2):
                    send_buf[cc] = scaled(acc[2 * cc + 1 - c]).astype(bf16)
                    swaps.append(pltpu.make_async_remote_copy(
                        src_ref=send_buf.at[cc], dst_ref=recv_buf.at[window, cc],
                        send_sem=send_sems.at[2 * window + cc], recv_sem=recv_sems.at[2 * window + cc],
                        device_id=(x, y, 1 - c), device_id_type=MESH_T))
                    swaps[cc].start()
                for cc in range(2):
                    swaps[cc].wait_recv()
                    o_ref[cc] = (scaled(acc[2 * cc + c]) + recv_buf[window, cc].astype(f32)).astype(bf16)
                for cc in range(2):
                    swaps[cc].wait_send()
                return
            for im in range(mb):
                for jn in range(nb):
                    res = scaled(acc[im * nb + jn])
                    if out_blocked:
                        o_ref[jn, im * tm:(im + 1) * tm, :] = res.astype(out_dtype)
                    else:
                        o_ref[im * tm:(im + 1) * tm, jn * tn:(jn + 1) * tn] = res.astype(out_dtype)

    a_spec = pl.BlockSpec((tk, mb * tm), lambda i, j, k: (k, i))
    aff_spec = pl.BlockSpec((1, mb * tm), lambda i, j, k: (0, i))
    if multi_b:
        b_specs = [pl.BlockSpec((tk, tn), lambda i, j, k: (k, 0))] * nb
    else:
        b_specs = [pl.BlockSpec((tk, nb * tn), lambda i, j, k: (k, j))]
    scratch = [pltpu.VMEM((mb * nb, tm, tn), f32)]
    if pair:
        o_spec = pl.BlockSpec((2, tm, tn), lambda i, j, k: (i + j, 0, 0))
        o_shape = jax.ShapeDtypeStruct((4, tm, tn), out_dtype)
        scratch += [pltpu.VMEM((2, tm, tn), bf16), pltpu.VMEM((2, 2, tm, tn), bf16),
                    pltpu.SemaphoreType.DMA((4,)), pltpu.SemaphoreType.DMA((4,))]
    elif out_blocked:
        o_spec = pl.BlockSpec((nb, mb * tm, tn), lambda i, j, k: (j, i, 0))
        o_shape = jax.ShapeDtypeStruct((n_dim // tn, m_dim, tn), out_dtype)
    else:
        o_spec = pl.BlockSpec((mb * tm, nb * tn), lambda i, j, k: (i, j))
        o_shape = jax.ShapeDtypeStruct((m_dim, n_dim), out_dtype)
    in_specs = [a_spec] + ([aff_spec, aff_spec] if affine is not None else []) + b_specs
    args = [a] + (list(affine) if affine is not None else []) + b_list
    res = _hosted_call(
        host, body, name=name, grid=grid, in_specs=in_specs, out_specs=o_spec, out_shape=o_shape,
        scratch_shapes=scratch, compiler_params=_params(("arbitrary", "arbitrary", "arbitrary")),
    )(*args)
    return res[0] if host is None else res


def _in_proj(xhat, g, b, w_in, *, tm, name):
    t = xhat.shape[0]
    n_qkv, n_l = 3 * FOX_W, 2 * LRU_W

    def body(x_ref, g_ref, b_ref, w_ref, qkv_ref, zl_ref, zfg_ref):
        xb = (x_ref[...] * g_ref[...] + b_ref[...]).astype(bf16)
        qkv_ref[...] = jnp.dot(xb, w_ref[:, :n_qkv], preferred_element_type=f32).astype(bf16)
        zl_ref[...] = jnp.dot(xb, w_ref[:, n_qkv:n_qkv + n_l], preferred_element_type=f32)
        zfg_ref[...] = jnp.dot(xb, w_ref[:, n_qkv + n_l:], preferred_element_type=f32)

    row = pl.BlockSpec((1, D_MODEL), lambda i: (0, 0))
    return pl.pallas_call(
        body, name=name, grid=(t // tm,),
        in_specs=[pl.BlockSpec((tm, D_MODEL), lambda i: (i, 0)), row, row,
                  pl.BlockSpec(w_in.shape, lambda i: (0, 0))],
        out_specs=[pl.BlockSpec((tm, n_qkv), lambda i: (i, 0)), pl.BlockSpec((tm, n_l), lambda i: (i, 0)),
                   pl.BlockSpec((tm, LANES), lambda i: (i, 0))],
        out_shape=[jax.ShapeDtypeStruct((t, n_qkv), bf16), jax.ShapeDtypeStruct((t, n_l), f32),
                   jax.ShapeDtypeStruct((t, LANES), f32)],
        compiler_params=_params(("arbitrary",)),
    )(xhat, g, b, w_in)


def _mmln(pairs, *, tm, name, resid=None, resid_scale=1.0, epi=None, ln=None, n_out=D_MODEL):
    t = pairs[0][0].shape[0]
    n_pairs = len(pairs)
    n_resid = 0 if resid is None else len(resid) - 1

    def body(*refs):
        pos = 0
        val = None
        for p in range(n_pairs):
            a_ref, b_ref = refs[pos], refs[pos + 1]
            pos += 2
            av = a_ref[...].astype(bf16)
            bv = b_ref[...].astype(bf16)
            if pairs[p][6] == "nn":
                term = jnp.dot(av, bv, preferred_element_type=f32)
            else:
                term = lax.dot_general(av, bv, _NT, preferred_element_type=f32)
            val = term if val is None else val + term
        if resid is not None:
            if resid[0] == "plain":
                r = refs[pos][...]
            else:
                r = refs[pos][...] * refs[pos + 1][...] + refs[pos + 2][...]
            pos += n_resid
            val = val + resid_scale * r
        if epi is None:
            o_ref = refs[pos]
            o_ref[...] = val.astype(o_ref.dtype)
        elif epi == "ln_fwd":
            xo, rstd = _ln_fwd_tile(val)
            refs[pos][...] = xo
            refs[pos + 1][...] = rstd
        else:
            xh_ref, rs_ref, g_ref, dx_ref, gg_ref, gb_ref = refs[pos:pos + 6]
            dprev, gg, gb = _ln_bwd_tile(val, xh_ref[...], rs_ref[...], g_ref[...])
            dx_ref[...] = dprev
            i = pl.program_id(0)

            @pl.when(i == 0)
            def _():
                gg_ref[...] = gg
                gb_ref[...] = gb

            @pl.when(i > 0)
            def _():
                gg_ref[...] += gg
                gb_ref[...] += gb

    in_specs, args = [], []
    for (a, acb, aw, b, bcb, bw, mode) in pairs:
        in_specs.append(pl.BlockSpec((tm, aw), lambda i, acb=acb: (i, acb)))
        args.append(a)
        if mode == "nn":
            in_specs.append(pl.BlockSpec((aw, n_out), lambda i, bcb=bcb: (bcb, 0)))
        else:
            in_specs.append(pl.BlockSpec((n_out, bw), lambda i, bcb=bcb: (0, bcb)))
        args.append(b)
    tok = pl.BlockSpec((tm, n_out), lambda i: (i, 0))
    row = pl.BlockSpec((1, n_out), lambda i: (0, 0))
    col = pl.BlockSpec((tm, 1), lambda i: (i, 0))
    if resid is not None:
        in_specs += [tok] if resid[0] == "plain" else [tok, row, row]
        args += list(resid[1:])
    if epi is None:
        out_specs, out_shape = tok, jax.ShapeDtypeStruct((t, n_out), f32)
    elif epi == "ln_fwd":
        out_specs = [tok, col]
        out_shape = [jax.ShapeDtypeStruct((t, n_out), f32), jax.ShapeDtypeStruct((t, 1), f32)]
    else:
        in_specs += [tok, col, row]
        args += list(ln)
        out_specs = [tok, row, row]
        out_shape = [jax.ShapeDtypeStruct((t, n_out), f32)] + [jax.ShapeDtypeStruct((1, n_out), f32)] * 2
    return pl.pallas_call(
        body, name=name, grid=(t // tm,), in_specs=in_specs, out_specs=out_specs, out_shape=out_shape,
        compiler_params=_params(("arbitrary",)),
    )(*args)


def _loss_bwd(xhat, rstd, g, b, target, *, tm, name):
    t = xhat.shape[0]

    def body(xh_ref, rs_ref, g_ref, b_ref, tg_ref, dx_ref, sq_ref, gg_ref, gb_ref):
        i = pl.program_id(0)
        xh = xh_ref[...]
        diff = xh * g_ref[...] + b_ref[...] - tg_ref[...]
        sq = jnp.sum(diff * diff, axis=0, keepdims=True)
        dprev, gg, gb = _ln_bwd_tile(diff * (1.0 / D_MODEL), xh, rs_ref[...], g_ref[...])
        dx_ref[...] = dprev

        @pl.when(i == 0)
        def _():
            sq_ref[...] = sq
            gg_ref[...] = gg
            gb_ref[...] = gb

        @pl.when(i > 0)
        def _():
            sq_ref[...] += sq
            gg_ref[...] += gg
            gb_ref[...] += gb

    tok = pl.BlockSpec((tm, D_MODEL), lambda i: (i, 0))
    row = pl.BlockSpec((1, D_MODEL), lambda i: (0, 0))
    return pl.pallas_call(
        body, name=name, grid=(t // tm,),
        in_specs=[tok, pl.BlockSpec((tm, 1), lambda i: (i, 0)), row, row, tok],
        out_specs=[tok, row, row, row],
        out_shape=[jax.ShapeDtypeStruct((t, D_MODEL), f32)] + [jax.ShapeDtypeStruct((1, D_MODEL), f32)] * 3,
        compiler_params=_params(("arbitrary",)),
    )(xhat, rstd, g, b, target)


CUM_TILE = 256


def _tri(n, lower):
    r = lax.broadcasted_iota(jnp.int32, (n, n), 0)
    c = lax.broadcasted_iota(jnp.int32, (n, n), 1)
    return jnp.where((r >= c) if lower else (r <= c), 1.0, 0.0).astype(f32)


def _cum_fwd(zfg, bfg, *, name):
    t = zfg.shape[0]

    def body(z_ref, b_ref, o_ref, carry):
        @pl.when(pl.program_id(0) == 0)
        def _():
            carry[...] = jnp.zeros_like(carry)

        ls = -_softplus(-(z_ref[...] + b_ref[...]))
        c = jnp.dot(_tri(CUM_TILE, True), ls, preferred_element_type=f32,
                    precision=lax.Precision.HIGHEST) + carry[...]
        o_ref[...] = c
        carry[...] = c[CUM_TILE - 1:CUM_TILE, :]

    blk = pl.BlockSpec((CUM_TILE, LANES), lambda i: (i, 0))
    return pl.pallas_call(
        body, name=name, grid=(t // CUM_TILE,),
        in_specs=[blk, pl.BlockSpec((1, LANES), lambda i: (0, 0))], out_specs=blk,
        out_shape=jax.ShapeDtypeStruct((t, LANES), f32), scratch_shapes=[pltpu.VMEM((1, LANES), f32)],
        compiler_params=_params(("arbitrary",)),
    )(zfg, bfg)


def _cum_bwd(dcum_q, dcum_k, zfg, bfg, *, name):
    t = zfg.shape[0]
    n = t // CUM_TILE

    def body(d_ref, d2_ref, z_ref, b_ref, o_ref, s_ref, carry):
        i = pl.program_id(0)

        @pl.when(i == 0)
        def _():
            carry[...] = jnp.zeros_like(carry)

        dls = jnp.dot(_tri(CUM_TILE, False), d_ref[...] + d2_ref[...], preferred_element_type=f32,
                      precision=lax.Precision.HIGHEST) + carry[...]
        carry[...] = dls[0:1, :]
        lane = lax.broadcasted_iota(jnp.int32, (CUM_TILE, LANES), 1)
        dfg = jnp.where(lane < HEADS, dls * _sigmoid(-(z_ref[...] + b_ref[...])), 0.0)
        o_ref[...] = dfg
        tot = jnp.sum(dfg, axis=0, keepdims=True)

        @pl.when(i == 0)
        def _():
            s_ref[...] = tot

        @pl.when(i > 0)
        def _():
            s_ref[...] += tot

    blk = pl.BlockSpec((CUM_TILE, LANES), lambda i: (n - 1 - i, 0))
    row = pl.BlockSpec((1, LANES), lambda i: (0, 0))
    return pl.pallas_call(
        body, name=name, grid=(n,), in_specs=[blk, blk, blk, row], out_specs=[blk, row],
        out_shape=[jax.ShapeDtypeStruct((t, LANES), f32), jax.ShapeDtypeStruct((1, LANES), f32)],
        scratch_shapes=[pltpu.VMEM((1, LANES), f32)],
        compiler_params=_params(("arbitrary",)),
    )(dcum_q, dcum_k, zfg, bfg)


ATT_TILE = 512


ATT_ROWS = 32


def _causal_rows(r, transposed):
    rr = lax.broadcasted_iota(jnp.int32, (ATT_ROWS, ATT_TILE), 0) + r * ATT_ROWS
    cc = lax.broadcasted_iota(jnp.int32, (ATT_ROWS, ATT_TILE), 1)
    return (cc >= rr) if transposed else (rr >= cc)


def _causal(i, j, transposed):
    r = lax.broadcasted_iota(jnp.int32, (ATT_TILE, ATT_TILE), 0)
    c = lax.broadcasted_iota(jnp.int32, (ATT_TILE, ATT_TILE), 1)
    if transposed:
        return (c + i * ATT_TILE) >= (r + j * ATT_TILE)
    return (r + i * ATT_TILE) >= (c + j * ATT_TILE)


def _attn_fwd(qkv, cum, cum_t, *, name, host=None):
    t = qkv.shape[0]
    n = t // ATT_TILE
    tq = ATT_TILE

    def body(q_ref, k_ref, v_ref, cq_ref, ck_ref, o_ref, lse_ref, acc, m_s, l_s, c_s, s_s, p_s):
        i = pl.program_id(0)
        j = pl.program_id(1)

        @pl.when(j == 0)
        def _():
            acc[...] = jnp.zeros_like(acc)
            m_s[...] = jnp.full_like(m_s, NEG_BIG)
            l_s[...] = jnp.zeros_like(l_s)

        def block(masked):
            for h in range(HEADS):
                hs = slice(HEAD_D * h, HEAD_D * (h + 1))
                s_s[...] = lax.dot_general(q_ref[:, hs] * ATT_SCALE, k_ref[:, hs], _NT, preferred_element_type=f32)
                ck = ck_ref[h:h + 1, :]

                def rows_chunk(r, carry):
                    rows = pl.ds(pl.multiple_of(r * ATT_ROWS, ATT_ROWS), ATT_ROWS)
                    s = s_s[rows, :] + (cq_ref[rows, h:h + 1] - ck)
                    if masked:
                        s = jnp.where(_causal_rows(r, False), s, NEG_BIG)
                    m_old = m_s[rows, h:h + 1]
                    m_new = jnp.maximum(m_old, jnp.max(s, axis=-1, keepdims=True))
                    corr = jnp.exp(m_old - m_new)
                    p = jnp.exp(s - m_new)
                    l_s[rows, h:h + 1] = corr * l_s[rows, h:h + 1] + jnp.sum(p, axis=-1, keepdims=True)
                    m_s[rows, h:h + 1] = m_new
                    c_s[rows, h:h + 1] = corr
                    p_s[rows, :] = p.astype(bf16)
                    return carry

                lax.fori_loop(0, tq // ATT_ROWS, rows_chunk, 0, unroll=4)
                acc[:, hs] = c_s[:, h:h + 1] * acc[:, hs] + jnp.dot(p_s[...], v_ref[:, hs],
                                                                   preferred_element_type=f32)

        @pl.when(j < i)
        def _():
            block(False)

        @pl.when(j == i)
        def _():
            block(True)
            lse_ref[...] = jnp.zeros_like(lse_ref)
            for h in range(HEADS):
                hs = slice(HEAD_D * h, HEAD_D * (h + 1))
                l = l_s[:, h:h + 1]
                o_ref[:, hs] = acc[:, hs] / l
                lse_ref[:, h:h + 1] = m_s[:, h:h + 1] + jnp.log(l)

    return _hosted_call(
        host, body, name=name, grid=(n, n),
        in_specs=[pl.BlockSpec((tq, FOX_W), lambda i, j: (i, 0)),
                  pl.BlockSpec((tq, FOX_W), lambda i, j: (jnp.minimum(i, j), 1)),
                  pl.BlockSpec((tq, FOX_W), lambda i, j: (jnp.minimum(i, j), 2)),
                  pl.BlockSpec((tq, LANES), lambda i, j: (i, 0)),
                  pl.BlockSpec((HEADS, tq), lambda i, j: (0, jnp.minimum(i, j)))],
        out_specs=[pl.BlockSpec((tq, FOX_W), lambda i, j: (i, 0)), pl.BlockSpec((tq, LANES), lambda i, j: (i, 0))],
        out_shape=[jax.ShapeDtypeStruct((t, FOX_W), f32), jax.ShapeDtypeStruct((t, LANES), f32)],
        scratch_shapes=[pltpu.VMEM((tq, FOX_W), f32), pltpu.VMEM((tq, LANES), f32), pltpu.VMEM((tq, LANES), f32),
                        pltpu.VMEM((tq, LANES), f32), pltpu.VMEM((tq, tq), f32), pltpu.VMEM((tq, tq), bf16)],
        compiler_params=_params(("arbitrary", "arbitrary")),
    )(qkv, qkv, qkv, cum, cum_t)


def _attn_delta(dmix, o, *, tm, name):
    t = o.shape[0]

    def body(do_ref, o_ref, d_ref):
        r = lax.broadcasted_iota(jnp.int32, (FOX_W, LANES), 0)
        c = lax.broadcasted_iota(jnp.int32, (FOX_W, LANES), 1)
        pick = jnp.where(r // HEAD_D == c, 1.0, 0.0).astype(f32)
        d_ref[...] = jnp.dot(do_ref[...] * o_ref[...], pick, preferred_element_type=f32,
                             precision=lax.Precision.HIGHEST)

    blk = pl.BlockSpec((tm, FOX_W), lambda i: (i, 0))
    return pl.pallas_call(
        body, name=name, grid=(t // tm,), in_specs=[blk, blk],
        out_specs=pl.BlockSpec((tm, LANES), lambda i: (i, 0)),
        out_shape=jax.ShapeDtypeStruct((t, LANES), f32), compiler_params=_params(("arbitrary",)),
    )(dmix, o)


def _attn_dq(qkv, dmix, cum, cum_t, lse, delta, *, name, host=None):
    t = qkv.shape[0]
    n = t // ATT_TILE
    tq = ATT_TILE

    def body(q_ref, k_ref, v_ref, do_ref, cq_ref, ck_ref, lse_ref, dl_ref, dq_ref, dc_ref, acc, dc_acc):
        i = pl.program_id(0)
        j = pl.program_id(1)

        @pl.when(j == 0)
        def _():
            acc[...] = jnp.zeros_like(acc)
            dc_acc[...] = jnp.zeros_like(dc_acc)

        def block(masked):
            mask = _causal(i, j, False) if masked else None
            for h in range(HEADS):
                hs = slice(HEAD_D * h, HEAD_D * (h + 1))
                kh = k_ref[:, hs]
                s = lax.dot_general(q_ref[:, hs] * ATT_SCALE, kh, _NT, preferred_element_type=f32)
                s = s + cq_ref[:, h:h + 1] - ck_ref[h:h + 1, :]
                if masked:
                    s = jnp.where(mask, s, NEG_BIG)
                p = jnp.exp(s - lse_ref[:, h:h + 1])
                dp = lax.dot_general(do_ref[:, hs].astype(bf16), v_ref[:, hs], _NT, preferred_element_type=f32)
                ds = p * (dp - dl_ref[:, h:h + 1])
                acc[:, hs] += jnp.dot(ds.astype(bf16), kh, preferred_element_type=f32)
                dc_acc[:, h:h + 1] += jnp.sum(ds, axis=-1, keepdims=True)

        @pl.when(j < i)
        def _():
            block(False)

        @pl.when(j == i)
        def _():
            block(True)
            dq_ref[...] = (acc[...] * ATT_SCALE).astype(bf16)
            dc_ref[...] = dc_acc[...]

    col = pl.BlockSpec((tq, LANES), lambda i, j: (i, 0))
    return _hosted_call(
        host, body, name=name, grid=(n, n),
        in_specs=[pl.BlockSpec((tq, FOX_W), lambda i, j: (i, 0)),
                  pl.BlockSpec((tq, FOX_W), lambda i, j: (jnp.minimum(i, j), 1)),
                  pl.BlockSpec((tq, FOX_W), lambda i, j: (jnp.minimum(i, j), 2)),
                  pl.BlockSpec((tq, FOX_W), lambda i, j: (i, 0)),
                  col, pl.BlockSpec((HEADS, tq), lambda i, j: (0, jnp.minimum(i, j))), col, col],
        out_specs=[pl.BlockSpec((tq, FOX_W), lambda i, j: (i, 0)), col],
        out_shape=[jax.ShapeDtypeStruct((t, FOX_W), bf16), jax.ShapeDtypeStruct((t, LANES), f32)],
        scratch_shapes=[pltpu.VMEM((tq, FOX_W), f32), pltpu.VMEM((tq, LANES), f32)],
        compiler_params=_params(("arbitrary", "arbitrary")),
    )(qkv, qkv, qkv, dmix, cum, cum_t, lse, delta)


def _attn_dkv(qkv, dmix, cum, cum_t, lse_t, delta_t, *, name):
    t = qkv.shape[0]
    n = t // ATT_TILE
    tk = ATT_TILE

    def body(q_ref, k_ref, v_ref, do_ref, cq_ref, ck_ref, lse_ref, dl_ref, dk_ref, dv_ref, dc_ref, dk_acc, dv_acc, dc_acc):
        j = pl.program_id(0)
        i = pl.program_id(1)

        @pl.when(i == 0)
        def _():
            dk_acc[...] = jnp.zeros_like(dk_acc)
            dv_acc[...] = jnp.zeros_like(dv_acc)
            dc_acc[...] = jnp.zeros_like(dc_acc)

        def block(masked):
            mask = _causal(i, j, True) if masked else None
            for h in range(HEADS):
                hs = slice(HEAD_D * h, HEAD_D * (h + 1))
                qh = q_ref[:, hs]
                doh = do_ref[:, hs].astype(bf16)
                s_t = lax.dot_general(k_ref[:, hs] * ATT_SCALE, qh, _NT, preferred_element_type=f32)
                s_t = s_t + cq_ref[h:h + 1, :] - ck_ref[:, h:h + 1]
                if masked:
                    s_t = jnp.where(mask, s_t, NEG_BIG)
                p_t = jnp.exp(s_t - lse_ref[h:h + 1, :])
                dv_acc[:, hs] += jnp.dot(p_t.astype(bf16), doh, preferred_element_type=f32)
                dp_t = lax.dot_general(v_ref[:, hs], doh, _NT, preferred_element_type=f32)
                ds_t = p_t * (dp_t - dl_ref[h:h + 1, :])
                dk_acc[:, hs] += jnp.dot(ds_t.astype(bf16), qh, preferred_element_type=f32)
                dc_acc[:, h:h + 1] -= jnp.sum(ds_t, axis=-1, keepdims=True)

        @pl.when(i > j)
        def _():
            block(False)

        @pl.when(i == j)
        def _():
            block(True)

        @pl.when(i == n - 1)
        def _():
            dk_ref[...] = (dk_acc[...] * ATT_SCALE).astype(bf16)
            dv_ref[...] = dv_acc[...].astype(bf16)
            dc_ref[...] = dc_acc[...]

    rowq = pl.BlockSpec((HEADS, tk), lambda j, i: (0, jnp.maximum(i, j)))
    return pl.pallas_call(
        body, name=name, grid=(n, n),
        in_specs=[pl.BlockSpec((tk, FOX_W), lambda j, i: (jnp.maximum(i, j), 0)),
                  pl.BlockSpec((tk, FOX_W), lambda j, i: (j, 1)),
                  pl.BlockSpec((tk, FOX_W), lambda j, i: (j, 2)),
                  pl.BlockSpec((tk, FOX_W), lambda j, i: (jnp.maximum(i, j), 0)),
                  rowq, pl.BlockSpec((tk, LANES), lambda j, i: (j, 0)), rowq, rowq],
        out_specs=[pl.BlockSpec((tk, FOX_W), lambda j, i: (j, 0)), pl.BlockSpec((tk, FOX_W), lambda j, i: (j, 0)),
                   pl.BlockSpec((tk, LANES), lambda j, i: (j, 0))],
        out_shape=[jax.ShapeDtypeStruct((t, FOX_W), bf16), jax.ShapeDtypeStruct((t, FOX_W), bf16),
                   jax.ShapeDtypeStruct((t, LANES), f32)],
        scratch_shapes=[pltpu.VMEM((tk, FOX_W), f32), pltpu.VMEM((tk, FOX_W), f32), pltpu.VMEM((tk, LANES), f32)],
        compiler_params=_params(("arbitrary", "arbitrary")),
    )(qkv, qkv, qkv, dmix, cum_t, cum, lse_t, delta_t)


ATT_W = HEADS * LANES


def _data_lane(h):
    return HEAD_D * (h % 2)


def _extra_lane(h):
    return HEAD_D - _data_lane(h)


def _split3(x):
    hi = x.astype(bf16)
    rest = x - hi.astype(f32)
    mid = rest.astype(bf16)
    lo = (rest - mid.astype(f32)).astype(bf16)
    return hi, mid, lo


def _augment(pair, h, first, second, fill=0.0):
    rows = pair.shape[0]
    lane = lax.broadcasted_iota(jnp.int32, (rows, LANES), 1)
    base = _extra_lane(h)
    own = (lane < HEAD_D) if h % 2 == 0 else (lane >= HEAD_D)
    out = jnp.where(own, pair, jnp.full((rows, LANES), fill, bf16))
    for off, src in ((0, first), (3, second)):
        for q in range(3):
            val = src[q] if isinstance(src, tuple) else jnp.full((rows, 1), src, bf16)
            out = jnp.where(lane == base + off + q, val, out)
    return out


def _attn_prep_fwd(qkv, cum, *, tm, name):
    t = qkv.shape[0]

    def body(q_ref, k_ref, v_ref, c_ref, qa_ref, ka_ref, va_ref):
        for h in range(HEADS):
            pair = slice(LANES * (h // 2), LANES * (h // 2 + 1))
            hs = slice(LANES * h, LANES * (h + 1))
            c3 = _split3(c_ref[:, h:h + 1])
            qa_ref[:, hs] = _augment(q_ref[:, pair] * ATT_SCALE, h, c3, 1.0)
            ka_ref[:, hs] = _augment(k_ref[:, pair], h, 1.0, tuple(-p for p in c3))
            va_ref[:, hs] = _augment(v_ref[:, pair], h, 1.0, 1.0, fill=1.0)

    wide = pl.BlockSpec((tm, ATT_W), lambda i: (i, 0))
    out = jax.ShapeDtypeStruct((t, ATT_W), bf16)
    return pl.pallas_call(
        body, name=name, grid=(t // tm,),
        in_specs=[pl.BlockSpec((tm, FOX_W), lambda i: (i, 0)), pl.BlockSpec((tm, FOX_W), lambda i: (i, 1)),
                  pl.BlockSpec((tm, FOX_W), lambda i: (i, 2)), pl.BlockSpec((tm, LANES), lambda i: (i, 0))],
        out_specs=[wide] * 3, out_shape=[out] * 3, compiler_params=_params(("arbitrary",)),
    )(qkv, qkv, qkv, cum)


def _attn_prep_bwd(qkv, cum, lse, dmix, o, *, tm, name):
    t = qkv.shape[0]

    def body(q_ref, c_ref, l_ref, do_ref, o_ref, qa_ref, da_ref):
        for h in range(HEADS):
            pair = slice(LANES * (h // 2), LANES * (h // 2 + 1))
            src = slice(HEAD_D * h, HEAD_D * (h + 1))
            hs = slice(LANES * h, LANES * (h + 1))
            delta = jnp.sum(do_ref[:, src] * o_ref[:, src], axis=-1, keepdims=True)
            qa_ref[:, hs] = _augment(q_ref[:, pair] * ATT_SCALE, h,
                                     _split3(c_ref[:, h:h + 1] - l_ref[:, h:h + 1]), 1.0)
            da_ref[:, hs] = _augment(do_ref[:, pair].astype(bf16), h, tuple(-p for p in _split3(delta)), 0.0)

    wide = pl.BlockSpec((tm, ATT_W), lambda i: (i, 0))
    half = pl.BlockSpec((tm, FOX_W), lambda i: (i, 0))
    col = pl.BlockSpec((tm, LANES), lambda i: (i, 0))
    out = jax.ShapeDtypeStruct((t, ATT_W), bf16)
    return pl.pallas_call(
        body, name=name, grid=(t // tm,), in_specs=[half, col, col, half, half],
        out_specs=[wide] * 2, out_shape=[out] * 2, compiler_params=_params(("arbitrary",)),
    )(qkv, cum, lse, dmix, o)


def _attn_fwd2(q_aug, k_aug, v_aug, *, name, host=None):
    t = q_aug.shape[0]
    n = t // ATT_TILE
    tq = ATT_TILE

    def body(q_ref, k_ref, v_ref, o_ref, lse_ref, acc, m_s):
        i = pl.program_id(0)
        j = pl.program_id(1)

        @pl.when(j == 0)
        def _():
            acc[...] = jnp.zeros_like(acc)
            m_s[...] = jnp.full_like(m_s, NEG_BIG)

        def block(masked):
            mask = _causal(i, j, False) if masked else None
            for h in range(HEADS):
                hs = slice(LANES * h, LANES * (h + 1))
                s = lax.dot_general(q_ref[:, hs], k_ref[:, hs], _NT, preferred_element_type=f32)
                if masked:
                    s = jnp.where(mask, s, NEG_BIG)
                blocks = [s[:, LANES * b:LANES * (b + 1)] for b in range(tq // LANES)]
                m_old = m_s[h]
                m_new = jnp.maximum(m_old, jnp.broadcast_to(
                    jnp.max(functools.reduce(jnp.maximum, blocks), axis=-1, keepdims=True), (tq, LANES)))
                p = jnp.concatenate([jnp.exp(b - m_new) for b in blocks], axis=1).astype(bf16)
                acc[h] = jnp.exp(m_old - m_new) * acc[h] + jnp.dot(p, v_ref[:, hs], preferred_element_type=f32)
                m_s[h] = m_new

        @pl.when(j < i)
        def _():
            block(False)

        @pl.when(j == i)
        def _():
            block(True)
            lse_ref[...] = jnp.zeros_like(lse_ref)
            for h in range(HEADS):
                a = acc[h]
                l = a[:, _extra_lane(h):_extra_lane(h) + 1]
                o_ref[:, HEAD_D * h:HEAD_D * (h + 1)] = a[:, _data_lane(h):_data_lane(h) + HEAD_D] / l
                lse_ref[:, h:h + 1] = m_s[h][:, 0:1] + jnp.log(l)

    kv = pl.BlockSpec((tq, ATT_W), lambda i, j: (jnp.minimum(i, j), 0))
    return _hosted_call(
        host, body, name=name, grid=(n, n),
        in_specs=[pl.BlockSpec((tq, ATT_W), lambda i, j: (i, 0)), kv, kv],
        out_specs=[pl.BlockSpec((tq, FOX_W), lambda i, j: (i, 0)), pl.BlockSpec((tq, LANES), lambda i, j: (i, 0))],
        out_shape=[jax.ShapeDtypeStruct((t, FOX_W), f32), jax.ShapeDtypeStruct((t, LANES), f32)],
        scratch_shapes=[pltpu.VMEM((HEADS, tq, LANES), f32), pltpu.VMEM((HEADS, tq, LANES), f32)],
        compiler_params=_params(("arbitrary", "arbitrary")),
    )(q_aug, k_aug, v_aug)


def _attn_dq2(qb_aug, k_aug, v_aug, do_aug, *, name, host=None):
    t = qb_aug.shape[0]
    n = t // ATT_TILE
    tq = ATT_TILE

    def body(q_ref, k_ref, v_ref, do_ref, dq_ref, dc_ref, acc):
        i = pl.program_id(0)
        j = pl.program_id(1)

        @pl.when(j == 0)
        def _():
            acc[...] = jnp.zeros_like(acc)

        def block(masked):
            mask = _causal(i, j, False) if masked else None
            for h in range(HEADS):
                hs = slice(LANES * h, LANES * (h + 1))
                kh = k_ref[:, hs]
                s = lax.dot_general(q_ref[:, hs], kh, _NT, preferred_element_type=f32)
                if masked:
                    s = jnp.where(mask, s, NEG_BIG)
                dp = lax.dot_general(do_ref[:, hs], v_ref[:, hs], _NT, preferred_element_type=f32)
                ds = (jnp.exp(s) * dp).astype(bf16)
                acc[h] += jnp.dot(ds, kh, preferred_element_type=f32)

        @pl.when(j < i)
        def _():
            block(False)

        @pl.when(j == i)
        def _():
            block(True)
            dc_ref[...] = jnp.zeros_like(dc_ref)
            for h in range(HEADS):
                a = acc[h]
                dq_ref[:, HEAD_D * h:HEAD_D * (h + 1)] = (
                    a[:, _data_lane(h):_data_lane(h) + HEAD_D] * ATT_SCALE).astype(bf16)
                dc_ref[:, h:h + 1] = a[:, _extra_lane(h):_extra_lane(h) + 1]

    own = pl.BlockSpec((tq, ATT_W), lambda i, j: (i, 0))
    kv = pl.BlockSpec((tq, ATT_W), lambda i, j: (jnp.minimum(i, j), 0))
    return _hosted_call(
        host, body, name=name, grid=(n, n), in_specs=[own, kv, kv, own],
        out_specs=[pl.BlockSpec((tq, FOX_W), lambda i, j: (i, 0)), pl.BlockSpec((tq, LANES), lambda i, j: (i, 0))],
        out_shape=[jax.ShapeDtypeStruct((t, FOX_W), bf16), jax.ShapeDtypeStruct((t, LANES), f32)],
        scratch_shapes=[pltpu.VMEM((HEADS, tq, LANES), f32)],
        compiler_params=_params(("arbitrary", "arbitrary")),
    )(qb_aug, k_aug, v_aug, do_aug)


def _attn_dkv2(qb_aug, k_aug, v_aug, do_aug, *, name, host=None):
    t = qb_aug.shape[0]
    n = t // ATT_TILE
    tk = ATT_TILE

    def body(q_ref, k_ref, v_ref, do_ref, dk_ref, dv_ref, dc_ref, dk_acc, dv_acc):
        j = pl.program_id(0)
        i = pl.program_id(1)

        @pl.when(i == 0)
        def _():
            dk_acc[...] = jnp.zeros_like(dk_acc)
            dv_acc[...] = jnp.zeros_like(dv_acc)

        def block(masked):
            mask = _causal(i, j, True) if masked else None
            for h in range(HEADS):
                hs = slice(LANES * h, LANES * (h + 1))
                qh = q_ref[:, hs]
                doh = do_ref[:, hs]
                s_t = lax.dot_general(k_ref[:, hs], qh, _NT, preferred_element_type=f32)
                if masked:
                    s_t = jnp.where(mask, s_t, NEG_BIG)
                p_t = jnp.exp(s_t)
                dv_acc[h] += jnp.dot(p_t.astype(bf16), doh, preferred_element_type=f32)
                dp_t = lax.dot_general(v_ref[:, hs], doh, _NT, preferred_element_type=f32)
                dk_acc[h] += jnp.dot((p_t * dp_t).astype(bf16), qh, preferred_element_type=f32)

        @pl.when(i > j)
        def _():
            block(False)

        @pl.when(i == j)
        def _():
            block(True)

        @pl.when(i == n - 1)
        def _():
            dc_ref[...] = jnp.zeros_like(dc_ref)
            for h in range(HEADS):
                a = dk_acc[h]
                cols = slice(_data_lane(h), _data_lane(h) + HEAD_D)
                dk_ref[:, HEAD_D * h:HEAD_D * (h + 1)] = a[:, cols].astype(bf16)
                dv_ref[:, HEAD_D * h:HEAD_D * (h + 1)] = dv_acc[h][:, cols].astype(bf16)
                dc_ref[:, h:h + 1] = -a[:, _extra_lane(h) + 3:_extra_lane(h) + 4]

    own = pl.BlockSpec((tk, ATT_W), lambda j, i: (j, 0))
    qs = pl.BlockSpec((tk, ATT_W), lambda j, i: (jnp.maximum(i, j), 0))
    half = pl.BlockSpec((tk, FOX_W), lambda j, i: (j, 0))
    return _hosted_call(
        host, body, name=name, grid=(n, n), in_specs=[qs, own, own, qs],
        out_specs=[half, half, pl.BlockSpec((tk, LANES), lambda j, i: (j, 0))],
        out_shape=[jax.ShapeDtypeStruct((t, FOX_W), bf16), jax.ShapeDtypeStruct((t, FOX_W), bf16),
                   jax.ShapeDtypeStruct((t, LANES), f32)],
        scratch_shapes=[pltpu.VMEM((HEADS, tk, LANES), f32), pltpu.VMEM((HEADS, tk, LANES), f32)],
        compiler_params=_params(("arbitrary", "arbitrary")),
    )(qb_aug, k_aug, v_aug, do_aug)


LRU_CHUNK = 64
LRU_G = 256
SUB = 8


def _row_ids(n):
    return lax.broadcasted_iota(jnp.int32, (n, LRU_G), 0)


def _shift_rows_down(ext, s):
    return pltpu.roll(ext, s, axis=0)[SUB:, :]


def _shift_rows_up(ext, s, n):
    return pltpu.roll(ext, ext.shape[0] - s, axis=0)[:n, :]


def _lru_gates(u, wa_ref, ba_ref, wx_ref, bx_ref, sp):
    ub = u.astype(bf16)
    r = _sigmoid(jnp.dot(ub, wa_ref[...], preferred_element_type=f32) + ba_ref[...])
    gi = _sigmoid(jnp.dot(ub, wx_ref[...], preferred_element_type=f32) + bx_ref[...])
    log_a = -LRU_C * r * sp
    a = jnp.exp(log_a)
    s = jnp.sqrt(_one_minus_exp(2.0 * log_a))
    return r, gi, a, s


def _conv_window(lx_ref, r0, ci):
    cur = lx_ref[pl.ds(r0, LRU_CHUNK), :]
    p0 = pl.multiple_of(jnp.maximum(r0 - SUB, 0), SUB)
    prev = jnp.where(ci > 0, lx_ref[pl.ds(p0, SUB), :], 0.0)
    return cur, jnp.concatenate([prev, cur], axis=0)


def _lru_fwd(zl, conv_w, conv_b, wa, ba, wx, bx, lam, *, name, host=None):
    t = zl.shape[0]
    n_chunk = t // LRU_CHUNK

    def body(lx_ref, lg_ref, cw_ref, cb_ref, wa_ref, ba_ref, wx_ref, bx_ref, lam_ref, u_ref, h_ref, y_ref):
        sp = _softplus(-lam_ref[...])
        rows = _row_ids(SUB)

        def chunk(ci, hc):
            r0 = pl.multiple_of(ci * LRU_CHUNK, LRU_CHUNK)
            cur, ext = _conv_window(lx_ref, r0, ci)
            u = cb_ref[...] + cw_ref[3:4, :] * cur
            for k in range(3):
                u = u + cw_ref[k:k + 1, :] * _shift_rows_down(ext, 3 - k)
            r, gi, a, s = _lru_gates(u, wa_ref, ba_ref, wx_ref, bx_ref, sp)
            b = s * (gi * u)
            tiles = []
            for q in range(LRU_CHUNK // SUB):
                ta = a[SUB * q:SUB * (q + 1), :]
                tb = b[SUB * q:SUB * (q + 1), :]
                for d in (1, 2, 4):
                    a_sh = jnp.where(rows >= d, pltpu.roll(ta, d, axis=0), 1.0)
                    b_sh = jnp.where(rows >= d, pltpu.roll(tb, d, axis=0), 0.0)
                    tb = ta * b_sh + tb
                    ta = ta * a_sh
                hq = tb + ta * hc
                hc = hq[SUB - 1:SUB, :]
                tiles.append(hq)
            h = jnp.concatenate(tiles, axis=0)
            u_ref[pl.ds(r0, LRU_CHUNK), :] = u
            h_ref[pl.ds(r0, LRU_CHUNK), :] = h
            gel, _ = _gelu_and_grad(lg_ref[pl.ds(r0, LRU_CHUNK), :])
            y_ref[pl.ds(r0, LRU_CHUNK), :] = gel * h
            return hc

        lax.fori_loop(0, n_chunk, chunk, jnp.zeros((1, LRU_G), f32))

    seq = lambda cb: pl.BlockSpec((t, LRU_G), lambda c, cb=cb: (0, c + cb))
    rowc = pl.BlockSpec((1, LRU_G), lambda c: (0, c))
    diag = pl.BlockSpec((LRU_G, LRU_G), lambda c: (c, c))
    out = jax.ShapeDtypeStruct((t, LRU_W), f32)
    return _hosted_call(
        host, body, name=name, grid=(LRU_W // LRU_G,),
        in_specs=[seq(0), seq(LRU_W // LRU_G), pl.BlockSpec((4, LRU_G), lambda c: (0, c)),
                  rowc, diag, rowc, diag, rowc, rowc],
        out_specs=[seq(0)] * 3, out_shape=[out] * 3,
        compiler_params=_params(("arbitrary",)),
    )(zl, zl, conv_w, conv_b, wa, ba, wx, bx, lam)


def _lru_bwd(dmix, zl, u_all, h_all, conv_w, wa, ba, wx, bx, lam, *, name, host=None):
    t = zl.shape[0]
    n_chunk = t // LRU_CHUNK

    def body(dy_ref, lx_ref, lg_ref, u_ref, h_ref, cw_ref, wa_ref, ba_ref, wx_ref, bx_ref, lam_ref,
             dlx_ref, dlg_ref, dcw_ref, dcb_ref, dba_ref, dbx_ref, dlam_ref, dwa_ref, dwx_ref, dpr_s, dpx_s):
        lam_v = lam_ref[...]
        sp = _softplus(-lam_v)
        rows = _row_ids(SUB)
        rows_c = _row_ids(LRU_CHUNK)
        zero_row = jnp.zeros((1, LRU_G), f32)

        def chunk(step, carry):
            dh_c, a_next0, du_next, dsp, dba, dbx, dcb, dw0, dw1, dw2, dw3 = carry
            ci = n_chunk - 1 - step
            r0 = pl.multiple_of(ci * LRU_CHUNK, LRU_CHUNK)
            sl = pl.ds(r0, LRU_CHUNK)
            u = u_ref[sl, :]
            r, gi, a, s = _lru_gates(u, wa_ref, ba_ref, wx_ref, bx_ref, sp)
            h = h_ref[sl, :]
            p0 = pl.multiple_of(jnp.maximum(r0 - SUB, 0), SUB)
            h_before = jnp.where(ci > 0, h_ref[pl.ds(p0, SUB), :], 0.0)[SUB - 1:SUB, :]
            h_prev = jnp.where(rows_c == 0, h_before, pltpu.roll(h, 1, axis=0))
            gel, dgel = _gelu_and_grad(lg_ref[sl, :])
            dy = dy_ref[sl, :]
            dlg_ref[sl, :] = (dy * h * dgel).astype(bf16)
            g_in = dy * gel
            a_next = jnp.where(rows_c == LRU_CHUNK - 1, a_next0, pltpu.roll(a, LRU_CHUNK - 1, axis=0))
            tiles = [None] * (LRU_CHUNK // SUB)
            for q in reversed(range(LRU_CHUNK // SUB)):
                ta = a_next[SUB * q:SUB * (q + 1), :]
                tb = g_in[SUB * q:SUB * (q + 1), :]
                for d in (1, 2, 4):
                    a_sh = jnp.where(rows < SUB - d, pltpu.roll(ta, SUB - d, axis=0), 1.0)
                    b_sh = jnp.where(rows < SUB - d, pltpu.roll(tb, SUB - d, axis=0), 0.0)
                    tb = ta * b_sh + tb
                    ta = ta * a_sh
                dhq = tb + ta * dh_c
                dh_c = dhq[0:1, :]
                tiles[q] = dhq
            dh = jnp.concatenate(tiles, axis=0)
            da = dh * h_prev
            ds = dh * gi * u
            dgi = dh * s * u
            du = dh * s * gi
            dlog_a = da * a - ds * (a * a) / s
            dr = dlog_a * (-LRU_C * sp)
            dsp = dsp + jnp.sum(dlog_a * (-LRU_C * r), axis=0, keepdims=True)
            dpr = dr * r * (1.0 - r)
            dpx = dgi * gi * (1.0 - gi)
            dprb = dpr.astype(bf16)
            dpxb = dpx.astype(bf16)
            dpr_s[sl, :] = dprb
            dpx_s[sl, :] = dpxb
            du = du + (lax.dot_general(dprb, wa_ref[...], _NT, preferred_element_type=f32)
                       + lax.dot_general(dpxb, wx_ref[...], _NT, preferred_element_type=f32))
            dba = dba + jnp.sum(dpr, axis=0, keepdims=True)
            dbx = dbx + jnp.sum(dpx, axis=0, keepdims=True)
            dcb = dcb + jnp.sum(du, axis=0, keepdims=True)
            du_ext = jnp.concatenate([du, du_next], axis=0)
            dlx = cw_ref[3:4, :] * du
            for k in range(3):
                dlx = dlx + cw_ref[k:k + 1, :] * _shift_rows_up(du_ext, 3 - k, LRU_CHUNK)
            dlx_ref[sl, :] = dlx.astype(bf16)
            cur, ext = _conv_window(lx_ref, r0, ci)
            dws = [dw0, dw1, dw2, dw3 + jnp.sum(du * cur, axis=0, keepdims=True)]
            for k in range(3):
                dws[k] = dws[k] + jnp.sum(du * _shift_rows_down(ext, 3 - k), axis=0, keepdims=True)
            return (dh_c, a[0:1, :], du[0:SUB, :], dsp, dba, dbx, dcb, dws[0], dws[1], dws[2], dws[3])

        init = (zero_row, zero_row, jnp.zeros((SUB, LRU_G), f32)) + (zero_row,) * 8
        out = lax.fori_loop(0, n_chunk, chunk, init)
        _, _, _, dsp, dba, dbx, dcb, dw0, dw1, dw2, dw3 = out
        dlam_ref[...] = dsp * (-_sigmoid(-lam_v))
        dba_ref[...] = dba
        dbx_ref[...] = dbx
        dcb_ref[...] = dcb
        dcw_ref[...] = jnp.concatenate([dw0, dw1, dw2, dw3], axis=0)
        ub = u_ref[...].astype(bf16)
        dwa_ref[...] = lax.dot_general(ub, dpr_s[...], _TN, preferred_element_type=f32)
        dwx_ref[...] = lax.dot_general(ub, dpx_s[...], _TN, preferred_element_type=f32)

    seq = lambda cb: pl.BlockSpec((t, LRU_G), lambda c, cb=cb: (0, c + cb))
    rowc = pl.BlockSpec((1, LRU_G), lambda c: (0, c))
    diag = pl.BlockSpec((LRU_G, LRU_G), lambda c: (c, c))
    gate_out = pl.BlockSpec((None, LRU_G, LRU_G), lambda c: (c, 0, 0))
    row_shape = jax.ShapeDtypeStruct((1, LRU_W), f32)
    return _hosted_call(
        host, body, name=name, grid=(LRU_W // LRU_G,),
        in_specs=[seq(LRU_W // LRU_G), seq(0), seq(LRU_W // LRU_G), seq(0), seq(0),
                  pl.BlockSpec((4, LRU_G), lambda c: (0, c)),
                  diag, rowc, diag, rowc, rowc],
        out_specs=[seq(0), seq(0), pl.BlockSpec((4, LRU_G), lambda c: (0, c)), rowc, rowc, rowc, rowc,
                   gate_out, gate_out],
        out_shape=[jax.ShapeDtypeStruct((t, LRU_W), bf16)] * 2
        + [jax.ShapeDtypeStruct((4, LRU_W), f32)] + [row_shape] * 4
        + [jax.ShapeDtypeStruct((LRU_W // LRU_G, LRU_G, LRU_G), f32)] * 2,
        scratch_shapes=[pltpu.VMEM((t, LRU_G), bf16), pltpu.VMEM((t, LRU_G), bf16)],
        compiler_params=_params(("arbitrary",)),
    )(dmix, zl, zl, u_all, h_all, conv_w, wa, ba, wx, bx, lam)


def _block_diag(w):
    eye = jnp.eye(HEADS, dtype=w.dtype)
    return jnp.einsum("hij,hk->hikj", w, eye).reshape(LRU_W, LRU_W)


def _diag_blocks(dw):
    per = dw.shape[1] // HEAD_D
    blocks = [dw[:, HEAD_D * b:HEAD_D * (b + 1), HEAD_D * b:HEAD_D * (b + 1)] for b in range(per)]
    return jnp.stack(blocks, axis=1).reshape(HEADS, HEAD_D, HEAD_D)


def _local_step(x, target, sent, small, *, tm=512, tm_ffn=1024):
    t = x.shape[0]
    ones = jnp.ones((1, D_MODEL), f32)
    zeros = jnp.zeros((1, D_MODEL), f32)
    ln1 = (small["ln1_g"], small["ln1_b"])
    ln2 = (small["ln2_g"], small["ln2_b"])
    ln3 = (small["ln3_g"], small["ln3_b"])

    wg1, wu1, wd1 = _exchange([sent["ffn1_w_gate"], sent["ffn1_w_up"], sent["ffn1_w_down"]], gather=True,
                              name="gather_ffn1")
    xh1, rs1, hg1, hu1, w_in_g, w_out_g, conv_w_g = _ffn_fwd(
        x, ones, zeros, wg1, wu1, wd1, tm=tm_ffn, name="ffn1_fwd",
        host=_Exchange([sent["w_in"], sent["w_out"], sent["conv_w"]], gather=True))
    w_in = jnp.pad(w_in_g.transpose(1, 0, 2).reshape(D_MODEL, IN_COLS), ((0, 0), (0, 21 * LANES - IN_COLS)))
    w_out = w_out_g.reshape(D_MODEL, D_MODEL)
    conv_w = conv_w_g.transpose(1, 0, 2).reshape(4, LRU_W)
    qkv, zl, zfg = _in_proj(xh1, ln1[0], ln1[1], w_in, tm=tm, name="in_proj")
    bfg = jnp.pad(small["b_forget"], ((0, 0), (0, LANES - HEADS)))
    cum = _cum_fwd(zfg, bfg, name="cum_fwd")
    q_aug, k_aug, v_aug = _attn_prep_fwd(qkv, cum, tm=tm, name="attn_prep_fwd")
    o, lse, wg2, wu2 = _attn_fwd2(q_aug, k_aug, v_aug, name="attn_fwd",
                                  host=_Exchange([sent["ffn2_w_gate"], sent["ffn2_w_up"]], gather=True))
    wa_bd = _block_diag(small["rg_wa"]).astype(bf16)
    wx_bd = _block_diag(small["rg_wx"]).astype(bf16)
    ba = small["rg_ba"].reshape(1, LRU_W)
    bx = small["rg_bx"].reshape(1, LRU_W)
    u, h, lru, wd2 = _lru_fwd(zl, conv_w, small["conv_b"], wa_bd, ba, wx_bd, bx, small["lru_lambda"],
                              name="lru_fwd", host=_Exchange([sent["ffn2_w_down"]], gather=True))
    xh2, rs2 = _mmln([(o, 0, FOX_W, w_out, 0, D_MODEL, "nn"), (lru, 0, LRU_W, w_out, 1, D_MODEL, "nn")],
                     tm=tm, name="mix_fwd", resid=("affine", xh1) + ln1, resid_scale=ALPHA, epi="ln_fwd")
    xh3, rs3, hg2, hu2 = _ffn_fwd(xh2, ln2[0], ln2[1], wg2, wu2, wd2, tm=tm_ffn, name="ffn2_fwd")

    dpre3, sq_rows, g_ln3g, g_ln3b = _loss_bwd(xh3, rs3, ln3[0], ln3[1], target, tm=tm, name="loss_bwd")
    dpre2, g_ln2g, g_ln2b, dhg2, dhu2, a2 = _ffn_bwd(dpre3, hg2, hu2, wg2, wu2, wd2,
                                                     (xh2, rs2, ln2[0]), tm=tm_ffn, name="ffn2_bwd")
    wgrad = dict(out_dtype=bf16, tm=D_MODEL, mb=1, tn=FF_TILE, nb=4, tk=512, pair=True)
    wdgrad = dict(out_dtype=bf16, tm=512, mb=4, tn=D_MODEL, nb=1, tk=512, out_scale=0.5, pair=True)
    between_chips = functools.partial(_Exchange, gather=False, chips=True)
    g_wg2 = _mm_tn(xh2, dhg2, name="g_wg2", affine=ln2, **wgrad)
    g_wu2 = _mm_tn(xh2, dhu2, name="g_wu2", affine=ln2, **wgrad)
    g_wd2 = _mm_tn(a2, dpre3, name="g_wd2", **wdgrad)

    dmix = _mmln([(dpre2, 0, D_MODEL, w_out, 0, D_MODEL, "nt")], tm=tm, name="dmix_bwd")
    g_wout_a = _mm(o, dpre2, mode="tn", out_dtype=bf16, tm=512, tn=D_MODEL, tk=512, name="g_wout_fox")
    g_wout_b = _mm(lru, dpre2, mode="tn", out_dtype=bf16, tm=512, tn=D_MODEL, tk=512, name="g_wout_lru")
    dlx, dlg, g_cw, g_cb, g_ba, g_bx, g_lam, g_wa4, g_wx4, *p_wg2 = _lru_bwd(
        dmix, zl, u, h, conv_w, wa_bd, ba, wx_bd, bx, small["lru_lambda"], name="lru_bwd",
        host=between_chips([g_wg2]))
    p_wg2 = p_wg2[0]
    qb_aug, do_aug = _attn_prep_bwd(qkv, cum, lse, dmix, o, tm=tm, name="attn_prep_bwd")
    dq, dcum_q, p_wu2 = _attn_dq2(qb_aug, k_aug, v_aug, do_aug, name="attn_dq",
                                  host=between_chips([g_wu2]))
    g_wout_blocked = jnp.concatenate([g_wout_a, g_wout_b], axis=0).reshape(N_DEV, D_MODEL // N_DEV, D_MODEL)
    dk, dv, dcum_k, p_wd2 = _attn_dkv2(qb_aug, k_aug, v_aug, do_aug, name="attn_dkv", host=between_chips([g_wd2]))
    dfg, g_bf = _cum_bwd(dcum_q, dcum_k, zfg, bfg, name="cum_bwd")

    dz = [(dq, 0, 512), (dk, 1, 512), (dv, 2, 512), (dlx, 3, 512), (dlg, 4, 512), (dfg, 20, LANES)]
    dpre1, g_ln1g, g_ln1b = _mmln(
        [(arr, 0, w, w_in, cb, w, "nt") for (arr, cb, w) in dz],
        tm=tm, name="dx1_bwd", resid=("plain", dpre2), resid_scale=ALPHA, epi="ln_bwd", ln=(xh1, rs1, ln1[0]))
    g_win_main = _mm_tn(xh1, [arr for arr, _, _ in dz[:5]], out_dtype=bf16, tm=D_MODEL, mb=1, tn=512, nb=5, tk=512,
                        name="g_win", affine=ln1, out_blocked=True)
    g_win = [g_win_main[n] for n in range(5)] + [
        _mm(xh1, dfg, mode="tn", out_dtype=bf16, tm=D_MODEL, tn=LANES, tk=512, name="g_win_fg", affine=ln1)]
    g_win_full = jnp.concatenate([g[:, :w] for g, (_, _, w) in zip(g_win, dz)], axis=1)[:, :IN_COLS]
    g_win_blocked = g_win_full.reshape(D_MODEL, N_DEV, IN_SHARD).transpose(1, 0, 2)
    dhg1, dhu1, a1, p_win, p_wout = _ffn_bwd_act(dpre1, hg1, hu1, wd1, tm=tm_ffn, name="ffn1_bwd_act",
                                                 host=_Exchange([g_win_blocked, g_wout_blocked], gather=False))
    small_g = {
        "ln1_g": g_ln1g, "ln1_b": g_ln1b, "b_forget": g_bf[:, :HEADS], "conv_w": g_cw, "conv_b": g_cb,
        "rg_wa": _diag_blocks(g_wa4), "rg_ba": g_ba.reshape(HEADS, HEAD_D),
        "rg_wx": _diag_blocks(g_wx4), "rg_bx": g_bx.reshape(HEADS, HEAD_D), "lru_lambda": g_lam,
        "ln2_g": g_ln2g, "ln2_b": g_ln2b, "ln3_g": g_ln3g, "ln3_b": g_ln3b,
    }
    small_g["loss"] = (0.5 / D_MODEL) * jnp.sum(sq_rows, keepdims=True)
    pieces = [small_g[n].reshape(-1) for n in PACKED]
    packed = jnp.concatenate(pieces + [jnp.zeros((PACK_ROWS * LANES - sum(p.shape[0] for p in pieces),), f32)])
    g_wg1, all_packed = _mm_tn(x, dhg1, name="g_wg1",
                               host=_Exchange([packed.reshape(PACK_ROWS, LANES)], gather=True), **wgrad)
    g_wu1, p_wg1 = _mm_tn(x, dhu1, name="g_wu1", host=between_chips([g_wg1]), **wgrad)
    g_wd1, p_wu1 = _mm_tn(a1, dpre1, name="g_wd1", host=between_chips([g_wu1]), **wdgrad)
    grad_x, p_wd1 = _ffn_bwd_dx(dpre1, dhg1, dhu1, wg1, wu1, tm=tm_ffn, name="ffn1_bwd_dx",
                                host=between_chips([g_wd1]))
    parts = {
        "ffn1_w_gate": p_wg1, "ffn1_w_up": p_wu1, "ffn1_w_down": p_wd1, "w_in": p_win, "w_out": p_wout,
        "ffn2_w_gate": p_wg2, "ffn2_w_up": p_wu2, "ffn2_w_down": p_wd2,
    }
    return sq_rows, grad_x, parts, all_packed, {n: small_g[n].shape for n in PACKED}


def _adam_math(w, g, m, v):
    m2 = ADAM_B1 * m + (1.0 - ADAM_B1) * g
    v2 = ADAM_B2 * v + (1.0 - ADAM_B2) * (g * g)
    m_hat = m2 / (1.0 - ADAM_B1 ** ADAM_STEP)
    v_hat = v2 / (1.0 - ADAM_B2 ** ADAM_STEP)
    delta = -ADAM_LR * (m_hat / (jnp.sqrt(v_hat) + ADAM_EPS) + ADAM_WD * w)
    return delta, m2, v2


ADAM_TILE_ELEMS = 128 * 1024


def _adamw_big(parts, w, m, v, *, name):
    r, c = w.shape
    n_parts = parts.shape[0]
    tr = max(d for d in range(8, r + 1, 8) if r % d == 0 and d * c <= ADAM_TILE_ELEMS)

    def body(p_ref, w_ref, m_ref, v_ref, g_ref, d_ref, m2_ref, v2_ref):
        g = p_ref[0].astype(f32)
        for q in range(1, n_parts):
            g = g + p_ref[q].astype(f32)
        d, m2, v2 = _adam_math(w_ref[...], g, m_ref[...], v_ref[...])
        g_ref[...] = g
        d_ref[...] = d
        m2_ref[...] = m2
        v2_ref[...] = v2

    blk = pl.BlockSpec((tr, c), lambda i: (i, 0))
    return pl.pallas_call(
        body, name=name, grid=(r // tr,),
        in_specs=[pl.BlockSpec((n_parts, tr, c), lambda i: (0, i, 0)), blk, blk, blk],
        out_specs=[blk] * 4, out_shape=[jax.ShapeDtypeStruct((r, c), f32)] * 4,
        compiler_params=_params(("arbitrary",)),
    )(parts, w, m, v)


def _adamw_small(items, *, name):
    n = len(items)

    def body(*refs):
        ins, outs = refs[:4 * n], refs[4 * n:]
        for k in range(n):
            g, w, m, v = (ins[4 * k + q][...] for q in range(4))
            d, m2, v2 = _adam_math(w, g, m, v)
            outs[3 * k][...] = d
            outs[3 * k + 1][...] = m2
            outs[3 * k + 2][...] = v2

    vm = pl.BlockSpec(memory_space=pltpu.VMEM)
    flat = [a for item in items for a in item]
    out_shape = [jax.ShapeDtypeStruct(item[1].shape, f32) for item in items for _ in range(3)]
    return pl.pallas_call(
        body, name=name, in_specs=[vm] * (4 * n), out_specs=[vm] * (3 * n), out_shape=out_shape,
    )(*flat)


def _sum_parts(parts, *, name):
    def body(p_ref, o_ref):
        acc = p_ref[0]
        for q in range(1, N_DEV):
            acc = acc + p_ref[q]
        o_ref[...] = acc

    vm = pl.BlockSpec(memory_space=pltpu.VMEM)
    return pl.pallas_call(
        body, name=name, in_specs=[vm], out_specs=vm, out_shape=jax.ShapeDtypeStruct(parts.shape[1:], f32),
    )(parts)


WEIGHTS = ["ffn1_w_gate", "ffn1_w_up", "ffn1_w_down", "ln1_g", "ln1_b", "w_in", "b_forget", "conv_w", "conv_b",
           "rg_wa", "rg_ba", "rg_wx", "rg_bx", "lru_lambda", "w_out", "ln2_g", "ln2_b",
           "ffn2_w_gate", "ffn2_w_up", "ffn2_w_down", "ln3_g", "ln3_b"]
BIG = ["ffn1_w_gate", "ffn1_w_up", "ffn1_w_down", "w_in", "w_out", "ffn2_w_gate", "ffn2_w_up", "ffn2_w_down"]
PACKED = ["ln1_g", "ln1_b", "ln2_g", "ln2_b", "ln3_g", "ln3_b", "conv_b", "rg_ba", "rg_bx", "lru_lambda",
          "conv_w", "rg_wa", "rg_wx", "b_forget", "loss"]
PACK_ROWS = 600


def _two_d(a):
    return a.reshape((-1, a.shape[-1]))


def _transport(a):
    return _two_d(a)


def kernel(x, ffn1_w_gate, ffn1_w_up, ffn1_w_down, ln1_g, ln1_b, w_in, b_forget, conv_w, conv_b, rg_wa, rg_ba, rg_wx, rg_bx, lru_lambda, w_out, ln2_g, ln2_b, ffn2_w_gate, ffn2_w_up, ffn2_w_down, ln3_g, ln3_b, loss_target, m_ffn1_w_gate, m_ffn1_w_up, m_ffn1_w_down, m_ln1_g, m_ln1_b, m_w_in, m_b_forget, m_conv_w, m_conv_b, m_rg_wa, m_rg_ba, m_rg_wx, m_rg_bx, m_lru_lambda, m_w_out, m_ln2_g, m_ln2_b, m_ffn2_w_gate, m_ffn2_w_up, m_ffn2_w_down, m_ln3_g, m_ln3_b, v_ffn1_w_gate, v_ffn1_w_up, v_ffn1_w_down, v_ln1_g, v_ln1_b, v_w_in, v_b_forget, v_conv_w, v_conv_b, v_rg_wa, v_rg_ba, v_rg_wx, v_rg_bx, v_lru_lambda, v_w_out, v_ln2_g, v_ln2_b, v_ffn2_w_gate, v_ffn2_w_up, v_ffn2_w_down, v_ln3_g, v_ln3_b):
    w_args = (ffn1_w_gate, ffn1_w_up, ffn1_w_down, ln1_g, ln1_b, w_in, b_forget, conv_w, conv_b, rg_wa, rg_ba, rg_wx, rg_bx, lru_lambda, w_out, ln2_g, ln2_b, ffn2_w_gate, ffn2_w_up, ffn2_w_down, ln3_g, ln3_b)
    m_args = (m_ffn1_w_gate, m_ffn1_w_up, m_ffn1_w_down, m_ln1_g, m_ln1_b, m_w_in, m_b_forget, m_conv_w, m_conv_b, m_rg_wa, m_rg_ba, m_rg_wx, m_rg_bx, m_lru_lambda, m_w_out, m_ln2_g, m_ln2_b, m_ffn2_w_gate, m_ffn2_w_up, m_ffn2_w_down, m_ln3_g, m_ln3_b)
    v_args = (v_ffn1_w_gate, v_ffn1_w_up, v_ffn1_w_down, v_ln1_g, v_ln1_b, v_w_in, v_b_forget, v_conv_w, v_conv_b, v_rg_wa, v_rg_ba, v_rg_wx, v_rg_bx, v_lru_lambda, v_w_out, v_ln2_g, v_ln2_b, v_ffn2_w_gate, v_ffn2_w_up, v_ffn2_w_down, v_ln3_g, v_ln3_b)
    w = dict(zip(WEIGHTS, w_args))
    m = dict(zip(WEIGHTS, m_args))
    v = dict(zip(WEIGHTS, v_args))
    me = 4 * lax.axis_index("x") + 2 * lax.axis_index("y") + lax.axis_index("c")

    sent = {n: _transport(w[n]).astype(bf16) for n in BIG}
    sent["conv_w"] = _two_d(w["conv_w"])
    small = {n: w[n] for n in ("ln1_g", "ln1_b", "ln2_g", "ln2_b", "ln3_g", "ln3_b", "b_forget", "conv_b",
                               "lru_lambda")}
    small.update({n: w[n][0] for n in ("rg_wa", "rg_ba", "rg_wx", "rg_bx")})

    sq_rows, grad_x, parts, all_packed, small_shapes = _local_step(x[0], loss_target[0], sent, small)

    total = _sum_parts(all_packed, name="sum_small_grads").reshape(-1)
    grads, off = {}, 0
    for n in PACKED:
        size = math.prod(small_shapes[n])
        grads[n] = total[off:off + size].reshape(small_shapes[n])
        off += size
    loss = grads.pop("loss").reshape(())
    grads["conv_w"] = lax.dynamic_slice_in_dim(grads["conv_w"], me * (LRU_W // N_DEV), LRU_W // N_DEV, axis=1)

    delta, new_m, new_v = {}, {}, {}
    for n in BIG:
        g, d, m2, v2 = _adamw_big(parts[n], _transport(w[n]), _transport(m[n]), _transport(v[n]),
                                  name="adamw_" + n)
        grads[n], delta[n], new_m[n], new_v[n] = g, d, m2, v2
    small_names = [n for n in WEIGHTS if n not in BIG]
    outs = _adamw_small([(_two_d(grads[n]), _two_d(w[n]), _two_d(m[n]), _two_d(v[n])) for n in small_names],
                        name="adamw_small")
    for k, n in enumerate(small_names):
        delta[n], new_m[n], new_v[n] = outs[3 * k], outs[3 * k + 1], outs[3 * k + 2]

    def shaped(d):
        return [d[n].reshape(w[n].shape) for n in WEIGHTS]

    return (loss, grad_x[None], *shaped(grads), *shaped(delta), *shaped(new_m), *shaped(new_v))
```

```python
import functools
import math

import jax
import jax.numpy as jnp
from jax import lax
from jax.experimental import pallas as pl
from jax.experimental.pallas import tpu as pltpu

f32 = jnp.float32
bf16 = jnp.bfloat16

N_DEV = 8
D_MODEL = 1024
D_FF = 4096
FF_TILE = D_FF // N_DEV
FOX_W = 512
LRU_W = 512
HEADS = 8
HEAD_D = 64
IN_COLS = 2568
IN_SHARD = IN_COLS // N_DEV
LANES = 128
LN_EPS = 1e-5
ALPHA = 2.0 ** 0.25
ATT_SCALE = 1.0 / math.sqrt(HEAD_D)
LRU_C = 8.0
NEG_BIG = -1e30

ADAM_LR = 0.001
ADAM_B1 = 0.9
ADAM_B2 = 0.999
ADAM_EPS = 1e-08
ADAM_WD = 0.01
ADAM_STEP = 10

VMEM_LIMIT = 56 * 1024 * 1024
MESH_T = pl.DeviceIdType.MESH


def _params(sem, **kw):
    return pltpu.CompilerParams(dimension_semantics=sem, vmem_limit_bytes=VMEM_LIMIT, **kw)


def _sigmoid(x):
    return 1.0 / (1.0 + jnp.exp(-x))


def _sigmoid_tanh(x):
    return 0.5 * jnp.tanh(0.5 * x) + 0.5


def _softplus(x):
    return jnp.maximum(x, 0.0) + jnp.log(1.0 + jnp.exp(-jnp.abs(x)))


def _one_minus_exp(x):
    series = -x * (1.0 + x * (0.5 + x * (1.0 / 6 + x * (1.0 / 24 + x * (1.0 / 120 + x * (1.0 / 720))))))
    return jnp.where(x > -0.125, series, 1.0 - jnp.exp(x))


_GELU_C = math.sqrt(2.0 / math.pi)


def _gelu_and_grad(x):
    inner = _GELU_C * (x + 0.044715 * x * x * x)
    t = jnp.tanh(inner)
    g = 0.5 * x * (1.0 + t)
    dg = 0.5 * (1.0 + t) + 0.5 * x * (1.0 - t * t) * _GELU_C * (1.0 + 3 * 0.044715 * x * x)
    return g, dg


def _ln_fwd_tile(pre):
    mu = jnp.mean(pre, axis=-1, keepdims=True)
    xc = pre - mu
    var = jnp.mean(xc * xc, axis=-1, keepdims=True)
    rstd = lax.rsqrt(var + LN_EPS)
    return xc * rstd, rstd


def _ln_bwd_tile(dy, xhat, rstd, g):
    dyg = dy * g
    m1 = jnp.mean(dyg, axis=-1, keepdims=True)
    m2 = jnp.mean(dyg * xhat, axis=-1, keepdims=True)
    dpre = rstd * (dyg - m1 - xhat * m2)
    return dpre, jnp.sum(dy * xhat, axis=0, keepdims=True), jnp.sum(dy, axis=0, keepdims=True)


_NT = (((1,), (1,)), ((), ()))
_TN = (((0,), (0,)), ((), ()))


class _Exchange:
    def __init__(self, arrs, gather, chips=False):
        self.arrs, self.gather, self.n, self.chips = list(arrs), gather, len(arrs), chips

    def out_shape(self):
        return [jax.ShapeDtypeStruct(((N_DEV,) + a.shape) if self.gather else a.shape, a.dtype) for a in self.arrs]

    def scratch(self):
        n_remote = self.n * (N_DEV - 1)
        return [pltpu.SemaphoreType.DMA((n_remote,)), pltpu.SemaphoreType.DMA((n_remote,)),
                pltpu.SemaphoreType.DMA((self.n,))]

    def copies(self, ins, outs, sems):
        send_sems, recv_sems, local_sems = sems
        x, y, c = lax.axis_index("x"), lax.axis_index("y"), lax.axis_index("c")
        me = 2 * x + y if self.chips else 4 * x + 2 * y + c
        out = []
        for k in range(self.n):
            for d in (range(2, N_DEV, 2) if self.chips else range(1, N_DEV)):
                px = 1 - x if d & 4 else x
                py = 1 - y if d & 2 else y
                pc = 1 - c if d & 1 else c
                sem = k * (N_DEV - 1) + d - 1
                out.append(pltpu.make_async_remote_copy(
                    src_ref=ins[k].at[2 * px + py if self.chips else 4 * px + 2 * py + pc], dst_ref=outs[k].at[me],
                    send_sem=send_sems.at[sem], recv_sem=recv_sems.at[sem],
                    device_id=(px, py, pc), device_id_type=MESH_T))
            out.append(pltpu.make_async_copy(ins[k].at[me], outs[k].at[me], local_sems.at[k]))
        return out

    def gather_copies(self, ins, outs, sems):
        send_sems, recv_sems, local_sems = sems
        x, y, c = lax.axis_index("x"), lax.axis_index("y"), lax.axis_index("c")
        sibling = (x, y, 1 - c)
        chips = [(1 - x, y), (x, 1 - y), (1 - x, 1 - y)]
        out = []
        for k in range(self.n):
            def copy(s, block, to, src=None, k=k):
                rows = outs[k].at[4 * block[0] + 2 * block[1] + block[2]]
                sem = k * (N_DEV - 1) + s
                return pltpu.make_async_remote_copy(
                    src_ref=rows if src is None else src, dst_ref=rows, send_sem=send_sems.at[sem],
                    recv_sem=recv_sems.at[sem], device_id=to, device_id_type=MESH_T)

            first = [copy(0, (x, y, c), sibling, src=ins[k])]
            first += [copy(1 + q, (x, y, c), (*chip, c), src=ins[k]) for q, chip in enumerate(chips)]
            passed = [copy(4 + q, (*chip, c), sibling) for q, chip in enumerate(chips)]
            own = pltpu.make_async_copy(ins[k], outs[k].at[4 * x + 2 * y + c], local_sems.at[k])
            out.append((first, passed, own, copy))
        return out, sibling, chips, (x, y, c)

    def start(self, ins, outs, sems):
        if not self.gather:
            for cp in self.copies(ins, outs, sems):
                cp.start()
            return
        per_array, _, _, _ = self.gather_copies(ins, outs, sems)
        for first, _, own, _ in per_array:
            own.start()
            for cp in first:
                cp.start()

    def wait(self, ins, outs, sems):
        if not self.gather:
            for cp in self.copies(ins, outs, sems):
                cp.wait()
            return
        per_array, sibling, chips, (x, y, c) = self.gather_copies(ins, outs, sems)
        for first, passed, own, copy in per_array:
            for q, chip in enumerate(chips):
                copy(1 + q, (*chip, c), (x, y, c)).wait_recv()
                passed[q].start()
        for first, passed, own, copy in per_array:
            copy(0, sibling, (x, y, c)).wait_recv()
            for q, chip in enumerate(chips):
                copy(4 + q, (*chip, 1 - c), (x, y, c)).wait_recv()
            for cp in first + passed:
                cp.wait_send()
            own.wait()


def _hosted_call(host, body, *, name, grid, in_specs, out_specs, out_shape, scratch_shapes=(), compiler_params):
    out_specs = list(out_specs) if isinstance(out_specs, (list, tuple)) else [out_specs]
    out_shape = list(out_shape) if isinstance(out_shape, (list, tuple)) else [out_shape]
    if host is None:
        return pl.pallas_call(body, name=name, grid=grid, in_specs=in_specs, out_specs=out_specs,
                              out_shape=out_shape, scratch_shapes=list(scratch_shapes),
                              compiler_params=compiler_params)
    n_in, n_out, n_scr, k = len(in_specs), len(out_shape), len(scratch_shapes), host.n

    def wrapped(*refs):
        ins, h_in = refs[:n_in], refs[n_in:n_in + k]
        outs, h_out = refs[n_in + k:n_in + k + n_out], refs[n_in + k + n_out:n_in + 2 * k + n_out]
        scr, sems = refs[n_in + 2 * k + n_out:n_in + 2 * k + n_out + n_scr], refs[n_in + 2 * k + n_out + n_scr:]
        ids = [pl.program_id(a) for a in range(len(grid))]
        first = functools.reduce(jnp.logical_and, [i == 0 for i in ids])
        last = functools.reduce(jnp.logical_and, [i == g - 1 for i, g in zip(ids, grid)])

        @pl.when(first)
        def _():
            host.start(h_in, h_out, sems)

        body(*ins, *outs, *scr)

        @pl.when(last)
        def _():
            host.wait(h_in, h_out, sems)

    hbm = pl.BlockSpec(memory_space=pl.ANY)
    call = pl.pallas_call(
        wrapped, name=name, grid=grid, in_specs=list(in_specs) + [hbm] * k, out_specs=out_specs + [hbm] * k,
        out_shape=out_shape + host.out_shape(), scratch_shapes=list(scratch_shapes) + host.scratch(),
        compiler_params=compiler_params)
    return lambda *args: call(*args, *host.arrs)


def _exchange(arrs, *, gather, name):
    host = _Exchange(arrs, gather)

    def body(*refs):
        ins, outs, sems = refs[:host.n], refs[host.n:2 * host.n], refs[2 * host.n:]
        host.start(ins, outs, sems)
        host.wait(ins, outs, sems)

    hbm = pl.BlockSpec(memory_space=pl.ANY)
    return pl.pallas_call(
        body, name=name, in_specs=[hbm] * host.n, out_specs=[hbm] * host.n, out_shape=host.out_shape(),
        scratch_shapes=host.scratch(), compiler_params=pltpu.CompilerParams(has_side_effects=True),
    )(*arrs)


def _ffn_fwd(xhat, g_in, b_in, wg, wu, wd, *, tm, name, host=None):
    t = xhat.shape[0]
    nj = N_DEV

    def body(x_ref, g_ref, b_ref, wg_ref, wu_ref, wd_ref, xo_ref, rstd_ref, hg_ref, hu_ref, xb, acc):
        j = pl.program_id(1)

        @pl.when(j == 0)
        def _():
            xb[...] = (x_ref[...] * g_ref[...] + b_ref[...]).astype(bf16)
            acc[...] = jnp.zeros_like(acc)

        hg = jnp.dot(xb[...], wg_ref[...], preferred_element_type=f32)
        hu = jnp.dot(xb[...], wu_ref[...], preferred_element_type=f32)
        hg_ref[...] = hg.astype(bf16)
        hu_ref[...] = hu.astype(bf16)
        a = hg * _sigmoid_tanh(hg) * hu
        acc[...] += jnp.dot(a.astype(bf16), wd_ref[...], preferred_element_type=f32)

        @pl.when(j == nj - 1)
        def _():
            x = x_ref[...] * g_ref[...] + b_ref[...]
            xo, rstd = _ln_fwd_tile(ALPHA * x + 0.5 * acc[...])
            xo_ref[...] = xo
            rstd_ref[...] = rstd

    row = pl.BlockSpec((1, D_MODEL), lambda i, j: (0, 0))
    return _hosted_call(
        host, body, name=name, grid=(t // tm, nj),
        in_specs=[pl.BlockSpec((tm, D_MODEL), lambda i, j: (i, 0)), row, row,
                  pl.BlockSpec((None, D_MODEL, FF_TILE), lambda i, j: (j, 0, 0)),
                  pl.BlockSpec((None, D_MODEL, FF_TILE), lambda i, j: (j, 0, 0)),
                  pl.BlockSpec((None, FF_TILE, D_MODEL), lambda i, j: (j, 0, 0))],
        out_specs=[pl.BlockSpec((tm, D_MODEL), lambda i, j: (i, 0)),
                   pl.BlockSpec((tm, 1), lambda i, j: (i, 0)),
                   pl.BlockSpec((tm, FF_TILE), lambda i, j: (i, j)),
                   pl.BlockSpec((tm, FF_TILE), lambda i, j: (i, j))],
        out_shape=[jax.ShapeDtypeStruct((t, D_MODEL), f32), jax.ShapeDtypeStruct((t, 1), f32),
                   jax.ShapeDtypeStruct((t, D_FF), bf16), jax.ShapeDtypeStruct((t, D_FF), bf16)],
        scratch_shapes=[pltpu.VMEM((tm, D_MODEL), bf16), pltpu.VMEM((tm, D_MODEL), f32)],
        compiler_params=_params(("arbitrary", "arbitrary")),
    )(xhat, g_in, b_in, wg, wu, wd)


def _ffn_bwd(dpre, hg, hu, wg, wu, wd, ln_in, *, tm, name, host=None):
    t = dpre.shape[0]
    nj = N_DEV
    with_ln = ln_in is not None

    def body(*refs):
        if with_ln:
            (dp_ref, hg_ref, hu_ref, wg_ref, wu_ref, wd_ref, xh_ref, rs_ref, g_ref,
             dx_ref, gg_ref, gb_ref, dhg_ref, dhu_ref, a_ref, dfb, acc) = refs
        else:
            (dp_ref, hg_ref, hu_ref, wg_ref, wu_ref, wd_ref,
             dx_ref, dhg_ref, dhu_ref, a_ref, dfb, acc) = refs
        i = pl.program_id(0)
        j = pl.program_id(1)

        @pl.when(j == 0)
        def _():
            dfb[...] = (0.5 * dp_ref[...]).astype(bf16)
            acc[...] = jnp.zeros_like(acc)

        da = lax.dot_general(dfb[...], wd_ref[...], _NT, preferred_element_type=f32)
        hgv = hg_ref[...].astype(f32)
        huv = hu_ref[...].astype(f32)
        sg = _sigmoid_tanh(hgv)
        silu = hgv * sg
        a_ref[...] = (silu * huv).astype(bf16)
        dhu = (da * silu).astype(bf16)
        dhg = (da * huv * (sg * (1.0 + hgv * (1.0 - sg)))).astype(bf16)
        dhg_ref[...] = dhg
        dhu_ref[...] = dhu
        acc[...] += (lax.dot_general(dhg, wg_ref[...], _NT, preferred_element_type=f32)
                     + lax.dot_general(dhu, wu_ref[...], _NT, preferred_element_type=f32))

        @pl.when(j == nj - 1)
        def _():
            dx = ALPHA * dp_ref[...] + acc[...]
            if with_ln:
                dprev, gg, gb = _ln_bwd_tile(dx, xh_ref[...], rs_ref[...], g_ref[...])
                dx_ref[...] = dprev

                @pl.when(i == 0)
                def _():
                    gg_ref[...] = gg
                    gb_ref[...] = gb

                @pl.when(i > 0)
                def _():
                    gg_ref[...] += gg
                    gb_ref[...] += gb
            else:
                dx_ref[...] = dx

    tok = pl.BlockSpec((tm, D_MODEL), lambda i, j: (i, 0), pipeline_mode=pl.Buffered(1))
    row = pl.BlockSpec((1, D_MODEL), lambda i, j: (0, 0))
    hid = pl.BlockSpec((tm, FF_TILE), lambda i, j: (i, j))
    in_specs = [tok, hid, hid,
                pl.BlockSpec((None, D_MODEL, FF_TILE), lambda i, j: (j, 0, 0)),
                pl.BlockSpec((None, D_MODEL, FF_TILE), lambda i, j: (j, 0, 0)),
                pl.BlockSpec((None, FF_TILE, D_MODEL), lambda i, j: (j, 0, 0))]
    args = [dpre, hg, hu, wg, wu, wd]
    out_specs = [tok]
    out_shape = [jax.ShapeDtypeStruct((t, D_MODEL), f32)]
    if with_ln:
        in_specs += [tok, pl.BlockSpec((tm, 1), lambda i, j: (i, 0)), row]
        args += list(ln_in)
        out_specs += [row, row]
        out_shape += [jax.ShapeDtypeStruct((1, D_MODEL), f32)] * 2
    out_specs += [hid, hid, hid]
    out_shape += [jax.ShapeDtypeStruct((t, D_FF), bf16)] * 3
    return _hosted_call(
        host, body, name=name, grid=(t // tm, nj), in_specs=in_specs, out_specs=out_specs, out_shape=out_shape,
        scratch_shapes=[pltpu.VMEM((tm, D_MODEL), bf16), pltpu.VMEM((tm, D_MODEL), f32)],
        compiler_params=_params(("arbitrary", "arbitrary")),
    )(*args)


def _ffn_bwd_act(dpre, hg, hu, wd, *, tm, name, host=None):
    t = dpre.shape[0]

    def body(dp_ref, hg_ref, hu_ref, wd_ref, dhg_ref, dhu_ref, a_ref, dfb):
        @pl.when(pl.program_id(1) == 0)
        def _():
            dfb[...] = (0.5 * dp_ref[...]).astype(bf16)

        da = lax.dot_general(dfb[...], wd_ref[...], _NT, preferred_element_type=f32)
        hgv = hg_ref[...].astype(f32)
        huv = hu_ref[...].astype(f32)
        sg = _sigmoid_tanh(hgv)
        silu = hgv * sg
        a_ref[...] = (silu * huv).astype(bf16)
        dhu_ref[...] = (da * silu).astype(bf16)
        dhg_ref[...] = (da * huv * (sg * (1.0 + hgv * (1.0 - sg)))).astype(bf16)

    hid = pl.BlockSpec((tm, FF_TILE), lambda i, j: (i, j))
    return _hosted_call(
        host, body, name=name, grid=(t // tm, N_DEV),
        in_specs=[pl.BlockSpec((tm, D_MODEL), lambda i, j: (i, 0)), hid, hid,
                  pl.BlockSpec((None, FF_TILE, D_MODEL), lambda i, j: (j, 0, 0))],
        out_specs=[hid, hid, hid], out_shape=[jax.ShapeDtypeStruct((t, D_FF), bf16)] * 3,
        scratch_shapes=[pltpu.VMEM((tm, D_MODEL), bf16)],
        compiler_params=_params(("arbitrary", "arbitrary")),
    )(dpre, hg, hu, wd)


def _ffn_bwd_dx(dpre, dhg, dhu, wg, wu, *, tm, name, host=None):
    t = dpre.shape[0]
    nj = N_DEV

    def body(dp_ref, dhg_ref, dhu_ref, wg_ref, wu_ref, dx_ref, acc):
        j = pl.program_id(1)

        @pl.when(j == 0)
        def _():
            acc[...] = jnp.zeros_like(acc)

        acc[...] += (lax.dot_general(dhg_ref[...], wg_ref[...], _NT, preferred_element_type=f32)
                     + lax.dot_general(dhu_ref[...], wu_ref[...], _NT, preferred_element_type=f32))

        @pl.when(j == nj - 1)
        def _():
            dx_ref[...] = ALPHA * dp_ref[...] + acc[...]

    tok = pl.BlockSpec((tm, D_MODEL), lambda i, j: (i, 0))
    hid = pl.BlockSpec((tm, FF_TILE), lambda i, j: (i, j))
    wspec = pl.BlockSpec((None, D_MODEL, FF_TILE), lambda i, j: (j, 0, 0))
    return _hosted_call(
        host, body, name=name, grid=(t // tm, nj), in_specs=[tok, hid, hid, wspec, wspec],
        out_specs=[tok], out_shape=[jax.ShapeDtypeStruct((t, D_MODEL), f32)],
        scratch_shapes=[pltpu.VMEM((tm, D_MODEL), f32)],
        compiler_params=_params(("arbitrary", "arbitrary")),
    )(dpre, dhg, dhu, wg, wu)


def _mm(a, b, *, mode, out_dtype, tm, tn, tk, name, affine=None, a_cols=None, b_cols=None,
        b_blocked=False, out_blocked=False, out_scale=None):
    if mode == "nn":
        m_full, k_full = a.shape
        m_dim, k_dim = (m_full, a_cols[1]) if a_cols else (m_full, k_full)
    else:
        k_dim, m_full = a.shape
        m_dim = a_cols[1] if a_cols else m_full
    a_off = a_cols[0] if a_cols else 0
    if b_blocked:
        n_dim = b.shape[0] * b.shape[2]
        assert b.shape[2] == tn
    else:
        n_dim = b_cols[1] if b_cols else b.shape[1]
    b_off = b_cols[0] if b_cols else 0
    assert m_dim % tm == 0 and n_dim % tn == 0 and k_dim % tk == 0, (name, m_dim, n_dim, k_dim)
    nk = k_dim // tk

    def body(*refs):
        if affine is not None:
            a_ref, g_ref, s_ref, b_ref, o_ref, acc = refs
        else:
            a_ref, b_ref, o_ref, acc = refs
        k = pl.program_id(2)

        @pl.when(k == 0)
        def _():
            acc[...] = jnp.zeros_like(acc)

        av = a_ref[...]
        if affine is not None:
            av = av * g_ref[...] + s_ref[...]
        av = av.astype(bf16)
        bv = b_ref[...].astype(bf16)
        if mode == "nn":
            acc[...] += jnp.dot(av, bv, preferred_element_type=f32)
        else:
            acc[...] += lax.dot_general(av, bv, _TN, preferred_element_type=f32)

        @pl.when(k == nk - 1)
        def _():
            res = acc[...] if out_scale is None else acc[...] * out_scale
            o_ref[...] = res.astype(out_dtype)

    if mode == "nn":
        a_spec = pl.BlockSpec((tm, tk), lambda i, j, k: (i, k + a_off))
        aff_spec = pl.BlockSpec((1, tk), lambda i, j, k: (0, k + a_off))
    else:
        a_spec = pl.BlockSpec((tk, tm), lambda i, j, k: (k, i + a_off))
        aff_spec = pl.BlockSpec((1, tm), lambda i, j, k: (0, i + a_off))
    if b_blocked:
        b_spec = pl.BlockSpec((None, tk, tn), lambda i, j, k: (j, k, 0))
    else:
        b_spec = pl.BlockSpec((tk, tn), lambda i, j, k: (k, j + b_off))
    if out_blocked:
        o_spec = pl.BlockSpec((None, tm, tn), lambda i, j, k: (j, i, 0))
        o_shape = jax.ShapeDtypeStruct((n_dim // tn, m_dim, tn), out_dtype)
    else:
        o_spec = pl.BlockSpec((tm, tn), lambda i, j, k: (i, j))
        o_shape = jax.ShapeDtypeStruct((m_dim, n_dim), out_dtype)
    in_specs = [a_spec] + ([aff_spec, aff_spec] if affine is not None else []) + [b_spec]
    args = [a] + (list(affine) if affine is not None else []) + [b]
    return pl.pallas_call(
        body, name=name, grid=(m_dim // tm, n_dim // tn, nk), in_specs=in_specs, out_specs=o_spec,
        out_shape=o_shape, scratch_shapes=[pltpu.VMEM((tm, tn), f32)],
        compiler_params=_params(("arbitrary", "arbitrary", "arbitrary")),
    )(*args)


def _mm_tn(a, b, *, out_dtype, tm, mb, tn, nb, tk, name, affine=None, out_blocked=False, out_scale=None,
           pair=False, host=None):
    k_dim, m_dim = a.shape
    multi_b = isinstance(b, (list, tuple))
    b_list = list(b) if multi_b else [b]
    n_dim = nb * tn if multi_b else b.shape[1]
    assert m_dim % (mb * tm) == 0 and n_dim % (nb * tn) == 0 and k_dim % tk == 0, (name, m_dim, n_dim, k_dim)
    nk = k_dim // tk
    grid = (m_dim // (mb * tm), n_dim // (nb * tn), nk)
    if pair:
        assert mb * nb == 4 and grid[0] * grid[1] == 2 and out_dtype == bf16, name

    def body(*refs):
        if pair:
            refs, (acc, send_buf, recv_buf, send_sems, recv_sems) = refs[:-5], refs[-5:]
        else:
            refs, acc = refs[:-1], refs[-1]
        a_ref, o_ref = refs[0], refs[-1]
        if affine is not None:
            g_ref, s_ref = refs[1:3]
        b_refs = refs[3 if affine is not None else 1:-1]
        k = pl.program_id(2)

        @pl.when(k == 0)
        def _():
            acc[...] = jnp.zeros_like(acc)

        av = a_ref[...]
        if affine is not None:
            av = av * g_ref[...] + s_ref[...]
        av = av.astype(bf16)
        if multi_b:
            pieces = [r[...].astype(bf16) for r in b_refs]
        else:
            bv = b_refs[0][...].astype(bf16)
            pieces = [bv[:, jn * tn:(jn + 1) * tn] for jn in range(nb)]
        for im in range(mb):
            a_t = av[:, im * tm:(im + 1) * tm].T
            for jn in range(nb):
                acc[im * nb + jn] += jnp.dot(a_t, pieces[jn], preferred_element_type=f32)

        def scaled(v):
            return v if out_scale is None else v * out_scale

        @pl.when(k == nk - 1)
        def _():
            if pair:
                x, y, c = lax.axis_index("x"), lax.axis_index("y"), lax.axis_index("c")
                window = pl.program_id(0) + pl.program_id(1)
                swaps = []
                for cc in range(2):
                    send_buf[cc] = scaled(acc[2 * cc + 1 - c]).astype(bf16)
                    swaps.append(pltpu.make_async_remote_copy(
                        src_ref=send_buf.at[cc], dst_ref=recv_buf.at[window, cc],
                        send_sem=send_sems.at[2 * window + cc], recv_sem=recv_sems.at[2 * window + cc],
                        device_id=(x, y, 1 - c), device_id_type=MESH_T))
                    swaps[cc].start()
                for cc in range(2):
                    swaps[cc].wait_recv()
                    o_ref[cc] = (scaled(acc[2 * cc + c]) + recv_buf[window, cc].astype(f32)).astype(bf16)
                for cc in range(2):
                    swaps[cc].wait_send()
                return
            for im in range(mb):
                for jn in range(nb):
                    res = scaled(acc[im * nb + jn])
                    if out_blocked:
                        o_ref[jn, im * tm:(im + 1) * tm, :] = res.astype(out_dtype)
                    else:
                        o_ref[im * tm:(im + 1) * tm, jn * tn:(jn + 1) * tn] = res.astype(out_dtype)

    a_spec = pl.BlockSpec((tk, mb * tm), lambda i, j, k: (k, i))
    aff_spec = pl.BlockSpec((1, mb * tm), lambda i, j, k: (0, i))
    if multi_b:
        b_specs = [pl.BlockSpec((tk, tn), lambda i, j, k: (k, 0))] * nb
    else:
        b_specs = [pl.BlockSpec((tk, nb * tn), lambda i, j, k: (k, j))]
    scratch = [pltpu.VMEM((mb * nb, tm, tn), f32)]
    if pair:
        o_spec = pl.BlockSpec((2, tm, tn), lambda i, j, k: (i + j, 0, 0))
        o_shape = jax.ShapeDtypeStruct((4, tm, tn), out_dtype)
        scratch += [pltpu.VMEM((2, tm, tn), bf16), pltpu.VMEM((2, 2, tm, tn), bf16),
                    pltpu.SemaphoreType.DMA((4,)), pltpu.SemaphoreType.DMA((4,))]
    elif out_blocked:
        o_spec = pl.BlockSpec((nb, mb * tm, tn), lambda i, j, k: (j, i, 0))
        o_shape = jax.ShapeDtypeStruct((n_dim // tn, m_dim, tn), out_dtype)
    else:
        o_spec = pl.BlockSpec((mb * tm, nb * tn), lambda i, j, k: (i, j))
        o_shape = jax.ShapeDtypeStruct((m_dim, n_dim), out_dtype)
    in_specs = [a_spec] + ([aff_spec, aff_spec] if affine is not None else []) + b_specs
    args = [a] + (list(affine) if affine is not None else []) + b_list
    res = _hosted_call(
        host, body, name=name, grid=grid, in_specs=in_specs, out_specs=o_spec, out_shape=o_shape,
        scratch_shapes=scratch, compiler_params=_params(("arbitrary", "arbitrary", "arbitrary")),
    )(*args)
    return res[0] if host is None else res


def _in_proj(xhat, g, b, w_in, *, tm, name):
    t = xhat.shape[0]
    n_qkv, n_l = 3 * FOX_W, 2 * LRU_W

    def body(x_ref, g_ref, b_ref, w_ref, qkv_ref, zl_ref, zfg_ref):
        xb = (x_ref[...] * g_ref[...] + b_ref[...]).astype(bf16)
        qkv_ref[...] = jnp.dot(xb, w_ref[:, :n_qkv], preferred_element_type=f32).astype(bf16)
        zl_ref[...] = jnp.dot(xb, w_ref[:, n_qkv:n_qkv + n_l], preferred_element_type=f32)
        zfg_ref[...] = jnp.dot(xb, w_ref[:, n_qkv + n_l:], preferred_element_type=f32)

    row = pl.BlockSpec((1, D_MODEL), lambda i: (0, 0))
    return pl.pallas_call(
        body, name=name, grid=(t // tm,),
        in_specs=[pl.BlockSpec((tm, D_MODEL), lambda i: (i, 0)), row, row,
                  pl.BlockSpec(w_in.shape, lambda i: (0, 0))],
        out_specs=[pl.BlockSpec((tm, n_qkv), lambda i: (i, 0)), pl.BlockSpec((tm, n_l), lambda i: (i, 0)),
                   pl.BlockSpec((tm, LANES), lambda i: (i, 0))],
        out_shape=[jax.ShapeDtypeStruct((t, n_qkv), bf16), jax.ShapeDtypeStruct((t, n_l), f32),
                   jax.ShapeDtypeStruct((t, LANES), f32)],
        compiler_params=_params(("arbitrary",)),
    )(xhat, g, b, w_in)


def _mmln(pairs, *, tm, name, resid=None, resid_scale=1.0, epi=None, ln=None, n_out=D_MODEL):
    t = pairs[0][0].shape[0]
    n_pairs = len(pairs)
    n_resid = 0 if resid is None else len(resid) - 1

    def body(*refs):
        pos = 0
        val = None
        for p in range(n_pairs):
            a_ref, b_ref = refs[pos], refs[pos + 1]
            pos += 2
            av = a_ref[...].astype(bf16)
            bv = b_ref[...].astype(bf16)
            if pairs[p][6] == "nn":
                term = jnp.dot(av, bv, preferred_element_type=f32)
            else:
                term = lax.dot_general(av, bv, _NT, preferred_element_type=f32)
            val = term if val is None else val + term
        if resid is not None:
            if resid[0] == "plain":
                r = refs[pos][...]
            else:
                r = refs[pos][...] * refs[pos + 1][...] + refs[pos + 2][...]
            pos += n_resid
            val = val + resid_scale * r
        if epi is None:
            o_ref = refs[pos]
            o_ref[...] = val.astype(o_ref.dtype)
        elif epi == "ln_fwd":
            xo, rstd = _ln_fwd_tile(val)
            refs[pos][...] = xo
            refs[pos + 1][...] = rstd
        else:
            xh_ref, rs_ref, g_ref, dx_ref, gg_ref, gb_ref = refs[pos:pos + 6]
            dprev, gg, gb = _ln_bwd_tile(val, xh_ref[...], rs_ref[...], g_ref[...])
            dx_ref[...] = dprev
            i = pl.program_id(0)

            @pl.when(i == 0)
            def _():
                gg_ref[...] = gg
                gb_ref[...] = gb

            @pl.when(i > 0)
            def _():
                gg_ref[...] += gg
                gb_ref[...] += gb

    in_specs, args = [], []
    for (a, acb, aw, b, bcb, bw, mode) in pairs:
        in_specs.append(pl.BlockSpec((tm, aw), lambda i, acb=acb: (i, acb)))
        args.append(a)
        if mode == "nn":
            in_specs.append(pl.BlockSpec((aw, n_out), lambda i, bcb=bcb: (bcb, 0)))
        else:
            in_specs.append(pl.BlockSpec((n_out, bw), lambda i, bcb=bcb: (0, bcb)))
        args.append(b)
    tok = pl.BlockSpec((tm, n_out), lambda i: (i, 0))
    row = pl.BlockSpec((1, n_out), lambda i: (0, 0))
    col = pl.BlockSpec((tm, 1), lambda i: (i, 0))
    if resid is not None:
        in_specs += [tok] if resid[0] == "plain" else [tok, row, row]
        args += list(resid[1:])
    if epi is None:
        out_specs, out_shape = tok, jax.ShapeDtypeStruct((t, n_out), f32)
    elif epi == "ln_fwd":
        out_specs = [tok, col]
        out_shape = [jax.ShapeDtypeStruct((t, n_out), f32), jax.ShapeDtypeStruct((t, 1), f32)]
    else:
        in_specs += [tok, col, row]
        args += list(ln)
        out_specs = [tok, row, row]
        out_shape = [jax.ShapeDtypeStruct((t, n_out), f32)] + [jax.ShapeDtypeStruct((1, n_out), f32)] * 2
    return pl.pallas_call(
        body, name=name, grid=(t // tm,), in_specs=in_specs, out_specs=out_specs, out_shape=out_shape,
        compiler_params=_params(("arbitrary",)),
    )(*args)


def _loss_bwd(xhat, rstd, g, b, target, *, tm, name):
    t = xhat.shape[0]

    def body(xh_ref, rs_ref, g_ref, b_ref, tg_ref, dx_ref, sq_ref, gg_ref, gb_ref):
        i = pl.program_id(0)
        xh = xh_ref[...]
        diff = xh * g_ref[...] + b_ref[...] - tg_ref[...]
        sq = jnp.sum(diff * diff, axis=0, keepdims=True)
        dprev, gg, gb = _ln_bwd_tile(diff * (1.0 / D_MODEL), xh, rs_ref[...], g_ref[...])
        dx_ref[...] = dprev

        @pl.when(i == 0)
        def _():
            sq_ref[...] = sq
            gg_ref[...] = gg
            gb_ref[...] = gb

        @pl.when(i > 0)
        def _():
            sq_ref[...] += sq
            gg_ref[...] += gg
            gb_ref[...] += gb

    tok = pl.BlockSpec((tm, D_MODEL), lambda i: (i, 0))
    row = pl.BlockSpec((1, D_MODEL), lambda i: (0, 0))
    return pl.pallas_call(
        body, name=name, grid=(t // tm,),
        in_specs=[tok, pl.BlockSpec((tm, 1), lambda i: (i, 0)), row, row, tok],
        out_specs=[tok, row, row, row],
        out_shape=[jax.ShapeDtypeStruct((t, D_MODEL), f32)] + [jax.ShapeDtypeStruct((1, D_MODEL), f32)] * 3,
        compiler_params=_params(("arbitrary",)),
    )(xhat, rstd, g, b, target)


CUM_TILE = 256


def _tri(n, lower):
    r = lax.broadcasted_iota(jnp.int32, (n, n), 0)
    c = lax.broadcasted_iota(jnp.int32, (n, n), 1)
    return jnp.where((r >= c) if lower else (r <= c), 1.0, 0.0).astype(f32)


def _cum_fwd(zfg, bfg, *, name):
    t = zfg.shape[0]

    def body(z_ref, b_ref, o_ref, carry):
        @pl.when(pl.program_id(0) == 0)
        def _():
            carry[...] = jnp.zeros_like(carry)

        ls = -_softplus(-(z_ref[...] + b_ref[...]))
        c = jnp.dot(_tri(CUM_TILE, True), ls, preferred_element_type=f32,
                    precision=lax.Precision.HIGHEST) + carry[...]
        o_ref[...] = c
        carry[...] = c[CUM_TILE - 1:CUM_TILE, :]

    blk = pl.BlockSpec((CUM_TILE, LANES), lambda i: (i, 0))
    return pl.pallas_call(
        body, name=name, grid=(t // CUM_TILE,),
        in_specs=[blk, pl.BlockSpec((1, LANES), lambda i: (0, 0))], out_specs=blk,
        out_shape=jax.ShapeDtypeStruct((t, LANES), f32), scratch_shapes=[pltpu.VMEM((1, LANES), f32)],
        compiler_params=_params(("arbitrary",)),
    )(zfg, bfg)


def _cum_bwd(dcum_q, dcum_k, zfg, bfg, *, name):
    t = zfg.shape[0]
    n = t // CUM_TILE

    def body(d_ref, d2_ref, z_ref, b_ref, o_ref, s_ref, carry):
        i = pl.program_id(0)

        @pl.when(i == 0)
        def _():
            carry[...] = jnp.zeros_like(carry)

        dls = jnp.dot(_tri(CUM_TILE, False), d_ref[...] + d2_ref[...], preferred_element_type=f32,
                      precision=lax.Precision.HIGHEST) + carry[...]
        carry[...] = dls[0:1, :]
        lane = lax.broadcasted_iota(jnp.int32, (CUM_TILE, LANES), 1)
        dfg = jnp.where(lane < HEADS, dls * _sigmoid(-(z_ref[...] + b_ref[...])), 0.0)
        o_ref[...] = dfg
        tot = jnp.sum(dfg, axis=0, keepdims=True)

        @pl.when(i == 0)
        def _():
            s_ref[...] = tot

        @pl.when(i > 0)
        def _():
            s_ref[...] += tot

    blk = pl.BlockSpec((CUM_TILE, LANES), lambda i: (n - 1 - i, 0))
    row = pl.BlockSpec((1, LANES), lambda i: (0, 0))
    return pl.pallas_call(
        body, name=name, grid=(n,), in_specs=[blk, blk, blk, row], out_specs=[blk, row],
        out_shape=[jax.ShapeDtypeStruct((t, LANES), f32), jax.ShapeDtypeStruct((1, LANES), f32)],
        scratch_shapes=[pltpu.VMEM((1, LANES), f32)],
        compiler_params=_params(("arbitrary",)),
    )(dcum_q, dcum_k, zfg, bfg)


ATT_TILE = 512


ATT_ROWS = 32


def _causal_rows(r, transposed):
    rr = lax.broadcasted_iota(jnp.int32, (ATT_ROWS, ATT_TILE), 0) + r * ATT_ROWS
    cc = lax.broadcasted_iota(jnp.int32, (ATT_ROWS, ATT_TILE), 1)
    return (cc >= rr) if transposed else (rr >= cc)


def _causal(i, j, transposed):
    r = lax.broadcasted_iota(jnp.int32, (ATT_TILE, ATT_TILE), 0)
    c = lax.broadcasted_iota(jnp.int32, (ATT_TILE, ATT_TILE), 1)
    if transposed:
        return (c + i * ATT_TILE) >= (r + j * ATT_TILE)
    return (r + i * ATT_TILE) >= (c + j * ATT_TILE)


def _attn_fwd(qkv, cum, cum_t, *, name, host=None):
    t = qkv.shape[0]
    n = t // ATT_TILE
    tq = ATT_TILE

    def body(q_ref, k_ref, v_ref, cq_ref, ck_ref, o_ref, lse_ref, acc, m_s, l_s, c_s, s_s, p_s):
        i = pl.program_id(0)
        j = pl.program_id(1)

        @pl.when(j == 0)
        def _():
            acc[...] = jnp.zeros_like(acc)
            m_s[...] = jnp.full_like(m_s, NEG_BIG)
            l_s[...] = jnp.zeros_like(l_s)

        def block(masked):
            for h in range(HEADS):
                hs = slice(HEAD_D * h, HEAD_D * (h + 1))
                s_s[...] = lax.dot_general(q_ref[:, hs] * ATT_SCALE, k_ref[:, hs], _NT, preferred_element_type=f32)
                ck = ck_ref[h:h + 1, :]

                def rows_chunk(r, carry):
                    rows = pl.ds(pl.multiple_of(r * ATT_ROWS, ATT_ROWS), ATT_ROWS)
                    s = s_s[rows, :] + (cq_ref[rows, h:h + 1] - ck)
                    if masked:
                        s = jnp.where(_causal_rows(r, False), s, NEG_BIG)
                    m_old = m_s[rows, h:h + 1]
                    m_new = jnp.maximum(m_old, jnp.max(s, axis=-1, keepdims=True))
                    corr = jnp.exp(m_old - m_new)
                    p = jnp.exp(s - m_new)
                    l_s[rows, h:h + 1] = corr * l_s[rows, h:h + 1] + jnp.sum(p, axis=-1, keepdims=True)
                    m_s[rows, h:h + 1] = m_new
                    c_s[rows, h:h + 1] = corr
                    p_s[rows, :] = p.astype(bf16)
                    return carry

                lax.fori_loop(0, tq // ATT_ROWS, rows_chunk, 0, unroll=4)
                acc[:, hs] = c_s[:, h:h + 1] * acc[:, hs] + jnp.dot(p_s[...], v_ref[:, hs],
                                                                   preferred_element_type=f32)

        @pl.when(j < i)
        def _():
            block(False)

        @pl.when(j == i)
        def _():
            block(True)
            lse_ref[...] = jnp.zeros_like(lse_ref)
            for h in range(HEADS):
                hs = slice(HEAD_D * h, HEAD_D * (h + 1))
                l = l_s[:, h:h + 1]
                o_ref[:, hs] = acc[:, hs] / l
                lse_ref[:, h:h + 1] = m_s[:, h:h + 1] + jnp.log(l)

    return _hosted_call(
        host, body, name=name, grid=(n, n),
        in_specs=[pl.BlockSpec((tq, FOX_W), lambda i, j: (i, 0)),
                  pl.BlockSpec((tq, FOX_W), lambda i, j: (jnp.minimum(i, j), 1)),
                  pl.BlockSpec((tq, FOX_W), lambda i, j: (jnp.minimum(i, j), 2)),
                  pl.BlockSpec((tq, LANES), lambda i, j: (i, 0)),
                  pl.BlockSpec((HEADS, tq), lambda i, j: (0, jnp.minimum(i, j)))],
        out_specs=[pl.BlockSpec((tq, FOX_W), lambda i, j: (i, 0)), pl.BlockSpec((tq, LANES), lambda i, j: (i, 0))],
        out_shape=[jax.ShapeDtypeStruct((t, FOX_W), f32), jax.ShapeDtypeStruct((t, LANES), f32)],
        scratch_shapes=[pltpu.VMEM((tq, FOX_W), f32), pltpu.VMEM((tq, LANES), f32), pltpu.VMEM((tq, LANES), f32),
                        pltpu.VMEM((tq, LANES), f32), pltpu.VMEM((tq, tq), f32), pltpu.VMEM((tq, tq), bf16)],
        compiler_params=_params(("arbitrary", "arbitrary")),
    )(qkv, qkv, qkv, cum, cum_t)


def _attn_delta(dmix, o, *, tm, name):
    t = o.shape[0]

    def body(do_ref, o_ref, d_ref):
        r = lax.broadcasted_iota(jnp.int32, (FOX_W, LANES), 0)
        c = lax.broadcasted_iota(jnp.int32, (FOX_W, LANES), 1)
        pick = jnp.where(r // HEAD_D == c, 1.0, 0.0).astype(f32)
        d_ref[...] = jnp.dot(do_ref[...] * o_ref[...], pick, preferred_element_type=f32,
                             precision=lax.Precision.HIGHEST)

    blk = pl.BlockSpec((tm, FOX_W), lambda i: (i, 0))
    return pl.pallas_call(
        body, name=name, grid=(t // tm,), in_specs=[blk, blk],
        out_specs=pl.BlockSpec((tm, LANES), lambda i: (i, 0)),
        out_shape=jax.ShapeDtypeStruct((t, LANES), f32), compiler_params=_params(("arbitrary",)),
    )(dmix, o)


def _attn_dq(qkv, dmix, cum, cum_t, lse, delta, *, name, host=None):
    t = qkv.shape[0]
    n = t // ATT_TILE
    tq = ATT_TILE

    def body(q_ref, k_ref, v_ref, do_ref, cq_ref, ck_ref, lse_ref, dl_ref, dq_ref, dc_ref, acc, dc_acc):
        i = pl.program_id(0)
        j = pl.program_id(1)

        @pl.when(j == 0)
        def _():
            acc[...] = jnp.zeros_like(acc)
            dc_acc[...] = jnp.zeros_like(dc_acc)

        def block(masked):
            mask = _causal(i, j, False) if masked else None
            for h in range(HEADS):
                hs = slice(HEAD_D * h, HEAD_D * (h + 1))
                kh = k_ref[:, hs]
                s = lax.dot_general(q_ref[:, hs] * ATT_SCALE, kh, _NT, preferred_element_type=f32)
                s = s + cq_ref[:, h:h + 1] - ck_ref[h:h + 1, :]
                if masked:
                    s = jnp.where(mask, s, NEG_BIG)
                p = jnp.exp(s - lse_ref[:, h:h + 1])
                dp = lax.dot_general(do_ref[:, hs].astype(bf16), v_ref[:, hs], _NT, preferred_element_type=f32)
                ds = p * (dp - dl_ref[:, h:h + 1])
                acc[:, hs] += jnp.dot(ds.astype(bf16), kh, preferred_element_type=f32)
                dc_acc[:, h:h + 1] += jnp.sum(ds, axis=-1, keepdims=True)

        @pl.when(j < i)
        def _():
            block(False)

        @pl.when(j == i)
        def _():
            block(True)
            dq_ref[...] = (acc[...] * ATT_SCALE).astype(bf16)
            dc_ref[...] = dc_acc[...]

    col = pl.BlockSpec((tq, LANES), lambda i, j: (i, 0))
    return _hosted_call(
        host, body, name=name, grid=(n, n),
        in_specs=[pl.BlockSpec((tq, FOX_W), lambda i, j: (i, 0)),
                  pl.BlockSpec((tq, FOX_W), lambda i, j: (jnp.minimum(i, j), 1)),
                  pl.BlockSpec((tq, FOX_W), lambda i, j: (jnp.minimum(i, j), 2)),
                  pl.BlockSpec((tq, FOX_W), lambda i, j: (i, 0)),
                  col, pl.BlockSpec((HEADS, tq), lambda i, j: (0, jnp.minimum(i, j))), col, col],
        out_specs=[pl.BlockSpec((tq, FOX_W), lambda i, j: (i, 0)), col],
        out_shape=[jax.ShapeDtypeStruct((t, FOX_W), bf16), jax.ShapeDtypeStruct((t, LANES), f32)],
        scratch_shapes=[pltpu.VMEM((tq, FOX_W), f32), pltpu.VMEM((tq, LANES), f32)],
        compiler_params=_params(("arbitrary", "arbitrary")),
    )(qkv, qkv, qkv, dmix, cum, cum_t, lse, delta)


def _attn_dkv(qkv, dmix, cum, cum_t, lse_t, delta_t, *, name):
    t = qkv.shape[0]
    n = t // ATT_TILE
    tk = ATT_TILE

    def body(q_ref, k_ref, v_ref, do_ref, cq_ref, ck_ref, lse_ref, dl_ref, dk_ref, dv_ref, dc_ref, dk_acc, dv_acc, dc_acc):
        j = pl.program_id(0)
        i = pl.program_id(1)

        @pl.when(i == 0)
        def _():
            dk_acc[...] = jnp.zeros_like(dk_acc)
            dv_acc[...] = jnp.zeros_like(dv_acc)
            dc_acc[...] = jnp.zeros_like(dc_acc)

        def block(masked):
            mask = _causal(i, j, True) if masked else None
            for h in range(HEADS):
                hs = slice(HEAD_D * h, HEAD_D * (h + 1))
                qh = q_ref[:, hs]
                doh = do_ref[:, hs].astype(bf16)
                s_t = lax.dot_general(k_ref[:, hs] * ATT_SCALE, qh, _NT, preferred_element_type=f32)
                s_t = s_t + cq_ref[h:h + 1, :] - ck_ref[:, h:h + 1]
                if masked:
                    s_t = jnp.where(mask, s_t, NEG_BIG)
                p_t = jnp.exp(s_t - lse_ref[h:h + 1, :])
                dv_acc[:, hs] += jnp.dot(p_t.astype(bf16), doh, preferred_element_type=f32)
                dp_t = lax.dot_general(v_ref[:, hs], doh, _NT, preferred_element_type=f32)
                ds_t = p_t * (dp_t - dl_ref[h:h + 1, :])
                dk_acc[:, hs] += jnp.dot(ds_t.astype(bf16), qh, preferred_element_type=f32)
                dc_acc[:, h:h + 1] -= jnp.sum(ds_t, axis=-1, keepdims=True)

        @pl.when(i > j)
        def _():
            block(False)

        @pl.when(i == j)
        def _():
            block(True)

        @pl.when(i == n - 1)
        def _():
            dk_ref[...] = (dk_acc[...] * ATT_SCALE).astype(bf16)
            dv_ref[...] = dv_acc[...].astype(bf16)
            dc_ref[...] = dc_acc[...]

    rowq = pl.BlockSpec((HEADS, tk), lambda j, i: (0, jnp.maximum(i, j)))
    return pl.pallas_call(
        body, name=name, grid=(n, n),
        in_specs=[pl.BlockSpec((tk, FOX_W), lambda j, i: (jnp.maximum(i, j), 0)),
                  pl.BlockSpec((tk, FOX_W), lambda j, i: (j, 1)),
                  pl.BlockSpec((tk, FOX_W), lambda j, i: (j, 2)),
                  pl.BlockSpec((tk, FOX_W), lambda j, i: (jnp.maximum(i, j), 0)),
                  rowq, pl.BlockSpec((tk, LANES), lambda j, i: (j, 0)), rowq, rowq],
        out_specs=[pl.BlockSpec((tk, FOX_W), lambda j, i: (j, 0)), pl.BlockSpec((tk, FOX_W), lambda j, i: (j, 0)),
                   pl.BlockSpec((tk, LANES), lambda j, i: (j, 0))],
        out_shape=[jax.ShapeDtypeStruct((t, FOX_W), bf16), jax.ShapeDtypeStruct((t, FOX_W), bf16),
                   jax.ShapeDtypeStruct((t, LANES), f32)],
        scratch_shapes=[pltpu.VMEM((tk, FOX_W), f32), pltpu.VMEM((tk, FOX_W), f32), pltpu.VMEM((tk, LANES), f32)],
        compiler_params=_params(("arbitrary", "arbitrary")),
    )(qkv, qkv, qkv, dmix, cum_t, cum, lse_t, delta_t)


ATT_W = HEADS * LANES


def _data_lane(h):
    return HEAD_D * (h % 2)


def _extra_lane(h):
    return HEAD_D - _data_lane(h)


def _split3(x):
    hi = x.astype(bf16)
    rest = x - hi.astype(f32)
    mid = rest.astype(bf16)
    lo = (rest - mid.astype(f32)).astype(bf16)
    return hi, mid, lo


def _augment(pair, h, first, second, fill=0.0):
    rows = pair.shape[0]
    lane = lax.broadcasted_iota(jnp.int32, (rows, LANES), 1)
    base = _extra_lane(h)
    own = (lane < HEAD_D) if h % 2 == 0 else (lane >= HEAD_D)
    out = jnp.where(own, pair, jnp.full((rows, LANES), fill, bf16))
    for off, src in ((0, first), (3, second)):
        for q in range(3):
            val = src[q] if isinstance(src, tuple) else jnp.full((rows, 1), src, bf16)
            out = jnp.where(lane == base + off + q, val, out)
    return out


def _attn_prep_fwd(qkv, cum, *, tm, name):
    t = qkv.shape[0]

    def body(q_ref, k_ref, v_ref, c_ref, qa_ref, ka_ref, va_ref):
        for h in range(HEADS):
            pair = slice(LANES * (h // 2), LANES * (h // 2 + 1))
            hs = slice(LANES * h, LANES * (h + 1))
            c3 = _split3(c_ref[:, h:h + 1])
            qa_ref[:, hs] = _augment(q_ref[:, pair] * ATT_SCALE, h, c3, 1.0)
            ka_ref[:, hs] = _augment(k_ref[:, pair], h, 1.0, tuple(-p for p in c3))
            va_ref[:, hs] = _augment(v_ref[:, pair], h, 1.0, 1.0, fill=1.0)

    wide = pl.BlockSpec((tm, ATT_W), lambda i: (i, 0))
    out = jax.ShapeDtypeStruct((t, ATT_W), bf16)
    return pl.pallas_call(
        body, name=name, grid=(t // tm,),
        in_specs=[pl.BlockSpec((tm, FOX_W), lambda i: (i, 0)), pl.BlockSpec((tm, FOX_W), lambda i: (i, 1)),
                  pl.BlockSpec((tm, FOX_W), lambda i: (i, 2)), pl.BlockSpec((tm, LANES), lambda i: (i, 0))],
        out_specs=[wide] * 3, out_shape=[out] * 3, compiler_params=_params(("arbitrary",)),
    )(qkv, qkv, qkv, cum)


def _attn_prep_bwd(qkv, cum, lse, dmix, o, *, tm, name):
    t = qkv.shape[0]

    def body(q_ref, c_ref, l_ref, do_ref, o_ref, qa_ref, da_ref):
        for h in range(HEADS):
            pair = slice(LANES * (h // 2), LANES * (h // 2 + 1))
            src = slice(HEAD_D * h, HEAD_D * (h + 1))
            hs = slice(LANES * h, LANES * (h + 1))
            delta = jnp.sum(do_ref[:, src] * o_ref[:, src], axis=-1, keepdims=True)
            qa_ref[:, hs] = _augment(q_ref[:, pair] * ATT_SCALE, h,
                                     _split3(c_ref[:, h:h + 1] - l_ref[:, h:h + 1]), 1.0)
            da_ref[:, hs] = _augment(do_ref[:, pair].astype(bf16), h, tuple(-p for p in _split3(delta)), 0.0)

    wide = pl.BlockSpec((tm, ATT_W), lambda i: (i, 0))
    half = pl.BlockSpec((tm, FOX_W), lambda i: (i, 0))
    col = pl.BlockSpec((tm, LANES), lambda i: (i, 0))
    out = jax.ShapeDtypeStruct((t, ATT_W), bf16)
    return pl.pallas_call(
        body, name=name, grid=(t // tm,), in_specs=[half, col, col, half, half],
        out_specs=[wide] * 2, out_shape=[out] * 2, compiler_params=_params(("arbitrary",)),
    )(qkv, cum, lse, dmix, o)


def _attn_fwd2(q_aug, k_aug, v_aug, *, name, host=None):
    t = q_aug.shape[0]
    n = t // ATT_TILE
    tq = ATT_TILE

    def body(q_ref, k_ref, v_ref, o_ref, lse_ref, acc, m_s):
        i = pl.program_id(0)
        j = pl.program_id(1)

        @pl.when(j == 0)
        def _():
            acc[...] = jnp.zeros_like(acc)
            m_s[...] = jnp.full_like(m_s, NEG_BIG)

        def block(masked):
            mask = _causal(i, j, False) if masked else None
            for h in range(HEADS):
                hs = slice(LANES * h, LANES * (h + 1))
                s = lax.dot_general(q_ref[:, hs], k_ref[:, hs], _NT, preferred_element_type=f32)
                if masked:
                    s = jnp.where(mask, s, NEG_BIG)
                blocks = [s[:, LANES * b:LANES * (b + 1)] for b in range(tq // LANES)]
                m_old = m_s[h]
                m_new = jnp.maximum(m_old, jnp.broadcast_to(
                    jnp.max(functools.reduce(jnp.maximum, blocks), axis=-1, keepdims=True), (tq, LANES)))
                p = jnp.concatenate([jnp.exp(b - m_new) for b in blocks], axis=1).astype(bf16)
                acc[h] = jnp.exp(m_old - m_new) * acc[h] + jnp.dot(p, v_ref[:, hs], preferred_element_type=f32)
                m_s[h] = m_new

        @pl.when(j < i)
        def _():
            block(False)

        @pl.when(j == i)
        def _():
            block(True)
            lse_ref[...] = jnp.zeros_like(lse_ref)
            for h in range(HEADS):
                a = acc[h]
                l = a[:, _extra_lane(h):_extra_lane(h) + 1]
                o_ref[:, HEAD_D * h:HEAD_D * (h + 1)] = a[:, _data_lane(h):_data_lane(h) + HEAD_D] / l
                lse_ref[:, h:h + 1] = m_s[h][:, 0:1] + jnp.log(l)

    kv = pl.BlockSpec((tq, ATT_W), lambda i, j: (jnp.minimum(i, j), 0))
    return _hosted_call(
        host, body, name=name, grid=(n, n),
        in_specs=[pl.BlockSpec((tq, ATT_W), lambda i, j: (i, 0)), kv, kv],
        out_specs=[pl.BlockSpec((tq, FOX_W), lambda i, j: (i, 0)), pl.BlockSpec((tq, LANES), lambda i, j: (i, 0))],
        out_shape=[jax.ShapeDtypeStruct((t, FOX_W), f32), jax.ShapeDtypeStruct((t, LANES), f32)],
        scratch_shapes=[pltpu.VMEM((HEADS, tq, LANES), f32), pltpu.VMEM((HEADS, tq, LANES), f32)],
        compiler_params=_params(("arbitrary", "arbitrary")),
    )(q_aug, k_aug, v_aug)


def _attn_bwd(qb_aug, k_aug, v_aug, do_aug, *, name, host=None):
    t = qb_aug.shape[0]
    n = t // ATT_TILE
    tk = ATT_TILE

    def body(q_ref, k_ref, v_ref, do_ref, dq_ref, dcq_ref, dk_ref, dv_ref, dck_ref, dk_acc, dv_acc, dq_all):
        j = pl.program_id(0)
        i = pl.program_id(1)

        @pl.when(jnp.logical_and(i == 0, j == 0))
        def _():
            dq_all[...] = jnp.zeros_like(dq_all)

        @pl.when(i == 0)
        def _():
            dk_acc[...] = jnp.zeros_like(dk_acc)
            dv_acc[...] = jnp.zeros_like(dv_acc)

        def block(masked):
            mask = _causal(i, j, True) if masked else None
            for h in range(HEADS):
                hs = slice(LANES * h, LANES * (h + 1))
                qh = q_ref[:, hs]
                doh = do_ref[:, hs]
                kh = k_ref[:, hs]
                s_t = lax.dot_general(kh, qh, _NT, preferred_element_type=f32)
                if masked:
                    s_t = jnp.where(mask, s_t, NEG_BIG)
                p_t = jnp.exp(s_t)
                dv_acc[h] += jnp.dot(p_t.astype(bf16), doh, preferred_element_type=f32)
                dp_t = lax.dot_general(v_ref[:, hs], doh, _NT, preferred_element_type=f32)
                ds_t = (p_t * dp_t).astype(bf16)
                dk_acc[h] += jnp.dot(ds_t, qh, preferred_element_type=f32)
                dq_all[i, h] += lax.dot_general(ds_t, kh, _TN, preferred_element_type=f32)

        @pl.when(i > j)
        def _():
            block(False)

        @pl.when(i == j)
        def _():
            block(True)
            dcq_ref[...] = jnp.zeros_like(dcq_ref)
            for h in range(HEADS):
                a = dq_all[j, h]
                dq_ref[:, HEAD_D * h:HEAD_D * (h + 1)] = (
                    a[:, _data_lane(h):_data_lane(h) + HEAD_D] * ATT_SCALE).astype(bf16)
                dcq_ref[:, h:h + 1] = a[:, _extra_lane(h):_extra_lane(h) + 1]

        @pl.when(i == n - 1)
        def _():
            dck_ref[...] = jnp.zeros_like(dck_ref)
            for h in range(HEADS):
                a = dk_acc[h]
                cols = slice(_data_lane(h), _data_lane(h) + HEAD_D)
                dk_ref[:, HEAD_D * h:HEAD_D * (h + 1)] = a[:, cols].astype(bf16)
                dv_ref[:, HEAD_D * h:HEAD_D * (h + 1)] = dv_acc[h][:, cols].astype(bf16)
                dck_ref[:, h:h + 1] = -a[:, _extra_lane(h) + 3:_extra_lane(h) + 4]

    own = pl.BlockSpec((tk, ATT_W), lambda j, i: (j, 0))
    qs = pl.BlockSpec((tk, ATT_W), lambda j, i: (jnp.maximum(i, j), 0))
    half = pl.BlockSpec((tk, FOX_W), lambda j, i: (j, 0))
    col = pl.BlockSpec((tk, LANES), lambda j, i: (j, 0))
    return _hosted_call(
        host, body, name=name, grid=(n, n), in_specs=[qs, own, own, qs],
        out_specs=[half, col, half, half, col],
        out_shape=[jax.ShapeDtypeStruct((t, FOX_W), bf16), jax.ShapeDtypeStruct((t, LANES), f32),
                   jax.ShapeDtypeStruct((t, FOX_W), bf16), jax.ShapeDtypeStruct((t, FOX_W), bf16),
                   jax.ShapeDtypeStruct((t, LANES), f32)],
        scratch_shapes=[pltpu.VMEM((HEADS, tk, LANES), f32), pltpu.VMEM((HEADS, tk, LANES), f32),
                        pltpu.VMEM((n, HEADS, tk, LANES), f32)],
        compiler_params=_params(("arbitrary", "arbitrary")),
    )(qb_aug, k_aug, v_aug, do_aug)


def _attn_dq2(qb_aug, k_aug, v_aug, do_aug, *, name, host=None):
    t = qb_aug.shape[0]
    n = t // ATT_TILE
    tq = ATT_TILE

    def body(q_ref, k_ref, v_ref, do_ref, dq_ref, dc_ref, acc):
        i = pl.program_id(0)
        j = pl.program_id(1)

        @pl.when(j == 0)
        def _():
            acc[...] = jnp.zeros_like(acc)

        def block(masked):
            mask = _causal(i, j, False) if masked else None
            for h in range(HEADS):
                hs = slice(LANES * h, LANES * (h + 1))
                kh = k_ref[:, hs]
                s = lax.dot_general(q_ref[:, hs], kh, _NT, preferred_element_type=f32)
                if masked:
                    s = jnp.where(mask, s, NEG_BIG)
                dp = lax.dot_general(do_ref[:, hs], v_ref[:, hs], _NT, preferred_element_type=f32)
                ds = (jnp.exp(s) * dp).astype(bf16)
                acc[h] += jnp.dot(ds, kh, preferred_element_type=f32)

        @pl.when(j < i)
        def _():
            block(False)

        @pl.when(j == i)
        def _():
            block(True)
            dc_ref[...] = jnp.zeros_like(dc_ref)
            for h in range(HEADS):
                a = acc[h]
                dq_ref[:, HEAD_D * h:HEAD_D * (h + 1)] = (
                    a[:, _data_lane(h):_data_lane(h) + HEAD_D] * ATT_SCALE).astype(bf16)
                dc_ref[:, h:h + 1] = a[:, _extra_lane(h):_extra_lane(h) + 1]

    own = pl.BlockSpec((tq, ATT_W), lambda i, j: (i, 0))
    kv = pl.BlockSpec((tq, ATT_W), lambda i, j: (jnp.minimum(i, j), 0))
    return _hosted_call(
        host, body, name=name, grid=(n, n), in_specs=[own, kv, kv, own],
        out_specs=[pl.BlockSpec((tq, FOX_W), lambda i, j: (i, 0)), pl.BlockSpec((tq, LANES), lambda i, j: (i, 0))],
        out_shape=[jax.ShapeDtypeStruct((t, FOX_W), bf16), jax.ShapeDtypeStruct((t, LANES), f32)],
        scratch_shapes=[pltpu.VMEM((HEADS, tq, LANES), f32)],
        compiler_params=_params(("arbitrary", "arbitrary")),
    )(qb_aug, k_aug, v_aug, do_aug)


def _attn_dkv2(qb_aug, k_aug, v_aug, do_aug, *, name, host=None):
    t = qb_aug.shape[0]
    n = t // ATT_TILE
    tk = ATT_TILE

    def body(q_ref, k_ref, v_ref, do_ref, dk_ref, dv_ref, dc_ref, dk_acc, dv_acc):
        j = pl.program_id(0)
        i = pl.program_id(1)

        @pl.when(i == 0)
        def _():
            dk_acc[...] = jnp.zeros_like(dk_acc)
            dv_acc[...] = jnp.zeros_like(dv_acc)

        def block(masked):
            mask = _causal(i, j, True) if masked else None
            for h in range(HEADS):
                hs = slice(LANES * h, LANES * (h + 1))
                qh = q_ref[:, hs]
                doh = do_ref[:, hs]
                s_t = lax.dot_general(k_ref[:, hs], qh, _NT, preferred_element_type=f32)
                if masked:
                    s_t = jnp.where(mask, s_t, NEG_BIG)
                p_t = jnp.exp(s_t)
                dv_acc[h] += jnp.dot(p_t.astype(bf16), doh, preferred_element_type=f32)
                dp_t = lax.dot_general(v_ref[:, hs], doh, _NT, preferred_element_type=f32)
                dk_acc[h] += jnp.dot((p_t * dp_t).astype(bf16), qh, preferred_element_type=f32)

        @pl.when(i > j)
        def _():
            block(False)

        @pl.when(i == j)
        def _():
            block(True)

        @pl.when(i == n - 1)
        def _():
            dc_ref[...] = jnp.zeros_like(dc_ref)
            for h in range(HEADS):
                a = dk_acc[h]
                cols = slice(_data_lane(h), _data_lane(h) + HEAD_D)
                dk_ref[:, HEAD_D * h:HEAD_D * (h + 1)] = a[:, cols].astype(bf16)
                dv_ref[:, HEAD_D * h:HEAD_D * (h + 1)] = dv_acc[h][:, cols].astype(bf16)
                dc_ref[:, h:h + 1] = -a[:, _extra_lane(h) + 3:_extra_lane(h) + 4]

    own = pl.BlockSpec((tk, ATT_W), lambda j, i: (j, 0))
    qs = pl.BlockSpec((tk, ATT_W), lambda j, i: (jnp.maximum(i, j), 0))
    half = pl.BlockSpec((tk, FOX_W), lambda j, i: (j, 0))
    return _hosted_call(
        host, body, name=name, grid=(n, n), in_specs=[qs, own, own, qs],
        out_specs=[half, half, pl.BlockSpec((tk, LANES), lambda j, i: (j, 0))],
        out_shape=[jax.ShapeDtypeStruct((t, FOX_W), bf16), jax.ShapeDtypeStruct((t, FOX_W), bf16),
                   jax.ShapeDtypeStruct((t, LANES), f32)],
        scratch_shapes=[pltpu.VMEM((HEADS, tk, LANES), f32), pltpu.VMEM((HEADS, tk, LANES), f32)],
        compiler_params=_params(("arbitrary", "arbitrary")),
    )(qb_aug, k_aug, v_aug, do_aug)


LRU_CHUNK = 64
LRU_G = 256
SUB = 8


def _row_ids(n):
    return lax.broadcasted_iota(jnp.int32, (n, LRU_G), 0)


def _shift_rows_down(ext, s):
    return pltpu.roll(ext, s, axis=0)[SUB:, :]


def _shift_rows_up(ext, s, n):
    return pltpu.roll(ext, ext.shape[0] - s, axis=0)[:n, :]


def _lru_gates(u, wa_ref, ba_ref, wx_ref, bx_ref, sp):
    ub = u.astype(bf16)
    r = _sigmoid(jnp.dot(ub, wa_ref[...], preferred_element_type=f32) + ba_ref[...])
    gi = _sigmoid(jnp.dot(ub, wx_ref[...], preferred_element_type=f32) + bx_ref[...])
    log_a = -LRU_C * r * sp
    a = jnp.exp(log_a)
    s = jnp.sqrt(_one_minus_exp(2.0 * log_a))
    return r, gi, a, s


def _conv_window(lx_ref, r0, ci):
    cur = lx_ref[pl.ds(r0, LRU_CHUNK), :]
    p0 = pl.multiple_of(jnp.maximum(r0 - SUB, 0), SUB)
    prev = jnp.where(ci > 0, lx_ref[pl.ds(p0, SUB), :], 0.0)
    return cur, jnp.concatenate([prev, cur], axis=0)


def _lru_fwd(zl, conv_w, conv_b, wa, ba, wx, bx, lam, *, name, host=None):
    t = zl.shape[0]
    n_chunk = t // LRU_CHUNK

    def body(lx_ref, lg_ref, cw_ref, cb_ref, wa_ref, ba_ref, wx_ref, bx_ref, lam_ref, u_ref, h_ref, y_ref):
        sp = _softplus(-lam_ref[...])
        rows = _row_ids(SUB)

        def chunk(ci, hc):
            r0 = pl.multiple_of(ci * LRU_CHUNK, LRU_CHUNK)
            cur, ext = _conv_window(lx_ref, r0, ci)
            u = cb_ref[...] + cw_ref[3:4, :] * cur
            for k in range(3):
                u = u + cw_ref[k:k + 1, :] * _shift_rows_down(ext, 3 - k)
            r, gi, a, s = _lru_gates(u, wa_ref, ba_ref, wx_ref, bx_ref, sp)
            b = s * (gi * u)
            tiles = []
            for q in range(LRU_CHUNK // SUB):
                ta = a[SUB * q:SUB * (q + 1), :]
                tb = b[SUB * q:SUB * (q + 1), :]
                for d in (1, 2, 4):
                    a_sh = jnp.where(rows >= d, pltpu.roll(ta, d, axis=0), 1.0)
                    b_sh = jnp.where(rows >= d, pltpu.roll(tb, d, axis=0), 0.0)
                    tb = ta * b_sh + tb
                    ta = ta * a_sh
                hq = tb + ta * hc
                hc = hq[SUB - 1:SUB, :]
                tiles.append(hq)
            h = jnp.concatenate(tiles, axis=0)
            u_ref[pl.ds(r0, LRU_CHUNK), :] = u
            h_ref[pl.ds(r0, LRU_CHUNK), :] = h
            gel, _ = _gelu_and_grad(lg_ref[pl.ds(r0, LRU_CHUNK), :])
            y_ref[pl.ds(r0, LRU_CHUNK), :] = gel * h
            return hc

        lax.fori_loop(0, n_chunk, chunk, jnp.zeros((1, LRU_G), f32))

    seq = lambda cb: pl.BlockSpec((t, LRU_G), lambda c, cb=cb: (0, c + cb))
    rowc = pl.BlockSpec((1, LRU_G), lambda c: (0, c))
    diag = pl.BlockSpec((LRU_G, LRU_G), lambda c: (c, c))
    out = jax.ShapeDtypeStruct((t, LRU_W), f32)
    return _hosted_call(
        host, body, name=name, grid=(LRU_W // LRU_G,),
        in_specs=[seq(0), seq(LRU_W // LRU_G), pl.BlockSpec((4, LRU_G), lambda c: (0, c)),
                  rowc, diag, rowc, diag, rowc, rowc],
        out_specs=[seq(0)] * 3, out_shape=[out] * 3,
        compiler_params=_params(("arbitrary",)),
    )(zl, zl, conv_w, conv_b, wa, ba, wx, bx, lam)


def _lru_bwd(dmix, zl, u_all, h_all, conv_w, wa, ba, wx, bx, lam, *, name, host=None):
    t = zl.shape[0]
    n_chunk = t // LRU_CHUNK

    def body(dy_ref, lx_ref, lg_ref, u_ref, h_ref, cw_ref, wa_ref, ba_ref, wx_ref, bx_ref, lam_ref,
             dlx_ref, dlg_ref, dcw_ref, dcb_ref, dba_ref, dbx_ref, dlam_ref, dwa_ref, dwx_ref, dpr_s, dpx_s):
        lam_v = lam_ref[...]
        sp = _softplus(-lam_v)
        rows = _row_ids(SUB)
        rows_c = _row_ids(LRU_CHUNK)
        zero_row = jnp.zeros((1, LRU_G), f32)

        def chunk(step, carry):
            dh_c, a_next0, du_next, dsp, dba, dbx, dcb, dw0, dw1, dw2, dw3 = carry
            ci = n_chunk - 1 - step
            r0 = pl.multiple_of(ci * LRU_CHUNK, LRU_CHUNK)
            sl = pl.ds(r0, LRU_CHUNK)
            u = u_ref[sl, :]
            r, gi, a, s = _lru_gates(u, wa_ref, ba_ref, wx_ref, bx_ref, sp)
            h = h_ref[sl, :]
            p0 = pl.multiple_of(jnp.maximum(r0 - SUB, 0), SUB)
            h_before = jnp.where(ci > 0, h_ref[pl.ds(p0, SUB), :], 0.0)[SUB - 1:SUB, :]
            h_prev = jnp.where(rows_c == 0, h_before, pltpu.roll(h, 1, axis=0))
            gel, dgel = _gelu_and_grad(lg_ref[sl, :])
            dy = dy_ref[sl, :]
            dlg_ref[sl, :] = (dy * h * dgel).astype(bf16)
            g_in = dy * gel
            a_next = jnp.where(rows_c == LRU_CHUNK - 1, a_next0, pltpu.roll(a, LRU_CHUNK - 1, axis=0))
            tiles = [None] * (LRU_CHUNK // SUB)
            for q in reversed(range(LRU_CHUNK // SUB)):
                ta = a_next[SUB * q:SUB * (q + 1), :]
                tb = g_in[SUB * q:SUB * (q + 1), :]
                for d in (1, 2, 4):
                    a_sh = jnp.where(rows < SUB - d, pltpu.roll(ta, SUB - d, axis=0), 1.0)
                    b_sh = jnp.where(rows < SUB - d, pltpu.roll(tb, SUB - d, axis=0), 0.0)
                    tb = ta * b_sh + tb
                    ta = ta * a_sh
                dhq = tb + ta * dh_c
                dh_c = dhq[0:1, :]
                tiles[q] = dhq
            dh = jnp.concatenate(tiles, axis=0)
            da = dh * h_prev
            ds = dh * gi * u
            dgi = dh * s * u
            du = dh * s * gi
            dlog_a = da * a - ds * (a * a) / s
            dr = dlog_a * (-LRU_C * sp)
            dsp = dsp + jnp.sum(dlog_a * (-LRU_C * r), axis=0, keepdims=True)
            dpr = dr * r * (1.0 - r)
            dpx = dgi * gi * (1.0 - gi)
            dprb = dpr.astype(bf16)
            dpxb = dpx.astype(bf16)
            dpr_s[sl, :] = dprb
            dpx_s[sl, :] = dpxb
            du = du + (lax.dot_general(dprb, wa_ref[...], _NT, preferred_element_type=f32)
                       + lax.dot_general(dpxb, wx_ref[...], _NT, preferred_element_type=f32))
            dba = dba + jnp.sum(dpr, axis=0, keepdims=True)
            dbx = dbx + jnp.sum(dpx, axis=0, keepdims=True)
            dcb = dcb + jnp.sum(du, axis=0, keepdims=True)
            du_ext = jnp.concatenate([du, du_next], axis=0)
            dlx = cw_ref[3:4, :] * du
            for k in range(3):
                dlx = dlx + cw_ref[k:k + 1, :] * _shift_rows_up(du_ext, 3 - k, LRU_CHUNK)
            dlx_ref[sl, :] = dlx.astype(bf16)
            cur, ext = _conv_window(lx_ref, r0, ci)
            dws = [dw0, dw1, dw2, dw3 + jnp.sum(du * cur, axis=0, keepdims=True)]
            for k in range(3):
                dws[k] = dws[k] + jnp.sum(du * _shift_rows_down(ext, 3 - k), axis=0, keepdims=True)
            return (dh_c, a[0:1, :], du[0:SUB, :], dsp, dba, dbx, dcb, dws[0], dws[1], dws[2], dws[3])

        init = (zero_row, zero_row, jnp.zeros((SUB, LRU_G), f32)) + (zero_row,) * 8
        out = lax.fori_loop(0, n_chunk, chunk, init)
        _, _, _, dsp, dba, dbx, dcb, dw0, dw1, dw2, dw3 = out
        dlam_ref[...] = dsp * (-_sigmoid(-lam_v))
        dba_ref[...] = dba
        dbx_ref[...] = dbx
        dcb_ref[...] = dcb
        dcw_ref[...] = jnp.concatenate([dw0, dw1, dw2, dw3], axis=0)
        ub = u_ref[...].astype(bf16)
        dwa_ref[...] = lax.dot_general(ub, dpr_s[...], _TN, preferred_element_type=f32)
        dwx_ref[...] = lax.dot_general(ub, dpx_s[...], _TN, preferred_element_type=f32)

    seq = lambda cb: pl.BlockSpec((t, LRU_G), lambda c, cb=cb: (0, c + cb))
    rowc = pl.BlockSpec((1, LRU_G), lambda c: (0, c))
    diag = pl.BlockSpec((LRU_G, LRU_G), lambda c: (c, c))
    gate_out = pl.BlockSpec((None, LRU_G, LRU_G), lambda c: (c, 0, 0))
    row_shape = jax.ShapeDtypeStruct((1, LRU_W), f32)
    return _hosted_call(
        host, body, name=name, grid=(LRU_W // LRU_G,),
        in_specs=[seq(LRU_W // LRU_G), seq(0), seq(LRU_W // LRU_G), seq(0), seq(0),
                  pl.BlockSpec((4, LRU_G), lambda c: (0, c)),
                  diag, rowc, diag, rowc, rowc],
        out_specs=[seq(0), seq(0), pl.BlockSpec((4, LRU_G), lambda c: (0, c)), rowc, rowc, rowc, rowc,
                   gate_out, gate_out],
        out_shape=[jax.ShapeDtypeStruct((t, LRU_W), bf16)] * 2
        + [jax.ShapeDtypeStruct((4, LRU_W), f32)] + [row_shape] * 4
        + [jax.ShapeDtypeStruct((LRU_W // LRU_G, LRU_G, LRU_G), f32)] * 2,
        scratch_shapes=[pltpu.VMEM((t, LRU_G), bf16), pltpu.VMEM((t, LRU_G), bf16)],
        compiler_params=_params(("arbitrary",)),
    )(dmix, zl, zl, u_all, h_all, conv_w, wa, ba, wx, bx, lam)


def _block_diag(w):
    eye = jnp.eye(HEADS, dtype=w.dtype)
    return jnp.einsum("hij,hk->hikj", w, eye).reshape(LRU_W, LRU_W)


def _diag_blocks(dw):
    per = dw.shape[1] // HEAD_D
    blocks = [dw[:, HEAD_D * b:HEAD_D * (b + 1), HEAD_D * b:HEAD_D * (b + 1)] for b in range(per)]
    return jnp.stack(blocks, axis=1).reshape(HEADS, HEAD_D, HEAD_D)


def _local_step(x, target, sent, small, *, tm=512, tm_ffn=1024):
    t = x.shape[0]
    ones = jnp.ones((1, D_MODEL), f32)
    zeros = jnp.zeros((1, D_MODEL), f32)
    ln1 = (small["ln1_g"], small["ln1_b"])
    ln2 = (small["ln2_g"], small["ln2_b"])
    ln3 = (small["ln3_g"], small["ln3_b"])

    wg1, wu1, wd1 = _exchange([sent["ffn1_w_gate"], sent["ffn1_w_up"], sent["ffn1_w_down"]], gather=True,
                              name="gather_ffn1")
    xh1, rs1, hg1, hu1, w_in_g, w_out_g, conv_w_g = _ffn_fwd(
        x, ones, zeros, wg1, wu1, wd1, tm=tm_ffn, name="ffn1_fwd",
        host=_Exchange([sent["w_in"], sent["w_out"], sent["conv_w"]], gather=True))
    w_in = jnp.pad(w_in_g.transpose(1, 0, 2).reshape(D_MODEL, IN_COLS), ((0, 0), (0, 21 * LANES - IN_COLS)))
    w_out = w_out_g.reshape(D_MODEL, D_MODEL)
    conv_w = conv_w_g.transpose(1, 0, 2).reshape(4, LRU_W)
    qkv, zl, zfg = _in_proj(xh1, ln1[0], ln1[1], w_in, tm=tm, name="in_proj")
    bfg = jnp.pad(small["b_forget"], ((0, 0), (0, LANES - HEADS)))
    cum = _cum_fwd(zfg, bfg, name="cum_fwd")
    q_aug, k_aug, v_aug = _attn_prep_fwd(qkv, cum, tm=tm, name="attn_prep_fwd")
    o, lse, wg2, wu2 = _attn_fwd2(q_aug, k_aug, v_aug, name="attn_fwd",
                                  host=_Exchange([sent["ffn2_w_gate"], sent["ffn2_w_up"]], gather=True))
    wa_bd = _block_diag(small["rg_wa"]).astype(bf16)
    wx_bd = _block_diag(small["rg_wx"]).astype(bf16)
    ba = small["rg_ba"].reshape(1, LRU_W)
    bx = small["rg_bx"].reshape(1, LRU_W)
    u, h, lru, wd2 = _lru_fwd(zl, conv_w, small["conv_b"], wa_bd, ba, wx_bd, bx, small["lru_lambda"],
                              name="lru_fwd", host=_Exchange([sent["ffn2_w_down"]], gather=True))
    xh2, rs2 = _mmln([(o, 0, FOX_W, w_out, 0, D_MODEL, "nn"), (lru, 0, LRU_W, w_out, 1, D_MODEL, "nn")],
                     tm=tm, name="mix_fwd", resid=("affine", xh1) + ln1, resid_scale=ALPHA, epi="ln_fwd")
    xh3, rs3, hg2, hu2 = _ffn_fwd(xh2, ln2[0], ln2[1], wg2, wu2, wd2, tm=tm_ffn, name="ffn2_fwd")

    dpre3, sq_rows, g_ln3g, g_ln3b = _loss_bwd(xh3, rs3, ln3[0], ln3[1], target, tm=tm, name="loss_bwd")
    dpre2, g_ln2g, g_ln2b, dhg2, dhu2, a2 = _ffn_bwd(dpre3, hg2, hu2, wg2, wu2, wd2,
                                                     (xh2, rs2, ln2[0]), tm=tm_ffn, name="ffn2_bwd")
    wgrad = dict(out_dtype=bf16, tm=D_MODEL, mb=1, tn=FF_TILE, nb=4, tk=512, pair=True)
    wdgrad = dict(out_dtype=bf16, tm=512, mb=4, tn=D_MODEL, nb=1, tk=512, out_scale=0.5, pair=True)
    between_chips = functools.partial(_Exchange, gather=False, chips=True)
    g_wg2 = _mm_tn(xh2, dhg2, name="g_wg2", affine=ln2, **wgrad)
    g_wu2 = _mm_tn(xh2, dhu2, name="g_wu2", affine=ln2, **wgrad)
    g_wd2 = _mm_tn(a2, dpre3, name="g_wd2", **wdgrad)

    dmix = _mmln([(dpre2, 0, D_MODEL, w_out, 0, D_MODEL, "nt")], tm=tm, name="dmix_bwd")
    g_wout_a = _mm(o, dpre2, mode="tn", out_dtype=bf16, tm=512, tn=D_MODEL, tk=512, name="g_wout_fox")
    g_wout_b = _mm(lru, dpre2, mode="tn", out_dtype=bf16, tm=512, tn=D_MODEL, tk=512, name="g_wout_lru")
    dlx, dlg, g_cw, g_cb, g_ba, g_bx, g_lam, g_wa4, g_wx4, *p_wg2 = _lru_bwd(
        dmix, zl, u, h, conv_w, wa_bd, ba, wx_bd, bx, small["lru_lambda"], name="lru_bwd",
        host=between_chips([g_wg2]))
    p_wg2 = p_wg2[0]
    qb_aug, do_aug = _attn_prep_bwd(qkv, cum, lse, dmix, o, tm=tm, name="attn_prep_bwd")
    dq, dcum_q, dk, dv, dcum_k, p_wu2, p_wd2 = _attn_bwd(qb_aug, k_aug, v_aug, do_aug, name="attn_bwd",
                                                         host=between_chips([g_wu2, g_wd2]))
    g_wout_blocked = jnp.concatenate([g_wout_a, g_wout_b], axis=0).reshape(N_DEV, D_MODEL // N_DEV, D_MODEL)
    dfg, g_bf = _cum_bwd(dcum_q, dcum_k, zfg, bfg, name="cum_bwd")

    dz = [(dq, 0, 512), (dk, 1, 512), (dv, 2, 512), (dlx, 3, 512), (dlg, 4, 512), (dfg, 20, LANES)]
    dpre1, g_ln1g, g_ln1b = _mmln(
        [(arr, 0, w, w_in, cb, w, "nt") for (arr, cb, w) in dz],
        tm=tm, name="dx1_bwd", resid=("plain", dpre2), resid_scale=ALPHA, epi="ln_bwd", ln=(xh1, rs1, ln1[0]))
    g_win_main = _mm_tn(xh1, [arr for arr, _, _ in dz[:5]], out_dtype=bf16, tm=D_MODEL, mb=1, tn=512, nb=5, tk=512,
                        name="g_win", affine=ln1, out_blocked=True)
    g_win = [g_win_main[n] for n in range(5)] + [
        _mm(xh1, dfg, mode="tn", out_dtype=bf16, tm=D_MODEL, tn=LANES, tk=512, name="g_win_fg", affine=ln1)]
    g_win_full = jnp.concatenate([g[:, :w] for g, (_, _, w) in zip(g_win, dz)], axis=1)[:, :IN_COLS]
    g_win_blocked = g_win_full.reshape(D_MODEL, N_DEV, IN_SHARD).transpose(1, 0, 2)
    dhg1, dhu1, a1, p_win, p_wout = _ffn_bwd_act(dpre1, hg1, hu1, wd1, tm=tm_ffn, name="ffn1_bwd_act",
                                                 host=_Exchange([g_win_blocked, g_wout_blocked], gather=False))
    small_g = {
        "ln1_g": g_ln1g, "ln1_b": g_ln1b, "b_forget": g_bf[:, :HEADS], "conv_w": g_cw, "conv_b": g_cb,
        "rg_wa": _diag_blocks(g_wa4), "rg_ba": g_ba.reshape(HEADS, HEAD_D),
        "rg_wx": _diag_blocks(g_wx4), "rg_bx": g_bx.reshape(HEADS, HEAD_D), "lru_lambda": g_lam,
        "ln2_g": g_ln2g, "ln2_b": g_ln2b, "ln3_g": g_ln3g, "ln3_b": g_ln3b,
    }
    small_g["loss"] = (0.5 / D_MODEL) * jnp.sum(sq_rows, keepdims=True)
    pieces = [small_g[n].reshape(-1) for n in PACKED]
    packed = jnp.concatenate(pieces + [jnp.zeros((PACK_ROWS * LANES - sum(p.shape[0] for p in pieces),), f32)])
    g_wg1, all_packed = _mm_tn(x, dhg1, name="g_wg1",
                               host=_Exchange([packed.reshape(PACK_ROWS, LANES)], gather=True), **wgrad)
    g_wu1, p_wg1 = _mm_tn(x, dhu1, name="g_wu1", host=between_chips([g_wg1]), **wgrad)
    g_wd1, p_wu1 = _mm_tn(a1, dpre1, name="g_wd1", host=between_chips([g_wu1]), **wdgrad)
    grad_x, p_wd1 = _ffn_bwd_dx(dpre1, dhg1, dhu1, wg1, wu1, tm=tm_ffn, name="ffn1_bwd_dx",
                                host=between_chips([g_wd1]))
    parts = {
        "ffn1_w_gate": p_wg1, "ffn1_w_up": p_wu1, "ffn1_w_down": p_wd1, "w_in": p_win, "w_out": p_wout,
        "ffn2_w_gate": p_wg2, "ffn2_w_up": p_wu2, "ffn2_w_down": p_wd2,
    }
    return sq_rows, grad_x, parts, all_packed, {n: small_g[n].shape for n in PACKED}


def _adam_math(w, g, m, v):
    m2 = ADAM_B1 * m + (1.0 - ADAM_B1) * g
    v2 = ADAM_B2 * v + (1.0 - ADAM_B2) * (g * g)
    m_hat = m2 / (1.0 - ADAM_B1 ** ADAM_STEP)
    v_hat = v2 / (1.0 - ADAM_B2 ** ADAM_STEP)
    delta = -ADAM_LR * (m_hat / (jnp.sqrt(v_hat) + ADAM_EPS) + ADAM_WD * w)
    return delta, m2, v2


ADAM_TILE_ELEMS = 128 * 1024


def _adamw_big(parts, w, m, v, *, name):
    r, c = w.shape
    n_parts = parts.shape[0]
    tr = max(d for d in range(8, r + 1, 8) if r % d == 0 and d * c <= ADAM_TILE_ELEMS)

    def body(p_ref, w_ref, m_ref, v_ref, g_ref, d_ref, m2_ref, v2_ref):
        g = p_ref[0].astype(f32)
        for q in range(1, n_parts):
            g = g + p_ref[q].astype(f32)
        d, m2, v2 = _adam_math(w_ref[...], g, m_ref[...], v_ref[...])
        g_ref[...] = g
        d_ref[...] = d
        m2_ref[...] = m2
        v2_ref[...] = v2

    blk = pl.BlockSpec((tr, c), lambda i: (i, 0))
    return pl.pallas_call(
        body, name=name, grid=(r // tr,),
        in_specs=[pl.BlockSpec((n_parts, tr, c), lambda i: (0, i, 0)), blk, blk, blk],
        out_specs=[blk] * 4, out_shape=[jax.ShapeDtypeStruct((r, c), f32)] * 4,
        compiler_params=_params(("arbitrary",)),
    )(parts, w, m, v)


def _adamw_small(items, *, name):
    n = len(items)

    def body(*refs):
        ins, outs = refs[:4 * n], refs[4 * n:]
        for k in range(n):
            g, w, m, v = (ins[4 * k + q][...] for q in range(4))
            d, m2, v2 = _adam_math(w, g, m, v)
            outs[3 * k][...] = d
            outs[3 * k + 1][...] = m2
            outs[3 * k + 2][...] = v2

    vm = pl.BlockSpec(memory_space=pltpu.VMEM)
    flat = [a for item in items for a in item]
    out_shape = [jax.ShapeDtypeStruct(item[1].shape, f32) for item in items for _ in range(3)]
    return pl.pallas_call(
        body, name=name, in_specs=[vm] * (4 * n), out_specs=[vm] * (3 * n), out_shape=out_shape,
    )(*flat)


def _sum_parts(parts, *, name):
    def body(p_ref, o_ref):
        acc = p_ref[0]
        for q in range(1, N_DEV):
            acc = acc + p_ref[q]
        o_ref[...] = acc

    vm = pl.BlockSpec(memory_space=pltpu.VMEM)
    return pl.pallas_call(
        body, name=name, in_specs=[vm], out_specs=vm, out_shape=jax.ShapeDtypeStruct(parts.shape[1:], f32),
    )(parts)


WEIGHTS = ["ffn1_w_gate", "ffn1_w_up", "ffn1_w_down", "ln1_g", "ln1_b", "w_in", "b_forget", "conv_w", "conv_b",
           "rg_wa", "rg_ba", "rg_wx", "rg_bx", "lru_lambda", "w_out", "ln2_g", "ln2_b",
           "ffn2_w_gate", "ffn2_w_up", "ffn2_w_down", "ln3_g", "ln3_b"]
BIG = ["ffn1_w_gate", "ffn1_w_up", "ffn1_w_down", "w_in", "w_out", "ffn2_w_gate", "ffn2_w_up", "ffn2_w_down"]
PACKED = ["ln1_g", "ln1_b", "ln2_g", "ln2_b", "ln3_g", "ln3_b", "conv_b", "rg_ba", "rg_bx", "lru_lambda",
          "conv_w", "rg_wa", "rg_wx", "b_forget", "loss"]
PACK_ROWS = 600


def _two_d(a):
    return a.reshape((-1, a.shape[-1]))


def _transport(a):
    return _two_d(a)


def kernel(x, ffn1_w_gate, ffn1_w_up, ffn1_w_down, ln1_g, ln1_b, w_in, b_forget, conv_w, conv_b, rg_wa, rg_ba, rg_wx, rg_bx, lru_lambda, w_out, ln2_g, ln2_b, ffn2_w_gate, ffn2_w_up, ffn2_w_down, ln3_g, ln3_b, loss_target, m_ffn1_w_gate, m_ffn1_w_up, m_ffn1_w_down, m_ln1_g, m_ln1_b, m_w_in, m_b_forget, m_conv_w, m_conv_b, m_rg_wa, m_rg_ba, m_rg_wx, m_rg_bx, m_lru_lambda, m_w_out, m_ln2_g, m_ln2_b, m_ffn2_w_gate, m_ffn2_w_up, m_ffn2_w_down, m_ln3_g, m_ln3_b, v_ffn1_w_gate, v_ffn1_w_up, v_ffn1_w_down, v_ln1_g, v_ln1_b, v_w_in, v_b_forget, v_conv_w, v_conv_b, v_rg_wa, v_rg_ba, v_rg_wx, v_rg_bx, v_lru_lambda, v_w_out, v_ln2_g, v_ln2_b, v_ffn2_w_gate, v_ffn2_w_up, v_ffn2_w_down, v_ln3_g, v_ln3_b):
    w_args = (ffn1_w_gate, ffn1_w_up, ffn1_w_down, ln1_g, ln1_b, w_in, b_forget, conv_w, conv_b, rg_wa, rg_ba, rg_wx, rg_bx, lru_lambda, w_out, ln2_g, ln2_b, ffn2_w_gate, ffn2_w_up, ffn2_w_down, ln3_g, ln3_b)
    m_args = (m_ffn1_w_gate, m_ffn1_w_up, m_ffn1_w_down, m_ln1_g, m_ln1_b, m_w_in, m_b_forget, m_conv_w, m_conv_b, m_rg_wa, m_rg_ba, m_rg_wx, m_rg_bx, m_lru_lambda, m_w_out, m_ln2_g, m_ln2_b, m_ffn2_w_gate, m_ffn2_w_up, m_ffn2_w_down, m_ln3_g, m_ln3_b)
    v_args = (v_ffn1_w_gate, v_ffn1_w_up, v_ffn1_w_down, v_ln1_g, v_ln1_b, v_w_in, v_b_forget, v_conv_w, v_conv_b, v_rg_wa, v_rg_ba, v_rg_wx, v_rg_bx, v_lru_lambda, v_w_out, v_ln2_g, v_ln2_b, v_ffn2_w_gate, v_ffn2_w_up, v_ffn2_w_down, v_ln3_g, v_ln3_b)
    w = dict(zip(WEIGHTS, w_args))
    m = dict(zip(WEIGHTS, m_args))
    v = dict(zip(WEIGHTS, v_args))
    me = 4 * lax.axis_index("x") + 2 * lax.axis_index("y") + lax.axis_index("c")

    sent = {n: _transport(w[n]).astype(bf16) for n in BIG}
    sent["conv_w"] = _two_d(w["conv_w"])
    small = {n: w[n] for n in ("ln1_g", "ln1_b", "ln2_g", "ln2_b", "ln3_g", "ln3_b", "b_forget", "conv_b",
                               "lru_lambda")}
    small.update({n: w[n][0] for n in ("rg_wa", "rg_ba", "rg_wx", "rg_bx")})

    sq_rows, grad_x, parts, all_packed, small_shapes = _local_step(x[0], loss_target[0], sent, small)

    total = _sum_parts(all_packed, name="sum_small_grads").reshape(-1)
    grads, off = {}, 0
    for n in PACKED:
        size = math.prod(small_shapes[n])
        grads[n] = total[off:off + size].reshape(small_shapes[n])
        off += size
    loss = grads.pop("loss").reshape(())
    grads["conv_w"] = lax.dynamic_slice_in_dim(grads["conv_w"], me * (LRU_W // N_DEV), LRU_W // N_DEV, axis=1)

    delta, new_m, new_v = {}, {}, {}
    for n in BIG:
        g, d, m2, v2 = _adamw_big(parts[n], _transport(w[n]), _transport(m[n]), _transport(v[n]),
                                  name="adamw_" + n)
        grads[n], delta[n], new_m[n], new_v[n] = g, d, m2, v2
    small_names = [n for n in WEIGHTS if n not in BIG]
    outs = _adamw_small([(_two_d(grads[n]), _two_d(w[n]), _two_d(m[n]), _two_d(v[n])) for n in small_names],
                        name="adamw_small")
    for k, n in enumerate(small_names):
        delta[n], new_m[n], new_v[n] = outs[3 * k], outs[3 * k + 1], outs[3 * k + 2]

    def shaped(d):
        return [d[n].reshape(w[n].shape) for n in WEIGHTS]

    return (loss, grad_x[None], *shaped(grads), *shaped(delta), *shaped(new_m), *shaped(new_v))
```

```python
import functools
import math

import jax
import jax.numpy as jnp
from jax import lax
from jax.experimental import pallas as pl
from jax.experimental.pallas import tpu as pltpu

f32 = jnp.float32
bf16 = jnp.bfloat16

N_DEV = 8
D_MODEL = 1024
D_FF = 4096
FF_TILE = D_FF // N_DEV
FOX_W = 512
LRU_W = 512
HEADS = 8
HEAD_D = 64
IN_COLS = 2568
IN_SHARD = IN_COLS // N_DEV
LANES = 128
LN_EPS = 1e-5
ALPHA = 2.0 ** 0.25
ATT_SCALE = 1.0 / math.sqrt(HEAD_D)
LRU_C = 8.0
NEG_BIG = -1e30

ADAM_LR = 0.001
ADAM_B1 = 0.9
ADAM_B2 = 0.999
ADAM_EPS = 1e-08
ADAM_WD = 0.01
ADAM_STEP = 10

VMEM_LIMIT = 56 * 1024 * 1024
MESH_T = pl.DeviceIdType.MESH


def _params(sem, **kw):
    return pltpu.CompilerParams(dimension_semantics=sem, vmem_limit_bytes=VMEM_LIMIT, **kw)


def _sigmoid(x):
    return 1.0 / (1.0 + jnp.exp(-x))


def _sigmoid_tanh(x):
    return 0.5 * jnp.tanh(0.5 * x) + 0.5


def _softplus(x):
    return jnp.maximum(x, 0.0) + jnp.log(1.0 + jnp.exp(-jnp.abs(x)))


def _one_minus_exp(x):
    series = -x * (1.0 + x * (0.5 + x * (1.0 / 6 + x * (1.0 / 24 + x * (1.0 / 120 + x * (1.0 / 720))))))
    return jnp.where(x > -0.125, series, 1.0 - jnp.exp(x))


_GELU_C = math.sqrt(2.0 / math.pi)


def _gelu_and_grad(x):
    inner = _GELU_C * (x + 0.044715 * x * x * x)
    t = jnp.tanh(inner)
    g = 0.5 * x * (1.0 + t)
    dg = 0.5 * (1.0 + t) + 0.5 * x * (1.0 - t * t) * _GELU_C * (1.0 + 3 * 0.044715 * x * x)
    return g, dg


def _ln_fwd_tile(pre):
    mu = jnp.mean(pre, axis=-1, keepdims=True)
    xc = pre - mu
    var = jnp.mean(xc * xc, axis=-1, keepdims=True)
    rstd = lax.rsqrt(var + LN_EPS)
    return xc * rstd, rstd


def _ln_bwd_tile(dy, xhat, rstd, g):
    dyg = dy * g
    m1 = jnp.mean(dyg, axis=-1, keepdims=True)
    m2 = jnp.mean(dyg * xhat, axis=-1, keepdims=True)
    dpre = rstd * (dyg - m1 - xhat * m2)
    return dpre, jnp.sum(dy * xhat, axis=0, keepdims=True), jnp.sum(dy, axis=0, keepdims=True)


_NT = (((1,), (1,)), ((), ()))
_TN = (((0,), (0,)), ((), ()))


class _Exchange:
    def __init__(self, arrs, gather, chips=False):
        self.arrs, self.gather, self.n, self.chips = list(arrs), gather, len(arrs), chips

    def out_shape(self):
        return [jax.ShapeDtypeStruct(((N_DEV,) + a.shape) if self.gather else a.shape, a.dtype) for a in self.arrs]

    def scratch(self):
        n_remote = self.n * (N_DEV - 1)
        return [pltpu.SemaphoreType.DMA((n_remote,)), pltpu.SemaphoreType.DMA((n_remote,)),
                pltpu.SemaphoreType.DMA((self.n,))]

    def copies(self, ins, outs, sems):
        send_sems, recv_sems, local_sems = sems
        x, y, c = lax.axis_index("x"), lax.axis_index("y"), lax.axis_index("c")
        me = 2 * x + y if self.chips else 4 * x + 2 * y + c
        out = []
        for k in range(self.n):
            for d in (range(2, N_DEV, 2) if self.chips else range(1, N_DEV)):
                px = 1 - x if d & 4 else x
                py = 1 - y if d & 2 else y
                pc = 1 - c if d & 1 else c
                sem = k * (N_DEV - 1) + d - 1
                out.append(pltpu.make_async_remote_copy(
                    src_ref=ins[k].at[2 * px + py if self.chips else 4 * px + 2 * py + pc], dst_ref=outs[k].at[me],
                    send_sem=send_sems.at[sem], recv_sem=recv_sems.at[sem],
                    device_id=(px, py, pc), device_id_type=MESH_T))
            out.append(pltpu.make_async_copy(ins[k].at[me], outs[k].at[me], local_sems.at[k]))
        return out

    def gather_copies(self, ins, outs, sems):
        send_sems, recv_sems, local_sems = sems
        x, y, c = lax.axis_index("x"), lax.axis_index("y"), lax.axis_index("c")
        sibling = (x, y, 1 - c)
        chips = [(1 - x, y), (x, 1 - y), (1 - x, 1 - y)]
        out = []
        for k in range(self.n):
            def copy(s, block, to, src=None, k=k):
                rows = outs[k].at[4 * block[0] + 2 * block[1] + block[2]]
                sem = k * (N_DEV - 1) + s
                return pltpu.make_async_remote_copy(
                    src_ref=rows if src is None else src, dst_ref=rows, send_sem=send_sems.at[sem],
                    recv_sem=recv_sems.at[sem], device_id=to, device_id_type=MESH_T)

            first = [copy(0, (x, y, c), sibling, src=ins[k])]
            first += [copy(1 + q, (x, y, c), (*chip, c), src=ins[k]) for q, chip in enumerate(chips)]
            passed = [copy(4 + q, (*chip, c), sibling) for q, chip in enumerate(chips)]
            own = pltpu.make_async_copy(ins[k], outs[k].at[4 * x + 2 * y + c], local_sems.at[k])
            out.append((first, passed, own, copy))
        return out, sibling, chips, (x, y, c)

    def start(self, ins, outs, sems):
        if not self.gather:
            for cp in self.copies(ins, outs, sems):
                cp.start()
            return
        per_array, _, _, _ = self.gather_copies(ins, outs, sems)
        for first, _, own, _ in per_array:
            own.start()
            for cp in first:
                cp.start()

    def relay(self, ins, outs, sems):
        per_array, sibling, chips, (x, y, c) = self.gather_copies(ins, outs, sems)
        for first, passed, own, copy in per_array:
            for q, chip in enumerate(chips):
                copy(1 + q, (*chip, c), (x, y, c)).wait_recv()
                passed[q].start()

    def wait(self, ins, outs, sems, relayed=False):
        if not self.gather:
            for cp in self.copies(ins, outs, sems):
                cp.wait()
            return
        if not relayed:
            self.relay(ins, outs, sems)
        per_array, sibling, chips, (x, y, c) = self.gather_copies(ins, outs, sems)
        for first, passed, own, copy in per_array:
            copy(0, sibling, (x, y, c)).wait_recv()
            for q, chip in enumerate(chips):
                copy(4 + q, (*chip, 1 - c), (x, y, c)).wait_recv()
            for cp in first + passed:
                cp.wait_send()
            own.wait()


def _hosted_call(host, body, *, name, grid, in_specs, out_specs, out_shape, scratch_shapes=(), compiler_params):
    out_specs = list(out_specs) if isinstance(out_specs, (list, tuple)) else [out_specs]
    out_shape = list(out_shape) if isinstance(out_shape, (list, tuple)) else [out_shape]
    if host is None:
        return pl.pallas_call(body, name=name, grid=grid, in_specs=in_specs, out_specs=out_specs,
                              out_shape=out_shape, scratch_shapes=list(scratch_shapes),
                              compiler_params=compiler_params)
    n_in, n_out, n_scr, k = len(in_specs), len(out_shape), len(scratch_shapes), host.n

    def wrapped(*refs):
        ins, h_in = refs[:n_in], refs[n_in:n_in + k]
        outs, h_out = refs[n_in + k:n_in + k + n_out], refs[n_in + k + n_out:n_in + 2 * k + n_out]
        scr, sems = refs[n_in + 2 * k + n_out:n_in + 2 * k + n_out + n_scr], refs[n_in + 2 * k + n_out + n_scr:]
        ids = [pl.program_id(a) for a in range(len(grid))]
        first = functools.reduce(jnp.logical_and, [i == 0 for i in ids])
        last = functools.reduce(jnp.logical_and, [i == g - 1 for i, g in zip(ids, grid)])
        steps = math.prod(grid)
        relay_at = (3 * steps) // 4 if host.gather and steps > 1 else None

        @pl.when(first)
        def _():
            host.start(h_in, h_out, sems)

        if relay_at is not None:
            coords, rest = [], relay_at
            for g in reversed(grid):
                coords.append(rest % g)
                rest //= g

            @pl.when(functools.reduce(jnp.logical_and, [i == cd for i, cd in zip(ids, reversed(coords))]))
            def _():
                host.relay(h_in, h_out, sems)

        body(*ins, *outs, *scr)

        @pl.when(last)
        def _():
            host.wait(h_in, h_out, sems, relayed=relay_at is not None)

    hbm = pl.BlockSpec(memory_space=pl.ANY)
    call = pl.pallas_call(
        wrapped, name=name, grid=grid, in_specs=list(in_specs) + [hbm] * k, out_specs=out_specs + [hbm] * k,
        out_shape=out_shape + host.out_shape(), scratch_shapes=list(scratch_shapes) + host.scratch(),
        compiler_params=compiler_params)
    return lambda *args: call(*args, *host.arrs)


def _exchange(arrs, *, gather, name):
    host = _Exchange(arrs, gather)

    def body(*refs):
        ins, outs, sems = refs[:host.n], refs[host.n:2 * host.n], refs[2 * host.n:]
        host.start(ins, outs, sems)
        host.wait(ins, outs, sems)

    hbm = pl.BlockSpec(memory_space=pl.ANY)
    return pl.pallas_call(
        body, name=name, in_specs=[hbm] * host.n, out_specs=[hbm] * host.n, out_shape=host.out_shape(),
        scratch_shapes=host.scratch(), compiler_params=pltpu.CompilerParams(has_side_effects=True),
    )(*arrs)


def _ffn_fwd(xhat, g_in, b_in, wg, wu, wd, *, tm, name, host=None):
    t = xhat.shape[0]
    nj = N_DEV

    def body(x_ref, g_ref, b_ref, wg_ref, wu_ref, wd_ref, xo_ref, rstd_ref, hg_ref, hu_ref, xb, acc):
        j = pl.program_id(1)

        @pl.when(j == 0)
        def _():
            xb[...] = (x_ref[...] * g_ref[...] + b_ref[...]).astype(bf16)
            acc[...] = jnp.zeros_like(acc)

        hg = jnp.dot(xb[...], wg_ref[...], preferred_element_type=f32)
        hu = jnp.dot(xb[...], wu_ref[...], preferred_element_type=f32)
        hg_ref[...] = hg.astype(bf16)
        hu_ref[...] = hu.astype(bf16)
        a = hg * _sigmoid_tanh(hg) * hu
        acc[...] += jnp.dot(a.astype(bf16), wd_ref[...], preferred_element_type=f32)

        @pl.when(j == nj - 1)
        def _():
            x = x_ref[...] * g_ref[...] + b_ref[...]
            xo, rstd = _ln_fwd_tile(ALPHA * x + 0.5 * acc[...])
            xo_ref[...] = xo
            rstd_ref[...] = rstd

    row = pl.BlockSpec((1, D_MODEL), lambda i, j: (0, 0))
    return _hosted_call(
        host, body, name=name, grid=(t // tm, nj),
        in_specs=[pl.BlockSpec((tm, D_MODEL), lambda i, j: (i, 0)), row, row,
                  pl.BlockSpec((None, D_MODEL, FF_TILE), lambda i, j: (j, 0, 0)),
                  pl.BlockSpec((None, D_MODEL, FF_TILE), lambda i, j: (j, 0, 0)),
                  pl.BlockSpec((None, FF_TILE, D_MODEL), lambda i, j: (j, 0, 0))],
        out_specs=[pl.BlockSpec((tm, D_MODEL), lambda i, j: (i, 0)),
                   pl.BlockSpec((tm, 1), lambda i, j: (i, 0)),
                   pl.BlockSpec((tm, FF_TILE), lambda i, j: (i, j)),
                   pl.BlockSpec((tm, FF_TILE), lambda i, j: (i, j))],
        out_shape=[jax.ShapeDtypeStruct((t, D_MODEL), f32), jax.ShapeDtypeStruct((t, 1), f32),
                   jax.ShapeDtypeStruct((t, D_FF), bf16), jax.ShapeDtypeStruct((t, D_FF), bf16)],
        scratch_shapes=[pltpu.VMEM((tm, D_MODEL), bf16), pltpu.VMEM((tm, D_MODEL), f32)],
        compiler_params=_params(("arbitrary", "arbitrary")),
    )(xhat, g_in, b_in, wg, wu, wd)


def _ffn_bwd(dpre, hg, hu, wg, wu, wd, ln_in, *, tm, name, host=None):
    t = dpre.shape[0]
    nj = N_DEV
    with_ln = ln_in is not None

    def body(*refs):
        if with_ln:
            (dp_ref, hg_ref, hu_ref, wg_ref, wu_ref, wd_ref, xh_ref, rs_ref, g_ref,
             dx_ref, gg_ref, gb_ref, dhg_ref, dhu_ref, a_ref, dfb, acc) = refs
        else:
            (dp_ref, hg_ref, hu_ref, wg_ref, wu_ref, wd_ref,
             dx_ref, dhg_ref, dhu_ref, a_ref, dfb, acc) = refs
        i = pl.program_id(0)
        j = pl.program_id(1)

        @pl.when(j == 0)
        def _():
            dfb[...] = (0.5 * dp_ref[...]).astype(bf16)
            acc[...] = jnp.zeros_like(acc)

        da = lax.dot_general(dfb[...], wd_ref[...], _NT, preferred_element_type=f32)
        hgv = hg_ref[...].astype(f32)
        huv = hu_ref[...].astype(f32)
        sg = _sigmoid_tanh(hgv)
        silu = hgv * sg
        a_ref[...] = (silu * huv).astype(bf16)
        dhu = (da * silu).astype(bf16)
        dhg = (da * huv * (sg * (1.0 + hgv * (1.0 - sg)))).astype(bf16)
        dhg_ref[...] = dhg
        dhu_ref[...] = dhu
        acc[...] += (lax.dot_general(dhg, wg_ref[...], _NT, preferred_element_type=f32)
                     + lax.dot_general(dhu, wu_ref[...], _NT, preferred_element_type=f32))

        @pl.when(j == nj - 1)
        def _():
            dx = ALPHA * dp_ref[...] + acc[...]
            if with_ln:
                dprev, gg, gb = _ln_bwd_tile(dx, xh_ref[...], rs_ref[...], g_ref[...])
                dx_ref[...] = dprev

                @pl.when(i == 0)
                def _():
                    gg_ref[...] = gg
                    gb_ref[...] = gb

                @pl.when(i > 0)
                def _():
                    gg_ref[...] += gg
                    gb_ref[...] += gb
            else:
                dx_ref[...] = dx

    tok = pl.BlockSpec((tm, D_MODEL), lambda i, j: (i, 0), pipeline_mode=pl.Buffered(1))
    row = pl.BlockSpec((1, D_MODEL), lambda i, j: (0, 0))
    hid = pl.BlockSpec((tm, FF_TILE), lambda i, j: (i, j))
    in_specs = [tok, hid, hid,
                pl.BlockSpec((None, D_MODEL, FF_TILE), lambda i, j: (j, 0, 0)),
                pl.BlockSpec((None, D_MODEL, FF_TILE), lambda i, j: (j, 0, 0)),
                pl.BlockSpec((None, FF_TILE, D_MODEL), lambda i, j: (j, 0, 0))]
    args = [dpre, hg, hu, wg, wu, wd]
    out_specs = [tok]
    out_shape = [jax.ShapeDtypeStruct((t, D_MODEL), f32)]
    if with_ln:
        in_specs += [tok, pl.BlockSpec((tm, 1), lambda i, j: (i, 0)), row]
        args += list(ln_in)
        out_specs += [row, row]
        out_shape += [jax.ShapeDtypeStruct((1, D_MODEL), f32)] * 2
    out_specs += [hid, hid, hid]
    out_shape += [jax.ShapeDtypeStruct((t, D_FF), bf16)] * 3
    return _hosted_call(
        host, body, name=name, grid=(t // tm, nj), in_specs=in_specs, out_specs=out_specs, out_shape=out_shape,
        scratch_shapes=[pltpu.VMEM((tm, D_MODEL), bf16), pltpu.VMEM((tm, D_MODEL), f32)],
        compiler_params=_params(("arbitrary", "arbitrary")),
    )(*args)


def _ffn_bwd_act(dpre, hg, hu, wd, *, tm, name, host=None):
    t = dpre.shape[0]

    def body(dp_ref, hg_ref, hu_ref, wd_ref, dhg_ref, dhu_ref, a_ref, dfb):
        @pl.when(pl.program_id(1) == 0)
        def _():
            dfb[...] = (0.5 * dp_ref[...]).astype(bf16)

        da = lax.dot_general(dfb[...], wd_ref[...], _NT, preferred_element_type=f32)
        hgv = hg_ref[...].astype(f32)
        huv = hu_ref[...].astype(f32)
        sg = _sigmoid_tanh(hgv)
        silu = hgv * sg
        a_ref[...] = (silu * huv).astype(bf16)
        dhu_ref[...] = (da * silu).astype(bf16)
        dhg_ref[...] = (da * huv * (sg * (1.0 + hgv * (1.0 - sg)))).astype(bf16)

    hid = pl.BlockSpec((tm, FF_TILE), lambda i, j: (i, j))
    return _hosted_call(
        host, body, name=name, grid=(t // tm, N_DEV),
        in_specs=[pl.BlockSpec((tm, D_MODEL), lambda i, j: (i, 0)), hid, hid,
                  pl.BlockSpec((None, FF_TILE, D_MODEL), lambda i, j: (j, 0, 0))],
        out_specs=[hid, hid, hid], out_shape=[jax.ShapeDtypeStruct((t, D_FF), bf16)] * 3,
        scratch_shapes=[pltpu.VMEM((tm, D_MODEL), bf16)],
        compiler_params=_params(("arbitrary", "arbitrary")),
    )(dpre, hg, hu, wd)


def _ffn_bwd_dx(dpre, dhg, dhu, wg, wu, *, tm, name, host=None):
    t = dpre.shape[0]
    nj = N_DEV

    def body(dp_ref, dhg_ref, dhu_ref, wg_ref, wu_ref, dx_ref, acc):
        j = pl.program_id(1)

        @pl.when(j == 0)
        def _():
            acc[...] = jnp.zeros_like(acc)

        acc[...] += (lax.dot_general(dhg_ref[...], wg_ref[...], _NT, preferred_element_type=f32)
                     + lax.dot_general(dhu_ref[...], wu_ref[...], _NT, preferred_element_type=f32))

        @pl.when(j == nj - 1)
        def _():
            dx_ref[...] = ALPHA * dp_ref[...] + acc[...]

    tok = pl.BlockSpec((tm, D_MODEL), lambda i, j: (i, 0))
    hid = pl.BlockSpec((tm, FF_TILE), lambda i, j: (i, j))
    wspec = pl.BlockSpec((None, D_MODEL, FF_TILE), lambda i, j: (j, 0, 0))
    return _hosted_call(
        host, body, name=name, grid=(t // tm, nj), in_specs=[tok, hid, hid, wspec, wspec],
        out_specs=[tok], out_shape=[jax.ShapeDtypeStruct((t, D_MODEL), f32)],
        scratch_shapes=[pltpu.VMEM((tm, D_MODEL), f32)],
        compiler_params=_params(("arbitrary", "arbitrary")),
    )(dpre, dhg, dhu, wg, wu)


def _mm(a, b, *, mode, out_dtype, tm, tn, tk, name, affine=None, a_cols=None, b_cols=None,
        b_blocked=False, out_blocked=False, out_scale=None):
    if mode == "nn":
        m_full, k_full = a.shape
        m_dim, k_dim = (m_full, a_cols[1]) if a_cols else (m_full, k_full)
    else:
        k_dim, m_full = a.shape
        m_dim = a_cols[1] if a_cols else m_full
    a_off = a_cols[0] if a_cols else 0
    if b_blocked:
        n_dim = b.shape[0] * b.shape[2]
        assert b.shape[2] == tn
    else:
        n_dim = b_cols[1] if b_cols else b.shape[1]
    b_off = b_cols[0] if b_cols else 0
    assert m_dim % tm == 0 and n_dim % tn == 0 and k_dim % tk == 0, (name, m_dim, n_dim, k_dim)
    nk = k_dim // tk

    def body(*refs):
        if affine is not None:
            a_ref, g_ref, s_ref, b_ref, o_ref, acc = refs
        else:
            a_ref, b_ref, o_ref, acc = refs
        k = pl.program_id(2)

        @pl.when(k == 0)
        def _():
            acc[...] = jnp.zeros_like(acc)

        av = a_ref[...]
        if affine is not None:
            av = av * g_ref[...] + s_ref[...]
        av = av.astype(bf16)
        bv = b_ref[...].astype(bf16)
        if mode == "nn":
            acc[...] += jnp.dot(av, bv, preferred_element_type=f32)
        else:
            acc[...] += lax.dot_general(av, bv, _TN, preferred_element_type=f32)

        @pl.when(k == nk - 1)
        def _():
            res = acc[...] if out_scale is None else acc[...] * out_scale
            o_ref[...] = res.astype(out_dtype)

    if mode == "nn":
        a_spec = pl.BlockSpec((tm, tk), lambda i, j, k: (i, k + a_off))
        aff_spec = pl.BlockSpec((1, tk), lambda i, j, k: (0, k + a_off))
    else:
        a_spec = pl.BlockSpec((tk, tm), lambda i, j, k: (k, i + a_off))
        aff_spec = pl.BlockSpec((1, tm), lambda i, j, k: (0, i + a_off))
    if b_blocked:
        b_spec = pl.BlockSpec((None, tk, tn), lambda i, j, k: (j, k, 0))
    else:
        b_spec = pl.BlockSpec((tk, tn), lambda i, j, k: (k, j + b_off))
    if out_blocked:
        o_spec = pl.BlockSpec((None, tm, tn), lambda i, j, k: (j, i, 0))
        o_shape = jax.ShapeDtypeStruct((n_dim // tn, m_dim, tn), out_dtype)
    else:
        o_spec = pl.BlockSpec((tm, tn), lambda i, j, k: (i, j))
        o_shape = jax.ShapeDtypeStruct((m_dim, n_dim), out_dtype)
    in_specs = [a_spec] + ([aff_spec, aff_spec] if affine is not None else []) + [b_spec]
    args = [a] + (list(affine) if affine is not None else []) + [b]
    return pl.pallas_call(
        body, name=name, grid=(m_dim // tm, n_dim // tn, nk), in_specs=in_specs, out_specs=o_spec,
        out_shape=o_shape, scratch_shapes=[pltpu.VMEM((tm, tn), f32)],
        compiler_params=_params(("arbitrary", "arbitrary", "arbitrary")),
    )(*args)


def _mm_tn(a, b, *, out_dtype, tm, mb, tn, nb, tk, name, affine=None, out_blocked=False, out_scale=None,
           pair=False, host=None):
    k_dim, m_dim = a.shape
    multi_b = isinstance(b, (list, tuple))
    b_list = list(b) if multi_b else [b]
    n_dim = nb * tn if multi_b else b.shape[1]
    assert m_dim % (mb * tm) == 0 and n_dim % (nb * tn) == 0 and k_dim % tk == 0, (name, m_dim, n_dim, k_dim)
    nk = k_dim // tk
    grid = (m_dim // (mb * tm), n_dim // (nb * tn), nk)
    if pair:
        assert mb * nb == 4 and grid[0] * grid[1] == 2 and out_dtype == bf16, name

    def body(*refs):
        if pair:
            refs, (acc, send_buf, recv_buf, send_sems, recv_sems) = refs[:-5], refs[-5:]
        else:
            refs, acc = refs[:-1], refs[-1]
        a_ref, o_ref = refs[0], refs[-1]
        if affine is not None:
            g_ref, s_ref = refs[1:3]
        b_refs = refs[3 if affine is not None else 1:-1]
        k = pl.program_id(2)

        @pl.when(k == 0)
        def _():
            acc[...] = jnp.zeros_like(acc)

        av = a_ref[...]
        if affine is not None:
            av = av * g_ref[...] + s_ref[...]
        av = av.astype(bf16)
        if multi_b:
            pieces = [r[...].astype(bf16) for r in b_refs]
        else:
            bv = b_refs[0][...].astype(bf16)
            pieces = [bv[:, jn * tn:(jn + 1) * tn] for jn in range(nb)]
        for im in range(mb):
            a_t = av[:, im * tm:(im + 1) * tm].T
            for jn in range(nb):
                acc[im * nb + jn] += jnp.dot(a_t, pieces[jn], preferred_element_type=f32)

        def scaled(v):
            return v if out_scale is None else v * out_scale

        @pl.when(k == nk - 1)
        def _():
            if pair:
                x, y, c = lax.axis_index("x"), lax.axis_index("y"), lax.axis_index("c")
                window = pl.program_id(0) + pl.program_id(1)
                swaps = []
                for cc in range(2):
                    send_buf[cc] = scaled(acc[2 * cc + 1 - c]).astype(bf16)
                    swaps.append(pltpu.make_async_remote_copy(
                        src_ref=send_buf.at[cc], dst_ref=recv_buf.at[window, cc],
                        send_sem=send_sems.at[2 * window + cc], recv_sem=recv_sems.at[2 * window + cc],
                        device_id=(x, y, 1 - c), device_id_type=MESH_T))
                    swaps[cc].start()
                for cc in range(2):
                    swaps[cc].wait_recv()
                    o_ref[cc] = (scaled(acc[2 * cc + c]) + recv_buf[window, cc].astype(f32)).astype(bf16)
                for cc in range(2):
                    swaps[cc].wait_send()
                return
            for im in range(mb):
                for jn in range(nb):
                    res = scaled(acc[im * nb + jn])
                    if out_blocked:
                        o_ref[jn, im * tm:(im + 1) * tm, :] = res.astype(out_dtype)
                    else:
                        o_ref[im * tm:(im + 1) * tm, jn * tn:(jn + 1) * tn] = res.astype(out_dtype)

    a_spec = pl.BlockSpec((tk, mb * tm), lambda i, j, k: (k, i))
    aff_spec = pl.BlockSpec((1, mb * tm), lambda i, j, k: (0, i))
    if multi_b:
        b_specs = [pl.BlockSpec((tk, tn), lambda i, j, k: (k, 0))] * nb
    else:
        b_specs = [pl.BlockSpec((tk, nb * tn), lambda i, j, k: (k, j))]
    scratch = [pltpu.VMEM((mb * nb, tm, tn), f32)]
    if pair:
        o_spec = pl.BlockSpec((2, tm, tn), lambda i, j, k: (i + j, 0, 0))
        o_shape = jax.ShapeDtypeStruct((4, tm, tn), out_dtype)
        scratch += [pltpu.VMEM((2, tm, tn), bf16), pltpu.VMEM((2, 2, tm, tn), bf16),
                    pltpu.SemaphoreType.DMA((4,)), pltpu.SemaphoreType.DMA((4,))]
    elif out_blocked:
        o_spec = pl.BlockSpec((nb, mb * tm, tn), lambda i, j, k: (j, i, 0))
        o_shape = jax.ShapeDtypeStruct((n_dim // tn, m_dim, tn), out_dtype)
    else:
        o_spec = pl.BlockSpec((mb * tm, nb * tn), lambda i, j, k: (i, j))
        o_shape = jax.ShapeDtypeStruct((m_dim, n_dim), out_dtype)
    in_specs = [a_spec] + ([aff_spec, aff_spec] if affine is not None else []) + b_specs
    args = [a] + (list(affine) if affine is not None else []) + b_list
    res = _hosted_call(
        host, body, name=name, grid=grid, in_specs=in_specs, out_specs=o_spec, out_shape=o_shape,
        scratch_shapes=scratch, compiler_params=_params(("arbitrary", "arbitrary", "arbitrary")),
    )(*args)
    return res[0] if host is None else res


def _in_proj(xhat, g, b, w_in, *, tm, name):
    t = xhat.shape[0]
    n_qkv, n_l = 3 * FOX_W, 2 * LRU_W

    def body(x_ref, g_ref, b_ref, w_ref, qkv_ref, zl_ref, zfg_ref):
        xb = (x_ref[...] * g_ref[...] + b_ref[...]).astype(bf16)
        qkv_ref[...] = jnp.dot(xb, w_ref[:, :n_qkv], preferred_element_type=f32).astype(bf16)
        zl_ref[...] = jnp.dot(xb, w_ref[:, n_qkv:n_qkv + n_l], preferred_element_type=f32)
        zfg_ref[...] = jnp.dot(xb, w_ref[:, n_qkv + n_l:], preferred_element_type=f32)

    row = pl.BlockSpec((1, D_MODEL), lambda i: (0, 0))
    return pl.pallas_call(
        body, name=name, grid=(t // tm,),
        in_specs=[pl.BlockSpec((tm, D_MODEL), lambda i: (i, 0)), row, row,
                  pl.BlockSpec(w_in.shape, lambda i: (0, 0))],
        out_specs=[pl.BlockSpec((tm, n_qkv), lambda i: (i, 0)), pl.BlockSpec((tm, n_l), lambda i: (i, 0)),
                   pl.BlockSpec((tm, LANES), lambda i: (i, 0))],
        out_shape=[jax.ShapeDtypeStruct((t, n_qkv), bf16), jax.ShapeDtypeStruct((t, n_l), f32),
                   jax.ShapeDtypeStruct((t, LANES), f32)],
        compiler_params=_params(("arbitrary",)),
    )(xhat, g, b, w_in)


def _mmln(pairs, *, tm, name, resid=None, resid_scale=1.0, epi=None, ln=None, n_out=D_MODEL):
    t = pairs[0][0].shape[0]
    n_pairs = len(pairs)
    n_resid = 0 if resid is None else len(resid) - 1

    def body(*refs):
        pos = 0
        val = None
        for p in range(n_pairs):
            a_ref, b_ref = refs[pos], refs[pos + 1]
            pos += 2
            av = a_ref[...].astype(bf16)
            bv = b_ref[...].astype(bf16)
            if pairs[p][6] == "nn":
                term = jnp.dot(av, bv, preferred_element_type=f32)
            else:
                term = lax.dot_general(av, bv, _NT, preferred_element_type=f32)
            val = term if val is None else val + term
        if resid is not None:
            if resid[0] == "plain":
                r = refs[pos][...]
            else:
                r = refs[pos][...] * refs[pos + 1][...] + refs[pos + 2][...]
            pos += n_resid
            val = val + resid_scale * r
        if epi is None:
            o_ref = refs[pos]
            o_ref[...] = val.astype(o_ref.dtype)
        elif epi == "ln_fwd":
            xo, rstd = _ln_fwd_tile(val)
            refs[pos][...] = xo
            refs[pos + 1][...] = rstd
        else:
            xh_ref, rs_ref, g_ref, dx_ref, gg_ref, gb_ref = refs[pos:pos + 6]
            dprev, gg, gb = _ln_bwd_tile(val, xh_ref[...], rs_ref[...], g_ref[...])
            dx_ref[...] = dprev
            i = pl.program_id(0)

            @pl.when(i == 0)
            def _():
                gg_ref[...] = gg
                gb_ref[...] = gb

            @pl.when(i > 0)
            def _():
                gg_ref[...] += gg
                gb_ref[...] += gb

    in_specs, args = [], []
    for (a, acb, aw, b, bcb, bw, mode) in pairs:
        in_specs.append(pl.BlockSpec((tm, aw), lambda i, acb=acb: (i, acb)))
        args.append(a)
        if mode == "nn":
            in_specs.append(pl.BlockSpec((aw, n_out), lambda i, bcb=bcb: (bcb, 0)))
        else:
            in_specs.append(pl.BlockSpec((n_out, bw), lambda i, bcb=bcb: (0, bcb)))
        args.append(b)
    tok = pl.BlockSpec((tm, n_out), lambda i: (i, 0))
    row = pl.BlockSpec((1, n_out), lambda i: (0, 0))
    col = pl.BlockSpec((tm, 1), lambda i: (i, 0))
    if resid is not None:
        in_specs += [tok] if resid[0] == "plain" else [tok, row, row]
        args += list(resid[1:])
    if epi is None:
        out_specs, out_shape = tok, jax.ShapeDtypeStruct((t, n_out), f32)
    elif epi == "ln_fwd":
        out_specs = [tok, col]
        out_shape = [jax.ShapeDtypeStruct((t, n_out), f32), jax.ShapeDtypeStruct((t, 1), f32)]
    else:
        in_specs += [tok, col, row]
        args += list(ln)
        out_specs = [tok, row, row]
        out_shape = [jax.ShapeDtypeStruct((t, n_out), f32)] + [jax.ShapeDtypeStruct((1, n_out), f32)] * 2
    return pl.pallas_call(
        body, name=name, grid=(t // tm,), in_specs=in_specs, out_specs=out_specs, out_shape=out_shape,
        compiler_params=_params(("arbitrary",)),
    )(*args)


def _loss_bwd(xhat, rstd, g, b, target, *, tm, name):
    t = xhat.shape[0]

    def body(xh_ref, rs_ref, g_ref, b_ref, tg_ref, dx_ref, sq_ref, gg_ref, gb_ref):
        i = pl.program_id(0)
        xh = xh_ref[...]
        diff = xh * g_ref[...] + b_ref[...] - tg_ref[...]
        sq = jnp.sum(diff * diff, axis=0, keepdims=True)
        dprev, gg, gb = _ln_bwd_tile(diff * (1.0 / D_MODEL), xh, rs_ref[...], g_ref[...])
        dx_ref[...] = dprev

        @pl.when(i == 0)
        def _():
            sq_ref[...] = sq
            gg_ref[...] = gg
            gb_ref[...] = gb

        @pl.when(i > 0)
        def _():
            sq_ref[...] += sq
            gg_ref[...] += gg
            gb_ref[...] += gb

    tok = pl.BlockSpec((tm, D_MODEL), lambda i: (i, 0))
    row = pl.BlockSpec((1, D_MODEL), lambda i: (0, 0))
    return pl.pallas_call(
        body, name=name, grid=(t // tm,),
        in_specs=[tok, pl.BlockSpec((tm, 1), lambda i: (i, 0)), row, row, tok],
        out_specs=[tok, row, row, row],
        out_shape=[jax.ShapeDtypeStruct((t, D_MODEL), f32)] + [jax.ShapeDtypeStruct((1, D_MODEL), f32)] * 3,
        compiler_params=_params(("arbitrary",)),
    )(xhat, rstd, g, b, target)


CUM_TILE = 256


def _tri(n, lower):
    r = lax.broadcasted_iota(jnp.int32, (n, n), 0)
    c = lax.broadcasted_iota(jnp.int32, (n, n), 1)
    return jnp.where((r >= c) if lower else (r <= c), 1.0, 0.0).astype(f32)


def _cum_fwd(zfg, bfg, *, name):
    t = zfg.shape[0]

    def body(z_ref, b_ref, o_ref, carry):
        @pl.when(pl.program_id(0) == 0)
        def _():
            carry[...] = jnp.zeros_like(carry)

        ls = -_softplus(-(z_ref[...] + b_ref[...]))
        c = jnp.dot(_tri(CUM_TILE, True), ls, preferred_element_type=f32,
                    precision=lax.Precision.HIGHEST) + carry[...]
        o_ref[...] = c
        carry[...] = c[CUM_TILE - 1:CUM_TILE, :]

    blk = pl.BlockSpec((CUM_TILE, LANES), lambda i: (i, 0))
    return pl.pallas_call(
        body, name=name, grid=(t // CUM_TILE,),
        in_specs=[blk, pl.BlockSpec((1, LANES), lambda i: (0, 0))], out_specs=blk,
        out_shape=jax.ShapeDtypeStruct((t, LANES), f32), scratch_shapes=[pltpu.VMEM((1, LANES), f32)],
        compiler_params=_params(("arbitrary",)),
    )(zfg, bfg)


def _cum_bwd(dcum_q, dcum_k, zfg, bfg, *, name):
    t = zfg.shape[0]
    n = t // CUM_TILE

    def body(d_ref, d2_ref, z_ref, b_ref, o_ref, s_ref, carry):
        i = pl.program_id(0)

        @pl.when(i == 0)
        def _():
            carry[...] = jnp.zeros_like(carry)

        dls = jnp.dot(_tri(CUM_TILE, False), d_ref[...] + d2_ref[...], preferred_element_type=f32,
                      precision=lax.Precision.HIGHEST) + carry[...]
        carry[...] = dls[0:1, :]
        lane = lax.broadcasted_iota(jnp.int32, (CUM_TILE, LANES), 1)
        dfg = jnp.where(lane < HEADS, dls * _sigmoid(-(z_ref[...] + b_ref[...])), 0.0)
        o_ref[...] = dfg
        tot = jnp.sum(dfg, axis=0, keepdims=True)

        @pl.when(i == 0)
        def _():
            s_ref[...] = tot

        @pl.when(i > 0)
        def _():
            s_ref[...] += tot

    blk = pl.BlockSpec((CUM_TILE, LANES), lambda i: (n - 1 - i, 0))
    row = pl.BlockSpec((1, LANES), lambda i: (0, 0))
    return pl.pallas_call(
        body, name=name, grid=(n,), in_specs=[blk, blk, blk, row], out_specs=[blk, row],
        out_shape=[jax.ShapeDtypeStruct((t, LANES), f32), jax.ShapeDtypeStruct((1, LANES), f32)],
        scratch_shapes=[pltpu.VMEM((1, LANES), f32)],
        compiler_params=_params(("arbitrary",)),
    )(dcum_q, dcum_k, zfg, bfg)


ATT_TILE = 512


ATT_ROWS = 32


def _causal_rows(r, transposed):
    rr = lax.broadcasted_iota(jnp.int32, (ATT_ROWS, ATT_TILE), 0) + r * ATT_ROWS
    cc = lax.broadcasted_iota(jnp.int32, (ATT_ROWS, ATT_TILE), 1)
    return (cc >= rr) if transposed else (rr >= cc)


def _causal(i, j, transposed):
    r = lax.broadcasted_iota(jnp.int32, (ATT_TILE, ATT_TILE), 0)
    c = lax.broadcasted_iota(jnp.int32, (ATT_TILE, ATT_TILE), 1)
    if transposed:
        return (c + i * ATT_TILE) >= (r + j * ATT_TILE)
    return (r + i * ATT_TILE) >= (c + j * ATT_TILE)


def _attn_fwd(qkv, cum, cum_t, *, name, host=None):
    t = qkv.shape[0]
    n = t // ATT_TILE
    tq = ATT_TILE

    def body(q_ref, k_ref, v_ref, cq_ref, ck_ref, o_ref, lse_ref, acc, m_s, l_s, c_s, s_s, p_s):
        i = pl.program_id(0)
        j = pl.program_id(1)

        @pl.when(j == 0)
        def _():
            acc[...] = jnp.zeros_like(acc)
            m_s[...] = jnp.full_like(m_s, NEG_BIG)
            l_s[...] = jnp.zeros_like(l_s)

        def block(masked):
            for h in range(HEADS):
                hs = slice(HEAD_D * h, HEAD_D * (h + 1))
                s_s[...] = lax.dot_general(q_ref[:, hs] * ATT_SCALE, k_ref[:, hs], _NT, preferred_element_type=f32)
                ck = ck_ref[h:h + 1, :]

                def rows_chunk(r, carry):
                    rows = pl.ds(pl.multiple_of(r * ATT_ROWS, ATT_ROWS), ATT_ROWS)
                    s = s_s[rows, :] + (cq_ref[rows, h:h + 1] - ck)
                    if masked:
                        s = jnp.where(_causal_rows(r, False), s, NEG_BIG)
                    m_old = m_s[rows, h:h + 1]
                    m_new = jnp.maximum(m_old, jnp.max(s, axis=-1, keepdims=True))
                    corr = jnp.exp(m_old - m_new)
                    p = jnp.exp(s - m_new)
                    l_s[rows, h:h + 1] = corr * l_s[rows, h:h + 1] + jnp.sum(p, axis=-1, keepdims=True)
                    m_s[rows, h:h + 1] = m_new
                    c_s[rows, h:h + 1] = corr
                    p_s[rows, :] = p.astype(bf16)
                    return carry

                lax.fori_loop(0, tq // ATT_ROWS, rows_chunk, 0, unroll=4)
                acc[:, hs] = c_s[:, h:h + 1] * acc[:, hs] + jnp.dot(p_s[...], v_ref[:, hs],
                                                                   preferred_element_type=f32)

        @pl.when(j < i)
        def _():
            block(False)

        @pl.when(j == i)
        def _():
            block(True)
            lse_ref[...] = jnp.zeros_like(lse_ref)
            for h in range(HEADS):
                hs = slice(HEAD_D * h, HEAD_D * (h + 1))
                l = l_s[:, h:h + 1]
                o_ref[:, hs] = acc[:, hs] / l
                lse_ref[:, h:h + 1] = m_s[:, h:h + 1] + jnp.log(l)

    return _hosted_call(
        host, body, name=name, grid=(n, n),
        in_specs=[pl.BlockSpec((tq, FOX_W), lambda i, j: (i, 0)),
                  pl.BlockSpec((tq, FOX_W), lambda i, j: (jnp.minimum(i, j), 1)),
                  pl.BlockSpec((tq, FOX_W), lambda i, j: (jnp.minimum(i, j), 2)),
                  pl.BlockSpec((tq, LANES), lambda i, j: (i, 0)),
                  pl.BlockSpec((HEADS, tq), lambda i, j: (0, jnp.minimum(i, j)))],
        out_specs=[pl.BlockSpec((tq, FOX_W), lambda i, j: (i, 0)), pl.BlockSpec((tq, LANES), lambda i, j: (i, 0))],
        out_shape=[jax.ShapeDtypeStruct((t, FOX_W), f32), jax.ShapeDtypeStruct((t, LANES), f32)],
        scratch_shapes=[pltpu.VMEM((tq, FOX_W), f32), pltpu.VMEM((tq, LANES), f32), pltpu.VMEM((tq, LANES), f32),
                        pltpu.VMEM((tq, LANES), f32), pltpu.VMEM((tq, tq), f32), pltpu.VMEM((tq, tq), bf16)],
        compiler_params=_params(("arbitrary", "arbitrary")),
    )(qkv, qkv, qkv, cum, cum_t)


def _attn_delta(dmix, o, *, tm, name):
    t = o.shape[0]

    def body(do_ref, o_ref, d_ref):
        r = lax.broadcasted_iota(jnp.int32, (FOX_W, LANES), 0)
        c = lax.broadcasted_iota(jnp.int32, (FOX_W, LANES), 1)
        pick = jnp.where(r // HEAD_D == c, 1.0, 0.0).astype(f32)
        d_ref[...] = jnp.dot(do_ref[...] * o_ref[...], pick, preferred_element_type=f32,
                             precision=lax.Precision.HIGHEST)

    blk = pl.BlockSpec((tm, FOX_W), lambda i: (i, 0))
    return pl.pallas_call(
        body, name=name, grid=(t // tm,), in_specs=[blk, blk],
        out_specs=pl.BlockSpec((tm, LANES), lambda i: (i, 0)),
        out_shape=jax.ShapeDtypeStruct((t, LANES), f32), compiler_params=_params(("arbitrary",)),
    )(dmix, o)


def _attn_dq(qkv, dmix, cum, cum_t, lse, delta, *, name, host=None):
    t = qkv.shape[0]
    n = t // ATT_TILE
    tq = ATT_TILE

    def body(q_ref, k_ref, v_ref, do_ref, cq_ref, ck_ref, lse_ref, dl_ref, dq_ref, dc_ref, acc, dc_acc):
        i = pl.program_id(0)
        j = pl.program_id(1)

        @pl.when(j == 0)
        def _():
            acc[...] = jnp.zeros_like(acc)
            dc_acc[...] = jnp.zeros_like(dc_acc)

        def block(masked):
            mask = _causal(i, j, False) if masked else None
            for h in range(HEADS):
                hs = slice(HEAD_D * h, HEAD_D * (h + 1))
                kh = k_ref[:, hs]
                s = lax.dot_general(q_ref[:, hs] * ATT_SCALE, kh, _NT, preferred_element_type=f32)
                s = s + cq_ref[:, h:h + 1] - ck_ref[h:h + 1, :]
                if masked:
                    s = jnp.where(mask, s, NEG_BIG)
                p = jnp.exp(s - lse_ref[:, h:h + 1])
                dp = lax.dot_general(do_ref[:, hs].astype(bf16), v_ref[:, hs], _NT, preferred_element_type=f32)
                ds = p * (dp - dl_ref[:, h:h + 1])
                acc[:, hs] += jnp.dot(ds.astype(bf16), kh, preferred_element_type=f32)
                dc_acc[:, h:h + 1] += jnp.sum(ds, axis=-1, keepdims=True)

        @pl.when(j < i)
        def _():
            block(False)

        @pl.when(j == i)
        def _():
            block(True)
            dq_ref[...] = (acc[...] * ATT_SCALE).astype(bf16)
            dc_ref[...] = dc_acc[...]

    col = pl.BlockSpec((tq, LANES), lambda i, j: (i, 0))
    return _hosted_call(
        host, body, name=name, grid=(n, n),
        in_specs=[pl.BlockSpec((tq, FOX_W), lambda i, j: (i, 0)),
                  pl.BlockSpec((tq, FOX_W), lambda i, j: (jnp.minimum(i, j), 1)),
                  pl.BlockSpec((tq, FOX_W), lambda i, j: (jnp.minimum(i, j), 2)),
                  pl.BlockSpec((tq, FOX_W), lambda i, j: (i, 0)),
                  col, pl.BlockSpec((HEADS, tq), lambda i, j: (0, jnp.minimum(i, j))), col, col],
        out_specs=[pl.BlockSpec((tq, FOX_W), lambda i, j: (i, 0)), col],
        out_shape=[jax.ShapeDtypeStruct((t, FOX_W), bf16), jax.ShapeDtypeStruct((t, LANES), f32)],
        scratch_shapes=[pltpu.VMEM((tq, FOX_W), f32), pltpu.VMEM((tq, LANES), f32)],
        compiler_params=_params(("arbitrary", "arbitrary")),
    )(qkv, qkv, qkv, dmix, cum, cum_t, lse, delta)


def _attn_dkv(qkv, dmix, cum, cum_t, lse_t, delta_t, *, name):
    t = qkv.shape[0]
    n = t // ATT_TILE
    tk = ATT_TILE

    def body(q_ref, k_ref, v_ref, do_ref, cq_ref, ck_ref, lse_ref, dl_ref, dk_ref, dv_ref, dc_ref, dk_acc, dv_acc, dc_acc):
        j = pl.program_id(0)
        i = pl.program_id(1)

        @pl.when(i == 0)
        def _():
            dk_acc[...] = jnp.zeros_like(dk_acc)
            dv_acc[...] = jnp.zeros_like(dv_acc)
            dc_acc[...] = jnp.zeros_like(dc_acc)

        def block(masked):
            mask = _causal(i, j, True) if masked else None
            for h in range(HEADS):
                hs = slice(HEAD_D * h, HEAD_D * (h + 1))
                qh = q_ref[:, hs]
                doh = do_ref[:, hs].astype(bf16)
                s_t = lax.dot_general(k_ref[:, hs] * ATT_SCALE, qh, _NT, preferred_element_type=f32)
                s_t = s_t + cq_ref[h:h + 1, :] - ck_ref[:, h:h + 1]
                if masked:
                    s_t = jnp.where(mask, s_t, NEG_BIG)
                p_t = jnp.exp(s_t - lse_ref[h:h + 1, :])
                dv_acc[:, hs] += jnp.dot(p_t.astype(bf16), doh, preferred_element_type=f32)
                dp_t = lax.dot_general(v_ref[:, hs], doh, _NT, preferred_element_type=f32)
                ds_t = p_t * (dp_t - dl_ref[h:h + 1, :])
                dk_acc[:, hs] += jnp.dot(ds_t.astype(bf16), qh, preferred_element_type=f32)
                dc_acc[:, h:h + 1] -= jnp.sum(ds_t, axis=-1, keepdims=True)

        @pl.when(i > j)
        def _():
            block(False)

        @pl.when(i == j)
        def _():
            block(True)

        @pl.when(i == n - 1)
        def _():
            dk_ref[...] = (dk_acc[...] * ATT_SCALE).astype(bf16)
            dv_ref[...] = dv_acc[...].astype(bf16)
            dc_ref[...] = dc_acc[...]

    rowq = pl.BlockSpec((HEADS, tk), lambda j, i: (0, jnp.maximum(i, j)))
    return pl.pallas_call(
        body, name=name, grid=(n, n),
        in_specs=[pl.BlockSpec((tk, FOX_W), lambda j, i: (jnp.maximum(i, j), 0)),
                  pl.BlockSpec((tk, FOX_W), lambda j, i: (j, 1)),
                  pl.BlockSpec((tk, FOX_W), lambda j, i: (j, 2)),
                  pl.BlockSpec((tk, FOX_W), lambda j, i: (jnp.maximum(i, j), 0)),
                  rowq, pl.BlockSpec((tk, LANES), lambda j, i: (j, 0)), rowq, rowq],
        out_specs=[pl.BlockSpec((tk, FOX_W), lambda j, i: (j, 0)), pl.BlockSpec((tk, FOX_W), lambda j, i: (j, 0)),
                   pl.BlockSpec((tk, LANES), lambda j, i: (j, 0))],
        out_shape=[jax.ShapeDtypeStruct((t, FOX_W), bf16), jax.ShapeDtypeStruct((t, FOX_W), bf16),
                   jax.ShapeDtypeStruct((t, LANES), f32)],
        scratch_shapes=[pltpu.VMEM((tk, FOX_W), f32), pltpu.VMEM((tk, FOX_W), f32), pltpu.VMEM((tk, LANES), f32)],
        compiler_params=_params(("arbitrary", "arbitrary")),
    )(qkv, qkv, qkv, dmix, cum_t, cum, lse_t, delta_t)


ATT_W = HEADS * LANES


def _data_lane(h):
    return HEAD_D * (h % 2)


def _extra_lane(h):
    return HEAD_D - _data_lane(h)


def _split3(x):
    hi = x.astype(bf16)
    rest = x - hi.astype(f32)
    mid = rest.astype(bf16)
    lo = (rest - mid.astype(f32)).astype(bf16)
    return hi, mid, lo


def _augment(pair, h, first, second, fill=0.0):
    rows = pair.shape[0]
    lane = lax.broadcasted_iota(jnp.int32, (rows, LANES), 1)
    base = _extra_lane(h)
    own = (lane < HEAD_D) if h % 2 == 0 else (lane >= HEAD_D)
    out = jnp.where(own, pair, jnp.full((rows, LANES), fill, bf16))
    for off, src in ((0, first), (3, second)):
        for q in range(3):
            val = src[q] if isinstance(src, tuple) else jnp.full((rows, 1), src, bf16)
            out = jnp.where(lane == base + off + q, val, out)
    return out


def _attn_prep_fwd(qkv, cum, *, tm, name):
    t = qkv.shape[0]

    def body(q_ref, k_ref, v_ref, c_ref, qa_ref, ka_ref, va_ref):
        for h in range(HEADS):
            pair = slice(LANES * (h // 2), LANES * (h // 2 + 1))
            hs = slice(LANES * h, LANES * (h + 1))
            c3 = _split3(c_ref[:, h:h + 1])
            qa_ref[:, hs] = _augment(q_ref[:, pair] * ATT_SCALE, h, c3, 1.0)
            ka_ref[:, hs] = _augment(k_ref[:, pair], h, 1.0, tuple(-p for p in c3))
            va_ref[:, hs] = _augment(v_ref[:, pair], h, 1.0, 1.0, fill=1.0)

    wide = pl.BlockSpec((tm, ATT_W), lambda i: (i, 0))
    out = jax.ShapeDtypeStruct((t, ATT_W), bf16)
    return pl.pallas_call(
        body, name=name, grid=(t // tm,),
        in_specs=[pl.BlockSpec((tm, FOX_W), lambda i: (i, 0)), pl.BlockSpec((tm, FOX_W), lambda i: (i, 1)),
                  pl.BlockSpec((tm, FOX_W), lambda i: (i, 2)), pl.BlockSpec((tm, LANES), lambda i: (i, 0))],
        out_specs=[wide] * 3, out_shape=[out] * 3, compiler_params=_params(("arbitrary",)),
    )(qkv, qkv, qkv, cum)


def _attn_prep_bwd(qkv, cum, lse, dmix, o, *, tm, name):
    t = qkv.shape[0]

    def body(q_ref, c_ref, l_ref, do_ref, o_ref, qa_ref, da_ref):
        for h in range(HEADS):
            pair = slice(LANES * (h // 2), LANES * (h // 2 + 1))
            src = slice(HEAD_D * h, HEAD_D * (h + 1))
            hs = slice(LANES * h, LANES * (h + 1))
            delta = jnp.sum(do_ref[:, src] * o_ref[:, src], axis=-1, keepdims=True)
            qa_ref[:, hs] = _augment(q_ref[:, pair] * ATT_SCALE, h,
                                     _split3(c_ref[:, h:h + 1] - l_ref[:, h:h + 1]), 1.0)
            da_ref[:, hs] = _augment(do_ref[:, pair].astype(bf16), h, tuple(-p for p in _split3(delta)), 0.0)

    wide = pl.BlockSpec((tm, ATT_W), lambda i: (i, 0))
    half = pl.BlockSpec((tm, FOX_W), lambda i: (i, 0))
    col = pl.BlockSpec((tm, LANES), lambda i: (i, 0))
    out = jax.ShapeDtypeStruct((t, ATT_W), bf16)
    return pl.pallas_call(
        body, name=name, grid=(t // tm,), in_specs=[half, col, col, half, half],
        out_specs=[wide] * 2, out_shape=[out] * 2, compiler_params=_params(("arbitrary",)),
    )(qkv, cum, lse, dmix, o)


def _attn_fwd2(q_aug, k_aug, v_aug, *, name, host=None):
    t = q_aug.shape[0]
    n = t // ATT_TILE
    tq = ATT_TILE

    def body(q_ref, k_ref, v_ref, o_ref, lse_ref, acc, m_s):
        i = pl.program_id(0)
        j = pl.program_id(1)

        @pl.when(j == 0)
        def _():
            acc[...] = jnp.zeros_like(acc)
            m_s[...] = jnp.full_like(m_s, NEG_BIG)

        def block(masked):
            mask = _causal(i, j, False) if masked else None
            for h in range(HEADS):
                hs = slice(LANES * h, LANES * (h + 1))
                s = lax.dot_general(q_ref[:, hs], k_ref[:, hs], _NT, preferred_element_type=f32)
                if masked:
                    s = jnp.where(mask, s, NEG_BIG)
                blocks = [s[:, LANES * b:LANES * (b + 1)] for b in range(tq // LANES)]
                m_old = m_s[h]
                m_new = jnp.maximum(m_old, jnp.broadcast_to(
                    jnp.max(functools.reduce(jnp.maximum, blocks), axis=-1, keepdims=True), (tq, LANES)))
                p = jnp.concatenate([jnp.exp(b - m_new) for b in blocks], axis=1).astype(bf16)
                acc[h] = jnp.exp(m_old - m_new) * acc[h] + jnp.dot(p, v_ref[:, hs], preferred_element_type=f32)
                m_s[h] = m_new

        @pl.when(j < i)
        def _():
            block(False)

        @pl.when(j == i)
        def _():
            block(True)
            lse_ref[...] = jnp.zeros_like(lse_ref)
            for h in range(HEADS):
                a = acc[h]
                l = a[:, _extra_lane(h):_extra_lane(h) + 1]
                o_ref[:, HEAD_D * h:HEAD_D * (h + 1)] = a[:, _data_lane(h):_data_lane(h) + HEAD_D] / l
                lse_ref[:, h:h + 1] = m_s[h][:, 0:1] + jnp.log(l)

    kv = pl.BlockSpec((tq, ATT_W), lambda i, j: (jnp.minimum(i, j), 0))
    return _hosted_call(
        host, body, name=name, grid=(n, n),
        in_specs=[pl.BlockSpec((tq, ATT_W), lambda i, j: (i, 0)), kv, kv],
        out_specs=[pl.BlockSpec((tq, FOX_W), lambda i, j: (i, 0)), pl.BlockSpec((tq, LANES), lambda i, j: (i, 0))],
        out_shape=[jax.ShapeDtypeStruct((t, FOX_W), f32), jax.ShapeDtypeStruct((t, LANES), f32)],
        scratch_shapes=[pltpu.VMEM((HEADS, tq, LANES), f32), pltpu.VMEM((HEADS, tq, LANES), f32)],
        compiler_params=_params(("arbitrary", "arbitrary")),
    )(q_aug, k_aug, v_aug)


def _attn_bwd(qb_aug, k_aug, v_aug, do_aug, *, name, host=None):
    t = qb_aug.shape[0]
    n = t // ATT_TILE
    tk = ATT_TILE

    def body(q_ref, k_ref, v_ref, do_ref, dq_ref, dcq_ref, dk_ref, dv_ref, dck_ref, dk_acc, dv_acc, dq_all):
        j = pl.program_id(0)
        i = pl.program_id(1)

        @pl.when(jnp.logical_and(i == 0, j == 0))
        def _():
            dq_all[...] = jnp.zeros_like(dq_all)

        @pl.when(i == 0)
        def _():
            dk_acc[...] = jnp.zeros_like(dk_acc)
            dv_acc[...] = jnp.zeros_like(dv_acc)

        def block(masked):
            mask = _causal(i, j, True) if masked else None
            for h in range(HEADS):
                hs = slice(LANES * h, LANES * (h + 1))
                qh = q_ref[:, hs]
                doh = do_ref[:, hs]
                kh = k_ref[:, hs]
                s_t = lax.dot_general(kh, qh, _NT, preferred_element_type=f32)
                if masked:
                    s_t = jnp.where(mask, s_t, NEG_BIG)
                p_t = jnp.exp(s_t)
                dv_acc[h] += jnp.dot(p_t.astype(bf16), doh, preferred_element_type=f32)
                dp_t = lax.dot_general(v_ref[:, hs], doh, _NT, preferred_element_type=f32)
                ds_t = (p_t * dp_t).astype(bf16)
                dk_acc[h] += jnp.dot(ds_t, qh, preferred_element_type=f32)
                dq_all[i, h] += lax.dot_general(ds_t, kh, _TN, preferred_element_type=f32)

        @pl.when(i > j)
        def _():
            block(False)

        @pl.when(i == j)
        def _():
            block(True)
            dcq_ref[...] = jnp.zeros_like(dcq_ref)
            for h in range(HEADS):
                a = dq_all[j, h]
                dq_ref[:, HEAD_D * h:HEAD_D * (h + 1)] = (
                    a[:, _data_lane(h):_data_lane(h) + HEAD_D] * ATT_SCALE).astype(bf16)
                dcq_ref[:, h:h + 1] = a[:, _extra_lane(h):_extra_lane(h) + 1]

        @pl.when(i == n - 1)
        def _():
            dck_ref[...] = jnp.zeros_like(dck_ref)
            for h in range(HEADS):
                a = dk_acc[h]
                cols = slice(_data_lane(h), _data_lane(h) + HEAD_D)
                dk_ref[:, HEAD_D * h:HEAD_D * (h + 1)] = a[:, cols].astype(bf16)
                dv_ref[:, HEAD_D * h:HEAD_D * (h + 1)] = dv_acc[h][:, cols].astype(bf16)
                dck_ref[:, h:h + 1] = -a[:, _extra_lane(h) + 3:_extra_lane(h) + 4]

    own = pl.BlockSpec((tk, ATT_W), lambda j, i: (j, 0))
    qs = pl.BlockSpec((tk, ATT_W), lambda j, i: (jnp.maximum(i, j), 0))
    half = pl.BlockSpec((tk, FOX_W), lambda j, i: (j, 0))
    col = pl.BlockSpec((tk, LANES), lambda j, i: (j, 0))
    return _hosted_call(
        host, body, name=name, grid=(n, n), in_specs=[qs, own, own, qs],
        out_specs=[half, col, half, half, col],
        out_shape=[jax.ShapeDtypeStruct((t, FOX_W), bf16), jax.ShapeDtypeStruct((t, LANES), f32),
                   jax.ShapeDtypeStruct((t, FOX_W), bf16), jax.ShapeDtypeStruct((t, FOX_W), bf16),
                   jax.ShapeDtypeStruct((t, LANES), f32)],
        scratch_shapes=[pltpu.VMEM((HEADS, tk, LANES), f32), pltpu.VMEM((HEADS, tk, LANES), f32),
                        pltpu.VMEM((n, HEADS, tk, LANES), f32)],
        compiler_params=_params(("arbitrary", "arbitrary")),
    )(qb_aug, k_aug, v_aug, do_aug)


def _attn_dq2(qb_aug, k_aug, v_aug, do_aug, *, name, host=None):
    t = qb_aug.shape[0]
    n = t // ATT_TILE
    tq = ATT_TILE

    def body(q_ref, k_ref, v_ref, do_ref, dq_ref, dc_ref, acc):
        i = pl.program_id(0)
        j = pl.program_id(1)

        @pl.when(j == 0)
        def _():
            acc[...] = jnp.zeros_like(acc)

        def block(masked):
            mask = _causal(i, j, False) if masked else None
            for h in range(HEADS):
                hs = slice(LANES * h, LANES * (h + 1))
                kh = k_ref[:, hs]
                s = lax.dot_general(q_ref[:, hs], kh, _NT, preferred_element_type=f32)
                if masked:
                    s = jnp.where(mask, s, NEG_BIG)
                dp = lax.dot_general(do_ref[:, hs], v_ref[:, hs], _NT, preferred_element_type=f32)
                ds = (jnp.exp(s) * dp).astype(bf16)
                acc[h] += jnp.dot(ds, kh, preferred_element_type=f32)

        @pl.when(j < i)
        def _():
            block(False)

        @pl.when(j == i)
        def _():
            block(True)
            dc_ref[...] = jnp.zeros_like(dc_ref)
            for h in range(HEADS):
                a = acc[h]
                dq_ref[:, HEAD_D * h:HEAD_D * (h + 1)] = (
                    a[:, _data_lane(h):_data_lane(h) + HEAD_D] * ATT_SCALE).astype(bf16)
                dc_ref[:, h:h + 1] = a[:, _extra_lane(h):_extra_lane(h) + 1]

    own = pl.BlockSpec((tq, ATT_W), lambda i, j: (i, 0))
    kv = pl.BlockSpec((tq, ATT_W), lambda i, j: (jnp.minimum(i, j), 0))
    return _hosted_call(
        host, body, name=name, grid=(n, n), in_specs=[own, kv, kv, own],
        out_specs=[pl.BlockSpec((tq, FOX_W), lambda i, j: (i, 0)), pl.BlockSpec((tq, LANES), lambda i, j: (i, 0))],
        out_shape=[jax.ShapeDtypeStruct((t, FOX_W), bf16), jax.ShapeDtypeStruct((t, LANES), f32)],
        scratch_shapes=[pltpu.VMEM((HEADS, tq, LANES), f32)],
        compiler_params=_params(("arbitrary", "arbitrary")),
    )(qb_aug, k_aug, v_aug, do_aug)


def _attn_dkv2(qb_aug, k_aug, v_aug, do_aug, *, name, host=None):
    t = qb_aug.shape[0]
    n = t // ATT_TILE
    tk = ATT_TILE

    def body(q_ref, k_ref, v_ref, do_ref, dk_ref, dv_ref, dc_ref, dk_acc, dv_acc):
        j = pl.program_id(0)
        i = pl.program_id(1)

        @pl.when(i == 0)
        def _():
            dk_acc[...] = jnp.zeros_like(dk_acc)
            dv_acc[...] = jnp.zeros_like(dv_acc)

        def block(masked):
            mask = _causal(i, j, True) if masked else None
            for h in range(HEADS):
                hs = slice(LANES * h, LANES * (h + 1))
                qh = q_ref[:, hs]
                doh = do_ref[:, hs]
                s_t = lax.dot_general(k_ref[:, hs], qh, _NT, preferred_element_type=f32)
                if masked:
                    s_t = jnp.where(mask, s_t, NEG_BIG)
                p_t = jnp.exp(s_t)
                dv_acc[h] += jnp.dot(p_t.astype(bf16), doh, preferred_element_type=f32)
                dp_t = lax.dot_general(v_ref[:, hs], doh, _NT, preferred_element_type=f32)
                dk_acc[h] += jnp.dot((p_t * dp_t).astype(bf16), qh, preferred_element_type=f32)

        @pl.when(i > j)
        def _():
            block(False)

        @pl.when(i == j)
        def _():
            block(True)

        @pl.when(i == n - 1)
        def _():
            dc_ref[...] = jnp.zeros_like(dc_ref)
            for h in range(HEADS):
                a = dk_acc[h]
                cols = slice(_data_lane(h), _data_lane(h) + HEAD_D)
                dk_ref[:, HEAD_D * h:HEAD_D * (h + 1)] = a[:, cols].astype(bf16)
                dv_ref[:, HEAD_D * h:HEAD_D * (h + 1)] = dv_acc[h][:, cols].astype(bf16)
                dc_ref[:, h:h + 1] = -a[:, _extra_lane(h) + 3:_extra_lane(h) + 4]

    own = pl.BlockSpec((tk, ATT_W), lambda j, i: (j, 0))
    qs = pl.BlockSpec((tk, ATT_W), lambda j, i: (jnp.maximum(i, j), 0))
    half = pl.BlockSpec((tk, FOX_W), lambda j, i: (j, 0))
    return _hosted_call(
        host, body, name=name, grid=(n, n), in_specs=[qs, own, own, qs],
        out_specs=[half, half, pl.BlockSpec((tk, LANES), lambda j, i: (j, 0))],
        out_shape=[jax.ShapeDtypeStruct((t, FOX_W), bf16), jax.ShapeDtypeStruct((t, FOX_W), bf16),
                   jax.ShapeDtypeStruct((t, LANES), f32)],
        scratch_shapes=[pltpu.VMEM((HEADS, tk, LANES), f32), pltpu.VMEM((HEADS, tk, LANES), f32)],
        compiler_params=_params(("arbitrary", "arbitrary")),
    )(qb_aug, k_aug, v_aug, do_aug)


LRU_CHUNK = 64
LRU_G = 256
SUB = 8


def _row_ids(n):
    return lax.broadcasted_iota(jnp.int32, (n, LRU_G), 0)


def _shift_rows_down(ext, s):
    return pltpu.roll(ext, s, axis=0)[SUB:, :]


def _shift_rows_up(ext, s, n):
    return pltpu.roll(ext, ext.shape[0] - s, axis=0)[:n, :]


def _lru_gates(u, wa_ref, ba_ref, wx_ref, bx_ref, sp):
    ub = u.astype(bf16)
    r = _sigmoid(jnp.dot(ub, wa_ref[...], preferred_element_type=f32) + ba_ref[...])
    gi = _sigmoid(jnp.dot(ub, wx_ref[...], preferred_element_type=f32) + bx_ref[...])
    log_a = -LRU_C * r * sp
    a = jnp.exp(log_a)
    s = jnp.sqrt(_one_minus_exp(2.0 * log_a))
    return r, gi, a, s


def _conv_window(lx_ref, r0, ci):
    cur = lx_ref[pl.ds(r0, LRU_CHUNK), :]
    p0 = pl.multiple_of(jnp.maximum(r0 - SUB, 0), SUB)
    prev = jnp.where(ci > 0, lx_ref[pl.ds(p0, SUB), :], 0.0)
    return cur, jnp.concatenate([prev, cur], axis=0)


def _lru_fwd(zl, conv_w, conv_b, wa, ba, wx, bx, lam, *, name, host=None):
    t = zl.shape[0]
    n_chunk = t // LRU_CHUNK

    def body(lx_ref, lg_ref, cw_ref, cb_ref, wa_ref, ba_ref, wx_ref, bx_ref, lam_ref, u_ref, h_ref, y_ref):
        sp = _softplus(-lam_ref[...])
        rows = _row_ids(SUB)

        def chunk(ci, hc):
            r0 = pl.multiple_of(ci * LRU_CHUNK, LRU_CHUNK)
            cur, ext = _conv_window(lx_ref, r0, ci)
            u = cb_ref[...] + cw_ref[3:4, :] * cur
            for k in range(3):
                u = u + cw_ref[k:k + 1, :] * _shift_rows_down(ext, 3 - k)
            r, gi, a, s = _lru_gates(u, wa_ref, ba_ref, wx_ref, bx_ref, sp)
            b = s * (gi * u)
            tiles = []
            for q in range(LRU_CHUNK // SUB):
                ta = a[SUB * q:SUB * (q + 1), :]
                tb = b[SUB * q:SUB * (q + 1), :]
                for d in (1, 2, 4):
                    a_sh = jnp.where(rows >= d, pltpu.roll(ta, d, axis=0), 1.0)
                    b_sh = jnp.where(rows >= d, pltpu.roll(tb, d, axis=0), 0.0)
                    tb = ta * b_sh + tb
                    ta = ta * a_sh
                hq = tb + ta * hc
                hc = hq[SUB - 1:SUB, :]
                tiles.append(hq)
            h = jnp.concatenate(tiles, axis=0)
            u_ref[pl.ds(r0, LRU_CHUNK), :] = u
            h_ref[pl.ds(r0, LRU_CHUNK), :] = h
            gel, _ = _gelu_and_grad(lg_ref[pl.ds(r0, LRU_CHUNK), :])
            y_ref[pl.ds(r0, LRU_CHUNK), :] = gel * h
            return hc

        lax.fori_loop(0, n_chunk, chunk, jnp.zeros((1, LRU_G), f32))

    seq = lambda cb: pl.BlockSpec((t, LRU_G), lambda c, cb=cb: (0, c + cb))
    rowc = pl.BlockSpec((1, LRU_G), lambda c: (0, c))
    diag = pl.BlockSpec((LRU_G, LRU_G), lambda c: (c, c))
    out = jax.ShapeDtypeStruct((t, LRU_W), f32)
    return _hosted_call(
        host, body, name=name, grid=(LRU_W // LRU_G,),
        in_specs=[seq(0), seq(LRU_W // LRU_G), pl.BlockSpec((4, LRU_G), lambda c: (0, c)),
                  rowc, diag, rowc, diag, rowc, rowc],
        out_specs=[seq(0)] * 3, out_shape=[out] * 3,
        compiler_params=_params(("arbitrary",)),
    )(zl, zl, conv_w, conv_b, wa, ba, wx, bx, lam)


def _lru_bwd(dmix, zl, u_all, h_all, conv_w, wa, ba, wx, bx, lam, *, name, host=None):
    t = zl.shape[0]
    n_chunk = t // LRU_CHUNK

    def body(dy_ref, lx_ref, lg_ref, u_ref, h_ref, cw_ref, wa_ref, ba_ref, wx_ref, bx_ref, lam_ref,
             dlx_ref, dlg_ref, dcw_ref, dcb_ref, dba_ref, dbx_ref, dlam_ref, dwa_ref, dwx_ref, dpr_s, dpx_s):
        lam_v = lam_ref[...]
        sp = _softplus(-lam_v)
        rows = _row_ids(SUB)
        rows_c = _row_ids(LRU_CHUNK)
        zero_row = jnp.zeros((1, LRU_G), f32)

        def chunk(step, carry):
            dh_c, a_next0, du_next, dsp, dba, dbx, dcb, dw0, dw1, dw2, dw3 = carry
            ci = n_chunk - 1 - step
            r0 = pl.multiple_of(ci * LRU_CHUNK, LRU_CHUNK)
            sl = pl.ds(r0, LRU_CHUNK)
            u = u_ref[sl, :]
            r, gi, a, s = _lru_gates(u, wa_ref, ba_ref, wx_ref, bx_ref, sp)
            h = h_ref[sl, :]
            p0 = pl.multiple_of(jnp.maximum(r0 - SUB, 0), SUB)
            h_before = jnp.where(ci > 0, h_ref[pl.ds(p0, SUB), :], 0.0)[SUB - 1:SUB, :]
            h_prev = jnp.where(rows_c == 0, h_before, pltpu.roll(h, 1, axis=0))
            gel, dgel = _gelu_and_grad(lg_ref[sl, :])
            dy = dy_ref[sl, :]
            dlg_ref[sl, :] = (dy * h * dgel).astype(bf16)
            g_in = dy * gel
            a_next = jnp.where(rows_c == LRU_CHUNK - 1, a_next0, pltpu.roll(a, LRU_CHUNK - 1, axis=0))
            tiles = [None] * (LRU_CHUNK // SUB)
            for q in reversed(range(LRU_CHUNK // SUB)):
                ta = a_next[SUB * q:SUB * (q + 1), :]
                tb = g_in[SUB * q:SUB * (q + 1), :]
                for d in (1, 2, 4):
                    a_sh = jnp.where(rows < SUB - d, pltpu.roll(ta, SUB - d, axis=0), 1.0)
                    b_sh = jnp.where(rows < SUB - d, pltpu.roll(tb, SUB - d, axis=0), 0.0)
                    tb = ta * b_sh + tb
                    ta = ta * a_sh
                dhq = tb + ta * dh_c
                dh_c = dhq[0:1, :]
                tiles[q] = dhq
            dh = jnp.concatenate(tiles, axis=0)
            da = dh * h_prev
            ds = dh * gi * u
            dgi = dh * s * u
            du = dh * s * gi
            dlog_a = da * a - ds * (a * a) / s
            dr = dlog_a * (-LRU_C * sp)
            dsp = dsp + jnp.sum(dlog_a * (-LRU_C * r), axis=0, keepdims=True)
            dpr = dr * r * (1.0 - r)
            dpx = dgi * gi * (1.0 - gi)
            dprb = dpr.astype(bf16)
            dpxb = dpx.astype(bf16)
            dpr_s[sl, :] = dprb
            dpx_s[sl, :] = dpxb
            du = du + (lax.dot_general(dprb, wa_ref[...], _NT, preferred_element_type=f32)
                       + lax.dot_general(dpxb, wx_ref[...], _NT, preferred_element_type=f32))
            dba = dba + jnp.sum(dpr, axis=0, keepdims=True)
            dbx = dbx + jnp.sum(dpx, axis=0, keepdims=True)
            dcb = dcb + jnp.sum(du, axis=0, keepdims=True)
            du_ext = jnp.concatenate([du, du_next], axis=0)
            dlx = cw_ref[3:4, :] * du
            for k in range(3):
                dlx = dlx + cw_ref[k:k + 1, :] * _shift_rows_up(du_ext, 3 - k, LRU_CHUNK)
            dlx_ref[sl, :] = dlx.astype(bf16)
            cur, ext = _conv_window(lx_ref, r0, ci)
            dws = [dw0, dw1, dw2, dw3 + jnp.sum(du * cur, axis=0, keepdims=True)]
            for k in range(3):
                dws[k] = dws[k] + jnp.sum(du * _shift_rows_down(ext, 3 - k), axis=0, keepdims=True)
            return (dh_c, a[0:1, :], du[0:SUB, :], dsp, dba, dbx, dcb, dws[0], dws[1], dws[2], dws[3])

        init = (zero_row, zero_row, jnp.zeros((SUB, LRU_G), f32)) + (zero_row,) * 8
        out = lax.fori_loop(0, n_chunk, chunk, init)
        _, _, _, dsp, dba, dbx, dcb, dw0, dw1, dw2, dw3 = out
        dlam_ref[...] = dsp * (-_sigmoid(-lam_v))
        dba_ref[...] = dba
        dbx_ref[...] = dbx
        dcb_ref[...] = dcb
        dcw_ref[...] = jnp.concatenate([dw0, dw1, dw2, dw3], axis=0)
        ub = u_ref[...].astype(bf16)
        dwa_ref[...] = lax.dot_general(ub, dpr_s[...], _TN, preferred_element_type=f32)
        dwx_ref[...] = lax.dot_general(ub, dpx_s[...], _TN, preferred_element_type=f32)

    seq = lambda cb: pl.BlockSpec((t, LRU_G), lambda c, cb=cb: (0, c + cb))
    rowc = pl.BlockSpec((1, LRU_G), lambda c: (0, c))
    diag = pl.BlockSpec((LRU_G, LRU_G), lambda c: (c, c))
    gate_out = pl.BlockSpec((None, LRU_G, LRU_G), lambda c: (c, 0, 0))
    row_shape = jax.ShapeDtypeStruct((1, LRU_W), f32)
    return _hosted_call(
        host, body, name=name, grid=(LRU_W // LRU_G,),
        in_specs=[seq(LRU_W // LRU_G), seq(0), seq(LRU_W // LRU_G), seq(0), seq(0),
                  pl.BlockSpec((4, LRU_G), lambda c: (0, c)),
                  diag, rowc, diag, rowc, rowc],
        out_specs=[seq(0), seq(0), pl.BlockSpec((4, LRU_G), lambda c: (0, c)), rowc, rowc, rowc, rowc,
                   gate_out, gate_out],
        out_shape=[jax.ShapeDtypeStruct((t, LRU_W), bf16)] * 2
        + [jax.ShapeDtypeStruct((4, LRU_W), f32)] + [row_shape] * 4
        + [jax.ShapeDtypeStruct((LRU_W // LRU_G, LRU_G, LRU_G), f32)] * 2,
        scratch_shapes=[pltpu.VMEM((t, LRU_G), bf16), pltpu.VMEM((t, LRU_G), bf16)],
        compiler_params=_params(("arbitrary",)),
    )(dmix, zl, zl, u_all, h_all, conv_w, wa, ba, wx, bx, lam)


def _block_diag(w):
    eye = jnp.eye(HEADS, dtype=w.dtype)
    return jnp.einsum("hij,hk->hikj", w, eye).reshape(LRU_W, LRU_W)


def _diag_blocks(dw):
    per = dw.shape[1] // HEAD_D
    blocks = [dw[:, HEAD_D * b:HEAD_D * (b + 1), HEAD_D * b:HEAD_D * (b + 1)] for b in range(per)]
    return jnp.stack(blocks, axis=1).reshape(HEADS, HEAD_D, HEAD_D)


def _local_step(x, target, sent, small, *, tm=512, tm_ffn=1024):
    t = x.shape[0]
    ones = jnp.ones((1, D_MODEL), f32)
    zeros = jnp.zeros((1, D_MODEL), f32)
    ln1 = (small["ln1_g"], small["ln1_b"])
    ln2 = (small["ln2_g"], small["ln2_b"])
    ln3 = (small["ln3_g"], small["ln3_b"])

    wg1, wu1, wd1 = _exchange([sent["ffn1_w_gate"], sent["ffn1_w_up"], sent["ffn1_w_down"]], gather=True,
                              name="gather_ffn1")
    xh1, rs1, hg1, hu1, w_in_g, w_out_g, conv_w_g = _ffn_fwd(
        x, ones, zeros, wg1, wu1, wd1, tm=tm_ffn, name="ffn1_fwd",
        host=_Exchange([sent["w_in"], sent["w_out"], sent["conv_w"]], gather=True))
    w_in = jnp.pad(w_in_g.transpose(1, 0, 2).reshape(D_MODEL, IN_COLS), ((0, 0), (0, 21 * LANES - IN_COLS)))
    w_out = w_out_g.reshape(D_MODEL, D_MODEL)
    conv_w = conv_w_g.transpose(1, 0, 2).reshape(4, LRU_W)
    qkv, zl, zfg = _in_proj(xh1, ln1[0], ln1[1], w_in, tm=tm, name="in_proj")
    bfg = jnp.pad(small["b_forget"], ((0, 0), (0, LANES - HEADS)))
    cum = _cum_fwd(zfg, bfg, name="cum_fwd")
    q_aug, k_aug, v_aug = _attn_prep_fwd(qkv, cum, tm=tm, name="attn_prep_fwd")
    o, lse, wg2, wu2 = _attn_fwd2(q_aug, k_aug, v_aug, name="attn_fwd",
                                  host=_Exchange([sent["ffn2_w_gate"], sent["ffn2_w_up"]], gather=True))
    wa_bd = _block_diag(small["rg_wa"]).astype(bf16)
    wx_bd = _block_diag(small["rg_wx"]).astype(bf16)
    ba = small["rg_ba"].reshape(1, LRU_W)
    bx = small["rg_bx"].reshape(1, LRU_W)
    u, h, lru, wd2 = _lru_fwd(zl, conv_w, small["conv_b"], wa_bd, ba, wx_bd, bx, small["lru_lambda"],
                              name="lru_fwd", host=_Exchange([sent["ffn2_w_down"]], gather=True))
    xh2, rs2 = _mmln([(o, 0, FOX_W, w_out, 0, D_MODEL, "nn"), (lru, 0, LRU_W, w_out, 1, D_MODEL, "nn")],
                     tm=tm, name="mix_fwd", resid=("affine", xh1) + ln1, resid_scale=ALPHA, epi="ln_fwd")
    xh3, rs3, hg2, hu2 = _ffn_fwd(xh2, ln2[0], ln2[1], wg2, wu2, wd2, tm=tm_ffn, name="ffn2_fwd")

    dpre3, sq_rows, g_ln3g, g_ln3b = _loss_bwd(xh3, rs3, ln3[0], ln3[1], target, tm=tm, name="loss_bwd")
    dpre2, g_ln2g, g_ln2b, dhg2, dhu2, a2 = _ffn_bwd(dpre3, hg2, hu2, wg2, wu2, wd2,
                                                     (xh2, rs2, ln2[0]), tm=tm_ffn, name="ffn2_bwd")
    wgrad = dict(out_dtype=bf16, tm=D_MODEL, mb=1, tn=FF_TILE, nb=4, tk=512, pair=True)
    wdgrad = dict(out_dtype=bf16, tm=512, mb=4, tn=D_MODEL, nb=1, tk=512, out_scale=0.5, pair=True)
    between_chips = functools.partial(_Exchange, gather=False, chips=True)
    g_wg2 = _mm_tn(xh2, dhg2, name="g_wg2", affine=ln2, **wgrad)
    g_wu2 = _mm_tn(xh2, dhu2, name="g_wu2", affine=ln2, **wgrad)
    g_wd2 = _mm_tn(a2, dpre3, name="g_wd2", **wdgrad)

    dmix = _mmln([(dpre2, 0, D_MODEL, w_out, 0, D_MODEL, "nt")], tm=tm, name="dmix_bwd")
    g_wout_a = _mm(o, dpre2, mode="tn", out_dtype=bf16, tm=512, tn=D_MODEL, tk=512, name="g_wout_fox")
    g_wout_b = _mm(lru, dpre2, mode="tn", out_dtype=bf16, tm=512, tn=D_MODEL, tk=512, name="g_wout_lru")
    dlx, dlg, g_cw, g_cb, g_ba, g_bx, g_lam, g_wa4, g_wx4, *p_wg2 = _lru_bwd(
        dmix, zl, u, h, conv_w, wa_bd, ba, wx_bd, bx, small["lru_lambda"], name="lru_bwd",
        host=between_chips([g_wg2]))
    p_wg2 = p_wg2[0]
    qb_aug, do_aug = _attn_prep_bwd(qkv, cum, lse, dmix, o, tm=tm, name="attn_prep_bwd")
    dq, dcum_q, dk, dv, dcum_k, p_wu2, p_wd2 = _attn_bwd(qb_aug, k_aug, v_aug, do_aug, name="attn_bwd",
                                                         host=between_chips([g_wu2, g_wd2]))
    g_wout_blocked = jnp.concatenate([g_wout_a, g_wout_b], axis=0).reshape(N_DEV, D_MODEL // N_DEV, D_MODEL)
    dfg, g_bf = _cum_bwd(dcum_q, dcum_k, zfg, bfg, name="cum_bwd")

    dz = [(dq, 0, 512), (dk, 1, 512), (dv, 2, 512), (dlx, 3, 512), (dlg, 4, 512), (dfg, 20, LANES)]
    dpre1, g_ln1g, g_ln1b = _mmln(
        [(arr, 0, w, w_in, cb, w, "nt") for (arr, cb, w) in dz],
        tm=tm, name="dx1_bwd", resid=("plain", dpre2), resid_scale=ALPHA, epi="ln_bwd", ln=(xh1, rs1, ln1[0]))
    g_win_main = _mm_tn(xh1, [arr for arr, _, _ in dz[:5]], out_dtype=bf16, tm=D_MODEL, mb=1, tn=512, nb=5, tk=512,
                        name="g_win", affine=ln1, out_blocked=True)
    g_win = [g_win_main[n] for n in range(5)] + [
        _mm(xh1, dfg, mode="tn", out_dtype=bf16, tm=D_MODEL, tn=LANES, tk=512, name="g_win_fg", affine=ln1)]
    g_win_full = jnp.concatenate([g[:, :w] for g, (_, _, w) in zip(g_win, dz)], axis=1)[:, :IN_COLS]
    g_win_blocked = g_win_full.reshape(D_MODEL, N_DEV, IN_SHARD).transpose(1, 0, 2)
    dhg1, dhu1, a1, p_win, p_wout = _ffn_bwd_act(dpre1, hg1, hu1, wd1, tm=tm_ffn, name="ffn1_bwd_act",
                                                 host=_Exchange([g_win_blocked, g_wout_blocked], gather=False))
    small_g = {
        "ln1_g": g_ln1g, "ln1_b": g_ln1b, "b_forget": g_bf[:, :HEADS], "conv_w": g_cw, "conv_b": g_cb,
        "rg_wa": _diag_blocks(g_wa4), "rg_ba": g_ba.reshape(HEADS, HEAD_D),
        "rg_wx": _diag_blocks(g_wx4), "rg_bx": g_bx.reshape(HEADS, HEAD_D), "lru_lambda": g_lam,
        "ln2_g": g_ln2g, "ln2_b": g_ln2b, "ln3_g": g_ln3g, "ln3_b": g_ln3b,
    }
    small_g["loss"] = (0.5 / D_MODEL) * jnp.sum(sq_rows, keepdims=True)
    pieces = [small_g[n].reshape(-1) for n in PACKED]
    packed = jnp.concatenate(pieces + [jnp.zeros((PACK_ROWS * LANES - sum(p.shape[0] for p in pieces),), f32)])
    g_wg1, all_packed = _mm_tn(x, dhg1, name="g_wg1",
                               host=_Exchange([packed.reshape(PACK_ROWS, LANES)], gather=True), **wgrad)
    g_wu1, p_wg1 = _mm_tn(x, dhu1, name="g_wu1", host=between_chips([g_wg1]), **wgrad)
    g_wd1, p_wu1 = _mm_tn(a1, dpre1, name="g_wd1", host=between_chips([g_wu1]), **wdgrad)
    grad_x, p_wd1 = _ffn_bwd_dx(dpre1, dhg1, dhu1, wg1, wu1, tm=tm_ffn, name="ffn1_bwd_dx",
                                host=between_chips([g_wd1]))
    parts = {
        "ffn1_w_gate": p_wg1, "ffn1_w_up": p_wu1, "ffn1_w_down": p_wd1, "w_in": p_win, "w_out": p_wout,
        "ffn2_w_gate": p_wg2, "ffn2_w_up": p_wu2, "ffn2_w_down": p_wd2,
    }
    return sq_rows, grad_x, parts, all_packed, {n: small_g[n].shape for n in PACKED}


def _adam_math(w, g, m, v):
    m2 = ADAM_B1 * m + (1.0 - ADAM_B1) * g
    v2 = ADAM_B2 * v + (1.0 - ADAM_B2) * (g * g)
    m_hat = m2 / (1.0 - ADAM_B1 ** ADAM_STEP)
    v_hat = v2 / (1.0 - ADAM_B2 ** ADAM_STEP)
    delta = -ADAM_LR * (m_hat / (jnp.sqrt(v_hat) + ADAM_EPS) + ADAM_WD * w)
    return delta, m2, v2


ADAM_TILE_ELEMS = 128 * 1024


def _adamw_big(parts, w, m, v, *, name):
    r, c = w.shape
    n_parts = parts.shape[0]
    tr = max(d for d in range(8, r + 1, 8) if r % d == 0 and d * c <= ADAM_TILE_ELEMS)

    def body(p_ref, w_ref, m_ref, v_ref, g_ref, d_ref, m2_ref, v2_ref):
        g = p_ref[0].astype(f32)
        for q in range(1, n_parts):
            g = g + p_ref[q].astype(f32)
        d, m2, v2 = _adam_math(w_ref[...], g, m_ref[...], v_ref[...])
        g_ref[...] = g
        d_ref[...] = d
        m2_ref[...] = m2
        v2_ref[...] = v2

    blk = pl.BlockSpec((tr, c), lambda i: (i, 0))
    return pl.pallas_call(
        body, name=name, grid=(r // tr,),
        in_specs=[pl.BlockSpec((n_parts, tr, c), lambda i: (0, i, 0)), blk, blk, blk],
        out_specs=[blk] * 4, out_shape=[jax.ShapeDtypeStruct((r, c), f32)] * 4,
        compiler_params=_params(("arbitrary",)),
    )(parts, w, m, v)


def _adamw_small(items, *, name):
    n = len(items)

    def body(*refs):
        ins, outs = refs[:4 * n], refs[4 * n:]
        for k in range(n):
            g, w, m, v = (ins[4 * k + q][...] for q in range(4))
            d, m2, v2 = _adam_math(w, g, m, v)
            outs[3 * k][...] = d
            outs[3 * k + 1][...] = m2
            outs[3 * k + 2][...] = v2

    vm = pl.BlockSpec(memory_space=pltpu.VMEM)
    flat = [a for item in items for a in item]
    out_shape = [jax.ShapeDtypeStruct(item[1].shape, f32) for item in items for _ in range(3)]
    return pl.pallas_call(
        body, name=name, in_specs=[vm] * (4 * n), out_specs=[vm] * (3 * n), out_shape=out_shape,
    )(*flat)


def _sum_parts(parts, *, name):
    def body(p_ref, o_ref):
        acc = p_ref[0]
        for q in range(1, N_DEV):
            acc = acc + p_ref[q]
        o_ref[...] = acc

    vm = pl.BlockSpec(memory_space=pltpu.VMEM)
    return pl.pallas_call(
        body, name=name, in_specs=[vm], out_specs=vm, out_shape=jax.ShapeDtypeStruct(parts.shape[1:], f32),
    )(parts)


WEIGHTS = ["ffn1_w_gate", "ffn1_w_up", "ffn1_w_down", "ln1_g", "ln1_b", "w_in", "b_forget", "conv_w", "conv_b",
           "rg_wa", "rg_ba", "rg_wx", "rg_bx", "lru_lambda", "w_out", "ln2_g", "ln2_b",
           "ffn2_w_gate", "ffn2_w_up", "ffn2_w_down", "ln3_g", "ln3_b"]
BIG = ["ffn1_w_gate", "ffn1_w_up", "ffn1_w_down", "w_in", "w_out", "ffn2_w_gate", "ffn2_w_up", "ffn2_w_down"]
PACKED = ["ln1_g", "ln1_b", "ln2_g", "ln2_b", "ln3_g", "ln3_b", "conv_b", "rg_ba", "rg_bx", "lru_lambda",
          "conv_w", "rg_wa", "rg_wx", "b_forget", "loss"]
PACK_ROWS = 600


def _two_d(a):
    return a.reshape((-1, a.shape[-1]))


def _transport(a):
    return _two_d(a)


def kernel(x, ffn1_w_gate, ffn1_w_up, ffn1_w_down, ln1_g, ln1_b, w_in, b_forget, conv_w, conv_b, rg_wa, rg_ba, rg_wx, rg_bx, lru_lambda, w_out, ln2_g, ln2_b, ffn2_w_gate, ffn2_w_up, ffn2_w_down, ln3_g, ln3_b, loss_target, m_ffn1_w_gate, m_ffn1_w_up, m_ffn1_w_down, m_ln1_g, m_ln1_b, m_w_in, m_b_forget, m_conv_w, m_conv_b, m_rg_wa, m_rg_ba, m_rg_wx, m_rg_bx, m_lru_lambda, m_w_out, m_ln2_g, m_ln2_b, m_ffn2_w_gate, m_ffn2_w_up, m_ffn2_w_down, m_ln3_g, m_ln3_b, v_ffn1_w_gate, v_ffn1_w_up, v_ffn1_w_down, v_ln1_g, v_ln1_b, v_w_in, v_b_forget, v_conv_w, v_conv_b, v_rg_wa, v_rg_ba, v_rg_wx, v_rg_bx, v_lru_lambda, v_w_out, v_ln2_g, v_ln2_b, v_ffn2_w_gate, v_ffn2_w_up, v_ffn2_w_down, v_ln3_g, v_ln3_b):
    w_args = (ffn1_w_gate, ffn1_w_up, ffn1_w_down, ln1_g, ln1_b, w_in, b_forget, conv_w, conv_b, rg_wa, rg_ba, rg_wx, rg_bx, lru_lambda, w_out, ln2_g, ln2_b, ffn2_w_gate, ffn2_w_up, ffn2_w_down, ln3_g, ln3_b)
    m_args = (m_ffn1_w_gate, m_ffn1_w_up, m_ffn1_w_down, m_ln1_g, m_ln1_b, m_w_in, m_b_forget, m_conv_w, m_conv_b, m_rg_wa, m_rg_ba, m_rg_wx, m_rg_bx, m_lru_lambda, m_w_out, m_ln2_g, m_ln2_b, m_ffn2_w_gate, m_ffn2_w_up, m_ffn2_w_down, m_ln3_g, m_ln3_b)
    v_args = (v_ffn1_w_gate, v_ffn1_w_up, v_ffn1_w_down, v_ln1_g, v_ln1_b, v_w_in, v_b_forget, v_conv_w, v_conv_b, v_rg_wa, v_rg_ba, v_rg_wx, v_rg_bx, v_lru_lambda, v_w_out, v_ln2_g, v_ln2_b, v_ffn2_w_gate, v_ffn2_w_up, v_ffn2_w_down, v_ln3_g, v_ln3_b)
    w = dict(zip(WEIGHTS, w_args))
    m = dict(zip(WEIGHTS, m_args))
    v = dict(zip(WEIGHTS, v_args))
    me = 4 * lax.axis_index("x") + 2 * lax.axis_index("y") + lax.axis_index("c")

    sent = {n: _transport(w[n]).astype(bf16) for n in BIG}
    sent["conv_w"] = _two_d(w["conv_w"])
    small = {n: w[n] for n in ("ln1_g", "ln1_b", "ln2_g", "ln2_b", "ln3_g", "ln3_b", "b_forget", "conv_b",
                               "lru_lambda")}
    small.update({n: w[n][0] for n in ("rg_wa", "rg_ba", "rg_wx", "rg_bx")})

    sq_rows, grad_x, parts, all_packed, small_shapes = _local_step(x[0], loss_target[0], sent, small)

    total = _sum_parts(all_packed, name="sum_small_grads").reshape(-1)
    grads, off = {}, 0
    for n in PACKED:
        size = math.prod(small_shapes[n])
        grads[n] = total[off:off + size].reshape(small_shapes[n])
        off += size
    loss = grads.pop("loss").reshape(())
    grads["conv_w"] = lax.dynamic_slice_in_dim(grads["conv_w"], me * (LRU_W // N_DEV), LRU_W // N_DEV, axis=1)

    delta, new_m, new_v = {}, {}, {}
    for n in BIG:
        g, d, m2, v2 = _adamw_big(parts[n], _transport(w[n]), _transport(m[n]), _transport(v[n]),
                                  name="adamw_" + n)
        grads[n], delta[n], new_m[n], new_v[n] = g, d, m2, v2
    small_names = [n for n in WEIGHTS if n not in BIG]
    outs = _adamw_small([(_two_d(grads[n]), _two_d(w[n]), _two_d(m[n]), _two_d(v[n])) for n in small_names],
                        name="adamw_small")
    for k, n in enumerate(small_names):
        delta[n], new_m[n], new_v[n] = outs[3 * k], outs[3 * k + 1], outs[3 * k + 2]

    def shaped(d):
        return [d[n].reshape(w[n].shape) for n in WEIGHTS]

    return (loss, grad_x[None], *shaped(grads), *shaped(delta), *shaped(new_m), *shaped(new_v))
```

```python
import functools
import math

import jax
import jax.numpy as jnp
from jax import lax
from jax.experimental import pallas as pl
from jax.experimental.pallas import tpu as pltpu

f32 = jnp.float32
bf16 = jnp.bfloat16

N_DEV = 8
D_MODEL = 1024
D_FF = 4096
FF_TILE = D_FF // N_DEV
FOX_W = 512
LRU_W = 512
HEADS = 8
HEAD_D = 64
IN_COLS = 2568
IN_SHARD = IN_COLS // N_DEV
LANES = 128
LN_EPS = 1e-5
ALPHA = 2.0 ** 0.25
ATT_SCALE = 1.0 / math.sqrt(HEAD_D)
LRU_C = 8.0
NEG_BIG = -1e30

ADAM_LR = 0.001
ADAM_B1 = 0.9
ADAM_B2 = 0.999
ADAM_EPS = 1e-08
ADAM_WD = 0.01
ADAM_STEP = 10

VMEM_LIMIT = 56 * 1024 * 1024
MESH_T = pl.DeviceIdType.MESH


def _params(sem, **kw):
    return pltpu.CompilerParams(dimension_semantics=sem, vmem_limit_bytes=VMEM_LIMIT, **kw)


def _sigmoid(x):
    return 1.0 / (1.0 + jnp.exp(-x))


def _sigmoid_tanh(x):
    return 0.5 * jnp.tanh(0.5 * x) + 0.5


def _softplus(x):
    return jnp.maximum(x, 0.0) + jnp.log(1.0 + jnp.exp(-jnp.abs(x)))


def _one_minus_exp(x):
    series = -x * (1.0 + x * (0.5 + x * (1.0 / 6 + x * (1.0 / 24 + x * (1.0 / 120 + x * (1.0 / 720))))))
    return jnp.where(x > -0.125, series, 1.0 - jnp.exp(x))


_GELU_C = math.sqrt(2.0 / math.pi)


def _gelu_and_grad(x):
    inner = _GELU_C * (x + 0.044715 * x * x * x)
    t = jnp.tanh(inner)
    g = 0.5 * x * (1.0 + t)
    dg = 0.5 * (1.0 + t) + 0.5 * x * (1.0 - t * t) * _GELU_C * (1.0 + 3 * 0.044715 * x * x)
    return g, dg


def _ln_fwd_tile(pre):
    mu = jnp.mean(pre, axis=-1, keepdims=True)
    xc = pre - mu
    var = jnp.mean(xc * xc, axis=-1, keepdims=True)
    rstd = lax.rsqrt(var + LN_EPS)
    return xc * rstd, rstd


def _ln_bwd_tile(dy, xhat, rstd, g):
    dyg = dy * g
    m1 = jnp.mean(dyg, axis=-1, keepdims=True)
    m2 = jnp.mean(dyg * xhat, axis=-1, keepdims=True)
    dpre = rstd * (dyg - m1 - xhat * m2)
    return dpre, jnp.sum(dy * xhat, axis=0, keepdims=True), jnp.sum(dy, axis=0, keepdims=True)


_NT = (((1,), (1,)), ((), ()))
_TN = (((0,), (0,)), ((), ()))


class _Exchange:
    def __init__(self, arrs, gather, chips=False):
        self.arrs, self.gather, self.n, self.chips = list(arrs), gather, len(arrs), chips

    def out_shape(self):
        return [jax.ShapeDtypeStruct(((N_DEV,) + a.shape) if self.gather else a.shape, a.dtype) for a in self.arrs]

    def scratch(self):
        n_remote = self.n * (N_DEV - 1)
        return [pltpu.SemaphoreType.DMA((n_remote,)), pltpu.SemaphoreType.DMA((n_remote,)),
                pltpu.SemaphoreType.DMA((self.n,))]

    def copies(self, ins, outs, sems):
        send_sems, recv_sems, local_sems = sems
        x, y, c = lax.axis_index("x"), lax.axis_index("y"), lax.axis_index("c")
        me = 2 * x + y if self.chips else 4 * x + 2 * y + c
        out = []
        for k in range(self.n):
            for d in (range(2, N_DEV, 2) if self.chips else range(1, N_DEV)):
                px = 1 - x if d & 4 else x
                py = 1 - y if d & 2 else y
                pc = 1 - c if d & 1 else c
                sem = k * (N_DEV - 1) + d - 1
                out.append(pltpu.make_async_remote_copy(
                    src_ref=ins[k].at[2 * px + py if self.chips else 4 * px + 2 * py + pc], dst_ref=outs[k].at[me],
                    send_sem=send_sems.at[sem], recv_sem=recv_sems.at[sem],
                    device_id=(px, py, pc), device_id_type=MESH_T))
            out.append(pltpu.make_async_copy(ins[k].at[me], outs[k].at[me], local_sems.at[k]))
        return out

    def gather_copies(self, ins, outs, sems):
        send_sems, recv_sems, local_sems = sems
        x, y, c = lax.axis_index("x"), lax.axis_index("y"), lax.axis_index("c")
        sibling = (x, y, 1 - c)
        chips = [(1 - x, y), (x, 1 - y), (1 - x, 1 - y)]
        out = []
        for k in range(self.n):
            def copy(s, block, to, src=None, k=k):
                rows = outs[k].at[4 * block[0] + 2 * block[1] + block[2]]
                sem = k * (N_DEV - 1) + s
                return pltpu.make_async_remote_copy(
                    src_ref=rows if src is None else src, dst_ref=rows, send_sem=send_sems.at[sem],
                    recv_sem=recv_sems.at[sem], device_id=to, device_id_type=MESH_T)

            first = [copy(0, (x, y, c), sibling, src=ins[k])]
            first += [copy(1 + q, (x, y, c), (*chip, c), src=ins[k]) for q, chip in enumerate(chips)]
            passed = [copy(4 + q, (*chip, c), sibling) for q, chip in enumerate(chips)]
            own = pltpu.make_async_copy(ins[k], outs[k].at[4 * x + 2 * y + c], local_sems.at[k])
            out.append((first, passed, own, copy))
        return out, sibling, chips, (x, y, c)

    def start(self, ins, outs, sems):
        if not self.gather:
            for cp in self.copies(ins, outs, sems):
                cp.start()
            return
        per_array, _, _, _ = self.gather_copies(ins, outs, sems)
        for first, _, own, _ in per_array:
            own.start()
            for cp in first:
                cp.start()

    def relay(self, ins, outs, sems):
        per_array, sibling, chips, (x, y, c) = self.gather_copies(ins, outs, sems)
        for first, passed, own, copy in per_array:
            for q, chip in enumerate(chips):
                copy(1 + q, (*chip, c), (x, y, c)).wait_recv()
                passed[q].start()

    def wait(self, ins, outs, sems, relayed=False):
        if not self.gather:
            for cp in self.copies(ins, outs, sems):
                cp.wait()
            return
        if not relayed:
            self.relay(ins, outs, sems)
        per_array, sibling, chips, (x, y, c) = self.gather_copies(ins, outs, sems)
        for first, passed, own, copy in per_array:
            copy(0, sibling, (x, y, c)).wait_recv()
            for q, chip in enumerate(chips):
                copy(4 + q, (*chip, 1 - c), (x, y, c)).wait_recv()
            for cp in first + passed:
                cp.wait_send()
            own.wait()


def _hosted_call(host, body, *, name, grid, in_specs, out_specs, out_shape, scratch_shapes=(), compiler_params):
    out_specs = list(out_specs) if isinstance(out_specs, (list, tuple)) else [out_specs]
    out_shape = list(out_shape) if isinstance(out_shape, (list, tuple)) else [out_shape]
    if host is None:
        return pl.pallas_call(body, name=name, grid=grid, in_specs=in_specs, out_specs=out_specs,
                              out_shape=out_shape, scratch_shapes=list(scratch_shapes),
                              compiler_params=compiler_params)
    n_in, n_out, n_scr, k = len(in_specs), len(out_shape), len(scratch_shapes), host.n

    def wrapped(*refs):
        ins, h_in = refs[:n_in], refs[n_in:n_in + k]
        outs, h_out = refs[n_in + k:n_in + k + n_out], refs[n_in + k + n_out:n_in + 2 * k + n_out]
        scr, sems = refs[n_in + 2 * k + n_out:n_in + 2 * k + n_out + n_scr], refs[n_in + 2 * k + n_out + n_scr:]
        ids = [pl.program_id(a) for a in range(len(grid))]
        first = functools.reduce(jnp.logical_and, [i == 0 for i in ids])
        last = functools.reduce(jnp.logical_and, [i == g - 1 for i, g in zip(ids, grid)])
        steps = math.prod(grid)
        relay_at = (3 * steps) // 4 if host.gather and steps >= 8 else None

        @pl.when(first)
        def _():
            host.start(h_in, h_out, sems)

        if relay_at is not None:
            coords, rest = [], relay_at
            for g in reversed(grid):
                coords.append(rest % g)
                rest //= g

            @pl.when(functools.reduce(jnp.logical_and, [i == cd for i, cd in zip(ids, reversed(coords))]))
            def _():
                host.relay(h_in, h_out, sems)

        body(*ins, *outs, *scr)

        @pl.when(last)
        def _():
            host.wait(h_in, h_out, sems, relayed=relay_at is not None)

    hbm = pl.BlockSpec(memory_space=pl.ANY)
    call = pl.pallas_call(
        wrapped, name=name, grid=grid, in_specs=list(in_specs) + [hbm] * k, out_specs=out_specs + [hbm] * k,
        out_shape=out_shape + host.out_shape(), scratch_shapes=list(scratch_shapes) + host.scratch(),
        compiler_params=compiler_params)
    return lambda *args: call(*args, *host.arrs)


def _exchange(arrs, *, gather, name):
    host = _Exchange(arrs, gather)

    def body(*refs):
        ins, outs, sems = refs[:host.n], refs[host.n:2 * host.n], refs[2 * host.n:]
        host.start(ins, outs, sems)
        host.wait(ins, outs, sems)

    hbm = pl.BlockSpec(memory_space=pl.ANY)
    return pl.pallas_call(
        body, name=name, in_specs=[hbm] * host.n, out_specs=[hbm] * host.n, out_shape=host.out_shape(),
        scratch_shapes=host.scratch(), compiler_params=pltpu.CompilerParams(has_side_effects=True),
    )(*arrs)


def _ffn_fwd(xhat, g_in, b_in, wg, wu, wd, *, tm, name, host=None):
    t = xhat.shape[0]
    nj = N_DEV

    def body(x_ref, g_ref, b_ref, wg_ref, wu_ref, wd_ref, xo_ref, rstd_ref, hg_ref, hu_ref, xb, acc):
        j = pl.program_id(1)

        @pl.when(j == 0)
        def _():
            xb[...] = (x_ref[...] * g_ref[...] + b_ref[...]).astype(bf16)
            acc[...] = jnp.zeros_like(acc)

        hg = jnp.dot(xb[...], wg_ref[...], preferred_element_type=f32)
        hu = jnp.dot(xb[...], wu_ref[...], preferred_element_type=f32)
        hg_ref[...] = hg.astype(bf16)
        hu_ref[...] = hu.astype(bf16)
        a = hg * _sigmoid_tanh(hg) * hu
        acc[...] += jnp.dot(a.astype(bf16), wd_ref[...], preferred_element_type=f32)

        @pl.when(j == nj - 1)
        def _():
            x = x_ref[...] * g_ref[...] + b_ref[...]
            xo, rstd = _ln_fwd_tile(ALPHA * x + 0.5 * acc[...])
            xo_ref[...] = xo
            rstd_ref[...] = rstd

    row = pl.BlockSpec((1, D_MODEL), lambda i, j: (0, 0))
    return _hosted_call(
        host, body, name=name, grid=(t // tm, nj),
        in_specs=[pl.BlockSpec((tm, D_MODEL), lambda i, j: (i, 0)), row, row,
                  pl.BlockSpec((None, D_MODEL, FF_TILE), lambda i, j: (j, 0, 0)),
                  pl.BlockSpec((None, D_MODEL, FF_TILE), lambda i, j: (j, 0, 0)),
                  pl.BlockSpec((None, FF_TILE, D_MODEL), lambda i, j: (j, 0, 0))],
        out_specs=[pl.BlockSpec((tm, D_MODEL), lambda i, j: (i, 0)),
                   pl.BlockSpec((tm, 1), lambda i, j: (i, 0)),
                   pl.BlockSpec((tm, FF_TILE), lambda i, j: (i, j)),
                   pl.BlockSpec((tm, FF_TILE), lambda i, j: (i, j))],
        out_shape=[jax.ShapeDtypeStruct((t, D_MODEL), f32), jax.ShapeDtypeStruct((t, 1), f32),
                   jax.ShapeDtypeStruct((t, D_FF), bf16), jax.ShapeDtypeStruct((t, D_FF), bf16)],
        scratch_shapes=[pltpu.VMEM((tm, D_MODEL), bf16), pltpu.VMEM((tm, D_MODEL), f32)],
        compiler_params=_params(("arbitrary", "arbitrary")),
    )(xhat, g_in, b_in, wg, wu, wd)


def _ffn1_fwd_gathering(x, own, extra, *, tm, name):
    t = x.shape[0]
    n_i = t // tm
    n_arr = 3
    k_extra = extra.n
    ex = _Exchange(list(own), gather=True)
    ax, ay, ac = lax.axis_index("x"), lax.axis_index("y"), lax.axis_index("c")
    order = jnp.stack([4 * px + 2 * py + pc for px, py in ((ax, ay), (1 - ax, ay), (ax, 1 - ay), (1 - ax, 1 - ay))
                       for pc in (ac, 1 - ac)]).astype(jnp.int32)
    arrival = [None, (0, None), (1, 0), (4, None), (2, 1), (5, None), (3, 2), (6, None)]

    def body(order_ref, x_ref, *refs):
        w_in, e_in = refs[:n_arr], refs[n_arr:n_arr + k_extra]
        refs = refs[n_arr + k_extra:]
        xo_ref, rstd_ref, hg_ref, hu_ref = refs[:4]
        w_all, e_out = refs[4:4 + n_arr], refs[4 + n_arr:4 + n_arr + k_extra]
        acc, wgb, wub, wdb, fetch_sems, send_sems, recv_sems, local_sems = refs[4 + n_arr + k_extra:12 + n_arr + k_extra]
        e_sems = refs[12 + n_arr + k_extra:]
        bufs = (wgb, wub, wdb)
        s = pl.program_id(0)
        i = pl.program_id(1)
        per_array, sibling, chips, (x_, y_, c_) = ex.gather_copies(w_in, w_all, (send_sems, recv_sems, local_sems))

        def fetch(pos, slot):
            return [pltpu.make_async_copy(w_in[a] if pos == 0 else w_all[a].at[order_ref[pos]],
                                          bufs[a].at[slot], fetch_sems.at[n_arr * slot + a]) for a in range(n_arr)]

        def source_of(pos):
            chip = (x_, y_) if pos < 2 else chips[(pos - 2) // 2]
            return (*chip, c_ if pos % 2 == 0 else 1 - c_)

        @pl.when(jnp.logical_and(s == 0, i == 0))
        def _():
            for first, _, own_copy, _ in per_array:
                own_copy.start()
                for cp in first:
                    cp.start()
            extra.start(e_in, e_out, e_sems)
            for cp in fetch(0, 0):
                cp.start()
            for cp in fetch(0, 0):
                cp.wait()

        for pos in range(1, N_DEV):
            @pl.when(jnp.logical_and(s == pos - 1, i == min(1, n_i - 1)))
            def _(pos=pos):
                sem, passes = arrival[pos]
                for _, passed, _, copy in per_array:
                    copy(sem, source_of(pos), (x_, y_, c_)).wait_recv()
                    if passes is not None:
                        passed[passes].start()
                for cp in fetch(pos, pos % 2):
                    cp.start()

            @pl.when(jnp.logical_and(s == pos, i == 0))
            def _(pos=pos):
                for cp in fetch(pos, pos % 2):
                    cp.wait()

        slot = s % 2
        xb = x_ref[...].astype(bf16)
        hg = jnp.dot(xb, wgb[slot], preferred_element_type=f32)
        hu = jnp.dot(xb, wub[slot], preferred_element_type=f32)
        hg_ref[...] = hg.astype(bf16)
        hu_ref[...] = hu.astype(bf16)
        a = hg * _sigmoid_tanh(hg) * hu
        part = jnp.dot(a.astype(bf16), wdb[slot], preferred_element_type=f32)

        @pl.when(s == 0)
        def _():
            acc[i] = part

        @pl.when(s > 0)
        def _():
            acc[i] += part

        @pl.when(s == N_DEV - 1)
        def _():
            xo, rstd = _ln_fwd_tile(ALPHA * x_ref[...] + 0.5 * acc[i])
            xo_ref[...] = xo
            rstd_ref[...] = rstd

        @pl.when(jnp.logical_and(s == N_DEV - 1, i == n_i - 1))
        def _():
            for first, passed, own_copy, _ in per_array:
                for cp in first + passed:
                    cp.wait_send()
                own_copy.wait()
            extra.wait(e_in, e_out, e_sems)

    hbm = pl.BlockSpec(memory_space=pl.ANY)
    last = N_DEV - 1
    tok_out = pl.BlockSpec((tm, D_MODEL), lambda s, i, o: (jnp.where(s == last, i, 0), 0))
    col_out = pl.BlockSpec((tm, 1), lambda s, i, o: (jnp.where(s == last, i, 0), 0))
    hid = pl.BlockSpec((tm, FF_TILE), lambda s, i, o: (i, o[s]))
    shard_shapes = [(N_DEV,) + w.shape for w in own]
    grid_spec = pltpu.PrefetchScalarGridSpec(
        num_scalar_prefetch=1, grid=(N_DEV, n_i),
        in_specs=[pl.BlockSpec((tm, D_MODEL), lambda s, i, o: (i, 0))] + [hbm] * (n_arr + k_extra),
        out_specs=[tok_out, col_out, hid, hid] + [hbm] * (n_arr + k_extra),
        scratch_shapes=[pltpu.VMEM((n_i, tm, D_MODEL), f32)]
        + [pltpu.VMEM((2,) + w.shape, bf16) for w in own]
        + [pltpu.SemaphoreType.DMA((2 * n_arr,))] + ex.scratch() + extra.scratch())
    res = pl.pallas_call(
        body, name=name, grid_spec=grid_spec,
        out_shape=[jax.ShapeDtypeStruct((t, D_MODEL), f32), jax.ShapeDtypeStruct((t, 1), f32),
                   jax.ShapeDtypeStruct((t, D_FF), bf16), jax.ShapeDtypeStruct((t, D_FF), bf16)]
        + [jax.ShapeDtypeStruct(sh, bf16) for sh in shard_shapes] + extra.out_shape(),
        compiler_params=_params(("arbitrary", "arbitrary")),
    )(order, x, *own, *extra.arrs)
    return res


def _ffn_bwd(dpre, hg, hu, wg, wu, wd, ln_in, *, tm, name, host=None):
    t = dpre.shape[0]
    nj = N_DEV
    with_ln = ln_in is not None

    def body(*refs):
        if with_ln:
            (dp_ref, hg_ref, hu_ref, wg_ref, wu_ref, wd_ref, xh_ref, rs_ref, g_ref,
             dx_ref, gg_ref, gb_ref, dhg_ref, dhu_ref, a_ref, dfb, acc) = refs
        else:
            (dp_ref, hg_ref, hu_ref, wg_ref, wu_ref, wd_ref,
             dx_ref, dhg_ref, dhu_ref, a_ref, dfb, acc) = refs
        i = pl.program_id(0)
        j = pl.program_id(1)

        @pl.when(j == 0)
        def _():
            dfb[...] = (0.5 * dp_ref[...]).astype(bf16)
            acc[...] = jnp.zeros_like(acc)

        da = lax.dot_general(dfb[...], wd_ref[...], _NT, preferred_element_type=f32)
        hgv = hg_ref[...].astype(f32)
        huv = hu_ref[...].astype(f32)
        sg = _sigmoid_tanh(hgv)
        silu = hgv * sg
        a_ref[...] = (silu * huv).astype(bf16)
        dhu = (da * silu).astype(bf16)
        dhg = (da * huv * (sg * (1.0 + hgv * (1.0 - sg)))).astype(bf16)
        dhg_ref[...] = dhg
        dhu_ref[...] = dhu
        acc[...] += (lax.dot_general(dhg, wg_ref[...], _NT, preferred_element_type=f32)
                     + lax.dot_general(dhu, wu_ref[...], _NT, preferred_element_type=f32))

        @pl.when(j == nj - 1)
        def _():
            dx = ALPHA * dp_ref[...] + acc[...]
            if with_ln:
                dprev, gg, gb = _ln_bwd_tile(dx, xh_ref[...], rs_ref[...], g_ref[...])
                dx_ref[...] = dprev

                @pl.when(i == 0)
                def _():
                    gg_ref[...] = gg
                    gb_ref[...] = gb

                @pl.when(i > 0)
                def _():
                    gg_ref[...] += gg
                    gb_ref[...] += gb
            else:
                dx_ref[...] = dx

    tok = pl.BlockSpec((tm, D_MODEL), lambda i, j: (i, 0), pipeline_mode=pl.Buffered(1))
    row = pl.BlockSpec((1, D_MODEL), lambda i, j: (0, 0))
    hid = pl.BlockSpec((tm, FF_TILE), lambda i, j: (i, j))
    in_specs = [tok, hid, hid,
                pl.BlockSpec((None, D_MODEL, FF_TILE), lambda i, j: (j, 0, 0)),
                pl.BlockSpec((None, D_MODEL, FF_TILE), lambda i, j: (j, 0, 0)),
                pl.BlockSpec((None, FF_TILE, D_MODEL), lambda i, j: (j, 0, 0))]
    args = [dpre, hg, hu, wg, wu, wd]
    out_specs = [tok]
    out_shape = [jax.ShapeDtypeStruct((t, D_MODEL), f32)]
    if with_ln:
        in_specs += [tok, pl.BlockSpec((tm, 1), lambda i, j: (i, 0)), row]
        args += list(ln_in)
        out_specs += [row, row]
        out_shape += [jax.ShapeDtypeStruct((1, D_MODEL), f32)] * 2
    out_specs += [hid, hid, hid]
    out_shape += [jax.ShapeDtypeStruct((t, D_FF), bf16)] * 3
    return _hosted_call(
        host, body, name=name, grid=(t // tm, nj), in_specs=in_specs, out_specs=out_specs, out_shape=out_shape,
        scratch_shapes=[pltpu.VMEM((tm, D_MODEL), bf16), pltpu.VMEM((tm, D_MODEL), f32)],
        compiler_params=_params(("arbitrary", "arbitrary")),
    )(*args)


def _ffn_bwd_act(dpre, hg, hu, wd, *, tm, name, host=None):
    t = dpre.shape[0]

    def body(dp_ref, hg_ref, hu_ref, wd_ref, dhg_ref, dhu_ref, a_ref, dfb):
        @pl.when(pl.program_id(1) == 0)
        def _():
            dfb[...] = (0.5 * dp_ref[...]).astype(bf16)

        da = lax.dot_general(dfb[...], wd_ref[...], _NT, preferred_element_type=f32)
        hgv = hg_ref[...].astype(f32)
        huv = hu_ref[...].astype(f32)
        sg = _sigmoid_tanh(hgv)
        silu = hgv * sg
        a_ref[...] = (silu * huv).astype(bf16)
        dhu_ref[...] = (da * silu).astype(bf16)
        dhg_ref[...] = (da * huv * (sg * (1.0 + hgv * (1.0 - sg)))).astype(bf16)

    hid = pl.BlockSpec((tm, FF_TILE), lambda i, j: (i, j))
    return _hosted_call(
        host, body, name=name, grid=(t // tm, N_DEV),
        in_specs=[pl.BlockSpec((tm, D_MODEL), lambda i, j: (i, 0)), hid, hid,
                  pl.BlockSpec((None, FF_TILE, D_MODEL), lambda i, j: (j, 0, 0))],
        out_specs=[hid, hid, hid], out_shape=[jax.ShapeDtypeStruct((t, D_FF), bf16)] * 3,
        scratch_shapes=[pltpu.VMEM((tm, D_MODEL), bf16)],
        compiler_params=_params(("arbitrary", "arbitrary")),
    )(dpre, hg, hu, wd)


def _ffn_bwd_dx(dpre, dhg, dhu, wg, wu, *, tm, name, host=None):
    t = dpre.shape[0]
    nj = N_DEV

    def body(dp_ref, dhg_ref, dhu_ref, wg_ref, wu_ref, dx_ref, acc):
        j = pl.program_id(1)

        @pl.when(j == 0)
        def _():
            acc[...] = jnp.zeros_like(acc)

        acc[...] += (lax.dot_general(dhg_ref[...], wg_ref[...], _NT, preferred_element_type=f32)
                     + lax.dot_general(dhu_ref[...], wu_ref[...], _NT, preferred_element_type=f32))

        @pl.when(j == nj - 1)
        def _():
            dx_ref[...] = ALPHA * dp_ref[...] + acc[...]

    tok = pl.BlockSpec((tm, D_MODEL), lambda i, j: (i, 0))
    hid = pl.BlockSpec((tm, FF_TILE), lambda i, j: (i, j))
    wspec = pl.BlockSpec((None, D_MODEL, FF_TILE), lambda i, j: (j, 0, 0))
    return _hosted_call(
        host, body, name=name, grid=(t // tm, nj), in_specs=[tok, hid, hid, wspec, wspec],
        out_specs=[tok], out_shape=[jax.ShapeDtypeStruct((t, D_MODEL), f32)],
        scratch_shapes=[pltpu.VMEM((tm, D_MODEL), f32)],
        compiler_params=_params(("arbitrary", "arbitrary")),
    )(dpre, dhg, dhu, wg, wu)


def _mm(a, b, *, mode, out_dtype, tm, tn, tk, name, affine=None, a_cols=None, b_cols=None,
        b_blocked=False, out_blocked=False, out_scale=None):
    if mode == "nn":
        m_full, k_full = a.shape
        m_dim, k_dim = (m_full, a_cols[1]) if a_cols else (m_full, k_full)
    else:
        k_dim, m_full = a.shape
        m_dim = a_cols[1] if a_cols else m_full
    a_off = a_cols[0] if a_cols else 0
    if b_blocked:
        n_dim = b.shape[0] * b.shape[2]
        assert b.shape[2] == tn
    else:
        n_dim = b_cols[1] if b_cols else b.shape[1]
    b_off = b_cols[0] if b_cols else 0
    assert m_dim % tm == 0 and n_dim % tn == 0 and k_dim % tk == 0, (name, m_dim, n_dim, k_dim)
    nk = k_dim // tk

    def body(*refs):
        if affine is not None:
            a_ref, g_ref, s_ref, b_ref, o_ref, acc = refs
        else:
            a_ref, b_ref, o_ref, acc = refs
        k = pl.program_id(2)

        @pl.when(k == 0)
        def _():
            acc[...] = jnp.zeros_like(acc)

        av = a_ref[...]
        if affine is not None:
            av = av * g_ref[...] + s_ref[...]
        av = av.astype(bf16)
        bv = b_ref[...].astype(bf16)
        if mode == "nn":
            acc[...] += jnp.dot(av, bv, preferred_element_type=f32)
        else:
            acc[...] += lax.dot_general(av, bv, _TN, preferred_element_type=f32)

        @pl.when(k == nk - 1)
        def _():
            res = acc[...] if out_scale is None else acc[...] * out_scale
            o_ref[...] = res.astype(out_dtype)

    if mode == "nn":
        a_spec = pl.BlockSpec((tm, tk), lambda i, j, k: (i, k + a_off))
        aff_spec = pl.BlockSpec((1, tk), lambda i, j, k: (0, k + a_off))
    else:
        a_spec = pl.BlockSpec((tk, tm), lambda i, j, k: (k, i + a_off))
        aff_spec = pl.BlockSpec((1, tm), lambda i, j, k: (0, i + a_off))
    if b_blocked:
        b_spec = pl.BlockSpec((None, tk, tn), lambda i, j, k: (j, k, 0))
    else:
        b_spec = pl.BlockSpec((tk, tn), lambda i, j, k: (k, j + b_off))
    if out_blocked:
        o_spec = pl.BlockSpec((None, tm, tn), lambda i, j, k: (j, i, 0))
        o_shape = jax.ShapeDtypeStruct((n_dim // tn, m_dim, tn), out_dtype)
    else:
        o_spec = pl.BlockSpec((tm, tn), lambda i, j, k: (i, j))
        o_shape = jax.ShapeDtypeStruct((m_dim, n_dim), out_dtype)
    in_specs = [a_spec] + ([aff_spec, aff_spec] if affine is not None else []) + [b_spec]
    args = [a] + (list(affine) if affine is not None else []) + [b]
    return pl.pallas_call(
        body, name=name, grid=(m_dim // tm, n_dim // tn, nk), in_specs=in_specs, out_specs=o_spec,
        out_shape=o_shape, scratch_shapes=[pltpu.VMEM((tm, tn), f32)],
        compiler_params=_params(("arbitrary", "arbitrary", "arbitrary")),
    )(*args)


def _mm_tn(a, b, *, out_dtype, tm, mb, tn, nb, tk, name, affine=None, out_blocked=False, out_scale=None,
           pair=False, host=None):
    k_dim, m_dim = a.shape
    multi_b = isinstance(b, (list, tuple))
    b_list = list(b) if multi_b else [b]
    n_dim = nb * tn if multi_b else b.shape[1]
    assert m_dim % (mb * tm) == 0 and n_dim % (nb * tn) == 0 and k_dim % tk == 0, (name, m_dim, n_dim, k_dim)
    nk = k_dim // tk
    grid = (m_dim // (mb * tm), n_dim // (nb * tn), nk)
    if pair:
        assert mb * nb == 4 and grid[0] * grid[1] == 2 and out_dtype == bf16, name

    def body(*refs):
        if pair:
            refs, (acc, send_buf, recv_buf, send_sems, recv_sems) = refs[:-5], refs[-5:]
        else:
            refs, acc = refs[:-1], refs[-1]
        a_ref, o_ref = refs[0], refs[-1]
        if affine is not None:
            g_ref, s_ref = refs[1:3]
        b_refs = refs[3 if affine is not None else 1:-1]
        k = pl.program_id(2)

        @pl.when(k == 0)
        def _():
            acc[...] = jnp.zeros_like(acc)

        av = a_ref[...]
        if affine is not None:
            av = av * g_ref[...] + s_ref[...]
        av = av.astype(bf16)
        if multi_b:
            pieces = [r[...].astype(bf16) for r in b_refs]
        else:
            bv = b_refs[0][...].astype(bf16)
            pieces = [bv[:, jn * tn:(jn + 1) * tn] for jn in range(nb)]
        for im in range(mb):
            a_t = av[:, im * tm:(im + 1) * tm].T
            for jn in range(nb):
                acc[im * nb + jn] += jnp.dot(a_t, pieces[jn], preferred_element_type=f32)

        def scaled(v):
            return v if out_scale is None else v * out_scale

        @pl.when(k == nk - 1)
        def _():
            if pair:
                x, y, c = lax.axis_index("x"), lax.axis_index("y"), lax.axis_index("c")
                window = pl.program_id(0) + pl.program_id(1)
                swaps = []
                for cc in range(2):
                    send_buf[cc] = scaled(acc[2 * cc + 1 - c]).astype(bf16)
                    swaps.append(pltpu.make_async_remote_copy(
                        src_ref=send_buf.at[cc], dst_ref=recv_buf.at[window, cc],
                        send_sem=send_sems.at[2 * window + cc], recv_sem=recv_sems.at[2 * window + cc],
                        device_id=(x, y, 1 - c), device_id_type=MESH_T))
                    swaps[cc].start()
                for cc in range(2):
                    swaps[cc].wait_recv()
                    o_ref[cc] = (scaled(acc[2 * cc + c]) + recv_buf[window, cc].astype(f32)).astype(bf16)
                for cc in range(2):
                    swaps[cc].wait_send()
                return
            for im in range(mb):
                for jn in range(nb):
                    res = scaled(acc[im * nb + jn])
                    if out_blocked:
                        o_ref[jn, im * tm:(im + 1) * tm, :] = res.astype(out_dtype)
                    else:
                        o_ref[im * tm:(im + 1) * tm, jn * tn:(jn + 1) * tn] = res.astype(out_dtype)

    a_spec = pl.BlockSpec((tk, mb * tm), lambda i, j, k: (k, i))
    aff_spec = pl.BlockSpec((1, mb * tm), lambda i, j, k: (0, i))
    if multi_b:
        b_specs = [pl.BlockSpec((tk, tn), lambda i, j, k: (k, 0))] * nb
    else:
        b_specs = [pl.BlockSpec((tk, nb * tn), lambda i, j, k: (k, j))]
    scratch = [pltpu.VMEM((mb * nb, tm, tn), f32)]
    if pair:
        o_spec = pl.BlockSpec((2, tm, tn), lambda i, j, k: (i + j, 0, 0))
        o_shape = jax.ShapeDtypeStruct((4, tm, tn), out_dtype)
        scratch += [pltpu.VMEM((2, tm, tn), bf16), pltpu.VMEM((2, 2, tm, tn), bf16),
                    pltpu.SemaphoreType.DMA((4,)), pltpu.SemaphoreType.DMA((4,))]
    elif out_blocked:
        o_spec = pl.BlockSpec((nb, mb * tm, tn), lambda i, j, k: (j, i, 0))
        o_shape = jax.ShapeDtypeStruct((n_dim // tn, m_dim, tn), out_dtype)
    else:
        o_spec = pl.BlockSpec((mb * tm, nb * tn), lambda i, j, k: (i, j))
        o_shape = jax.ShapeDtypeStruct((m_dim, n_dim), out_dtype)
    in_specs = [a_spec] + ([aff_spec, aff_spec] if affine is not None else []) + b_specs
    args = [a] + (list(affine) if affine is not None else []) + b_list
    res = _hosted_call(
        host, body, name=name, grid=grid, in_specs=in_specs, out_specs=o_spec, out_shape=o_shape,
        scratch_shapes=scratch, compiler_params=_params(("arbitrary", "arbitrary", "arbitrary")),
    )(*args)
    return res[0] if host is None else res


def _in_proj(xhat, g, b, w_in, *, tm, name):
    t = xhat.shape[0]
    n_qkv, n_l = 3 * FOX_W, 2 * LRU_W

    def body(x_ref, g_ref, b_ref, w_ref, qkv_ref, zl_ref, zfg_ref):
        xb = (x_ref[...] * g_ref[...] + b_ref[...]).astype(bf16)
        qkv_ref[...] = jnp.dot(xb, w_ref[:, :n_qkv], preferred_element_type=f32).astype(bf16)
        zl_ref[...] = jnp.dot(xb, w_ref[:, n_qkv:n_qkv + n_l], preferred_element_type=f32)
        zfg_ref[...] = jnp.dot(xb, w_ref[:, n_qkv + n_l:], preferred_element_type=f32)

    row = pl.BlockSpec((1, D_MODEL), lambda i: (0, 0))
    return pl.pallas_call(
        body, name=name, grid=(t // tm,),
        in_specs=[pl.BlockSpec((tm, D_MODEL), lambda i: (i, 0)), row, row,
                  pl.BlockSpec(w_in.shape, lambda i: (0, 0))],
        out_specs=[pl.BlockSpec((tm, n_qkv), lambda i: (i, 0)), pl.BlockSpec((tm, n_l), lambda i: (i, 0)),
                   pl.BlockSpec((tm, LANES), lambda i: (i, 0))],
        out_shape=[jax.ShapeDtypeStruct((t, n_qkv), bf16), jax.ShapeDtypeStruct((t, n_l), f32),
                   jax.ShapeDtypeStruct((t, LANES), f32)],
        compiler_params=_params(("arbitrary",)),
    )(xhat, g, b, w_in)


def _mmln(pairs, *, tm, name, resid=None, resid_scale=1.0, epi=None, ln=None, n_out=D_MODEL):
    t = pairs[0][0].shape[0]
    n_pairs = len(pairs)
    n_resid = 0 if resid is None else len(resid) - 1

    def body(*refs):
        pos = 0
        val = None
        for p in range(n_pairs):
            a_ref, b_ref = refs[pos], refs[pos + 1]
            pos += 2
            av = a_ref[...].astype(bf16)
            bv = b_ref[...].astype(bf16)
            if pairs[p][6] == "nn":
                term = jnp.dot(av, bv, preferred_element_type=f32)
            else:
                term = lax.dot_general(av, bv, _NT, preferred_element_type=f32)
            val = term if val is None else val + term
        if resid is not None:
            if resid[0] == "plain":
                r = refs[pos][...]
            else:
                r = refs[pos][...] * refs[pos + 1][...] + refs[pos + 2][...]
            pos += n_resid
            val = val + resid_scale * r
        if epi is None:
            o_ref = refs[pos]
            o_ref[...] = val.astype(o_ref.dtype)
        elif epi == "ln_fwd":
            xo, rstd = _ln_fwd_tile(val)
            refs[pos][...] = xo
            refs[pos + 1][...] = rstd
        else:
            xh_ref, rs_ref, g_ref, dx_ref, gg_ref, gb_ref = refs[pos:pos + 6]
            dprev, gg, gb = _ln_bwd_tile(val, xh_ref[...], rs_ref[...], g_ref[...])
            dx_ref[...] = dprev
            i = pl.program_id(0)

            @pl.when(i == 0)
            def _():
                gg_ref[...] = gg
                gb_ref[...] = gb

            @pl.when(i > 0)
            def _():
                gg_ref[...] += gg
                gb_ref[...] += gb

    in_specs, args = [], []
    for (a, acb, aw, b, bcb, bw, mode) in pairs:
        in_specs.append(pl.BlockSpec((tm, aw), lambda i, acb=acb: (i, acb)))
        args.append(a)
        if mode == "nn":
            in_specs.append(pl.BlockSpec((aw, n_out), lambda i, bcb=bcb: (bcb, 0)))
        else:
            in_specs.append(pl.BlockSpec((n_out, bw), lambda i, bcb=bcb: (0, bcb)))
        args.append(b)
    tok = pl.BlockSpec((tm, n_out), lambda i: (i, 0))
    row = pl.BlockSpec((1, n_out), lambda i: (0, 0))
    col = pl.BlockSpec((tm, 1), lambda i: (i, 0))
    if resid is not None:
        in_specs += [tok] if resid[0] == "plain" else [tok, row, row]
        args += list(resid[1:])
    if epi is None:
        out_specs, out_shape = tok, jax.ShapeDtypeStruct((t, n_out), f32)
    elif epi == "ln_fwd":
        out_specs = [tok, col]
        out_shape = [jax.ShapeDtypeStruct((t, n_out), f32), jax.ShapeDtypeStruct((t, 1), f32)]
    else:
        in_specs += [tok, col, row]
        args += list(ln)
        out_specs = [tok, row, row]
        out_shape = [jax.ShapeDtypeStruct((t, n_out), f32)] + [jax.ShapeDtypeStruct((1, n_out), f32)] * 2
    return pl.pallas_call(
        body, name=name, grid=(t // tm,), in_specs=in_specs, out_specs=out_specs, out_shape=out_shape,
        compiler_params=_params(("arbitrary",)),
    )(*args)


def _loss_bwd(xhat, rstd, g, b, target, *, tm, name):
    t = xhat.shape[0]

    def body(xh_ref, rs_ref, g_ref, b_ref, tg_ref, dx_ref, sq_ref, gg_ref, gb_ref):
        i = pl.program_id(0)
        xh = xh_ref[...]
        diff = xh * g_ref[...] + b_ref[...] - tg_ref[...]
        sq = jnp.sum(diff * diff, axis=0, keepdims=True)
        dprev, gg, gb = _ln_bwd_tile(diff * (1.0 / D_MODEL), xh, rs_ref[...], g_ref[...])
        dx_ref[...] = dprev

        @pl.when(i == 0)
        def _():
            sq_ref[...] = sq
            gg_ref[...] = gg
            gb_ref[...] = gb

        @pl.when(i > 0)
        def _():
            sq_ref[...] += sq
            gg_ref[...] += gg
            gb_ref[...] += gb

    tok = pl.BlockSpec((tm, D_MODEL), lambda i: (i, 0))
    row = pl.BlockSpec((1, D_MODEL), lambda i: (0, 0))
    return pl.pallas_call(
        body, name=name, grid=(t // tm,),
        in_specs=[tok, pl.BlockSpec((tm, 1), lambda i: (i, 0)), row, row, tok],
        out_specs=[tok, row, row, row],
        out_shape=[jax.ShapeDtypeStruct((t, D_MODEL), f32)] + [jax.ShapeDtypeStruct((1, D_MODEL), f32)] * 3,
        compiler_params=_params(("arbitrary",)),
    )(xhat, rstd, g, b, target)


CUM_TILE = 256


def _tri(n, lower):
    r = lax.broadcasted_iota(jnp.int32, (n, n), 0)
    c = lax.broadcasted_iota(jnp.int32, (n, n), 1)
    return jnp.where((r >= c) if lower else (r <= c), 1.0, 0.0).astype(f32)


def _cum_fwd(zfg, bfg, *, name):
    t = zfg.shape[0]

    def body(z_ref, b_ref, o_ref, carry):
        @pl.when(pl.program_id(0) == 0)
        def _():
            carry[...] = jnp.zeros_like(carry)

        ls = -_softplus(-(z_ref[...] + b_ref[...]))
        c = jnp.dot(_tri(CUM_TILE, True), ls, preferred_element_type=f32,
                    precision=lax.Precision.HIGHEST) + carry[...]
        o_ref[...] = c
        carry[...] = c[CUM_TILE - 1:CUM_TILE, :]

    blk = pl.BlockSpec((CUM_TILE, LANES), lambda i: (i, 0))
    return pl.pallas_call(
        body, name=name, grid=(t // CUM_TILE,),
        in_specs=[blk, pl.BlockSpec((1, LANES), lambda i: (0, 0))], out_specs=blk,
        out_shape=jax.ShapeDtypeStruct((t, LANES), f32), scratch_shapes=[pltpu.VMEM((1, LANES), f32)],
        compiler_params=_params(("arbitrary",)),
    )(zfg, bfg)


def _cum_bwd(dcum_q, dcum_k, zfg, bfg, *, name):
    t = zfg.shape[0]
    n = t // CUM_TILE

    def body(d_ref, d2_ref, z_ref, b_ref, o_ref, s_ref, carry):
        i = pl.program_id(0)

        @pl.when(i == 0)
        def _():
            carry[...] = jnp.zeros_like(carry)

        dls = jnp.dot(_tri(CUM_TILE, False), d_ref[...] + d2_ref[...], preferred_element_type=f32,
                      precision=lax.Precision.HIGHEST) + carry[...]
        carry[...] = dls[0:1, :]
        lane = lax.broadcasted_iota(jnp.int32, (CUM_TILE, LANES), 1)
        dfg = jnp.where(lane < HEADS, dls * _sigmoid(-(z_ref[...] + b_ref[...])), 0.0)
        o_ref[...] = dfg
        tot = jnp.sum(dfg, axis=0, keepdims=True)

        @pl.when(i == 0)
        def _():
            s_ref[...] = tot

        @pl.when(i > 0)
        def _():
            s_ref[...] += tot

    blk = pl.BlockSpec((CUM_TILE, LANES), lambda i: (n - 1 - i, 0))
    row = pl.BlockSpec((1, LANES), lambda i: (0, 0))
    return pl.pallas_call(
        body, name=name, grid=(n,), in_specs=[blk, blk, blk, row], out_specs=[blk, row],
        out_shape=[jax.ShapeDtypeStruct((t, LANES), f32), jax.ShapeDtypeStruct((1, LANES), f32)],
        scratch_shapes=[pltpu.VMEM((1, LANES), f32)],
        compiler_params=_params(("arbitrary",)),
    )(dcum_q, dcum_k, zfg, bfg)


ATT_TILE = 512


ATT_ROWS = 32


def _causal_rows(r, transposed):
    rr = lax.broadcasted_iota(jnp.int32, (ATT_ROWS, ATT_TILE), 0) + r * ATT_ROWS
    cc = lax.broadcasted_iota(jnp.int32, (ATT_ROWS, ATT_TILE), 1)
    return (cc >= rr) if transposed else (rr >= cc)


def _causal(i, j, transposed):
    r = lax.broadcasted_iota(jnp.int32, (ATT_TILE, ATT_TILE), 0)
    c = lax.broadcasted_iota(jnp.int32, (ATT_TILE, ATT_TILE), 1)
    if transposed:
        return (c + i * ATT_TILE) >= (r + j * ATT_TILE)
    return (r + i * ATT_TILE) >= (c + j * ATT_TILE)


def _attn_fwd(qkv, cum, cum_t, *, name, host=None):
    t = qkv.shape[0]
    n = t // ATT_TILE
    tq = ATT_TILE

    def body(q_ref, k_ref, v_ref, cq_ref, ck_ref, o_ref, lse_ref, acc, m_s, l_s, c_s, s_s, p_s):
        i = pl.program_id(0)
        j = pl.program_id(1)

        @pl.when(j == 0)
        def _():
            acc[...] = jnp.zeros_like(acc)
            m_s[...] = jnp.full_like(m_s, NEG_BIG)
            l_s[...] = jnp.zeros_like(l_s)

        def block(masked):
            for h in range(HEADS):
                hs = slice(HEAD_D * h, HEAD_D * (h + 1))
                s_s[...] = lax.dot_general(q_ref[:, hs] * ATT_SCALE, k_ref[:, hs], _NT, preferred_element_type=f32)
                ck = ck_ref[h:h + 1, :]

                def rows_chunk(r, carry):
                    rows = pl.ds(pl.multiple_of(r * ATT_ROWS, ATT_ROWS), ATT_ROWS)
                    s = s_s[rows, :] + (cq_ref[rows, h:h + 1] - ck)
                    if masked:
                        s = jnp.where(_causal_rows(r, False), s, NEG_BIG)
                    m_old = m_s[rows, h:h + 1]
                    m_new = jnp.maximum(m_old, jnp.max(s, axis=-1, keepdims=True))
                    corr = jnp.exp(m_old - m_new)
                    p = jnp.exp(s - m_new)
                    l_s[rows, h:h + 1] = corr * l_s[rows, h:h + 1] + jnp.sum(p, axis=-1, keepdims=True)
                    m_s[rows, h:h + 1] = m_new
                    c_s[rows, h:h + 1] = corr
                    p_s[rows, :] = p.astype(bf16)
                    return carry

                lax.fori_loop(0, tq // ATT_ROWS, rows_chunk, 0, unroll=4)
                acc[:, hs] = c_s[:, h:h + 1] * acc[:, hs] + jnp.dot(p_s[...], v_ref[:, hs],
                                                                   preferred_element_type=f32)

        @pl.when(j < i)
        def _():
            block(False)

        @pl.when(j == i)
        def _():
            block(True)
            lse_ref[...] = jnp.zeros_like(lse_ref)
            for h in range(HEADS):
                hs = slice(HEAD_D * h, HEAD_D * (h + 1))
                l = l_s[:, h:h + 1]
                o_ref[:, hs] = acc[:, hs] / l
                lse_ref[:, h:h + 1] = m_s[:, h:h + 1] + jnp.log(l)

    return _hosted_call(
        host, body, name=name, grid=(n, n),
        in_specs=[pl.BlockSpec((tq, FOX_W), lambda i, j: (i, 0)),
                  pl.BlockSpec((tq, FOX_W), lambda i, j: (jnp.minimum(i, j), 1)),
                  pl.BlockSpec((tq, FOX_W), lambda i, j: (jnp.minimum(i, j), 2)),
                  pl.BlockSpec((tq, LANES), lambda i, j: (i, 0)),
                  pl.BlockSpec((HEADS, tq), lambda i, j: (0, jnp.minimum(i, j)))],
        out_specs=[pl.BlockSpec((tq, FOX_W), lambda i, j: (i, 0)), pl.BlockSpec((tq, LANES), lambda i, j: (i, 0))],
        out_shape=[jax.ShapeDtypeStruct((t, FOX_W), f32), jax.ShapeDtypeStruct((t, LANES), f32)],
        scratch_shapes=[pltpu.VMEM((tq, FOX_W), f32), pltpu.VMEM((tq, LANES), f32), pltpu.VMEM((tq, LANES), f32),
                        pltpu.VMEM((tq, LANES), f32), pltpu.VMEM((tq, tq), f32), pltpu.VMEM((tq, tq), bf16)],
        compiler_params=_params(("arbitrary", "arbitrary")),
    )(qkv, qkv, qkv, cum, cum_t)


def _attn_delta(dmix, o, *, tm, name):
    t = o.shape[0]

    def body(do_ref, o_ref, d_ref):
        r = lax.broadcasted_iota(jnp.int32, (FOX_W, LANES), 0)
        c = lax.broadcasted_iota(jnp.int32, (FOX_W, LANES), 1)
        pick = jnp.where(r // HEAD_D == c, 1.0, 0.0).astype(f32)
        d_ref[...] = jnp.dot(do_ref[...] * o_ref[...], pick, preferred_element_type=f32,
                             precision=lax.Precision.HIGHEST)

    blk = pl.BlockSpec((tm, FOX_W), lambda i: (i, 0))
    return pl.pallas_call(
        body, name=name, grid=(t // tm,), in_specs=[blk, blk],
        out_specs=pl.BlockSpec((tm, LANES), lambda i: (i, 0)),
        out_shape=jax.ShapeDtypeStruct((t, LANES), f32), compiler_params=_params(("arbitrary",)),
    )(dmix, o)


def _attn_dq(qkv, dmix, cum, cum_t, lse, delta, *, name, host=None):
    t = qkv.shape[0]
    n = t // ATT_TILE
    tq = ATT_TILE

    def body(q_ref, k_ref, v_ref, do_ref, cq_ref, ck_ref, lse_ref, dl_ref, dq_ref, dc_ref, acc, dc_acc):
        i = pl.program_id(0)
        j = pl.program_id(1)

        @pl.when(j == 0)
        def _():
            acc[...] = jnp.zeros_like(acc)
            dc_acc[...] = jnp.zeros_like(dc_acc)

        def block(masked):
            mask = _causal(i, j, False) if masked else None
            for h in range(HEADS):
                hs = slice(HEAD_D * h, HEAD_D * (h + 1))
                kh = k_ref[:, hs]
                s = lax.dot_general(q_ref[:, hs] * ATT_SCALE, kh, _NT, preferred_element_type=f32)
                s = s + cq_ref[:, h:h + 1] - ck_ref[h:h + 1, :]
                if masked:
                    s = jnp.where(mask, s, NEG_BIG)
                p = jnp.exp(s - lse_ref[:, h:h + 1])
                dp = lax.dot_general(do_ref[:, hs].astype(bf16), v_ref[:, hs], _NT, preferred_element_type=f32)
                ds = p * (dp - dl_ref[:, h:h + 1])
                acc[:, hs] += jnp.dot(ds.astype(bf16), kh, preferred_element_type=f32)
                dc_acc[:, h:h + 1] += jnp.sum(ds, axis=-1, keepdims=True)

        @pl.when(j < i)
        def _():
            block(False)

        @pl.when(j == i)
        def _():
            block(True)
            dq_ref[...] = (acc[...] * ATT_SCALE).astype(bf16)
            dc_ref[...] = dc_acc[...]

    col = pl.BlockSpec((tq, LANES), lambda i, j: (i, 0))
    return _hosted_call(
        host, body, name=name, grid=(n, n),
        in_specs=[pl.BlockSpec((tq, FOX_W), lambda i, j: (i, 0)),
                  pl.BlockSpec((tq, FOX_W), lambda i, j: (jnp.minimum(i, j), 1)),
                  pl.BlockSpec((tq, FOX_W), lambda i, j: (jnp.minimum(i, j), 2)),
                  pl.BlockSpec((tq, FOX_W), lambda i, j: (i, 0)),
                  col, pl.BlockSpec((HEADS, tq), lambda i, j: (0, jnp.minimum(i, j))), col, col],
        out_specs=[pl.BlockSpec((tq, FOX_W), lambda i, j: (i, 0)), col],
        out_shape=[jax.ShapeDtypeStruct((t, FOX_W), bf16), jax.ShapeDtypeStruct((t, LANES), f32)],
        scratch_shapes=[pltpu.VMEM((tq, FOX_W), f32), pltpu.VMEM((tq, LANES), f32)],
        compiler_params=_params(("arbitrary", "arbitrary")),
    )(qkv, qkv, qkv, dmix, cum, cum_t, lse, delta)


def _attn_dkv(qkv, dmix, cum, cum_t, lse_t, delta_t, *, name):
    t = qkv.shape[0]
    n = t // ATT_TILE
    tk = ATT_TILE

    def body(q_ref, k_ref, v_ref, do_ref, cq_ref, ck_ref, lse_ref, dl_ref, dk_ref, dv_ref, dc_ref, dk_acc, dv_acc, dc_acc):
        j = pl.program_id(0)
        i = pl.program_id(1)

        @pl.when(i == 0)
        def _():
            dk_acc[...] = jnp.zeros_like(dk_acc)
            dv_acc[...] = jnp.zeros_like(dv_acc)
            dc_acc[...] = jnp.zeros_like(dc_acc)

        def block(masked):
            mask = _causal(i, j, True) if masked else None
            for h in range(HEADS):
                hs = slice(HEAD_D * h, HEAD_D * (h + 1))
                qh = q_ref[:, hs]
                doh = do_ref[:, hs].astype(bf16)
                s_t = lax.dot_general(k_ref[:, hs] * ATT_SCALE, qh, _NT, preferred_element_type=f32)
                s_t = s_t + cq_ref[h:h + 1, :] - ck_ref[:, h:h + 1]
                if masked:
                    s_t = jnp.where(mask, s_t, NEG_BIG)
                p_t = jnp.exp(s_t - lse_ref[h:h + 1, :])
                dv_acc[:, hs] += jnp.dot(p_t.astype(bf16), doh, preferred_element_type=f32)
                dp_t = lax.dot_general(v_ref[:, hs], doh, _NT, preferred_element_type=f32)
                ds_t = p_t * (dp_t - dl_ref[h:h + 1, :])
                dk_acc[:, hs] += jnp.dot(ds_t.astype(bf16), qh, preferred_element_type=f32)
                dc_acc[:, h:h + 1] -= jnp.sum(ds_t, axis=-1, keepdims=True)

        @pl.when(i > j)
        def _():
            block(False)

        @pl.when(i == j)
        def _():
            block(True)

        @pl.when(i == n - 1)
        def _():
            dk_ref[...] = (dk_acc[...] * ATT_SCALE).astype(bf16)
            dv_ref[...] = dv_acc[...].astype(bf16)
            dc_ref[...] = dc_acc[...]

    rowq = pl.BlockSpec((HEADS, tk), lambda j, i: (0, jnp.maximum(i, j)))
    return pl.pallas_call(
        body, name=name, grid=(n, n),
        in_specs=[pl.BlockSpec((tk, FOX_W), lambda j, i: (jnp.maximum(i, j), 0)),
                  pl.BlockSpec((tk, FOX_W), lambda j, i: (j, 1)),
                  pl.BlockSpec((tk, FOX_W), lambda j, i: (j, 2)),
                  pl.BlockSpec((tk, FOX_W), lambda j, i: (jnp.maximum(i, j), 0)),
                  rowq, pl.BlockSpec((tk, LANES), lambda j, i: (j, 0)), rowq, rowq],
        out_specs=[pl.BlockSpec((tk, FOX_W), lambda j, i: (j, 0)), pl.BlockSpec((tk, FOX_W), lambda j, i: (j, 0)),
                   pl.BlockSpec((tk, LANES), lambda j, i: (j, 0))],
        out_shape=[jax.ShapeDtypeStruct((t, FOX_W), bf16), jax.ShapeDtypeStruct((t, FOX_W), bf16),
                   jax.ShapeDtypeStruct((t, LANES), f32)],
        scratch_shapes=[pltpu.VMEM((tk, FOX_W), f32), pltpu.VMEM((tk, FOX_W), f32), pltpu.VMEM((tk, LANES), f32)],
        compiler_params=_params(("arbitrary", "arbitrary")),
    )(qkv, qkv, qkv, dmix, cum_t, cum, lse_t, delta_t)


ATT_W = HEADS * LANES


def _data_lane(h):
    return HEAD_D * (h % 2)


def _extra_lane(h):
    return HEAD_D - _data_lane(h)


def _split3(x):
    hi = x.astype(bf16)
    rest = x - hi.astype(f32)
    mid = rest.astype(bf16)
    lo = (rest - mid.astype(f32)).astype(bf16)
    return hi, mid, lo


def _augment(pair, h, first, second, fill=0.0):
    rows = pair.shape[0]
    lane = lax.broadcasted_iota(jnp.int32, (rows, LANES), 1)
    base = _extra_lane(h)
    own = (lane < HEAD_D) if h % 2 == 0 else (lane >= HEAD_D)
    out = jnp.where(own, pair, jnp.full((rows, LANES), fill, bf16))
    for off, src in ((0, first), (3, second)):
        for q in range(3):
            val = src[q] if isinstance(src, tuple) else jnp.full((rows, 1), src, bf16)
            out = jnp.where(lane == base + off + q, val, out)
    return out


def _attn_prep_fwd(qkv, cum, *, tm, name):
    t = qkv.shape[0]

    def body(q_ref, k_ref, v_ref, c_ref, qa_ref, ka_ref, va_ref):
        for h in range(HEADS):
            pair = slice(LANES * (h // 2), LANES * (h // 2 + 1))
            hs = slice(LANES * h, LANES * (h + 1))
            c3 = _split3(c_ref[:, h:h + 1])
            qa_ref[:, hs] = _augment(q_ref[:, pair] * ATT_SCALE, h, c3, 1.0)
            ka_ref[:, hs] = _augment(k_ref[:, pair], h, 1.0, tuple(-p for p in c3))
            va_ref[:, hs] = _augment(v_ref[:, pair], h, 1.0, 1.0, fill=1.0)

    wide = pl.BlockSpec((tm, ATT_W), lambda i: (i, 0))
    out = jax.ShapeDtypeStruct((t, ATT_W), bf16)
    return pl.pallas_call(
        body, name=name, grid=(t // tm,),
        in_specs=[pl.BlockSpec((tm, FOX_W), lambda i: (i, 0)), pl.BlockSpec((tm, FOX_W), lambda i: (i, 1)),
                  pl.BlockSpec((tm, FOX_W), lambda i: (i, 2)), pl.BlockSpec((tm, LANES), lambda i: (i, 0))],
        out_specs=[wide] * 3, out_shape=[out] * 3, compiler_params=_params(("arbitrary",)),
    )(qkv, qkv, qkv, cum)


def _attn_prep_bwd(qkv, cum, lse, dmix, o, *, tm, name):
    t = qkv.shape[0]

    def body(q_ref, c_ref, l_ref, do_ref, o_ref, qa_ref, da_ref):
        for h in range(HEADS):
            pair = slice(LANES * (h // 2), LANES * (h // 2 + 1))
            src = slice(HEAD_D * h, HEAD_D * (h + 1))
            hs = slice(LANES * h, LANES * (h + 1))
            delta = jnp.sum(do_ref[:, src] * o_ref[:, src], axis=-1, keepdims=True)
            qa_ref[:, hs] = _augment(q_ref[:, pair] * ATT_SCALE, h,
                                     _split3(c_ref[:, h:h + 1] - l_ref[:, h:h + 1]), 1.0)
            da_ref[:, hs] = _augment(do_ref[:, pair].astype(bf16), h, tuple(-p for p in _split3(delta)), 0.0)

    wide = pl.BlockSpec((tm, ATT_W), lambda i: (i, 0))
    half = pl.BlockSpec((tm, FOX_W), lambda i: (i, 0))
    col = pl.BlockSpec((tm, LANES), lambda i: (i, 0))
    out = jax.ShapeDtypeStruct((t, ATT_W), bf16)
    return pl.pallas_call(
        body, name=name, grid=(t // tm,), in_specs=[half, col, col, half, half],
        out_specs=[wide] * 2, out_shape=[out] * 2, compiler_params=_params(("arbitrary",)),
    )(qkv, cum, lse, dmix, o)


def _attn_fwd2(q_aug, k_aug, v_aug, *, name, host=None):
    t = q_aug.shape[0]
    n = t // ATT_TILE
    tq = ATT_TILE

    def body(q_ref, k_ref, v_ref, o_ref, lse_ref, acc, m_s):
        i = pl.program_id(0)
        j = pl.program_id(1)

        @pl.when(j == 0)
        def _():
            acc[...] = jnp.zeros_like(acc)
            m_s[...] = jnp.full_like(m_s, NEG_BIG)

        def block(masked):
            mask = _causal(i, j, False) if masked else None
            for h in range(HEADS):
                hs = slice(LANES * h, LANES * (h + 1))
                s = lax.dot_general(q_ref[:, hs], k_ref[:, hs], _NT, preferred_element_type=f32)
                if masked:
                    s = jnp.where(mask, s, NEG_BIG)
                blocks = [s[:, LANES * b:LANES * (b + 1)] for b in range(tq // LANES)]
                m_old = m_s[h]
                m_new = jnp.maximum(m_old, jnp.broadcast_to(
                    jnp.max(functools.reduce(jnp.maximum, blocks), axis=-1, keepdims=True), (tq, LANES)))
                p = jnp.concatenate([jnp.exp(b - m_new) for b in blocks], axis=1).astype(bf16)
                acc[h] = jnp.exp(m_old - m_new) * acc[h] + jnp.dot(p, v_ref[:, hs], preferred_element_type=f32)
                m_s[h] = m_new

        @pl.when(j < i)
        def _():
            block(False)

        @pl.when(j == i)
        def _():
            block(True)
            lse_ref[...] = jnp.zeros_like(lse_ref)
            for h in range(HEADS):
                a = acc[h]
                l = a[:, _extra_lane(h):_extra_lane(h) + 1]
                o_ref[:, HEAD_D * h:HEAD_D * (h + 1)] = a[:, _data_lane(h):_data_lane(h) + HEAD_D] / l
                lse_ref[:, h:h + 1] = m_s[h][:, 0:1] + jnp.log(l)

    kv = pl.BlockSpec((tq, ATT_W), lambda i, j: (jnp.minimum(i, j), 0))
    return _hosted_call(
        host, body, name=name, grid=(n, n),
        in_specs=[pl.BlockSpec((tq, ATT_W), lambda i, j: (i, 0)), kv, kv],
        out_specs=[pl.BlockSpec((tq, FOX_W), lambda i, j: (i, 0)), pl.BlockSpec((tq, LANES), lambda i, j: (i, 0))],
        out_shape=[jax.ShapeDtypeStruct((t, FOX_W), f32), jax.ShapeDtypeStruct((t, LANES), f32)],
        scratch_shapes=[pltpu.VMEM((HEADS, tq, LANES), f32), pltpu.VMEM((HEADS, tq, LANES), f32)],
        compiler_params=_params(("arbitrary", "arbitrary")),
    )(q_aug, k_aug, v_aug)


def _attn_bwd(qb_aug, k_aug, v_aug, do_aug, *, name, host=None):
    t = qb_aug.shape[0]
    n = t // ATT_TILE
    tk = ATT_TILE

    def body(q_ref, k_ref, v_ref, do_ref, dq_ref, dcq_ref, dk_ref, dv_ref, dck_ref, dk_acc, dv_acc, dq_all):
        j = pl.program_id(0)
        i = pl.program_id(1)

        @pl.when(jnp.logical_and(i == 0, j == 0))
        def _():
            dq_all[...] = jnp.zeros_like(dq_all)

        @pl.when(i == 0)
        def _():
            dk_acc[...] = jnp.zeros_like(dk_acc)
            dv_acc[...] = jnp.zeros_like(dv_acc)

        def block(masked):
            mask = _causal(i, j, True) if masked else None
            for h in range(HEADS):
                hs = slice(LANES * h, LANES * (h + 1))
                qh = q_ref[:, hs]
                doh = do_ref[:, hs]
                kh = k_ref[:, hs]
                s_t = lax.dot_general(kh, qh, _NT, preferred_element_type=f32)
                if masked:
                    s_t = jnp.where(mask, s_t, NEG_BIG)
                p_t = jnp.exp(s_t)
                dv_acc[h] += jnp.dot(p_t.astype(bf16), doh, preferred_element_type=f32)
                dp_t = lax.dot_general(v_ref[:, hs], doh, _NT, preferred_element_type=f32)
                ds_t = (p_t * dp_t).astype(bf16)
                dk_acc[h] += jnp.dot(ds_t, qh, preferred_element_type=f32)
                dq_all[i, h] += lax.dot_general(ds_t, kh, _TN, preferred_element_type=f32)

        @pl.when(i > j)
        def _():
            block(False)

        @pl.when(i == j)
        def _():
            block(True)
            dcq_ref[...] = jnp.zeros_like(dcq_ref)
            for h in range(HEADS):
                a = dq_all[j, h]
                dq_ref[:, HEAD_D * h:HEAD_D * (h + 1)] = (
                    a[:, _data_lane(h):_data_lane(h) + HEAD_D] * ATT_SCALE).astype(bf16)
                dcq_ref[:, h:h + 1] = a[:, _extra_lane(h):_extra_lane(h) + 1]

        @pl.when(i == n - 1)
        def _():
            dck_ref[...] = jnp.zeros_like(dck_ref)
            for h in range(HEADS):
                a = dk_acc[h]
                cols = slice(_data_lane(h), _data_lane(h) + HEAD_D)
                dk_ref[:, HEAD_D * h:HEAD_D * (h + 1)] = a[:, cols].astype(bf16)
                dv_ref[:, HEAD_D * h:HEAD_D * (h + 1)] = dv_acc[h][:, cols].astype(bf16)
                dck_ref[:, h:h + 1] = -a[:, _extra_lane(h) + 3:_extra_lane(h) + 4]

    own = pl.BlockSpec((tk, ATT_W), lambda j, i: (j, 0))
    qs = pl.BlockSpec((tk, ATT_W), lambda j, i: (jnp.maximum(i, j), 0))
    half = pl.BlockSpec((tk, FOX_W), lambda j, i: (j, 0))
    col = pl.BlockSpec((tk, LANES), lambda j, i: (j, 0))
    return _hosted_call(
        host, body, name=name, grid=(n, n), in_specs=[qs, own, own, qs],
        out_specs=[half, col, half, half, col],
        out_shape=[jax.ShapeDtypeStruct((t, FOX_W), bf16), jax.ShapeDtypeStruct((t, LANES), f32),
                   jax.ShapeDtypeStruct((t, FOX_W), bf16), jax.ShapeDtypeStruct((t, FOX_W), bf16),
                   jax.ShapeDtypeStruct((t, LANES), f32)],
        scratch_shapes=[pltpu.VMEM((HEADS, tk, LANES), f32), pltpu.VMEM((HEADS, tk, LANES), f32),
                        pltpu.VMEM((n, HEADS, tk, LANES), f32)],
        compiler_params=_params(("arbitrary", "arbitrary")),
    )(qb_aug, k_aug, v_aug, do_aug)


def _attn_dq2(qb_aug, k_aug, v_aug, do_aug, *, name, host=None):
    t = qb_aug.shape[0]
    n = t // ATT_TILE
    tq = ATT_TILE

    def body(q_ref, k_ref, v_ref, do_ref, dq_ref, dc_ref, acc):
        i = pl.program_id(0)
        j = pl.program_id(1)

        @pl.when(j == 0)
        def _():
            acc[...] = jnp.zeros_like(acc)

        def block(masked):
            mask = _causal(i, j, False) if masked else None
            for h in range(HEADS):
                hs = slice(LANES * h, LANES * (h + 1))
                kh = k_ref[:, hs]
                s = lax.dot_general(q_ref[:, hs], kh, _NT, preferred_element_type=f32)
                if masked:
                    s = jnp.where(mask, s, NEG_BIG)
                dp = lax.dot_general(do_ref[:, hs], v_ref[:, hs], _NT, preferred_element_type=f32)
                ds = (jnp.exp(s) * dp).astype(bf16)
                acc[h] += jnp.dot(ds, kh, preferred_element_type=f32)

        @pl.when(j < i)
        def _():
            block(False)

        @pl.when(j == i)
        def _():
            block(True)
            dc_ref[...] = jnp.zeros_like(dc_ref)
            for h in range(HEADS):
                a = acc[h]
                dq_ref[:, HEAD_D * h:HEAD_D * (h + 1)] = (
                    a[:, _data_lane(h):_data_lane(h) + HEAD_D] * ATT_SCALE).astype(bf16)
                dc_ref[:, h:h + 1] = a[:, _extra_lane(h):_extra_lane(h) + 1]

    own = pl.BlockSpec((tq, ATT_W), lambda i, j: (i, 0))
    kv = pl.BlockSpec((tq, ATT_W), lambda i, j: (jnp.minimum(i, j), 0))
    return _hosted_call(
        host, body, name=name, grid=(n, n), in_specs=[own, kv, kv, own],
        out_specs=[pl.BlockSpec((tq, FOX_W), lambda i, j: (i, 0)), pl.BlockSpec((tq, LANES), lambda i, j: (i, 0))],
        out_shape=[jax.ShapeDtypeStruct((t, FOX_W), bf16), jax.ShapeDtypeStruct((t, LANES), f32)],
        scratch_shapes=[pltpu.VMEM((HEADS, tq, LANES), f32)],
        compiler_params=_params(("arbitrary", "arbitrary")),
    )(qb_aug, k_aug, v_aug, do_aug)


def _attn_dkv2(qb_aug, k_aug, v_aug, do_aug, *, name, host=None):
    t = qb_aug.shape[0]
    n = t // ATT_TILE
    tk = ATT_TILE

    def body(q_ref, k_ref, v_ref, do_ref, dk_ref, dv_ref, dc_ref, dk_acc, dv_acc):
        j = pl.program_id(0)
        i = pl.program_id(1)

        @pl.when(i == 0)
        def _():
            dk_acc[...] = jnp.zeros_like(dk_acc)
            dv_acc[...] = jnp.zeros_like(dv_acc)

        def block(masked):
            mask = _causal(i, j, True) if masked else None
            for h in range(HEADS):
                hs = slice(LANES * h, LANES * (h + 1))
                qh = q_ref[:, hs]
                doh = do_ref[:, hs]
                s_t = lax.dot_general(k_ref[:, hs], qh, _NT, preferred_element_type=f32)
                if masked:
                    s_t = jnp.where(mask, s_t, NEG_BIG)
                p_t = jnp.exp(s_t)
                dv_acc[h] += jnp.dot(p_t.astype(bf16), doh, preferred_element_type=f32)
                dp_t = lax.dot_general(v_ref[:, hs], doh, _NT, preferred_element_type=f32)
                dk_acc[h] += jnp.dot((p_t * dp_t).astype(bf16), qh, preferred_element_type=f32)

        @pl.when(i > j)
        def _():
            block(False)

        @pl.when(i == j)
        def _():
            block(True)

        @pl.when(i == n - 1)
        def _():
            dc_ref[...] = jnp.zeros_like(dc_ref)
            for h in range(HEADS):
                a = dk_acc[h]
                cols = slice(_data_lane(h), _data_lane(h) + HEAD_D)
                dk_ref[:, HEAD_D * h:HEAD_D * (h + 1)] = a[:, cols].astype(bf16)
                dv_ref[:, HEAD_D * h:HEAD_D * (h + 1)] = dv_acc[h][:, cols].astype(bf16)
                dc_ref[:, h:h + 1] = -a[:, _extra_lane(h) + 3:_extra_lane(h) + 4]

    own = pl.BlockSpec((tk, ATT_W), lambda j, i: (j, 0))
    qs = pl.BlockSpec((tk, ATT_W), lambda j, i: (jnp.maximum(i, j), 0))
    half = pl.BlockSpec((tk, FOX_W), lambda j, i: (j, 0))
    return _hosted_call(
        host, body, name=name, grid=(n, n), in_specs=[qs, own, own, qs],
        out_specs=[half, half, pl.BlockSpec((tk, LANES), lambda j, i: (j, 0))],
        out_shape=[jax.ShapeDtypeStruct((t, FOX_W), bf16), jax.ShapeDtypeStruct((t, FOX_W), bf16),
                   jax.ShapeDtypeStruct((t, LANES), f32)],
        scratch_shapes=[pltpu.VMEM((HEADS, tk, LANES), f32), pltpu.VMEM((HEADS, tk, LANES), f32)],
        compiler_params=_params(("arbitrary", "arbitrary")),
    )(qb_aug, k_aug, v_aug, do_aug)


LRU_CHUNK = 64
LRU_G = 256
SUB = 8


def _row_ids(n):
    return lax.broadcasted_iota(jnp.int32, (n, LRU_G), 0)


def _shift_rows_down(ext, s):
    return pltpu.roll(ext, s, axis=0)[SUB:, :]


def _shift_rows_up(ext, s, n):
    return pltpu.roll(ext, ext.shape[0] - s, axis=0)[:n, :]


def _lru_gates(u, wa_ref, ba_ref, wx_ref, bx_ref, sp):
    ub = u.astype(bf16)
    r = _sigmoid(jnp.dot(ub, wa_ref[...], preferred_element_type=f32) + ba_ref[...])
    gi = _sigmoid(jnp.dot(ub, wx_ref[...], preferred_element_type=f32) + bx_ref[...])
    log_a = -LRU_C * r * sp
    a = jnp.exp(log_a)
    s = jnp.sqrt(_one_minus_exp(2.0 * log_a))
    return r, gi, a, s


def _conv_window(lx_ref, r0, ci):
    cur = lx_ref[pl.ds(r0, LRU_CHUNK), :]
    p0 = pl.multiple_of(jnp.maximum(r0 - SUB, 0), SUB)
    prev = jnp.where(ci > 0, lx_ref[pl.ds(p0, SUB), :], 0.0)
    return cur, jnp.concatenate([prev, cur], axis=0)


def _lru_fwd(zl, conv_w, conv_b, wa, ba, wx, bx, lam, *, name, host=None):
    t = zl.shape[0]
    n_chunk = t // LRU_CHUNK

    def body(lx_ref, lg_ref, cw_ref, cb_ref, wa_ref, ba_ref, wx_ref, bx_ref, lam_ref, u_ref, h_ref, y_ref):
        sp = _softplus(-lam_ref[...])
        rows = _row_ids(SUB)

        def chunk(ci, hc):
            r0 = pl.multiple_of(ci * LRU_CHUNK, LRU_CHUNK)
            cur, ext = _conv_window(lx_ref, r0, ci)
            u = cb_ref[...] + cw_ref[3:4, :] * cur
            for k in range(3):
                u = u + cw_ref[k:k + 1, :] * _shift_rows_down(ext, 3 - k)
            r, gi, a, s = _lru_gates(u, wa_ref, ba_ref, wx_ref, bx_ref, sp)
            b = s * (gi * u)
            tiles = []
            for q in range(LRU_CHUNK // SUB):
                ta = a[SUB * q:SUB * (q + 1), :]
                tb = b[SUB * q:SUB * (q + 1), :]
                for d in (1, 2, 4):
                    a_sh = jnp.where(rows >= d, pltpu.roll(ta, d, axis=0), 1.0)
                    b_sh = jnp.where(rows >= d, pltpu.roll(tb, d, axis=0), 0.0)
                    tb = ta * b_sh + tb
                    ta = ta * a_sh
                hq = tb + ta * hc
                hc = hq[SUB - 1:SUB, :]
                tiles.append(hq)
            h = jnp.concatenate(tiles, axis=0)
            u_ref[pl.ds(r0, LRU_CHUNK), :] = u
            h_ref[pl.ds(r0, LRU_CHUNK), :] = h
            gel, _ = _gelu_and_grad(lg_ref[pl.ds(r0, LRU_CHUNK), :])
            y_ref[pl.ds(r0, LRU_CHUNK), :] = gel * h
            return hc

        lax.fori_loop(0, n_chunk, chunk, jnp.zeros((1, LRU_G), f32))

    seq = lambda cb: pl.BlockSpec((t, LRU_G), lambda c, cb=cb: (0, c + cb))
    rowc = pl.BlockSpec((1, LRU_G), lambda c: (0, c))
    diag = pl.BlockSpec((LRU_G, LRU_G), lambda c: (c, c))
    out = jax.ShapeDtypeStruct((t, LRU_W), f32)
    return _hosted_call(
        host, body, name=name, grid=(LRU_W // LRU_G,),
        in_specs=[seq(0), seq(LRU_W // LRU_G), pl.BlockSpec((4, LRU_G), lambda c: (0, c)),
                  rowc, diag, rowc, diag, rowc, rowc],
        out_specs=[seq(0)] * 3, out_shape=[out] * 3,
        compiler_params=_params(("arbitrary",)),
    )(zl, zl, conv_w, conv_b, wa, ba, wx, bx, lam)


def _lru_bwd(dmix, zl, u_all, h_all, conv_w, wa, ba, wx, bx, lam, *, name, host=None):
    t = zl.shape[0]
    n_chunk = t // LRU_CHUNK

    def body(dy_ref, lx_ref, lg_ref, u_ref, h_ref, cw_ref, wa_ref, ba_ref, wx_ref, bx_ref, lam_ref,
             dlx_ref, dlg_ref, dcw_ref, dcb_ref, dba_ref, dbx_ref, dlam_ref, dwa_ref, dwx_ref, dpr_s, dpx_s):
        lam_v = lam_ref[...]
        sp = _softplus(-lam_v)
        rows = _row_ids(SUB)
        rows_c = _row_ids(LRU_CHUNK)
        zero_row = jnp.zeros((1, LRU_G), f32)

        def chunk(step, carry):
            dh_c, a_next0, du_next, dsp, dba, dbx, dcb, dw0, dw1, dw2, dw3 = carry
            ci = n_chunk - 1 - step
            r0 = pl.multiple_of(ci * LRU_CHUNK, LRU_CHUNK)
            sl = pl.ds(r0, LRU_CHUNK)
            u = u_ref[sl, :]
            r, gi, a, s = _lru_gates(u, wa_ref, ba_ref, wx_ref, bx_ref, sp)
            h = h_ref[sl, :]
            p0 = pl.multiple_of(jnp.maximum(r0 - SUB, 0), SUB)
            h_before = jnp.where(ci > 0, h_ref[pl.ds(p0, SUB), :], 0.0)[SUB - 1:SUB, :]
            h_prev = jnp.where(rows_c == 0, h_before, pltpu.roll(h, 1, axis=0))
            gel, dgel = _gelu_and_grad(lg_ref[sl, :])
            dy = dy_ref[sl, :]
            dlg_ref[sl, :] = (dy * h * dgel).astype(bf16)
            g_in = dy * gel
            a_next = jnp.where(rows_c == LRU_CHUNK - 1, a_next0, pltpu.roll(a, LRU_CHUNK - 1, axis=0))
            tiles = [None] * (LRU_CHUNK // SUB)
            for q in reversed(range(LRU_CHUNK // SUB)):
                ta = a_next[SUB * q:SUB * (q + 1), :]
                tb = g_in[SUB * q:SUB * (q + 1), :]
                for d in (1, 2, 4):
                    a_sh = jnp.where(rows < SUB - d, pltpu.roll(ta, SUB - d, axis=0), 1.0)
                    b_sh = jnp.where(rows < SUB - d, pltpu.roll(tb, SUB - d, axis=0), 0.0)
                    tb = ta * b_sh + tb
                    ta = ta * a_sh
                dhq = tb + ta * dh_c
                dh_c = dhq[0:1, :]
                tiles[q] = dhq
            dh = jnp.concatenate(tiles, axis=0)
            da = dh * h_prev
            ds = dh * gi * u
            dgi = dh * s * u
            du = dh * s * gi
            dlog_a = da * a - ds * (a * a) / s
            dr = dlog_a * (-LRU_C * sp)
            dsp = dsp + jnp.sum(dlog_a * (-LRU_C * r), axis=0, keepdims=True)
            dpr = dr * r * (1.0 - r)
            dpx = dgi * gi * (1.0 - gi)
            dprb = dpr.astype(bf16)
            dpxb = dpx.astype(bf16)
            dpr_s[sl, :] = dprb
            dpx_s[sl, :] = dpxb
            du = du + (lax.dot_general(dprb, wa_ref[...], _NT, preferred_element_type=f32)
                       + lax.dot_general(dpxb, wx_ref[...], _NT, preferred_element_type=f32))
            dba = dba + jnp.sum(dpr, axis=0, keepdims=True)
            dbx = dbx + jnp.sum(dpx, axis=0, keepdims=True)
            dcb = dcb + jnp.sum(du, axis=0, keepdims=True)
            du_ext = jnp.concatenate([du, du_next], axis=0)
            dlx = cw_ref[3:4, :] * du
            for k in range(3):
                dlx = dlx + cw_ref[k:k + 1, :] * _shift_rows_up(du_ext, 3 - k, LRU_CHUNK)
            dlx_ref[sl, :] = dlx.astype(bf16)
            cur, ext = _conv_window(lx_ref, r0, ci)
            dws = [dw0, dw1, dw2, dw3 + jnp.sum(du * cur, axis=0, keepdims=True)]
            for k in range(3):
                dws[k] = dws[k] + jnp.sum(du * _shift_rows_down(ext, 3 - k), axis=0, keepdims=True)
            return (dh_c, a[0:1, :], du[0:SUB, :], dsp, dba, dbx, dcb, dws[0], dws[1], dws[2], dws[3])

        init = (zero_row, zero_row, jnp.zeros((SUB, LRU_G), f32)) + (zero_row,) * 8
        out = lax.fori_loop(0, n_chunk, chunk, init)
        _, _, _, dsp, dba, dbx, dcb, dw0, dw1, dw2, dw3 = out
        dlam_ref[...] = dsp * (-_sigmoid(-lam_v))
        dba_ref[...] = dba
        dbx_ref[...] = dbx
        dcb_ref[...] = dcb
        dcw_ref[...] = jnp.concatenate([dw0, dw1, dw2, dw3], axis=0)
        ub = u_ref[...].astype(bf16)
        dwa_ref[...] = lax.dot_general(ub, dpr_s[...], _TN, preferred_element_type=f32)
        dwx_ref[...] = lax.dot_general(ub, dpx_s[...], _TN, preferred_element_type=f32)

    seq = lambda cb: pl.BlockSpec((t, LRU_G), lambda c, cb=cb: (0, c + cb))
    rowc = pl.BlockSpec((1, LRU_G), lambda c: (0, c))
    diag = pl.BlockSpec((LRU_G, LRU_G), lambda c: (c, c))
    gate_out = pl.BlockSpec((None, LRU_G, LRU_G), lambda c: (c, 0, 0))
    row_shape = jax.ShapeDtypeStruct((1, LRU_W), f32)
    return _hosted_call(
        host, body, name=name, grid=(LRU_W // LRU_G,),
        in_specs=[seq(LRU_W // LRU_G), seq(0), seq(LRU_W // LRU_G), seq(0), seq(0),
                  pl.BlockSpec((4, LRU_G), lambda c: (0, c)),
                  diag, rowc, diag, rowc, rowc],
        out_specs=[seq(0), seq(0), pl.BlockSpec((4, LRU_G), lambda c: (0, c)), rowc, rowc, rowc, rowc,
                   gate_out, gate_out],
        out_shape=[jax.ShapeDtypeStruct((t, LRU_W), bf16)] * 2
        + [jax.ShapeDtypeStruct((4, LRU_W), f32)] + [row_shape] * 4
        + [jax.ShapeDtypeStruct((LRU_W // LRU_G, LRU_G, LRU_G), f32)] * 2,
        scratch_shapes=[pltpu.VMEM((t, LRU_G), bf16), pltpu.VMEM((t, LRU_G), bf16)],
        compiler_params=_params(("arbitrary",)),
    )(dmix, zl, zl, u_all, h_all, conv_w, wa, ba, wx, bx, lam)


def _block_diag(w):
    eye = jnp.eye(HEADS, dtype=w.dtype)
    return jnp.einsum("hij,hk->hikj", w, eye).reshape(LRU_W, LRU_W)


def _diag_blocks(dw):
    per = dw.shape[1] // HEAD_D
    blocks = [dw[:, HEAD_D * b:HEAD_D * (b + 1), HEAD_D * b:HEAD_D * (b + 1)] for b in range(per)]
    return jnp.stack(blocks, axis=1).reshape(HEADS, HEAD_D, HEAD_D)


def _local_step(x, target, sent, small, *, tm=512, tm_ffn=1024):
    t = x.shape[0]
    ones = jnp.ones((1, D_MODEL), f32)
    zeros = jnp.zeros((1, D_MODEL), f32)
    ln1 = (small["ln1_g"], small["ln1_b"])
    ln2 = (small["ln2_g"], small["ln2_b"])
    ln3 = (small["ln3_g"], small["ln3_b"])

    xh1, rs1, hg1, hu1, wg1, wu1, wd1, w_in_g, w_out_g, conv_w_g = _ffn1_fwd_gathering(
        x, (sent["ffn1_w_gate"], sent["ffn1_w_up"], sent["ffn1_w_down"]),
        _Exchange([sent["w_in"], sent["w_out"], sent["conv_w"]], gather=True), tm=tm_ffn, name="ffn1_fwd")
    w_in = jnp.pad(w_in_g.transpose(1, 0, 2).reshape(D_MODEL, IN_COLS), ((0, 0), (0, 21 * LANES - IN_COLS)))
    w_out = w_out_g.reshape(D_MODEL, D_MODEL)
    conv_w = conv_w_g.transpose(1, 0, 2).reshape(4, LRU_W)
    qkv, zl, zfg = _in_proj(xh1, ln1[0], ln1[1], w_in, tm=tm, name="in_proj")
    bfg = jnp.pad(small["b_forget"], ((0, 0), (0, LANES - HEADS)))
    cum = _cum_fwd(zfg, bfg, name="cum_fwd")
    q_aug, k_aug, v_aug = _attn_prep_fwd(qkv, cum, tm=tm, name="attn_prep_fwd")
    o, lse, wg2, wu2 = _attn_fwd2(q_aug, k_aug, v_aug, name="attn_fwd",
                                  host=_Exchange([sent["ffn2_w_gate"], sent["ffn2_w_up"]], gather=True))
    wa_bd = _block_diag(small["rg_wa"]).astype(bf16)
    wx_bd = _block_diag(small["rg_wx"]).astype(bf16)
    ba = small["rg_ba"].reshape(1, LRU_W)
    bx = small["rg_bx"].reshape(1, LRU_W)
    u, h, lru, wd2 = _lru_fwd(zl, conv_w, small["conv_b"], wa_bd, ba, wx_bd, bx, small["lru_lambda"],
                              name="lru_fwd", host=_Exchange([sent["ffn2_w_down"]], gather=True))
    xh2, rs2 = _mmln([(o, 0, FOX_W, w_out, 0, D_MODEL, "nn"), (lru, 0, LRU_W, w_out, 1, D_MODEL, "nn")],
                     tm=tm, name="mix_fwd", resid=("affine", xh1) + ln1, resid_scale=ALPHA, epi="ln_fwd")
    xh3, rs3, hg2, hu2 = _ffn_fwd(xh2, ln2[0], ln2[1], wg2, wu2, wd2, tm=tm_ffn, name="ffn2_fwd")

    dpre3, sq_rows, g_ln3g, g_ln3b = _loss_bwd(xh3, rs3, ln3[0], ln3[1], target, tm=tm, name="loss_bwd")
    dpre2, g_ln2g, g_ln2b, dhg2, dhu2, a2 = _ffn_bwd(dpre3, hg2, hu2, wg2, wu2, wd2,
                                                     (xh2, rs2, ln2[0]), tm=tm_ffn, name="ffn2_bwd")
    wgrad = dict(out_dtype=bf16, tm=D_MODEL, mb=1, tn=FF_TILE, nb=4, tk=512, pair=True)
    wdgrad = dict(out_dtype=bf16, tm=512, mb=4, tn=D_MODEL, nb=1, tk=512, out_scale=0.5, pair=True)
    between_chips = functools.partial(_Exchange, gather=False, chips=True)
    g_wg2 = _mm_tn(xh2, dhg2, name="g_wg2", affine=ln2, **wgrad)
    g_wu2 = _mm_tn(xh2, dhu2, name="g_wu2", affine=ln2, **wgrad)
    g_wd2 = _mm_tn(a2, dpre3, name="g_wd2", **wdgrad)

    dmix = _mmln([(dpre2, 0, D_MODEL, w_out, 0, D_MODEL, "nt")], tm=tm, name="dmix_bwd")
    g_wout_a = _mm(o, dpre2, mode="tn", out_dtype=bf16, tm=512, tn=D_MODEL, tk=512, name="g_wout_fox")
    g_wout_b = _mm(lru, dpre2, mode="tn", out_dtype=bf16, tm=512, tn=D_MODEL, tk=512, name="g_wout_lru")
    dlx, dlg, g_cw, g_cb, g_ba, g_bx, g_lam, g_wa4, g_wx4, *p_wg2 = _lru_bwd(
        dmix, zl, u, h, conv_w, wa_bd, ba, wx_bd, bx, small["lru_lambda"], name="lru_bwd",
        host=between_chips([g_wg2]))
    p_wg2 = p_wg2[0]
    qb_aug, do_aug = _attn_prep_bwd(qkv, cum, lse, dmix, o, tm=tm, name="attn_prep_bwd")
    dq, dcum_q, dk, dv, dcum_k, p_wu2, p_wd2 = _attn_bwd(qb_aug, k_aug, v_aug, do_aug, name="attn_bwd",
                                                         host=between_chips([g_wu2, g_wd2]))
    g_wout_blocked = jnp.concatenate([g_wout_a, g_wout_b], axis=0).reshape(N_DEV, D_MODEL // N_DEV, D_MODEL)
    dfg, g_bf = _cum_bwd(dcum_q, dcum_k, zfg, bfg, name="cum_bwd")

    dz = [(dq, 0, 512), (dk, 1, 512), (dv, 2, 512), (dlx, 3, 512), (dlg, 4, 512), (dfg, 20, LANES)]
    dpre1, g_ln1g, g_ln1b = _mmln(
        [(arr, 0, w, w_in, cb, w, "nt") for (arr, cb, w) in dz],
        tm=tm, name="dx1_bwd", resid=("plain", dpre2), resid_scale=ALPHA, epi="ln_bwd", ln=(xh1, rs1, ln1[0]))
    g_win_main = _mm_tn(xh1, [arr for arr, _, _ in dz[:5]], out_dtype=bf16, tm=D_MODEL, mb=1, tn=512, nb=5, tk=512,
                        name="g_win", affine=ln1, out_blocked=True)
    g_win = [g_win_main[n] for n in range(5)] + [
        _mm(xh1, dfg, mode="tn", out_dtype=bf16, tm=D_MODEL, tn=LANES, tk=512, name="g_win_fg", affine=ln1)]
    g_win_full = jnp.concatenate([g[:, :w] for g, (_, _, w) in zip(g_win, dz)], axis=1)[:, :IN_COLS]
    g_win_blocked = g_win_full.reshape(D_MODEL, N_DEV, IN_SHARD).transpose(1, 0, 2)
    dhg1, dhu1, a1, p_win, p_wout = _ffn_bwd_act(dpre1, hg1, hu1, wd1, tm=tm_ffn, name="ffn1_bwd_act",
                                                 host=_Exchange([g_win_blocked, g_wout_blocked], gather=False))
    small_g = {
        "ln1_g": g_ln1g, "ln1_b": g_ln1b, "b_forget": g_bf[:, :HEADS], "conv_w": g_cw, "conv_b": g_cb,
        "rg_wa": _diag_blocks(g_wa4), "rg_ba": g_ba.reshape(HEADS, HEAD_D),
        "rg_wx": _diag_blocks(g_wx4), "rg_bx": g_bx.reshape(HEADS, HEAD_D), "lru_lambda": g_lam,
        "ln2_g": g_ln2g, "ln2_b": g_ln2b, "ln3_g": g_ln3g, "ln3_b": g_ln3b,
    }
    small_g["loss"] = (0.5 / D_MODEL) * jnp.sum(sq_rows, keepdims=True)
    pieces = [small_g[n].reshape(-1) for n in PACKED]
    packed = jnp.concatenate(pieces + [jnp.zeros((PACK_ROWS * LANES - sum(p.shape[0] for p in pieces),), f32)])
    g_wg1, all_packed = _mm_tn(x, dhg1, name="g_wg1",
                               host=_Exchange([packed.reshape(PACK_ROWS, LANES)], gather=True), **wgrad)
    g_wu1, p_wg1 = _mm_tn(x, dhu1, name="g_wu1", host=between_chips([g_wg1]), **wgrad)
    g_wd1, p_wu1 = _mm_tn(a1, dpre1, name="g_wd1", host=between_chips([g_wu1]), **wdgrad)
    grad_x, p_wd1 = _ffn_bwd_dx(dpre1, dhg1, dhu1, wg1, wu1, tm=tm_ffn, name="ffn1_bwd_dx",
                                host=between_chips([g_wd1]))
    parts = {
        "ffn1_w_gate": p_wg1, "ffn1_w_up": p_wu1, "ffn1_w_down": p_wd1, "w_in": p_win, "w_out": p_wout,
        "ffn2_w_gate": p_wg2, "ffn2_w_up": p_wu2, "ffn2_w_down": p_wd2,
    }
    return sq_rows, grad_x, parts, all_packed, {n: small_g[n].shape for n in PACKED}


def _adam_math(w, g, m, v):
    m2 = ADAM_B1 * m + (1.0 - ADAM_B1) * g
    v2 = ADAM_B2 * v + (1.0 - ADAM_B2) * (g * g)
    m_hat = m2 / (1.0 - ADAM_B1 ** ADAM_STEP)
    v_hat = v2 / (1.0 - ADAM_B2 ** ADAM_STEP)
    delta = -ADAM_LR * (m_hat / (jnp.sqrt(v_hat) + ADAM_EPS) + ADAM_WD * w)
    return delta, m2, v2


ADAM_TILE_ELEMS = 128 * 1024


def _adamw_big(parts, w, m, v, *, name):
    r, c = w.shape
    n_parts = parts.shape[0]
    tr = max(d for d in range(8, r + 1, 8) if r % d == 0 and d * c <= ADAM_TILE_ELEMS)

    def body(p_ref, w_ref, m_ref, v_ref, g_ref, d_ref, m2_ref, v2_ref):
        g = p_ref[0].astype(f32)
        for q in range(1, n_parts):
            g = g + p_ref[q].astype(f32)
        d, m2, v2 = _adam_math(w_ref[...], g, m_ref[...], v_ref[...])
        g_ref[...] = g
        d_ref[...] = d
        m2_ref[...] = m2
        v2_ref[...] = v2

    blk = pl.BlockSpec((tr, c), lambda i: (i, 0))
    return pl.pallas_call(
        body, name=name, grid=(r // tr,),
        in_specs=[pl.BlockSpec((n_parts, tr, c), lambda i: (0, i, 0)), blk, blk, blk],
        out_specs=[blk] * 4, out_shape=[jax.ShapeDtypeStruct((r, c), f32)] * 4,
        compiler_params=_params(("arbitrary",)),
    )(parts, w, m, v)


def _adamw_small(items, *, name):
    n = len(items)

    def body(*refs):
        ins, outs = refs[:4 * n], refs[4 * n:]
        for k in range(n):
            g, w, m, v = (ins[4 * k + q][...] for q in range(4))
            d, m2, v2 = _adam_math(w, g, m, v)
            outs[3 * k][...] = d
            outs[3 * k + 1][...] = m2
            outs[3 * k + 2][...] = v2

    vm = pl.BlockSpec(memory_space=pltpu.VMEM)
    flat = [a for item in items for a in item]
    out_shape = [jax.ShapeDtypeStruct(item[1].shape, f32) for item in items for _ in range(3)]
    return pl.pallas_call(
        body, name=name, in_specs=[vm] * (4 * n), out_specs=[vm] * (3 * n), out_shape=out_shape,
    )(*flat)


def _sum_parts(parts, *, name):
    def body(p_ref, o_ref):
        acc = p_ref[0]
        for q in range(1, N_DEV):
            acc = acc + p_ref[q]
        o_ref[...] = acc

    vm = pl.BlockSpec(memory_space=pltpu.VMEM)
    return pl.pallas_call(
        body, name=name, in_specs=[vm], out_specs=vm, out_shape=jax.ShapeDtypeStruct(parts.shape[1:], f32),
    )(parts)


WEIGHTS = ["ffn1_w_gate", "ffn1_w_up", "ffn1_w_down", "ln1_g", "ln1_b", "w_in", "b_forget", "conv_w", "conv_b",
           "rg_wa", "rg_ba", "rg_wx", "rg_bx", "lru_lambda", "w_out", "ln2_g", "ln2_b",
           "ffn2_w_gate", "ffn2_w_up", "ffn2_w_down", "ln3_g", "ln3_b"]
BIG = ["ffn1_w_gate", "ffn1_w_up", "ffn1_w_down", "w_in", "w_out", "ffn2_w_gate", "ffn2_w_up", "ffn2_w_down"]
PACKED = ["ln1_g", "ln1_b", "ln2_g", "ln2_b", "ln3_g", "ln3_b", "conv_b", "rg_ba", "rg_bx", "lru_lambda",
          "conv_w", "rg_wa", "rg_wx", "b_forget", "loss"]
PACK_ROWS = 600


def _two_d(a):
    return a.reshape((-1, a.shape[-1]))


def _transport(a):
    return _two_d(a)


def kernel(x, ffn1_w_gate, ffn1_w_up, ffn1_w_down, ln1_g, ln1_b, w_in, b_forget, conv_w, conv_b, rg_wa, rg_ba, rg_wx, rg_bx, lru_lambda, w_out, ln2_g, ln2_b, ffn2_w_gate, ffn2_w_up, ffn2_w_down, ln3_g, ln3_b, loss_target, m_ffn1_w_gate, m_ffn1_w_up, m_ffn1_w_down, m_ln1_g, m_ln1_b, m_w_in, m_b_forget, m_conv_w, m_conv_b, m_rg_wa, m_rg_ba, m_rg_wx, m_rg_bx, m_lru_lambda, m_w_out, m_ln2_g, m_ln2_b, m_ffn2_w_gate, m_ffn2_w_up, m_ffn2_w_down, m_ln3_g, m_ln3_b, v_ffn1_w_gate, v_ffn1_w_up, v_ffn1_w_down, v_ln1_g, v_ln1_b, v_w_in, v_b_forget, v_conv_w, v_conv_b, v_rg_wa, v_rg_ba, v_rg_wx, v_rg_bx, v_lru_lambda, v_w_out, v_ln2_g, v_ln2_b, v_ffn2_w_gate, v_ffn2_w_up, v_ffn2_w_down, v_ln3_g, v_ln3_b):
    w_args = (ffn1_w_gate, ffn1_w_up, ffn1_w_down, ln1_g, ln1_b, w_in, b_forget, conv_w, conv_b, rg_wa, rg_ba, rg_wx, rg_bx, lru_lambda, w_out, ln2_g, ln2_b, ffn2_w_gate, ffn2_w_up, ffn2_w_down, ln3_g, ln3_b)
    m_args = (m_ffn1_w_gate, m_ffn1_w_up, m_ffn1_w_down, m_ln1_g, m_ln1_b, m_w_in, m_b_forget, m_conv_w, m_conv_b, m_rg_wa, m_rg_ba, m_rg_wx, m_rg_bx, m_lru_lambda, m_w_out, m_ln2_g, m_ln2_b, m_ffn2_w_gate, m_ffn2_w_up, m_ffn2_w_down, m_ln3_g, m_ln3_b)
    v_args = (v_ffn1_w_gate, v_ffn1_w_up, v_ffn1_w_down, v_ln1_g, v_ln1_b, v_w_in, v_b_forget, v_conv_w, v_conv_b, v_rg_wa, v_rg_ba, v_rg_wx, v_rg_bx, v_lru_lambda, v_w_out, v_ln2_g, v_ln2_b, v_ffn2_w_gate, v_ffn2_w_up, v_ffn2_w_down, v_ln3_g, v_ln3_b)
    w = dict(zip(WEIGHTS, w_args))
    m = dict(zip(WEIGHTS, m_args))
    v = dict(zip(WEIGHTS, v_args))
    me = 4 * lax.axis_index("x") + 2 * lax.axis_index("y") + lax.axis_index("c")

    sent = {n: _transport(w[n]).astype(bf16) for n in BIG}
    sent["conv_w"] = _two_d(w["conv_w"])
    small = {n: w[n] for n in ("ln1_g", "ln1_b", "ln2_g", "ln2_b", "ln3_g", "ln3_b", "b_forget", "conv_b",
                               "lru_lambda")}
    small.update({n: w[n][0] for n in ("rg_wa", "rg_ba", "rg_wx", "rg_bx")})

    sq_rows, grad_x, parts, all_packed, small_shapes = _local_step(x[0], loss_target[0], sent, small)

    total = _sum_parts(all_packed, name="sum_small_grads").reshape(-1)
    grads, off = {}, 0
    for n in PACKED:
        size = math.prod(small_shapes[n])
        grads[n] = total[off:off + size].reshape(small_shapes[n])
        off += size
    loss = grads.pop("loss").reshape(())
    grads["conv_w"] = lax.dynamic_slice_in_dim(grads["conv_w"], me * (LRU_W // N_DEV), LRU_W // N_DEV, axis=1)

    delta, new_m, new_v = {}, {}, {}
    for n in BIG:
        g, d, m2, v2 = _adamw_big(parts[n], _transport(w[n]), _transport(m[n]), _transport(v[n]),
                                  name="adamw_" + n)
        grads[n], delta[n], new_m[n], new_v[n] = g, d, m2, v2
    small_names = [n for n in WEIGHTS if n not in BIG]
    outs = _adamw_small([(_two_d(grads[n]), _two_d(w[n]), _two_d(m[n]), _two_d(v[n])) for n in small_names],
                        name="adamw_small")
    for k, n in enumerate(small_names):
        delta[n], new_m[n], new_v[n] = outs[3 * k], outs[3 * k + 1], outs[3 * k + 2]

    def shaped(d):
        return [d[n].reshape(w[n].shape) for n in WEIGHTS]

    return (loss, grad_x[None], *shaped(grads), *shaped(delta), *shaped(new_m), *shaped(new_v))
```

```python
import functools
import math

import jax
import jax.numpy as jnp
from jax import lax
from jax.experimental import pallas as pl
from jax.experimental.pallas import tpu as pltpu

f32 = jnp.float32
bf16 = jnp.bfloat16

N_DEV = 8
D_MODEL = 1024
D_FF = 4096
FF_TILE = D_FF // N_DEV
FOX_W = 512
LRU_W = 512
HEADS = 8
HEAD_D = 64
IN_COLS = 2568
IN_SHARD = IN_COLS // N_DEV
LANES = 128
LN_EPS = 1e-5
ALPHA = 2.0 ** 0.25
ATT_SCALE = 1.0 / math.sqrt(HEAD_D)
LRU_C = 8.0
NEG_BIG = -1e30

ADAM_LR = 0.001
ADAM_B1 = 0.9
ADAM_B2 = 0.999
ADAM_EPS = 1e-08
ADAM_WD = 0.01
ADAM_STEP = 10

VMEM_LIMIT = 56 * 1024 * 1024
MESH_T = pl.DeviceIdType.MESH


def _params(sem, **kw):
    return pltpu.CompilerParams(dimension_semantics=sem, vmem_limit_bytes=VMEM_LIMIT, **kw)


def _sigmoid(x):
    return 1.0 / (1.0 + jnp.exp(-x))


def _sigmoid_tanh(x):
    return 0.5 * jnp.tanh(0.5 * x) + 0.5


def _softplus(x):
    return jnp.maximum(x, 0.0) + jnp.log(1.0 + jnp.exp(-jnp.abs(x)))


def _one_minus_exp(x):
    series = -x * (1.0 + x * (0.5 + x * (1.0 / 6 + x * (1.0 / 24 + x * (1.0 / 120 + x * (1.0 / 720))))))
    return jnp.where(x > -0.125, series, 1.0 - jnp.exp(x))


_GELU_C = math.sqrt(2.0 / math.pi)


def _gelu_and_grad(x):
    inner = _GELU_C * (x + 0.044715 * x * x * x)
    t = jnp.tanh(inner)
    g = 0.5 * x * (1.0 + t)
    dg = 0.5 * (1.0 + t) + 0.5 * x * (1.0 - t * t) * _GELU_C * (1.0 + 3 * 0.044715 * x * x)
    return g, dg


def _ln_fwd_tile(pre):
    mu = jnp.mean(pre, axis=-1, keepdims=True)
    xc = pre - mu
    var = jnp.mean(xc * xc, axis=-1, keepdims=True)
    rstd = lax.rsqrt(var + LN_EPS)
    return xc * rstd, rstd


def _ln_bwd_tile(dy, xhat, rstd, g):
    dyg = dy * g
    m1 = jnp.mean(dyg, axis=-1, keepdims=True)
    m2 = jnp.mean(dyg * xhat, axis=-1, keepdims=True)
    dpre = rstd * (dyg - m1 - xhat * m2)
    return dpre, jnp.sum(dy * xhat, axis=0, keepdims=True), jnp.sum(dy, axis=0, keepdims=True)


_NT = (((1,), (1,)), ((), ()))
_TN = (((0,), (0,)), ((), ()))


class _Exchange:
    def __init__(self, arrs, gather, chips=False):
        self.arrs, self.gather, self.n, self.chips = list(arrs), gather, len(arrs), chips

    def out_shape(self):
        return [jax.ShapeDtypeStruct(((N_DEV,) + a.shape) if self.gather else a.shape, a.dtype) for a in self.arrs]

    def scratch(self):
        n_remote = self.n * (N_DEV - 1)
        return [pltpu.SemaphoreType.DMA((n_remote,)), pltpu.SemaphoreType.DMA((n_remote,)),
                pltpu.SemaphoreType.DMA((self.n,))]

    def copies(self, ins, outs, sems):
        send_sems, recv_sems, local_sems = sems
        x, y, c = lax.axis_index("x"), lax.axis_index("y"), lax.axis_index("c")
        me = 2 * x + y if self.chips else 4 * x + 2 * y + c
        out = []
        for k in range(self.n):
            for d in (range(2, N_DEV, 2) if self.chips else range(1, N_DEV)):
                px = 1 - x if d & 4 else x
                py = 1 - y if d & 2 else y
                pc = 1 - c if d & 1 else c
                sem = k * (N_DEV - 1) + d - 1
                out.append(pltpu.make_async_remote_copy(
                    src_ref=ins[k].at[2 * px + py if self.chips else 4 * px + 2 * py + pc], dst_ref=outs[k].at[me],
                    send_sem=send_sems.at[sem], recv_sem=recv_sems.at[sem],
                    device_id=(px, py, pc), device_id_type=MESH_T))
            out.append(pltpu.make_async_copy(ins[k].at[me], outs[k].at[me], local_sems.at[k]))
        return out

    def gather_copies(self, ins, outs, sems):
        send_sems, recv_sems, local_sems = sems
        x, y, c = lax.axis_index("x"), lax.axis_index("y"), lax.axis_index("c")
        sibling = (x, y, 1 - c)
        chips = [(1 - x, y), (x, 1 - y), (1 - x, 1 - y)]
        out = []
        for k in range(self.n):
            def copy(s, block, to, src=None, k=k):
                rows = outs[k].at[4 * block[0] + 2 * block[1] + block[2]]
                sem = k * (N_DEV - 1) + s
                return pltpu.make_async_remote_copy(
                    src_ref=rows if src is None else src, dst_ref=rows, send_sem=send_sems.at[sem],
                    recv_sem=recv_sems.at[sem], device_id=to, device_id_type=MESH_T)

            first = [copy(0, (x, y, c), sibling, src=ins[k])]
            first += [copy(1 + q, (x, y, c), (*chip, c), src=ins[k]) for q, chip in enumerate(chips)]
            passed = [copy(4 + q, (*chip, c), sibling) for q, chip in enumerate(chips)]
            own = pltpu.make_async_copy(ins[k], outs[k].at[4 * x + 2 * y + c], local_sems.at[k])
            out.append((first, passed, own, copy))
        return out, sibling, chips, (x, y, c)

    def start(self, ins, outs, sems):
        if not self.gather:
            for cp in self.copies(ins, outs, sems):
                cp.start()
            return
        per_array, _, _, _ = self.gather_copies(ins, outs, sems)
        for first, _, own, _ in per_array:
            own.start()
            for cp in first:
                cp.start()

    def relay(self, ins, outs, sems):
        per_array, sibling, chips, (x, y, c) = self.gather_copies(ins, outs, sems)
        for first, passed, own, copy in per_array:
            for q, chip in enumerate(chips):
                copy(1 + q, (*chip, c), (x, y, c)).wait_recv()
                passed[q].start()

    def wait(self, ins, outs, sems, relayed=False):
        if not self.gather:
            for cp in self.copies(ins, outs, sems):
                cp.wait()
            return
        if not relayed:
            self.relay(ins, outs, sems)
        per_array, sibling, chips, (x, y, c) = self.gather_copies(ins, outs, sems)
        for first, passed, own, copy in per_array:
            copy(0, sibling, (x, y, c)).wait_recv()
            for q, chip in enumerate(chips):
                copy(4 + q, (*chip, 1 - c), (x, y, c)).wait_recv()
            for cp in first + passed:
                cp.wait_send()
            own.wait()


def _hosted_call(host, body, *, name, grid, in_specs, out_specs, out_shape, scratch_shapes=(), compiler_params):
    out_specs = list(out_specs) if isinstance(out_specs, (list, tuple)) else [out_specs]
    out_shape = list(out_shape) if isinstance(out_shape, (list, tuple)) else [out_shape]
    if host is None:
        return pl.pallas_call(body, name=name, grid=grid, in_specs=in_specs, out_specs=out_specs,
                              out_shape=out_shape, scratch_shapes=list(scratch_shapes),
                              compiler_params=compiler_params)
    n_in, n_out, n_scr, k = len(in_specs), len(out_shape), len(scratch_shapes), host.n

    def wrapped(*refs):
        ins, h_in = refs[:n_in], refs[n_in:n_in + k]
        outs, h_out = refs[n_in + k:n_in + k + n_out], refs[n_in + k + n_out:n_in + 2 * k + n_out]
        scr, sems = refs[n_in + 2 * k + n_out:n_in + 2 * k + n_out + n_scr], refs[n_in + 2 * k + n_out + n_scr:]
        ids = [pl.program_id(a) for a in range(len(grid))]
        first = functools.reduce(jnp.logical_and, [i == 0 for i in ids])
        last = functools.reduce(jnp.logical_and, [i == g - 1 for i, g in zip(ids, grid)])
        steps = math.prod(grid)
        relay_at = (3 * steps) // 4 if host.gather and steps >= 8 else None

        @pl.when(first)
        def _():
            host.start(h_in, h_out, sems)

        if relay_at is not None:
            coords, rest = [], relay_at
            for g in reversed(grid):
                coords.append(rest % g)
                rest //= g

            @pl.when(functools.reduce(jnp.logical_and, [i == cd for i, cd in zip(ids, reversed(coords))]))
            def _():
                host.relay(h_in, h_out, sems)

        body(*ins, *outs, *scr)

        @pl.when(last)
        def _():
            host.wait(h_in, h_out, sems, relayed=relay_at is not None)

    hbm = pl.BlockSpec(memory_space=pl.ANY)
    call = pl.pallas_call(
        wrapped, name=name, grid=grid, in_specs=list(in_specs) + [hbm] * k, out_specs=out_specs + [hbm] * k,
        out_shape=out_shape + host.out_shape(), scratch_shapes=list(scratch_shapes) + host.scratch(),
        compiler_params=compiler_params)
    return lambda *args: call(*args, *host.arrs)


def _exchange(arrs, *, gather, name):
    host = _Exchange(arrs, gather)

    def body(*refs):
        ins, outs, sems = refs[:host.n], refs[host.n:2 * host.n], refs[2 * host.n:]
        host.start(ins, outs, sems)
        host.wait(ins, outs, sems)

    hbm = pl.BlockSpec(memory_space=pl.ANY)
    return pl.pallas_call(
        body, name=name, in_specs=[hbm] * host.n, out_specs=[hbm] * host.n, out_shape=host.out_shape(),
        scratch_shapes=host.scratch(), compiler_params=pltpu.CompilerParams(has_side_effects=True),
    )(*arrs)


def _ffn_fwd(xhat, g_in, b_in, wg, wu, wd, *, tm, name, host=None):
    t = xhat.shape[0]
    nj = N_DEV

    def body(x_ref, g_ref, b_ref, wg_ref, wu_ref, wd_ref, xo_ref, rstd_ref, hg_ref, hu_ref, xb, acc):
        j = pl.program_id(1)

        @pl.when(j == 0)
        def _():
            xb[...] = (x_ref[...] * g_ref[...] + b_ref[...]).astype(bf16)
            acc[...] = jnp.zeros_like(acc)

        hg = jnp.dot(xb[...], wg_ref[...], preferred_element_type=f32)
        hu = jnp.dot(xb[...], wu_ref[...], preferred_element_type=f32)
        hg_ref[...] = hg.astype(bf16)
        hu_ref[...] = hu.astype(bf16)
        a = hg * _sigmoid_tanh(hg) * hu
        acc[...] += jnp.dot(a.astype(bf16), wd_ref[...], preferred_element_type=f32)

        @pl.when(j == nj - 1)
        def _():
            x = x_ref[...] * g_ref[...] + b_ref[...]
            xo, rstd = _ln_fwd_tile(ALPHA * x + 0.5 * acc[...])
            xo_ref[...] = xo
            rstd_ref[...] = rstd

    row = pl.BlockSpec((1, D_MODEL), lambda i, j: (0, 0))
    return _hosted_call(
        host, body, name=name, grid=(t // tm, nj),
        in_specs=[pl.BlockSpec((tm, D_MODEL), lambda i, j: (i, 0)), row, row,
                  pl.BlockSpec((None, D_MODEL, FF_TILE), lambda i, j: (j, 0, 0)),
                  pl.BlockSpec((None, D_MODEL, FF_TILE), lambda i, j: (j, 0, 0)),
                  pl.BlockSpec((None, FF_TILE, D_MODEL), lambda i, j: (j, 0, 0))],
        out_specs=[pl.BlockSpec((tm, D_MODEL), lambda i, j: (i, 0)),
                   pl.BlockSpec((tm, 1), lambda i, j: (i, 0)),
                   pl.BlockSpec((tm, FF_TILE), lambda i, j: (i, j)),
                   pl.BlockSpec((tm, FF_TILE), lambda i, j: (i, j))],
        out_shape=[jax.ShapeDtypeStruct((t, D_MODEL), f32), jax.ShapeDtypeStruct((t, 1), f32),
                   jax.ShapeDtypeStruct((t, D_FF), bf16), jax.ShapeDtypeStruct((t, D_FF), bf16)],
        scratch_shapes=[pltpu.VMEM((tm, D_MODEL), bf16), pltpu.VMEM((tm, D_MODEL), f32)],
        compiler_params=_params(("arbitrary", "arbitrary")),
    )(xhat, g_in, b_in, wg, wu, wd)


def _ffn1_fwd_gathering(x, own, extra, *, tm, name):
    t = x.shape[0]
    n_i = t // tm
    n_arr = 3
    k_extra = extra.n
    ex = _Exchange(list(own), gather=True)
    ax, ay, ac = lax.axis_index("x"), lax.axis_index("y"), lax.axis_index("c")
    order = jnp.stack([4 * px + 2 * py + pc for px, py in ((ax, ay), (1 - ax, ay), (ax, 1 - ay), (1 - ax, 1 - ay))
                       for pc in (ac, 1 - ac)]).astype(jnp.int32)
    arrival = [None, (0, None), (1, 0), (4, None), (2, 1), (5, None), (3, 2), (6, None)]

    def body(order_ref, x_ref, *refs):
        w_in, e_in = refs[:n_arr], refs[n_arr:n_arr + k_extra]
        refs = refs[n_arr + k_extra:]
        xo_ref, rstd_ref, hg_ref, hu_ref = refs[:4]
        w_all, e_out = refs[4:4 + n_arr], refs[4 + n_arr:4 + n_arr + k_extra]
        acc, wgb, wub, wdb, fetch_sems, send_sems, recv_sems, local_sems = refs[4 + n_arr + k_extra:12 + n_arr + k_extra]
        e_sems = refs[12 + n_arr + k_extra:]
        bufs = (wgb, wub, wdb)
        s = pl.program_id(0)
        i = pl.program_id(1)
        per_array, sibling, chips, (x_, y_, c_) = ex.gather_copies(w_in, w_all, (send_sems, recv_sems, local_sems))

        def fetch(pos, slot):
            return [pltpu.make_async_copy(w_in[a] if pos == 0 else w_all[a].at[order_ref[pos]],
                                          bufs[a].at[slot], fetch_sems.at[n_arr * slot + a]) for a in range(n_arr)]

        def source_of(pos):
            chip = (x_, y_) if pos < 2 else chips[(pos - 2) // 2]
            return (*chip, c_ if pos % 2 == 0 else 1 - c_)

        @pl.when(jnp.logical_and(s == 0, i == 0))
        def _():
            for q in range(4):
                for first, _, own_copy, _ in per_array:
                    if q == 0:
                        own_copy.start()
                    first[q].start()
            for cp in fetch(0, 0):
                cp.start()
            for cp in fetch(0, 0):
                cp.wait()

        @pl.when(jnp.logical_and(s == N_DEV // 2, i == 0))
        def _():
            extra.start(e_in, e_out, e_sems)

        for pos in range(1, N_DEV):
            @pl.when(jnp.logical_and(s == pos - 1, i == min(1, n_i - 1)))
            def _(pos=pos):
                sem, passes = arrival[pos]
                for _, passed, _, copy in per_array:
                    copy(sem, source_of(pos), (x_, y_, c_)).wait_recv()
                    if passes is not None:
                        passed[passes].start()
                for cp in fetch(pos, pos % 2):
                    cp.start()

            @pl.when(jnp.logical_and(s == pos, i == 0))
            def _(pos=pos):
                for cp in fetch(pos, pos % 2):
                    cp.wait()

        slot = s % 2
        xb = x_ref[...].astype(bf16)
        hg = jnp.dot(xb, wgb[slot], preferred_element_type=f32)
        hu = jnp.dot(xb, wub[slot], preferred_element_type=f32)
        hg_ref[...] = hg.astype(bf16)
        hu_ref[...] = hu.astype(bf16)
        a = hg * _sigmoid_tanh(hg) * hu
        part = jnp.dot(a.astype(bf16), wdb[slot], preferred_element_type=f32)

        @pl.when(s == 0)
        def _():
            acc[i] = part

        @pl.when(s > 0)
        def _():
            acc[i] += part

        @pl.when(s == N_DEV - 1)
        def _():
            xo, rstd = _ln_fwd_tile(ALPHA * x_ref[...] + 0.5 * acc[i])
            xo_ref[...] = xo
            rstd_ref[...] = rstd

        @pl.when(jnp.logical_and(s == N_DEV - 1, i == n_i - 1))
        def _():
            for first, passed, own_copy, _ in per_array:
                for cp in first + passed:
                    cp.wait_send()
                own_copy.wait()
            extra.wait(e_in, e_out, e_sems)

    hbm = pl.BlockSpec(memory_space=pl.ANY)
    last = N_DEV - 1
    tok_out = pl.BlockSpec((tm, D_MODEL), lambda s, i, o: (jnp.where(s == last, i, 0), 0))
    col_out = pl.BlockSpec((tm, 1), lambda s, i, o: (jnp.where(s == last, i, 0), 0))
    hid = pl.BlockSpec((tm, FF_TILE), lambda s, i, o: (i, o[s]))
    shard_shapes = [(N_DEV,) + w.shape for w in own]
    grid_spec = pltpu.PrefetchScalarGridSpec(
        num_scalar_prefetch=1, grid=(N_DEV, n_i),
        in_specs=[pl.BlockSpec((tm, D_MODEL), lambda s, i, o: (i, 0))] + [hbm] * (n_arr + k_extra),
        out_specs=[tok_out, col_out, hid, hid] + [hbm] * (n_arr + k_extra),
        scratch_shapes=[pltpu.VMEM((n_i, tm, D_MODEL), f32)]
        + [pltpu.VMEM((2,) + w.shape, bf16) for w in own]
        + [pltpu.SemaphoreType.DMA((2 * n_arr,))] + ex.scratch() + extra.scratch())
    res = pl.pallas_call(
        body, name=name, grid_spec=grid_spec,
        out_shape=[jax.ShapeDtypeStruct((t, D_MODEL), f32), jax.ShapeDtypeStruct((t, 1), f32),
                   jax.ShapeDtypeStruct((t, D_FF), bf16), jax.ShapeDtypeStruct((t, D_FF), bf16)]
        + [jax.ShapeDtypeStruct(sh, bf16) for sh in shard_shapes] + extra.out_shape(),
        compiler_params=_params(("arbitrary", "arbitrary")),
    )(order, x, *own, *extra.arrs)
    return res


def _ffn_bwd(dpre, hg, hu, wg, wu, wd, ln_in, *, tm, name, host=None):
    t = dpre.shape[0]
    nj = N_DEV
    with_ln = ln_in is not None

    def body(*refs):
        if with_ln:
            (dp_ref, hg_ref, hu_ref, wg_ref, wu_ref, wd_ref, xh_ref, rs_ref, g_ref,
             dx_ref, gg_ref, gb_ref, dhg_ref, dhu_ref, a_ref, dfb, acc) = refs
        else:
            (dp_ref, hg_ref, hu_ref, wg_ref, wu_ref, wd_ref,
             dx_ref, dhg_ref, dhu_ref, a_ref, dfb, acc) = refs
        i = pl.program_id(0)
        j = pl.program_id(1)

        @pl.when(j == 0)
        def _():
            dfb[...] = (0.5 * dp_ref[...]).astype(bf16)
            acc[...] = jnp.zeros_like(acc)

        da = lax.dot_general(dfb[...], wd_ref[...], _NT, preferred_element_type=f32)
        hgv = hg_ref[...].astype(f32)
        huv = hu_ref[...].astype(f32)
        sg = _sigmoid_tanh(hgv)
        silu = hgv * sg
        a_ref[...] = (silu * huv).astype(bf16)
        dhu = (da * silu).astype(bf16)
        dhg = (da * huv * (sg * (1.0 + hgv * (1.0 - sg)))).astype(bf16)
        dhg_ref[...] = dhg
        dhu_ref[...] = dhu
        acc[...] += (lax.dot_general(dhg, wg_ref[...], _NT, preferred_element_type=f32)
                     + lax.dot_general(dhu, wu_ref[...], _NT, preferred_element_type=f32))

        @pl.when(j == nj - 1)
        def _():
            dx = ALPHA * dp_ref[...] + acc[...]
            if with_ln:
                dprev, gg, gb = _ln_bwd_tile(dx, xh_ref[...], rs_ref[...], g_ref[...])
                dx_ref[...] = dprev

                @pl.when(i == 0)
                def _():
                    gg_ref[...] = gg
                    gb_ref[...] = gb

                @pl.when(i > 0)
                def _():
                    gg_ref[...] += gg
                    gb_ref[...] += gb
            else:
                dx_ref[...] = dx

    tok = pl.BlockSpec((tm, D_MODEL), lambda i, j: (i, 0), pipeline_mode=pl.Buffered(1))
    row = pl.BlockSpec((1, D_MODEL), lambda i, j: (0, 0))
    hid = pl.BlockSpec((tm, FF_TILE), lambda i, j: (i, j))
    in_specs = [tok, hid, hid,
                pl.BlockSpec((None, D_MODEL, FF_TILE), lambda i, j: (j, 0, 0)),
                pl.BlockSpec((None, D_MODEL, FF_TILE), lambda i, j: (j, 0, 0)),
                pl.BlockSpec((None, FF_TILE, D_MODEL), lambda i, j: (j, 0, 0))]
    args = [dpre, hg, hu, wg, wu, wd]
    out_specs = [tok]
    out_shape = [jax.ShapeDtypeStruct((t, D_MODEL), f32)]
    if with_ln:
        in_specs += [tok, pl.BlockSpec((tm, 1), lambda i, j: (i, 0)), row]
        args += list(ln_in)
        out_specs += [row, row]
        out_shape += [jax.ShapeDtypeStruct((1, D_MODEL), f32)] * 2
    out_specs += [hid, hid, hid]
    out_shape += [jax.ShapeDtypeStruct((t, D_FF), bf16)] * 3
    return _hosted_call(
        host, body, name=name, grid=(t // tm, nj), in_specs=in_specs, out_specs=out_specs, out_shape=out_shape,
        scratch_shapes=[pltpu.VMEM((tm, D_MODEL), bf16), pltpu.VMEM((tm, D_MODEL), f32)],
        compiler_params=_params(("arbitrary", "arbitrary")),
    )(*args)


def _ffn_bwd_act(dpre, hg, hu, wd, *, tm, name, host=None):
    t = dpre.shape[0]

    def body(dp_ref, hg_ref, hu_ref, wd_ref, dhg_ref, dhu_ref, a_ref, dfb):
        @pl.when(pl.program_id(1) == 0)
        def _():
            dfb[...] = (0.5 * dp_ref[...]).astype(bf16)

        da = lax.dot_general(dfb[...], wd_ref[...], _NT, preferred_element_type=f32)
        hgv = hg_ref[...].astype(f32)
        huv = hu_ref[...].astype(f32)
        sg = _sigmoid_tanh(hgv)
        silu = hgv * sg
        a_ref[...] = (silu * huv).astype(bf16)
        dhu_ref[...] = (da * silu).astype(bf16)
        dhg_ref[...] = (da * huv * (sg * (1.0 + hgv * (1.0 - sg)))).astype(bf16)

    hid = pl.BlockSpec((tm, FF_TILE), lambda i, j: (i, j))
    return _hosted_call(
        host, body, name=name, grid=(t // tm, N_DEV),
        in_specs=[pl.BlockSpec((tm, D_MODEL), lambda i, j: (i, 0)), hid, hid,
                  pl.BlockSpec((None, FF_TILE, D_MODEL), lambda i, j: (j, 0, 0))],
        out_specs=[hid, hid, hid], out_shape=[jax.ShapeDtypeStruct((t, D_FF), bf16)] * 3,
        scratch_shapes=[pltpu.VMEM((tm, D_MODEL), bf16)],
        compiler_params=_params(("arbitrary", "arbitrary")),
    )(dpre, hg, hu, wd)


def _ffn_bwd_dx(dpre, dhg, dhu, wg, wu, *, tm, name, host=None):
    t = dpre.shape[0]
    nj = N_DEV

    def body(dp_ref, dhg_ref, dhu_ref, wg_ref, wu_ref, dx_ref, acc):
        j = pl.program_id(1)

        @pl.when(j == 0)
        def _():
            acc[...] = jnp.zeros_like(acc)

        acc[...] += (lax.dot_general(dhg_ref[...], wg_ref[...], _NT, preferred_element_type=f32)
                     + lax.dot_general(dhu_ref[...], wu_ref[...], _NT, preferred_element_type=f32))

        @pl.when(j == nj - 1)
        def _():
            dx_ref[...] = ALPHA * dp_ref[...] + acc[...]

    tok = pl.BlockSpec((tm, D_MODEL), lambda i, j: (i, 0))
    hid = pl.BlockSpec((tm, FF_TILE), lambda i, j: (i, j))
    wspec = pl.BlockSpec((None, D_MODEL, FF_TILE), lambda i, j: (j, 0, 0))
    return _hosted_call(
        host, body, name=name, grid=(t // tm, nj), in_specs=[tok, hid, hid, wspec, wspec],
        out_specs=[tok], out_shape=[jax.ShapeDtypeStruct((t, D_MODEL), f32)],
        scratch_shapes=[pltpu.VMEM((tm, D_MODEL), f32)],
        compiler_params=_params(("arbitrary", "arbitrary")),
    )(dpre, dhg, dhu, wg, wu)


def _mm(a, b, *, mode, out_dtype, tm, tn, tk, name, affine=None, a_cols=None, b_cols=None,
        b_blocked=False, out_blocked=False, out_scale=None):
    if mode == "nn":
        m_full, k_full = a.shape
        m_dim, k_dim = (m_full, a_cols[1]) if a_cols else (m_full, k_full)
    else:
        k_dim, m_full = a.shape
        m_dim = a_cols[1] if a_cols else m_full
    a_off = a_cols[0] if a_cols else 0
    if b_blocked:
        n_dim = b.shape[0] * b.shape[2]
        assert b.shape[2] == tn
    else:
        n_dim = b_cols[1] if b_cols else b.shape[1]
    b_off = b_cols[0] if b_cols else 0
    assert m_dim % tm == 0 and n_dim % tn == 0 and k_dim % tk == 0, (name, m_dim, n_dim, k_dim)
    nk = k_dim // tk

    def body(*refs):
        if affine is not None:
            a_ref, g_ref, s_ref, b_ref, o_ref, acc = refs
        else:
            a_ref, b_ref, o_ref, acc = refs
        k = pl.program_id(2)

        @pl.when(k == 0)
        def _():
            acc[...] = jnp.zeros_like(acc)

        av = a_ref[...]
        if affine is not None:
            av = av * g_ref[...] + s_ref[...]
        av = av.astype(bf16)
        bv = b_ref[...].astype(bf16)
        if mode == "nn":
            acc[...] += jnp.dot(av, bv, preferred_element_type=f32)
        else:
            acc[...] += lax.dot_general(av, bv, _TN, preferred_element_type=f32)

        @pl.when(k == nk - 1)
        def _():
            res = acc[...] if out_scale is None else acc[...] * out_scale
            o_ref[...] = res.astype(out_dtype)

    if mode == "nn":
        a_spec = pl.BlockSpec((tm, tk), lambda i, j, k: (i, k + a_off))
        aff_spec = pl.BlockSpec((1, tk), lambda i, j, k: (0, k + a_off))
    else:
        a_spec = pl.BlockSpec((tk, tm), lambda i, j, k: (k, i + a_off))
        aff_spec = pl.BlockSpec((1, tm), lambda i, j, k: (0, i + a_off))
    if b_blocked:
        b_spec = pl.BlockSpec((None, tk, tn), lambda i, j, k: (j, k, 0))
    else:
        b_spec = pl.BlockSpec((tk, tn), lambda i, j, k: (k, j + b_off))
    if out_blocked:
        o_spec = pl.BlockSpec((None, tm, tn), lambda i, j, k: (j, i, 0))
        o_shape = jax.ShapeDtypeStruct((n_dim // tn, m_dim, tn), out_dtype)
    else:
        o_spec = pl.BlockSpec((tm, tn), lambda i, j, k: (i, j))
        o_shape = jax.ShapeDtypeStruct((m_dim, n_dim), out_dtype)
    in_specs = [a_spec] + ([aff_spec, aff_spec] if affine is not None else []) + [b_spec]
    args = [a] + (list(affine) if affine is not None else []) + [b]
    return pl.pallas_call(
        body, name=name, grid=(m_dim // tm, n_dim // tn, nk), in_specs=in_specs, out_specs=o_spec,
        out_shape=o_shape, scratch_shapes=[pltpu.VMEM((tm, tn), f32)],
        compiler_params=_params(("arbitrary", "arbitrary", "arbitrary")),
    )(*args)


def _mm_tn(a, b, *, out_dtype, tm, mb, tn, nb, tk, name, affine=None, out_blocked=False, out_scale=None,
           pair=False, host=None):
    k_dim, m_dim = a.shape
    multi_b = isinstance(b, (list, tuple))
    b_list = list(b) if multi_b else [b]
    n_dim = nb * tn if multi_b else b.shape[1]
    assert m_dim % (mb * tm) == 0 and n_dim % (nb * tn) == 0 and k_dim % tk == 0, (name, m_dim, n_dim, k_dim)
    nk = k_dim // tk
    grid = (m_dim // (mb * tm), n_dim // (nb * tn), nk)
    if pair:
        assert mb * nb == 4 and grid[0] * grid[1] == 2 and out_dtype == bf16, name

    def body(*refs):
        if pair:
            refs, (acc, send_buf, recv_buf, send_sems, recv_sems) = refs[:-5], refs[-5:]
        else:
            refs, acc = refs[:-1], refs[-1]
        a_ref, o_ref = refs[0], refs[-1]
        if affine is not None:
            g_ref, s_ref = refs[1:3]
        b_refs = refs[3 if affine is not None else 1:-1]
        k = pl.program_id(2)

        @pl.when(k == 0)
        def _():
            acc[...] = jnp.zeros_like(acc)

        av = a_ref[...]
        if affine is not None:
            av = av * g_ref[...] + s_ref[...]
        av = av.astype(bf16)
        if multi_b:
            pieces = [r[...].astype(bf16) for r in b_refs]
        else:
            bv = b_refs[0][...].astype(bf16)
            pieces = [bv[:, jn * tn:(jn + 1) * tn] for jn in range(nb)]
        for im in range(mb):
            a_t = av[:, im * tm:(im + 1) * tm].T
            for jn in range(nb):
                acc[im * nb + jn] += jnp.dot(a_t, pieces[jn], preferred_element_type=f32)

        def scaled(v):
            return v if out_scale is None else v * out_scale

        @pl.when(k == nk - 1)
        def _():
            if pair:
                x, y, c = lax.axis_index("x"), lax.axis_index("y"), lax.axis_index("c")
                window = pl.program_id(0) + pl.program_id(1)
                swaps = []
                for cc in range(2):
                    send_buf[cc] = scaled(acc[2 * cc + 1 - c]).astype(bf16)
                    swaps.append(pltpu.make_async_remote_copy(
                        src_ref=send_buf.at[cc], dst_ref=recv_buf.at[window, cc],
                        send_sem=send_sems.at[2 * window + cc], recv_sem=recv_sems.at[2 * window + cc],
                        device_id=(x, y, 1 - c), device_id_type=MESH_T))
                    swaps[cc].start()
                for cc in range(2):
                    swaps[cc].wait_recv()
                    o_ref[cc] = (scaled(acc[2 * cc + c]) + recv_buf[window, cc].astype(f32)).astype(bf16)
                for cc in range(2):
                    swaps[cc].wait_send()
                return
            for im in range(mb):
                for jn in range(nb):
                    res = scaled(acc[im * nb + jn])
                    if out_blocked:
                        o_ref[jn, im * tm:(im + 1) * tm, :] = res.astype(out_dtype)
                    else:
                        o_ref[im * tm:(im + 1) * tm, jn * tn:(jn + 1) * tn] = res.astype(out_dtype)

    a_spec = pl.BlockSpec((tk, mb * tm), lambda i, j, k: (k, i))
    aff_spec = pl.BlockSpec((1, mb * tm), lambda i, j, k: (0, i))
    if multi_b:
        b_specs = [pl.BlockSpec((tk, tn), lambda i, j, k: (k, 0))] * nb
    else:
        b_specs = [pl.BlockSpec((tk, nb * tn), lambda i, j, k: (k, j))]
    scratch = [pltpu.VMEM((mb * nb, tm, tn), f32)]
    if pair:
        o_spec = pl.BlockSpec((2, tm, tn), lambda i, j, k: (i + j, 0, 0))
        o_shape = jax.ShapeDtypeStruct((4, tm, tn), out_dtype)
        scratch += [pltpu.VMEM((2, tm, tn), bf16), pltpu.VMEM((2, 2, tm, tn), bf16),
                    pltpu.SemaphoreType.DMA((4,)), pltpu.SemaphoreType.DMA((4,))]
    elif out_blocked:
        o_spec = pl.BlockSpec((nb, mb * tm, tn), lambda i, j, k: (j, i, 0))
        o_shape = jax.ShapeDtypeStruct((n_dim // tn, m_dim, tn), out_dtype)
    else:
        o_spec = pl.BlockSpec((mb * tm, nb * tn), lambda i, j, k: (i, j))
        o_shape = jax.ShapeDtypeStruct((m_dim, n_dim), out_dtype)
    in_specs = [a_spec] + ([aff_spec, aff_spec] if affine is not None else []) + b_specs
    args = [a] + (list(affine) if affine is not None else []) + b_list
    res = _hosted_call(
        host, body, name=name, grid=grid, in_specs=in_specs, out_specs=o_spec, out_shape=o_shape,
        scratch_shapes=scratch, compiler_params=_params(("arbitrary", "arbitrary", "arbitrary")),
    )(*args)
    return res[0] if host is None else res


def _in_proj(xhat, g, b, w_in, *, tm, name):
    t = xhat.shape[0]
    n_qkv, n_l = 3 * FOX_W, 2 * LRU_W

    def body(x_ref, g_ref, b_ref, w_ref, qkv_ref, zl_ref, zfg_ref):
        xb = (x_ref[...] * g_ref[...] + b_ref[...]).astype(bf16)
        qkv_ref[...] = jnp.dot(xb, w_ref[:, :n_qkv], preferred_element_type=f32).astype(bf16)
        zl_ref[...] = jnp.dot(xb, w_ref[:, n_qkv:n_qkv + n_l], preferred_element_type=f32)
        zfg_ref[...] = jnp.dot(xb, w_ref[:, n_qkv + n_l:], preferred_element_type=f32)

    row = pl.BlockSpec((1, D_MODEL), lambda i: (0, 0))
    return pl.pallas_call(
        body, name=name, grid=(t // tm,),
        in_specs=[pl.BlockSpec((tm, D_MODEL), lambda i: (i, 0)), row, row,
                  pl.BlockSpec(w_in.shape, lambda i: (0, 0))],
        out_specs=[pl.BlockSpec((tm, n_qkv), lambda i: (i, 0)), pl.BlockSpec((tm, n_l), lambda i: (i, 0)),
                   pl.BlockSpec((tm, LANES), lambda i: (i, 0))],
        out_shape=[jax.ShapeDtypeStruct((t, n_qkv), bf16), jax.ShapeDtypeStruct((t, n_l), f32),
                   jax.ShapeDtypeStruct((t, LANES), f32)],
        compiler_params=_params(("arbitrary",)),
    )(xhat, g, b, w_in)


def _mmln(pairs, *, tm, name, resid=None, resid_scale=1.0, epi=None, ln=None, n_out=D_MODEL):
    t = pairs[0][0].shape[0]
    n_pairs = len(pairs)
    n_resid = 0 if resid is None else len(resid) - 1

    def body(*refs):
        pos = 0
        val = None
        for p in range(n_pairs):
            a_ref, b_ref = refs[pos], refs[pos + 1]
            pos += 2
            av = a_ref[...].astype(bf16)
            bv = b_ref[...].astype(bf16)
            if pairs[p][6] == "nn":
                term = jnp.dot(av, bv, preferred_element_type=f32)
            else:
                term = lax.dot_general(av, bv, _NT, preferred_element_type=f32)
            val = term if val is None else val + term
        if resid is not None:
            if resid[0] == "plain":
                r = refs[pos][...]
            else:
                r = refs[pos][...] * refs[pos + 1][...] + refs[pos + 2][...]
            pos += n_resid
            val = val + resid_scale * r
        if epi is None:
            o_ref = refs[pos]
            o_ref[...] = val.astype(o_ref.dtype)
        elif epi == "ln_fwd":
            xo, rstd = _ln_fwd_tile(val)
            refs[pos][...] = xo
            refs[pos + 1][...] = rstd
        else:
            xh_ref, rs_ref, g_ref, dx_ref, gg_ref, gb_ref = refs[pos:pos + 6]
            dprev, gg, gb = _ln_bwd_tile(val, xh_ref[...], rs_ref[...], g_ref[...])
            dx_ref[...] = dprev
            i = pl.program_id(0)

            @pl.when(i == 0)
            def _():
                gg_ref[...] = gg
                gb_ref[...] = gb

            @pl.when(i > 0)
            def _():
                gg_ref[...] += gg
                gb_ref[...] += gb

    in_specs, args = [], []
    for (a, acb, aw, b, bcb, bw, mode) in pairs:
        in_specs.append(pl.BlockSpec((tm, aw), lambda i, acb=acb: (i, acb)))
        args.append(a)
        if mode == "nn":
            in_specs.append(pl.BlockSpec((aw, n_out), lambda i, bcb=bcb: (bcb, 0)))
        else:
            in_specs.append(pl.BlockSpec((n_out, bw), lambda i, bcb=bcb: (0, bcb)))
        args.append(b)
    tok = pl.BlockSpec((tm, n_out), lambda i: (i, 0))
    row = pl.BlockSpec((1, n_out), lambda i: (0, 0))
    col = pl.BlockSpec((tm, 1), lambda i: (i, 0))
    if resid is not None:
        in_specs += [tok] if resid[0] == "plain" else [tok, row, row]
        args += list(resid[1:])
    if epi is None:
        out_specs, out_shape = tok, jax.ShapeDtypeStruct((t, n_out), f32)
    elif epi == "ln_fwd":
        out_specs = [tok, col]
        out_shape = [jax.ShapeDtypeStruct((t, n_out), f32), jax.ShapeDtypeStruct((t, 1), f32)]
    else:
        in_specs += [tok, col, row]
        args += list(ln)
        out_specs = [tok, row, row]
        out_shape = [jax.ShapeDtypeStruct((t, n_out), f32)] + [jax.ShapeDtypeStruct((1, n_out), f32)] * 2
    return pl.pallas_call(
        body, name=name, grid=(t // tm,), in_specs=in_specs, out_specs=out_specs, out_shape=out_shape,
        compiler_params=_params(("arbitrary",)),
    )(*args)


def _loss_bwd(xhat, rstd, g, b, target, *, tm, name):
    t = xhat.shape[0]

    def body(xh_ref, rs_ref, g_ref, b_ref, tg_ref, dx_ref, sq_ref, gg_ref, gb_ref):
        i = pl.program_id(0)
        xh = xh_ref[...]
        diff = xh * g_ref[...] + b_ref[...] - tg_ref[...]
        sq = jnp.sum(diff * diff, axis=0, keepdims=True)
        dprev, gg, gb = _ln_bwd_tile(diff * (1.0 / D_MODEL), xh, rs_ref[...], g_ref[...])
        dx_ref[...] = dprev

        @pl.when(i == 0)
        def _():
            sq_ref[...] = sq
            gg_ref[...] = gg
            gb_ref[...] = gb

        @pl.when(i > 0)
        def _():
            sq_ref[...] += sq
            gg_ref[...] += gg
            gb_ref[...] += gb

    tok = pl.BlockSpec((tm, D_MODEL), lambda i: (i, 0))
    row = pl.BlockSpec((1, D_MODEL), lambda i: (0, 0))
    return pl.pallas_call(
        body, name=name, grid=(t // tm,),
        in_specs=[tok, pl.BlockSpec((tm, 1), lambda i: (i, 0)), row, row, tok],
        out_specs=[tok, row, row, row],
        out_shape=[jax.ShapeDtypeStruct((t, D_MODEL), f32)] + [jax.ShapeDtypeStruct((1, D_MODEL), f32)] * 3,
        compiler_params=_params(("arbitrary",)),
    )(xhat, rstd, g, b, target)


CUM_TILE = 256


def _tri(n, lower):
    r = lax.broadcasted_iota(jnp.int32, (n, n), 0)
    c = lax.broadcasted_iota(jnp.int32, (n, n), 1)
    return jnp.where((r >= c) if lower else (r <= c), 1.0, 0.0).astype(f32)


def _cum_fwd(zfg, bfg, *, name):
    t = zfg.shape[0]

    def body(z_ref, b_ref, o_ref, carry):
        @pl.when(pl.program_id(0) == 0)
        def _():
            carry[...] = jnp.zeros_like(carry)

        ls = -_softplus(-(z_ref[...] + b_ref[...]))
        c = jnp.dot(_tri(CUM_TILE, True), ls, preferred_element_type=f32,
                    precision=lax.Precision.HIGHEST) + carry[...]
        o_ref[...] = c
        carry[...] = c[CUM_TILE - 1:CUM_TILE, :]

    blk = pl.BlockSpec((CUM_TILE, LANES), lambda i: (i, 0))
    return pl.pallas_call(
        body, name=name, grid=(t // CUM_TILE,),
        in_specs=[blk, pl.BlockSpec((1, LANES), lambda i: (0, 0))], out_specs=blk,
        out_shape=jax.ShapeDtypeStruct((t, LANES), f32), scratch_shapes=[pltpu.VMEM((1, LANES), f32)],
        compiler_params=_params(("arbitrary",)),
    )(zfg, bfg)


def _cum_bwd(dcum_q, dcum_k, zfg, bfg, *, name):
    t = zfg.shape[0]
    n = t // CUM_TILE

    def body(d_ref, d2_ref, z_ref, b_ref, o_ref, s_ref, carry):
        i = pl.program_id(0)

        @pl.when(i == 0)
        def _():
            carry[...] = jnp.zeros_like(carry)

        dls = jnp.dot(_tri(CUM_TILE, False), d_ref[...] + d2_ref[...], preferred_element_type=f32,
                      precision=lax.Precision.HIGHEST) + carry[...]
        carry[...] = dls[0:1, :]
        lane = lax.broadcasted_iota(jnp.int32, (CUM_TILE, LANES), 1)
        dfg = jnp.where(lane < HEADS, dls * _sigmoid(-(z_ref[...] + b_ref[...])), 0.0)
        o_ref[...] = dfg
        tot = jnp.sum(dfg, axis=0, keepdims=True)

        @pl.when(i == 0)
        def _():
            s_ref[...] = tot

        @pl.when(i > 0)
        def _():
            s_ref[...] += tot

    blk = pl.BlockSpec((CUM_TILE, LANES), lambda i: (n - 1 - i, 0))
    row = pl.BlockSpec((1, LANES), lambda i: (0, 0))
    return pl.pallas_call(
        body, name=name, grid=(n,), in_specs=[blk, blk, blk, row], out_specs=[blk, row],
        out_shape=[jax.ShapeDtypeStruct((t, LANES), f32), jax.ShapeDtypeStruct((1, LANES), f32)],
        scratch_shapes=[pltpu.VMEM((1, LANES), f32)],
        compiler_params=_params(("arbitrary",)),
    )(dcum_q, dcum_k, zfg, bfg)


ATT_TILE = 512


ATT_ROWS = 32


def _causal_rows(r, transposed):
    rr = lax.broadcasted_iota(jnp.int32, (ATT_ROWS, ATT_TILE), 0) + r * ATT_ROWS
    cc = lax.broadcasted_iota(jnp.int32, (ATT_ROWS, ATT_TILE), 1)
    return (cc >= rr) if transposed else (rr >= cc)


def _causal(i, j, transposed):
    r = lax.broadcasted_iota(jnp.int32, (ATT_TILE, ATT_TILE), 0)
    c = lax.broadcasted_iota(jnp.int32, (ATT_TILE, ATT_TILE), 1)
    if transposed:
        return (c + i * ATT_TILE) >= (r + j * ATT_TILE)
    return (r + i * ATT_TILE) >= (c + j * ATT_TILE)


def _attn_fwd(qkv, cum, cum_t, *, name, host=None):
    t = qkv.shape[0]
    n = t // ATT_TILE
    tq = ATT_TILE

    def body(q_ref, k_ref, v_ref, cq_ref, ck_ref, o_ref, lse_ref, acc, m_s, l_s, c_s, s_s, p_s):
        i = pl.program_id(0)
        j = pl.program_id(1)

        @pl.when(j == 0)
        def _():
            acc[...] = jnp.zeros_like(acc)
            m_s[...] = jnp.full_like(m_s, NEG_BIG)
            l_s[...] = jnp.zeros_like(l_s)

        def block(masked):
            for h in range(HEADS):
                hs = slice(HEAD_D * h, HEAD_D * (h + 1))
                s_s[...] = lax.dot_general(q_ref[:, hs] * ATT_SCALE, k_ref[:, hs], _NT, preferred_element_type=f32)
                ck = ck_ref[h:h + 1, :]

                def rows_chunk(r, carry):
                    rows = pl.ds(pl.multiple_of(r * ATT_ROWS, ATT_ROWS), ATT_ROWS)
                    s = s_s[rows, :] + (cq_ref[rows, h:h + 1] - ck)
                    if masked:
                        s = jnp.where(_causal_rows(r, False), s, NEG_BIG)
                    m_old = m_s[rows, h:h + 1]
                    m_new = jnp.maximum(m_old, jnp.max(s, axis=-1, keepdims=True))
                    corr = jnp.exp(m_old - m_new)
                    p = jnp.exp(s - m_new)
                    l_s[rows, h:h + 1] = corr * l_s[rows, h:h + 1] + jnp.sum(p, axis=-1, keepdims=True)
                    m_s[rows, h:h + 1] = m_new
                    c_s[rows, h:h + 1] = corr
                    p_s[rows, :] = p.astype(bf16)
                    return carry

                lax.fori_loop(0, tq // ATT_ROWS, rows_chunk, 0, unroll=4)
                acc[:, hs] = c_s[:, h:h + 1] * acc[:, hs] + jnp.dot(p_s[...], v_ref[:, hs],
                                                                   preferred_element_type=f32)

        @pl.when(j < i)
        def _():
            block(False)

        @pl.when(j == i)
        def _():
            block(True)
            lse_ref[...] = jnp.zeros_like(lse_ref)
            for h in range(HEADS):
                hs = slice(HEAD_D * h, HEAD_D * (h + 1))
                l = l_s[:, h:h + 1]
                o_ref[:, hs] = acc[:, hs] / l
                lse_ref[:, h:h + 1] = m_s[:, h:h + 1] + jnp.log(l)

    return _hosted_call(
        host, body, name=name, grid=(n, n),
        in_specs=[pl.BlockSpec((tq, FOX_W), lambda i, j: (i, 0)),
                  pl.BlockSpec((tq, FOX_W), lambda i, j: (jnp.minimum(i, j), 1)),
                  pl.BlockSpec((tq, FOX_W), lambda i, j: (jnp.minimum(i, j), 2)),
                  pl.BlockSpec((tq, LANES), lambda i, j: (i, 0)),
                  pl.BlockSpec((HEADS, tq), lambda i, j: (0, jnp.minimum(i, j)))],
        out_specs=[pl.BlockSpec((tq, FOX_W), lambda i, j: (i, 0)), pl.BlockSpec((tq, LANES), lambda i, j: (i, 0))],
        out_shape=[jax.ShapeDtypeStruct((t, FOX_W), f32), jax.ShapeDtypeStruct((t, LANES), f32)],
        scratch_shapes=[pltpu.VMEM((tq, FOX_W), f32), pltpu.VMEM((tq, LANES), f32), pltpu.VMEM((tq, LANES), f32),
                        pltpu.VMEM((tq, LANES), f32), pltpu.VMEM((tq, tq), f32), pltpu.VMEM((tq, tq), bf16)],
        compiler_params=_params(("arbitrary", "arbitrary")),
    )(qkv, qkv, qkv, cum, cum_t)


def _attn_delta(dmix, o, *, tm, name):
    t = o.shape[0]

    def body(do_ref, o_ref, d_ref):
        r = lax.broadcasted_iota(jnp.int32, (FOX_W, LANES), 0)
        c = lax.broadcasted_iota(jnp.int32, (FOX_W, LANES), 1)
        pick = jnp.where(r // HEAD_D == c, 1.0, 0.0).astype(f32)
        d_ref[...] = jnp.dot(do_ref[...] * o_ref[...], pick, preferred_element_type=f32,
                             precision=lax.Precision.HIGHEST)

    blk = pl.BlockSpec((tm, FOX_W), lambda i: (i, 0))
    return pl.pallas_call(
        body, name=name, grid=(t // tm,), in_specs=[blk, blk],
        out_specs=pl.BlockSpec((tm, LANES), lambda i: (i, 0)),
        out_shape=jax.ShapeDtypeStruct((t, LANES), f32), compiler_params=_params(("arbitrary",)),
    )(dmix, o)


def _attn_dq(qkv, dmix, cum, cum_t, lse, delta, *, name, host=None):
    t = qkv.shape[0]
    n = t // ATT_TILE
    tq = ATT_TILE

    def body(q_ref, k_ref, v_ref, do_ref, cq_ref, ck_ref, lse_ref, dl_ref, dq_ref, dc_ref, acc, dc_acc):
        i = pl.program_id(0)
        j = pl.program_id(1)

        @pl.when(j == 0)
        def _():
            acc[...] = jnp.zeros_like(acc)
            dc_acc[...] = jnp.zeros_like(dc_acc)

        def block(masked):
            mask = _causal(i, j, False) if masked else None
            for h in range(HEADS):
                hs = slice(HEAD_D * h, HEAD_D * (h + 1))
                kh = k_ref[:, hs]
                s = lax.dot_general(q_ref[:, hs] * ATT_SCALE, kh, _NT, preferred_element_type=f32)
                s = s + cq_ref[:, h:h + 1] - ck_ref[h:h + 1, :]
                if masked:
                    s = jnp.where(mask, s, NEG_BIG)
                p = jnp.exp(s - lse_ref[:, h:h + 1])
                dp = lax.dot_general(do_ref[:, hs].astype(bf16), v_ref[:, hs], _NT, preferred_element_type=f32)
                ds = p * (dp - dl_ref[:, h:h + 1])
                acc[:, hs] += jnp.dot(ds.astype(bf16), kh, preferred_element_type=f32)
                dc_acc[:, h:h + 1] += jnp.sum(ds, axis=-1, keepdims=True)

        @pl.when(j < i)
        def _():
            block(False)

        @pl.when(j == i)
        def _():
            block(True)
            dq_ref[...] = (acc[...] * ATT_SCALE).astype(bf16)
            dc_ref[...] = dc_acc[...]

    col = pl.BlockSpec((tq, LANES), lambda i, j: (i, 0))
    return _hosted_call(
        host, body, name=name, grid=(n, n),
        in_specs=[pl.BlockSpec((tq, FOX_W), lambda i, j: (i, 0)),
                  pl.BlockSpec((tq, FOX_W), lambda i, j: (jnp.minimum(i, j), 1)),
                  pl.BlockSpec((tq, FOX_W), lambda i, j: (jnp.minimum(i, j), 2)),
                  pl.BlockSpec((tq, FOX_W), lambda i, j: (i, 0)),
                  col, pl.BlockSpec((HEADS, tq), lambda i, j: (0, jnp.minimum(i, j))), col, col],
        out_specs=[pl.BlockSpec((tq, FOX_W), lambda i, j: (i, 0)), col],
        out_shape=[jax.ShapeDtypeStruct((t, FOX_W), bf16), jax.ShapeDtypeStruct((t, LANES), f32)],
        scratch_shapes=[pltpu.VMEM((tq, FOX_W), f32), pltpu.VMEM((tq, LANES), f32)],
        compiler_params=_params(("arbitrary", "arbitrary")),
    )(qkv, qkv, qkv, dmix, cum, cum_t, lse, delta)


def _attn_dkv(qkv, dmix, cum, cum_t, lse_t, delta_t, *, name):
    t = qkv.shape[0]
    n = t // ATT_TILE
    tk = ATT_TILE

    def body(q_ref, k_ref, v_ref, do_ref, cq_ref, ck_ref, lse_ref, dl_ref, dk_ref, dv_ref, dc_ref, dk_acc, dv_acc, dc_acc):
        j = pl.program_id(0)
        i = pl.program_id(1)

        @pl.when(i == 0)
        def _():
            dk_acc[...] = jnp.zeros_like(dk_acc)
            dv_acc[...] = jnp.zeros_like(dv_acc)
            dc_acc[...] = jnp.zeros_like(dc_acc)

        def block(masked):
            mask = _causal(i, j, True) if masked else None
            for h in range(HEADS):
                hs = slice(HEAD_D * h, HEAD_D * (h + 1))
                qh = q_ref[:, hs]
                doh = do_ref[:, hs].astype(bf16)
                s_t = lax.dot_general(k_ref[:, hs] * ATT_SCALE, qh, _NT, preferred_element_type=f32)
                s_t = s_t + cq_ref[h:h + 1, :] - ck_ref[:, h:h + 1]
                if masked:
                    s_t = jnp.where(mask, s_t, NEG_BIG)
                p_t = jnp.exp(s_t - lse_ref[h:h + 1, :])
                dv_acc[:, hs] += jnp.dot(p_t.astype(bf16), doh, preferred_element_type=f32)
                dp_t = lax.dot_general(v_ref[:, hs], doh, _NT, preferred_element_type=f32)
                ds_t = p_t * (dp_t - dl_ref[h:h + 1, :])
                dk_acc[:, hs] += jnp.dot(ds_t.astype(bf16), qh, preferred_element_type=f32)
                dc_acc[:, h:h + 1] -= jnp.sum(ds_t, axis=-1, keepdims=True)

        @pl.when(i > j)
        def _():
            block(False)

        @pl.when(i == j)
        def _():
            block(True)

        @pl.when(i == n - 1)
        def _():
            dk_ref[...] = (dk_acc[...] * ATT_SCALE).astype(bf16)
            dv_ref[...] = dv_acc[...].astype(bf16)
            dc_ref[...] = dc_acc[...]

    rowq = pl.BlockSpec((HEADS, tk), lambda j, i: (0, jnp.maximum(i, j)))
    return pl.pallas_call(
        body, name=name, grid=(n, n),
        in_specs=[pl.BlockSpec((tk, FOX_W), lambda j, i: (jnp.maximum(i, j), 0)),
                  pl.BlockSpec((tk, FOX_W), lambda j, i: (j, 1)),
                  pl.BlockSpec((tk, FOX_W), lambda j, i: (j, 2)),
                  pl.BlockSpec((tk, FOX_W), lambda j, i: (jnp.maximum(i, j), 0)),
                  rowq, pl.BlockSpec((tk, LANES), lambda j, i: (j, 0)), rowq, rowq],
        out_specs=[pl.BlockSpec((tk, FOX_W), lambda j, i: (j, 0)), pl.BlockSpec((tk, FOX_W), lambda j, i: (j, 0)),
                   pl.BlockSpec((tk, LANES), lambda j, i: (j, 0))],
        out_shape=[jax.ShapeDtypeStruct((t, FOX_W), bf16), jax.ShapeDtypeStruct((t, FOX_W), bf16),
                   jax.ShapeDtypeStruct((t, LANES), f32)],
        scratch_shapes=[pltpu.VMEM((tk, FOX_W), f32), pltpu.VMEM((tk, FOX_W), f32), pltpu.VMEM((tk, LANES), f32)],
        compiler_params=_params(("arbitrary", "arbitrary")),
    )(qkv, qkv, qkv, dmix, cum_t, cum, lse_t, delta_t)


ATT_W = HEADS * LANES


def _data_lane(h):
    return HEAD_D * (h % 2)


def _extra_lane(h):
    return HEAD_D - _data_lane(h)


def _split3(x):
    hi = x.astype(bf16)
    rest = x - hi.astype(f32)
    mid = rest.astype(bf16)
    lo = (rest - mid.astype(f32)).astype(bf16)
    return hi, mid, lo


def _augment(pair, h, first, second, fill=0.0):
    rows = pair.shape[0]
    lane = lax.broadcasted_iota(jnp.int32, (rows, LANES), 1)
    base = _extra_lane(h)
    own = (lane < HEAD_D) if h % 2 == 0 else (lane >= HEAD_D)
    out = jnp.where(own, pair, jnp.full((rows, LANES), fill, bf16))
    for off, src in ((0, first), (3, second)):
        for q in range(3):
            val = src[q] if isinstance(src, tuple) else jnp.full((rows, 1), src, bf16)
            out = jnp.where(lane == base + off + q, val, out)
    return out


def _attn_prep_fwd(qkv, cum, *, tm, name):
    t = qkv.shape[0]

    def body(q_ref, k_ref, v_ref, c_ref, qa_ref, ka_ref, va_ref):
        for h in range(HEADS):
            pair = slice(LANES * (h // 2), LANES * (h // 2 + 1))
            hs = slice(LANES * h, LANES * (h + 1))
            c3 = _split3(c_ref[:, h:h + 1])
            qa_ref[:, hs] = _augment(q_ref[:, pair] * ATT_SCALE, h, c3, 1.0)
            ka_ref[:, hs] = _augment(k_ref[:, pair], h, 1.0, tuple(-p for p in c3))
            va_ref[:, hs] = _augment(v_ref[:, pair], h, 1.0, 1.0, fill=1.0)

    wide = pl.BlockSpec((tm, ATT_W), lambda i: (i, 0))
    out = jax.ShapeDtypeStruct((t, ATT_W), bf16)
    return pl.pallas_call(
        body, name=name, grid=(t // tm,),
        in_specs=[pl.BlockSpec((tm, FOX_W), lambda i: (i, 0)), pl.BlockSpec((tm, FOX_W), lambda i: (i, 1)),
                  pl.BlockSpec((tm, FOX_W), lambda i: (i, 2)), pl.BlockSpec((tm, LANES), lambda i: (i, 0))],
        out_specs=[wide] * 3, out_shape=[out] * 3, compiler_params=_params(("arbitrary",)),
    )(qkv, qkv, qkv, cum)


def _attn_prep_bwd(qkv, cum, lse, dmix, o, *, tm, name):
    t = qkv.shape[0]

    def body(q_ref, c_ref, l_ref, do_ref, o_ref, qa_ref, da_ref):
        for h in range(HEADS):
            pair = slice(LANES * (h // 2), LANES * (h // 2 + 1))
            src = slice(HEAD_D * h, HEAD_D * (h + 1))
            hs = slice(LANES * h, LANES * (h + 1))
            delta = jnp.sum(do_ref[:, src] * o_ref[:, src], axis=-1, keepdims=True)
            qa_ref[:, hs] = _augment(q_ref[:, pair] * ATT_SCALE, h,
                                     _split3(c_ref[:, h:h + 1] - l_ref[:, h:h + 1]), 1.0)
            da_ref[:, hs] = _augment(do_ref[:, pair].astype(bf16), h, tuple(-p for p in _split3(delta)), 0.0)

    wide = pl.BlockSpec((tm, ATT_W), lambda i: (i, 0))
    half = pl.BlockSpec((tm, FOX_W), lambda i: (i, 0))
    col = pl.BlockSpec((tm, LANES), lambda i: (i, 0))
    out = jax.ShapeDtypeStruct((t, ATT_W), bf16)
    return pl.pallas_call(
        body, name=name, grid=(t // tm,), in_specs=[half, col, col, half, half],
        out_specs=[wide] * 2, out_shape=[out] * 2, compiler_params=_params(("arbitrary",)),
    )(qkv, cum, lse, dmix, o)


def _attn_fwd2(q_aug, k_aug, v_aug, *, name, host=None):
    t = q_aug.shape[0]
    n = t // ATT_TILE
    tq = ATT_TILE

    def body(q_ref, k_ref, v_ref, o_ref, lse_ref, acc, m_s):
        i = pl.program_id(0)
        j = pl.program_id(1)

        @pl.when(j == 0)
        def _():
            acc[...] = jnp.zeros_like(acc)
            m_s[...] = jnp.full_like(m_s, NEG_BIG)

        def block(masked):
            mask = _causal(i, j, False) if masked else None
            for h in range(HEADS):
                hs = slice(LANES * h, LANES * (h + 1))
                s = lax.dot_general(q_ref[:, hs], k_ref[:, hs], _NT, preferred_element_type=f32)
                if masked:
                    s = jnp.where(mask, s, NEG_BIG)
                blocks = [s[:, LANES * b:LANES * (b + 1)] for b in range(tq // LANES)]
                m_old = m_s[h]
                m_new = jnp.maximum(m_old, jnp.broadcast_to(
                    jnp.max(functools.reduce(jnp.maximum, blocks), axis=-1, keepdims=True), (tq, LANES)))
                p = jnp.concatenate([jnp.exp(b - m_new) for b in blocks], axis=1).astype(bf16)
                acc[h] = jnp.exp(m_old - m_new) * acc[h] + jnp.dot(p, v_ref[:, hs], preferred_element_type=f32)
                m_s[h] = m_new

        @pl.when(j < i)
        def _():
            block(False)

        @pl.when(j == i)
        def _():
            block(True)
            lse_ref[...] = jnp.zeros_like(lse_ref)
            for h in range(HEADS):
                a = acc[h]
                l = a[:, _extra_lane(h):_extra_lane(h) + 1]
                o_ref[:, HEAD_D * h:HEAD_D * (h + 1)] = a[:, _data_lane(h):_data_lane(h) + HEAD_D] / l
                lse_ref[:, h:h + 1] = m_s[h][:, 0:1] + jnp.log(l)

    kv = pl.BlockSpec((tq, ATT_W), lambda i, j: (jnp.minimum(i, j), 0))
    return _hosted_call(
        host, body, name=name, grid=(n, n),
        in_specs=[pl.BlockSpec((tq, ATT_W), lambda i, j: (i, 0)), kv, kv],
        out_specs=[pl.BlockSpec((tq, FOX_W), lambda i, j: (i, 0)), pl.BlockSpec((tq, LANES), lambda i, j: (i, 0))],
        out_shape=[jax.ShapeDtypeStruct((t, FOX_W), f32), jax.ShapeDtypeStruct((t, LANES), f32)],
        scratch_shapes=[pltpu.VMEM((HEADS, tq, LANES), f32), pltpu.VMEM((HEADS, tq, LANES), f32)],
        compiler_params=_params(("arbitrary", "arbitrary")),
    )(q_aug, k_aug, v_aug)


def _attn_bwd(qb_aug, k_aug, v_aug, do_aug, *, name, host=None):
    t = qb_aug.shape[0]
    n = t // ATT_TILE
    tk = ATT_TILE

    def body(q_ref, k_ref, v_ref, do_ref, dq_ref, dcq_ref, dk_ref, dv_ref, dck_ref, dk_acc, dv_acc, dq_all):
        j = pl.program_id(0)
        i = pl.program_id(1)

        @pl.when(jnp.logical_and(i == 0, j == 0))
        def _():
            dq_all[...] = jnp.zeros_like(dq_all)

        @pl.when(i == 0)
        def _():
            dk_acc[...] = jnp.zeros_like(dk_acc)
            dv_acc[...] = jnp.zeros_like(dv_acc)

        def block(masked):
            mask = _causal(i, j, True) if masked else None
            for h in range(HEADS):
                hs = slice(LANES * h, LANES * (h + 1))
                qh = q_ref[:, hs]
                doh = do_ref[:, hs]
                kh = k_ref[:, hs]
                s_t = lax.dot_general(kh, qh, _NT, preferred_element_type=f32)
                if masked:
                    s_t = jnp.where(mask, s_t, NEG_BIG)
                p_t = jnp.exp(s_t)
                dv_acc[h] += jnp.dot(p_t.astype(bf16), doh, preferred_element_type=f32)
                dp_t = lax.dot_general(v_ref[:, hs], doh, _NT, preferred_element_type=f32)
                ds_t = (p_t * dp_t).astype(bf16)
                dk_acc[h] += jnp.dot(ds_t, qh, preferred_element_type=f32)
                dq_all[i, h] += lax.dot_general(ds_t, kh, _TN, preferred_element_type=f32)

        @pl.when(i > j)
        def _():
            block(False)

        @pl.when(i == j)
        def _():
            block(True)
            dcq_ref[...] = jnp.zeros_like(dcq_ref)
            for h in range(HEADS):
                a = dq_all[j, h]
                dq_ref[:, HEAD_D * h:HEAD_D * (h + 1)] = (
                    a[:, _data_lane(h):_data_lane(h) + HEAD_D] * ATT_SCALE).astype(bf16)
                dcq_ref[:, h:h + 1] = a[:, _extra_lane(h):_extra_lane(h) + 1]

        @pl.when(i == n - 1)
        def _():
            dck_ref[...] = jnp.zeros_like(dck_ref)
            for h in range(HEADS):
                a = dk_acc[h]
                cols = slice(_data_lane(h), _data_lane(h) + HEAD_D)
                dk_ref[:, HEAD_D * h:HEAD_D * (h + 1)] = a[:, cols].astype(bf16)
                dv_ref[:, HEAD_D * h:HEAD_D * (h + 1)] = dv_acc[h][:, cols].astype(bf16)
                dck_ref[:, h:h + 1] = -a[:, _extra_lane(h) + 3:_extra_lane(h) + 4]

    own = pl.BlockSpec((tk, ATT_W), lambda j, i: (j, 0))
    qs = pl.BlockSpec((tk, ATT_W), lambda j, i: (jnp.maximum(i, j), 0))
    half = pl.BlockSpec((tk, FOX_W), lambda j, i: (j, 0))
    col = pl.BlockSpec((tk, LANES), lambda j, i: (j, 0))
    return _hosted_call(
        host, body, name=name, grid=(n, n), in_specs=[qs, own, own, qs],
        out_specs=[half, col, half, half, col],
        out_shape=[jax.ShapeDtypeStruct((t, FOX_W), bf16), jax.ShapeDtypeStruct((t, LANES), f32),
                   jax.ShapeDtypeStruct((t, FOX_W), bf16), jax.ShapeDtypeStruct((t, FOX_W), bf16),
                   jax.ShapeDtypeStruct((t, LANES), f32)],
        scratch_shapes=[pltpu.VMEM((HEADS, tk, LANES), f32), pltpu.VMEM((HEADS, tk, LANES), f32),
                        pltpu.VMEM((n, HEADS, tk, LANES), f32)],
        compiler_params=_params(("arbitrary", "arbitrary")),
    )(qb_aug, k_aug, v_aug, do_aug)


def _attn_dq2(qb_aug, k_aug, v_aug, do_aug, *, name, host=None):
    t = qb_aug.shape[0]
    n = t // ATT_TILE
    tq = ATT_TILE

    def body(q_ref, k_ref, v_ref, do_ref, dq_ref, dc_ref, acc):
        i = pl.program_id(0)
        j = pl.program_id(1)

        @pl.when(j == 0)
        def _():
            acc[...] = jnp.zeros_like(acc)

        def block(masked):
            mask = _causal(i, j, False) if masked else None
            for h in range(HEADS):
                hs = slice(LANES * h, LANES * (h + 1))
                kh = k_ref[:, hs]
                s = lax.dot_general(q_ref[:, hs], kh, _NT, preferred_element_type=f32)
                if masked:
                    s = jnp.where(mask, s, NEG_BIG)
                dp = lax.dot_general(do_ref[:, hs], v_ref[:, hs], _NT, preferred_element_type=f32)
                ds = (jnp.exp(s) * dp).astype(bf16)
                acc[h] += jnp.dot(ds, kh, preferred_element_type=f32)

        @pl.when(j < i)
        def _():
            block(False)

        @pl.when(j == i)
        def _():
            block(True)
            dc_ref[...] = jnp.zeros_like(dc_ref)
            for h in range(HEADS):
                a = acc[h]
                dq_ref[:, HEAD_D * h:HEAD_D * (h + 1)] = (
                    a[:, _data_lane(h):_data_lane(h) + HEAD_D] * ATT_SCALE).astype(bf16)
                dc_ref[:, h:h + 1] = a[:, _extra_lane(h):_extra_lane(h) + 1]

    own = pl.BlockSpec((tq, ATT_W), lambda i, j: (i, 0))
    kv = pl.BlockSpec((tq, ATT_W), lambda i, j: (jnp.minimum(i, j), 0))
    return _hosted_call(
        host, body, name=name, grid=(n, n), in_specs=[own, kv, kv, own],
        out_specs=[pl.BlockSpec((tq, FOX_W), lambda i, j: (i, 0)), pl.BlockSpec((tq, LANES), lambda i, j: (i, 0))],
        out_shape=[jax.ShapeDtypeStruct((t, FOX_W), bf16), jax.ShapeDtypeStruct((t, LANES), f32)],
        scratch_shapes=[pltpu.VMEM((HEADS, tq, LANES), f32)],
        compiler_params=_params(("arbitrary", "arbitrary")),
    )(qb_aug, k_aug, v_aug, do_aug)


def _attn_dkv2(qb_aug, k_aug, v_aug, do_aug, *, name, host=None):
    t = qb_aug.shape[0]
    n = t // ATT_TILE
    tk = ATT_TILE

    def body(q_ref, k_ref, v_ref, do_ref, dk_ref, dv_ref, dc_ref, dk_acc, dv_acc):
        j = pl.program_id(0)
        i = pl.program_id(1)

        @pl.when(i == 0)
        def _():
            dk_acc[...] = jnp.zeros_like(dk_acc)
            dv_acc[...] = jnp.zeros_like(dv_acc)

        def block(masked):
            mask = _causal(i, j, True) if masked else None
            for h in range(HEADS):
                hs = slice(LANES * h, LANES * (h + 1))
                qh = q_ref[:, hs]
                doh = do_ref[:, hs]
                s_t = lax.dot_general(k_ref[:, hs], qh, _NT, preferred_element_type=f32)
                if masked:
                    s_t = jnp.where(mask, s_t, NEG_BIG)
                p_t = jnp.exp(s_t)
                dv_acc[h] += jnp.dot(p_t.astype(bf16), doh, preferred_element_type=f32)
                dp_t = lax.dot_general(v_ref[:, hs], doh, _NT, preferred_element_type=f32)
                dk_acc[h] += jnp.dot((p_t * dp_t).astype(bf16), qh, preferred_element_type=f32)

        @pl.when(i > j)
        def _():
            block(False)

        @pl.when(i == j)
        def _():
            block(True)

        @pl.when(i == n - 1)
        def _():
            dc_ref[...] = jnp.zeros_like(dc_ref)
            for h in range(HEADS):
                a = dk_acc[h]
                cols = slice(_data_lane(h), _data_lane(h) + HEAD_D)
                dk_ref[:, HEAD_D * h:HEAD_D * (h + 1)] = a[:, cols].astype(bf16)
                dv_ref[:, HEAD_D * h:HEAD_D * (h + 1)] = dv_acc[h][:, cols].astype(bf16)
                dc_ref[:, h:h + 1] = -a[:, _extra_lane(h) + 3:_extra_lane(h) + 4]

    own = pl.BlockSpec((tk, ATT_W), lambda j, i: (j, 0))
    qs = pl.BlockSpec((tk, ATT_W), lambda j, i: (jnp.maximum(i, j), 0))
    half = pl.BlockSpec((tk, FOX_W), lambda j, i: (j, 0))
    return _hosted_call(
        host, body, name=name, grid=(n, n), in_specs=[qs, own, own, qs],
        out_specs=[half, half, pl.BlockSpec((tk, LANES), lambda j, i: (j, 0))],
        out_shape=[jax.ShapeDtypeStruct((t, FOX_W), bf16), jax.ShapeDtypeStruct((t, FOX_W), bf16),
                   jax.ShapeDtypeStruct((t, LANES), f32)],
        scratch_shapes=[pltpu.VMEM((HEADS, tk, LANES), f32), pltpu.VMEM((HEADS, tk, LANES), f32)],
        compiler_params=_params(("arbitrary", "arbitrary")),
    )(qb_aug, k_aug, v_aug, do_aug)


LRU_CHUNK = 64
LRU_G = 256
SUB = 8


def _row_ids(n):
    return lax.broadcasted_iota(jnp.int32, (n, LRU_G), 0)


def _shift_rows_down(ext, s):
    return pltpu.roll(ext, s, axis=0)[SUB:, :]


def _shift_rows_up(ext, s, n):
    return pltpu.roll(ext, ext.shape[0] - s, axis=0)[:n, :]


def _lru_gates(u, wa_ref, ba_ref, wx_ref, bx_ref, sp):
    ub = u.astype(bf16)
    r = _sigmoid(jnp.dot(ub, wa_ref[...], preferred_element_type=f32) + ba_ref[...])
    gi = _sigmoid(jnp.dot(ub, wx_ref[...], preferred_element_type=f32) + bx_ref[...])
    log_a = -LRU_C * r * sp
    a = jnp.exp(log_a)
    s = jnp.sqrt(_one_minus_exp(2.0 * log_a))
    return r, gi, a, s


def _conv_window(lx_ref, r0, ci):
    cur = lx_ref[pl.ds(r0, LRU_CHUNK), :]
    p0 = pl.multiple_of(jnp.maximum(r0 - SUB, 0), SUB)
    prev = jnp.where(ci > 0, lx_ref[pl.ds(p0, SUB), :], 0.0)
    return cur, jnp.concatenate([prev, cur], axis=0)


def _lru_fwd(zl, conv_w, conv_b, wa, ba, wx, bx, lam, *, name, host=None):
    t = zl.shape[0]
    n_chunk = t // LRU_CHUNK

    def body(lx_ref, lg_ref, cw_ref, cb_ref, wa_ref, ba_ref, wx_ref, bx_ref, lam_ref, u_ref, h_ref, y_ref):
        sp = _softplus(-lam_ref[...])
        rows = _row_ids(SUB)

        def chunk(ci, hc):
            r0 = pl.multiple_of(ci * LRU_CHUNK, LRU_CHUNK)
            cur, ext = _conv_window(lx_ref, r0, ci)
            u = cb_ref[...] + cw_ref[3:4, :] * cur
            for k in range(3):
                u = u + cw_ref[k:k + 1, :] * _shift_rows_down(ext, 3 - k)
            r, gi, a, s = _lru_gates(u, wa_ref, ba_ref, wx_ref, bx_ref, sp)
            b = s * (gi * u)
            tiles = []
            for q in range(LRU_CHUNK // SUB):
                ta = a[SUB * q:SUB * (q + 1), :]
                tb = b[SUB * q:SUB * (q + 1), :]
                for d in (1, 2, 4):
                    a_sh = jnp.where(rows >= d, pltpu.roll(ta, d, axis=0), 1.0)
                    b_sh = jnp.where(rows >= d, pltpu.roll(tb, d, axis=0), 0.0)
                    tb = ta * b_sh + tb
                    ta = ta * a_sh
                hq = tb + ta * hc
                hc = hq[SUB - 1:SUB, :]
                tiles.append(hq)
            h = jnp.concatenate(tiles, axis=0)
            u_ref[pl.ds(r0, LRU_CHUNK), :] = u
            h_ref[pl.ds(r0, LRU_CHUNK), :] = h
            gel, _ = _gelu_and_grad(lg_ref[pl.ds(r0, LRU_CHUNK), :])
            y_ref[pl.ds(r0, LRU_CHUNK), :] = gel * h
            return hc

        lax.fori_loop(0, n_chunk, chunk, jnp.zeros((1, LRU_G), f32))

    seq = lambda cb: pl.BlockSpec((t, LRU_G), lambda c, cb=cb: (0, c + cb))
    rowc = pl.BlockSpec((1, LRU_G), lambda c: (0, c))
    diag = pl.BlockSpec((LRU_G, LRU_G), lambda c: (c, c))
    out = jax.ShapeDtypeStruct((t, LRU_W), f32)
    return _hosted_call(
        host, body, name=name, grid=(LRU_W // LRU_G,),
        in_specs=[seq(0), seq(LRU_W // LRU_G), pl.BlockSpec((4, LRU_G), lambda c: (0, c)),
                  rowc, diag, rowc, diag, rowc, rowc],
        out_specs=[seq(0)] * 3, out_shape=[out] * 3,
        compiler_params=_params(("arbitrary",)),
    )(zl, zl, conv_w, conv_b, wa, ba, wx, bx, lam)


def _lru_bwd(dmix, zl, u_all, h_all, conv_w, wa, ba, wx, bx, lam, *, name, host=None):
    t = zl.shape[0]
    n_chunk = t // LRU_CHUNK

    def body(dy_ref, lx_ref, lg_ref, u_ref, h_ref, cw_ref, wa_ref, ba_ref, wx_ref, bx_ref, lam_ref,
             dlx_ref, dlg_ref, dcw_ref, dcb_ref, dba_ref, dbx_ref, dlam_ref, dwa_ref, dwx_ref, dpr_s, dpx_s):
        lam_v = lam_ref[...]
        sp = _softplus(-lam_v)
        rows = _row_ids(SUB)
        rows_c = _row_ids(LRU_CHUNK)
        zero_row = jnp.zeros((1, LRU_G), f32)

        def chunk(step, carry):
            dh_c, a_next0, du_next, dsp, dba, dbx, dcb, dw0, dw1, dw2, dw3 = carry
            ci = n_chunk - 1 - step
            r0 = pl.multiple_of(ci * LRU_CHUNK, LRU_CHUNK)
            sl = pl.ds(r0, LRU_CHUNK)
            u = u_ref[sl, :]
            r, gi, a, s = _lru_gates(u, wa_ref, ba_ref, wx_ref, bx_ref, sp)
            h = h_ref[sl, :]
            p0 = pl.multiple_of(jnp.maximum(r0 - SUB, 0), SUB)
            h_before = jnp.where(ci > 0, h_ref[pl.ds(p0, SUB), :], 0.0)[SUB - 1:SUB, :]
            h_prev = jnp.where(rows_c == 0, h_before, pltpu.roll(h, 1, axis=0))
            gel, dgel = _gelu_and_grad(lg_ref[sl, :])
            dy = dy_ref[sl, :]
            dlg_ref[sl, :] = (dy * h * dgel).astype(bf16)
            g_in = dy * gel
            a_next = jnp.where(rows_c == LRU_CHUNK - 1, a_next0, pltpu.roll(a, LRU_CHUNK - 1, axis=0))
            tiles = [None] * (LRU_CHUNK // SUB)
            for q in reversed(range(LRU_CHUNK // SUB)):
                ta = a_next[SUB * q:SUB * (q + 1), :]
                tb = g_in[SUB * q:SUB * (q + 1), :]
                for d in (1, 2, 4):
                    a_sh = jnp.where(rows < SUB - d, pltpu.roll(ta, SUB - d, axis=0), 1.0)
                    b_sh = jnp.where(rows < SUB - d, pltpu.roll(tb, SUB - d, axis=0), 0.0)
                    tb = ta * b_sh + tb
                    ta = ta * a_sh
                dhq = tb + ta * dh_c
                dh_c = dhq[0:1, :]
                tiles[q] = dhq
            dh = jnp.concatenate(tiles, axis=0)
            da = dh * h_prev
            ds = dh * gi * u
            dgi = dh * s * u
            du = dh * s * gi
            dlog_a = da * a - ds * (a * a) / s
            dr = dlog_a * (-LRU_C * sp)
            dsp = dsp + jnp.sum(dlog_a * (-LRU_C * r), axis=0, keepdims=True)
            dpr = dr * r * (1.0 - r)
            dpx = dgi * gi * (1.0 - gi)
            dprb = dpr.astype(bf16)
            dpxb = dpx.astype(bf16)
            dpr_s[sl, :] = dprb
            dpx_s[sl, :] = dpxb
            du = du + (lax.dot_general(dprb, wa_ref[...], _NT, preferred_element_type=f32)
                       + lax.dot_general(dpxb, wx_ref[...], _NT, preferred_element_type=f32))
            dba = dba + jnp.sum(dpr, axis=0, keepdims=True)
            dbx = dbx + jnp.sum(dpx, axis=0, keepdims=True)
            dcb = dcb + jnp.sum(du, axis=0, keepdims=True)
            du_ext = jnp.concatenate([du, du_next], axis=0)
            dlx = cw_ref[3:4, :] * du
            for k in range(3):
                dlx = dlx + cw_ref[k:k + 1, :] * _shift_rows_up(du_ext, 3 - k, LRU_CHUNK)
            dlx_ref[sl, :] = dlx.astype(bf16)
            cur, ext = _conv_window(lx_ref, r0, ci)
            dws = [dw0, dw1, dw2, dw3 + jnp.sum(du * cur, axis=0, keepdims=True)]
            for k in range(3):
                dws[k] = dws[k] + jnp.sum(du * _shift_rows_down(ext, 3 - k), axis=0, keepdims=True)
            return (dh_c, a[0:1, :], du[0:SUB, :], dsp, dba, dbx, dcb, dws[0], dws[1], dws[2], dws[3])

        init = (zero_row, zero_row, jnp.zeros((SUB, LRU_G), f32)) + (zero_row,) * 8
        out = lax.fori_loop(0, n_chunk, chunk, init)
        _, _, _, dsp, dba, dbx, dcb, dw0, dw1, dw2, dw3 = out
        dlam_ref[...] = dsp * (-_sigmoid(-lam_v))
        dba_ref[...] = dba
        dbx_ref[...] = dbx
        dcb_ref[...] = dcb
        dcw_ref[...] = jnp.concatenate([dw0, dw1, dw2, dw3], axis=0)
        ub = u_ref[...].astype(bf16)
        dwa_ref[...] = lax.dot_general(ub, dpr_s[...], _TN, preferred_element_type=f32)
        dwx_ref[...] = lax.dot_general(ub, dpx_s[...], _TN, preferred_element_type=f32)

    seq = lambda cb: pl.BlockSpec((t, LRU_G), lambda c, cb=cb: (0, c + cb))
    rowc = pl.BlockSpec((1, LRU_G), lambda c: (0, c))
    diag = pl.BlockSpec((LRU_G, LRU_G), lambda c: (c, c))
    gate_out = pl.BlockSpec((None, LRU_G, LRU_G), lambda c: (c, 0, 0))
    row_shape = jax.ShapeDtypeStruct((1, LRU_W), f32)
    return _hosted_call(
        host, body, name=name, grid=(LRU_W // LRU_G,),
        in_specs=[seq(LRU_W // LRU_G), seq(0), seq(LRU_W // LRU_G), seq(0), seq(0),
                  pl.BlockSpec((4, LRU_G), lambda c: (0, c)),
                  diag, rowc, diag, rowc, rowc],
        out_specs=[seq(0), seq(0), pl.BlockSpec((4, LRU_G), lambda c: (0, c)), rowc, rowc, rowc, rowc,
                   gate_out, gate_out],
        out_shape=[jax.ShapeDtypeStruct((t, LRU_W), bf16)] * 2
        + [jax.ShapeDtypeStruct((4, LRU_W), f32)] + [row_shape] * 4
        + [jax.ShapeDtypeStruct((LRU_W // LRU_G, LRU_G, LRU_G), f32)] * 2,
        scratch_shapes=[pltpu.VMEM((t, LRU_G), bf16), pltpu.VMEM((t, LRU_G), bf16)],
        compiler_params=_params(("arbitrary",)),
    )(dmix, zl, zl, u_all, h_all, conv_w, wa, ba, wx, bx, lam)


def _block_diag(w):
    eye = jnp.eye(HEADS, dtype=w.dtype)
    return jnp.einsum("hij,hk->hikj", w, eye).reshape(LRU_W, LRU_W)


def _diag_blocks(dw):
    per = dw.shape[1] // HEAD_D
    blocks = [dw[:, HEAD_D * b:HEAD_D * (b + 1), HEAD_D * b:HEAD_D * (b + 1)] for b in range(per)]
    return jnp.stack(blocks, axis=1).reshape(HEADS, HEAD_D, HEAD_D)


def _local_step(x, target, sent, small, *, tm=512, tm_ffn=1024):
    t = x.shape[0]
    ones = jnp.ones((1, D_MODEL), f32)
    zeros = jnp.zeros((1, D_MODEL), f32)
    ln1 = (small["ln1_g"], small["ln1_b"])
    ln2 = (small["ln2_g"], small["ln2_b"])
    ln3 = (small["ln3_g"], small["ln3_b"])

    xh1, rs1, hg1, hu1, wg1, wu1, wd1, w_in_g, w_out_g, conv_w_g = _ffn1_fwd_gathering(
        x, (sent["ffn1_w_gate"], sent["ffn1_w_up"], sent["ffn1_w_down"]),
        _Exchange([sent["w_in"], sent["w_out"], sent["conv_w"]], gather=True), tm=tm_ffn, name="ffn1_fwd")
    w_in = jnp.pad(w_in_g.transpose(1, 0, 2).reshape(D_MODEL, IN_COLS), ((0, 0), (0, 21 * LANES - IN_COLS)))
    w_out = w_out_g.reshape(D_MODEL, D_MODEL)
    conv_w = conv_w_g.transpose(1, 0, 2).reshape(4, LRU_W)
    qkv, zl, zfg = _in_proj(xh1, ln1[0], ln1[1], w_in, tm=tm, name="in_proj")
    bfg = jnp.pad(small["b_forget"], ((0, 0), (0, LANES - HEADS)))
    cum = _cum_fwd(zfg, bfg, name="cum_fwd")
    q_aug, k_aug, v_aug = _attn_prep_fwd(qkv, cum, tm=tm, name="attn_prep_fwd")
    o, lse, wg2, wu2 = _attn_fwd2(q_aug, k_aug, v_aug, name="attn_fwd",
                                  host=_Exchange([sent["ffn2_w_gate"], sent["ffn2_w_up"]], gather=True))
    wa_bd = _block_diag(small["rg_wa"]).astype(bf16)
    wx_bd = _block_diag(small["rg_wx"]).astype(bf16)
    ba = small["rg_ba"].reshape(1, LRU_W)
    bx = small["rg_bx"].reshape(1, LRU_W)
    u, h, lru, wd2 = _lru_fwd(zl, conv_w, small["conv_b"], wa_bd, ba, wx_bd, bx, small["lru_lambda"],
                              name="lru_fwd", host=_Exchange([sent["ffn2_w_down"]], gather=True))
    xh2, rs2 = _mmln([(o, 0, FOX_W, w_out, 0, D_MODEL, "nn"), (lru, 0, LRU_W, w_out, 1, D_MODEL, "nn")],
                     tm=tm, name="mix_fwd", resid=("affine", xh1) + ln1, resid_scale=ALPHA, epi="ln_fwd")
    xh3, rs3, hg2, hu2 = _ffn_fwd(xh2, ln2[0], ln2[1], wg2, wu2, wd2, tm=tm_ffn, name="ffn2_fwd")

    dpre3, sq_rows, g_ln3g, g_ln3b = _loss_bwd(xh3, rs3, ln3[0], ln3[1], target, tm=tm, name="loss_bwd")
    dpre2, g_ln2g, g_ln2b, dhg2, dhu2, a2 = _ffn_bwd(dpre3, hg2, hu2, wg2, wu2, wd2,
                                                     (xh2, rs2, ln2[0]), tm=tm_ffn, name="ffn2_bwd")
    wgrad = dict(out_dtype=bf16, tm=D_MODEL, mb=1, tn=FF_TILE, nb=4, tk=512, pair=True)
    wdgrad = dict(out_dtype=bf16, tm=512, mb=4, tn=D_MODEL, nb=1, tk=512, out_scale=0.5, pair=True)
    between_chips = functools.partial(_Exchange, gather=False, chips=True)
    g_wg2 = _mm_tn(xh2, dhg2, name="g_wg2", affine=ln2, **wgrad)
    g_wu2 = _mm_tn(xh2, dhu2, name="g_wu2", affine=ln2, **wgrad)
    g_wd2 = _mm_tn(a2, dpre3, name="g_wd2", **wdgrad)

    dmix = _mmln([(dpre2, 0, D_MODEL, w_out, 0, D_MODEL, "nt")], tm=tm, name="dmix_bwd")
    g_wout_a = _mm(o, dpre2, mode="tn", out_dtype=bf16, tm=512, tn=D_MODEL, tk=512, name="g_wout_fox")
    g_wout_b = _mm(lru, dpre2, mode="tn", out_dtype=bf16, tm=512, tn=D_MODEL, tk=512, name="g_wout_lru")
    dlx, dlg, g_cw, g_cb, g_ba, g_bx, g_lam, g_wa4, g_wx4, *p_wg2 = _lru_bwd(
        dmix, zl, u, h, conv_w, wa_bd, ba, wx_bd, bx, small["lru_lambda"], name="lru_bwd",
        host=between_chips([g_wg2]))
    p_wg2 = p_wg2[0]
    qb_aug, do_aug = _attn_prep_bwd(qkv, cum, lse, dmix, o, tm=tm, name="attn_prep_bwd")
    dq, dcum_q, dk, dv, dcum_k, p_wu2, p_wd2 = _attn_bwd(qb_aug, k_aug, v_aug, do_aug, name="attn_bwd",
                                                         host=between_chips([g_wu2, g_wd2]))
    g_wout_blocked = jnp.concatenate([g_wout_a, g_wout_b], axis=0).reshape(N_DEV, D_MODEL // N_DEV, D_MODEL)
    dfg, g_bf = _cum_bwd(dcum_q, dcum_k, zfg, bfg, name="cum_bwd")

    dz = [(dq, 0, 512), (dk, 1, 512), (dv, 2, 512), (dlx, 3, 512), (dlg, 4, 512), (dfg, 20, LANES)]
    dpre1, g_ln1g, g_ln1b = _mmln(
        [(arr, 0, w, w_in, cb, w, "nt") for (arr, cb, w) in dz],
        tm=tm, name="dx1_bwd", resid=("plain", dpre2), resid_scale=ALPHA, epi="ln_bwd", ln=(xh1, rs1, ln1[0]))
    g_win_main = _mm_tn(xh1, [arr for arr, _, _ in dz[:5]], out_dtype=bf16, tm=D_MODEL, mb=1, tn=512, nb=5, tk=512,
                        name="g_win", affine=ln1, out_blocked=True)
    g_win = [g_win_main[n] for n in range(5)] + [
        _mm(xh1, dfg, mode="tn", out_dtype=bf16, tm=D_MODEL, tn=LANES, tk=512, name="g_win_fg", affine=ln1)]
    g_win_full = jnp.concatenate([g[:, :w] for g, (_, _, w) in zip(g_win, dz)], axis=1)[:, :IN_COLS]
    g_win_blocked = g_win_full.reshape(D_MODEL, N_DEV, IN_SHARD).transpose(1, 0, 2)
    dhg1, dhu1, a1, p_win, p_wout = _ffn_bwd_act(dpre1, hg1, hu1, wd1, tm=tm_ffn, name="ffn1_bwd_act",
                                                 host=_Exchange([g_win_blocked, g_wout_blocked], gather=False))
    small_g = {
        "ln1_g": g_ln1g, "ln1_b": g_ln1b, "b_forget": g_bf[:, :HEADS], "conv_w": g_cw, "conv_b": g_cb,
        "rg_wa": _diag_blocks(g_wa4), "rg_ba": g_ba.reshape(HEADS, HEAD_D),
        "rg_wx": _diag_blocks(g_wx4), "rg_bx": g_bx.reshape(HEADS, HEAD_D), "lru_lambda": g_lam,
        "ln2_g": g_ln2g, "ln2_b": g_ln2b, "ln3_g": g_ln3g, "ln3_b": g_ln3b,
    }
    small_g["loss"] = (0.5 / D_MODEL) * jnp.sum(sq_rows, keepdims=True)
    pieces = [small_g[n].reshape(-1) for n in PACKED]
    packed = jnp.concatenate(pieces + [jnp.zeros((PACK_ROWS * LANES - sum(p.shape[0] for p in pieces),), f32)])
    g_wg1, all_packed = _mm_tn(x, dhg1, name="g_wg1",
                               host=_Exchange([packed.reshape(PACK_ROWS, LANES)], gather=True), **wgrad)
    g_wu1, p_wg1 = _mm_tn(x, dhu1, name="g_wu1", host=between_chips([g_wg1]), **wgrad)
    g_wd1, p_wu1 = _mm_tn(a1, dpre1, name="g_wd1", host=between_chips([g_wu1]), **wdgrad)
    grad_x, p_wd1 = _ffn_bwd_dx(dpre1, dhg1, dhu1, wg1, wu1, tm=tm_ffn, name="ffn1_bwd_dx",
                                host=between_chips([g_wd1]))
    parts = {
        "ffn1_w_gate": p_wg1, "ffn1_w_up": p_wu1, "ffn1_w_down": p_wd1, "w_in": p_win, "w_out": p_wout,
        "ffn2_w_gate": p_wg2, "ffn2_w_up": p_wu2, "ffn2_w_down": p_wd2,
    }
    return sq_rows, grad_x, parts, all_packed, {n: small_g[n].shape for n in PACKED}


def _adam_math(w, g, m, v):
    m2 = ADAM_B1 * m + (1.0 - ADAM_B1) * g
    v2 = ADAM_B2 * v + (1.0 - ADAM_B2) * (g * g)
    m_hat = m2 / (1.0 - ADAM_B1 ** ADAM_STEP)
    v_hat = v2 / (1.0 - ADAM_B2 ** ADAM_STEP)
    delta = -ADAM_LR * (m_hat / (jnp.sqrt(v_hat) + ADAM_EPS) + ADAM_WD * w)
    return delta, m2, v2


ADAM_TILE_ELEMS = 128 * 1024


def _adamw_big(parts, w, m, v, *, name):
    r, c = w.shape
    n_parts = parts.shape[0]
    tr = max(d for d in range(8, r + 1, 8) if r % d == 0 and d * c <= ADAM_TILE_ELEMS)

    def body(p_ref, w_ref, m_ref, v_ref, g_ref, d_ref, m2_ref, v2_ref):
        g = p_ref[0].astype(f32)
        for q in range(1, n_parts):
            g = g + p_ref[q].astype(f32)
        d, m2, v2 = _adam_math(w_ref[...], g, m_ref[...], v_ref[...])
        g_ref[...] = g
        d_ref[...] = d
        m2_ref[...] = m2
        v2_ref[...] = v2

    blk = pl.BlockSpec((tr, c), lambda i: (i, 0))
    return pl.pallas_call(
        body, name=name, grid=(r // tr,),
        in_specs=[pl.BlockSpec((n_parts, tr, c), lambda i: (0, i, 0)), blk, blk, blk],
        out_specs=[blk] * 4, out_shape=[jax.ShapeDtypeStruct((r, c), f32)] * 4,
        compiler_params=_params(("arbitrary",)),
    )(parts, w, m, v)


def _adamw_small(items, *, name):
    n = len(items)

    def body(*refs):
        ins, outs = refs[:4 * n], refs[4 * n:]
        for k in range(n):
            g, w, m, v = (ins[4 * k + q][...] for q in range(4))
            d, m2, v2 = _adam_math(w, g, m, v)
            outs[3 * k][...] = d
            outs[3 * k + 1][...] = m2
            outs[3 * k + 2][...] = v2

    vm = pl.BlockSpec(memory_space=pltpu.VMEM)
    flat = [a for item in items for a in item]
    out_shape = [jax.ShapeDtypeStruct(item[1].shape, f32) for item in items for _ in range(3)]
    return pl.pallas_call(
        body, name=name, in_specs=[vm] * (4 * n), out_specs=[vm] * (3 * n), out_shape=out_shape,
    )(*flat)


def _sum_parts(parts, *, name):
    def body(p_ref, o_ref):
        acc = p_ref[0]
        for q in range(1, N_DEV):
            acc = acc + p_ref[q]
        o_ref[...] = acc

    vm = pl.BlockSpec(memory_space=pltpu.VMEM)
    return pl.pallas_call(
        body, name=name, in_specs=[vm], out_specs=vm, out_shape=jax.ShapeDtypeStruct(parts.shape[1:], f32),
    )(parts)


WEIGHTS = ["ffn1_w_gate", "ffn1_w_up", "ffn1_w_down", "ln1_g", "ln1_b", "w_in", "b_forget", "conv_w", "conv_b",
           "rg_wa", "rg_ba", "rg_wx", "rg_bx", "lru_lambda", "w_out", "ln2_g", "ln2_b",
           "ffn2_w_gate", "ffn2_w_up", "ffn2_w_down", "ln3_g", "ln3_b"]
BIG = ["ffn1_w_gate", "ffn1_w_up", "ffn1_w_down", "w_in", "w_out", "ffn2_w_gate", "ffn2_w_up", "ffn2_w_down"]
PACKED = ["ln1_g", "ln1_b", "ln2_g", "ln2_b", "ln3_g", "ln3_b", "conv_b", "rg_ba", "rg_bx", "lru_lambda",
          "conv_w", "rg_wa", "rg_wx", "b_forget", "loss"]
PACK_ROWS = 600


def _two_d(a):
    return a.reshape((-1, a.shape[-1]))


def _transport(a):
    return _two_d(a)


def kernel(x, ffn1_w_gate, ffn1_w_up, ffn1_w_down, ln1_g, ln1_b, w_in, b_forget, conv_w, conv_b, rg_wa, rg_ba, rg_wx, rg_bx, lru_lambda, w_out, ln2_g, ln2_b, ffn2_w_gate, ffn2_w_up, ffn2_w_down, ln3_g, ln3_b, loss_target, m_ffn1_w_gate, m_ffn1_w_up, m_ffn1_w_down, m_ln1_g, m_ln1_b, m_w_in, m_b_forget, m_conv_w, m_conv_b, m_rg_wa, m_rg_ba, m_rg_wx, m_rg_bx, m_lru_lambda, m_w_out, m_ln2_g, m_ln2_b, m_ffn2_w_gate, m_ffn2_w_up, m_ffn2_w_down, m_ln3_g, m_ln3_b, v_ffn1_w_gate, v_ffn1_w_up, v_ffn1_w_down, v_ln1_g, v_ln1_b, v_w_in, v_b_forget, v_conv_w, v_conv_b, v_rg_wa, v_rg_ba, v_rg_wx, v_rg_bx, v_lru_lambda, v_w_out, v_ln2_g, v_ln2_b, v_ffn2_w_gate, v_ffn2_w_up, v_ffn2_w_down, v_ln3_g, v_ln3_b):
    w_args = (ffn1_w_gate, ffn1_w_up, ffn1_w_down, ln1_g, ln1_b, w_in, b_forget, conv_w, conv_b, rg_wa, rg_ba, rg_wx, rg_bx, lru_lambda, w_out, ln2_g, ln2_b, ffn2_w_gate, ffn2_w_up, ffn2_w_down, ln3_g, ln3_b)
    m_args = (m_ffn1_w_gate, m_ffn1_w_up, m_ffn1_w_down, m_ln1_g, m_ln1_b, m_w_in, m_b_forget, m_conv_w, m_conv_b, m_rg_wa, m_rg_ba, m_rg_wx, m_rg_bx, m_lru_lambda, m_w_out, m_ln2_g, m_ln2_b, m_ffn2_w_gate, m_ffn2_w_up, m_ffn2_w_down, m_ln3_g, m_ln3_b)
    v_args = (v_ffn1_w_gate, v_ffn1_w_up, v_ffn1_w_down, v_ln1_g, v_ln1_b, v_w_in, v_b_forget, v_conv_w, v_conv_b, v_rg_wa, v_rg_ba, v_rg_wx, v_rg_bx, v_lru_lambda, v_w_out, v_ln2_g, v_ln2_b, v_ffn2_w_gate, v_ffn2_w_up, v_ffn2_w_down, v_ln3_g, v_ln3_b)
    w = dict(zip(WEIGHTS, w_args))
    m = dict(zip(WEIGHTS, m_args))
    v = dict(zip(WEIGHTS, v_args))
    me = 4 * lax.axis_index("x") + 2 * lax.axis_index("y") + lax.axis_index("c")

    sent = {n: _transport(w[n]).astype(bf16) for n in BIG}
    sent["conv_w"] = _two_d(w["conv_w"])
    small = {n: w[n] for n in ("ln1_g", "ln1_b", "ln2_g", "ln2_b", "ln3_g", "ln3_b", "b_forget", "conv_b",
                               "lru_lambda")}
    small.update({n: w[n][0] for n in ("rg_wa", "rg_ba", "rg_wx", "rg_bx")})

    sq_rows, grad_x, parts, all_packed, small_shapes = _local_step(x[0], loss_target[0], sent, small)

    total = _sum_parts(all_packed, name="sum_small_grads").reshape(-1)
    grads, off = {}, 0
    for n in PACKED:
        size = math.prod(small_shapes[n])
        grads[n] = total[off:off + size].reshape(small_shapes[n])
        off += size
    loss = grads.pop("loss").reshape(())
    grads["conv_w"] = lax.dynamic_slice_in_dim(grads["conv_w"], me * (LRU_W // N_DEV), LRU_W // N_DEV, axis=1)

    delta, new_m, new_v = {}, {}, {}
    for n in BIG:
        g, d, m2, v2 = _adamw_big(parts[n], _transport(w[n]), _transport(m[n]), _transport(v[n]),
                                  name="adamw_" + n)
        grads[n], delta[n], new_m[n], new_v[n] = g, d, m2, v2
    small_names = [n for n in WEIGHTS if n not in BIG]
    outs = _adamw_small([(_two_d(grads[n]), _two_d(w[n]), _two_d(m[n]), _two_d(v[n])) for n in small_names],
                        name="adamw_small")
    for k, n in enumerate(small_names):
        delta[n], new_m[n], new_v[n] = outs[3 * k], outs[3 * k + 1], outs[3 * k + 2]

    def shaped(d):
        return [d[n].reshape(w[n].shape) for n in WEIGHTS]

    return (loss, grad_x[None], *shaped(grads), *shaped(delta), *shaped(new_m), *shaped(new_v))
```

```python
import functools
import math

import jax
import jax.numpy as jnp
from jax import lax
from jax.experimental import pallas as pl
from jax.experimental.pallas import tpu as pltpu

f32 = jnp.float32
bf16 = jnp.bfloat16

N_DEV = 8
D_MODEL = 1024
D_FF = 4096
FF_TILE = D_FF // N_DEV
FOX_W = 512
LRU_W = 512
HEADS = 8
HEAD_D = 64
IN_COLS = 2568
IN_SHARD = IN_COLS // N_DEV
LANES = 128
LN_EPS = 1e-5
ALPHA = 2.0 ** 0.25
ATT_SCALE = 1.0 / math.sqrt(HEAD_D)
LRU_C = 8.0
NEG_BIG = -1e30

ADAM_LR = 0.001
ADAM_B1 = 0.9
ADAM_B2 = 0.999
ADAM_EPS = 1e-08
ADAM_WD = 0.01
ADAM_STEP = 10

VMEM_LIMIT = 56 * 1024 * 1024
MESH_T = pl.DeviceIdType.MESH


def _params(sem, **kw):
    return pltpu.CompilerParams(dimension_semantics=sem, vmem_limit_bytes=VMEM_LIMIT, **kw)


def _sigmoid(x):
    return 1.0 / (1.0 + jnp.exp(-x))


def _sigmoid_tanh(x):
    return 0.5 * jnp.tanh(0.5 * x) + 0.5


def _softplus(x):
    return jnp.maximum(x, 0.0) + jnp.log(1.0 + jnp.exp(-jnp.abs(x)))


def _one_minus_exp(x):
    series = -x * (1.0 + x * (0.5 + x * (1.0 / 6 + x * (1.0 / 24 + x * (1.0 / 120 + x * (1.0 / 720))))))
    return jnp.where(x > -0.125, series, 1.0 - jnp.exp(x))


_GELU_C = math.sqrt(2.0 / math.pi)


def _gelu_and_grad(x):
    inner = _GELU_C * (x + 0.044715 * x * x * x)
    t = jnp.tanh(inner)
    g = 0.5 * x * (1.0 + t)
    dg = 0.5 * (1.0 + t) + 0.5 * x * (1.0 - t * t) * _GELU_C * (1.0 + 3 * 0.044715 * x * x)
    return g, dg


def _ln_fwd_tile(pre):
    mu = jnp.mean(pre, axis=-1, keepdims=True)
    xc = pre - mu
    var = jnp.mean(xc * xc, axis=-1, keepdims=True)
    rstd = lax.rsqrt(var + LN_EPS)
    return xc * rstd, rstd


def _ln_bwd_tile(dy, xhat, rstd, g):
    dyg = dy * g
    m1 = jnp.mean(dyg, axis=-1, keepdims=True)
    m2 = jnp.mean(dyg * xhat, axis=-1, keepdims=True)
    dpre = rstd * (dyg - m1 - xhat * m2)
    return dpre, jnp.sum(dy * xhat, axis=0, keepdims=True), jnp.sum(dy, axis=0, keepdims=True)


_NT = (((1,), (1,)), ((), ()))
_TN = (((0,), (0,)), ((), ()))


class _Exchange:
    def __init__(self, arrs, gather, chips=False):
        self.arrs, self.gather, self.n, self.chips = list(arrs), gather, len(arrs), chips

    def out_shape(self):
        return [jax.ShapeDtypeStruct(((N_DEV,) + a.shape) if self.gather else a.shape, a.dtype) for a in self.arrs]

    def scratch(self):
        n_remote = self.n * (N_DEV - 1)
        return [pltpu.SemaphoreType.DMA((n_remote,)), pltpu.SemaphoreType.DMA((n_remote,)),
                pltpu.SemaphoreType.DMA((self.n,))]

    def copies(self, ins, outs, sems):
        send_sems, recv_sems, local_sems = sems
        x, y, c = lax.axis_index("x"), lax.axis_index("y"), lax.axis_index("c")
        me = 2 * x + y if self.chips else 4 * x + 2 * y + c
        out = []
        for k in range(self.n):
            for d in (range(2, N_DEV, 2) if self.chips else range(1, N_DEV)):
                px = 1 - x if d & 4 else x
                py = 1 - y if d & 2 else y
                pc = 1 - c if d & 1 else c
                sem = k * (N_DEV - 1) + d - 1
                out.append(pltpu.make_async_remote_copy(
                    src_ref=ins[k].at[2 * px + py if self.chips else 4 * px + 2 * py + pc], dst_ref=outs[k].at[me],
                    send_sem=send_sems.at[sem], recv_sem=recv_sems.at[sem],
                    device_id=(px, py, pc), device_id_type=MESH_T))
            out.append(pltpu.make_async_copy(ins[k].at[me], outs[k].at[me], local_sems.at[k]))
        return out

    def gather_copies(self, ins, outs, sems):
        send_sems, recv_sems, local_sems = sems
        x, y, c = lax.axis_index("x"), lax.axis_index("y"), lax.axis_index("c")
        sibling = (x, y, 1 - c)
        chips = [(1 - x, y), (x, 1 - y), (1 - x, 1 - y)]
        out = []
        for k in range(self.n):
            def copy(s, block, to, src=None, k=k):
                rows = outs[k].at[4 * block[0] + 2 * block[1] + block[2]]
                sem = k * (N_DEV - 1) + s
                return pltpu.make_async_remote_copy(
                    src_ref=rows if src is None else src, dst_ref=rows, send_sem=send_sems.at[sem],
                    recv_sem=recv_sems.at[sem], device_id=to, device_id_type=MESH_T)

            first = [copy(0, (x, y, c), sibling, src=ins[k])]
            first += [copy(1 + q, (x, y, c), (*chip, c), src=ins[k]) for q, chip in enumerate(chips)]
            passed = [copy(4 + q, (*chip, c), sibling) for q, chip in enumerate(chips)]
            own = pltpu.make_async_copy(ins[k], outs[k].at[4 * x + 2 * y + c], local_sems.at[k])
            out.append((first, passed, own, copy))
        return out, sibling, chips, (x, y, c)

    def start(self, ins, outs, sems):
        if not self.gather:
            for cp in self.copies(ins, outs, sems):
                cp.start()
            return
        per_array, _, _, _ = self.gather_copies(ins, outs, sems)
        for first, _, own, _ in per_array:
            own.start()
            for cp in first:
                cp.start()

    def relay(self, ins, outs, sems):
        per_array, sibling, chips, (x, y, c) = self.gather_copies(ins, outs, sems)
        for first, passed, own, copy in per_array:
            for q, chip in enumerate(chips):
                copy(1 + q, (*chip, c), (x, y, c)).wait_recv()
                passed[q].start()

    def wait(self, ins, outs, sems, relayed=False):
        if not self.gather:
            for cp in self.copies(ins, outs, sems):
                cp.wait()
            return
        if not relayed:
            self.relay(ins, outs, sems)
        per_array, sibling, chips, (x, y, c) = self.gather_copies(ins, outs, sems)
        for first, passed, own, copy in per_array:
            copy(0, sibling, (x, y, c)).wait_recv()
            for q, chip in enumerate(chips):
                copy(4 + q, (*chip, 1 - c), (x, y, c)).wait_recv()
            for cp in first + passed:
                cp.wait_send()
            own.wait()


def _hosted_call(host, body, *, name, grid, in_specs, out_specs, out_shape, scratch_shapes=(), compiler_params):
    out_specs = list(out_specs) if isinstance(out_specs, (list, tuple)) else [out_specs]
    out_shape = list(out_shape) if isinstance(out_shape, (list, tuple)) else [out_shape]
    if host is None:
        return pl.pallas_call(body, name=name, grid=grid, in_specs=in_specs, out_specs=out_specs,
                              out_shape=out_shape, scratch_shapes=list(scratch_shapes),
                              compiler_params=compiler_params)
    n_in, n_out, n_scr, k = len(in_specs), len(out_shape), len(scratch_shapes), host.n

    def wrapped(*refs):
        ins, h_in = refs[:n_in], refs[n_in:n_in + k]
        outs, h_out = refs[n_in + k:n_in + k + n_out], refs[n_in + k + n_out:n_in + 2 * k + n_out]
        scr, sems = refs[n_in + 2 * k + n_out:n_in + 2 * k + n_out + n_scr], refs[n_in + 2 * k + n_out + n_scr:]
        ids = [pl.program_id(a) for a in range(len(grid))]
        first = functools.reduce(jnp.logical_and, [i == 0 for i in ids])
        last = functools.reduce(jnp.logical_and, [i == g - 1 for i, g in zip(ids, grid)])
        steps = math.prod(grid)
        relay_at = (3 * steps) // 4 if host.gather and steps >= 8 else None

        @pl.when(first)
        def _():
            host.start(h_in, h_out, sems)

        if relay_at is not None:
            coords, rest = [], relay_at
            for g in reversed(grid):
                coords.append(rest % g)
                rest //= g

            @pl.when(functools.reduce(jnp.logical_and, [i == cd for i, cd in zip(ids, reversed(coords))]))
            def _():
                host.relay(h_in, h_out, sems)

        body(*ins, *outs, *scr)

        @pl.when(last)
        def _():
            host.wait(h_in, h_out, sems, relayed=relay_at is not None)

    hbm = pl.BlockSpec(memory_space=pl.ANY)
    call = pl.pallas_call(
        wrapped, name=name, grid=grid, in_specs=list(in_specs) + [hbm] * k, out_specs=out_specs + [hbm] * k,
        out_shape=out_shape + host.out_shape(), scratch_shapes=list(scratch_shapes) + host.scratch(),
        compiler_params=compiler_params)
    return lambda *args: call(*args, *host.arrs)


def _exchange(arrs, *, gather, name):
    host = _Exchange(arrs, gather)

    def body(*refs):
        ins, outs, sems = refs[:host.n], refs[host.n:2 * host.n], refs[2 * host.n:]
        host.start(ins, outs, sems)
        host.wait(ins, outs, sems)

    hbm = pl.BlockSpec(memory_space=pl.ANY)
    return pl.pallas_call(
        body, name=name, in_specs=[hbm] * host.n, out_specs=[hbm] * host.n, out_shape=host.out_shape(),
        scratch_shapes=host.scratch(), compiler_params=pltpu.CompilerParams(has_side_effects=True),
    )(*arrs)


def _ffn_fwd(xhat, g_in, b_in, wg, wu, wd, *, tm, name, host=None):
    t = xhat.shape[0]
    nj = N_DEV

    def body(x_ref, g_ref, b_ref, wg_ref, wu_ref, wd_ref, xo_ref, rstd_ref, hg_ref, hu_ref, xb, acc):
        j = pl.program_id(1)

        @pl.when(j == 0)
        def _():
            xb[...] = (x_ref[...] * g_ref[...] + b_ref[...]).astype(bf16)
            acc[...] = jnp.zeros_like(acc)

        hg = jnp.dot(xb[...], wg_ref[...], preferred_element_type=f32)
        hu = jnp.dot(xb[...], wu_ref[...], preferred_element_type=f32)
        hg_ref[...] = hg.astype(bf16)
        hu_ref[...] = hu.astype(bf16)
        a = hg * _sigmoid_tanh(hg) * hu
        acc[...] += jnp.dot(a.astype(bf16), wd_ref[...], preferred_element_type=f32)

        @pl.when(j == nj - 1)
        def _():
            x = x_ref[...] * g_ref[...] + b_ref[...]
            xo, rstd = _ln_fwd_tile(ALPHA * x + 0.5 * acc[...])
            xo_ref[...] = xo
            rstd_ref[...] = rstd

    row = pl.BlockSpec((1, D_MODEL), lambda i, j: (0, 0))
    return _hosted_call(
        host, body, name=name, grid=(t // tm, nj),
        in_specs=[pl.BlockSpec((tm, D_MODEL), lambda i, j: (i, 0)), row, row,
                  pl.BlockSpec((None, D_MODEL, FF_TILE), lambda i, j: (j, 0, 0)),
                  pl.BlockSpec((None, D_MODEL, FF_TILE), lambda i, j: (j, 0, 0)),
                  pl.BlockSpec((None, FF_TILE, D_MODEL), lambda i, j: (j, 0, 0))],
        out_specs=[pl.BlockSpec((tm, D_MODEL), lambda i, j: (i, 0)),
                   pl.BlockSpec((tm, 1), lambda i, j: (i, 0)),
                   pl.BlockSpec((tm, FF_TILE), lambda i, j: (i, j)),
                   pl.BlockSpec((tm, FF_TILE), lambda i, j: (i, j))],
        out_shape=[jax.ShapeDtypeStruct((t, D_MODEL), f32), jax.ShapeDtypeStruct((t, 1), f32),
                   jax.ShapeDtypeStruct((t, D_FF), bf16), jax.ShapeDtypeStruct((t, D_FF), bf16)],
        scratch_shapes=[pltpu.VMEM((tm, D_MODEL), bf16), pltpu.VMEM((tm, D_MODEL), f32)],
        compiler_params=_params(("arbitrary", "arbitrary")),
    )(xhat, g_in, b_in, wg, wu, wd)


def _ffn1_fwd_gathering(x, own, extra, *, tm, name):
    t = x.shape[0]
    n_i = t // tm
    n_arr = 3
    k_extra = extra.n
    ex = _Exchange(list(own), gather=True)
    ax, ay, ac = lax.axis_index("x"), lax.axis_index("y"), lax.axis_index("c")
    order = jnp.stack([4 * px + 2 * py + pc for px, py in ((ax, ay), (1 - ax, ay), (ax, 1 - ay), (1 - ax, 1 - ay))
                       for pc in (ac, 1 - ac)]).astype(jnp.int32)
    arrival = [None, (0, None), (1, 0), (4, None), (2, 1), (5, None), (3, 2), (6, None)]

    def body(order_ref, x_ref, *refs):
        w_in, e_in = refs[:n_arr], refs[n_arr:n_arr + k_extra]
        refs = refs[n_arr + k_extra:]
        xo_ref, rstd_ref, hg_ref, hu_ref = refs[:4]
        w_all, e_out = refs[4:4 + n_arr], refs[4 + n_arr:4 + n_arr + k_extra]
        acc, wgb, wub, wdb, fetch_sems, send_sems, recv_sems, local_sems = refs[4 + n_arr + k_extra:12 + n_arr + k_extra]
        e_sems = refs[12 + n_arr + k_extra:]
        bufs = (wgb, wub, wdb)
        s = pl.program_id(0)
        i = pl.program_id(1)
        per_array, sibling, chips, (x_, y_, c_) = ex.gather_copies(w_in, w_all, (send_sems, recv_sems, local_sems))

        def fetch(pos, slot):
            return [pltpu.make_async_copy(w_in[a] if pos == 0 else w_all[a].at[order_ref[pos]],
                                          bufs[a].at[slot], fetch_sems.at[n_arr * slot + a]) for a in range(n_arr)]

        def source_of(pos):
            chip = (x_, y_) if pos < 2 else chips[(pos - 2) // 2]
            return (*chip, c_ if pos % 2 == 0 else 1 - c_)

        @pl.when(jnp.logical_and(s == 0, i == 0))
        def _():
            for q in range(4):
                for first, _, own_copy, _ in per_array:
                    if q == 0:
                        own_copy.start()
                    first[q].start()
            for cp in fetch(0, 0):
                cp.start()
            for cp in fetch(0, 0):
                cp.wait()

        @pl.when(jnp.logical_and(s == N_DEV - 3, i == 0))
        def _():
            extra.start(e_in, e_out, e_sems)

        for pos in range(1, N_DEV):
            @pl.when(jnp.logical_and(s == pos - 1, i == min(1, n_i - 1)))
            def _(pos=pos):
                sem, passes = arrival[pos]
                for _, passed, _, copy in per_array:
                    copy(sem, source_of(pos), (x_, y_, c_)).wait_recv()
                    if passes is not None:
                        passed[passes].start()
                for cp in fetch(pos, pos % 2):
                    cp.start()

            @pl.when(jnp.logical_and(s == pos, i == 0))
            def _(pos=pos):
                for cp in fetch(pos, pos % 2):
                    cp.wait()

        slot = s % 2
        xb = x_ref[...].astype(bf16)
        hg = jnp.dot(xb, wgb[slot], preferred_element_type=f32)
        hu = jnp.dot(xb, wub[slot], preferred_element_type=f32)
        hg_ref[...] = hg.astype(bf16)
        hu_ref[...] = hu.astype(bf16)
        a = hg * _sigmoid_tanh(hg) * hu
        part = jnp.dot(a.astype(bf16), wdb[slot], preferred_element_type=f32)

        @pl.when(s == 0)
        def _():
            acc[i] = part

        @pl.when(s > 0)
        def _():
            acc[i] += part

        @pl.when(s == N_DEV - 1)
        def _():
            xo, rstd = _ln_fwd_tile(ALPHA * x_ref[...] + 0.5 * acc[i])
            xo_ref[...] = xo
            rstd_ref[...] = rstd

        @pl.when(jnp.logical_and(s == N_DEV - 1, i == n_i - 1))
        def _():
            for first, passed, own_copy, _ in per_array:
                for cp in first + passed:
                    cp.wait_send()
                own_copy.wait()
            extra.wait(e_in, e_out, e_sems)

    hbm = pl.BlockSpec(memory_space=pl.ANY)
    last = N_DEV - 1
    tok_out = pl.BlockSpec((tm, D_MODEL), lambda s, i, o: (jnp.where(s == last, i, 0), 0))
    col_out = pl.BlockSpec((tm, 1), lambda s, i, o: (jnp.where(s == last, i, 0), 0))
    hid = pl.BlockSpec((tm, FF_TILE), lambda s, i, o: (i, o[s]))
    shard_shapes = [(N_DEV,) + w.shape for w in own]
    grid_spec = pltpu.PrefetchScalarGridSpec(
        num_scalar_prefetch=1, grid=(N_DEV, n_i),
        in_specs=[pl.BlockSpec((tm, D_MODEL), lambda s, i, o: (i, 0))] + [hbm] * (n_arr + k_extra),
        out_specs=[tok_out, col_out, hid, hid] + [hbm] * (n_arr + k_extra),
        scratch_shapes=[pltpu.VMEM((n_i, tm, D_MODEL), f32)]
        + [pltpu.VMEM((2,) + w.shape, bf16) for w in own]
        + [pltpu.SemaphoreType.DMA((2 * n_arr,))] + ex.scratch() + extra.scratch())
    res = pl.pallas_call(
        body, name=name, grid_spec=grid_spec,
        out_shape=[jax.ShapeDtypeStruct((t, D_MODEL), f32), jax.ShapeDtypeStruct((t, 1), f32),
                   jax.ShapeDtypeStruct((t, D_FF), bf16), jax.ShapeDtypeStruct((t, D_FF), bf16)]
        + [jax.ShapeDtypeStruct(sh, bf16) for sh in shard_shapes] + extra.out_shape(),
        compiler_params=_params(("arbitrary", "arbitrary")),
    )(order, x, *own, *extra.arrs)
    return res


def _ffn_bwd(dpre, hg, hu, wg, wu, wd, ln_in, *, tm, name, host=None):
    t = dpre.shape[0]
    nj = N_DEV
    with_ln = ln_in is not None

    def body(*refs):
        if with_ln:
            (dp_ref, hg_ref, hu_ref, wg_ref, wu_ref, wd_ref, xh_ref, rs_ref, g_ref,
             dx_ref, gg_ref, gb_ref, dhg_ref, dhu_ref, a_ref, dfb, acc) = refs
        else:
            (dp_ref, hg_ref, hu_ref, wg_ref, wu_ref, wd_ref,
             dx_ref, dhg_ref, dhu_ref, a_ref, dfb, acc) = refs
        i = pl.program_id(0)
        j = pl.program_id(1)

        @pl.when(j == 0)
        def _():
            dfb[...] = (0.5 * dp_ref[...]).astype(bf16)
            acc[...] = jnp.zeros_like(acc)

        da = lax.dot_general(dfb[...], wd_ref[...], _NT, preferred_element_type=f32)
        hgv = hg_ref[...].astype(f32)
        huv = hu_ref[...].astype(f32)
        sg = _sigmoid_tanh(hgv)
        silu = hgv * sg
        a_ref[...] = (silu * huv).astype(bf16)
        dhu = (da * silu).astype(bf16)
        dhg = (da * huv * (sg * (1.0 + hgv * (1.0 - sg)))).astype(bf16)
        dhg_ref[...] = dhg
        dhu_ref[...] = dhu
        acc[...] += (lax.dot_general(dhg, wg_ref[...], _NT, preferred_element_type=f32)
                     + lax.dot_general(dhu, wu_ref[...], _NT, preferred_element_type=f32))

        @pl.when(j == nj - 1)
        def _():
            dx = ALPHA * dp_ref[...] + acc[...]
            if with_ln:
                dprev, gg, gb = _ln_bwd_tile(dx, xh_ref[...], rs_ref[...], g_ref[...])
                dx_ref[...] = dprev

                @pl.when(i == 0)
                def _():
                    gg_ref[...] = gg
                    gb_ref[...] = gb

                @pl.when(i > 0)
                def _():
                    gg_ref[...] += gg
                    gb_ref[...] += gb
            else:
                dx_ref[...] = dx

    tok = pl.BlockSpec((tm, D_MODEL), lambda i, j: (i, 0), pipeline_mode=pl.Buffered(1))
    row = pl.BlockSpec((1, D_MODEL), lambda i, j: (0, 0))
    hid = pl.BlockSpec((tm, FF_TILE), lambda i, j: (i, j))
    in_specs = [tok, hid, hid,
                pl.BlockSpec((None, D_MODEL, FF_TILE), lambda i, j: (j, 0, 0)),
                pl.BlockSpec((None, D_MODEL, FF_TILE), lambda i, j: (j, 0, 0)),
                pl.BlockSpec((None, FF_TILE, D_MODEL), lambda i, j: (j, 0, 0))]
    args = [dpre, hg, hu, wg, wu, wd]
    out_specs = [tok]
    out_shape = [jax.ShapeDtypeStruct((t, D_MODEL), f32)]
    if with_ln:
        in_specs += [tok, pl.BlockSpec((tm, 1), lambda i, j: (i, 0)), row]
        args += list(ln_in)
        out_specs += [row, row]
        out_shape += [jax.ShapeDtypeStruct((1, D_MODEL), f32)] * 2
    out_specs += [hid, hid, hid]
    out_shape += [jax.ShapeDtypeStruct((t, D_FF), bf16)] * 3
    return _hosted_call(
        host, body, name=name, grid=(t // tm, nj), in_specs=in_specs, out_specs=out_specs, out_shape=out_shape,
        scratch_shapes=[pltpu.VMEM((tm, D_MODEL), bf16), pltpu.VMEM((tm, D_MODEL), f32)],
        compiler_params=_params(("arbitrary", "arbitrary")),
    )(*args)


def _ffn_bwd_act(dpre, hg, hu, wd, *, tm, name, host=None):
    t = dpre.shape[0]

    def body(dp_ref, hg_ref, hu_ref, wd_ref, dhg_ref, dhu_ref, a_ref, dfb):
        @pl.when(pl.program_id(1) == 0)
        def _():
            dfb[...] = (0.5 * dp_ref[...]).astype(bf16)

        da = lax.dot_general(dfb[...], wd_ref[...], _NT, preferred_element_type=f32)
        hgv = hg_ref[...].astype(f32)
        huv = hu_ref[...].astype(f32)
        sg = _sigmoid_tanh(hgv)
        silu = hgv * sg
        a_ref[...] = (silu * huv).astype(bf16)
        dhu_ref[...] = (da * silu).astype(bf16)
        dhg_ref[...] = (da * huv * (sg * (1.0 + hgv * (1.0 - sg)))).astype(bf16)

    hid = pl.BlockSpec((tm, FF_TILE), lambda i, j: (i, j))
    return _hosted_call(
        host, body, name=name, grid=(t // tm, N_DEV),
        in_specs=[pl.BlockSpec((tm, D_MODEL), lambda i, j: (i, 0)), hid, hid,
                  pl.BlockSpec((None, FF_TILE, D_MODEL), lambda i, j: (j, 0, 0))],
        out_specs=[hid, hid, hid], out_shape=[jax.ShapeDtypeStruct((t, D_FF), bf16)] * 3,
        scratch_shapes=[pltpu.VMEM((tm, D_MODEL), bf16)],
        compiler_params=_params(("arbitrary", "arbitrary")),
    )(dpre, hg, hu, wd)


def _ffn_bwd_dx(dpre, dhg, dhu, wg, wu, *, tm, name, host=None):
    t = dpre.shape[0]
    nj = N_DEV

    def body(dp_ref, dhg_ref, dhu_ref, wg_ref, wu_ref, dx_ref, acc):
        j = pl.program_id(1)

        @pl.when(j == 0)
        def _():
            acc[...] = jnp.zeros_like(acc)

        acc[...] += (lax.dot_general(dhg_ref[...], wg_ref[...], _NT, preferred_element_type=f32)
                     + lax.dot_general(dhu_ref[...], wu_ref[...], _NT, preferred_element_type=f32))

        @pl.when(j == nj - 1)
        def _():
            dx_ref[...] = ALPHA * dp_ref[...] + acc[...]

    tok = pl.BlockSpec((tm, D_MODEL), lambda i, j: (i, 0))
    hid = pl.BlockSpec((tm, FF_TILE), lambda i, j: (i, j))
    wspec = pl.BlockSpec((None, D_MODEL, FF_TILE), lambda i, j: (j, 0, 0))
    return _hosted_call(
        host, body, name=name, grid=(t // tm, nj), in_specs=[tok, hid, hid, wspec, wspec],
        out_specs=[tok], out_shape=[jax.ShapeDtypeStruct((t, D_MODEL), f32)],
        scratch_shapes=[pltpu.VMEM((tm, D_MODEL), f32)],
        compiler_params=_params(("arbitrary", "arbitrary")),
    )(dpre, dhg, dhu, wg, wu)


def _mm(a, b, *, mode, out_dtype, tm, tn, tk, name, affine=None, a_cols=None, b_cols=None,
        b_blocked=False, out_blocked=False, out_scale=None):
    if mode == "nn":
        m_full, k_full = a.shape
        m_dim, k_dim = (m_full, a_cols[1]) if a_cols else (m_full, k_full)
    else:
        k_dim, m_full = a.shape
        m_dim = a_cols[1] if a_cols else m_full
    a_off = a_cols[0] if a_cols else 0
    if b_blocked:
        n_dim = b.shape[0] * b.shape[2]
        assert b.shape[2] == tn
    else:
        n_dim = b_cols[1] if b_cols else b.shape[1]
    b_off = b_cols[0] if b_cols else 0
    assert m_dim % tm == 0 and n_dim % tn == 0 and k_dim % tk == 0, (name, m_dim, n_dim, k_dim)
    nk = k_dim // tk

    def body(*refs):
        if affine is not None:
            a_ref, g_ref, s_ref, b_ref, o_ref, acc = refs
        else:
            a_ref, b_ref, o_ref, acc = refs
        k = pl.program_id(2)

        @pl.when(k == 0)
        def _():
            acc[...] = jnp.zeros_like(acc)

        av = a_ref[...]
        if affine is not None:
            av = av * g_ref[...] + s_ref[...]
        av = av.astype(bf16)
        bv = b_ref[...].astype(bf16)
        if mode == "nn":
            acc[...] += jnp.dot(av, bv, preferred_element_type=f32)
        else:
            acc[...] += lax.dot_general(av, bv, _TN, preferred_element_type=f32)

        @pl.when(k == nk - 1)
        def _():
            res = acc[...] if out_scale is None else acc[...] * out_scale
            o_ref[...] = res.astype(out_dtype)

    if mode == "nn":
        a_spec = pl.BlockSpec((tm, tk), lambda i, j, k: (i, k + a_off))
        aff_spec = pl.BlockSpec((1, tk), lambda i, j, k: (0, k + a_off))
    else:
        a_spec = pl.BlockSpec((tk, tm), lambda i, j, k: (k, i + a_off))
        aff_spec = pl.BlockSpec((1, tm), lambda i, j, k: (0, i + a_off))
    if b_blocked:
        b_spec = pl.BlockSpec((None, tk, tn), lambda i, j, k: (j, k, 0))
    else:
        b_spec = pl.BlockSpec((tk, tn), lambda i, j, k: (k, j + b_off))
    if out_blocked:
        o_spec = pl.BlockSpec((None, tm, tn), lambda i, j, k: (j, i, 0))
        o_shape = jax.ShapeDtypeStruct((n_dim // tn, m_dim, tn), out_dtype)
    else:
        o_spec = pl.BlockSpec((tm, tn), lambda i, j, k: (i, j))
        o_shape = jax.ShapeDtypeStruct((m_dim, n_dim), out_dtype)
    in_specs = [a_spec] + ([aff_spec, aff_spec] if affine is not None else []) + [b_spec]
    args = [a] + (list(affine) if affine is not None else []) + [b]
    return pl.pallas_call(
        body, name=name, grid=(m_dim // tm, n_dim // tn, nk), in_specs=in_specs, out_specs=o_spec,
        out_shape=o_shape, scratch_shapes=[pltpu.VMEM((tm, tn), f32)],
        compiler_params=_params(("arbitrary", "arbitrary", "arbitrary")),
    )(*args)


def _mm_tn(a, b, *, out_dtype, tm, mb, tn, nb, tk, name, affine=None, out_blocked=False, out_scale=None,
           pair=False, host=None):
    k_dim, m_dim = a.shape
    multi_b = isinstance(b, (list, tuple))
    b_list = list(b) if multi_b else [b]
    n_dim = nb * tn if multi_b else b.shape[1]
    assert m_dim % (mb * tm) == 0 and n_dim % (nb * tn) == 0 and k_dim % tk == 0, (name, m_dim, n_dim, k_dim)
    nk = k_dim // tk
    grid = (m_dim // (mb * tm), n_dim // (nb * tn), nk)
    if pair:
        assert mb * nb == 4 and grid[0] * grid[1] == 2 and out_dtype == bf16, name

    def body(*refs):
        if pair:
            refs, (acc, send_buf, recv_buf, send_sems, recv_sems) = refs[:-5], refs[-5:]
        else:
            refs, acc = refs[:-1], refs[-1]
        a_ref, o_ref = refs[0], refs[-1]
        if affine is not None:
            g_ref, s_ref = refs[1:3]
        b_refs = refs[3 if affine is not None else 1:-1]
        k = pl.program_id(2)

        @pl.when(k == 0)
        def _():
            acc[...] = jnp.zeros_like(acc)

        av = a_ref[...]
        if affine is not None:
            av = av * g_ref[...] + s_ref[...]
        av = av.astype(bf16)
        if multi_b:
            pieces = [r[...].astype(bf16) for r in b_refs]
        else:
            bv = b_refs[0][...].astype(bf16)
            pieces = [bv[:, jn * tn:(jn + 1) * tn] for jn in range(nb)]
        for im in range(mb):
            a_t = av[:, im * tm:(im + 1) * tm].T
            for jn in range(nb):
                acc[im * nb + jn] += jnp.dot(a_t, pieces[jn], preferred_element_type=f32)

        def scaled(v):
            return v if out_scale is None else v * out_scale

        @pl.when(k == nk - 1)
        def _():
            if pair:
                x, y, c = lax.axis_index("x"), lax.axis_index("y"), lax.axis_index("c")
                window = pl.program_id(0) + pl.program_id(1)
                swaps = []
                for cc in range(2):
                    send_buf[cc] = scaled(acc[2 * cc + 1 - c]).astype(bf16)
                    swaps.append(pltpu.make_async_remote_copy(
                        src_ref=send_buf.at[cc], dst_ref=recv_buf.at[window, cc],
                        send_sem=send_sems.at[2 * window + cc], recv_sem=recv_sems.at[2 * window + cc],
                        device_id=(x, y, 1 - c), device_id_type=MESH_T))
                    swaps[cc].start()
                for cc in range(2):
                    swaps[cc].wait_recv()
                    o_ref[cc] = (scaled(acc[2 * cc + c]) + recv_buf[window, cc].astype(f32)).astype(bf16)
                for cc in range(2):
                    swaps[cc].wait_send()
                return
            for im in range(mb):
                for jn in range(nb):
                    res = scaled(acc[im * nb + jn])
                    if out_blocked:
                        o_ref[jn, im * tm:(im + 1) * tm, :] = res.astype(out_dtype)
                    else:
                        o_ref[im * tm:(im + 1) * tm, jn * tn:(jn + 1) * tn] = res.astype(out_dtype)

    a_spec = pl.BlockSpec((tk, mb * tm), lambda i, j, k: (k, i))
    aff_spec = pl.BlockSpec((1, mb * tm), lambda i, j, k: (0, i))
    if multi_b:
        b_specs = [pl.BlockSpec((tk, tn), lambda i, j, k: (k, 0))] * nb
    else:
        b_specs = [pl.BlockSpec((tk, nb * tn), lambda i, j, k: (k, j))]
    scratch = [pltpu.VMEM((mb * nb, tm, tn), f32)]
    if pair:
        o_spec = pl.BlockSpec((2, tm, tn), lambda i, j, k: (i + j, 0, 0))
        o_shape = jax.ShapeDtypeStruct((4, tm, tn), out_dtype)
        scratch += [pltpu.VMEM((2, tm, tn), bf16), pltpu.VMEM((2, 2, tm, tn), bf16),
                    pltpu.SemaphoreType.DMA((4,)), pltpu.SemaphoreType.DMA((4,))]
    elif out_blocked:
        o_spec = pl.BlockSpec((nb, mb * tm, tn), lambda i, j, k: (j, i, 0))
        o_shape = jax.ShapeDtypeStruct((n_dim // tn, m_dim, tn), out_dtype)
    else:
        o_spec = pl.BlockSpec((mb * tm, nb * tn), lambda i, j, k: (i, j))
        o_shape = jax.ShapeDtypeStruct((m_dim, n_dim), out_dtype)
    in_specs = [a_spec] + ([aff_spec, aff_spec] if affine is not None else []) + b_specs
    args = [a] + (list(affine) if affine is not None else []) + b_list
    res = _hosted_call(
        host, body, name=name, grid=grid, in_specs=in_specs, out_specs=o_spec, out_shape=o_shape,
        scratch_shapes=scratch, compiler_params=_params(("arbitrary", "arbitrary", "arbitrary")),
    )(*args)
    return res[0] if host is None else res


def _in_proj(xhat, g, b, w_in, *, tm, name, host=None):
    t = xhat.shape[0]
    n_qkv, n_l = 3 * FOX_W, 2 * LRU_W

    def body(x_ref, g_ref, b_ref, w_ref, qkv_ref, zl_ref, zfg_ref):
        xb = (x_ref[...] * g_ref[...] + b_ref[...]).astype(bf16)
        qkv_ref[...] = jnp.dot(xb, w_ref[:, :n_qkv], preferred_element_type=f32).astype(bf16)
        zl_ref[...] = jnp.dot(xb, w_ref[:, n_qkv:n_qkv + n_l], preferred_element_type=f32)
        zfg_ref[...] = jnp.dot(xb, w_ref[:, n_qkv + n_l:], preferred_element_type=f32)

    row = pl.BlockSpec((1, D_MODEL), lambda i: (0, 0))
    return _hosted_call(
        host, body, name=name, grid=(t // tm,),
        in_specs=[pl.BlockSpec((tm, D_MODEL), lambda i: (i, 0)), row, row,
                  pl.BlockSpec(w_in.shape, lambda i: (0, 0))],
        out_specs=[pl.BlockSpec((tm, n_qkv), lambda i: (i, 0)), pl.BlockSpec((tm, n_l), lambda i: (i, 0)),
                   pl.BlockSpec((tm, LANES), lambda i: (i, 0))],
        out_shape=[jax.ShapeDtypeStruct((t, n_qkv), bf16), jax.ShapeDtypeStruct((t, n_l), f32),
                   jax.ShapeDtypeStruct((t, LANES), f32)],
        compiler_params=_params(("arbitrary",)),
    )(xhat, g, b, w_in)


def _mmln(pairs, *, tm, name, resid=None, resid_scale=1.0, epi=None, ln=None, n_out=D_MODEL):
    t = pairs[0][0].shape[0]
    n_pairs = len(pairs)
    n_resid = 0 if resid is None else len(resid) - 1

    def body(*refs):
        pos = 0
        val = None
        for p in range(n_pairs):
            a_ref, b_ref = refs[pos], refs[pos + 1]
            pos += 2
            av = a_ref[...].astype(bf16)
            bv = b_ref[...].astype(bf16)
            if pairs[p][6] == "nn":
                term = jnp.dot(av, bv, preferred_element_type=f32)
            else:
                term = lax.dot_general(av, bv, _NT, preferred_element_type=f32)
            val = term if val is None else val + term
        if resid is not None:
            if resid[0] == "plain":
                r = refs[pos][...]
            else:
                r = refs[pos][...] * refs[pos + 1][...] + refs[pos + 2][...]
            pos += n_resid
            val = val + resid_scale * r
        if epi is None:
            o_ref = refs[pos]
            o_ref[...] = val.astype(o_ref.dtype)
        elif epi == "ln_fwd":
            xo, rstd = _ln_fwd_tile(val)
            refs[pos][...] = xo
            refs[pos + 1][...] = rstd
        else:
            xh_ref, rs_ref, g_ref, dx_ref, gg_ref, gb_ref = refs[pos:pos + 6]
            dprev, gg, gb = _ln_bwd_tile(val, xh_ref[...], rs_ref[...], g_ref[...])
            dx_ref[...] = dprev
            i = pl.program_id(0)

            @pl.when(i == 0)
            def _():
                gg_ref[...] = gg
                gb_ref[...] = gb

            @pl.when(i > 0)
            def _():
                gg_ref[...] += gg
                gb_ref[...] += gb

    in_specs, args = [], []
    for (a, acb, aw, b, bcb, bw, mode) in pairs:
        in_specs.append(pl.BlockSpec((tm, aw), lambda i, acb=acb: (i, acb)))
        args.append(a)
        if mode == "nn":
            in_specs.append(pl.BlockSpec((aw, n_out), lambda i, bcb=bcb: (bcb, 0)))
        else:
            in_specs.append(pl.BlockSpec((n_out, bw), lambda i, bcb=bcb: (0, bcb)))
        args.append(b)
    tok = pl.BlockSpec((tm, n_out), lambda i: (i, 0))
    row = pl.BlockSpec((1, n_out), lambda i: (0, 0))
    col = pl.BlockSpec((tm, 1), lambda i: (i, 0))
    if resid is not None:
        in_specs += [tok] if resid[0] == "plain" else [tok, row, row]
        args += list(resid[1:])
    if epi is None:
        out_specs, out_shape = tok, jax.ShapeDtypeStruct((t, n_out), f32)
    elif epi == "ln_fwd":
        out_specs = [tok, col]
        out_shape = [jax.ShapeDtypeStruct((t, n_out), f32), jax.ShapeDtypeStruct((t, 1), f32)]
    else:
        in_specs += [tok, col, row]
        args += list(ln)
        out_specs = [tok, row, row]
        out_shape = [jax.ShapeDtypeStruct((t, n_out), f32)] + [jax.ShapeDtypeStruct((1, n_out), f32)] * 2
    return pl.pallas_call(
        body, name=name, grid=(t // tm,), in_specs=in_specs, out_specs=out_specs, out_shape=out_shape,
        compiler_params=_params(("arbitrary",)),
    )(*args)


def _loss_bwd(xhat, rstd, g, b, target, *, tm, name):
    t = xhat.shape[0]

    def body(xh_ref, rs_ref, g_ref, b_ref, tg_ref, dx_ref, sq_ref, gg_ref, gb_ref):
        i = pl.program_id(0)
        xh = xh_ref[...]
        diff = xh * g_ref[...] + b_ref[...] - tg_ref[...]
        sq = jnp.sum(diff * diff, axis=0, keepdims=True)
        dprev, gg, gb = _ln_bwd_tile(diff * (1.0 / D_MODEL), xh, rs_ref[...], g_ref[...])
        dx_ref[...] = dprev

        @pl.when(i == 0)
        def _():
            sq_ref[...] = sq
            gg_ref[...] = gg
            gb_ref[...] = gb

        @pl.when(i > 0)
        def _():
            sq_ref[...] += sq
            gg_ref[...] += gg
            gb_ref[...] += gb

    tok = pl.BlockSpec((tm, D_MODEL), lambda i: (i, 0))
    row = pl.BlockSpec((1, D_MODEL), lambda i: (0, 0))
    return pl.pallas_call(
        body, name=name, grid=(t // tm,),
        in_specs=[tok, pl.BlockSpec((tm, 1), lambda i: (i, 0)), row, row, tok],
        out_specs=[tok, row, row, row],
        out_shape=[jax.ShapeDtypeStruct((t, D_MODEL), f32)] + [jax.ShapeDtypeStruct((1, D_MODEL), f32)] * 3,
        compiler_params=_params(("arbitrary",)),
    )(xhat, rstd, g, b, target)


CUM_TILE = 256


def _tri(n, lower):
    r = lax.broadcasted_iota(jnp.int32, (n, n), 0)
    c = lax.broadcasted_iota(jnp.int32, (n, n), 1)
    return jnp.where((r >= c) if lower else (r <= c), 1.0, 0.0).astype(f32)


def _cum_fwd(zfg, bfg, *, name):
    t = zfg.shape[0]

    def body(z_ref, b_ref, o_ref, carry):
        @pl.when(pl.program_id(0) == 0)
        def _():
            carry[...] = jnp.zeros_like(carry)

        ls = -_softplus(-(z_ref[...] + b_ref[...]))
        c = jnp.dot(_tri(CUM_TILE, True), ls, preferred_element_type=f32,
                    precision=lax.Precision.HIGHEST) + carry[...]
        o_ref[...] = c
        carry[...] = c[CUM_TILE - 1:CUM_TILE, :]

    blk = pl.BlockSpec((CUM_TILE, LANES), lambda i: (i, 0))
    return pl.pallas_call(
        body, name=name, grid=(t // CUM_TILE,),
        in_specs=[blk, pl.BlockSpec((1, LANES), lambda i: (0, 0))], out_specs=blk,
        out_shape=jax.ShapeDtypeStruct((t, LANES), f32), scratch_shapes=[pltpu.VMEM((1, LANES), f32)],
        compiler_params=_params(("arbitrary",)),
    )(zfg, bfg)


def _cum_bwd(dcum_q, dcum_k, zfg, bfg, *, name):
    t = zfg.shape[0]
    n = t // CUM_TILE

    def body(d_ref, d2_ref, z_ref, b_ref, o_ref, s_ref, carry):
        i = pl.program_id(0)

        @pl.when(i == 0)
        def _():
            carry[...] = jnp.zeros_like(carry)

        dls = jnp.dot(_tri(CUM_TILE, False), d_ref[...] + d2_ref[...], preferred_element_type=f32,
                      precision=lax.Precision.HIGHEST) + carry[...]
        carry[...] = dls[0:1, :]
        lane = lax.broadcasted_iota(jnp.int32, (CUM_TILE, LANES), 1)
        dfg = jnp.where(lane < HEADS, dls * _sigmoid(-(z_ref[...] + b_ref[...])), 0.0)
        o_ref[...] = dfg
        tot = jnp.sum(dfg, axis=0, keepdims=True)

        @pl.when(i == 0)
        def _():
            s_ref[...] = tot

        @pl.when(i > 0)
        def _():
            s_ref[...] += tot

    blk = pl.BlockSpec((CUM_TILE, LANES), lambda i: (n - 1 - i, 0))
    row = pl.BlockSpec((1, LANES), lambda i: (0, 0))
    return pl.pallas_call(
        body, name=name, grid=(n,), in_specs=[blk, blk, blk, row], out_specs=[blk, row],
        out_shape=[jax.ShapeDtypeStruct((t, LANES), f32), jax.ShapeDtypeStruct((1, LANES), f32)],
        scratch_shapes=[pltpu.VMEM((1, LANES), f32)],
        compiler_params=_params(("arbitrary",)),
    )(dcum_q, dcum_k, zfg, bfg)


ATT_TILE = 512


ATT_ROWS = 32


def _causal_rows(r, transposed):
    rr = lax.broadcasted_iota(jnp.int32, (ATT_ROWS, ATT_TILE), 0) + r * ATT_ROWS
    cc = lax.broadcasted_iota(jnp.int32, (ATT_ROWS, ATT_TILE), 1)
    return (cc >= rr) if transposed else (rr >= cc)


def _causal(i, j, transposed):
    r = lax.broadcasted_iota(jnp.int32, (ATT_TILE, ATT_TILE), 0)
    c = lax.broadcasted_iota(jnp.int32, (ATT_TILE, ATT_TILE), 1)
    if transposed:
        return (c + i * ATT_TILE) >= (r + j * ATT_TILE)
    return (r + i * ATT_TILE) >= (c + j * ATT_TILE)


def _attn_fwd(qkv, cum, cum_t, *, name, host=None):
    t = qkv.shape[0]
    n = t // ATT_TILE
    tq = ATT_TILE

    def body(q_ref, k_ref, v_ref, cq_ref, ck_ref, o_ref, lse_ref, acc, m_s, l_s, c_s, s_s, p_s):
        i = pl.program_id(0)
        j = pl.program_id(1)

        @pl.when(j == 0)
        def _():
            acc[...] = jnp.zeros_like(acc)
            m_s[...] = jnp.full_like(m_s, NEG_BIG)
            l_s[...] = jnp.zeros_like(l_s)

        def block(masked):
            for h in range(HEADS):
                hs = slice(HEAD_D * h, HEAD_D * (h + 1))
                s_s[...] = lax.dot_general(q_ref[:, hs] * ATT_SCALE, k_ref[:, hs], _NT, preferred_element_type=f32)
                ck = ck_ref[h:h + 1, :]

                def rows_chunk(r, carry):
                    rows = pl.ds(pl.multiple_of(r * ATT_ROWS, ATT_ROWS), ATT_ROWS)
                    s = s_s[rows, :] + (cq_ref[rows, h:h + 1] - ck)
                    if masked:
                        s = jnp.where(_causal_rows(r, False), s, NEG_BIG)
                    m_old = m_s[rows, h:h + 1]
                    m_new = jnp.maximum(m_old, jnp.max(s, axis=-1, keepdims=True))
                    corr = jnp.exp(m_old - m_new)
                    p = jnp.exp(s - m_new)
                    l_s[rows, h:h + 1] = corr * l_s[rows, h:h + 1] + jnp.sum(p, axis=-1, keepdims=True)
                    m_s[rows, h:h + 1] = m_new
                    c_s[rows, h:h + 1] = corr
                    p_s[rows, :] = p.astype(bf16)
                    return carry

                lax.fori_loop(0, tq // ATT_ROWS, rows_chunk, 0, unroll=4)
                acc[:, hs] = c_s[:, h:h + 1] * acc[:, hs] + jnp.dot(p_s[...], v_ref[:, hs],
                                                                   preferred_element_type=f32)

        @pl.when(j < i)
        def _():
            block(False)

        @pl.when(j == i)
        def _():
            block(True)
            lse_ref[...] = jnp.zeros_like(lse_ref)
            for h in range(HEADS):
                hs = slice(HEAD_D * h, HEAD_D * (h + 1))
                l = l_s[:, h:h + 1]
                o_ref[:, hs] = acc[:, hs] / l
                lse_ref[:, h:h + 1] = m_s[:, h:h + 1] + jnp.log(l)

    return _hosted_call(
        host, body, name=name, grid=(n, n),
        in_specs=[pl.BlockSpec((tq, FOX_W), lambda i, j: (i, 0)),
                  pl.BlockSpec((tq, FOX_W), lambda i, j: (jnp.minimum(i, j), 1)),
                  pl.BlockSpec((tq, FOX_W), lambda i, j: (jnp.minimum(i, j), 2)),
                  pl.BlockSpec((tq, LANES), lambda i, j: (i, 0)),
                  pl.BlockSpec((HEADS, tq), lambda i, j: (0, jnp.minimum(i, j)))],
        out_specs=[pl.BlockSpec((tq, FOX_W), lambda i, j: (i, 0)), pl.BlockSpec((tq, LANES), lambda i, j: (i, 0))],
        out_shape=[jax.ShapeDtypeStruct((t, FOX_W), f32), jax.ShapeDtypeStruct((t, LANES), f32)],
        scratch_shapes=[pltpu.VMEM((tq, FOX_W), f32), pltpu.VMEM((tq, LANES), f32), pltpu.VMEM((tq, LANES), f32),
                        pltpu.VMEM((tq, LANES), f32), pltpu.VMEM((tq, tq), f32), pltpu.VMEM((tq, tq), bf16)],
        compiler_params=_params(("arbitrary", "arbitrary")),
    )(qkv, qkv, qkv, cum, cum_t)


def _attn_delta(dmix, o, *, tm, name):
    t = o.shape[0]

    def body(do_ref, o_ref, d_ref):
        r = lax.broadcasted_iota(jnp.int32, (FOX_W, LANES), 0)
        c = lax.broadcasted_iota(jnp.int32, (FOX_W, LANES), 1)
        pick = jnp.where(r // HEAD_D == c, 1.0, 0.0).astype(f32)
        d_ref[...] = jnp.dot(do_ref[...] * o_ref[...], pick, preferred_element_type=f32,
                             precision=lax.Precision.HIGHEST)

    blk = pl.BlockSpec((tm, FOX_W), lambda i: (i, 0))
    return pl.pallas_call(
        body, name=name, grid=(t // tm,), in_specs=[blk, blk],
        out_specs=pl.BlockSpec((tm, LANES), lambda i: (i, 0)),
        out_shape=jax.ShapeDtypeStruct((t, LANES), f32), compiler_params=_params(("arbitrary",)),
    )(dmix, o)


def _attn_dq(qkv, dmix, cum, cum_t, lse, delta, *, name, host=None):
    t = qkv.shape[0]
    n = t // ATT_TILE
    tq = ATT_TILE

    def body(q_ref, k_ref, v_ref, do_ref, cq_ref, ck_ref, lse_ref, dl_ref, dq_ref, dc_ref, acc, dc_acc):
        i = pl.program_id(0)
        j = pl.program_id(1)

        @pl.when(j == 0)
        def _():
            acc[...] = jnp.zeros_like(acc)
            dc_acc[...] = jnp.zeros_like(dc_acc)

        def block(masked):
            mask = _causal(i, j, False) if masked else None
            for h in range(HEADS):
                hs = slice(HEAD_D * h, HEAD_D * (h + 1))
                kh = k_ref[:, hs]
                s = lax.dot_general(q_ref[:, hs] * ATT_SCALE, kh, _NT, preferred_element_type=f32)
                s = s + cq_ref[:, h:h + 1] - ck_ref[h:h + 1, :]
                if masked:
                    s = jnp.where(mask, s, NEG_BIG)
                p = jnp.exp(s - lse_ref[:, h:h + 1])
                dp = lax.dot_general(do_ref[:, hs].astype(bf16), v_ref[:, hs], _NT, preferred_element_type=f32)
                ds = p * (dp - dl_ref[:, h:h + 1])
                acc[:, hs] += jnp.dot(ds.astype(bf16), kh, preferred_element_type=f32)
                dc_acc[:, h:h + 1] += jnp.sum(ds, axis=-1, keepdims=True)

        @pl.when(j < i)
        def _():
            block(False)

        @pl.when(j == i)
        def _():
            block(True)
            dq_ref[...] = (acc[...] * ATT_SCALE).astype(bf16)
            dc_ref[...] = dc_acc[...]

    col = pl.BlockSpec((tq, LANES), lambda i, j: (i, 0))
    return _hosted_call(
        host, body, name=name, grid=(n, n),
        in_specs=[pl.BlockSpec((tq, FOX_W), lambda i, j: (i, 0)),
                  pl.BlockSpec((tq, FOX_W), lambda i, j: (jnp.minimum(i, j), 1)),
                  pl.BlockSpec((tq, FOX_W), lambda i, j: (jnp.minimum(i, j), 2)),
                  pl.BlockSpec((tq, FOX_W), lambda i, j: (i, 0)),
                  col, pl.BlockSpec((HEADS, tq), lambda i, j: (0, jnp.minimum(i, j))), col, col],
        out_specs=[pl.BlockSpec((tq, FOX_W), lambda i, j: (i, 0)), col],
        out_shape=[jax.ShapeDtypeStruct((t, FOX_W), bf16), jax.ShapeDtypeStruct((t, LANES), f32)],
        scratch_shapes=[pltpu.VMEM((tq, FOX_W), f32), pltpu.VMEM((tq, LANES), f32)],
        compiler_params=_params(("arbitrary", "arbitrary")),
    )(qkv, qkv, qkv, dmix, cum, cum_t, lse, delta)


def _attn_dkv(qkv, dmix, cum, cum_t, lse_t, delta_t, *, name):
    t = qkv.shape[0]
    n = t // ATT_TILE
    tk = ATT_TILE

    def body(q_ref, k_ref, v_ref, do_ref, cq_ref, ck_ref, lse_ref, dl_ref, dk_ref, dv_ref, dc_ref, dk_acc, dv_acc, dc_acc):
        j = pl.program_id(0)
        i = pl.program_id(1)

        @pl.when(i == 0)
        def _():
            dk_acc[...] = jnp.zeros_like(dk_acc)
            dv_acc[...] = jnp.zeros_like(dv_acc)
            dc_acc[...] = jnp.zeros_like(dc_acc)

        def block(masked):
            mask = _causal(i, j, True) if masked else None
            for h in range(HEADS):
                hs = slice(HEAD_D * h, HEAD_D * (h + 1))
                qh = q_ref[:, hs]
                doh = do_ref[:, hs].astype(bf16)
                s_t = lax.dot_general(k_ref[:, hs] * ATT_SCALE, qh, _NT, preferred_element_type=f32)
                s_t = s_t + cq_ref[h:h + 1, :] - ck_ref[:, h:h + 1]
                if masked:
                    s_t = jnp.where(mask, s_t, NEG_BIG)
                p_t = jnp.exp(s_t - lse_ref[h:h + 1, :])
                dv_acc[:, hs] += jnp.dot(p_t.astype(bf16), doh, preferred_element_type=f32)
                dp_t = lax.dot_general(v_ref[:, hs], doh, _NT, preferred_element_type=f32)
                ds_t = p_t * (dp_t - dl_ref[h:h + 1, :])
                dk_acc[:, hs] += jnp.dot(ds_t.astype(bf16), qh, preferred_element_type=f32)
                dc_acc[:, h:h + 1] -= jnp.sum(ds_t, axis=-1, keepdims=True)

        @pl.when(i > j)
        def _():
            block(False)

        @pl.when(i == j)
        def _():
            block(True)

        @pl.when(i == n - 1)
        def _():
            dk_ref[...] = (dk_acc[...] * ATT_SCALE).astype(bf16)
            dv_ref[...] = dv_acc[...].astype(bf16)
            dc_ref[...] = dc_acc[...]

    rowq = pl.BlockSpec((HEADS, tk), lambda j, i: (0, jnp.maximum(i, j)))
    return pl.pallas_call(
        body, name=name, grid=(n, n),
        in_specs=[pl.BlockSpec((tk, FOX_W), lambda j, i: (jnp.maximum(i, j), 0)),
                  pl.BlockSpec((tk, FOX_W), lambda j, i: (j, 1)),
                  pl.BlockSpec((tk, FOX_W), lambda j, i: (j, 2)),
                  pl.BlockSpec((tk, FOX_W), lambda j, i: (jnp.maximum(i, j), 0)),
                  rowq, pl.BlockSpec((tk, LANES), lambda j, i: (j, 0)), rowq, rowq],
        out_specs=[pl.BlockSpec((tk, FOX_W), lambda j, i: (j, 0)), pl.BlockSpec((tk, FOX_W), lambda j, i: (j, 0)),
                   pl.BlockSpec((tk, LANES), lambda j, i: (j, 0))],
        out_shape=[jax.ShapeDtypeStruct((t, FOX_W), bf16), jax.ShapeDtypeStruct((t, FOX_W), bf16),
                   jax.ShapeDtypeStruct((t, LANES), f32)],
        scratch_shapes=[pltpu.VMEM((tk, FOX_W), f32), pltpu.VMEM((tk, FOX_W), f32), pltpu.VMEM((tk, LANES), f32)],
        compiler_params=_params(("arbitrary", "arbitrary")),
    )(qkv, qkv, qkv, dmix, cum_t, cum, lse_t, delta_t)


ATT_W = HEADS * LANES


def _data_lane(h):
    return HEAD_D * (h % 2)


def _extra_lane(h):
    return HEAD_D - _data_lane(h)


def _split3(x):
    hi = x.astype(bf16)
    rest = x - hi.astype(f32)
    mid = rest.astype(bf16)
    lo = (rest - mid.astype(f32)).astype(bf16)
    return hi, mid, lo


def _augment(pair, h, first, second, fill=0.0):
    rows = pair.shape[0]
    lane = lax.broadcasted_iota(jnp.int32, (rows, LANES), 1)
    base = _extra_lane(h)
    own = (lane < HEAD_D) if h % 2 == 0 else (lane >= HEAD_D)
    out = jnp.where(own, pair, jnp.full((rows, LANES), fill, bf16))
    for off, src in ((0, first), (3, second)):
        for q in range(3):
            val = src[q] if isinstance(src, tuple) else jnp.full((rows, 1), src, bf16)
            out = jnp.where(lane == base + off + q, val, out)
    return out


def _attn_prep_fwd(qkv, cum, *, tm, name):
    t = qkv.shape[0]

    def body(q_ref, k_ref, v_ref, c_ref, qa_ref, ka_ref, va_ref):
        for h in range(HEADS):
            pair = slice(LANES * (h // 2), LANES * (h // 2 + 1))
            hs = slice(LANES * h, LANES * (h + 1))
            c3 = _split3(c_ref[:, h:h + 1])
            qa_ref[:, hs] = _augment(q_ref[:, pair] * ATT_SCALE, h, c3, 1.0)
            ka_ref[:, hs] = _augment(k_ref[:, pair], h, 1.0, tuple(-p for p in c3))
            va_ref[:, hs] = _augment(v_ref[:, pair], h, 1.0, 1.0, fill=1.0)

    wide = pl.BlockSpec((tm, ATT_W), lambda i: (i, 0))
    out = jax.ShapeDtypeStruct((t, ATT_W), bf16)
    return pl.pallas_call(
        body, name=name, grid=(t // tm,),
        in_specs=[pl.BlockSpec((tm, FOX_W), lambda i: (i, 0)), pl.BlockSpec((tm, FOX_W), lambda i: (i, 1)),
                  pl.BlockSpec((tm, FOX_W), lambda i: (i, 2)), pl.BlockSpec((tm, LANES), lambda i: (i, 0))],
        out_specs=[wide] * 3, out_shape=[out] * 3, compiler_params=_params(("arbitrary",)),
    )(qkv, qkv, qkv, cum)


def _attn_prep_bwd(qkv, cum, lse, dmix, o, *, tm, name):
    t = qkv.shape[0]

    def body(q_ref, c_ref, l_ref, do_ref, o_ref, qa_ref, da_ref):
        for h in range(HEADS):
            pair = slice(LANES * (h // 2), LANES * (h // 2 + 1))
            src = slice(HEAD_D * h, HEAD_D * (h + 1))
            hs = slice(LANES * h, LANES * (h + 1))
            delta = jnp.sum(do_ref[:, src] * o_ref[:, src], axis=-1, keepdims=True)
            qa_ref[:, hs] = _augment(q_ref[:, pair] * ATT_SCALE, h,
                                     _split3(c_ref[:, h:h + 1] - l_ref[:, h:h + 1]), 1.0)
            da_ref[:, hs] = _augment(do_ref[:, pair].astype(bf16), h, tuple(-p for p in _split3(delta)), 0.0)

    wide = pl.BlockSpec((tm, ATT_W), lambda i: (i, 0))
    half = pl.BlockSpec((tm, FOX_W), lambda i: (i, 0))
    col = pl.BlockSpec((tm, LANES), lambda i: (i, 0))
    out = jax.ShapeDtypeStruct((t, ATT_W), bf16)
    return pl.pallas_call(
        body, name=name, grid=(t // tm,), in_specs=[half, col, col, half, half],
        out_specs=[wide] * 2, out_shape=[out] * 2, compiler_params=_params(("arbitrary",)),
    )(qkv, cum, lse, dmix, o)


def _attn_fwd2(q_aug, k_aug, v_aug, *, name, host=None):
    t = q_aug.shape[0]
    n = t // ATT_TILE
    tq = ATT_TILE

    def body(q_ref, k_ref, v_ref, o_ref, lse_ref, acc, m_s):
        i = pl.program_id(0)
        j = pl.program_id(1)

        @pl.when(j == 0)
        def _():
            acc[...] = jnp.zeros_like(acc)
            m_s[...] = jnp.full_like(m_s, NEG_BIG)

        def block(masked):
            mask = _causal(i, j, False) if masked else None
            for h in range(HEADS):
                hs = slice(LANES * h, LANES * (h + 1))
                s = lax.dot_general(q_ref[:, hs], k_ref[:, hs], _NT, preferred_element_type=f32)
                if masked:
                    s = jnp.where(mask, s, NEG_BIG)
                blocks = [s[:, LANES * b:LANES * (b + 1)] for b in range(tq // LANES)]
                m_old = m_s[h]
                m_new = jnp.maximum(m_old, jnp.broadcast_to(
                    jnp.max(functools.reduce(jnp.maximum, blocks), axis=-1, keepdims=True), (tq, LANES)))
                p = jnp.concatenate([jnp.exp(b - m_new) for b in blocks], axis=1).astype(bf16)
                acc[h] = jnp.exp(m_old - m_new) * acc[h] + jnp.dot(p, v_ref[:, hs], preferred_element_type=f32)
                m_s[h] = m_new

        @pl.when(j < i)
        def _():
            block(False)

        @pl.when(j == i)
        def _():
            block(True)
            lse_ref[...] = jnp.zeros_like(lse_ref)
            for h in range(HEADS):
                a = acc[h]
                l = a[:, _extra_lane(h):_extra_lane(h) + 1]
                o_ref[:, HEAD_D * h:HEAD_D * (h + 1)] = a[:, _data_lane(h):_data_lane(h) + HEAD_D] / l
                lse_ref[:, h:h + 1] = m_s[h][:, 0:1] + jnp.log(l)

    kv = pl.BlockSpec((tq, ATT_W), lambda i, j: (jnp.minimum(i, j), 0))
    return _hosted_call(
        host, body, name=name, grid=(n, n),
        in_specs=[pl.BlockSpec((tq, ATT_W), lambda i, j: (i, 0)), kv, kv],
        out_specs=[pl.BlockSpec((tq, FOX_W), lambda i, j: (i, 0)), pl.BlockSpec((tq, LANES), lambda i, j: (i, 0))],
        out_shape=[jax.ShapeDtypeStruct((t, FOX_W), f32), jax.ShapeDtypeStruct((t, LANES), f32)],
        scratch_shapes=[pltpu.VMEM((HEADS, tq, LANES), f32), pltpu.VMEM((HEADS, tq, LANES), f32)],
        compiler_params=_params(("arbitrary", "arbitrary")),
    )(q_aug, k_aug, v_aug)


def _attn_bwd(qb_aug, k_aug, v_aug, do_aug, *, name, host=None):
    t = qb_aug.shape[0]
    n = t // ATT_TILE
    tk = ATT_TILE

    def body(q_ref, k_ref, v_ref, do_ref, dq_ref, dcq_ref, dk_ref, dv_ref, dck_ref, dk_acc, dv_acc, dq_all):
        j = pl.program_id(0)
        i = pl.program_id(1)

        @pl.when(jnp.logical_and(i == 0, j == 0))
        def _():
            dq_all[...] = jnp.zeros_like(dq_all)

        @pl.when(i == 0)
        def _():
            dk_acc[...] = jnp.zeros_like(dk_acc)
            dv_acc[...] = jnp.zeros_like(dv_acc)

        def block(masked):
            mask = _causal(i, j, True) if masked else None
            for h in range(HEADS):
                hs = slice(LANES * h, LANES * (h + 1))
                qh = q_ref[:, hs]
                doh = do_ref[:, hs]
                kh = k_ref[:, hs]
                s_t = lax.dot_general(kh, qh, _NT, preferred_element_type=f32)
                if masked:
                    s_t = jnp.where(mask, s_t, NEG_BIG)
                p_t = jnp.exp(s_t)
                dv_acc[h] += jnp.dot(p_t.astype(bf16), doh, preferred_element_type=f32)
                dp_t = lax.dot_general(v_ref[:, hs], doh, _NT, preferred_element_type=f32)
                ds_t = (p_t * dp_t).astype(bf16)
                dk_acc[h] += jnp.dot(ds_t, qh, preferred_element_type=f32)
                dq_all[i, h] += lax.dot_general(ds_t, kh, _TN, preferred_element_type=f32)

        @pl.when(i > j)
        def _():
            block(False)

        @pl.when(i == j)
        def _():
            block(True)
            dcq_ref[...] = jnp.zeros_like(dcq_ref)
            for h in range(HEADS):
                a = dq_all[j, h]
                dq_ref[:, HEAD_D * h:HEAD_D * (h + 1)] = (
                    a[:, _data_lane(h):_data_lane(h) + HEAD_D] * ATT_SCALE).astype(bf16)
                dcq_ref[:, h:h + 1] = a[:, _extra_lane(h):_extra_lane(h) + 1]

        @pl.when(i == n - 1)
        def _():
            dck_ref[...] = jnp.zeros_like(dck_ref)
            for h in range(HEADS):
                a = dk_acc[h]
                cols = slice(_data_lane(h), _data_lane(h) + HEAD_D)
                dk_ref[:, HEAD_D * h:HEAD_D * (h + 1)] = a[:, cols].astype(bf16)
                dv_ref[:, HEAD_D * h:HEAD_D * (h + 1)] = dv_acc[h][:, cols].astype(bf16)
                dck_ref[:, h:h + 1] = -a[:, _extra_lane(h) + 3:_extra_lane(h) + 4]

    own = pl.BlockSpec((tk, ATT_W), lambda j, i: (j, 0))
    qs = pl.BlockSpec((tk, ATT_W), lambda j, i: (jnp.maximum(i, j), 0))
    half = pl.BlockSpec((tk, FOX_W), lambda j, i: (j, 0))
    col = pl.BlockSpec((tk, LANES), lambda j, i: (j, 0))
    return _hosted_call(
        host, body, name=name, grid=(n, n), in_specs=[qs, own, own, qs],
        out_specs=[half, col, half, half, col],
        out_shape=[jax.ShapeDtypeStruct((t, FOX_W), bf16), jax.ShapeDtypeStruct((t, LANES), f32),
                   jax.ShapeDtypeStruct((t, FOX_W), bf16), jax.ShapeDtypeStruct((t, FOX_W), bf16),
                   jax.ShapeDtypeStruct((t, LANES), f32)],
        scratch_shapes=[pltpu.VMEM((HEADS, tk, LANES), f32), pltpu.VMEM((HEADS, tk, LANES), f32),
                        pltpu.VMEM((n, HEADS, tk, LANES), f32)],
        compiler_params=_params(("arbitrary", "arbitrary")),
    )(qb_aug, k_aug, v_aug, do_aug)


def _attn_dq2(qb_aug, k_aug, v_aug, do_aug, *, name, host=None):
    t = qb_aug.shape[0]
    n = t // ATT_TILE
    tq = ATT_TILE

    def body(q_ref, k_ref, v_ref, do_ref, dq_ref, dc_ref, acc):
        i = pl.program_id(0)
        j = pl.program_id(1)

        @pl.when(j == 0)
        def _():
            acc[...] = jnp.zeros_like(acc)

        def block(masked):
            mask = _causal(i, j, False) if masked else None
            for h in range(HEADS):
                hs = slice(LANES * h, LANES * (h + 1))
                kh = k_ref[:, hs]
                s = lax.dot_general(q_ref[:, hs], kh, _NT, preferred_element_type=f32)
                if masked:
                    s = jnp.where(mask, s, NEG_BIG)
                dp = lax.dot_general(do_ref[:, hs], v_ref[:, hs], _NT, preferred_element_type=f32)
                ds = (jnp.exp(s) * dp).astype(bf16)
                acc[h] += jnp.dot(ds, kh, preferred_element_type=f32)

        @pl.when(j < i)
        def _():
            block(False)

        @pl.when(j == i)
        def _():
            block(True)
            dc_ref[...] = jnp.zeros_like(dc_ref)
            for h in range(HEADS):
                a = acc[h]
                dq_ref[:, HEAD_D * h:HEAD_D * (h + 1)] = (
                    a[:, _data_lane(h):_data_lane(h) + HEAD_D] * ATT_SCALE).astype(bf16)
                dc_ref[:, h:h + 1] = a[:, _extra_lane(h):_extra_lane(h) + 1]

    own = pl.BlockSpec((tq, ATT_W), lambda i, j: (i, 0))
    kv = pl.BlockSpec((tq, ATT_W), lambda i, j: (jnp.minimum(i, j), 0))
    return _hosted_call(
        host, body, name=name, grid=(n, n), in_specs=[own, kv, kv, own],
        out_specs=[pl.BlockSpec((tq, FOX_W), lambda i, j: (i, 0)), pl.BlockSpec((tq, LANES), lambda i, j: (i, 0))],
        out_shape=[jax.ShapeDtypeStruct((t, FOX_W), bf16), jax.ShapeDtypeStruct((t, LANES), f32)],
        scratch_shapes=[pltpu.VMEM((HEADS, tq, LANES), f32)],
        compiler_params=_params(("arbitrary", "arbitrary")),
    )(qb_aug, k_aug, v_aug, do_aug)


def _attn_dkv2(qb_aug, k_aug, v_aug, do_aug, *, name, host=None):
    t = qb_aug.shape[0]
    n = t // ATT_TILE
    tk = ATT_TILE

    def body(q_ref, k_ref, v_ref, do_ref, dk_ref, dv_ref, dc_ref, dk_acc, dv_acc):
        j = pl.program_id(0)
        i = pl.program_id(1)

        @pl.when(i == 0)
        def _():
            dk_acc[...] = jnp.zeros_like(dk_acc)
            dv_acc[...] = jnp.zeros_like(dv_acc)

        def block(masked):
            mask = _causal(i, j, True) if masked else None
            for h in range(HEADS):
                hs = slice(LANES * h, LANES * (h + 1))
                qh = q_ref[:, hs]
                doh = do_ref[:, hs]
                s_t = lax.dot_general(k_ref[:, hs], qh, _NT, preferred_element_type=f32)
                if masked:
                    s_t = jnp.where(mask, s_t, NEG_BIG)
                p_t = jnp.exp(s_t)
                dv_acc[h] += jnp.dot(p_t.astype(bf16), doh, preferred_element_type=f32)
                dp_t = lax.dot_general(v_ref[:, hs], doh, _NT, preferred_element_type=f32)
                dk_acc[h] += jnp.dot((p_t * dp_t).astype(bf16), qh, preferred_element_type=f32)

        @pl.when(i > j)
        def _():
            block(False)

        @pl.when(i == j)
        def _():
            block(True)

        @pl.when(i == n - 1)
        def _():
            dc_ref[...] = jnp.zeros_like(dc_ref)
            for h in range(HEADS):
                a = dk_acc[h]
                cols = slice(_data_lane(h), _data_lane(h) + HEAD_D)
                dk_ref[:, HEAD_D * h:HEAD_D * (h + 1)] = a[:, cols].astype(bf16)
                dv_ref[:, HEAD_D * h:HEAD_D * (h + 1)] = dv_acc[h][:, cols].astype(bf16)
                dc_ref[:, h:h + 1] = -a[:, _extra_lane(h) + 3:_extra_lane(h) + 4]

    own = pl.BlockSpec((tk, ATT_W), lambda j, i: (j, 0))
    qs = pl.BlockSpec((tk, ATT_W), lambda j, i: (jnp.maximum(i, j), 0))
    half = pl.BlockSpec((tk, FOX_W), lambda j, i: (j, 0))
    return _hosted_call(
        host, body, name=name, grid=(n, n), in_specs=[qs, own, own, qs],
        out_specs=[half, half, pl.BlockSpec((tk, LANES), lambda j, i: (j, 0))],
        out_shape=[jax.ShapeDtypeStruct((t, FOX_W), bf16), jax.ShapeDtypeStruct((t, FOX_W), bf16),
                   jax.ShapeDtypeStruct((t, LANES), f32)],
        scratch_shapes=[pltpu.VMEM((HEADS, tk, LANES), f32), pltpu.VMEM((HEADS, tk, LANES), f32)],
        compiler_params=_params(("arbitrary", "arbitrary")),
    )(qb_aug, k_aug, v_aug, do_aug)


LRU_CHUNK = 64
LRU_G = 256
SUB = 8


def _row_ids(n):
    return lax.broadcasted_iota(jnp.int32, (n, LRU_G), 0)


def _shift_rows_down(ext, s):
    return pltpu.roll(ext, s, axis=0)[SUB:, :]


def _shift_rows_up(ext, s, n):
    return pltpu.roll(ext, ext.shape[0] - s, axis=0)[:n, :]


def _lru_gates(u, wa_ref, ba_ref, wx_ref, bx_ref, sp):
    ub = u.astype(bf16)
    r = _sigmoid(jnp.dot(ub, wa_ref[...], preferred_element_type=f32) + ba_ref[...])
    gi = _sigmoid(jnp.dot(ub, wx_ref[...], preferred_element_type=f32) + bx_ref[...])
    log_a = -LRU_C * r * sp
    a = jnp.exp(log_a)
    s = jnp.sqrt(_one_minus_exp(2.0 * log_a))
    return r, gi, a, s


def _conv_window(lx_ref, r0, ci):
    cur = lx_ref[pl.ds(r0, LRU_CHUNK), :]
    p0 = pl.multiple_of(jnp.maximum(r0 - SUB, 0), SUB)
    prev = jnp.where(ci > 0, lx_ref[pl.ds(p0, SUB), :], 0.0)
    return cur, jnp.concatenate([prev, cur], axis=0)


def _lru_fwd(zl, conv_w, conv_b, wa, ba, wx, bx, lam, *, name, host=None):
    t = zl.shape[0]
    n_chunk = t // LRU_CHUNK

    def body(lx_ref, lg_ref, cw_ref, cb_ref, wa_ref, ba_ref, wx_ref, bx_ref, lam_ref, u_ref, h_ref, y_ref):
        sp = _softplus(-lam_ref[...])
        rows = _row_ids(SUB)

        def chunk(ci, hc):
            r0 = pl.multiple_of(ci * LRU_CHUNK, LRU_CHUNK)
            cur, ext = _conv_window(lx_ref, r0, ci)
            u = cb_ref[...] + cw_ref[3:4, :] * cur
            for k in range(3):
                u = u + cw_ref[k:k + 1, :] * _shift_rows_down(ext, 3 - k)
            r, gi, a, s = _lru_gates(u, wa_ref, ba_ref, wx_ref, bx_ref, sp)
            b = s * (gi * u)
            tiles = []
            for q in range(LRU_CHUNK // SUB):
                ta = a[SUB * q:SUB * (q + 1), :]
                tb = b[SUB * q:SUB * (q + 1), :]
                for d in (1, 2, 4):
                    a_sh = jnp.where(rows >= d, pltpu.roll(ta, d, axis=0), 1.0)
                    b_sh = jnp.where(rows >= d, pltpu.roll(tb, d, axis=0), 0.0)
                    tb = ta * b_sh + tb
                    ta = ta * a_sh
                hq = tb + ta * hc
                hc = hq[SUB - 1:SUB, :]
                tiles.append(hq)
            h = jnp.concatenate(tiles, axis=0)
            u_ref[pl.ds(r0, LRU_CHUNK), :] = u
            h_ref[pl.ds(r0, LRU_CHUNK), :] = h
            gel, _ = _gelu_and_grad(lg_ref[pl.ds(r0, LRU_CHUNK), :])
            y_ref[pl.ds(r0, LRU_CHUNK), :] = gel * h
            return hc

        lax.fori_loop(0, n_chunk, chunk, jnp.zeros((1, LRU_G), f32))

    seq = lambda cb: pl.BlockSpec((t, LRU_G), lambda c, cb=cb: (0, c + cb))
    rowc = pl.BlockSpec((1, LRU_G), lambda c: (0, c))
    diag = pl.BlockSpec((LRU_G, LRU_G), lambda c: (c, c))
    out = jax.ShapeDtypeStruct((t, LRU_W), f32)
    return _hosted_call(
        host, body, name=name, grid=(LRU_W // LRU_G,),
        in_specs=[seq(0), seq(LRU_W // LRU_G), pl.BlockSpec((4, LRU_G), lambda c: (0, c)),
                  rowc, diag, rowc, diag, rowc, rowc],
        out_specs=[seq(0)] * 3, out_shape=[out] * 3,
        compiler_params=_params(("arbitrary",)),
    )(zl, zl, conv_w, conv_b, wa, ba, wx, bx, lam)


def _lru_bwd(dmix, zl, u_all, h_all, conv_w, wa, ba, wx, bx, lam, *, name, host=None):
    t = zl.shape[0]
    n_chunk = t // LRU_CHUNK

    def body(dy_ref, lx_ref, lg_ref, u_ref, h_ref, cw_ref, wa_ref, ba_ref, wx_ref, bx_ref, lam_ref,
             dlx_ref, dlg_ref, dcw_ref, dcb_ref, dba_ref, dbx_ref, dlam_ref, dwa_ref, dwx_ref, dpr_s, dpx_s):
        lam_v = lam_ref[...]
        sp = _softplus(-lam_v)
        rows = _row_ids(SUB)
        rows_c = _row_ids(LRU_CHUNK)
        zero_row = jnp.zeros((1, LRU_G), f32)

        def chunk(step, carry):
            dh_c, a_next0, du_next, dsp, dba, dbx, dcb, dw0, dw1, dw2, dw3 = carry
            ci = n_chunk - 1 - step
            r0 = pl.multiple_of(ci * LRU_CHUNK, LRU_CHUNK)
            sl = pl.ds(r0, LRU_CHUNK)
            u = u_ref[sl, :]
            r, gi, a, s = _lru_gates(u, wa_ref, ba_ref, wx_ref, bx_ref, sp)
            h = h_ref[sl, :]
            p0 = pl.multiple_of(jnp.maximum(r0 - SUB, 0), SUB)
            h_before = jnp.where(ci > 0, h_ref[pl.ds(p0, SUB), :], 0.0)[SUB - 1:SUB, :]
            h_prev = jnp.where(rows_c == 0, h_before, pltpu.roll(h, 1, axis=0))
            gel, dgel = _gelu_and_grad(lg_ref[sl, :])
            dy = dy_ref[sl, :]
            dlg_ref[sl, :] = (dy * h * dgel).astype(bf16)
            g_in = dy * gel
            a_next = jnp.where(rows_c == LRU_CHUNK - 1, a_next0, pltpu.roll(a, LRU_CHUNK - 1, axis=0))
            tiles = [None] * (LRU_CHUNK // SUB)
            for q in reversed(range(LRU_CHUNK // SUB)):
                ta = a_next[SUB * q:SUB * (q + 1), :]
                tb = g_in[SUB * q:SUB * (q + 1), :]
                for d in (1, 2, 4):
                    a_sh = jnp.where(rows < SUB - d, pltpu.roll(ta, SUB - d, axis=0), 1.0)
                    b_sh = jnp.where(rows < SUB - d, pltpu.roll(tb, SUB - d, axis=0), 0.0)
                    tb = ta * b_sh + tb
                    ta = ta * a_sh
                dhq = tb + ta * dh_c
                dh_c = dhq[0:1, :]
                tiles[q] = dhq
            dh = jnp.concatenate(tiles, axis=0)
            da = dh * h_prev
            ds = dh * gi * u
            dgi = dh * s * u
            du = dh * s * gi
            dlog_a = da * a - ds * (a * a) / s
            dr = dlog_a * (-LRU_C * sp)
            dsp = dsp + jnp.sum(dlog_a * (-LRU_C * r), axis=0, keepdims=True)
            dpr = dr * r * (1.0 - r)
            dpx = dgi * gi * (1.0 - gi)
            dprb = dpr.astype(bf16)
            dpxb = dpx.astype(bf16)
            dpr_s[sl, :] = dprb
            dpx_s[sl, :] = dpxb
            du = du + (lax.dot_general(dprb, wa_ref[...], _NT, preferred_element_type=f32)
                       + lax.dot_general(dpxb, wx_ref[...], _NT, preferred_element_type=f32))
            dba = dba + jnp.sum(dpr, axis=0, keepdims=True)
            dbx = dbx + jnp.sum(dpx, axis=0, keepdims=True)
            dcb = dcb + jnp.sum(du, axis=0, keepdims=True)
            du_ext = jnp.concatenate([du, du_next], axis=0)
            dlx = cw_ref[3:4, :] * du
            for k in range(3):
                dlx = dlx + cw_ref[k:k + 1, :] * _shift_rows_up(du_ext, 3 - k, LRU_CHUNK)
            dlx_ref[sl, :] = dlx.astype(bf16)
            cur, ext = _conv_window(lx_ref, r0, ci)
            dws = [dw0, dw1, dw2, dw3 + jnp.sum(du * cur, axis=0, keepdims=True)]
            for k in range(3):
                dws[k] = dws[k] + jnp.sum(du * _shift_rows_down(ext, 3 - k), axis=0, keepdims=True)
            return (dh_c, a[0:1, :], du[0:SUB, :], dsp, dba, dbx, dcb, dws[0], dws[1], dws[2], dws[3])

        init = (zero_row, zero_row, jnp.zeros((SUB, LRU_G), f32)) + (zero_row,) * 8
        out = lax.fori_loop(0, n_chunk, chunk, init)
        _, _, _, dsp, dba, dbx, dcb, dw0, dw1, dw2, dw3 = out
        dlam_ref[...] = dsp * (-_sigmoid(-lam_v))
        dba_ref[...] = dba
        dbx_ref[...] = dbx
        dcb_ref[...] = dcb
        dcw_ref[...] = jnp.concatenate([dw0, dw1, dw2, dw3], axis=0)
        ub = u_ref[...].astype(bf16)
        dwa_ref[...] = lax.dot_general(ub, dpr_s[...], _TN, preferred_element_type=f32)
        dwx_ref[...] = lax.dot_general(ub, dpx_s[...], _TN, preferred_element_type=f32)

    seq = lambda cb: pl.BlockSpec((t, LRU_G), lambda c, cb=cb: (0, c + cb))
    rowc = pl.BlockSpec((1, LRU_G), lambda c: (0, c))
    diag = pl.BlockSpec((LRU_G, LRU_G), lambda c: (c, c))
    gate_out = pl.BlockSpec((None, LRU_G, LRU_G), lambda c: (c, 0, 0))
    row_shape = jax.ShapeDtypeStruct((1, LRU_W), f32)
    return _hosted_call(
        host, body, name=name, grid=(LRU_W // LRU_G,),
        in_specs=[seq(LRU_W // LRU_G), seq(0), seq(LRU_W // LRU_G), seq(0), seq(0),
                  pl.BlockSpec((4, LRU_G), lambda c: (0, c)),
                  diag, rowc, diag, rowc, rowc],
        out_specs=[seq(0), seq(0), pl.BlockSpec((4, LRU_G), lambda c: (0, c)), rowc, rowc, rowc, rowc,
                   gate_out, gate_out],
        out_shape=[jax.ShapeDtypeStruct((t, LRU_W), bf16)] * 2
        + [jax.ShapeDtypeStruct((4, LRU_W), f32)] + [row_shape] * 4
        + [jax.ShapeDtypeStruct((LRU_W // LRU_G, LRU_G, LRU_G), f32)] * 2,
        scratch_shapes=[pltpu.VMEM((t, LRU_G), bf16), pltpu.VMEM((t, LRU_G), bf16)],
        compiler_params=_params(("arbitrary",)),
    )(dmix, zl, zl, u_all, h_all, conv_w, wa, ba, wx, bx, lam)


def _block_diag(w):
    eye = jnp.eye(HEADS, dtype=w.dtype)
    return jnp.einsum("hij,hk->hikj", w, eye).reshape(LRU_W, LRU_W)


def _diag_blocks(dw):
    per = dw.shape[1] // HEAD_D
    blocks = [dw[:, HEAD_D * b:HEAD_D * (b + 1), HEAD_D * b:HEAD_D * (b + 1)] for b in range(per)]
    return jnp.stack(blocks, axis=1).reshape(HEADS, HEAD_D, HEAD_D)


def _local_step(x, target, sent, small, *, tm=512, tm_ffn=1024):
    t = x.shape[0]
    ones = jnp.ones((1, D_MODEL), f32)
    zeros = jnp.zeros((1, D_MODEL), f32)
    ln1 = (small["ln1_g"], small["ln1_b"])
    ln2 = (small["ln2_g"], small["ln2_b"])
    ln3 = (small["ln3_g"], small["ln3_b"])

    xh1, rs1, hg1, hu1, wg1, wu1, wd1, w_in_g, wd2 = _ffn1_fwd_gathering(
        x, (sent["ffn1_w_gate"], sent["ffn1_w_up"], sent["ffn1_w_down"]),
        _Exchange([sent["w_in"], sent["ffn2_w_down"]], gather=True), tm=tm_ffn, name="ffn1_fwd")
    w_in = jnp.pad(w_in_g.transpose(1, 0, 2).reshape(D_MODEL, IN_COLS), ((0, 0), (0, 21 * LANES - IN_COLS)))
    qkv, zl, zfg, w_out_g, conv_w_g = _in_proj(xh1, ln1[0], ln1[1], w_in, tm=tm, name="in_proj",
                                               host=_Exchange([sent["w_out"], sent["conv_w"]], gather=True))
    w_out = w_out_g.reshape(D_MODEL, D_MODEL)
    conv_w = conv_w_g.transpose(1, 0, 2).reshape(4, LRU_W)
    bfg = jnp.pad(small["b_forget"], ((0, 0), (0, LANES - HEADS)))
    cum = _cum_fwd(zfg, bfg, name="cum_fwd")
    q_aug, k_aug, v_aug = _attn_prep_fwd(qkv, cum, tm=tm, name="attn_prep_fwd")
    o, lse, wg2 = _attn_fwd2(q_aug, k_aug, v_aug, name="attn_fwd",
                             host=_Exchange([sent["ffn2_w_gate"]], gather=True))
    wa_bd = _block_diag(small["rg_wa"]).astype(bf16)
    wx_bd = _block_diag(small["rg_wx"]).astype(bf16)
    ba = small["rg_ba"].reshape(1, LRU_W)
    bx = small["rg_bx"].reshape(1, LRU_W)
    u, h, lru, wu2 = _lru_fwd(zl, conv_w, small["conv_b"], wa_bd, ba, wx_bd, bx, small["lru_lambda"],
                              name="lru_fwd", host=_Exchange([sent["ffn2_w_up"]], gather=True))
    xh2, rs2 = _mmln([(o, 0, FOX_W, w_out, 0, D_MODEL, "nn"), (lru, 0, LRU_W, w_out, 1, D_MODEL, "nn")],
                     tm=tm, name="mix_fwd", resid=("affine", xh1) + ln1, resid_scale=ALPHA, epi="ln_fwd")
    xh3, rs3, hg2, hu2 = _ffn_fwd(xh2, ln2[0], ln2[1], wg2, wu2, wd2, tm=tm_ffn, name="ffn2_fwd")

    dpre3, sq_rows, g_ln3g, g_ln3b = _loss_bwd(xh3, rs3, ln3[0], ln3[1], target, tm=tm, name="loss_bwd")
    dpre2, g_ln2g, g_ln2b, dhg2, dhu2, a2 = _ffn_bwd(dpre3, hg2, hu2, wg2, wu2, wd2,
                                                     (xh2, rs2, ln2[0]), tm=tm_ffn, name="ffn2_bwd")
    wgrad = dict(out_dtype=bf16, tm=D_MODEL, mb=1, tn=FF_TILE, nb=4, tk=512, pair=True)
    wdgrad = dict(out_dtype=bf16, tm=512, mb=4, tn=D_MODEL, nb=1, tk=512, out_scale=0.5, pair=True)
    between_chips = functools.partial(_Exchange, gather=False, chips=True)
    g_wg2 = _mm_tn(xh2, dhg2, name="g_wg2", affine=ln2, **wgrad)
    g_wu2 = _mm_tn(xh2, dhu2, name="g_wu2", affine=ln2, **wgrad)
    g_wd2 = _mm_tn(a2, dpre3, name="g_wd2", **wdgrad)

    dmix = _mmln([(dpre2, 0, D_MODEL, w_out, 0, D_MODEL, "nt")], tm=tm, name="dmix_bwd")
    g_wout_a = _mm(o, dpre2, mode="tn", out_dtype=bf16, tm=512, tn=D_MODEL, tk=512, name="g_wout_fox")
    g_wout_b = _mm(lru, dpre2, mode="tn", out_dtype=bf16, tm=512, tn=D_MODEL, tk=512, name="g_wout_lru")
    dlx, dlg, g_cw, g_cb, g_ba, g_bx, g_lam, g_wa4, g_wx4, *p_wg2 = _lru_bwd(
        dmix, zl, u, h, conv_w, wa_bd, ba, wx_bd, bx, small["lru_lambda"], name="lru_bwd",
        host=between_chips([g_wg2]))
    p_wg2 = p_wg2[0]
    qb_aug, do_aug = _attn_prep_bwd(qkv, cum, lse, dmix, o, tm=tm, name="attn_prep_bwd")
    dq, dcum_q, dk, dv, dcum_k, p_wu2, p_wd2 = _attn_bwd(qb_aug, k_aug, v_aug, do_aug, name="attn_bwd",
                                                         host=between_chips([g_wu2, g_wd2]))
    g_wout_blocked = jnp.concatenate([g_wout_a, g_wout_b], axis=0).reshape(N_DEV, D_MODEL // N_DEV, D_MODEL)
    dfg, g_bf = _cum_bwd(dcum_q, dcum_k, zfg, bfg, name="cum_bwd")

    dz = [(dq, 0, 512), (dk, 1, 512), (dv, 2, 512), (dlx, 3, 512), (dlg, 4, 512), (dfg, 20, LANES)]
    dpre1, g_ln1g, g_ln1b = _mmln(
        [(arr, 0, w, w_in, cb, w, "nt") for (arr, cb, w) in dz],
        tm=tm, name="dx1_bwd", resid=("plain", dpre2), resid_scale=ALPHA, epi="ln_bwd", ln=(xh1, rs1, ln1[0]))
    g_win_main = _mm_tn(xh1, [arr for arr, _, _ in dz[:5]], out_dtype=bf16, tm=D_MODEL, mb=1, tn=512, nb=5, tk=512,
                        name="g_win", affine=ln1, out_blocked=True)
    g_win = [g_win_main[n] for n in range(5)] + [
        _mm(xh1, dfg, mode="tn", out_dtype=bf16, tm=D_MODEL, tn=LANES, tk=512, name="g_win_fg", affine=ln1)]
    g_win_full = jnp.concatenate([g[:, :w] for g, (_, _, w) in zip(g_win, dz)], axis=1)[:, :IN_COLS]
    g_win_blocked = g_win_full.reshape(D_MODEL, N_DEV, IN_SHARD).transpose(1, 0, 2)
    dhg1, dhu1, a1, p_win, p_wout = _ffn_bwd_act(dpre1, hg1, hu1, wd1, tm=tm_ffn, name="ffn1_bwd_act",
                                                 host=_Exchange([g_win_blocked, g_wout_blocked], gather=False))
    small_g = {
        "ln1_g": g_ln1g, "ln1_b": g_ln1b, "b_forget": g_bf[:, :HEADS], "conv_w": g_cw, "conv_b": g_cb,
        "rg_wa": _diag_blocks(g_wa4), "rg_ba": g_ba.reshape(HEADS, HEAD_D),
        "rg_wx": _diag_blocks(g_wx4), "rg_bx": g_bx.reshape(HEADS, HEAD_D), "lru_lambda": g_lam,
        "ln2_g": g_ln2g, "ln2_b": g_ln2b, "ln3_g": g_ln3g, "ln3_b": g_ln3b,
    }
    small_g["loss"] = (0.5 / D_MODEL) * jnp.sum(sq_rows, keepdims=True)
    pieces = [small_g[n].reshape(-1) for n in PACKED]
    packed = jnp.concatenate(pieces + [jnp.zeros((PACK_ROWS * LANES - sum(p.shape[0] for p in pieces),), f32)])
    g_wg1, all_packed = _mm_tn(x, dhg1, name="g_wg1",
                               host=_Exchange([packed.reshape(PACK_ROWS, LANES)], gather=True), **wgrad)
    g_wu1, p_wg1 = _mm_tn(x, dhu1, name="g_wu1", host=between_chips([g_wg1]), **wgrad)
    g_wd1, p_wu1 = _mm_tn(a1, dpre1, name="g_wd1", host=between_chips([g_wu1]), **wdgrad)
    grad_x, p_wd1 = _ffn_bwd_dx(dpre1, dhg1, dhu1, wg1, wu1, tm=tm_ffn, name="ffn1_bwd_dx",
                                host=between_chips([g_wd1]))
    parts = {
        "ffn1_w_gate": p_wg1, "ffn1_w_up": p_wu1, "ffn1_w_down": p_wd1, "w_in": p_win, "w_out": p_wout,
        "ffn2_w_gate": p_wg2, "ffn2_w_up": p_wu2, "ffn2_w_down": p_wd2,
    }
    return sq_rows, grad_x, parts, all_packed, {n: small_g[n].shape for n in PACKED}


def _adam_math(w, g, m, v):
    m2 = ADAM_B1 * m + (1.0 - ADAM_B1) * g
    v2 = ADAM_B2 * v + (1.0 - ADAM_B2) * (g * g)
    m_hat = m2 / (1.0 - ADAM_B1 ** ADAM_STEP)
    v_hat = v2 / (1.0 - ADAM_B2 ** ADAM_STEP)
    delta = -ADAM_LR * (m_hat / (jnp.sqrt(v_hat) + ADAM_EPS) + ADAM_WD * w)
    return delta, m2, v2


ADAM_TILE_ELEMS = 128 * 1024


def _adamw_big(parts, w, m, v, *, name):
    r, c = w.shape
    n_parts = parts.shape[0]
    tr = max(d for d in range(8, r + 1, 8) if r % d == 0 and d * c <= ADAM_TILE_ELEMS)

    def body(p_ref, w_ref, m_ref, v_ref, g_ref, d_ref, m2_ref, v2_ref):
        g = p_ref[0].astype(f32)
        for q in range(1, n_parts):
            g = g + p_ref[q].astype(f32)
        d, m2, v2 = _adam_math(w_ref[...], g, m_ref[...], v_ref[...])
        g_ref[...] = g
        d_ref[...] = d
        m2_ref[...] = m2
        v2_ref[...] = v2

    blk = pl.BlockSpec((tr, c), lambda i: (i, 0))
    return pl.pallas_call(
        body, name=name, grid=(r // tr,),
        in_specs=[pl.BlockSpec((n_parts, tr, c), lambda i: (0, i, 0)), blk, blk, blk],
        out_specs=[blk] * 4, out_shape=[jax.ShapeDtypeStruct((r, c), f32)] * 4,
        compiler_params=_params(("arbitrary",)),
    )(parts, w, m, v)


def _adamw_small(items, *, name):
    n = len(items)

    def body(*refs):
        ins, outs = refs[:4 * n], refs[4 * n:]
        for k in range(n):
            g, w, m, v = (ins[4 * k + q][...] for q in range(4))
            d, m2, v2 = _adam_math(w, g, m, v)
            outs[3 * k][...] = d
            outs[3 * k + 1][...] = m2
            outs[3 * k + 2][...] = v2

    vm = pl.BlockSpec(memory_space=pltpu.VMEM)
    flat = [a for item in items for a in item]
    out_shape = [jax.ShapeDtypeStruct(item[1].shape, f32) for item in items for _ in range(3)]
    return pl.pallas_call(
        body, name=name, in_specs=[vm] * (4 * n), out_specs=[vm] * (3 * n), out_shape=out_shape,
    )(*flat)


def _sum_parts(parts, *, name):
    def body(p_ref, o_ref):
        acc = p_ref[0]
        for q in range(1, N_DEV):
            acc = acc + p_ref[q]
        o_ref[...] = acc

    vm = pl.BlockSpec(memory_space=pltpu.VMEM)
    return pl.pallas_call(
        body, name=name, in_specs=[vm], out_specs=vm, out_shape=jax.ShapeDtypeStruct(parts.shape[1:], f32),
    )(parts)


WEIGHTS = ["ffn1_w_gate", "ffn1_w_up", "ffn1_w_down", "ln1_g", "ln1_b", "w_in", "b_forget", "conv_w", "conv_b",
           "rg_wa", "rg_ba", "rg_wx", "rg_bx", "lru_lambda", "w_out", "ln2_g", "ln2_b",
           "ffn2_w_gate", "ffn2_w_up", "ffn2_w_down", "ln3_g", "ln3_b"]
BIG = ["ffn1_w_gate", "ffn1_w_up", "ffn1_w_down", "w_in", "w_out", "ffn2_w_gate", "ffn2_w_up", "ffn2_w_down"]
PACKED = ["ln1_g", "ln1_b", "ln2_g", "ln2_b", "ln3_g", "ln3_b", "conv_b", "rg_ba", "rg_bx", "lru_lambda",
          "conv_w", "rg_wa", "rg_wx", "b_forget", "loss"]
PACK_ROWS = 600


def _two_d(a):
    return a.reshape((-1, a.shape[-1]))


def _transport(a):
    return _two_d(a)


def kernel(x, ffn1_w_gate, ffn1_w_up, ffn1_w_down, ln1_g, ln1_b, w_in, b_forget, conv_w, conv_b, rg_wa, rg_ba, rg_wx, rg_bx, lru_lambda, w_out, ln2_g, ln2_b, ffn2_w_gate, ffn2_w_up, ffn2_w_down, ln3_g, ln3_b, loss_target, m_ffn1_w_gate, m_ffn1_w_up, m_ffn1_w_down, m_ln1_g, m_ln1_b, m_w_in, m_b_forget, m_conv_w, m_conv_b, m_rg_wa, m_rg_ba, m_rg_wx, m_rg_bx, m_lru_lambda, m_w_out, m_ln2_g, m_ln2_b, m_ffn2_w_gate, m_ffn2_w_up, m_ffn2_w_down, m_ln3_g, m_ln3_b, v_ffn1_w_gate, v_ffn1_w_up, v_ffn1_w_down, v_ln1_g, v_ln1_b, v_w_in, v_b_forget, v_conv_w, v_conv_b, v_rg_wa, v_rg_ba, v_rg_wx, v_rg_bx, v_lru_lambda, v_w_out, v_ln2_g, v_ln2_b, v_ffn2_w_gate, v_ffn2_w_up, v_ffn2_w_down, v_ln3_g, v_ln3_b):
    w_args = (ffn1_w_gate, ffn1_w_up, ffn1_w_down, ln1_g, ln1_b, w_in, b_forget, conv_w, conv_b, rg_wa, rg_ba, rg_wx, rg_bx, lru_lambda, w_out, ln2_g, ln2_b, ffn2_w_gate, ffn2_w_up, ffn2_w_down, ln3_g, ln3_b)
    m_args = (m_ffn1_w_gate, m_ffn1_w_up, m_ffn1_w_down, m_ln1_g, m_ln1_b, m_w_in, m_b_forget, m_conv_w, m_conv_b, m_rg_wa, m_rg_ba, m_rg_wx, m_rg_bx, m_lru_lambda, m_w_out, m_ln2_g, m_ln2_b, m_ffn2_w_gate, m_ffn2_w_up, m_ffn2_w_down, m_ln3_g, m_ln3_b)
    v_args = (v_ffn1_w_gate, v_ffn1_w_up, v_ffn1_w_down, v_ln1_g, v_ln1_b, v_w_in, v_b_forget, v_conv_w, v_conv_b, v_rg_wa, v_rg_ba, v_rg_wx, v_rg_bx, v_lru_lambda, v_w_out, v_ln2_g, v_ln2_b, v_ffn2_w_gate, v_ffn2_w_up, v_ffn2_w_down, v_ln3_g, v_ln3_b)
    w = dict(zip(WEIGHTS, w_args))
    m = dict(zip(WEIGHTS, m_args))
    v = dict(zip(WEIGHTS, v_args))
    me = 4 * lax.axis_index("x") + 2 * lax.axis_index("y") + lax.axis_index("c")

    sent = {n: _transport(w[n]).astype(bf16) for n in BIG}
    sent["conv_w"] = _two_d(w["conv_w"])
    small = {n: w[n] for n in ("ln1_g", "ln1_b", "ln2_g", "ln2_b", "ln3_g", "ln3_b", "b_forget", "conv_b",
                               "lru_lambda")}
    small.update({n: w[n][0] for n in ("rg_wa", "rg_ba", "rg_wx", "rg_bx")})

    sq_rows, grad_x, parts, all_packed, small_shapes = _local_step(x[0], loss_target[0], sent, small)

    total = _sum_parts(all_packed, name="sum_small_grads").reshape(-1)
    grads, off = {}, 0
    for n in PACKED:
        size = math.prod(small_shapes[n])
        grads[n] = total[off:off + size].reshape(small_shapes[n])
        off += size
    loss = grads.pop("loss").reshape(())
    grads["conv_w"] = lax.dynamic_slice_in_dim(grads["conv_w"], me * (LRU_W // N_DEV), LRU_W // N_DEV, axis=1)

    delta, new_m, new_v = {}, {}, {}
    for n in BIG:
        g, d, m2, v2 = _adamw_big(parts[n], _transport(w[n]), _transport(m[n]), _transport(v[n]),
                                  name="adamw_" + n)
        grads[n], delta[n], new_m[n], new_v[n] = g, d, m2, v2
    small_names = [n for n in WEIGHTS if n not in BIG]
    outs = _adamw_small([(_two_d(grads[n]), _two_d(w[n]), _two_d(m[n]), _two_d(v[n])) for n in small_names],
                        name="adamw_small")
    for k, n in enumerate(small_names):
        delta[n], new_m[n], new_v[n] = outs[3 * k], outs[3 * k + 1], outs[3 * k + 2]

    def shaped(d):
        return [d[n].reshape(w[n].shape) for n in WEIGHTS]

    return (loss, grad_x[None], *shaped(grads), *shaped(delta), *shaped(new_m), *shaped(new_v))
```

```python
import functools
import math

import jax
import jax.numpy as jnp
from jax import lax
from jax.experimental import pallas as pl
from jax.experimental.pallas import tpu as pltpu

f32 = jnp.float32
bf16 = jnp.bfloat16

N_DEV = 8
D_MODEL = 1024
D_FF = 4096
FF_TILE = D_FF // N_DEV
FOX_W = 512
LRU_W = 512
HEADS = 8
HEAD_D = 64
IN_COLS = 2568
IN_SHARD = IN_COLS // N_DEV
LANES = 128
LN_EPS = 1e-5
ALPHA = 2.0 ** 0.25
ATT_SCALE = 1.0 / math.sqrt(HEAD_D)
LRU_C = 8.0
NEG_BIG = -1e30

ADAM_LR = 0.001
ADAM_B1 = 0.9
ADAM_B2 = 0.999
ADAM_EPS = 1e-08
ADAM_WD = 0.01
ADAM_STEP = 10

VMEM_LIMIT = 56 * 1024 * 1024
MESH_T = pl.DeviceIdType.MESH


def _params(sem, **kw):
    return pltpu.CompilerParams(dimension_semantics=sem, vmem_limit_bytes=VMEM_LIMIT, **kw)


def _sigmoid(x):
    return 1.0 / (1.0 + jnp.exp(-x))


def _sigmoid_tanh(x):
    return 0.5 * jnp.tanh(0.5 * x) + 0.5


def _softplus(x):
    return jnp.maximum(x, 0.0) + jnp.log(1.0 + jnp.exp(-jnp.abs(x)))


def _one_minus_exp(x):
    series = -x * (1.0 + x * (0.5 + x * (1.0 / 6 + x * (1.0 / 24 + x * (1.0 / 120 + x * (1.0 / 720))))))
    return jnp.where(x > -0.125, series, 1.0 - jnp.exp(x))


_GELU_C = math.sqrt(2.0 / math.pi)


def _gelu_and_grad(x):
    inner = _GELU_C * (x + 0.044715 * x * x * x)
    t = jnp.tanh(inner)
    g = 0.5 * x * (1.0 + t)
    dg = 0.5 * (1.0 + t) + 0.5 * x * (1.0 - t * t) * _GELU_C * (1.0 + 3 * 0.044715 * x * x)
    return g, dg


def _ln_fwd_tile(pre):
    mu = jnp.mean(pre, axis=-1, keepdims=True)
    xc = pre - mu
    var = jnp.mean(xc * xc, axis=-1, keepdims=True)
    rstd = lax.rsqrt(var + LN_EPS)
    return xc * rstd, rstd


def _ln_bwd_tile(dy, xhat, rstd, g):
    dyg = dy * g
    m1 = jnp.mean(dyg, axis=-1, keepdims=True)
    m2 = jnp.mean(dyg * xhat, axis=-1, keepdims=True)
    dpre = rstd * (dyg - m1 - xhat * m2)
    return dpre, jnp.sum(dy * xhat, axis=0, keepdims=True), jnp.sum(dy, axis=0, keepdims=True)


_NT = (((1,), (1,)), ((), ()))
_TN = (((0,), (0,)), ((), ()))


class _Exchange:
    def __init__(self, arrs, gather, chips=False):
        self.arrs, self.gather, self.n, self.chips = list(arrs), gather, len(arrs), chips

    def out_shape(self):
        return [jax.ShapeDtypeStruct(((N_DEV,) + a.shape) if self.gather else a.shape, a.dtype) for a in self.arrs]

    def scratch(self):
        n_remote = self.n * (N_DEV - 1)
        return [pltpu.SemaphoreType.DMA((n_remote,)), pltpu.SemaphoreType.DMA((n_remote,)),
                pltpu.SemaphoreType.DMA((self.n,))]

    def copies(self, ins, outs, sems):
        send_sems, recv_sems, local_sems = sems
        x, y, c = lax.axis_index("x"), lax.axis_index("y"), lax.axis_index("c")
        me = 2 * x + y if self.chips else 4 * x + 2 * y + c
        out = []
        for k in range(self.n):
            for d in (range(2, N_DEV, 2) if self.chips else range(1, N_DEV)):
                px = 1 - x if d & 4 else x
                py = 1 - y if d & 2 else y
                pc = 1 - c if d & 1 else c
                sem = k * (N_DEV - 1) + d - 1
                out.append(pltpu.make_async_remote_copy(
                    src_ref=ins[k].at[2 * px + py if self.chips else 4 * px + 2 * py + pc], dst_ref=outs[k].at[me],
                    send_sem=send_sems.at[sem], recv_sem=recv_sems.at[sem],
                    device_id=(px, py, pc), device_id_type=MESH_T))
            out.append(pltpu.make_async_copy(ins[k].at[me], outs[k].at[me], local_sems.at[k]))
        return out

    def gather_copies(self, ins, outs, sems):
        send_sems, recv_sems, local_sems = sems
        x, y, c = lax.axis_index("x"), lax.axis_index("y"), lax.axis_index("c")
        sibling = (x, y, 1 - c)
        chips = [(1 - x, y), (x, 1 - y), (1 - x, 1 - y)]
        out = []
        for k in range(self.n):
            def copy(s, block, to, src=None, k=k):
                rows = outs[k].at[4 * block[0] + 2 * block[1] + block[2]]
                sem = k * (N_DEV - 1) + s
                return pltpu.make_async_remote_copy(
                    src_ref=rows if src is None else src, dst_ref=rows, send_sem=send_sems.at[sem],
                    recv_sem=recv_sems.at[sem], device_id=to, device_id_type=MESH_T)

            first = [copy(0, (x, y, c), sibling, src=ins[k])]
            first += [copy(1 + q, (x, y, c), (*chip, c), src=ins[k]) for q, chip in enumerate(chips)]
            passed = [copy(4 + q, (*chip, c), sibling) for q, chip in enumerate(chips)]
            own = pltpu.make_async_copy(ins[k], outs[k].at[4 * x + 2 * y + c], local_sems.at[k])
            out.append((first, passed, own, copy))
        return out, sibling, chips, (x, y, c)

    def start(self, ins, outs, sems):
        if not self.gather:
            for cp in self.copies(ins, outs, sems):
                cp.start()
            return
        per_array, _, _, _ = self.gather_copies(ins, outs, sems)
        for first, _, own, _ in per_array:
            own.start()
            for cp in first:
                cp.start()

    def relay(self, ins, outs, sems):
        per_array, sibling, chips, (x, y, c) = self.gather_copies(ins, outs, sems)
        for first, passed, own, copy in per_array:
            for q, chip in enumerate(chips):
                copy(1 + q, (*chip, c), (x, y, c)).wait_recv()
                passed[q].start()

    def wait(self, ins, outs, sems, relayed=False):
        if not self.gather:
            for cp in self.copies(ins, outs, sems):
                cp.wait()
            return
        if not relayed:
            self.relay(ins, outs, sems)
        per_array, sibling, chips, (x, y, c) = self.gather_copies(ins, outs, sems)
        for first, passed, own, copy in per_array:
            copy(0, sibling, (x, y, c)).wait_recv()
            for q, chip in enumerate(chips):
                copy(4 + q, (*chip, 1 - c), (x, y, c)).wait_recv()
            for cp in first + passed:
                cp.wait_send()
            own.wait()


def _hosted_call(host, body, *, name, grid, in_specs, out_specs, out_shape, scratch_shapes=(), compiler_params):
    out_specs = list(out_specs) if isinstance(out_specs, (list, tuple)) else [out_specs]
    out_shape = list(out_shape) if isinstance(out_shape, (list, tuple)) else [out_shape]
    if host is None:
        return pl.pallas_call(body, name=name, grid=grid, in_specs=in_specs, out_specs=out_specs,
                              out_shape=out_shape, scratch_shapes=list(scratch_shapes),
                              compiler_params=compiler_params)
    n_in, n_out, n_scr, k = len(in_specs), len(out_shape), len(scratch_shapes), host.n

    def wrapped(*refs):
        ins, h_in = refs[:n_in], refs[n_in:n_in + k]
        outs, h_out = refs[n_in + k:n_in + k + n_out], refs[n_in + k + n_out:n_in + 2 * k + n_out]
        scr, sems = refs[n_in + 2 * k + n_out:n_in + 2 * k + n_out + n_scr], refs[n_in + 2 * k + n_out + n_scr:]
        ids = [pl.program_id(a) for a in range(len(grid))]
        first = functools.reduce(jnp.logical_and, [i == 0 for i in ids])
        last = functools.reduce(jnp.logical_and, [i == g - 1 for i, g in zip(ids, grid)])
        steps = math.prod(grid)
        relay_at = (3 * steps) // 4 if host.gather and steps >= 8 else None

        @pl.when(first)
        def _():
            host.start(h_in, h_out, sems)

        if relay_at is not None:
            coords, rest = [], relay_at
            for g in reversed(grid):
                coords.append(rest % g)
                rest //= g

            @pl.when(functools.reduce(jnp.logical_and, [i == cd for i, cd in zip(ids, reversed(coords))]))
            def _():
                host.relay(h_in, h_out, sems)

        body(*ins, *outs, *scr)

        @pl.when(last)
        def _():
            host.wait(h_in, h_out, sems, relayed=relay_at is not None)

    hbm = pl.BlockSpec(memory_space=pl.ANY)
    call = pl.pallas_call(
        wrapped, name=name, grid=grid, in_specs=list(in_specs) + [hbm] * k, out_specs=out_specs + [hbm] * k,
        out_shape=out_shape + host.out_shape(), scratch_shapes=list(scratch_shapes) + host.scratch(),
        compiler_params=compiler_params)
    return lambda *args: call(*args, *host.arrs)


def _exchange(arrs, *, gather, name):
    host = _Exchange(arrs, gather)

    def body(*refs):
        ins, outs, sems = refs[:host.n], refs[host.n:2 * host.n], refs[2 * host.n:]
        host.start(ins, outs, sems)
        host.wait(ins, outs, sems)

    hbm = pl.BlockSpec(memory_space=pl.ANY)
    return pl.pallas_call(
        body, name=name, in_specs=[hbm] * host.n, out_specs=[hbm] * host.n, out_shape=host.out_shape(),
        scratch_shapes=host.scratch(), compiler_params=pltpu.CompilerParams(has_side_effects=True),
    )(*arrs)


def _ffn_fwd(xhat, g_in, b_in, wg, wu, wd, *, tm, name, host=None):
    t = xhat.shape[0]
    nj = N_DEV

    def body(x_ref, g_ref, b_ref, wg_ref, wu_ref, wd_ref, xo_ref, rstd_ref, hg_ref, hu_ref, xb, acc):
        j = pl.program_id(1)

        @pl.when(j == 0)
        def _():
            xb[...] = (x_ref[...] * g_ref[...] + b_ref[...]).astype(bf16)
            acc[...] = jnp.zeros_like(acc)

        hg = jnp.dot(xb[...], wg_ref[...], preferred_element_type=f32)
        hu = jnp.dot(xb[...], wu_ref[...], preferred_element_type=f32)
        hg_ref[...] = hg.astype(bf16)
        hu_ref[...] = hu.astype(bf16)
        a = hg * _sigmoid_tanh(hg) * hu
        acc[...] += jnp.dot(a.astype(bf16), wd_ref[...], preferred_element_type=f32)

        @pl.when(j == nj - 1)
        def _():
            x = x_ref[...] * g_ref[...] + b_ref[...]
            xo, rstd = _ln_fwd_tile(ALPHA * x + 0.5 * acc[...])
            xo_ref[...] = xo
            rstd_ref[...] = rstd

    row = pl.BlockSpec((1, D_MODEL), lambda i, j: (0, 0))
    return _hosted_call(
        host, body, name=name, grid=(t // tm, nj),
        in_specs=[pl.BlockSpec((tm, D_MODEL), lambda i, j: (i, 0)), row, row,
                  pl.BlockSpec((None, D_MODEL, FF_TILE), lambda i, j: (j, 0, 0)),
                  pl.BlockSpec((None, D_MODEL, FF_TILE), lambda i, j: (j, 0, 0)),
                  pl.BlockSpec((None, FF_TILE, D_MODEL), lambda i, j: (j, 0, 0))],
        out_specs=[pl.BlockSpec((tm, D_MODEL), lambda i, j: (i, 0)),
                   pl.BlockSpec((tm, 1), lambda i, j: (i, 0)),
                   pl.BlockSpec((tm, FF_TILE), lambda i, j: (i, j)),
                   pl.BlockSpec((tm, FF_TILE), lambda i, j: (i, j))],
        out_shape=[jax.ShapeDtypeStruct((t, D_MODEL), f32), jax.ShapeDtypeStruct((t, 1), f32),
                   jax.ShapeDtypeStruct((t, D_FF), bf16), jax.ShapeDtypeStruct((t, D_FF), bf16)],
        scratch_shapes=[pltpu.VMEM((tm, D_MODEL), bf16), pltpu.VMEM((tm, D_MODEL), f32)],
        compiler_params=_params(("arbitrary", "arbitrary")),
    )(xhat, g_in, b_in, wg, wu, wd)


def _ffn1_fwd_gathering(x, own, extra, *, tm, name):
    t = x.shape[0]
    n_i = t // tm
    n_arr = 3
    k_extra = extra.n
    ex = _Exchange(list(own), gather=True)
    ax, ay, ac = lax.axis_index("x"), lax.axis_index("y"), lax.axis_index("c")
    order = jnp.stack([4 * px + 2 * py + pc for px, py in ((ax, ay), (1 - ax, ay), (ax, 1 - ay), (1 - ax, 1 - ay))
                       for pc in (ac, 1 - ac)]).astype(jnp.int32)
    arrival = [None, (0, None), (1, 0), (4, None), (2, 1), (5, None), (3, 2), (6, None)]

    def body(order_ref, x_ref, *refs):
        w_in, e_in = refs[:n_arr], refs[n_arr:n_arr + k_extra]
        refs = refs[n_arr + k_extra:]
        xo_ref, rstd_ref, hg_ref, hu_ref = refs[:4]
        w_all, e_out = refs[4:4 + n_arr], refs[4 + n_arr:4 + n_arr + k_extra]
        acc, wgb, wub, wdb, fetch_sems, send_sems, recv_sems, local_sems = refs[4 + n_arr + k_extra:12 + n_arr + k_extra]
        e_sems = refs[12 + n_arr + k_extra:]
        bufs = (wgb, wub, wdb)
        s = pl.program_id(0)
        i = pl.program_id(1)
        per_array, sibling, chips, (x_, y_, c_) = ex.gather_copies(w_in, w_all, (send_sems, recv_sems, local_sems))

        def fetch(pos, slot):
            return [pltpu.make_async_copy(w_in[a] if pos == 0 else w_all[a].at[order_ref[pos]],
                                          bufs[a].at[slot], fetch_sems.at[n_arr * slot + a]) for a in range(n_arr)]

        def source_of(pos):
            chip = (x_, y_) if pos < 2 else chips[(pos - 2) // 2]
            return (*chip, c_ if pos % 2 == 0 else 1 - c_)

        @pl.when(jnp.logical_and(s == 0, i == 0))
        def _():
            for q in range(4):
                for first, _, own_copy, _ in per_array:
                    if q == 0:
                        own_copy.start()
                    first[q].start()
            for cp in fetch(0, 0):
                cp.start()
            for cp in fetch(0, 0):
                cp.wait()

        @pl.when(jnp.logical_and(s == N_DEV - 3, i == 0))
        def _():
            extra.start(e_in, e_out, e_sems)

        for pos in range(1, N_DEV):
            @pl.when(jnp.logical_and(s == pos - 1, i == min(1, n_i - 1)))
            def _(pos=pos):
                sem, passes = arrival[pos]
                for _, passed, _, copy in per_array:
                    copy(sem, source_of(pos), (x_, y_, c_)).wait_recv()
                    if passes is not None:
                        passed[passes].start()
                for cp in fetch(pos, pos % 2):
                    cp.start()

            @pl.when(jnp.logical_and(s == pos, i == 0))
            def _(pos=pos):
                for cp in fetch(pos, pos % 2):
                    cp.wait()

        slot = s % 2
        xb = x_ref[...].astype(bf16)
        hg = jnp.dot(xb, wgb[slot], preferred_element_type=f32)
        hu = jnp.dot(xb, wub[slot], preferred_element_type=f32)
        hg_ref[...] = hg.astype(bf16)
        hu_ref[...] = hu.astype(bf16)
        a = hg * _sigmoid_tanh(hg) * hu
        part = jnp.dot(a.astype(bf16), wdb[slot], preferred_element_type=f32)

        @pl.when(s == 0)
        def _():
            acc[i] = part

        @pl.when(s > 0)
        def _():
            acc[i] += part

        @pl.when(s == N_DEV - 1)
        def _():
            xo, rstd = _ln_fwd_tile(ALPHA * x_ref[...] + 0.5 * acc[i])
            xo_ref[...] = xo
            rstd_ref[...] = rstd

        @pl.when(jnp.logical_and(s == N_DEV - 1, i == n_i - 1))
        def _():
            for first, passed, own_copy, _ in per_array:
                for cp in first + passed:
                    cp.wait_send()
                own_copy.wait()
            extra.wait(e_in, e_out, e_sems)

    hbm = pl.BlockSpec(memory_space=pl.ANY)
    last = N_DEV - 1
    tok_out = pl.BlockSpec((tm, D_MODEL), lambda s, i, o: (jnp.where(s == last, i, 0), 0))
    col_out = pl.BlockSpec((tm, 1), lambda s, i, o: (jnp.where(s == last, i, 0), 0))
    hid = pl.BlockSpec((tm, FF_TILE), lambda s, i, o: (i, o[s]))
    shard_shapes = [(N_DEV,) + w.shape for w in own]
    grid_spec = pltpu.PrefetchScalarGridSpec(
        num_scalar_prefetch=1, grid=(N_DEV, n_i),
        in_specs=[pl.BlockSpec((tm, D_MODEL), lambda s, i, o: (i, 0))] + [hbm] * (n_arr + k_extra),
        out_specs=[tok_out, col_out, hid, hid] + [hbm] * (n_arr + k_extra),
        scratch_shapes=[pltpu.VMEM((n_i, tm, D_MODEL), f32)]
        + [pltpu.VMEM((2,) + w.shape, bf16) for w in own]
        + [pltpu.SemaphoreType.DMA((2 * n_arr,))] + ex.scratch() + extra.scratch())
    res = pl.pallas_call(
        body, name=name, grid_spec=grid_spec,
        out_shape=[jax.ShapeDtypeStruct((t, D_MODEL), f32), jax.ShapeDtypeStruct((t, 1), f32),
                   jax.ShapeDtypeStruct((t, D_FF), bf16), jax.ShapeDtypeStruct((t, D_FF), bf16)]
        + [jax.ShapeDtypeStruct(sh, bf16) for sh in shard_shapes] + extra.out_shape(),
        compiler_params=_params(("arbitrary", "arbitrary")),
    )(order, x, *own, *extra.arrs)
    return res


def _ffn_bwd(dpre, hg, hu, wg, wu, wd, ln_in, *, tm, name, host=None):
    t = dpre.shape[0]
    nj = N_DEV
    with_ln = ln_in is not None

    def body(*refs):
        if with_ln:
            (dp_ref, hg_ref, hu_ref, wg_ref, wu_ref, wd_ref, xh_ref, rs_ref, g_ref,
             dx_ref, gg_ref, gb_ref, dhg_ref, dhu_ref, a_ref, dfb, acc) = refs
        else:
            (dp_ref, hg_ref, hu_ref, wg_ref, wu_ref, wd_ref,
             dx_ref, dhg_ref, dhu_ref, a_ref, dfb, acc) = refs
        i = pl.program_id(0)
        j = pl.program_id(1)

        @pl.when(j == 0)
        def _():
            dfb[...] = (0.5 * dp_ref[...]).astype(bf16)
            acc[...] = jnp.zeros_like(acc)

        da = lax.dot_general(dfb[...], wd_ref[...], _NT, preferred_element_type=f32)
        hgv = hg_ref[...].astype(f32)
        huv = hu_ref[...].astype(f32)
        sg = _sigmoid_tanh(hgv)
        silu = hgv * sg
        a_ref[...] = (silu * huv).astype(bf16)
        dhu = (da * silu).astype(bf16)
        dhg = (da * huv * (sg * (1.0 + hgv * (1.0 - sg)))).astype(bf16)
        dhg_ref[...] = dhg
        dhu_ref[...] = dhu
        acc[...] += (lax.dot_general(dhg, wg_ref[...], _NT, preferred_element_type=f32)
                     + lax.dot_general(dhu, wu_ref[...], _NT, preferred_element_type=f32))

        @pl.when(j == nj - 1)
        def _():
            dx = ALPHA * dp_ref[...] + acc[...]
            if with_ln:
                dprev, gg, gb = _ln_bwd_tile(dx, xh_ref[...], rs_ref[...], g_ref[...])
                dx_ref[...] = dprev

                @pl.when(i == 0)
                def _():
                    gg_ref[...] = gg
                    gb_ref[...] = gb

                @pl.when(i > 0)
                def _():
                    gg_ref[...] += gg
                    gb_ref[...] += gb
            else:
                dx_ref[...] = dx

    tok = pl.BlockSpec((tm, D_MODEL), lambda i, j: (i, 0), pipeline_mode=pl.Buffered(1))
    row = pl.BlockSpec((1, D_MODEL), lambda i, j: (0, 0))
    hid = pl.BlockSpec((tm, FF_TILE), lambda i, j: (i, j))
    in_specs = [tok, hid, hid,
                pl.BlockSpec((None, D_MODEL, FF_TILE), lambda i, j: (j, 0, 0)),
                pl.BlockSpec((None, D_MODEL, FF_TILE), lambda i, j: (j, 0, 0)),
                pl.BlockSpec((None, FF_TILE, D_MODEL), lambda i, j: (j, 0, 0))]
    args = [dpre, hg, hu, wg, wu, wd]
    out_specs = [tok]
    out_shape = [jax.ShapeDtypeStruct((t, D_MODEL), f32)]
    if with_ln:
        in_specs += [tok, pl.BlockSpec((tm, 1), lambda i, j: (i, 0)), row]
        args += list(ln_in)
        out_specs += [row, row]
        out_shape += [jax.ShapeDtypeStruct((1, D_MODEL), f32)] * 2
    out_specs += [hid, hid, hid]
    out_shape += [jax.ShapeDtypeStruct((t, D_FF), bf16)] * 3
    return _hosted_call(
        host, body, name=name, grid=(t // tm, nj), in_specs=in_specs, out_specs=out_specs, out_shape=out_shape,
        scratch_shapes=[pltpu.VMEM((tm, D_MODEL), bf16), pltpu.VMEM((tm, D_MODEL), f32)],
        compiler_params=_params(("arbitrary", "arbitrary")),
    )(*args)


def _ffn_bwd_act(dpre, hg, hu, wd, *, tm, name, host=None):
    t = dpre.shape[0]

    def body(dp_ref, hg_ref, hu_ref, wd_ref, dhg_ref, dhu_ref, a_ref, dfb):
        @pl.when(pl.program_id(1) == 0)
        def _():
            dfb[...] = (0.5 * dp_ref[...]).astype(bf16)

        da = lax.dot_general(dfb[...], wd_ref[...], _NT, preferred_element_type=f32)
        hgv = hg_ref[...].astype(f32)
        huv = hu_ref[...].astype(f32)
        sg = _sigmoid_tanh(hgv)
        silu = hgv * sg
        a_ref[...] = (silu * huv).astype(bf16)
        dhu_ref[...] = (da * silu).astype(bf16)
        dhg_ref[...] = (da * huv * (sg * (1.0 + hgv * (1.0 - sg)))).astype(bf16)

    hid = pl.BlockSpec((tm, FF_TILE), lambda i, j: (i, j))
    return _hosted_call(
        host, body, name=name, grid=(t // tm, N_DEV),
        in_specs=[pl.BlockSpec((tm, D_MODEL), lambda i, j: (i, 0)), hid, hid,
                  pl.BlockSpec((None, FF_TILE, D_MODEL), lambda i, j: (j, 0, 0))],
        out_specs=[hid, hid, hid], out_shape=[jax.ShapeDtypeStruct((t, D_FF), bf16)] * 3,
        scratch_shapes=[pltpu.VMEM((tm, D_MODEL), bf16)],
        compiler_params=_params(("arbitrary", "arbitrary")),
    )(dpre, hg, hu, wd)


def _ffn_bwd_dx(dpre, dhg, dhu, wg, wu, *, tm, name, host=None):
    t = dpre.shape[0]
    nj = N_DEV

    def body(dp_ref, dhg_ref, dhu_ref, wg_ref, wu_ref, dx_ref, acc):
        j = pl.program_id(1)

        @pl.when(j == 0)
        def _():
            acc[...] = jnp.zeros_like(acc)

        acc[...] += (lax.dot_general(dhg_ref[...], wg_ref[...], _NT, preferred_element_type=f32)
                     + lax.dot_general(dhu_ref[...], wu_ref[...], _NT, preferred_element_type=f32))

        @pl.when(j == nj - 1)
        def _():
            dx_ref[...] = ALPHA * dp_ref[...] + acc[...]

    tok = pl.BlockSpec((tm, D_MODEL), lambda i, j: (i, 0))
    hid = pl.BlockSpec((tm, FF_TILE), lambda i, j: (i, j))
    wspec = pl.BlockSpec((None, D_MODEL, FF_TILE), lambda i, j: (j, 0, 0))
    return _hosted_call(
        host, body, name=name, grid=(t // tm, nj), in_specs=[tok, hid, hid, wspec, wspec],
        out_specs=[tok], out_shape=[jax.ShapeDtypeStruct((t, D_MODEL), f32)],
        scratch_shapes=[pltpu.VMEM((tm, D_MODEL), f32)],
        compiler_params=_params(("arbitrary", "arbitrary")),
    )(dpre, dhg, dhu, wg, wu)


def _mm(a, b, *, mode, out_dtype, tm, tn, tk, name, affine=None, a_cols=None, b_cols=None,
        b_blocked=False, out_blocked=False, out_scale=None):
    if mode == "nn":
        m_full, k_full = a.shape
        m_dim, k_dim = (m_full, a_cols[1]) if a_cols else (m_full, k_full)
    else:
        k_dim, m_full = a.shape
        m_dim = a_cols[1] if a_cols else m_full
    a_off = a_cols[0] if a_cols else 0
    if b_blocked:
        n_dim = b.shape[0] * b.shape[2]
        assert b.shape[2] == tn
    else:
        n_dim = b_cols[1] if b_cols else b.shape[1]
    b_off = b_cols[0] if b_cols else 0
    assert m_dim % tm == 0 and n_dim % tn == 0 and k_dim % tk == 0, (name, m_dim, n_dim, k_dim)
    nk = k_dim // tk

    def body(*refs):
        if affine is not None:
            a_ref, g_ref, s_ref, b_ref, o_ref, acc = refs
        else:
            a_ref, b_ref, o_ref, acc = refs
        k = pl.program_id(2)

        @pl.when(k == 0)
        def _():
            acc[...] = jnp.zeros_like(acc)

        av = a_ref[...]
        if affine is not None:
            av = av * g_ref[...] + s_ref[...]
        av = av.astype(bf16)
        bv = b_ref[...].astype(bf16)
        if mode == "nn":
            acc[...] += jnp.dot(av, bv, preferred_element_type=f32)
        else:
            acc[...] += lax.dot_general(av, bv, _TN, preferred_element_type=f32)

        @pl.when(k == nk - 1)
        def _():
            res = acc[...] if out_scale is None else acc[...] * out_scale
            o_ref[...] = res.astype(out_dtype)

    if mode == "nn":
        a_spec = pl.BlockSpec((tm, tk), lambda i, j, k: (i, k + a_off))
        aff_spec = pl.BlockSpec((1, tk), lambda i, j, k: (0, k + a_off))
    else:
        a_spec = pl.BlockSpec((tk, tm), lambda i, j, k: (k, i + a_off))
        aff_spec = pl.BlockSpec((1, tm), lambda i, j, k: (0, i + a_off))
    if b_blocked:
        b_spec = pl.BlockSpec((None, tk, tn), lambda i, j, k: (j, k, 0))
    else:
        b_spec = pl.BlockSpec((tk, tn), lambda i, j, k: (k, j + b_off))
    if out_blocked:
        o_spec = pl.BlockSpec((None, tm, tn), lambda i, j, k: (j, i, 0))
        o_shape = jax.ShapeDtypeStruct((n_dim // tn, m_dim, tn), out_dtype)
    else:
        o_spec = pl.BlockSpec((tm, tn), lambda i, j, k: (i, j))
        o_shape = jax.ShapeDtypeStruct((m_dim, n_dim), out_dtype)
    in_specs = [a_spec] + ([aff_spec, aff_spec] if affine is not None else []) + [b_spec]
    args = [a] + (list(affine) if affine is not None else []) + [b]
    return pl.pallas_call(
        body, name=name, grid=(m_dim // tm, n_dim // tn, nk), in_specs=in_specs, out_specs=o_spec,
        out_shape=o_shape, scratch_shapes=[pltpu.VMEM((tm, tn), f32)],
        compiler_params=_params(("arbitrary", "arbitrary", "arbitrary")),
    )(*args)


def _mm_tn(a, b, *, out_dtype, tm, mb, tn, nb, tk, name, affine=None, out_blocked=False, out_scale=None,
           pair=False, host=None):
    k_dim, m_dim = a.shape
    multi_b = isinstance(b, (list, tuple))
    b_list = list(b) if multi_b else [b]
    n_dim = nb * tn if multi_b else b.shape[1]
    assert m_dim % (mb * tm) == 0 and n_dim % (nb * tn) == 0 and k_dim % tk == 0, (name, m_dim, n_dim, k_dim)
    nk = k_dim // tk
    grid = (m_dim // (mb * tm), n_dim // (nb * tn), nk)
    if pair:
        assert mb * nb == 4 and grid[0] * grid[1] == 2 and out_dtype == bf16, name

    def body(*refs):
        if pair:
            refs, (acc, send_buf, recv_buf, send_sems, recv_sems) = refs[:-5], refs[-5:]
        else:
            refs, acc = refs[:-1], refs[-1]
        a_ref, o_ref = refs[0], refs[-1]
        if affine is not None:
            g_ref, s_ref = refs[1:3]
        b_refs = refs[3 if affine is not None else 1:-1]
        k = pl.program_id(2)

        @pl.when(k == 0)
        def _():
            acc[...] = jnp.zeros_like(acc)

        av = a_ref[...]
        if affine is not None:
            av = av * g_ref[...] + s_ref[...]
        av = av.astype(bf16)
        if multi_b:
            pieces = [r[...].astype(bf16) for r in b_refs]
        else:
            bv = b_refs[0][...].astype(bf16)
            pieces = [bv[:, jn * tn:(jn + 1) * tn] for jn in range(nb)]
        for im in range(mb):
            a_t = av[:, im * tm:(im + 1) * tm].T
            for jn in range(nb):
                acc[im * nb + jn] += jnp.dot(a_t, pieces[jn], preferred_element_type=f32)

        def scaled(v):
            return v if out_scale is None else v * out_scale

        @pl.when(k == nk - 1)
        def _():
            if pair:
                x, y, c = lax.axis_index("x"), lax.axis_index("y"), lax.axis_index("c")
                window = pl.program_id(0) + pl.program_id(1)
                swaps = []
                for cc in range(2):
                    send_buf[cc] = scaled(acc[2 * cc + 1 - c]).astype(bf16)
                    swaps.append(pltpu.make_async_remote_copy(
                        src_ref=send_buf.at[cc], dst_ref=recv_buf.at[window, cc],
                        send_sem=send_sems.at[2 * window + cc], recv_sem=recv_sems.at[2 * window + cc],
                        device_id=(x, y, 1 - c), device_id_type=MESH_T))
                    swaps[cc].start()
                for cc in range(2):
                    swaps[cc].wait_recv()
                    o_ref[cc] = (scaled(acc[2 * cc + c]) + recv_buf[window, cc].astype(f32)).astype(bf16)
                for cc in range(2):
                    swaps[cc].wait_send()
                return
            for im in range(mb):
                for jn in range(nb):
                    res = scaled(acc[im * nb + jn])
                    if out_blocked:
                        o_ref[jn, im * tm:(im + 1) * tm, :] = res.astype(out_dtype)
                    else:
                        o_ref[im * tm:(im + 1) * tm, jn * tn:(jn + 1) * tn] = res.astype(out_dtype)

    a_spec = pl.BlockSpec((tk, mb * tm), lambda i, j, k: (k, i))
    aff_spec = pl.BlockSpec((1, mb * tm), lambda i, j, k: (0, i))
    if multi_b:
        b_specs = [pl.BlockSpec((tk, tn), lambda i, j, k: (k, 0))] * nb
    else:
        b_specs = [pl.BlockSpec((tk, nb * tn), lambda i, j, k: (k, j))]
    scratch = [pltpu.VMEM((mb * nb, tm, tn), f32)]
    if pair:
        o_spec = pl.BlockSpec((2, tm, tn), lambda i, j, k: (i + j, 0, 0))
        o_shape = jax.ShapeDtypeStruct((4, tm, tn), out_dtype)
        scratch += [pltpu.VMEM((2, tm, tn), bf16), pltpu.VMEM((2, 2, tm, tn), bf16),
                    pltpu.SemaphoreType.DMA((4,)), pltpu.SemaphoreType.DMA((4,))]
    elif out_blocked:
        o_spec = pl.BlockSpec((nb, mb * tm, tn), lambda i, j, k: (j, i, 0))
        o_shape = jax.ShapeDtypeStruct((n_dim // tn, m_dim, tn), out_dtype)
    else:
        o_spec = pl.BlockSpec((mb * tm, nb * tn), lambda i, j, k: (i, j))
        o_shape = jax.ShapeDtypeStruct((m_dim, n_dim), out_dtype)
    in_specs = [a_spec] + ([aff_spec, aff_spec] if affine is not None else []) + b_specs
    args = [a] + (list(affine) if affine is not None else []) + b_list
    res = _hosted_call(
        host, body, name=name, grid=grid, in_specs=in_specs, out_specs=o_spec, out_shape=o_shape,
        scratch_shapes=scratch, compiler_params=_params(("arbitrary", "arbitrary", "arbitrary")),
    )(*args)
    return res[0] if host is None else res


def _in_proj(xhat, g, b, w_in, *, tm, name, host=None):
    t = xhat.shape[0]
    n_qkv, n_l = 3 * FOX_W, 2 * LRU_W

    def body(x_ref, g_ref, b_ref, w_ref, qkv_ref, zl_ref, zfg_ref):
        xb = (x_ref[...] * g_ref[...] + b_ref[...]).astype(bf16)
        qkv_ref[...] = jnp.dot(xb, w_ref[:, :n_qkv], preferred_element_type=f32).astype(bf16)
        zl_ref[...] = jnp.dot(xb, w_ref[:, n_qkv:n_qkv + n_l], preferred_element_type=f32)
        zfg_ref[...] = jnp.dot(xb, w_ref[:, n_qkv + n_l:], preferred_element_type=f32)

    row = pl.BlockSpec((1, D_MODEL), lambda i: (0, 0))
    return _hosted_call(
        host, body, name=name, grid=(t // tm,),
        in_specs=[pl.BlockSpec((tm, D_MODEL), lambda i: (i, 0)), row, row,
                  pl.BlockSpec(w_in.shape, lambda i: (0, 0))],
        out_specs=[pl.BlockSpec((tm, n_qkv), lambda i: (i, 0)), pl.BlockSpec((tm, n_l), lambda i: (i, 0)),
                   pl.BlockSpec((tm, LANES), lambda i: (i, 0))],
        out_shape=[jax.ShapeDtypeStruct((t, n_qkv), bf16), jax.ShapeDtypeStruct((t, n_l), f32),
                   jax.ShapeDtypeStruct((t, LANES), f32)],
        compiler_params=_params(("arbitrary",)),
    )(xhat, g, b, w_in)


def _mmln(pairs, *, tm, name, resid=None, resid_scale=1.0, epi=None, ln=None, n_out=D_MODEL):
    t = pairs[0][0].shape[0]
    n_pairs = len(pairs)
    n_resid = 0 if resid is None else len(resid) - 1

    def body(*refs):
        pos = 0
        val = None
        for p in range(n_pairs):
            a_ref, b_ref = refs[pos], refs[pos + 1]
            pos += 2
            av = a_ref[...].astype(bf16)
            bv = b_ref[...].astype(bf16)
            if pairs[p][6] == "nn":
                term = jnp.dot(av, bv, preferred_element_type=f32)
            else:
                term = lax.dot_general(av, bv, _NT, preferred_element_type=f32)
            val = term if val is None else val + term
        if resid is not None:
            if resid[0] == "plain":
                r = refs[pos][...]
            else:
                r = refs[pos][...] * refs[pos + 1][...] + refs[pos + 2][...]
            pos += n_resid
            val = val + resid_scale * r
        if epi is None:
            o_ref = refs[pos]
            o_ref[...] = val.astype(o_ref.dtype)
        elif epi == "ln_fwd":
            xo, rstd = _ln_fwd_tile(val)
            refs[pos][...] = xo
            refs[pos + 1][...] = rstd
        else:
            xh_ref, rs_ref, g_ref, dx_ref, gg_ref, gb_ref = refs[pos:pos + 6]
            dprev, gg, gb = _ln_bwd_tile(val, xh_ref[...], rs_ref[...], g_ref[...])
            dx_ref[...] = dprev
            i = pl.program_id(0)

            @pl.when(i == 0)
            def _():
                gg_ref[...] = gg
                gb_ref[...] = gb

            @pl.when(i > 0)
            def _():
                gg_ref[...] += gg
                gb_ref[...] += gb

    in_specs, args = [], []
    for (a, acb, aw, b, bcb, bw, mode) in pairs:
        in_specs.append(pl.BlockSpec((tm, aw), lambda i, acb=acb: (i, acb)))
        args.append(a)
        if mode == "nn":
            in_specs.append(pl.BlockSpec((aw, n_out), lambda i, bcb=bcb: (bcb, 0)))
        else:
            in_specs.append(pl.BlockSpec((n_out, bw), lambda i, bcb=bcb: (0, bcb)))
        args.append(b)
    tok = pl.BlockSpec((tm, n_out), lambda i: (i, 0))
    row = pl.BlockSpec((1, n_out), lambda i: (0, 0))
    col = pl.BlockSpec((tm, 1), lambda i: (i, 0))
    if resid is not None:
        in_specs += [tok] if resid[0] == "plain" else [tok, row, row]
        args += list(resid[1:])
    if epi is None:
        out_specs, out_shape = tok, jax.ShapeDtypeStruct((t, n_out), f32)
    elif epi == "ln_fwd":
        out_specs = [tok, col]
        out_shape = [jax.ShapeDtypeStruct((t, n_out), f32), jax.ShapeDtypeStruct((t, 1), f32)]
    else:
        in_specs += [tok, col, row]
        args += list(ln)
        out_specs = [tok, row, row]
        out_shape = [jax.ShapeDtypeStruct((t, n_out), f32)] + [jax.ShapeDtypeStruct((1, n_out), f32)] * 2
    return pl.pallas_call(
        body, name=name, grid=(t // tm,), in_specs=in_specs, out_specs=out_specs, out_shape=out_shape,
        compiler_params=_params(("arbitrary",)),
    )(*args)


def _loss_bwd(xhat, rstd, g, b, target, *, tm, name):
    t = xhat.shape[0]

    def body(xh_ref, rs_ref, g_ref, b_ref, tg_ref, dx_ref, sq_ref, gg_ref, gb_ref):
        i = pl.program_id(0)
        xh = xh_ref[...]
        diff = xh * g_ref[...] + b_ref[...] - tg_ref[...]
        sq = jnp.sum(diff * diff, axis=0, keepdims=True)
        dprev, gg, gb = _ln_bwd_tile(diff * (1.0 / D_MODEL), xh, rs_ref[...], g_ref[...])
        dx_ref[...] = dprev

        @pl.when(i == 0)
        def _():
            sq_ref[...] = sq
            gg_ref[...] = gg
            gb_ref[...] = gb

        @pl.when(i > 0)
        def _():
            sq_ref[...] += sq
            gg_ref[...] += gg
            gb_ref[...] += gb

    tok = pl.BlockSpec((tm, D_MODEL), lambda i: (i, 0))
    row = pl.BlockSpec((1, D_MODEL), lambda i: (0, 0))
    return pl.pallas_call(
        body, name=name, grid=(t // tm,),
        in_specs=[tok, pl.BlockSpec((tm, 1), lambda i: (i, 0)), row, row, tok],
        out_specs=[tok, row, row, row],
        out_shape=[jax.ShapeDtypeStruct((t, D_MODEL), f32)] + [jax.ShapeDtypeStruct((1, D_MODEL), f32)] * 3,
        compiler_params=_params(("arbitrary",)),
    )(xhat, rstd, g, b, target)


CUM_TILE = 256


def _tri(n, lower):
    r = lax.broadcasted_iota(jnp.int32, (n, n), 0)
    c = lax.broadcasted_iota(jnp.int32, (n, n), 1)
    return jnp.where((r >= c) if lower else (r <= c), 1.0, 0.0).astype(f32)


def _cum_fwd(zfg, bfg, *, name):
    t = zfg.shape[0]

    def body(z_ref, b_ref, o_ref, carry):
        @pl.when(pl.program_id(0) == 0)
        def _():
            carry[...] = jnp.zeros_like(carry)

        ls = -_softplus(-(z_ref[...] + b_ref[...]))
        c = jnp.dot(_tri(CUM_TILE, True), ls, preferred_element_type=f32,
                    precision=lax.Precision.HIGHEST) + carry[...]
        o_ref[...] = c
        carry[...] = c[CUM_TILE - 1:CUM_TILE, :]

    blk = pl.BlockSpec((CUM_TILE, LANES), lambda i: (i, 0))
    return pl.pallas_call(
        body, name=name, grid=(t // CUM_TILE,),
        in_specs=[blk, pl.BlockSpec((1, LANES), lambda i: (0, 0))], out_specs=blk,
        out_shape=jax.ShapeDtypeStruct((t, LANES), f32), scratch_shapes=[pltpu.VMEM((1, LANES), f32)],
        compiler_params=_params(("arbitrary",)),
    )(zfg, bfg)


def _cum_bwd(dcum_q, dcum_k, zfg, bfg, *, name):
    t = zfg.shape[0]
    n = t // CUM_TILE

    def body(d_ref, d2_ref, z_ref, b_ref, o_ref, s_ref, carry):
        i = pl.program_id(0)

        @pl.when(i == 0)
        def _():
            carry[...] = jnp.zeros_like(carry)

        dls = jnp.dot(_tri(CUM_TILE, False), d_ref[...] + d2_ref[...], preferred_element_type=f32,
                      precision=lax.Precision.HIGHEST) + carry[...]
        carry[...] = dls[0:1, :]
        lane = lax.broadcasted_iota(jnp.int32, (CUM_TILE, LANES), 1)
        dfg = jnp.where(lane < HEADS, dls * _sigmoid(-(z_ref[...] + b_ref[...])), 0.0)
        o_ref[...] = dfg
        tot = jnp.sum(dfg, axis=0, keepdims=True)

        @pl.when(i == 0)
        def _():
            s_ref[...] = tot

        @pl.when(i > 0)
        def _():
            s_ref[...] += tot

    blk = pl.BlockSpec((CUM_TILE, LANES), lambda i: (n - 1 - i, 0))
    row = pl.BlockSpec((1, LANES), lambda i: (0, 0))
    return pl.pallas_call(
        body, name=name, grid=(n,), in_specs=[blk, blk, blk, row], out_specs=[blk, row],
        out_shape=[jax.ShapeDtypeStruct((t, LANES), f32), jax.ShapeDtypeStruct((1, LANES), f32)],
        scratch_shapes=[pltpu.VMEM((1, LANES), f32)],
        compiler_params=_params(("arbitrary",)),
    )(dcum_q, dcum_k, zfg, bfg)


ATT_TILE = 512


ATT_ROWS = 32


def _causal_rows(r, transposed):
    rr = lax.broadcasted_iota(jnp.int32, (ATT_ROWS, ATT_TILE), 0) + r * ATT_ROWS
    cc = lax.broadcasted_iota(jnp.int32, (ATT_ROWS, ATT_TILE), 1)
    return (cc >= rr) if transposed else (rr >= cc)


def _causal(i, j, transposed):
    r = lax.broadcasted_iota(jnp.int32, (ATT_TILE, ATT_TILE), 0)
    c = lax.broadcasted_iota(jnp.int32, (ATT_TILE, ATT_TILE), 1)
    if transposed:
        return (c + i * ATT_TILE) >= (r + j * ATT_TILE)
    return (r + i * ATT_TILE) >= (c + j * ATT_TILE)


def _attn_fwd(qkv, cum, cum_t, *, name, host=None):
    t = qkv.shape[0]
    n = t // ATT_TILE
    tq = ATT_TILE

    def body(q_ref, k_ref, v_ref, cq_ref, ck_ref, o_ref, lse_ref, acc, m_s, l_s, c_s, s_s, p_s):
        i = pl.program_id(0)
        j = pl.program_id(1)

        @pl.when(j == 0)
        def _():
            acc[...] = jnp.zeros_like(acc)
            m_s[...] = jnp.full_like(m_s, NEG_BIG)
            l_s[...] = jnp.zeros_like(l_s)

        def block(masked):
            for h in range(HEADS):
                hs = slice(HEAD_D * h, HEAD_D * (h + 1))
                s_s[...] = lax.dot_general(q_ref[:, hs] * ATT_SCALE, k_ref[:, hs], _NT, preferred_element_type=f32)
                ck = ck_ref[h:h + 1, :]

                def rows_chunk(r, carry):
                    rows = pl.ds(pl.multiple_of(r * ATT_ROWS, ATT_ROWS), ATT_ROWS)
                    s = s_s[rows, :] + (cq_ref[rows, h:h + 1] - ck)
                    if masked:
                        s = jnp.where(_causal_rows(r, False), s, NEG_BIG)
                    m_old = m_s[rows, h:h + 1]
                    m_new = jnp.maximum(m_old, jnp.max(s, axis=-1, keepdims=True))
                    corr = jnp.exp(m_old - m_new)
                    p = jnp.exp(s - m_new)
                    l_s[rows, h:h + 1] = corr * l_s[rows, h:h + 1] + jnp.sum(p, axis=-1, keepdims=True)
                    m_s[rows, h:h + 1] = m_new
                    c_s[rows, h:h + 1] = corr
                    p_s[rows, :] = p.astype(bf16)
                    return carry

                lax.fori_loop(0, tq // ATT_ROWS, rows_chunk, 0, unroll=4)
                acc[:, hs] = c_s[:, h:h + 1] * acc[:, hs] + jnp.dot(p_s[...], v_ref[:, hs],
                                                                   preferred_element_type=f32)

        @pl.when(j < i)
        def _():
            block(False)

        @pl.when(j == i)
        def _():
            block(True)
            lse_ref[...] = jnp.zeros_like(lse_ref)
            for h in range(HEADS):
                hs = slice(HEAD_D * h, HEAD_D * (h + 1))
                l = l_s[:, h:h + 1]
                o_ref[:, hs] = acc[:, hs] / l
                lse_ref[:, h:h + 1] = m_s[:, h:h + 1] + jnp.log(l)

    return _hosted_call(
        host, body, name=name, grid=(n, n),
        in_specs=[pl.BlockSpec((tq, FOX_W), lambda i, j: (i, 0)),
                  pl.BlockSpec((tq, FOX_W), lambda i, j: (jnp.minimum(i, j), 1)),
                  pl.BlockSpec((tq, FOX_W), lambda i, j: (jnp.minimum(i, j), 2)),
                  pl.BlockSpec((tq, LANES), lambda i, j: (i, 0)),
                  pl.BlockSpec((HEADS, tq), lambda i, j: (0, jnp.minimum(i, j)))],
        out_specs=[pl.BlockSpec((tq, FOX_W), lambda i, j: (i, 0)), pl.BlockSpec((tq, LANES), lambda i, j: (i, 0))],
        out_shape=[jax.ShapeDtypeStruct((t, FOX_W), f32), jax.ShapeDtypeStruct((t, LANES), f32)],
        scratch_shapes=[pltpu.VMEM((tq, FOX_W), f32), pltpu.VMEM((tq, LANES), f32), pltpu.VMEM((tq, LANES), f32),
                        pltpu.VMEM((tq, LANES), f32), pltpu.VMEM((tq, tq), f32), pltpu.VMEM((tq, tq), bf16)],
        compiler_params=_params(("arbitrary", "arbitrary")),
    )(qkv, qkv, qkv, cum, cum_t)


def _attn_delta(dmix, o, *, tm, name):
    t = o.shape[0]

    def body(do_ref, o_ref, d_ref):
        r = lax.broadcasted_iota(jnp.int32, (FOX_W, LANES), 0)
        c = lax.broadcasted_iota(jnp.int32, (FOX_W, LANES), 1)
        pick = jnp.where(r // HEAD_D == c, 1.0, 0.0).astype(f32)
        d_ref[...] = jnp.dot(do_ref[...] * o_ref[...], pick, preferred_element_type=f32,
                             precision=lax.Precision.HIGHEST)

    blk = pl.BlockSpec((tm, FOX_W), lambda i: (i, 0))
    return pl.pallas_call(
        body, name=name, grid=(t // tm,), in_specs=[blk, blk],
        out_specs=pl.BlockSpec((tm, LANES), lambda i: (i, 0)),
        out_shape=jax.ShapeDtypeStruct((t, LANES), f32), compiler_params=_params(("arbitrary",)),
    )(dmix, o)


def _attn_dq(qkv, dmix, cum, cum_t, lse, delta, *, name, host=None):
    t = qkv.shape[0]
    n = t // ATT_TILE
    tq = ATT_TILE

    def body(q_ref, k_ref, v_ref, do_ref, cq_ref, ck_ref, lse_ref, dl_ref, dq_ref, dc_ref, acc, dc_acc):
        i = pl.program_id(0)
        j = pl.program_id(1)

        @pl.when(j == 0)
        def _():
            acc[...] = jnp.zeros_like(acc)
            dc_acc[...] = jnp.zeros_like(dc_acc)

        def block(masked):
            mask = _causal(i, j, False) if masked else None
            for h in range(HEADS):
                hs = slice(HEAD_D * h, HEAD_D * (h + 1))
                kh = k_ref[:, hs]
                s = lax.dot_general(q_ref[:, hs] * ATT_SCALE, kh, _NT, preferred_element_type=f32)
                s = s + cq_ref[:, h:h + 1] - ck_ref[h:h + 1, :]
                if masked:
                    s = jnp.where(mask, s, NEG_BIG)
                p = jnp.exp(s - lse_ref[:, h:h + 1])
                dp = lax.dot_general(do_ref[:, hs].astype(bf16), v_ref[:, hs], _NT, preferred_element_type=f32)
                ds = p * (dp - dl_ref[:, h:h + 1])
                acc[:, hs] += jnp.dot(ds.astype(bf16), kh, preferred_element_type=f32)
                dc_acc[:, h:h + 1] += jnp.sum(ds, axis=-1, keepdims=True)

        @pl.when(j < i)
        def _():
            block(False)

        @pl.when(j == i)
        def _():
            block(True)
            dq_ref[...] = (acc[...] * ATT_SCALE).astype(bf16)
            dc_ref[...] = dc_acc[...]

    col = pl.BlockSpec((tq, LANES), lambda i, j: (i, 0))
    return _hosted_call(
        host, body, name=name, grid=(n, n),
        in_specs=[pl.BlockSpec((tq, FOX_W), lambda i, j: (i, 0)),
                  pl.BlockSpec((tq, FOX_W), lambda i, j: (jnp.minimum(i, j), 1)),
                  pl.BlockSpec((tq, FOX_W), lambda i, j: (jnp.minimum(i, j), 2)),
                  pl.BlockSpec((tq, FOX_W), lambda i, j: (i, 0)),
                  col, pl.BlockSpec((HEADS, tq), lambda i, j: (0, jnp.minimum(i, j))), col, col],
        out_specs=[pl.BlockSpec((tq, FOX_W), lambda i, j: (i, 0)), col],
        out_shape=[jax.ShapeDtypeStruct((t, FOX_W), bf16), jax.ShapeDtypeStruct((t, LANES), f32)],
        scratch_shapes=[pltpu.VMEM((tq, FOX_W), f32), pltpu.VMEM((tq, LANES), f32)],
        compiler_params=_params(("arbitrary", "arbitrary")),
    )(qkv, qkv, qkv, dmix, cum, cum_t, lse, delta)


def _attn_dkv(qkv, dmix, cum, cum_t, lse_t, delta_t, *, name):
    t = qkv.shape[0]
    n = t // ATT_TILE
    tk = ATT_TILE

    def body(q_ref, k_ref, v_ref, do_ref, cq_ref, ck_ref, lse_ref, dl_ref, dk_ref, dv_ref, dc_ref, dk_acc, dv_acc, dc_acc):
        j = pl.program_id(0)
        i = pl.program_id(1)

        @pl.when(i == 0)
        def _():
            dk_acc[...] = jnp.zeros_like(dk_acc)
            dv_acc[...] = jnp.zeros_like(dv_acc)
            dc_acc[...] = jnp.zeros_like(dc_acc)

        def block(masked):
            mask = _causal(i, j, True) if masked else None
            for h in range(HEADS):
                hs = slice(HEAD_D * h, HEAD_D * (h + 1))
                qh = q_ref[:, hs]
                doh = do_ref[:, hs].astype(bf16)
                s_t = lax.dot_general(k_ref[:, hs] * ATT_SCALE, qh, _NT, preferred_element_type=f32)
                s_t = s_t + cq_ref[h:h + 1, :] - ck_ref[:, h:h + 1]
                if masked:
                    s_t = jnp.where(mask, s_t, NEG_BIG)
                p_t = jnp.exp(s_t - lse_ref[h:h + 1, :])
                dv_acc[:, hs] += jnp.dot(p_t.astype(bf16), doh, preferred_element_type=f32)
                dp_t = lax.dot_general(v_ref[:, hs], doh, _NT, preferred_element_type=f32)
                ds_t = p_t * (dp_t - dl_ref[h:h + 1, :])
                dk_acc[:, hs] += jnp.dot(ds_t.astype(bf16), qh, preferred_element_type=f32)
                dc_acc[:, h:h + 1] -= jnp.sum(ds_t, axis=-1, keepdims=True)

        @pl.when(i > j)
        def _():
            block(False)

        @pl.when(i == j)
        def _():
            block(True)

        @pl.when(i == n - 1)
        def _():
            dk_ref[...] = (dk_acc[...] * ATT_SCALE).astype(bf16)
            dv_ref[...] = dv_acc[...].astype(bf16)
            dc_ref[...] = dc_acc[...]

    rowq = pl.BlockSpec((HEADS, tk), lambda j, i: (0, jnp.maximum(i, j)))
    return pl.pallas_call(
        body, name=name, grid=(n, n),
        in_specs=[pl.BlockSpec((tk, FOX_W), lambda j, i: (jnp.maximum(i, j), 0)),
                  pl.BlockSpec((tk, FOX_W), lambda j, i: (j, 1)),
                  pl.BlockSpec((tk, FOX_W), lambda j, i: (j, 2)),
                  pl.BlockSpec((tk, FOX_W), lambda j, i: (jnp.maximum(i, j), 0)),
                  rowq, pl.BlockSpec((tk, LANES), lambda j, i: (j, 0)), rowq, rowq],
        out_specs=[pl.BlockSpec((tk, FOX_W), lambda j, i: (j, 0)), pl.BlockSpec((tk, FOX_W), lambda j, i: (j, 0)),
                   pl.BlockSpec((tk, LANES), lambda j, i: (j, 0))],
        out_shape=[jax.ShapeDtypeStruct((t, FOX_W), bf16), jax.ShapeDtypeStruct((t, FOX_W), bf16),
                   jax.ShapeDtypeStruct((t, LANES), f32)],
        scratch_shapes=[pltpu.VMEM((tk, FOX_W), f32), pltpu.VMEM((tk, FOX_W), f32), pltpu.VMEM((tk, LANES), f32)],
        compiler_params=_params(("arbitrary", "arbitrary")),
    )(qkv, qkv, qkv, dmix, cum_t, cum, lse_t, delta_t)


ATT_W = HEADS * LANES


def _data_lane(h):
    return HEAD_D * (h % 2)


def _extra_lane(h):
    return HEAD_D - _data_lane(h)


def _split3(x):
    hi = x.astype(bf16)
    rest = x - hi.astype(f32)
    mid = rest.astype(bf16)
    lo = (rest - mid.astype(f32)).astype(bf16)
    return hi, mid, lo


def _augment(pair, h, first, second, fill=0.0):
    rows = pair.shape[0]
    lane = lax.broadcasted_iota(jnp.int32, (rows, LANES), 1)
    base = _extra_lane(h)
    own = (lane < HEAD_D) if h % 2 == 0 else (lane >= HEAD_D)
    out = jnp.where(own, pair, jnp.full((rows, LANES), fill, bf16))
    for off, src in ((0, first), (3, second)):
        for q in range(3):
            val = src[q] if isinstance(src, tuple) else jnp.full((rows, 1), src, bf16)
            out = jnp.where(lane == base + off + q, val, out)
    return out


def _attn_prep_fwd(qkv, cum, *, tm, name):
    t = qkv.shape[0]

    def body(q_ref, k_ref, v_ref, c_ref, qa_ref, ka_ref, va_ref):
        for h in range(HEADS):
            pair = slice(LANES * (h // 2), LANES * (h // 2 + 1))
            hs = slice(LANES * h, LANES * (h + 1))
            c3 = _split3(c_ref[:, h:h + 1])
            qa_ref[:, hs] = _augment(q_ref[:, pair] * ATT_SCALE, h, c3, 1.0)
            ka_ref[:, hs] = _augment(k_ref[:, pair], h, 1.0, tuple(-p for p in c3))
            va_ref[:, hs] = _augment(v_ref[:, pair], h, 1.0, 1.0, fill=1.0)

    wide = pl.BlockSpec((tm, ATT_W), lambda i: (i, 0))
    out = jax.ShapeDtypeStruct((t, ATT_W), bf16)
    return pl.pallas_call(
        body, name=name, grid=(t // tm,),
        in_specs=[pl.BlockSpec((tm, FOX_W), lambda i: (i, 0)), pl.BlockSpec((tm, FOX_W), lambda i: (i, 1)),
                  pl.BlockSpec((tm, FOX_W), lambda i: (i, 2)), pl.BlockSpec((tm, LANES), lambda i: (i, 0))],
        out_specs=[wide] * 3, out_shape=[out] * 3, compiler_params=_params(("arbitrary",)),
    )(qkv, qkv, qkv, cum)


def _attn_prep_bwd(qkv, cum, lse, dmix, o, *, tm, name):
    t = qkv.shape[0]

    def body(q_ref, c_ref, l_ref, do_ref, o_ref, qa_ref, da_ref):
        for h in range(HEADS):
            pair = slice(LANES * (h // 2), LANES * (h // 2 + 1))
            src = slice(HEAD_D * h, HEAD_D * (h + 1))
            hs = slice(LANES * h, LANES * (h + 1))
            delta = jnp.sum(do_ref[:, src] * o_ref[:, src], axis=-1, keepdims=True)
            qa_ref[:, hs] = _augment(q_ref[:, pair] * ATT_SCALE, h,
                                     _split3(c_ref[:, h:h + 1] - l_ref[:, h:h + 1]), 1.0)
            da_ref[:, hs] = _augment(do_ref[:, pair].astype(bf16), h, tuple(-p for p in _split3(delta)), 0.0)

    wide = pl.BlockSpec((tm, ATT_W), lambda i: (i, 0))
    half = pl.BlockSpec((tm, FOX_W), lambda i: (i, 0))
    col = pl.BlockSpec((tm, LANES), lambda i: (i, 0))
    out = jax.ShapeDtypeStruct((t, ATT_W), bf16)
    return pl.pallas_call(
        body, name=name, grid=(t // tm,), in_specs=[half, col, col, half, half],
        out_specs=[wide] * 2, out_shape=[out] * 2, compiler_params=_params(("arbitrary",)),
    )(qkv, cum, lse, dmix, o)


def _attn_fwd2(q_aug, k_aug, v_aug, *, name, host=None):
    t = q_aug.shape[0]
    n = t // ATT_TILE
    tq = ATT_TILE

    def body(q_ref, k_ref, v_ref, o_ref, lse_ref, acc, m_s):
        i = pl.program_id(0)
        j = pl.program_id(1)

        @pl.when(j == 0)
        def _():
            acc[...] = jnp.zeros_like(acc)
            m_s[...] = jnp.full_like(m_s, NEG_BIG)

        def block(masked):
            mask = _causal(i, j, False) if masked else None
            for h in range(HEADS):
                hs = slice(LANES * h, LANES * (h + 1))
                s = lax.dot_general(q_ref[:, hs], k_ref[:, hs], _NT, preferred_element_type=f32)
                if masked:
                    s = jnp.where(mask, s, NEG_BIG)
                blocks = [s[:, LANES * b:LANES * (b + 1)] for b in range(tq // LANES)]
                m_old = m_s[h]
                m_new = jnp.maximum(m_old, jnp.broadcast_to(
                    jnp.max(functools.reduce(jnp.maximum, blocks), axis=-1, keepdims=True), (tq, LANES)))
                p = jnp.concatenate([jnp.exp(b - m_new) for b in blocks], axis=1).astype(bf16)
                acc[h] = jnp.exp(m_old - m_new) * acc[h] + jnp.dot(p, v_ref[:, hs], preferred_element_type=f32)
                m_s[h] = m_new

        @pl.when(j < i)
        def _():
            block(False)

        @pl.when(j == i)
        def _():
            block(True)
            lse_ref[...] = jnp.zeros_like(lse_ref)
            for h in range(HEADS):
                a = acc[h]
                l = a[:, _extra_lane(h):_extra_lane(h) + 1]
                o_ref[:, HEAD_D * h:HEAD_D * (h + 1)] = a[:, _data_lane(h):_data_lane(h) + HEAD_D] / l
                lse_ref[:, h:h + 1] = m_s[h][:, 0:1] + jnp.log(l)

    kv = pl.BlockSpec((tq, ATT_W), lambda i, j: (jnp.minimum(i, j), 0))
    return _hosted_call(
        host, body, name=name, grid=(n, n),
        in_specs=[pl.BlockSpec((tq, ATT_W), lambda i, j: (i, 0)), kv, kv],
        out_specs=[pl.BlockSpec((tq, FOX_W), lambda i, j: (i, 0)), pl.BlockSpec((tq, LANES), lambda i, j: (i, 0))],
        out_shape=[jax.ShapeDtypeStruct((t, FOX_W), f32), jax.ShapeDtypeStruct((t, LANES), f32)],
        scratch_shapes=[pltpu.VMEM((HEADS, tq, LANES), f32), pltpu.VMEM((HEADS, tq, LANES), f32)],
        compiler_params=_params(("arbitrary", "arbitrary")),
    )(q_aug, k_aug, v_aug)


def _attn_bwd(qb_aug, k_aug, v_aug, do_aug, *, name, host=None):
    t = qb_aug.shape[0]
    n = t // ATT_TILE
    tk = ATT_TILE

    def body(q_ref, k_ref, v_ref, do_ref, dq_ref, dcq_ref, dk_ref, dv_ref, dck_ref, dk_acc, dv_acc, dq_all):
        j = pl.program_id(0)
        i = pl.program_id(1)

        @pl.when(jnp.logical_and(i == 0, j == 0))
        def _():
            dq_all[...] = jnp.zeros_like(dq_all)

        @pl.when(i == 0)
        def _():
            dk_acc[...] = jnp.zeros_like(dk_acc)
            dv_acc[...] = jnp.zeros_like(dv_acc)

        def block(masked):
            mask = _causal(i, j, True) if masked else None
            for h in range(HEADS):
                hs = slice(LANES * h, LANES * (h + 1))
                qh = q_ref[:, hs]
                doh = do_ref[:, hs]
                kh = k_ref[:, hs]
                s_t = lax.dot_general(kh, qh, _NT, preferred_element_type=f32)
                if masked:
                    s_t = jnp.where(mask, s_t, NEG_BIG)
                p_t = jnp.exp(s_t)
                dv_acc[h] += jnp.dot(p_t.astype(bf16), doh, preferred_element_type=f32)
                dp_t = lax.dot_general(v_ref[:, hs], doh, _NT, preferred_element_type=f32)
                ds_t = (p_t * dp_t).astype(bf16)
                dk_acc[h] += jnp.dot(ds_t, qh, preferred_element_type=f32)
                dq_all[i, h] += lax.dot_general(ds_t, kh, _TN, preferred_element_type=f32)

        @pl.when(i > j)
        def _():
            block(False)

        @pl.when(i == j)
        def _():
            block(True)
            dcq_ref[...] = jnp.zeros_like(dcq_ref)
            for h in range(HEADS):
                a = dq_all[j, h]
                dq_ref[:, HEAD_D * h:HEAD_D * (h + 1)] = (
                    a[:, _data_lane(h):_data_lane(h) + HEAD_D] * ATT_SCALE).astype(bf16)
                dcq_ref[:, h:h + 1] = a[:, _extra_lane(h):_extra_lane(h) + 1]

        @pl.when(i == n - 1)
        def _():
            dck_ref[...] = jnp.zeros_like(dck_ref)
            for h in range(HEADS):
                a = dk_acc[h]
                cols = slice(_data_lane(h), _data_lane(h) + HEAD_D)
                dk_ref[:, HEAD_D * h:HEAD_D * (h + 1)] = a[:, cols].astype(bf16)
                dv_ref[:, HEAD_D * h:HEAD_D * (h + 1)] = dv_acc[h][:, cols].astype(bf16)
                dck_ref[:, h:h + 1] = -a[:, _extra_lane(h) + 3:_extra_lane(h) + 4]

    own = pl.BlockSpec((tk, ATT_W), lambda j, i: (j, 0))
    qs = pl.BlockSpec((tk, ATT_W), lambda j, i: (jnp.maximum(i, j), 0))
    half = pl.BlockSpec((tk, FOX_W), lambda j, i: (j, 0))
    col = pl.BlockSpec((tk, LANES), lambda j, i: (j, 0))
    return _hosted_call(
        host, body, name=name, grid=(n, n), in_specs=[qs, own, own, qs],
        out_specs=[half, col, half, half, col],
        out_shape=[jax.ShapeDtypeStruct((t, FOX_W), bf16), jax.ShapeDtypeStruct((t, LANES), f32),
                   jax.ShapeDtypeStruct((t, FOX_W), bf16), jax.ShapeDtypeStruct((t, FOX_W), bf16),
                   jax.ShapeDtypeStruct((t, LANES), f32)],
        scratch_shapes=[pltpu.VMEM((HEADS, tk, LANES), f32), pltpu.VMEM((HEADS, tk, LANES), f32),
                        pltpu.VMEM((n, HEADS, tk, LANES), f32)],
        compiler_params=_params(("arbitrary", "arbitrary")),
    )(qb_aug, k_aug, v_aug, do_aug)


def _attn_dq2(qb_aug, k_aug, v_aug, do_aug, *, name, host=None):
    t = qb_aug.shape[0]
    n = t // ATT_TILE
    tq = ATT_TILE

    def body(q_ref, k_ref, v_ref, do_ref, dq_ref, dc_ref, acc):
        i = pl.program_id(0)
        j = pl.program_id(1)

        @pl.when(j == 0)
        def _():
            acc[...] = jnp.zeros_like(acc)

        def block(masked):
            mask = _causal(i, j, False) if masked else None
            for h in range(HEADS):
                hs = slice(LANES * h, LANES * (h + 1))
                kh = k_ref[:, hs]
                s = lax.dot_general(q_ref[:, hs], kh, _NT, preferred_element_type=f32)
                if masked:
                    s = jnp.where(mask, s, NEG_BIG)
                dp = lax.dot_general(do_ref[:, hs], v_ref[:, hs], _NT, preferred_element_type=f32)
                ds = (jnp.exp(s) * dp).astype(bf16)
                acc[h] += jnp.dot(ds, kh, preferred_element_type=f32)

        @pl.when(j < i)
        def _():
            block(False)

        @pl.when(j == i)
        def _():
            block(True)
            dc_ref[...] = jnp.zeros_like(dc_ref)
            for h in range(HEADS):
                a = acc[h]
                dq_ref[:, HEAD_D * h:HEAD_D * (h + 1)] = (
                    a[:, _data_lane(h):_data_lane(h) + HEAD_D] * ATT_SCALE).astype(bf16)
                dc_ref[:, h:h + 1] = a[:, _extra_lane(h):_extra_lane(h) + 1]

    own = pl.BlockSpec((tq, ATT_W), lambda i, j: (i, 0))
    kv = pl.BlockSpec((tq, ATT_W), lambda i, j: (jnp.minimum(i, j), 0))
    return _hosted_call(
        host, body, name=name, grid=(n, n), in_specs=[own, kv, kv, own],
        out_specs=[pl.BlockSpec((tq, FOX_W), lambda i, j: (i, 0)), pl.BlockSpec((tq, LANES), lambda i, j: (i, 0))],
        out_shape=[jax.ShapeDtypeStruct((t, FOX_W), bf16), jax.ShapeDtypeStruct((t, LANES), f32)],
        scratch_shapes=[pltpu.VMEM((HEADS, tq, LANES), f32)],
        compiler_params=_params(("arbitrary", "arbitrary")),
    )(qb_aug, k_aug, v_aug, do_aug)


def _attn_dkv2(qb_aug, k_aug, v_aug, do_aug, *, name, host=None):
    t = qb_aug.shape[0]
    n = t // ATT_TILE
    tk = ATT_TILE

    def body(q_ref, k_ref, v_ref, do_ref, dk_ref, dv_ref, dc_ref, dk_acc, dv_acc):
        j = pl.program_id(0)
        i = pl.program_id(1)

        @pl.when(i == 0)
        def _():
            dk_acc[...] = jnp.zeros_like(dk_acc)
            dv_acc[...] = jnp.zeros_like(dv_acc)

        def block(masked):
            mask = _causal(i, j, True) if masked else None
            for h in range(HEADS):
                hs = slice(LANES * h, LANES * (h + 1))
                qh = q_ref[:, hs]
                doh = do_ref[:, hs]
                s_t = lax.dot_general(k_ref[:, hs], qh, _NT, preferred_element_type=f32)
                if masked:
                    s_t = jnp.where(mask, s_t, NEG_BIG)
                p_t = jnp.exp(s_t)
                dv_acc[h] += jnp.dot(p_t.astype(bf16), doh, preferred_element_type=f32)
                dp_t = lax.dot_general(v_ref[:, hs], doh, _NT, preferred_element_type=f32)
                dk_acc[h] += jnp.dot((p_t * dp_t).astype(bf16), qh, preferred_element_type=f32)

        @pl.when(i > j)
        def _():
            block(False)

        @pl.when(i == j)
        def _():
            block(True)

        @pl.when(i == n - 1)
        def _():
            dc_ref[...] = jnp.zeros_like(dc_ref)
            for h in range(HEADS):
                a = dk_acc[h]
                cols = slice(_data_lane(h), _data_lane(h) + HEAD_D)
                dk_ref[:, HEAD_D * h:HEAD_D * (h + 1)] = a[:, cols].astype(bf16)
                dv_ref[:, HEAD_D * h:HEAD_D * (h + 1)] = dv_acc[h][:, cols].astype(bf16)
                dc_ref[:, h:h + 1] = -a[:, _extra_lane(h) + 3:_extra_lane(h) + 4]

    own = pl.BlockSpec((tk, ATT_W), lambda j, i: (j, 0))
    qs = pl.BlockSpec((tk, ATT_W), lambda j, i: (jnp.maximum(i, j), 0))
    half = pl.BlockSpec((tk, FOX_W), lambda j, i: (j, 0))
    return _hosted_call(
        host, body, name=name, grid=(n, n), in_specs=[qs, own, own, qs],
        out_specs=[half, half, pl.BlockSpec((tk, LANES), lambda j, i: (j, 0))],
        out_shape=[jax.ShapeDtypeStruct((t, FOX_W), bf16), jax.ShapeDtypeStruct((t, FOX_W), bf16),
                   jax.ShapeDtypeStruct((t, LANES), f32)],
        scratch_shapes=[pltpu.VMEM((HEADS, tk, LANES), f32), pltpu.VMEM((HEADS, tk, LANES), f32)],
        compiler_params=_params(("arbitrary", "arbitrary")),
    )(qb_aug, k_aug, v_aug, do_aug)


LRU_CHUNK = 64
LRU_G = 256
SUB = 8


def _row_ids(n):
    return lax.broadcasted_iota(jnp.int32, (n, LRU_G), 0)


def _shift_rows_down(ext, s):
    return pltpu.roll(ext, s, axis=0)[SUB:, :]


def _shift_rows_up(ext, s, n):
    return pltpu.roll(ext, ext.shape[0] - s, axis=0)[:n, :]


def _lru_gates(u, wa_ref, ba_ref, wx_ref, bx_ref, sp):
    ub = u.astype(bf16)
    r = _sigmoid(jnp.dot(ub, wa_ref[...], preferred_element_type=f32) + ba_ref[...])
    gi = _sigmoid(jnp.dot(ub, wx_ref[...], preferred_element_type=f32) + bx_ref[...])
    log_a = -LRU_C * r * sp
    a = jnp.exp(log_a)
    s = jnp.sqrt(_one_minus_exp(2.0 * log_a))
    return r, gi, a, s


def _conv_window(lx_ref, r0, ci):
    cur = lx_ref[pl.ds(r0, LRU_CHUNK), :]
    p0 = pl.multiple_of(jnp.maximum(r0 - SUB, 0), SUB)
    prev = jnp.where(ci > 0, lx_ref[pl.ds(p0, SUB), :], 0.0)
    return cur, jnp.concatenate([prev, cur], axis=0)


def _lru_fwd(zl, conv_w, conv_b, wa, ba, wx, bx, lam, *, name, host=None):
    t = zl.shape[0]
    n_chunk = t // LRU_CHUNK

    def body(lx_ref, lg_ref, cw_ref, cb_ref, wa_ref, ba_ref, wx_ref, bx_ref, lam_ref, u_ref, h_ref, y_ref):
        sp = _softplus(-lam_ref[...])
        rows = _row_ids(SUB)

        def chunk(ci, hc):
            r0 = pl.multiple_of(ci * LRU_CHUNK, LRU_CHUNK)
            cur, ext = _conv_window(lx_ref, r0, ci)
            u = cb_ref[...] + cw_ref[3:4, :] * cur
            for k in range(3):
                u = u + cw_ref[k:k + 1, :] * _shift_rows_down(ext, 3 - k)
            r, gi, a, s = _lru_gates(u, wa_ref, ba_ref, wx_ref, bx_ref, sp)
            b = s * (gi * u)
            tiles = []
            for q in range(LRU_CHUNK // SUB):
                ta = a[SUB * q:SUB * (q + 1), :]
                tb = b[SUB * q:SUB * (q + 1), :]
                for d in (1, 2, 4):
                    a_sh = jnp.where(rows >= d, pltpu.roll(ta, d, axis=0), 1.0)
                    b_sh = jnp.where(rows >= d, pltpu.roll(tb, d, axis=0), 0.0)
                    tb = ta * b_sh + tb
                    ta = ta * a_sh
                hq = tb + ta * hc
                hc = hq[SUB - 1:SUB, :]
                tiles.append(hq)
            h = jnp.concatenate(tiles, axis=0)
            u_ref[pl.ds(r0, LRU_CHUNK), :] = u
            h_ref[pl.ds(r0, LRU_CHUNK), :] = h
            gel, _ = _gelu_and_grad(lg_ref[pl.ds(r0, LRU_CHUNK), :])
            y_ref[pl.ds(r0, LRU_CHUNK), :] = gel * h
            return hc

        lax.fori_loop(0, n_chunk, chunk, jnp.zeros((1, LRU_G), f32))

    seq = lambda cb: pl.BlockSpec((t, LRU_G), lambda c, cb=cb: (0, c + cb))
    rowc = pl.BlockSpec((1, LRU_G), lambda c: (0, c))
    diag = pl.BlockSpec((LRU_G, LRU_G), lambda c: (c, c))
    out = jax.ShapeDtypeStruct((t, LRU_W), f32)
    return _hosted_call(
        host, body, name=name, grid=(LRU_W // LRU_G,),
        in_specs=[seq(0), seq(LRU_W // LRU_G), pl.BlockSpec((4, LRU_G), lambda c: (0, c)),
                  rowc, diag, rowc, diag, rowc, rowc],
        out_specs=[seq(0)] * 3, out_shape=[out] * 3,
        compiler_params=_params(("arbitrary",)),
    )(zl, zl, conv_w, conv_b, wa, ba, wx, bx, lam)


def _lru_bwd(dmix, zl, u_all, h_all, conv_w, wa, ba, wx, bx, lam, *, name, host=None):
    t = zl.shape[0]
    n_chunk = t // LRU_CHUNK

    def body(dy_ref, lx_ref, lg_ref, u_ref, h_ref, cw_ref, wa_ref, ba_ref, wx_ref, bx_ref, lam_ref,
             dlx_ref, dlg_ref, dcw_ref, dcb_ref, dba_ref, dbx_ref, dlam_ref, dwa_ref, dwx_ref, dpr_s, dpx_s):
        lam_v = lam_ref[...]
        sp = _softplus(-lam_v)
        rows = _row_ids(SUB)
        rows_c = _row_ids(LRU_CHUNK)
        zero_row = jnp.zeros((1, LRU_G), f32)

        def chunk(step, carry):
            dh_c, a_next0, du_next, dsp, dba, dbx, dcb, dw0, dw1, dw2, dw3 = carry
            ci = n_chunk - 1 - step
            r0 = pl.multiple_of(ci * LRU_CHUNK, LRU_CHUNK)
            sl = pl.ds(r0, LRU_CHUNK)
            u = u_ref[sl, :]
            r, gi, a, s = _lru_gates(u, wa_ref, ba_ref, wx_ref, bx_ref, sp)
            h = h_ref[sl, :]
            p0 = pl.multiple_of(jnp.maximum(r0 - SUB, 0), SUB)
            h_before = jnp.where(ci > 0, h_ref[pl.ds(p0, SUB), :], 0.0)[SUB - 1:SUB, :]
            h_prev = jnp.where(rows_c == 0, h_before, pltpu.roll(h, 1, axis=0))
            gel, dgel = _gelu_and_grad(lg_ref[sl, :])
            dy = dy_ref[sl, :]
            dlg_ref[sl, :] = (dy * h * dgel).astype(bf16)
            g_in = dy * gel
            a_next = jnp.where(rows_c == LRU_CHUNK - 1, a_next0, pltpu.roll(a, LRU_CHUNK - 1, axis=0))
            tiles = [None] * (LRU_CHUNK // SUB)
            for q in reversed(range(LRU_CHUNK // SUB)):
                ta = a_next[SUB * q:SUB * (q + 1), :]
                tb = g_in[SUB * q:SUB * (q + 1), :]
                for d in (1, 2, 4):
                    a_sh = jnp.where(rows < SUB - d, pltpu.roll(ta, SUB - d, axis=0), 1.0)
                    b_sh = jnp.where(rows < SUB - d, pltpu.roll(tb, SUB - d, axis=0), 0.0)
                    tb = ta * b_sh + tb
                    ta = ta * a_sh
                dhq = tb + ta * dh_c
                dh_c = dhq[0:1, :]
                tiles[q] = dhq
            dh = jnp.concatenate(tiles, axis=0)
            da = dh * h_prev
            ds = dh * gi * u
            dgi = dh * s * u
            du = dh * s * gi
            dlog_a = da * a - ds * (a * a) / s
            dr = dlog_a * (-LRU_C * sp)
            dsp = dsp + jnp.sum(dlog_a * (-LRU_C * r), axis=0, keepdims=True)
            dpr = dr * r * (1.0 - r)
            dpx = dgi * gi * (1.0 - gi)
            dprb = dpr.astype(bf16)
            dpxb = dpx.astype(bf16)
            dpr_s[sl, :] = dprb
            dpx_s[sl, :] = dpxb
            du = du + (lax.dot_general(dprb, wa_ref[...], _NT, preferred_element_type=f32)
                       + lax.dot_general(dpxb, wx_ref[...], _NT, preferred_element_type=f32))
            dba = dba + jnp.sum(dpr, axis=0, keepdims=True)
            dbx = dbx + jnp.sum(dpx, axis=0, keepdims=True)
            dcb = dcb + jnp.sum(du, axis=0, keepdims=True)
            du_ext = jnp.concatenate([du, du_next], axis=0)
            dlx = cw_ref[3:4, :] * du
            for k in range(3):
                dlx = dlx + cw_ref[k:k + 1, :] * _shift_rows_up(du_ext, 3 - k, LRU_CHUNK)
            dlx_ref[sl, :] = dlx.astype(bf16)
            cur, ext = _conv_window(lx_ref, r0, ci)
            dws = [dw0, dw1, dw2, dw3 + jnp.sum(du * cur, axis=0, keepdims=True)]
            for k in range(3):
                dws[k] = dws[k] + jnp.sum(du * _shift_rows_down(ext, 3 - k), axis=0, keepdims=True)
            return (dh_c, a[0:1, :], du[0:SUB, :], dsp, dba, dbx, dcb, dws[0], dws[1], dws[2], dws[3])

        init = (zero_row, zero_row, jnp.zeros((SUB, LRU_G), f32)) + (zero_row,) * 8
        out = lax.fori_loop(0, n_chunk, chunk, init)
        _, _, _, dsp, dba, dbx, dcb, dw0, dw1, dw2, dw3 = out
        dlam_ref[...] = dsp * (-_sigmoid(-lam_v))
        dba_ref[...] = dba
        dbx_ref[...] = dbx
        dcb_ref[...] = dcb
        dcw_ref[...] = jnp.concatenate([dw0, dw1, dw2, dw3], axis=0)
        ub = u_ref[...].astype(bf16)
        dwa_ref[...] = lax.dot_general(ub, dpr_s[...], _TN, preferred_element_type=f32)
        dwx_ref[...] = lax.dot_general(ub, dpx_s[...], _TN, preferred_element_type=f32)

    seq = lambda cb: pl.BlockSpec((t, LRU_G), lambda c, cb=cb: (0, c + cb))
    rowc = pl.BlockSpec((1, LRU_G), lambda c: (0, c))
    diag = pl.BlockSpec((LRU_G, LRU_G), lambda c: (c, c))
    gate_out = pl.BlockSpec((None, LRU_G, LRU_G), lambda c: (c, 0, 0))
    row_shape = jax.ShapeDtypeStruct((1, LRU_W), f32)
    return _hosted_call(
        host, body, name=name, grid=(LRU_W // LRU_G,),
        in_specs=[seq(LRU_W // LRU_G), seq(0), seq(LRU_W // LRU_G), seq(0), seq(0),
                  pl.BlockSpec((4, LRU_G), lambda c: (0, c)),
                  diag, rowc, diag, rowc, rowc],
        out_specs=[seq(0), seq(0), pl.BlockSpec((4, LRU_G), lambda c: (0, c)), rowc, rowc, rowc, rowc,
                   gate_out, gate_out],
        out_shape=[jax.ShapeDtypeStruct((t, LRU_W), bf16)] * 2
        + [jax.ShapeDtypeStruct((4, LRU_W), f32)] + [row_shape] * 4
        + [jax.ShapeDtypeStruct((LRU_W // LRU_G, LRU_G, LRU_G), f32)] * 2,
        scratch_shapes=[pltpu.VMEM((t, LRU_G), bf16), pltpu.VMEM((t, LRU_G), bf16)],
        compiler_params=_params(("arbitrary",)),
    )(dmix, zl, zl, u_all, h_all, conv_w, wa, ba, wx, bx, lam)


def _block_diag(w):
    eye = jnp.eye(HEADS, dtype=w.dtype)
    return jnp.einsum("hij,hk->hikj", w, eye).reshape(LRU_W, LRU_W)


def _diag_blocks(dw):
    per = dw.shape[1] // HEAD_D
    blocks = [dw[:, HEAD_D * b:HEAD_D * (b + 1), HEAD_D * b:HEAD_D * (b + 1)] for b in range(per)]
    return jnp.stack(blocks, axis=1).reshape(HEADS, HEAD_D, HEAD_D)


def _local_step(x, target, sent, small, *, tm=512, tm_ffn=1024):
    t = x.shape[0]
    ones = jnp.ones((1, D_MODEL), f32)
    zeros = jnp.zeros((1, D_MODEL), f32)
    ln1 = (small["ln1_g"], small["ln1_b"])
    ln2 = (small["ln2_g"], small["ln2_b"])
    ln3 = (small["ln3_g"], small["ln3_b"])

    xh1, rs1, hg1, hu1, wg1, wu1, wd1, w_in_g = _ffn1_fwd_gathering(
        x, (sent["ffn1_w_gate"], sent["ffn1_w_up"], sent["ffn1_w_down"]),
        _Exchange([sent["w_in"]], gather=True), tm=tm_ffn, name="ffn1_fwd")
    w_in = jnp.pad(w_in_g.transpose(1, 0, 2).reshape(D_MODEL, IN_COLS), ((0, 0), (0, 21 * LANES - IN_COLS)))
    qkv, zl, zfg, w_out_g, conv_w_g, wd2 = _in_proj(
        xh1, ln1[0], ln1[1], w_in, tm=tm, name="in_proj",
        host=_Exchange([sent["w_out"], sent["conv_w"], sent["ffn2_w_down"]], gather=True))
    w_out = w_out_g.reshape(D_MODEL, D_MODEL)
    conv_w = conv_w_g.transpose(1, 0, 2).reshape(4, LRU_W)
    bfg = jnp.pad(small["b_forget"], ((0, 0), (0, LANES - HEADS)))
    cum = _cum_fwd(zfg, bfg, name="cum_fwd")
    q_aug, k_aug, v_aug = _attn_prep_fwd(qkv, cum, tm=tm, name="attn_prep_fwd")
    o, lse, wg2 = _attn_fwd2(q_aug, k_aug, v_aug, name="attn_fwd",
                             host=_Exchange([sent["ffn2_w_gate"]], gather=True))
    wa_bd = _block_diag(small["rg_wa"]).astype(bf16)
    wx_bd = _block_diag(small["rg_wx"]).astype(bf16)
    ba = small["rg_ba"].reshape(1, LRU_W)
    bx = small["rg_bx"].reshape(1, LRU_W)
    u, h, lru, wu2 = _lru_fwd(zl, conv_w, small["conv_b"], wa_bd, ba, wx_bd, bx, small["lru_lambda"],
                              name="lru_fwd", host=_Exchange([sent["ffn2_w_up"]], gather=True))
    xh2, rs2 = _mmln([(o, 0, FOX_W, w_out, 0, D_MODEL, "nn"), (lru, 0, LRU_W, w_out, 1, D_MODEL, "nn")],
                     tm=tm, name="mix_fwd", resid=("affine", xh1) + ln1, resid_scale=ALPHA, epi="ln_fwd")
    xh3, rs3, hg2, hu2 = _ffn_fwd(xh2, ln2[0], ln2[1], wg2, wu2, wd2, tm=tm_ffn, name="ffn2_fwd")

    dpre3, sq_rows, g_ln3g, g_ln3b = _loss_bwd(xh3, rs3, ln3[0], ln3[1], target, tm=tm, name="loss_bwd")
    dpre2, g_ln2g, g_ln2b, dhg2, dhu2, a2 = _ffn_bwd(dpre3, hg2, hu2, wg2, wu2, wd2,
                                                     (xh2, rs2, ln2[0]), tm=tm_ffn, name="ffn2_bwd")
    wgrad = dict(out_dtype=bf16, tm=D_MODEL, mb=1, tn=FF_TILE, nb=4, tk=512, pair=True)
    wdgrad = dict(out_dtype=bf16, tm=512, mb=4, tn=D_MODEL, nb=1, tk=512, out_scale=0.5, pair=True)
    between_chips = functools.partial(_Exchange, gather=False, chips=True)
    g_wg2 = _mm_tn(xh2, dhg2, name="g_wg2", affine=ln2, **wgrad)
    g_wu2 = _mm_tn(xh2, dhu2, name="g_wu2", affine=ln2, **wgrad)
    g_wd2 = _mm_tn(a2, dpre3, name="g_wd2", **wdgrad)

    dmix = _mmln([(dpre2, 0, D_MODEL, w_out, 0, D_MODEL, "nt")], tm=tm, name="dmix_bwd")
    g_wout_a = _mm(o, dpre2, mode="tn", out_dtype=bf16, tm=512, tn=D_MODEL, tk=512, name="g_wout_fox")
    g_wout_b = _mm(lru, dpre2, mode="tn", out_dtype=bf16, tm=512, tn=D_MODEL, tk=512, name="g_wout_lru")
    dlx, dlg, g_cw, g_cb, g_ba, g_bx, g_lam, g_wa4, g_wx4, *p_wg2 = _lru_bwd(
        dmix, zl, u, h, conv_w, wa_bd, ba, wx_bd, bx, small["lru_lambda"], name="lru_bwd",
        host=between_chips([g_wg2]))
    p_wg2 = p_wg2[0]
    qb_aug, do_aug = _attn_prep_bwd(qkv, cum, lse, dmix, o, tm=tm, name="attn_prep_bwd")
    dq, dcum_q, dk, dv, dcum_k, p_wu2, p_wd2 = _attn_bwd(qb_aug, k_aug, v_aug, do_aug, name="attn_bwd",
                                                         host=between_chips([g_wu2, g_wd2]))
    g_wout_blocked = jnp.concatenate([g_wout_a, g_wout_b], axis=0).reshape(N_DEV, D_MODEL // N_DEV, D_MODEL)
    dfg, g_bf = _cum_bwd(dcum_q, dcum_k, zfg, bfg, name="cum_bwd")

    dz = [(dq, 0, 512), (dk, 1, 512), (dv, 2, 512), (dlx, 3, 512), (dlg, 4, 512), (dfg, 20, LANES)]
    dpre1, g_ln1g, g_ln1b = _mmln(
        [(arr, 0, w, w_in, cb, w, "nt") for (arr, cb, w) in dz],
        tm=tm, name="dx1_bwd", resid=("plain", dpre2), resid_scale=ALPHA, epi="ln_bwd", ln=(xh1, rs1, ln1[0]))
    g_win_main = _mm_tn(xh1, [arr for arr, _, _ in dz[:5]], out_dtype=bf16, tm=D_MODEL, mb=1, tn=512, nb=5, tk=512,
                        name="g_win", affine=ln1, out_blocked=True)
    g_win = [g_win_main[n] for n in range(5)] + [
        _mm(xh1, dfg, mode="tn", out_dtype=bf16, tm=D_MODEL, tn=LANES, tk=512, name="g_win_fg", affine=ln1)]
    g_win_full = jnp.concatenate([g[:, :w] for g, (_, _, w) in zip(g_win, dz)], axis=1)[:, :IN_COLS]
    g_win_blocked = g_win_full.reshape(D_MODEL, N_DEV, IN_SHARD).transpose(1, 0, 2)
    dhg1, dhu1, a1, p_win, p_wout = _ffn_bwd_act(dpre1, hg1, hu1, wd1, tm=tm_ffn, name="ffn1_bwd_act",
                                                 host=_Exchange([g_win_blocked, g_wout_blocked], gather=False))
    small_g = {
        "ln1_g": g_ln1g, "ln1_b": g_ln1b, "b_forget": g_bf[:, :HEADS], "conv_w": g_cw, "conv_b": g_cb,
        "rg_wa": _diag_blocks(g_wa4), "rg_ba": g_ba.reshape(HEADS, HEAD_D),
        "rg_wx": _diag_blocks(g_wx4), "rg_bx": g_bx.reshape(HEADS, HEAD_D), "lru_lambda": g_lam,
        "ln2_g": g_ln2g, "ln2_b": g_ln2b, "ln3_g": g_ln3g, "ln3_b": g_ln3b,
    }
    small_g["loss"] = (0.5 / D_MODEL) * jnp.sum(sq_rows, keepdims=True)
    pieces = [small_g[n].reshape(-1) for n in PACKED]
    packed = jnp.concatenate(pieces + [jnp.zeros((PACK_ROWS * LANES - sum(p.shape[0] for p in pieces),), f32)])
    g_wg1, all_packed = _mm_tn(x, dhg1, name="g_wg1",
                               host=_Exchange([packed.reshape(PACK_ROWS, LANES)], gather=True), **wgrad)
    g_wu1, p_wg1 = _mm_tn(x, dhu1, name="g_wu1", host=between_chips([g_wg1]), **wgrad)
    g_wd1, p_wu1 = _mm_tn(a1, dpre1, name="g_wd1", host=between_chips([g_wu1]), **wdgrad)
    grad_x, p_wd1 = _ffn_bwd_dx(dpre1, dhg1, dhu1, wg1, wu1, tm=tm_ffn, name="ffn1_bwd_dx",
                                host=between_chips([g_wd1]))
    parts = {
        "ffn1_w_gate": p_wg1, "ffn1_w_up": p_wu1, "ffn1_w_down": p_wd1, "w_in": p_win, "w_out": p_wout,
        "ffn2_w_gate": p_wg2, "ffn2_w_up": p_wu2, "ffn2_w_down": p_wd2,
    }
    return sq_rows, grad_x, parts, all_packed, {n: small_g[n].shape for n in PACKED}


def _adam_math(w, g, m, v):
    m2 = ADAM_B1 * m + (1.0 - ADAM_B1) * g
    v2 = ADAM_B2 * v + (1.0 - ADAM_B2) * (g * g)
    m_hat = m2 / (1.0 - ADAM_B1 ** ADAM_STEP)
    v_hat = v2 / (1.0 - ADAM_B2 ** ADAM_STEP)
    delta = -ADAM_LR * (m_hat / (jnp.sqrt(v_hat) + ADAM_EPS) + ADAM_WD * w)
    return delta, m2, v2


ADAM_TILE_ELEMS = 128 * 1024


def _adamw_big(parts, w, m, v, *, name):
    r, c = w.shape
    n_parts = parts.shape[0]
    tr = max(d for d in range(8, r + 1, 8) if r % d == 0 and d * c <= ADAM_TILE_ELEMS)

    def body(p_ref, w_ref, m_ref, v_ref, g_ref, d_ref, m2_ref, v2_ref):
        g = p_ref[0].astype(f32)
        for q in range(1, n_parts):
            g = g + p_ref[q].astype(f32)
        d, m2, v2 = _adam_math(w_ref[...], g, m_ref[...], v_ref[...])
        g_ref[...] = g
        d_ref[...] = d
        m2_ref[...] = m2
        v2_ref[...] = v2

    blk = pl.BlockSpec((tr, c), lambda i: (i, 0))
    return pl.pallas_call(
        body, name=name, grid=(r // tr,),
        in_specs=[pl.BlockSpec((n_parts, tr, c), lambda i: (0, i, 0)), blk, blk, blk],
        out_specs=[blk] * 4, out_shape=[jax.ShapeDtypeStruct((r, c), f32)] * 4,
        compiler_params=_params(("arbitrary",)),
    )(parts, w, m, v)


def _adamw_small(items, *, name):
    n = len(items)

    def body(*refs):
        ins, outs = refs[:4 * n], refs[4 * n:]
        for k in range(n):
            g, w, m, v = (ins[4 * k + q][...] for q in range(4))
            d, m2, v2 = _adam_math(w, g, m, v)
            outs[3 * k][...] = d
            outs[3 * k + 1][...] = m2
            outs[3 * k + 2][...] = v2

    vm = pl.BlockSpec(memory_space=pltpu.VMEM)
    flat = [a for item in items for a in item]
    out_shape = [jax.ShapeDtypeStruct(item[1].shape, f32) for item in items for _ in range(3)]
    return pl.pallas_call(
        body, name=name, in_specs=[vm] * (4 * n), out_specs=[vm] * (3 * n), out_shape=out_shape,
    )(*flat)


def _sum_parts(parts, *, name):
    def body(p_ref, o_ref):
        acc = p_ref[0]
        for q in range(1, N_DEV):
            acc = acc + p_ref[q]
        o_ref[...] = acc

    vm = pl.BlockSpec(memory_space=pltpu.VMEM)
    return pl.pallas_call(
        body, name=name, in_specs=[vm], out_specs=vm, out_shape=jax.ShapeDtypeStruct(parts.shape[1:], f32),
    )(parts)


WEIGHTS = ["ffn1_w_gate", "ffn1_w_up", "ffn1_w_down", "ln1_g", "ln1_b", "w_in", "b_forget", "conv_w", "conv_b",
           "rg_wa", "rg_ba", "rg_wx", "rg_bx", "lru_lambda", "w_out", "ln2_g", "ln2_b",
           "ffn2_w_gate", "ffn2_w_up", "ffn2_w_down", "ln3_g", "ln3_b"]
BIG = ["ffn1_w_gate", "ffn1_w_up", "ffn1_w_down", "w_in", "w_out", "ffn2_w_gate", "ffn2_w_up", "ffn2_w_down"]
PACKED = ["ln1_g", "ln1_b", "ln2_g", "ln2_b", "ln3_g", "ln3_b", "conv_b", "rg_ba", "rg_bx", "lru_lambda",
          "conv_w", "rg_wa", "rg_wx", "b_forget", "loss"]
PACK_ROWS = 600


def _two_d(a):
    return a.reshape((-1, a.shape[-1]))


def _transport(a):
    return _two_d(a)


def kernel(x, ffn1_w_gate, ffn1_w_up, ffn1_w_down, ln1_g, ln1_b, w_in, b_forget, conv_w, conv_b, rg_wa, rg_ba, rg_wx, rg_bx, lru_lambda, w_out, ln2_g, ln2_b, ffn2_w_gate, ffn2_w_up, ffn2_w_down, ln3_g, ln3_b, loss_target, m_ffn1_w_gate, m_ffn1_w_up, m_ffn1_w_down, m_ln1_g, m_ln1_b, m_w_in, m_b_forget, m_conv_w, m_conv_b, m_rg_wa, m_rg_ba, m_rg_wx, m_rg_bx, m_lru_lambda, m_w_out, m_ln2_g, m_ln2_b, m_ffn2_w_gate, m_ffn2_w_up, m_ffn2_w_down, m_ln3_g, m_ln3_b, v_ffn1_w_gate, v_ffn1_w_up, v_ffn1_w_down, v_ln1_g, v_ln1_b, v_w_in, v_b_forget, v_conv_w, v_conv_b, v_rg_wa, v_rg_ba, v_rg_wx, v_rg_bx, v_lru_lambda, v_w_out, v_ln2_g, v_ln2_b, v_ffn2_w_gate, v_ffn2_w_up, v_ffn2_w_down, v_ln3_g, v_ln3_b):
    w_args = (ffn1_w_gate, ffn1_w_up, ffn1_w_down, ln1_g, ln1_b, w_in, b_forget, conv_w, conv_b, rg_wa, rg_ba, rg_wx, rg_bx, lru_lambda, w_out, ln2_g, ln2_b, ffn2_w_gate, ffn2_w_up, ffn2_w_down, ln3_g, ln3_b)
    m_args = (m_ffn1_w_gate, m_ffn1_w_up, m_ffn1_w_down, m_ln1_g, m_ln1_b, m_w_in, m_b_forget, m_conv_w, m_conv_b, m_rg_wa, m_rg_ba, m_rg_wx, m_rg_bx, m_lru_lambda, m_w_out, m_ln2_g, m_ln2_b, m_ffn2_w_gate, m_ffn2_w_up, m_ffn2_w_down, m_ln3_g, m_ln3_b)
    v_args = (v_ffn1_w_gate, v_ffn1_w_up, v_ffn1_w_down, v_ln1_g, v_ln1_b, v_w_in, v_b_forget, v_conv_w, v_conv_b, v_rg_wa, v_rg_ba, v_rg_wx, v_rg_bx, v_lru_lambda, v_w_out, v_ln2_g, v_ln2_b, v_ffn2_w_gate, v_ffn2_w_up, v_ffn2_w_down, v_ln3_g, v_ln3_b)
    w = dict(zip(WEIGHTS, w_args))
    m = dict(zip(WEIGHTS, m_args))
    v = dict(zip(WEIGHTS, v_args))
    me = 4 * lax.axis_index("x") + 2 * lax.axis_index("y") + lax.axis_index("c")

    sent = {n: _transport(w[n]).astype(bf16) for n in BIG}
    sent["conv_w"] = _two_d(w["conv_w"])
    small = {n: w[n] for n in ("ln1_g", "ln1_b", "ln2_g", "ln2_b", "ln3_g", "ln3_b", "b_forget", "conv_b",
                               "lru_lambda")}
    small.update({n: w[n][0] for n in ("rg_wa", "rg_ba", "rg_wx", "rg_bx")})

    sq_rows, grad_x, parts, all_packed, small_shapes = _local_step(x[0], loss_target[0], sent, small)

    total = _sum_parts(all_packed, name="sum_small_grads").reshape(-1)
    grads, off = {}, 0
    for n in PACKED:
        size = math.prod(small_shapes[n])
        grads[n] = total[off:off + size].reshape(small_shapes[n])
        off += size
    loss = grads.pop("loss").reshape(())
    grads["conv_w"] = lax.dynamic_slice_in_dim(grads["conv_w"], me * (LRU_W // N_DEV), LRU_W // N_DEV, axis=1)

    delta, new_m, new_v = {}, {}, {}
    for n in BIG:
        g, d, m2, v2 = _adamw_big(parts[n], _transport(w[n]), _transport(m[n]), _transport(v[n]),
                                  name="adamw_" + n)
        grads[n], delta[n], new_m[n], new_v[n] = g, d, m2, v2
    small_names = [n for n in WEIGHTS if n not in BIG]
    outs = _adamw_small([(_two_d(grads[n]), _two_d(w[n]), _two_d(m[n]), _two_d(v[n])) for n in small_names],
                        name="adamw_small")
    for k, n in enumerate(small_names):
        delta[n], new_m[n], new_v[n] = outs[3 * k], outs[3 * k + 1], outs[3 * k + 2]

    def shaped(d):
        return [d[n].reshape(w[n].shape) for n in WEIGHTS]

    return (loss, grad_x[None], *shaped(grads), *shaped(delta), *shaped(new_m), *shaped(new_v))
```

```python
import functools
import math

import jax
import jax.numpy as jnp
from jax import lax
from jax.experimental import pallas as pl
from jax.experimental.pallas import tpu as pltpu

f32 = jnp.float32
bf16 = jnp.bfloat16

N_DEV = 8
D_MODEL = 1024
D_FF = 4096
FF_TILE = D_FF // N_DEV
FOX_W = 512
LRU_W = 512
HEADS = 8
HEAD_D = 64
IN_COLS = 2568
IN_SHARD = IN_COLS // N_DEV
LANES = 128
LN_EPS = 1e-5
ALPHA = 2.0 ** 0.25
ATT_SCALE = 1.0 / math.sqrt(HEAD_D)
LRU_C = 8.0
NEG_BIG = -1e30

ADAM_LR = 0.001
ADAM_B1 = 0.9
ADAM_B2 = 0.999
ADAM_EPS = 1e-08
ADAM_WD = 0.01
ADAM_STEP = 10

VMEM_LIMIT = 56 * 1024 * 1024
MESH_T = pl.DeviceIdType.MESH


def _params(sem, **kw):
    return pltpu.CompilerParams(dimension_semantics=sem, vmem_limit_bytes=VMEM_LIMIT, **kw)


def _sigmoid(x):
    return 1.0 / (1.0 + jnp.exp(-x))


def _sigmoid_tanh(x):
    return 0.5 * jnp.tanh(0.5 * x) + 0.5


def _softplus(x):
    return jnp.maximum(x, 0.0) + jnp.log(1.0 + jnp.exp(-jnp.abs(x)))


def _one_minus_exp(x):
    series = -x * (1.0 + x * (0.5 + x * (1.0 / 6 + x * (1.0 / 24 + x * (1.0 / 120 + x * (1.0 / 720))))))
    return jnp.where(x > -0.125, series, 1.0 - jnp.exp(x))


_GELU_C = math.sqrt(2.0 / math.pi)


def _gelu_and_grad(x):
    inner = _GELU_C * (x + 0.044715 * x * x * x)
    t = jnp.tanh(inner)
    g = 0.5 * x * (1.0 + t)
    dg = 0.5 * (1.0 + t) + 0.5 * x * (1.0 - t * t) * _GELU_C * (1.0 + 3 * 0.044715 * x * x)
    return g, dg


def _ln_fwd_tile(pre):
    mu = jnp.mean(pre, axis=-1, keepdims=True)
    xc = pre - mu
    var = jnp.mean(xc * xc, axis=-1, keepdims=True)
    rstd = lax.rsqrt(var + LN_EPS)
    return xc * rstd, rstd


def _ln_bwd_tile(dy, xhat, rstd, g):
    dyg = dy * g
    m1 = jnp.mean(dyg, axis=-1, keepdims=True)
    m2 = jnp.mean(dyg * xhat, axis=-1, keepdims=True)
    dpre = rstd * (dyg - m1 - xhat * m2)
    return dpre, jnp.sum(dy * xhat, axis=0, keepdims=True), jnp.sum(dy, axis=0, keepdims=True)


_NT = (((1,), (1,)), ((), ()))
_TN = (((0,), (0,)), ((), ()))


class _Exchange:
    def __init__(self, arrs, gather, chips=False):
        self.arrs, self.gather, self.n, self.chips = list(arrs), gather, len(arrs), chips

    def out_shape(self):
        return [jax.ShapeDtypeStruct(((N_DEV,) + a.shape) if self.gather else a.shape, a.dtype) for a in self.arrs]

    def scratch(self):
        n_remote = self.n * (N_DEV - 1)
        return [pltpu.SemaphoreType.DMA((n_remote,)), pltpu.SemaphoreType.DMA((n_remote,)),
                pltpu.SemaphoreType.DMA((self.n,))]

    def copies(self, ins, outs, sems):
        send_sems, recv_sems, local_sems = sems
        x, y, c = lax.axis_index("x"), lax.axis_index("y"), lax.axis_index("c")
        me = 2 * x + y if self.chips else 4 * x + 2 * y + c
        out = []
        for k in range(self.n):
            for d in (range(2, N_DEV, 2) if self.chips else range(1, N_DEV)):
                px = 1 - x if d & 4 else x
                py = 1 - y if d & 2 else y
                pc = 1 - c if d & 1 else c
                sem = k * (N_DEV - 1) + d - 1
                out.append(pltpu.make_async_remote_copy(
                    src_ref=ins[k].at[2 * px + py if self.chips else 4 * px + 2 * py + pc], dst_ref=outs[k].at[me],
                    send_sem=send_sems.at[sem], recv_sem=recv_sems.at[sem],
                    device_id=(px, py, pc), device_id_type=MESH_T))
            out.append(pltpu.make_async_copy(ins[k].at[me], outs[k].at[me], local_sems.at[k]))
        return out

    def gather_copies(self, ins, outs, sems):
        send_sems, recv_sems, local_sems = sems
        x, y, c = lax.axis_index("x"), lax.axis_index("y"), lax.axis_index("c")
        sibling = (x, y, 1 - c)
        chips = [(1 - x, y), (x, 1 - y), (1 - x, 1 - y)]
        out = []
        for k in range(self.n):
            def copy(s, block, to, src=None, k=k):
                rows = outs[k].at[4 * block[0] + 2 * block[1] + block[2]]
                sem = k * (N_DEV - 1) + s
                return pltpu.make_async_remote_copy(
                    src_ref=rows if src is None else src, dst_ref=rows, send_sem=send_sems.at[sem],
                    recv_sem=recv_sems.at[sem], device_id=to, device_id_type=MESH_T)

            first = [copy(0, (x, y, c), sibling, src=ins[k])]
            first += [copy(1 + q, (x, y, c), (*chip, c), src=ins[k]) for q, chip in enumerate(chips)]
            passed = [copy(4 + q, (*chip, c), sibling) for q, chip in enumerate(chips)]
            own = pltpu.make_async_copy(ins[k], outs[k].at[4 * x + 2 * y + c], local_sems.at[k])
            out.append((first, passed, own, copy))
        return out, sibling, chips, (x, y, c)

    def start(self, ins, outs, sems):
        if not self.gather:
            for cp in self.copies(ins, outs, sems):
                cp.start()
            return
        per_array, _, _, _ = self.gather_copies(ins, outs, sems)
        for first, _, own, _ in per_array:
            own.start()
            for cp in first:
                cp.start()

    def relay(self, ins, outs, sems):
        per_array, sibling, chips, (x, y, c) = self.gather_copies(ins, outs, sems)
        for first, passed, own, copy in per_array:
            for q, chip in enumerate(chips):
                copy(1 + q, (*chip, c), (x, y, c)).wait_recv()
                passed[q].start()

    def wait(self, ins, outs, sems, relayed=False):
        if not self.gather:
            for cp in self.copies(ins, outs, sems):
                cp.wait()
            return
        if not relayed:
            self.relay(ins, outs, sems)
        per_array, sibling, chips, (x, y, c) = self.gather_copies(ins, outs, sems)
        for first, passed, own, copy in per_array:
            copy(0, sibling, (x, y, c)).wait_recv()
            for q, chip in enumerate(chips):
                copy(4 + q, (*chip, 1 - c), (x, y, c)).wait_recv()
            for cp in first + passed:
                cp.wait_send()
            own.wait()


class _Hosts:
    gather = False

    def __init__(self, *hosts):
        self.hosts = hosts
        self.n = sum(h.n for h in hosts)
        self.arrs = [a for h in hosts for a in h.arrs]

    def out_shape(self):
        return [sh for h in self.hosts for sh in h.out_shape()]

    def scratch(self):
        return [sc for h in self.hosts for sc in h.scratch()]

    def _each(self, ins, outs, sems):
        at = 0
        for k, h in enumerate(self.hosts):
            yield h, ins[at:at + h.n], outs[at:at + h.n], sems[3 * k:3 * k + 3]
            at += h.n

    def start(self, ins, outs, sems):
        for h, h_in, h_out, h_sems in self._each(ins, outs, sems):
            h.start(h_in, h_out, h_sems)

    def wait(self, ins, outs, sems, relayed=False):
        for h, h_in, h_out, h_sems in self._each(ins, outs, sems):
            h.wait(h_in, h_out, h_sems)


def _hosted_call(host, body, *, name, grid, in_specs, out_specs, out_shape, scratch_shapes=(), compiler_params):
    out_specs = list(out_specs) if isinstance(out_specs, (list, tuple)) else [out_specs]
    out_shape = list(out_shape) if isinstance(out_shape, (list, tuple)) else [out_shape]
    if host is None:
        return pl.pallas_call(body, name=name, grid=grid, in_specs=in_specs, out_specs=out_specs,
                              out_shape=out_shape, scratch_shapes=list(scratch_shapes),
                              compiler_params=compiler_params)
    n_in, n_out, n_scr, k = len(in_specs), len(out_shape), len(scratch_shapes), host.n

    def wrapped(*refs):
        ins, h_in = refs[:n_in], refs[n_in:n_in + k]
        outs, h_out = refs[n_in + k:n_in + k + n_out], refs[n_in + k + n_out:n_in + 2 * k + n_out]
        scr, sems = refs[n_in + 2 * k + n_out:n_in + 2 * k + n_out + n_scr], refs[n_in + 2 * k + n_out + n_scr:]
        ids = [pl.program_id(a) for a in range(len(grid))]
        first = functools.reduce(jnp.logical_and, [i == 0 for i in ids])
        last = functools.reduce(jnp.logical_and, [i == g - 1 for i, g in zip(ids, grid)])
        steps = math.prod(grid)
        relay_at = (3 * steps) // 4 if host.gather and steps >= 8 else None

        @pl.when(first)
        def _():
            host.start(h_in, h_out, sems)

        if relay_at is not None:
            coords, rest = [], relay_at
            for g in reversed(grid):
                coords.append(rest % g)
                rest //= g

            @pl.when(functools.reduce(jnp.logical_and, [i == cd for i, cd in zip(ids, reversed(coords))]))
            def _():
                host.relay(h_in, h_out, sems)

        body(*ins, *outs, *scr)

        @pl.when(last)
        def _():
            host.wait(h_in, h_out, sems, relayed=relay_at is not None)

    hbm = pl.BlockSpec(memory_space=pl.ANY)
    call = pl.pallas_call(
        wrapped, name=name, grid=grid, in_specs=list(in_specs) + [hbm] * k, out_specs=out_specs + [hbm] * k,
        out_shape=out_shape + host.out_shape(), scratch_shapes=list(scratch_shapes) + host.scratch(),
        compiler_params=compiler_params)
    return lambda *args: call(*args, *host.arrs)


def _exchange(arrs, *, gather, name):
    host = _Exchange(arrs, gather)

    def body(*refs):
        ins, outs, sems = refs[:host.n], refs[host.n:2 * host.n], refs[2 * host.n:]
        host.start(ins, outs, sems)
        host.wait(ins, outs, sems)

    hbm = pl.BlockSpec(memory_space=pl.ANY)
    return pl.pallas_call(
        body, name=name, in_specs=[hbm] * host.n, out_specs=[hbm] * host.n, out_shape=host.out_shape(),
        scratch_shapes=host.scratch(), compiler_params=pltpu.CompilerParams(has_side_effects=True),
    )(*arrs)


def _ffn_fwd(xhat, g_in, b_in, wg, wu, wd, *, tm, name, host=None):
    t = xhat.shape[0]
    nj = N_DEV

    def body(x_ref, g_ref, b_ref, wg_ref, wu_ref, wd_ref, xo_ref, rstd_ref, hg_ref, hu_ref, xb, acc):
        j = pl.program_id(1)

        @pl.when(j == 0)
        def _():
            xb[...] = (x_ref[...] * g_ref[...] + b_ref[...]).astype(bf16)
            acc[...] = jnp.zeros_like(acc)

        hg = jnp.dot(xb[...], wg_ref[...], preferred_element_type=f32)
        hu = jnp.dot(xb[...], wu_ref[...], preferred_element_type=f32)
        hg_ref[...] = hg.astype(bf16)
        hu_ref[...] = hu.astype(bf16)
        a = hg * _sigmoid_tanh(hg) * hu
        acc[...] += jnp.dot(a.astype(bf16), wd_ref[...], preferred_element_type=f32)

        @pl.when(j == nj - 1)
        def _():
            x = x_ref[...] * g_ref[...] + b_ref[...]
            xo, rstd = _ln_fwd_tile(ALPHA * x + 0.5 * acc[...])
            xo_ref[...] = xo
            rstd_ref[...] = rstd

    row = pl.BlockSpec((1, D_MODEL), lambda i, j: (0, 0))
    return _hosted_call(
        host, body, name=name, grid=(t // tm, nj),
        in_specs=[pl.BlockSpec((tm, D_MODEL), lambda i, j: (i, 0)), row, row,
                  pl.BlockSpec((None, D_MODEL, FF_TILE), lambda i, j: (j, 0, 0)),
                  pl.BlockSpec((None, D_MODEL, FF_TILE), lambda i, j: (j, 0, 0)),
                  pl.BlockSpec((None, FF_TILE, D_MODEL), lambda i, j: (j, 0, 0))],
        out_specs=[pl.BlockSpec((tm, D_MODEL), lambda i, j: (i, 0)),
                   pl.BlockSpec((tm, 1), lambda i, j: (i, 0)),
                   pl.BlockSpec((tm, FF_TILE), lambda i, j: (i, j)),
                   pl.BlockSpec((tm, FF_TILE), lambda i, j: (i, j))],
        out_shape=[jax.ShapeDtypeStruct((t, D_MODEL), f32), jax.ShapeDtypeStruct((t, 1), f32),
                   jax.ShapeDtypeStruct((t, D_FF), bf16), jax.ShapeDtypeStruct((t, D_FF), bf16)],
        scratch_shapes=[pltpu.VMEM((tm, D_MODEL), bf16), pltpu.VMEM((tm, D_MODEL), f32)],
        compiler_params=_params(("arbitrary", "arbitrary")),
    )(xhat, g_in, b_in, wg, wu, wd)


def _ffn1_fwd_gathering(x, own, extra, *, tm, name):
    t = x.shape[0]
    n_i = t // tm
    n_arr = 3
    k_extra = extra.n
    ex = _Exchange(list(own), gather=True)
    ax, ay, ac = lax.axis_index("x"), lax.axis_index("y"), lax.axis_index("c")
    order = jnp.stack([4 * px + 2 * py + pc for px, py in ((ax, ay), (1 - ax, ay), (ax, 1 - ay), (1 - ax, 1 - ay))
                       for pc in (ac, 1 - ac)]).astype(jnp.int32)
    arrival = [None, (0, None), (1, 0), (4, None), (2, 1), (5, None), (3, 2), (6, None)]

    def body(order_ref, x_ref, *refs):
        w_in, e_in = refs[:n_arr], refs[n_arr:n_arr + k_extra]
        refs = refs[n_arr + k_extra:]
        xo_ref, rstd_ref, hg_ref, hu_ref = refs[:4]
        w_all, e_out = refs[4:4 + n_arr], refs[4 + n_arr:4 + n_arr + k_extra]
        acc, wgb, wub, wdb, fetch_sems, send_sems, recv_sems, local_sems = refs[4 + n_arr + k_extra:12 + n_arr + k_extra]
        e_sems = refs[12 + n_arr + k_extra:]
        bufs = (wgb, wub, wdb)
        s = pl.program_id(0)
        i = pl.program_id(1)
        per_array, sibling, chips, (x_, y_, c_) = ex.gather_copies(w_in, w_all, (send_sems, recv_sems, local_sems))

        def fetch(pos, slot):
            return [pltpu.make_async_copy(w_in[a] if pos == 0 else w_all[a].at[order_ref[pos]],
                                          bufs[a].at[slot], fetch_sems.at[n_arr * slot + a]) for a in range(n_arr)]

        def source_of(pos):
            chip = (x_, y_) if pos < 2 else chips[(pos - 2) // 2]
            return (*chip, c_ if pos % 2 == 0 else 1 - c_)

        @pl.when(jnp.logical_and(s == 0, i == 0))
        def _():
            for q in range(4):
                for first, _, own_copy, _ in per_array:
                    if q == 0:
                        own_copy.start()
                    first[q].start()
            for cp in fetch(0, 0):
                cp.start()
            for cp in fetch(0, 0):
                cp.wait()

        @pl.when(jnp.logical_and(s == N_DEV // 2, i == 0))
        def _():
            extra.start(e_in, e_out, e_sems)

        for pos in range(1, N_DEV):
            @pl.when(jnp.logical_and(s == pos - 1, i == n_i - 1))
            def _(pos=pos):
                sem, passes = arrival[pos]
                for _, passed, _, copy in per_array:
                    copy(sem, source_of(pos), (x_, y_, c_)).wait_recv()
                    if passes is not None:
                        passed[passes].start()
                for cp in fetch(pos, pos % 2):
                    cp.start()

            @pl.when(jnp.logical_and(s == pos, i == 0))
            def _(pos=pos):
                for cp in fetch(pos, pos % 2):
                    cp.wait()

        slot = s % 2
        xb = x_ref[...].astype(bf16)
        hg = jnp.dot(xb, wgb[slot], preferred_element_type=f32)
        hu = jnp.dot(xb, wub[slot], preferred_element_type=f32)
        hg_ref[...] = hg.astype(bf16)
        hu_ref[...] = hu.astype(bf16)
        a = hg * _sigmoid_tanh(hg) * hu
        part = jnp.dot(a.astype(bf16), wdb[slot], preferred_element_type=f32)

        @pl.when(s == 0)
        def _():
            acc[i] = part

        @pl.when(s > 0)
        def _():
            acc[i] += part

        @pl.when(s == N_DEV - 1)
        def _():
            xo, rstd = _ln_fwd_tile(ALPHA * x_ref[...] + 0.5 * acc[i])
            xo_ref[...] = xo
            rstd_ref[...] = rstd

        @pl.when(jnp.logical_and(s == N_DEV - 1, i == n_i - 1))
        def _():
            for first, passed, own_copy, _ in per_array:
                for cp in first + passed:
                    cp.wait_send()
                own_copy.wait()
            extra.wait(e_in, e_out, e_sems)

    hbm = pl.BlockSpec(memory_space=pl.ANY)
    last = N_DEV - 1
    tok_out = pl.BlockSpec((tm, D_MODEL), lambda s, i, o: (jnp.where(s == last, i, 0), 0))
    col_out = pl.BlockSpec((tm, 1), lambda s, i, o: (jnp.where(s == last, i, 0), 0))
    hid = pl.BlockSpec((tm, FF_TILE), lambda s, i, o: (i, o[s]))
    shard_shapes = [(N_DEV,) + w.shape for w in own]
    grid_spec = pltpu.PrefetchScalarGridSpec(
        num_scalar_prefetch=1, grid=(N_DEV, n_i),
        in_specs=[pl.BlockSpec((tm, D_MODEL), lambda s, i, o: (i, 0))] + [hbm] * (n_arr + k_extra),
        out_specs=[tok_out, col_out, hid, hid] + [hbm] * (n_arr + k_extra),
        scratch_shapes=[pltpu.VMEM((n_i, tm, D_MODEL), f32)]
        + [pltpu.VMEM((2,) + w.shape, bf16) for w in own]
        + [pltpu.SemaphoreType.DMA((2 * n_arr,))] + ex.scratch() + extra.scratch())
    res = pl.pallas_call(
        body, name=name, grid_spec=grid_spec,
        out_shape=[jax.ShapeDtypeStruct((t, D_MODEL), f32), jax.ShapeDtypeStruct((t, 1), f32),
                   jax.ShapeDtypeStruct((t, D_FF), bf16), jax.ShapeDtypeStruct((t, D_FF), bf16)]
        + [jax.ShapeDtypeStruct(sh, bf16) for sh in shard_shapes] + extra.out_shape(),
        compiler_params=_params(("arbitrary", "arbitrary")),
    )(order, x, *own, *extra.arrs)
    return res


def _ffn_bwd(dpre, hg, hu, wg, wu, wd, ln_in, *, tm, name, host=None):
    t = dpre.shape[0]
    nj = N_DEV
    with_ln = ln_in is not None

    def body(*refs):
        if with_ln:
            (dp_ref, hg_ref, hu_ref, wg_ref, wu_ref, wd_ref, xh_ref, rs_ref, g_ref,
             dx_ref, gg_ref, gb_ref, dhg_ref, dhu_ref, a_ref, dfb, acc) = refs
        else:
            (dp_ref, hg_ref, hu_ref, wg_ref, wu_ref, wd_ref,
             dx_ref, dhg_ref, dhu_ref, a_ref, dfb, acc) = refs
        i = pl.program_id(0)
        j = pl.program_id(1)

        @pl.when(j == 0)
        def _():
            dfb[...] = (0.5 * dp_ref[...]).astype(bf16)
            acc[...] = jnp.zeros_like(acc)

        da = lax.dot_general(dfb[...], wd_ref[...], _NT, preferred_element_type=f32)
        hgv = hg_ref[...].astype(f32)
        huv = hu_ref[...].astype(f32)
        sg = _sigmoid_tanh(hgv)
        silu = hgv * sg
        a_ref[...] = (silu * huv).astype(bf16)
        dhu = (da * silu).astype(bf16)
        dhg = (da * huv * (sg * (1.0 + hgv * (1.0 - sg)))).astype(bf16)
        dhg_ref[...] = dhg
        dhu_ref[...] = dhu
        acc[...] += (lax.dot_general(dhg, wg_ref[...], _NT, preferred_element_type=f32)
                     + lax.dot_general(dhu, wu_ref[...], _NT, preferred_element_type=f32))

        @pl.when(j == nj - 1)
        def _():
            dx = ALPHA * dp_ref[...] + acc[...]
            if with_ln:
                dprev, gg, gb = _ln_bwd_tile(dx, xh_ref[...], rs_ref[...], g_ref[...])
                dx_ref[...] = dprev

                @pl.when(i == 0)
                def _():
                    gg_ref[...] = gg
                    gb_ref[...] = gb

                @pl.when(i > 0)
                def _():
                    gg_ref[...] += gg
                    gb_ref[...] += gb
            else:
                dx_ref[...] = dx

    tok = pl.BlockSpec((tm, D_MODEL), lambda i, j: (i, 0), pipeline_mode=pl.Buffered(1))
    row = pl.BlockSpec((1, D_MODEL), lambda i, j: (0, 0))
    hid = pl.BlockSpec((tm, FF_TILE), lambda i, j: (i, j))
    in_specs = [tok, hid, hid,
                pl.BlockSpec((None, D_MODEL, FF_TILE), lambda i, j: (j, 0, 0)),
                pl.BlockSpec((None, D_MODEL, FF_TILE), lambda i, j: (j, 0, 0)),
                pl.BlockSpec((None, FF_TILE, D_MODEL), lambda i, j: (j, 0, 0))]
    args = [dpre, hg, hu, wg, wu, wd]
    out_specs = [tok]
    out_shape = [jax.ShapeDtypeStruct((t, D_MODEL), f32)]
    if with_ln:
        in_specs += [tok, pl.BlockSpec((tm, 1), lambda i, j: (i, 0)), row]
        args += list(ln_in)
        out_specs += [row, row]
        out_shape += [jax.ShapeDtypeStruct((1, D_MODEL), f32)] * 2
    out_specs += [hid, hid, hid]
    out_shape += [jax.ShapeDtypeStruct((t, D_FF), bf16)] * 3
    return _hosted_call(
        host, body, name=name, grid=(t // tm, nj), in_specs=in_specs, out_specs=out_specs, out_shape=out_shape,
        scratch_shapes=[pltpu.VMEM((tm, D_MODEL), bf16), pltpu.VMEM((tm, D_MODEL), f32)],
        compiler_params=_params(("arbitrary", "arbitrary")),
    )(*args)


def _ffn_bwd_act(dpre, hg, hu, wd, *, tm, name, host=None):
    t = dpre.shape[0]

    def body(dp_ref, hg_ref, hu_ref, wd_ref, dhg_ref, dhu_ref, a_ref, dfb):
        @pl.when(pl.program_id(1) == 0)
        def _():
            dfb[...] = (0.5 * dp_ref[...]).astype(bf16)

        da = lax.dot_general(dfb[...], wd_ref[...], _NT, preferred_element_type=f32)
        hgv = hg_ref[...].astype(f32)
        huv = hu_ref[...].astype(f32)
        sg = _sigmoid_tanh(hgv)
        silu = hgv * sg
        a_ref[...] = (silu * huv).astype(bf16)
        dhu_ref[...] = (da * silu).astype(bf16)
        dhg_ref[...] = (da * huv * (sg * (1.0 + hgv * (1.0 - sg)))).astype(bf16)

    hid = pl.BlockSpec((tm, FF_TILE), lambda i, j: (i, j))
    return _hosted_call(
        host, body, name=name, grid=(t // tm, N_DEV),
        in_specs=[pl.BlockSpec((tm, D_MODEL), lambda i, j: (i, 0)), hid, hid,
                  pl.BlockSpec((None, FF_TILE, D_MODEL), lambda i, j: (j, 0, 0))],
        out_specs=[hid, hid, hid], out_shape=[jax.ShapeDtypeStruct((t, D_FF), bf16)] * 3,
        scratch_shapes=[pltpu.VMEM((tm, D_MODEL), bf16)],
        compiler_params=_params(("arbitrary", "arbitrary")),
    )(dpre, hg, hu, wd)


def _ffn_bwd_dx(dpre, dhg, dhu, wg, wu, *, tm, name, host=None):
    t = dpre.shape[0]
    nj = N_DEV

    def body(dp_ref, dhg_ref, dhu_ref, wg_ref, wu_ref, dx_ref, acc):
        j = pl.program_id(1)

        @pl.when(j == 0)
        def _():
            acc[...] = jnp.zeros_like(acc)

        acc[...] += (lax.dot_general(dhg_ref[...], wg_ref[...], _NT, preferred_element_type=f32)
                     + lax.dot_general(dhu_ref[...], wu_ref[...], _NT, preferred_element_type=f32))

        @pl.when(j == nj - 1)
        def _():
            dx_ref[...] = ALPHA * dp_ref[...] + acc[...]

    tok = pl.BlockSpec((tm, D_MODEL), lambda i, j: (i, 0))
    hid = pl.BlockSpec((tm, FF_TILE), lambda i, j: (i, j))
    wspec = pl.BlockSpec((None, D_MODEL, FF_TILE), lambda i, j: (j, 0, 0))
    return _hosted_call(
        host, body, name=name, grid=(t // tm, nj), in_specs=[tok, hid, hid, wspec, wspec],
        out_specs=[tok], out_shape=[jax.ShapeDtypeStruct((t, D_MODEL), f32)],
        scratch_shapes=[pltpu.VMEM((tm, D_MODEL), f32)],
        compiler_params=_params(("arbitrary", "arbitrary")),
    )(dpre, dhg, dhu, wg, wu)


def _mm(a, b, *, mode, out_dtype, tm, tn, tk, name, affine=None, a_cols=None, b_cols=None,
        b_blocked=False, out_blocked=False, out_scale=None):
    if mode == "nn":
        m_full, k_full = a.shape
        m_dim, k_dim = (m_full, a_cols[1]) if a_cols else (m_full, k_full)
    else:
        k_dim, m_full = a.shape
        m_dim = a_cols[1] if a_cols else m_full
    a_off = a_cols[0] if a_cols else 0
    if b_blocked:
        n_dim = b.shape[0] * b.shape[2]
        assert b.shape[2] == tn
    else:
        n_dim = b_cols[1] if b_cols else b.shape[1]
    b_off = b_cols[0] if b_cols else 0
    assert m_dim % tm == 0 and n_dim % tn == 0 and k_dim % tk == 0, (name, m_dim, n_dim, k_dim)
    nk = k_dim // tk

    def body(*refs):
        if affine is not None:
            a_ref, g_ref, s_ref, b_ref, o_ref, acc = refs
        else:
            a_ref, b_ref, o_ref, acc = refs
        k = pl.program_id(2)

        @pl.when(k == 0)
        def _():
            acc[...] = jnp.zeros_like(acc)

        av = a_ref[...]
        if affine is not None:
            av = av * g_ref[...] + s_ref[...]
        av = av.astype(bf16)
        bv = b_ref[...].astype(bf16)
        if mode == "nn":
            acc[...] += jnp.dot(av, bv, preferred_element_type=f32)
        else:
            acc[...] += lax.dot_general(av, bv, _TN, preferred_element_type=f32)

        @pl.when(k == nk - 1)
        def _():
            res = acc[...] if out_scale is None else acc[...] * out_scale
            o_ref[...] = res.astype(out_dtype)

    if mode == "nn":
        a_spec = pl.BlockSpec((tm, tk), lambda i, j, k: (i, k + a_off))
        aff_spec = pl.BlockSpec((1, tk), lambda i, j, k: (0, k + a_off))
    else:
        a_spec = pl.BlockSpec((tk, tm), lambda i, j, k: (k, i + a_off))
        aff_spec = pl.BlockSpec((1, tm), lambda i, j, k: (0, i + a_off))
    if b_blocked:
        b_spec = pl.BlockSpec((None, tk, tn), lambda i, j, k: (j, k, 0))
    else:
        b_spec = pl.BlockSpec((tk, tn), lambda i, j, k: (k, j + b_off))
    if out_blocked:
        o_spec = pl.BlockSpec((None, tm, tn), lambda i, j, k: (j, i, 0))
        o_shape = jax.ShapeDtypeStruct((n_dim // tn, m_dim, tn), out_dtype)
    else:
        o_spec = pl.BlockSpec((tm, tn), lambda i, j, k: (i, j))
        o_shape = jax.ShapeDtypeStruct((m_dim, n_dim), out_dtype)
    in_specs = [a_spec] + ([aff_spec, aff_spec] if affine is not None else []) + [b_spec]
    args = [a] + (list(affine) if affine is not None else []) + [b]
    return pl.pallas_call(
        body, name=name, grid=(m_dim // tm, n_dim // tn, nk), in_specs=in_specs, out_specs=o_spec,
        out_shape=o_shape, scratch_shapes=[pltpu.VMEM((tm, tn), f32)],
        compiler_params=_params(("arbitrary", "arbitrary", "arbitrary")),
    )(*args)


def _mm_tn(a, b, *, out_dtype, tm, mb, tn, nb, tk, name, affine=None, out_blocked=False, out_scale=None,
           pair=False, host=None):
    k_dim, m_dim = a.shape
    multi_b = isinstance(b, (list, tuple))
    b_list = list(b) if multi_b else [b]
    n_dim = nb * tn if multi_b else b.shape[1]
    assert m_dim % (mb * tm) == 0 and n_dim % (nb * tn) == 0 and k_dim % tk == 0, (name, m_dim, n_dim, k_dim)
    nk = k_dim // tk
    grid = (m_dim // (mb * tm), n_dim // (nb * tn), nk)
    if pair:
        assert mb * nb == 4 and grid[0] * grid[1] == 2 and out_dtype == bf16, name

    def body(*refs):
        if pair:
            refs, (acc, send_buf, recv_buf, send_sems, recv_sems) = refs[:-5], refs[-5:]
        else:
            refs, acc = refs[:-1], refs[-1]
        a_ref, o_ref = refs[0], refs[-1]
        if affine is not None:
            g_ref, s_ref = refs[1:3]
        b_refs = refs[3 if affine is not None else 1:-1]
        k = pl.program_id(2)

        @pl.when(k == 0)
        def _():
            acc[...] = jnp.zeros_like(acc)

        av = a_ref[...]
        if affine is not None:
            av = av * g_ref[...] + s_ref[...]
        av = av.astype(bf16)
        if multi_b:
            pieces = [r[...].astype(bf16) for r in b_refs]
        else:
            bv = b_refs[0][...].astype(bf16)
            pieces = [bv[:, jn * tn:(jn + 1) * tn] for jn in range(nb)]
        for im in range(mb):
            a_t = av[:, im * tm:(im + 1) * tm].T
            for jn in range(nb):
                acc[im * nb + jn] += jnp.dot(a_t, pieces[jn], preferred_element_type=f32)

        def scaled(v):
            return v if out_scale is None else v * out_scale

        @pl.when(k == nk - 1)
        def _():
            if pair:
                x, y, c = lax.axis_index("x"), lax.axis_index("y"), lax.axis_index("c")
                window = pl.program_id(0) + pl.program_id(1)
                swaps = []
                for cc in range(2):
                    send_buf[cc] = scaled(acc[2 * cc + 1 - c]).astype(bf16)
                    swaps.append(pltpu.make_async_remote_copy(
                        src_ref=send_buf.at[cc], dst_ref=recv_buf.at[window, cc],
                        send_sem=send_sems.at[2 * window + cc], recv_sem=recv_sems.at[2 * window + cc],
                        device_id=(x, y, 1 - c), device_id_type=MESH_T))
                    swaps[cc].start()
                for cc in range(2):
                    swaps[cc].wait_recv()
                    o_ref[cc] = (scaled(acc[2 * cc + c]) + recv_buf[window, cc].astype(f32)).astype(bf16)
                for cc in range(2):
                    swaps[cc].wait_send()
                return
            for im in range(mb):
                for jn in range(nb):
                    res = scaled(acc[im * nb + jn])
                    if out_blocked:
                        o_ref[jn, im * tm:(im + 1) * tm, :] = res.astype(out_dtype)
                    else:
                        o_ref[im * tm:(im + 1) * tm, jn * tn:(jn + 1) * tn] = res.astype(out_dtype)

    a_spec = pl.BlockSpec((tk, mb * tm), lambda i, j, k: (k, i))
    aff_spec = pl.BlockSpec((1, mb * tm), lambda i, j, k: (0, i))
    if multi_b:
        b_specs = [pl.BlockSpec((tk, tn), lambda i, j, k: (k, 0))] * nb
    else:
        b_specs = [pl.BlockSpec((tk, nb * tn), lambda i, j, k: (k, j))]
    scratch = [pltpu.VMEM((mb * nb, tm, tn), f32)]
    if pair:
        o_spec = pl.BlockSpec((2, tm, tn), lambda i, j, k: (i + j, 0, 0))
        o_shape = jax.ShapeDtypeStruct((4, tm, tn), out_dtype)
        scratch += [pltpu.VMEM((2, tm, tn), bf16), pltpu.VMEM((2, 2, tm, tn), bf16),
                    pltpu.SemaphoreType.DMA((4,)), pltpu.SemaphoreType.DMA((4,))]
    elif out_blocked:
        o_spec = pl.BlockSpec((nb, mb * tm, tn), lambda i, j, k: (j, i, 0))
        o_shape = jax.ShapeDtypeStruct((n_dim // tn, m_dim, tn), out_dtype)
    else:
        o_spec = pl.BlockSpec((mb * tm, nb * tn), lambda i, j, k: (i, j))
        o_shape = jax.ShapeDtypeStruct((m_dim, n_dim), out_dtype)
    in_specs = [a_spec] + ([aff_spec, aff_spec] if affine is not None else []) + b_specs
    args = [a] + (list(affine) if affine is not None else []) + b_list
    res = _hosted_call(
        host, body, name=name, grid=grid, in_specs=in_specs, out_specs=o_spec, out_shape=o_shape,
        scratch_shapes=scratch, compiler_params=_params(("arbitrary", "arbitrary", "arbitrary")),
    )(*args)
    return res[0] if host is None else res


def _in_proj(xhat, g, b, w_in, *, tm, name):
    t = xhat.shape[0]
    n_qkv, n_l = 3 * FOX_W, 2 * LRU_W

    def body(x_ref, g_ref, b_ref, w_ref, qkv_ref, zl_ref, zfg_ref):
        xb = (x_ref[...] * g_ref[...] + b_ref[...]).astype(bf16)
        qkv_ref[...] = jnp.dot(xb, w_ref[:, :n_qkv], preferred_element_type=f32).astype(bf16)
        zl_ref[...] = jnp.dot(xb, w_ref[:, n_qkv:n_qkv + n_l], preferred_element_type=f32)
        zfg_ref[...] = jnp.dot(xb, w_ref[:, n_qkv + n_l:], preferred_element_type=f32)

    row = pl.BlockSpec((1, D_MODEL), lambda i: (0, 0))
    return pl.pallas_call(
        body, name=name, grid=(t // tm,),
        in_specs=[pl.BlockSpec((tm, D_MODEL), lambda i: (i, 0)), row, row,
                  pl.BlockSpec(w_in.shape, lambda i: (0, 0))],
        out_specs=[pl.BlockSpec((tm, n_qkv), lambda i: (i, 0)), pl.BlockSpec((tm, n_l), lambda i: (i, 0)),
                   pl.BlockSpec((tm, LANES), lambda i: (i, 0))],
        out_shape=[jax.ShapeDtypeStruct((t, n_qkv), bf16), jax.ShapeDtypeStruct((t, n_l), f32),
                   jax.ShapeDtypeStruct((t, LANES), f32)],
        compiler_params=_params(("arbitrary",)),
    )(xhat, g, b, w_in)


def _mmln(pairs, *, tm, name, resid=None, resid_scale=1.0, epi=None, ln=None, n_out=D_MODEL):
    t = pairs[0][0].shape[0]
    n_pairs = len(pairs)
    n_resid = 0 if resid is None else len(resid) - 1

    def body(*refs):
        pos = 0
        val = None
        for p in range(n_pairs):
            a_ref, b_ref = refs[pos], refs[pos + 1]
            pos += 2
            av = a_ref[...].astype(bf16)
            bv = b_ref[...].astype(bf16)
            if pairs[p][6] == "nn":
                term = jnp.dot(av, bv, preferred_element_type=f32)
            else:
                term = lax.dot_general(av, bv, _NT, preferred_element_type=f32)
            val = term if val is None else val + term
        if resid is not None:
            if resid[0] == "plain":
                r = refs[pos][...]
            else:
                r = refs[pos][...] * refs[pos + 1][...] + refs[pos + 2][...]
            pos += n_resid
            val = val + resid_scale * r
        if epi is None:
            o_ref = refs[pos]
            o_ref[...] = val.astype(o_ref.dtype)
        elif epi == "ln_fwd":
            xo, rstd = _ln_fwd_tile(val)
            refs[pos][...] = xo
            refs[pos + 1][...] = rstd
        else:
            xh_ref, rs_ref, g_ref, dx_ref, gg_ref, gb_ref = refs[pos:pos + 6]
            dprev, gg, gb = _ln_bwd_tile(val, xh_ref[...], rs_ref[...], g_ref[...])
            dx_ref[...] = dprev
            i = pl.program_id(0)

            @pl.when(i == 0)
            def _():
                gg_ref[...] = gg
                gb_ref[...] = gb

            @pl.when(i > 0)
            def _():
                gg_ref[...] += gg
                gb_ref[...] += gb

    in_specs, args = [], []
    for (a, acb, aw, b, bcb, bw, mode) in pairs:
        in_specs.append(pl.BlockSpec((tm, aw), lambda i, acb=acb: (i, acb)))
        args.append(a)
        if mode == "nn":
            in_specs.append(pl.BlockSpec((aw, n_out), lambda i, bcb=bcb: (bcb, 0)))
        else:
            in_specs.append(pl.BlockSpec((n_out, bw), lambda i, bcb=bcb: (0, bcb)))
        args.append(b)
    tok = pl.BlockSpec((tm, n_out), lambda i: (i, 0))
    row = pl.BlockSpec((1, n_out), lambda i: (0, 0))
    col = pl.BlockSpec((tm, 1), lambda i: (i, 0))
    if resid is not None:
        in_specs += [tok] if resid[0] == "plain" else [tok, row, row]
        args += list(resid[1:])
    if epi is None:
        out_specs, out_shape = tok, jax.ShapeDtypeStruct((t, n_out), f32)
    elif epi == "ln_fwd":
        out_specs = [tok, col]
        out_shape = [jax.ShapeDtypeStruct((t, n_out), f32), jax.ShapeDtypeStruct((t, 1), f32)]
    else:
        in_specs += [tok, col, row]
        args += list(ln)
        out_specs = [tok, row, row]
        out_shape = [jax.ShapeDtypeStruct((t, n_out), f32)] + [jax.ShapeDtypeStruct((1, n_out), f32)] * 2
    return pl.pallas_call(
        body, name=name, grid=(t // tm,), in_specs=in_specs, out_specs=out_specs, out_shape=out_shape,
        compiler_params=_params(("arbitrary",)),
    )(*args)


def _loss_bwd(xhat, rstd, g, b, target, *, tm, name):
    t = xhat.shape[0]

    def body(xh_ref, rs_ref, g_ref, b_ref, tg_ref, dx_ref, sq_ref, gg_ref, gb_ref):
        i = pl.program_id(0)
        xh = xh_ref[...]
        diff = xh * g_ref[...] + b_ref[...] - tg_ref[...]
        sq = jnp.sum(diff * diff, axis=0, keepdims=True)
        dprev, gg, gb = _ln_bwd_tile(diff * (1.0 / D_MODEL), xh, rs_ref[...], g_ref[...])
        dx_ref[...] = dprev

        @pl.when(i == 0)
        def _():
            sq_ref[...] = sq
            gg_ref[...] = gg
            gb_ref[...] = gb

        @pl.when(i > 0)
        def _():
            sq_ref[...] += sq
            gg_ref[...] += gg
            gb_ref[...] += gb

    tok = pl.BlockSpec((tm, D_MODEL), lambda i: (i, 0))
    row = pl.BlockSpec((1, D_MODEL), lambda i: (0, 0))
    return pl.pallas_call(
        body, name=name, grid=(t // tm,),
        in_specs=[tok, pl.BlockSpec((tm, 1), lambda i: (i, 0)), row, row, tok],
        out_specs=[tok, row, row, row],
        out_shape=[jax.ShapeDtypeStruct((t, D_MODEL), f32)] + [jax.ShapeDtypeStruct((1, D_MODEL), f32)] * 3,
        compiler_params=_params(("arbitrary",)),
    )(xhat, rstd, g, b, target)


CUM_TILE = 256


def _tri(n, lower):
    r = lax.broadcasted_iota(jnp.int32, (n, n), 0)
    c = lax.broadcasted_iota(jnp.int32, (n, n), 1)
    return jnp.where((r >= c) if lower else (r <= c), 1.0, 0.0).astype(f32)


def _cum_fwd(zfg, bfg, *, name):
    t = zfg.shape[0]

    def body(z_ref, b_ref, o_ref, carry):
        @pl.when(pl.program_id(0) == 0)
        def _():
            carry[...] = jnp.zeros_like(carry)

        ls = -_softplus(-(z_ref[...] + b_ref[...]))
        c = jnp.dot(_tri(CUM_TILE, True), ls, preferred_element_type=f32,
                    precision=lax.Precision.HIGHEST) + carry[...]
        o_ref[...] = c
        carry[...] = c[CUM_TILE - 1:CUM_TILE, :]

    blk = pl.BlockSpec((CUM_TILE, LANES), lambda i: (i, 0))
    return pl.pallas_call(
        body, name=name, grid=(t // CUM_TILE,),
        in_specs=[blk, pl.BlockSpec((1, LANES), lambda i: (0, 0))], out_specs=blk,
        out_shape=jax.ShapeDtypeStruct((t, LANES), f32), scratch_shapes=[pltpu.VMEM((1, LANES), f32)],
        compiler_params=_params(("arbitrary",)),
    )(zfg, bfg)


def _cum_bwd(dcum_q, dcum_k, zfg, bfg, *, name):
    t = zfg.shape[0]
    n = t // CUM_TILE

    def body(d_ref, d2_ref, z_ref, b_ref, o_ref, s_ref, carry):
        i = pl.program_id(0)

        @pl.when(i == 0)
        def _():
            carry[...] = jnp.zeros_like(carry)

        dls = jnp.dot(_tri(CUM_TILE, False), d_ref[...] + d2_ref[...], preferred_element_type=f32,
                      precision=lax.Precision.HIGHEST) + carry[...]
        carry[...] = dls[0:1, :]
        lane = lax.broadcasted_iota(jnp.int32, (CUM_TILE, LANES), 1)
        dfg = jnp.where(lane < HEADS, dls * _sigmoid(-(z_ref[...] + b_ref[...])), 0.0)
        o_ref[...] = dfg
        tot = jnp.sum(dfg, axis=0, keepdims=True)

        @pl.when(i == 0)
        def _():
            s_ref[...] = tot

        @pl.when(i > 0)
        def _():
            s_ref[...] += tot

    blk = pl.BlockSpec((CUM_TILE, LANES), lambda i: (n - 1 - i, 0))
    row = pl.BlockSpec((1, LANES), lambda i: (0, 0))
    return pl.pallas_call(
        body, name=name, grid=(n,), in_specs=[blk, blk, blk, row], out_specs=[blk, row],
        out_shape=[jax.ShapeDtypeStruct((t, LANES), f32), jax.ShapeDtypeStruct((1, LANES), f32)],
        scratch_shapes=[pltpu.VMEM((1, LANES), f32)],
        compiler_params=_params(("arbitrary",)),
    )(dcum_q, dcum_k, zfg, bfg)


ATT_TILE = 512


ATT_ROWS = 32


def _causal_rows(r, transposed):
    rr = lax.broadcasted_iota(jnp.int32, (ATT_ROWS, ATT_TILE), 0) + r * ATT_ROWS
    cc = lax.broadcasted_iota(jnp.int32, (ATT_ROWS, ATT_TILE), 1)
    return (cc >= rr) if transposed else (rr >= cc)


def _causal(i, j, transposed):
    r = lax.broadcasted_iota(jnp.int32, (ATT_TILE, ATT_TILE), 0)
    c = lax.broadcasted_iota(jnp.int32, (ATT_TILE, ATT_TILE), 1)
    if transposed:
        return (c + i * ATT_TILE) >= (r + j * ATT_TILE)
    return (r + i * ATT_TILE) >= (c + j * ATT_TILE)


def _attn_fwd(qkv, cum, cum_t, *, name, host=None):
    t = qkv.shape[0]
    n = t // ATT_TILE
    tq = ATT_TILE

    def body(q_ref, k_ref, v_ref, cq_ref, ck_ref, o_ref, lse_ref, acc, m_s, l_s, c_s, s_s, p_s):
        i = pl.program_id(0)
        j = pl.program_id(1)

        @pl.when(j == 0)
        def _():
            acc[...] = jnp.zeros_like(acc)
            m_s[...] = jnp.full_like(m_s, NEG_BIG)
            l_s[...] = jnp.zeros_like(l_s)

        def block(masked):
            for h in range(HEADS):
                hs = slice(HEAD_D * h, HEAD_D * (h + 1))
                s_s[...] = lax.dot_general(q_ref[:, hs] * ATT_SCALE, k_ref[:, hs], _NT, preferred_element_type=f32)
                ck = ck_ref[h:h + 1, :]

                def rows_chunk(r, carry):
                    rows = pl.ds(pl.multiple_of(r * ATT_ROWS, ATT_ROWS), ATT_ROWS)
                    s = s_s[rows, :] + (cq_ref[rows, h:h + 1] - ck)
                    if masked:
                        s = jnp.where(_causal_rows(r, False), s, NEG_BIG)
                    m_old = m_s[rows, h:h + 1]
                    m_new = jnp.maximum(m_old, jnp.max(s, axis=-1, keepdims=True))
                    corr = jnp.exp(m_old - m_new)
                    p = jnp.exp(s - m_new)
                    l_s[rows, h:h + 1] = corr * l_s[rows, h:h + 1] + jnp.sum(p, axis=-1, keepdims=True)
                    m_s[rows, h:h + 1] = m_new
                    c_s[rows, h:h + 1] = corr
                    p_s[rows, :] = p.astype(bf16)
                    return carry

                lax.fori_loop(0, tq // ATT_ROWS, rows_chunk, 0, unroll=4)
                acc[:, hs] = c_s[:, h:h + 1] * acc[:, hs] + jnp.dot(p_s[...], v_ref[:, hs],
                                                                   preferred_element_type=f32)

        @pl.when(j < i)
        def _():
            block(False)

        @pl.when(j == i)
        def _():
            block(True)
            lse_ref[...] = jnp.zeros_like(lse_ref)
            for h in range(HEADS):
                hs = slice(HEAD_D * h, HEAD_D * (h + 1))
                l = l_s[:, h:h + 1]
                o_ref[:, hs] = acc[:, hs] / l
                lse_ref[:, h:h + 1] = m_s[:, h:h + 1] + jnp.log(l)

    return _hosted_call(
        host, body, name=name, grid=(n, n),
        in_specs=[pl.BlockSpec((tq, FOX_W), lambda i, j: (i, 0)),
                  pl.BlockSpec((tq, FOX_W), lambda i, j: (jnp.minimum(i, j), 1)),
                  pl.BlockSpec((tq, FOX_W), lambda i, j: (jnp.minimum(i, j), 2)),
                  pl.BlockSpec((tq, LANES), lambda i, j: (i, 0)),
                  pl.BlockSpec((HEADS, tq), lambda i, j: (0, jnp.minimum(i, j)))],
        out_specs=[pl.BlockSpec((tq, FOX_W), lambda i, j: (i, 0)), pl.BlockSpec((tq, LANES), lambda i, j: (i, 0))],
        out_shape=[jax.ShapeDtypeStruct((t, FOX_W), f32), jax.ShapeDtypeStruct((t, LANES), f32)],
        scratch_shapes=[pltpu.VMEM((tq, FOX_W), f32), pltpu.VMEM((tq, LANES), f32), pltpu.VMEM((tq, LANES), f32),
                        pltpu.VMEM((tq, LANES), f32), pltpu.VMEM((tq, tq), f32), pltpu.VMEM((tq, tq), bf16)],
        compiler_params=_params(("arbitrary", "arbitrary")),
    )(qkv, qkv, qkv, cum, cum_t)


def _attn_delta(dmix, o, *, tm, name):
    t = o.shape[0]

    def body(do_ref, o_ref, d_ref):
        r = lax.broadcasted_iota(jnp.int32, (FOX_W, LANES), 0)
        c = lax.broadcasted_iota(jnp.int32, (FOX_W, LANES), 1)
        pick = jnp.where(r // HEAD_D == c, 1.0, 0.0).astype(f32)
        d_ref[...] = jnp.dot(do_ref[...] * o_ref[...], pick, preferred_element_type=f32,
                             precision=lax.Precision.HIGHEST)

    blk = pl.BlockSpec((tm, FOX_W), lambda i: (i, 0))
    return pl.pallas_call(
        body, name=name, grid=(t // tm,), in_specs=[blk, blk],
        out_specs=pl.BlockSpec((tm, LANES), lambda i: (i, 0)),
        out_shape=jax.ShapeDtypeStruct((t, LANES), f32), compiler_params=_params(("arbitrary",)),
    )(dmix, o)


def _attn_dq(qkv, dmix, cum, cum_t, lse, delta, *, name, host=None):
    t = qkv.shape[0]
    n = t // ATT_TILE
    tq = ATT_TILE

    def body(q_ref, k_ref, v_ref, do_ref, cq_ref, ck_ref, lse_ref, dl_ref, dq_ref, dc_ref, acc, dc_acc):
        i = pl.program_id(0)
        j = pl.program_id(1)

        @pl.when(j == 0)
        def _():
            acc[...] = jnp.zeros_like(acc)
            dc_acc[...] = jnp.zeros_like(dc_acc)

        def block(masked):
            mask = _causal(i, j, False) if masked else None
            for h in range(HEADS):
                hs = slice(HEAD_D * h, HEAD_D * (h + 1))
                kh = k_ref[:, hs]
                s = lax.dot_general(q_ref[:, hs] * ATT_SCALE, kh, _NT, preferred_element_type=f32)
                s = s + cq_ref[:, h:h + 1] - ck_ref[h:h + 1, :]
                if masked:
                    s = jnp.where(mask, s, NEG_BIG)
                p = jnp.exp(s - lse_ref[:, h:h + 1])
                dp = lax.dot_general(do_ref[:, hs].astype(bf16), v_ref[:, hs], _NT, preferred_element_type=f32)
                ds = p * (dp - dl_ref[:, h:h + 1])
                acc[:, hs] += jnp.dot(ds.astype(bf16), kh, preferred_element_type=f32)
                dc_acc[:, h:h + 1] += jnp.sum(ds, axis=-1, keepdims=True)

        @pl.when(j < i)
        def _():
            block(False)

        @pl.when(j == i)
        def _():
            block(True)
            dq_ref[...] = (acc[...] * ATT_SCALE).astype(bf16)
            dc_ref[...] = dc_acc[...]

    col = pl.BlockSpec((tq, LANES), lambda i, j: (i, 0))
    return _hosted_call(
        host, body, name=name, grid=(n, n),
        in_specs=[pl.BlockSpec((tq, FOX_W), lambda i, j: (i, 0)),
                  pl.BlockSpec((tq, FOX_W), lambda i, j: (jnp.minimum(i, j), 1)),
                  pl.BlockSpec((tq, FOX_W), lambda i, j: (jnp.minimum(i, j), 2)),
                  pl.BlockSpec((tq, FOX_W), lambda i, j: (i, 0)),
                  col, pl.BlockSpec((HEADS, tq), lambda i, j: (0, jnp.minimum(i, j))), col, col],
        out_specs=[pl.BlockSpec((tq, FOX_W), lambda i, j: (i, 0)), col],
        out_shape=[jax.ShapeDtypeStruct((t, FOX_W), bf16), jax.ShapeDtypeStruct((t, LANES), f32)],
        scratch_shapes=[pltpu.VMEM((tq, FOX_W), f32), pltpu.VMEM((tq, LANES), f32)],
        compiler_params=_params(("arbitrary", "arbitrary")),
    )(qkv, qkv, qkv, dmix, cum, cum_t, lse, delta)


def _attn_dkv(qkv, dmix, cum, cum_t, lse_t, delta_t, *, name):
    t = qkv.shape[0]
    n = t // ATT_TILE
    tk = ATT_TILE

    def body(q_ref, k_ref, v_ref, do_ref, cq_ref, ck_ref, lse_ref, dl_ref, dk_ref, dv_ref, dc_ref, dk_acc, dv_acc, dc_acc):
        j = pl.program_id(0)
        i = pl.program_id(1)

        @pl.when(i == 0)
        def _():
            dk_acc[...] = jnp.zeros_like(dk_acc)
            dv_acc[...] = jnp.zeros_like(dv_acc)
            dc_acc[...] = jnp.zeros_like(dc_acc)

        def block(masked):
            mask = _causal(i, j, True) if masked else None
            for h in range(HEADS):
                hs = slice(HEAD_D * h, HEAD_D * (h + 1))
                qh = q_ref[:, hs]
                doh = do_ref[:, hs].astype(bf16)
                s_t = lax.dot_general(k_ref[:, hs] * ATT_SCALE, qh, _NT, preferred_element_type=f32)
                s_t = s_t + cq_ref[h:h + 1, :] - ck_ref[:, h:h + 1]
                if masked:
                    s_t = jnp.where(mask, s_t, NEG_BIG)
                p_t = jnp.exp(s_t - lse_ref[h:h + 1, :])
                dv_acc[:, hs] += jnp.dot(p_t.astype(bf16), doh, preferred_element_type=f32)
                dp_t = lax.dot_general(v_ref[:, hs], doh, _NT, preferred_element_type=f32)
                ds_t = p_t * (dp_t - dl_ref[h:h + 1, :])
                dk_acc[:, hs] += jnp.dot(ds_t.astype(bf16), qh, preferred_element_type=f32)
                dc_acc[:, h:h + 1] -= jnp.sum(ds_t, axis=-1, keepdims=True)

        @pl.when(i > j)
        def _():
            block(False)

        @pl.when(i == j)
        def _():
            block(True)

        @pl.when(i == n - 1)
        def _():
            dk_ref[...] = (dk_acc[...] * ATT_SCALE).astype(bf16)
            dv_ref[...] = dv_acc[...].astype(bf16)
            dc_ref[...] = dc_acc[...]

    rowq = pl.BlockSpec((HEADS, tk), lambda j, i: (0, jnp.maximum(i, j)))
    return pl.pallas_call(
        body, name=name, grid=(n, n),
        in_specs=[pl.BlockSpec((tk, FOX_W), lambda j, i: (jnp.maximum(i, j), 0)),
                  pl.BlockSpec((tk, FOX_W), lambda j, i: (j, 1)),
                  pl.BlockSpec((tk, FOX_W), lambda j, i: (j, 2)),
                  pl.BlockSpec((tk, FOX_W), lambda j, i: (jnp.maximum(i, j), 0)),
                  rowq, pl.BlockSpec((tk, LANES), lambda j, i: (j, 0)), rowq, rowq],
        out_specs=[pl.BlockSpec((tk, FOX_W), lambda j, i: (j, 0)), pl.BlockSpec((tk, FOX_W), lambda j, i: (j, 0)),
                   pl.BlockSpec((tk, LANES), lambda j, i: (j, 0))],
        out_shape=[jax.ShapeDtypeStruct((t, FOX_W), bf16), jax.ShapeDtypeStruct((t, FOX_W), bf16),
                   jax.ShapeDtypeStruct((t, LANES), f32)],
        scratch_shapes=[pltpu.VMEM((tk, FOX_W), f32), pltpu.VMEM((tk, FOX_W), f32), pltpu.VMEM((tk, LANES), f32)],
        compiler_params=_params(("arbitrary", "arbitrary")),
    )(qkv, qkv, qkv, dmix, cum_t, cum, lse_t, delta_t)


ATT_W = HEADS * LANES


def _data_lane(h):
    return HEAD_D * (h % 2)


def _extra_lane(h):
    return HEAD_D - _data_lane(h)


def _split3(x):
    hi = x.astype(bf16)
    rest = x - hi.astype(f32)
    mid = rest.astype(bf16)
    lo = (rest - mid.astype(f32)).astype(bf16)
    return hi, mid, lo


def _augment(pair, h, first, second, fill=0.0):
    rows = pair.shape[0]
    lane = lax.broadcasted_iota(jnp.int32, (rows, LANES), 1)
    base = _extra_lane(h)
    own = (lane < HEAD_D) if h % 2 == 0 else (lane >= HEAD_D)
    out = jnp.where(own, pair, jnp.full((rows, LANES), fill, bf16))
    for off, src in ((0, first), (3, second)):
        for q in range(3):
            val = src[q] if isinstance(src, tuple) else jnp.full((rows, 1), src, bf16)
            out = jnp.where(lane == base + off + q, val, out)
    return out


def _attn_prep_fwd(qkv, cum, *, tm, name):
    t = qkv.shape[0]

    def body(q_ref, k_ref, v_ref, c_ref, qa_ref, ka_ref, va_ref):
        for h in range(HEADS):
            pair = slice(LANES * (h // 2), LANES * (h // 2 + 1))
            hs = slice(LANES * h, LANES * (h + 1))
            c3 = _split3(c_ref[:, h:h + 1])
            qa_ref[:, hs] = _augment(q_ref[:, pair] * ATT_SCALE, h, c3, 1.0)
            ka_ref[:, hs] = _augment(k_ref[:, pair], h, 1.0, tuple(-p for p in c3))
            va_ref[:, hs] = _augment(v_ref[:, pair], h, 1.0, 1.0, fill=1.0)

    wide = pl.BlockSpec((tm, ATT_W), lambda i: (i, 0))
    out = jax.ShapeDtypeStruct((t, ATT_W), bf16)
    return pl.pallas_call(
        body, name=name, grid=(t // tm,),
        in_specs=[pl.BlockSpec((tm, FOX_W), lambda i: (i, 0)), pl.BlockSpec((tm, FOX_W), lambda i: (i, 1)),
                  pl.BlockSpec((tm, FOX_W), lambda i: (i, 2)), pl.BlockSpec((tm, LANES), lambda i: (i, 0))],
        out_specs=[wide] * 3, out_shape=[out] * 3, compiler_params=_params(("arbitrary",)),
    )(qkv, qkv, qkv, cum)


def _attn_prep_bwd(qkv, cum, lse, dmix, o, *, tm, name):
    t = qkv.shape[0]

    def body(q_ref, c_ref, l_ref, do_ref, o_ref, qa_ref, da_ref):
        for h in range(HEADS):
            pair = slice(LANES * (h // 2), LANES * (h // 2 + 1))
            src = slice(HEAD_D * h, HEAD_D * (h + 1))
            hs = slice(LANES * h, LANES * (h + 1))
            delta = jnp.sum(do_ref[:, src] * o_ref[:, src], axis=-1, keepdims=True)
            qa_ref[:, hs] = _augment(q_ref[:, pair] * ATT_SCALE, h,
                                     _split3(c_ref[:, h:h + 1] - l_ref[:, h:h + 1]), 1.0)
            da_ref[:, hs] = _augment(do_ref[:, pair].astype(bf16), h, tuple(-p for p in _split3(delta)), 0.0)

    wide = pl.BlockSpec((tm, ATT_W), lambda i: (i, 0))
    half = pl.BlockSpec((tm, FOX_W), lambda i: (i, 0))
    col = pl.BlockSpec((tm, LANES), lambda i: (i, 0))
    out = jax.ShapeDtypeStruct((t, ATT_W), bf16)
    return pl.pallas_call(
        body, name=name, grid=(t // tm,), in_specs=[half, col, col, half, half],
        out_specs=[wide] * 2, out_shape=[out] * 2, compiler_params=_params(("arbitrary",)),
    )(qkv, cum, lse, dmix, o)


def _attn_fwd2(q_aug, k_aug, v_aug, *, name, host=None):
    t = q_aug.shape[0]
    n = t // ATT_TILE
    tq = ATT_TILE

    def body(q_ref, k_ref, v_ref, o_ref, lse_ref, acc, m_s):
        i = pl.program_id(0)
        j = pl.program_id(1)

        @pl.when(j == 0)
        def _():
            acc[...] = jnp.zeros_like(acc)
            m_s[...] = jnp.full_like(m_s, NEG_BIG)

        def block(masked):
            mask = _causal(i, j, False) if masked else None
            for h in range(HEADS):
                hs = slice(LANES * h, LANES * (h + 1))
                s = lax.dot_general(q_ref[:, hs], k_ref[:, hs], _NT, preferred_element_type=f32)
                if masked:
                    s = jnp.where(mask, s, NEG_BIG)
                blocks = [s[:, LANES * b:LANES * (b + 1)] for b in range(tq // LANES)]
                m_old = m_s[h]
                m_new = jnp.maximum(m_old, jnp.broadcast_to(
                    jnp.max(functools.reduce(jnp.maximum, blocks), axis=-1, keepdims=True), (tq, LANES)))
                p = jnp.concatenate([jnp.exp(b - m_new) for b in blocks], axis=1).astype(bf16)
                acc[h] = jnp.exp(m_old - m_new) * acc[h] + jnp.dot(p, v_ref[:, hs], preferred_element_type=f32)
                m_s[h] = m_new

        @pl.when(j < i)
        def _():
            block(False)

        @pl.when(j == i)
        def _():
            block(True)
            lse_ref[...] = jnp.zeros_like(lse_ref)
            for h in range(HEADS):
                a = acc[h]
                l = a[:, _extra_lane(h):_extra_lane(h) + 1]
                o_ref[:, HEAD_D * h:HEAD_D * (h + 1)] = a[:, _data_lane(h):_data_lane(h) + HEAD_D] / l
                lse_ref[:, h:h + 1] = m_s[h][:, 0:1] + jnp.log(l)

    kv = pl.BlockSpec((tq, ATT_W), lambda i, j: (jnp.minimum(i, j), 0))
    return _hosted_call(
        host, body, name=name, grid=(n, n),
        in_specs=[pl.BlockSpec((tq, ATT_W), lambda i, j: (i, 0)), kv, kv],
        out_specs=[pl.BlockSpec((tq, FOX_W), lambda i, j: (i, 0)), pl.BlockSpec((tq, LANES), lambda i, j: (i, 0))],
        out_shape=[jax.ShapeDtypeStruct((t, FOX_W), f32), jax.ShapeDtypeStruct((t, LANES), f32)],
        scratch_shapes=[pltpu.VMEM((HEADS, tq, LANES), f32), pltpu.VMEM((HEADS, tq, LANES), f32)],
        compiler_params=_params(("arbitrary", "arbitrary")),
    )(q_aug, k_aug, v_aug)


def _attn_bwd(qb_aug, k_aug, v_aug, do_aug, *, name, host=None):
    t = qb_aug.shape[0]
    n = t // ATT_TILE
    tk = ATT_TILE

    def body(q_ref, k_ref, v_ref, do_ref, dq_ref, dcq_ref, dk_ref, dv_ref, dck_ref, dk_acc, dv_acc, dq_all):
        j = pl.program_id(0)
        i = pl.program_id(1)

        @pl.when(jnp.logical_and(i == 0, j == 0))
        def _():
            dq_all[...] = jnp.zeros_like(dq_all)

        @pl.when(i == 0)
        def _():
            dk_acc[...] = jnp.zeros_like(dk_acc)
            dv_acc[...] = jnp.zeros_like(dv_acc)

        def block(masked):
            mask = _causal(i, j, True) if masked else None
            for h in range(HEADS):
                hs = slice(LANES * h, LANES * (h + 1))
                qh = q_ref[:, hs]
                doh = do_ref[:, hs]
                kh = k_ref[:, hs]
                s_t = lax.dot_general(kh, qh, _NT, preferred_element_type=f32)
                if masked:
                    s_t = jnp.where(mask, s_t, NEG_BIG)
                p_t = jnp.exp(s_t)
                dv_acc[h] += jnp.dot(p_t.astype(bf16), doh, preferred_element_type=f32)
                dp_t = lax.dot_general(v_ref[:, hs], doh, _NT, preferred_element_type=f32)
                ds_t = (p_t * dp_t).astype(bf16)
                dk_acc[h] += jnp.dot(ds_t, qh, preferred_element_type=f32)
                dq_all[i, h] += lax.dot_general(ds_t, kh, _TN, preferred_element_type=f32)

        @pl.when(i > j)
        def _():
            block(False)

        @pl.when(i == j)
        def _():
            block(True)
            dcq_ref[...] = jnp.zeros_like(dcq_ref)
            for h in range(HEADS):
                a = dq_all[j, h]
                dq_ref[:, HEAD_D * h:HEAD_D * (h + 1)] = (
                    a[:, _data_lane(h):_data_lane(h) + HEAD_D] * ATT_SCALE).astype(bf16)
                dcq_ref[:, h:h + 1] = a[:, _extra_lane(h):_extra_lane(h) + 1]

        @pl.when(i == n - 1)
        def _():
            dck_ref[...] = jnp.zeros_like(dck_ref)
            for h in range(HEADS):
                a = dk_acc[h]
                cols = slice(_data_lane(h), _data_lane(h) + HEAD_D)
                dk_ref[:, HEAD_D * h:HEAD_D * (h + 1)] = a[:, cols].astype(bf16)
                dv_ref[:, HEAD_D * h:HEAD_D * (h + 1)] = dv_acc[h][:, cols].astype(bf16)
                dck_ref[:, h:h + 1] = -a[:, _extra_lane(h) + 3:_extra_lane(h) + 4]

    own = pl.BlockSpec((tk, ATT_W), lambda j, i: (j, 0))
    qs = pl.BlockSpec((tk, ATT_W), lambda j, i: (jnp.maximum(i, j), 0))
    half = pl.BlockSpec((tk, FOX_W), lambda j, i: (j, 0))
    col = pl.BlockSpec((tk, LANES), lambda j, i: (j, 0))
    return _hosted_call(
        host, body, name=name, grid=(n, n), in_specs=[qs, own, own, qs],
        out_specs=[half, col, half, half, col],
        out_shape=[jax.ShapeDtypeStruct((t, FOX_W), bf16), jax.ShapeDtypeStruct((t, LANES), f32),
                   jax.ShapeDtypeStruct((t, FOX_W), bf16), jax.ShapeDtypeStruct((t, FOX_W), bf16),
                   jax.ShapeDtypeStruct((t, LANES), f32)],
        scratch_shapes=[pltpu.VMEM((HEADS, tk, LANES), f32), pltpu.VMEM((HEADS, tk, LANES), f32),
                        pltpu.VMEM((n, HEADS, tk, LANES), f32)],
        compiler_params=_params(("arbitrary", "arbitrary")),
    )(qb_aug, k_aug, v_aug, do_aug)


def _attn_dq2(qb_aug, k_aug, v_aug, do_aug, *, name, host=None):
    t = qb_aug.shape[0]
    n = t // ATT_TILE
    tq = ATT_TILE

    def body(q_ref, k_ref, v_ref, do_ref, dq_ref, dc_ref, acc):
        i = pl.program_id(0)
        j = pl.program_id(1)

        @pl.when(j == 0)
        def _():
            acc[...] = jnp.zeros_like(acc)

        def block(masked):
            mask = _causal(i, j, False) if masked else None
            for h in range(HEADS):
                hs = slice(LANES * h, LANES * (h + 1))
                kh = k_ref[:, hs]
                s = lax.dot_general(q_ref[:, hs], kh, _NT, preferred_element_type=f32)
                if masked:
                    s = jnp.where(mask, s, NEG_BIG)
                dp = lax.dot_general(do_ref[:, hs], v_ref[:, hs], _NT, preferred_element_type=f32)
                ds = (jnp.exp(s) * dp).astype(bf16)
                acc[h] += jnp.dot(ds, kh, preferred_element_type=f32)

        @pl.when(j < i)
        def _():
            block(False)

        @pl.when(j == i)
        def _():
            block(True)
            dc_ref[...] = jnp.zeros_like(dc_ref)
            for h in range(HEADS):
                a = acc[h]
                dq_ref[:, HEAD_D * h:HEAD_D * (h + 1)] = (
                    a[:, _data_lane(h):_data_lane(h) + HEAD_D] * ATT_SCALE).astype(bf16)
                dc_ref[:, h:h + 1] = a[:, _extra_lane(h):_extra_lane(h) + 1]

    own = pl.BlockSpec((tq, ATT_W), lambda i, j: (i, 0))
    kv = pl.BlockSpec((tq, ATT_W), lambda i, j: (jnp.minimum(i, j), 0))
    return _hosted_call(
        host, body, name=name, grid=(n, n), in_specs=[own, kv, kv, own],
        out_specs=[pl.BlockSpec((tq, FOX_W), lambda i, j: (i, 0)), pl.BlockSpec((tq, LANES), lambda i, j: (i, 0))],
        out_shape=[jax.ShapeDtypeStruct((t, FOX_W), bf16), jax.ShapeDtypeStruct((t, LANES), f32)],
        scratch_shapes=[pltpu.VMEM((HEADS, tq, LANES), f32)],
        compiler_params=_params(("arbitrary", "arbitrary")),
    )(qb_aug, k_aug, v_aug, do_aug)


def _attn_dkv2(qb_aug, k_aug, v_aug, do_aug, *, name, host=None):
    t = qb_aug.shape[0]
    n = t // ATT_TILE
    tk = ATT_TILE

    def body(q_ref, k_ref, v_ref, do_ref, dk_ref, dv_ref, dc_ref, dk_acc, dv_acc):
        j = pl.program_id(0)
        i = pl.program_id(1)

        @pl.when(i == 0)
        def _():
            dk_acc[...] = jnp.zeros_like(dk_acc)
            dv_acc[...] = jnp.zeros_like(dv_acc)

        def block(masked):
            mask = _causal(i, j, True) if masked else None
            for h in range(HEADS):
                hs = slice(LANES * h, LANES * (h + 1))
                qh = q_ref[:, hs]
                doh = do_ref[:, hs]
                s_t = lax.dot_general(k_ref[:, hs], qh, _NT, preferred_element_type=f32)
                if masked:
                    s_t = jnp.where(mask, s_t, NEG_BIG)
                p_t = jnp.exp(s_t)
                dv_acc[h] += jnp.dot(p_t.astype(bf16), doh, preferred_element_type=f32)
                dp_t = lax.dot_general(v_ref[:, hs], doh, _NT, preferred_element_type=f32)
                dk_acc[h] += jnp.dot((p_t * dp_t).astype(bf16), qh, preferred_element_type=f32)

        @pl.when(i > j)
        def _():
            block(False)

        @pl.when(i == j)
        def _():
            block(True)

        @pl.when(i == n - 1)
        def _():
            dc_ref[...] = jnp.zeros_like(dc_ref)
            for h in range(HEADS):
                a = dk_acc[h]
                cols = slice(_data_lane(h), _data_lane(h) + HEAD_D)
                dk_ref[:, HEAD_D * h:HEAD_D * (h + 1)] = a[:, cols].astype(bf16)
                dv_ref[:, HEAD_D * h:HEAD_D * (h + 1)] = dv_acc[h][:, cols].astype(bf16)
                dc_ref[:, h:h + 1] = -a[:, _extra_lane(h) + 3:_extra_lane(h) + 4]

    own = pl.BlockSpec((tk, ATT_W), lambda j, i: (j, 0))
    qs = pl.BlockSpec((tk, ATT_W), lambda j, i: (jnp.maximum(i, j), 0))
    half = pl.BlockSpec((tk, FOX_W), lambda j, i: (j, 0))
    return _hosted_call(
        host, body, name=name, grid=(n, n), in_specs=[qs, own, own, qs],
        out_specs=[half, half, pl.BlockSpec((tk, LANES), lambda j, i: (j, 0))],
        out_shape=[jax.ShapeDtypeStruct((t, FOX_W), bf16), jax.ShapeDtypeStruct((t, FOX_W), bf16),
                   jax.ShapeDtypeStruct((t, LANES), f32)],
        scratch_shapes=[pltpu.VMEM((HEADS, tk, LANES), f32), pltpu.VMEM((HEADS, tk, LANES), f32)],
        compiler_params=_params(("arbitrary", "arbitrary")),
    )(qb_aug, k_aug, v_aug, do_aug)


LRU_CHUNK = 64
LRU_G = 256
SUB = 8


def _row_ids(n):
    return lax.broadcasted_iota(jnp.int32, (n, LRU_G), 0)


def _shift_rows_down(ext, s):
    return pltpu.roll(ext, s, axis=0)[SUB:, :]


def _shift_rows_up(ext, s, n):
    return pltpu.roll(ext, ext.shape[0] - s, axis=0)[:n, :]


def _lru_gates(u, wa_ref, ba_ref, wx_ref, bx_ref, sp):
    ub = u.astype(bf16)
    r = _sigmoid(jnp.dot(ub, wa_ref[...], preferred_element_type=f32) + ba_ref[...])
    gi = _sigmoid(jnp.dot(ub, wx_ref[...], preferred_element_type=f32) + bx_ref[...])
    log_a = -LRU_C * r * sp
    a = jnp.exp(log_a)
    s = jnp.sqrt(_one_minus_exp(2.0 * log_a))
    return r, gi, a, s


def _conv_window(lx_ref, r0, ci):
    cur = lx_ref[pl.ds(r0, LRU_CHUNK), :]
    p0 = pl.multiple_of(jnp.maximum(r0 - SUB, 0), SUB)
    prev = jnp.where(ci > 0, lx_ref[pl.ds(p0, SUB), :], 0.0)
    return cur, jnp.concatenate([prev, cur], axis=0)


def _lru_fwd(zl, conv_w, conv_b, wa, ba, wx, bx, lam, *, name, host=None):
    t = zl.shape[0]
    n_chunk = t // LRU_CHUNK

    def body(lx_ref, lg_ref, cw_ref, cb_ref, wa_ref, ba_ref, wx_ref, bx_ref, lam_ref, u_ref, h_ref, y_ref):
        sp = _softplus(-lam_ref[...])
        rows = _row_ids(SUB)

        def chunk(ci, hc):
            r0 = pl.multiple_of(ci * LRU_CHUNK, LRU_CHUNK)
            cur, ext = _conv_window(lx_ref, r0, ci)
            u = cb_ref[...] + cw_ref[3:4, :] * cur
            for k in range(3):
                u = u + cw_ref[k:k + 1, :] * _shift_rows_down(ext, 3 - k)
            r, gi, a, s = _lru_gates(u, wa_ref, ba_ref, wx_ref, bx_ref, sp)
            b = s * (gi * u)
            tiles = []
            for q in range(LRU_CHUNK // SUB):
                ta = a[SUB * q:SUB * (q + 1), :]
                tb = b[SUB * q:SUB * (q + 1), :]
                for d in (1, 2, 4):
                    a_sh = jnp.where(rows >= d, pltpu.roll(ta, d, axis=0), 1.0)
                    b_sh = jnp.where(rows >= d, pltpu.roll(tb, d, axis=0), 0.0)
                    tb = ta * b_sh + tb
                    ta = ta * a_sh
                hq = tb + ta * hc
                hc = hq[SUB - 1:SUB, :]
                tiles.append(hq)
            h = jnp.concatenate(tiles, axis=0)
            u_ref[pl.ds(r0, LRU_CHUNK), :] = u
            h_ref[pl.ds(r0, LRU_CHUNK), :] = h
            gel, _ = _gelu_and_grad(lg_ref[pl.ds(r0, LRU_CHUNK), :])
            y_ref[pl.ds(r0, LRU_CHUNK), :] = gel * h
            return hc

        lax.fori_loop(0, n_chunk, chunk, jnp.zeros((1, LRU_G), f32))

    seq = lambda cb: pl.BlockSpec((t, LRU_G), lambda c, cb=cb: (0, c + cb))
    rowc = pl.BlockSpec((1, LRU_G), lambda c: (0, c))
    diag = pl.BlockSpec((LRU_G, LRU_G), lambda c: (c, c))
    out = jax.ShapeDtypeStruct((t, LRU_W), f32)
    return _hosted_call(
        host, body, name=name, grid=(LRU_W // LRU_G,),
        in_specs=[seq(0), seq(LRU_W // LRU_G), pl.BlockSpec((4, LRU_G), lambda c: (0, c)),
                  rowc, diag, rowc, diag, rowc, rowc],
        out_specs=[seq(0)] * 3, out_shape=[out] * 3,
        compiler_params=_params(("arbitrary",)),
    )(zl, zl, conv_w, conv_b, wa, ba, wx, bx, lam)


def _lru_bwd(dmix, zl, u_all, h_all, conv_w, wa, ba, wx, bx, lam, *, name, host=None):
    t = zl.shape[0]
    n_chunk = t // LRU_CHUNK

    def body(dy_ref, lx_ref, lg_ref, u_ref, h_ref, cw_ref, wa_ref, ba_ref, wx_ref, bx_ref, lam_ref,
             dlx_ref, dlg_ref, dcw_ref, dcb_ref, dba_ref, dbx_ref, dlam_ref, dwa_ref, dwx_ref, dpr_s, dpx_s):
        lam_v = lam_ref[...]
        sp = _softplus(-lam_v)
        rows = _row_ids(SUB)
        rows_c = _row_ids(LRU_CHUNK)
        zero_row = jnp.zeros((1, LRU_G), f32)

        def chunk(step, carry):
            dh_c, a_next0, du_next, dsp, dba, dbx, dcb, dw0, dw1, dw2, dw3 = carry
            ci = n_chunk - 1 - step
            r0 = pl.multiple_of(ci * LRU_CHUNK, LRU_CHUNK)
            sl = pl.ds(r0, LRU_CHUNK)
            u = u_ref[sl, :]
            r, gi, a, s = _lru_gates(u, wa_ref, ba_ref, wx_ref, bx_ref, sp)
            h = h_ref[sl, :]
            p0 = pl.multiple_of(jnp.maximum(r0 - SUB, 0), SUB)
            h_before = jnp.where(ci > 0, h_ref[pl.ds(p0, SUB), :], 0.0)[SUB - 1:SUB, :]
            h_prev = jnp.where(rows_c == 0, h_before, pltpu.roll(h, 1, axis=0))
            gel, dgel = _gelu_and_grad(lg_ref[sl, :])
            dy = dy_ref[sl, :]
            dlg_ref[sl, :] = (dy * h * dgel).astype(bf16)
            g_in = dy * gel
            a_next = jnp.where(rows_c == LRU_CHUNK - 1, a_next0, pltpu.roll(a, LRU_CHUNK - 1, axis=0))
            tiles = [None] * (LRU_CHUNK // SUB)
            for q in reversed(range(LRU_CHUNK // SUB)):
                ta = a_next[SUB * q:SUB * (q + 1), :]
                tb = g_in[SUB * q:SUB * (q + 1), :]
                for d in (1, 2, 4):
                    a_sh = jnp.where(rows < SUB - d, pltpu.roll(ta, SUB - d, axis=0), 1.0)
                    b_sh = jnp.where(rows < SUB - d, pltpu.roll(tb, SUB - d, axis=0), 0.0)
                    tb = ta * b_sh + tb
                    ta = ta * a_sh
                dhq = tb + ta * dh_c
                dh_c = dhq[0:1, :]
                tiles[q] = dhq
            dh = jnp.concatenate(tiles, axis=0)
            da = dh * h_prev
            ds = dh * gi * u
            dgi = dh * s * u
            du = dh * s * gi
            dlog_a = da * a - ds * (a * a) / s
            dr = dlog_a * (-LRU_C * sp)
            dsp = dsp + jnp.sum(dlog_a * (-LRU_C * r), axis=0, keepdims=True)
            dpr = dr * r * (1.0 - r)
            dpx = dgi * gi * (1.0 - gi)
            dprb = dpr.astype(bf16)
            dpxb = dpx.astype(bf16)
            dpr_s[sl, :] = dprb
            dpx_s[sl, :] = dpxb
            du = du + (lax.dot_general(dprb, wa_ref[...], _NT, preferred_element_type=f32)
                       + lax.dot_general(dpxb, wx_ref[...], _NT, preferred_element_type=f32))
            dba = dba + jnp.sum(dpr, axis=0, keepdims=True)
            dbx = dbx + jnp.sum(dpx, axis=0, keepdims=True)
            dcb = dcb + jnp.sum(du, axis=0, keepdims=True)
            du_ext = jnp.concatenate([du, du_next], axis=0)
            dlx = cw_ref[3:4, :] * du
            for k in range(3):
                dlx = dlx + cw_ref[k:k + 1, :] * _shift_rows_up(du_ext, 3 - k, LRU_CHUNK)
            dlx_ref[sl, :] = dlx.astype(bf16)
            cur, ext = _conv_window(lx_ref, r0, ci)
            dws = [dw0, dw1, dw2, dw3 + jnp.sum(du * cur, axis=0, keepdims=True)]
            for k in range(3):
                dws[k] = dws[k] + jnp.sum(du * _shift_rows_down(ext, 3 - k), axis=0, keepdims=True)
            return (dh_c, a[0:1, :], du[0:SUB, :], dsp, dba, dbx, dcb, dws[0], dws[1], dws[2], dws[3])

        init = (zero_row, zero_row, jnp.zeros((SUB, LRU_G), f32)) + (zero_row,) * 8
        out = lax.fori_loop(0, n_chunk, chunk, init)
        _, _, _, dsp, dba, dbx, dcb, dw0, dw1, dw2, dw3 = out
        dlam_ref[...] = dsp * (-_sigmoid(-lam_v))
        dba_ref[...] = dba
        dbx_ref[...] = dbx
        dcb_ref[...] = dcb
        dcw_ref[...] = jnp.concatenate([dw0, dw1, dw2, dw3], axis=0)
        ub = u_ref[...].astype(bf16)
        dwa_ref[...] = lax.dot_general(ub, dpr_s[...], _TN, preferred_element_type=f32)
        dwx_ref[...] = lax.dot_general(ub, dpx_s[...], _TN, preferred_element_type=f32)

    seq = lambda cb: pl.BlockSpec((t, LRU_G), lambda c, cb=cb: (0, c + cb))
    rowc = pl.BlockSpec((1, LRU_G), lambda c: (0, c))
    diag = pl.BlockSpec((LRU_G, LRU_G), lambda c: (c, c))
    gate_out = pl.BlockSpec((None, LRU_G, LRU_G), lambda c: (c, 0, 0))
    row_shape = jax.ShapeDtypeStruct((1, LRU_W), f32)
    return _hosted_call(
        host, body, name=name, grid=(LRU_W // LRU_G,),
        in_specs=[seq(LRU_W // LRU_G), seq(0), seq(LRU_W // LRU_G), seq(0), seq(0),
                  pl.BlockSpec((4, LRU_G), lambda c: (0, c)),
                  diag, rowc, diag, rowc, rowc],
        out_specs=[seq(0), seq(0), pl.BlockSpec((4, LRU_G), lambda c: (0, c)), rowc, rowc, rowc, rowc,
                   gate_out, gate_out],
        out_shape=[jax.ShapeDtypeStruct((t, LRU_W), bf16)] * 2
        + [jax.ShapeDtypeStruct((4, LRU_W), f32)] + [row_shape] * 4
        + [jax.ShapeDtypeStruct((LRU_W // LRU_G, LRU_G, LRU_G), f32)] * 2,
        scratch_shapes=[pltpu.VMEM((t, LRU_G), bf16), pltpu.VMEM((t, LRU_G), bf16)],
        compiler_params=_params(("arbitrary",)),
    )(dmix, zl, zl, u_all, h_all, conv_w, wa, ba, wx, bx, lam)


def _block_diag(w):
    eye = jnp.eye(HEADS, dtype=w.dtype)
    return jnp.einsum("hij,hk->hikj", w, eye).reshape(LRU_W, LRU_W)


def _diag_blocks(dw):
    per = dw.shape[1] // HEAD_D
    blocks = [dw[:, HEAD_D * b:HEAD_D * (b + 1), HEAD_D * b:HEAD_D * (b + 1)] for b in range(per)]
    return jnp.stack(blocks, axis=1).reshape(HEADS, HEAD_D, HEAD_D)


def _local_step(x, target, sent, small, *, tm=512, tm_ffn=1024):
    t = x.shape[0]
    ones = jnp.ones((1, D_MODEL), f32)
    zeros = jnp.zeros((1, D_MODEL), f32)
    ln1 = (small["ln1_g"], small["ln1_b"])
    ln2 = (small["ln2_g"], small["ln2_b"])
    ln3 = (small["ln3_g"], small["ln3_b"])

    xh1, rs1, hg1, hu1, wg1, wu1, wd1, w_in_g, w_out_g, conv_w_g = _ffn1_fwd_gathering(
        x, (sent["ffn1_w_gate"], sent["ffn1_w_up"], sent["ffn1_w_down"]),
        _Exchange([sent["w_in"], sent["w_out"], sent["conv_w"]], gather=True), tm=tm_ffn, name="ffn1_fwd")
    w_in = jnp.pad(w_in_g.transpose(1, 0, 2).reshape(D_MODEL, IN_COLS), ((0, 0), (0, 21 * LANES - IN_COLS)))
    w_out = w_out_g.reshape(D_MODEL, D_MODEL)
    conv_w = conv_w_g.transpose(1, 0, 2).reshape(4, LRU_W)
    qkv, zl, zfg = _in_proj(xh1, ln1[0], ln1[1], w_in, tm=tm, name="in_proj")
    bfg = jnp.pad(small["b_forget"], ((0, 0), (0, LANES - HEADS)))
    cum = _cum_fwd(zfg, bfg, name="cum_fwd")
    q_aug, k_aug, v_aug = _attn_prep_fwd(qkv, cum, tm=tm, name="attn_prep_fwd")
    o, lse, wg2, wu2 = _attn_fwd2(q_aug, k_aug, v_aug, name="attn_fwd",
                                  host=_Exchange([sent["ffn2_w_gate"], sent["ffn2_w_up"]], gather=True))
    wa_bd = _block_diag(small["rg_wa"]).astype(bf16)
    wx_bd = _block_diag(small["rg_wx"]).astype(bf16)
    ba = small["rg_ba"].reshape(1, LRU_W)
    bx = small["rg_bx"].reshape(1, LRU_W)
    u, h, lru, wd2 = _lru_fwd(zl, conv_w, small["conv_b"], wa_bd, ba, wx_bd, bx, small["lru_lambda"],
                              name="lru_fwd", host=_Exchange([sent["ffn2_w_down"]], gather=True))
    xh2, rs2 = _mmln([(o, 0, FOX_W, w_out, 0, D_MODEL, "nn"), (lru, 0, LRU_W, w_out, 1, D_MODEL, "nn")],
                     tm=tm, name="mix_fwd", resid=("affine", xh1) + ln1, resid_scale=ALPHA, epi="ln_fwd")
    xh3, rs3, hg2, hu2 = _ffn_fwd(xh2, ln2[0], ln2[1], wg2, wu2, wd2, tm=tm_ffn, name="ffn2_fwd")

    dpre3, sq_rows, g_ln3g, g_ln3b = _loss_bwd(xh3, rs3, ln3[0], ln3[1], target, tm=tm, name="loss_bwd")
    dpre2, g_ln2g, g_ln2b, dhg2, dhu2, a2 = _ffn_bwd(dpre3, hg2, hu2, wg2, wu2, wd2,
                                                     (xh2, rs2, ln2[0]), tm=tm_ffn, name="ffn2_bwd")
    wgrad = dict(out_dtype=bf16, tm=D_MODEL, mb=1, tn=FF_TILE, nb=4, tk=512, pair=True)
    wdgrad = dict(out_dtype=bf16, tm=512, mb=4, tn=D_MODEL, nb=1, tk=512, out_scale=0.5, pair=True)
    between_chips = functools.partial(_Exchange, gather=False, chips=True)
    g_wg2 = _mm_tn(xh2, dhg2, name="g_wg2", affine=ln2, **wgrad)
    g_wu2 = _mm_tn(xh2, dhu2, name="g_wu2", affine=ln2, **wgrad)
    g_wd2 = _mm_tn(a2, dpre3, name="g_wd2", **wdgrad)

    dmix = _mmln([(dpre2, 0, D_MODEL, w_out, 0, D_MODEL, "nt")], tm=tm, name="dmix_bwd")
    g_wout_a = _mm(o, dpre2, mode="tn", out_dtype=bf16, tm=512, tn=D_MODEL, tk=512, name="g_wout_fox")
    g_wout_b = _mm(lru, dpre2, mode="tn", out_dtype=bf16, tm=512, tn=D_MODEL, tk=512, name="g_wout_lru")
    g_wout_blocked = jnp.concatenate([g_wout_a, g_wout_b], axis=0).reshape(N_DEV, D_MODEL // N_DEV, D_MODEL)
    dlx, dlg, g_cw, g_cb, g_ba, g_bx, g_lam, g_wa4, g_wx4, p_wg2, p_wout = _lru_bwd(
        dmix, zl, u, h, conv_w, wa_bd, ba, wx_bd, bx, small["lru_lambda"], name="lru_bwd",
        host=_Hosts(between_chips([g_wg2]), _Exchange([g_wout_blocked], gather=False)))
    qb_aug, do_aug = _attn_prep_bwd(qkv, cum, lse, dmix, o, tm=tm, name="attn_prep_bwd")
    dq, dcum_q, dk, dv, dcum_k, p_wu2, p_wd2 = _attn_bwd(qb_aug, k_aug, v_aug, do_aug, name="attn_bwd",
                                                         host=between_chips([g_wu2, g_wd2]))
    dfg, g_bf = _cum_bwd(dcum_q, dcum_k, zfg, bfg, name="cum_bwd")

    dz = [(dq, 0, 512), (dk, 1, 512), (dv, 2, 512), (dlx, 3, 512), (dlg, 4, 512), (dfg, 20, LANES)]
    dpre1, g_ln1g, g_ln1b = _mmln(
        [(arr, 0, w, w_in, cb, w, "nt") for (arr, cb, w) in dz],
        tm=tm, name="dx1_bwd", resid=("plain", dpre2), resid_scale=ALPHA, epi="ln_bwd", ln=(xh1, rs1, ln1[0]))
    g_win_main = _mm_tn(xh1, [arr for arr, _, _ in dz[:5]], out_dtype=bf16, tm=D_MODEL, mb=1, tn=512, nb=5, tk=512,
                        name="g_win", affine=ln1, out_blocked=True)
    g_win = [g_win_main[n] for n in range(5)] + [
        _mm(xh1, dfg, mode="tn", out_dtype=bf16, tm=D_MODEL, tn=LANES, tk=512, name="g_win_fg", affine=ln1)]
    g_win_full = jnp.concatenate([g[:, :w] for g, (_, _, w) in zip(g_win, dz)], axis=1)[:, :IN_COLS]
    g_win_blocked = g_win_full.reshape(D_MODEL, N_DEV, IN_SHARD).transpose(1, 0, 2)
    dhg1, dhu1, a1, p_win = _ffn_bwd_act(dpre1, hg1, hu1, wd1, tm=tm_ffn, name="ffn1_bwd_act",
                                         host=_Exchange([g_win_blocked], gather=False))
    small_g = {
        "ln1_g": g_ln1g, "ln1_b": g_ln1b, "b_forget": g_bf[:, :HEADS], "conv_w": g_cw, "conv_b": g_cb,
        "rg_wa": _diag_blocks(g_wa4), "rg_ba": g_ba.reshape(HEADS, HEAD_D),
        "rg_wx": _diag_blocks(g_wx4), "rg_bx": g_bx.reshape(HEADS, HEAD_D), "lru_lambda": g_lam,
        "ln2_g": g_ln2g, "ln2_b": g_ln2b, "ln3_g": g_ln3g, "ln3_b": g_ln3b,
    }
    small_g["loss"] = (0.5 / D_MODEL) * jnp.sum(sq_rows, keepdims=True)
    pieces = [small_g[n].reshape(-1) for n in PACKED]
    packed = jnp.concatenate(pieces + [jnp.zeros((PACK_ROWS * LANES - sum(p.shape[0] for p in pieces),), f32)])
    g_wg1, all_packed = _mm_tn(x, dhg1, name="g_wg1",
                               host=_Exchange([packed.reshape(PACK_ROWS, LANES)], gather=True), **wgrad)
    g_wu1, p_wg1 = _mm_tn(x, dhu1, name="g_wu1", host=between_chips([g_wg1]), **wgrad)
    g_wd1, p_wu1 = _mm_tn(a1, dpre1, name="g_wd1", host=between_chips([g_wu1]), **wdgrad)
    grad_x, p_wd1 = _ffn_bwd_dx(dpre1, dhg1, dhu1, wg1, wu1, tm=tm_ffn, name="ffn1_bwd_dx",
                                host=between_chips([g_wd1]))
    parts = {
        "ffn1_w_gate": p_wg1, "ffn1_w_up": p_wu1, "ffn1_w_down": p_wd1, "w_in": p_win, "w_out": p_wout,
        "ffn2_w_gate": p_wg2, "ffn2_w_up": p_wu2, "ffn2_w_down": p_wd2,
    }
    return sq_rows, grad_x, parts, all_packed, {n: small_g[n].shape for n in PACKED}


def _adam_math(w, g, m, v):
    m2 = ADAM_B1 * m + (1.0 - ADAM_B1) * g
    v2 = ADAM_B2 * v + (1.0 - ADAM_B2) * (g * g)
    m_hat = m2 / (1.0 - ADAM_B1 ** ADAM_STEP)
    v_hat = v2 / (1.0 - ADAM_B2 ** ADAM_STEP)
    delta = -ADAM_LR * (m_hat / (jnp.sqrt(v_hat) + ADAM_EPS) + ADAM_WD * w)
    return delta, m2, v2


ADAM_TILE_ELEMS = 128 * 1024


def _adamw_big(parts, w, m, v, *, name):
    r, c = w.shape
    n_parts = parts.shape[0]
    tr = max(d for d in range(8, r + 1, 8) if r % d == 0 and d * c <= ADAM_TILE_ELEMS)

    def body(p_ref, w_ref, m_ref, v_ref, g_ref, d_ref, m2_ref, v2_ref):
        g = p_ref[0].astype(f32)
        for q in range(1, n_parts):
            g = g + p_ref[q].astype(f32)
        d, m2, v2 = _adam_math(w_ref[...], g, m_ref[...], v_ref[...])
        g_ref[...] = g
        d_ref[...] = d
        m2_ref[...] = m2
        v2_ref[...] = v2

    blk = pl.BlockSpec((tr, c), lambda i: (i, 0))
    return pl.pallas_call(
        body, name=name, grid=(r // tr,),
        in_specs=[pl.BlockSpec((n_parts, tr, c), lambda i: (0, i, 0)), blk, blk, blk],
        out_specs=[blk] * 4, out_shape=[jax.ShapeDtypeStruct((r, c), f32)] * 4,
        compiler_params=_params(("arbitrary",)),
    )(parts, w, m, v)


def _adamw_small(items, *, name):
    n = len(items)

    def body(*refs):
        ins, outs = refs[:4 * n], refs[4 * n:]
        for k in range(n):
            g, w, m, v = (ins[4 * k + q][...] for q in range(4))
            d, m2, v2 = _adam_math(w, g, m, v)
            outs[3 * k][...] = d
            outs[3 * k + 1][...] = m2
            outs[3 * k + 2][...] = v2

    vm = pl.BlockSpec(memory_space=pltpu.VMEM)
    flat = [a for item in items for a in item]
    out_shape = [jax.ShapeDtypeStruct(item[1].shape, f32) for item in items for _ in range(3)]
    return pl.pallas_call(
        body, name=name, in_specs=[vm] * (4 * n), out_specs=[vm] * (3 * n), out_shape=out_shape,
    )(*flat)


def _sum_parts(parts, *, name):
    def body(p_ref, o_ref):
        acc = p_ref[0]
        for q in range(1, N_DEV):
            acc = acc + p_ref[q]
        o_ref[...] = acc

    vm = pl.BlockSpec(memory_space=pltpu.VMEM)
    return pl.pallas_call(
        body, name=name, in_specs=[vm], out_specs=vm, out_shape=jax.ShapeDtypeStruct(parts.shape[1:], f32),
    )(parts)


WEIGHTS = ["ffn1_w_gate", "ffn1_w_up", "ffn1_w_down", "ln1_g", "ln1_b", "w_in", "b_forget", "conv_w", "conv_b",
           "rg_wa", "rg_ba", "rg_wx", "rg_bx", "lru_lambda", "w_out", "ln2_g", "ln2_b",
           "ffn2_w_gate", "ffn2_w_up", "ffn2_w_down", "ln3_g", "ln3_b"]
BIG = ["ffn1_w_gate", "ffn1_w_up", "ffn1_w_down", "w_in", "w_out", "ffn2_w_gate", "ffn2_w_up", "ffn2_w_down"]
PACKED = ["ln1_g", "ln1_b", "ln2_g", "ln2_b", "ln3_g", "ln3_b", "conv_b", "rg_ba", "rg_bx", "lru_lambda",
          "conv_w", "rg_wa", "rg_wx", "b_forget", "loss"]
PACK_ROWS = 600


def _two_d(a):
    return a.reshape((-1, a.shape[-1]))


def _transport(a):
    return _two_d(a)


def kernel(x, ffn1_w_gate, ffn1_w_up, ffn1_w_down, ln1_g, ln1_b, w_in, b_forget, conv_w, conv_b, rg_wa, rg_ba, rg_wx, rg_bx, lru_lambda, w_out, ln2_g, ln2_b, ffn2_w_gate, ffn2_w_up, ffn2_w_down, ln3_g, ln3_b, loss_target, m_ffn1_w_gate, m_ffn1_w_up, m_ffn1_w_down, m_ln1_g, m_ln1_b, m_w_in, m_b_forget, m_conv_w, m_conv_b, m_rg_wa, m_rg_ba, m_rg_wx, m_rg_bx, m_lru_lambda, m_w_out, m_ln2_g, m_ln2_b, m_ffn2_w_gate, m_ffn2_w_up, m_ffn2_w_down, m_ln3_g, m_ln3_b, v_ffn1_w_gate, v_ffn1_w_up, v_ffn1_w_down, v_ln1_g, v_ln1_b, v_w_in, v_b_forget, v_conv_w, v_conv_b, v_rg_wa, v_rg_ba, v_rg_wx, v_rg_bx, v_lru_lambda, v_w_out, v_ln2_g, v_ln2_b, v_ffn2_w_gate, v_ffn2_w_up, v_ffn2_w_down, v_ln3_g, v_ln3_b):
    w_args = (ffn1_w_gate, ffn1_w_up, ffn1_w_down, ln1_g, ln1_b, w_in, b_forget, conv_w, conv_b, rg_wa, rg_ba, rg_wx, rg_bx, lru_lambda, w_out, ln2_g, ln2_b, ffn2_w_gate, ffn2_w_up, ffn2_w_down, ln3_g, ln3_b)
    m_args = (m_ffn1_w_gate, m_ffn1_w_up, m_ffn1_w_down, m_ln1_g, m_ln1_b, m_w_in, m_b_forget, m_conv_w, m_conv_b, m_rg_wa, m_rg_ba, m_rg_wx, m_rg_bx, m_lru_lambda, m_w_out, m_ln2_g, m_ln2_b, m_ffn2_w_gate, m_ffn2_w_up, m_ffn2_w_down, m_ln3_g, m_ln3_b)
    v_args = (v_ffn1_w_gate, v_ffn1_w_up, v_ffn1_w_down, v_ln1_g, v_ln1_b, v_w_in, v_b_forget, v_conv_w, v_conv_b, v_rg_wa, v_rg_ba, v_rg_wx, v_rg_bx, v_lru_lambda, v_w_out, v_ln2_g, v_ln2_b, v_ffn2_w_gate, v_ffn2_w_up, v_ffn2_w_down, v_ln3_g, v_ln3_b)
    w = dict(zip(WEIGHTS, w_args))
    m = dict(zip(WEIGHTS, m_args))
    v = dict(zip(WEIGHTS, v_args))
    me = 4 * lax.axis_index("x") + 2 * lax.axis_index("y") + lax.axis_index("c")

    sent = {n: _transport(w[n]).astype(bf16) for n in BIG}
    sent["conv_w"] = _two_d(w["conv_w"])
    small = {n: w[n] for n in ("ln1_g", "ln1_b", "ln2_g", "ln2_b", "ln3_g", "ln3_b", "b_forget", "conv_b",
                               "lru_lambda")}
    small.update({n: w[n][0] for n in ("rg_wa", "rg_ba", "rg_wx", "rg_bx")})

    sq_rows, grad_x, parts, all_packed, small_shapes = _local_step(x[0], loss_target[0], sent, small)

    total = _sum_parts(all_packed, name="sum_small_grads").reshape(-1)
    grads, off = {}, 0
    for n in PACKED:
        size = math.prod(small_shapes[n])
        grads[n] = total[off:off + size].reshape(small_shapes[n])
        off += size
    loss = grads.pop("loss").reshape(())
    grads["conv_w"] = lax.dynamic_slice_in_dim(grads["conv_w"], me * (LRU_W // N_DEV), LRU_W // N_DEV, axis=1)

    delta, new_m, new_v = {}, {}, {}
    for n in BIG:
        g, d, m2, v2 = _adamw_big(parts[n], _transport(w[n]), _transport(m[n]), _transport(v[n]),
                                  name="adamw_" + n)
        grads[n], delta[n], new_m[n], new_v[n] = g, d, m2, v2
    small_names = [n for n in WEIGHTS if n not in BIG]
    outs = _adamw_small([(_two_d(grads[n]), _two_d(w[n]), _two_d(m[n]), _two_d(v[n])) for n in small_names],
                        name="adamw_small")
    for k, n in enumerate(small_names):
        delta[n], new_m[n], new_v[n] = outs[3 * k], outs[3 * k + 1], outs[3 * k + 2]

    def shaped(d):
        return [d[n].reshape(w[n].shape) for n in WEIGHTS]

    return (loss, grad_x[None], *shaped(grads), *shaped(delta), *shaped(new_m), *shaped(new_v))
```

```python
import functools
import math

import jax
import jax.numpy as jnp
from jax import lax
from jax.experimental import pallas as pl
from jax.experimental.pallas import tpu as pltpu

f32 = jnp.float32
bf16 = jnp.bfloat16

N_DEV = 8
D_MODEL = 1024
D_FF = 4096
FF_TILE = D_FF // N_DEV
FOX_W = 512
LRU_W = 512
HEADS = 8
HEAD_D = 64
IN_COLS = 2568
IN_SHARD = IN_COLS // N_DEV
LANES = 128
LN_EPS = 1e-5
ALPHA = 2.0 ** 0.25
ATT_SCALE = 1.0 / math.sqrt(HEAD_D)
LRU_C = 8.0
NEG_BIG = -1e30

ADAM_LR = 0.001
ADAM_B1 = 0.9
ADAM_B2 = 0.999
ADAM_EPS = 1e-08
ADAM_WD = 0.01
ADAM_STEP = 10

VMEM_LIMIT = 56 * 1024 * 1024
MESH_T = pl.DeviceIdType.MESH


def _params(sem, **kw):
    return pltpu.CompilerParams(dimension_semantics=sem, vmem_limit_bytes=VMEM_LIMIT, **kw)


def _sigmoid(x):
    return 1.0 / (1.0 + jnp.exp(-x))


def _sigmoid_tanh(x):
    return 0.5 * jnp.tanh(0.5 * x) + 0.5


def _softplus(x):
    return jnp.maximum(x, 0.0) + jnp.log(1.0 + jnp.exp(-jnp.abs(x)))


def _one_minus_exp(x):
    series = -x * (1.0 + x * (0.5 + x * (1.0 / 6 + x * (1.0 / 24 + x * (1.0 / 120 + x * (1.0 / 720))))))
    return jnp.where(x > -0.125, series, 1.0 - jnp.exp(x))


_GELU_C = math.sqrt(2.0 / math.pi)


def _gelu_and_grad(x):
    inner = _GELU_C * (x + 0.044715 * x * x * x)
    t = jnp.tanh(inner)
    g = 0.5 * x * (1.0 + t)
    dg = 0.5 * (1.0 + t) + 0.5 * x * (1.0 - t * t) * _GELU_C * (1.0 + 3 * 0.044715 * x * x)
    return g, dg


def _ln_fwd_tile(pre):
    mu = jnp.mean(pre, axis=-1, keepdims=True)
    xc = pre - mu
    var = jnp.mean(xc * xc, axis=-1, keepdims=True)
    rstd = lax.rsqrt(var + LN_EPS)
    return xc * rstd, rstd


def _ln_bwd_tile(dy, xhat, rstd, g):
    dyg = dy * g
    m1 = jnp.mean(dyg, axis=-1, keepdims=True)
    m2 = jnp.mean(dyg * xhat, axis=-1, keepdims=True)
    dpre = rstd * (dyg - m1 - xhat * m2)
    return dpre, jnp.sum(dy * xhat, axis=0, keepdims=True), jnp.sum(dy, axis=0, keepdims=True)


_NT = (((1,), (1,)), ((), ()))
_TN = (((0,), (0,)), ((), ()))


class _Exchange:
    def __init__(self, arrs, gather, chips=False):
        self.arrs, self.gather, self.n, self.chips = list(arrs), gather, len(arrs), chips

    def out_shape(self):
        return [jax.ShapeDtypeStruct(((N_DEV,) + a.shape) if self.gather else a.shape, a.dtype) for a in self.arrs]

    def scratch(self):
        n_remote = self.n * (N_DEV - 1)
        return [pltpu.SemaphoreType.DMA((n_remote,)), pltpu.SemaphoreType.DMA((n_remote,)),
                pltpu.SemaphoreType.DMA((self.n,))]

    def copies(self, ins, outs, sems):
        send_sems, recv_sems, local_sems = sems
        x, y, c = lax.axis_index("x"), lax.axis_index("y"), lax.axis_index("c")
        me = 2 * x + y if self.chips else 4 * x + 2 * y + c
        out = []
        for k in range(self.n):
            for d in (range(2, N_DEV, 2) if self.chips else range(1, N_DEV)):
                px = 1 - x if d & 4 else x
                py = 1 - y if d & 2 else y
                pc = 1 - c if d & 1 else c
                sem = k * (N_DEV - 1) + d - 1
                out.append(pltpu.make_async_remote_copy(
                    src_ref=ins[k].at[2 * px + py if self.chips else 4 * px + 2 * py + pc], dst_ref=outs[k].at[me],
                    send_sem=send_sems.at[sem], recv_sem=recv_sems.at[sem],
                    device_id=(px, py, pc), device_id_type=MESH_T))
            out.append(pltpu.make_async_copy(ins[k].at[me], outs[k].at[me], local_sems.at[k]))
        return out

    def gather_copies(self, ins, outs, sems):
        send_sems, recv_sems, local_sems = sems
        x, y, c = lax.axis_index("x"), lax.axis_index("y"), lax.axis_index("c")
        sibling = (x, y, 1 - c)
        chips = [(1 - x, y), (x, 1 - y), (1 - x, 1 - y)]
        out = []
        for k in range(self.n):
            def copy(s, block, to, src=None, k=k):
                rows = outs[k].at[4 * block[0] + 2 * block[1] + block[2]]
                sem = k * (N_DEV - 1) + s
                return pltpu.make_async_remote_copy(
                    src_ref=rows if src is None else src, dst_ref=rows, send_sem=send_sems.at[sem],
                    recv_sem=recv_sems.at[sem], device_id=to, device_id_type=MESH_T)

            first = [copy(0, (x, y, c), sibling, src=ins[k])]
            first += [copy(1 + q, (x, y, c), (*chip, c), src=ins[k]) for q, chip in enumerate(chips)]
            passed = [copy(4 + q, (*chip, c), sibling) for q, chip in enumerate(chips)]
            own = pltpu.make_async_copy(ins[k], outs[k].at[4 * x + 2 * y + c], local_sems.at[k])
            out.append((first, passed, own, copy))
        return out, sibling, chips, (x, y, c)

    def start(self, ins, outs, sems):
        if not self.gather:
            for cp in self.copies(ins, outs, sems):
                cp.start()
            return
        per_array, _, _, _ = self.gather_copies(ins, outs, sems)
        for first, _, own, _ in per_array:
            own.start()
            for cp in first:
                cp.start()

    def relay(self, ins, outs, sems):
        per_array, sibling, chips, (x, y, c) = self.gather_copies(ins, outs, sems)
        for first, passed, own, copy in per_array:
            for q, chip in enumerate(chips):
                copy(1 + q, (*chip, c), (x, y, c)).wait_recv()
                passed[q].start()

    def wait(self, ins, outs, sems, relayed=False):
        if not self.gather:
            for cp in self.copies(ins, outs, sems):
                cp.wait()
            return
        if not relayed:
            self.relay(ins, outs, sems)
        per_array, sibling, chips, (x, y, c) = self.gather_copies(ins, outs, sems)
        for first, passed, own, copy in per_array:
            copy(0, sibling, (x, y, c)).wait_recv()
            for q, chip in enumerate(chips):
                copy(4 + q, (*chip, 1 - c), (x, y, c)).wait_recv()
            for cp in first + passed:
                cp.wait_send()
            own.wait()


class _Hosts:
    gather = False

    def __init__(self, *hosts):
        self.hosts = hosts
        self.n = sum(h.n for h in hosts)
        self.arrs = [a for h in hosts for a in h.arrs]

    def out_shape(self):
        return [sh for h in self.hosts for sh in h.out_shape()]

    def scratch(self):
        return [sc for h in self.hosts for sc in h.scratch()]

    def _each(self, ins, outs, sems):
        at = 0
        for k, h in enumerate(self.hosts):
            yield h, ins[at:at + h.n], outs[at:at + h.n], sems[3 * k:3 * k + 3]
            at += h.n

    def start(self, ins, outs, sems):
        for h, h_in, h_out, h_sems in self._each(ins, outs, sems):
            h.start(h_in, h_out, h_sems)

    def wait(self, ins, outs, sems, relayed=False):
        for h, h_in, h_out, h_sems in self._each(ins, outs, sems):
            h.wait(h_in, h_out, h_sems)


def _hosted_call(host, body, *, name, grid, in_specs, out_specs, out_shape, scratch_shapes=(), compiler_params):
    out_specs = list(out_specs) if isinstance(out_specs, (list, tuple)) else [out_specs]
    out_shape = list(out_shape) if isinstance(out_shape, (list, tuple)) else [out_shape]
    if host is None:
        return pl.pallas_call(body, name=name, grid=grid, in_specs=in_specs, out_specs=out_specs,
                              out_shape=out_shape, scratch_shapes=list(scratch_shapes),
                              compiler_params=compiler_params)
    n_in, n_out, n_scr, k = len(in_specs), len(out_shape), len(scratch_shapes), host.n

    def wrapped(*refs):
        ins, h_in = refs[:n_in], refs[n_in:n_in + k]
        outs, h_out = refs[n_in + k:n_in + k + n_out], refs[n_in + k + n_out:n_in + 2 * k + n_out]
        scr, sems = refs[n_in + 2 * k + n_out:n_in + 2 * k + n_out + n_scr], refs[n_in + 2 * k + n_out + n_scr:]
        ids = [pl.program_id(a) for a in range(len(grid))]
        first = functools.reduce(jnp.logical_and, [i == 0 for i in ids])
        last = functools.reduce(jnp.logical_and, [i == g - 1 for i, g in zip(ids, grid)])
        steps = math.prod(grid)
        relay_at = (3 * steps) // 4 if host.gather and steps >= 8 else None

        @pl.when(first)
        def _():
            host.start(h_in, h_out, sems)

        if relay_at is not None:
            coords, rest = [], relay_at
            for g in reversed(grid):
                coords.append(rest % g)
                rest //= g

            @pl.when(functools.reduce(jnp.logical_and, [i == cd for i, cd in zip(ids, reversed(coords))]))
            def _():
                host.relay(h_in, h_out, sems)

        body(*ins, *outs, *scr)

        @pl.when(last)
        def _():
            host.wait(h_in, h_out, sems, relayed=relay_at is not None)

    hbm = pl.BlockSpec(memory_space=pl.ANY)
    call = pl.pallas_call(
        wrapped, name=name, grid=grid, in_specs=list(in_specs) + [hbm] * k, out_specs=out_specs + [hbm] * k,
        out_shape=out_shape + host.out_shape(), scratch_shapes=list(scratch_shapes) + host.scratch(),
        compiler_params=compiler_params)
    return lambda *args: call(*args, *host.arrs)


def _ffn_fwd(xhat, g_in, b_in, wg, wu, wd, *, tm, name, host=None):
    t = xhat.shape[0]
    nj = N_DEV

    def body(x_ref, g_ref, b_ref, wg_ref, wu_ref, wd_ref, xo_ref, rstd_ref, hg_ref, hu_ref, xb, acc):
        j = pl.program_id(1)

        @pl.when(j == 0)
        def _():
            xb[...] = (x_ref[...] * g_ref[...] + b_ref[...]).astype(bf16)
            acc[...] = jnp.zeros_like(acc)

        hg = jnp.dot(xb[...], wg_ref[...], preferred_element_type=f32)
        hu = jnp.dot(xb[...], wu_ref[...], preferred_element_type=f32)
        hg_ref[...] = hg.astype(bf16)
        hu_ref[...] = hu.astype(bf16)
        a = hg * _sigmoid_tanh(hg) * hu
        acc[...] += jnp.dot(a.astype(bf16), wd_ref[...], preferred_element_type=f32)

        @pl.when(j == nj - 1)
        def _():
            x = x_ref[...] * g_ref[...] + b_ref[...]
            xo, rstd = _ln_fwd_tile(ALPHA * x + 0.5 * acc[...])
            xo_ref[...] = xo
            rstd_ref[...] = rstd

    row = pl.BlockSpec((1, D_MODEL), lambda i, j: (0, 0))
    return _hosted_call(
        host, body, name=name, grid=(t // tm, nj),
        in_specs=[pl.BlockSpec((tm, D_MODEL), lambda i, j: (i, 0)), row, row,
                  pl.BlockSpec((None, D_MODEL, FF_TILE), lambda i, j: (j, 0, 0)),
                  pl.BlockSpec((None, D_MODEL, FF_TILE), lambda i, j: (j, 0, 0)),
                  pl.BlockSpec((None, FF_TILE, D_MODEL), lambda i, j: (j, 0, 0))],
        out_specs=[pl.BlockSpec((tm, D_MODEL), lambda i, j: (i, 0)),
                   pl.BlockSpec((tm, 1), lambda i, j: (i, 0)),
                   pl.BlockSpec((tm, FF_TILE), lambda i, j: (i, j)),
                   pl.BlockSpec((tm, FF_TILE), lambda i, j: (i, j))],
        out_shape=[jax.ShapeDtypeStruct((t, D_MODEL), f32), jax.ShapeDtypeStruct((t, 1), f32),
                   jax.ShapeDtypeStruct((t, D_FF), bf16), jax.ShapeDtypeStruct((t, D_FF), bf16)],
        scratch_shapes=[pltpu.VMEM((tm, D_MODEL), bf16), pltpu.VMEM((tm, D_MODEL), f32)],
        compiler_params=_params(("arbitrary", "arbitrary")),
    )(xhat, g_in, b_in, wg, wu, wd)


def _ffn1_fwd_gathering(x, own, extra, *, tm, name):
    t = x.shape[0]
    n_i = t // tm
    n_arr = 3
    k_extra = extra.n
    ex = _Exchange(list(own), gather=True)
    ax, ay, ac = lax.axis_index("x"), lax.axis_index("y"), lax.axis_index("c")
    order = jnp.stack([4 * px + 2 * py + pc for px, py in ((ax, ay), (1 - ax, ay), (ax, 1 - ay), (1 - ax, 1 - ay))
                       for pc in (ac, 1 - ac)]).astype(jnp.int32)
    arrival = [None, (0, None), (1, 0), (4, None), (2, 1), (5, None), (3, 2), (6, None)]

    def body(order_ref, x_ref, *refs):
        w_in, e_in = refs[:n_arr], refs[n_arr:n_arr + k_extra]
        refs = refs[n_arr + k_extra:]
        xo_ref, rstd_ref, hg_ref, hu_ref = refs[:4]
        w_all, e_out = refs[4:4 + n_arr], refs[4 + n_arr:4 + n_arr + k_extra]
        acc, wgb, wub, wdb, fetch_sems, send_sems, recv_sems, local_sems = refs[4 + n_arr + k_extra:12 + n_arr + k_extra]
        e_sems = refs[12 + n_arr + k_extra:]
        bufs = (wgb, wub, wdb)
        s = pl.program_id(0)
        i = pl.program_id(1)
        per_array, sibling, chips, (x_, y_, c_) = ex.gather_copies(w_in, w_all, (send_sems, recv_sems, local_sems))

        def fetch(pos, slot):
            return [pltpu.make_async_copy(w_in[a] if pos == 0 else w_all[a].at[order_ref[pos]],
                                          bufs[a].at[slot], fetch_sems.at[n_arr * slot + a]) for a in range(n_arr)]

        def source_of(pos):
            chip = (x_, y_) if pos < 2 else chips[(pos - 2) // 2]
            return (*chip, c_ if pos % 2 == 0 else 1 - c_)

        @pl.when(jnp.logical_and(s == 0, i == 0))
        def _():
            for q in range(4):
                for first, _, own_copy, _ in per_array:
                    if q == 0:
                        own_copy.start()
                    first[q].start()
            for cp in fetch(0, 0):
                cp.start()
            for cp in fetch(0, 0):
                cp.wait()

        @pl.when(jnp.logical_and(s == N_DEV // 2, i == 0))
        def _():
            extra.start(e_in, e_out, e_sems)

        for pos in range(1, N_DEV):
            @pl.when(jnp.logical_and(s == pos - 1, i == n_i - 1))
            def _(pos=pos):
                sem, passes = arrival[pos]
                for _, passed, _, copy in per_array:
                    copy(sem, source_of(pos), (x_, y_, c_)).wait_recv()
                    if passes is not None:
                        passed[passes].start()
                for cp in fetch(pos, pos % 2):
                    cp.start()

            @pl.when(jnp.logical_and(s == pos, i == 0))
            def _(pos=pos):
                for cp in fetch(pos, pos % 2):
                    cp.wait()

        slot = s % 2
        xb = x_ref[...].astype(bf16)
        hg = jnp.dot(xb, wgb[slot], preferred_element_type=f32)
        hu = jnp.dot(xb, wub[slot], preferred_element_type=f32)
        hg_ref[...] = hg.astype(bf16)
        hu_ref[...] = hu.astype(bf16)
        a = hg * _sigmoid_tanh(hg) * hu
        part = jnp.dot(a.astype(bf16), wdb[slot], preferred_element_type=f32)

        @pl.when(s == 0)
        def _():
            acc[i] = part

        @pl.when(s > 0)
        def _():
            acc[i] += part

        @pl.when(s == N_DEV - 1)
        def _():
            xo, rstd = _ln_fwd_tile(ALPHA * x_ref[...] + 0.5 * acc[i])
            xo_ref[...] = xo
            rstd_ref[...] = rstd

        @pl.when(jnp.logical_and(s == N_DEV - 1, i == n_i - 1))
        def _():
            for first, passed, own_copy, _ in per_array:
                for cp in first + passed:
                    cp.wait_send()
                own_copy.wait()
            extra.wait(e_in, e_out, e_sems)

    hbm = pl.BlockSpec(memory_space=pl.ANY)
    last = N_DEV - 1
    tok_out = pl.BlockSpec((tm, D_MODEL), lambda s, i, o: (jnp.where(s == last, i, 0), 0))
    col_out = pl.BlockSpec((tm, 1), lambda s, i, o: (jnp.where(s == last, i, 0), 0))
    hid = pl.BlockSpec((tm, FF_TILE), lambda s, i, o: (i, o[s]))
    shard_shapes = [(N_DEV,) + w.shape for w in own]
    grid_spec = pltpu.PrefetchScalarGridSpec(
        num_scalar_prefetch=1, grid=(N_DEV, n_i),
        in_specs=[pl.BlockSpec((tm, D_MODEL), lambda s, i, o: (i, 0))] + [hbm] * (n_arr + k_extra),
        out_specs=[tok_out, col_out, hid, hid] + [hbm] * (n_arr + k_extra),
        scratch_shapes=[pltpu.VMEM((n_i, tm, D_MODEL), f32)]
        + [pltpu.VMEM((2,) + w.shape, bf16) for w in own]
        + [pltpu.SemaphoreType.DMA((2 * n_arr,))] + ex.scratch() + extra.scratch())
    res = pl.pallas_call(
        body, name=name, grid_spec=grid_spec,
        out_shape=[jax.ShapeDtypeStruct((t, D_MODEL), f32), jax.ShapeDtypeStruct((t, 1), f32),
                   jax.ShapeDtypeStruct((t, D_FF), bf16), jax.ShapeDtypeStruct((t, D_FF), bf16)]
        + [jax.ShapeDtypeStruct(sh, bf16) for sh in shard_shapes] + extra.out_shape(),
        compiler_params=_params(("arbitrary", "arbitrary")),
    )(order, x, *own, *extra.arrs)
    return res


def _ffn_bwd(dpre, hg, hu, wg, wu, wd, ln_in, *, tm, name, host=None):
    t = dpre.shape[0]
    nj = N_DEV
    with_ln = ln_in is not None

    def body(*refs):
        if with_ln:
            (dp_ref, hg_ref, hu_ref, wg_ref, wu_ref, wd_ref, xh_ref, rs_ref, g_ref,
             dx_ref, gg_ref, gb_ref, dhg_ref, dhu_ref, a_ref, dfb, acc) = refs
        else:
            (dp_ref, hg_ref, hu_ref, wg_ref, wu_ref, wd_ref,
             dx_ref, dhg_ref, dhu_ref, a_ref, dfb, acc) = refs
        i = pl.program_id(0)
        j = pl.program_id(1)

        @pl.when(j == 0)
        def _():
            dfb[...] = (0.5 * dp_ref[...]).astype(bf16)
            acc[...] = jnp.zeros_like(acc)

        da = lax.dot_general(dfb[...], wd_ref[...], _NT, preferred_element_type=f32)
        hgv = hg_ref[...].astype(f32)
        huv = hu_ref[...].astype(f32)
        sg = _sigmoid_tanh(hgv)
        silu = hgv * sg
        a_ref[...] = (silu * huv).astype(bf16)
        dhu = (da * silu).astype(bf16)
        dhg = (da * huv * (sg * (1.0 + hgv * (1.0 - sg)))).astype(bf16)
        dhg_ref[...] = dhg
        dhu_ref[...] = dhu
        acc[...] += (lax.dot_general(dhg, wg_ref[...], _NT, preferred_element_type=f32)
                     + lax.dot_general(dhu, wu_ref[...], _NT, preferred_element_type=f32))

        @pl.when(j == nj - 1)
        def _():
            dx = ALPHA * dp_ref[...] + acc[...]
            if with_ln:
                dprev, gg, gb = _ln_bwd_tile(dx, xh_ref[...], rs_ref[...], g_ref[...])
                dx_ref[...] = dprev

                @pl.when(i == 0)
                def _():
                    gg_ref[...] = gg
                    gb_ref[...] = gb

                @pl.when(i > 0)
                def _():
                    gg_ref[...] += gg
                    gb_ref[...] += gb
            else:
                dx_ref[...] = dx

    tok = pl.BlockSpec((tm, D_MODEL), lambda i, j: (i, 0), pipeline_mode=pl.Buffered(1))
    row = pl.BlockSpec((1, D_MODEL), lambda i, j: (0, 0))
    hid = pl.BlockSpec((tm, FF_TILE), lambda i, j: (i, j))
    in_specs = [tok, hid, hid,
                pl.BlockSpec((None, D_MODEL, FF_TILE), lambda i, j: (j, 0, 0)),
                pl.BlockSpec((None, D_MODEL, FF_TILE), lambda i, j: (j, 0, 0)),
                pl.BlockSpec((None, FF_TILE, D_MODEL), lambda i, j: (j, 0, 0))]
    args = [dpre, hg, hu, wg, wu, wd]
    out_specs = [tok]
    out_shape = [jax.ShapeDtypeStruct((t, D_MODEL), f32)]
    if with_ln:
        in_specs += [tok, pl.BlockSpec((tm, 1), lambda i, j: (i, 0)), row]
        args += list(ln_in)
        out_specs += [row, row]
        out_shape += [jax.ShapeDtypeStruct((1, D_MODEL), f32)] * 2
    out_specs += [hid, hid, hid]
    out_shape += [jax.ShapeDtypeStruct((t, D_FF), bf16)] * 3
    return _hosted_call(
        host, body, name=name, grid=(t // tm, nj), in_specs=in_specs, out_specs=out_specs, out_shape=out_shape,
        scratch_shapes=[pltpu.VMEM((tm, D_MODEL), bf16), pltpu.VMEM((tm, D_MODEL), f32)],
        compiler_params=_params(("arbitrary", "arbitrary")),
    )(*args)


def _ffn_bwd_act(dpre, hg, hu, wd, *, tm, name, host=None):
    t = dpre.shape[0]

    def body(dp_ref, hg_ref, hu_ref, wd_ref, dhg_ref, dhu_ref, a_ref, dfb):
        @pl.when(pl.program_id(1) == 0)
        def _():
            dfb[...] = (0.5 * dp_ref[...]).astype(bf16)

        da = lax.dot_general(dfb[...], wd_ref[...], _NT, preferred_element_type=f32)
        hgv = hg_ref[...].astype(f32)
        huv = hu_ref[...].astype(f32)
        sg = _sigmoid_tanh(hgv)
        silu = hgv * sg
        a_ref[...] = (silu * huv).astype(bf16)
        dhu_ref[...] = (da * silu).astype(bf16)
        dhg_ref[...] = (da * huv * (sg * (1.0 + hgv * (1.0 - sg)))).astype(bf16)

    hid = pl.BlockSpec((tm, FF_TILE), lambda i, j: (i, j))
    return _hosted_call(
        host, body, name=name, grid=(t // tm, N_DEV),
        in_specs=[pl.BlockSpec((tm, D_MODEL), lambda i, j: (i, 0)), hid, hid,
                  pl.BlockSpec((None, FF_TILE, D_MODEL), lambda i, j: (j, 0, 0))],
        out_specs=[hid, hid, hid], out_shape=[jax.ShapeDtypeStruct((t, D_FF), bf16)] * 3,
        scratch_shapes=[pltpu.VMEM((tm, D_MODEL), bf16)],
        compiler_params=_params(("arbitrary", "arbitrary")),
    )(dpre, hg, hu, wd)


def _ffn_bwd_dx(dpre, dhg, dhu, wg, wu, *, tm, name, host=None):
    t = dpre.shape[0]
    nj = N_DEV

    def body(dp_ref, dhg_ref, dhu_ref, wg_ref, wu_ref, dx_ref, acc):
        j = pl.program_id(1)

        @pl.when(j == 0)
        def _():
            acc[...] = jnp.zeros_like(acc)

        acc[...] += (lax.dot_general(dhg_ref[...], wg_ref[...], _NT, preferred_element_type=f32)
                     + lax.dot_general(dhu_ref[...], wu_ref[...], _NT, preferred_element_type=f32))

        @pl.when(j == nj - 1)
        def _():
            dx_ref[...] = ALPHA * dp_ref[...] + acc[...]

    tok = pl.BlockSpec((tm, D_MODEL), lambda i, j: (i, 0))
    hid = pl.BlockSpec((tm, FF_TILE), lambda i, j: (i, j))
    wspec = pl.BlockSpec((None, D_MODEL, FF_TILE), lambda i, j: (j, 0, 0))
    return _hosted_call(
        host, body, name=name, grid=(t // tm, nj), in_specs=[tok, hid, hid, wspec, wspec],
        out_specs=[tok], out_shape=[jax.ShapeDtypeStruct((t, D_MODEL), f32)],
        scratch_shapes=[pltpu.VMEM((tm, D_MODEL), f32)],
        compiler_params=_params(("arbitrary", "arbitrary")),
    )(dpre, dhg, dhu, wg, wu)


def _mm(a, b, *, mode, out_dtype, tm, tn, tk, name, affine=None, a_cols=None, b_cols=None,
        b_blocked=False, out_blocked=False, out_scale=None):
    if mode == "nn":
        m_full, k_full = a.shape
        m_dim, k_dim = (m_full, a_cols[1]) if a_cols else (m_full, k_full)
    else:
        k_dim, m_full = a.shape
        m_dim = a_cols[1] if a_cols else m_full
    a_off = a_cols[0] if a_cols else 0
    if b_blocked:
        n_dim = b.shape[0] * b.shape[2]
        assert b.shape[2] == tn
    else:
        n_dim = b_cols[1] if b_cols else b.shape[1]
    b_off = b_cols[0] if b_cols else 0
    assert m_dim % tm == 0 and n_dim % tn == 0 and k_dim % tk == 0, (name, m_dim, n_dim, k_dim)
    nk = k_dim // tk

    def body(*refs):
        if affine is not None:
            a_ref, g_ref, s_ref, b_ref, o_ref, acc = refs
        else:
            a_ref, b_ref, o_ref, acc = refs
        k = pl.program_id(2)

        @pl.when(k == 0)
        def _():
            acc[...] = jnp.zeros_like(acc)

        av = a_ref[...]
        if affine is not None:
            av = av * g_ref[...] + s_ref[...]
        av = av.astype(bf16)
        bv = b_ref[...].astype(bf16)
        if mode == "nn":
            acc[...] += jnp.dot(av, bv, preferred_element_type=f32)
        else:
            acc[...] += lax.dot_general(av, bv, _TN, preferred_element_type=f32)

        @pl.when(k == nk - 1)
        def _():
            res = acc[...] if out_scale is None else acc[...] * out_scale
            o_ref[...] = res.astype(out_dtype)

    if mode == "nn":
        a_spec = pl.BlockSpec((tm, tk), lambda i, j, k: (i, k + a_off))
        aff_spec = pl.BlockSpec((1, tk), lambda i, j, k: (0, k + a_off))
    else:
        a_spec = pl.BlockSpec((tk, tm), lambda i, j, k: (k, i + a_off))
        aff_spec = pl.BlockSpec((1, tm), lambda i, j, k: (0, i + a_off))
    if b_blocked:
        b_spec = pl.BlockSpec((None, tk, tn), lambda i, j, k: (j, k, 0))
    else:
        b_spec = pl.BlockSpec((tk, tn), lambda i, j, k: (k, j + b_off))
    if out_blocked:
        o_spec = pl.BlockSpec((None, tm, tn), lambda i, j, k: (j, i, 0))
        o_shape = jax.ShapeDtypeStruct((n_dim // tn, m_dim, tn), out_dtype)
    else:
        o_spec = pl.BlockSpec((tm, tn), lambda i, j, k: (i, j))
        o_shape = jax.ShapeDtypeStruct((m_dim, n_dim), out_dtype)
    in_specs = [a_spec] + ([aff_spec, aff_spec] if affine is not None else []) + [b_spec]
    args = [a] + (list(affine) if affine is not None else []) + [b]
    return pl.pallas_call(
        body, name=name, grid=(m_dim // tm, n_dim // tn, nk), in_specs=in_specs, out_specs=o_spec,
        out_shape=o_shape, scratch_shapes=[pltpu.VMEM((tm, tn), f32)],
        compiler_params=_params(("arbitrary", "arbitrary", "arbitrary")),
    )(*args)


def _mm_tn(a, b, *, out_dtype, tm, mb, tn, nb, tk, name, affine=None, out_blocked=False, out_scale=None,
           pair=False, host=None):
    k_dim, m_dim = a.shape
    multi_b = isinstance(b, (list, tuple))
    b_list = list(b) if multi_b else [b]
    n_dim = nb * tn if multi_b else b.shape[1]
    assert m_dim % (mb * tm) == 0 and n_dim % (nb * tn) == 0 and k_dim % tk == 0, (name, m_dim, n_dim, k_dim)
    nk = k_dim // tk
    grid = (m_dim // (mb * tm), n_dim // (nb * tn), nk)
    if pair:
        assert mb * nb == 4 and grid[0] * grid[1] == 2 and out_dtype == bf16, name

    def body(*refs):
        if pair:
            refs, (acc, send_buf, recv_buf, send_sems, recv_sems) = refs[:-5], refs[-5:]
        else:
            refs, acc = refs[:-1], refs[-1]
        a_ref, o_ref = refs[0], refs[-1]
        if affine is not None:
            g_ref, s_ref = refs[1:3]
        b_refs = refs[3 if affine is not None else 1:-1]
        k = pl.program_id(2)

        @pl.when(k == 0)
        def _():
            acc[...] = jnp.zeros_like(acc)

        av = a_ref[...]
        if affine is not None:
            av = av * g_ref[...] + s_ref[...]
        av = av.astype(bf16)
        if multi_b:
            pieces = [r[...].astype(bf16) for r in b_refs]
        else:
            bv = b_refs[0][...].astype(bf16)
            pieces = [bv[:, jn * tn:(jn + 1) * tn] for jn in range(nb)]
        for im in range(mb):
            a_t = av[:, im * tm:(im + 1) * tm].T
            for jn in range(nb):
                acc[im * nb + jn] += jnp.dot(a_t, pieces[jn], preferred_element_type=f32)

        def scaled(v):
            return v if out_scale is None else v * out_scale

        @pl.when(k == nk - 1)
        def _():
            if pair:
                x, y, c = lax.axis_index("x"), lax.axis_index("y"), lax.axis_index("c")
                window = pl.program_id(0) + pl.program_id(1)
                swaps = []
                for cc in range(2):
                    send_buf[cc] = scaled(acc[2 * cc + 1 - c]).astype(bf16)
                    swaps.append(pltpu.make_async_remote_copy(
                        src_ref=send_buf.at[cc], dst_ref=recv_buf.at[window, cc],
                        send_sem=send_sems.at[2 * window + cc], recv_sem=recv_sems.at[2 * window + cc],
                        device_id=(x, y, 1 - c), device_id_type=MESH_T))
                    swaps[cc].start()
                for cc in range(2):
                    swaps[cc].wait_recv()
                    o_ref[cc] = (scaled(acc[2 * cc + c]) + recv_buf[window, cc].astype(f32)).astype(bf16)
                for cc in range(2):
                    swaps[cc].wait_send()
                return
            for im in range(mb):
                for jn in range(nb):
                    res = scaled(acc[im * nb + jn])
                    if out_blocked:
                        o_ref[jn, im * tm:(im + 1) * tm, :] = res.astype(out_dtype)
                    else:
                        o_ref[im * tm:(im + 1) * tm, jn * tn:(jn + 1) * tn] = res.astype(out_dtype)

    a_spec = pl.BlockSpec((tk, mb * tm), lambda i, j, k: (k, i))
    aff_spec = pl.BlockSpec((1, mb * tm), lambda i, j, k: (0, i))
    if multi_b:
        b_specs = [pl.BlockSpec((tk, tn), lambda i, j, k: (k, 0))] * nb
    else:
        b_specs = [pl.BlockSpec((tk, nb * tn), lambda i, j, k: (k, j))]
    scratch = [pltpu.VMEM((mb * nb, tm, tn), f32)]
    if pair:
        o_spec = pl.BlockSpec((2, tm, tn), lambda i, j, k: (i + j, 0, 0))
        o_shape = jax.ShapeDtypeStruct((4, tm, tn), out_dtype)
        scratch += [pltpu.VMEM((2, tm, tn), bf16), pltpu.VMEM((2, 2, tm, tn), bf16),
                    pltpu.SemaphoreType.DMA((4,)), pltpu.SemaphoreType.DMA((4,))]
    elif out_blocked:
        o_spec = pl.BlockSpec((nb, mb * tm, tn), lambda i, j, k: (j, i, 0))
        o_shape = jax.ShapeDtypeStruct((n_dim // tn, m_dim, tn), out_dtype)
    else:
        o_spec = pl.BlockSpec((mb * tm, nb * tn), lambda i, j, k: (i, j))
        o_shape = jax.ShapeDtypeStruct((m_dim, n_dim), out_dtype)
    in_specs = [a_spec] + ([aff_spec, aff_spec] if affine is not None else []) + b_specs
    args = [a] + (list(affine) if affine is not None else []) + b_list
    res = _hosted_call(
        host, body, name=name, grid=grid, in_specs=in_specs, out_specs=o_spec, out_shape=o_shape,
        scratch_shapes=scratch, compiler_params=_params(("arbitrary", "arbitrary", "arbitrary")),
    )(*args)
    return res[0] if host is None else res


def _in_proj(xhat, g, b, w_in, *, tm, name):
    t = xhat.shape[0]
    n_qkv, n_l = 3 * FOX_W, 2 * LRU_W

    def body(x_ref, g_ref, b_ref, w_ref, qkv_ref, zl_ref, zfg_ref):
        xb = (x_ref[...] * g_ref[...] + b_ref[...]).astype(bf16)
        qkv_ref[...] = jnp.dot(xb, w_ref[:, :n_qkv], preferred_element_type=f32).astype(bf16)
        zl_ref[...] = jnp.dot(xb, w_ref[:, n_qkv:n_qkv + n_l], preferred_element_type=f32)
        zfg_ref[...] = jnp.dot(xb, w_ref[:, n_qkv + n_l:], preferred_element_type=f32)

    row = pl.BlockSpec((1, D_MODEL), lambda i: (0, 0))
    return pl.pallas_call(
        body, name=name, grid=(t // tm,),
        in_specs=[pl.BlockSpec((tm, D_MODEL), lambda i: (i, 0)), row, row,
                  pl.BlockSpec(w_in.shape, lambda i: (0, 0))],
        out_specs=[pl.BlockSpec((tm, n_qkv), lambda i: (i, 0)), pl.BlockSpec((tm, n_l), lambda i: (i, 0)),
                   pl.BlockSpec((tm, LANES), lambda i: (i, 0))],
        out_shape=[jax.ShapeDtypeStruct((t, n_qkv), bf16), jax.ShapeDtypeStruct((t, n_l), f32),
                   jax.ShapeDtypeStruct((t, LANES), f32)],
        compiler_params=_params(("arbitrary",)),
    )(xhat, g, b, w_in)


def _mmln(pairs, *, tm, name, resid=None, resid_scale=1.0, epi=None, ln=None, n_out=D_MODEL):
    t = pairs[0][0].shape[0]
    n_pairs = len(pairs)
    n_resid = 0 if resid is None else len(resid) - 1

    def body(*refs):
        pos = 0
        val = None
        for p in range(n_pairs):
            a_ref, b_ref = refs[pos], refs[pos + 1]
            pos += 2
            av = a_ref[...].astype(bf16)
            bv = b_ref[...].astype(bf16)
            if pairs[p][6] == "nn":
                term = jnp.dot(av, bv, preferred_element_type=f32)
            else:
                term = lax.dot_general(av, bv, _NT, preferred_element_type=f32)
            val = term if val is None else val + term
        if resid is not None:
            if resid[0] == "plain":
                r = refs[pos][...]
            else:
                r = refs[pos][...] * refs[pos + 1][...] + refs[pos + 2][...]
            pos += n_resid
            val = val + resid_scale * r
        if epi is None:
            o_ref = refs[pos]
            o_ref[...] = val.astype(o_ref.dtype)
        elif epi == "ln_fwd":
            xo, rstd = _ln_fwd_tile(val)
            refs[pos][...] = xo
            refs[pos + 1][...] = rstd
        else:
            xh_ref, rs_ref, g_ref, dx_ref, gg_ref, gb_ref = refs[pos:pos + 6]
            dprev, gg, gb = _ln_bwd_tile(val, xh_ref[...], rs_ref[...], g_ref[...])
            dx_ref[...] = dprev
            i = pl.program_id(0)

            @pl.when(i == 0)
            def _():
                gg_ref[...] = gg
                gb_ref[...] = gb

            @pl.when(i > 0)
            def _():
                gg_ref[...] += gg
                gb_ref[...] += gb

    in_specs, args = [], []
    for (a, acb, aw, b, bcb, bw, mode) in pairs:
        in_specs.append(pl.BlockSpec((tm, aw), lambda i, acb=acb: (i, acb)))
        args.append(a)
        if mode == "nn":
            in_specs.append(pl.BlockSpec((aw, n_out), lambda i, bcb=bcb: (bcb, 0)))
        else:
            in_specs.append(pl.BlockSpec((n_out, bw), lambda i, bcb=bcb: (0, bcb)))
        args.append(b)
    tok = pl.BlockSpec((tm, n_out), lambda i: (i, 0))
    row = pl.BlockSpec((1, n_out), lambda i: (0, 0))
    col = pl.BlockSpec((tm, 1), lambda i: (i, 0))
    if resid is not None:
        in_specs += [tok] if resid[0] == "plain" else [tok, row, row]
        args += list(resid[1:])
    if epi is None:
        out_specs, out_shape = tok, jax.ShapeDtypeStruct((t, n_out), f32)
    elif epi == "ln_fwd":
        out_specs = [tok, col]
        out_shape = [jax.ShapeDtypeStruct((t, n_out), f32), jax.ShapeDtypeStruct((t, 1), f32)]
    else:
        in_specs += [tok, col, row]
        args += list(ln)
        out_specs = [tok, row, row]
        out_shape = [jax.ShapeDtypeStruct((t, n_out), f32)] + [jax.ShapeDtypeStruct((1, n_out), f32)] * 2
    return pl.pallas_call(
        body, name=name, grid=(t // tm,), in_specs=in_specs, out_specs=out_specs, out_shape=out_shape,
        compiler_params=_params(("arbitrary",)),
    )(*args)


def _loss_bwd(xhat, rstd, g, b, target, *, tm, name):
    t = xhat.shape[0]

    def body(xh_ref, rs_ref, g_ref, b_ref, tg_ref, dx_ref, sq_ref, gg_ref, gb_ref):
        i = pl.program_id(0)
        xh = xh_ref[...]
        diff = xh * g_ref[...] + b_ref[...] - tg_ref[...]
        sq = jnp.sum(diff * diff, axis=0, keepdims=True)
        dprev, gg, gb = _ln_bwd_tile(diff * (1.0 / D_MODEL), xh, rs_ref[...], g_ref[...])
        dx_ref[...] = dprev

        @pl.when(i == 0)
        def _():
            sq_ref[...] = sq
            gg_ref[...] = gg
            gb_ref[...] = gb

        @pl.when(i > 0)
        def _():
            sq_ref[...] += sq
            gg_ref[...] += gg
            gb_ref[...] += gb

    tok = pl.BlockSpec((tm, D_MODEL), lambda i: (i, 0))
    row = pl.BlockSpec((1, D_MODEL), lambda i: (0, 0))
    return pl.pallas_call(
        body, name=name, grid=(t // tm,),
        in_specs=[tok, pl.BlockSpec((tm, 1), lambda i: (i, 0)), row, row, tok],
        out_specs=[tok, row, row, row],
        out_shape=[jax.ShapeDtypeStruct((t, D_MODEL), f32)] + [jax.ShapeDtypeStruct((1, D_MODEL), f32)] * 3,
        compiler_params=_params(("arbitrary",)),
    )(xhat, rstd, g, b, target)


CUM_TILE = 256


def _tri(n, lower):
    r = lax.broadcasted_iota(jnp.int32, (n, n), 0)
    c = lax.broadcasted_iota(jnp.int32, (n, n), 1)
    return jnp.where((r >= c) if lower else (r <= c), 1.0, 0.0).astype(f32)


def _cum_fwd(zfg, bfg, *, name):
    t = zfg.shape[0]

    def body(z_ref, b_ref, o_ref, carry):
        @pl.when(pl.program_id(0) == 0)
        def _():
            carry[...] = jnp.zeros_like(carry)

        ls = -_softplus(-(z_ref[...] + b_ref[...]))
        c = jnp.dot(_tri(CUM_TILE, True), ls, preferred_element_type=f32,
                    precision=lax.Precision.HIGHEST) + carry[...]
        o_ref[...] = c
        carry[...] = c[CUM_TILE - 1:CUM_TILE, :]

    blk = pl.BlockSpec((CUM_TILE, LANES), lambda i: (i, 0))
    return pl.pallas_call(
        body, name=name, grid=(t // CUM_TILE,),
        in_specs=[blk, pl.BlockSpec((1, LANES), lambda i: (0, 0))], out_specs=blk,
        out_shape=jax.ShapeDtypeStruct((t, LANES), f32), scratch_shapes=[pltpu.VMEM((1, LANES), f32)],
        compiler_params=_params(("arbitrary",)),
    )(zfg, bfg)


def _cum_bwd(dcum_q, dcum_k, zfg, bfg, *, name):
    t = zfg.shape[0]
    n = t // CUM_TILE

    def body(d_ref, d2_ref, z_ref, b_ref, o_ref, s_ref, carry):
        i = pl.program_id(0)

        @pl.when(i == 0)
        def _():
            carry[...] = jnp.zeros_like(carry)

        dls = jnp.dot(_tri(CUM_TILE, False), d_ref[...] + d2_ref[...], preferred_element_type=f32,
                      precision=lax.Precision.HIGHEST) + carry[...]
        carry[...] = dls[0:1, :]
        lane = lax.broadcasted_iota(jnp.int32, (CUM_TILE, LANES), 1)
        dfg = jnp.where(lane < HEADS, dls * _sigmoid(-(z_ref[...] + b_ref[...])), 0.0)
        o_ref[...] = dfg
        tot = jnp.sum(dfg, axis=0, keepdims=True)

        @pl.when(i == 0)
        def _():
            s_ref[...] = tot

        @pl.when(i > 0)
        def _():
            s_ref[...] += tot

    blk = pl.BlockSpec((CUM_TILE, LANES), lambda i: (n - 1 - i, 0))
    row = pl.BlockSpec((1, LANES), lambda i: (0, 0))
    return pl.pallas_call(
        body, name=name, grid=(n,), in_specs=[blk, blk, blk, row], out_specs=[blk, row],
        out_shape=[jax.ShapeDtypeStruct((t, LANES), f32), jax.ShapeDtypeStruct((1, LANES), f32)],
        scratch_shapes=[pltpu.VMEM((1, LANES), f32)],
        compiler_params=_params(("arbitrary",)),
    )(dcum_q, dcum_k, zfg, bfg)


ATT_TILE = 512


def _causal(i, j, transposed):
    r = lax.broadcasted_iota(jnp.int32, (ATT_TILE, ATT_TILE), 0)
    c = lax.broadcasted_iota(jnp.int32, (ATT_TILE, ATT_TILE), 1)
    if transposed:
        return (c + i * ATT_TILE) >= (r + j * ATT_TILE)
    return (r + i * ATT_TILE) >= (c + j * ATT_TILE)


ATT_W = HEADS * LANES


def _data_lane(h):
    return HEAD_D * (h % 2)


def _extra_lane(h):
    return HEAD_D - _data_lane(h)


def _split3(x):
    hi = x.astype(bf16)
    rest = x - hi.astype(f32)
    mid = rest.astype(bf16)
    lo = (rest - mid.astype(f32)).astype(bf16)
    return hi, mid, lo


def _augment(pair, h, first, second, fill=0.0):
    rows = pair.shape[0]
    lane = lax.broadcasted_iota(jnp.int32, (rows, LANES), 1)
    base = _extra_lane(h)
    own = (lane < HEAD_D) if h % 2 == 0 else (lane >= HEAD_D)
    out = jnp.where(own, pair, jnp.full((rows, LANES), fill, bf16))
    for off, src in ((0, first), (3, second)):
        for q in range(3):
            val = src[q] if isinstance(src, tuple) else jnp.full((rows, 1), src, bf16)
            out = jnp.where(lane == base + off + q, val, out)
    return out


def _attn_prep_fwd(qkv, cum, *, tm, name):
    t = qkv.shape[0]

    def body(q_ref, k_ref, v_ref, c_ref, qa_ref, ka_ref, va_ref):
        for h in range(HEADS):
            pair = slice(LANES * (h // 2), LANES * (h // 2 + 1))
            hs = slice(LANES * h, LANES * (h + 1))
            c3 = _split3(c_ref[:, h:h + 1])
            qa_ref[:, hs] = _augment(q_ref[:, pair] * ATT_SCALE, h, c3, 1.0)
            ka_ref[:, hs] = _augment(k_ref[:, pair], h, 1.0, tuple(-p for p in c3))
            va_ref[:, hs] = _augment(v_ref[:, pair], h, 1.0, 1.0, fill=1.0)

    wide = pl.BlockSpec((tm, ATT_W), lambda i: (i, 0))
    out = jax.ShapeDtypeStruct((t, ATT_W), bf16)
    return pl.pallas_call(
        body, name=name, grid=(t // tm,),
        in_specs=[pl.BlockSpec((tm, FOX_W), lambda i: (i, 0)), pl.BlockSpec((tm, FOX_W), lambda i: (i, 1)),
                  pl.BlockSpec((tm, FOX_W), lambda i: (i, 2)), pl.BlockSpec((tm, LANES), lambda i: (i, 0))],
        out_specs=[wide] * 3, out_shape=[out] * 3, compiler_params=_params(("arbitrary",)),
    )(qkv, qkv, qkv, cum)


def _attn_prep_bwd(qkv, cum, lse, dmix, o, *, tm, name):
    t = qkv.shape[0]

    def body(q_ref, c_ref, l_ref, do_ref, o_ref, qa_ref, da_ref):
        for h in range(HEADS):
            pair = slice(LANES * (h // 2), LANES * (h // 2 + 1))
            src = slice(HEAD_D * h, HEAD_D * (h + 1))
            hs = slice(LANES * h, LANES * (h + 1))
            delta = jnp.sum(do_ref[:, src] * o_ref[:, src], axis=-1, keepdims=True)
            qa_ref[:, hs] = _augment(q_ref[:, pair] * ATT_SCALE, h,
                                     _split3(c_ref[:, h:h + 1] - l_ref[:, h:h + 1]), 1.0)
            da_ref[:, hs] = _augment(do_ref[:, pair].astype(bf16), h, tuple(-p for p in _split3(delta)), 0.0)

    wide = pl.BlockSpec((tm, ATT_W), lambda i: (i, 0))
    half = pl.BlockSpec((tm, FOX_W), lambda i: (i, 0))
    col = pl.BlockSpec((tm, LANES), lambda i: (i, 0))
    out = jax.ShapeDtypeStruct((t, ATT_W), bf16)
    return pl.pallas_call(
        body, name=name, grid=(t // tm,), in_specs=[half, col, col, half, half],
        out_specs=[wide] * 2, out_shape=[out] * 2, compiler_params=_params(("arbitrary",)),
    )(qkv, cum, lse, dmix, o)


def _attn_fwd2(q_aug, k_aug, v_aug, *, name, host=None):
    t = q_aug.shape[0]
    n = t // ATT_TILE
    tq = ATT_TILE

    def body(q_ref, k_ref, v_ref, o_ref, lse_ref, acc, m_s):
        i = pl.program_id(0)
        j = pl.program_id(1)

        @pl.when(j == 0)
        def _():
            acc[...] = jnp.zeros_like(acc)
            m_s[...] = jnp.full_like(m_s, NEG_BIG)

        def block(masked):
            mask = _causal(i, j, False) if masked else None
            for h in range(HEADS):
                hs = slice(LANES * h, LANES * (h + 1))
                s = lax.dot_general(q_ref[:, hs], k_ref[:, hs], _NT, preferred_element_type=f32)
                if masked:
                    s = jnp.where(mask, s, NEG_BIG)
                blocks = [s[:, LANES * b:LANES * (b + 1)] for b in range(tq // LANES)]
                m_old = m_s[h]
                m_new = jnp.maximum(m_old, jnp.broadcast_to(
                    jnp.max(functools.reduce(jnp.maximum, blocks), axis=-1, keepdims=True), (tq, LANES)))
                p = jnp.concatenate([jnp.exp(b - m_new) for b in blocks], axis=1).astype(bf16)
                acc[h] = jnp.exp(m_old - m_new) * acc[h] + jnp.dot(p, v_ref[:, hs], preferred_element_type=f32)
                m_s[h] = m_new

        @pl.when(j < i)
        def _():
            block(False)

        @pl.when(j == i)
        def _():
            block(True)
            lse_ref[...] = jnp.zeros_like(lse_ref)
            for h in range(HEADS):
                a = acc[h]
                l = a[:, _extra_lane(h):_extra_lane(h) + 1]
                o_ref[:, HEAD_D * h:HEAD_D * (h + 1)] = a[:, _data_lane(h):_data_lane(h) + HEAD_D] / l
                lse_ref[:, h:h + 1] = m_s[h][:, 0:1] + jnp.log(l)

    kv = pl.BlockSpec((tq, ATT_W), lambda i, j: (jnp.minimum(i, j), 0))
    return _hosted_call(
        host, body, name=name, grid=(n, n),
        in_specs=[pl.BlockSpec((tq, ATT_W), lambda i, j: (i, 0)), kv, kv],
        out_specs=[pl.BlockSpec((tq, FOX_W), lambda i, j: (i, 0)), pl.BlockSpec((tq, LANES), lambda i, j: (i, 0))],
        out_shape=[jax.ShapeDtypeStruct((t, FOX_W), f32), jax.ShapeDtypeStruct((t, LANES), f32)],
        scratch_shapes=[pltpu.VMEM((HEADS, tq, LANES), f32), pltpu.VMEM((HEADS, tq, LANES), f32)],
        compiler_params=_params(("arbitrary", "arbitrary")),
    )(q_aug, k_aug, v_aug)


def _attn_bwd(qb_aug, k_aug, v_aug, do_aug, *, name, host=None):
    t = qb_aug.shape[0]
    n = t // ATT_TILE
    tk = ATT_TILE

    def body(q_ref, k_ref, v_ref, do_ref, dq_ref, dcq_ref, dk_ref, dv_ref, dck_ref, dk_acc, dv_acc, dq_all):
        j = pl.program_id(0)
        i = pl.program_id(1)

        @pl.when(jnp.logical_and(i == 0, j == 0))
        def _():
            dq_all[...] = jnp.zeros_like(dq_all)

        @pl.when(i == 0)
        def _():
            dk_acc[...] = jnp.zeros_like(dk_acc)
            dv_acc[...] = jnp.zeros_like(dv_acc)

        def block(masked):
            mask = _causal(i, j, True) if masked else None
            for h in range(HEADS):
                hs = slice(LANES * h, LANES * (h + 1))
                qh = q_ref[:, hs]
                doh = do_ref[:, hs]
                kh = k_ref[:, hs]
                s_t = lax.dot_general(kh, qh, _NT, preferred_element_type=f32)
                if masked:
                    s_t = jnp.where(mask, s_t, NEG_BIG)
                p_t = jnp.exp(s_t)
                dv_acc[h] += jnp.dot(p_t.astype(bf16), doh, preferred_element_type=f32)
                dp_t = lax.dot_general(v_ref[:, hs], doh, _NT, preferred_element_type=f32)
                ds_t = (p_t * dp_t).astype(bf16)
                dk_acc[h] += jnp.dot(ds_t, qh, preferred_element_type=f32)
                dq_all[i, h] += lax.dot_general(ds_t, kh, _TN, preferred_element_type=f32)

        @pl.when(i > j)
        def _():
            block(False)

        @pl.when(i == j)
        def _():
            block(True)
            dcq_ref[...] = jnp.zeros_like(dcq_ref)
            for h in range(HEADS):
                a = dq_all[j, h]
                dq_ref[:, HEAD_D * h:HEAD_D * (h + 1)] = (
                    a[:, _data_lane(h):_data_lane(h) + HEAD_D] * ATT_SCALE).astype(bf16)
                dcq_ref[:, h:h + 1] = a[:, _extra_lane(h):_extra_lane(h) + 1]

        @pl.when(i == n - 1)
        def _():
            dck_ref[...] = jnp.zeros_like(dck_ref)
            for h in range(HEADS):
                a = dk_acc[h]
                cols = slice(_data_lane(h), _data_lane(h) + HEAD_D)
                dk_ref[:, HEAD_D * h:HEAD_D * (h + 1)] = a[:, cols].astype(bf16)
                dv_ref[:, HEAD_D * h:HEAD_D * (h + 1)] = dv_acc[h][:, cols].astype(bf16)
                dck_ref[:, h:h + 1] = -a[:, _extra_lane(h) + 3:_extra_lane(h) + 4]

    own = pl.BlockSpec((tk, ATT_W), lambda j, i: (j, 0))
    qs = pl.BlockSpec((tk, ATT_W), lambda j, i: (jnp.maximum(i, j), 0))
    half = pl.BlockSpec((tk, FOX_W), lambda j, i: (j, 0))
    col = pl.BlockSpec((tk, LANES), lambda j, i: (j, 0))
    return _hosted_call(
        host, body, name=name, grid=(n, n), in_specs=[qs, own, own, qs],
        out_specs=[half, col, half, half, col],
        out_shape=[jax.ShapeDtypeStruct((t, FOX_W), bf16), jax.ShapeDtypeStruct((t, LANES), f32),
                   jax.ShapeDtypeStruct((t, FOX_W), bf16), jax.ShapeDtypeStruct((t, FOX_W), bf16),
                   jax.ShapeDtypeStruct((t, LANES), f32)],
        scratch_shapes=[pltpu.VMEM((HEADS, tk, LANES), f32), pltpu.VMEM((HEADS, tk, LANES), f32),
                        pltpu.VMEM((n, HEADS, tk, LANES), f32)],
        compiler_params=_params(("arbitrary", "arbitrary")),
    )(qb_aug, k_aug, v_aug, do_aug)


LRU_CHUNK = 64
LRU_G = 256
SUB = 8


def _row_ids(n):
    return lax.broadcasted_iota(jnp.int32, (n, LRU_G), 0)


def _shift_rows_down(ext, s):
    return pltpu.roll(ext, s, axis=0)[SUB:, :]


def _shift_rows_up(ext, s, n):
    return pltpu.roll(ext, ext.shape[0] - s, axis=0)[:n, :]


def _lru_gates(u, wa_ref, ba_ref, wx_ref, bx_ref, sp):
    ub = u.astype(bf16)
    r = _sigmoid(jnp.dot(ub, wa_ref[...], preferred_element_type=f32) + ba_ref[...])
    gi = _sigmoid(jnp.dot(ub, wx_ref[...], preferred_element_type=f32) + bx_ref[...])
    log_a = -LRU_C * r * sp
    a = jnp.exp(log_a)
    s = jnp.sqrt(_one_minus_exp(2.0 * log_a))
    return r, gi, a, s


def _conv_window(lx_ref, r0, ci):
    cur = lx_ref[pl.ds(r0, LRU_CHUNK), :]
    p0 = pl.multiple_of(jnp.maximum(r0 - SUB, 0), SUB)
    prev = jnp.where(ci > 0, lx_ref[pl.ds(p0, SUB), :], 0.0)
    return cur, jnp.concatenate([prev, cur], axis=0)


def _lru_fwd(zl, conv_w, conv_b, wa, ba, wx, bx, lam, *, name, host=None):
    t = zl.shape[0]
    n_chunk = t // LRU_CHUNK

    def body(lx_ref, lg_ref, cw_ref, cb_ref, wa_ref, ba_ref, wx_ref, bx_ref, lam_ref, u_ref, h_ref, y_ref):
        sp = _softplus(-lam_ref[...])
        rows = _row_ids(SUB)

        def chunk(ci, hc):
            r0 = pl.multiple_of(ci * LRU_CHUNK, LRU_CHUNK)
            cur, ext = _conv_window(lx_ref, r0, ci)
            u = cb_ref[...] + cw_ref[3:4, :] * cur
            for k in range(3):
                u = u + cw_ref[k:k + 1, :] * _shift_rows_down(ext, 3 - k)
            r, gi, a, s = _lru_gates(u, wa_ref, ba_ref, wx_ref, bx_ref, sp)
            b = s * (gi * u)
            tiles = []
            for q in range(LRU_CHUNK // SUB):
                ta = a[SUB * q:SUB * (q + 1), :]
                tb = b[SUB * q:SUB * (q + 1), :]
                for d in (1, 2, 4):
                    a_sh = jnp.where(rows >= d, pltpu.roll(ta, d, axis=0), 1.0)
                    b_sh = jnp.where(rows >= d, pltpu.roll(tb, d, axis=0), 0.0)
                    tb = ta * b_sh + tb
                    ta = ta * a_sh
                hq = tb + ta * hc
                hc = hq[SUB - 1:SUB, :]
                tiles.append(hq)
            h = jnp.concatenate(tiles, axis=0)
            u_ref[pl.ds(r0, LRU_CHUNK), :] = u
            h_ref[pl.ds(r0, LRU_CHUNK), :] = h
            gel, _ = _gelu_and_grad(lg_ref[pl.ds(r0, LRU_CHUNK), :])
            y_ref[pl.ds(r0, LRU_CHUNK), :] = gel * h
            return hc

        lax.fori_loop(0, n_chunk, chunk, jnp.zeros((1, LRU_G), f32))

    seq = lambda cb: pl.BlockSpec((t, LRU_G), lambda c, cb=cb: (0, c + cb))
    rowc = pl.BlockSpec((1, LRU_G), lambda c: (0, c))
    diag = pl.BlockSpec((LRU_G, LRU_G), lambda c: (c, c))
    out = jax.ShapeDtypeStruct((t, LRU_W), f32)
    return _hosted_call(
        host, body, name=name, grid=(LRU_W // LRU_G,),
        in_specs=[seq(0), seq(LRU_W // LRU_G), pl.BlockSpec((4, LRU_G), lambda c: (0, c)),
                  rowc, diag, rowc, diag, rowc, rowc],
        out_specs=[seq(0)] * 3, out_shape=[out] * 3,
        compiler_params=_params(("arbitrary",)),
    )(zl, zl, conv_w, conv_b, wa, ba, wx, bx, lam)


def _lru_bwd(dmix, zl, u_all, h_all, conv_w, wa, ba, wx, bx, lam, *, name, host=None):
    t = zl.shape[0]
    n_chunk = t // LRU_CHUNK

    def body(dy_ref, lx_ref, lg_ref, u_ref, h_ref, cw_ref, wa_ref, ba_ref, wx_ref, bx_ref, lam_ref,
             dlx_ref, dlg_ref, dcw_ref, dcb_ref, dba_ref, dbx_ref, dlam_ref, dwa_ref, dwx_ref, dpr_s, dpx_s):
        lam_v = lam_ref[...]
        sp = _softplus(-lam_v)
        rows = _row_ids(SUB)
        rows_c = _row_ids(LRU_CHUNK)
        zero_row = jnp.zeros((1, LRU_G), f32)

        def chunk(step, carry):
            dh_c, a_next0, du_next, dsp, dba, dbx, dcb, dw0, dw1, dw2, dw3 = carry
            ci = n_chunk - 1 - step
            r0 = pl.multiple_of(ci * LRU_CHUNK, LRU_CHUNK)
            sl = pl.ds(r0, LRU_CHUNK)
            u = u_ref[sl, :]
            r, gi, a, s = _lru_gates(u, wa_ref, ba_ref, wx_ref, bx_ref, sp)
            h = h_ref[sl, :]
            p0 = pl.multiple_of(jnp.maximum(r0 - SUB, 0), SUB)
            h_before = jnp.where(ci > 0, h_ref[pl.ds(p0, SUB), :], 0.0)[SUB - 1:SUB, :]
            h_prev = jnp.where(rows_c == 0, h_before, pltpu.roll(h, 1, axis=0))
            gel, dgel = _gelu_and_grad(lg_ref[sl, :])
            dy = dy_ref[sl, :]
            dlg_ref[sl, :] = (dy * h * dgel).astype(bf16)
            g_in = dy * gel
            a_next = jnp.where(rows_c == LRU_CHUNK - 1, a_next0, pltpu.roll(a, LRU_CHUNK - 1, axis=0))
            tiles = [None] * (LRU_CHUNK // SUB)
            for q in reversed(range(LRU_CHUNK // SUB)):
                ta = a_next[SUB * q:SUB * (q + 1), :]
                tb = g_in[SUB * q:SUB * (q + 1), :]
                for d in (1, 2, 4):
                    a_sh = jnp.where(rows < SUB - d, pltpu.roll(ta, SUB - d, axis=0), 1.0)
                    b_sh = jnp.where(rows < SUB - d, pltpu.roll(tb, SUB - d, axis=0), 0.0)
                    tb = ta * b_sh + tb
                    ta = ta * a_sh
                dhq = tb + ta * dh_c
                dh_c = dhq[0:1, :]
                tiles[q] = dhq
            dh = jnp.concatenate(tiles, axis=0)
            da = dh * h_prev
            ds = dh * gi * u
            dgi = dh * s * u
            du = dh * s * gi
            dlog_a = da * a - ds * (a * a) / s
            dr = dlog_a * (-LRU_C * sp)
            dsp = dsp + jnp.sum(dlog_a * (-LRU_C * r), axis=0, keepdims=True)
            dpr = dr * r * (1.0 - r)
            dpx = dgi * gi * (1.0 - gi)
            dprb = dpr.astype(bf16)
            dpxb = dpx.astype(bf16)
            dpr_s[sl, :] = dprb
            dpx_s[sl, :] = dpxb
            du = du + (lax.dot_general(dprb, wa_ref[...], _NT, preferred_element_type=f32)
                       + lax.dot_general(dpxb, wx_ref[...], _NT, preferred_element_type=f32))
            dba = dba + jnp.sum(dpr, axis=0, keepdims=True)
            dbx = dbx + jnp.sum(dpx, axis=0, keepdims=True)
            dcb = dcb + jnp.sum(du, axis=0, keepdims=True)
            du_ext = jnp.concatenate([du, du_next], axis=0)
            dlx = cw_ref[3:4, :] * du
            for k in range(3):
                dlx = dlx + cw_ref[k:k + 1, :] * _shift_rows_up(du_ext, 3 - k, LRU_CHUNK)
            dlx_ref[sl, :] = dlx.astype(bf16)
            cur, ext = _conv_window(lx_ref, r0, ci)
            dws = [dw0, dw1, dw2, dw3 + jnp.sum(du * cur, axis=0, keepdims=True)]
            for k in range(3):
                dws[k] = dws[k] + jnp.sum(du * _shift_rows_down(ext, 3 - k), axis=0, keepdims=True)
            return (dh_c, a[0:1, :], du[0:SUB, :], dsp, dba, dbx, dcb, dws[0], dws[1], dws[2], dws[3])

        init = (zero_row, zero_row, jnp.zeros((SUB, LRU_G), f32)) + (zero_row,) * 8
        out = lax.fori_loop(0, n_chunk, chunk, init)
        _, _, _, dsp, dba, dbx, dcb, dw0, dw1, dw2, dw3 = out
        dlam_ref[...] = dsp * (-_sigmoid(-lam_v))
        dba_ref[...] = dba
        dbx_ref[...] = dbx
        dcb_ref[...] = dcb
        dcw_ref[...] = jnp.concatenate([dw0, dw1, dw2, dw3], axis=0)
        ub = u_ref[...].astype(bf16)
        dwa_ref[...] = lax.dot_general(ub, dpr_s[...], _TN, preferred_element_type=f32)
        dwx_ref[...] = lax.dot_general(ub, dpx_s[...], _TN, preferred_element_type=f32)

    seq = lambda cb: pl.BlockSpec((t, LRU_G), lambda c, cb=cb: (0, c + cb))
    rowc = pl.BlockSpec((1, LRU_G), lambda c: (0, c))
    diag = pl.BlockSpec((LRU_G, LRU_G), lambda c: (c, c))
    gate_out = pl.BlockSpec((None, LRU_G, LRU_G), lambda c: (c, 0, 0))
    row_shape = jax.ShapeDtypeStruct((1, LRU_W), f32)
    return _hosted_call(
        host, body, name=name, grid=(LRU_W // LRU_G,),
        in_specs=[seq(LRU_W // LRU_G), seq(0), seq(LRU_W // LRU_G), seq(0), seq(0),
                  pl.BlockSpec((4, LRU_G), lambda c: (0, c)),
                  diag, rowc, diag, rowc, rowc],
        out_specs=[seq(0), seq(0), pl.BlockSpec((4, LRU_G), lambda c: (0, c)), rowc, rowc, rowc, rowc,
                   gate_out, gate_out],
        out_shape=[jax.ShapeDtypeStruct((t, LRU_W), bf16)] * 2
        + [jax.ShapeDtypeStruct((4, LRU_W), f32)] + [row_shape] * 4
        + [jax.ShapeDtypeStruct((LRU_W // LRU_G, LRU_G, LRU_G), f32)] * 2,
        scratch_shapes=[pltpu.VMEM((t, LRU_G), bf16), pltpu.VMEM((t, LRU_G), bf16)],
        compiler_params=_params(("arbitrary",)),
    )(dmix, zl, zl, u_all, h_all, conv_w, wa, ba, wx, bx, lam)


def _pack_rows(a):
    flat = a.reshape(-1)
    rows = -(-flat.shape[0] // LANES)
    return jnp.pad(flat, (0, rows * LANES - flat.shape[0])).reshape(rows, LANES)


W_IN_PAD = 21 * LANES


def _w_in_join(blocks, *, name):
    tm = 256

    def body(b_ref, o_ref):
        o_ref[:, IN_COLS:] = jnp.zeros((tm, W_IN_PAD - IN_COLS), bf16)
        for q in range(N_DEV):
            o_ref[:, IN_SHARD * q:IN_SHARD * (q + 1)] = b_ref[q]

    return pl.pallas_call(
        body, name=name, grid=(D_MODEL // tm,),
        in_specs=[pl.BlockSpec((N_DEV, tm, IN_SHARD), lambda i: (0, i, 0))],
        out_specs=pl.BlockSpec((tm, W_IN_PAD), lambda i: (i, 0)),
        out_shape=jax.ShapeDtypeStruct((D_MODEL, W_IN_PAD), bf16), compiler_params=_params(("arbitrary",)),
    )(blocks)


def _w_in_split(main, fg, *, name):
    tm = 256
    n_main = main.shape[0]

    def body(m_ref, f_ref, o_ref):
        full = jnp.concatenate([m_ref[n] for n in range(n_main)] + [f_ref[...]], axis=1)
        for q in range(N_DEV):
            o_ref[q] = full[:, IN_SHARD * q:IN_SHARD * (q + 1)]

    return pl.pallas_call(
        body, name=name, grid=(D_MODEL // tm,),
        in_specs=[pl.BlockSpec((n_main, tm, 512), lambda i: (0, i, 0)), pl.BlockSpec((tm, LANES), lambda i: (i, 0))],
        out_specs=pl.BlockSpec((N_DEV, tm, IN_SHARD), lambda i: (0, i, 0)),
        out_shape=jax.ShapeDtypeStruct((N_DEV, D_MODEL, IN_SHARD), bf16), compiler_params=_params(("arbitrary",)),
    )(main, fg)


def _block_diag(w):
    eye = jnp.eye(HEADS, dtype=w.dtype)
    return jnp.einsum("hij,hk->hikj", w, eye).reshape(LRU_W, LRU_W)


def _diag_blocks(dw):
    per = dw.shape[1] // HEAD_D
    blocks = [dw[:, HEAD_D * b:HEAD_D * (b + 1), HEAD_D * b:HEAD_D * (b + 1)] for b in range(per)]
    return jnp.stack(blocks, axis=1).reshape(HEADS, HEAD_D, HEAD_D)


def _local_step(x, target, sent, small, *, tm=512, tm_ffn=1024):
    ln1 = (small["ln1_g"], small["ln1_b"])
    ln2 = (small["ln2_g"], small["ln2_b"])
    ln3 = (small["ln3_g"], small["ln3_b"])

    xh1, rs1, hg1, hu1, wg1, wu1, wd1, w_in_g, w_out_g, conv_w_g = _ffn1_fwd_gathering(
        x, (sent["ffn1_w_gate"], sent["ffn1_w_up"], sent["ffn1_w_down"]),
        _Exchange([sent["w_in"], sent["w_out"], sent["conv_w"]], gather=True), tm=tm_ffn, name="ffn1_fwd")
    w_in = _w_in_join(w_in_g, name="w_in_join")
    w_out = w_out_g.reshape(D_MODEL, D_MODEL)
    conv_w = conv_w_g.transpose(1, 0, 2).reshape(4, LRU_W)
    qkv, zl, zfg = _in_proj(xh1, ln1[0], ln1[1], w_in, tm=tm, name="in_proj")
    bfg = jnp.pad(small["b_forget"], ((0, 0), (0, LANES - HEADS)))
    cum = _cum_fwd(zfg, bfg, name="cum_fwd")
    q_aug, k_aug, v_aug = _attn_prep_fwd(qkv, cum, tm=tm, name="attn_prep_fwd")
    o, lse, wg2, wu2 = _attn_fwd2(q_aug, k_aug, v_aug, name="attn_fwd",
                                  host=_Exchange([sent["ffn2_w_gate"], sent["ffn2_w_up"]], gather=True))
    wa_bd = _block_diag(small["rg_wa"]).astype(bf16)
    wx_bd = _block_diag(small["rg_wx"]).astype(bf16)
    ba = small["rg_ba"].reshape(1, LRU_W)
    bx = small["rg_bx"].reshape(1, LRU_W)
    u, h, lru, wd2 = _lru_fwd(zl, conv_w, small["conv_b"], wa_bd, ba, wx_bd, bx, small["lru_lambda"],
                              name="lru_fwd", host=_Exchange([sent["ffn2_w_down"]], gather=True))
    xh2, rs2 = _mmln([(o, 0, FOX_W, w_out, 0, D_MODEL, "nn"), (lru, 0, LRU_W, w_out, 1, D_MODEL, "nn")],
                     tm=tm, name="mix_fwd", resid=("affine", xh1) + ln1, resid_scale=ALPHA, epi="ln_fwd")
    xh3, rs3, hg2, hu2 = _ffn_fwd(xh2, ln2[0], ln2[1], wg2, wu2, wd2, tm=tm_ffn, name="ffn2_fwd")

    dpre3, sq_rows, g_ln3g, g_ln3b = _loss_bwd(xh3, rs3, ln3[0], ln3[1], target, tm=tm, name="loss_bwd")
    dpre2, g_ln2g, g_ln2b, dhg2, dhu2, a2 = _ffn_bwd(dpre3, hg2, hu2, wg2, wu2, wd2,
                                                     (xh2, rs2, ln2[0]), tm=tm_ffn, name="ffn2_bwd")
    wgrad = dict(out_dtype=bf16, tm=D_MODEL, mb=1, tn=FF_TILE, nb=4, tk=512, pair=True)
    wdgrad = dict(out_dtype=bf16, tm=512, mb=4, tn=D_MODEL, nb=1, tk=512, out_scale=0.5, pair=True)
    between_chips = functools.partial(_Exchange, gather=False, chips=True)
    g_wg2 = _mm_tn(xh2, dhg2, name="g_wg2", affine=ln2, **wgrad)
    g_wu2 = _mm_tn(xh2, dhu2, name="g_wu2", affine=ln2, **wgrad)
    g_wd2 = _mm_tn(a2, dpre3, name="g_wd2", **wdgrad)

    dmix = _mmln([(dpre2, 0, D_MODEL, w_out, 0, D_MODEL, "nt")], tm=tm, name="dmix_bwd")
    g_wout_a = _mm(o, dpre2, mode="tn", out_dtype=bf16, tm=512, tn=D_MODEL, tk=512, name="g_wout_fox")
    g_wout_b = _mm(lru, dpre2, mode="tn", out_dtype=bf16, tm=512, tn=D_MODEL, tk=512, name="g_wout_lru")
    g_wout_blocked = jnp.concatenate([g_wout_a, g_wout_b], axis=0).reshape(N_DEV, D_MODEL // N_DEV, D_MODEL)
    dlx, dlg, g_cw, g_cb, g_ba, g_bx, g_lam, g_wa4, g_wx4, p_wg2, p_wout = _lru_bwd(
        dmix, zl, u, h, conv_w, wa_bd, ba, wx_bd, bx, small["lru_lambda"], name="lru_bwd",
        host=_Hosts(between_chips([g_wg2]), _Exchange([g_wout_blocked], gather=False)))
    qb_aug, do_aug = _attn_prep_bwd(qkv, cum, lse, dmix, o, tm=tm, name="attn_prep_bwd")
    dq, dcum_q, dk, dv, dcum_k, p_wu2, p_wd2 = _attn_bwd(qb_aug, k_aug, v_aug, do_aug, name="attn_bwd",
                                                         host=between_chips([g_wu2, g_wd2]))
    dfg, g_bf = _cum_bwd(dcum_q, dcum_k, zfg, bfg, name="cum_bwd")

    dz = [(dq, 0, 512), (dk, 1, 512), (dv, 2, 512), (dlx, 3, 512), (dlg, 4, 512), (dfg, 20, LANES)]
    dpre1, g_ln1g, g_ln1b = _mmln(
        [(arr, 0, w, w_in, cb, w, "nt") for (arr, cb, w) in dz],
        tm=tm, name="dx1_bwd", resid=("plain", dpre2), resid_scale=ALPHA, epi="ln_bwd", ln=(xh1, rs1, ln1[0]))
    g_win_main = _mm_tn(xh1, [arr for arr, _, _ in dz[:5]], out_dtype=bf16, tm=D_MODEL, mb=1, tn=512, nb=5, tk=512,
                        name="g_win", affine=ln1, out_blocked=True)
    g_win_fg = _mm(xh1, dfg, mode="tn", out_dtype=bf16, tm=D_MODEL, tn=LANES, tk=512, name="g_win_fg", affine=ln1)
    g_win_blocked = _w_in_split(g_win_main, g_win_fg, name="w_in_split")
    dhg1, dhu1, a1, p_win = _ffn_bwd_act(dpre1, hg1, hu1, wd1, tm=tm_ffn, name="ffn1_bwd_act",
                                         host=_Exchange([g_win_blocked], gather=False))
    small_g = {
        "ln1_g": g_ln1g, "ln1_b": g_ln1b, "b_forget": g_bf[:, :HEADS], "conv_w": g_cw, "conv_b": g_cb,
        "rg_wa": _diag_blocks(g_wa4), "rg_ba": g_ba.reshape(HEADS, HEAD_D),
        "rg_wx": _diag_blocks(g_wx4), "rg_bx": g_bx.reshape(HEADS, HEAD_D), "lru_lambda": g_lam,
        "ln2_g": g_ln2g, "ln2_b": g_ln2b, "ln3_g": g_ln3g, "ln3_b": g_ln3b,
    }
    small_g["loss"] = (0.5 / D_MODEL) * jnp.sum(sq_rows, keepdims=True)
    pieces = [_pack_rows(small_g[n]) for n in PACKED]
    packed = jnp.concatenate(pieces + [jnp.zeros((PACK_ROWS - sum(p.shape[0] for p in pieces), LANES), f32)])
    g_wg1, all_packed = _mm_tn(x, dhg1, name="g_wg1", host=_Exchange([packed], gather=True), **wgrad)
    g_wu1, p_wg1 = _mm_tn(x, dhu1, name="g_wu1", host=between_chips([g_wg1]), **wgrad)
    g_wd1, p_wu1 = _mm_tn(a1, dpre1, name="g_wd1", host=between_chips([g_wu1]), **wdgrad)
    grad_x, p_wd1 = _ffn_bwd_dx(dpre1, dhg1, dhu1, wg1, wu1, tm=tm_ffn, name="ffn1_bwd_dx",
                                host=between_chips([g_wd1]))
    parts = {
        "ffn1_w_gate": p_wg1, "ffn1_w_up": p_wu1, "ffn1_w_down": p_wd1, "w_in": p_win, "w_out": p_wout,
        "ffn2_w_gate": p_wg2, "ffn2_w_up": p_wu2, "ffn2_w_down": p_wd2,
    }
    return grad_x, parts, all_packed, {n: small_g[n].shape for n in PACKED}


def _adam_math(w, g, m, v):
    m2 = ADAM_B1 * m + (1.0 - ADAM_B1) * g
    v2 = ADAM_B2 * v + (1.0 - ADAM_B2) * (g * g)
    m_hat = m2 / (1.0 - ADAM_B1 ** ADAM_STEP)
    v_hat = v2 / (1.0 - ADAM_B2 ** ADAM_STEP)
    delta = -ADAM_LR * (m_hat / (jnp.sqrt(v_hat) + ADAM_EPS) + ADAM_WD * w)
    return delta, m2, v2


ADAM_TILE_ELEMS = 128 * 1024


def _adamw_big(parts, w, m, v, *, name):
    r, c = w.shape
    n_parts = parts.shape[0]
    tr = max(d for d in range(8, r + 1, 8) if r % d == 0 and d * c <= ADAM_TILE_ELEMS)

    def body(p_ref, w_ref, m_ref, v_ref, g_ref, d_ref, m2_ref, v2_ref):
        g = p_ref[0].astype(f32)
        for q in range(1, n_parts):
            g = g + p_ref[q].astype(f32)
        d, m2, v2 = _adam_math(w_ref[...], g, m_ref[...], v_ref[...])
        g_ref[...] = g
        d_ref[...] = d
        m2_ref[...] = m2
        v2_ref[...] = v2

    blk = pl.BlockSpec((tr, c), lambda i: (i, 0))
    return pl.pallas_call(
        body, name=name, grid=(r // tr,),
        in_specs=[pl.BlockSpec((n_parts, tr, c), lambda i: (0, i, 0)), blk, blk, blk],
        out_specs=[blk] * 4, out_shape=[jax.ShapeDtypeStruct((r, c), f32)] * 4,
        compiler_params=_params(("arbitrary",)),
    )(parts, w, m, v)


def _adamw_small(items, *, name):
    n = len(items)

    def body(*refs):
        ins, outs = refs[:4 * n], refs[4 * n:]
        for k in range(n):
            g, w, m, v = (ins[4 * k + q][...] for q in range(4))
            d, m2, v2 = _adam_math(w, g, m, v)
            outs[3 * k][...] = d
            outs[3 * k + 1][...] = m2
            outs[3 * k + 2][...] = v2

    vm = pl.BlockSpec(memory_space=pltpu.VMEM)
    flat = [a for item in items for a in item]
    out_shape = [jax.ShapeDtypeStruct(item[1].shape, f32) for item in items for _ in range(3)]
    return pl.pallas_call(
        body, name=name, in_specs=[vm] * (4 * n), out_specs=[vm] * (3 * n), out_shape=out_shape,
    )(*flat)


def _sum_parts(parts, *, name):
    def body(p_ref, o_ref):
        acc = p_ref[0]
        for q in range(1, N_DEV):
            acc = acc + p_ref[q]
        o_ref[...] = acc

    vm = pl.BlockSpec(memory_space=pltpu.VMEM)
    return pl.pallas_call(
        body, name=name, in_specs=[vm], out_specs=vm, out_shape=jax.ShapeDtypeStruct(parts.shape[1:], f32),
    )(parts)


WEIGHTS = ["ffn1_w_gate", "ffn1_w_up", "ffn1_w_down", "ln1_g", "ln1_b", "w_in", "b_forget", "conv_w", "conv_b",
           "rg_wa", "rg_ba", "rg_wx", "rg_bx", "lru_lambda", "w_out", "ln2_g", "ln2_b",
           "ffn2_w_gate", "ffn2_w_up", "ffn2_w_down", "ln3_g", "ln3_b"]
BIG = ["ffn1_w_gate", "ffn1_w_up", "ffn1_w_down", "w_in", "w_out", "ffn2_w_gate", "ffn2_w_up", "ffn2_w_down"]
PACKED = ["ln1_g", "ln1_b", "ln2_g", "ln2_b", "ln3_g", "ln3_b", "conv_b", "rg_ba", "rg_bx", "lru_lambda",
          "conv_w", "rg_wa", "rg_wx", "b_forget", "loss"]
PACK_ROWS = 600


def _two_d(a):
    return a.reshape((-1, a.shape[-1]))


def _transport(a):
    return _two_d(a)


def kernel(x, ffn1_w_gate, ffn1_w_up, ffn1_w_down, ln1_g, ln1_b, w_in, b_forget, conv_w, conv_b, rg_wa, rg_ba, rg_wx, rg_bx, lru_lambda, w_out, ln2_g, ln2_b, ffn2_w_gate, ffn2_w_up, ffn2_w_down, ln3_g, ln3_b, loss_target, m_ffn1_w_gate, m_ffn1_w_up, m_ffn1_w_down, m_ln1_g, m_ln1_b, m_w_in, m_b_forget, m_conv_w, m_conv_b, m_rg_wa, m_rg_ba, m_rg_wx, m_rg_bx, m_lru_lambda, m_w_out, m_ln2_g, m_ln2_b, m_ffn2_w_gate, m_ffn2_w_up, m_ffn2_w_down, m_ln3_g, m_ln3_b, v_ffn1_w_gate, v_ffn1_w_up, v_ffn1_w_down, v_ln1_g, v_ln1_b, v_w_in, v_b_forget, v_conv_w, v_conv_b, v_rg_wa, v_rg_ba, v_rg_wx, v_rg_bx, v_lru_lambda, v_w_out, v_ln2_g, v_ln2_b, v_ffn2_w_gate, v_ffn2_w_up, v_ffn2_w_down, v_ln3_g, v_ln3_b):
    w_args = (ffn1_w_gate, ffn1_w_up, ffn1_w_down, ln1_g, ln1_b, w_in, b_forget, conv_w, conv_b, rg_wa, rg_ba, rg_wx, rg_bx, lru_lambda, w_out, ln2_g, ln2_b, ffn2_w_gate, ffn2_w_up, ffn2_w_down, ln3_g, ln3_b)
    m_args = (m_ffn1_w_gate, m_ffn1_w_up, m_ffn1_w_down, m_ln1_g, m_ln1_b, m_w_in, m_b_forget, m_conv_w, m_conv_b, m_rg_wa, m_rg_ba, m_rg_wx, m_rg_bx, m_lru_lambda, m_w_out, m_ln2_g, m_ln2_b, m_ffn2_w_gate, m_ffn2_w_up, m_ffn2_w_down, m_ln3_g, m_ln3_b)
    v_args = (v_ffn1_w_gate, v_ffn1_w_up, v_ffn1_w_down, v_ln1_g, v_ln1_b, v_w_in, v_b_forget, v_conv_w, v_conv_b, v_rg_wa, v_rg_ba, v_rg_wx, v_rg_bx, v_lru_lambda, v_w_out, v_ln2_g, v_ln2_b, v_ffn2_w_gate, v_ffn2_w_up, v_ffn2_w_down, v_ln3_g, v_ln3_b)
    w = dict(zip(WEIGHTS, w_args))
    m = dict(zip(WEIGHTS, m_args))
    v = dict(zip(WEIGHTS, v_args))
    me = 4 * lax.axis_index("x") + 2 * lax.axis_index("y") + lax.axis_index("c")

    sent = {n: _transport(w[n]).astype(bf16) for n in BIG}
    sent["conv_w"] = _two_d(w["conv_w"])
    small = {n: w[n] for n in ("ln1_g", "ln1_b", "ln2_g", "ln2_b", "ln3_g", "ln3_b", "b_forget", "conv_b",
                               "lru_lambda")}
    small.update({n: w[n][0] for n in ("rg_wa", "rg_ba", "rg_wx", "rg_bx")})

    grad_x, parts, all_packed, small_shapes = _local_step(x[0], loss_target[0], sent, small)

    total = _sum_parts(all_packed, name="sum_small_grads")
    grads, off = {}, 0
    for n in PACKED:
        size = math.prod(small_shapes[n])
        rows = -(-size // LANES)
        grads[n] = total[off:off + rows].reshape(-1)[:size].reshape(small_shapes[n])
        off += rows
    loss = grads.pop("loss").reshape(())
    grads["conv_w"] = lax.dynamic_slice_in_dim(grads["conv_w"], me * (LRU_W // N_DEV), LRU_W // N_DEV, axis=1)

    delta, new_m, new_v = {}, {}, {}
    for n in BIG:
        g, d, m2, v2 = _adamw_big(parts[n], _transport(w[n]), _transport(m[n]), _transport(v[n]),
                                  name="adamw_" + n)
        grads[n], delta[n], new_m[n], new_v[n] = g, d, m2, v2
    small_names = [n for n in WEIGHTS if n not in BIG]
    outs = _adamw_small([(_two_d(grads[n]), _two_d(w[n]), _two_d(m[n]), _two_d(v[n])) for n in small_names],
                        name="adamw_small")
    for k, n in enumerate(small_names):
        delta[n], new_m[n], new_v[n] = outs[3 * k], outs[3 * k + 1], outs[3 * k + 2]

    def shaped(d):
        return [d[n].reshape(w[n].shape) for n in WEIGHTS]

    return (loss, grad_x[None], *shaped(grads), *shaped(delta), *shaped(new_m), *shaped(new_v))
```

```python
import functools
import math

import jax
import jax.numpy as jnp
from jax import lax
from jax.experimental import pallas as pl
from jax.experimental.pallas import tpu as pltpu

f32 = jnp.float32
bf16 = jnp.bfloat16

N_DEV = 8
D_MODEL = 1024
D_FF = 4096
FF_TILE = D_FF // N_DEV
FOX_W = 512
LRU_W = 512
HEADS = 8
HEAD_D = 64
IN_COLS = 2568
IN_SHARD = IN_COLS // N_DEV
LANES = 128
LN_EPS = 1e-5
ALPHA = 2.0 ** 0.25
ATT_SCALE = 1.0 / math.sqrt(HEAD_D)
LRU_C = 8.0
NEG_BIG = -1e30

ADAM_LR = 0.001
ADAM_B1 = 0.9
ADAM_B2 = 0.999
ADAM_EPS = 1e-08
ADAM_WD = 0.01
ADAM_STEP = 10

VMEM_LIMIT = 56 * 1024 * 1024
MESH_T = pl.DeviceIdType.MESH


def _params(sem, **kw):
    return pltpu.CompilerParams(dimension_semantics=sem, vmem_limit_bytes=VMEM_LIMIT, **kw)


def _sigmoid(x):
    return 1.0 / (1.0 + jnp.exp(-x))


def _sigmoid_tanh(x):
    return 0.5 * jnp.tanh(0.5 * x) + 0.5


def _softplus(x):
    return jnp.maximum(x, 0.0) + jnp.log(1.0 + jnp.exp(-jnp.abs(x)))


def _one_minus_exp(x):
    series = -x * (1.0 + x * (0.5 + x * (1.0 / 6 + x * (1.0 / 24 + x * (1.0 / 120 + x * (1.0 / 720))))))
    return jnp.where(x > -0.125, series, 1.0 - jnp.exp(x))


_GELU_C = math.sqrt(2.0 / math.pi)


def _gelu_and_grad(x):
    inner = _GELU_C * (x + 0.044715 * x * x * x)
    t = jnp.tanh(inner)
    g = 0.5 * x * (1.0 + t)
    dg = 0.5 * (1.0 + t) + 0.5 * x * (1.0 - t * t) * _GELU_C * (1.0 + 3 * 0.044715 * x * x)
    return g, dg


def _ln_fwd_tile(pre):
    mu = jnp.mean(pre, axis=-1, keepdims=True)
    xc = pre - mu
    var = jnp.mean(xc * xc, axis=-1, keepdims=True)
    rstd = lax.rsqrt(var + LN_EPS)
    return xc * rstd, rstd


def _ln_bwd_tile(dy, xhat, rstd, g):
    dyg = dy * g
    m1 = jnp.mean(dyg, axis=-1, keepdims=True)
    m2 = jnp.mean(dyg * xhat, axis=-1, keepdims=True)
    dpre = rstd * (dyg - m1 - xhat * m2)
    return dpre, jnp.sum(dy * xhat, axis=0, keepdims=True), jnp.sum(dy, axis=0, keepdims=True)


_NT = (((1,), (1,)), ((), ()))
_TN = (((0,), (0,)), ((), ()))


class _Exchange:
    def __init__(self, arrs, gather, chips=False):
        self.arrs, self.gather, self.n, self.chips = list(arrs), gather, len(arrs), chips

    def out_shape(self):
        return [jax.ShapeDtypeStruct(((N_DEV,) + a.shape) if self.gather else a.shape, a.dtype) for a in self.arrs]

    def scratch(self):
        n_remote = self.n * (N_DEV - 1)
        return [pltpu.SemaphoreType.DMA((n_remote,)), pltpu.SemaphoreType.DMA((n_remote,)),
                pltpu.SemaphoreType.DMA((self.n,))]

    def copies(self, ins, outs, sems):
        send_sems, recv_sems, local_sems = sems
        x, y, c = lax.axis_index("x"), lax.axis_index("y"), lax.axis_index("c")
        me = 2 * x + y if self.chips else 4 * x + 2 * y + c
        out = []
        for k in range(self.n):
            for d in (range(2, N_DEV, 2) if self.chips else range(1, N_DEV)):
                px = 1 - x if d & 4 else x
                py = 1 - y if d & 2 else y
                pc = 1 - c if d & 1 else c
                sem = k * (N_DEV - 1) + d - 1
                out.append(pltpu.make_async_remote_copy(
                    src_ref=ins[k].at[2 * px + py if self.chips else 4 * px + 2 * py + pc], dst_ref=outs[k].at[me],
                    send_sem=send_sems.at[sem], recv_sem=recv_sems.at[sem],
                    device_id=(px, py, pc), device_id_type=MESH_T))
            out.append(pltpu.make_async_copy(ins[k].at[me], outs[k].at[me], local_sems.at[k]))
        return out

    def gather_copies(self, ins, outs, sems):
        send_sems, recv_sems, local_sems = sems
        x, y, c = lax.axis_index("x"), lax.axis_index("y"), lax.axis_index("c")
        sibling = (x, y, 1 - c)
        chips = [(1 - x, y), (x, 1 - y), (1 - x, 1 - y)]
        out = []
        for k in range(self.n):
            def copy(s, block, to, src=None, k=k):
                rows = outs[k].at[4 * block[0] + 2 * block[1] + block[2]]
                sem = k * (N_DEV - 1) + s
                return pltpu.make_async_remote_copy(
                    src_ref=rows if src is None else src, dst_ref=rows, send_sem=send_sems.at[sem],
                    recv_sem=recv_sems.at[sem], device_id=to, device_id_type=MESH_T)

            first = [copy(0, (x, y, c), sibling, src=ins[k])]
            first += [copy(1 + q, (x, y, c), (*chip, c), src=ins[k]) for q, chip in enumerate(chips)]
            passed = [copy(4 + q, (*chip, c), sibling) for q, chip in enumerate(chips)]
            own = pltpu.make_async_copy(ins[k], outs[k].at[4 * x + 2 * y + c], local_sems.at[k])
            out.append((first, passed, own, copy))
        return out, sibling, chips, (x, y, c)

    def start(self, ins, outs, sems):
        if not self.gather:
            for cp in self.copies(ins, outs, sems):
                cp.start()
            return
        per_array, _, _, _ = self.gather_copies(ins, outs, sems)
        for first, _, own, _ in per_array:
            own.start()
            for cp in first:
                cp.start()

    def relay(self, ins, outs, sems):
        per_array, sibling, chips, (x, y, c) = self.gather_copies(ins, outs, sems)
        for first, passed, own, copy in per_array:
            for q, chip in enumerate(chips):
                copy(1 + q, (*chip, c), (x, y, c)).wait_recv()
                passed[q].start()

    def wait(self, ins, outs, sems, relayed=False):
        if not self.gather:
            for cp in self.copies(ins, outs, sems):
                cp.wait()
            return
        if not relayed:
            self.relay(ins, outs, sems)
        per_array, sibling, chips, (x, y, c) = self.gather_copies(ins, outs, sems)
        for first, passed, own, copy in per_array:
            copy(0, sibling, (x, y, c)).wait_recv()
            for q, chip in enumerate(chips):
                copy(4 + q, (*chip, 1 - c), (x, y, c)).wait_recv()
            for cp in first + passed:
                cp.wait_send()
            own.wait()


class _Hosts:
    gather = False

    def __init__(self, *hosts):
        self.hosts = hosts
        self.n = sum(h.n for h in hosts)
        self.arrs = [a for h in hosts for a in h.arrs]

    def out_shape(self):
        return [sh for h in self.hosts for sh in h.out_shape()]

    def scratch(self):
        return [sc for h in self.hosts for sc in h.scratch()]

    def _each(self, ins, outs, sems):
        at = 0
        for k, h in enumerate(self.hosts):
            yield h, ins[at:at + h.n], outs[at:at + h.n], sems[3 * k:3 * k + 3]
            at += h.n

    def start(self, ins, outs, sems):
        for h, h_in, h_out, h_sems in self._each(ins, outs, sems):
            h.start(h_in, h_out, h_sems)

    def wait(self, ins, outs, sems, relayed=False):
        for h, h_in, h_out, h_sems in self._each(ins, outs, sems):
            h.wait(h_in, h_out, h_sems)


def _hosted_call(host, body, *, name, grid, in_specs, out_specs, out_shape, scratch_shapes=(), compiler_params):
    out_specs = list(out_specs) if isinstance(out_specs, (list, tuple)) else [out_specs]
    out_shape = list(out_shape) if isinstance(out_shape, (list, tuple)) else [out_shape]
    if host is None:
        return pl.pallas_call(body, name=name, grid=grid, in_specs=in_specs, out_specs=out_specs,
                              out_shape=out_shape, scratch_shapes=list(scratch_shapes),
                              compiler_params=compiler_params)
    n_in, n_out, n_scr, k = len(in_specs), len(out_shape), len(scratch_shapes), host.n

    def wrapped(*refs):
        ins, h_in = refs[:n_in], refs[n_in:n_in + k]
        outs, h_out = refs[n_in + k:n_in + k + n_out], refs[n_in + k + n_out:n_in + 2 * k + n_out]
        scr, sems = refs[n_in + 2 * k + n_out:n_in + 2 * k + n_out + n_scr], refs[n_in + 2 * k + n_out + n_scr:]
        ids = [pl.program_id(a) for a in range(len(grid))]
        first = functools.reduce(jnp.logical_and, [i == 0 for i in ids])
        last = functools.reduce(jnp.logical_and, [i == g - 1 for i, g in zip(ids, grid)])
        steps = math.prod(grid)
        relay_at = (3 * steps) // 4 if host.gather and steps >= 8 else None

        @pl.when(first)
        def _():
            host.start(h_in, h_out, sems)

        if relay_at is not None:
            coords, rest = [], relay_at
            for g in reversed(grid):
                coords.append(rest % g)
                rest //= g

            @pl.when(functools.reduce(jnp.logical_and, [i == cd for i, cd in zip(ids, reversed(coords))]))
            def _():
                host.relay(h_in, h_out, sems)

        body(*ins, *outs, *scr)

        @pl.when(last)
        def _():
            host.wait(h_in, h_out, sems, relayed=relay_at is not None)

    hbm = pl.BlockSpec(memory_space=pl.ANY)
    call = pl.pallas_call(
        wrapped, name=name, grid=grid, in_specs=list(in_specs) + [hbm] * k, out_specs=out_specs + [hbm] * k,
        out_shape=out_shape + host.out_shape(), scratch_shapes=list(scratch_shapes) + host.scratch(),
        compiler_params=compiler_params)
    return lambda *args: call(*args, *host.arrs)


def _ffn_fwd(xhat, g_in, b_in, wg, wu, wd, *, tm, name, host=None):
    t = xhat.shape[0]
    nj = N_DEV

    def body(x_ref, g_ref, b_ref, wg_ref, wu_ref, wd_ref, xo_ref, rstd_ref, hg_ref, hu_ref, xb, acc):
        j = pl.program_id(1)

        @pl.when(j == 0)
        def _():
            xb[...] = (x_ref[...] * g_ref[...] + b_ref[...]).astype(bf16)
            acc[...] = jnp.zeros_like(acc)

        hg = jnp.dot(xb[...], wg_ref[...], preferred_element_type=f32)
        hu = jnp.dot(xb[...], wu_ref[...], preferred_element_type=f32)
        hg_ref[...] = hg.astype(bf16)
        hu_ref[...] = hu.astype(bf16)
        a = hg * _sigmoid_tanh(hg) * hu
        acc[...] += jnp.dot(a.astype(bf16), wd_ref[...], preferred_element_type=f32)

        @pl.when(j == nj - 1)
        def _():
            x = x_ref[...] * g_ref[...] + b_ref[...]
            xo, rstd = _ln_fwd_tile(ALPHA * x + 0.5 * acc[...])
            xo_ref[...] = xo
            rstd_ref[...] = rstd

    row = pl.BlockSpec((1, D_MODEL), lambda i, j: (0, 0))
    return _hosted_call(
        host, body, name=name, grid=(t // tm, nj),
        in_specs=[pl.BlockSpec((tm, D_MODEL), lambda i, j: (i, 0)), row, row,
                  pl.BlockSpec((None, D_MODEL, FF_TILE), lambda i, j: (j, 0, 0)),
                  pl.BlockSpec((None, D_MODEL, FF_TILE), lambda i, j: (j, 0, 0)),
                  pl.BlockSpec((None, FF_TILE, D_MODEL), lambda i, j: (j, 0, 0))],
        out_specs=[pl.BlockSpec((tm, D_MODEL), lambda i, j: (i, 0)),
                   pl.BlockSpec((tm, 1), lambda i, j: (i, 0)),
                   pl.BlockSpec((tm, FF_TILE), lambda i, j: (i, j)),
                   pl.BlockSpec((tm, FF_TILE), lambda i, j: (i, j))],
        out_shape=[jax.ShapeDtypeStruct((t, D_MODEL), f32), jax.ShapeDtypeStruct((t, 1), f32),
                   jax.ShapeDtypeStruct((t, D_FF), bf16), jax.ShapeDtypeStruct((t, D_FF), bf16)],
        scratch_shapes=[pltpu.VMEM((tm, D_MODEL), bf16), pltpu.VMEM((tm, D_MODEL), f32)],
        compiler_params=_params(("arbitrary", "arbitrary")),
    )(xhat, g_in, b_in, wg, wu, wd)


def _ffn1_fwd_gathering(x, own, extra, *, tm, name):
    t = x.shape[0]
    n_i = t // tm
    n_arr = 3
    k_extra = extra.n
    ex = _Exchange(list(own), gather=True)
    ax, ay, ac = lax.axis_index("x"), lax.axis_index("y"), lax.axis_index("c")
    order = jnp.stack([4 * px + 2 * py + pc for px, py in ((ax, ay), (1 - ax, ay), (ax, 1 - ay), (1 - ax, 1 - ay))
                       for pc in (ac, 1 - ac)]).astype(jnp.int32)
    arrival = [None, (0, None), (1, 0), (4, None), (2, 1), (5, None), (3, 2), (6, None)]

    def body(order_ref, x_ref, *refs):
        w_in, e_in = refs[:n_arr], refs[n_arr:n_arr + k_extra]
        refs = refs[n_arr + k_extra:]
        xo_ref, rstd_ref, hg_ref, hu_ref = refs[:4]
        w_all, e_out = refs[4:4 + n_arr], refs[4 + n_arr:4 + n_arr + k_extra]
        acc, wgb, wub, wdb, fetch_sems, send_sems, recv_sems, local_sems = refs[4 + n_arr + k_extra:12 + n_arr + k_extra]
        e_sems = refs[12 + n_arr + k_extra:]
        bufs = (wgb, wub, wdb)
        s = pl.program_id(0)
        i = pl.program_id(1)
        per_array, sibling, chips, (x_, y_, c_) = ex.gather_copies(w_in, w_all, (send_sems, recv_sems, local_sems))

        def fetch(pos, slot):
            return [pltpu.make_async_copy(w_in[a] if pos == 0 else w_all[a].at[order_ref[pos]],
                                          bufs[a].at[slot], fetch_sems.at[n_arr * slot + a]) for a in range(n_arr)]

        def source_of(pos):
            chip = (x_, y_) if pos < 2 else chips[(pos - 2) // 2]
            return (*chip, c_ if pos % 2 == 0 else 1 - c_)

        @pl.when(jnp.logical_and(s == 0, i == 0))
        def _():
            for q in range(4):
                for first, _, own_copy, _ in per_array:
                    if q == 0:
                        own_copy.start()
                    first[q].start()
            for cp in fetch(0, 0):
                cp.start()
            for cp in fetch(0, 0):
                cp.wait()

        @pl.when(jnp.logical_and(s == N_DEV // 2, i == 0))
        def _():
            extra.start(e_in, e_out, e_sems)

        for pos in range(1, N_DEV):
            @pl.when(jnp.logical_and(s == pos - 1, i == n_i - 1))
            def _(pos=pos):
                sem, passes = arrival[pos]
                for _, passed, _, copy in per_array:
                    copy(sem, source_of(pos), (x_, y_, c_)).wait_recv()
                    if passes is not None:
                        passed[passes].start()
                for cp in fetch(pos, pos % 2):
                    cp.start()

            @pl.when(jnp.logical_and(s == pos, i == 0))
            def _(pos=pos):
                for cp in fetch(pos, pos % 2):
                    cp.wait()

        slot = s % 2
        xb = x_ref[...].astype(bf16)
        hg = jnp.dot(xb, wgb[slot], preferred_element_type=f32)
        hu = jnp.dot(xb, wub[slot], preferred_element_type=f32)
        hg_ref[...] = hg.astype(bf16)
        hu_ref[...] = hu.astype(bf16)
        a = hg * _sigmoid_tanh(hg) * hu
        part = jnp.dot(a.astype(bf16), wdb[slot], preferred_element_type=f32)

        @pl.when(s == 0)
        def _():
            acc[i] = part

        @pl.when(s > 0)
        def _():
            acc[i] += part

        @pl.when(s == N_DEV - 1)
        def _():
            xo, rstd = _ln_fwd_tile(ALPHA * x_ref[...] + 0.5 * acc[i])
            xo_ref[...] = xo
            rstd_ref[...] = rstd

        @pl.when(jnp.logical_and(s == N_DEV - 1, i == n_i - 1))
        def _():
            for first, passed, own_copy, _ in per_array:
                for cp in first + passed:
                    cp.wait_send()
                own_copy.wait()
            extra.wait(e_in, e_out, e_sems)

    hbm = pl.BlockSpec(memory_space=pl.ANY)
    last = N_DEV - 1
    tok_out = pl.BlockSpec((tm, D_MODEL), lambda s, i, o: (jnp.where(s == last, i, 0), 0))
    col_out = pl.BlockSpec((tm, 1), lambda s, i, o: (jnp.where(s == last, i, 0), 0))
    hid = pl.BlockSpec((tm, FF_TILE), lambda s, i, o: (i, o[s]))
    shard_shapes = [(N_DEV,) + w.shape for w in own]
    grid_spec = pltpu.PrefetchScalarGridSpec(
        num_scalar_prefetch=1, grid=(N_DEV, n_i),
        in_specs=[pl.BlockSpec((tm, D_MODEL), lambda s, i, o: (i, 0))] + [hbm] * (n_arr + k_extra),
        out_specs=[tok_out, col_out, hid, hid] + [hbm] * (n_arr + k_extra),
        scratch_shapes=[pltpu.VMEM((n_i, tm, D_MODEL), f32)]
        + [pltpu.VMEM((2,) + w.shape, bf16) for w in own]
        + [pltpu.SemaphoreType.DMA((2 * n_arr,))] + ex.scratch() + extra.scratch())
    res = pl.pallas_call(
        body, name=name, grid_spec=grid_spec,
        out_shape=[jax.ShapeDtypeStruct((t, D_MODEL), f32), jax.ShapeDtypeStruct((t, 1), f32),
                   jax.ShapeDtypeStruct((t, D_FF), bf16), jax.ShapeDtypeStruct((t, D_FF), bf16)]
        + [jax.ShapeDtypeStruct(sh, bf16) for sh in shard_shapes] + extra.out_shape(),
        compiler_params=_params(("arbitrary", "arbitrary")),
    )(order, x, *own, *extra.arrs)
    return res


def _ffn_bwd(dpre, hg, hu, wg, wu, wd, ln_in, *, tm, name, host=None):
    t = dpre.shape[0]
    nj = N_DEV
    with_ln = ln_in is not None

    def body(*refs):
        if with_ln:
            (dp_ref, hg_ref, hu_ref, wg_ref, wu_ref, wd_ref, xh_ref, rs_ref, g_ref,
             dx_ref, gg_ref, gb_ref, dhg_ref, dhu_ref, a_ref, dfb, acc) = refs
        else:
            (dp_ref, hg_ref, hu_ref, wg_ref, wu_ref, wd_ref,
             dx_ref, dhg_ref, dhu_ref, a_ref, dfb, acc) = refs
        i = pl.program_id(0)
        j = pl.program_id(1)

        @pl.when(j == 0)
        def _():
            dfb[...] = (0.5 * dp_ref[...]).astype(bf16)
            acc[...] = jnp.zeros_like(acc)

        da = lax.dot_general(dfb[...], wd_ref[...], _NT, preferred_element_type=f32)
        hgv = hg_ref[...].astype(f32)
        huv = hu_ref[...].astype(f32)
        sg = _sigmoid_tanh(hgv)
        silu = hgv * sg
        a_ref[...] = (silu * huv).astype(bf16)
        dhu = (da * silu).astype(bf16)
        dhg = (da * huv * (sg * (1.0 + hgv * (1.0 - sg)))).astype(bf16)
        dhg_ref[...] = dhg
        dhu_ref[...] = dhu
        acc[...] += (lax.dot_general(dhg, wg_ref[...], _NT, preferred_element_type=f32)
                     + lax.dot_general(dhu, wu_ref[...], _NT, preferred_element_type=f32))

        @pl.when(j == nj - 1)
        def _():
            dx = ALPHA * dp_ref[...] + acc[...]
            if with_ln:
                dprev, gg, gb = _ln_bwd_tile(dx, xh_ref[...], rs_ref[...], g_ref[...])
                dx_ref[...] = dprev

                @pl.when(i == 0)
                def _():
                    gg_ref[...] = gg
                    gb_ref[...] = gb

                @pl.when(i > 0)
                def _():
                    gg_ref[...] += gg
                    gb_ref[...] += gb
            else:
                dx_ref[...] = dx

    tok = pl.BlockSpec((tm, D_MODEL), lambda i, j: (i, 0), pipeline_mode=pl.Buffered(1))
    row = pl.BlockSpec((1, D_MODEL), lambda i, j: (0, 0))
    hid = pl.BlockSpec((tm, FF_TILE), lambda i, j: (i, j))
    in_specs = [tok, hid, hid,
                pl.BlockSpec((None, D_MODEL, FF_TILE), lambda i, j: (j, 0, 0)),
                pl.BlockSpec((None, D_MODEL, FF_TILE), lambda i, j: (j, 0, 0)),
                pl.BlockSpec((None, FF_TILE, D_MODEL), lambda i, j: (j, 0, 0))]
    args = [dpre, hg, hu, wg, wu, wd]
    out_specs = [tok]
    out_shape = [jax.ShapeDtypeStruct((t, D_MODEL), f32)]
    if with_ln:
        in_specs += [tok, pl.BlockSpec((tm, 1), lambda i, j: (i, 0)), row]
        args += list(ln_in)
        out_specs += [row, row]
        out_shape += [jax.ShapeDtypeStruct((1, D_MODEL), f32)] * 2
    out_specs += [hid, hid, hid]
    out_shape += [jax.ShapeDtypeStruct((t, D_FF), bf16)] * 3
    return _hosted_call(
        host, body, name=name, grid=(t // tm, nj), in_specs=in_specs, out_specs=out_specs, out_shape=out_shape,
        scratch_shapes=[pltpu.VMEM((tm, D_MODEL), bf16), pltpu.VMEM((tm, D_MODEL), f32)],
        compiler_params=_params(("arbitrary", "arbitrary")),
    )(*args)


def _ffn_bwd_act(dpre, hg, hu, wd, *, tm, name, host=None):
    t = dpre.shape[0]

    def body(dp_ref, hg_ref, hu_ref, wd_ref, dhg_ref, dhu_ref, a_ref, dfb):
        @pl.when(pl.program_id(1) == 0)
        def _():
            dfb[...] = (0.5 * dp_ref[...]).astype(bf16)

        da = lax.dot_general(dfb[...], wd_ref[...], _NT, preferred_element_type=f32)
        hgv = hg_ref[...].astype(f32)
        huv = hu_ref[...].astype(f32)
        sg = _sigmoid_tanh(hgv)
        silu = hgv * sg
        a_ref[...] = (silu * huv).astype(bf16)
        dhu_ref[...] = (da * silu).astype(bf16)
        dhg_ref[...] = (da * huv * (sg * (1.0 + hgv * (1.0 - sg)))).astype(bf16)

    hid = pl.BlockSpec((tm, FF_TILE), lambda i, j: (i, j))
    return _hosted_call(
        host, body, name=name, grid=(t // tm, N_DEV),
        in_specs=[pl.BlockSpec((tm, D_MODEL), lambda i, j: (i, 0)), hid, hid,
                  pl.BlockSpec((None, FF_TILE, D_MODEL), lambda i, j: (j, 0, 0))],
        out_specs=[hid, hid, hid], out_shape=[jax.ShapeDtypeStruct((t, D_FF), bf16)] * 3,
        scratch_shapes=[pltpu.VMEM((tm, D_MODEL), bf16)],
        compiler_params=_params(("arbitrary", "arbitrary")),
    )(dpre, hg, hu, wd)


def _ffn_bwd_dx(dpre, dhg, dhu, wg, wu, *, tm, name, host=None):
    t = dpre.shape[0]
    nj = N_DEV

    def body(dp_ref, dhg_ref, dhu_ref, wg_ref, wu_ref, dx_ref, acc):
        j = pl.program_id(1)

        @pl.when(j == 0)
        def _():
            acc[...] = jnp.zeros_like(acc)

        acc[...] += (lax.dot_general(dhg_ref[...], wg_ref[...], _NT, preferred_element_type=f32)
                     + lax.dot_general(dhu_ref[...], wu_ref[...], _NT, preferred_element_type=f32))

        @pl.when(j == nj - 1)
        def _():
            dx_ref[...] = ALPHA * dp_ref[...] + acc[...]

    tok = pl.BlockSpec((tm, D_MODEL), lambda i, j: (i, 0))
    hid = pl.BlockSpec((tm, FF_TILE), lambda i, j: (i, j))
    wspec = pl.BlockSpec((None, D_MODEL, FF_TILE), lambda i, j: (j, 0, 0))
    return _hosted_call(
        host, body, name=name, grid=(t // tm, nj), in_specs=[tok, hid, hid, wspec, wspec],
        out_specs=[tok], out_shape=[jax.ShapeDtypeStruct((t, D_MODEL), f32)],
        scratch_shapes=[pltpu.VMEM((tm, D_MODEL), f32)],
        compiler_params=_params(("arbitrary", "arbitrary")),
    )(dpre, dhg, dhu, wg, wu)


def _mm(a, b, *, mode, out_dtype, tm, tn, tk, name, affine=None, a_cols=None, b_cols=None,
        b_blocked=False, out_blocked=False, out_scale=None):
    if mode == "nn":
        m_full, k_full = a.shape
        m_dim, k_dim = (m_full, a_cols[1]) if a_cols else (m_full, k_full)
    else:
        k_dim, m_full = a.shape
        m_dim = a_cols[1] if a_cols else m_full
    a_off = a_cols[0] if a_cols else 0
    if b_blocked:
        n_dim = b.shape[0] * b.shape[2]
        assert b.shape[2] == tn
    else:
        n_dim = b_cols[1] if b_cols else b.shape[1]
    b_off = b_cols[0] if b_cols else 0
    assert m_dim % tm == 0 and n_dim % tn == 0 and k_dim % tk == 0, (name, m_dim, n_dim, k_dim)
    nk = k_dim // tk

    def body(*refs):
        if affine is not None:
            a_ref, g_ref, s_ref, b_ref, o_ref, acc = refs
        else:
            a_ref, b_ref, o_ref, acc = refs
        k = pl.program_id(2)

        @pl.when(k == 0)
        def _():
            acc[...] = jnp.zeros_like(acc)

        av = a_ref[...]
        if affine is not None:
            av = av * g_ref[...] + s_ref[...]
        av = av.astype(bf16)
        bv = b_ref[...].astype(bf16)
        if mode == "nn":
            acc[...] += jnp.dot(av, bv, preferred_element_type=f32)
        else:
            acc[...] += lax.dot_general(av, bv, _TN, preferred_element_type=f32)

        @pl.when(k == nk - 1)
        def _():
            res = acc[...] if out_scale is None else acc[...] * out_scale
            o_ref[...] = res.astype(out_dtype)

    if mode == "nn":
        a_spec = pl.BlockSpec((tm, tk), lambda i, j, k: (i, k + a_off))
        aff_spec = pl.BlockSpec((1, tk), lambda i, j, k: (0, k + a_off))
    else:
        a_spec = pl.BlockSpec((tk, tm), lambda i, j, k: (k, i + a_off))
        aff_spec = pl.BlockSpec((1, tm), lambda i, j, k: (0, i + a_off))
    if b_blocked:
        b_spec = pl.BlockSpec((None, tk, tn), lambda i, j, k: (j, k, 0))
    else:
        b_spec = pl.BlockSpec((tk, tn), lambda i, j, k: (k, j + b_off))
    if out_blocked:
        o_spec = pl.BlockSpec((None, tm, tn), lambda i, j, k: (j, i, 0))
        o_shape = jax.ShapeDtypeStruct((n_dim // tn, m_dim, tn), out_dtype)
    else:
        o_spec = pl.BlockSpec((tm, tn), lambda i, j, k: (i, j))
        o_shape = jax.ShapeDtypeStruct((m_dim, n_dim), out_dtype)
    in_specs = [a_spec] + ([aff_spec, aff_spec] if affine is not None else []) + [b_spec]
    args = [a] + (list(affine) if affine is not None else []) + [b]
    return pl.pallas_call(
        body, name=name, grid=(m_dim // tm, n_dim // tn, nk), in_specs=in_specs, out_specs=o_spec,
        out_shape=o_shape, scratch_shapes=[pltpu.VMEM((tm, tn), f32)],
        compiler_params=_params(("arbitrary", "arbitrary", "arbitrary")),
    )(*args)


def _mm_tn(a, b, *, out_dtype, tm, mb, tn, nb, tk, name, affine=None, out_blocked=False, out_scale=None,
           pair=False, host=None):
    k_dim, m_dim = a.shape
    multi_b = isinstance(b, (list, tuple))
    b_list = list(b) if multi_b else [b]
    n_dim = nb * tn if multi_b else b.shape[1]
    assert m_dim % (mb * tm) == 0 and n_dim % (nb * tn) == 0 and k_dim % tk == 0, (name, m_dim, n_dim, k_dim)
    nk = k_dim // tk
    grid = (m_dim // (mb * tm), n_dim // (nb * tn), nk)
    if pair:
        assert mb * nb == 4 and grid[0] * grid[1] == 2 and out_dtype == bf16, name

    def body(*refs):
        if pair:
            refs, (acc, send_buf, recv_buf, send_sems, recv_sems) = refs[:-5], refs[-5:]
        else:
            refs, acc = refs[:-1], refs[-1]
        a_ref, o_ref = refs[0], refs[-1]
        if affine is not None:
            g_ref, s_ref = refs[1:3]
        b_refs = refs[3 if affine is not None else 1:-1]
        k = pl.program_id(2)

        @pl.when(k == 0)
        def _():
            acc[...] = jnp.zeros_like(acc)

        av = a_ref[...]
        if affine is not None:
            av = av * g_ref[...] + s_ref[...]
        av = av.astype(bf16)
        if multi_b:
            pieces = [r[...].astype(bf16) for r in b_refs]
        else:
            bv = b_refs[0][...].astype(bf16)
            pieces = [bv[:, jn * tn:(jn + 1) * tn] for jn in range(nb)]
        for im in range(mb):
            a_t = av[:, im * tm:(im + 1) * tm].T
            for jn in range(nb):
                acc[im * nb + jn] += jnp.dot(a_t, pieces[jn], preferred_element_type=f32)

        def scaled(v):
            return v if out_scale is None else v * out_scale

        @pl.when(k == nk - 1)
        def _():
            if pair:
                x, y, c = lax.axis_index("x"), lax.axis_index("y"), lax.axis_index("c")
                window = pl.program_id(0) + pl.program_id(1)
                swaps = []
                for cc in range(2):
                    send_buf[cc] = scaled(acc[2 * cc + 1 - c]).astype(bf16)
                    swaps.append(pltpu.make_async_remote_copy(
                        src_ref=send_buf.at[cc], dst_ref=recv_buf.at[window, cc],
                        send_sem=send_sems.at[2 * window + cc], recv_sem=recv_sems.at[2 * window + cc],
                        device_id=(x, y, 1 - c), device_id_type=MESH_T))
                    swaps[cc].start()
                for cc in range(2):
                    swaps[cc].wait_recv()
                    o_ref[cc] = (scaled(acc[2 * cc + c]) + recv_buf[window, cc].astype(f32)).astype(bf16)
                for cc in range(2):
                    swaps[cc].wait_send()
                return
            for im in range(mb):
                for jn in range(nb):
                    res = scaled(acc[im * nb + jn])
                    if out_blocked:
                        o_ref[jn, im * tm:(im + 1) * tm, :] = res.astype(out_dtype)
                    else:
                        o_ref[im * tm:(im + 1) * tm, jn * tn:(jn + 1) * tn] = res.astype(out_dtype)

    a_spec = pl.BlockSpec((tk, mb * tm), lambda i, j, k: (k, i))
    aff_spec = pl.BlockSpec((1, mb * tm), lambda i, j, k: (0, i))
    if multi_b:
        b_specs = [pl.BlockSpec((tk, tn), lambda i, j, k: (k, 0))] * nb
    else:
        b_specs = [pl.BlockSpec((tk, nb * tn), lambda i, j, k: (k, j))]
    scratch = [pltpu.VMEM((mb * nb, tm, tn), f32)]
    if pair:
        o_spec = pl.BlockSpec((2, tm, tn), lambda i, j, k: (i + j, 0, 0))
        o_shape = jax.ShapeDtypeStruct((4, tm, tn), out_dtype)
        scratch += [pltpu.VMEM((2, tm, tn), bf16), pltpu.VMEM((2, 2, tm, tn), bf16),
                    pltpu.SemaphoreType.DMA((4,)), pltpu.SemaphoreType.DMA((4,))]
    elif out_blocked:
        o_spec = pl.BlockSpec((nb, mb * tm, tn), lambda i, j, k: (j, i, 0))
        o_shape = jax.ShapeDtypeStruct((n_dim // tn, m_dim, tn), out_dtype)
    else:
        o_spec = pl.BlockSpec((mb * tm, nb * tn), lambda i, j, k: (i, j))
        o_shape = jax.ShapeDtypeStruct((m_dim, n_dim), out_dtype)
    in_specs = [a_spec] + ([aff_spec, aff_spec] if affine is not None else []) + b_specs
    args = [a] + (list(affine) if affine is not None else []) + b_list
    res = _hosted_call(
        host, body, name=name, grid=grid, in_specs=in_specs, out_specs=o_spec, out_shape=o_shape,
        scratch_shapes=scratch, compiler_params=_params(("arbitrary", "arbitrary", "arbitrary")),
    )(*args)
    return res[0] if host is None else res


def _in_proj(xhat, g, b, w_in, *, tm, name):
    t = xhat.shape[0]
    n_qkv, n_l = 3 * FOX_W, 2 * LRU_W

    def body(x_ref, g_ref, b_ref, w_ref, qkv_ref, zl_ref, zfg_ref):
        xb = (x_ref[...] * g_ref[...] + b_ref[...]).astype(bf16)
        qkv_ref[...] = jnp.dot(xb, w_ref[:, :n_qkv], preferred_element_type=f32).astype(bf16)
        zl_ref[...] = jnp.dot(xb, w_ref[:, n_qkv:n_qkv + n_l], preferred_element_type=f32)
        zfg_ref[...] = jnp.dot(xb, w_ref[:, n_qkv + n_l:], preferred_element_type=f32)

    row = pl.BlockSpec((1, D_MODEL), lambda i: (0, 0))
    return pl.pallas_call(
        body, name=name, grid=(t // tm,),
        in_specs=[pl.BlockSpec((tm, D_MODEL), lambda i: (i, 0)), row, row,
                  pl.BlockSpec(w_in.shape, lambda i: (0, 0))],
        out_specs=[pl.BlockSpec((tm, n_qkv), lambda i: (i, 0)), pl.BlockSpec((tm, n_l), lambda i: (i, 0)),
                   pl.BlockSpec((tm, LANES), lambda i: (i, 0))],
        out_shape=[jax.ShapeDtypeStruct((t, n_qkv), bf16), jax.ShapeDtypeStruct((t, n_l), f32),
                   jax.ShapeDtypeStruct((t, LANES), f32)],
        compiler_params=_params(("arbitrary",)),
    )(xhat, g, b, w_in)


def _mmln(pairs, *, tm, name, resid=None, resid_scale=1.0, epi=None, ln=None, n_out=D_MODEL):
    t = pairs[0][0].shape[0]
    n_pairs = len(pairs)
    n_resid = 0 if resid is None else len(resid) - 1

    def body(*refs):
        pos = 0
        val = None
        for p in range(n_pairs):
            a_ref, b_ref = refs[pos], refs[pos + 1]
            pos += 2
            av = a_ref[...].astype(bf16)
            bv = b_ref[...].astype(bf16)
            if pairs[p][6] == "nn":
                term = jnp.dot(av, bv, preferred_element_type=f32)
            else:
                term = lax.dot_general(av, bv, _NT, preferred_element_type=f32)
            val = term if val is None else val + term
        if resid is not None:
            if resid[0] == "plain":
                r = refs[pos][...]
            else:
                r = refs[pos][...] * refs[pos + 1][...] + refs[pos + 2][...]
            pos += n_resid
            val = val + resid_scale * r
        if epi is None:
            o_ref = refs[pos]
            o_ref[...] = val.astype(o_ref.dtype)
        elif epi == "ln_fwd":
            xo, rstd = _ln_fwd_tile(val)
            refs[pos][...] = xo
            refs[pos + 1][...] = rstd
        else:
            xh_ref, rs_ref, g_ref, dx_ref, gg_ref, gb_ref = refs[pos:pos + 6]
            dprev, gg, gb = _ln_bwd_tile(val, xh_ref[...], rs_ref[...], g_ref[...])
            dx_ref[...] = dprev
            i = pl.program_id(0)

            @pl.when(i == 0)
            def _():
                gg_ref[...] = gg
                gb_ref[...] = gb

            @pl.when(i > 0)
            def _():
                gg_ref[...] += gg
                gb_ref[...] += gb

    in_specs, args = [], []
    for (a, acb, aw, b, bcb, bw, mode) in pairs:
        in_specs.append(pl.BlockSpec((tm, aw), lambda i, acb=acb: (i, acb)))
        args.append(a)
        if mode == "nn":
            in_specs.append(pl.BlockSpec((aw, n_out), lambda i, bcb=bcb: (bcb, 0)))
        else:
            in_specs.append(pl.BlockSpec((n_out, bw), lambda i, bcb=bcb: (0, bcb)))
        args.append(b)
    tok = pl.BlockSpec((tm, n_out), lambda i: (i, 0))
    row = pl.BlockSpec((1, n_out), lambda i: (0, 0))
    col = pl.BlockSpec((tm, 1), lambda i: (i, 0))
    if resid is not None:
        in_specs += [tok] if resid[0] == "plain" else [tok, row, row]
        args += list(resid[1:])
    if epi is None:
        out_specs, out_shape = tok, jax.ShapeDtypeStruct((t, n_out), f32)
    elif epi == "ln_fwd":
        out_specs = [tok, col]
        out_shape = [jax.ShapeDtypeStruct((t, n_out), f32), jax.ShapeDtypeStruct((t, 1), f32)]
    else:
        in_specs += [tok, col, row]
        args += list(ln)
        out_specs = [tok, row, row]
        out_shape = [jax.ShapeDtypeStruct((t, n_out), f32)] + [jax.ShapeDtypeStruct((1, n_out), f32)] * 2
    return pl.pallas_call(
        body, name=name, grid=(t // tm,), in_specs=in_specs, out_specs=out_specs, out_shape=out_shape,
        compiler_params=_params(("arbitrary",)),
    )(*args)


def _loss_bwd(xhat, rstd, g, b, target, *, tm, name):
    t = xhat.shape[0]

    def body(xh_ref, rs_ref, g_ref, b_ref, tg_ref, dx_ref, sq_ref, gg_ref, gb_ref):
        i = pl.program_id(0)
        xh = xh_ref[...]
        diff = xh * g_ref[...] + b_ref[...] - tg_ref[...]
        sq = jnp.sum(diff * diff, axis=0, keepdims=True)
        dprev, gg, gb = _ln_bwd_tile(diff * (1.0 / D_MODEL), xh, rs_ref[...], g_ref[...])
        dx_ref[...] = dprev

        @pl.when(i == 0)
        def _():
            sq_ref[...] = sq
            gg_ref[...] = gg
            gb_ref[...] = gb

        @pl.when(i > 0)
        def _():
            sq_ref[...] += sq
            gg_ref[...] += gg
            gb_ref[...] += gb

    tok = pl.BlockSpec((tm, D_MODEL), lambda i: (i, 0))
    row = pl.BlockSpec((1, D_MODEL), lambda i: (0, 0))
    return pl.pallas_call(
        body, name=name, grid=(t // tm,),
        in_specs=[tok, pl.BlockSpec((tm, 1), lambda i: (i, 0)), row, row, tok],
        out_specs=[tok, row, row, row],
        out_shape=[jax.ShapeDtypeStruct((t, D_MODEL), f32)] + [jax.ShapeDtypeStruct((1, D_MODEL), f32)] * 3,
        compiler_params=_params(("arbitrary",)),
    )(xhat, rstd, g, b, target)


CUM_TILE = 256


def _tri(n, lower):
    r = lax.broadcasted_iota(jnp.int32, (n, n), 0)
    c = lax.broadcasted_iota(jnp.int32, (n, n), 1)
    return jnp.where((r >= c) if lower else (r <= c), 1.0, 0.0).astype(f32)


def _cum_fwd(zfg, bfg, *, name):
    t = zfg.shape[0]

    def body(z_ref, b_ref, o_ref, carry):
        @pl.when(pl.program_id(0) == 0)
        def _():
            carry[...] = jnp.zeros_like(carry)

        ls = -_softplus(-(z_ref[...] + b_ref[...]))
        c = jnp.dot(_tri(CUM_TILE, True), ls, preferred_element_type=f32,
                    precision=lax.Precision.HIGHEST) + carry[...]
        o_ref[...] = c
        carry[...] = c[CUM_TILE - 1:CUM_TILE, :]

    blk = pl.BlockSpec((CUM_TILE, LANES), lambda i: (i, 0))
    return pl.pallas_call(
        body, name=name, grid=(t // CUM_TILE,),
        in_specs=[blk, pl.BlockSpec((1, LANES), lambda i: (0, 0))], out_specs=blk,
        out_shape=jax.ShapeDtypeStruct((t, LANES), f32), scratch_shapes=[pltpu.VMEM((1, LANES), f32)],
        compiler_params=_params(("arbitrary",)),
    )(zfg, bfg)


def _cum_bwd(dcum_q, dcum_k, zfg, bfg, *, name):
    t = zfg.shape[0]
    n = t // CUM_TILE

    def body(d_ref, d2_ref, z_ref, b_ref, o_ref, s_ref, carry):
        i = pl.program_id(0)

        @pl.when(i == 0)
        def _():
            carry[...] = jnp.zeros_like(carry)

        dls = jnp.dot(_tri(CUM_TILE, False), d_ref[...] + d2_ref[...], preferred_element_type=f32,
                      precision=lax.Precision.HIGHEST) + carry[...]
        carry[...] = dls[0:1, :]
        lane = lax.broadcasted_iota(jnp.int32, (CUM_TILE, LANES), 1)
        dfg = jnp.where(lane < HEADS, dls * _sigmoid(-(z_ref[...] + b_ref[...])), 0.0)
        o_ref[...] = dfg
        tot = jnp.sum(dfg, axis=0, keepdims=True)

        @pl.when(i == 0)
        def _():
            s_ref[...] = tot

        @pl.when(i > 0)
        def _():
            s_ref[...] += tot

    blk = pl.BlockSpec((CUM_TILE, LANES), lambda i: (n - 1 - i, 0))
    row = pl.BlockSpec((1, LANES), lambda i: (0, 0))
    return pl.pallas_call(
        body, name=name, grid=(n,), in_specs=[blk, blk, blk, row], out_specs=[blk, row],
        out_shape=[jax.ShapeDtypeStruct((t, LANES), f32), jax.ShapeDtypeStruct((1, LANES), f32)],
        scratch_shapes=[pltpu.VMEM((1, LANES), f32)],
        compiler_params=_params(("arbitrary",)),
    )(dcum_q, dcum_k, zfg, bfg)


ATT_TILE = 512


def _causal(i, j, transposed):
    r = lax.broadcasted_iota(jnp.int32, (ATT_TILE, ATT_TILE), 0)
    c = lax.broadcasted_iota(jnp.int32, (ATT_TILE, ATT_TILE), 1)
    if transposed:
        return (c + i * ATT_TILE) >= (r + j * ATT_TILE)
    return (r + i * ATT_TILE) >= (c + j * ATT_TILE)


ATT_W = HEADS * LANES


def _data_lane(h):
    return HEAD_D * (h % 2)


def _extra_lane(h):
    return HEAD_D - _data_lane(h)


def _split3(x):
    hi = x.astype(bf16)
    rest = x - hi.astype(f32)
    mid = rest.astype(bf16)
    lo = (rest - mid.astype(f32)).astype(bf16)
    return hi, mid, lo


def _three_pieces(x):
    hi, mid, lo = (p.astype(f32) for p in _split3(x))
    return (hi + pltpu.roll(mid, HEADS, axis=1) + pltpu.roll(lo, 2 * HEADS, axis=1)).astype(bf16)


def _move(h, first):
    r = lax.broadcasted_iota(jnp.int32, (LANES, LANES), 0)
    c = lax.broadcasted_iota(jnp.int32, (LANES, LANES), 1)
    hit = functools.reduce(jnp.logical_or, [jnp.logical_and(r == HEADS * q + h, c == first + q) for q in range(3)])
    return jnp.where(hit, 1.0, 0.0).astype(bf16)


def _ones_from(first, rows):
    lane = lax.broadcasted_iota(jnp.int32, (rows, LANES), 1)
    return jnp.where(jnp.logical_and(lane >= first, lane < first + 3), 1.0, 0.0)


def _own_lanes(h, rows):
    lane = lax.broadcasted_iota(jnp.int32, (rows, LANES), 1)
    return (lane < HEAD_D) if h % 2 == 0 else (lane >= HEAD_D)


def _head_values(x):
    lane = lax.broadcasted_iota(jnp.int32, x.shape, 1)
    return jnp.where(lane < HEADS, x, 0.0)


def _attn_prep_fwd(qkv, cum, *, tm, name):
    t = qkv.shape[0]

    def body(q_ref, k_ref, v_ref, c_ref, qa_ref, ka_ref, va_ref):
        c3 = _three_pieces(_head_values(c_ref[...]))
        ones = jnp.ones((tm, LANES), bf16)
        for h in range(HEADS):
            pair = slice(LANES * (h // 2), LANES * (h // 2 + 1))
            hs = slice(LANES * h, LANES * (h + 1))
            base, own = _extra_lane(h), _own_lanes(h, tm)
            eq = jnp.dot(c3, _move(h, base), preferred_element_type=f32) + _ones_from(base + 3, tm)
            ek = _ones_from(base, tm) - jnp.dot(c3, _move(h, base + 3), preferred_element_type=f32)
            qa_ref[:, hs] = jnp.where(own, q_ref[:, pair] * ATT_SCALE, eq.astype(bf16))
            ka_ref[:, hs] = jnp.where(own, k_ref[:, pair], ek.astype(bf16))
            va_ref[:, hs] = jnp.where(own, v_ref[:, pair], ones)

    wide = pl.BlockSpec((tm, ATT_W), lambda i: (i, 0))
    out = jax.ShapeDtypeStruct((t, ATT_W), bf16)
    return pl.pallas_call(
        body, name=name, grid=(t // tm,),
        in_specs=[pl.BlockSpec((tm, FOX_W), lambda i: (i, 0)), pl.BlockSpec((tm, FOX_W), lambda i: (i, 1)),
                  pl.BlockSpec((tm, FOX_W), lambda i: (i, 2)), pl.BlockSpec((tm, LANES), lambda i: (i, 0))],
        out_specs=[wide] * 3, out_shape=[out] * 3, compiler_params=_params(("arbitrary",)),
    )(qkv, qkv, qkv, cum)


def _attn_prep_bwd(qkv, cum, lse, dmix, o, *, tm, name):
    t = qkv.shape[0]

    def body(q_ref, c_ref, l_ref, do_ref, o_ref, qa_ref, da_ref):
        b3 = _three_pieces(_head_values(c_ref[...] - l_ref[...]))
        r = lax.broadcasted_iota(jnp.int32, (FOX_W, LANES), 0)
        c = lax.broadcasted_iota(jnp.int32, (FOX_W, LANES), 1)
        per_head = jnp.where(r // HEAD_D == c, 1.0, 0.0).astype(bf16)
        delta = sum(jnp.dot(p, per_head, preferred_element_type=f32) for p in _split3(do_ref[...] * o_ref[...]))
        d3 = _three_pieces(delta)
        for h in range(HEADS):
            pair = slice(LANES * (h // 2), LANES * (h // 2 + 1))
            hs = slice(LANES * h, LANES * (h + 1))
            base, own = _extra_lane(h), _own_lanes(h, tm)
            eq = jnp.dot(b3, _move(h, base), preferred_element_type=f32) + _ones_from(base + 3, tm)
            ed = -jnp.dot(d3, _move(h, base), preferred_element_type=f32)
            qa_ref[:, hs] = jnp.where(own, q_ref[:, pair] * ATT_SCALE, eq.astype(bf16))
            da_ref[:, hs] = jnp.where(own, do_ref[:, pair].astype(bf16), ed.astype(bf16))

    wide = pl.BlockSpec((tm, ATT_W), lambda i: (i, 0))
    half = pl.BlockSpec((tm, FOX_W), lambda i: (i, 0))
    col = pl.BlockSpec((tm, LANES), lambda i: (i, 0))
    out = jax.ShapeDtypeStruct((t, ATT_W), bf16)
    return pl.pallas_call(
        body, name=name, grid=(t // tm,), in_specs=[half, col, col, half, half],
        out_specs=[wide] * 2, out_shape=[out] * 2, compiler_params=_params(("arbitrary",)),
    )(qkv, cum, lse, dmix, o)


def _attn_fwd2(q_aug, k_aug, v_aug, *, name, host=None):
    t = q_aug.shape[0]
    n = t // ATT_TILE
    tq = ATT_TILE

    def body(q_ref, k_ref, v_ref, o_ref, lse_ref, acc, m_s):
        i = pl.program_id(0)
        j = pl.program_id(1)

        @pl.when(j == 0)
        def _():
            acc[...] = jnp.zeros_like(acc)
            m_s[...] = jnp.full_like(m_s, NEG_BIG)

        def block(masked):
            mask = _causal(i, j, False) if masked else None
            for h in range(HEADS):
                hs = slice(LANES * h, LANES * (h + 1))
                s = lax.dot_general(q_ref[:, hs], k_ref[:, hs], _NT, preferred_element_type=f32)
                if masked:
                    s = jnp.where(mask, s, NEG_BIG)
                blocks = [s[:, LANES * b:LANES * (b + 1)] for b in range(tq // LANES)]
                m_old = m_s[h]
                m_new = jnp.maximum(m_old, jnp.broadcast_to(
                    jnp.max(functools.reduce(jnp.maximum, blocks), axis=-1, keepdims=True), (tq, LANES)))
                p = jnp.concatenate([jnp.exp(b - m_new) for b in blocks], axis=1).astype(bf16)
                acc[h] = jnp.exp(m_old - m_new) * acc[h] + jnp.dot(p, v_ref[:, hs], preferred_element_type=f32)
                m_s[h] = m_new

        @pl.when(j < i)
        def _():
            block(False)

        @pl.when(j == i)
        def _():
            block(True)
            lse_ref[...] = jnp.zeros_like(lse_ref)
            for h in range(HEADS):
                a = acc[h]
                l = a[:, _extra_lane(h):_extra_lane(h) + 1]
                o_ref[:, HEAD_D * h:HEAD_D * (h + 1)] = a[:, _data_lane(h):_data_lane(h) + HEAD_D] / l
                lse_ref[:, h:h + 1] = m_s[h][:, 0:1] + jnp.log(l)

    kv = pl.BlockSpec((tq, ATT_W), lambda i, j: (jnp.minimum(i, j), 0))
    return _hosted_call(
        host, body, name=name, grid=(n, n),
        in_specs=[pl.BlockSpec((tq, ATT_W), lambda i, j: (i, 0)), kv, kv],
        out_specs=[pl.BlockSpec((tq, FOX_W), lambda i, j: (i, 0)), pl.BlockSpec((tq, LANES), lambda i, j: (i, 0))],
        out_shape=[jax.ShapeDtypeStruct((t, FOX_W), f32), jax.ShapeDtypeStruct((t, LANES), f32)],
        scratch_shapes=[pltpu.VMEM((HEADS, tq, LANES), f32), pltpu.VMEM((HEADS, tq, LANES), f32)],
        compiler_params=_params(("arbitrary", "arbitrary")),
    )(q_aug, k_aug, v_aug)


def _attn_bwd(qb_aug, k_aug, v_aug, do_aug, *, name, host=None):
    t = qb_aug.shape[0]
    n = t // ATT_TILE
    tk = ATT_TILE

    def body(q_ref, k_ref, v_ref, do_ref, dq_ref, dcq_ref, dk_ref, dv_ref, dck_ref, dk_acc, dv_acc, dq_all):
        j = pl.program_id(0)
        i = pl.program_id(1)

        @pl.when(jnp.logical_and(i == 0, j == 0))
        def _():
            dq_all[...] = jnp.zeros_like(dq_all)

        @pl.when(i == 0)
        def _():
            dk_acc[...] = jnp.zeros_like(dk_acc)
            dv_acc[...] = jnp.zeros_like(dv_acc)

        def block(masked):
            mask = _causal(i, j, True) if masked else None
            for h in range(HEADS):
                hs = slice(LANES * h, LANES * (h + 1))
                qh = q_ref[:, hs]
                doh = do_ref[:, hs]
                kh = k_ref[:, hs]
                s_t = lax.dot_general(kh, qh, _NT, preferred_element_type=f32)
                if masked:
                    s_t = jnp.where(mask, s_t, NEG_BIG)
                p_t = jnp.exp(s_t)
                dv_acc[h] += jnp.dot(p_t.astype(bf16), doh, preferred_element_type=f32)
                dp_t = lax.dot_general(v_ref[:, hs], doh, _NT, preferred_element_type=f32)
                ds_t = (p_t * dp_t).astype(bf16)
                dk_acc[h] += jnp.dot(ds_t, qh, preferred_element_type=f32)
                dq_all[i, h] += lax.dot_general(ds_t, kh, _TN, preferred_element_type=f32)

        @pl.when(i > j)
        def _():
            block(False)

        @pl.when(i == j)
        def _():
            block(True)
            dcq_ref[...] = jnp.zeros_like(dcq_ref)
            for h in range(HEADS):
                a = dq_all[j, h]
                dq_ref[:, HEAD_D * h:HEAD_D * (h + 1)] = (
                    a[:, _data_lane(h):_data_lane(h) + HEAD_D] * ATT_SCALE).astype(bf16)
                dcq_ref[:, h:h + 1] = a[:, _extra_lane(h):_extra_lane(h) + 1]

        @pl.when(i == n - 1)
        def _():
            dck_ref[...] = jnp.zeros_like(dck_ref)
            for h in range(HEADS):
                a = dk_acc[h]
                cols = slice(_data_lane(h), _data_lane(h) + HEAD_D)
                dk_ref[:, HEAD_D * h:HEAD_D * (h + 1)] = a[:, cols].astype(bf16)
                dv_ref[:, HEAD_D * h:HEAD_D * (h + 1)] = dv_acc[h][:, cols].astype(bf16)
                dck_ref[:, h:h + 1] = -a[:, _extra_lane(h) + 3:_extra_lane(h) + 4]

    own = pl.BlockSpec((tk, ATT_W), lambda j, i: (j, 0))
    qs = pl.BlockSpec((tk, ATT_W), lambda j, i: (jnp.maximum(i, j), 0))
    half = pl.BlockSpec((tk, FOX_W), lambda j, i: (j, 0))
    col = pl.BlockSpec((tk, LANES), lambda j, i: (j, 0))
    return _hosted_call(
        host, body, name=name, grid=(n, n), in_specs=[qs, own, own, qs],
        out_specs=[half, col, half, half, col],
        out_shape=[jax.ShapeDtypeStruct((t, FOX_W), bf16), jax.ShapeDtypeStruct((t, LANES), f32),
                   jax.ShapeDtypeStruct((t, FOX_W), bf16), jax.ShapeDtypeStruct((t, FOX_W), bf16),
                   jax.ShapeDtypeStruct((t, LANES), f32)],
        scratch_shapes=[pltpu.VMEM((HEADS, tk, LANES), f32), pltpu.VMEM((HEADS, tk, LANES), f32),
                        pltpu.VMEM((n, HEADS, tk, LANES), f32)],
        compiler_params=_params(("arbitrary", "arbitrary")),
    )(qb_aug, k_aug, v_aug, do_aug)


LRU_CHUNK = 64
LRU_G = 256
SUB = 8


def _row_ids(n):
    return lax.broadcasted_iota(jnp.int32, (n, LRU_G), 0)


def _shift_rows_down(ext, s):
    return pltpu.roll(ext, s, axis=0)[SUB:, :]


def _shift_rows_up(ext, s, n):
    return pltpu.roll(ext, ext.shape[0] - s, axis=0)[:n, :]


def _lru_gates(u, wa_ref, ba_ref, wx_ref, bx_ref, sp):
    ub = u.astype(bf16)
    r = _sigmoid(jnp.dot(ub, wa_ref[...], preferred_element_type=f32) + ba_ref[...])
    gi = _sigmoid(jnp.dot(ub, wx_ref[...], preferred_element_type=f32) + bx_ref[...])
    log_a = -LRU_C * r * sp
    a = jnp.exp(log_a)
    s = jnp.sqrt(_one_minus_exp(2.0 * log_a))
    return r, gi, a, s


def _conv_window(lx_ref, r0, ci):
    cur = lx_ref[pl.ds(r0, LRU_CHUNK), :]
    p0 = pl.multiple_of(jnp.maximum(r0 - SUB, 0), SUB)
    prev = jnp.where(ci > 0, lx_ref[pl.ds(p0, SUB), :], 0.0)
    return cur, jnp.concatenate([prev, cur], axis=0)


def _lru_fwd(zl, conv_w, conv_b, wa, ba, wx, bx, lam, *, name, host=None):
    t = zl.shape[0]
    n_chunk = t // LRU_CHUNK

    def body(lx_ref, lg_ref, cw_ref, cb_ref, wa_ref, ba_ref, wx_ref, bx_ref, lam_ref, u_ref, h_ref, y_ref):
        sp = _softplus(-lam_ref[...])
        rows = _row_ids(SUB)

        def chunk(ci, hc):
            r0 = pl.multiple_of(ci * LRU_CHUNK, LRU_CHUNK)
            cur, ext = _conv_window(lx_ref, r0, ci)
            u = cb_ref[...] + cw_ref[3:4, :] * cur
            for k in range(3):
                u = u + cw_ref[k:k + 1, :] * _shift_rows_down(ext, 3 - k)
            r, gi, a, s = _lru_gates(u, wa_ref, ba_ref, wx_ref, bx_ref, sp)
            b = s * (gi * u)
            tiles = []
            for q in range(LRU_CHUNK // SUB):
                ta = a[SUB * q:SUB * (q + 1), :]
                tb = b[SUB * q:SUB * (q + 1), :]
                for d in (1, 2, 4):
                    a_sh = jnp.where(rows >= d, pltpu.roll(ta, d, axis=0), 1.0)
                    b_sh = jnp.where(rows >= d, pltpu.roll(tb, d, axis=0), 0.0)
                    tb = ta * b_sh + tb
                    ta = ta * a_sh
                hq = tb + ta * hc
                hc = hq[SUB - 1:SUB, :]
                tiles.append(hq)
            h = jnp.concatenate(tiles, axis=0)
            u_ref[pl.ds(r0, LRU_CHUNK), :] = u
            h_ref[pl.ds(r0, LRU_CHUNK), :] = h
            gel, _ = _gelu_and_grad(lg_ref[pl.ds(r0, LRU_CHUNK), :])
            y_ref[pl.ds(r0, LRU_CHUNK), :] = gel * h
            return hc

        lax.fori_loop(0, n_chunk, chunk, jnp.zeros((1, LRU_G), f32))

    seq = lambda cb: pl.BlockSpec((t, LRU_G), lambda c, cb=cb: (0, c + cb))
    rowc = pl.BlockSpec((1, LRU_G), lambda c: (0, c))
    diag = pl.BlockSpec((LRU_G, LRU_G), lambda c: (c, c))
    out = jax.ShapeDtypeStruct((t, LRU_W), f32)
    return _hosted_call(
        host, body, name=name, grid=(LRU_W // LRU_G,),
        in_specs=[seq(0), seq(LRU_W // LRU_G), pl.BlockSpec((4, LRU_G), lambda c: (0, c)),
                  rowc, diag, rowc, diag, rowc, rowc],
        out_specs=[seq(0)] * 3, out_shape=[out] * 3,
        compiler_params=_params(("arbitrary",)),
    )(zl, zl, conv_w, conv_b, wa, ba, wx, bx, lam)


def _lru_bwd(dmix, zl, u_all, h_all, conv_w, wa, ba, wx, bx, lam, *, name, host=None):
    t = zl.shape[0]
    n_chunk = t // LRU_CHUNK

    def body(dy_ref, lx_ref, lg_ref, u_ref, h_ref, cw_ref, wa_ref, ba_ref, wx_ref, bx_ref, lam_ref,
             dlx_ref, dlg_ref, dcw_ref, dcb_ref, dba_ref, dbx_ref, dlam_ref, dwa_ref, dwx_ref, dpr_s, dpx_s):
        lam_v = lam_ref[...]
        sp = _softplus(-lam_v)
        rows = _row_ids(SUB)
        rows_c = _row_ids(LRU_CHUNK)
        zero_row = jnp.zeros((1, LRU_G), f32)

        def chunk(step, carry):
            dh_c, a_next0, du_next, dsp, dba, dbx, dcb, dw0, dw1, dw2, dw3 = carry
            ci = n_chunk - 1 - step
            r0 = pl.multiple_of(ci * LRU_CHUNK, LRU_CHUNK)
            sl = pl.ds(r0, LRU_CHUNK)
            u = u_ref[sl, :]
            r, gi, a, s = _lru_gates(u, wa_ref, ba_ref, wx_ref, bx_ref, sp)
            h = h_ref[sl, :]
            p0 = pl.multiple_of(jnp.maximum(r0 - SUB, 0), SUB)
            h_before = jnp.where(ci > 0, h_ref[pl.ds(p0, SUB), :], 0.0)[SUB - 1:SUB, :]
            h_prev = jnp.where(rows_c == 0, h_before, pltpu.roll(h, 1, axis=0))
            gel, dgel = _gelu_and_grad(lg_ref[sl, :])
            dy = dy_ref[sl, :]
            dlg_ref[sl, :] = (dy * h * dgel).astype(bf16)
            g_in = dy * gel
            a_next = jnp.where(rows_c == LRU_CHUNK - 1, a_next0, pltpu.roll(a, LRU_CHUNK - 1, axis=0))
            tiles = [None] * (LRU_CHUNK // SUB)
            for q in reversed(range(LRU_CHUNK // SUB)):
                ta = a_next[SUB * q:SUB * (q + 1), :]
                tb = g_in[SUB * q:SUB * (q + 1), :]
                for d in (1, 2, 4):
                    a_sh = jnp.where(rows < SUB - d, pltpu.roll(ta, SUB - d, axis=0), 1.0)
                    b_sh = jnp.where(rows < SUB - d, pltpu.roll(tb, SUB - d, axis=0), 0.0)
                    tb = ta * b_sh + tb
                    ta = ta * a_sh
                dhq = tb + ta * dh_c
                dh_c = dhq[0:1, :]
                tiles[q] = dhq
            dh = jnp.concatenate(tiles, axis=0)
            da = dh * h_prev
            ds = dh * gi * u
            dgi = dh * s * u
            du = dh * s * gi
            dlog_a = da * a - ds * (a * a) / s
            dr = dlog_a * (-LRU_C * sp)
            dsp = dsp + jnp.sum(dlog_a * (-LRU_C * r), axis=0, keepdims=True)
            dpr = dr * r * (1.0 - r)
            dpx = dgi * gi * (1.0 - gi)
            dprb = dpr.astype(bf16)
            dpxb = dpx.astype(bf16)
            dpr_s[sl, :] = dprb
            dpx_s[sl, :] = dpxb
            du = du + (lax.dot_general(dprb, wa_ref[...], _NT, preferred_element_type=f32)
                       + lax.dot_general(dpxb, wx_ref[...], _NT, preferred_element_type=f32))
            dba = dba + jnp.sum(dpr, axis=0, keepdims=True)
            dbx = dbx + jnp.sum(dpx, axis=0, keepdims=True)
            dcb = dcb + jnp.sum(du, axis=0, keepdims=True)
            du_ext = jnp.concatenate([du, du_next], axis=0)
            dlx = cw_ref[3:4, :] * du
            for k in range(3):
                dlx = dlx + cw_ref[k:k + 1, :] * _shift_rows_up(du_ext, 3 - k, LRU_CHUNK)
            dlx_ref[sl, :] = dlx.astype(bf16)
            cur, ext = _conv_window(lx_ref, r0, ci)
            dws = [dw0, dw1, dw2, dw3 + jnp.sum(du * cur, axis=0, keepdims=True)]
            for k in range(3):
                dws[k] = dws[k] + jnp.sum(du * _shift_rows_down(ext, 3 - k), axis=0, keepdims=True)
            return (dh_c, a[0:1, :], du[0:SUB, :], dsp, dba, dbx, dcb, dws[0], dws[1], dws[2], dws[3])

        init = (zero_row, zero_row, jnp.zeros((SUB, LRU_G), f32)) + (zero_row,) * 8
        out = lax.fori_loop(0, n_chunk, chunk, init)
        _, _, _, dsp, dba, dbx, dcb, dw0, dw1, dw2, dw3 = out
        dlam_ref[...] = dsp * (-_sigmoid(-lam_v))
        dba_ref[...] = dba
        dbx_ref[...] = dbx
        dcb_ref[...] = dcb
        dcw_ref[...] = jnp.concatenate([dw0, dw1, dw2, dw3], axis=0)
        ub = u_ref[...].astype(bf16)
        dwa_ref[...] = lax.dot_general(ub, dpr_s[...], _TN, preferred_element_type=f32)
        dwx_ref[...] = lax.dot_general(ub, dpx_s[...], _TN, preferred_element_type=f32)

    seq = lambda cb: pl.BlockSpec((t, LRU_G), lambda c, cb=cb: (0, c + cb))
    rowc = pl.BlockSpec((1, LRU_G), lambda c: (0, c))
    diag = pl.BlockSpec((LRU_G, LRU_G), lambda c: (c, c))
    gate_out = pl.BlockSpec((None, LRU_G, LRU_G), lambda c: (c, 0, 0))
    row_shape = jax.ShapeDtypeStruct((1, LRU_W), f32)
    return _hosted_call(
        host, body, name=name, grid=(LRU_W // LRU_G,),
        in_specs=[seq(LRU_W // LRU_G), seq(0), seq(LRU_W // LRU_G), seq(0), seq(0),
                  pl.BlockSpec((4, LRU_G), lambda c: (0, c)),
                  diag, rowc, diag, rowc, rowc],
        out_specs=[seq(0), seq(0), pl.BlockSpec((4, LRU_G), lambda c: (0, c)), rowc, rowc, rowc, rowc,
                   gate_out, gate_out],
        out_shape=[jax.ShapeDtypeStruct((t, LRU_W), bf16)] * 2
        + [jax.ShapeDtypeStruct((4, LRU_W), f32)] + [row_shape] * 4
        + [jax.ShapeDtypeStruct((LRU_W // LRU_G, LRU_G, LRU_G), f32)] * 2,
        scratch_shapes=[pltpu.VMEM((t, LRU_G), bf16), pltpu.VMEM((t, LRU_G), bf16)],
        compiler_params=_params(("arbitrary",)),
    )(dmix, zl, zl, u_all, h_all, conv_w, wa, ba, wx, bx, lam)


def _pack_rows(a):
    flat = a.reshape(-1)
    rows = -(-flat.shape[0] // LANES)
    return jnp.pad(flat, (0, rows * LANES - flat.shape[0])).reshape(rows, LANES)


W_IN_PAD = 21 * LANES


def _w_in_join(blocks, *, name):
    tm = 256

    def body(b_ref, o_ref):
        o_ref[:, IN_COLS:] = jnp.zeros((tm, W_IN_PAD - IN_COLS), bf16)
        for q in range(N_DEV):
            o_ref[:, IN_SHARD * q:IN_SHARD * (q + 1)] = b_ref[q]

    return pl.pallas_call(
        body, name=name, grid=(D_MODEL // tm,),
        in_specs=[pl.BlockSpec((N_DEV, tm, IN_SHARD), lambda i: (0, i, 0))],
        out_specs=pl.BlockSpec((tm, W_IN_PAD), lambda i: (i, 0)),
        out_shape=jax.ShapeDtypeStruct((D_MODEL, W_IN_PAD), bf16), compiler_params=_params(("arbitrary",)),
    )(blocks)


def _w_in_split(main, fg, *, name):
    tm = 256
    n_main = main.shape[0]

    def body(m_ref, f_ref, o_ref):
        full = jnp.concatenate([m_ref[n] for n in range(n_main)] + [f_ref[...]], axis=1)
        for q in range(N_DEV):
            o_ref[q] = full[:, IN_SHARD * q:IN_SHARD * (q + 1)]

    return pl.pallas_call(
        body, name=name, grid=(D_MODEL // tm,),
        in_specs=[pl.BlockSpec((n_main, tm, 512), lambda i: (0, i, 0)), pl.BlockSpec((tm, LANES), lambda i: (i, 0))],
        out_specs=pl.BlockSpec((N_DEV, tm, IN_SHARD), lambda i: (0, i, 0)),
        out_shape=jax.ShapeDtypeStruct((N_DEV, D_MODEL, IN_SHARD), bf16), compiler_params=_params(("arbitrary",)),
    )(main, fg)


def _block_diag(w):
    eye = jnp.eye(HEADS, dtype=w.dtype)
    return jnp.einsum("hij,hk->hikj", w, eye).reshape(LRU_W, LRU_W)


def _diag_blocks(dw):
    per = dw.shape[1] // HEAD_D
    blocks = [dw[:, HEAD_D * b:HEAD_D * (b + 1), HEAD_D * b:HEAD_D * (b + 1)] for b in range(per)]
    return jnp.stack(blocks, axis=1).reshape(HEADS, HEAD_D, HEAD_D)


def _local_step(x, target, sent, small, *, tm=512, tm_ffn=1024):
    ln1 = (small["ln1_g"], small["ln1_b"])
    ln2 = (small["ln2_g"], small["ln2_b"])
    ln3 = (small["ln3_g"], small["ln3_b"])

    xh1, rs1, hg1, hu1, wg1, wu1, wd1, w_in_g, w_out_g, conv_w_g = _ffn1_fwd_gathering(
        x, (sent["ffn1_w_gate"], sent["ffn1_w_up"], sent["ffn1_w_down"]),
        _Exchange([sent["w_in"], sent["w_out"], sent["conv_w"]], gather=True), tm=tm_ffn, name="ffn1_fwd")
    w_in = _w_in_join(w_in_g, name="w_in_join")
    w_out = w_out_g.reshape(D_MODEL, D_MODEL)
    conv_w = conv_w_g.transpose(1, 0, 2).reshape(4, LRU_W)
    qkv, zl, zfg = _in_proj(xh1, ln1[0], ln1[1], w_in, tm=tm, name="in_proj")
    bfg = jnp.pad(small["b_forget"], ((0, 0), (0, LANES - HEADS)))
    cum = _cum_fwd(zfg, bfg, name="cum_fwd")
    q_aug, k_aug, v_aug = _attn_prep_fwd(qkv, cum, tm=tm, name="attn_prep_fwd")
    o, lse, wg2, wu2 = _attn_fwd2(q_aug, k_aug, v_aug, name="attn_fwd",
                                  host=_Exchange([sent["ffn2_w_gate"], sent["ffn2_w_up"]], gather=True))
    wa_bd = _block_diag(small["rg_wa"]).astype(bf16)
    wx_bd = _block_diag(small["rg_wx"]).astype(bf16)
    ba = small["rg_ba"].reshape(1, LRU_W)
    bx = small["rg_bx"].reshape(1, LRU_W)
    u, h, lru, wd2 = _lru_fwd(zl, conv_w, small["conv_b"], wa_bd, ba, wx_bd, bx, small["lru_lambda"],
                              name="lru_fwd", host=_Exchange([sent["ffn2_w_down"]], gather=True))
    xh2, rs2 = _mmln([(o, 0, FOX_W, w_out, 0, D_MODEL, "nn"), (lru, 0, LRU_W, w_out, 1, D_MODEL, "nn")],
                     tm=tm, name="mix_fwd", resid=("affine", xh1) + ln1, resid_scale=ALPHA, epi="ln_fwd")
    xh3, rs3, hg2, hu2 = _ffn_fwd(xh2, ln2[0], ln2[1], wg2, wu2, wd2, tm=tm_ffn, name="ffn2_fwd")

    dpre3, sq_rows, g_ln3g, g_ln3b = _loss_bwd(xh3, rs3, ln3[0], ln3[1], target, tm=tm, name="loss_bwd")
    dpre2, g_ln2g, g_ln2b, dhg2, dhu2, a2 = _ffn_bwd(dpre3, hg2, hu2, wg2, wu2, wd2,
                                                     (xh2, rs2, ln2[0]), tm=tm_ffn, name="ffn2_bwd")
    wgrad = dict(out_dtype=bf16, tm=D_MODEL, mb=1, tn=FF_TILE, nb=4, tk=512, pair=True)
    wdgrad = dict(out_dtype=bf16, tm=512, mb=4, tn=D_MODEL, nb=1, tk=512, out_scale=0.5, pair=True)
    between_chips = functools.partial(_Exchange, gather=False, chips=True)
    g_wg2 = _mm_tn(xh2, dhg2, name="g_wg2", affine=ln2, **wgrad)
    g_wu2 = _mm_tn(xh2, dhu2, name="g_wu2", affine=ln2, **wgrad)
    g_wd2 = _mm_tn(a2, dpre3, name="g_wd2", **wdgrad)

    dmix = _mmln([(dpre2, 0, D_MODEL, w_out, 0, D_MODEL, "nt")], tm=tm, name="dmix_bwd")
    g_wout_a = _mm(o, dpre2, mode="tn", out_dtype=bf16, tm=512, tn=D_MODEL, tk=512, name="g_wout_fox")
    g_wout_b = _mm(lru, dpre2, mode="tn", out_dtype=bf16, tm=512, tn=D_MODEL, tk=512, name="g_wout_lru")
    g_wout_blocked = jnp.concatenate([g_wout_a, g_wout_b], axis=0).reshape(N_DEV, D_MODEL // N_DEV, D_MODEL)
    dlx, dlg, g_cw, g_cb, g_ba, g_bx, g_lam, g_wa4, g_wx4, p_wg2, p_wout = _lru_bwd(
        dmix, zl, u, h, conv_w, wa_bd, ba, wx_bd, bx, small["lru_lambda"], name="lru_bwd",
        host=_Hosts(between_chips([g_wg2]), _Exchange([g_wout_blocked], gather=False)))
    qb_aug, do_aug = _attn_prep_bwd(qkv, cum, lse, dmix, o, tm=tm, name="attn_prep_bwd")
    dq, dcum_q, dk, dv, dcum_k, p_wu2, p_wd2 = _attn_bwd(qb_aug, k_aug, v_aug, do_aug, name="attn_bwd",
                                                         host=between_chips([g_wu2, g_wd2]))
    dfg, g_bf = _cum_bwd(dcum_q, dcum_k, zfg, bfg, name="cum_bwd")

    dz = [(dq, 0, 512), (dk, 1, 512), (dv, 2, 512), (dlx, 3, 512), (dlg, 4, 512), (dfg, 20, LANES)]
    dpre1, g_ln1g, g_ln1b = _mmln(
        [(arr, 0, w, w_in, cb, w, "nt") for (arr, cb, w) in dz],
        tm=tm, name="dx1_bwd", resid=("plain", dpre2), resid_scale=ALPHA, epi="ln_bwd", ln=(xh1, rs1, ln1[0]))
    g_win_main = _mm_tn(xh1, [arr for arr, _, _ in dz[:5]], out_dtype=bf16, tm=D_MODEL, mb=1, tn=512, nb=5, tk=512,
                        name="g_win", affine=ln1, out_blocked=True)
    g_win_fg = _mm(xh1, dfg, mode="tn", out_dtype=bf16, tm=D_MODEL, tn=LANES, tk=512, name="g_win_fg", affine=ln1)
    g_win_blocked = _w_in_split(g_win_main, g_win_fg, name="w_in_split")
    dhg1, dhu1, a1, p_win = _ffn_bwd_act(dpre1, hg1, hu1, wd1, tm=tm_ffn, name="ffn1_bwd_act",
                                         host=_Exchange([g_win_blocked], gather=False))
    small_g = {
        "ln1_g": g_ln1g, "ln1_b": g_ln1b, "b_forget": g_bf[:, :HEADS], "conv_w": g_cw, "conv_b": g_cb,
        "rg_wa": _diag_blocks(g_wa4), "rg_ba": g_ba.reshape(HEADS, HEAD_D),
        "rg_wx": _diag_blocks(g_wx4), "rg_bx": g_bx.reshape(HEADS, HEAD_D), "lru_lambda": g_lam,
        "ln2_g": g_ln2g, "ln2_b": g_ln2b, "ln3_g": g_ln3g, "ln3_b": g_ln3b,
    }
    small_g["loss"] = (0.5 / D_MODEL) * jnp.sum(sq_rows, keepdims=True)
    pieces = [_pack_rows(small_g[n]) for n in PACKED]
    packed = jnp.concatenate(pieces + [jnp.zeros((PACK_ROWS - sum(p.shape[0] for p in pieces), LANES), f32)])
    g_wg1, all_packed = _mm_tn(x, dhg1, name="g_wg1", host=_Exchange([packed], gather=True), **wgrad)
    g_wu1, p_wg1 = _mm_tn(x, dhu1, name="g_wu1", host=between_chips([g_wg1]), **wgrad)
    g_wd1, p_wu1 = _mm_tn(a1, dpre1, name="g_wd1", host=between_chips([g_wu1]), **wdgrad)
    grad_x, p_wd1 = _ffn_bwd_dx(dpre1, dhg1, dhu1, wg1, wu1, tm=tm_ffn, name="ffn1_bwd_dx",
                                host=between_chips([g_wd1]))
    parts = {
        "ffn1_w_gate": p_wg1, "ffn1_w_up": p_wu1, "ffn1_w_down": p_wd1, "w_in": p_win, "w_out": p_wout,
        "ffn2_w_gate": p_wg2, "ffn2_w_up": p_wu2, "ffn2_w_down": p_wd2,
    }
    return grad_x, parts, all_packed, {n: small_g[n].shape for n in PACKED}


def _adam_math(w, g, m, v):
    m2 = ADAM_B1 * m + (1.0 - ADAM_B1) * g
    v2 = ADAM_B2 * v + (1.0 - ADAM_B2) * (g * g)
    m_hat = m2 / (1.0 - ADAM_B1 ** ADAM_STEP)
    v_hat = v2 / (1.0 - ADAM_B2 ** ADAM_STEP)
    delta = -ADAM_LR * (m_hat / (jnp.sqrt(v_hat) + ADAM_EPS) + ADAM_WD * w)
    return delta, m2, v2


ADAM_TILE_ELEMS = 128 * 1024


def _adamw_big(parts, w, m, v, *, name):
    _, r, c = w.shape
    n_parts = parts.shape[0]
    tr = max(d for d in range(8, r + 1, 8) if r % d == 0 and d * c <= ADAM_TILE_ELEMS)

    def body(p_ref, w_ref, m_ref, v_ref, g_ref, d_ref, m2_ref, v2_ref):
        g = p_ref[0].astype(f32)
        for q in range(1, n_parts):
            g = g + p_ref[q].astype(f32)
        d, m2, v2 = _adam_math(w_ref[...], g, m_ref[...], v_ref[...])
        g_ref[...] = g
        d_ref[...] = d
        m2_ref[...] = m2
        v2_ref[...] = v2

    blk = pl.BlockSpec((None, tr, c), lambda i: (0, i, 0))
    return pl.pallas_call(
        body, name=name, grid=(r // tr,),
        in_specs=[pl.BlockSpec((n_parts, tr, c), lambda i: (0, i, 0)), blk, blk, blk],
        out_specs=[blk] * 4, out_shape=[jax.ShapeDtypeStruct((1, r, c), f32)] * 4,
        compiler_params=_params(("arbitrary",)),
    )(parts, w, m, v)


def _adamw_small(items, *, name):
    n = len(items)

    def body(*refs):
        ins, outs = refs[:4 * n], refs[4 * n:]
        for k in range(n):
            g, w, m, v = (ins[4 * k + q][...] for q in range(4))
            d, m2, v2 = _adam_math(w, g, m, v)
            outs[3 * k][...] = d
            outs[3 * k + 1][...] = m2
            outs[3 * k + 2][...] = v2

    vm = pl.BlockSpec(memory_space=pltpu.VMEM)
    flat = [a for item in items for a in item]
    out_shape = [jax.ShapeDtypeStruct(item[1].shape, f32) for item in items for _ in range(3)]
    return pl.pallas_call(
        body, name=name, in_specs=[vm] * (4 * n), out_specs=[vm] * (3 * n), out_shape=out_shape,
    )(*flat)


def _sum_parts(parts, *, name):
    def body(p_ref, o_ref):
        acc = p_ref[0]
        for q in range(1, N_DEV):
            acc = acc + p_ref[q]
        o_ref[...] = acc

    vm = pl.BlockSpec(memory_space=pltpu.VMEM)
    return pl.pallas_call(
        body, name=name, in_specs=[vm], out_specs=vm, out_shape=jax.ShapeDtypeStruct(parts.shape[1:], f32),
    )(parts)


WEIGHTS = ["ffn1_w_gate", "ffn1_w_up", "ffn1_w_down", "ln1_g", "ln1_b", "w_in", "b_forget", "conv_w", "conv_b",
           "rg_wa", "rg_ba", "rg_wx", "rg_bx", "lru_lambda", "w_out", "ln2_g", "ln2_b",
           "ffn2_w_gate", "ffn2_w_up", "ffn2_w_down", "ln3_g", "ln3_b"]
BIG = ["ffn1_w_gate", "ffn1_w_up", "ffn1_w_down", "w_in", "w_out", "ffn2_w_gate", "ffn2_w_up", "ffn2_w_down"]
PACKED = ["ln1_g", "ln1_b", "ln2_g", "ln2_b", "ln3_g", "ln3_b", "conv_b", "rg_ba", "rg_bx", "lru_lambda",
          "conv_w", "rg_wa", "rg_wx", "b_forget", "loss"]
PACK_ROWS = 600


def _two_d(a):
    return a.reshape((-1, a.shape[-1]))


def _transport(a):
    return _two_d(a)


def kernel(x, ffn1_w_gate, ffn1_w_up, ffn1_w_down, ln1_g, ln1_b, w_in, b_forget, conv_w, conv_b, rg_wa, rg_ba, rg_wx, rg_bx, lru_lambda, w_out, ln2_g, ln2_b, ffn2_w_gate, ffn2_w_up, ffn2_w_down, ln3_g, ln3_b, loss_target, m_ffn1_w_gate, m_ffn1_w_up, m_ffn1_w_down, m_ln1_g, m_ln1_b, m_w_in, m_b_forget, m_conv_w, m_conv_b, m_rg_wa, m_rg_ba, m_rg_wx, m_rg_bx, m_lru_lambda, m_w_out, m_ln2_g, m_ln2_b, m_ffn2_w_gate, m_ffn2_w_up, m_ffn2_w_down, m_ln3_g, m_ln3_b, v_ffn1_w_gate, v_ffn1_w_up, v_ffn1_w_down, v_ln1_g, v_ln1_b, v_w_in, v_b_forget, v_conv_w, v_conv_b, v_rg_wa, v_rg_ba, v_rg_wx, v_rg_bx, v_lru_lambda, v_w_out, v_ln2_g, v_ln2_b, v_ffn2_w_gate, v_ffn2_w_up, v_ffn2_w_down, v_ln3_g, v_ln3_b):
    w_args = (ffn1_w_gate, ffn1_w_up, ffn1_w_down, ln1_g, ln1_b, w_in, b_forget, conv_w, conv_b, rg_wa, rg_ba, rg_wx, rg_bx, lru_lambda, w_out, ln2_g, ln2_b, ffn2_w_gate, ffn2_w_up, ffn2_w_down, ln3_g, ln3_b)
    m_args = (m_ffn1_w_gate, m_ffn1_w_up, m_ffn1_w_down, m_ln1_g, m_ln1_b, m_w_in, m_b_forget, m_conv_w, m_conv_b, m_rg_wa, m_rg_ba, m_rg_wx, m_rg_bx, m_lru_lambda, m_w_out, m_ln2_g, m_ln2_b, m_ffn2_w_gate, m_ffn2_w_up, m_ffn2_w_down, m_ln3_g, m_ln3_b)
    v_args = (v_ffn1_w_gate, v_ffn1_w_up, v_ffn1_w_down, v_ln1_g, v_ln1_b, v_w_in, v_b_forget, v_conv_w, v_conv_b, v_rg_wa, v_rg_ba, v_rg_wx, v_rg_bx, v_lru_lambda, v_w_out, v_ln2_g, v_ln2_b, v_ffn2_w_gate, v_ffn2_w_up, v_ffn2_w_down, v_ln3_g, v_ln3_b)
    w = dict(zip(WEIGHTS, w_args))
    m = dict(zip(WEIGHTS, m_args))
    v = dict(zip(WEIGHTS, v_args))
    me = 4 * lax.axis_index("x") + 2 * lax.axis_index("y") + lax.axis_index("c")

    sent = {n: _transport(w[n]).astype(bf16) for n in BIG}
    sent["conv_w"] = _two_d(w["conv_w"])
    small = {n: w[n] for n in ("ln1_g", "ln1_b", "ln2_g", "ln2_b", "ln3_g", "ln3_b", "b_forget", "conv_b",
                               "lru_lambda")}
    small.update({n: w[n][0] for n in ("rg_wa", "rg_ba", "rg_wx", "rg_bx")})

    grad_x, parts, all_packed, small_shapes = _local_step(x[0], loss_target[0], sent, small)

    total = _sum_parts(all_packed, name="sum_small_grads")
    grads, off = {}, 0
    for n in PACKED:
        size = math.prod(small_shapes[n])
        rows = -(-size // LANES)
        grads[n] = total[off:off + rows].reshape(-1)[:size].reshape(small_shapes[n])
        off += rows
    loss = grads.pop("loss").reshape(())
    grads["conv_w"] = lax.dynamic_slice_in_dim(grads["conv_w"], me * (LRU_W // N_DEV), LRU_W // N_DEV, axis=1)

    delta, new_m, new_v = {}, {}, {}
    for n in BIG:
        g, d, m2, v2 = _adamw_big(parts[n], w[n], m[n], v[n], name="adamw_" + n)
        grads[n], delta[n], new_m[n], new_v[n] = g, d, m2, v2
    small_names = [n for n in WEIGHTS if n not in BIG]
    outs = _adamw_small([(_two_d(grads[n]), _two_d(w[n]), _two_d(m[n]), _two_d(v[n])) for n in small_names],
                        name="adamw_small")
    for k, n in enumerate(small_names):
        delta[n], new_m[n], new_v[n] = outs[3 * k], outs[3 * k + 1], outs[3 * k + 2]

    def shaped(d):
        return [d[n].reshape(w[n].shape) for n in WEIGHTS]

    return (loss, grad_x[None], *shaped(grads), *shaped(delta), *shaped(new_m), *shaped(new_v))
```

```python
import functools
import math

import jax
import jax.numpy as jnp
from jax import lax
from jax.experimental import pallas as pl
from jax.experimental.pallas import tpu as pltpu

f32 = jnp.float32
bf16 = jnp.bfloat16

N_DEV = 8
D_MODEL = 1024
D_FF = 4096
FF_TILE = D_FF // N_DEV
FOX_W = 512
LRU_W = 512
HEADS = 8
HEAD_D = 64
IN_COLS = 2568
IN_SHARD = IN_COLS // N_DEV
LANES = 128
LN_EPS = 1e-5
ALPHA = 2.0 ** 0.25
ATT_SCALE = 1.0 / math.sqrt(HEAD_D)
LRU_C = 8.0
NEG_BIG = -1e30

ADAM_LR = 0.001
ADAM_B1 = 0.9
ADAM_B2 = 0.999
ADAM_EPS = 1e-08
ADAM_WD = 0.01
ADAM_STEP = 10

VMEM_LIMIT = 56 * 1024 * 1024
MESH_T = pl.DeviceIdType.MESH


def _params(sem, **kw):
    return pltpu.CompilerParams(dimension_semantics=sem, vmem_limit_bytes=VMEM_LIMIT, **kw)


def _sigmoid(x):
    return 1.0 / (1.0 + jnp.exp(-x))


def _sigmoid_tanh(x):
    return 0.5 * jnp.tanh(0.5 * x) + 0.5


def _softplus(x):
    return jnp.maximum(x, 0.0) + jnp.log(1.0 + jnp.exp(-jnp.abs(x)))


def _one_minus_exp(x):
    series = -x * (1.0 + x * (0.5 + x * (1.0 / 6 + x * (1.0 / 24 + x * (1.0 / 120 + x * (1.0 / 720))))))
    return jnp.where(x > -0.125, series, 1.0 - jnp.exp(x))


_GELU_C = math.sqrt(2.0 / math.pi)


def _gelu_and_grad(x):
    inner = _GELU_C * (x + 0.044715 * x * x * x)
    t = jnp.tanh(inner)
    g = 0.5 * x * (1.0 + t)
    dg = 0.5 * (1.0 + t) + 0.5 * x * (1.0 - t * t) * _GELU_C * (1.0 + 3 * 0.044715 * x * x)
    return g, dg


def _ln_fwd_tile(pre):
    mu = jnp.mean(pre, axis=-1, keepdims=True)
    xc = pre - mu
    var = jnp.mean(xc * xc, axis=-1, keepdims=True)
    rstd = lax.rsqrt(var + LN_EPS)
    return xc * rstd, rstd


def _ln_bwd_tile(dy, xhat, rstd, g):
    dyg = dy * g
    m1 = jnp.mean(dyg, axis=-1, keepdims=True)
    m2 = jnp.mean(dyg * xhat, axis=-1, keepdims=True)
    dpre = rstd * (dyg - m1 - xhat * m2)
    return dpre, jnp.sum(dy * xhat, axis=0, keepdims=True), jnp.sum(dy, axis=0, keepdims=True)


_NT = (((1,), (1,)), ((), ()))
_TN = (((0,), (0,)), ((), ()))


class _Exchange:
    def __init__(self, arrs, gather, chips=False):
        self.arrs, self.gather, self.n, self.chips = list(arrs), gather, len(arrs), chips

    def out_shape(self):
        return [jax.ShapeDtypeStruct(((N_DEV,) + a.shape) if self.gather else a.shape, a.dtype) for a in self.arrs]

    def scratch(self):
        n_remote = self.n * (N_DEV - 1)
        return [pltpu.SemaphoreType.DMA((n_remote,)), pltpu.SemaphoreType.DMA((n_remote,)),
                pltpu.SemaphoreType.DMA((self.n,))]

    def copies(self, ins, outs, sems):
        send_sems, recv_sems, local_sems = sems
        x, y, c = lax.axis_index("x"), lax.axis_index("y"), lax.axis_index("c")
        me = 2 * x + y if self.chips else 4 * x + 2 * y + c
        out = []
        for k in range(self.n):
            for d in (range(2, N_DEV, 2) if self.chips else range(1, N_DEV)):
                px = 1 - x if d & 4 else x
                py = 1 - y if d & 2 else y
                pc = 1 - c if d & 1 else c
                sem = k * (N_DEV - 1) + d - 1
                out.append(pltpu.make_async_remote_copy(
                    src_ref=ins[k].at[2 * px + py if self.chips else 4 * px + 2 * py + pc], dst_ref=outs[k].at[me],
                    send_sem=send_sems.at[sem], recv_sem=recv_sems.at[sem],
                    device_id=(px, py, pc), device_id_type=MESH_T))
            out.append(pltpu.make_async_copy(ins[k].at[me], outs[k].at[me], local_sems.at[k]))
        return out

    def gather_copies(self, ins, outs, sems):
        send_sems, recv_sems, local_sems = sems
        x, y, c = lax.axis_index("x"), lax.axis_index("y"), lax.axis_index("c")
        sibling = (x, y, 1 - c)
        chips = [(1 - x, y), (x, 1 - y), (1 - x, 1 - y)]
        out = []
        for k in range(self.n):
            def copy(s, block, to, src=None, k=k):
                rows = outs[k].at[4 * block[0] + 2 * block[1] + block[2]]
                sem = k * (N_DEV - 1) + s
                return pltpu.make_async_remote_copy(
                    src_ref=rows if src is None else src, dst_ref=rows, send_sem=send_sems.at[sem],
                    recv_sem=recv_sems.at[sem], device_id=to, device_id_type=MESH_T)

            first = [copy(0, (x, y, c), sibling, src=ins[k])]
            first += [copy(1 + q, (x, y, c), (*chip, c), src=ins[k]) for q, chip in enumerate(chips)]
            passed = [copy(4 + q, (*chip, c), sibling) for q, chip in enumerate(chips)]
            own = pltpu.make_async_copy(ins[k], outs[k].at[4 * x + 2 * y + c], local_sems.at[k])
            out.append((first, passed, own, copy))
        return out, sibling, chips, (x, y, c)

    def start(self, ins, outs, sems):
        if not self.gather:
            for cp in self.copies(ins, outs, sems):
                cp.start()
            return
        per_array, _, _, _ = self.gather_copies(ins, outs, sems)
        for first, _, own, _ in per_array:
            own.start()
            for cp in first:
                cp.start()

    def relay(self, ins, outs, sems):
        per_array, sibling, chips, (x, y, c) = self.gather_copies(ins, outs, sems)
        for first, passed, own, copy in per_array:
            for q, chip in enumerate(chips):
                copy(1 + q, (*chip, c), (x, y, c)).wait_recv()
                passed[q].start()

    def wait(self, ins, outs, sems, relayed=False):
        if not self.gather:
            for cp in self.copies(ins, outs, sems):
                cp.wait()
            return
        if not relayed:
            self.relay(ins, outs, sems)
        per_array, sibling, chips, (x, y, c) = self.gather_copies(ins, outs, sems)
        for first, passed, own, copy in per_array:
            copy(0, sibling, (x, y, c)).wait_recv()
            for q, chip in enumerate(chips):
                copy(4 + q, (*chip, 1 - c), (x, y, c)).wait_recv()
            for cp in first + passed:
                cp.wait_send()
            own.wait()


class _Hosts:
    gather = False

    def __init__(self, *hosts):
        self.hosts = hosts
        self.n = sum(h.n for h in hosts)
        self.arrs = [a for h in hosts for a in h.arrs]

    def out_shape(self):
        return [sh for h in self.hosts for sh in h.out_shape()]

    def scratch(self):
        return [sc for h in self.hosts for sc in h.scratch()]

    def _each(self, ins, outs, sems):
        at = 0
        for k, h in enumerate(self.hosts):
            yield h, ins[at:at + h.n], outs[at:at + h.n], sems[3 * k:3 * k + 3]
            at += h.n

    def start(self, ins, outs, sems):
        for h, h_in, h_out, h_sems in self._each(ins, outs, sems):
            h.start(h_in, h_out, h_sems)

    def wait(self, ins, outs, sems, relayed=False):
        for h, h_in, h_out, h_sems in self._each(ins, outs, sems):
            h.wait(h_in, h_out, h_sems)


def _hosted_call(host, body, *, name, grid, in_specs, out_specs, out_shape, scratch_shapes=(), compiler_params):
    out_specs = list(out_specs) if isinstance(out_specs, (list, tuple)) else [out_specs]
    out_shape = list(out_shape) if isinstance(out_shape, (list, tuple)) else [out_shape]
    if host is None:
        return pl.pallas_call(body, name=name, grid=grid, in_specs=in_specs, out_specs=out_specs,
                              out_shape=out_shape, scratch_shapes=list(scratch_shapes),
                              compiler_params=compiler_params)
    n_in, n_out, n_scr, k = len(in_specs), len(out_shape), len(scratch_shapes), host.n

    def wrapped(*refs):
        ins, h_in = refs[:n_in], refs[n_in:n_in + k]
        outs, h_out = refs[n_in + k:n_in + k + n_out], refs[n_in + k + n_out:n_in + 2 * k + n_out]
        scr, sems = refs[n_in + 2 * k + n_out:n_in + 2 * k + n_out + n_scr], refs[n_in + 2 * k + n_out + n_scr:]
        ids = [pl.program_id(a) for a in range(len(grid))]
        first = functools.reduce(jnp.logical_and, [i == 0 for i in ids])
        last = functools.reduce(jnp.logical_and, [i == g - 1 for i, g in zip(ids, grid)])
        steps = math.prod(grid)
        relay_at = (3 * steps) // 4 if host.gather and steps >= 8 else None

        @pl.when(first)
        def _():
            host.start(h_in, h_out, sems)

        if relay_at is not None:
            coords, rest = [], relay_at
            for g in reversed(grid):
                coords.append(rest % g)
                rest //= g

            @pl.when(functools.reduce(jnp.logical_and, [i == cd for i, cd in zip(ids, reversed(coords))]))
            def _():
                host.relay(h_in, h_out, sems)

        body(*ins, *outs, *scr)

        @pl.when(last)
        def _():
            host.wait(h_in, h_out, sems, relayed=relay_at is not None)

    hbm = pl.BlockSpec(memory_space=pl.ANY)
    call = pl.pallas_call(
        wrapped, name=name, grid=grid, in_specs=list(in_specs) + [hbm] * k, out_specs=out_specs + [hbm] * k,
        out_shape=out_shape + host.out_shape(), scratch_shapes=list(scratch_shapes) + host.scratch(),
        compiler_params=compiler_params)
    return lambda *args: call(*args, *host.arrs)


def _ffn_fwd(xhat, g_in, b_in, wg, wu, wd, *, tm, name, host=None):
    t = xhat.shape[0]
    nj = N_DEV

    def body(x_ref, g_ref, b_ref, wg_ref, wu_ref, wd_ref, xo_ref, rstd_ref, hg_ref, hu_ref, xb, acc):
        j = pl.program_id(1)

        @pl.when(j == 0)
        def _():
            xb[...] = (x_ref[...] * g_ref[...] + b_ref[...]).astype(bf16)
            acc[...] = jnp.zeros_like(acc)

        hg = jnp.dot(xb[...], wg_ref[...], preferred_element_type=f32)
        hu = jnp.dot(xb[...], wu_ref[...], preferred_element_type=f32)
        hg_ref[...] = hg.astype(bf16)
        hu_ref[...] = hu.astype(bf16)
        a = hg * _sigmoid_tanh(hg) * hu
        acc[...] += jnp.dot(a.astype(bf16), wd_ref[...], preferred_element_type=f32)

        @pl.when(j == nj - 1)
        def _():
            x = x_ref[...] * g_ref[...] + b_ref[...]
            xo, rstd = _ln_fwd_tile(ALPHA * x + 0.5 * acc[...])
            xo_ref[...] = xo
            rstd_ref[...] = rstd

    row = pl.BlockSpec((1, D_MODEL), lambda i, j: (0, 0))
    return _hosted_call(
        host, body, name=name, grid=(t // tm, nj),
        in_specs=[pl.BlockSpec((tm, D_MODEL), lambda i, j: (i, 0)), row, row,
                  pl.BlockSpec((None, D_MODEL, FF_TILE), lambda i, j: (j, 0, 0)),
                  pl.BlockSpec((None, D_MODEL, FF_TILE), lambda i, j: (j, 0, 0)),
                  pl.BlockSpec((None, FF_TILE, D_MODEL), lambda i, j: (j, 0, 0))],
        out_specs=[pl.BlockSpec((tm, D_MODEL), lambda i, j: (i, 0)),
                   pl.BlockSpec((tm, 1), lambda i, j: (i, 0)),
                   pl.BlockSpec((tm, FF_TILE), lambda i, j: (i, j)),
                   pl.BlockSpec((tm, FF_TILE), lambda i, j: (i, j))],
        out_shape=[jax.ShapeDtypeStruct((t, D_MODEL), f32), jax.ShapeDtypeStruct((t, 1), f32),
                   jax.ShapeDtypeStruct((t, D_FF), bf16), jax.ShapeDtypeStruct((t, D_FF), bf16)],
        scratch_shapes=[pltpu.VMEM((tm, D_MODEL), bf16), pltpu.VMEM((tm, D_MODEL), f32)],
        compiler_params=_params(("arbitrary", "arbitrary")),
    )(xhat, g_in, b_in, wg, wu, wd)


def _ffn1_fwd_gathering(x, own, extra, *, tm, name):
    t = x.shape[0]
    n_i = t // tm
    n_arr = 3
    k_extra = extra.n
    ex = _Exchange(list(own), gather=True)
    ax, ay, ac = lax.axis_index("x"), lax.axis_index("y"), lax.axis_index("c")
    order = jnp.stack([4 * px + 2 * py + pc for px, py in ((ax, ay), (1 - ax, ay), (ax, 1 - ay), (1 - ax, 1 - ay))
                       for pc in (ac, 1 - ac)]).astype(jnp.int32)
    arrival = [None, (0, None), (1, 0), (4, None), (2, 1), (5, None), (3, 2), (6, None)]

    def body(order_ref, x_ref, *refs):
        w_in, e_in = refs[:n_arr], refs[n_arr:n_arr + k_extra]
        refs = refs[n_arr + k_extra:]
        xo_ref, rstd_ref, hg_ref, hu_ref = refs[:4]
        w_all, e_out = refs[4:4 + n_arr], refs[4 + n_arr:4 + n_arr + k_extra]
        acc, wgb, wub, wdb, fetch_sems, send_sems, recv_sems, local_sems = refs[4 + n_arr + k_extra:12 + n_arr + k_extra]
        e_sems = refs[12 + n_arr + k_extra:]
        bufs = (wgb, wub, wdb)
        s = pl.program_id(0)
        i = pl.program_id(1)
        per_array, sibling, chips, (x_, y_, c_) = ex.gather_copies(w_in, w_all, (send_sems, recv_sems, local_sems))

        def fetch(pos, slot):
            return [pltpu.make_async_copy(w_in[a] if pos == 0 else w_all[a].at[order_ref[pos]],
                                          bufs[a].at[slot], fetch_sems.at[n_arr * slot + a]) for a in range(n_arr)]

        def source_of(pos):
            chip = (x_, y_) if pos < 2 else chips[(pos - 2) // 2]
            return (*chip, c_ if pos % 2 == 0 else 1 - c_)

        @pl.when(jnp.logical_and(s == 0, i == 0))
        def _():
            for q in range(4):
                for first, _, own_copy, _ in per_array:
                    if q == 0:
                        own_copy.start()
                    first[q].start()
            for cp in fetch(0, 0):
                cp.start()
            for cp in fetch(0, 0):
                cp.wait()

        @pl.when(jnp.logical_and(s == N_DEV // 2, i == 0))
        def _():
            extra.start(e_in, e_out, e_sems)

        for pos in range(1, N_DEV):
            @pl.when(jnp.logical_and(s == pos - 1, i == n_i - 1))
            def _(pos=pos):
                sem, passes = arrival[pos]
                for _, passed, _, copy in per_array:
                    copy(sem, source_of(pos), (x_, y_, c_)).wait_recv()
                    if passes is not None:
                        passed[passes].start()
                for cp in fetch(pos, pos % 2):
                    cp.start()

            @pl.when(jnp.logical_and(s == pos, i == 0))
            def _(pos=pos):
                for cp in fetch(pos, pos % 2):
                    cp.wait()

        slot = s % 2
        xb = x_ref[...].astype(bf16)
        hg = jnp.dot(xb, wgb[slot], preferred_element_type=f32)
        hu = jnp.dot(xb, wub[slot], preferred_element_type=f32)
        hg_ref[...] = hg.astype(bf16)
        hu_ref[...] = hu.astype(bf16)
        a = hg * _sigmoid_tanh(hg) * hu
        part = jnp.dot(a.astype(bf16), wdb[slot], preferred_element_type=f32)

        @pl.when(s == 0)
        def _():
            acc[i] = part

        @pl.when(s > 0)
        def _():
            acc[i] += part

        @pl.when(s == N_DEV - 1)
        def _():
            xo, rstd = _ln_fwd_tile(ALPHA * x_ref[...] + 0.5 * acc[i])
            xo_ref[...] = xo
            rstd_ref[...] = rstd

        @pl.when(jnp.logical_and(s == N_DEV - 1, i == n_i - 1))
        def _():
            for first, passed, own_copy, _ in per_array:
                for cp in first + passed:
                    cp.wait_send()
                own_copy.wait()
            extra.wait(e_in, e_out, e_sems)

    hbm = pl.BlockSpec(memory_space=pl.ANY)
    last = N_DEV - 1
    tok_out = pl.BlockSpec((tm, D_MODEL), lambda s, i, o: (jnp.where(s == last, i, 0), 0))
    col_out = pl.BlockSpec((tm, 1), lambda s, i, o: (jnp.where(s == last, i, 0), 0))
    hid = pl.BlockSpec((tm, FF_TILE), lambda s, i, o: (i, o[s]))
    shard_shapes = [(N_DEV,) + w.shape for w in own]
    grid_spec = pltpu.PrefetchScalarGridSpec(
        num_scalar_prefetch=1, grid=(N_DEV, n_i),
        in_specs=[pl.BlockSpec((tm, D_MODEL), lambda s, i, o: (i, 0))] + [hbm] * (n_arr + k_extra),
        out_specs=[tok_out, col_out, hid, hid] + [hbm] * (n_arr + k_extra),
        scratch_shapes=[pltpu.VMEM((n_i, tm, D_MODEL), f32)]
        + [pltpu.VMEM((2,) + w.shape, bf16) for w in own]
        + [pltpu.SemaphoreType.DMA((2 * n_arr,))] + ex.scratch() + extra.scratch())
    res = pl.pallas_call(
        body, name=name, grid_spec=grid_spec,
        out_shape=[jax.ShapeDtypeStruct((t, D_MODEL), f32), jax.ShapeDtypeStruct((t, 1), f32),
                   jax.ShapeDtypeStruct((t, D_FF), bf16), jax.ShapeDtypeStruct((t, D_FF), bf16)]
        + [jax.ShapeDtypeStruct(sh, bf16) for sh in shard_shapes] + extra.out_shape(),
        compiler_params=_params(("arbitrary", "arbitrary")),
    )(order, x, *own, *extra.arrs)
    return res


def _ffn_bwd(dpre, hg, hu, wg, wu, wd, ln_in, *, tm, name, host=None):
    t = dpre.shape[0]
    nj = N_DEV
    with_ln = ln_in is not None

    def body(*refs):
        if with_ln:
            (dp_ref, hg_ref, hu_ref, wg_ref, wu_ref, wd_ref, xh_ref, rs_ref, g_ref,
             dx_ref, gg_ref, gb_ref, dhg_ref, dhu_ref, a_ref, dfb, acc) = refs
        else:
            (dp_ref, hg_ref, hu_ref, wg_ref, wu_ref, wd_ref,
             dx_ref, dhg_ref, dhu_ref, a_ref, dfb, acc) = refs
        i = pl.program_id(0)
        j = pl.program_id(1)

        @pl.when(j == 0)
        def _():
            dfb[...] = (0.5 * dp_ref[...]).astype(bf16)
            acc[...] = jnp.zeros_like(acc)

        da = lax.dot_general(dfb[...], wd_ref[...], _NT, preferred_element_type=f32)
        hgv = hg_ref[...].astype(f32)
        huv = hu_ref[...].astype(f32)
        sg = _sigmoid_tanh(hgv)
        silu = hgv * sg
        a_ref[...] = (silu * huv).astype(bf16)
        dhu = (da * silu).astype(bf16)
        dhg = (da * huv * (sg * (1.0 + hgv * (1.0 - sg)))).astype(bf16)
        dhg_ref[...] = dhg
        dhu_ref[...] = dhu
        acc[...] += (lax.dot_general(dhg, wg_ref[...], _NT, preferred_element_type=f32)
                     + lax.dot_general(dhu, wu_ref[...], _NT, preferred_element_type=f32))

        @pl.when(j == nj - 1)
        def _():
            dx = ALPHA * dp_ref[...] + acc[...]
            if with_ln:
                dprev, gg, gb = _ln_bwd_tile(dx, xh_ref[...], rs_ref[...], g_ref[...])
                dx_ref[...] = dprev

                @pl.when(i == 0)
                def _():
                    gg_ref[...] = gg
                    gb_ref[...] = gb

                @pl.when(i > 0)
                def _():
                    gg_ref[...] += gg
                    gb_ref[...] += gb
            else:
                dx_ref[...] = dx

    tok = pl.BlockSpec((tm, D_MODEL), lambda i, j: (i, 0), pipeline_mode=pl.Buffered(1))
    row = pl.BlockSpec((1, D_MODEL), lambda i, j: (0, 0))
    hid = pl.BlockSpec((tm, FF_TILE), lambda i, j: (i, j))
    in_specs = [tok, hid, hid,
                pl.BlockSpec((None, D_MODEL, FF_TILE), lambda i, j: (j, 0, 0)),
                pl.BlockSpec((None, D_MODEL, FF_TILE), lambda i, j: (j, 0, 0)),
                pl.BlockSpec((None, FF_TILE, D_MODEL), lambda i, j: (j, 0, 0))]
    args = [dpre, hg, hu, wg, wu, wd]
    out_specs = [tok]
    out_shape = [jax.ShapeDtypeStruct((t, D_MODEL), f32)]
    if with_ln:
        in_specs += [tok, pl.BlockSpec((tm, 1), lambda i, j: (i, 0)), row]
        args += list(ln_in)
        out_specs += [row, row]
        out_shape += [jax.ShapeDtypeStruct((1, D_MODEL), f32)] * 2
    out_specs += [hid, hid, hid]
    out_shape += [jax.ShapeDtypeStruct((t, D_FF), bf16)] * 3
    return _hosted_call(
        host, body, name=name, grid=(t // tm, nj), in_specs=in_specs, out_specs=out_specs, out_shape=out_shape,
        scratch_shapes=[pltpu.VMEM((tm, D_MODEL), bf16), pltpu.VMEM((tm, D_MODEL), f32)],
        compiler_params=_params(("arbitrary", "arbitrary")),
    )(*args)


def _ffn_bwd_act(dpre, hg, hu, wd, *, tm, name, host=None):
    t = dpre.shape[0]

    def body(dp_ref, hg_ref, hu_ref, wd_ref, dhg_ref, dhu_ref, a_ref, dfb):
        @pl.when(pl.program_id(1) == 0)
        def _():
            dfb[...] = (0.5 * dp_ref[...]).astype(bf16)

        da = lax.dot_general(dfb[...], wd_ref[...], _NT, preferred_element_type=f32)
        hgv = hg_ref[...].astype(f32)
        huv = hu_ref[...].astype(f32)
        sg = _sigmoid_tanh(hgv)
        silu = hgv * sg
        a_ref[...] = (silu * huv).astype(bf16)
        dhu_ref[...] = (da * silu).astype(bf16)
        dhg_ref[...] = (da * huv * (sg * (1.0 + hgv * (1.0 - sg)))).astype(bf16)

    hid = pl.BlockSpec((tm, FF_TILE), lambda i, j: (i, j))
    return _hosted_call(
        host, body, name=name, grid=(t // tm, N_DEV),
        in_specs=[pl.BlockSpec((tm, D_MODEL), lambda i, j: (i, 0)), hid, hid,
                  pl.BlockSpec((None, FF_TILE, D_MODEL), lambda i, j: (j, 0, 0))],
        out_specs=[hid, hid, hid], out_shape=[jax.ShapeDtypeStruct((t, D_FF), bf16)] * 3,
        scratch_shapes=[pltpu.VMEM((tm, D_MODEL), bf16)],
        compiler_params=_params(("arbitrary", "arbitrary")),
    )(dpre, hg, hu, wd)


def _ffn_bwd_dx(dpre, dhg, dhu, wg, wu, *, tm, name, host=None):
    t = dpre.shape[0]
    nj = N_DEV

    def body(dp_ref, dhg_ref, dhu_ref, wg_ref, wu_ref, dx_ref, acc):
        j = pl.program_id(1)

        @pl.when(j == 0)
        def _():
            acc[...] = jnp.zeros_like(acc)

        acc[...] += (lax.dot_general(dhg_ref[...], wg_ref[...], _NT, preferred_element_type=f32)
                     + lax.dot_general(dhu_ref[...], wu_ref[...], _NT, preferred_element_type=f32))

        @pl.when(j == nj - 1)
        def _():
            dx_ref[...] = ALPHA * dp_ref[...] + acc[...]

    tok = pl.BlockSpec((tm, D_MODEL), lambda i, j: (i, 0))
    hid = pl.BlockSpec((tm, FF_TILE), lambda i, j: (i, j))
    wspec = pl.BlockSpec((None, D_MODEL, FF_TILE), lambda i, j: (j, 0, 0))
    return _hosted_call(
        host, body, name=name, grid=(t // tm, nj), in_specs=[tok, hid, hid, wspec, wspec],
        out_specs=[tok], out_shape=[jax.ShapeDtypeStruct((t, D_MODEL), f32)],
        scratch_shapes=[pltpu.VMEM((tm, D_MODEL), f32)],
        compiler_params=_params(("arbitrary", "arbitrary")),
    )(dpre, dhg, dhu, wg, wu)


def _mm(a, b, *, mode, out_dtype, tm, tn, tk, name, affine=None, a_cols=None, b_cols=None,
        b_blocked=False, out_blocked=False, out_scale=None):
    if mode == "nn":
        m_full, k_full = a.shape
        m_dim, k_dim = (m_full, a_cols[1]) if a_cols else (m_full, k_full)
    else:
        k_dim, m_full = a.shape
        m_dim = a_cols[1] if a_cols else m_full
    a_off = a_cols[0] if a_cols else 0
    if b_blocked:
        n_dim = b.shape[0] * b.shape[2]
        assert b.shape[2] == tn
    else:
        n_dim = b_cols[1] if b_cols else b.shape[1]
    b_off = b_cols[0] if b_cols else 0
    assert m_dim % tm == 0 and n_dim % tn == 0 and k_dim % tk == 0, (name, m_dim, n_dim, k_dim)
    nk = k_dim // tk

    def body(*refs):
        if affine is not None:
            a_ref, g_ref, s_ref, b_ref, o_ref, acc = refs
        else:
            a_ref, b_ref, o_ref, acc = refs
        k = pl.program_id(2)

        @pl.when(k == 0)
        def _():
            acc[...] = jnp.zeros_like(acc)

        av = a_ref[...]
        if affine is not None:
            av = av * g_ref[...] + s_ref[...]
        av = av.astype(bf16)
        bv = b_ref[...].astype(bf16)
        if mode == "nn":
            acc[...] += jnp.dot(av, bv, preferred_element_type=f32)
        else:
            acc[...] += lax.dot_general(av, bv, _TN, preferred_element_type=f32)

        @pl.when(k == nk - 1)
        def _():
            res = acc[...] if out_scale is None else acc[...] * out_scale
            o_ref[...] = res.astype(out_dtype)

    if mode == "nn":
        a_spec = pl.BlockSpec((tm, tk), lambda i, j, k: (i, k + a_off))
        aff_spec = pl.BlockSpec((1, tk), lambda i, j, k: (0, k + a_off))
    else:
        a_spec = pl.BlockSpec((tk, tm), lambda i, j, k: (k, i + a_off))
        aff_spec = pl.BlockSpec((1, tm), lambda i, j, k: (0, i + a_off))
    if b_blocked:
        b_spec = pl.BlockSpec((None, tk, tn), lambda i, j, k: (j, k, 0))
    else:
        b_spec = pl.BlockSpec((tk, tn), lambda i, j, k: (k, j + b_off))
    if out_blocked:
        o_spec = pl.BlockSpec((None, tm, tn), lambda i, j, k: (j, i, 0))
        o_shape = jax.ShapeDtypeStruct((n_dim // tn, m_dim, tn), out_dtype)
    else:
        o_spec = pl.BlockSpec((tm, tn), lambda i, j, k: (i, j))
        o_shape = jax.ShapeDtypeStruct((m_dim, n_dim), out_dtype)
    in_specs = [a_spec] + ([aff_spec, aff_spec] if affine is not None else []) + [b_spec]
    args = [a] + (list(affine) if affine is not None else []) + [b]
    return pl.pallas_call(
        body, name=name, grid=(m_dim // tm, n_dim // tn, nk), in_specs=in_specs, out_specs=o_spec,
        out_shape=o_shape, scratch_shapes=[pltpu.VMEM((tm, tn), f32)],
        compiler_params=_params(("arbitrary", "arbitrary", "arbitrary")),
    )(*args)


def _mm_tn(a, b, *, out_dtype, tm, mb, tn, nb, tk, name, affine=None, out_blocked=False, out_scale=None,
           pair=False, host=None):
    k_dim, m_dim = a.shape
    multi_b = isinstance(b, (list, tuple))
    b_list = list(b) if multi_b else [b]
    n_dim = nb * tn if multi_b else b.shape[1]
    assert m_dim % (mb * tm) == 0 and n_dim % (nb * tn) == 0 and k_dim % tk == 0, (name, m_dim, n_dim, k_dim)
    nk = k_dim // tk
    grid = (m_dim // (mb * tm), n_dim // (nb * tn), nk)
    if pair:
        assert mb * nb == 4 and grid[0] * grid[1] == 2 and out_dtype == bf16, name

    def body(*refs):
        if pair:
            refs, (acc, send_buf, recv_buf, send_sems, recv_sems) = refs[:-5], refs[-5:]
        else:
            refs, acc = refs[:-1], refs[-1]
        a_ref, o_ref = refs[0], refs[-1]
        if affine is not None:
            g_ref, s_ref = refs[1:3]
        b_refs = refs[3 if affine is not None else 1:-1]
        k = pl.program_id(2)

        @pl.when(k == 0)
        def _():
            acc[...] = jnp.zeros_like(acc)

        av = a_ref[...]
        if affine is not None:
            av = av * g_ref[...] + s_ref[...]
        av = av.astype(bf16)
        if multi_b:
            pieces = [r[...].astype(bf16) for r in b_refs]
        else:
            bv = b_refs[0][...].astype(bf16)
            pieces = [bv[:, jn * tn:(jn + 1) * tn] for jn in range(nb)]
        for im in range(mb):
            a_t = av[:, im * tm:(im + 1) * tm].T
            for jn in range(nb):
                acc[im * nb + jn] += jnp.dot(a_t, pieces[jn], preferred_element_type=f32)

        def scaled(v):
            return v if out_scale is None else v * out_scale

        @pl.when(k == nk - 1)
        def _():
            if pair:
                x, y, c = lax.axis_index("x"), lax.axis_index("y"), lax.axis_index("c")
                window = pl.program_id(0) + pl.program_id(1)
                swaps = []
                for cc in range(2):
                    send_buf[cc] = scaled(acc[2 * cc + 1 - c]).astype(bf16)
                    swaps.append(pltpu.make_async_remote_copy(
                        src_ref=send_buf.at[cc], dst_ref=recv_buf.at[window, cc],
                        send_sem=send_sems.at[2 * window + cc], recv_sem=recv_sems.at[2 * window + cc],
                        device_id=(x, y, 1 - c), device_id_type=MESH_T))
                    swaps[cc].start()
                for cc in range(2):
                    swaps[cc].wait_recv()
                    o_ref[cc] = (scaled(acc[2 * cc + c]) + recv_buf[window, cc].astype(f32)).astype(bf16)
                for cc in range(2):
                    swaps[cc].wait_send()
                return
            for im in range(mb):
                for jn in range(nb):
                    res = scaled(acc[im * nb + jn])
                    if out_blocked:
                        o_ref[jn, im * tm:(im + 1) * tm, :] = res.astype(out_dtype)
                    else:
                        o_ref[im * tm:(im + 1) * tm, jn * tn:(jn + 1) * tn] = res.astype(out_dtype)

    a_spec = pl.BlockSpec((tk, mb * tm), lambda i, j, k: (k, i))
    aff_spec = pl.BlockSpec((1, mb * tm), lambda i, j, k: (0, i))
    if multi_b:
        b_specs = [pl.BlockSpec((tk, tn), lambda i, j, k: (k, 0))] * nb
    else:
        b_specs = [pl.BlockSpec((tk, nb * tn), lambda i, j, k: (k, j))]
    scratch = [pltpu.VMEM((mb * nb, tm, tn), f32)]
    if pair:
        o_spec = pl.BlockSpec((2, tm, tn), lambda i, j, k: (i + j, 0, 0))
        o_shape = jax.ShapeDtypeStruct((4, tm, tn), out_dtype)
        scratch += [pltpu.VMEM((2, tm, tn), bf16), pltpu.VMEM((2, 2, tm, tn), bf16),
                    pltpu.SemaphoreType.DMA((4,)), pltpu.SemaphoreType.DMA((4,))]
    elif out_blocked:
        o_spec = pl.BlockSpec((nb, mb * tm, tn), lambda i, j, k: (j, i, 0))
        o_shape = jax.ShapeDtypeStruct((n_dim // tn, m_dim, tn), out_dtype)
    else:
        o_spec = pl.BlockSpec((mb * tm, nb * tn), lambda i, j, k: (i, j))
        o_shape = jax.ShapeDtypeStruct((m_dim, n_dim), out_dtype)
    in_specs = [a_spec] + ([aff_spec, aff_spec] if affine is not None else []) + b_specs
    args = [a] + (list(affine) if affine is not None else []) + b_list
    res = _hosted_call(
        host, body, name=name, grid=grid, in_specs=in_specs, out_specs=o_spec, out_shape=o_shape,
        scratch_shapes=scratch, compiler_params=_params(("arbitrary", "arbitrary", "arbitrary")),
    )(*args)
    return res[0] if host is None else res


def _in_proj(xhat, g, b, w_in, *, tm, name):
    t = xhat.shape[0]
    n_qkv, n_l = 3 * FOX_W, 2 * LRU_W

    def body(x_ref, g_ref, b_ref, w_ref, qkv_ref, zl_ref, zfg_ref):
        xb = (x_ref[...] * g_ref[...] + b_ref[...]).astype(bf16)
        qkv_ref[...] = jnp.dot(xb, w_ref[:, :n_qkv], preferred_element_type=f32).astype(bf16)
        zl_ref[...] = jnp.dot(xb, w_ref[:, n_qkv:n_qkv + n_l], preferred_element_type=f32)
        zfg_ref[...] = jnp.dot(xb, w_ref[:, n_qkv + n_l:], preferred_element_type=f32)

    row = pl.BlockSpec((1, D_MODEL), lambda i: (0, 0))
    return pl.pallas_call(
        body, name=name, grid=(t // tm,),
        in_specs=[pl.BlockSpec((tm, D_MODEL), lambda i: (i, 0)), row, row,
                  pl.BlockSpec(w_in.shape, lambda i: (0, 0))],
        out_specs=[pl.BlockSpec((tm, n_qkv), lambda i: (i, 0)), pl.BlockSpec((tm, n_l), lambda i: (i, 0)),
                   pl.BlockSpec((tm, LANES), lambda i: (i, 0))],
        out_shape=[jax.ShapeDtypeStruct((t, n_qkv), bf16), jax.ShapeDtypeStruct((t, n_l), f32),
                   jax.ShapeDtypeStruct((t, LANES), f32)],
        compiler_params=_params(("arbitrary",)),
    )(xhat, g, b, w_in)


def _mmln(pairs, *, tm, name, resid=None, resid_scale=1.0, epi=None, ln=None, n_out=D_MODEL):
    t = pairs[0][0].shape[0]
    n_pairs = len(pairs)
    n_resid = 0 if resid is None else len(resid) - 1

    def body(*refs):
        pos = 0
        val = None
        for p in range(n_pairs):
            a_ref, b_ref = refs[pos], refs[pos + 1]
            pos += 2
            av = a_ref[...].astype(bf16)
            bv = b_ref[...].astype(bf16)
            if pairs[p][6] == "nn":
                term = jnp.dot(av, bv, preferred_element_type=f32)
            else:
                term = lax.dot_general(av, bv, _NT, preferred_element_type=f32)
            val = term if val is None else val + term
        if resid is not None:
            if resid[0] == "plain":
                r = refs[pos][...]
            else:
                r = refs[pos][...] * refs[pos + 1][...] + refs[pos + 2][...]
            pos += n_resid
            val = val + resid_scale * r
        if epi is None:
            o_ref = refs[pos]
            o_ref[...] = val.astype(o_ref.dtype)
        elif epi == "ln_fwd":
            xo, rstd = _ln_fwd_tile(val)
            refs[pos][...] = xo
            refs[pos + 1][...] = rstd
        else:
            xh_ref, rs_ref, g_ref, dx_ref, gg_ref, gb_ref = refs[pos:pos + 6]
            dprev, gg, gb = _ln_bwd_tile(val, xh_ref[...], rs_ref[...], g_ref[...])
            dx_ref[...] = dprev
            i = pl.program_id(0)

            @pl.when(i == 0)
            def _():
                gg_ref[...] = gg
                gb_ref[...] = gb

            @pl.when(i > 0)
            def _():
                gg_ref[...] += gg
                gb_ref[...] += gb

    in_specs, args = [], []
    for (a, acb, aw, b, bcb, bw, mode) in pairs:
        in_specs.append(pl.BlockSpec((tm, aw), lambda i, acb=acb: (i, acb)))
        args.append(a)
        if mode == "nn":
            in_specs.append(pl.BlockSpec((aw, n_out), lambda i, bcb=bcb: (bcb, 0)))
        else:
            in_specs.append(pl.BlockSpec((n_out, bw), lambda i, bcb=bcb: (0, bcb)))
        args.append(b)
    tok = pl.BlockSpec((tm, n_out), lambda i: (i, 0))
    row = pl.BlockSpec((1, n_out), lambda i: (0, 0))
    col = pl.BlockSpec((tm, 1), lambda i: (i, 0))
    if resid is not None:
        in_specs += [tok] if resid[0] == "plain" else [tok, row, row]
        args += list(resid[1:])
    if epi is None:
        out_specs, out_shape = tok, jax.ShapeDtypeStruct((t, n_out), f32)
    elif epi == "ln_fwd":
        out_specs = [tok, col]
        out_shape = [jax.ShapeDtypeStruct((t, n_out), f32), jax.ShapeDtypeStruct((t, 1), f32)]
    else:
        in_specs += [tok, col, row]
        args += list(ln)
        out_specs = [tok, row, row]
        out_shape = [jax.ShapeDtypeStruct((t, n_out), f32)] + [jax.ShapeDtypeStruct((1, n_out), f32)] * 2
    return pl.pallas_call(
        body, name=name, grid=(t // tm,), in_specs=in_specs, out_specs=out_specs, out_shape=out_shape,
        compiler_params=_params(("arbitrary",)),
    )(*args)


def _loss_bwd(xhat, rstd, g, b, target, *, tm, name):
    t = xhat.shape[0]

    def body(xh_ref, rs_ref, g_ref, b_ref, tg_ref, dx_ref, sq_ref, gg_ref, gb_ref):
        i = pl.program_id(0)
        xh = xh_ref[...]
        diff = xh * g_ref[...] + b_ref[...] - tg_ref[...]
        sq = jnp.sum(diff * diff, axis=0, keepdims=True)
        dprev, gg, gb = _ln_bwd_tile(diff * (1.0 / D_MODEL), xh, rs_ref[...], g_ref[...])
        dx_ref[...] = dprev

        @pl.when(i == 0)
        def _():
            sq_ref[...] = sq
            gg_ref[...] = gg
            gb_ref[...] = gb

        @pl.when(i > 0)
        def _():
            sq_ref[...] += sq
            gg_ref[...] += gg
            gb_ref[...] += gb

    tok = pl.BlockSpec((tm, D_MODEL), lambda i: (i, 0))
    row = pl.BlockSpec((1, D_MODEL), lambda i: (0, 0))
    return pl.pallas_call(
        body, name=name, grid=(t // tm,),
        in_specs=[tok, pl.BlockSpec((tm, 1), lambda i: (i, 0)), row, row, tok],
        out_specs=[tok, row, row, row],
        out_shape=[jax.ShapeDtypeStruct((t, D_MODEL), f32)] + [jax.ShapeDtypeStruct((1, D_MODEL), f32)] * 3,
        compiler_params=_params(("arbitrary",)),
    )(xhat, rstd, g, b, target)


CUM_TILE = 256


def _tri(n, lower):
    r = lax.broadcasted_iota(jnp.int32, (n, n), 0)
    c = lax.broadcasted_iota(jnp.int32, (n, n), 1)
    return jnp.where((r >= c) if lower else (r <= c), 1.0, 0.0).astype(f32)


def _cum_fwd(zfg, bfg, *, name):
    t = zfg.shape[0]

    def body(z_ref, b_ref, o_ref, carry):
        @pl.when(pl.program_id(0) == 0)
        def _():
            carry[...] = jnp.zeros_like(carry)

        ls = -_softplus(-(z_ref[...] + b_ref[...]))
        c = jnp.dot(_tri(CUM_TILE, True), ls, preferred_element_type=f32,
                    precision=lax.Precision.HIGHEST) + carry[...]
        o_ref[...] = c
        carry[...] = c[CUM_TILE - 1:CUM_TILE, :]

    blk = pl.BlockSpec((CUM_TILE, LANES), lambda i: (i, 0))
    return pl.pallas_call(
        body, name=name, grid=(t // CUM_TILE,),
        in_specs=[blk, pl.BlockSpec((1, LANES), lambda i: (0, 0))], out_specs=blk,
        out_shape=jax.ShapeDtypeStruct((t, LANES), f32), scratch_shapes=[pltpu.VMEM((1, LANES), f32)],
        compiler_params=_params(("arbitrary",)),
    )(zfg, bfg)


def _cum_bwd(dcum_q, dcum_k, zfg, bfg, *, name):
    t = zfg.shape[0]
    n = t // CUM_TILE

    def body(d_ref, d2_ref, z_ref, b_ref, o_ref, s_ref, carry):
        i = pl.program_id(0)

        @pl.when(i == 0)
        def _():
            carry[...] = jnp.zeros_like(carry)

        dls = jnp.dot(_tri(CUM_TILE, False), d_ref[...] + d2_ref[...], preferred_element_type=f32,
                      precision=lax.Precision.HIGHEST) + carry[...]
        carry[...] = dls[0:1, :]
        lane = lax.broadcasted_iota(jnp.int32, (CUM_TILE, LANES), 1)
        dfg = jnp.where(lane < HEADS, dls * _sigmoid(-(z_ref[...] + b_ref[...])), 0.0)
        o_ref[...] = dfg
        tot = jnp.sum(dfg, axis=0, keepdims=True)

        @pl.when(i == 0)
        def _():
            s_ref[...] = tot

        @pl.when(i > 0)
        def _():
            s_ref[...] += tot

    blk = pl.BlockSpec((CUM_TILE, LANES), lambda i: (n - 1 - i, 0))
    row = pl.BlockSpec((1, LANES), lambda i: (0, 0))
    return pl.pallas_call(
        body, name=name, grid=(n,), in_specs=[blk, blk, blk, row], out_specs=[blk, row],
        out_shape=[jax.ShapeDtypeStruct((t, LANES), f32), jax.ShapeDtypeStruct((1, LANES), f32)],
        scratch_shapes=[pltpu.VMEM((1, LANES), f32)],
        compiler_params=_params(("arbitrary",)),
    )(dcum_q, dcum_k, zfg, bfg)


ATT_TILE = 512


def _causal(i, j, transposed):
    r = lax.broadcasted_iota(jnp.int32, (ATT_TILE, ATT_TILE), 0)
    c = lax.broadcasted_iota(jnp.int32, (ATT_TILE, ATT_TILE), 1)
    if transposed:
        return (c + i * ATT_TILE) >= (r + j * ATT_TILE)
    return (r + i * ATT_TILE) >= (c + j * ATT_TILE)


ATT_W = HEADS * LANES


def _data_lane(h):
    return HEAD_D * (h % 2)


def _extra_lane(h):
    return HEAD_D - _data_lane(h)


def _split3(x):
    hi = x.astype(bf16)
    rest = x - hi.astype(f32)
    mid = rest.astype(bf16)
    lo = (rest - mid.astype(f32)).astype(bf16)
    return hi, mid, lo


def _three_pieces(x):
    hi, mid, lo = (p.astype(f32) for p in _split3(x))
    return (hi + pltpu.roll(mid, HEADS, axis=1) + pltpu.roll(lo, 2 * HEADS, axis=1)).astype(bf16)


def _move(h, first):
    r = lax.broadcasted_iota(jnp.int32, (LANES, LANES), 0)
    c = lax.broadcasted_iota(jnp.int32, (LANES, LANES), 1)
    hit = functools.reduce(jnp.logical_or, [jnp.logical_and(r == HEADS * q + h, c == first + q) for q in range(3)])
    return jnp.where(hit, 1.0, 0.0).astype(bf16)


def _ones_from(first, rows):
    lane = lax.broadcasted_iota(jnp.int32, (rows, LANES), 1)
    return jnp.where(jnp.logical_and(lane >= first, lane < first + 3), 1.0, 0.0)


def _own_lanes(h, rows):
    lane = lax.broadcasted_iota(jnp.int32, (rows, LANES), 1)
    return (lane < HEAD_D) if h % 2 == 0 else (lane >= HEAD_D)


def _head_values(x):
    lane = lax.broadcasted_iota(jnp.int32, x.shape, 1)
    return jnp.where(lane < HEADS, x, 0.0)


def _attn_prep_fwd(qkv, cum, *, tm, name):
    t = qkv.shape[0]

    def body(q_ref, k_ref, v_ref, c_ref, qa_ref, ka_ref, va_ref):
        c3 = _three_pieces(_head_values(c_ref[...]))
        ones = jnp.ones((tm, LANES), bf16)
        for h in range(HEADS):
            pair = slice(LANES * (h // 2), LANES * (h // 2 + 1))
            hs = slice(LANES * h, LANES * (h + 1))
            base, own = _extra_lane(h), _own_lanes(h, tm)
            eq = jnp.dot(c3, _move(h, base), preferred_element_type=f32) + _ones_from(base + 3, tm)
            ek = _ones_from(base, tm) - jnp.dot(c3, _move(h, base + 3), preferred_element_type=f32)
            qa_ref[:, hs] = jnp.where(own, q_ref[:, pair] * ATT_SCALE, eq.astype(bf16))
            ka_ref[:, hs] = jnp.where(own, k_ref[:, pair], ek.astype(bf16))
            va_ref[:, hs] = jnp.where(own, v_ref[:, pair], ones)

    wide = pl.BlockSpec((tm, ATT_W), lambda i: (i, 0))
    out = jax.ShapeDtypeStruct((t, ATT_W), bf16)
    return pl.pallas_call(
        body, name=name, grid=(t // tm,),
        in_specs=[pl.BlockSpec((tm, FOX_W), lambda i: (i, 0)), pl.BlockSpec((tm, FOX_W), lambda i: (i, 1)),
                  pl.BlockSpec((tm, FOX_W), lambda i: (i, 2)), pl.BlockSpec((tm, LANES), lambda i: (i, 0))],
        out_specs=[wide] * 3, out_shape=[out] * 3, compiler_params=_params(("arbitrary",)),
    )(qkv, qkv, qkv, cum)


def _attn_prep_bwd(qkv, cum, lse, dmix, o, *, tm, name):
    t = qkv.shape[0]

    def body(q_ref, c_ref, l_ref, do_ref, o_ref, qa_ref, da_ref):
        b3 = _three_pieces(_head_values(c_ref[...] - l_ref[...]))
        r = lax.broadcasted_iota(jnp.int32, (FOX_W, LANES), 0)
        c = lax.broadcasted_iota(jnp.int32, (FOX_W, LANES), 1)
        per_head = jnp.where(r // HEAD_D == c, 1.0, 0.0).astype(bf16)
        delta = sum(jnp.dot(p, per_head, preferred_element_type=f32) for p in _split3(do_ref[...] * o_ref[...]))
        d3 = _three_pieces(delta)
        for h in range(HEADS):
            pair = slice(LANES * (h // 2), LANES * (h // 2 + 1))
            hs = slice(LANES * h, LANES * (h + 1))
            base, own = _extra_lane(h), _own_lanes(h, tm)
            eq = jnp.dot(b3, _move(h, base), preferred_element_type=f32) + _ones_from(base + 3, tm)
            ed = -jnp.dot(d3, _move(h, base), preferred_element_type=f32)
            qa_ref[:, hs] = jnp.where(own, q_ref[:, pair] * ATT_SCALE, eq.astype(bf16))
            da_ref[:, hs] = jnp.where(own, do_ref[:, pair].astype(bf16), ed.astype(bf16))

    wide = pl.BlockSpec((tm, ATT_W), lambda i: (i, 0))
    half = pl.BlockSpec((tm, FOX_W), lambda i: (i, 0))
    col = pl.BlockSpec((tm, LANES), lambda i: (i, 0))
    out = jax.ShapeDtypeStruct((t, ATT_W), bf16)
    return pl.pallas_call(
        body, name=name, grid=(t // tm,), in_specs=[half, col, col, half, half],
        out_specs=[wide] * 2, out_shape=[out] * 2, compiler_params=_params(("arbitrary",)),
    )(qkv, cum, lse, dmix, o)


def _attn_fwd2(q_aug, k_aug, v_aug, *, name, host=None):
    t = q_aug.shape[0]
    n = t // ATT_TILE
    tq = ATT_TILE

    def body(q_ref, k_ref, v_ref, o_ref, lse_ref, acc, m_s):
        i = pl.program_id(0)
        j = pl.program_id(1)

        @pl.when(j == 0)
        def _():
            acc[...] = jnp.zeros_like(acc)
            m_s[...] = jnp.full_like(m_s, NEG_BIG)

        def block(masked):
            mask = _causal(i, j, False) if masked else None
            for h in range(HEADS):
                hs = slice(LANES * h, LANES * (h + 1))
                s = lax.dot_general(q_ref[:, hs], k_ref[:, hs], _NT, preferred_element_type=f32)
                if masked:
                    s = jnp.where(mask, s, NEG_BIG)
                blocks = [s[:, LANES * b:LANES * (b + 1)] for b in range(tq // LANES)]
                m_old = m_s[h]
                m_new = jnp.maximum(m_old, jnp.broadcast_to(
                    jnp.max(functools.reduce(jnp.maximum, blocks), axis=-1, keepdims=True), (tq, LANES)))
                p = jnp.concatenate([jnp.exp(b - m_new) for b in blocks], axis=1).astype(bf16)
                acc[h] = jnp.exp(m_old - m_new) * acc[h] + jnp.dot(p, v_ref[:, hs], preferred_element_type=f32)
                m_s[h] = m_new

        @pl.when(j < i)
        def _():
            block(False)

        @pl.when(j == i)
        def _():
            block(True)
            lse_ref[...] = jnp.zeros_like(lse_ref)
            for h in range(HEADS):
                a = acc[h]
                l = a[:, _extra_lane(h):_extra_lane(h) + 1]
                o_ref[:, HEAD_D * h:HEAD_D * (h + 1)] = a[:, _data_lane(h):_data_lane(h) + HEAD_D] / l
                lse_ref[:, h:h + 1] = m_s[h][:, 0:1] + jnp.log(l)

    kv = pl.BlockSpec((tq, ATT_W), lambda i, j: (jnp.minimum(i, j), 0))
    return _hosted_call(
        host, body, name=name, grid=(n, n),
        in_specs=[pl.BlockSpec((tq, ATT_W), lambda i, j: (i, 0)), kv, kv],
        out_specs=[pl.BlockSpec((tq, FOX_W), lambda i, j: (i, 0)), pl.BlockSpec((tq, LANES), lambda i, j: (i, 0))],
        out_shape=[jax.ShapeDtypeStruct((t, FOX_W), f32), jax.ShapeDtypeStruct((t, LANES), f32)],
        scratch_shapes=[pltpu.VMEM((HEADS, tq, LANES), f32), pltpu.VMEM((HEADS, tq, LANES), f32)],
        compiler_params=_params(("arbitrary", "arbitrary")),
    )(q_aug, k_aug, v_aug)


def _attn_bwd(qb_aug, k_aug, v_aug, do_aug, *, name, host=None):
    t = qb_aug.shape[0]
    n = t // ATT_TILE
    tk = ATT_TILE

    def body(q_ref, k_ref, v_ref, do_ref, dq_ref, dcq_ref, dk_ref, dv_ref, dck_ref, dk_acc, dv_acc, dq_all):
        j = pl.program_id(0)
        i = pl.program_id(1)

        @pl.when(jnp.logical_and(i == 0, j == 0))
        def _():
            dq_all[...] = jnp.zeros_like(dq_all)

        @pl.when(i == 0)
        def _():
            dk_acc[...] = jnp.zeros_like(dk_acc)
            dv_acc[...] = jnp.zeros_like(dv_acc)

        def block(masked):
            mask = _causal(i, j, True) if masked else None
            for h in range(HEADS):
                hs = slice(LANES * h, LANES * (h + 1))
                qh = q_ref[:, hs]
                doh = do_ref[:, hs]
                kh = k_ref[:, hs]
                s_t = lax.dot_general(kh, qh, _NT, preferred_element_type=f32)
                if masked:
                    s_t = jnp.where(mask, s_t, NEG_BIG)
                p_t = jnp.exp(s_t)
                dv_acc[h] += jnp.dot(p_t.astype(bf16), doh, preferred_element_type=f32)
                dp_t = lax.dot_general(v_ref[:, hs], doh, _NT, preferred_element_type=f32)
                ds_t = (p_t * dp_t).astype(bf16)
                dk_acc[h] += jnp.dot(ds_t, qh, preferred_element_type=f32)
                dq_all[i, h] += lax.dot_general(ds_t, kh, _TN, preferred_element_type=f32)

        @pl.when(i > j)
        def _():
            block(False)

        @pl.when(i == j)
        def _():
            block(True)
            dcq_ref[...] = jnp.zeros_like(dcq_ref)
            for h in range(HEADS):
                a = dq_all[j, h]
                dq_ref[:, HEAD_D * h:HEAD_D * (h + 1)] = (
                    a[:, _data_lane(h):_data_lane(h) + HEAD_D] * ATT_SCALE).astype(bf16)
                dcq_ref[:, h:h + 1] = a[:, _extra_lane(h):_extra_lane(h) + 1]

        @pl.when(i == n - 1)
        def _():
            dck_ref[...] = jnp.zeros_like(dck_ref)
            for h in range(HEADS):
                a = dk_acc[h]
                cols = slice(_data_lane(h), _data_lane(h) + HEAD_D)
                dk_ref[:, HEAD_D * h:HEAD_D * (h + 1)] = a[:, cols].astype(bf16)
                dv_ref[:, HEAD_D * h:HEAD_D * (h + 1)] = dv_acc[h][:, cols].astype(bf16)
                dck_ref[:, h:h + 1] = -a[:, _extra_lane(h) + 3:_extra_lane(h) + 4]

    own = pl.BlockSpec((tk, ATT_W), lambda j, i: (j, 0))
    qs = pl.BlockSpec((tk, ATT_W), lambda j, i: (jnp.maximum(i, j), 0))
    half = pl.BlockSpec((tk, FOX_W), lambda j, i: (j, 0))
    col = pl.BlockSpec((tk, LANES), lambda j, i: (j, 0))
    return _hosted_call(
        host, body, name=name, grid=(n, n), in_specs=[qs, own, own, qs],
        out_specs=[half, col, half, half, col],
        out_shape=[jax.ShapeDtypeStruct((t, FOX_W), bf16), jax.ShapeDtypeStruct((t, LANES), f32),
                   jax.ShapeDtypeStruct((t, FOX_W), bf16), jax.ShapeDtypeStruct((t, FOX_W), bf16),
                   jax.ShapeDtypeStruct((t, LANES), f32)],
        scratch_shapes=[pltpu.VMEM((HEADS, tk, LANES), f32), pltpu.VMEM((HEADS, tk, LANES), f32),
                        pltpu.VMEM((n, HEADS, tk, LANES), f32)],
        compiler_params=_params(("arbitrary", "arbitrary")),
    )(qb_aug, k_aug, v_aug, do_aug)


LRU_CHUNK = 64
LRU_G = 256
SUB = 8


def _row_ids(n):
    return lax.broadcasted_iota(jnp.int32, (n, LRU_G), 0)


def _shift_rows_down(ext, s):
    return pltpu.roll(ext, s, axis=0)[SUB:, :]


def _shift_rows_up(ext, s, n):
    return pltpu.roll(ext, ext.shape[0] - s, axis=0)[:n, :]


def _lru_gates(u, wa_ref, ba_ref, wx_ref, bx_ref, sp):
    ub = u.astype(bf16)
    r = _sigmoid(jnp.dot(ub, wa_ref[...], preferred_element_type=f32) + ba_ref[...])
    gi = _sigmoid(jnp.dot(ub, wx_ref[...], preferred_element_type=f32) + bx_ref[...])
    log_a = -LRU_C * r * sp
    a = jnp.exp(log_a)
    s = jnp.sqrt(_one_minus_exp(2.0 * log_a))
    return r, gi, a, s


def _conv_window(lx_ref, r0, ci):
    cur = lx_ref[pl.ds(r0, LRU_CHUNK), :]
    p0 = pl.multiple_of(jnp.maximum(r0 - SUB, 0), SUB)
    prev = jnp.where(ci > 0, lx_ref[pl.ds(p0, SUB), :], 0.0)
    return cur, jnp.concatenate([prev, cur], axis=0)


def _lru_fwd(zl, conv_w, conv_b, wa, ba, wx, bx, lam, *, name, host=None):
    t = zl.shape[0]
    n_chunk = t // LRU_CHUNK

    def body(lx_ref, lg_ref, cw_ref, cb_ref, wa_ref, ba_ref, wx_ref, bx_ref, lam_ref, u_ref, h_ref, y_ref):
        sp = _softplus(-lam_ref[...])
        rows = _row_ids(SUB)

        def chunk(ci, hc):
            r0 = pl.multiple_of(ci * LRU_CHUNK, LRU_CHUNK)
            cur, ext = _conv_window(lx_ref, r0, ci)
            u = cb_ref[...] + cw_ref[3:4, :] * cur
            for k in range(3):
                u = u + cw_ref[k:k + 1, :] * _shift_rows_down(ext, 3 - k)
            r, gi, a, s = _lru_gates(u, wa_ref, ba_ref, wx_ref, bx_ref, sp)
            b = s * (gi * u)
            tiles = []
            for q in range(LRU_CHUNK // SUB):
                ta = a[SUB * q:SUB * (q + 1), :]
                tb = b[SUB * q:SUB * (q + 1), :]
                for d in (1, 2, 4):
                    a_sh = jnp.where(rows >= d, pltpu.roll(ta, d, axis=0), 1.0)
                    b_sh = jnp.where(rows >= d, pltpu.roll(tb, d, axis=0), 0.0)
                    tb = ta * b_sh + tb
                    ta = ta * a_sh
                hq = tb + ta * hc
                hc = hq[SUB - 1:SUB, :]
                tiles.append(hq)
            h = jnp.concatenate(tiles, axis=0)
            u_ref[pl.ds(r0, LRU_CHUNK), :] = u
            h_ref[pl.ds(r0, LRU_CHUNK), :] = h
            gel, _ = _gelu_and_grad(lg_ref[pl.ds(r0, LRU_CHUNK), :])
            y_ref[pl.ds(r0, LRU_CHUNK), :] = gel * h
            return hc

        lax.fori_loop(0, n_chunk, chunk, jnp.zeros((1, LRU_G), f32))

    seq = lambda cb: pl.BlockSpec((t, LRU_G), lambda c, cb=cb: (0, c + cb))
    rowc = pl.BlockSpec((1, LRU_G), lambda c: (0, c))
    diag = pl.BlockSpec((LRU_G, LRU_G), lambda c: (c, c))
    out = jax.ShapeDtypeStruct((t, LRU_W), f32)
    return _hosted_call(
        host, body, name=name, grid=(LRU_W // LRU_G,),
        in_specs=[seq(0), seq(LRU_W // LRU_G), pl.BlockSpec((4, LRU_G), lambda c: (0, c)),
                  rowc, diag, rowc, diag, rowc, rowc],
        out_specs=[seq(0)] * 3, out_shape=[out] * 3,
        compiler_params=_params(("arbitrary",)),
    )(zl, zl, conv_w, conv_b, wa, ba, wx, bx, lam)


def _lru_bwd(dmix, zl, u_all, h_all, conv_w, wa, ba, wx, bx, lam, *, name, host=None):
    t = zl.shape[0]
    n_chunk = t // LRU_CHUNK

    def body(dy_ref, lx_ref, lg_ref, u_ref, h_ref, cw_ref, wa_ref, ba_ref, wx_ref, bx_ref, lam_ref,
             dlx_ref, dlg_ref, dcw_ref, dcb_ref, dba_ref, dbx_ref, dlam_ref, dwa_ref, dwx_ref, dpr_s, dpx_s):
        lam_v = lam_ref[...]
        sp = _softplus(-lam_v)
        rows = _row_ids(SUB)
        rows_c = _row_ids(LRU_CHUNK)
        zero_row = jnp.zeros((1, LRU_G), f32)

        def chunk(step, carry):
            dh_c, a_next0, du_next, dsp, dba, dbx, dcb, dw0, dw1, dw2, dw3 = carry
            ci = n_chunk - 1 - step
            r0 = pl.multiple_of(ci * LRU_CHUNK, LRU_CHUNK)
            sl = pl.ds(r0, LRU_CHUNK)
            u = u_ref[sl, :]
            r, gi, a, s = _lru_gates(u, wa_ref, ba_ref, wx_ref, bx_ref, sp)
            h = h_ref[sl, :]
            p0 = pl.multiple_of(jnp.maximum(r0 - SUB, 0), SUB)
            h_before = jnp.where(ci > 0, h_ref[pl.ds(p0, SUB), :], 0.0)[SUB - 1:SUB, :]
            h_prev = jnp.where(rows_c == 0, h_before, pltpu.roll(h, 1, axis=0))
            gel, dgel = _gelu_and_grad(lg_ref[sl, :])
            dy = dy_ref[sl, :]
            dlg_ref[sl, :] = (dy * h * dgel).astype(bf16)
            g_in = dy * gel
            a_next = jnp.where(rows_c == LRU_CHUNK - 1, a_next0, pltpu.roll(a, LRU_CHUNK - 1, axis=0))
            tiles = [None] * (LRU_CHUNK // SUB)
            for q in reversed(range(LRU_CHUNK // SUB)):
                ta = a_next[SUB * q:SUB * (q + 1), :]
                tb = g_in[SUB * q:SUB * (q + 1), :]
                for d in (1, 2, 4):
                    a_sh = jnp.where(rows < SUB - d, pltpu.roll(ta, SUB - d, axis=0), 1.0)
                    b_sh = jnp.where(rows < SUB - d, pltpu.roll(tb, SUB - d, axis=0), 0.0)
                    tb = ta * b_sh + tb
                    ta = ta * a_sh
                dhq = tb + ta * dh_c
                dh_c = dhq[0:1, :]
                tiles[q] = dhq
            dh = jnp.concatenate(tiles, axis=0)
            da = dh * h_prev
            ds = dh * gi * u
            dgi = dh * s * u
            du = dh * s * gi
            dlog_a = da * a - ds * (a * a) / s
            dr = dlog_a * (-LRU_C * sp)
            dsp = dsp + jnp.sum(dlog_a * (-LRU_C * r), axis=0, keepdims=True)
            dpr = dr * r * (1.0 - r)
            dpx = dgi * gi * (1.0 - gi)
            dprb = dpr.astype(bf16)
            dpxb = dpx.astype(bf16)
            dpr_s[sl, :] = dprb
            dpx_s[sl, :] = dpxb
            du = du + (lax.dot_general(dprb, wa_ref[...], _NT, preferred_element_type=f32)
                       + lax.dot_general(dpxb, wx_ref[...], _NT, preferred_element_type=f32))
            dba = dba + jnp.sum(dpr, axis=0, keepdims=True)
            dbx = dbx + jnp.sum(dpx, axis=0, keepdims=True)
            dcb = dcb + jnp.sum(du, axis=0, keepdims=True)
            du_ext = jnp.concatenate([du, du_next], axis=0)
            dlx = cw_ref[3:4, :] * du
            for k in range(3):
                dlx = dlx + cw_ref[k:k + 1, :] * _shift_rows_up(du_ext, 3 - k, LRU_CHUNK)
            dlx_ref[sl, :] = dlx.astype(bf16)
            cur, ext = _conv_window(lx_ref, r0, ci)
            dws = [dw0, dw1, dw2, dw3 + jnp.sum(du * cur, axis=0, keepdims=True)]
            for k in range(3):
                dws[k] = dws[k] + jnp.sum(du * _shift_rows_down(ext, 3 - k), axis=0, keepdims=True)
            return (dh_c, a[0:1, :], du[0:SUB, :], dsp, dba, dbx, dcb, dws[0], dws[1], dws[2], dws[3])

        init = (zero_row, zero_row, jnp.zeros((SUB, LRU_G), f32)) + (zero_row,) * 8
        out = lax.fori_loop(0, n_chunk, chunk, init)
        _, _, _, dsp, dba, dbx, dcb, dw0, dw1, dw2, dw3 = out
        dlam_ref[...] = dsp * (-_sigmoid(-lam_v))
        dba_ref[...] = dba
        dbx_ref[...] = dbx
        dcb_ref[...] = dcb
        dcw_ref[...] = jnp.concatenate([dw0, dw1, dw2, dw3], axis=0)
        ub = u_ref[...].astype(bf16)
        dwa_ref[...] = lax.dot_general(ub, dpr_s[...], _TN, preferred_element_type=f32)
        dwx_ref[...] = lax.dot_general(ub, dpx_s[...], _TN, preferred_element_type=f32)

    seq = lambda cb: pl.BlockSpec((t, LRU_G), lambda c, cb=cb: (0, c + cb))
    rowc = pl.BlockSpec((1, LRU_G), lambda c: (0, c))
    diag = pl.BlockSpec((LRU_G, LRU_G), lambda c: (c, c))
    gate_out = pl.BlockSpec((None, LRU_G, LRU_G), lambda c: (c, 0, 0))
    row_shape = jax.ShapeDtypeStruct((1, LRU_W), f32)
    return _hosted_call(
        host, body, name=name, grid=(LRU_W // LRU_G,),
        in_specs=[seq(LRU_W // LRU_G), seq(0), seq(LRU_W // LRU_G), seq(0), seq(0),
                  pl.BlockSpec((4, LRU_G), lambda c: (0, c)),
                  diag, rowc, diag, rowc, rowc],
        out_specs=[seq(0), seq(0), pl.BlockSpec((4, LRU_G), lambda c: (0, c)), rowc, rowc, rowc, rowc,
                   gate_out, gate_out],
        out_shape=[jax.ShapeDtypeStruct((t, LRU_W), bf16)] * 2
        + [jax.ShapeDtypeStruct((4, LRU_W), f32)] + [row_shape] * 4
        + [jax.ShapeDtypeStruct((LRU_W // LRU_G, LRU_G, LRU_G), f32)] * 2,
        scratch_shapes=[pltpu.VMEM((t, LRU_G), bf16), pltpu.VMEM((t, LRU_G), bf16)],
        compiler_params=_params(("arbitrary",)),
    )(dmix, zl, zl, u_all, h_all, conv_w, wa, ba, wx, bx, lam)


def _pack_rows(a):
    flat = a.reshape(-1)
    rows = -(-flat.shape[0] // LANES)
    return jnp.pad(flat, (0, rows * LANES - flat.shape[0])).reshape(rows, LANES)


W_IN_PAD = 21 * LANES


def _w_in_join(blocks, *, name):
    tm = 256

    def body(b_ref, o_ref):
        o_ref[:, IN_COLS:] = jnp.zeros((tm, W_IN_PAD - IN_COLS), bf16)
        for q in range(N_DEV):
            o_ref[:, IN_SHARD * q:IN_SHARD * (q + 1)] = b_ref[q]

    return pl.pallas_call(
        body, name=name, grid=(D_MODEL // tm,),
        in_specs=[pl.BlockSpec((N_DEV, tm, IN_SHARD), lambda i: (0, i, 0))],
        out_specs=pl.BlockSpec((tm, W_IN_PAD), lambda i: (i, 0)),
        out_shape=jax.ShapeDtypeStruct((D_MODEL, W_IN_PAD), bf16), compiler_params=_params(("arbitrary",)),
    )(blocks)


def _w_in_split(main, fg, *, name):
    tm = 256
    n_main = main.shape[0]

    def body(m_ref, f_ref, o_ref):
        full = jnp.concatenate([m_ref[n] for n in range(n_main)] + [f_ref[...]], axis=1)
        for q in range(N_DEV):
            o_ref[q] = full[:, IN_SHARD * q:IN_SHARD * (q + 1)]

    return pl.pallas_call(
        body, name=name, grid=(D_MODEL // tm,),
        in_specs=[pl.BlockSpec((n_main, tm, 512), lambda i: (0, i, 0)), pl.BlockSpec((tm, LANES), lambda i: (i, 0))],
        out_specs=pl.BlockSpec((N_DEV, tm, IN_SHARD), lambda i: (0, i, 0)),
        out_shape=jax.ShapeDtypeStruct((N_DEV, D_MODEL, IN_SHARD), bf16), compiler_params=_params(("arbitrary",)),
    )(main, fg)


def _block_diag(w):
    eye = jnp.eye(HEADS, dtype=w.dtype)
    return jnp.einsum("hij,hk->hikj", w, eye).reshape(LRU_W, LRU_W)


def _diag_blocks(dw):
    per = dw.shape[1] // HEAD_D
    blocks = [dw[:, HEAD_D * b:HEAD_D * (b + 1), HEAD_D * b:HEAD_D * (b + 1)] for b in range(per)]
    return jnp.stack(blocks, axis=1).reshape(HEADS, HEAD_D, HEAD_D)


def _local_step(x, target, sent, small, *, tm=512, tm_ffn=1024):
    ln1 = (small["ln1_g"], small["ln1_b"])
    ln2 = (small["ln2_g"], small["ln2_b"])
    ln3 = (small["ln3_g"], small["ln3_b"])

    xh1, rs1, hg1, hu1, wg1, wu1, wd1, w_in_g, w_out_g, conv_w_g = _ffn1_fwd_gathering(
        x, (sent["ffn1_w_gate"], sent["ffn1_w_up"], sent["ffn1_w_down"]),
        _Exchange([sent["w_in"], sent["w_out"], sent["conv_w"]], gather=True), tm=tm_ffn, name="ffn1_fwd")
    w_in = _w_in_join(w_in_g, name="w_in_join")
    w_out = w_out_g.reshape(D_MODEL, D_MODEL)
    conv_w = conv_w_g.transpose(1, 0, 2).reshape(4, LRU_W)
    qkv, zl, zfg = _in_proj(xh1, ln1[0], ln1[1], w_in, tm=tm, name="in_proj")
    bfg = jnp.pad(small["b_forget"], ((0, 0), (0, LANES - HEADS)))
    cum = _cum_fwd(zfg, bfg, name="cum_fwd")
    q_aug, k_aug, v_aug = _attn_prep_fwd(qkv, cum, tm=tm, name="attn_prep_fwd")
    o, lse, wg2, wu2 = _attn_fwd2(q_aug, k_aug, v_aug, name="attn_fwd",
                                  host=_Exchange([sent["ffn2_w_gate"], sent["ffn2_w_up"]], gather=True))
    wa_bd = _block_diag(small["rg_wa"]).astype(bf16)
    wx_bd = _block_diag(small["rg_wx"]).astype(bf16)
    ba = small["rg_ba"].reshape(1, LRU_W)
    bx = small["rg_bx"].reshape(1, LRU_W)
    u, h, lru, wd2 = _lru_fwd(zl, conv_w, small["conv_b"], wa_bd, ba, wx_bd, bx, small["lru_lambda"],
                              name="lru_fwd", host=_Exchange([sent["ffn2_w_down"]], gather=True))
    xh2, rs2 = _mmln([(o, 0, FOX_W, w_out, 0, D_MODEL, "nn"), (lru, 0, LRU_W, w_out, 1, D_MODEL, "nn")],
                     tm=tm, name="mix_fwd", resid=("affine", xh1) + ln1, resid_scale=ALPHA, epi="ln_fwd")
    xh3, rs3, hg2, hu2 = _ffn_fwd(xh2, ln2[0], ln2[1], wg2, wu2, wd2, tm=tm_ffn, name="ffn2_fwd")

    dpre3, sq_rows, g_ln3g, g_ln3b = _loss_bwd(xh3, rs3, ln3[0], ln3[1], target, tm=tm, name="loss_bwd")
    dpre2, g_ln2g, g_ln2b, dhg2, dhu2, a2 = _ffn_bwd(dpre3, hg2, hu2, wg2, wu2, wd2,
                                                     (xh2, rs2, ln2[0]), tm=tm_ffn, name="ffn2_bwd")
    wgrad = dict(out_dtype=bf16, tm=D_MODEL, mb=1, tn=FF_TILE, nb=4, tk=512, pair=True)
    wdgrad = dict(out_dtype=bf16, tm=512, mb=4, tn=D_MODEL, nb=1, tk=512, out_scale=0.5, pair=True)
    between_chips = functools.partial(_Exchange, gather=False, chips=True)
    g_wg2 = _mm_tn(xh2, dhg2, name="g_wg2", affine=ln2, **wgrad)
    g_wu2 = _mm_tn(xh2, dhu2, name="g_wu2", affine=ln2, **wgrad)
    g_wd2 = _mm_tn(a2, dpre3, name="g_wd2", **wdgrad)

    dmix = _mmln([(dpre2, 0, D_MODEL, w_out, 0, D_MODEL, "nt")], tm=tm, name="dmix_bwd")
    g_wout_a = _mm(o, dpre2, mode="tn", out_dtype=bf16, tm=512, tn=D_MODEL, tk=512, name="g_wout_fox")
    g_wout_b = _mm(lru, dpre2, mode="tn", out_dtype=bf16, tm=512, tn=D_MODEL, tk=512, name="g_wout_lru")
    g_wout_blocked = jnp.concatenate([g_wout_a, g_wout_b], axis=0).reshape(N_DEV, D_MODEL // N_DEV, D_MODEL)
    dlx, dlg, g_cw, g_cb, g_ba, g_bx, g_lam, g_wa4, g_wx4, p_wg2, p_wout = _lru_bwd(
        dmix, zl, u, h, conv_w, wa_bd, ba, wx_bd, bx, small["lru_lambda"], name="lru_bwd",
        host=_Hosts(between_chips([g_wg2]), _Exchange([g_wout_blocked], gather=False)))
    qb_aug, do_aug = _attn_prep_bwd(qkv, cum, lse, dmix, o, tm=tm, name="attn_prep_bwd")
    dq, dcum_q, dk, dv, dcum_k, p_wu2, p_wd2 = _attn_bwd(qb_aug, k_aug, v_aug, do_aug, name="attn_bwd",
                                                         host=between_chips([g_wu2, g_wd2]))
    dfg, g_bf = _cum_bwd(dcum_q, dcum_k, zfg, bfg, name="cum_bwd")

    dz = [(dq, 0, 512), (dk, 1, 512), (dv, 2, 512), (dlx, 3, 512), (dlg, 4, 512), (dfg, 20, LANES)]
    dpre1, g_ln1g, g_ln1b = _mmln(
        [(arr, 0, w, w_in, cb, w, "nt") for (arr, cb, w) in dz],
        tm=tm, name="dx1_bwd", resid=("plain", dpre2), resid_scale=ALPHA, epi="ln_bwd", ln=(xh1, rs1, ln1[0]))
    g_win_main = _mm_tn(xh1, [arr for arr, _, _ in dz[:5]], out_dtype=bf16, tm=D_MODEL, mb=1, tn=512, nb=5, tk=512,
                        name="g_win", affine=ln1, out_blocked=True)
    g_win_fg = _mm(xh1, dfg, mode="tn", out_dtype=bf16, tm=D_MODEL, tn=LANES, tk=512, name="g_win_fg", affine=ln1)
    g_win_blocked = _w_in_split(g_win_main, g_win_fg, name="w_in_split")
    dhg1, dhu1, a1, p_win = _ffn_bwd_act(dpre1, hg1, hu1, wd1, tm=tm_ffn, name="ffn1_bwd_act",
                                         host=_Exchange([g_win_blocked], gather=False))
    small_g = {
        "ln1_g": g_ln1g, "ln1_b": g_ln1b, "b_forget": g_bf[:, :HEADS], "conv_w": g_cw, "conv_b": g_cb,
        "rg_wa": _diag_blocks(g_wa4), "rg_ba": g_ba.reshape(HEADS, HEAD_D),
        "rg_wx": _diag_blocks(g_wx4), "rg_bx": g_bx.reshape(HEADS, HEAD_D), "lru_lambda": g_lam,
        "ln2_g": g_ln2g, "ln2_b": g_ln2b, "ln3_g": g_ln3g, "ln3_b": g_ln3b,
    }
    small_g["loss"] = (0.5 / D_MODEL) * jnp.sum(sq_rows, keepdims=True)
    pieces = [_pack_rows(small_g[n]) for n in PACKED]
    packed = jnp.concatenate(pieces + [jnp.zeros((PACK_ROWS - sum(p.shape[0] for p in pieces), LANES), f32)])
    g_wg1, all_packed = _mm_tn(x, dhg1, name="g_wg1", host=_Exchange([packed], gather=True), **wgrad)
    g_wu1, p_wg1 = _mm_tn(x, dhu1, name="g_wu1", host=between_chips([g_wg1]), **wgrad)
    g_wd1, p_wu1 = _mm_tn(a1, dpre1, name="g_wd1", host=between_chips([g_wu1]), **wdgrad)
    grad_x, p_wd1 = _ffn_bwd_dx(dpre1, dhg1, dhu1, wg1, wu1, tm=tm_ffn, name="ffn1_bwd_dx",
                                host=between_chips([g_wd1]))
    parts = {
        "ffn1_w_gate": p_wg1, "ffn1_w_up": p_wu1, "ffn1_w_down": p_wd1, "w_in": p_win, "w_out": p_wout,
        "ffn2_w_gate": p_wg2, "ffn2_w_up": p_wu2, "ffn2_w_down": p_wd2,
    }
    return grad_x, parts, all_packed, {n: small_g[n].shape for n in PACKED}


def _adam_math(w, g, m, v):
    m2 = ADAM_B1 * m + (1.0 - ADAM_B1) * g
    v2 = ADAM_B2 * v + (1.0 - ADAM_B2) * (g * g)
    m_hat = m2 / (1.0 - ADAM_B1 ** ADAM_STEP)
    v_hat = v2 / (1.0 - ADAM_B2 ** ADAM_STEP)
    delta = -ADAM_LR * (m_hat / (jnp.sqrt(v_hat) + ADAM_EPS) + ADAM_WD * w)
    return delta, m2, v2


ADAM_TILE_ELEMS = 128 * 1024


def _adamw_big(items, *, name):
    _, r, c = items[0][1].shape
    n_parts = items[0][0].shape[0]
    n_items = len(items)
    assert all(it[1].shape == (1, r, c) and it[0].shape == (n_parts, r, c) for it in items), name
    tr = max(d for d in range(8, r + 1, 8) if r % d == 0 and d * c <= ADAM_TILE_ELEMS)

    def body(*refs):
        ins, outs = refs[:4 * n_items], refs[4 * n_items:]
        for k in range(n_items):
            p_ref, w_ref, m_ref, v_ref = ins[4 * k:4 * k + 4]
            g = p_ref[0].astype(f32)
            for q in range(1, n_parts):
                g = g + p_ref[q].astype(f32)
            d, m2, v2 = _adam_math(w_ref[...], g, m_ref[...], v_ref[...])
            for o_ref, val in zip(outs[4 * k:4 * k + 4], (g, d, m2, v2)):
                o_ref[...] = val

    blk = pl.BlockSpec((None, tr, c), lambda i: (0, i, 0))
    res = pl.pallas_call(
        body, name=name, grid=(r // tr,),
        in_specs=[pl.BlockSpec((n_parts, tr, c), lambda i: (0, i, 0)), blk, blk, blk] * n_items,
        out_specs=[blk] * (4 * n_items), out_shape=[jax.ShapeDtypeStruct((1, r, c), f32)] * (4 * n_items),
        compiler_params=_params(("arbitrary",)),
    )(*[a for it in items for a in it])
    return [res[4 * k:4 * k + 4] for k in range(n_items)]


def _adamw_small(items, *, name):
    n = len(items)

    def body(*refs):
        ins, outs = refs[:4 * n], refs[4 * n:]
        for k in range(n):
            g, w, m, v = (ins[4 * k + q][...] for q in range(4))
            d, m2, v2 = _adam_math(w, g, m, v)
            outs[3 * k][...] = d
            outs[3 * k + 1][...] = m2
            outs[3 * k + 2][...] = v2

    vm = pl.BlockSpec(memory_space=pltpu.VMEM)
    flat = [a for item in items for a in item]
    out_shape = [jax.ShapeDtypeStruct(item[1].shape, f32) for item in items for _ in range(3)]
    return pl.pallas_call(
        body, name=name, in_specs=[vm] * (4 * n), out_specs=[vm] * (3 * n), out_shape=out_shape,
    )(*flat)


def _sum_parts(parts, *, name):
    def body(p_ref, o_ref):
        acc = p_ref[0]
        for q in range(1, N_DEV):
            acc = acc + p_ref[q]
        o_ref[...] = acc

    vm = pl.BlockSpec(memory_space=pltpu.VMEM)
    return pl.pallas_call(
        body, name=name, in_specs=[vm], out_specs=vm, out_shape=jax.ShapeDtypeStruct(parts.shape[1:], f32),
    )(parts)


WEIGHTS = ["ffn1_w_gate", "ffn1_w_up", "ffn1_w_down", "ln1_g", "ln1_b", "w_in", "b_forget", "conv_w", "conv_b",
           "rg_wa", "rg_ba", "rg_wx", "rg_bx", "lru_lambda", "w_out", "ln2_g", "ln2_b",
           "ffn2_w_gate", "ffn2_w_up", "ffn2_w_down", "ln3_g", "ln3_b"]
BIG = ["ffn1_w_gate", "ffn1_w_up", "ffn1_w_down", "w_in", "w_out", "ffn2_w_gate", "ffn2_w_up", "ffn2_w_down"]
PACKED = ["ln1_g", "ln1_b", "ln2_g", "ln2_b", "ln3_g", "ln3_b", "conv_b", "rg_ba", "rg_bx", "lru_lambda",
          "conv_w", "rg_wa", "rg_wx", "b_forget", "loss"]
PACK_ROWS = 600


def _two_d(a):
    return a.reshape((-1, a.shape[-1]))


def _transport(a):
    return _two_d(a)


def kernel(x, ffn1_w_gate, ffn1_w_up, ffn1_w_down, ln1_g, ln1_b, w_in, b_forget, conv_w, conv_b, rg_wa, rg_ba, rg_wx, rg_bx, lru_lambda, w_out, ln2_g, ln2_b, ffn2_w_gate, ffn2_w_up, ffn2_w_down, ln3_g, ln3_b, loss_target, m_ffn1_w_gate, m_ffn1_w_up, m_ffn1_w_down, m_ln1_g, m_ln1_b, m_w_in, m_b_forget, m_conv_w, m_conv_b, m_rg_wa, m_rg_ba, m_rg_wx, m_rg_bx, m_lru_lambda, m_w_out, m_ln2_g, m_ln2_b, m_ffn2_w_gate, m_ffn2_w_up, m_ffn2_w_down, m_ln3_g, m_ln3_b, v_ffn1_w_gate, v_ffn1_w_up, v_ffn1_w_down, v_ln1_g, v_ln1_b, v_w_in, v_b_forget, v_conv_w, v_conv_b, v_rg_wa, v_rg_ba, v_rg_wx, v_rg_bx, v_lru_lambda, v_w_out, v_ln2_g, v_ln2_b, v_ffn2_w_gate, v_ffn2_w_up, v_ffn2_w_down, v_ln3_g, v_ln3_b):
    w_args = (ffn1_w_gate, ffn1_w_up, ffn1_w_down, ln1_g, ln1_b, w_in, b_forget, conv_w, conv_b, rg_wa, rg_ba, rg_wx, rg_bx, lru_lambda, w_out, ln2_g, ln2_b, ffn2_w_gate, ffn2_w_up, ffn2_w_down, ln3_g, ln3_b)
    m_args = (m_ffn1_w_gate, m_ffn1_w_up, m_ffn1_w_down, m_ln1_g, m_ln1_b, m_w_in, m_b_forget, m_conv_w, m_conv_b, m_rg_wa, m_rg_ba, m_rg_wx, m_rg_bx, m_lru_lambda, m_w_out, m_ln2_g, m_ln2_b, m_ffn2_w_gate, m_ffn2_w_up, m_ffn2_w_down, m_ln3_g, m_ln3_b)
    v_args = (v_ffn1_w_gate, v_ffn1_w_up, v_ffn1_w_down, v_ln1_g, v_ln1_b, v_w_in, v_b_forget, v_conv_w, v_conv_b, v_rg_wa, v_rg_ba, v_rg_wx, v_rg_bx, v_lru_lambda, v_w_out, v_ln2_g, v_ln2_b, v_ffn2_w_gate, v_ffn2_w_up, v_ffn2_w_down, v_ln3_g, v_ln3_b)
    w = dict(zip(WEIGHTS, w_args))
    m = dict(zip(WEIGHTS, m_args))
    v = dict(zip(WEIGHTS, v_args))
    me = 4 * lax.axis_index("x") + 2 * lax.axis_index("y") + lax.axis_index("c")

    sent = {n: _transport(w[n]).astype(bf16) for n in BIG}
    sent["conv_w"] = _two_d(w["conv_w"])
    small = {n: w[n] for n in ("ln1_g", "ln1_b", "ln2_g", "ln2_b", "ln3_g", "ln3_b", "b_forget", "conv_b",
                               "lru_lambda")}
    small.update({n: w[n][0] for n in ("rg_wa", "rg_ba", "rg_wx", "rg_bx")})

    grad_x, parts, all_packed, small_shapes = _local_step(x[0], loss_target[0], sent, small)

    total = _sum_parts(all_packed, name="sum_small_grads")
    grads, off = {}, 0
    for n in PACKED:
        size = math.prod(small_shapes[n])
        rows = -(-size // LANES)
        grads[n] = total[off:off + rows].reshape(-1)[:size].reshape(small_shapes[n])
        off += rows
    loss = grads.pop("loss").reshape(())
    grads["conv_w"] = lax.dynamic_slice_in_dim(grads["conv_w"], me * (LRU_W // N_DEV), LRU_W // N_DEV, axis=1)

    delta, new_m, new_v = {}, {}, {}
    for group in (("ffn1_w_gate", "ffn1_w_up", "ffn2_w_gate", "ffn2_w_up"), ("ffn1_w_down", "ffn2_w_down"),
                  ("w_in",), ("w_out",)):
        done = _adamw_big([(parts[n], w[n], m[n], v[n]) for n in group], name="adamw_" + group[0])
        for n, (g, d, m2, v2) in zip(group, done):
            grads[n], delta[n], new_m[n], new_v[n] = g, d, m2, v2
    small_names = [n for n in WEIGHTS if n not in BIG]
    outs = _adamw_small([(_two_d(grads[n]), _two_d(w[n]), _two_d(m[n]), _two_d(v[n])) for n in small_names],
                        name="adamw_small")
    for k, n in enumerate(small_names):
        delta[n], new_m[n], new_v[n] = outs[3 * k], outs[3 * k + 1], outs[3 * k + 2]

    def shaped(d):
        return [d[n].reshape(w[n].shape) for n in WEIGHTS]

    return (loss, grad_x[None], *shaped(grads), *shaped(delta), *shaped(new_m), *shaped(new_v))
```

```python
import functools
import math

import jax
import jax.numpy as jnp
from jax import lax
from jax.experimental import pallas as pl
from jax.experimental.pallas import tpu as pltpu

f32 = jnp.float32
bf16 = jnp.bfloat16

N_DEV = 8
D_MODEL = 1024
D_FF = 4096
FF_TILE = D_FF // N_DEV
FOX_W = 512
LRU_W = 512
HEADS = 8
HEAD_D = 64
IN_COLS = 2568
IN_SHARD = IN_COLS // N_DEV
LANES = 128
LN_EPS = 1e-5
ALPHA = 2.0 ** 0.25
ATT_SCALE = 1.0 / math.sqrt(HEAD_D)
LRU_C = 8.0
NEG_BIG = -1e30

ADAM_LR = 0.001
ADAM_B1 = 0.9
ADAM_B2 = 0.999
ADAM_EPS = 1e-08
ADAM_WD = 0.01
ADAM_STEP = 10

VMEM_LIMIT = 56 * 1024 * 1024
MESH_T = pl.DeviceIdType.MESH


def _params(sem, **kw):
    return pltpu.CompilerParams(dimension_semantics=sem, vmem_limit_bytes=VMEM_LIMIT, **kw)


def _sigmoid(x):
    return 1.0 / (1.0 + jnp.exp(-x))


def _sigmoid_tanh(x):
    return 0.5 * jnp.tanh(0.5 * x) + 0.5


def _softplus(x):
    return jnp.maximum(x, 0.0) + jnp.log(1.0 + jnp.exp(-jnp.abs(x)))


def _one_minus_exp(x):
    series = -x * (1.0 + x * (0.5 + x * (1.0 / 6 + x * (1.0 / 24 + x * (1.0 / 120 + x * (1.0 / 720))))))
    return jnp.where(x > -0.125, series, 1.0 - jnp.exp(x))


_GELU_C = math.sqrt(2.0 / math.pi)


def _gelu_and_grad(x):
    inner = _GELU_C * (x + 0.044715 * x * x * x)
    t = jnp.tanh(inner)
    g = 0.5 * x * (1.0 + t)
    dg = 0.5 * (1.0 + t) + 0.5 * x * (1.0 - t * t) * _GELU_C * (1.0 + 3 * 0.044715 * x * x)
    return g, dg


def _ln_fwd_tile(pre):
    mu = jnp.mean(pre, axis=-1, keepdims=True)
    xc = pre - mu
    var = jnp.mean(xc * xc, axis=-1, keepdims=True)
    rstd = lax.rsqrt(var + LN_EPS)
    return xc * rstd, rstd


def _ln_bwd_tile(dy, xhat, rstd, g):
    dyg = dy * g
    m1 = jnp.mean(dyg, axis=-1, keepdims=True)
    m2 = jnp.mean(dyg * xhat, axis=-1, keepdims=True)
    dpre = rstd * (dyg - m1 - xhat * m2)
    return dpre, jnp.sum(dy * xhat, axis=0, keepdims=True), jnp.sum(dy, axis=0, keepdims=True)


_NT = (((1,), (1,)), ((), ()))
_TN = (((0,), (0,)), ((), ()))


class _Exchange:
    def __init__(self, arrs, gather, chips=False):
        self.arrs, self.gather, self.n, self.chips = list(arrs), gather, len(arrs), chips

    def out_shape(self):
        return [jax.ShapeDtypeStruct(((N_DEV,) + a.shape) if self.gather else a.shape, a.dtype) for a in self.arrs]

    def scratch(self):
        n_remote = self.n * (N_DEV - 1)
        return [pltpu.SemaphoreType.DMA((n_remote,)), pltpu.SemaphoreType.DMA((n_remote,)),
                pltpu.SemaphoreType.DMA((self.n,))]

    def copies(self, ins, outs, sems):
        send_sems, recv_sems, local_sems = sems
        x, y, c = lax.axis_index("x"), lax.axis_index("y"), lax.axis_index("c")
        me = 2 * x + y if self.chips else 4 * x + 2 * y + c
        out = []
        for k in range(self.n):
            for d in (range(2, N_DEV, 2) if self.chips else range(1, N_DEV)):
                px = 1 - x if d & 4 else x
                py = 1 - y if d & 2 else y
                pc = 1 - c if d & 1 else c
                sem = k * (N_DEV - 1) + d - 1
                out.append(pltpu.make_async_remote_copy(
                    src_ref=ins[k].at[2 * px + py if self.chips else 4 * px + 2 * py + pc], dst_ref=outs[k].at[me],
                    send_sem=send_sems.at[sem], recv_sem=recv_sems.at[sem],
                    device_id=(px, py, pc), device_id_type=MESH_T))
            out.append(pltpu.make_async_copy(ins[k].at[me], outs[k].at[me], local_sems.at[k]))
        return out

    def gather_copies(self, ins, outs, sems):
        send_sems, recv_sems, local_sems = sems
        x, y, c = lax.axis_index("x"), lax.axis_index("y"), lax.axis_index("c")
        sibling = (x, y, 1 - c)
        chips = [(1 - x, y), (x, 1 - y), (1 - x, 1 - y)]
        out = []
        for k in range(self.n):
            def copy(s, block, to, src=None, k=k):
                rows = outs[k].at[4 * block[0] + 2 * block[1] + block[2]]
                sem = k * (N_DEV - 1) + s
                return pltpu.make_async_remote_copy(
                    src_ref=rows if src is None else src, dst_ref=rows, send_sem=send_sems.at[sem],
                    recv_sem=recv_sems.at[sem], device_id=to, device_id_type=MESH_T)

            first = [copy(0, (x, y, c), sibling, src=ins[k])]
            first += [copy(1 + q, (x, y, c), (*chip, c), src=ins[k]) for q, chip in enumerate(chips)]
            passed = [copy(4 + q, (*chip, c), sibling) for q, chip in enumerate(chips)]
            own = pltpu.make_async_copy(ins[k], outs[k].at[4 * x + 2 * y + c], local_sems.at[k])
            out.append((first, passed, own, copy))
        return out, sibling, chips, (x, y, c)

    def start(self, ins, outs, sems):
        if not self.gather:
            for cp in self.copies(ins, outs, sems):
                cp.start()
            return
        per_array, _, _, _ = self.gather_copies(ins, outs, sems)
        for first, _, own, _ in per_array:
            own.start()
            for cp in first:
                cp.start()

    def relay(self, ins, outs, sems):
        per_array, sibling, chips, (x, y, c) = self.gather_copies(ins, outs, sems)
        for first, passed, own, copy in per_array:
            for q, chip in enumerate(chips):
                copy(1 + q, (*chip, c), (x, y, c)).wait_recv()
                passed[q].start()

    def wait(self, ins, outs, sems, relayed=False):
        if not self.gather:
            for cp in self.copies(ins, outs, sems):
                cp.wait()
            return
        if not relayed:
            self.relay(ins, outs, sems)
        per_array, sibling, chips, (x, y, c) = self.gather_copies(ins, outs, sems)
        for first, passed, own, copy in per_array:
            copy(0, sibling, (x, y, c)).wait_recv()
            for q, chip in enumerate(chips):
                copy(4 + q, (*chip, 1 - c), (x, y, c)).wait_recv()
            for cp in first + passed:
                cp.wait_send()
            own.wait()


class _Hosts:
    gather = False

    def __init__(self, *hosts, stages=None):
        self.hosts, self.stages = hosts, stages
        self.n = sum(h.n for h in hosts)
        self.arrs = [a for h in hosts for a in h.arrs]

    def out_shape(self):
        return [sh for h in self.hosts for sh in h.out_shape()]

    def scratch(self):
        return [sc for h in self.hosts for sc in h.scratch()]

    def _each(self, ins, outs, sems):
        at = 0
        for k, h in enumerate(self.hosts):
            yield h, ins[at:at + h.n], outs[at:at + h.n], sems[3 * k:3 * k + 3]
            at += h.n

    def start(self, ins, outs, sems, member=None):
        for k, (h, h_in, h_out, h_sems) in enumerate(self._each(ins, outs, sems)):
            if member is None or member == k:
                h.start(h_in, h_out, h_sems)

    def wait(self, ins, outs, sems, relayed=False):
        for h, h_in, h_out, h_sems in self._each(ins, outs, sems):
            h.wait(h_in, h_out, h_sems)


def _hosted_call(host, body, *, name, grid, in_specs, out_specs, out_shape, scratch_shapes=(), compiler_params):
    out_specs = list(out_specs) if isinstance(out_specs, (list, tuple)) else [out_specs]
    out_shape = list(out_shape) if isinstance(out_shape, (list, tuple)) else [out_shape]
    if host is None:
        return pl.pallas_call(body, name=name, grid=grid, in_specs=in_specs, out_specs=out_specs,
                              out_shape=out_shape, scratch_shapes=list(scratch_shapes),
                              compiler_params=compiler_params)
    n_in, n_out, n_scr, k = len(in_specs), len(out_shape), len(scratch_shapes), host.n

    def wrapped(*refs):
        ins, h_in = refs[:n_in], refs[n_in:n_in + k]
        outs, h_out = refs[n_in + k:n_in + k + n_out], refs[n_in + k + n_out:n_in + 2 * k + n_out]
        scr, sems = refs[n_in + 2 * k + n_out:n_in + 2 * k + n_out + n_scr], refs[n_in + 2 * k + n_out + n_scr:]
        ids = [pl.program_id(a) for a in range(len(grid))]
        first = functools.reduce(jnp.logical_and, [i == 0 for i in ids])
        last = functools.reduce(jnp.logical_and, [i == g - 1 for i, g in zip(ids, grid)])
        steps = math.prod(grid)
        relay_at = (3 * steps) // 4 if host.gather and steps >= 8 else None

        @pl.when(first)
        def _():
            host.start(h_in, h_out, sems)

        if relay_at is not None:
            coords, rest = [], relay_at
            for g in reversed(grid):
                coords.append(rest % g)
                rest //= g

            @pl.when(functools.reduce(jnp.logical_and, [i == cd for i, cd in zip(ids, reversed(coords))]))
            def _():
                host.relay(h_in, h_out, sems)

        body(*ins, *outs, *scr)

        @pl.when(last)
        def _():
            host.wait(h_in, h_out, sems, relayed=relay_at is not None)

    hbm = pl.BlockSpec(memory_space=pl.ANY)
    call = pl.pallas_call(
        wrapped, name=name, grid=grid, in_specs=list(in_specs) + [hbm] * k, out_specs=out_specs + [hbm] * k,
        out_shape=out_shape + host.out_shape(), scratch_shapes=list(scratch_shapes) + host.scratch(),
        compiler_params=compiler_params)
    return lambda *args: call(*args, *host.arrs)


def _ffn_fwd(xhat, g_in, b_in, wg, wu, wd, *, tm, name, host=None):
    t = xhat.shape[0]
    nj = N_DEV

    def body(x_ref, g_ref, b_ref, wg_ref, wu_ref, wd_ref, xo_ref, rstd_ref, hg_ref, hu_ref, xb, acc):
        j = pl.program_id(1)

        @pl.when(j == 0)
        def _():
            xb[...] = (x_ref[...] * g_ref[...] + b_ref[...]).astype(bf16)
            acc[...] = jnp.zeros_like(acc)

        hg = jnp.dot(xb[...], wg_ref[...], preferred_element_type=f32)
        hu = jnp.dot(xb[...], wu_ref[...], preferred_element_type=f32)
        hg_ref[...] = hg.astype(bf16)
        hu_ref[...] = hu.astype(bf16)
        a = hg * _sigmoid_tanh(hg) * hu
        acc[...] += jnp.dot(a.astype(bf16), wd_ref[...], preferred_element_type=f32)

        @pl.when(j == nj - 1)
        def _():
            x = x_ref[...] * g_ref[...] + b_ref[...]
            xo, rstd = _ln_fwd_tile(ALPHA * x + 0.5 * acc[...])
            xo_ref[...] = xo
            rstd_ref[...] = rstd

    row = pl.BlockSpec((1, D_MODEL), lambda i, j: (0, 0))
    return _hosted_call(
        host, body, name=name, grid=(t // tm, nj),
        in_specs=[pl.BlockSpec((tm, D_MODEL), lambda i, j: (i, 0)), row, row,
                  pl.BlockSpec((None, D_MODEL, FF_TILE), lambda i, j: (j, 0, 0)),
                  pl.BlockSpec((None, D_MODEL, FF_TILE), lambda i, j: (j, 0, 0)),
                  pl.BlockSpec((None, FF_TILE, D_MODEL), lambda i, j: (j, 0, 0))],
        out_specs=[pl.BlockSpec((tm, D_MODEL), lambda i, j: (i, 0)),
                   pl.BlockSpec((tm, 1), lambda i, j: (i, 0)),
                   pl.BlockSpec((tm, FF_TILE), lambda i, j: (i, j)),
                   pl.BlockSpec((tm, FF_TILE), lambda i, j: (i, j))],
        out_shape=[jax.ShapeDtypeStruct((t, D_MODEL), f32), jax.ShapeDtypeStruct((t, 1), f32),
                   jax.ShapeDtypeStruct((t, D_FF), bf16), jax.ShapeDtypeStruct((t, D_FF), bf16)],
        scratch_shapes=[pltpu.VMEM((tm, D_MODEL), bf16), pltpu.VMEM((tm, D_MODEL), f32)],
        compiler_params=_params(("arbitrary", "arbitrary")),
    )(xhat, g_in, b_in, wg, wu, wd)


def _ffn1_fwd_gathering(x, own, extra, *, tm, name):
    t = x.shape[0]
    n_i = t // tm
    n_arr = 3
    k_extra = extra.n
    ex = _Exchange(list(own), gather=True)
    ax, ay, ac = lax.axis_index("x"), lax.axis_index("y"), lax.axis_index("c")
    order = jnp.stack([4 * px + 2 * py + pc for px, py in ((ax, ay), (1 - ax, ay), (ax, 1 - ay), (1 - ax, 1 - ay))
                       for pc in (ac, 1 - ac)]).astype(jnp.int32)
    arrival = [None, (0, None), (1, 0), (4, None), (2, 1), (5, None), (3, 2), (6, None)]

    def body(order_ref, x_ref, *refs):
        w_in, e_in = refs[:n_arr], refs[n_arr:n_arr + k_extra]
        refs = refs[n_arr + k_extra:]
        xo_ref, rstd_ref, hg_ref, hu_ref = refs[:4]
        w_all, e_out = refs[4:4 + n_arr], refs[4 + n_arr:4 + n_arr + k_extra]
        acc, wgb, wub, wdb, fetch_sems, send_sems, recv_sems, local_sems = refs[4 + n_arr + k_extra:12 + n_arr + k_extra]
        e_sems = refs[12 + n_arr + k_extra:]
        bufs = (wgb, wub, wdb)
        s = pl.program_id(0)
        i = pl.program_id(1)
        per_array, sibling, chips, (x_, y_, c_) = ex.gather_copies(w_in, w_all, (send_sems, recv_sems, local_sems))

        def fetch(pos, slot):
            return [pltpu.make_async_copy(w_in[a] if pos == 0 else w_all[a].at[order_ref[pos]],
                                          bufs[a].at[slot], fetch_sems.at[n_arr * slot + a]) for a in range(n_arr)]

        def source_of(pos):
            chip = (x_, y_) if pos < 2 else chips[(pos - 2) // 2]
            return (*chip, c_ if pos % 2 == 0 else 1 - c_)

        @pl.when(jnp.logical_and(s == 0, i == 0))
        def _():
            for q in range(4):
                for first, _, own_copy, _ in per_array:
                    if q == 0:
                        own_copy.start()
                    first[q].start()
            for cp in fetch(0, 0):
                cp.start()
            for cp in fetch(0, 0):
                cp.wait()

        for member, stage in enumerate(extra.stages):
            @pl.when(jnp.logical_and(s == stage, i == 0))
            def _(member=member):
                extra.start(e_in, e_out, e_sems, member=member)

        for pos in range(1, N_DEV):
            @pl.when(jnp.logical_and(s == pos - 1, i == n_i - 1))
            def _(pos=pos):
                sem, passes = arrival[pos]
                for _, passed, _, copy in per_array:
                    copy(sem, source_of(pos), (x_, y_, c_)).wait_recv()
                    if passes is not None:
                        passed[passes].start()
                for cp in fetch(pos, pos % 2):
                    cp.start()

            @pl.when(jnp.logical_and(s == pos, i == 0))
            def _(pos=pos):
                for cp in fetch(pos, pos % 2):
                    cp.wait()

        slot = s % 2
        xb = x_ref[...].astype(bf16)
        hg = jnp.dot(xb, wgb[slot], preferred_element_type=f32)
        hu = jnp.dot(xb, wub[slot], preferred_element_type=f32)
        hg_ref[...] = hg.astype(bf16)
        hu_ref[...] = hu.astype(bf16)
        a = hg * _sigmoid_tanh(hg) * hu
        part = jnp.dot(a.astype(bf16), wdb[slot], preferred_element_type=f32)

        @pl.when(s == 0)
        def _():
            acc[i] = part

        @pl.when(s > 0)
        def _():
            acc[i] += part

        @pl.when(s == N_DEV - 1)
        def _():
            xo, rstd = _ln_fwd_tile(ALPHA * x_ref[...] + 0.5 * acc[i])
            xo_ref[...] = xo
            rstd_ref[...] = rstd

        @pl.when(jnp.logical_and(s == N_DEV - 1, i == n_i - 1))
        def _():
            for first, passed, own_copy, _ in per_array:
                for cp in first + passed:
                    cp.wait_send()
                own_copy.wait()
            extra.wait(e_in, e_out, e_sems)

    hbm = pl.BlockSpec(memory_space=pl.ANY)
    last = N_DEV - 1
    tok_out = pl.BlockSpec((tm, D_MODEL), lambda s, i, o: (jnp.where(s == last, i, 0), 0))
    col_out = pl.BlockSpec((tm, 1), lambda s, i, o: (jnp.where(s == last, i, 0), 0))
    hid = pl.BlockSpec((tm, FF_TILE), lambda s, i, o: (i, o[s]))
    shard_shapes = [(N_DEV,) + w.shape for w in own]
    grid_spec = pltpu.PrefetchScalarGridSpec(
        num_scalar_prefetch=1, grid=(N_DEV, n_i),
        in_specs=[pl.BlockSpec((tm, D_MODEL), lambda s, i, o: (i, 0))] + [hbm] * (n_arr + k_extra),
        out_specs=[tok_out, col_out, hid, hid] + [hbm] * (n_arr + k_extra),
        scratch_shapes=[pltpu.VMEM((n_i, tm, D_MODEL), f32)]
        + [pltpu.VMEM((2,) + w.shape, bf16) for w in own]
        + [pltpu.SemaphoreType.DMA((2 * n_arr,))] + ex.scratch() + extra.scratch())
    res = pl.pallas_call(
        body, name=name, grid_spec=grid_spec,
        out_shape=[jax.ShapeDtypeStruct((t, D_MODEL), f32), jax.ShapeDtypeStruct((t, 1), f32),
                   jax.ShapeDtypeStruct((t, D_FF), bf16), jax.ShapeDtypeStruct((t, D_FF), bf16)]
        + [jax.ShapeDtypeStruct(sh, bf16) for sh in shard_shapes] + extra.out_shape(),
        compiler_params=_params(("arbitrary", "arbitrary")),
    )(order, x, *own, *extra.arrs)
    return res


def _ffn_bwd(dpre, hg, hu, wg, wu, wd, ln_in, *, tm, name, host=None):
    t = dpre.shape[0]
    nj = N_DEV
    with_ln = ln_in is not None

    def body(*refs):
        if with_ln:
            (dp_ref, hg_ref, hu_ref, wg_ref, wu_ref, wd_ref, xh_ref, rs_ref, g_ref,
             dx_ref, gg_ref, gb_ref, dhg_ref, dhu_ref, a_ref, dfb, acc) = refs
        else:
            (dp_ref, hg_ref, hu_ref, wg_ref, wu_ref, wd_ref,
             dx_ref, dhg_ref, dhu_ref, a_ref, dfb, acc) = refs
        i = pl.program_id(0)
        j = pl.program_id(1)

        @pl.when(j == 0)
        def _():
            dfb[...] = (0.5 * dp_ref[...]).astype(bf16)
            acc[...] = jnp.zeros_like(acc)

        da = lax.dot_general(dfb[...], wd_ref[...], _NT, preferred_element_type=f32)
        hgv = hg_ref[...].astype(f32)
        huv = hu_ref[...].astype(f32)
        sg = _sigmoid_tanh(hgv)
        silu = hgv * sg
        a_ref[...] = (silu * huv).astype(bf16)
        dhu = (da * silu).astype(bf16)
        dhg = (da * huv * (sg * (1.0 + hgv * (1.0 - sg)))).astype(bf16)
        dhg_ref[...] = dhg
        dhu_ref[...] = dhu
        acc[...] += (lax.dot_general(dhg, wg_ref[...], _NT, preferred_element_type=f32)
                     + lax.dot_general(dhu, wu_ref[...], _NT, preferred_element_type=f32))

        @pl.when(j == nj - 1)
        def _():
            dx = ALPHA * dp_ref[...] + acc[...]
            if with_ln:
                dprev, gg, gb = _ln_bwd_tile(dx, xh_ref[...], rs_ref[...], g_ref[...])
                dx_ref[...] = dprev

                @pl.when(i == 0)
                def _():
                    gg_ref[...] = gg
                    gb_ref[...] = gb

                @pl.when(i > 0)
                def _():
                    gg_ref[...] += gg
                    gb_ref[...] += gb
            else:
                dx_ref[...] = dx

    tok = pl.BlockSpec((tm, D_MODEL), lambda i, j: (i, 0), pipeline_mode=pl.Buffered(1))
    row = pl.BlockSpec((1, D_MODEL), lambda i, j: (0, 0))
    hid = pl.BlockSpec((tm, FF_TILE), lambda i, j: (i, j))
    in_specs = [tok, hid, hid,
                pl.BlockSpec((None, D_MODEL, FF_TILE), lambda i, j: (j, 0, 0)),
                pl.BlockSpec((None, D_MODEL, FF_TILE), lambda i, j: (j, 0, 0)),
                pl.BlockSpec((None, FF_TILE, D_MODEL), lambda i, j: (j, 0, 0))]
    args = [dpre, hg, hu, wg, wu, wd]
    out_specs = [tok]
    out_shape = [jax.ShapeDtypeStruct((t, D_MODEL), f32)]
    if with_ln:
        in_specs += [tok, pl.BlockSpec((tm, 1), lambda i, j: (i, 0)), row]
        args += list(ln_in)
        out_specs += [row, row]
        out_shape += [jax.ShapeDtypeStruct((1, D_MODEL), f32)] * 2
    out_specs += [hid, hid, hid]
    out_shape += [jax.ShapeDtypeStruct((t, D_FF), bf16)] * 3
    return _hosted_call(
        host, body, name=name, grid=(t // tm, nj), in_specs=in_specs, out_specs=out_specs, out_shape=out_shape,
        scratch_shapes=[pltpu.VMEM((tm, D_MODEL), bf16), pltpu.VMEM((tm, D_MODEL), f32)],
        compiler_params=_params(("arbitrary", "arbitrary")),
    )(*args)


def _ffn_bwd_act(dpre, hg, hu, wd, *, tm, name, host=None):
    t = dpre.shape[0]

    def body(dp_ref, hg_ref, hu_ref, wd_ref, dhg_ref, dhu_ref, a_ref, dfb):
        @pl.when(pl.program_id(1) == 0)
        def _():
            dfb[...] = (0.5 * dp_ref[...]).astype(bf16)

        da = lax.dot_general(dfb[...], wd_ref[...], _NT, preferred_element_type=f32)
        hgv = hg_ref[...].astype(f32)
        huv = hu_ref[...].astype(f32)
        sg = _sigmoid_tanh(hgv)
        silu = hgv * sg
        a_ref[...] = (silu * huv).astype(bf16)
        dhu_ref[...] = (da * silu).astype(bf16)
        dhg_ref[...] = (da * huv * (sg * (1.0 + hgv * (1.0 - sg)))).astype(bf16)

    hid = pl.BlockSpec((tm, FF_TILE), lambda i, j: (i, j))
    return _hosted_call(
        host, body, name=name, grid=(t // tm, N_DEV),
        in_specs=[pl.BlockSpec((tm, D_MODEL), lambda i, j: (i, 0)), hid, hid,
                  pl.BlockSpec((None, FF_TILE, D_MODEL), lambda i, j: (j, 0, 0))],
        out_specs=[hid, hid, hid], out_shape=[jax.ShapeDtypeStruct((t, D_FF), bf16)] * 3,
        scratch_shapes=[pltpu.VMEM((tm, D_MODEL), bf16)],
        compiler_params=_params(("arbitrary", "arbitrary")),
    )(dpre, hg, hu, wd)


def _ffn_bwd_dx(dpre, dhg, dhu, wg, wu, *, tm, name, host=None):
    t = dpre.shape[0]
    nj = N_DEV

    def body(dp_ref, dhg_ref, dhu_ref, wg_ref, wu_ref, dx_ref, acc):
        j = pl.program_id(1)

        @pl.when(j == 0)
        def _():
            acc[...] = jnp.zeros_like(acc)

        acc[...] += (lax.dot_general(dhg_ref[...], wg_ref[...], _NT, preferred_element_type=f32)
                     + lax.dot_general(dhu_ref[...], wu_ref[...], _NT, preferred_element_type=f32))

        @pl.when(j == nj - 1)
        def _():
            dx_ref[...] = ALPHA * dp_ref[...] + acc[...]

    tok = pl.BlockSpec((tm, D_MODEL), lambda i, j: (i, 0))
    hid = pl.BlockSpec((tm, FF_TILE), lambda i, j: (i, j))
    wspec = pl.BlockSpec((None, D_MODEL, FF_TILE), lambda i, j: (j, 0, 0))
    return _hosted_call(
        host, body, name=name, grid=(t // tm, nj), in_specs=[tok, hid, hid, wspec, wspec],
        out_specs=[tok], out_shape=[jax.ShapeDtypeStruct((t, D_MODEL), f32)],
        scratch_shapes=[pltpu.VMEM((tm, D_MODEL), f32)],
        compiler_params=_params(("arbitrary", "arbitrary")),
    )(dpre, dhg, dhu, wg, wu)


def _mm(a, b, *, mode, out_dtype, tm, tn, tk, name, affine=None, a_cols=None, b_cols=None,
        b_blocked=False, out_blocked=False, out_scale=None):
    if mode == "nn":
        m_full, k_full = a.shape
        m_dim, k_dim = (m_full, a_cols[1]) if a_cols else (m_full, k_full)
    else:
        k_dim, m_full = a.shape
        m_dim = a_cols[1] if a_cols else m_full
    a_off = a_cols[0] if a_cols else 0
    if b_blocked:
        n_dim = b.shape[0] * b.shape[2]
        assert b.shape[2] == tn
    else:
        n_dim = b_cols[1] if b_cols else b.shape[1]
    b_off = b_cols[0] if b_cols else 0
    assert m_dim % tm == 0 and n_dim % tn == 0 and k_dim % tk == 0, (name, m_dim, n_dim, k_dim)
    nk = k_dim // tk

    def body(*refs):
        if affine is not None:
            a_ref, g_ref, s_ref, b_ref, o_ref, acc = refs
        else:
            a_ref, b_ref, o_ref, acc = refs
        k = pl.program_id(2)

        @pl.when(k == 0)
        def _():
            acc[...] = jnp.zeros_like(acc)

        av = a_ref[...]
        if affine is not None:
            av = av * g_ref[...] + s_ref[...]
        av = av.astype(bf16)
        bv = b_ref[...].astype(bf16)
        if mode == "nn":
            acc[...] += jnp.dot(av, bv, preferred_element_type=f32)
        else:
            acc[...] += lax.dot_general(av, bv, _TN, preferred_element_type=f32)

        @pl.when(k == nk - 1)
        def _():
            res = acc[...] if out_scale is None else acc[...] * out_scale
            o_ref[...] = res.astype(out_dtype)

    if mode == "nn":
        a_spec = pl.BlockSpec((tm, tk), lambda i, j, k: (i, k + a_off))
        aff_spec = pl.BlockSpec((1, tk), lambda i, j, k: (0, k + a_off))
    else:
        a_spec = pl.BlockSpec((tk, tm), lambda i, j, k: (k, i + a_off))
        aff_spec = pl.BlockSpec((1, tm), lambda i, j, k: (0, i + a_off))
    if b_blocked:
        b_spec = pl.BlockSpec((None, tk, tn), lambda i, j, k: (j, k, 0))
    else:
        b_spec = pl.BlockSpec((tk, tn), lambda i, j, k: (k, j + b_off))
    if out_blocked:
        o_spec = pl.BlockSpec((None, tm, tn), lambda i, j, k: (j, i, 0))
        o_shape = jax.ShapeDtypeStruct((n_dim // tn, m_dim, tn), out_dtype)
    else:
        o_spec = pl.BlockSpec((tm, tn), lambda i, j, k: (i, j))
        o_shape = jax.ShapeDtypeStruct((m_dim, n_dim), out_dtype)
    in_specs = [a_spec] + ([aff_spec, aff_spec] if affine is not None else []) + [b_spec]
    args = [a] + (list(affine) if affine is not None else []) + [b]
    return pl.pallas_call(
        body, name=name, grid=(m_dim // tm, n_dim // tn, nk), in_specs=in_specs, out_specs=o_spec,
        out_shape=o_shape, scratch_shapes=[pltpu.VMEM((tm, tn), f32)],
        compiler_params=_params(("arbitrary", "arbitrary", "arbitrary")),
    )(*args)


def _mm_tn(a, b, *, out_dtype, tm, mb, tn, nb, tk, name, affine=None, out_blocked=False, out_scale=None,
           pair=False, host=None):
    k_dim, m_dim = a.shape
    multi_b = isinstance(b, (list, tuple))
    b_list = list(b) if multi_b else [b]
    n_dim = nb * tn if multi_b else b.shape[1]
    assert m_dim % (mb * tm) == 0 and n_dim % (nb * tn) == 0 and k_dim % tk == 0, (name, m_dim, n_dim, k_dim)
    nk = k_dim // tk
    grid = (m_dim // (mb * tm), n_dim // (nb * tn), nk)
    if pair:
        assert mb * nb == 4 and grid[0] * grid[1] == 2 and out_dtype == bf16, name

    def body(*refs):
        if pair:
            refs, (acc, send_buf, recv_buf, send_sems, recv_sems) = refs[:-5], refs[-5:]
        else:
            refs, acc = refs[:-1], refs[-1]
        a_ref, o_ref = refs[0], refs[-1]
        if affine is not None:
            g_ref, s_ref = refs[1:3]
        b_refs = refs[3 if affine is not None else 1:-1]
        k = pl.program_id(2)

        @pl.when(k == 0)
        def _():
            acc[...] = jnp.zeros_like(acc)

        av = a_ref[...]
        if affine is not None:
            av = av * g_ref[...] + s_ref[...]
        av = av.astype(bf16)
        if multi_b:
            pieces = [r[...].astype(bf16) for r in b_refs]
        else:
            bv = b_refs[0][...].astype(bf16)
            pieces = [bv[:, jn * tn:(jn + 1) * tn] for jn in range(nb)]
        for im in range(mb):
            a_t = av[:, im * tm:(im + 1) * tm].T
            for jn in range(nb):
                acc[im * nb + jn] += jnp.dot(a_t, pieces[jn], preferred_element_type=f32)

        def scaled(v):
            return v if out_scale is None else v * out_scale

        @pl.when(k == nk - 1)
        def _():
            if pair:
                x, y, c = lax.axis_index("x"), lax.axis_index("y"), lax.axis_index("c")
                window = pl.program_id(0) + pl.program_id(1)
                swaps = []
                for cc in range(2):
                    send_buf[cc] = scaled(acc[2 * cc + 1 - c]).astype(bf16)
                    swaps.append(pltpu.make_async_remote_copy(
                        src_ref=send_buf.at[cc], dst_ref=recv_buf.at[window, cc],
                        send_sem=send_sems.at[2 * window + cc], recv_sem=recv_sems.at[2 * window + cc],
                        device_id=(x, y, 1 - c), device_id_type=MESH_T))
                    swaps[cc].start()
                for cc in range(2):
                    swaps[cc].wait_recv()
                    o_ref[cc] = (scaled(acc[2 * cc + c]) + recv_buf[window, cc].astype(f32)).astype(bf16)
                for cc in range(2):
                    swaps[cc].wait_send()
                return
            for im in range(mb):
                for jn in range(nb):
                    res = scaled(acc[im * nb + jn])
                    if out_blocked:
                        o_ref[jn, im * tm:(im + 1) * tm, :] = res.astype(out_dtype)
                    else:
                        o_ref[im * tm:(im + 1) * tm, jn * tn:(jn + 1) * tn] = res.astype(out_dtype)

    a_spec = pl.BlockSpec((tk, mb * tm), lambda i, j, k: (k, i))
    aff_spec = pl.BlockSpec((1, mb * tm), lambda i, j, k: (0, i))
    if multi_b:
        b_specs = [pl.BlockSpec((tk, tn), lambda i, j, k: (k, 0))] * nb
    else:
        b_specs = [pl.BlockSpec((tk, nb * tn), lambda i, j, k: (k, j))]
    scratch = [pltpu.VMEM((mb * nb, tm, tn), f32)]
    if pair:
        o_spec = pl.BlockSpec((2, tm, tn), lambda i, j, k: (i + j, 0, 0))
        o_shape = jax.ShapeDtypeStruct((4, tm, tn), out_dtype)
        scratch += [pltpu.VMEM((2, tm, tn), bf16), pltpu.VMEM((2, 2, tm, tn), bf16),
                    pltpu.SemaphoreType.DMA((4,)), pltpu.SemaphoreType.DMA((4,))]
    elif out_blocked:
        o_spec = pl.BlockSpec((nb, mb * tm, tn), lambda i, j, k: (j, i, 0))
        o_shape = jax.ShapeDtypeStruct((n_dim // tn, m_dim, tn), out_dtype)
    else:
        o_spec = pl.BlockSpec((mb * tm, nb * tn), lambda i, j, k: (i, j))
        o_shape = jax.ShapeDtypeStruct((m_dim, n_dim), out_dtype)
    in_specs = [a_spec] + ([aff_spec, aff_spec] if affine is not None else []) + b_specs
    args = [a] + (list(affine) if affine is not None else []) + b_list
    res = _hosted_call(
        host, body, name=name, grid=grid, in_specs=in_specs, out_specs=o_spec, out_shape=o_shape,
        scratch_shapes=scratch, compiler_params=_params(("arbitrary", "arbitrary", "arbitrary")),
    )(*args)
    return res[0] if host is None else res


def _in_proj(xhat, g, b, w_in, *, tm, name):
    t = xhat.shape[0]
    n_qkv, n_l = 3 * FOX_W, 2 * LRU_W

    def body(x_ref, g_ref, b_ref, w_ref, qkv_ref, zl_ref, zfg_ref):
        xb = (x_ref[...] * g_ref[...] + b_ref[...]).astype(bf16)
        qkv_ref[...] = jnp.dot(xb, w_ref[:, :n_qkv], preferred_element_type=f32).astype(bf16)
        zl_ref[...] = jnp.dot(xb, w_ref[:, n_qkv:n_qkv + n_l], preferred_element_type=f32)
        zfg_ref[...] = jnp.dot(xb, w_ref[:, n_qkv + n_l:], preferred_element_type=f32)

    row = pl.BlockSpec((1, D_MODEL), lambda i: (0, 0))
    return pl.pallas_call(
        body, name=name, grid=(t // tm,),
        in_specs=[pl.BlockSpec((tm, D_MODEL), lambda i: (i, 0)), row, row,
                  pl.BlockSpec(w_in.shape, lambda i: (0, 0))],
        out_specs=[pl.BlockSpec((tm, n_qkv), lambda i: (i, 0)), pl.BlockSpec((tm, n_l), lambda i: (i, 0)),
                   pl.BlockSpec((tm, LANES), lambda i: (i, 0))],
        out_shape=[jax.ShapeDtypeStruct((t, n_qkv), bf16), jax.ShapeDtypeStruct((t, n_l), f32),
                   jax.ShapeDtypeStruct((t, LANES), f32)],
        compiler_params=_params(("arbitrary",)),
    )(xhat, g, b, w_in)


def _mmln(pairs, *, tm, name, resid=None, resid_scale=1.0, epi=None, ln=None, n_out=D_MODEL):
    t = pairs[0][0].shape[0]
    n_pairs = len(pairs)
    n_resid = 0 if resid is None else len(resid) - 1

    def body(*refs):
        pos = 0
        val = None
        for p in range(n_pairs):
            a_ref, b_ref = refs[pos], refs[pos + 1]
            pos += 2
            av = a_ref[...].astype(bf16)
            bv = b_ref[...].astype(bf16)
            if pairs[p][6] == "nn":
                term = jnp.dot(av, bv, preferred_element_type=f32)
            else:
                term = lax.dot_general(av, bv, _NT, preferred_element_type=f32)
            val = term if val is None else val + term
        if resid is not None:
            if resid[0] == "plain":
                r = refs[pos][...]
            else:
                r = refs[pos][...] * refs[pos + 1][...] + refs[pos + 2][...]
            pos += n_resid
            val = val + resid_scale * r
        if epi is None:
            o_ref = refs[pos]
            o_ref[...] = val.astype(o_ref.dtype)
        elif epi == "ln_fwd":
            xo, rstd = _ln_fwd_tile(val)
            refs[pos][...] = xo
            refs[pos + 1][...] = rstd
        else:
            xh_ref, rs_ref, g_ref, dx_ref, gg_ref, gb_ref = refs[pos:pos + 6]
            dprev, gg, gb = _ln_bwd_tile(val, xh_ref[...], rs_ref[...], g_ref[...])
            dx_ref[...] = dprev
            i = pl.program_id(0)

            @pl.when(i == 0)
            def _():
                gg_ref[...] = gg
                gb_ref[...] = gb

            @pl.when(i > 0)
            def _():
                gg_ref[...] += gg
                gb_ref[...] += gb

    in_specs, args = [], []
    for (a, acb, aw, b, bcb, bw, mode) in pairs:
        in_specs.append(pl.BlockSpec((tm, aw), lambda i, acb=acb: (i, acb)))
        args.append(a)
        if mode == "nn":
            in_specs.append(pl.BlockSpec((aw, n_out), lambda i, bcb=bcb: (bcb, 0)))
        else:
            in_specs.append(pl.BlockSpec((n_out, bw), lambda i, bcb=bcb: (0, bcb)))
        args.append(b)
    tok = pl.BlockSpec((tm, n_out), lambda i: (i, 0))
    row = pl.BlockSpec((1, n_out), lambda i: (0, 0))
    col = pl.BlockSpec((tm, 1), lambda i: (i, 0))
    if resid is not None:
        in_specs += [tok] if resid[0] == "plain" else [tok, row, row]
        args += list(resid[1:])
    if epi is None:
        out_specs, out_shape = tok, jax.ShapeDtypeStruct((t, n_out), f32)
    elif epi == "ln_fwd":
        out_specs = [tok, col]
        out_shape = [jax.ShapeDtypeStruct((t, n_out), f32), jax.ShapeDtypeStruct((t, 1), f32)]
    else:
        in_specs += [tok, col, row]
        args += list(ln)
        out_specs = [tok, row, row]
        out_shape = [jax.ShapeDtypeStruct((t, n_out), f32)] + [jax.ShapeDtypeStruct((1, n_out), f32)] * 2
    return pl.pallas_call(
        body, name=name, grid=(t // tm,), in_specs=in_specs, out_specs=out_specs, out_shape=out_shape,
        compiler_params=_params(("arbitrary",)),
    )(*args)


def _loss_bwd(xhat, rstd, g, b, target, *, tm, name):
    t = xhat.shape[0]

    def body(xh_ref, rs_ref, g_ref, b_ref, tg_ref, dx_ref, sq_ref, gg_ref, gb_ref):
        i = pl.program_id(0)
        xh = xh_ref[...]
        diff = xh * g_ref[...] + b_ref[...] - tg_ref[...]
        sq = jnp.sum(diff * diff, axis=0, keepdims=True)
        dprev, gg, gb = _ln_bwd_tile(diff * (1.0 / D_MODEL), xh, rs_ref[...], g_ref[...])
        dx_ref[...] = dprev

        @pl.when(i == 0)
        def _():
            sq_ref[...] = sq
            gg_ref[...] = gg
            gb_ref[...] = gb

        @pl.when(i > 0)
        def _():
            sq_ref[...] += sq
            gg_ref[...] += gg
            gb_ref[...] += gb

    tok = pl.BlockSpec((tm, D_MODEL), lambda i: (i, 0))
    row = pl.BlockSpec((1, D_MODEL), lambda i: (0, 0))
    return pl.pallas_call(
        body, name=name, grid=(t // tm,),
        in_specs=[tok, pl.BlockSpec((tm, 1), lambda i: (i, 0)), row, row, tok],
        out_specs=[tok, row, row, row],
        out_shape=[jax.ShapeDtypeStruct((t, D_MODEL), f32)] + [jax.ShapeDtypeStruct((1, D_MODEL), f32)] * 3,
        compiler_params=_params(("arbitrary",)),
    )(xhat, rstd, g, b, target)


CUM_TILE = 256


def _tri(n, lower):
    r = lax.broadcasted_iota(jnp.int32, (n, n), 0)
    c = lax.broadcasted_iota(jnp.int32, (n, n), 1)
    return jnp.where((r >= c) if lower else (r <= c), 1.0, 0.0).astype(f32)


def _cum_fwd(zfg, bfg, *, name):
    t = zfg.shape[0]

    def body(z_ref, b_ref, o_ref, carry):
        @pl.when(pl.program_id(0) == 0)
        def _():
            carry[...] = jnp.zeros_like(carry)

        ls = -_softplus(-(z_ref[...] + b_ref[...]))
        c = jnp.dot(_tri(CUM_TILE, True), ls, preferred_element_type=f32,
                    precision=lax.Precision.HIGHEST) + carry[...]
        o_ref[...] = c
        carry[...] = c[CUM_TILE - 1:CUM_TILE, :]

    blk = pl.BlockSpec((CUM_TILE, LANES), lambda i: (i, 0))
    return pl.pallas_call(
        body, name=name, grid=(t // CUM_TILE,),
        in_specs=[blk, pl.BlockSpec((1, LANES), lambda i: (0, 0))], out_specs=blk,
        out_shape=jax.ShapeDtypeStruct((t, LANES), f32), scratch_shapes=[pltpu.VMEM((1, LANES), f32)],
        compiler_params=_params(("arbitrary",)),
    )(zfg, bfg)


def _cum_bwd(dcum_q, dcum_k, zfg, bfg, *, name):
    t = zfg.shape[0]
    n = t // CUM_TILE

    def body(d_ref, d2_ref, z_ref, b_ref, o_ref, s_ref, carry):
        i = pl.program_id(0)

        @pl.when(i == 0)
        def _():
            carry[...] = jnp.zeros_like(carry)

        dls = jnp.dot(_tri(CUM_TILE, False), d_ref[...] + d2_ref[...], preferred_element_type=f32,
                      precision=lax.Precision.HIGHEST) + carry[...]
        carry[...] = dls[0:1, :]
        lane = lax.broadcasted_iota(jnp.int32, (CUM_TILE, LANES), 1)
        dfg = jnp.where(lane < HEADS, dls * _sigmoid(-(z_ref[...] + b_ref[...])), 0.0)
        o_ref[...] = dfg
        tot = jnp.sum(dfg, axis=0, keepdims=True)

        @pl.when(i == 0)
        def _():
            s_ref[...] = tot

        @pl.when(i > 0)
        def _():
            s_ref[...] += tot

    blk = pl.BlockSpec((CUM_TILE, LANES), lambda i: (n - 1 - i, 0))
    row = pl.BlockSpec((1, LANES), lambda i: (0, 0))
    return pl.pallas_call(
        body, name=name, grid=(n,), in_specs=[blk, blk, blk, row], out_specs=[blk, row],
        out_shape=[jax.ShapeDtypeStruct((t, LANES), f32), jax.ShapeDtypeStruct((1, LANES), f32)],
        scratch_shapes=[pltpu.VMEM((1, LANES), f32)],
        compiler_params=_params(("arbitrary",)),
    )(dcum_q, dcum_k, zfg, bfg)


ATT_TILE = 512


def _causal(i, j, transposed):
    r = lax.broadcasted_iota(jnp.int32, (ATT_TILE, ATT_TILE), 0)
    c = lax.broadcasted_iota(jnp.int32, (ATT_TILE, ATT_TILE), 1)
    if transposed:
        return (c + i * ATT_TILE) >= (r + j * ATT_TILE)
    return (r + i * ATT_TILE) >= (c + j * ATT_TILE)


ATT_W = HEADS * LANES


def _data_lane(h):
    return HEAD_D * (h % 2)


def _extra_lane(h):
    return HEAD_D - _data_lane(h)


def _split3(x):
    hi = x.astype(bf16)
    rest = x - hi.astype(f32)
    mid = rest.astype(bf16)
    lo = (rest - mid.astype(f32)).astype(bf16)
    return hi, mid, lo


def _three_pieces(x):
    hi, mid, lo = (p.astype(f32) for p in _split3(x))
    return (hi + pltpu.roll(mid, HEADS, axis=1) + pltpu.roll(lo, 2 * HEADS, axis=1)).astype(bf16)


def _move(h, first):
    r = lax.broadcasted_iota(jnp.int32, (LANES, LANES), 0)
    c = lax.broadcasted_iota(jnp.int32, (LANES, LANES), 1)
    hit = functools.reduce(jnp.logical_or, [jnp.logical_and(r == HEADS * q + h, c == first + q) for q in range(3)])
    return jnp.where(hit, 1.0, 0.0).astype(bf16)


def _ones_from(first, rows):
    lane = lax.broadcasted_iota(jnp.int32, (rows, LANES), 1)
    return jnp.where(jnp.logical_and(lane >= first, lane < first + 3), 1.0, 0.0)


def _own_lanes(h, rows):
    lane = lax.broadcasted_iota(jnp.int32, (rows, LANES), 1)
    return (lane < HEAD_D) if h % 2 == 0 else (lane >= HEAD_D)


def _head_values(x):
    lane = lax.broadcasted_iota(jnp.int32, x.shape, 1)
    return jnp.where(lane < HEADS, x, 0.0)


def _attn_prep_fwd(qkv, cum, *, tm, name):
    t = qkv.shape[0]

    def body(q_ref, k_ref, v_ref, c_ref, qa_ref, ka_ref, va_ref):
        c3 = _three_pieces(_head_values(c_ref[...]))
        ones = jnp.ones((tm, LANES), bf16)
        for h in range(HEADS):
            pair = slice(LANES * (h // 2), LANES * (h // 2 + 1))
            hs = slice(LANES * h, LANES * (h + 1))
            base, own = _extra_lane(h), _own_lanes(h, tm)
            eq = jnp.dot(c3, _move(h, base), preferred_element_type=f32) + _ones_from(base + 3, tm)
            ek = _ones_from(base, tm) - jnp.dot(c3, _move(h, base + 3), preferred_element_type=f32)
            qa_ref[:, hs] = jnp.where(own, q_ref[:, pair] * ATT_SCALE, eq.astype(bf16))
            ka_ref[:, hs] = jnp.where(own, k_ref[:, pair], ek.astype(bf16))
            va_ref[:, hs] = jnp.where(own, v_ref[:, pair], ones)

    wide = pl.BlockSpec((tm, ATT_W), lambda i: (i, 0))
    out = jax.ShapeDtypeStruct((t, ATT_W), bf16)
    return pl.pallas_call(
        body, name=name, grid=(t // tm,),
        in_specs=[pl.BlockSpec((tm, FOX_W), lambda i: (i, 0)), pl.BlockSpec((tm, FOX_W), lambda i: (i, 1)),
                  pl.BlockSpec((tm, FOX_W), lambda i: (i, 2)), pl.BlockSpec((tm, LANES), lambda i: (i, 0))],
        out_specs=[wide] * 3, out_shape=[out] * 3, compiler_params=_params(("arbitrary",)),
    )(qkv, qkv, qkv, cum)


def _attn_prep_bwd(qkv, cum, lse, dmix, o, *, tm, name):
    t = qkv.shape[0]

    def body(q_ref, c_ref, l_ref, do_ref, o_ref, qa_ref, da_ref):
        b3 = _three_pieces(_head_values(c_ref[...] - l_ref[...]))
        r = lax.broadcasted_iota(jnp.int32, (FOX_W, LANES), 0)
        c = lax.broadcasted_iota(jnp.int32, (FOX_W, LANES), 1)
        per_head = jnp.where(r // HEAD_D == c, 1.0, 0.0).astype(bf16)
        delta = sum(jnp.dot(p, per_head, preferred_element_type=f32) for p in _split3(do_ref[...] * o_ref[...]))
        d3 = _three_pieces(delta)
        for h in range(HEADS):
            pair = slice(LANES * (h // 2), LANES * (h // 2 + 1))
            hs = slice(LANES * h, LANES * (h + 1))
            base, own = _extra_lane(h), _own_lanes(h, tm)
            eq = jnp.dot(b3, _move(h, base), preferred_element_type=f32) + _ones_from(base + 3, tm)
            ed = -jnp.dot(d3, _move(h, base), preferred_element_type=f32)
            qa_ref[:, hs] = jnp.where(own, q_ref[:, pair] * ATT_SCALE, eq.astype(bf16))
            da_ref[:, hs] = jnp.where(own, do_ref[:, pair].astype(bf16), ed.astype(bf16))

    wide = pl.BlockSpec((tm, ATT_W), lambda i: (i, 0))
    half = pl.BlockSpec((tm, FOX_W), lambda i: (i, 0))
    col = pl.BlockSpec((tm, LANES), lambda i: (i, 0))
    out = jax.ShapeDtypeStruct((t, ATT_W), bf16)
    return pl.pallas_call(
        body, name=name, grid=(t // tm,), in_specs=[half, col, col, half, half],
        out_specs=[wide] * 2, out_shape=[out] * 2, compiler_params=_params(("arbitrary",)),
    )(qkv, cum, lse, dmix, o)


def _attn_fwd2(q_aug, k_aug, v_aug, *, name, host=None):
    t = q_aug.shape[0]
    n = t // ATT_TILE
    tq = ATT_TILE

    def body(q_ref, k_ref, v_ref, o_ref, lse_ref, acc, m_s):
        i = pl.program_id(0)
        j = pl.program_id(1)

        @pl.when(j == 0)
        def _():
            acc[...] = jnp.zeros_like(acc)
            m_s[...] = jnp.full_like(m_s, NEG_BIG)

        def block(masked):
            mask = _causal(i, j, False) if masked else None
            for h in range(HEADS):
                hs = slice(LANES * h, LANES * (h + 1))
                s = lax.dot_general(q_ref[:, hs], k_ref[:, hs], _NT, preferred_element_type=f32)
                if masked:
                    s = jnp.where(mask, s, NEG_BIG)
                blocks = [s[:, LANES * b:LANES * (b + 1)] for b in range(tq // LANES)]
                m_old = m_s[h]
                m_new = jnp.maximum(m_old, jnp.broadcast_to(
                    jnp.max(functools.reduce(jnp.maximum, blocks), axis=-1, keepdims=True), (tq, LANES)))
                p = jnp.concatenate([jnp.exp(b - m_new) for b in blocks], axis=1).astype(bf16)
                acc[h] = jnp.exp(m_old - m_new) * acc[h] + jnp.dot(p, v_ref[:, hs], preferred_element_type=f32)
                m_s[h] = m_new

        @pl.when(j < i)
        def _():
            block(False)

        @pl.when(j == i)
        def _():
            block(True)
            lse_ref[...] = jnp.zeros_like(lse_ref)
            for h in range(HEADS):
                a = acc[h]
                l = a[:, _extra_lane(h):_extra_lane(h) + 1]
                o_ref[:, HEAD_D * h:HEAD_D * (h + 1)] = a[:, _data_lane(h):_data_lane(h) + HEAD_D] / l
                lse_ref[:, h:h + 1] = m_s[h][:, 0:1] + jnp.log(l)

    kv = pl.BlockSpec((tq, ATT_W), lambda i, j: (jnp.minimum(i, j), 0))
    return _hosted_call(
        host, body, name=name, grid=(n, n),
        in_specs=[pl.BlockSpec((tq, ATT_W), lambda i, j: (i, 0)), kv, kv],
        out_specs=[pl.BlockSpec((tq, FOX_W), lambda i, j: (i, 0)), pl.BlockSpec((tq, LANES), lambda i, j: (i, 0))],
        out_shape=[jax.ShapeDtypeStruct((t, FOX_W), f32), jax.ShapeDtypeStruct((t, LANES), f32)],
        scratch_shapes=[pltpu.VMEM((HEADS, tq, LANES), f32), pltpu.VMEM((HEADS, tq, LANES), f32)],
        compiler_params=_params(("arbitrary", "arbitrary")),
    )(q_aug, k_aug, v_aug)


def _attn_bwd(qb_aug, k_aug, v_aug, do_aug, *, name, host=None):
    t = qb_aug.shape[0]
    n = t // ATT_TILE
    tk = ATT_TILE

    def body(q_ref, k_ref, v_ref, do_ref, dq_ref, dcq_ref, dk_ref, dv_ref, dck_ref, dk_acc, dv_acc, dq_all):
        j = pl.program_id(0)
        i = pl.program_id(1)

        @pl.when(jnp.logical_and(i == 0, j == 0))
        def _():
            dq_all[...] = jnp.zeros_like(dq_all)

        @pl.when(i == 0)
        def _():
            dk_acc[...] = jnp.zeros_like(dk_acc)
            dv_acc[...] = jnp.zeros_like(dv_acc)

        def block(masked):
            mask = _causal(i, j, True) if masked else None
            for h in range(HEADS):
                hs = slice(LANES * h, LANES * (h + 1))
                qh = q_ref[:, hs]
                doh = do_ref[:, hs]
                kh = k_ref[:, hs]
                s_t = lax.dot_general(kh, qh, _NT, preferred_element_type=f32)
                if masked:
                    s_t = jnp.where(mask, s_t, NEG_BIG)
                p_t = jnp.exp(s_t)
                dv_acc[h] += jnp.dot(p_t.astype(bf16), doh, preferred_element_type=f32)
                dp_t = lax.dot_general(v_ref[:, hs], doh, _NT, preferred_element_type=f32)
                ds_t = (p_t * dp_t).astype(bf16)
                dk_acc[h] += jnp.dot(ds_t, qh, preferred_element_type=f32)
                dq_all[i, h] += lax.dot_general(ds_t, kh, _TN, preferred_element_type=f32)

        @pl.when(i > j)
        def _():
            block(False)

        @pl.when(i == j)
        def _():
            block(True)
            dcq_ref[...] = jnp.zeros_like(dcq_ref)
            for h in range(HEADS):
                a = dq_all[j, h]
                dq_ref[:, HEAD_D * h:HEAD_D * (h + 1)] = (
                    a[:, _data_lane(h):_data_lane(h) + HEAD_D] * ATT_SCALE).astype(bf16)
                dcq_ref[:, h:h + 1] = a[:, _extra_lane(h):_extra_lane(h) + 1]

        @pl.when(i == n - 1)
        def _():
            dck_ref[...] = jnp.zeros_like(dck_ref)
            for h in range(HEADS):
                a = dk_acc[h]
                cols = slice(_data_lane(h), _data_lane(h) + HEAD_D)
                dk_ref[:, HEAD_D * h:HEAD_D * (h + 1)] = a[:, cols].astype(bf16)
                dv_ref[:, HEAD_D * h:HEAD_D * (h + 1)] = dv_acc[h][:, cols].astype(bf16)
                dck_ref[:, h:h + 1] = -a[:, _extra_lane(h) + 3:_extra_lane(h) + 4]

    own = pl.BlockSpec((tk, ATT_W), lambda j, i: (j, 0))
    qs = pl.BlockSpec((tk, ATT_W), lambda j, i: (jnp.maximum(i, j), 0))
    half = pl.BlockSpec((tk, FOX_W), lambda j, i: (j, 0))
    col = pl.BlockSpec((tk, LANES), lambda j, i: (j, 0))
    return _hosted_call(
        host, body, name=name, grid=(n, n), in_specs=[qs, own, own, qs],
        out_specs=[half, col, half, half, col],
        out_shape=[jax.ShapeDtypeStruct((t, FOX_W), bf16), jax.ShapeDtypeStruct((t, LANES), f32),
                   jax.ShapeDtypeStruct((t, FOX_W), bf16), jax.ShapeDtypeStruct((t, FOX_W), bf16),
                   jax.ShapeDtypeStruct((t, LANES), f32)],
        scratch_shapes=[pltpu.VMEM((HEADS, tk, LANES), f32), pltpu.VMEM((HEADS, tk, LANES), f32),
                        pltpu.VMEM((n, HEADS, tk, LANES), f32)],
        compiler_params=_params(("arbitrary", "arbitrary")),
    )(qb_aug, k_aug, v_aug, do_aug)


LRU_CHUNK = 64
LRU_G = 256
SUB = 8


def _row_ids(n):
    return lax.broadcasted_iota(jnp.int32, (n, LRU_G), 0)


def _shift_rows_down(ext, s):
    return pltpu.roll(ext, s, axis=0)[SUB:, :]


def _shift_rows_up(ext, s, n):
    return pltpu.roll(ext, ext.shape[0] - s, axis=0)[:n, :]


def _lru_gates(u, wa_ref, ba_ref, wx_ref, bx_ref, sp):
    ub = u.astype(bf16)
    r = _sigmoid(jnp.dot(ub, wa_ref[...], preferred_element_type=f32) + ba_ref[...])
    gi = _sigmoid(jnp.dot(ub, wx_ref[...], preferred_element_type=f32) + bx_ref[...])
    log_a = -LRU_C * r * sp
    a = jnp.exp(log_a)
    s = jnp.sqrt(_one_minus_exp(2.0 * log_a))
    return r, gi, a, s


def _conv_window(lx_ref, r0, ci):
    cur = lx_ref[pl.ds(r0, LRU_CHUNK), :]
    p0 = pl.multiple_of(jnp.maximum(r0 - SUB, 0), SUB)
    prev = jnp.where(ci > 0, lx_ref[pl.ds(p0, SUB), :], 0.0)
    return cur, jnp.concatenate([prev, cur], axis=0)


def _lru_fwd(zl, conv_w, conv_b, wa, ba, wx, bx, lam, *, name, host=None):
    t = zl.shape[0]
    n_chunk = t // LRU_CHUNK

    def body(lx_ref, lg_ref, cw_ref, cb_ref, wa_ref, ba_ref, wx_ref, bx_ref, lam_ref, u_ref, h_ref, y_ref):
        sp = _softplus(-lam_ref[...])
        rows = _row_ids(SUB)

        def chunk(ci, hc):
            r0 = pl.multiple_of(ci * LRU_CHUNK, LRU_CHUNK)
            cur, ext = _conv_window(lx_ref, r0, ci)
            u = cb_ref[...] + cw_ref[3:4, :] * cur
            for k in range(3):
                u = u + cw_ref[k:k + 1, :] * _shift_rows_down(ext, 3 - k)
            r, gi, a, s = _lru_gates(u, wa_ref, ba_ref, wx_ref, bx_ref, sp)
            b = s * (gi * u)
            tiles = []
            for q in range(LRU_CHUNK // SUB):
                ta = a[SUB * q:SUB * (q + 1), :]
                tb = b[SUB * q:SUB * (q + 1), :]
                for d in (1, 2, 4):
                    a_sh = jnp.where(rows >= d, pltpu.roll(ta, d, axis=0), 1.0)
                    b_sh = jnp.where(rows >= d, pltpu.roll(tb, d, axis=0), 0.0)
                    tb = ta * b_sh + tb
                    ta = ta * a_sh
                hq = tb + ta * hc
                hc = hq[SUB - 1:SUB, :]
                tiles.append(hq)
            h = jnp.concatenate(tiles, axis=0)
            u_ref[pl.ds(r0, LRU_CHUNK), :] = u
            h_ref[pl.ds(r0, LRU_CHUNK), :] = h
            gel, _ = _gelu_and_grad(lg_ref[pl.ds(r0, LRU_CHUNK), :])
            y_ref[pl.ds(r0, LRU_CHUNK), :] = gel * h
            return hc

        lax.fori_loop(0, n_chunk, chunk, jnp.zeros((1, LRU_G), f32))

    seq = lambda cb: pl.BlockSpec((t, LRU_G), lambda c, cb=cb: (0, c + cb))
    rowc = pl.BlockSpec((1, LRU_G), lambda c: (0, c))
    diag = pl.BlockSpec((LRU_G, LRU_G), lambda c: (c, c))
    out = jax.ShapeDtypeStruct((t, LRU_W), f32)
    return _hosted_call(
        host, body, name=name, grid=(LRU_W // LRU_G,),
        in_specs=[seq(0), seq(LRU_W // LRU_G), pl.BlockSpec((4, LRU_G), lambda c: (0, c)),
                  rowc, diag, rowc, diag, rowc, rowc],
        out_specs=[seq(0)] * 3, out_shape=[out] * 3,
        compiler_params=_params(("arbitrary",)),
    )(zl, zl, conv_w, conv_b, wa, ba, wx, bx, lam)


def _lru_bwd(dmix, zl, u_all, h_all, conv_w, wa, ba, wx, bx, lam, *, name, host=None):
    t = zl.shape[0]
    n_chunk = t // LRU_CHUNK

    def body(dy_ref, lx_ref, lg_ref, u_ref, h_ref, cw_ref, wa_ref, ba_ref, wx_ref, bx_ref, lam_ref,
             dlx_ref, dlg_ref, dcw_ref, dcb_ref, dba_ref, dbx_ref, dlam_ref, dwa_ref, dwx_ref, dpr_s, dpx_s):
        lam_v = lam_ref[...]
        sp = _softplus(-lam_v)
        rows = _row_ids(SUB)
        rows_c = _row_ids(LRU_CHUNK)
        zero_row = jnp.zeros((1, LRU_G), f32)

        def chunk(step, carry):
            dh_c, a_next0, du_next, dsp, dba, dbx, dcb, dw0, dw1, dw2, dw3 = carry
            ci = n_chunk - 1 - step
            r0 = pl.multiple_of(ci * LRU_CHUNK, LRU_CHUNK)
            sl = pl.ds(r0, LRU_CHUNK)
            u = u_ref[sl, :]
            r, gi, a, s = _lru_gates(u, wa_ref, ba_ref, wx_ref, bx_ref, sp)
            h = h_ref[sl, :]
            p0 = pl.multiple_of(jnp.maximum(r0 - SUB, 0), SUB)
            h_before = jnp.where(ci > 0, h_ref[pl.ds(p0, SUB), :], 0.0)[SUB - 1:SUB, :]
            h_prev = jnp.where(rows_c == 0, h_before, pltpu.roll(h, 1, axis=0))
            gel, dgel = _gelu_and_grad(lg_ref[sl, :])
            dy = dy_ref[sl, :]
            dlg_ref[sl, :] = (dy * h * dgel).astype(bf16)
            g_in = dy * gel
            a_next = jnp.where(rows_c == LRU_CHUNK - 1, a_next0, pltpu.roll(a, LRU_CHUNK - 1, axis=0))
            tiles = [None] * (LRU_CHUNK // SUB)
            for q in reversed(range(LRU_CHUNK // SUB)):
                ta = a_next[SUB * q:SUB * (q + 1), :]
                tb = g_in[SUB * q:SUB * (q + 1), :]
                for d in (1, 2, 4):
                    a_sh = jnp.where(rows < SUB - d, pltpu.roll(ta, SUB - d, axis=0), 1.0)
                    b_sh = jnp.where(rows < SUB - d, pltpu.roll(tb, SUB - d, axis=0), 0.0)
                    tb = ta * b_sh + tb
                    ta = ta * a_sh
                dhq = tb + ta * dh_c
                dh_c = dhq[0:1, :]
                tiles[q] = dhq
            dh = jnp.concatenate(tiles, axis=0)
            da = dh * h_prev
            ds = dh * gi * u
            dgi = dh * s * u
            du = dh * s * gi
            dlog_a = da * a - ds * (a * a) / s
            dr = dlog_a * (-LRU_C * sp)
            dsp = dsp + jnp.sum(dlog_a * (-LRU_C * r), axis=0, keepdims=True)
            dpr = dr * r * (1.0 - r)
            dpx = dgi * gi * (1.0 - gi)
            dprb = dpr.astype(bf16)
            dpxb = dpx.astype(bf16)
            dpr_s[sl, :] = dprb
            dpx_s[sl, :] = dpxb
            du = du + (lax.dot_general(dprb, wa_ref[...], _NT, preferred_element_type=f32)
                       + lax.dot_general(dpxb, wx_ref[...], _NT, preferred_element_type=f32))
            dba = dba + jnp.sum(dpr, axis=0, keepdims=True)
            dbx = dbx + jnp.sum(dpx, axis=0, keepdims=True)
            dcb = dcb + jnp.sum(du, axis=0, keepdims=True)
            du_ext = jnp.concatenate([du, du_next], axis=0)
            dlx = cw_ref[3:4, :] * du
            for k in range(3):
                dlx = dlx + cw_ref[k:k + 1, :] * _shift_rows_up(du_ext, 3 - k, LRU_CHUNK)
            dlx_ref[sl, :] = dlx.astype(bf16)
            cur, ext = _conv_window(lx_ref, r0, ci)
            dws = [dw0, dw1, dw2, dw3 + jnp.sum(du * cur, axis=0, keepdims=True)]
            for k in range(3):
                dws[k] = dws[k] + jnp.sum(du * _shift_rows_down(ext, 3 - k), axis=0, keepdims=True)
            return (dh_c, a[0:1, :], du[0:SUB, :], dsp, dba, dbx, dcb, dws[0], dws[1], dws[2], dws[3])

        init = (zero_row, zero_row, jnp.zeros((SUB, LRU_G), f32)) + (zero_row,) * 8
        out = lax.fori_loop(0, n_chunk, chunk, init)
        _, _, _, dsp, dba, dbx, dcb, dw0, dw1, dw2, dw3 = out
        dlam_ref[...] = dsp * (-_sigmoid(-lam_v))
        dba_ref[...] = dba
        dbx_ref[...] = dbx
        dcb_ref[...] = dcb
        dcw_ref[...] = jnp.concatenate([dw0, dw1, dw2, dw3], axis=0)
        ub = u_ref[...].astype(bf16)
        dwa_ref[...] = lax.dot_general(ub, dpr_s[...], _TN, preferred_element_type=f32)
        dwx_ref[...] = lax.dot_general(ub, dpx_s[...], _TN, preferred_element_type=f32)

    seq = lambda cb: pl.BlockSpec((t, LRU_G), lambda c, cb=cb: (0, c + cb))
    rowc = pl.BlockSpec((1, LRU_G), lambda c: (0, c))
    diag = pl.BlockSpec((LRU_G, LRU_G), lambda c: (c, c))
    gate_out = pl.BlockSpec((None, LRU_G, LRU_G), lambda c: (c, 0, 0))
    row_shape = jax.ShapeDtypeStruct((1, LRU_W), f32)
    return _hosted_call(
        host, body, name=name, grid=(LRU_W // LRU_G,),
        in_specs=[seq(LRU_W // LRU_G), seq(0), seq(LRU_W // LRU_G), seq(0), seq(0),
                  pl.BlockSpec((4, LRU_G), lambda c: (0, c)),
                  diag, rowc, diag, rowc, rowc],
        out_specs=[seq(0), seq(0), pl.BlockSpec((4, LRU_G), lambda c: (0, c)), rowc, rowc, rowc, rowc,
                   gate_out, gate_out],
        out_shape=[jax.ShapeDtypeStruct((t, LRU_W), bf16)] * 2
        + [jax.ShapeDtypeStruct((4, LRU_W), f32)] + [row_shape] * 4
        + [jax.ShapeDtypeStruct((LRU_W // LRU_G, LRU_G, LRU_G), f32)] * 2,
        scratch_shapes=[pltpu.VMEM((t, LRU_G), bf16), pltpu.VMEM((t, LRU_G), bf16)],
        compiler_params=_params(("arbitrary",)),
    )(dmix, zl, zl, u_all, h_all, conv_w, wa, ba, wx, bx, lam)


def _pack_rows(a):
    flat = a.reshape(-1)
    rows = -(-flat.shape[0] // LANES)
    return jnp.pad(flat, (0, rows * LANES - flat.shape[0])).reshape(rows, LANES)


W_IN_PAD = 21 * LANES


def _w_in_join(blocks, *, name):
    tm = 256

    def body(b_ref, o_ref):
        o_ref[:, IN_COLS:] = jnp.zeros((tm, W_IN_PAD - IN_COLS), bf16)
        for q in range(N_DEV):
            o_ref[:, IN_SHARD * q:IN_SHARD * (q + 1)] = b_ref[q]

    return pl.pallas_call(
        body, name=name, grid=(D_MODEL // tm,),
        in_specs=[pl.BlockSpec((N_DEV, tm, IN_SHARD), lambda i: (0, i, 0))],
        out_specs=pl.BlockSpec((tm, W_IN_PAD), lambda i: (i, 0)),
        out_shape=jax.ShapeDtypeStruct((D_MODEL, W_IN_PAD), bf16), compiler_params=_params(("arbitrary",)),
    )(blocks)


def _w_in_split(main, fg, *, name):
    tm = 256
    n_main = main.shape[0]

    def body(m_ref, f_ref, o_ref):
        full = jnp.concatenate([m_ref[n] for n in range(n_main)] + [f_ref[...]], axis=1)
        for q in range(N_DEV):
            o_ref[q] = full[:, IN_SHARD * q:IN_SHARD * (q + 1)]

    return pl.pallas_call(
        body, name=name, grid=(D_MODEL // tm,),
        in_specs=[pl.BlockSpec((n_main, tm, 512), lambda i: (0, i, 0)), pl.BlockSpec((tm, LANES), lambda i: (i, 0))],
        out_specs=pl.BlockSpec((N_DEV, tm, IN_SHARD), lambda i: (0, i, 0)),
        out_shape=jax.ShapeDtypeStruct((N_DEV, D_MODEL, IN_SHARD), bf16), compiler_params=_params(("arbitrary",)),
    )(main, fg)


def _block_diag(w):
    eye = jnp.eye(HEADS, dtype=w.dtype)
    return jnp.einsum("hij,hk->hikj", w, eye).reshape(LRU_W, LRU_W)


def _diag_blocks(dw):
    per = dw.shape[1] // HEAD_D
    blocks = [dw[:, HEAD_D * b:HEAD_D * (b + 1), HEAD_D * b:HEAD_D * (b + 1)] for b in range(per)]
    return jnp.stack(blocks, axis=1).reshape(HEADS, HEAD_D, HEAD_D)


def _local_step(x, target, sent, small, *, tm=512, tm_ffn=1024):
    ln1 = (small["ln1_g"], small["ln1_b"])
    ln2 = (small["ln2_g"], small["ln2_b"])
    ln3 = (small["ln3_g"], small["ln3_b"])

    xh1, rs1, hg1, hu1, wg1, wu1, wd1, w_in_g, w_out_g, conv_w_g, wu2 = _ffn1_fwd_gathering(
        x, (sent["ffn1_w_gate"], sent["ffn1_w_up"], sent["ffn1_w_down"]),
        _Hosts(_Exchange([sent["w_in"], sent["w_out"], sent["conv_w"]], gather=True),
               _Exchange([sent["ffn2_w_up"]], gather=True), stages=(N_DEV // 2, N_DEV - 2)),
        tm=tm_ffn, name="ffn1_fwd")
    w_in = _w_in_join(w_in_g, name="w_in_join")
    w_out = w_out_g.reshape(D_MODEL, D_MODEL)
    conv_w = conv_w_g.transpose(1, 0, 2).reshape(4, LRU_W)
    qkv, zl, zfg = _in_proj(xh1, ln1[0], ln1[1], w_in, tm=tm, name="in_proj")
    bfg = jnp.pad(small["b_forget"], ((0, 0), (0, LANES - HEADS)))
    cum = _cum_fwd(zfg, bfg, name="cum_fwd")
    q_aug, k_aug, v_aug = _attn_prep_fwd(qkv, cum, tm=tm, name="attn_prep_fwd")
    o, lse, wg2 = _attn_fwd2(q_aug, k_aug, v_aug, name="attn_fwd",
                             host=_Exchange([sent["ffn2_w_gate"]], gather=True))
    wa_bd = _block_diag(small["rg_wa"]).astype(bf16)
    wx_bd = _block_diag(small["rg_wx"]).astype(bf16)
    ba = small["rg_ba"].reshape(1, LRU_W)
    bx = small["rg_bx"].reshape(1, LRU_W)
    u, h, lru, wd2 = _lru_fwd(zl, conv_w, small["conv_b"], wa_bd, ba, wx_bd, bx, small["lru_lambda"],
                              name="lru_fwd", host=_Exchange([sent["ffn2_w_down"]], gather=True))
    xh2, rs2 = _mmln([(o, 0, FOX_W, w_out, 0, D_MODEL, "nn"), (lru, 0, LRU_W, w_out, 1, D_MODEL, "nn")],
                     tm=tm, name="mix_fwd", resid=("affine", xh1) + ln1, resid_scale=ALPHA, epi="ln_fwd")
    xh3, rs3, hg2, hu2 = _ffn_fwd(xh2, ln2[0], ln2[1], wg2, wu2, wd2, tm=tm_ffn, name="ffn2_fwd")

    dpre3, sq_rows, g_ln3g, g_ln3b = _loss_bwd(xh3, rs3, ln3[0], ln3[1], target, tm=tm, name="loss_bwd")
    dpre2, g_ln2g, g_ln2b, dhg2, dhu2, a2 = _ffn_bwd(dpre3, hg2, hu2, wg2, wu2, wd2,
                                                     (xh2, rs2, ln2[0]), tm=tm_ffn, name="ffn2_bwd")
    wgrad = dict(out_dtype=bf16, tm=D_MODEL, mb=1, tn=FF_TILE, nb=4, tk=512, pair=True)
    wdgrad = dict(out_dtype=bf16, tm=512, mb=4, tn=D_MODEL, nb=1, tk=512, out_scale=0.5, pair=True)
    between_chips = functools.partial(_Exchange, gather=False, chips=True)
    g_wg2 = _mm_tn(xh2, dhg2, name="g_wg2", affine=ln2, **wgrad)
    g_wu2 = _mm_tn(xh2, dhu2, name="g_wu2", affine=ln2, **wgrad)
    g_wd2 = _mm_tn(a2, dpre3, name="g_wd2", **wdgrad)

    dmix = _mmln([(dpre2, 0, D_MODEL, w_out, 0, D_MODEL, "nt")], tm=tm, name="dmix_bwd")
    g_wout_a = _mm(o, dpre2, mode="tn", out_dtype=bf16, tm=512, tn=D_MODEL, tk=512, name="g_wout_fox")
    g_wout_b = _mm(lru, dpre2, mode="tn", out_dtype=bf16, tm=512, tn=D_MODEL, tk=512, name="g_wout_lru")
    g_wout_blocked = jnp.concatenate([g_wout_a, g_wout_b], axis=0).reshape(N_DEV, D_MODEL // N_DEV, D_MODEL)
    dlx, dlg, g_cw, g_cb, g_ba, g_bx, g_lam, g_wa4, g_wx4, p_wg2, p_wout = _lru_bwd(
        dmix, zl, u, h, conv_w, wa_bd, ba, wx_bd, bx, small["lru_lambda"], name="lru_bwd",
        host=_Hosts(between_chips([g_wg2]), _Exchange([g_wout_blocked], gather=False)))
    qb_aug, do_aug = _attn_prep_bwd(qkv, cum, lse, dmix, o, tm=tm, name="attn_prep_bwd")
    dq, dcum_q, dk, dv, dcum_k, p_wu2, p_wd2 = _attn_bwd(qb_aug, k_aug, v_aug, do_aug, name="attn_bwd",
                                                         host=between_chips([g_wu2, g_wd2]))
    dfg, g_bf = _cum_bwd(dcum_q, dcum_k, zfg, bfg, name="cum_bwd")

    dz = [(dq, 0, 512), (dk, 1, 512), (dv, 2, 512), (dlx, 3, 512), (dlg, 4, 512), (dfg, 20, LANES)]
    dpre1, g_ln1g, g_ln1b = _mmln(
        [(arr, 0, w, w_in, cb, w, "nt") for (arr, cb, w) in dz],
        tm=tm, name="dx1_bwd", resid=("plain", dpre2), resid_scale=ALPHA, epi="ln_bwd", ln=(xh1, rs1, ln1[0]))
    g_win_main = _mm_tn(xh1, [arr for arr, _, _ in dz[:5]], out_dtype=bf16, tm=D_MODEL, mb=1, tn=512, nb=5, tk=512,
                        name="g_win", affine=ln1, out_blocked=True)
    g_win_fg = _mm(xh1, dfg, mode="tn", out_dtype=bf16, tm=D_MODEL, tn=LANES, tk=512, name="g_win_fg", affine=ln1)
    g_win_blocked = _w_in_split(g_win_main, g_win_fg, name="w_in_split")
    dhg1, dhu1, a1, p_win = _ffn_bwd_act(dpre1, hg1, hu1, wd1, tm=tm_ffn, name="ffn1_bwd_act",
                                         host=_Exchange([g_win_blocked], gather=False))
    small_g = {
        "ln1_g": g_ln1g, "ln1_b": g_ln1b, "b_forget": g_bf[:, :HEADS], "conv_w": g_cw, "conv_b": g_cb,
        "rg_wa": _diag_blocks(g_wa4), "rg_ba": g_ba.reshape(HEADS, HEAD_D),
        "rg_wx": _diag_blocks(g_wx4), "rg_bx": g_bx.reshape(HEADS, HEAD_D), "lru_lambda": g_lam,
        "ln2_g": g_ln2g, "ln2_b": g_ln2b, "ln3_g": g_ln3g, "ln3_b": g_ln3b,
    }
    small_g["loss"] = (0.5 / D_MODEL) * jnp.sum(sq_rows, keepdims=True)
    pieces = [_pack_rows(small_g[n]) for n in PACKED]
    packed = jnp.concatenate(pieces + [jnp.zeros((PACK_ROWS - sum(p.shape[0] for p in pieces), LANES), f32)])
    g_wg1, all_packed = _mm_tn(x, dhg1, name="g_wg1", host=_Exchange([packed], gather=True), **wgrad)
    g_wu1, p_wg1 = _mm_tn(x, dhu1, name="g_wu1", host=between_chips([g_wg1]), **wgrad)
    g_wd1, p_wu1 = _mm_tn(a1, dpre1, name="g_wd1", host=between_chips([g_wu1]), **wdgrad)
    grad_x, p_wd1 = _ffn_bwd_dx(dpre1, dhg1, dhu1, wg1, wu1, tm=tm_ffn, name="ffn1_bwd_dx",
                                host=between_chips([g_wd1]))
    parts = {
        "ffn1_w_gate": p_wg1, "ffn1_w_up": p_wu1, "ffn1_w_down": p_wd1, "w_in": p_win, "w_out": p_wout,
        "ffn2_w_gate": p_wg2, "ffn2_w_up": p_wu2, "ffn2_w_down": p_wd2,
    }
    return grad_x, parts, all_packed, {n: small_g[n].shape for n in PACKED}


def _adam_math(w, g, m, v):
    m2 = ADAM_B1 * m + (1.0 - ADAM_B1) * g
    v2 = ADAM_B2 * v + (1.0 - ADAM_B2) * (g * g)
    m_hat = m2 / (1.0 - ADAM_B1 ** ADAM_STEP)
    v_hat = v2 / (1.0 - ADAM_B2 ** ADAM_STEP)
    delta = -ADAM_LR * (m_hat / (jnp.sqrt(v_hat) + ADAM_EPS) + ADAM_WD * w)
    return delta, m2, v2


ADAM_TILE_ELEMS = 128 * 1024


def _adamw_big(items, *, name):
    _, r, c = items[0][1].shape
    n_parts = items[0][0].shape[0]
    n_items = len(items)
    assert all(it[1].shape == (1, r, c) and it[0].shape == (n_parts, r, c) for it in items), name
    tr = max(d for d in range(8, r + 1, 8) if r % d == 0 and d * c <= ADAM_TILE_ELEMS)

    def body(*refs):
        ins, outs = refs[:4 * n_items], refs[4 * n_items:]
        for k in range(n_items):
            p_ref, w_ref, m_ref, v_ref = ins[4 * k:4 * k + 4]
            g = p_ref[0].astype(f32)
            for q in range(1, n_parts):
                g = g + p_ref[q].astype(f32)
            d, m2, v2 = _adam_math(w_ref[...], g, m_ref[...], v_ref[...])
            for o_ref, val in zip(outs[4 * k:4 * k + 4], (g, d, m2, v2)):
                o_ref[...] = val

    blk = pl.BlockSpec((None, tr, c), lambda i: (0, i, 0))
    res = pl.pallas_call(
        body, name=name, grid=(r // tr,),
        in_specs=[pl.BlockSpec((n_parts, tr, c), lambda i: (0, i, 0)), blk, blk, blk] * n_items,
        out_specs=[blk] * (4 * n_items), out_shape=[jax.ShapeDtypeStruct((1, r, c), f32)] * (4 * n_items),
        compiler_params=_params(("arbitrary",)),
    )(*[a for it in items for a in it])
    return [res[4 * k:4 * k + 4] for k in range(n_items)]


def _adamw_small(items, *, name):
    n = len(items)

    def body(*refs):
        ins, outs = refs[:4 * n], refs[4 * n:]
        for k in range(n):
            g, w, m, v = (ins[4 * k + q][...] for q in range(4))
            d, m2, v2 = _adam_math(w, g, m, v)
            outs[3 * k][...] = d
            outs[3 * k + 1][...] = m2
            outs[3 * k + 2][...] = v2

    vm = pl.BlockSpec(memory_space=pltpu.VMEM)
    flat = [a for item in items for a in item]
    out_shape = [jax.ShapeDtypeStruct(item[1].shape, f32) for item in items for _ in range(3)]
    return pl.pallas_call(
        body, name=name, in_specs=[vm] * (4 * n), out_specs=[vm] * (3 * n), out_shape=out_shape,
    )(*flat)


def _sum_parts(parts, *, name):
    def body(p_ref, o_ref):
        acc = p_ref[0]
        for q in range(1, N_DEV):
            acc = acc + p_ref[q]
        o_ref[...] = acc

    vm = pl.BlockSpec(memory_space=pltpu.VMEM)
    return pl.pallas_call(
        body, name=name, in_specs=[vm], out_specs=vm, out_shape=jax.ShapeDtypeStruct(parts.shape[1:], f32),
    )(parts)


WEIGHTS = ["ffn1_w_gate", "ffn1_w_up", "ffn1_w_down", "ln1_g", "ln1_b", "w_in", "b_forget", "conv_w", "conv_b",
           "rg_wa", "rg_ba", "rg_wx", "rg_bx", "lru_lambda", "w_out", "ln2_g", "ln2_b",
           "ffn2_w_gate", "ffn2_w_up", "ffn2_w_down", "ln3_g", "ln3_b"]
BIG = ["ffn1_w_gate", "ffn1_w_up", "ffn1_w_down", "w_in", "w_out", "ffn2_w_gate", "ffn2_w_up", "ffn2_w_down"]
PACKED = ["ln1_g", "ln1_b", "ln2_g", "ln2_b", "ln3_g", "ln3_b", "conv_b", "rg_ba", "rg_bx", "lru_lambda",
          "conv_w", "rg_wa", "rg_wx", "b_forget", "loss"]
PACK_ROWS = 600


def _two_d(a):
    return a.reshape((-1, a.shape[-1]))


def _transport(a):
    return _two_d(a)


def kernel(x, ffn1_w_gate, ffn1_w_up, ffn1_w_down, ln1_g, ln1_b, w_in, b_forget, conv_w, conv_b, rg_wa, rg_ba, rg_wx, rg_bx, lru_lambda, w_out, ln2_g, ln2_b, ffn2_w_gate, ffn2_w_up, ffn2_w_down, ln3_g, ln3_b, loss_target, m_ffn1_w_gate, m_ffn1_w_up, m_ffn1_w_down, m_ln1_g, m_ln1_b, m_w_in, m_b_forget, m_conv_w, m_conv_b, m_rg_wa, m_rg_ba, m_rg_wx, m_rg_bx, m_lru_lambda, m_w_out, m_ln2_g, m_ln2_b, m_ffn2_w_gate, m_ffn2_w_up, m_ffn2_w_down, m_ln3_g, m_ln3_b, v_ffn1_w_gate, v_ffn1_w_up, v_ffn1_w_down, v_ln1_g, v_ln1_b, v_w_in, v_b_forget, v_conv_w, v_conv_b, v_rg_wa, v_rg_ba, v_rg_wx, v_rg_bx, v_lru_lambda, v_w_out, v_ln2_g, v_ln2_b, v_ffn2_w_gate, v_ffn2_w_up, v_ffn2_w_down, v_ln3_g, v_ln3_b):
    w_args = (ffn1_w_gate, ffn1_w_up, ffn1_w_down, ln1_g, ln1_b, w_in, b_forget, conv_w, conv_b, rg_wa, rg_ba, rg_wx, rg_bx, lru_lambda, w_out, ln2_g, ln2_b, ffn2_w_gate, ffn2_w_up, ffn2_w_down, ln3_g, ln3_b)
    m_args = (m_ffn1_w_gate, m_ffn1_w_up, m_ffn1_w_down, m_ln1_g, m_ln1_b, m_w_in, m_b_forget, m_conv_w, m_conv_b, m_rg_wa, m_rg_ba, m_rg_wx, m_rg_bx, m_lru_lambda, m_w_out, m_ln2_g, m_ln2_b, m_ffn2_w_gate, m_ffn2_w_up, m_ffn2_w_down, m_ln3_g, m_ln3_b)
    v_args = (v_ffn1_w_gate, v_ffn1_w_up, v_ffn1_w_down, v_ln1_g, v_ln1_b, v_w_in, v_b_forget, v_conv_w, v_conv_b, v_rg_wa, v_rg_ba, v_rg_wx, v_rg_bx, v_lru_lambda, v_w_out, v_ln2_g, v_ln2_b, v_ffn2_w_gate, v_ffn2_w_up, v_ffn2_w_down, v_ln3_g, v_ln3_b)
    w = dict(zip(WEIGHTS, w_args))
    m = dict(zip(WEIGHTS, m_args))
    v = dict(zip(WEIGHTS, v_args))
    me = 4 * lax.axis_index("x") + 2 * lax.axis_index("y") + lax.axis_index("c")

    sent = {n: _transport(w[n]).astype(bf16) for n in BIG}
    sent["conv_w"] = _two_d(w["conv_w"])
    small = {n: w[n] for n in ("ln1_g", "ln1_b", "ln2_g", "ln2_b", "ln3_g", "ln3_b", "b_forget", "conv_b",
                               "lru_lambda")}
    small.update({n: w[n][0] for n in ("rg_wa", "rg_ba", "rg_wx", "rg_bx")})

    grad_x, parts, all_packed, small_shapes = _local_step(x[0], loss_target[0], sent, small)

    total = _sum_parts(all_packed, name="sum_small_grads")
    grads, off = {}, 0
    for n in PACKED:
        size = math.prod(small_shapes[n])
        rows = -(-size // LANES)
        grads[n] = total[off:off + rows].reshape(-1)[:size].reshape(small_shapes[n])
        off += rows
    loss = grads.pop("loss").reshape(())
    grads["conv_w"] = lax.dynamic_slice_in_dim(grads["conv_w"], me * (LRU_W // N_DEV), LRU_W // N_DEV, axis=1)

    delta, new_m, new_v = {}, {}, {}
    for group in (("ffn1_w_gate", "ffn1_w_up", "ffn2_w_gate", "ffn2_w_up"), ("ffn1_w_down", "ffn2_w_down"),
                  ("w_in",), ("w_out",)):
        done = _adamw_big([(parts[n], w[n], m[n], v[n]) for n in group], name="adamw_" + group[0])
        for n, (g, d, m2, v2) in zip(group, done):
            grads[n], delta[n], new_m[n], new_v[n] = g, d, m2, v2
    small_names = [n for n in WEIGHTS if n not in BIG]
    outs = _adamw_small([(_two_d(grads[n]), _two_d(w[n]), _two_d(m[n]), _two_d(v[n])) for n in small_names],
                        name="adamw_small")
    for k, n in enumerate(small_names):
        delta[n], new_m[n], new_v[n] = outs[3 * k], outs[3 * k + 1], outs[3 * k + 2]

    def shaped(d):
        return [d[n].reshape(w[n].shape) for n in WEIGHTS]

    return (loss, grad_x[None], *shaped(grads), *shaped(delta), *shaped(new_m), *shaped(new_v))
```

```python
import functools
import math

import jax
import jax.numpy as jnp
from jax import lax
from jax.experimental import pallas as pl
from jax.experimental.pallas import tpu as pltpu

f32 = jnp.float32
bf16 = jnp.bfloat16

N_DEV = 8
D_MODEL = 1024
D_FF = 4096
FF_TILE = D_FF // N_DEV
FOX_W = 512
LRU_W = 512
HEADS = 8
HEAD_D = 64
IN_COLS = 2568
IN_SHARD = IN_COLS // N_DEV
LANES = 128
LN_EPS = 1e-5
ALPHA = 2.0 ** 0.25
ATT_SCALE = 1.0 / math.sqrt(HEAD_D)
LRU_C = 8.0
NEG_BIG = -1e30

ADAM_LR = 0.001
ADAM_B1 = 0.9
ADAM_B2 = 0.999
ADAM_EPS = 1e-08
ADAM_WD = 0.01
ADAM_STEP = 10

VMEM_LIMIT = 56 * 1024 * 1024
MESH_T = pl.DeviceIdType.MESH


def _params(sem, **kw):
    return pltpu.CompilerParams(dimension_semantics=sem, vmem_limit_bytes=VMEM_LIMIT, **kw)


def _sigmoid(x):
    return 1.0 / (1.0 + jnp.exp(-x))


def _sigmoid_tanh(x):
    return 0.5 * jnp.tanh(0.5 * x) + 0.5


def _softplus(x):
    return jnp.maximum(x, 0.0) + jnp.log(1.0 + jnp.exp(-jnp.abs(x)))


def _one_minus_exp(x):
    series = -x * (1.0 + x * (0.5 + x * (1.0 / 6 + x * (1.0 / 24 + x * (1.0 / 120 + x * (1.0 / 720))))))
    return jnp.where(x > -0.125, series, 1.0 - jnp.exp(x))


_GELU_C = math.sqrt(2.0 / math.pi)


def _gelu_and_grad(x):
    inner = _GELU_C * (x + 0.044715 * x * x * x)
    t = jnp.tanh(inner)
    g = 0.5 * x * (1.0 + t)
    dg = 0.5 * (1.0 + t) + 0.5 * x * (1.0 - t * t) * _GELU_C * (1.0 + 3 * 0.044715 * x * x)
    return g, dg


def _ln_fwd_tile(pre):
    mu = jnp.mean(pre, axis=-1, keepdims=True)
    xc = pre - mu
    var = jnp.mean(xc * xc, axis=-1, keepdims=True)
    rstd = lax.rsqrt(var + LN_EPS)
    return xc * rstd, rstd


def _ln_bwd_tile(dy, xhat, rstd, g):
    dyg = dy * g
    m1 = jnp.mean(dyg, axis=-1, keepdims=True)
    m2 = jnp.mean(dyg * xhat, axis=-1, keepdims=True)
    dpre = rstd * (dyg - m1 - xhat * m2)
    return dpre, jnp.sum(dy * xhat, axis=0, keepdims=True), jnp.sum(dy, axis=0, keepdims=True)


_NT = (((1,), (1,)), ((), ()))
_TN = (((0,), (0,)), ((), ()))


class _Exchange:
    def __init__(self, arrs, gather, chips=False):
        self.arrs, self.gather, self.n, self.chips = list(arrs), gather, len(arrs), chips

    def out_shape(self):
        return [jax.ShapeDtypeStruct(((N_DEV,) + a.shape) if self.gather else a.shape, a.dtype) for a in self.arrs]

    def scratch(self):
        n_remote = self.n * (N_DEV - 1)
        return [pltpu.SemaphoreType.DMA((n_remote,)), pltpu.SemaphoreType.DMA((n_remote,)),
                pltpu.SemaphoreType.DMA((self.n,))]

    def copies(self, ins, outs, sems):
        send_sems, recv_sems, local_sems = sems
        x, y, c = lax.axis_index("x"), lax.axis_index("y"), lax.axis_index("c")
        me = 2 * x + y if self.chips else 4 * x + 2 * y + c
        out = []
        for k in range(self.n):
            for d in (range(2, N_DEV, 2) if self.chips else range(1, N_DEV)):
                px = 1 - x if d & 4 else x
                py = 1 - y if d & 2 else y
                pc = 1 - c if d & 1 else c
                sem = k * (N_DEV - 1) + d - 1
                out.append(pltpu.make_async_remote_copy(
                    src_ref=ins[k].at[2 * px + py if self.chips else 4 * px + 2 * py + pc], dst_ref=outs[k].at[me],
                    send_sem=send_sems.at[sem], recv_sem=recv_sems.at[sem],
                    device_id=(px, py, pc), device_id_type=MESH_T))
            out.append(pltpu.make_async_copy(ins[k].at[me], outs[k].at[me], local_sems.at[k]))
        return out

    def gather_copies(self, ins, outs, sems):
        send_sems, recv_sems, local_sems = sems
        x, y, c = lax.axis_index("x"), lax.axis_index("y"), lax.axis_index("c")
        sibling = (x, y, 1 - c)
        chips = [(1 - x, y), (x, 1 - y), (1 - x, 1 - y)]
        out = []
        for k in range(self.n):
            def copy(s, block, to, src=None, k=k):
                rows = outs[k].at[4 * block[0] + 2 * block[1] + block[2]]
                sem = k * (N_DEV - 1) + s
                return pltpu.make_async_remote_copy(
                    src_ref=rows if src is None else src, dst_ref=rows, send_sem=send_sems.at[sem],
                    recv_sem=recv_sems.at[sem], device_id=to, device_id_type=MESH_T)

            first = [copy(0, (x, y, c), sibling, src=ins[k])]
            first += [copy(1 + q, (x, y, c), (*chip, c), src=ins[k]) for q, chip in enumerate(chips)]
            passed = [copy(4 + q, (*chip, c), sibling) for q, chip in enumerate(chips)]
            own = pltpu.make_async_copy(ins[k], outs[k].at[4 * x + 2 * y + c], local_sems.at[k])
            out.append((first, passed, own, copy))
        return out, sibling, chips, (x, y, c)

    def start(self, ins, outs, sems):
        if not self.gather:
            for cp in self.copies(ins, outs, sems):
                cp.start()
            return
        per_array, _, _, _ = self.gather_copies(ins, outs, sems)
        for first, _, own, _ in per_array:
            own.start()
            for cp in first:
                cp.start()

    def relay(self, ins, outs, sems):
        per_array, sibling, chips, (x, y, c) = self.gather_copies(ins, outs, sems)
        for first, passed, own, copy in per_array:
            for q, chip in enumerate(chips):
                copy(1 + q, (*chip, c), (x, y, c)).wait_recv()
                passed[q].start()

    def wait(self, ins, outs, sems, relayed=False):
        if not self.gather:
            for cp in self.copies(ins, outs, sems):
                cp.wait()
            return
        if not relayed:
            self.relay(ins, outs, sems)
        per_array, sibling, chips, (x, y, c) = self.gather_copies(ins, outs, sems)
        for first, passed, own, copy in per_array:
            copy(0, sibling, (x, y, c)).wait_recv()
            for q, chip in enumerate(chips):
                copy(4 + q, (*chip, 1 - c), (x, y, c)).wait_recv()
            for cp in first + passed:
                cp.wait_send()
            own.wait()


class _Hosts:
    gather = False

    def __init__(self, *hosts):
        self.hosts = hosts
        self.n = sum(h.n for h in hosts)
        self.arrs = [a for h in hosts for a in h.arrs]

    def out_shape(self):
        return [sh for h in self.hosts for sh in h.out_shape()]

    def scratch(self):
        return [sc for h in self.hosts for sc in h.scratch()]

    def _each(self, ins, outs, sems):
        at = 0
        for k, h in enumerate(self.hosts):
            yield h, ins[at:at + h.n], outs[at:at + h.n], sems[3 * k:3 * k + 3]
            at += h.n

    def start(self, ins, outs, sems):
        for h, h_in, h_out, h_sems in self._each(ins, outs, sems):
            h.start(h_in, h_out, h_sems)

    def wait(self, ins, outs, sems, relayed=False):
        for h, h_in, h_out, h_sems in self._each(ins, outs, sems):
            h.wait(h_in, h_out, h_sems)


def _hosted_call(host, body, *, name, grid, in_specs, out_specs, out_shape, scratch_shapes=(), compiler_params):
    out_specs = list(out_specs) if isinstance(out_specs, (list, tuple)) else [out_specs]
    out_shape = list(out_shape) if isinstance(out_shape, (list, tuple)) else [out_shape]
    if host is None:
        return pl.pallas_call(body, name=name, grid=grid, in_specs=in_specs, out_specs=out_specs,
                              out_shape=out_shape, scratch_shapes=list(scratch_shapes),
                              compiler_params=compiler_params)
    n_in, n_out, n_scr, k = len(in_specs), len(out_shape), len(scratch_shapes), host.n

    def wrapped(*refs):
        ins, h_in = refs[:n_in], refs[n_in:n_in + k]
        outs, h_out = refs[n_in + k:n_in + k + n_out], refs[n_in + k + n_out:n_in + 2 * k + n_out]
        scr, sems = refs[n_in + 2 * k + n_out:n_in + 2 * k + n_out + n_scr], refs[n_in + 2 * k + n_out + n_scr:]
        ids = [pl.program_id(a) for a in range(len(grid))]
        first = functools.reduce(jnp.logical_and, [i == 0 for i in ids])
        last = functools.reduce(jnp.logical_and, [i == g - 1 for i, g in zip(ids, grid)])
        steps = math.prod(grid)
        relay_at = (3 * steps) // 4 if host.gather and steps >= 8 else None

        @pl.when(first)
        def _():
            host.start(h_in, h_out, sems)

        if relay_at is not None:
            coords, rest = [], relay_at
            for g in reversed(grid):
                coords.append(rest % g)
                rest //= g

            @pl.when(functools.reduce(jnp.logical_and, [i == cd for i, cd in zip(ids, reversed(coords))]))
            def _():
                host.relay(h_in, h_out, sems)

        body(*ins, *outs, *scr)

        @pl.when(last)
        def _():
            host.wait(h_in, h_out, sems, relayed=relay_at is not None)

    hbm = pl.BlockSpec(memory_space=pl.ANY)
    call = pl.pallas_call(
        wrapped, name=name, grid=grid, in_specs=list(in_specs) + [hbm] * k, out_specs=out_specs + [hbm] * k,
        out_shape=out_shape + host.out_shape(), scratch_shapes=list(scratch_shapes) + host.scratch(),
        compiler_params=compiler_params)
    return lambda *args: call(*args, *host.arrs)


def _ffn_fwd(xhat, g_in, b_in, wg, wu, wd, *, tm, name, host=None):
    t = xhat.shape[0]
    nj = N_DEV

    def body(x_ref, g_ref, b_ref, wg_ref, wu_ref, wd_ref, xo_ref, rstd_ref, hg_ref, hu_ref, xb, acc):
        j = pl.program_id(1)

        @pl.when(j == 0)
        def _():
            xb[...] = (x_ref[...] * g_ref[...] + b_ref[...]).astype(bf16)
            acc[...] = jnp.zeros_like(acc)

        hg = jnp.dot(xb[...], wg_ref[...], preferred_element_type=f32)
        hu = jnp.dot(xb[...], wu_ref[...], preferred_element_type=f32)
        hg_ref[...] = hg.astype(bf16)
        hu_ref[...] = hu.astype(bf16)
        a = hg * _sigmoid_tanh(hg) * hu
        acc[...] += jnp.dot(a.astype(bf16), wd_ref[...], preferred_element_type=f32)

        @pl.when(j == nj - 1)
        def _():
            x = x_ref[...] * g_ref[...] + b_ref[...]
            xo, rstd = _ln_fwd_tile(ALPHA * x + 0.5 * acc[...])
            xo_ref[...] = xo
            rstd_ref[...] = rstd

    row = pl.BlockSpec((1, D_MODEL), lambda i, j: (0, 0))
    return _hosted_call(
        host, body, name=name, grid=(t // tm, nj),
        in_specs=[pl.BlockSpec((tm, D_MODEL), lambda i, j: (i, 0)), row, row,
                  pl.BlockSpec((None, D_MODEL, FF_TILE), lambda i, j: (j, 0, 0)),
                  pl.BlockSpec((None, D_MODEL, FF_TILE), lambda i, j: (j, 0, 0)),
                  pl.BlockSpec((None, FF_TILE, D_MODEL), lambda i, j: (j, 0, 0))],
        out_specs=[pl.BlockSpec((tm, D_MODEL), lambda i, j: (i, 0)),
                   pl.BlockSpec((tm, 1), lambda i, j: (i, 0)),
                   pl.BlockSpec((tm, FF_TILE), lambda i, j: (i, j)),
                   pl.BlockSpec((tm, FF_TILE), lambda i, j: (i, j))],
        out_shape=[jax.ShapeDtypeStruct((t, D_MODEL), f32), jax.ShapeDtypeStruct((t, 1), f32),
                   jax.ShapeDtypeStruct((t, D_FF), bf16), jax.ShapeDtypeStruct((t, D_FF), bf16)],
        scratch_shapes=[pltpu.VMEM((tm, D_MODEL), bf16), pltpu.VMEM((tm, D_MODEL), f32)],
        compiler_params=_params(("arbitrary", "arbitrary")),
    )(xhat, g_in, b_in, wg, wu, wd)


def _ffn1_fwd_gathering(x, own, extra, *, tm, name):
    t = x.shape[0]
    n_i = t // tm
    n_arr = 3
    k_extra = extra.n
    ex = _Exchange(list(own), gather=True)
    ax, ay, ac = lax.axis_index("x"), lax.axis_index("y"), lax.axis_index("c")
    order = jnp.stack([4 * px + 2 * py + pc for px, py in ((ax, ay), (1 - ax, ay), (ax, 1 - ay), (1 - ax, 1 - ay))
                       for pc in (ac, 1 - ac)]).astype(jnp.int32)
    arrival = [None, (0, None), (1, 0), (4, None), (2, 1), (5, None), (3, 2), (6, None)]

    def body(order_ref, x_ref, *refs):
        w_in, e_in = refs[:n_arr], refs[n_arr:n_arr + k_extra]
        refs = refs[n_arr + k_extra:]
        xo_ref, rstd_ref, hg_ref, hu_ref = refs[:4]
        w_all, e_out = refs[4:4 + n_arr], refs[4 + n_arr:4 + n_arr + k_extra]
        acc, wgb, wub, wdb, fetch_sems, send_sems, recv_sems, local_sems = refs[4 + n_arr + k_extra:12 + n_arr + k_extra]
        e_sems = refs[12 + n_arr + k_extra:]
        bufs = (wgb, wub, wdb)
        s = pl.program_id(0)
        i = pl.program_id(1)
        per_array, sibling, chips, (x_, y_, c_) = ex.gather_copies(w_in, w_all, (send_sems, recv_sems, local_sems))

        def fetch(pos, slot):
            return [pltpu.make_async_copy(w_in[a] if pos == 0 else w_all[a].at[order_ref[pos]],
                                          bufs[a].at[slot], fetch_sems.at[n_arr * slot + a]) for a in range(n_arr)]

        def source_of(pos):
            chip = (x_, y_) if pos < 2 else chips[(pos - 2) // 2]
            return (*chip, c_ if pos % 2 == 0 else 1 - c_)

        @pl.when(jnp.logical_and(s == 0, i == 0))
        def _():
            for q in range(4):
                for first, _, own_copy, _ in per_array:
                    if q == 0:
                        own_copy.start()
                    first[q].start()
            for cp in fetch(0, 0):
                cp.start()
            for cp in fetch(0, 0):
                cp.wait()

        @pl.when(jnp.logical_and(s == N_DEV // 2, i == 0))
        def _():
            extra.start(e_in, e_out, e_sems)

        for pos in range(1, N_DEV):
            @pl.when(jnp.logical_and(s == pos - 1, i == n_i - 1))
            def _(pos=pos):
                sem, passes = arrival[pos]
                for _, passed, _, copy in per_array:
                    copy(sem, source_of(pos), (x_, y_, c_)).wait_recv()
                    if passes is not None:
                        passed[passes].start()
                for cp in fetch(pos, pos % 2):
                    cp.start()

            @pl.when(jnp.logical_and(s == pos, i == 0))
            def _(pos=pos):
                for cp in fetch(pos, pos % 2):
                    cp.wait()

        slot = s % 2
        xb = x_ref[...].astype(bf16)
        hg = jnp.dot(xb, wgb[slot], preferred_element_type=f32)
        hu = jnp.dot(xb, wub[slot], preferred_element_type=f32)
        hg_ref[...] = hg.astype(bf16)
        hu_ref[...] = hu.astype(bf16)
        a = hg * _sigmoid_tanh(hg) * hu
        part = jnp.dot(a.astype(bf16), wdb[slot], preferred_element_type=f32)

        @pl.when(s == 0)
        def _():
            acc[i] = part

        @pl.when(s > 0)
        def _():
            acc[i] += part

        @pl.when(s == N_DEV - 1)
        def _():
            xo, rstd = _ln_fwd_tile(ALPHA * x_ref[...] + 0.5 * acc[i])
            xo_ref[...] = xo
            rstd_ref[...] = rstd

        @pl.when(jnp.logical_and(s == N_DEV - 1, i == n_i - 1))
        def _():
            for first, passed, own_copy, _ in per_array:
                for cp in first + passed:
                    cp.wait_send()
                own_copy.wait()
            extra.wait(e_in, e_out, e_sems)

    hbm = pl.BlockSpec(memory_space=pl.ANY)
    last = N_DEV - 1
    tok_out = pl.BlockSpec((tm, D_MODEL), lambda s, i, o: (jnp.where(s == last, i, 0), 0))
    col_out = pl.BlockSpec((tm, 1), lambda s, i, o: (jnp.where(s == last, i, 0), 0))
    hid = pl.BlockSpec((tm, FF_TILE), lambda s, i, o: (i, o[s]))
    shard_shapes = [(N_DEV,) + w.shape for w in own]
    grid_spec = pltpu.PrefetchScalarGridSpec(
        num_scalar_prefetch=1, grid=(N_DEV, n_i),
        in_specs=[pl.BlockSpec((tm, D_MODEL), lambda s, i, o: (i, 0))] + [hbm] * (n_arr + k_extra),
        out_specs=[tok_out, col_out, hid, hid] + [hbm] * (n_arr + k_extra),
        scratch_shapes=[pltpu.VMEM((n_i, tm, D_MODEL), f32)]
        + [pltpu.VMEM((2,) + w.shape, bf16) for w in own]
        + [pltpu.SemaphoreType.DMA((2 * n_arr,))] + ex.scratch() + extra.scratch())
    res = pl.pallas_call(
        body, name=name, grid_spec=grid_spec,
        out_shape=[jax.ShapeDtypeStruct((t, D_MODEL), f32), jax.ShapeDtypeStruct((t, 1), f32),
                   jax.ShapeDtypeStruct((t, D_FF), bf16), jax.ShapeDtypeStruct((t, D_FF), bf16)]
        + [jax.ShapeDtypeStruct(sh, bf16) for sh in shard_shapes] + extra.out_shape(),
        compiler_params=_params(("arbitrary", "arbitrary")),
    )(order, x, *own, *extra.arrs)
    return res


def _ffn_bwd(dpre, hg, hu, wg, wu, wd, ln_in, *, tm, name, host=None):
    t = dpre.shape[0]
    nj = N_DEV
    with_ln = ln_in is not None

    def body(*refs):
        if with_ln:
            (dp_ref, hg_ref, hu_ref, wg_ref, wu_ref, wd_ref, xh_ref, rs_ref, g_ref,
             dx_ref, gg_ref, gb_ref, dhg_ref, dhu_ref, a_ref, dfb, acc) = refs
        else:
            (dp_ref, hg_ref, hu_ref, wg_ref, wu_ref, wd_ref,
             dx_ref, dhg_ref, dhu_ref, a_ref, dfb, acc) = refs
        i = pl.program_id(0)
        j = pl.program_id(1)

        @pl.when(j == 0)
        def _():
            dfb[...] = (0.5 * dp_ref[...]).astype(bf16)
            acc[...] = jnp.zeros_like(acc)

        da = lax.dot_general(dfb[...], wd_ref[...], _NT, preferred_element_type=f32)
        hgv = hg_ref[...].astype(f32)
        huv = hu_ref[...].astype(f32)
        sg = _sigmoid_tanh(hgv)
        silu = hgv * sg
        a_ref[...] = (silu * huv).astype(bf16)
        dhu = (da * silu).astype(bf16)
        dhg = (da * huv * (sg * (1.0 + hgv * (1.0 - sg)))).astype(bf16)
        dhg_ref[...] = dhg
        dhu_ref[...] = dhu
        acc[...] += (lax.dot_general(dhg, wg_ref[...], _NT, preferred_element_type=f32)
                     + lax.dot_general(dhu, wu_ref[...], _NT, preferred_element_type=f32))

        @pl.when(j == nj - 1)
        def _():
            dx = ALPHA * dp_ref[...] + acc[...]
            if with_ln:
                dprev, gg, gb = _ln_bwd_tile(dx, xh_ref[...], rs_ref[...], g_ref[...])
                dx_ref[...] = dprev

                @pl.when(i == 0)
                def _():
                    gg_ref[...] = gg
                    gb_ref[...] = gb

                @pl.when(i > 0)
                def _():
                    gg_ref[...] += gg
                    gb_ref[...] += gb
            else:
                dx_ref[...] = dx

    tok = pl.BlockSpec((tm, D_MODEL), lambda i, j: (i, 0), pipeline_mode=pl.Buffered(1))
    row = pl.BlockSpec((1, D_MODEL), lambda i, j: (0, 0))
    hid = pl.BlockSpec((tm, FF_TILE), lambda i, j: (i, j))
    in_specs = [tok, hid, hid,
                pl.BlockSpec((None, D_MODEL, FF_TILE), lambda i, j: (j, 0, 0)),
                pl.BlockSpec((None, D_MODEL, FF_TILE), lambda i, j: (j, 0, 0)),
                pl.BlockSpec((None, FF_TILE, D_MODEL), lambda i, j: (j, 0, 0))]
    args = [dpre, hg, hu, wg, wu, wd]
    out_specs = [tok]
    out_shape = [jax.ShapeDtypeStruct((t, D_MODEL), f32)]
    if with_ln:
        in_specs += [tok, pl.BlockSpec((tm, 1), lambda i, j: (i, 0)), row]
        args += list(ln_in)
        out_specs += [row, row]
        out_shape += [jax.ShapeDtypeStruct((1, D_MODEL), f32)] * 2
    out_specs += [hid, hid, hid]
    out_shape += [jax.ShapeDtypeStruct((t, D_FF), bf16)] * 3
    return _hosted_call(
        host, body, name=name, grid=(t // tm, nj), in_specs=in_specs, out_specs=out_specs, out_shape=out_shape,
        scratch_shapes=[pltpu.VMEM((tm, D_MODEL), bf16), pltpu.VMEM((tm, D_MODEL), f32)],
        compiler_params=_params(("arbitrary", "arbitrary")),
    )(*args)


def _ffn_bwd_act(dpre, hg, hu, wd, *, tm, name, host=None):
    t = dpre.shape[0]

    def body(dp_ref, hg_ref, hu_ref, wd_ref, dhg_ref, dhu_ref, a_ref, dfb):
        @pl.when(pl.program_id(1) == 0)
        def _():
            dfb[...] = (0.5 * dp_ref[...]).astype(bf16)

        da = lax.dot_general(dfb[...], wd_ref[...], _NT, preferred_element_type=f32)
        hgv = hg_ref[...].astype(f32)
        huv = hu_ref[...].astype(f32)
        sg = _sigmoid_tanh(hgv)
        silu = hgv * sg
        a_ref[...] = (silu * huv).astype(bf16)
        dhu_ref[...] = (da * silu).astype(bf16)
        dhg_ref[...] = (da * huv * (sg * (1.0 + hgv * (1.0 - sg)))).astype(bf16)

    hid = pl.BlockSpec((tm, FF_TILE), lambda i, j: (i, j))
    return _hosted_call(
        host, body, name=name, grid=(t // tm, N_DEV),
        in_specs=[pl.BlockSpec((tm, D_MODEL), lambda i, j: (i, 0)), hid, hid,
                  pl.BlockSpec((None, FF_TILE, D_MODEL), lambda i, j: (j, 0, 0))],
        out_specs=[hid, hid, hid], out_shape=[jax.ShapeDtypeStruct((t, D_FF), bf16)] * 3,
        scratch_shapes=[pltpu.VMEM((tm, D_MODEL), bf16)],
        compiler_params=_params(("arbitrary", "arbitrary")),
    )(dpre, hg, hu, wd)


def _ffn_bwd_dx(dpre, dhg, dhu, wg, wu, *, tm, name, host=None):
    t = dpre.shape[0]
    nj = N_DEV

    def body(dp_ref, dhg_ref, dhu_ref, wg_ref, wu_ref, dx_ref, acc):
        j = pl.program_id(1)

        @pl.when(j == 0)
        def _():
            acc[...] = jnp.zeros_like(acc)

        acc[...] += (lax.dot_general(dhg_ref[...], wg_ref[...], _NT, preferred_element_type=f32)
                     + lax.dot_general(dhu_ref[...], wu_ref[...], _NT, preferred_element_type=f32))

        @pl.when(j == nj - 1)
        def _():
            dx_ref[...] = ALPHA * dp_ref[...] + acc[...]

    tok = pl.BlockSpec((tm, D_MODEL), lambda i, j: (i, 0))
    hid = pl.BlockSpec((tm, FF_TILE), lambda i, j: (i, j))
    wspec = pl.BlockSpec((None, D_MODEL, FF_TILE), lambda i, j: (j, 0, 0))
    return _hosted_call(
        host, body, name=name, grid=(t // tm, nj), in_specs=[tok, hid, hid, wspec, wspec],
        out_specs=[tok], out_shape=[jax.ShapeDtypeStruct((t, D_MODEL), f32)],
        scratch_shapes=[pltpu.VMEM((tm, D_MODEL), f32)],
        compiler_params=_params(("arbitrary", "arbitrary")),
    )(dpre, dhg, dhu, wg, wu)


def _mm(a, b, *, mode, out_dtype, tm, tn, tk, name, affine=None, a_cols=None, b_cols=None,
        b_blocked=False, out_blocked=False, out_scale=None):
    if mode == "nn":
        m_full, k_full = a.shape
        m_dim, k_dim = (m_full, a_cols[1]) if a_cols else (m_full, k_full)
    else:
        k_dim, m_full = a.shape
        m_dim = a_cols[1] if a_cols else m_full
    a_off = a_cols[0] if a_cols else 0
    if b_blocked:
        n_dim = b.shape[0] * b.shape[2]
        assert b.shape[2] == tn
    else:
        n_dim = b_cols[1] if b_cols else b.shape[1]
    b_off = b_cols[0] if b_cols else 0
    assert m_dim % tm == 0 and n_dim % tn == 0 and k_dim % tk == 0, (name, m_dim, n_dim, k_dim)
    nk = k_dim // tk

    def body(*refs):
        if affine is not None:
            a_ref, g_ref, s_ref, b_ref, o_ref, acc = refs
        else:
            a_ref, b_ref, o_ref, acc = refs
        k = pl.program_id(2)

        @pl.when(k == 0)
        def _():
            acc[...] = jnp.zeros_like(acc)

        av = a_ref[...]
        if affine is not None:
            av = av * g_ref[...] + s_ref[...]
        av = av.astype(bf16)
        bv = b_ref[...].astype(bf16)
        if mode == "nn":
            acc[...] += jnp.dot(av, bv, preferred_element_type=f32)
        else:
            acc[...] += lax.dot_general(av, bv, _TN, preferred_element_type=f32)

        @pl.when(k == nk - 1)
        def _():
            res = acc[...] if out_scale is None else acc[...] * out_scale
            o_ref[...] = res.astype(out_dtype)

    if mode == "nn":
        a_spec = pl.BlockSpec((tm, tk), lambda i, j, k: (i, k + a_off))
        aff_spec = pl.BlockSpec((1, tk), lambda i, j, k: (0, k + a_off))
    else:
        a_spec = pl.BlockSpec((tk, tm), lambda i, j, k: (k, i + a_off))
        aff_spec = pl.BlockSpec((1, tm), lambda i, j, k: (0, i + a_off))
    if b_blocked:
        b_spec = pl.BlockSpec((None, tk, tn), lambda i, j, k: (j, k, 0))
    else:
        b_spec = pl.BlockSpec((tk, tn), lambda i, j, k: (k, j + b_off))
    if out_blocked:
        o_spec = pl.BlockSpec((None, tm, tn), lambda i, j, k: (j, i, 0))
        o_shape = jax.ShapeDtypeStruct((n_dim // tn, m_dim, tn), out_dtype)
    else:
        o_spec = pl.BlockSpec((tm, tn), lambda i, j, k: (i, j))
        o_shape = jax.ShapeDtypeStruct((m_dim, n_dim), out_dtype)
    in_specs = [a_spec] + ([aff_spec, aff_spec] if affine is not None else []) + [b_spec]
    args = [a] + (list(affine) if affine is not None else []) + [b]
    return pl.pallas_call(
        body, name=name, grid=(m_dim // tm, n_dim // tn, nk), in_specs=in_specs, out_specs=o_spec,
        out_shape=o_shape, scratch_shapes=[pltpu.VMEM((tm, tn), f32)],
        compiler_params=_params(("arbitrary", "arbitrary", "arbitrary")),
    )(*args)


def _mm_tn(a, b, *, out_dtype, tm, mb, tn, nb, tk, name, affine=None, out_blocked=False, out_scale=None,
           pair=False, host=None):
    k_dim, m_dim = a.shape
    multi_b = isinstance(b, (list, tuple))
    b_list = list(b) if multi_b else [b]
    n_dim = nb * tn if multi_b else b.shape[1]
    assert m_dim % (mb * tm) == 0 and n_dim % (nb * tn) == 0 and k_dim % tk == 0, (name, m_dim, n_dim, k_dim)
    nk = k_dim // tk
    grid = (m_dim // (mb * tm), n_dim // (nb * tn), nk)
    if pair:
        assert mb * nb == 4 and grid[0] * grid[1] == 2 and out_dtype == bf16, name

    def body(*refs):
        if pair:
            refs, (acc, send_buf, recv_buf, keep, send_sems, recv_sems) = refs[:-6], refs[-6:]
        else:
            refs, acc = refs[:-1], refs[-1]
        a_ref, o_ref = refs[0], refs[-1]
        if affine is not None:
            g_ref, s_ref = refs[1:3]
        b_refs = refs[3 if affine is not None else 1:-1]
        k = pl.program_id(2)

        @pl.when(k == 0)
        def _():
            acc[...] = jnp.zeros_like(acc)

        av = a_ref[...]
        if affine is not None:
            av = av * g_ref[...] + s_ref[...]
        av = av.astype(bf16)
        if multi_b:
            pieces = [r[...].astype(bf16) for r in b_refs]
        else:
            bv = b_refs[0][...].astype(bf16)
            pieces = [bv[:, jn * tn:(jn + 1) * tn] for jn in range(nb)]
        for im in range(mb):
            a_t = av[:, im * tm:(im + 1) * tm].T
            for jn in range(nb):
                acc[im * nb + jn] += jnp.dot(a_t, pieces[jn], preferred_element_type=f32)

        def scaled(v):
            return v if out_scale is None else v * out_scale

        @pl.when(k == nk - 1)
        def _():
            if pair:
                x, y, c = lax.axis_index("x"), lax.axis_index("y"), lax.axis_index("c")
                window = pl.program_id(0) + pl.program_id(1)

                def swap(w, cc):
                    return pltpu.make_async_remote_copy(
                        src_ref=send_buf.at[w, cc], dst_ref=recv_buf.at[w, cc],
                        send_sem=send_sems.at[2 * w + cc], recv_sem=recv_sems.at[2 * w + cc],
                        device_id=(x, y, 1 - c), device_id_type=MESH_T)

                for w in range(2):
                    @pl.when(window == w)
                    def _(w=w):
                        for cc in range(2):
                            send_buf[w, cc] = scaled(acc[2 * cc + 1 - c]).astype(bf16)
                            swap(w, cc).start()
                            if w == 0:
                                keep[cc] = scaled(acc[2 * cc + c])

                @pl.when(window == 1)
                def _():
                    for w in range(2):
                        for cc in range(2):
                            swap(w, cc).wait_recv()
                            mine = keep[cc] if w == 0 else scaled(acc[2 * cc + c])
                            o_ref[2 * w + cc] = (mine + recv_buf[w, cc].astype(f32)).astype(bf16)
                    for w in range(2):
                        for cc in range(2):
                            swap(w, cc).wait_send()
                return
            for im in range(mb):
                for jn in range(nb):
                    res = scaled(acc[im * nb + jn])
                    if out_blocked:
                        o_ref[jn, im * tm:(im + 1) * tm, :] = res.astype(out_dtype)
                    else:
                        o_ref[im * tm:(im + 1) * tm, jn * tn:(jn + 1) * tn] = res.astype(out_dtype)

    a_spec = pl.BlockSpec((tk, mb * tm), lambda i, j, k: (k, i))
    aff_spec = pl.BlockSpec((1, mb * tm), lambda i, j, k: (0, i))
    if multi_b:
        b_specs = [pl.BlockSpec((tk, tn), lambda i, j, k: (k, 0))] * nb
    else:
        b_specs = [pl.BlockSpec((tk, nb * tn), lambda i, j, k: (k, j))]
    scratch = [pltpu.VMEM((mb * nb, tm, tn), f32)]
    if pair:
        o_spec = pl.BlockSpec((4, tm, tn), lambda i, j, k: (0, 0, 0))
        o_shape = jax.ShapeDtypeStruct((4, tm, tn), out_dtype)
        scratch += [pltpu.VMEM((2, 2, tm, tn), bf16), pltpu.VMEM((2, 2, tm, tn), bf16), pltpu.VMEM((2, tm, tn), f32),
                    pltpu.SemaphoreType.DMA((4,)), pltpu.SemaphoreType.DMA((4,))]
    elif out_blocked:
        o_spec = pl.BlockSpec((nb, mb * tm, tn), lambda i, j, k: (j, i, 0))
        o_shape = jax.ShapeDtypeStruct((n_dim // tn, m_dim, tn), out_dtype)
    else:
        o_spec = pl.BlockSpec((mb * tm, nb * tn), lambda i, j, k: (i, j))
        o_shape = jax.ShapeDtypeStruct((m_dim, n_dim), out_dtype)
    in_specs = [a_spec] + ([aff_spec, aff_spec] if affine is not None else []) + b_specs
    args = [a] + (list(affine) if affine is not None else []) + b_list
    res = _hosted_call(
        host, body, name=name, grid=grid, in_specs=in_specs, out_specs=o_spec, out_shape=o_shape,
        scratch_shapes=scratch, compiler_params=_params(("arbitrary", "arbitrary", "arbitrary")),
    )(*args)
    return res[0] if host is None else res


def _in_proj(xhat, g, b, w_in, *, tm, name):
    t = xhat.shape[0]
    n_qkv, n_l = 3 * FOX_W, 2 * LRU_W

    def body(x_ref, g_ref, b_ref, w_ref, qkv_ref, zl_ref, zfg_ref):
        xb = (x_ref[...] * g_ref[...] + b_ref[...]).astype(bf16)
        qkv_ref[...] = jnp.dot(xb, w_ref[:, :n_qkv], preferred_element_type=f32).astype(bf16)
        zl_ref[...] = jnp.dot(xb, w_ref[:, n_qkv:n_qkv + n_l], preferred_element_type=f32)
        zfg_ref[...] = jnp.dot(xb, w_ref[:, n_qkv + n_l:], preferred_element_type=f32)

    row = pl.BlockSpec((1, D_MODEL), lambda i: (0, 0))
    return pl.pallas_call(
        body, name=name, grid=(t // tm,),
        in_specs=[pl.BlockSpec((tm, D_MODEL), lambda i: (i, 0)), row, row,
                  pl.BlockSpec(w_in.shape, lambda i: (0, 0))],
        out_specs=[pl.BlockSpec((tm, n_qkv), lambda i: (i, 0)), pl.BlockSpec((tm, n_l), lambda i: (i, 0)),
                   pl.BlockSpec((tm, LANES), lambda i: (i, 0))],
        out_shape=[jax.ShapeDtypeStruct((t, n_qkv), bf16), jax.ShapeDtypeStruct((t, n_l), f32),
                   jax.ShapeDtypeStruct((t, LANES), f32)],
        compiler_params=_params(("arbitrary",)),
    )(xhat, g, b, w_in)


def _mmln(pairs, *, tm, name, resid=None, resid_scale=1.0, epi=None, ln=None, n_out=D_MODEL):
    t = pairs[0][0].shape[0]
    n_pairs = len(pairs)
    n_resid = 0 if resid is None else len(resid) - 1

    def body(*refs):
        pos = 0
        val = None
        for p in range(n_pairs):
            a_ref, b_ref = refs[pos], refs[pos + 1]
            pos += 2
            av = a_ref[...].astype(bf16)
            bv = b_ref[...].astype(bf16)
            if pairs[p][6] == "nn":
                term = jnp.dot(av, bv, preferred_element_type=f32)
            else:
                term = lax.dot_general(av, bv, _NT, preferred_element_type=f32)
            val = term if val is None else val + term
        if resid is not None:
            if resid[0] == "plain":
                r = refs[pos][...]
            else:
                r = refs[pos][...] * refs[pos + 1][...] + refs[pos + 2][...]
            pos += n_resid
            val = val + resid_scale * r
        if epi is None:
            o_ref = refs[pos]
            o_ref[...] = val.astype(o_ref.dtype)
        elif epi == "ln_fwd":
            xo, rstd = _ln_fwd_tile(val)
            refs[pos][...] = xo
            refs[pos + 1][...] = rstd
        else:
            xh_ref, rs_ref, g_ref, dx_ref, gg_ref, gb_ref = refs[pos:pos + 6]
            dprev, gg, gb = _ln_bwd_tile(val, xh_ref[...], rs_ref[...], g_ref[...])
            dx_ref[...] = dprev
            i = pl.program_id(0)

            @pl.when(i == 0)
            def _():
                gg_ref[...] = gg
                gb_ref[...] = gb

            @pl.when(i > 0)
            def _():
                gg_ref[...] += gg
                gb_ref[...] += gb

    in_specs, args = [], []
    for (a, acb, aw, b, bcb, bw, mode) in pairs:
        in_specs.append(pl.BlockSpec((tm, aw), lambda i, acb=acb: (i, acb)))
        args.append(a)
        if mode == "nn":
            in_specs.append(pl.BlockSpec((aw, n_out), lambda i, bcb=bcb: (bcb, 0)))
        else:
            in_specs.append(pl.BlockSpec((n_out, bw), lambda i, bcb=bcb: (0, bcb)))
        args.append(b)
    tok = pl.BlockSpec((tm, n_out), lambda i: (i, 0))
    row = pl.BlockSpec((1, n_out), lambda i: (0, 0))
    col = pl.BlockSpec((tm, 1), lambda i: (i, 0))
    if resid is not None:
        in_specs += [tok] if resid[0] == "plain" else [tok, row, row]
        args += list(resid[1:])
    if epi is None:
        out_specs, out_shape = tok, jax.ShapeDtypeStruct((t, n_out), f32)
    elif epi == "ln_fwd":
        out_specs = [tok, col]
        out_shape = [jax.ShapeDtypeStruct((t, n_out), f32), jax.ShapeDtypeStruct((t, 1), f32)]
    else:
        in_specs += [tok, col, row]
        args += list(ln)
        out_specs = [tok, row, row]
        out_shape = [jax.ShapeDtypeStruct((t, n_out), f32)] + [jax.ShapeDtypeStruct((1, n_out), f32)] * 2
    return pl.pallas_call(
        body, name=name, grid=(t // tm,), in_specs=in_specs, out_specs=out_specs, out_shape=out_shape,
        compiler_params=_params(("arbitrary",)),
    )(*args)


def _loss_bwd(xhat, rstd, g, b, target, *, tm, name):
    t = xhat.shape[0]

    def body(xh_ref, rs_ref, g_ref, b_ref, tg_ref, dx_ref, sq_ref, gg_ref, gb_ref):
        i = pl.program_id(0)
        xh = xh_ref[...]
        diff = xh * g_ref[...] + b_ref[...] - tg_ref[...]
        sq = jnp.sum(diff * diff, axis=0, keepdims=True)
        dprev, gg, gb = _ln_bwd_tile(diff * (1.0 / D_MODEL), xh, rs_ref[...], g_ref[...])
        dx_ref[...] = dprev

        @pl.when(i == 0)
        def _():
            sq_ref[...] = sq
            gg_ref[...] = gg
            gb_ref[...] = gb

        @pl.when(i > 0)
        def _():
            sq_ref[...] += sq
            gg_ref[...] += gg
            gb_ref[...] += gb

    tok = pl.BlockSpec((tm, D_MODEL), lambda i: (i, 0))
    row = pl.BlockSpec((1, D_MODEL), lambda i: (0, 0))
    return pl.pallas_call(
        body, name=name, grid=(t // tm,),
        in_specs=[tok, pl.BlockSpec((tm, 1), lambda i: (i, 0)), row, row, tok],
        out_specs=[tok, row, row, row],
        out_shape=[jax.ShapeDtypeStruct((t, D_MODEL), f32)] + [jax.ShapeDtypeStruct((1, D_MODEL), f32)] * 3,
        compiler_params=_params(("arbitrary",)),
    )(xhat, rstd, g, b, target)


CUM_TILE = 256


def _tri(n, lower):
    r = lax.broadcasted_iota(jnp.int32, (n, n), 0)
    c = lax.broadcasted_iota(jnp.int32, (n, n), 1)
    return jnp.where((r >= c) if lower else (r <= c), 1.0, 0.0).astype(f32)


def _cum_fwd(zfg, bfg, *, name):
    t = zfg.shape[0]

    def body(z_ref, b_ref, o_ref, carry):
        @pl.when(pl.program_id(0) == 0)
        def _():
            carry[...] = jnp.zeros_like(carry)

        ls = -_softplus(-(z_ref[...] + b_ref[...]))
        c = jnp.dot(_tri(CUM_TILE, True), ls, preferred_element_type=f32,
                    precision=lax.Precision.HIGHEST) + carry[...]
        o_ref[...] = c
        carry[...] = c[CUM_TILE - 1:CUM_TILE, :]

    blk = pl.BlockSpec((CUM_TILE, LANES), lambda i: (i, 0))
    return pl.pallas_call(
        body, name=name, grid=(t // CUM_TILE,),
        in_specs=[blk, pl.BlockSpec((1, LANES), lambda i: (0, 0))], out_specs=blk,
        out_shape=jax.ShapeDtypeStruct((t, LANES), f32), scratch_shapes=[pltpu.VMEM((1, LANES), f32)],
        compiler_params=_params(("arbitrary",)),
    )(zfg, bfg)


def _cum_bwd(dcum_q, dcum_k, zfg, bfg, *, name):
    t = zfg.shape[0]
    n = t // CUM_TILE

    def body(d_ref, d2_ref, z_ref, b_ref, o_ref, s_ref, carry):
        i = pl.program_id(0)

        @pl.when(i == 0)
        def _():
            carry[...] = jnp.zeros_like(carry)

        dls = jnp.dot(_tri(CUM_TILE, False), d_ref[...] + d2_ref[...], preferred_element_type=f32,
                      precision=lax.Precision.HIGHEST) + carry[...]
        carry[...] = dls[0:1, :]
        lane = lax.broadcasted_iota(jnp.int32, (CUM_TILE, LANES), 1)
        dfg = jnp.where(lane < HEADS, dls * _sigmoid(-(z_ref[...] + b_ref[...])), 0.0)
        o_ref[...] = dfg
        tot = jnp.sum(dfg, axis=0, keepdims=True)

        @pl.when(i == 0)
        def _():
            s_ref[...] = tot

        @pl.when(i > 0)
        def _():
            s_ref[...] += tot

    blk = pl.BlockSpec((CUM_TILE, LANES), lambda i: (n - 1 - i, 0))
    row = pl.BlockSpec((1, LANES), lambda i: (0, 0))
    return pl.pallas_call(
        body, name=name, grid=(n,), in_specs=[blk, blk, blk, row], out_specs=[blk, row],
        out_shape=[jax.ShapeDtypeStruct((t, LANES), f32), jax.ShapeDtypeStruct((1, LANES), f32)],
        scratch_shapes=[pltpu.VMEM((1, LANES), f32)],
        compiler_params=_params(("arbitrary",)),
    )(dcum_q, dcum_k, zfg, bfg)


ATT_TILE = 512


def _causal(i, j, transposed):
    r = lax.broadcasted_iota(jnp.int32, (ATT_TILE, ATT_TILE), 0)
    c = lax.broadcasted_iota(jnp.int32, (ATT_TILE, ATT_TILE), 1)
    if transposed:
        return (c + i * ATT_TILE) >= (r + j * ATT_TILE)
    return (r + i * ATT_TILE) >= (c + j * ATT_TILE)


ATT_W = HEADS * LANES


def _data_lane(h):
    return HEAD_D * (h % 2)


def _extra_lane(h):
    return HEAD_D - _data_lane(h)


def _split3(x):
    hi = x.astype(bf16)
    rest = x - hi.astype(f32)
    mid = rest.astype(bf16)
    lo = (rest - mid.astype(f32)).astype(bf16)
    return hi, mid, lo


def _three_pieces(x):
    hi, mid, lo = (p.astype(f32) for p in _split3(x))
    return (hi + pltpu.roll(mid, HEADS, axis=1) + pltpu.roll(lo, 2 * HEADS, axis=1)).astype(bf16)


def _move(h, first):
    r = lax.broadcasted_iota(jnp.int32, (LANES, LANES), 0)
    c = lax.broadcasted_iota(jnp.int32, (LANES, LANES), 1)
    hit = functools.reduce(jnp.logical_or, [jnp.logical_and(r == HEADS * q + h, c == first + q) for q in range(3)])
    return jnp.where(hit, 1.0, 0.0).astype(bf16)


def _ones_from(first, rows):
    lane = lax.broadcasted_iota(jnp.int32, (rows, LANES), 1)
    return jnp.where(jnp.logical_and(lane >= first, lane < first + 3), 1.0, 0.0)


def _own_lanes(h, rows):
    lane = lax.broadcasted_iota(jnp.int32, (rows, LANES), 1)
    return (lane < HEAD_D) if h % 2 == 0 else (lane >= HEAD_D)


def _head_values(x):
    lane = lax.broadcasted_iota(jnp.int32, x.shape, 1)
    return jnp.where(lane < HEADS, x, 0.0)


def _attn_prep_fwd(qkv, cum, *, tm, name):
    t = qkv.shape[0]

    def body(q_ref, k_ref, v_ref, c_ref, qa_ref, ka_ref, va_ref):
        c3 = _three_pieces(_head_values(c_ref[...]))
        ones = jnp.ones((tm, LANES), bf16)
        for h in range(HEADS):
            pair = slice(LANES * (h // 2), LANES * (h // 2 + 1))
            hs = slice(LANES * h, LANES * (h + 1))
            base, own = _extra_lane(h), _own_lanes(h, tm)
            eq = jnp.dot(c3, _move(h, base), preferred_element_type=f32) + _ones_from(base + 3, tm)
            ek = _ones_from(base, tm) - jnp.dot(c3, _move(h, base + 3), preferred_element_type=f32)
            qa_ref[:, hs] = jnp.where(own, q_ref[:, pair] * ATT_SCALE, eq.astype(bf16))
            ka_ref[:, hs] = jnp.where(own, k_ref[:, pair], ek.astype(bf16))
            va_ref[:, hs] = jnp.where(own, v_ref[:, pair], ones)

    wide = pl.BlockSpec((tm, ATT_W), lambda i: (i, 0))
    out = jax.ShapeDtypeStruct((t, ATT_W), bf16)
    return pl.pallas_call(
        body, name=name, grid=(t // tm,),
        in_specs=[pl.BlockSpec((tm, FOX_W), lambda i: (i, 0)), pl.BlockSpec((tm, FOX_W), lambda i: (i, 1)),
                  pl.BlockSpec((tm, FOX_W), lambda i: (i, 2)), pl.BlockSpec((tm, LANES), lambda i: (i, 0))],
        out_specs=[wide] * 3, out_shape=[out] * 3, compiler_params=_params(("arbitrary",)),
    )(qkv, qkv, qkv, cum)


def _attn_prep_bwd(qkv, cum, lse, dmix, o, *, tm, name):
    t = qkv.shape[0]

    def body(q_ref, c_ref, l_ref, do_ref, o_ref, qa_ref, da_ref):
        b3 = _three_pieces(_head_values(c_ref[...] - l_ref[...]))
        r = lax.broadcasted_iota(jnp.int32, (FOX_W, LANES), 0)
        c = lax.broadcasted_iota(jnp.int32, (FOX_W, LANES), 1)
        per_head = jnp.where(r // HEAD_D == c, 1.0, 0.0).astype(bf16)
        delta = sum(jnp.dot(p, per_head, preferred_element_type=f32) for p in _split3(do_ref[...] * o_ref[...]))
        d3 = _three_pieces(delta)
        for h in range(HEADS):
            pair = slice(LANES * (h // 2), LANES * (h // 2 + 1))
            hs = slice(LANES * h, LANES * (h + 1))
            base, own = _extra_lane(h), _own_lanes(h, tm)
            eq = jnp.dot(b3, _move(h, base), preferred_element_type=f32) + _ones_from(base + 3, tm)
            ed = -jnp.dot(d3, _move(h, base), preferred_element_type=f32)
            qa_ref[:, hs] = jnp.where(own, q_ref[:, pair] * ATT_SCALE, eq.astype(bf16))
            da_ref[:, hs] = jnp.where(own, do_ref[:, pair].astype(bf16), ed.astype(bf16))

    wide = pl.BlockSpec((tm, ATT_W), lambda i: (i, 0))
    half = pl.BlockSpec((tm, FOX_W), lambda i: (i, 0))
    col = pl.BlockSpec((tm, LANES), lambda i: (i, 0))
    out = jax.ShapeDtypeStruct((t, ATT_W), bf16)
    return pl.pallas_call(
        body, name=name, grid=(t // tm,), in_specs=[half, col, col, half, half],
        out_specs=[wide] * 2, out_shape=[out] * 2, compiler_params=_params(("arbitrary",)),
    )(qkv, cum, lse, dmix, o)


def _attn_fwd2(q_aug, k_aug, v_aug, *, name, host=None):
    t = q_aug.shape[0]
    n = t // ATT_TILE
    tq = ATT_TILE

    def body(q_ref, k_ref, v_ref, o_ref, lse_ref, acc, m_s):
        i = pl.program_id(0)
        j = pl.program_id(1)

        @pl.when(j == 0)
        def _():
            acc[...] = jnp.zeros_like(acc)
            m_s[...] = jnp.full_like(m_s, NEG_BIG)

        def block(masked):
            mask = _causal(i, j, False) if masked else None
            for h in range(HEADS):
                hs = slice(LANES * h, LANES * (h + 1))
                s = lax.dot_general(q_ref[:, hs], k_ref[:, hs], _NT, preferred_element_type=f32)
                if masked:
                    s = jnp.where(mask, s, NEG_BIG)
                blocks = [s[:, LANES * b:LANES * (b + 1)] for b in range(tq // LANES)]
                m_old = m_s[h]
                m_new = jnp.maximum(m_old, jnp.broadcast_to(
                    jnp.max(functools.reduce(jnp.maximum, blocks), axis=-1, keepdims=True), (tq, LANES)))
                p = jnp.concatenate([jnp.exp(b - m_new) for b in blocks], axis=1).astype(bf16)
                acc[h] = jnp.exp(m_old - m_new) * acc[h] + jnp.dot(p, v_ref[:, hs], preferred_element_type=f32)
                m_s[h] = m_new

        @pl.when(j < i)
        def _():
            block(False)

        @pl.when(j == i)
        def _():
            block(True)
            lse_ref[...] = jnp.zeros_like(lse_ref)
            for h in range(HEADS):
                a = acc[h]
                l = a[:, _extra_lane(h):_extra_lane(h) + 1]
                o_ref[:, HEAD_D * h:HEAD_D * (h + 1)] = a[:, _data_lane(h):_data_lane(h) + HEAD_D] / l
                lse_ref[:, h:h + 1] = m_s[h][:, 0:1] + jnp.log(l)

    kv = pl.BlockSpec((tq, ATT_W), lambda i, j: (jnp.minimum(i, j), 0))
    return _hosted_call(
        host, body, name=name, grid=(n, n),
        in_specs=[pl.BlockSpec((tq, ATT_W), lambda i, j: (i, 0)), kv, kv],
        out_specs=[pl.BlockSpec((tq, FOX_W), lambda i, j: (i, 0)), pl.BlockSpec((tq, LANES), lambda i, j: (i, 0))],
        out_shape=[jax.ShapeDtypeStruct((t, FOX_W), f32), jax.ShapeDtypeStruct((t, LANES), f32)],
        scratch_shapes=[pltpu.VMEM((HEADS, tq, LANES), f32), pltpu.VMEM((HEADS, tq, LANES), f32)],
        compiler_params=_params(("arbitrary", "arbitrary")),
    )(q_aug, k_aug, v_aug)


def _attn_bwd(qb_aug, k_aug, v_aug, do_aug, *, name, host=None):
    t = qb_aug.shape[0]
    n = t // ATT_TILE
    tk = ATT_TILE

    def body(q_ref, k_ref, v_ref, do_ref, dq_ref, dcq_ref, dk_ref, dv_ref, dck_ref, dk_acc, dv_acc, dq_all):
        j = pl.program_id(0)
        i = pl.program_id(1)

        @pl.when(jnp.logical_and(i == 0, j == 0))
        def _():
            dq_all[...] = jnp.zeros_like(dq_all)

        @pl.when(i == 0)
        def _():
            dk_acc[...] = jnp.zeros_like(dk_acc)
            dv_acc[...] = jnp.zeros_like(dv_acc)

        def block(masked):
            mask = _causal(i, j, True) if masked else None
            for h in range(HEADS):
                hs = slice(LANES * h, LANES * (h + 1))
                qh = q_ref[:, hs]
                doh = do_ref[:, hs]
                kh = k_ref[:, hs]
                s_t = lax.dot_general(kh, qh, _NT, preferred_element_type=f32)
                if masked:
                    s_t = jnp.where(mask, s_t, NEG_BIG)
                p_t = jnp.exp(s_t)
                dv_acc[h] += jnp.dot(p_t.astype(bf16), doh, preferred_element_type=f32)
                dp_t = lax.dot_general(v_ref[:, hs], doh, _NT, preferred_element_type=f32)
                ds_t = (p_t * dp_t).astype(bf16)
                dk_acc[h] += jnp.dot(ds_t, qh, preferred_element_type=f32)
                dq_all[i, h] += lax.dot_general(ds_t, kh, _TN, preferred_element_type=f32)

        @pl.when(i > j)
        def _():
            block(False)

        @pl.when(i == j)
        def _():
            block(True)
            dcq_ref[...] = jnp.zeros_like(dcq_ref)
            for h in range(HEADS):
                a = dq_all[j, h]
                dq_ref[:, HEAD_D * h:HEAD_D * (h + 1)] = (
                    a[:, _data_lane(h):_data_lane(h) + HEAD_D] * ATT_SCALE).astype(bf16)
                dcq_ref[:, h:h + 1] = a[:, _extra_lane(h):_extra_lane(h) + 1]

        @pl.when(i == n - 1)
        def _():
            dck_ref[...] = jnp.zeros_like(dck_ref)
            for h in range(HEADS):
                a = dk_acc[h]
                cols = slice(_data_lane(h), _data_lane(h) + HEAD_D)
                dk_ref[:, HEAD_D * h:HEAD_D * (h + 1)] = a[:, cols].astype(bf16)
                dv_ref[:, HEAD_D * h:HEAD_D * (h + 1)] = dv_acc[h][:, cols].astype(bf16)
                dck_ref[:, h:h + 1] = -a[:, _extra_lane(h) + 3:_extra_lane(h) + 4]

    own = pl.BlockSpec((tk, ATT_W), lambda j, i: (j, 0))
    qs = pl.BlockSpec((tk, ATT_W), lambda j, i: (jnp.maximum(i, j), 0))
    half = pl.BlockSpec((tk, FOX_W), lambda j, i: (j, 0))
    col = pl.BlockSpec((tk, LANES), lambda j, i: (j, 0))
    return _hosted_call(
        host, body, name=name, grid=(n, n), in_specs=[qs, own, own, qs],
        out_specs=[half, col, half, half, col],
        out_shape=[jax.ShapeDtypeStruct((t, FOX_W), bf16), jax.ShapeDtypeStruct((t, LANES), f32),
                   jax.ShapeDtypeStruct((t, FOX_W), bf16), jax.ShapeDtypeStruct((t, FOX_W), bf16),
                   jax.ShapeDtypeStruct((t, LANES), f32)],
        scratch_shapes=[pltpu.VMEM((HEADS, tk, LANES), f32), pltpu.VMEM((HEADS, tk, LANES), f32),
                        pltpu.VMEM((n, HEADS, tk, LANES), f32)],
        compiler_params=_params(("arbitrary", "arbitrary")),
    )(qb_aug, k_aug, v_aug, do_aug)


LRU_CHUNK = 64
LRU_G = 256
SUB = 8


def _row_ids(n):
    return lax.broadcasted_iota(jnp.int32, (n, LRU_G), 0)


def _shift_rows_down(ext, s):
    return pltpu.roll(ext, s, axis=0)[SUB:, :]


def _shift_rows_up(ext, s, n):
    return pltpu.roll(ext, ext.shape[0] - s, axis=0)[:n, :]


def _lru_gates(u, wa_ref, ba_ref, wx_ref, bx_ref, sp):
    ub = u.astype(bf16)
    r = _sigmoid(jnp.dot(ub, wa_ref[...], preferred_element_type=f32) + ba_ref[...])
    gi = _sigmoid(jnp.dot(ub, wx_ref[...], preferred_element_type=f32) + bx_ref[...])
    log_a = -LRU_C * r * sp
    a = jnp.exp(log_a)
    s = jnp.sqrt(_one_minus_exp(2.0 * log_a))
    return r, gi, a, s


def _conv_window(lx_ref, r0, ci):
    cur = lx_ref[pl.ds(r0, LRU_CHUNK), :]
    p0 = pl.multiple_of(jnp.maximum(r0 - SUB, 0), SUB)
    prev = jnp.where(ci > 0, lx_ref[pl.ds(p0, SUB), :], 0.0)
    return cur, jnp.concatenate([prev, cur], axis=0)


def _lru_fwd(zl, conv_w, conv_b, wa, ba, wx, bx, lam, *, name, host=None):
    t = zl.shape[0]
    n_chunk = t // LRU_CHUNK

    def body(lx_ref, lg_ref, cw_ref, cb_ref, wa_ref, ba_ref, wx_ref, bx_ref, lam_ref, u_ref, h_ref, y_ref):
        sp = _softplus(-lam_ref[...])
        rows = _row_ids(SUB)

        def chunk(ci, hc):
            r0 = pl.multiple_of(ci * LRU_CHUNK, LRU_CHUNK)
            cur, ext = _conv_window(lx_ref, r0, ci)
            u = cb_ref[...] + cw_ref[3:4, :] * cur
            for k in range(3):
                u = u + cw_ref[k:k + 1, :] * _shift_rows_down(ext, 3 - k)
            r, gi, a, s = _lru_gates(u, wa_ref, ba_ref, wx_ref, bx_ref, sp)
            b = s * (gi * u)
            tiles = []
            for q in range(LRU_CHUNK // SUB):
                ta = a[SUB * q:SUB * (q + 1), :]
                tb = b[SUB * q:SUB * (q + 1), :]
                for d in (1, 2, 4):
                    a_sh = jnp.where(rows >= d, pltpu.roll(ta, d, axis=0), 1.0)
                    b_sh = jnp.where(rows >= d, pltpu.roll(tb, d, axis=0), 0.0)
                    tb = ta * b_sh + tb
                    ta = ta * a_sh
                hq = tb + ta * hc
                hc = hq[SUB - 1:SUB, :]
                tiles.append(hq)
            h = jnp.concatenate(tiles, axis=0)
            u_ref[pl.ds(r0, LRU_CHUNK), :] = u
            h_ref[pl.ds(r0, LRU_CHUNK), :] = h
            gel, _ = _gelu_and_grad(lg_ref[pl.ds(r0, LRU_CHUNK), :])
            y_ref[pl.ds(r0, LRU_CHUNK), :] = gel * h
            return hc

        lax.fori_loop(0, n_chunk, chunk, jnp.zeros((1, LRU_G), f32))

    seq = lambda cb: pl.BlockSpec((t, LRU_G), lambda c, cb=cb: (0, c + cb))
    rowc = pl.BlockSpec((1, LRU_G), lambda c: (0, c))
    diag = pl.BlockSpec((LRU_G, LRU_G), lambda c: (c, c))
    out = jax.ShapeDtypeStruct((t, LRU_W), f32)
    return _hosted_call(
        host, body, name=name, grid=(LRU_W // LRU_G,),
        in_specs=[seq(0), seq(LRU_W // LRU_G), pl.BlockSpec((4, LRU_G), lambda c: (0, c)),
                  rowc, diag, rowc, diag, rowc, rowc],
        out_specs=[seq(0)] * 3, out_shape=[out] * 3,
        compiler_params=_params(("arbitrary",)),
    )(zl, zl, conv_w, conv_b, wa, ba, wx, bx, lam)


def _lru_bwd(dmix, zl, u_all, h_all, conv_w, wa, ba, wx, bx, lam, *, name, host=None):
    t = zl.shape[0]
    n_chunk = t // LRU_CHUNK

    def body(dy_ref, lx_ref, lg_ref, u_ref, h_ref, cw_ref, wa_ref, ba_ref, wx_ref, bx_ref, lam_ref,
             dlx_ref, dlg_ref, dcw_ref, dcb_ref, dba_ref, dbx_ref, dlam_ref, dwa_ref, dwx_ref, dpr_s, dpx_s):
        lam_v = lam_ref[...]
        sp = _softplus(-lam_v)
        rows = _row_ids(SUB)
        rows_c = _row_ids(LRU_CHUNK)
        zero_row = jnp.zeros((1, LRU_G), f32)

        def chunk(step, carry):
            dh_c, a_next0, du_next, dsp, dba, dbx, dcb, dw0, dw1, dw2, dw3 = carry
            ci = n_chunk - 1 - step
            r0 = pl.multiple_of(ci * LRU_CHUNK, LRU_CHUNK)
            sl = pl.ds(r0, LRU_CHUNK)
            u = u_ref[sl, :]
            r, gi, a, s = _lru_gates(u, wa_ref, ba_ref, wx_ref, bx_ref, sp)
            h = h_ref[sl, :]
            p0 = pl.multiple_of(jnp.maximum(r0 - SUB, 0), SUB)
            h_before = jnp.where(ci > 0, h_ref[pl.ds(p0, SUB), :], 0.0)[SUB - 1:SUB, :]
            h_prev = jnp.where(rows_c == 0, h_before, pltpu.roll(h, 1, axis=0))
            gel, dgel = _gelu_and_grad(lg_ref[sl, :])
            dy = dy_ref[sl, :]
            dlg_ref[sl, :] = (dy * h * dgel).astype(bf16)
            g_in = dy * gel
            a_next = jnp.where(rows_c == LRU_CHUNK - 1, a_next0, pltpu.roll(a, LRU_CHUNK - 1, axis=0))
            tiles = [None] * (LRU_CHUNK // SUB)
            for q in reversed(range(LRU_CHUNK // SUB)):
                ta = a_next[SUB * q:SUB * (q + 1), :]
                tb = g_in[SUB * q:SUB * (q + 1), :]
                for d in (1, 2, 4):
                    a_sh = jnp.where(rows < SUB - d, pltpu.roll(ta, SUB - d, axis=0), 1.0)
                    b_sh = jnp.where(rows < SUB - d, pltpu.roll(tb, SUB - d, axis=0), 0.0)
                    tb = ta * b_sh + tb
                    ta = ta * a_sh
                dhq = tb + ta * dh_c
                dh_c = dhq[0:1, :]
                tiles[q] = dhq
            dh = jnp.concatenate(tiles, axis=0)
            da = dh * h_prev
            ds = dh * gi * u
            dgi = dh * s * u
            du = dh * s * gi
            dlog_a = da * a - ds * (a * a) / s
            dr = dlog_a * (-LRU_C * sp)
            dsp = dsp + jnp.sum(dlog_a * (-LRU_C * r), axis=0, keepdims=True)
            dpr = dr * r * (1.0 - r)
            dpx = dgi * gi * (1.0 - gi)
            dprb = dpr.astype(bf16)
            dpxb = dpx.astype(bf16)
            dpr_s[sl, :] = dprb
            dpx_s[sl, :] = dpxb
            du = du + (lax.dot_general(dprb, wa_ref[...], _NT, preferred_element_type=f32)
                       + lax.dot_general(dpxb, wx_ref[...], _NT, preferred_element_type=f32))
            dba = dba + jnp.sum(dpr, axis=0, keepdims=True)
            dbx = dbx + jnp.sum(dpx, axis=0, keepdims=True)
            dcb = dcb + jnp.sum(du, axis=0, keepdims=True)
            du_ext = jnp.concatenate([du, du_next], axis=0)
            dlx = cw_ref[3:4, :] * du
            for k in range(3):
                dlx = dlx + cw_ref[k:k + 1, :] * _shift_rows_up(du_ext, 3 - k, LRU_CHUNK)
            dlx_ref[sl, :] = dlx.astype(bf16)
            cur, ext = _conv_window(lx_ref, r0, ci)
            dws = [dw0, dw1, dw2, dw3 + jnp.sum(du * cur, axis=0, keepdims=True)]
            for k in range(3):
                dws[k] = dws[k] + jnp.sum(du * _shift_rows_down(ext, 3 - k), axis=0, keepdims=True)
            return (dh_c, a[0:1, :], du[0:SUB, :], dsp, dba, dbx, dcb, dws[0], dws[1], dws[2], dws[3])

        init = (zero_row, zero_row, jnp.zeros((SUB, LRU_G), f32)) + (zero_row,) * 8
        out = lax.fori_loop(0, n_chunk, chunk, init)
        _, _, _, dsp, dba, dbx, dcb, dw0, dw1, dw2, dw3 = out
        dlam_ref[...] = dsp * (-_sigmoid(-lam_v))
        dba_ref[...] = dba
        dbx_ref[...] = dbx
        dcb_ref[...] = dcb
        dcw_ref[...] = jnp.concatenate([dw0, dw1, dw2, dw3], axis=0)
        ub = u_ref[...].astype(bf16)
        dwa_ref[...] = lax.dot_general(ub, dpr_s[...], _TN, preferred_element_type=f32)
        dwx_ref[...] = lax.dot_general(ub, dpx_s[...], _TN, preferred_element_type=f32)

    seq = lambda cb: pl.BlockSpec((t, LRU_G), lambda c, cb=cb: (0, c + cb))
    rowc = pl.BlockSpec((1, LRU_G), lambda c: (0, c))
    diag = pl.BlockSpec((LRU_G, LRU_G), lambda c: (c, c))
    gate_out = pl.BlockSpec((None, LRU_G, LRU_G), lambda c: (c, 0, 0))
    row_shape = jax.ShapeDtypeStruct((1, LRU_W), f32)
    return _hosted_call(
        host, body, name=name, grid=(LRU_W // LRU_G,),
        in_specs=[seq(LRU_W // LRU_G), seq(0), seq(LRU_W // LRU_G), seq(0), seq(0),
                  pl.BlockSpec((4, LRU_G), lambda c: (0, c)),
                  diag, rowc, diag, rowc, rowc],
        out_specs=[seq(0), seq(0), pl.BlockSpec((4, LRU_G), lambda c: (0, c)), rowc, rowc, rowc, rowc,
                   gate_out, gate_out],
        out_shape=[jax.ShapeDtypeStruct((t, LRU_W), bf16)] * 2
        + [jax.ShapeDtypeStruct((4, LRU_W), f32)] + [row_shape] * 4
        + [jax.ShapeDtypeStruct((LRU_W // LRU_G, LRU_G, LRU_G), f32)] * 2,
        scratch_shapes=[pltpu.VMEM((t, LRU_G), bf16), pltpu.VMEM((t, LRU_G), bf16)],
        compiler_params=_params(("arbitrary",)),
    )(dmix, zl, zl, u_all, h_all, conv_w, wa, ba, wx, bx, lam)


def _pack_rows(a):
    flat = a.reshape(-1)
    rows = -(-flat.shape[0] // LANES)
    return jnp.pad(flat, (0, rows * LANES - flat.shape[0])).reshape(rows, LANES)


W_IN_PAD = 21 * LANES


def _w_in_join(blocks, *, name):
    tm = 256

    def body(b_ref, o_ref):
        o_ref[:, IN_COLS:] = jnp.zeros((tm, W_IN_PAD - IN_COLS), bf16)
        for q in range(N_DEV):
            o_ref[:, IN_SHARD * q:IN_SHARD * (q + 1)] = b_ref[q]

    return pl.pallas_call(
        body, name=name, grid=(D_MODEL // tm,),
        in_specs=[pl.BlockSpec((N_DEV, tm, IN_SHARD), lambda i: (0, i, 0))],
        out_specs=pl.BlockSpec((tm, W_IN_PAD), lambda i: (i, 0)),
        out_shape=jax.ShapeDtypeStruct((D_MODEL, W_IN_PAD), bf16), compiler_params=_params(("arbitrary",)),
    )(blocks)


def _w_in_split(main, fg, *, name):
    tm = 256
    n_main = main.shape[0]

    def body(m_ref, f_ref, o_ref):
        full = jnp.concatenate([m_ref[n] for n in range(n_main)] + [f_ref[...]], axis=1)
        for q in range(N_DEV):
            o_ref[q] = full[:, IN_SHARD * q:IN_SHARD * (q + 1)]

    return pl.pallas_call(
        body, name=name, grid=(D_MODEL // tm,),
        in_specs=[pl.BlockSpec((n_main, tm, 512), lambda i: (0, i, 0)), pl.BlockSpec((tm, LANES), lambda i: (i, 0))],
        out_specs=pl.BlockSpec((N_DEV, tm, IN_SHARD), lambda i: (0, i, 0)),
        out_shape=jax.ShapeDtypeStruct((N_DEV, D_MODEL, IN_SHARD), bf16), compiler_params=_params(("arbitrary",)),
    )(main, fg)


def _block_diag(w):
    eye = jnp.eye(HEADS, dtype=w.dtype)
    return jnp.einsum("hij,hk->hikj", w, eye).reshape(LRU_W, LRU_W)


def _diag_blocks(dw):
    per = dw.shape[1] // HEAD_D
    blocks = [dw[:, HEAD_D * b:HEAD_D * (b + 1), HEAD_D * b:HEAD_D * (b + 1)] for b in range(per)]
    return jnp.stack(blocks, axis=1).reshape(HEADS, HEAD_D, HEAD_D)


def _local_step(x, target, sent, small, *, tm=512, tm_ffn=1024):
    ln1 = (small["ln1_g"], small["ln1_b"])
    ln2 = (small["ln2_g"], small["ln2_b"])
    ln3 = (small["ln3_g"], small["ln3_b"])

    xh1, rs1, hg1, hu1, wg1, wu1, wd1, w_in_g, w_out_g, conv_w_g = _ffn1_fwd_gathering(
        x, (sent["ffn1_w_gate"], sent["ffn1_w_up"], sent["ffn1_w_down"]),
        _Exchange([sent["w_in"], sent["w_out"], sent["conv_w"]], gather=True), tm=tm_ffn, name="ffn1_fwd")
    w_in = _w_in_join(w_in_g, name="w_in_join")
    w_out = w_out_g.reshape(D_MODEL, D_MODEL)
    conv_w = conv_w_g.transpose(1, 0, 2).reshape(4, LRU_W)
    qkv, zl, zfg = _in_proj(xh1, ln1[0], ln1[1], w_in, tm=tm, name="in_proj")
    bfg = jnp.pad(small["b_forget"], ((0, 0), (0, LANES - HEADS)))
    cum = _cum_fwd(zfg, bfg, name="cum_fwd")
    q_aug, k_aug, v_aug = _attn_prep_fwd(qkv, cum, tm=tm, name="attn_prep_fwd")
    o, lse, wg2, wu2 = _attn_fwd2(q_aug, k_aug, v_aug, name="attn_fwd",
                                  host=_Exchange([sent["ffn2_w_gate"], sent["ffn2_w_up"]], gather=True))
    wa_bd = _block_diag(small["rg_wa"]).astype(bf16)
    wx_bd = _block_diag(small["rg_wx"]).astype(bf16)
    ba = small["rg_ba"].reshape(1, LRU_W)
    bx = small["rg_bx"].reshape(1, LRU_W)
    u, h, lru, wd2 = _lru_fwd(zl, conv_w, small["conv_b"], wa_bd, ba, wx_bd, bx, small["lru_lambda"],
                              name="lru_fwd", host=_Exchange([sent["ffn2_w_down"]], gather=True))
    xh2, rs2 = _mmln([(o, 0, FOX_W, w_out, 0, D_MODEL, "nn"), (lru, 0, LRU_W, w_out, 1, D_MODEL, "nn")],
                     tm=tm, name="mix_fwd", resid=("affine", xh1) + ln1, resid_scale=ALPHA, epi="ln_fwd")
    xh3, rs3, hg2, hu2 = _ffn_fwd(xh2, ln2[0], ln2[1], wg2, wu2, wd2, tm=tm_ffn, name="ffn2_fwd")

    dpre3, sq_rows, g_ln3g, g_ln3b = _loss_bwd(xh3, rs3, ln3[0], ln3[1], target, tm=tm, name="loss_bwd")
    dpre2, g_ln2g, g_ln2b, dhg2, dhu2, a2 = _ffn_bwd(dpre3, hg2, hu2, wg2, wu2, wd2,
                                                     (xh2, rs2, ln2[0]), tm=tm_ffn, name="ffn2_bwd")
    wgrad = dict(out_dtype=bf16, tm=D_MODEL, mb=1, tn=FF_TILE, nb=4, tk=512, pair=True)
    wdgrad = dict(out_dtype=bf16, tm=512, mb=4, tn=D_MODEL, nb=1, tk=512, out_scale=0.5, pair=True)
    between_chips = functools.partial(_Exchange, gather=False, chips=True)
    g_wg2 = _mm_tn(xh2, dhg2, name="g_wg2", affine=ln2, **wgrad)
    g_wu2 = _mm_tn(xh2, dhu2, name="g_wu2", affine=ln2, **wgrad)
    g_wd2 = _mm_tn(a2, dpre3, name="g_wd2", **wdgrad)

    dmix = _mmln([(dpre2, 0, D_MODEL, w_out, 0, D_MODEL, "nt")], tm=tm, name="dmix_bwd")
    g_wout_a = _mm(o, dpre2, mode="tn", out_dtype=bf16, tm=512, tn=D_MODEL, tk=512, name="g_wout_fox")
    g_wout_b = _mm(lru, dpre2, mode="tn", out_dtype=bf16, tm=512, tn=D_MODEL, tk=512, name="g_wout_lru")
    g_wout_blocked = jnp.concatenate([g_wout_a, g_wout_b], axis=0).reshape(N_DEV, D_MODEL // N_DEV, D_MODEL)
    dlx, dlg, g_cw, g_cb, g_ba, g_bx, g_lam, g_wa4, g_wx4, p_wg2, p_wout = _lru_bwd(
        dmix, zl, u, h, conv_w, wa_bd, ba, wx_bd, bx, small["lru_lambda"], name="lru_bwd",
        host=_Hosts(between_chips([g_wg2]), _Exchange([g_wout_blocked], gather=False)))
    qb_aug, do_aug = _attn_prep_bwd(qkv, cum, lse, dmix, o, tm=tm, name="attn_prep_bwd")
    dq, dcum_q, dk, dv, dcum_k, p_wu2, p_wd2 = _attn_bwd(qb_aug, k_aug, v_aug, do_aug, name="attn_bwd",
                                                         host=between_chips([g_wu2, g_wd2]))
    dfg, g_bf = _cum_bwd(dcum_q, dcum_k, zfg, bfg, name="cum_bwd")

    dz = [(dq, 0, 512), (dk, 1, 512), (dv, 2, 512), (dlx, 3, 512), (dlg, 4, 512), (dfg, 20, LANES)]
    dpre1, g_ln1g, g_ln1b = _mmln(
        [(arr, 0, w, w_in, cb, w, "nt") for (arr, cb, w) in dz],
        tm=tm, name="dx1_bwd", resid=("plain", dpre2), resid_scale=ALPHA, epi="ln_bwd", ln=(xh1, rs1, ln1[0]))
    g_win_main = _mm_tn(xh1, [arr for arr, _, _ in dz[:5]], out_dtype=bf16, tm=D_MODEL, mb=1, tn=512, nb=5, tk=512,
                        name="g_win", affine=ln1, out_blocked=True)
    g_win_fg = _mm(xh1, dfg, mode="tn", out_dtype=bf16, tm=D_MODEL, tn=LANES, tk=512, name="g_win_fg", affine=ln1)
    g_win_blocked = _w_in_split(g_win_main, g_win_fg, name="w_in_split")
    dhg1, dhu1, a1, p_win = _ffn_bwd_act(dpre1, hg1, hu1, wd1, tm=tm_ffn, name="ffn1_bwd_act",
                                         host=_Exchange([g_win_blocked], gather=False))
    small_g = {
        "ln1_g": g_ln1g, "ln1_b": g_ln1b, "b_forget": g_bf[:, :HEADS], "conv_w": g_cw, "conv_b": g_cb,
        "rg_wa": _diag_blocks(g_wa4), "rg_ba": g_ba.reshape(HEADS, HEAD_D),
        "rg_wx": _diag_blocks(g_wx4), "rg_bx": g_bx.reshape(HEADS, HEAD_D), "lru_lambda": g_lam,
        "ln2_g": g_ln2g, "ln2_b": g_ln2b, "ln3_g": g_ln3g, "ln3_b": g_ln3b,
    }
    small_g["loss"] = (0.5 / D_MODEL) * jnp.sum(sq_rows, keepdims=True)
    pieces = [_pack_rows(small_g[n]) for n in PACKED]
    packed = jnp.concatenate(pieces + [jnp.zeros((PACK_ROWS - sum(p.shape[0] for p in pieces), LANES), f32)])
    g_wg1, all_packed = _mm_tn(x, dhg1, name="g_wg1", host=_Exchange([packed], gather=True), **wgrad)
    g_wu1, p_wg1 = _mm_tn(x, dhu1, name="g_wu1", host=between_chips([g_wg1]), **wgrad)
    g_wd1, p_wu1 = _mm_tn(a1, dpre1, name="g_wd1", host=between_chips([g_wu1]), **wdgrad)
    grad_x, p_wd1 = _ffn_bwd_dx(dpre1, dhg1, dhu1, wg1, wu1, tm=tm_ffn, name="ffn1_bwd_dx",
                                host=between_chips([g_wd1]))
    parts = {
        "ffn1_w_gate": p_wg1, "ffn1_w_up": p_wu1, "ffn1_w_down": p_wd1, "w_in": p_win, "w_out": p_wout,
        "ffn2_w_gate": p_wg2, "ffn2_w_up": p_wu2, "ffn2_w_down": p_wd2,
    }
    return grad_x, parts, all_packed, {n: small_g[n].shape for n in PACKED}


def _adam_math(w, g, m, v):
    m2 = ADAM_B1 * m + (1.0 - ADAM_B1) * g
    v2 = ADAM_B2 * v + (1.0 - ADAM_B2) * (g * g)
    m_hat = m2 / (1.0 - ADAM_B1 ** ADAM_STEP)
    v_hat = v2 / (1.0 - ADAM_B2 ** ADAM_STEP)
    delta = -ADAM_LR * (m_hat / (jnp.sqrt(v_hat) + ADAM_EPS) + ADAM_WD * w)
    return delta, m2, v2


ADAM_TILE_ELEMS = 128 * 1024


def _adamw_big(items, *, name):
    _, r, c = items[0][1].shape
    n_parts = items[0][0].shape[0]
    n_items = len(items)
    assert all(it[1].shape == (1, r, c) and it[0].shape == (n_parts, r, c) for it in items), name
    tr = max(d for d in range(8, r + 1, 8) if r % d == 0 and d * c <= ADAM_TILE_ELEMS)

    def body(*refs):
        ins, outs = refs[:4 * n_items], refs[4 * n_items:]
        for k in range(n_items):
            p_ref, w_ref, m_ref, v_ref = ins[4 * k:4 * k + 4]
            g = p_ref[0].astype(f32)
            for q in range(1, n_parts):
                g = g + p_ref[q].astype(f32)
            d, m2, v2 = _adam_math(w_ref[...], g, m_ref[...], v_ref[...])
            for o_ref, val in zip(outs[4 * k:4 * k + 4], (g, d, m2, v2)):
                o_ref[...] = val

    blk = pl.BlockSpec((None, tr, c), lambda i: (0, i, 0))
    res = pl.pallas_call(
        body, name=name, grid=(r // tr,),
        in_specs=[pl.BlockSpec((n_parts, tr, c), lambda i: (0, i, 0)), blk, blk, blk] * n_items,
        out_specs=[blk] * (4 * n_items), out_shape=[jax.ShapeDtypeStruct((1, r, c), f32)] * (4 * n_items),
        compiler_params=_params(("arbitrary",)),
    )(*[a for it in items for a in it])
    return [res[4 * k:4 * k + 4] for k in range(n_items)]


def _adamw_small(items, *, name):
    n = len(items)

    def body(*refs):
        ins, outs = refs[:4 * n], refs[4 * n:]
        for k in range(n):
            g, w, m, v = (ins[4 * k + q][...] for q in range(4))
            d, m2, v2 = _adam_math(w, g, m, v)
            outs[3 * k][...] = d
            outs[3 * k + 1][...] = m2
            outs[3 * k + 2][...] = v2

    vm = pl.BlockSpec(memory_space=pltpu.VMEM)
    flat = [a for item in items for a in item]
    out_shape = [jax.ShapeDtypeStruct(item[1].shape, f32) for item in items for _ in range(3)]
    return pl.pallas_call(
        body, name=name, in_specs=[vm] * (4 * n), out_specs=[vm] * (3 * n), out_shape=out_shape,
    )(*flat)


def _sum_parts(parts, *, name):
    def body(p_ref, o_ref):
        acc = p_ref[0]
        for q in range(1, N_DEV):
            acc = acc + p_ref[q]
        o_ref[...] = acc

    vm = pl.BlockSpec(memory_space=pltpu.VMEM)
    return pl.pallas_call(
        body, name=name, in_specs=[vm], out_specs=vm, out_shape=jax.ShapeDtypeStruct(parts.shape[1:], f32),
    )(parts)


WEIGHTS = ["ffn1_w_gate", "ffn1_w_up", "ffn1_w_down", "ln1_g", "ln1_b", "w_in", "b_forget", "conv_w", "conv_b",
           "rg_wa", "rg_ba", "rg_wx", "rg_bx", "lru_lambda", "w_out", "ln2_g", "ln2_b",
           "ffn2_w_gate", "ffn2_w_up", "ffn2_w_down", "ln3_g", "ln3_b"]
BIG = ["ffn1_w_gate", "ffn1_w_up", "ffn1_w_down", "w_in", "w_out", "ffn2_w_gate", "ffn2_w_up", "ffn2_w_down"]
PACKED = ["ln1_g", "ln1_b", "ln2_g", "ln2_b", "ln3_g", "ln3_b", "conv_b", "rg_ba", "rg_bx", "lru_lambda",
          "conv_w", "rg_wa", "rg_wx", "b_forget", "loss"]
PACK_ROWS = 600


def _two_d(a):
    return a.reshape((-1, a.shape[-1]))


def _transport(a):
    return _two_d(a)


def kernel(x, ffn1_w_gate, ffn1_w_up, ffn1_w_down, ln1_g, ln1_b, w_in, b_forget, conv_w, conv_b, rg_wa, rg_ba, rg_wx, rg_bx, lru_lambda, w_out, ln2_g, ln2_b, ffn2_w_gate, ffn2_w_up, ffn2_w_down, ln3_g, ln3_b, loss_target, m_ffn1_w_gate, m_ffn1_w_up, m_ffn1_w_down, m_ln1_g, m_ln1_b, m_w_in, m_b_forget, m_conv_w, m_conv_b, m_rg_wa, m_rg_ba, m_rg_wx, m_rg_bx, m_lru_lambda, m_w_out, m_ln2_g, m_ln2_b, m_ffn2_w_gate, m_ffn2_w_up, m_ffn2_w_down, m_ln3_g, m_ln3_b, v_ffn1_w_gate, v_ffn1_w_up, v_ffn1_w_down, v_ln1_g, v_ln1_b, v_w_in, v_b_forget, v_conv_w, v_conv_b, v_rg_wa, v_rg_ba, v_rg_wx, v_rg_bx, v_lru_lambda, v_w_out, v_ln2_g, v_ln2_b, v_ffn2_w_gate, v_ffn2_w_up, v_ffn2_w_down, v_ln3_g, v_ln3_b):
    w_args = (ffn1_w_gate, ffn1_w_up, ffn1_w_down, ln1_g, ln1_b, w_in, b_forget, conv_w, conv_b, rg_wa, rg_ba, rg_wx, rg_bx, lru_lambda, w_out, ln2_g, ln2_b, ffn2_w_gate, ffn2_w_up, ffn2_w_down, ln3_g, ln3_b)
    m_args = (m_ffn1_w_gate, m_ffn1_w_up, m_ffn1_w_down, m_ln1_g, m_ln1_b, m_w_in, m_b_forget, m_conv_w, m_conv_b, m_rg_wa, m_rg_ba, m_rg_wx, m_rg_bx, m_lru_lambda, m_w_out, m_ln2_g, m_ln2_b, m_ffn2_w_gate, m_ffn2_w_up, m_ffn2_w_down, m_ln3_g, m_ln3_b)
    v_args = (v_ffn1_w_gate, v_ffn1_w_up, v_ffn1_w_down, v_ln1_g, v_ln1_b, v_w_in, v_b_forget, v_conv_w, v_conv_b, v_rg_wa, v_rg_ba, v_rg_wx, v_rg_bx, v_lru_lambda, v_w_out, v_ln2_g, v_ln2_b, v_ffn2_w_gate, v_ffn2_w_up, v_ffn2_w_down, v_ln3_g, v_ln3_b)
    w = dict(zip(WEIGHTS, w_args))
    m = dict(zip(WEIGHTS, m_args))
    v = dict(zip(WEIGHTS, v_args))
    me = 4 * lax.axis_index("x") + 2 * lax.axis_index("y") + lax.axis_index("c")

    sent = {n: _transport(w[n]).astype(bf16) for n in BIG}
    sent["conv_w"] = _two_d(w["conv_w"])
    small = {n: w[n] for n in ("ln1_g", "ln1_b", "ln2_g", "ln2_b", "ln3_g", "ln3_b", "b_forget", "conv_b",
                               "lru_lambda")}
    small.update({n: w[n][0] for n in ("rg_wa", "rg_ba", "rg_wx", "rg_bx")})

    grad_x, parts, all_packed, small_shapes = _local_step(x[0], loss_target[0], sent, small)

    total = _sum_parts(all_packed, name="sum_small_grads")
    grads, off = {}, 0
    for n in PACKED:
        size = math.prod(small_shapes[n])
        rows = -(-size // LANES)
        grads[n] = total[off:off + rows].reshape(-1)[:size].reshape(small_shapes[n])
        off += rows
    loss = grads.pop("loss").reshape(())
    grads["conv_w"] = lax.dynamic_slice_in_dim(grads["conv_w"], me * (LRU_W // N_DEV), LRU_W // N_DEV, axis=1)

    delta, new_m, new_v = {}, {}, {}
    for group in (("ffn1_w_gate", "ffn1_w_up", "ffn2_w_gate", "ffn2_w_up"), ("ffn1_w_down", "ffn2_w_down"),
                  ("w_in",), ("w_out",)):
        done = _adamw_big([(parts[n], w[n], m[n], v[n]) for n in group], name="adamw_" + group[0])
        for n, (g, d, m2, v2) in zip(group, done):
            grads[n], delta[n], new_m[n], new_v[n] = g, d, m2, v2
    small_names = [n for n in WEIGHTS if n not in BIG]
    outs = _adamw_small([(_two_d(grads[n]), _two_d(w[n]), _two_d(m[n]), _two_d(v[n])) for n in small_names],
                        name="adamw_small")
    for k, n in enumerate(small_names):
        delta[n], new_m[n], new_v[n] = outs[3 * k], outs[3 * k + 1], outs[3 * k + 2]

    def shaped(d):
        return [d[n].reshape(w[n].shape) for n in WEIGHTS]

    return (loss, grad_x[None], *shaped(grads), *shaped(delta), *shaped(new_m), *shaped(new_v))
```

```python
import functools
import math

import jax
import jax.numpy as jnp
from jax import lax
from jax.experimental import pallas as pl
from jax.experimental.pallas import tpu as pltpu

f32 = jnp.float32
bf16 = jnp.bfloat16

N_DEV = 8
D_MODEL = 1024
D_FF = 4096
FF_TILE = D_FF // N_DEV
FOX_W = 512
LRU_W = 512
HEADS = 8
HEAD_D = 64
IN_COLS = 2568
IN_SHARD = IN_COLS // N_DEV
LANES = 128
LN_EPS = 1e-5
ALPHA = 2.0 ** 0.25
ATT_SCALE = 1.0 / math.sqrt(HEAD_D)
LRU_C = 8.0
NEG_BIG = -1e30

ADAM_LR = 0.001
ADAM_B1 = 0.9
ADAM_B2 = 0.999
ADAM_EPS = 1e-08
ADAM_WD = 0.01
ADAM_STEP = 10

VMEM_LIMIT = 56 * 1024 * 1024
MESH_T = pl.DeviceIdType.MESH


def _params(sem, **kw):
    return pltpu.CompilerParams(dimension_semantics=sem, vmem_limit_bytes=VMEM_LIMIT, **kw)


def _sigmoid(x):
    return 1.0 / (1.0 + jnp.exp(-x))


def _sigmoid_tanh(x):
    return 0.5 * jnp.tanh(0.5 * x) + 0.5


def _softplus(x):
    return jnp.maximum(x, 0.0) + jnp.log(1.0 + jnp.exp(-jnp.abs(x)))


def _one_minus_exp(x):
    series = -x * (1.0 + x * (0.5 + x * (1.0 / 6 + x * (1.0 / 24 + x * (1.0 / 120 + x * (1.0 / 720))))))
    return jnp.where(x > -0.125, series, 1.0 - jnp.exp(x))


_GELU_C = math.sqrt(2.0 / math.pi)


def _gelu_and_grad(x):
    inner = _GELU_C * (x + 0.044715 * x * x * x)
    t = jnp.tanh(inner)
    g = 0.5 * x * (1.0 + t)
    dg = 0.5 * (1.0 + t) + 0.5 * x * (1.0 - t * t) * _GELU_C * (1.0 + 3 * 0.044715 * x * x)
    return g, dg


def _ln_fwd_tile(pre):
    mu = jnp.mean(pre, axis=-1, keepdims=True)
    xc = pre - mu
    var = jnp.mean(xc * xc, axis=-1, keepdims=True)
    rstd = lax.rsqrt(var + LN_EPS)
    return xc * rstd, rstd


def _ln_bwd_tile(dy, xhat, rstd, g):
    dyg = dy * g
    m1 = jnp.mean(dyg, axis=-1, keepdims=True)
    m2 = jnp.mean(dyg * xhat, axis=-1, keepdims=True)
    dpre = rstd * (dyg - m1 - xhat * m2)
    return dpre, jnp.sum(dy * xhat, axis=0, keepdims=True), jnp.sum(dy, axis=0, keepdims=True)


_NT = (((1,), (1,)), ((), ()))
_TN = (((0,), (0,)), ((), ()))


class _Exchange:
    def __init__(self, arrs, gather, chips=False, direct=False):
        self.arrs, self.gather, self.n, self.chips = list(arrs), gather, len(arrs), chips
        self.relays = gather and not direct

    def out_shape(self):
        return [jax.ShapeDtypeStruct(((N_DEV,) + a.shape) if self.gather else a.shape, a.dtype) for a in self.arrs]

    def scratch(self):
        n_remote = self.n * (N_DEV - 1)
        return [pltpu.SemaphoreType.DMA((n_remote,)), pltpu.SemaphoreType.DMA((n_remote,)),
                pltpu.SemaphoreType.DMA((self.n,))]

    def copies(self, ins, outs, sems):
        send_sems, recv_sems, local_sems = sems
        x, y, c = lax.axis_index("x"), lax.axis_index("y"), lax.axis_index("c")
        me = 2 * x + y if self.chips else 4 * x + 2 * y + c
        out = []
        for k in range(self.n):
            for d in (range(2, N_DEV, 2) if self.chips else range(1, N_DEV)):
                px = 1 - x if d & 4 else x
                py = 1 - y if d & 2 else y
                pc = 1 - c if d & 1 else c
                sem = k * (N_DEV - 1) + d - 1
                out.append(pltpu.make_async_remote_copy(
                    src_ref=ins[k] if self.gather else ins[k].at[2 * px + py if self.chips else 4 * px + 2 * py + pc],
                    dst_ref=outs[k].at[me],
                    send_sem=send_sems.at[sem], recv_sem=recv_sems.at[sem],
                    device_id=(px, py, pc), device_id_type=MESH_T))
            out.append(pltpu.make_async_copy(ins[k] if self.gather else ins[k].at[me], outs[k].at[me],
                                             local_sems.at[k]))
        return out

    def gather_copies(self, ins, outs, sems):
        send_sems, recv_sems, local_sems = sems
        x, y, c = lax.axis_index("x"), lax.axis_index("y"), lax.axis_index("c")
        sibling = (x, y, 1 - c)
        chips = [(1 - x, y), (x, 1 - y), (1 - x, 1 - y)]
        out = []
        for k in range(self.n):
            def copy(s, block, to, src=None, k=k):
                rows = outs[k].at[4 * block[0] + 2 * block[1] + block[2]]
                sem = k * (N_DEV - 1) + s
                return pltpu.make_async_remote_copy(
                    src_ref=rows if src is None else src, dst_ref=rows, send_sem=send_sems.at[sem],
                    recv_sem=recv_sems.at[sem], device_id=to, device_id_type=MESH_T)

            first = [copy(0, (x, y, c), sibling, src=ins[k])]
            first += [copy(1 + q, (x, y, c), (*chip, c), src=ins[k]) for q, chip in enumerate(chips)]
            passed = [copy(4 + q, (*chip, c), sibling) for q, chip in enumerate(chips)]
            own = pltpu.make_async_copy(ins[k], outs[k].at[4 * x + 2 * y + c], local_sems.at[k])
            out.append((first, passed, own, copy))
        return out, sibling, chips, (x, y, c)

    def start(self, ins, outs, sems):
        if not self.relays:
            for cp in self.copies(ins, outs, sems):
                cp.start()
            return
        per_array, _, _, _ = self.gather_copies(ins, outs, sems)
        for first, _, own, _ in per_array:
            own.start()
            for cp in first:
                cp.start()

    def relay(self, ins, outs, sems):
        per_array, sibling, chips, (x, y, c) = self.gather_copies(ins, outs, sems)
        for first, passed, own, copy in per_array:
            for q, chip in enumerate(chips):
                copy(1 + q, (*chip, c), (x, y, c)).wait_recv()
                passed[q].start()

    def wait(self, ins, outs, sems, relayed=False):
        if not self.relays:
            for cp in self.copies(ins, outs, sems):
                cp.wait()
            return
        if not relayed:
            self.relay(ins, outs, sems)
        per_array, sibling, chips, (x, y, c) = self.gather_copies(ins, outs, sems)
        for first, passed, own, copy in per_array:
            copy(0, sibling, (x, y, c)).wait_recv()
            for q, chip in enumerate(chips):
                copy(4 + q, (*chip, 1 - c), (x, y, c)).wait_recv()
            for cp in first + passed:
                cp.wait_send()
            own.wait()


class _Hosts:
    def __init__(self, *hosts):
        self.hosts = hosts
        self.relays = any(h.relays for h in hosts)
        self.n = sum(h.n for h in hosts)
        self.arrs = [a for h in hosts for a in h.arrs]

    def out_shape(self):
        return [sh for h in self.hosts for sh in h.out_shape()]

    def scratch(self):
        return [sc for h in self.hosts for sc in h.scratch()]

    def _each(self, ins, outs, sems):
        at = 0
        for k, h in enumerate(self.hosts):
            yield h, ins[at:at + h.n], outs[at:at + h.n], sems[3 * k:3 * k + 3]
            at += h.n

    def start(self, ins, outs, sems):
        for h, h_in, h_out, h_sems in self._each(ins, outs, sems):
            h.start(h_in, h_out, h_sems)

    def relay(self, ins, outs, sems):
        for h, h_in, h_out, h_sems in self._each(ins, outs, sems):
            if h.relays:
                h.relay(h_in, h_out, h_sems)

    def wait(self, ins, outs, sems, relayed=False):
        for h, h_in, h_out, h_sems in self._each(ins, outs, sems):
            h.wait(h_in, h_out, h_sems, relayed=relayed and h.relays)


def _hosted_call(host, body, *, name, grid, in_specs, out_specs, out_shape, scratch_shapes=(), compiler_params):
    out_specs = list(out_specs) if isinstance(out_specs, (list, tuple)) else [out_specs]
    out_shape = list(out_shape) if isinstance(out_shape, (list, tuple)) else [out_shape]
    if host is None:
        return pl.pallas_call(body, name=name, grid=grid, in_specs=in_specs, out_specs=out_specs,
                              out_shape=out_shape, scratch_shapes=list(scratch_shapes),
                              compiler_params=compiler_params)
    n_in, n_out, n_scr, k = len(in_specs), len(out_shape), len(scratch_shapes), host.n

    def wrapped(*refs):
        ins, h_in = refs[:n_in], refs[n_in:n_in + k]
        outs, h_out = refs[n_in + k:n_in + k + n_out], refs[n_in + k + n_out:n_in + 2 * k + n_out]
        scr, sems = refs[n_in + 2 * k + n_out:n_in + 2 * k + n_out + n_scr], refs[n_in + 2 * k + n_out + n_scr:]
        ids = [pl.program_id(a) for a in range(len(grid))]
        first = functools.reduce(jnp.logical_and, [i == 0 for i in ids])
        last = functools.reduce(jnp.logical_and, [i == g - 1 for i, g in zip(ids, grid)])
        steps = math.prod(grid)
        relay_at = (3 * steps) // 4 if host.relays and steps >= 8 else None

        @pl.when(first)
        def _():
            host.start(h_in, h_out, sems)

        if relay_at is not None:
            coords, rest = [], relay_at
            for g in reversed(grid):
                coords.append(rest % g)
                rest //= g

            @pl.when(functools.reduce(jnp.logical_and, [i == cd for i, cd in zip(ids, reversed(coords))]))
            def _():
                host.relay(h_in, h_out, sems)

        body(*ins, *outs, *scr)

        @pl.when(last)
        def _():
            host.wait(h_in, h_out, sems, relayed=relay_at is not None)

    hbm = pl.BlockSpec(memory_space=pl.ANY)
    call = pl.pallas_call(
        wrapped, name=name, grid=grid, in_specs=list(in_specs) + [hbm] * k, out_specs=out_specs + [hbm] * k,
        out_shape=out_shape + host.out_shape(), scratch_shapes=list(scratch_shapes) + host.scratch(),
        compiler_params=compiler_params)
    return lambda *args: call(*args, *host.arrs)


def _ffn_fwd(xhat, g_in, b_in, wg, wu, wd, *, tm, name, host=None):
    t = xhat.shape[0]
    nj = N_DEV

    def body(x_ref, g_ref, b_ref, wg_ref, wu_ref, wd_ref, xo_ref, rstd_ref, hg_ref, hu_ref, xb, acc):
        j = pl.program_id(1)

        @pl.when(j == 0)
        def _():
            xb[...] = (x_ref[...] * g_ref[...] + b_ref[...]).astype(bf16)
            acc[...] = jnp.zeros_like(acc)

        hg = jnp.dot(xb[...], wg_ref[...], preferred_element_type=f32)
        hu = jnp.dot(xb[...], wu_ref[...], preferred_element_type=f32)
        hg_ref[...] = hg.astype(bf16)
        hu_ref[...] = hu.astype(bf16)
        a = hg * _sigmoid_tanh(hg) * hu
        acc[...] += jnp.dot(a.astype(bf16), wd_ref[...], preferred_element_type=f32)

        @pl.when(j == nj - 1)
        def _():
            x = x_ref[...] * g_ref[...] + b_ref[...]
            xo, rstd = _ln_fwd_tile(ALPHA * x + 0.5 * acc[...])
            xo_ref[...] = xo
            rstd_ref[...] = rstd

    row = pl.BlockSpec((1, D_MODEL), lambda i, j: (0, 0))
    return _hosted_call(
        host, body, name=name, grid=(t // tm, nj),
        in_specs=[pl.BlockSpec((tm, D_MODEL), lambda i, j: (i, 0)), row, row,
                  pl.BlockSpec((None, D_MODEL, FF_TILE), lambda i, j: (j, 0, 0)),
                  pl.BlockSpec((None, D_MODEL, FF_TILE), lambda i, j: (j, 0, 0)),
                  pl.BlockSpec((None, FF_TILE, D_MODEL), lambda i, j: (j, 0, 0))],
        out_specs=[pl.BlockSpec((tm, D_MODEL), lambda i, j: (i, 0)),
                   pl.BlockSpec((tm, 1), lambda i, j: (i, 0)),
                   pl.BlockSpec((tm, FF_TILE), lambda i, j: (i, j)),
                   pl.BlockSpec((tm, FF_TILE), lambda i, j: (i, j))],
        out_shape=[jax.ShapeDtypeStruct((t, D_MODEL), f32), jax.ShapeDtypeStruct((t, 1), f32),
                   jax.ShapeDtypeStruct((t, D_FF), bf16), jax.ShapeDtypeStruct((t, D_FF), bf16)],
        scratch_shapes=[pltpu.VMEM((tm, D_MODEL), bf16), pltpu.VMEM((tm, D_MODEL), f32)],
        compiler_params=_params(("arbitrary", "arbitrary")),
    )(xhat, g_in, b_in, wg, wu, wd)


def _ffn1_fwd_gathering(x, own, extra, *, tm, name):
    t = x.shape[0]
    n_i = t // tm
    n_arr = 3
    k_extra = extra.n
    ex = _Exchange(list(own), gather=True)
    ax, ay, ac = lax.axis_index("x"), lax.axis_index("y"), lax.axis_index("c")
    order = jnp.stack([4 * px + 2 * py + pc for px, py in ((ax, ay), (1 - ax, ay), (ax, 1 - ay), (1 - ax, 1 - ay))
                       for pc in (ac, 1 - ac)]).astype(jnp.int32)
    arrival = [None, (0, None), (1, 0), (4, None), (2, 1), (5, None), (3, 2), (6, None)]

    def body(order_ref, x_ref, *refs):
        w_in, e_in = refs[:n_arr], refs[n_arr:n_arr + k_extra]
        refs = refs[n_arr + k_extra:]
        xo_ref, rstd_ref, hg_ref, hu_ref = refs[:4]
        w_all, e_out = refs[4:4 + n_arr], refs[4 + n_arr:4 + n_arr + k_extra]
        acc, wgb, wub, wdb, fetch_sems, send_sems, recv_sems, local_sems = refs[4 + n_arr + k_extra:12 + n_arr + k_extra]
        e_sems = refs[12 + n_arr + k_extra:]
        bufs = (wgb, wub, wdb)
        s = pl.program_id(0)
        i = pl.program_id(1)
        per_array, sibling, chips, (x_, y_, c_) = ex.gather_copies(w_in, w_all, (send_sems, recv_sems, local_sems))

        def fetch(pos, slot):
            return [pltpu.make_async_copy(w_in[a] if pos == 0 else w_all[a].at[order_ref[pos]],
                                          bufs[a].at[slot], fetch_sems.at[n_arr * slot + a]) for a in range(n_arr)]

        def source_of(pos):
            chip = (x_, y_) if pos < 2 else chips[(pos - 2) // 2]
            return (*chip, c_ if pos % 2 == 0 else 1 - c_)

        @pl.when(jnp.logical_and(s == 0, i == 0))
        def _():
            for q in range(4):
                for first, _, own_copy, _ in per_array:
                    if q == 0:
                        own_copy.start()
                    first[q].start()
            for cp in fetch(0, 0):
                cp.start()
            for cp in fetch(0, 0):
                cp.wait()

        @pl.when(jnp.logical_and(s == N_DEV // 2, i == 0))
        def _():
            extra.start(e_in, e_out, e_sems)

        for pos in range(1, N_DEV):
            @pl.when(jnp.logical_and(s == pos - 1, i == n_i - 1))
            def _(pos=pos):
                sem, passes = arrival[pos]
                for _, passed, _, copy in per_array:
                    copy(sem, source_of(pos), (x_, y_, c_)).wait_recv()
                    if passes is not None:
                        passed[passes].start()
                for cp in fetch(pos, pos % 2):
                    cp.start()

            @pl.when(jnp.logical_and(s == pos, i == 0))
            def _(pos=pos):
                for cp in fetch(pos, pos % 2):
                    cp.wait()

        slot = s % 2
        xb = x_ref[...].astype(bf16)
        hg = jnp.dot(xb, wgb[slot], preferred_element_type=f32)
        hu = jnp.dot(xb, wub[slot], preferred_element_type=f32)
        hg_ref[...] = hg.astype(bf16)
        hu_ref[...] = hu.astype(bf16)
        a = hg * _sigmoid_tanh(hg) * hu
        part = jnp.dot(a.astype(bf16), wdb[slot], preferred_element_type=f32)

        @pl.when(s == 0)
        def _():
            acc[i] = part

        @pl.when(s > 0)
        def _():
            acc[i] += part

        @pl.when(s == N_DEV - 1)
        def _():
            xo, rstd = _ln_fwd_tile(ALPHA * x_ref[...] + 0.5 * acc[i])
            xo_ref[...] = xo
            rstd_ref[...] = rstd

        @pl.when(jnp.logical_and(s == N_DEV - 1, i == n_i - 1))
        def _():
            for first, passed, own_copy, _ in per_array:
                for cp in first + passed:
                    cp.wait_send()
                own_copy.wait()
            extra.wait(e_in, e_out, e_sems)

    hbm = pl.BlockSpec(memory_space=pl.ANY)
    last = N_DEV - 1
    tok_out = pl.BlockSpec((tm, D_MODEL), lambda s, i, o: (jnp.where(s == last, i, 0), 0))
    col_out = pl.BlockSpec((tm, 1), lambda s, i, o: (jnp.where(s == last, i, 0), 0))
    hid = pl.BlockSpec((tm, FF_TILE), lambda s, i, o: (i, o[s]))
    shard_shapes = [(N_DEV,) + w.shape for w in own]
    grid_spec = pltpu.PrefetchScalarGridSpec(
        num_scalar_prefetch=1, grid=(N_DEV, n_i),
        in_specs=[pl.BlockSpec((tm, D_MODEL), lambda s, i, o: (i, 0))] + [hbm] * (n_arr + k_extra),
        out_specs=[tok_out, col_out, hid, hid] + [hbm] * (n_arr + k_extra),
        scratch_shapes=[pltpu.VMEM((n_i, tm, D_MODEL), f32)]
        + [pltpu.VMEM((2,) + w.shape, bf16) for w in own]
        + [pltpu.SemaphoreType.DMA((2 * n_arr,))] + ex.scratch() + extra.scratch())
    res = pl.pallas_call(
        body, name=name, grid_spec=grid_spec,
        out_shape=[jax.ShapeDtypeStruct((t, D_MODEL), f32), jax.ShapeDtypeStruct((t, 1), f32),
                   jax.ShapeDtypeStruct((t, D_FF), bf16), jax.ShapeDtypeStruct((t, D_FF), bf16)]
        + [jax.ShapeDtypeStruct(sh, bf16) for sh in shard_shapes] + extra.out_shape(),
        compiler_params=_params(("arbitrary", "arbitrary")),
    )(order, x, *own, *extra.arrs)
    return res


def _ffn_bwd(dpre, hg, hu, wg, wu, wd, ln_in, *, tm, name, host=None):
    t = dpre.shape[0]
    nj = N_DEV
    with_ln = ln_in is not None

    def body(*refs):
        if with_ln:
            (dp_ref, hg_ref, hu_ref, wg_ref, wu_ref, wd_ref, xh_ref, rs_ref, g_ref,
             dx_ref, gg_ref, gb_ref, dhg_ref, dhu_ref, a_ref, dfb, acc) = refs
        else:
            (dp_ref, hg_ref, hu_ref, wg_ref, wu_ref, wd_ref,
             dx_ref, dhg_ref, dhu_ref, a_ref, dfb, acc) = refs
        i = pl.program_id(0)
        j = pl.program_id(1)

        @pl.when(j == 0)
        def _():
            dfb[...] = (0.5 * dp_ref[...]).astype(bf16)
            acc[...] = jnp.zeros_like(acc)

        da = lax.dot_general(dfb[...], wd_ref[...], _NT, preferred_element_type=f32)
        hgv = hg_ref[...].astype(f32)
        huv = hu_ref[...].astype(f32)
        sg = _sigmoid_tanh(hgv)
        silu = hgv * sg
        a_ref[...] = (silu * huv).astype(bf16)
        dhu = (da * silu).astype(bf16)
        dhg = (da * huv * (sg * (1.0 + hgv * (1.0 - sg)))).astype(bf16)
        dhg_ref[...] = dhg
        dhu_ref[...] = dhu
        acc[...] += (lax.dot_general(dhg, wg_ref[...], _NT, preferred_element_type=f32)
                     + lax.dot_general(dhu, wu_ref[...], _NT, preferred_element_type=f32))

        @pl.when(j == nj - 1)
        def _():
            dx = ALPHA * dp_ref[...] + acc[...]
            if with_ln:
                dprev, gg, gb = _ln_bwd_tile(dx, xh_ref[...], rs_ref[...], g_ref[...])
                dx_ref[...] = dprev

                @pl.when(i == 0)
                def _():
                    gg_ref[...] = gg
                    gb_ref[...] = gb

                @pl.when(i > 0)
                def _():
                    gg_ref[...] += gg
                    gb_ref[...] += gb
            else:
                dx_ref[...] = dx

    tok = pl.BlockSpec((tm, D_MODEL), lambda i, j: (i, 0), pipeline_mode=pl.Buffered(1))
    row = pl.BlockSpec((1, D_MODEL), lambda i, j: (0, 0))
    hid = pl.BlockSpec((tm, FF_TILE), lambda i, j: (i, j))
    in_specs = [tok, hid, hid,
                pl.BlockSpec((None, D_MODEL, FF_TILE), lambda i, j: (j, 0, 0)),
                pl.BlockSpec((None, D_MODEL, FF_TILE), lambda i, j: (j, 0, 0)),
                pl.BlockSpec((None, FF_TILE, D_MODEL), lambda i, j: (j, 0, 0))]
    args = [dpre, hg, hu, wg, wu, wd]
    out_specs = [tok]
    out_shape = [jax.ShapeDtypeStruct((t, D_MODEL), f32)]
    if with_ln:
        in_specs += [tok, pl.BlockSpec((tm, 1), lambda i, j: (i, 0)), row]
        args += list(ln_in)
        out_specs += [row, row]
        out_shape += [jax.ShapeDtypeStruct((1, D_MODEL), f32)] * 2
    out_specs += [hid, hid, hid]
    out_shape += [jax.ShapeDtypeStruct((t, D_FF), bf16)] * 3
    return _hosted_call(
        host, body, name=name, grid=(t // tm, nj), in_specs=in_specs, out_specs=out_specs, out_shape=out_shape,
        scratch_shapes=[pltpu.VMEM((tm, D_MODEL), bf16), pltpu.VMEM((tm, D_MODEL), f32)],
        compiler_params=_params(("arbitrary", "arbitrary")),
    )(*args)


def _ffn_bwd_act(dpre, hg, hu, wd, *, tm, name, host=None):
    t = dpre.shape[0]

    def body(dp_ref, hg_ref, hu_ref, wd_ref, dhg_ref, dhu_ref, a_ref, dfb):
        @pl.when(pl.program_id(1) == 0)
        def _():
            dfb[...] = (0.5 * dp_ref[...]).astype(bf16)

        da = lax.dot_general(dfb[...], wd_ref[...], _NT, preferred_element_type=f32)
        hgv = hg_ref[...].astype(f32)
        huv = hu_ref[...].astype(f32)
        sg = _sigmoid_tanh(hgv)
        silu = hgv * sg
        a_ref[...] = (silu * huv).astype(bf16)
        dhu_ref[...] = (da * silu).astype(bf16)
        dhg_ref[...] = (da * huv * (sg * (1.0 + hgv * (1.0 - sg)))).astype(bf16)

    hid = pl.BlockSpec((tm, FF_TILE), lambda i, j: (i, j))
    return _hosted_call(
        host, body, name=name, grid=(t // tm, N_DEV),
        in_specs=[pl.BlockSpec((tm, D_MODEL), lambda i, j: (i, 0)), hid, hid,
                  pl.BlockSpec((None, FF_TILE, D_MODEL), lambda i, j: (j, 0, 0))],
        out_specs=[hid, hid, hid], out_shape=[jax.ShapeDtypeStruct((t, D_FF), bf16)] * 3,
        scratch_shapes=[pltpu.VMEM((tm, D_MODEL), bf16)],
        compiler_params=_params(("arbitrary", "arbitrary")),
    )(dpre, hg, hu, wd)


def _ffn_bwd_dx(dpre, dhg, dhu, wg, wu, *, tm, name, host=None):
    t = dpre.shape[0]
    nj = N_DEV

    def body(dp_ref, dhg_ref, dhu_ref, wg_ref, wu_ref, dx_ref, acc):
        j = pl.program_id(1)

        @pl.when(j == 0)
        def _():
            acc[...] = jnp.zeros_like(acc)

        acc[...] += (lax.dot_general(dhg_ref[...], wg_ref[...], _NT, preferred_element_type=f32)
                     + lax.dot_general(dhu_ref[...], wu_ref[...], _NT, preferred_element_type=f32))

        @pl.when(j == nj - 1)
        def _():
            dx_ref[...] = ALPHA * dp_ref[...] + acc[...]

    tok = pl.BlockSpec((tm, D_MODEL), lambda i, j: (i, 0))
    hid = pl.BlockSpec((tm, FF_TILE), lambda i, j: (i, j))
    wspec = pl.BlockSpec((None, D_MODEL, FF_TILE), lambda i, j: (j, 0, 0))
    return _hosted_call(
        host, body, name=name, grid=(t // tm, nj), in_specs=[tok, hid, hid, wspec, wspec],
        out_specs=[tok], out_shape=[jax.ShapeDtypeStruct((t, D_MODEL), f32)],
        scratch_shapes=[pltpu.VMEM((tm, D_MODEL), f32)],
        compiler_params=_params(("arbitrary", "arbitrary")),
    )(dpre, dhg, dhu, wg, wu)


def _mm(a, b, *, mode, out_dtype, tm, tn, tk, name, affine=None, a_cols=None, b_cols=None,
        b_blocked=False, out_blocked=False, out_scale=None):
    if mode == "nn":
        m_full, k_full = a.shape
        m_dim, k_dim = (m_full, a_cols[1]) if a_cols else (m_full, k_full)
    else:
        k_dim, m_full = a.shape
        m_dim = a_cols[1] if a_cols else m_full
    a_off = a_cols[0] if a_cols else 0
    if b_blocked:
        n_dim = b.shape[0] * b.shape[2]
        assert b.shape[2] == tn
    else:
        n_dim = b_cols[1] if b_cols else b.shape[1]
    b_off = b_cols[0] if b_cols else 0
    assert m_dim % tm == 0 and n_dim % tn == 0 and k_dim % tk == 0, (name, m_dim, n_dim, k_dim)
    nk = k_dim // tk

    def body(*refs):
        if affine is not None:
            a_ref, g_ref, s_ref, b_ref, o_ref, acc = refs
        else:
            a_ref, b_ref, o_ref, acc = refs
        k = pl.program_id(2)

        @pl.when(k == 0)
        def _():
            acc[...] = jnp.zeros_like(acc)

        av = a_ref[...]
        if affine is not None:
            av = av * g_ref[...] + s_ref[...]
        av = av.astype(bf16)
        bv = b_ref[...].astype(bf16)
        if mode == "nn":
            acc[...] += jnp.dot(av, bv, preferred_element_type=f32)
        else:
            acc[...] += lax.dot_general(av, bv, _TN, preferred_element_type=f32)

        @pl.when(k == nk - 1)
        def _():
            res = acc[...] if out_scale is None else acc[...] * out_scale
            o_ref[...] = res.astype(out_dtype)

    if mode == "nn":
        a_spec = pl.BlockSpec((tm, tk), lambda i, j, k: (i, k + a_off))
        aff_spec = pl.BlockSpec((1, tk), lambda i, j, k: (0, k + a_off))
    else:
        a_spec = pl.BlockSpec((tk, tm), lambda i, j, k: (k, i + a_off))
        aff_spec = pl.BlockSpec((1, tm), lambda i, j, k: (0, i + a_off))
    if b_blocked:
        b_spec = pl.BlockSpec((None, tk, tn), lambda i, j, k: (j, k, 0))
    else:
        b_spec = pl.BlockSpec((tk, tn), lambda i, j, k: (k, j + b_off))
    if out_blocked:
        o_spec = pl.BlockSpec((None, tm, tn), lambda i, j, k: (j, i, 0))
        o_shape = jax.ShapeDtypeStruct((n_dim // tn, m_dim, tn), out_dtype)
    else:
        o_spec = pl.BlockSpec((tm, tn), lambda i, j, k: (i, j))
        o_shape = jax.ShapeDtypeStruct((m_dim, n_dim), out_dtype)
    in_specs = [a_spec] + ([aff_spec, aff_spec] if affine is not None else []) + [b_spec]
    args = [a] + (list(affine) if affine is not None else []) + [b]
    return pl.pallas_call(
        body, name=name, grid=(m_dim // tm, n_dim // tn, nk), in_specs=in_specs, out_specs=o_spec,
        out_shape=o_shape, scratch_shapes=[pltpu.VMEM((tm, tn), f32)],
        compiler_params=_params(("arbitrary", "arbitrary", "arbitrary")),
    )(*args)


def _mm_tn(a, b, *, out_dtype, tm, mb, tn, nb, tk, name, affine=None, out_blocked=False, out_scale=None,
           pair=False, host=None):
    k_dim, m_dim = a.shape
    multi_b = isinstance(b, (list, tuple))
    b_list = list(b) if multi_b else [b]
    n_dim = nb * tn if multi_b else b.shape[1]
    assert m_dim % (mb * tm) == 0 and n_dim % (nb * tn) == 0 and k_dim % tk == 0, (name, m_dim, n_dim, k_dim)
    nk = k_dim // tk
    grid = (m_dim // (mb * tm), n_dim // (nb * tn), nk)
    if pair:
        assert mb * nb == 4 and grid[0] * grid[1] == 2 and out_dtype == bf16, name

    def body(*refs):
        if pair:
            refs, (acc, send_buf, recv_buf, keep, send_sems, recv_sems) = refs[:-6], refs[-6:]
        else:
            refs, acc = refs[:-1], refs[-1]
        a_ref, o_ref = refs[0], refs[-1]
        if affine is not None:
            g_ref, s_ref = refs[1:3]
        b_refs = refs[3 if affine is not None else 1:-1]
        k = pl.program_id(2)

        @pl.when(k == 0)
        def _():
            acc[...] = jnp.zeros_like(acc)

        av = a_ref[...]
        if affine is not None:
            av = av * g_ref[...] + s_ref[...]
        av = av.astype(bf16)
        if multi_b:
            pieces = [r[...].astype(bf16) for r in b_refs]
        else:
            bv = b_refs[0][...].astype(bf16)
            pieces = [bv[:, jn * tn:(jn + 1) * tn] for jn in range(nb)]
        for im in range(mb):
            a_t = av[:, im * tm:(im + 1) * tm].T
            for jn in range(nb):
                acc[im * nb + jn] += jnp.dot(a_t, pieces[jn], preferred_element_type=f32)

        def scaled(v):
            return v if out_scale is None else v * out_scale

        @pl.when(k == nk - 1)
        def _():
            if pair:
                x, y, c = lax.axis_index("x"), lax.axis_index("y"), lax.axis_index("c")
                window = pl.program_id(0) + pl.program_id(1)

                def swap(w, cc):
                    return pltpu.make_async_remote_copy(
                        src_ref=send_buf.at[w, cc], dst_ref=recv_buf.at[w, cc],
                        send_sem=send_sems.at[2 * w + cc], recv_sem=recv_sems.at[2 * w + cc],
                        device_id=(x, y, 1 - c), device_id_type=MESH_T)

                for w in range(2):
                    @pl.when(window == w)
                    def _(w=w):
                        for cc in range(2):
                            send_buf[w, cc] = scaled(acc[2 * cc + 1 - c]).astype(bf16)
                            swap(w, cc).start()
                            if w == 0:
                                keep[cc] = scaled(acc[2 * cc + c])

                @pl.when(window == 1)
                def _():
                    for w in range(2):
                        for cc in range(2):
                            swap(w, cc).wait_recv()
                            mine = keep[cc] if w == 0 else scaled(acc[2 * cc + c])
                            o_ref[2 * w + cc] = (mine + recv_buf[w, cc].astype(f32)).astype(bf16)
                    for w in range(2):
                        for cc in range(2):
                            swap(w, cc).wait_send()
                return
            for im in range(mb):
                for jn in range(nb):
                    res = scaled(acc[im * nb + jn])
                    if out_blocked:
                        o_ref[jn, im * tm:(im + 1) * tm, :] = res.astype(out_dtype)
                    else:
                        o_ref[im * tm:(im + 1) * tm, jn * tn:(jn + 1) * tn] = res.astype(out_dtype)

    a_spec = pl.BlockSpec((tk, mb * tm), lambda i, j, k: (k, i))
    aff_spec = pl.BlockSpec((1, mb * tm), lambda i, j, k: (0, i))
    if multi_b:
        b_specs = [pl.BlockSpec((tk, tn), lambda i, j, k: (k, 0))] * nb
    else:
        b_specs = [pl.BlockSpec((tk, nb * tn), lambda i, j, k: (k, j))]
    scratch = [pltpu.VMEM((mb * nb, tm, tn), f32)]
    if pair:
        o_spec = pl.BlockSpec((4, tm, tn), lambda i, j, k: (0, 0, 0))
        o_shape = jax.ShapeDtypeStruct((4, tm, tn), out_dtype)
        scratch += [pltpu.VMEM((2, 2, tm, tn), bf16), pltpu.VMEM((2, 2, tm, tn), bf16), pltpu.VMEM((2, tm, tn), f32),
                    pltpu.SemaphoreType.DMA((4,)), pltpu.SemaphoreType.DMA((4,))]
    elif out_blocked:
        o_spec = pl.BlockSpec((nb, mb * tm, tn), lambda i, j, k: (j, i, 0))
        o_shape = jax.ShapeDtypeStruct((n_dim // tn, m_dim, tn), out_dtype)
    else:
        o_spec = pl.BlockSpec((mb * tm, nb * tn), lambda i, j, k: (i, j))
        o_shape = jax.ShapeDtypeStruct((m_dim, n_dim), out_dtype)
    in_specs = [a_spec] + ([aff_spec, aff_spec] if affine is not None else []) + b_specs
    args = [a] + (list(affine) if affine is not None else []) + b_list
    res = _hosted_call(
        host, body, name=name, grid=grid, in_specs=in_specs, out_specs=o_spec, out_shape=o_shape,
        scratch_shapes=scratch, compiler_params=_params(("arbitrary", "arbitrary", "arbitrary")),
    )(*args)
    return res[0] if host is None else res


def _in_proj(xhat, g, b, w_in, *, tm, name):
    t = xhat.shape[0]
    n_qkv, n_l = 3 * FOX_W, 2 * LRU_W

    def body(x_ref, g_ref, b_ref, w_ref, qkv_ref, zl_ref, zfg_ref):
        xb = (x_ref[...] * g_ref[...] + b_ref[...]).astype(bf16)
        qkv_ref[...] = jnp.dot(xb, w_ref[:, :n_qkv], preferred_element_type=f32).astype(bf16)
        zl_ref[...] = jnp.dot(xb, w_ref[:, n_qkv:n_qkv + n_l], preferred_element_type=f32)
        zfg_ref[...] = jnp.dot(xb, w_ref[:, n_qkv + n_l:], preferred_element_type=f32)

    row = pl.BlockSpec((1, D_MODEL), lambda i: (0, 0))
    return pl.pallas_call(
        body, name=name, grid=(t // tm,),
        in_specs=[pl.BlockSpec((tm, D_MODEL), lambda i: (i, 0)), row, row,
                  pl.BlockSpec(w_in.shape, lambda i: (0, 0))],
        out_specs=[pl.BlockSpec((tm, n_qkv), lambda i: (i, 0)), pl.BlockSpec((tm, n_l), lambda i: (i, 0)),
                   pl.BlockSpec((tm, LANES), lambda i: (i, 0))],
        out_shape=[jax.ShapeDtypeStruct((t, n_qkv), bf16), jax.ShapeDtypeStruct((t, n_l), f32),
                   jax.ShapeDtypeStruct((t, LANES), f32)],
        compiler_params=_params(("arbitrary",)),
    )(xhat, g, b, w_in)


def _mmln(pairs, *, tm, name, resid=None, resid_scale=1.0, epi=None, ln=None, n_out=D_MODEL):
    t = pairs[0][0].shape[0]
    n_pairs = len(pairs)
    n_resid = 0 if resid is None else len(resid) - 1

    def body(*refs):
        pos = 0
        val = None
        for p in range(n_pairs):
            a_ref, b_ref = refs[pos], refs[pos + 1]
            pos += 2
            av = a_ref[...].astype(bf16)
            bv = b_ref[...].astype(bf16)
            if pairs[p][6] == "nn":
                term = jnp.dot(av, bv, preferred_element_type=f32)
            else:
                term = lax.dot_general(av, bv, _NT, preferred_element_type=f32)
            val = term if val is None else val + term
        if resid is not None:
            if resid[0] == "plain":
                r = refs[pos][...]
            else:
                r = refs[pos][...] * refs[pos + 1][...] + refs[pos + 2][...]
            pos += n_resid
            val = val + resid_scale * r
        if epi is None:
            o_ref = refs[pos]
            o_ref[...] = val.astype(o_ref.dtype)
        elif epi == "ln_fwd":
            xo, rstd = _ln_fwd_tile(val)
            refs[pos][...] = xo
            refs[pos + 1][...] = rstd
        else:
            xh_ref, rs_ref, g_ref, dx_ref, gg_ref, gb_ref = refs[pos:pos + 6]
            dprev, gg, gb = _ln_bwd_tile(val, xh_ref[...], rs_ref[...], g_ref[...])
            dx_ref[...] = dprev
            i = pl.program_id(0)

            @pl.when(i == 0)
            def _():
                gg_ref[...] = gg
                gb_ref[...] = gb

            @pl.when(i > 0)
            def _():
                gg_ref[...] += gg
                gb_ref[...] += gb

    in_specs, args = [], []
    for (a, acb, aw, b, bcb, bw, mode) in pairs:
        in_specs.append(pl.BlockSpec((tm, aw), lambda i, acb=acb: (i, acb)))
        args.append(a)
        if mode == "nn":
            in_specs.append(pl.BlockSpec((aw, n_out), lambda i, bcb=bcb: (bcb, 0)))
        else:
            in_specs.append(pl.BlockSpec((n_out, bw), lambda i, bcb=bcb: (0, bcb)))
        args.append(b)
    tok = pl.BlockSpec((tm, n_out), lambda i: (i, 0))
    row = pl.BlockSpec((1, n_out), lambda i: (0, 0))
    col = pl.BlockSpec((tm, 1), lambda i: (i, 0))
    if resid is not None:
        in_specs += [tok] if resid[0] == "plain" else [tok, row, row]
        args += list(resid[1:])
    if epi is None:
        out_specs, out_shape = tok, jax.ShapeDtypeStruct((t, n_out), f32)
    elif epi == "ln_fwd":
        out_specs = [tok, col]
        out_shape = [jax.ShapeDtypeStruct((t, n_out), f32), jax.ShapeDtypeStruct((t, 1), f32)]
    else:
        in_specs += [tok, col, row]
        args += list(ln)
        out_specs = [tok, row, row]
        out_shape = [jax.ShapeDtypeStruct((t, n_out), f32)] + [jax.ShapeDtypeStruct((1, n_out), f32)] * 2
    return pl.pallas_call(
        body, name=name, grid=(t // tm,), in_specs=in_specs, out_specs=out_specs, out_shape=out_shape,
        compiler_params=_params(("arbitrary",)),
    )(*args)


def _loss_bwd(xhat, rstd, g, b, target, *, tm, name):
    t = xhat.shape[0]

    def body(xh_ref, rs_ref, g_ref, b_ref, tg_ref, dx_ref, sq_ref, gg_ref, gb_ref):
        i = pl.program_id(0)
        xh = xh_ref[...]
        diff = xh * g_ref[...] + b_ref[...] - tg_ref[...]
        sq = jnp.sum(diff * diff, axis=0, keepdims=True)
        dprev, gg, gb = _ln_bwd_tile(diff * (1.0 / D_MODEL), xh, rs_ref[...], g_ref[...])
        dx_ref[...] = dprev

        @pl.when(i == 0)
        def _():
            sq_ref[...] = sq
            gg_ref[...] = gg
            gb_ref[...] = gb

        @pl.when(i > 0)
        def _():
            sq_ref[...] += sq
            gg_ref[...] += gg
            gb_ref[...] += gb

    tok = pl.BlockSpec((tm, D_MODEL), lambda i: (i, 0))
    row = pl.BlockSpec((1, D_MODEL), lambda i: (0, 0))
    return pl.pallas_call(
        body, name=name, grid=(t // tm,),
        in_specs=[tok, pl.BlockSpec((tm, 1), lambda i: (i, 0)), row, row, tok],
        out_specs=[tok, row, row, row],
        out_shape=[jax.ShapeDtypeStruct((t, D_MODEL), f32)] + [jax.ShapeDtypeStruct((1, D_MODEL), f32)] * 3,
        compiler_params=_params(("arbitrary",)),
    )(xhat, rstd, g, b, target)


CUM_TILE = 256


def _tri(n, lower):
    r = lax.broadcasted_iota(jnp.int32, (n, n), 0)
    c = lax.broadcasted_iota(jnp.int32, (n, n), 1)
    return jnp.where((r >= c) if lower else (r <= c), 1.0, 0.0).astype(f32)


def _cum_fwd(zfg, bfg, *, name):
    t = zfg.shape[0]

    def body(z_ref, b_ref, o_ref, carry):
        @pl.when(pl.program_id(0) == 0)
        def _():
            carry[...] = jnp.zeros_like(carry)

        ls = -_softplus(-(z_ref[...] + b_ref[...]))
        c = jnp.dot(_tri(CUM_TILE, True), ls, preferred_element_type=f32,
                    precision=lax.Precision.HIGHEST) + carry[...]
        o_ref[...] = c
        carry[...] = c[CUM_TILE - 1:CUM_TILE, :]

    blk = pl.BlockSpec((CUM_TILE, LANES), lambda i: (i, 0))
    return pl.pallas_call(
        body, name=name, grid=(t // CUM_TILE,),
        in_specs=[blk, pl.BlockSpec((1, LANES), lambda i: (0, 0))], out_specs=blk,
        out_shape=jax.ShapeDtypeStruct((t, LANES), f32), scratch_shapes=[pltpu.VMEM((1, LANES), f32)],
        compiler_params=_params(("arbitrary",)),
    )(zfg, bfg)


def _cum_bwd(dcum_q, dcum_k, zfg, bfg, *, name):
    t = zfg.shape[0]
    n = t // CUM_TILE

    def body(d_ref, d2_ref, z_ref, b_ref, o_ref, s_ref, carry):
        i = pl.program_id(0)

        @pl.when(i == 0)
        def _():
            carry[...] = jnp.zeros_like(carry)

        dls = jnp.dot(_tri(CUM_TILE, False), d_ref[...] + d2_ref[...], preferred_element_type=f32,
                      precision=lax.Precision.HIGHEST) + carry[...]
        carry[...] = dls[0:1, :]
        lane = lax.broadcasted_iota(jnp.int32, (CUM_TILE, LANES), 1)
        dfg = jnp.where(lane < HEADS, dls * _sigmoid(-(z_ref[...] + b_ref[...])), 0.0)
        o_ref[...] = dfg
        tot = jnp.sum(dfg, axis=0, keepdims=True)

        @pl.when(i == 0)
        def _():
            s_ref[...] = tot

        @pl.when(i > 0)
        def _():
            s_ref[...] += tot

    blk = pl.BlockSpec((CUM_TILE, LANES), lambda i: (n - 1 - i, 0))
    row = pl.BlockSpec((1, LANES), lambda i: (0, 0))
    return pl.pallas_call(
        body, name=name, grid=(n,), in_specs=[blk, blk, blk, row], out_specs=[blk, row],
        out_shape=[jax.ShapeDtypeStruct((t, LANES), f32), jax.ShapeDtypeStruct((1, LANES), f32)],
        scratch_shapes=[pltpu.VMEM((1, LANES), f32)],
        compiler_params=_params(("arbitrary",)),
    )(dcum_q, dcum_k, zfg, bfg)


ATT_TILE = 512


def _causal(i, j, transposed):
    r = lax.broadcasted_iota(jnp.int32, (ATT_TILE, ATT_TILE), 0)
    c = lax.broadcasted_iota(jnp.int32, (ATT_TILE, ATT_TILE), 1)
    if transposed:
        return (c + i * ATT_TILE) >= (r + j * ATT_TILE)
    return (r + i * ATT_TILE) >= (c + j * ATT_TILE)


ATT_W = HEADS * LANES


def _data_lane(h):
    return HEAD_D * (h % 2)


def _extra_lane(h):
    return HEAD_D - _data_lane(h)


def _split3(x):
    hi = x.astype(bf16)
    rest = x - hi.astype(f32)
    mid = rest.astype(bf16)
    lo = (rest - mid.astype(f32)).astype(bf16)
    return hi, mid, lo


def _three_pieces(x):
    hi, mid, lo = (p.astype(f32) for p in _split3(x))
    return (hi + pltpu.roll(mid, HEADS, axis=1) + pltpu.roll(lo, 2 * HEADS, axis=1)).astype(bf16)


def _move(h, first):
    r = lax.broadcasted_iota(jnp.int32, (LANES, LANES), 0)
    c = lax.broadcasted_iota(jnp.int32, (LANES, LANES), 1)
    hit = functools.reduce(jnp.logical_or, [jnp.logical_and(r == HEADS * q + h, c == first + q) for q in range(3)])
    return jnp.where(hit, 1.0, 0.0).astype(bf16)


def _ones_from(first, rows):
    lane = lax.broadcasted_iota(jnp.int32, (rows, LANES), 1)
    return jnp.where(jnp.logical_and(lane >= first, lane < first + 3), 1.0, 0.0)


def _own_lanes(h, rows):
    lane = lax.broadcasted_iota(jnp.int32, (rows, LANES), 1)
    return (lane < HEAD_D) if h % 2 == 0 else (lane >= HEAD_D)


def _head_values(x):
    lane = lax.broadcasted_iota(jnp.int32, x.shape, 1)
    return jnp.where(lane < HEADS, x, 0.0)


def _attn_prep_fwd(qkv, cum, *, tm, name):
    t = qkv.shape[0]

    def body(q_ref, k_ref, v_ref, c_ref, qa_ref, ka_ref, va_ref):
        c3 = _three_pieces(_head_values(c_ref[...]))
        ones = jnp.ones((tm, LANES), bf16)
        for h in range(HEADS):
            pair = slice(LANES * (h // 2), LANES * (h // 2 + 1))
            hs = slice(LANES * h, LANES * (h + 1))
            base, own = _extra_lane(h), _own_lanes(h, tm)
            eq = jnp.dot(c3, _move(h, base), preferred_element_type=f32) + _ones_from(base + 3, tm)
            ek = _ones_from(base, tm) - jnp.dot(c3, _move(h, base + 3), preferred_element_type=f32)
            qa_ref[:, hs] = jnp.where(own, q_ref[:, pair] * ATT_SCALE, eq.astype(bf16))
            ka_ref[:, hs] = jnp.where(own, k_ref[:, pair], ek.astype(bf16))
            va_ref[:, hs] = jnp.where(own, v_ref[:, pair], ones)

    wide = pl.BlockSpec((tm, ATT_W), lambda i: (i, 0))
    out = jax.ShapeDtypeStruct((t, ATT_W), bf16)
    return pl.pallas_call(
        body, name=name, grid=(t // tm,),
        in_specs=[pl.BlockSpec((tm, FOX_W), lambda i: (i, 0)), pl.BlockSpec((tm, FOX_W), lambda i: (i, 1)),
                  pl.BlockSpec((tm, FOX_W), lambda i: (i, 2)), pl.BlockSpec((tm, LANES), lambda i: (i, 0))],
        out_specs=[wide] * 3, out_shape=[out] * 3, compiler_params=_params(("arbitrary",)),
    )(qkv, qkv, qkv, cum)


def _attn_prep_bwd(qkv, cum, lse, dmix, o, *, tm, name):
    t = qkv.shape[0]

    def body(q_ref, c_ref, l_ref, do_ref, o_ref, qa_ref, da_ref):
        b3 = _three_pieces(_head_values(c_ref[...] - l_ref[...]))
        r = lax.broadcasted_iota(jnp.int32, (FOX_W, LANES), 0)
        c = lax.broadcasted_iota(jnp.int32, (FOX_W, LANES), 1)
        per_head = jnp.where(r // HEAD_D == c, 1.0, 0.0).astype(bf16)
        delta = sum(jnp.dot(p, per_head, preferred_element_type=f32) for p in _split3(do_ref[...] * o_ref[...]))
        d3 = _three_pieces(delta)
        for h in range(HEADS):
            pair = slice(LANES * (h // 2), LANES * (h // 2 + 1))
            hs = slice(LANES * h, LANES * (h + 1))
            base, own = _extra_lane(h), _own_lanes(h, tm)
            eq = jnp.dot(b3, _move(h, base), preferred_element_type=f32) + _ones_from(base + 3, tm)
            ed = -jnp.dot(d3, _move(h, base), preferred_element_type=f32)
            qa_ref[:, hs] = jnp.where(own, q_ref[:, pair] * ATT_SCALE, eq.astype(bf16))
            da_ref[:, hs] = jnp.where(own, do_ref[:, pair].astype(bf16), ed.astype(bf16))

    wide = pl.BlockSpec((tm, ATT_W), lambda i: (i, 0))
    half = pl.BlockSpec((tm, FOX_W), lambda i: (i, 0))
    col = pl.BlockSpec((tm, LANES), lambda i: (i, 0))
    out = jax.ShapeDtypeStruct((t, ATT_W), bf16)
    return pl.pallas_call(
        body, name=name, grid=(t // tm,), in_specs=[half, col, col, half, half],
        out_specs=[wide] * 2, out_shape=[out] * 2, compiler_params=_params(("arbitrary",)),
    )(qkv, cum, lse, dmix, o)


def _attn_fwd2(q_aug, k_aug, v_aug, *, name, host=None):
    t = q_aug.shape[0]
    n = t // ATT_TILE
    tq = ATT_TILE

    def body(q_ref, k_ref, v_ref, o_ref, lse_ref, acc, m_s):
        i = pl.program_id(0)
        j = pl.program_id(1)

        @pl.when(j == 0)
        def _():
            acc[...] = jnp.zeros_like(acc)
            m_s[...] = jnp.full_like(m_s, NEG_BIG)

        def block(masked):
            mask = _causal(i, j, False) if masked else None
            for h in range(HEADS):
                hs = slice(LANES * h, LANES * (h + 1))
                s = lax.dot_general(q_ref[:, hs], k_ref[:, hs], _NT, preferred_element_type=f32)
                if masked:
                    s = jnp.where(mask, s, NEG_BIG)
                blocks = [s[:, LANES * b:LANES * (b + 1)] for b in range(tq // LANES)]
                m_old = m_s[h]
                m_new = jnp.maximum(m_old, jnp.broadcast_to(
                    jnp.max(functools.reduce(jnp.maximum, blocks), axis=-1, keepdims=True), (tq, LANES)))
                p = jnp.concatenate([jnp.exp(b - m_new) for b in blocks], axis=1).astype(bf16)
                acc[h] = jnp.exp(m_old - m_new) * acc[h] + jnp.dot(p, v_ref[:, hs], preferred_element_type=f32)
                m_s[h] = m_new

        @pl.when(j < i)
        def _():
            block(False)

        @pl.when(j == i)
        def _():
            block(True)
            lse_ref[...] = jnp.zeros_like(lse_ref)
            for h in range(HEADS):
                a = acc[h]
                l = a[:, _extra_lane(h):_extra_lane(h) + 1]
                o_ref[:, HEAD_D * h:HEAD_D * (h + 1)] = a[:, _data_lane(h):_data_lane(h) + HEAD_D] / l
                lse_ref[:, h:h + 1] = m_s[h][:, 0:1] + jnp.log(l)

    kv = pl.BlockSpec((tq, ATT_W), lambda i, j: (jnp.minimum(i, j), 0))
    return _hosted_call(
        host, body, name=name, grid=(n, n),
        in_specs=[pl.BlockSpec((tq, ATT_W), lambda i, j: (i, 0)), kv, kv],
        out_specs=[pl.BlockSpec((tq, FOX_W), lambda i, j: (i, 0)), pl.BlockSpec((tq, LANES), lambda i, j: (i, 0))],
        out_shape=[jax.ShapeDtypeStruct((t, FOX_W), f32), jax.ShapeDtypeStruct((t, LANES), f32)],
        scratch_shapes=[pltpu.VMEM((HEADS, tq, LANES), f32), pltpu.VMEM((HEADS, tq, LANES), f32)],
        compiler_params=_params(("arbitrary", "arbitrary")),
    )(q_aug, k_aug, v_aug)


def _attn_bwd(qb_aug, k_aug, v_aug, do_aug, *, name, host=None):
    t = qb_aug.shape[0]
    n = t // ATT_TILE
    tk = ATT_TILE

    def body(q_ref, k_ref, v_ref, do_ref, dq_ref, dcq_ref, dk_ref, dv_ref, dck_ref, dk_acc, dv_acc, dq_all):
        j = pl.program_id(0)
        i = pl.program_id(1)

        @pl.when(jnp.logical_and(i == 0, j == 0))
        def _():
            dq_all[...] = jnp.zeros_like(dq_all)

        @pl.when(i == 0)
        def _():
            dk_acc[...] = jnp.zeros_like(dk_acc)
            dv_acc[...] = jnp.zeros_like(dv_acc)

        def block(masked):
            mask = _causal(i, j, True) if masked else None
            for h in range(HEADS):
                hs = slice(LANES * h, LANES * (h + 1))
                qh = q_ref[:, hs]
                doh = do_ref[:, hs]
                kh = k_ref[:, hs]
                s_t = lax.dot_general(kh, qh, _NT, preferred_element_type=f32)
                if masked:
                    s_t = jnp.where(mask, s_t, NEG_BIG)
                p_t = jnp.exp(s_t)
                dv_acc[h] += jnp.dot(p_t.astype(bf16), doh, preferred_element_type=f32)
                dp_t = lax.dot_general(v_ref[:, hs], doh, _NT, preferred_element_type=f32)
                ds_t = (p_t * dp_t).astype(bf16)
                dk_acc[h] += jnp.dot(ds_t, qh, preferred_element_type=f32)
                dq_all[i, h] += lax.dot_general(ds_t, kh, _TN, preferred_element_type=f32)

        @pl.when(i > j)
        def _():
            block(False)

        @pl.when(i == j)
        def _():
            block(True)
            dcq_ref[...] = jnp.zeros_like(dcq_ref)
            for h in range(HEADS):
                a = dq_all[j, h]
                dq_ref[:, HEAD_D * h:HEAD_D * (h + 1)] = (
                    a[:, _data_lane(h):_data_lane(h) + HEAD_D] * ATT_SCALE).astype(bf16)
                dcq_ref[:, h:h + 1] = a[:, _extra_lane(h):_extra_lane(h) + 1]

        @pl.when(i == n - 1)
        def _():
            dck_ref[...] = jnp.zeros_like(dck_ref)
            for h in range(HEADS):
                a = dk_acc[h]
                cols = slice(_data_lane(h), _data_lane(h) + HEAD_D)
                dk_ref[:, HEAD_D * h:HEAD_D * (h + 1)] = a[:, cols].astype(bf16)
                dv_ref[:, HEAD_D * h:HEAD_D * (h + 1)] = dv_acc[h][:, cols].astype(bf16)
                dck_ref[:, h:h + 1] = -a[:, _extra_lane(h) + 3:_extra_lane(h) + 4]

    own = pl.BlockSpec((tk, ATT_W), lambda j, i: (j, 0))
    qs = pl.BlockSpec((tk, ATT_W), lambda j, i: (jnp.maximum(i, j), 0))
    half = pl.BlockSpec((tk, FOX_W), lambda j, i: (j, 0))
    col = pl.BlockSpec((tk, LANES), lambda j, i: (j, 0))
    return _hosted_call(
        host, body, name=name, grid=(n, n), in_specs=[qs, own, own, qs],
        out_specs=[half, col, half, half, col],
        out_shape=[jax.ShapeDtypeStruct((t, FOX_W), bf16), jax.ShapeDtypeStruct((t, LANES), f32),
                   jax.ShapeDtypeStruct((t, FOX_W), bf16), jax.ShapeDtypeStruct((t, FOX_W), bf16),
                   jax.ShapeDtypeStruct((t, LANES), f32)],
        scratch_shapes=[pltpu.VMEM((HEADS, tk, LANES), f32), pltpu.VMEM((HEADS, tk, LANES), f32),
                        pltpu.VMEM((n, HEADS, tk, LANES), f32)],
        compiler_params=_params(("arbitrary", "arbitrary")),
    )(qb_aug, k_aug, v_aug, do_aug)


LRU_CHUNK = 64
LRU_G = 256
SUB = 8


def _row_ids(n):
    return lax.broadcasted_iota(jnp.int32, (n, LRU_G), 0)


def _shift_rows_down(ext, s):
    return pltpu.roll(ext, s, axis=0)[SUB:, :]


def _shift_rows_up(ext, s, n):
    return pltpu.roll(ext, ext.shape[0] - s, axis=0)[:n, :]


def _lru_gates(u, wa_ref, ba_ref, wx_ref, bx_ref, sp):
    ub = u.astype(bf16)
    r = _sigmoid(jnp.dot(ub, wa_ref[...], preferred_element_type=f32) + ba_ref[...])
    gi = _sigmoid(jnp.dot(ub, wx_ref[...], preferred_element_type=f32) + bx_ref[...])
    log_a = -LRU_C * r * sp
    a = jnp.exp(log_a)
    s = jnp.sqrt(_one_minus_exp(2.0 * log_a))
    return r, gi, a, s


def _conv_window(lx_ref, r0, ci):
    cur = lx_ref[pl.ds(r0, LRU_CHUNK), :]
    p0 = pl.multiple_of(jnp.maximum(r0 - SUB, 0), SUB)
    prev = jnp.where(ci > 0, lx_ref[pl.ds(p0, SUB), :], 0.0)
    return cur, jnp.concatenate([prev, cur], axis=0)


def _lru_fwd(zl, conv_w, conv_b, wa, ba, wx, bx, lam, *, name, host=None):
    t = zl.shape[0]
    n_chunk = t // LRU_CHUNK

    def body(lx_ref, lg_ref, cw_ref, cb_ref, wa_ref, ba_ref, wx_ref, bx_ref, lam_ref, u_ref, h_ref, y_ref):
        sp = _softplus(-lam_ref[...])
        rows = _row_ids(SUB)

        def chunk(ci, hc):
            r0 = pl.multiple_of(ci * LRU_CHUNK, LRU_CHUNK)
            cur, ext = _conv_window(lx_ref, r0, ci)
            u = cb_ref[...] + cw_ref[3:4, :] * cur
            for k in range(3):
                u = u + cw_ref[k:k + 1, :] * _shift_rows_down(ext, 3 - k)
            r, gi, a, s = _lru_gates(u, wa_ref, ba_ref, wx_ref, bx_ref, sp)
            b = s * (gi * u)
            tiles = []
            for q in range(LRU_CHUNK // SUB):
                ta = a[SUB * q:SUB * (q + 1), :]
                tb = b[SUB * q:SUB * (q + 1), :]
                for d in (1, 2, 4):
                    a_sh = jnp.where(rows >= d, pltpu.roll(ta, d, axis=0), 1.0)
                    b_sh = jnp.where(rows >= d, pltpu.roll(tb, d, axis=0), 0.0)
                    tb = ta * b_sh + tb
                    ta = ta * a_sh
                hq = tb + ta * hc
                hc = hq[SUB - 1:SUB, :]
                tiles.append(hq)
            h = jnp.concatenate(tiles, axis=0)
            u_ref[pl.ds(r0, LRU_CHUNK), :] = u
            h_ref[pl.ds(r0, LRU_CHUNK), :] = h
            gel, _ = _gelu_and_grad(lg_ref[pl.ds(r0, LRU_CHUNK), :])
            y_ref[pl.ds(r0, LRU_CHUNK), :] = gel * h
            return hc

        lax.fori_loop(0, n_chunk, chunk, jnp.zeros((1, LRU_G), f32))

    seq = lambda cb: pl.BlockSpec((t, LRU_G), lambda c, cb=cb: (0, c + cb))
    rowc = pl.BlockSpec((1, LRU_G), lambda c: (0, c))
    diag = pl.BlockSpec((LRU_G, LRU_G), lambda c: (c, c))
    out = jax.ShapeDtypeStruct((t, LRU_W), f32)
    return _hosted_call(
        host, body, name=name, grid=(LRU_W // LRU_G,),
        in_specs=[seq(0), seq(LRU_W // LRU_G), pl.BlockSpec((4, LRU_G), lambda c: (0, c)),
                  rowc, diag, rowc, diag, rowc, rowc],
        out_specs=[seq(0)] * 3, out_shape=[out] * 3,
        compiler_params=_params(("arbitrary",)),
    )(zl, zl, conv_w, conv_b, wa, ba, wx, bx, lam)


def _lru_bwd(dmix, zl, u_all, h_all, conv_w, wa, ba, wx, bx, lam, *, name, host=None):
    t = zl.shape[0]
    n_chunk = t // LRU_CHUNK

    def body(dy_ref, lx_ref, lg_ref, u_ref, h_ref, cw_ref, wa_ref, ba_ref, wx_ref, bx_ref, lam_ref,
             dlx_ref, dlg_ref, dcw_ref, dcb_ref, dba_ref, dbx_ref, dlam_ref, dwa_ref, dwx_ref, dpr_s, dpx_s):
        lam_v = lam_ref[...]
        sp = _softplus(-lam_v)
        rows = _row_ids(SUB)
        rows_c = _row_ids(LRU_CHUNK)
        zero_row = jnp.zeros((1, LRU_G), f32)

        def chunk(step, carry):
            dh_c, a_next0, du_next, dsp, dba, dbx, dcb, dw0, dw1, dw2, dw3 = carry
            ci = n_chunk - 1 - step
            r0 = pl.multiple_of(ci * LRU_CHUNK, LRU_CHUNK)
            sl = pl.ds(r0, LRU_CHUNK)
            u = u_ref[sl, :]
            r, gi, a, s = _lru_gates(u, wa_ref, ba_ref, wx_ref, bx_ref, sp)
            h = h_ref[sl, :]
            p0 = pl.multiple_of(jnp.maximum(r0 - SUB, 0), SUB)
            h_before = jnp.where(ci > 0, h_ref[pl.ds(p0, SUB), :], 0.0)[SUB - 1:SUB, :]
            h_prev = jnp.where(rows_c == 0, h_before, pltpu.roll(h, 1, axis=0))
            gel, dgel = _gelu_and_grad(lg_ref[sl, :])
            dy = dy_ref[sl, :]
            dlg_ref[sl, :] = (dy * h * dgel).astype(bf16)
            g_in = dy * gel
            a_next = jnp.where(rows_c == LRU_CHUNK - 1, a_next0, pltpu.roll(a, LRU_CHUNK - 1, axis=0))
            tiles = [None] * (LRU_CHUNK // SUB)
            for q in reversed(range(LRU_CHUNK // SUB)):
                ta = a_next[SUB * q:SUB * (q + 1), :]
                tb = g_in[SUB * q:SUB * (q + 1), :]
                for d in (1, 2, 4):
                    a_sh = jnp.where(rows < SUB - d, pltpu.roll(ta, SUB - d, axis=0), 1.0)
                    b_sh = jnp.where(rows < SUB - d, pltpu.roll(tb, SUB - d, axis=0), 0.0)
                    tb = ta * b_sh + tb
                    ta = ta * a_sh
                dhq = tb + ta * dh_c
                dh_c = dhq[0:1, :]
                tiles[q] = dhq
            dh = jnp.concatenate(tiles, axis=0)
            da = dh * h_prev
            ds = dh * gi * u
            dgi = dh * s * u
            du = dh * s * gi
            dlog_a = da * a - ds * (a * a) / s
            dr = dlog_a * (-LRU_C * sp)
            dsp = dsp + jnp.sum(dlog_a * (-LRU_C * r), axis=0, keepdims=True)
            dpr = dr * r * (1.0 - r)
            dpx = dgi * gi * (1.0 - gi)
            dprb = dpr.astype(bf16)
            dpxb = dpx.astype(bf16)
            dpr_s[sl, :] = dprb
            dpx_s[sl, :] = dpxb
            du = du + (lax.dot_general(dprb, wa_ref[...], _NT, preferred_element_type=f32)
                       + lax.dot_general(dpxb, wx_ref[...], _NT, preferred_element_type=f32))
            dba = dba + jnp.sum(dpr, axis=0, keepdims=True)
            dbx = dbx + jnp.sum(dpx, axis=0, keepdims=True)
            dcb = dcb + jnp.sum(du, axis=0, keepdims=True)
            du_ext = jnp.concatenate([du, du_next], axis=0)
            dlx = cw_ref[3:4, :] * du
            for k in range(3):
                dlx = dlx + cw_ref[k:k + 1, :] * _shift_rows_up(du_ext, 3 - k, LRU_CHUNK)
            dlx_ref[sl, :] = dlx.astype(bf16)
            cur, ext = _conv_window(lx_ref, r0, ci)
            dws = [dw0, dw1, dw2, dw3 + jnp.sum(du * cur, axis=0, keepdims=True)]
            for k in range(3):
                dws[k] = dws[k] + jnp.sum(du * _shift_rows_down(ext, 3 - k), axis=0, keepdims=True)
            return (dh_c, a[0:1, :], du[0:SUB, :], dsp, dba, dbx, dcb, dws[0], dws[1], dws[2], dws[3])

        init = (zero_row, zero_row, jnp.zeros((SUB, LRU_G), f32)) + (zero_row,) * 8
        out = lax.fori_loop(0, n_chunk, chunk, init)
        _, _, _, dsp, dba, dbx, dcb, dw0, dw1, dw2, dw3 = out
        dlam_ref[...] = dsp * (-_sigmoid(-lam_v))
        dba_ref[...] = dba
        dbx_ref[...] = dbx
        dcb_ref[...] = dcb
        dcw_ref[...] = jnp.concatenate([dw0, dw1, dw2, dw3], axis=0)
        ub = u_ref[...].astype(bf16)
        dwa_ref[...] = lax.dot_general(ub, dpr_s[...], _TN, preferred_element_type=f32)
        dwx_ref[...] = lax.dot_general(ub, dpx_s[...], _TN, preferred_element_type=f32)

    seq = lambda cb: pl.BlockSpec((t, LRU_G), lambda c, cb=cb: (0, c + cb))
    rowc = pl.BlockSpec((1, LRU_G), lambda c: (0, c))
    diag = pl.BlockSpec((LRU_G, LRU_G), lambda c: (c, c))
    gate_out = pl.BlockSpec((None, LRU_G, LRU_G), lambda c: (c, 0, 0))
    row_shape = jax.ShapeDtypeStruct((1, LRU_W), f32)
    return _hosted_call(
        host, body, name=name, grid=(LRU_W // LRU_G,),
        in_specs=[seq(LRU_W // LRU_G), seq(0), seq(LRU_W // LRU_G), seq(0), seq(0),
                  pl.BlockSpec((4, LRU_G), lambda c: (0, c)),
                  diag, rowc, diag, rowc, rowc],
        out_specs=[seq(0), seq(0), pl.BlockSpec((4, LRU_G), lambda c: (0, c)), rowc, rowc, rowc, rowc,
                   gate_out, gate_out],
        out_shape=[jax.ShapeDtypeStruct((t, LRU_W), bf16)] * 2
        + [jax.ShapeDtypeStruct((4, LRU_W), f32)] + [row_shape] * 4
        + [jax.ShapeDtypeStruct((LRU_W // LRU_G, LRU_G, LRU_G), f32)] * 2,
        scratch_shapes=[pltpu.VMEM((t, LRU_G), bf16), pltpu.VMEM((t, LRU_G), bf16)],
        compiler_params=_params(("arbitrary",)),
    )(dmix, zl, zl, u_all, h_all, conv_w, wa, ba, wx, bx, lam)


def _pack_rows(a):
    flat = a.reshape(-1)
    rows = -(-flat.shape[0] // LANES)
    return jnp.pad(flat, (0, rows * LANES - flat.shape[0])).reshape(rows, LANES)


W_IN_PAD = 21 * LANES


def _w_in_join(blocks, *, name):
    tm = 256

    def body(b_ref, o_ref):
        o_ref[:, IN_COLS:] = jnp.zeros((tm, W_IN_PAD - IN_COLS), bf16)
        for q in range(N_DEV):
            o_ref[:, IN_SHARD * q:IN_SHARD * (q + 1)] = b_ref[q]

    return pl.pallas_call(
        body, name=name, grid=(D_MODEL // tm,),
        in_specs=[pl.BlockSpec((N_DEV, tm, IN_SHARD), lambda i: (0, i, 0))],
        out_specs=pl.BlockSpec((tm, W_IN_PAD), lambda i: (i, 0)),
        out_shape=jax.ShapeDtypeStruct((D_MODEL, W_IN_PAD), bf16), compiler_params=_params(("arbitrary",)),
    )(blocks)


def _w_in_split(main, fg, *, name):
    tm = 256
    n_main = main.shape[0]

    def body(m_ref, f_ref, o_ref):
        full = jnp.concatenate([m_ref[n] for n in range(n_main)] + [f_ref[...]], axis=1)
        for q in range(N_DEV):
            o_ref[q] = full[:, IN_SHARD * q:IN_SHARD * (q + 1)]

    return pl.pallas_call(
        body, name=name, grid=(D_MODEL // tm,),
        in_specs=[pl.BlockSpec((n_main, tm, 512), lambda i: (0, i, 0)), pl.BlockSpec((tm, LANES), lambda i: (i, 0))],
        out_specs=pl.BlockSpec((N_DEV, tm, IN_SHARD), lambda i: (0, i, 0)),
        out_shape=jax.ShapeDtypeStruct((N_DEV, D_MODEL, IN_SHARD), bf16), compiler_params=_params(("arbitrary",)),
    )(main, fg)


def _block_diag(w):
    eye = jnp.eye(HEADS, dtype=w.dtype)
    return jnp.einsum("hij,hk->hikj", w, eye).reshape(LRU_W, LRU_W)


def _diag_blocks(dw):
    per = dw.shape[1] // HEAD_D
    blocks = [dw[:, HEAD_D * b:HEAD_D * (b + 1), HEAD_D * b:HEAD_D * (b + 1)] for b in range(per)]
    return jnp.stack(blocks, axis=1).reshape(HEADS, HEAD_D, HEAD_D)


def _local_step(x, target, sent, small, *, tm=512, tm_ffn=1024):
    ln1 = (small["ln1_g"], small["ln1_b"])
    ln2 = (small["ln2_g"], small["ln2_b"])
    ln3 = (small["ln3_g"], small["ln3_b"])

    xh1, rs1, hg1, hu1, wg1, wu1, wd1, w_in_g, w_out_g, conv_w_g = _ffn1_fwd_gathering(
        x, (sent["ffn1_w_gate"], sent["ffn1_w_up"], sent["ffn1_w_down"]),
        _Exchange([sent["w_in"], sent["w_out"], sent["conv_w"]], gather=True), tm=tm_ffn, name="ffn1_fwd")
    w_in = _w_in_join(w_in_g, name="w_in_join")
    w_out = w_out_g.reshape(D_MODEL, D_MODEL)
    conv_w = conv_w_g.transpose(1, 0, 2).reshape(4, LRU_W)
    qkv, zl, zfg = _in_proj(xh1, ln1[0], ln1[1], w_in, tm=tm, name="in_proj")
    bfg = jnp.pad(small["b_forget"], ((0, 0), (0, LANES - HEADS)))
    cum = _cum_fwd(zfg, bfg, name="cum_fwd")
    q_aug, k_aug, v_aug = _attn_prep_fwd(qkv, cum, tm=tm, name="attn_prep_fwd")
    o, lse, wg2, wu2 = _attn_fwd2(
        q_aug, k_aug, v_aug, name="attn_fwd",
        host=_Hosts(_Exchange([sent["ffn2_w_gate"]], gather=True),
                    _Exchange([sent["ffn2_w_up"]], gather=True, direct=True)))
    wa_bd = _block_diag(small["rg_wa"]).astype(bf16)
    wx_bd = _block_diag(small["rg_wx"]).astype(bf16)
    ba = small["rg_ba"].reshape(1, LRU_W)
    bx = small["rg_bx"].reshape(1, LRU_W)
    u, h, lru, wd2 = _lru_fwd(zl, conv_w, small["conv_b"], wa_bd, ba, wx_bd, bx, small["lru_lambda"],
                              name="lru_fwd", host=_Exchange([sent["ffn2_w_down"]], gather=True))
    xh2, rs2 = _mmln([(o, 0, FOX_W, w_out, 0, D_MODEL, "nn"), (lru, 0, LRU_W, w_out, 1, D_MODEL, "nn")],
                     tm=tm, name="mix_fwd", resid=("affine", xh1) + ln1, resid_scale=ALPHA, epi="ln_fwd")
    xh3, rs3, hg2, hu2 = _ffn_fwd(xh2, ln2[0], ln2[1], wg2, wu2, wd2, tm=tm_ffn, name="ffn2_fwd")

    dpre3, sq_rows, g_ln3g, g_ln3b = _loss_bwd(xh3, rs3, ln3[0], ln3[1], target, tm=tm, name="loss_bwd")
    dpre2, g_ln2g, g_ln2b, dhg2, dhu2, a2 = _ffn_bwd(dpre3, hg2, hu2, wg2, wu2, wd2,
                                                     (xh2, rs2, ln2[0]), tm=tm_ffn, name="ffn2_bwd")
    wgrad = dict(out_dtype=bf16, tm=D_MODEL, mb=1, tn=FF_TILE, nb=4, tk=512, pair=True)
    wdgrad = dict(out_dtype=bf16, tm=512, mb=4, tn=D_MODEL, nb=1, tk=512, out_scale=0.5, pair=True)
    between_chips = functools.partial(_Exchange, gather=False, chips=True)
    g_wg2 = _mm_tn(xh2, dhg2, name="g_wg2", affine=ln2, **wgrad)
    g_wu2 = _mm_tn(xh2, dhu2, name="g_wu2", affine=ln2, **wgrad)
    g_wd2 = _mm_tn(a2, dpre3, name="g_wd2", **wdgrad)

    dmix = _mmln([(dpre2, 0, D_MODEL, w_out, 0, D_MODEL, "nt")], tm=tm, name="dmix_bwd")
    g_wout_a = _mm(o, dpre2, mode="tn", out_dtype=bf16, tm=512, tn=D_MODEL, tk=512, name="g_wout_fox")
    g_wout_b = _mm(lru, dpre2, mode="tn", out_dtype=bf16, tm=512, tn=D_MODEL, tk=512, name="g_wout_lru")
    g_wout_blocked = jnp.concatenate([g_wout_a, g_wout_b], axis=0).reshape(N_DEV, D_MODEL // N_DEV, D_MODEL)
    dlx, dlg, g_cw, g_cb, g_ba, g_bx, g_lam, g_wa4, g_wx4, p_wg2, p_wout = _lru_bwd(
        dmix, zl, u, h, conv_w, wa_bd, ba, wx_bd, bx, small["lru_lambda"], name="lru_bwd",
        host=_Hosts(between_chips([g_wg2]), _Exchange([g_wout_blocked], gather=False)))
    qb_aug, do_aug = _attn_prep_bwd(qkv, cum, lse, dmix, o, tm=tm, name="attn_prep_bwd")
    dq, dcum_q, dk, dv, dcum_k, p_wu2, p_wd2 = _attn_bwd(qb_aug, k_aug, v_aug, do_aug, name="attn_bwd",
                                                         host=between_chips([g_wu2, g_wd2]))
    dfg, g_bf = _cum_bwd(dcum_q, dcum_k, zfg, bfg, name="cum_bwd")

    dz = [(dq, 0, 512), (dk, 1, 512), (dv, 2, 512), (dlx, 3, 512), (dlg, 4, 512), (dfg, 20, LANES)]
    dpre1, g_ln1g, g_ln1b = _mmln(
        [(arr, 0, w, w_in, cb, w, "nt") for (arr, cb, w) in dz],
        tm=tm, name="dx1_bwd", resid=("plain", dpre2), resid_scale=ALPHA, epi="ln_bwd", ln=(xh1, rs1, ln1[0]))
    g_win_main = _mm_tn(xh1, [arr for arr, _, _ in dz[:5]], out_dtype=bf16, tm=D_MODEL, mb=1, tn=512, nb=5, tk=512,
                        name="g_win", affine=ln1, out_blocked=True)
    g_win_fg = _mm(xh1, dfg, mode="tn", out_dtype=bf16, tm=D_MODEL, tn=LANES, tk=512, name="g_win_fg", affine=ln1)
    g_win_blocked = _w_in_split(g_win_main, g_win_fg, name="w_in_split")
    dhg1, dhu1, a1, p_win = _ffn_bwd_act(dpre1, hg1, hu1, wd1, tm=tm_ffn, name="ffn1_bwd_act",
                                         host=_Exchange([g_win_blocked], gather=False))
    small_g = {
        "ln1_g": g_ln1g, "ln1_b": g_ln1b, "b_forget": g_bf[:, :HEADS], "conv_w": g_cw, "conv_b": g_cb,
        "rg_wa": _diag_blocks(g_wa4), "rg_ba": g_ba.reshape(HEADS, HEAD_D),
        "rg_wx": _diag_blocks(g_wx4), "rg_bx": g_bx.reshape(HEADS, HEAD_D), "lru_lambda": g_lam,
        "ln2_g": g_ln2g, "ln2_b": g_ln2b, "ln3_g": g_ln3g, "ln3_b": g_ln3b,
    }
    small_g["loss"] = (0.5 / D_MODEL) * jnp.sum(sq_rows, keepdims=True)
    pieces = [_pack_rows(small_g[n]) for n in PACKED]
    packed = jnp.concatenate(pieces + [jnp.zeros((PACK_ROWS - sum(p.shape[0] for p in pieces), LANES), f32)])
    g_wg1, all_packed = _mm_tn(x, dhg1, name="g_wg1", host=_Exchange([packed], gather=True), **wgrad)
    g_wu1, p_wg1 = _mm_tn(x, dhu1, name="g_wu1", host=between_chips([g_wg1]), **wgrad)
    g_wd1, p_wu1 = _mm_tn(a1, dpre1, name="g_wd1", host=between_chips([g_wu1]), **wdgrad)
    grad_x, p_wd1 = _ffn_bwd_dx(dpre1, dhg1, dhu1, wg1, wu1, tm=tm_ffn, name="ffn1_bwd_dx",
                                host=between_chips([g_wd1]))
    parts = {
        "ffn1_w_gate": p_wg1, "ffn1_w_up": p_wu1, "ffn1_w_down": p_wd1, "w_in": p_win, "w_out": p_wout,
        "ffn2_w_gate": p_wg2, "ffn2_w_up": p_wu2, "ffn2_w_down": p_wd2,
    }
    return grad_x, parts, all_packed, {n: small_g[n].shape for n in PACKED}


def _adam_math(w, g, m, v):
    m2 = ADAM_B1 * m + (1.0 - ADAM_B1) * g
    v2 = ADAM_B2 * v + (1.0 - ADAM_B2) * (g * g)
    m_hat = m2 / (1.0 - ADAM_B1 ** ADAM_STEP)
    v_hat = v2 / (1.0 - ADAM_B2 ** ADAM_STEP)
    delta = -ADAM_LR * (m_hat / (jnp.sqrt(v_hat) + ADAM_EPS) + ADAM_WD * w)
    return delta, m2, v2


ADAM_TILE_ELEMS = 128 * 1024


def _adamw_big(items, *, name):
    _, r, c = items[0][1].shape
    n_parts = items[0][0].shape[0]
    n_items = len(items)
    assert all(it[1].shape == (1, r, c) and it[0].shape == (n_parts, r, c) for it in items), name
    tr = max(d for d in range(8, r + 1, 8) if r % d == 0 and d * c <= ADAM_TILE_ELEMS)

    def body(*refs):
        ins, outs = refs[:4 * n_items], refs[4 * n_items:]
        for k in range(n_items):
            p_ref, w_ref, m_ref, v_ref = ins[4 * k:4 * k + 4]
            g = p_ref[0].astype(f32)
            for q in range(1, n_parts):
                g = g + p_ref[q].astype(f32)
            d, m2, v2 = _adam_math(w_ref[...], g, m_ref[...], v_ref[...])
            for o_ref, val in zip(outs[4 * k:4 * k + 4], (g, d, m2, v2)):
                o_ref[...] = val

    blk = pl.BlockSpec((None, tr, c), lambda i: (0, i, 0))
    res = pl.pallas_call(
        body, name=name, grid=(r // tr,),
        in_specs=[pl.BlockSpec((n_parts, tr, c), lambda i: (0, i, 0)), blk, blk, blk] * n_items,
        out_specs=[blk] * (4 * n_items), out_shape=[jax.ShapeDtypeStruct((1, r, c), f32)] * (4 * n_items),
        compiler_params=_params(("arbitrary",)),
    )(*[a for it in items for a in it])
    return [res[4 * k:4 * k + 4] for k in range(n_items)]


def _adamw_small(items, *, name):
    n = len(items)

    def body(*refs):
        ins, outs = refs[:4 * n], refs[4 * n:]
        for k in range(n):
            g, w, m, v = (ins[4 * k + q][...] for q in range(4))
            d, m2, v2 = _adam_math(w, g, m, v)
            outs[3 * k][...] = d
            outs[3 * k + 1][...] = m2
            outs[3 * k + 2][...] = v2

    vm = pl.BlockSpec(memory_space=pltpu.VMEM)
    flat = [a for item in items for a in item]
    out_shape = [jax.ShapeDtypeStruct(item[1].shape, f32) for item in items for _ in range(3)]
    return pl.pallas_call(
        body, name=name, in_specs=[vm] * (4 * n), out_specs=[vm] * (3 * n), out_shape=out_shape,
    )(*flat)


def _sum_parts(parts, *, name):
    def body(p_ref, o_ref):
        acc = p_ref[0]
        for q in range(1, N_DEV):
            acc = acc + p_ref[q]
        o_ref[...] = acc

    vm = pl.BlockSpec(memory_space=pltpu.VMEM)
    return pl.pallas_call(
        body, name=name, in_specs=[vm], out_specs=vm, out_shape=jax.ShapeDtypeStruct(parts.shape[1:], f32),
    )(parts)


WEIGHTS = ["ffn1_w_gate", "ffn1_w_up", "ffn1_w_down", "ln1_g", "ln1_b", "w_in", "b_forget", "conv_w", "conv_b",
           "rg_wa", "rg_ba", "rg_wx", "rg_bx", "lru_lambda", "w_out", "ln2_g", "ln2_b",
           "ffn2_w_gate", "ffn2_w_up", "ffn2_w_down", "ln3_g", "ln3_b"]
BIG = ["ffn1_w_gate", "ffn1_w_up", "ffn1_w_down", "w_in", "w_out", "ffn2_w_gate", "ffn2_w_up", "ffn2_w_down"]
PACKED = ["ln1_g", "ln1_b", "ln2_g", "ln2_b", "ln3_g", "ln3_b", "conv_b", "rg_ba", "rg_bx", "lru_lambda",
          "conv_w", "rg_wa", "rg_wx", "b_forget", "loss"]
PACK_ROWS = 600


def _two_d(a):
    return a.reshape((-1, a.shape[-1]))


def _transport(a):
    return _two_d(a)


def kernel(x, ffn1_w_gate, ffn1_w_up, ffn1_w_down, ln1_g, ln1_b, w_in, b_forget, conv_w, conv_b, rg_wa, rg_ba, rg_wx, rg_bx, lru_lambda, w_out, ln2_g, ln2_b, ffn2_w_gate, ffn2_w_up, ffn2_w_down, ln3_g, ln3_b, loss_target, m_ffn1_w_gate, m_ffn1_w_up, m_ffn1_w_down, m_ln1_g, m_ln1_b, m_w_in, m_b_forget, m_conv_w, m_conv_b, m_rg_wa, m_rg_ba, m_rg_wx, m_rg_bx, m_lru_lambda, m_w_out, m_ln2_g, m_ln2_b, m_ffn2_w_gate, m_ffn2_w_up, m_ffn2_w_down, m_ln3_g, m_ln3_b, v_ffn1_w_gate, v_ffn1_w_up, v_ffn1_w_down, v_ln1_g, v_ln1_b, v_w_in, v_b_forget, v_conv_w, v_conv_b, v_rg_wa, v_rg_ba, v_rg_wx, v_rg_bx, v_lru_lambda, v_w_out, v_ln2_g, v_ln2_b, v_ffn2_w_gate, v_ffn2_w_up, v_ffn2_w_down, v_ln3_g, v_ln3_b):
    w_args = (ffn1_w_gate, ffn1_w_up, ffn1_w_down, ln1_g, ln1_b, w_in, b_forget, conv_w, conv_b, rg_wa, rg_ba, rg_wx, rg_bx, lru_lambda, w_out, ln2_g, ln2_b, ffn2_w_gate, ffn2_w_up, ffn2_w_down, ln3_g, ln3_b)
    m_args = (m_ffn1_w_gate, m_ffn1_w_up, m_ffn1_w_down, m_ln1_g, m_ln1_b, m_w_in, m_b_forget, m_conv_w, m_conv_b, m_rg_wa, m_rg_ba, m_rg_wx, m_rg_bx, m_lru_lambda, m_w_out, m_ln2_g, m_ln2_b, m_ffn2_w_gate, m_ffn2_w_up, m_ffn2_w_down, m_ln3_g, m_ln3_b)
    v_args = (v_ffn1_w_gate, v_ffn1_w_up, v_ffn1_w_down, v_ln1_g, v_ln1_b, v_w_in, v_b_forget, v_conv_w, v_conv_b, v_rg_wa, v_rg_ba, v_rg_wx, v_rg_bx, v_lru_lambda, v_w_out, v_ln2_g, v_ln2_b, v_ffn2_w_gate, v_ffn2_w_up, v_ffn2_w_down, v_ln3_g, v_ln3_b)
    w = dict(zip(WEIGHTS, w_args))
    m = dict(zip(WEIGHTS, m_args))
    v = dict(zip(WEIGHTS, v_args))
    me = 4 * lax.axis_index("x") + 2 * lax.axis_index("y") + lax.axis_index("c")

    sent = {n: _transport(w[n]).astype(bf16) for n in BIG}
    sent["conv_w"] = _two_d(w["conv_w"])
    small = {n: w[n] for n in ("ln1_g", "ln1_b", "ln2_g", "ln2_b", "ln3_g", "ln3_b", "b_forget", "conv_b",
                               "lru_lambda")}
    small.update({n: w[n][0] for n in ("rg_wa", "rg_ba", "rg_wx", "rg_bx")})

    grad_x, parts, all_packed, small_shapes = _local_step(x[0], loss_target[0], sent, small)

    total = _sum_parts(all_packed, name="sum_small_grads")
    grads, off = {}, 0
    for n in PACKED:
        size = math.prod(small_shapes[n])
        rows = -(-size // LANES)
        grads[n] = total[off:off + rows].reshape(-1)[:size].reshape(small_shapes[n])
        off += rows
    loss = grads.pop("loss").reshape(())
    grads["conv_w"] = lax.dynamic_slice_in_dim(grads["conv_w"], me * (LRU_W // N_DEV), LRU_W // N_DEV, axis=1)

    delta, new_m, new_v = {}, {}, {}
    for group in (("ffn1_w_gate", "ffn1_w_up", "ffn2_w_gate", "ffn2_w_up"), ("ffn1_w_down", "ffn2_w_down"),
                  ("w_in",), ("w_out",)):
        done = _adamw_big([(parts[n], w[n], m[n], v[n]) for n in group], name="adamw_" + group[0])
        for n, (g, d, m2, v2) in zip(group, done):
            grads[n], delta[n], new_m[n], new_v[n] = g, d, m2, v2
    small_names = [n for n in WEIGHTS if n not in BIG]
    outs = _adamw_small([(_two_d(grads[n]), _two_d(w[n]), _two_d(m[n]), _two_d(v[n])) for n in small_names],
                        name="adamw_small")
    for k, n in enumerate(small_names):
        delta[n], new_m[n], new_v[n] = outs[3 * k], outs[3 * k + 1], outs[3 * k + 2]

    def shaped(d):
        return [d[n].reshape(w[n].shape) for n in WEIGHTS]

    return (loss, grad_x[None], *shaped(grads), *shaped(delta), *shaped(new_m), *shaped(new_v))
```

```python
import functools
import math

import jax
import jax.numpy as jnp
from jax import lax
from jax.experimental import pallas as pl
from jax.experimental.pallas import tpu as pltpu

f32 = jnp.float32
bf16 = jnp.bfloat16

N_DEV = 8
D_MODEL = 1024
D_FF = 4096
FF_TILE = D_FF // N_DEV
FOX_W = 512
LRU_W = 512
HEADS = 8
HEAD_D = 64
IN_COLS = 2568
IN_SHARD = IN_COLS // N_DEV
LANES = 128
LN_EPS = 1e-5
ALPHA = 2.0 ** 0.25
ATT_SCALE = 1.0 / math.sqrt(HEAD_D)
LRU_C = 8.0
NEG_BIG = -1e30

ADAM_LR = 0.001
ADAM_B1 = 0.9
ADAM_B2 = 0.999
ADAM_EPS = 1e-08
ADAM_WD = 0.01
ADAM_STEP = 10

VMEM_LIMIT = 56 * 1024 * 1024
MESH_T = pl.DeviceIdType.MESH


def _params(sem, **kw):
    return pltpu.CompilerParams(dimension_semantics=sem, vmem_limit_bytes=VMEM_LIMIT, **kw)


def _sigmoid(x):
    return 1.0 / (1.0 + jnp.exp(-x))


def _sigmoid_tanh(x):
    return 0.5 * jnp.tanh(0.5 * x) + 0.5


def _softplus(x):
    return jnp.maximum(x, 0.0) + jnp.log(1.0 + jnp.exp(-jnp.abs(x)))


def _one_minus_exp(x):
    series = -x * (1.0 + x * (0.5 + x * (1.0 / 6 + x * (1.0 / 24 + x * (1.0 / 120 + x * (1.0 / 720))))))
    return jnp.where(x > -0.125, series, 1.0 - jnp.exp(x))


_GELU_C = math.sqrt(2.0 / math.pi)


def _gelu_and_grad(x):
    inner = _GELU_C * (x + 0.044715 * x * x * x)
    t = jnp.tanh(inner)
    g = 0.5 * x * (1.0 + t)
    dg = 0.5 * (1.0 + t) + 0.5 * x * (1.0 - t * t) * _GELU_C * (1.0 + 3 * 0.044715 * x * x)
    return g, dg


def _ln_fwd_tile(pre):
    mu = jnp.mean(pre, axis=-1, keepdims=True)
    xc = pre - mu
    var = jnp.mean(xc * xc, axis=-1, keepdims=True)
    rstd = lax.rsqrt(var + LN_EPS)
    return xc * rstd, rstd


def _ln_bwd_tile(dy, xhat, rstd, g):
    dyg = dy * g
    m1 = jnp.mean(dyg, axis=-1, keepdims=True)
    m2 = jnp.mean(dyg * xhat, axis=-1, keepdims=True)
    dpre = rstd * (dyg - m1 - xhat * m2)
    return dpre, jnp.sum(dy * xhat, axis=0, keepdims=True), jnp.sum(dy, axis=0, keepdims=True)


_NT = (((1,), (1,)), ((), ()))
_TN = (((0,), (0,)), ((), ()))


class _Exchange:
    def __init__(self, arrs, gather, chips=False, direct=False):
        self.arrs, self.gather, self.n, self.chips = list(arrs), gather, len(arrs), chips
        self.relays = gather and not direct

    def out_shape(self):
        return [jax.ShapeDtypeStruct(((N_DEV,) + a.shape) if self.gather else a.shape, a.dtype) for a in self.arrs]

    def scratch(self):
        n_remote = self.n * (N_DEV - 1)
        return [pltpu.SemaphoreType.DMA((n_remote,)), pltpu.SemaphoreType.DMA((n_remote,)),
                pltpu.SemaphoreType.DMA((self.n,))]

    def copies(self, ins, outs, sems):
        send_sems, recv_sems, local_sems = sems
        x, y, c = lax.axis_index("x"), lax.axis_index("y"), lax.axis_index("c")
        me = 2 * x + y if self.chips else 4 * x + 2 * y + c
        out = []
        for k in range(self.n):
            for d in (range(2, N_DEV, 2) if self.chips else range(1, N_DEV)):
                px = 1 - x if d & 4 else x
                py = 1 - y if d & 2 else y
                pc = 1 - c if d & 1 else c
                sem = k * (N_DEV - 1) + d - 1
                out.append(pltpu.make_async_remote_copy(
                    src_ref=ins[k] if self.gather else ins[k].at[2 * px + py if self.chips else 4 * px + 2 * py + pc],
                    dst_ref=outs[k].at[me],
                    send_sem=send_sems.at[sem], recv_sem=recv_sems.at[sem],
                    device_id=(px, py, pc), device_id_type=MESH_T))
            out.append(pltpu.make_async_copy(ins[k] if self.gather else ins[k].at[me], outs[k].at[me],
                                             local_sems.at[k]))
        return out

    def gather_copies(self, ins, outs, sems):
        send_sems, recv_sems, local_sems = sems
        x, y, c = lax.axis_index("x"), lax.axis_index("y"), lax.axis_index("c")
        sibling = (x, y, 1 - c)
        chips = [(1 - x, y), (x, 1 - y), (1 - x, 1 - y)]
        out = []
        for k in range(self.n):
            def copy(s, block, to, src=None, k=k):
                rows = outs[k].at[4 * block[0] + 2 * block[1] + block[2]]
                sem = k * (N_DEV - 1) + s
                return pltpu.make_async_remote_copy(
                    src_ref=rows if src is None else src, dst_ref=rows, send_sem=send_sems.at[sem],
                    recv_sem=recv_sems.at[sem], device_id=to, device_id_type=MESH_T)

            first = [copy(0, (x, y, c), sibling, src=ins[k])]
            first += [copy(1 + q, (x, y, c), (*chip, c), src=ins[k]) for q, chip in enumerate(chips)]
            passed = [copy(4 + q, (*chip, c), sibling) for q, chip in enumerate(chips)]
            own = pltpu.make_async_copy(ins[k], outs[k].at[4 * x + 2 * y + c], local_sems.at[k])
            out.append((first, passed, own, copy))
        return out, sibling, chips, (x, y, c)

    def start(self, ins, outs, sems):
        if not self.relays:
            for cp in self.copies(ins, outs, sems):
                cp.start()
            return
        per_array, _, _, _ = self.gather_copies(ins, outs, sems)
        for first, _, own, _ in per_array:
            own.start()
            for cp in first:
                cp.start()

    def relay(self, ins, outs, sems):
        per_array, sibling, chips, (x, y, c) = self.gather_copies(ins, outs, sems)
        for first, passed, own, copy in per_array:
            for q, chip in enumerate(chips):
                copy(1 + q, (*chip, c), (x, y, c)).wait_recv()
                passed[q].start()

    def wait(self, ins, outs, sems, relayed=False):
        if not self.relays:
            for cp in self.copies(ins, outs, sems):
                cp.wait()
            return
        if not relayed:
            self.relay(ins, outs, sems)
        per_array, sibling, chips, (x, y, c) = self.gather_copies(ins, outs, sems)
        for first, passed, own, copy in per_array:
            copy(0, sibling, (x, y, c)).wait_recv()
            for q, chip in enumerate(chips):
                copy(4 + q, (*chip, 1 - c), (x, y, c)).wait_recv()
            for cp in first + passed:
                cp.wait_send()
            own.wait()


class _Hosts:
    def __init__(self, *hosts):
        self.hosts = hosts
        self.relays = any(h.relays for h in hosts)
        self.n = sum(h.n for h in hosts)
        self.arrs = [a for h in hosts for a in h.arrs]

    def out_shape(self):
        return [sh for h in self.hosts for sh in h.out_shape()]

    def scratch(self):
        return [sc for h in self.hosts for sc in h.scratch()]

    def _each(self, ins, outs, sems):
        at = 0
        for k, h in enumerate(self.hosts):
            yield h, ins[at:at + h.n], outs[at:at + h.n], sems[3 * k:3 * k + 3]
            at += h.n

    def start(self, ins, outs, sems):
        for h, h_in, h_out, h_sems in self._each(ins, outs, sems):
            h.start(h_in, h_out, h_sems)

    def relay(self, ins, outs, sems):
        for h, h_in, h_out, h_sems in self._each(ins, outs, sems):
            if h.relays:
                h.relay(h_in, h_out, h_sems)

    def wait(self, ins, outs, sems, relayed=False):
        for h, h_in, h_out, h_sems in self._each(ins, outs, sems):
            h.wait(h_in, h_out, h_sems, relayed=relayed and h.relays)


def _hosted_call(host, body, *, name, grid, in_specs, out_specs, out_shape, scratch_shapes=(), compiler_params):
    out_specs = list(out_specs) if isinstance(out_specs, (list, tuple)) else [out_specs]
    out_shape = list(out_shape) if isinstance(out_shape, (list, tuple)) else [out_shape]
    if host is None:
        return pl.pallas_call(body, name=name, grid=grid, in_specs=in_specs, out_specs=out_specs,
                              out_shape=out_shape, scratch_shapes=list(scratch_shapes),
                              compiler_params=compiler_params)
    n_in, n_out, n_scr, k = len(in_specs), len(out_shape), len(scratch_shapes), host.n

    def wrapped(*refs):
        ins, h_in = refs[:n_in], refs[n_in:n_in + k]
        outs, h_out = refs[n_in + k:n_in + k + n_out], refs[n_in + k + n_out:n_in + 2 * k + n_out]
        scr, sems = refs[n_in + 2 * k + n_out:n_in + 2 * k + n_out + n_scr], refs[n_in + 2 * k + n_out + n_scr:]
        ids = [pl.program_id(a) for a in range(len(grid))]
        first = functools.reduce(jnp.logical_and, [i == 0 for i in ids])
        last = functools.reduce(jnp.logical_and, [i == g - 1 for i, g in zip(ids, grid)])
        steps = math.prod(grid)
        relay_at = (3 * steps) // 4 if host.relays and steps >= 8 else None

        @pl.when(first)
        def _():
            host.start(h_in, h_out, sems)

        if relay_at is not None:
            coords, rest = [], relay_at
            for g in reversed(grid):
                coords.append(rest % g)
                rest //= g

            @pl.when(functools.reduce(jnp.logical_and, [i == cd for i, cd in zip(ids, reversed(coords))]))
            def _():
                host.relay(h_in, h_out, sems)

        body(*ins, *outs, *scr)

        @pl.when(last)
        def _():
            host.wait(h_in, h_out, sems, relayed=relay_at is not None)

    hbm = pl.BlockSpec(memory_space=pl.ANY)
    call = pl.pallas_call(
        wrapped, name=name, grid=grid, in_specs=list(in_specs) + [hbm] * k, out_specs=out_specs + [hbm] * k,
        out_shape=out_shape + host.out_shape(), scratch_shapes=list(scratch_shapes) + host.scratch(),
        compiler_params=compiler_params)
    return lambda *args: call(*args, *host.arrs)


def _ffn_fwd(xhat, g_in, b_in, wg, wu, wd, *, tm, name, host=None):
    t = xhat.shape[0]
    nj = N_DEV

    def body(x_ref, g_ref, b_ref, wg_ref, wu_ref, wd_ref, xo_ref, rstd_ref, hg_ref, hu_ref, xb, acc):
        j = pl.program_id(1)

        @pl.when(j == 0)
        def _():
            xb[...] = (x_ref[...] * g_ref[...] + b_ref[...]).astype(bf16)
            acc[...] = jnp.zeros_like(acc)

        hg = jnp.dot(xb[...], wg_ref[...], preferred_element_type=f32)
        hu = jnp.dot(xb[...], wu_ref[...], preferred_element_type=f32)
        hg_ref[...] = hg.astype(bf16)
        hu_ref[...] = hu.astype(bf16)
        a = hg * _sigmoid_tanh(hg) * hu
        acc[...] += jnp.dot(a.astype(bf16), wd_ref[...], preferred_element_type=f32)

        @pl.when(j == nj - 1)
        def _():
            x = x_ref[...] * g_ref[...] + b_ref[...]
            xo, rstd = _ln_fwd_tile(ALPHA * x + 0.5 * acc[...])
            xo_ref[...] = xo
            rstd_ref[...] = rstd

    row = pl.BlockSpec((1, D_MODEL), lambda i, j: (0, 0))
    return _hosted_call(
        host, body, name=name, grid=(t // tm, nj),
        in_specs=[pl.BlockSpec((tm, D_MODEL), lambda i, j: (i, 0)), row, row,
                  pl.BlockSpec((None, D_MODEL, FF_TILE), lambda i, j: (j, 0, 0)),
                  pl.BlockSpec((None, D_MODEL, FF_TILE), lambda i, j: (j, 0, 0)),
                  pl.BlockSpec((None, FF_TILE, D_MODEL), lambda i, j: (j, 0, 0))],
        out_specs=[pl.BlockSpec((tm, D_MODEL), lambda i, j: (i, 0)),
                   pl.BlockSpec((tm, 1), lambda i, j: (i, 0)),
                   pl.BlockSpec((tm, FF_TILE), lambda i, j: (i, j)),
                   pl.BlockSpec((tm, FF_TILE), lambda i, j: (i, j))],
        out_shape=[jax.ShapeDtypeStruct((t, D_MODEL), f32), jax.ShapeDtypeStruct((t, 1), f32),
                   jax.ShapeDtypeStruct((t, D_FF), bf16), jax.ShapeDtypeStruct((t, D_FF), bf16)],
        scratch_shapes=[pltpu.VMEM((tm, D_MODEL), bf16), pltpu.VMEM((tm, D_MODEL), f32)],
        compiler_params=_params(("arbitrary", "arbitrary")),
    )(xhat, g_in, b_in, wg, wu, wd)


def _ffn1_fwd_gathering(x, own, extra, *, tm, name):
    t = x.shape[0]
    n_i = t // tm
    n_arr = 3
    k_extra = extra.n
    ex = _Exchange(list(own), gather=True)
    ax, ay, ac = lax.axis_index("x"), lax.axis_index("y"), lax.axis_index("c")
    order = jnp.stack([4 * px + 2 * py + pc for px, py in ((ax, ay), (1 - ax, ay), (ax, 1 - ay), (1 - ax, 1 - ay))
                       for pc in (ac, 1 - ac)]).astype(jnp.int32)
    arrival = [None, (0, None), (1, 0), (4, None), (2, 1), (5, None), (3, 2), (6, None)]

    def body(order_ref, x_ref, *refs):
        w_in, e_in = refs[:n_arr], refs[n_arr:n_arr + k_extra]
        refs = refs[n_arr + k_extra:]
        xo_ref, rstd_ref, hg_ref, hu_ref = refs[:4]
        w_all, e_out = refs[4:4 + n_arr], refs[4 + n_arr:4 + n_arr + k_extra]
        acc, wgb, wub, wdb, fetch_sems, send_sems, recv_sems, local_sems = refs[4 + n_arr + k_extra:12 + n_arr + k_extra]
        e_sems = refs[12 + n_arr + k_extra:]
        bufs = (wgb, wub, wdb)
        s = pl.program_id(0)
        i = pl.program_id(1)
        per_array, sibling, chips, (x_, y_, c_) = ex.gather_copies(w_in, w_all, (send_sems, recv_sems, local_sems))

        def fetch(pos, slot):
            return [pltpu.make_async_copy(w_in[a] if pos == 0 else w_all[a].at[order_ref[pos]],
                                          bufs[a].at[slot], fetch_sems.at[n_arr * slot + a]) for a in range(n_arr)]

        def source_of(pos):
            chip = (x_, y_) if pos < 2 else chips[(pos - 2) // 2]
            return (*chip, c_ if pos % 2 == 0 else 1 - c_)

        @pl.when(jnp.logical_and(s == 0, i == 0))
        def _():
            for q in range(4):
                for first, _, own_copy, _ in per_array:
                    if q == 0:
                        own_copy.start()
                    first[q].start()
            for cp in fetch(0, 0):
                cp.start()
            for cp in fetch(0, 0):
                cp.wait()

        @pl.when(jnp.logical_and(s == N_DEV // 2 + 1, i == 0))
        def _():
            extra.start(e_in, e_out, e_sems)

        for pos in range(1, N_DEV):
            @pl.when(jnp.logical_and(s == pos - 1, i == n_i - 1))
            def _(pos=pos):
                sem, passes = arrival[pos]
                for _, passed, _, copy in per_array:
                    copy(sem, source_of(pos), (x_, y_, c_)).wait_recv()
                    if passes is not None:
                        passed[passes].start()
                for cp in fetch(pos, pos % 2):
                    cp.start()

            @pl.when(jnp.logical_and(s == pos, i == 0))
            def _(pos=pos):
                for cp in fetch(pos, pos % 2):
                    cp.wait()

        slot = s % 2
        xb = x_ref[...].astype(bf16)
        hg = jnp.dot(xb, wgb[slot], preferred_element_type=f32)
        hu = jnp.dot(xb, wub[slot], preferred_element_type=f32)
        hg_ref[...] = hg.astype(bf16)
        hu_ref[...] = hu.astype(bf16)
        a = hg * _sigmoid_tanh(hg) * hu
        part = jnp.dot(a.astype(bf16), wdb[slot], preferred_element_type=f32)

        @pl.when(s == 0)
        def _():
            acc[i] = part

        @pl.when(s > 0)
        def _():
            acc[i] += part

        @pl.when(s == N_DEV - 1)
        def _():
            xo, rstd = _ln_fwd_tile(ALPHA * x_ref[...] + 0.5 * acc[i])
            xo_ref[...] = xo
            rstd_ref[...] = rstd

        @pl.when(jnp.logical_and(s == N_DEV - 1, i == n_i - 1))
        def _():
            for first, passed, own_copy, _ in per_array:
                for cp in first + passed:
                    cp.wait_send()
                own_copy.wait()
            extra.wait(e_in, e_out, e_sems)

    hbm = pl.BlockSpec(memory_space=pl.ANY)
    last = N_DEV - 1
    tok_out = pl.BlockSpec((tm, D_MODEL), lambda s, i, o: (jnp.where(s == last, i, 0), 0))
    col_out = pl.BlockSpec((tm, 1), lambda s, i, o: (jnp.where(s == last, i, 0), 0))
    hid = pl.BlockSpec((tm, FF_TILE), lambda s, i, o: (i, o[s]))
    shard_shapes = [(N_DEV,) + w.shape for w in own]
    grid_spec = pltpu.PrefetchScalarGridSpec(
        num_scalar_prefetch=1, grid=(N_DEV, n_i),
        in_specs=[pl.BlockSpec((tm, D_MODEL), lambda s, i, o: (i, 0))] + [hbm] * (n_arr + k_extra),
        out_specs=[tok_out, col_out, hid, hid] + [hbm] * (n_arr + k_extra),
        scratch_shapes=[pltpu.VMEM((n_i, tm, D_MODEL), f32)]
        + [pltpu.VMEM((2,) + w.shape, bf16) for w in own]
        + [pltpu.SemaphoreType.DMA((2 * n_arr,))] + ex.scratch() + extra.scratch())
    res = pl.pallas_call(
        body, name=name, grid_spec=grid_spec,
        out_shape=[jax.ShapeDtypeStruct((t, D_MODEL), f32), jax.ShapeDtypeStruct((t, 1), f32),
                   jax.ShapeDtypeStruct((t, D_FF), bf16), jax.ShapeDtypeStruct((t, D_FF), bf16)]
        + [jax.ShapeDtypeStruct(sh, bf16) for sh in shard_shapes] + extra.out_shape(),
        compiler_params=_params(("arbitrary", "arbitrary")),
    )(order, x, *own, *extra.arrs)
    return res


def _ffn_bwd(dpre, hg, hu, wg, wu, wd, ln_in, *, tm, name, host=None):
    t = dpre.shape[0]
    nj = N_DEV
    with_ln = ln_in is not None

    def body(*refs):
        if with_ln:
            (dp_ref, hg_ref, hu_ref, wg_ref, wu_ref, wd_ref, xh_ref, rs_ref, g_ref,
             dx_ref, gg_ref, gb_ref, dhg_ref, dhu_ref, a_ref, dfb, acc) = refs
        else:
            (dp_ref, hg_ref, hu_ref, wg_ref, wu_ref, wd_ref,
             dx_ref, dhg_ref, dhu_ref, a_ref, dfb, acc) = refs
        i = pl.program_id(0)
        j = pl.program_id(1)

        @pl.when(j == 0)
        def _():
            dfb[...] = (0.5 * dp_ref[...]).astype(bf16)
            acc[...] = jnp.zeros_like(acc)

        da = lax.dot_general(dfb[...], wd_ref[...], _NT, preferred_element_type=f32)
        hgv = hg_ref[...].astype(f32)
        huv = hu_ref[...].astype(f32)
        sg = _sigmoid_tanh(hgv)
        silu = hgv * sg
        a_ref[...] = (silu * huv).astype(bf16)
        dhu = (da * silu).astype(bf16)
        dhg = (da * huv * (sg * (1.0 + hgv * (1.0 - sg)))).astype(bf16)
        dhg_ref[...] = dhg
        dhu_ref[...] = dhu
        acc[...] += (lax.dot_general(dhg, wg_ref[...], _NT, preferred_element_type=f32)
                     + lax.dot_general(dhu, wu_ref[...], _NT, preferred_element_type=f32))

        @pl.when(j == nj - 1)
        def _():
            dx = ALPHA * dp_ref[...] + acc[...]
            if with_ln:
                dprev, gg, gb = _ln_bwd_tile(dx, xh_ref[...], rs_ref[...], g_ref[...])
                dx_ref[...] = dprev

                @pl.when(i == 0)
                def _():
                    gg_ref[...] = gg
                    gb_ref[...] = gb

                @pl.when(i > 0)
                def _():
                    gg_ref[...] += gg
                    gb_ref[...] += gb
            else:
                dx_ref[...] = dx

    tok = pl.BlockSpec((tm, D_MODEL), lambda i, j: (i, 0), pipeline_mode=pl.Buffered(1))
    row = pl.BlockSpec((1, D_MODEL), lambda i, j: (0, 0))
    hid = pl.BlockSpec((tm, FF_TILE), lambda i, j: (i, j))
    in_specs = [tok, hid, hid,
                pl.BlockSpec((None, D_MODEL, FF_TILE), lambda i, j: (j, 0, 0)),
                pl.BlockSpec((None, D_MODEL, FF_TILE), lambda i, j: (j, 0, 0)),
                pl.BlockSpec((None, FF_TILE, D_MODEL), lambda i, j: (j, 0, 0))]
    args = [dpre, hg, hu, wg, wu, wd]
    out_specs = [tok]
    out_shape = [jax.ShapeDtypeStruct((t, D_MODEL), f32)]
    if with_ln:
        in_specs += [tok, pl.BlockSpec((tm, 1), lambda i, j: (i, 0)), row]
        args += list(ln_in)
        out_specs += [row, row]
        out_shape += [jax.ShapeDtypeStruct((1, D_MODEL), f32)] * 2
    out_specs += [hid, hid, hid]
    out_shape += [jax.ShapeDtypeStruct((t, D_FF), bf16)] * 3
    return _hosted_call(
        host, body, name=name, grid=(t // tm, nj), in_specs=in_specs, out_specs=out_specs, out_shape=out_shape,
        scratch_shapes=[pltpu.VMEM((tm, D_MODEL), bf16), pltpu.VMEM((tm, D_MODEL), f32)],
        compiler_params=_params(("arbitrary", "arbitrary")),
    )(*args)


def _ffn_bwd_act(dpre, hg, hu, wd, *, tm, name, host=None):
    t = dpre.shape[0]

    def body(dp_ref, hg_ref, hu_ref, wd_ref, dhg_ref, dhu_ref, a_ref, dfb):
        @pl.when(pl.program_id(1) == 0)
        def _():
            dfb[...] = (0.5 * dp_ref[...]).astype(bf16)

        da = lax.dot_general(dfb[...], wd_ref[...], _NT, preferred_element_type=f32)
        hgv = hg_ref[...].astype(f32)
        huv = hu_ref[...].astype(f32)
        sg = _sigmoid_tanh(hgv)
        silu = hgv * sg
        a_ref[...] = (silu * huv).astype(bf16)
        dhu_ref[...] = (da * silu).astype(bf16)
        dhg_ref[...] = (da * huv * (sg * (1.0 + hgv * (1.0 - sg)))).astype(bf16)

    hid = pl.BlockSpec((tm, FF_TILE), lambda i, j: (i, j))
    return _hosted_call(
        host, body, name=name, grid=(t // tm, N_DEV),
        in_specs=[pl.BlockSpec((tm, D_MODEL), lambda i, j: (i, 0)), hid, hid,
                  pl.BlockSpec((None, FF_TILE, D_MODEL), lambda i, j: (j, 0, 0))],
        out_specs=[hid, hid, hid], out_shape=[jax.ShapeDtypeStruct((t, D_FF), bf16)] * 3,
        scratch_shapes=[pltpu.VMEM((tm, D_MODEL), bf16)],
        compiler_params=_params(("arbitrary", "arbitrary")),
    )(dpre, hg, hu, wd)


def _ffn_bwd_dx(dpre, dhg, dhu, wg, wu, *, tm, name, host=None):
    t = dpre.shape[0]
    nj = N_DEV

    def body(dp_ref, dhg_ref, dhu_ref, wg_ref, wu_ref, dx_ref, acc):
        j = pl.program_id(1)

        @pl.when(j == 0)
        def _():
            acc[...] = jnp.zeros_like(acc)

        acc[...] += (lax.dot_general(dhg_ref[...], wg_ref[...], _NT, preferred_element_type=f32)
                     + lax.dot_general(dhu_ref[...], wu_ref[...], _NT, preferred_element_type=f32))

        @pl.when(j == nj - 1)
        def _():
            dx_ref[...] = ALPHA * dp_ref[...] + acc[...]

    tok = pl.BlockSpec((tm, D_MODEL), lambda i, j: (i, 0))
    hid = pl.BlockSpec((tm, FF_TILE), lambda i, j: (i, j))
    wspec = pl.BlockSpec((None, D_MODEL, FF_TILE), lambda i, j: (j, 0, 0))
    return _hosted_call(
        host, body, name=name, grid=(t // tm, nj), in_specs=[tok, hid, hid, wspec, wspec],
        out_specs=[tok], out_shape=[jax.ShapeDtypeStruct((t, D_MODEL), f32)],
        scratch_shapes=[pltpu.VMEM((tm, D_MODEL), f32)],
        compiler_params=_params(("arbitrary", "arbitrary")),
    )(dpre, dhg, dhu, wg, wu)


def _mm(a, b, *, mode, out_dtype, tm, tn, tk, name, affine=None, a_cols=None, b_cols=None,
        b_blocked=False, out_blocked=False, out_scale=None):
    if mode == "nn":
        m_full, k_full = a.shape
        m_dim, k_dim = (m_full, a_cols[1]) if a_cols else (m_full, k_full)
    else:
        k_dim, m_full = a.shape
        m_dim = a_cols[1] if a_cols else m_full
    a_off = a_cols[0] if a_cols else 0
    if b_blocked:
        n_dim = b.shape[0] * b.shape[2]
        assert b.shape[2] == tn
    else:
        n_dim = b_cols[1] if b_cols else b.shape[1]
    b_off = b_cols[0] if b_cols else 0
    assert m_dim % tm == 0 and n_dim % tn == 0 and k_dim % tk == 0, (name, m_dim, n_dim, k_dim)
    nk = k_dim // tk

    def body(*refs):
        if affine is not None:
            a_ref, g_ref, s_ref, b_ref, o_ref, acc = refs
        else:
            a_ref, b_ref, o_ref, acc = refs
        k = pl.program_id(2)

        @pl.when(k == 0)
        def _():
            acc[...] = jnp.zeros_like(acc)

        av = a_ref[...]
        if affine is not None:
            av = av * g_ref[...] + s_ref[...]
        av = av.astype(bf16)
        bv = b_ref[...].astype(bf16)
        if mode == "nn":
            acc[...] += jnp.dot(av, bv, preferred_element_type=f32)
        else:
            acc[...] += lax.dot_general(av, bv, _TN, preferred_element_type=f32)

        @pl.when(k == nk - 1)
        def _():
            res = acc[...] if out_scale is None else acc[...] * out_scale
            o_ref[...] = res.astype(out_dtype)

    if mode == "nn":
        a_spec = pl.BlockSpec((tm, tk), lambda i, j, k: (i, k + a_off))
        aff_spec = pl.BlockSpec((1, tk), lambda i, j, k: (0, k + a_off))
    else:
        a_spec = pl.BlockSpec((tk, tm), lambda i, j, k: (k, i + a_off))
        aff_spec = pl.BlockSpec((1, tm), lambda i, j, k: (0, i + a_off))
    if b_blocked:
        b_spec = pl.BlockSpec((None, tk, tn), lambda i, j, k: (j, k, 0))
    else:
        b_spec = pl.BlockSpec((tk, tn), lambda i, j, k: (k, j + b_off))
    if out_blocked:
        o_spec = pl.BlockSpec((None, tm, tn), lambda i, j, k: (j, i, 0))
        o_shape = jax.ShapeDtypeStruct((n_dim // tn, m_dim, tn), out_dtype)
    else:
        o_spec = pl.BlockSpec((tm, tn), lambda i, j, k: (i, j))
        o_shape = jax.ShapeDtypeStruct((m_dim, n_dim), out_dtype)
    in_specs = [a_spec] + ([aff_spec, aff_spec] if affine is not None else []) + [b_spec]
    args = [a] + (list(affine) if affine is not None else []) + [b]
    return pl.pallas_call(
        body, name=name, grid=(m_dim // tm, n_dim // tn, nk), in_specs=in_specs, out_specs=o_spec,
        out_shape=o_shape, scratch_shapes=[pltpu.VMEM((tm, tn), f32)],
        compiler_params=_params(("arbitrary", "arbitrary", "arbitrary")),
    )(*args)


def _mm_tn(a, b, *, out_dtype, tm, mb, tn, nb, tk, name, affine=None, out_blocked=False, out_scale=None,
           pair=False, host=None):
    k_dim, m_dim = a.shape
    multi_b = isinstance(b, (list, tuple))
    b_list = list(b) if multi_b else [b]
    n_dim = nb * tn if multi_b else b.shape[1]
    assert m_dim % (mb * tm) == 0 and n_dim % (nb * tn) == 0 and k_dim % tk == 0, (name, m_dim, n_dim, k_dim)
    nk = k_dim // tk
    grid = (m_dim // (mb * tm), n_dim // (nb * tn), nk)
    if pair:
        assert mb * nb == 4 and grid[0] * grid[1] == 2 and out_dtype == bf16, name

    def body(*refs):
        if pair:
            refs, (acc, send_buf, recv_buf, keep, send_sems, recv_sems) = refs[:-6], refs[-6:]
        else:
            refs, acc = refs[:-1], refs[-1]
        a_ref, o_ref = refs[0], refs[-1]
        if affine is not None:
            g_ref, s_ref = refs[1:3]
        b_refs = refs[3 if affine is not None else 1:-1]
        k = pl.program_id(2)

        @pl.when(k == 0)
        def _():
            acc[...] = jnp.zeros_like(acc)

        av = a_ref[...]
        if affine is not None:
            av = av * g_ref[...] + s_ref[...]
        av = av.astype(bf16)
        if multi_b:
            pieces = [r[...].astype(bf16) for r in b_refs]
        else:
            bv = b_refs[0][...].astype(bf16)
            pieces = [bv[:, jn * tn:(jn + 1) * tn] for jn in range(nb)]
        for im in range(mb):
            a_t = av[:, im * tm:(im + 1) * tm].T
            for jn in range(nb):
                acc[im * nb + jn] += jnp.dot(a_t, pieces[jn], preferred_element_type=f32)

        def scaled(v):
            return v if out_scale is None else v * out_scale

        @pl.when(k == nk - 1)
        def _():
            if pair:
                x, y, c = lax.axis_index("x"), lax.axis_index("y"), lax.axis_index("c")
                window = pl.program_id(0) + pl.program_id(1)

                def swap(w, cc):
                    return pltpu.make_async_remote_copy(
                        src_ref=send_buf.at[w, cc], dst_ref=recv_buf.at[w, cc],
                        send_sem=send_sems.at[2 * w + cc], recv_sem=recv_sems.at[2 * w + cc],
                        device_id=(x, y, 1 - c), device_id_type=MESH_T)

                for w in range(2):
                    @pl.when(window == w)
                    def _(w=w):
                        for cc in range(2):
                            send_buf[w, cc] = scaled(acc[2 * cc + 1 - c]).astype(bf16)
                            swap(w, cc).start()
                            if w == 0:
                                keep[cc] = scaled(acc[2 * cc + c])

                @pl.when(window == 1)
                def _():
                    for w in range(2):
                        for cc in range(2):
                            swap(w, cc).wait_recv()
                            mine = keep[cc] if w == 0 else scaled(acc[2 * cc + c])
                            o_ref[2 * w + cc] = (mine + recv_buf[w, cc].astype(f32)).astype(bf16)
                    for w in range(2):
                        for cc in range(2):
                            swap(w, cc).wait_send()
                return
            for im in range(mb):
                for jn in range(nb):
                    res = scaled(acc[im * nb + jn])
                    if out_blocked:
                        o_ref[jn, im * tm:(im + 1) * tm, :] = res.astype(out_dtype)
                    else:
                        o_ref[im * tm:(im + 1) * tm, jn * tn:(jn + 1) * tn] = res.astype(out_dtype)

    a_spec = pl.BlockSpec((tk, mb * tm), lambda i, j, k: (k, i))
    aff_spec = pl.BlockSpec((1, mb * tm), lambda i, j, k: (0, i))
    if multi_b:
        b_specs = [pl.BlockSpec((tk, tn), lambda i, j, k: (k, 0))] * nb
    else:
        b_specs = [pl.BlockSpec((tk, nb * tn), lambda i, j, k: (k, j))]
    scratch = [pltpu.VMEM((mb * nb, tm, tn), f32)]
    if pair:
        o_spec = pl.BlockSpec((4, tm, tn), lambda i, j, k: (0, 0, 0))
        o_shape = jax.ShapeDtypeStruct((4, tm, tn), out_dtype)
        scratch += [pltpu.VMEM((2, 2, tm, tn), bf16), pltpu.VMEM((2, 2, tm, tn), bf16), pltpu.VMEM((2, tm, tn), f32),
                    pltpu.SemaphoreType.DMA((4,)), pltpu.SemaphoreType.DMA((4,))]
    elif out_blocked:
        o_spec = pl.BlockSpec((nb, mb * tm, tn), lambda i, j, k: (j, i, 0))
        o_shape = jax.ShapeDtypeStruct((n_dim // tn, m_dim, tn), out_dtype)
    else:
        o_spec = pl.BlockSpec((mb * tm, nb * tn), lambda i, j, k: (i, j))
        o_shape = jax.ShapeDtypeStruct((m_dim, n_dim), out_dtype)
    in_specs = [a_spec] + ([aff_spec, aff_spec] if affine is not None else []) + b_specs
    args = [a] + (list(affine) if affine is not None else []) + b_list
    res = _hosted_call(
        host, body, name=name, grid=grid, in_specs=in_specs, out_specs=o_spec, out_shape=o_shape,
        scratch_shapes=scratch, compiler_params=_params(("arbitrary", "arbitrary", "arbitrary")),
    )(*args)
    return res[0] if host is None else res


def _in_proj(xhat, g, b, w_in, *, tm, name):
    t = xhat.shape[0]
    n_qkv, n_l = 3 * FOX_W, 2 * LRU_W

    def body(x_ref, g_ref, b_ref, w_ref, qkv_ref, zl_ref, zfg_ref):
        xb = (x_ref[...] * g_ref[...] + b_ref[...]).astype(bf16)
        qkv_ref[...] = jnp.dot(xb, w_ref[:, :n_qkv], preferred_element_type=f32).astype(bf16)
        zl_ref[...] = jnp.dot(xb, w_ref[:, n_qkv:n_qkv + n_l], preferred_element_type=f32)
        zfg_ref[...] = jnp.dot(xb, w_ref[:, n_qkv + n_l:], preferred_element_type=f32)

    row = pl.BlockSpec((1, D_MODEL), lambda i: (0, 0))
    return pl.pallas_call(
        body, name=name, grid=(t // tm,),
        in_specs=[pl.BlockSpec((tm, D_MODEL), lambda i: (i, 0)), row, row,
                  pl.BlockSpec(w_in.shape, lambda i: (0, 0))],
        out_specs=[pl.BlockSpec((tm, n_qkv), lambda i: (i, 0)), pl.BlockSpec((tm, n_l), lambda i: (i, 0)),
                   pl.BlockSpec((tm, LANES), lambda i: (i, 0))],
        out_shape=[jax.ShapeDtypeStruct((t, n_qkv), bf16), jax.ShapeDtypeStruct((t, n_l), f32),
                   jax.ShapeDtypeStruct((t, LANES), f32)],
        compiler_params=_params(("arbitrary",)),
    )(xhat, g, b, w_in)


def _mmln(pairs, *, tm, name, resid=None, resid_scale=1.0, epi=None, ln=None, n_out=D_MODEL):
    t = pairs[0][0].shape[0]
    n_pairs = len(pairs)
    n_resid = 0 if resid is None else len(resid) - 1

    def body(*refs):
        pos = 0
        val = None
        for p in range(n_pairs):
            a_ref, b_ref = refs[pos], refs[pos + 1]
            pos += 2
            av = a_ref[...].astype(bf16)
            bv = b_ref[...].astype(bf16)
            if pairs[p][6] == "nn":
                term = jnp.dot(av, bv, preferred_element_type=f32)
            else:
                term = lax.dot_general(av, bv, _NT, preferred_element_type=f32)
            val = term if val is None else val + term
        if resid is not None:
            if resid[0] == "plain":
                r = refs[pos][...]
            else:
                r = refs[pos][...] * refs[pos + 1][...] + refs[pos + 2][...]
            pos += n_resid
            val = val + resid_scale * r
        if epi is None:
            o_ref = refs[pos]
            o_ref[...] = val.astype(o_ref.dtype)
        elif epi == "ln_fwd":
            xo, rstd = _ln_fwd_tile(val)
            refs[pos][...] = xo
            refs[pos + 1][...] = rstd
        else:
            xh_ref, rs_ref, g_ref, dx_ref, gg_ref, gb_ref = refs[pos:pos + 6]
            dprev, gg, gb = _ln_bwd_tile(val, xh_ref[...], rs_ref[...], g_ref[...])
            dx_ref[...] = dprev
            i = pl.program_id(0)

            @pl.when(i == 0)
            def _():
                gg_ref[...] = gg
                gb_ref[...] = gb

            @pl.when(i > 0)
            def _():
                gg_ref[...] += gg
                gb_ref[...] += gb

    in_specs, args = [], []
    for (a, acb, aw, b, bcb, bw, mode) in pairs:
        in_specs.append(pl.BlockSpec((tm, aw), lambda i, acb=acb: (i, acb)))
        args.append(a)
        if mode == "nn":
            in_specs.append(pl.BlockSpec((aw, n_out), lambda i, bcb=bcb: (bcb, 0)))
        else:
            in_specs.append(pl.BlockSpec((n_out, bw), lambda i, bcb=bcb: (0, bcb)))
        args.append(b)
    tok = pl.BlockSpec((tm, n_out), lambda i: (i, 0))
    row = pl.BlockSpec((1, n_out), lambda i: (0, 0))
    col = pl.BlockSpec((tm, 1), lambda i: (i, 0))
    if resid is not None:
        in_specs += [tok] if resid[0] == "plain" else [tok, row, row]
        args += list(resid[1:])
    if epi is None:
        out_specs, out_shape = tok, jax.ShapeDtypeStruct((t, n_out), f32)
    elif epi == "ln_fwd":
        out_specs = [tok, col]
        out_shape = [jax.ShapeDtypeStruct((t, n_out), f32), jax.ShapeDtypeStruct((t, 1), f32)]
    else:
        in_specs += [tok, col, row]
        args += list(ln)
        out_specs = [tok, row, row]
        out_shape = [jax.ShapeDtypeStruct((t, n_out), f32)] + [jax.ShapeDtypeStruct((1, n_out), f32)] * 2
    return pl.pallas_call(
        body, name=name, grid=(t // tm,), in_specs=in_specs, out_specs=out_specs, out_shape=out_shape,
        compiler_params=_params(("arbitrary",)),
    )(*args)


def _loss_bwd(xhat, rstd, g, b, target, *, tm, name):
    t = xhat.shape[0]

    def body(xh_ref, rs_ref, g_ref, b_ref, tg_ref, dx_ref, sq_ref, gg_ref, gb_ref):
        i = pl.program_id(0)
        xh = xh_ref[...]
        diff = xh * g_ref[...] + b_ref[...] - tg_ref[...]
        sq = jnp.sum(diff * diff, axis=0, keepdims=True)
        dprev, gg, gb = _ln_bwd_tile(diff * (1.0 / D_MODEL), xh, rs_ref[...], g_ref[...])
        dx_ref[...] = dprev

        @pl.when(i == 0)
        def _():
            sq_ref[...] = sq
            gg_ref[...] = gg
            gb_ref[...] = gb

        @pl.when(i > 0)
        def _():
            sq_ref[...] += sq
            gg_ref[...] += gg
            gb_ref[...] += gb

    tok = pl.BlockSpec((tm, D_MODEL), lambda i: (i, 0))
    row = pl.BlockSpec((1, D_MODEL), lambda i: (0, 0))
    return pl.pallas_call(
        body, name=name, grid=(t // tm,),
        in_specs=[tok, pl.BlockSpec((tm, 1), lambda i: (i, 0)), row, row, tok],
        out_specs=[tok, row, row, row],
        out_shape=[jax.ShapeDtypeStruct((t, D_MODEL), f32)] + [jax.ShapeDtypeStruct((1, D_MODEL), f32)] * 3,
        compiler_params=_params(("arbitrary",)),
    )(xhat, rstd, g, b, target)


CUM_TILE = 256


def _tri(n, lower):
    r = lax.broadcasted_iota(jnp.int32, (n, n), 0)
    c = lax.broadcasted_iota(jnp.int32, (n, n), 1)
    return jnp.where((r >= c) if lower else (r <= c), 1.0, 0.0).astype(f32)


def _cum_fwd(zfg, bfg, *, name):
    t = zfg.shape[0]

    def body(z_ref, b_ref, o_ref, carry):
        @pl.when(pl.program_id(0) == 0)
        def _():
            carry[...] = jnp.zeros_like(carry)

        ls = -_softplus(-(z_ref[...] + b_ref[...]))
        c = jnp.dot(_tri(CUM_TILE, True), ls, preferred_element_type=f32,
                    precision=lax.Precision.HIGHEST) + carry[...]
        o_ref[...] = c
        carry[...] = c[CUM_TILE - 1:CUM_TILE, :]

    blk = pl.BlockSpec((CUM_TILE, LANES), lambda i: (i, 0))
    return pl.pallas_call(
        body, name=name, grid=(t // CUM_TILE,),
        in_specs=[blk, pl.BlockSpec((1, LANES), lambda i: (0, 0))], out_specs=blk,
        out_shape=jax.ShapeDtypeStruct((t, LANES), f32), scratch_shapes=[pltpu.VMEM((1, LANES), f32)],
        compiler_params=_params(("arbitrary",)),
    )(zfg, bfg)


def _cum_bwd(dcum_q, dcum_k, zfg, bfg, *, name):
    t = zfg.shape[0]
    n = t // CUM_TILE

    def body(d_ref, d2_ref, z_ref, b_ref, o_ref, s_ref, carry):
        i = pl.program_id(0)

        @pl.when(i == 0)
        def _():
            carry[...] = jnp.zeros_like(carry)

        dls = jnp.dot(_tri(CUM_TILE, False), d_ref[...] + d2_ref[...], preferred_element_type=f32,
                      precision=lax.Precision.HIGHEST) + carry[...]
        carry[...] = dls[0:1, :]
        lane = lax.broadcasted_iota(jnp.int32, (CUM_TILE, LANES), 1)
        dfg = jnp.where(lane < HEADS, dls * _sigmoid(-(z_ref[...] + b_ref[...])), 0.0)
        o_ref[...] = dfg
        tot = jnp.sum(dfg, axis=0, keepdims=True)

        @pl.when(i == 0)
        def _():
            s_ref[...] = tot

        @pl.when(i > 0)
        def _():
            s_ref[...] += tot

    blk = pl.BlockSpec((CUM_TILE, LANES), lambda i: (n - 1 - i, 0))
    row = pl.BlockSpec((1, LANES), lambda i: (0, 0))
    return pl.pallas_call(
        body, name=name, grid=(n,), in_specs=[blk, blk, blk, row], out_specs=[blk, row],
        out_shape=[jax.ShapeDtypeStruct((t, LANES), f32), jax.ShapeDtypeStruct((1, LANES), f32)],
        scratch_shapes=[pltpu.VMEM((1, LANES), f32)],
        compiler_params=_params(("arbitrary",)),
    )(dcum_q, dcum_k, zfg, bfg)


ATT_TILE = 512


def _causal(i, j, transposed):
    r = lax.broadcasted_iota(jnp.int32, (ATT_TILE, ATT_TILE), 0)
    c = lax.broadcasted_iota(jnp.int32, (ATT_TILE, ATT_TILE), 1)
    if transposed:
        return (c + i * ATT_TILE) >= (r + j * ATT_TILE)
    return (r + i * ATT_TILE) >= (c + j * ATT_TILE)


ATT_W = HEADS * LANES


def _data_lane(h):
    return HEAD_D * (h % 2)


def _extra_lane(h):
    return HEAD_D - _data_lane(h)


def _split3(x):
    hi = x.astype(bf16)
    rest = x - hi.astype(f32)
    mid = rest.astype(bf16)
    lo = (rest - mid.astype(f32)).astype(bf16)
    return hi, mid, lo


def _three_pieces(x):
    hi, mid, lo = (p.astype(f32) for p in _split3(x))
    return (hi + pltpu.roll(mid, HEADS, axis=1) + pltpu.roll(lo, 2 * HEADS, axis=1)).astype(bf16)


def _move(h, first):
    r = lax.broadcasted_iota(jnp.int32, (LANES, LANES), 0)
    c = lax.broadcasted_iota(jnp.int32, (LANES, LANES), 1)
    hit = functools.reduce(jnp.logical_or, [jnp.logical_and(r == HEADS * q + h, c == first + q) for q in range(3)])
    return jnp.where(hit, 1.0, 0.0).astype(bf16)


def _ones_from(first, rows):
    lane = lax.broadcasted_iota(jnp.int32, (rows, LANES), 1)
    return jnp.where(jnp.logical_and(lane >= first, lane < first + 3), 1.0, 0.0)


def _own_lanes(h, rows):
    lane = lax.broadcasted_iota(jnp.int32, (rows, LANES), 1)
    return (lane < HEAD_D) if h % 2 == 0 else (lane >= HEAD_D)


def _head_values(x):
    lane = lax.broadcasted_iota(jnp.int32, x.shape, 1)
    return jnp.where(lane < HEADS, x, 0.0)


def _attn_prep_fwd(qkv, cum, *, tm, name):
    t = qkv.shape[0]

    def body(q_ref, k_ref, v_ref, c_ref, qa_ref, ka_ref, va_ref):
        c3 = _three_pieces(_head_values(c_ref[...]))
        ones = jnp.ones((tm, LANES), bf16)
        for h in range(HEADS):
            pair = slice(LANES * (h // 2), LANES * (h // 2 + 1))
            hs = slice(LANES * h, LANES * (h + 1))
            base, own = _extra_lane(h), _own_lanes(h, tm)
            eq = jnp.dot(c3, _move(h, base), preferred_element_type=f32) + _ones_from(base + 3, tm)
            ek = _ones_from(base, tm) - jnp.dot(c3, _move(h, base + 3), preferred_element_type=f32)
            qa_ref[:, hs] = jnp.where(own, q_ref[:, pair] * ATT_SCALE, eq.astype(bf16))
            ka_ref[:, hs] = jnp.where(own, k_ref[:, pair], ek.astype(bf16))
            va_ref[:, hs] = jnp.where(own, v_ref[:, pair], ones)

    wide = pl.BlockSpec((tm, ATT_W), lambda i: (i, 0))
    out = jax.ShapeDtypeStruct((t, ATT_W), bf16)
    return pl.pallas_call(
        body, name=name, grid=(t // tm,),
        in_specs=[pl.BlockSpec((tm, FOX_W), lambda i: (i, 0)), pl.BlockSpec((tm, FOX_W), lambda i: (i, 1)),
                  pl.BlockSpec((tm, FOX_W), lambda i: (i, 2)), pl.BlockSpec((tm, LANES), lambda i: (i, 0))],
        out_specs=[wide] * 3, out_shape=[out] * 3, compiler_params=_params(("arbitrary",)),
    )(qkv, qkv, qkv, cum)


def _attn_prep_bwd(qkv, cum, lse, dmix, o, *, tm, name):
    t = qkv.shape[0]

    def body(q_ref, c_ref, l_ref, do_ref, o_ref, qa_ref, da_ref):
        b3 = _three_pieces(_head_values(c_ref[...] - l_ref[...]))
        r = lax.broadcasted_iota(jnp.int32, (FOX_W, LANES), 0)
        c = lax.broadcasted_iota(jnp.int32, (FOX_W, LANES), 1)
        per_head = jnp.where(r // HEAD_D == c, 1.0, 0.0).astype(bf16)
        delta = sum(jnp.dot(p, per_head, preferred_element_type=f32) for p in _split3(do_ref[...] * o_ref[...]))
        d3 = _three_pieces(delta)
        for h in range(HEADS):
            pair = slice(LANES * (h // 2), LANES * (h // 2 + 1))
            hs = slice(LANES * h, LANES * (h + 1))
            base, own = _extra_lane(h), _own_lanes(h, tm)
            eq = jnp.dot(b3, _move(h, base), preferred_element_type=f32) + _ones_from(base + 3, tm)
            ed = -jnp.dot(d3, _move(h, base), preferred_element_type=f32)
            qa_ref[:, hs] = jnp.where(own, q_ref[:, pair] * ATT_SCALE, eq.astype(bf16))
            da_ref[:, hs] = jnp.where(own, do_ref[:, pair].astype(bf16), ed.astype(bf16))

    wide = pl.BlockSpec((tm, ATT_W), lambda i: (i, 0))
    half = pl.BlockSpec((tm, FOX_W), lambda i: (i, 0))
    col = pl.BlockSpec((tm, LANES), lambda i: (i, 0))
    out = jax.ShapeDtypeStruct((t, ATT_W), bf16)
    return pl.pallas_call(
        body, name=name, grid=(t // tm,), in_specs=[half, col, col, half, half],
        out_specs=[wide] * 2, out_shape=[out] * 2, compiler_params=_params(("arbitrary",)),
    )(qkv, cum, lse, dmix, o)


def _attn_fwd2(q_aug, k_aug, v_aug, *, name, host=None):
    t = q_aug.shape[0]
    n = t // ATT_TILE
    tq = ATT_TILE

    def body(q_ref, k_ref, v_ref, o_ref, lse_ref, acc, m_s):
        i = pl.program_id(0)
        j = pl.program_id(1)

        @pl.when(j == 0)
        def _():
            acc[...] = jnp.zeros_like(acc)
            m_s[...] = jnp.full_like(m_s, NEG_BIG)

        def block(masked):
            mask = _causal(i, j, False) if masked else None
            for h in range(HEADS):
                hs = slice(LANES * h, LANES * (h + 1))
                s = lax.dot_general(q_ref[:, hs], k_ref[:, hs], _NT, preferred_element_type=f32)
                if masked:
                    s = jnp.where(mask, s, NEG_BIG)
                blocks = [s[:, LANES * b:LANES * (b + 1)] for b in range(tq // LANES)]
                m_old = m_s[h]
                m_new = jnp.maximum(m_old, jnp.broadcast_to(
                    jnp.max(functools.reduce(jnp.maximum, blocks), axis=-1, keepdims=True), (tq, LANES)))
                p = jnp.concatenate([jnp.exp(b - m_new) for b in blocks], axis=1).astype(bf16)
                acc[h] = jnp.exp(m_old - m_new) * acc[h] + jnp.dot(p, v_ref[:, hs], preferred_element_type=f32)
                m_s[h] = m_new

        @pl.when(j < i)
        def _():
            block(False)

        @pl.when(j == i)
        def _():
            block(True)
            lse_ref[...] = jnp.zeros_like(lse_ref)
            for h in range(HEADS):
                a = acc[h]
                l = a[:, _extra_lane(h):_extra_lane(h) + 1]
                o_ref[:, HEAD_D * h:HEAD_D * (h + 1)] = a[:, _data_lane(h):_data_lane(h) + HEAD_D] / l
                lse_ref[:, h:h + 1] = m_s[h][:, 0:1] + jnp.log(l)

    kv = pl.BlockSpec((tq, ATT_W), lambda i, j: (jnp.minimum(i, j), 0))
    return _hosted_call(
        host, body, name=name, grid=(n, n),
        in_specs=[pl.BlockSpec((tq, ATT_W), lambda i, j: (i, 0)), kv, kv],
        out_specs=[pl.BlockSpec((tq, FOX_W), lambda i, j: (i, 0)), pl.BlockSpec((tq, LANES), lambda i, j: (i, 0))],
        out_shape=[jax.ShapeDtypeStruct((t, FOX_W), f32), jax.ShapeDtypeStruct((t, LANES), f32)],
        scratch_shapes=[pltpu.VMEM((HEADS, tq, LANES), f32), pltpu.VMEM((HEADS, tq, LANES), f32)],
        compiler_params=_params(("arbitrary", "arbitrary")),
    )(q_aug, k_aug, v_aug)


def _attn_bwd(qb_aug, k_aug, v_aug, do_aug, *, name, host=None):
    t = qb_aug.shape[0]
    n = t // ATT_TILE
    tk = ATT_TILE

    def body(q_ref, k_ref, v_ref, do_ref, dq_ref, dcq_ref, dk_ref, dv_ref, dck_ref, dk_acc, dv_acc, dq_all):
        j = pl.program_id(0)
        i = pl.program_id(1)

        @pl.when(jnp.logical_and(i == 0, j == 0))
        def _():
            dq_all[...] = jnp.zeros_like(dq_all)

        @pl.when(i == 0)
        def _():
            dk_acc[...] = jnp.zeros_like(dk_acc)
            dv_acc[...] = jnp.zeros_like(dv_acc)

        def block(masked):
            mask = _causal(i, j, True) if masked else None
            for h in range(HEADS):
                hs = slice(LANES * h, LANES * (h + 1))
                qh = q_ref[:, hs]
                doh = do_ref[:, hs]
                kh = k_ref[:, hs]
                s_t = lax.dot_general(kh, qh, _NT, preferred_element_type=f32)
                if masked:
                    s_t = jnp.where(mask, s_t, NEG_BIG)
                p_t = jnp.exp(s_t)
                dv_acc[h] += jnp.dot(p_t.astype(bf16), doh, preferred_element_type=f32)
                dp_t = lax.dot_general(v_ref[:, hs], doh, _NT, preferred_element_type=f32)
                ds_t = (p_t * dp_t).astype(bf16)
                dk_acc[h] += jnp.dot(ds_t, qh, preferred_element_type=f32)
                dq_all[i, h] += lax.dot_general(ds_t, kh, _TN, preferred_element_type=f32)

        @pl.when(i > j)
        def _():
            block(False)

        @pl.when(i == j)
        def _():
            block(True)
            dcq_ref[...] = jnp.zeros_like(dcq_ref)
            for h in range(HEADS):
                a = dq_all[j, h]
                dq_ref[:, HEAD_D * h:HEAD_D * (h + 1)] = (
                    a[:, _data_lane(h):_data_lane(h) + HEAD_D] * ATT_SCALE).astype(bf16)
                dcq_ref[:, h:h + 1] = a[:, _extra_lane(h):_extra_lane(h) + 1]

        @pl.when(i == n - 1)
        def _():
            dck_ref[...] = jnp.zeros_like(dck_ref)
            for h in range(HEADS):
                a = dk_acc[h]
                cols = slice(_data_lane(h), _data_lane(h) + HEAD_D)
                dk_ref[:, HEAD_D * h:HEAD_D * (h + 1)] = a[:, cols].astype(bf16)
                dv_ref[:, HEAD_D * h:HEAD_D * (h + 1)] = dv_acc[h][:, cols].astype(bf16)
                dck_ref[:, h:h + 1] = -a[:, _extra_lane(h) + 3:_extra_lane(h) + 4]

    own = pl.BlockSpec((tk, ATT_W), lambda j, i: (j, 0))
    qs = pl.BlockSpec((tk, ATT_W), lambda j, i: (jnp.maximum(i, j), 0))
    half = pl.BlockSpec((tk, FOX_W), lambda j, i: (j, 0))
    col = pl.BlockSpec((tk, LANES), lambda j, i: (j, 0))
    return _hosted_call(
        host, body, name=name, grid=(n, n), in_specs=[qs, own, own, qs],
        out_specs=[half, col, half, half, col],
        out_shape=[jax.ShapeDtypeStruct((t, FOX_W), bf16), jax.ShapeDtypeStruct((t, LANES), f32),
                   jax.ShapeDtypeStruct((t, FOX_W), bf16), jax.ShapeDtypeStruct((t, FOX_W), bf16),
                   jax.ShapeDtypeStruct((t, LANES), f32)],
        scratch_shapes=[pltpu.VMEM((HEADS, tk, LANES), f32), pltpu.VMEM((HEADS, tk, LANES), f32),
                        pltpu.VMEM((n, HEADS, tk, LANES), f32)],
        compiler_params=_params(("arbitrary", "arbitrary")),
    )(qb_aug, k_aug, v_aug, do_aug)


LRU_CHUNK = 64
LRU_G = 256
SUB = 8


def _row_ids(n):
    return lax.broadcasted_iota(jnp.int32, (n, LRU_G), 0)


def _shift_rows_down(ext, s):
    return pltpu.roll(ext, s, axis=0)[SUB:, :]


def _shift_rows_up(ext, s, n):
    return pltpu.roll(ext, ext.shape[0] - s, axis=0)[:n, :]


def _lru_gates(u, wa_ref, ba_ref, wx_ref, bx_ref, sp):
    ub = u.astype(bf16)
    r = _sigmoid(jnp.dot(ub, wa_ref[...], preferred_element_type=f32) + ba_ref[...])
    gi = _sigmoid(jnp.dot(ub, wx_ref[...], preferred_element_type=f32) + bx_ref[...])
    log_a = -LRU_C * r * sp
    a = jnp.exp(log_a)
    s = jnp.sqrt(_one_minus_exp(2.0 * log_a))
    return r, gi, a, s


def _conv_window(lx_ref, r0, ci):
    cur = lx_ref[pl.ds(r0, LRU_CHUNK), :]
    p0 = pl.multiple_of(jnp.maximum(r0 - SUB, 0), SUB)
    prev = jnp.where(ci > 0, lx_ref[pl.ds(p0, SUB), :], 0.0)
    return cur, jnp.concatenate([prev, cur], axis=0)


def _lru_fwd(zl, conv_w, conv_b, wa, ba, wx, bx, lam, *, name, host=None):
    t = zl.shape[0]
    n_chunk = t // LRU_CHUNK

    def body(lx_ref, lg_ref, cw_ref, cb_ref, wa_ref, ba_ref, wx_ref, bx_ref, lam_ref, u_ref, h_ref, y_ref):
        sp = _softplus(-lam_ref[...])
        rows = _row_ids(SUB)

        def chunk(ci, hc):
            r0 = pl.multiple_of(ci * LRU_CHUNK, LRU_CHUNK)
            cur, ext = _conv_window(lx_ref, r0, ci)
            u = cb_ref[...] + cw_ref[3:4, :] * cur
            for k in range(3):
                u = u + cw_ref[k:k + 1, :] * _shift_rows_down(ext, 3 - k)
            r, gi, a, s = _lru_gates(u, wa_ref, ba_ref, wx_ref, bx_ref, sp)
            b = s * (gi * u)
            tiles = []
            for q in range(LRU_CHUNK // SUB):
                ta = a[SUB * q:SUB * (q + 1), :]
                tb = b[SUB * q:SUB * (q + 1), :]
                for d in (1, 2, 4):
                    a_sh = jnp.where(rows >= d, pltpu.roll(ta, d, axis=0), 1.0)
                    b_sh = jnp.where(rows >= d, pltpu.roll(tb, d, axis=0), 0.0)
                    tb = ta * b_sh + tb
                    ta = ta * a_sh
                hq = tb + ta * hc
                hc = hq[SUB - 1:SUB, :]
                tiles.append(hq)
            h = jnp.concatenate(tiles, axis=0)
            u_ref[pl.ds(r0, LRU_CHUNK), :] = u
            h_ref[pl.ds(r0, LRU_CHUNK), :] = h
            gel, _ = _gelu_and_grad(lg_ref[pl.ds(r0, LRU_CHUNK), :])
            y_ref[pl.ds(r0, LRU_CHUNK), :] = gel * h
            return hc

        lax.fori_loop(0, n_chunk, chunk, jnp.zeros((1, LRU_G), f32))

    seq = lambda cb: pl.BlockSpec((t, LRU_G), lambda c, cb=cb: (0, c + cb))
    rowc = pl.BlockSpec((1, LRU_G), lambda c: (0, c))
    diag = pl.BlockSpec((LRU_G, LRU_G), lambda c: (c, c))
    out = jax.ShapeDtypeStruct((t, LRU_W), f32)
    return _hosted_call(
        host, body, name=name, grid=(LRU_W // LRU_G,),
        in_specs=[seq(0), seq(LRU_W // LRU_G), pl.BlockSpec((4, LRU_G), lambda c: (0, c)),
                  rowc, diag, rowc, diag, rowc, rowc],
        out_specs=[seq(0)] * 3, out_shape=[out] * 3,
        compiler_params=_params(("arbitrary",)),
    )(zl, zl, conv_w, conv_b, wa, ba, wx, bx, lam)


def _lru_bwd(dmix, zl, u_all, h_all, conv_w, wa, ba, wx, bx, lam, *, name, host=None):
    t = zl.shape[0]
    n_chunk = t // LRU_CHUNK

    def body(dy_ref, lx_ref, lg_ref, u_ref, h_ref, cw_ref, wa_ref, ba_ref, wx_ref, bx_ref, lam_ref,
             dlx_ref, dlg_ref, dcw_ref, dcb_ref, dba_ref, dbx_ref, dlam_ref, dwa_ref, dwx_ref, dpr_s, dpx_s):
        lam_v = lam_ref[...]
        sp = _softplus(-lam_v)
        rows = _row_ids(SUB)
        rows_c = _row_ids(LRU_CHUNK)
        zero_row = jnp.zeros((1, LRU_G), f32)

        def chunk(step, carry):
            dh_c, a_next0, du_next, dsp, dba, dbx, dcb, dw0, dw1, dw2, dw3 = carry
            ci = n_chunk - 1 - step
            r0 = pl.multiple_of(ci * LRU_CHUNK, LRU_CHUNK)
            sl = pl.ds(r0, LRU_CHUNK)
            u = u_ref[sl, :]
            r, gi, a, s = _lru_gates(u, wa_ref, ba_ref, wx_ref, bx_ref, sp)
            h = h_ref[sl, :]
            p0 = pl.multiple_of(jnp.maximum(r0 - SUB, 0), SUB)
            h_before = jnp.where(ci > 0, h_ref[pl.ds(p0, SUB), :], 0.0)[SUB - 1:SUB, :]
            h_prev = jnp.where(rows_c == 0, h_before, pltpu.roll(h, 1, axis=0))
            gel, dgel = _gelu_and_grad(lg_ref[sl, :])
            dy = dy_ref[sl, :]
            dlg_ref[sl, :] = (dy * h * dgel).astype(bf16)
            g_in = dy * gel
            a_next = jnp.where(rows_c == LRU_CHUNK - 1, a_next0, pltpu.roll(a, LRU_CHUNK - 1, axis=0))
            tiles = [None] * (LRU_CHUNK // SUB)
            for q in reversed(range(LRU_CHUNK // SUB)):
                ta = a_next[SUB * q:SUB * (q + 1), :]
                tb = g_in[SUB * q:SUB * (q + 1), :]
                for d in (1, 2, 4):
                    a_sh = jnp.where(rows < SUB - d, pltpu.roll(ta, SUB - d, axis=0), 1.0)
                    b_sh = jnp.where(rows < SUB - d, pltpu.roll(tb, SUB - d, axis=0), 0.0)
                    tb = ta * b_sh + tb
                    ta = ta * a_sh
                dhq = tb + ta * dh_c
                dh_c = dhq[0:1, :]
                tiles[q] = dhq
            dh = jnp.concatenate(tiles, axis=0)
            da = dh * h_prev
            ds = dh * gi * u
            dgi = dh * s * u
            du = dh * s * gi
            dlog_a = da * a - ds * (a * a) / s
            dr = dlog_a * (-LRU_C * sp)
            dsp = dsp + jnp.sum(dlog_a * (-LRU_C * r), axis=0, keepdims=True)
            dpr = dr * r * (1.0 - r)
            dpx = dgi * gi * (1.0 - gi)
            dprb = dpr.astype(bf16)
            dpxb = dpx.astype(bf16)
            dpr_s[sl, :] = dprb
            dpx_s[sl, :] = dpxb
            du = du + (lax.dot_general(dprb, wa_ref[...], _NT, preferred_element_type=f32)
                       + lax.dot_general(dpxb, wx_ref[...], _NT, preferred_element_type=f32))
            dba = dba + jnp.sum(dpr, axis=0, keepdims=True)
            dbx = dbx + jnp.sum(dpx, axis=0, keepdims=True)
            dcb = dcb + jnp.sum(du, axis=0, keepdims=True)
            du_ext = jnp.concatenate([du, du_next], axis=0)
            dlx = cw_ref[3:4, :] * du
            for k in range(3):
                dlx = dlx + cw_ref[k:k + 1, :] * _shift_rows_up(du_ext, 3 - k, LRU_CHUNK)
            dlx_ref[sl, :] = dlx.astype(bf16)
            cur, ext = _conv_window(lx_ref, r0, ci)
            dws = [dw0, dw1, dw2, dw3 + jnp.sum(du * cur, axis=0, keepdims=True)]
            for k in range(3):
                dws[k] = dws[k] + jnp.sum(du * _shift_rows_down(ext, 3 - k), axis=0, keepdims=True)
            return (dh_c, a[0:1, :], du[0:SUB, :], dsp, dba, dbx, dcb, dws[0], dws[1], dws[2], dws[3])

        init = (zero_row, zero_row, jnp.zeros((SUB, LRU_G), f32)) + (zero_row,) * 8
        out = lax.fori_loop(0, n_chunk, chunk, init)
        _, _, _, dsp, dba, dbx, dcb, dw0, dw1, dw2, dw3 = out
        dlam_ref[...] = dsp * (-_sigmoid(-lam_v))
        dba_ref[...] = dba
        dbx_ref[...] = dbx
        dcb_ref[...] = dcb
        dcw_ref[...] = jnp.concatenate([dw0, dw1, dw2, dw3], axis=0)
        ub = u_ref[...].astype(bf16)
        dwa_ref[...] = lax.dot_general(ub, dpr_s[...], _TN, preferred_element_type=f32)
        dwx_ref[...] = lax.dot_general(ub, dpx_s[...], _TN, preferred_element_type=f32)

    seq = lambda cb: pl.BlockSpec((t, LRU_G), lambda c, cb=cb: (0, c + cb))
    rowc = pl.BlockSpec((1, LRU_G), lambda c: (0, c))
    diag = pl.BlockSpec((LRU_G, LRU_G), lambda c: (c, c))
    gate_out = pl.BlockSpec((None, LRU_G, LRU_G), lambda c: (c, 0, 0))
    row_shape = jax.ShapeDtypeStruct((1, LRU_W), f32)
    return _hosted_call(
        host, body, name=name, grid=(LRU_W // LRU_G,),
        in_specs=[seq(LRU_W // LRU_G), seq(0), seq(LRU_W // LRU_G), seq(0), seq(0),
                  pl.BlockSpec((4, LRU_G), lambda c: (0, c)),
                  diag, rowc, diag, rowc, rowc],
        out_specs=[seq(0), seq(0), pl.BlockSpec((4, LRU_G), lambda c: (0, c)), rowc, rowc, rowc, rowc,
                   gate_out, gate_out],
        out_shape=[jax.ShapeDtypeStruct((t, LRU_W), bf16)] * 2
        + [jax.ShapeDtypeStruct((4, LRU_W), f32)] + [row_shape] * 4
        + [jax.ShapeDtypeStruct((LRU_W // LRU_G, LRU_G, LRU_G), f32)] * 2,
        scratch_shapes=[pltpu.VMEM((t, LRU_G), bf16), pltpu.VMEM((t, LRU_G), bf16)],
        compiler_params=_params(("arbitrary",)),
    )(dmix, zl, zl, u_all, h_all, conv_w, wa, ba, wx, bx, lam)


def _pack_rows(a):
    flat = a.reshape(-1)
    rows = -(-flat.shape[0] // LANES)
    return jnp.pad(flat, (0, rows * LANES - flat.shape[0])).reshape(rows, LANES)


W_IN_PAD = 21 * LANES


def _w_in_join(blocks, *, name):
    tm = 256

    def body(b_ref, o_ref):
        o_ref[:, IN_COLS:] = jnp.zeros((tm, W_IN_PAD - IN_COLS), bf16)
        for q in range(N_DEV):
            o_ref[:, IN_SHARD * q:IN_SHARD * (q + 1)] = b_ref[q]

    return pl.pallas_call(
        body, name=name, grid=(D_MODEL // tm,),
        in_specs=[pl.BlockSpec((N_DEV, tm, IN_SHARD), lambda i: (0, i, 0))],
        out_specs=pl.BlockSpec((tm, W_IN_PAD), lambda i: (i, 0)),
        out_shape=jax.ShapeDtypeStruct((D_MODEL, W_IN_PAD), bf16), compiler_params=_params(("arbitrary",)),
    )(blocks)


def _w_in_split(main, fg, *, name):
    tm = 256
    n_main = main.shape[0]

    def body(m_ref, f_ref, o_ref):
        full = jnp.concatenate([m_ref[n] for n in range(n_main)] + [f_ref[...]], axis=1)
        for q in range(N_DEV):
            o_ref[q] = full[:, IN_SHARD * q:IN_SHARD * (q + 1)]

    return pl.pallas_call(
        body, name=name, grid=(D_MODEL // tm,),
        in_specs=[pl.BlockSpec((n_main, tm, 512), lambda i: (0, i, 0)), pl.BlockSpec((tm, LANES), lambda i: (i, 0))],
        out_specs=pl.BlockSpec((N_DEV, tm, IN_SHARD), lambda i: (0, i, 0)),
        out_shape=jax.ShapeDtypeStruct((N_DEV, D_MODEL, IN_SHARD), bf16), compiler_params=_params(("arbitrary",)),
    )(main, fg)


def _block_diag(w):
    eye = jnp.eye(HEADS, dtype=w.dtype)
    return jnp.einsum("hij,hk->hikj", w, eye).reshape(LRU_W, LRU_W)


def _diag_blocks(dw):
    per = dw.shape[1] // HEAD_D
    blocks = [dw[:, HEAD_D * b:HEAD_D * (b + 1), HEAD_D * b:HEAD_D * (b + 1)] for b in range(per)]
    return jnp.stack(blocks, axis=1).reshape(HEADS, HEAD_D, HEAD_D)


def _local_step(x, target, sent, small, *, tm=512, tm_ffn=1024):
    ln1 = (small["ln1_g"], small["ln1_b"])
    ln2 = (small["ln2_g"], small["ln2_b"])
    ln3 = (small["ln3_g"], small["ln3_b"])

    xh1, rs1, hg1, hu1, wg1, wu1, wd1, w_in_g, w_out_g, conv_w_g = _ffn1_fwd_gathering(
        x, (sent["ffn1_w_gate"], sent["ffn1_w_up"], sent["ffn1_w_down"]),
        _Exchange([sent["w_in"], sent["w_out"], sent["conv_w"]], gather=True), tm=tm_ffn, name="ffn1_fwd")
    w_in = _w_in_join(w_in_g, name="w_in_join")
    w_out = w_out_g.reshape(D_MODEL, D_MODEL)
    conv_w = conv_w_g.transpose(1, 0, 2).reshape(4, LRU_W)
    qkv, zl, zfg = _in_proj(xh1, ln1[0], ln1[1], w_in, tm=tm, name="in_proj")
    bfg = jnp.pad(small["b_forget"], ((0, 0), (0, LANES - HEADS)))
    cum = _cum_fwd(zfg, bfg, name="cum_fwd")
    q_aug, k_aug, v_aug = _attn_prep_fwd(qkv, cum, tm=tm, name="attn_prep_fwd")
    o, lse, wg2, wu2 = _attn_fwd2(
        q_aug, k_aug, v_aug, name="attn_fwd",
        host=_Hosts(_Exchange([sent["ffn2_w_gate"]], gather=True),
                    _Exchange([sent["ffn2_w_up"]], gather=True, direct=True)))
    wa_bd = _block_diag(small["rg_wa"]).astype(bf16)
    wx_bd = _block_diag(small["rg_wx"]).astype(bf16)
    ba = small["rg_ba"].reshape(1, LRU_W)
    bx = small["rg_bx"].reshape(1, LRU_W)
    u, h, lru, wd2 = _lru_fwd(zl, conv_w, small["conv_b"], wa_bd, ba, wx_bd, bx, small["lru_lambda"],
                              name="lru_fwd", host=_Exchange([sent["ffn2_w_down"]], gather=True))
    xh2, rs2 = _mmln([(o, 0, FOX_W, w_out, 0, D_MODEL, "nn"), (lru, 0, LRU_W, w_out, 1, D_MODEL, "nn")],
                     tm=tm, name="mix_fwd", resid=("affine", xh1) + ln1, resid_scale=ALPHA, epi="ln_fwd")
    xh3, rs3, hg2, hu2 = _ffn_fwd(xh2, ln2[0], ln2[1], wg2, wu2, wd2, tm=tm_ffn, name="ffn2_fwd")

    dpre3, sq_rows, g_ln3g, g_ln3b = _loss_bwd(xh3, rs3, ln3[0], ln3[1], target, tm=tm, name="loss_bwd")
    dpre2, g_ln2g, g_ln2b, dhg2, dhu2, a2 = _ffn_bwd(dpre3, hg2, hu2, wg2, wu2, wd2,
                                                     (xh2, rs2, ln2[0]), tm=tm_ffn, name="ffn2_bwd")
    wgrad = dict(out_dtype=bf16, tm=D_MODEL, mb=1, tn=FF_TILE, nb=4, tk=512, pair=True)
    wdgrad = dict(out_dtype=bf16, tm=512, mb=4, tn=D_MODEL, nb=1, tk=512, out_scale=0.5, pair=True)
    between_chips = functools.partial(_Exchange, gather=False, chips=True)
    g_wg2 = _mm_tn(xh2, dhg2, name="g_wg2", affine=ln2, **wgrad)
    g_wu2 = _mm_tn(xh2, dhu2, name="g_wu2", affine=ln2, **wgrad)
    g_wd2 = _mm_tn(a2, dpre3, name="g_wd2", **wdgrad)

    dmix = _mmln([(dpre2, 0, D_MODEL, w_out, 0, D_MODEL, "nt")], tm=tm, name="dmix_bwd")
    g_wout_a = _mm(o, dpre2, mode="tn", out_dtype=bf16, tm=512, tn=D_MODEL, tk=512, name="g_wout_fox")
    g_wout_b = _mm(lru, dpre2, mode="tn", out_dtype=bf16, tm=512, tn=D_MODEL, tk=512, name="g_wout_lru")
    g_wout_blocked = jnp.concatenate([g_wout_a, g_wout_b], axis=0).reshape(N_DEV, D_MODEL // N_DEV, D_MODEL)
    dlx, dlg, g_cw, g_cb, g_ba, g_bx, g_lam, g_wa4, g_wx4, p_wg2, p_wout = _lru_bwd(
        dmix, zl, u, h, conv_w, wa_bd, ba, wx_bd, bx, small["lru_lambda"], name="lru_bwd",
        host=_Hosts(between_chips([g_wg2]), _Exchange([g_wout_blocked], gather=False)))
    qb_aug, do_aug = _attn_prep_bwd(qkv, cum, lse, dmix, o, tm=tm, name="attn_prep_bwd")
    dq, dcum_q, dk, dv, dcum_k, p_wu2, p_wd2 = _attn_bwd(qb_aug, k_aug, v_aug, do_aug, name="attn_bwd",
                                                         host=between_chips([g_wu2, g_wd2]))
    dfg, g_bf = _cum_bwd(dcum_q, dcum_k, zfg, bfg, name="cum_bwd")

    dz = [(dq, 0, 512), (dk, 1, 512), (dv, 2, 512), (dlx, 3, 512), (dlg, 4, 512), (dfg, 20, LANES)]
    dpre1, g_ln1g, g_ln1b = _mmln(
        [(arr, 0, w, w_in, cb, w, "nt") for (arr, cb, w) in dz],
        tm=tm, name="dx1_bwd", resid=("plain", dpre2), resid_scale=ALPHA, epi="ln_bwd", ln=(xh1, rs1, ln1[0]))
    g_win_main = _mm_tn(xh1, [arr for arr, _, _ in dz[:5]], out_dtype=bf16, tm=D_MODEL, mb=1, tn=512, nb=5, tk=512,
                        name="g_win", affine=ln1, out_blocked=True)
    g_win_fg = _mm(xh1, dfg, mode="tn", out_dtype=bf16, tm=D_MODEL, tn=LANES, tk=512, name="g_win_fg", affine=ln1)
    g_win_blocked = _w_in_split(g_win_main, g_win_fg, name="w_in_split")
    dhg1, dhu1, a1, p_win = _ffn_bwd_act(dpre1, hg1, hu1, wd1, tm=tm_ffn, name="ffn1_bwd_act",
                                         host=_Exchange([g_win_blocked], gather=False))
    small_g = {
        "ln1_g": g_ln1g, "ln1_b": g_ln1b, "b_forget": g_bf[:, :HEADS], "conv_w": g_cw, "conv_b": g_cb,
        "rg_wa": _diag_blocks(g_wa4), "rg_ba": g_ba.reshape(HEADS, HEAD_D),
        "rg_wx": _diag_blocks(g_wx4), "rg_bx": g_bx.reshape(HEADS, HEAD_D), "lru_lambda": g_lam,
        "ln2_g": g_ln2g, "ln2_b": g_ln2b, "ln3_g": g_ln3g, "ln3_b": g_ln3b,
    }
    small_g["loss"] = (0.5 / D_MODEL) * jnp.sum(sq_rows, keepdims=True)
    pieces = [_pack_rows(small_g[n]) for n in PACKED]
    packed = jnp.concatenate(pieces + [jnp.zeros((PACK_ROWS - sum(p.shape[0] for p in pieces), LANES), f32)])
    g_wg1, all_packed = _mm_tn(x, dhg1, name="g_wg1", host=_Exchange([packed], gather=True), **wgrad)
    g_wu1, p_wg1 = _mm_tn(x, dhu1, name="g_wu1", host=between_chips([g_wg1]), **wgrad)
    g_wd1, p_wu1 = _mm_tn(a1, dpre1, name="g_wd1", host=between_chips([g_wu1]), **wdgrad)
    grad_x, p_wd1 = _ffn_bwd_dx(dpre1, dhg1, dhu1, wg1, wu1, tm=tm_ffn, name="ffn1_bwd_dx",
                                host=between_chips([g_wd1]))
    parts = {
        "ffn1_w_gate": p_wg1, "ffn1_w_up": p_wu1, "ffn1_w_down": p_wd1, "w_in": p_win, "w_out": p_wout,
        "ffn2_w_gate": p_wg2, "ffn2_w_up": p_wu2, "ffn2_w_down": p_wd2,
    }
    return grad_x, parts, all_packed, {n: small_g[n].shape for n in PACKED}


def _adam_math(w, g, m, v):
    m2 = ADAM_B1 * m + (1.0 - ADAM_B1) * g
    v2 = ADAM_B2 * v + (1.0 - ADAM_B2) * (g * g)
    m_hat = m2 / (1.0 - ADAM_B1 ** ADAM_STEP)
    v_hat = v2 / (1.0 - ADAM_B2 ** ADAM_STEP)
    delta = -ADAM_LR * (m_hat / (jnp.sqrt(v_hat) + ADAM_EPS) + ADAM_WD * w)
    return delta, m2, v2


ADAM_TILE_ELEMS = 128 * 1024


def _adamw_big(items, *, name):
    _, r, c = items[0][1].shape
    n_parts = items[0][0].shape[0]
    n_items = len(items)
    assert all(it[1].shape == (1, r, c) and it[0].shape == (n_parts, r, c) for it in items), name
    tr = max(d for d in range(8, r + 1, 8) if r % d == 0 and d * c <= ADAM_TILE_ELEMS)

    def body(*refs):
        ins, outs = refs[:4 * n_items], refs[4 * n_items:]
        for k in range(n_items):
            p_ref, w_ref, m_ref, v_ref = ins[4 * k:4 * k + 4]
            g = p_ref[0].astype(f32)
            for q in range(1, n_parts):
                g = g + p_ref[q].astype(f32)
            d, m2, v2 = _adam_math(w_ref[...], g, m_ref[...], v_ref[...])
            for o_ref, val in zip(outs[4 * k:4 * k + 4], (g, d, m2, v2)):
                o_ref[...] = val

    blk = pl.BlockSpec((None, tr, c), lambda i: (0, i, 0))
    res = pl.pallas_call(
        body, name=name, grid=(r // tr,),
        in_specs=[pl.BlockSpec((n_parts, tr, c), lambda i: (0, i, 0)), blk, blk, blk] * n_items,
        out_specs=[blk] * (4 * n_items), out_shape=[jax.ShapeDtypeStruct((1, r, c), f32)] * (4 * n_items),
        compiler_params=_params(("arbitrary",)),
    )(*[a for it in items for a in it])
    return [res[4 * k:4 * k + 4] for k in range(n_items)]


def _adamw_small(items, *, name):
    n = len(items)

    def body(*refs):
        ins, outs = refs[:4 * n], refs[4 * n:]
        for k in range(n):
            g, w, m, v = (ins[4 * k + q][...] for q in range(4))
            d, m2, v2 = _adam_math(w, g, m, v)
            outs[3 * k][...] = d
            outs[3 * k + 1][...] = m2
            outs[3 * k + 2][...] = v2

    vm = pl.BlockSpec(memory_space=pltpu.VMEM)
    flat = [a for item in items for a in item]
    out_shape = [jax.ShapeDtypeStruct(item[1].shape, f32) for item in items for _ in range(3)]
    return pl.pallas_call(
        body, name=name, in_specs=[vm] * (4 * n), out_specs=[vm] * (3 * n), out_shape=out_shape,
    )(*flat)


def _sum_parts(parts, *, name):
    def body(p_ref, o_ref):
        acc = p_ref[0]
        for q in range(1, N_DEV):
            acc = acc + p_ref[q]
        o_ref[...] = acc

    vm = pl.BlockSpec(memory_space=pltpu.VMEM)
    return pl.pallas_call(
        body, name=name, in_specs=[vm], out_specs=vm, out_shape=jax.ShapeDtypeStruct(parts.shape[1:], f32),
    )(parts)


WEIGHTS = ["ffn1_w_gate", "ffn1_w_up", "ffn1_w_down", "ln1_g", "ln1_b", "w_in", "b_forget", "conv_w", "conv_b",
           "rg_wa", "rg_ba", "rg_wx", "rg_bx", "lru_lambda", "w_out", "ln2_g", "ln2_b",
           "ffn2_w_gate", "ffn2_w_up", "ffn2_w_down", "ln3_g", "ln3_b"]
BIG = ["ffn1_w_gate", "ffn1_w_up", "ffn1_w_down", "w_in", "w_out", "ffn2_w_gate", "ffn2_w_up", "ffn2_w_down"]
PACKED = ["ln1_g", "ln1_b", "ln2_g", "ln2_b", "ln3_g", "ln3_b", "conv_b", "rg_ba", "rg_bx", "lru_lambda",
          "conv_w", "rg_wa", "rg_wx", "b_forget", "loss"]
PACK_ROWS = 600


def _two_d(a):
    return a.reshape((-1, a.shape[-1]))


def _transport(a):
    return _two_d(a)


def kernel(x, ffn1_w_gate, ffn1_w_up, ffn1_w_down, ln1_g, ln1_b, w_in, b_forget, conv_w, conv_b, rg_wa, rg_ba, rg_wx, rg_bx, lru_lambda, w_out, ln2_g, ln2_b, ffn2_w_gate, ffn2_w_up, ffn2_w_down, ln3_g, ln3_b, loss_target, m_ffn1_w_gate, m_ffn1_w_up, m_ffn1_w_down, m_ln1_g, m_ln1_b, m_w_in, m_b_forget, m_conv_w, m_conv_b, m_rg_wa, m_rg_ba, m_rg_wx, m_rg_bx, m_lru_lambda, m_w_out, m_ln2_g, m_ln2_b, m_ffn2_w_gate, m_ffn2_w_up, m_ffn2_w_down, m_ln3_g, m_ln3_b, v_ffn1_w_gate, v_ffn1_w_up, v_ffn1_w_down, v_ln1_g, v_ln1_b, v_w_in, v_b_forget, v_conv_w, v_conv_b, v_rg_wa, v_rg_ba, v_rg_wx, v_rg_bx, v_lru_lambda, v_w_out, v_ln2_g, v_ln2_b, v_ffn2_w_gate, v_ffn2_w_up, v_ffn2_w_down, v_ln3_g, v_ln3_b):
    w_args = (ffn1_w_gate, ffn1_w_up, ffn1_w_down, ln1_g, ln1_b, w_in, b_forget, conv_w, conv_b, rg_wa, rg_ba, rg_wx, rg_bx, lru_lambda, w_out, ln2_g, ln2_b, ffn2_w_gate, ffn2_w_up, ffn2_w_down, ln3_g, ln3_b)
    m_args = (m_ffn1_w_gate, m_ffn1_w_up, m_ffn1_w_down, m_ln1_g, m_ln1_b, m_w_in, m_b_forget, m_conv_w, m_conv_b, m_rg_wa, m_rg_ba, m_rg_wx, m_rg_bx, m_lru_lambda, m_w_out, m_ln2_g, m_ln2_b, m_ffn2_w_gate, m_ffn2_w_up, m_ffn2_w_down, m_ln3_g, m_ln3_b)
    v_args = (v_ffn1_w_gate, v_ffn1_w_up, v_ffn1_w_down, v_ln1_g, v_ln1_b, v_w_in, v_b_forget, v_conv_w, v_conv_b, v_rg_wa, v_rg_ba, v_rg_wx, v_rg_bx, v_lru_lambda, v_w_out, v_ln2_g, v_ln2_b, v_ffn2_w_gate, v_ffn2_w_up, v_ffn2_w_down, v_ln3_g, v_ln3_b)
    w = dict(zip(WEIGHTS, w_args))
    m = dict(zip(WEIGHTS, m_args))
    v = dict(zip(WEIGHTS, v_args))
    me = 4 * lax.axis_index("x") + 2 * lax.axis_index("y") + lax.axis_index("c")

    sent = {n: _transport(w[n]).astype(bf16) for n in BIG}
    sent["conv_w"] = _two_d(w["conv_w"])
    small = {n: w[n] for n in ("ln1_g", "ln1_b", "ln2_g", "ln2_b", "ln3_g", "ln3_b", "b_forget", "conv_b",
                               "lru_lambda")}
    small.update({n: w[n][0] for n in ("rg_wa", "rg_ba", "rg_wx", "rg_bx")})

    grad_x, parts, all_packed, small_shapes = _local_step(x[0], loss_target[0], sent, small)

    total = _sum_parts(all_packed, name="sum_small_grads")
    grads, off = {}, 0
    for n in PACKED:
        size = math.prod(small_shapes[n])
        rows = -(-size // LANES)
        grads[n] = total[off:off + rows].reshape(-1)[:size].reshape(small_shapes[n])
        off += rows
    loss = grads.pop("loss").reshape(())
    grads["conv_w"] = lax.dynamic_slice_in_dim(grads["conv_w"], me * (LRU_W // N_DEV), LRU_W // N_DEV, axis=1)

    delta, new_m, new_v = {}, {}, {}
    for group in (("ffn1_w_gate", "ffn1_w_up", "ffn2_w_gate", "ffn2_w_up"), ("ffn1_w_down", "ffn2_w_down"),
                  ("w_in",), ("w_out",)):
        done = _adamw_big([(parts[n], w[n], m[n], v[n]) for n in group], name="adamw_" + group[0])
        for n, (g, d, m2, v2) in zip(group, done):
            grads[n], delta[n], new_m[n], new_v[n] = g, d, m2, v2
    small_names = [n for n in WEIGHTS if n not in BIG]
    outs = _adamw_small([(_two_d(grads[n]), _two_d(w[n]), _two_d(m[n]), _two_d(v[n])) for n in small_names],
                        name="adamw_small")
    for k, n in enumerate(small_names):
        delta[n], new_m[n], new_v[n] = outs[3 * k], outs[3 * k + 1], outs[3 * k + 2]

    def shaped(d):
        return [d[n].reshape(w[n].shape) for n in WEIGHTS]

    return (loss, grad_x[None], *shaped(grads), *shaped(delta), *shaped(new_m), *shaped(new_v))
```

```python
import functools
import math

import jax
import jax.numpy as jnp
from jax import lax
from jax.experimental import pallas as pl
from jax.experimental.pallas import tpu as pltpu

f32 = jnp.float32
bf16 = jnp.bfloat16

N_DEV = 8
D_MODEL = 1024
D_FF = 4096
FF_TILE = D_FF // N_DEV
FOX_W = 512
LRU_W = 512
HEADS = 8
HEAD_D = 64
IN_COLS = 2568
IN_SHARD = IN_COLS // N_DEV
LANES = 128
LN_EPS = 1e-5
ALPHA = 2.0 ** 0.25
ATT_SCALE = 1.0 / math.sqrt(HEAD_D)
LRU_C = 8.0
NEG_BIG = -1e30

ADAM_LR = 0.001
ADAM_B1 = 0.9
ADAM_B2 = 0.999
ADAM_EPS = 1e-08
ADAM_WD = 0.01
ADAM_STEP = 10

VMEM_LIMIT = 56 * 1024 * 1024
MESH_T = pl.DeviceIdType.MESH


def _params(sem, **kw):
    return pltpu.CompilerParams(dimension_semantics=sem, vmem_limit_bytes=VMEM_LIMIT, **kw)


def _sigmoid(x):
    return 1.0 / (1.0 + jnp.exp(-x))


def _sigmoid_tanh(x):
    return 0.5 * jnp.tanh(0.5 * x) + 0.5


def _softplus(x):
    return jnp.maximum(x, 0.0) + jnp.log(1.0 + jnp.exp(-jnp.abs(x)))


def _one_minus_exp(x):
    series = -x * (1.0 + x * (0.5 + x * (1.0 / 6 + x * (1.0 / 24 + x * (1.0 / 120 + x * (1.0 / 720))))))
    return jnp.where(x > -0.125, series, 1.0 - jnp.exp(x))


_GELU_C = math.sqrt(2.0 / math.pi)


def _gelu_and_grad(x):
    inner = _GELU_C * (x + 0.044715 * x * x * x)
    t = jnp.tanh(inner)
    g = 0.5 * x * (1.0 + t)
    dg = 0.5 * (1.0 + t) + 0.5 * x * (1.0 - t * t) * _GELU_C * (1.0 + 3 * 0.044715 * x * x)
    return g, dg


def _ln_fwd_tile(pre):
    mu = jnp.mean(pre, axis=-1, keepdims=True)
    xc = pre - mu
    var = jnp.mean(xc * xc, axis=-1, keepdims=True)
    rstd = lax.rsqrt(var + LN_EPS)
    return xc * rstd, rstd


def _ln_bwd_tile(dy, xhat, rstd, g):
    dyg = dy * g
    m1 = jnp.mean(dyg, axis=-1, keepdims=True)
    m2 = jnp.mean(dyg * xhat, axis=-1, keepdims=True)
    dpre = rstd * (dyg - m1 - xhat * m2)
    return dpre, jnp.sum(dy * xhat, axis=0, keepdims=True), jnp.sum(dy, axis=0, keepdims=True)


_NT = (((1,), (1,)), ((), ()))
_TN = (((0,), (0,)), ((), ()))


class _Exchange:
    def __init__(self, arrs, gather, chips=False, direct=False):
        self.arrs, self.gather, self.n, self.chips = list(arrs), gather, len(arrs), chips
        self.relays = gather and not direct

    def out_shape(self):
        return [jax.ShapeDtypeStruct(((N_DEV,) + a.shape) if self.gather else a.shape, a.dtype) for a in self.arrs]

    def scratch(self):
        n_remote = self.n * (N_DEV - 1)
        return [pltpu.SemaphoreType.DMA((n_remote,)), pltpu.SemaphoreType.DMA((n_remote,)),
                pltpu.SemaphoreType.DMA((self.n,))]

    def copies(self, ins, outs, sems):
        send_sems, recv_sems, local_sems = sems
        x, y, c = lax.axis_index("x"), lax.axis_index("y"), lax.axis_index("c")
        me = 2 * x + y if self.chips else 4 * x + 2 * y + c
        out = []
        for k in range(self.n):
            for d in (range(2, N_DEV, 2) if self.chips else range(1, N_DEV)):
                px = 1 - x if d & 4 else x
                py = 1 - y if d & 2 else y
                pc = 1 - c if d & 1 else c
                sem = k * (N_DEV - 1) + d - 1
                out.append(pltpu.make_async_remote_copy(
                    src_ref=ins[k] if self.gather else ins[k].at[2 * px + py if self.chips else 4 * px + 2 * py + pc],
                    dst_ref=outs[k].at[me],
                    send_sem=send_sems.at[sem], recv_sem=recv_sems.at[sem],
                    device_id=(px, py, pc), device_id_type=MESH_T))
            out.append(pltpu.make_async_copy(ins[k] if self.gather else ins[k].at[me], outs[k].at[me],
                                             local_sems.at[k]))
        return out

    def gather_copies(self, ins, outs, sems):
        send_sems, recv_sems, local_sems = sems
        x, y, c = lax.axis_index("x"), lax.axis_index("y"), lax.axis_index("c")
        sibling = (x, y, 1 - c)
        chips = [(1 - x, y), (x, 1 - y), (1 - x, 1 - y)]
        out = []
        for k in range(self.n):
            def copy(s, block, to, src=None, k=k):
                rows = outs[k].at[4 * block[0] + 2 * block[1] + block[2]]
                sem = k * (N_DEV - 1) + s
                return pltpu.make_async_remote_copy(
                    src_ref=rows if src is None else src, dst_ref=rows, send_sem=send_sems.at[sem],
                    recv_sem=recv_sems.at[sem], device_id=to, device_id_type=MESH_T)

            first = [copy(0, (x, y, c), sibling, src=ins[k])]
            first += [copy(1 + q, (x, y, c), (*chip, c), src=ins[k]) for q, chip in enumerate(chips)]
            passed = [copy(4 + q, (*chip, c), sibling) for q, chip in enumerate(chips)]
            own = pltpu.make_async_copy(ins[k], outs[k].at[4 * x + 2 * y + c], local_sems.at[k])
            out.append((first, passed, own, copy))
        return out, sibling, chips, (x, y, c)

    def start(self, ins, outs, sems):
        if not self.relays:
            for cp in self.copies(ins, outs, sems):
                cp.start()
            return
        per_array, _, _, _ = self.gather_copies(ins, outs, sems)
        for first, _, own, _ in per_array:
            own.start()
            for cp in first:
                cp.start()

    def relay(self, ins, outs, sems):
        per_array, sibling, chips, (x, y, c) = self.gather_copies(ins, outs, sems)
        for first, passed, own, copy in per_array:
            for q, chip in enumerate(chips):
                copy(1 + q, (*chip, c), (x, y, c)).wait_recv()
                passed[q].start()

    def wait(self, ins, outs, sems, relayed=False):
        if not self.relays:
            for cp in self.copies(ins, outs, sems):
                cp.wait()
            return
        if not relayed:
            self.relay(ins, outs, sems)
        per_array, sibling, chips, (x, y, c) = self.gather_copies(ins, outs, sems)
        for first, passed, own, copy in per_array:
            copy(0, sibling, (x, y, c)).wait_recv()
            for q, chip in enumerate(chips):
                copy(4 + q, (*chip, 1 - c), (x, y, c)).wait_recv()
            for cp in first + passed:
                cp.wait_send()
            own.wait()


class _Hosts:
    def __init__(self, *hosts):
        self.hosts = hosts
        self.relays = any(h.relays for h in hosts)
        self.n = sum(h.n for h in hosts)
        self.arrs = [a for h in hosts for a in h.arrs]

    def out_shape(self):
        return [sh for h in self.hosts for sh in h.out_shape()]

    def scratch(self):
        return [sc for h in self.hosts for sc in h.scratch()]

    def _each(self, ins, outs, sems):
        at = 0
        for k, h in enumerate(self.hosts):
            yield h, ins[at:at + h.n], outs[at:at + h.n], sems[3 * k:3 * k + 3]
            at += h.n

    def start(self, ins, outs, sems):
        for h, h_in, h_out, h_sems in self._each(ins, outs, sems):
            h.start(h_in, h_out, h_sems)

    def relay(self, ins, outs, sems):
        for h, h_in, h_out, h_sems in self._each(ins, outs, sems):
            if h.relays:
                h.relay(h_in, h_out, h_sems)

    def wait(self, ins, outs, sems, relayed=False):
        for h, h_in, h_out, h_sems in self._each(ins, outs, sems):
            h.wait(h_in, h_out, h_sems, relayed=relayed and h.relays)


def _hosted_call(host, body, *, name, grid, in_specs, out_specs, out_shape, scratch_shapes=(), compiler_params):
    out_specs = list(out_specs) if isinstance(out_specs, (list, tuple)) else [out_specs]
    out_shape = list(out_shape) if isinstance(out_shape, (list, tuple)) else [out_shape]
    if host is None:
        return pl.pallas_call(body, name=name, grid=grid, in_specs=in_specs, out_specs=out_specs,
                              out_shape=out_shape, scratch_shapes=list(scratch_shapes),
                              compiler_params=compiler_params)
    n_in, n_out, n_scr, k = len(in_specs), len(out_shape), len(scratch_shapes), host.n

    def wrapped(*refs):
        ins, h_in = refs[:n_in], refs[n_in:n_in + k]
        outs, h_out = refs[n_in + k:n_in + k + n_out], refs[n_in + k + n_out:n_in + 2 * k + n_out]
        scr, sems = refs[n_in + 2 * k + n_out:n_in + 2 * k + n_out + n_scr], refs[n_in + 2 * k + n_out + n_scr:]
        ids = [pl.program_id(a) for a in range(len(grid))]
        first = functools.reduce(jnp.logical_and, [i == 0 for i in ids])
        last = functools.reduce(jnp.logical_and, [i == g - 1 for i, g in zip(ids, grid)])
        steps = math.prod(grid)
        relay_at = (3 * steps) // 4 if host.relays and steps >= 8 else None

        @pl.when(first)
        def _():
            host.start(h_in, h_out, sems)

        if relay_at is not None:
            coords, rest = [], relay_at
            for g in reversed(grid):
                coords.append(rest % g)
                rest //= g

            @pl.when(functools.reduce(jnp.logical_and, [i == cd for i, cd in zip(ids, reversed(coords))]))
            def _():
                host.relay(h_in, h_out, sems)

        body(*ins, *outs, *scr)

        @pl.when(last)
        def _():
            host.wait(h_in, h_out, sems, relayed=relay_at is not None)

    hbm = pl.BlockSpec(memory_space=pl.ANY)
    call = pl.pallas_call(
        wrapped, name=name, grid=grid, in_specs=list(in_specs) + [hbm] * k, out_specs=out_specs + [hbm] * k,
        out_shape=out_shape + host.out_shape(), scratch_shapes=list(scratch_shapes) + host.scratch(),
        compiler_params=compiler_params)
    return lambda *args: call(*args, *host.arrs)


def _ffn_fwd_loss(xhat, g_in, b_in, wg, wu, wd, g_out, b_out, target, *, tm, name):
    t = xhat.shape[0]
    nj = N_DEV

    def body(x_ref, g_ref, b_ref, wg_ref, wu_ref, wd_ref, go_ref, bo_ref, tg_ref,
             dx_ref, sq_ref, gg_ref, gb_ref, hg_ref, hu_ref, xb, acc):
        i = pl.program_id(0)
        j = pl.program_id(1)

        @pl.when(j == 0)
        def _():
            xb[...] = (x_ref[...] * g_ref[...] + b_ref[...]).astype(bf16)
            acc[...] = jnp.zeros_like(acc)

        hg = jnp.dot(xb[...], wg_ref[...], preferred_element_type=f32)
        hu = jnp.dot(xb[...], wu_ref[...], preferred_element_type=f32)
        hg_ref[...] = hg.astype(bf16)
        hu_ref[...] = hu.astype(bf16)
        a = hg * _sigmoid_tanh(hg) * hu
        acc[...] += jnp.dot(a.astype(bf16), wd_ref[...], preferred_element_type=f32)

        @pl.when(j == nj - 1)
        def _():
            x = x_ref[...] * g_ref[...] + b_ref[...]
            xo, rstd = _ln_fwd_tile(ALPHA * x + 0.5 * acc[...])
            diff = xo * go_ref[...] + bo_ref[...] - tg_ref[...]
            sq = jnp.sum(diff * diff, axis=0, keepdims=True)
            dprev, gg, gb = _ln_bwd_tile(diff * (1.0 / D_MODEL), xo, rstd, go_ref[...])
            dx_ref[...] = dprev

            @pl.when(i == 0)
            def _():
                sq_ref[...] = sq
                gg_ref[...] = gg
                gb_ref[...] = gb

            @pl.when(i > 0)
            def _():
                sq_ref[...] += sq
                gg_ref[...] += gg
                gb_ref[...] += gb

    tok = pl.BlockSpec((tm, D_MODEL), lambda i, j: (i, 0))
    row = pl.BlockSpec((1, D_MODEL), lambda i, j: (0, 0))
    hid = pl.BlockSpec((tm, FF_TILE), lambda i, j: (i, j))
    return pl.pallas_call(
        body, name=name, grid=(t // tm, nj),
        in_specs=[tok, row, row,
                  pl.BlockSpec((None, D_MODEL, FF_TILE), lambda i, j: (j, 0, 0)),
                  pl.BlockSpec((None, D_MODEL, FF_TILE), lambda i, j: (j, 0, 0)),
                  pl.BlockSpec((None, FF_TILE, D_MODEL), lambda i, j: (j, 0, 0)), row, row, tok],
        out_specs=[tok, row, row, row, hid, hid],
        out_shape=[jax.ShapeDtypeStruct((t, D_MODEL), f32)] + [jax.ShapeDtypeStruct((1, D_MODEL), f32)] * 3
        + [jax.ShapeDtypeStruct((t, D_FF), bf16)] * 2,
        scratch_shapes=[pltpu.VMEM((tm, D_MODEL), bf16), pltpu.VMEM((tm, D_MODEL), f32)],
        compiler_params=_params(("arbitrary", "arbitrary")),
    )(xhat, g_in, b_in, wg, wu, wd, g_out, b_out, target)


def _ffn1_fwd_gathering(x, own, extra, *, tm, name):
    t = x.shape[0]
    n_i = t // tm
    n_arr = 3
    k_extra = extra.n
    ex = _Exchange(list(own), gather=True)
    ax, ay, ac = lax.axis_index("x"), lax.axis_index("y"), lax.axis_index("c")
    order = jnp.stack([4 * px + 2 * py + pc for px, py in ((ax, ay), (1 - ax, ay), (ax, 1 - ay), (1 - ax, 1 - ay))
                       for pc in (ac, 1 - ac)]).astype(jnp.int32)
    arrival = [None, (0, None), (1, 0), (4, None), (2, 1), (5, None), (3, 2), (6, None)]

    def body(order_ref, x_ref, *refs):
        w_in, e_in = refs[:n_arr], refs[n_arr:n_arr + k_extra]
        refs = refs[n_arr + k_extra:]
        xo_ref, rstd_ref, hg_ref, hu_ref = refs[:4]
        w_all, e_out = refs[4:4 + n_arr], refs[4 + n_arr:4 + n_arr + k_extra]
        acc, wgb, wub, wdb, fetch_sems, send_sems, recv_sems, local_sems = refs[4 + n_arr + k_extra:12 + n_arr + k_extra]
        e_sems = refs[12 + n_arr + k_extra:]
        bufs = (wgb, wub, wdb)
        s = pl.program_id(0)
        i = pl.program_id(1)
        per_array, sibling, chips, (x_, y_, c_) = ex.gather_copies(w_in, w_all, (send_sems, recv_sems, local_sems))

        def fetch(pos, slot):
            return [pltpu.make_async_copy(w_in[a] if pos == 0 else w_all[a].at[order_ref[pos]],
                                          bufs[a].at[slot], fetch_sems.at[n_arr * slot + a]) for a in range(n_arr)]

        def source_of(pos):
            chip = (x_, y_) if pos < 2 else chips[(pos - 2) // 2]
            return (*chip, c_ if pos % 2 == 0 else 1 - c_)

        @pl.when(jnp.logical_and(s == 0, i == 0))
        def _():
            for q in range(4):
                for first, _, own_copy, _ in per_array:
                    if q == 0:
                        own_copy.start()
                    first[q].start()
            for cp in fetch(0, 0):
                cp.start()
            for cp in fetch(0, 0):
                cp.wait()

        @pl.when(jnp.logical_and(s == N_DEV // 2, i == 0))
        def _():
            extra.start(e_in, e_out, e_sems)

        for pos in range(1, N_DEV):
            @pl.when(jnp.logical_and(s == pos - 1, i == n_i - 1))
            def _(pos=pos):
                sem, passes = arrival[pos]
                for _, passed, _, copy in per_array:
                    copy(sem, source_of(pos), (x_, y_, c_)).wait_recv()
                    if passes is not None:
                        passed[passes].start()
                for cp in fetch(pos, pos % 2):
                    cp.start()

            @pl.when(jnp.logical_and(s == pos, i == 0))
            def _(pos=pos):
                for cp in fetch(pos, pos % 2):
                    cp.wait()

        slot = s % 2
        xb = x_ref[...].astype(bf16)
        hg = jnp.dot(xb, wgb[slot], preferred_element_type=f32)
        hu = jnp.dot(xb, wub[slot], preferred_element_type=f32)
        hg_ref[...] = hg.astype(bf16)
        hu_ref[...] = hu.astype(bf16)
        a = hg * _sigmoid_tanh(hg) * hu
        part = jnp.dot(a.astype(bf16), wdb[slot], preferred_element_type=f32)

        @pl.when(s == 0)
        def _():
            acc[i] = part

        @pl.when(s > 0)
        def _():
            acc[i] += part

        @pl.when(s == N_DEV - 1)
        def _():
            xo, rstd = _ln_fwd_tile(ALPHA * x_ref[...] + 0.5 * acc[i])
            xo_ref[...] = xo
            rstd_ref[...] = rstd

        @pl.when(jnp.logical_and(s == N_DEV - 1, i == n_i - 1))
        def _():
            for first, passed, own_copy, _ in per_array:
                for cp in first + passed:
                    cp.wait_send()
                own_copy.wait()
            extra.wait(e_in, e_out, e_sems)

    hbm = pl.BlockSpec(memory_space=pl.ANY)
    last = N_DEV - 1
    tok_out = pl.BlockSpec((tm, D_MODEL), lambda s, i, o: (jnp.where(s == last, i, 0), 0))
    col_out = pl.BlockSpec((tm, 1), lambda s, i, o: (jnp.where(s == last, i, 0), 0))
    hid = pl.BlockSpec((tm, FF_TILE), lambda s, i, o: (i, o[s]))
    shard_shapes = [(N_DEV,) + w.shape for w in own]
    grid_spec = pltpu.PrefetchScalarGridSpec(
        num_scalar_prefetch=1, grid=(N_DEV, n_i),
        in_specs=[pl.BlockSpec((tm, D_MODEL), lambda s, i, o: (i, 0))] + [hbm] * (n_arr + k_extra),
        out_specs=[tok_out, col_out, hid, hid] + [hbm] * (n_arr + k_extra),
        scratch_shapes=[pltpu.VMEM((n_i, tm, D_MODEL), f32)]
        + [pltpu.VMEM((2,) + w.shape, bf16) for w in own]
        + [pltpu.SemaphoreType.DMA((2 * n_arr,))] + ex.scratch() + extra.scratch())
    res = pl.pallas_call(
        body, name=name, grid_spec=grid_spec,
        out_shape=[jax.ShapeDtypeStruct((t, D_MODEL), f32), jax.ShapeDtypeStruct((t, 1), f32),
                   jax.ShapeDtypeStruct((t, D_FF), bf16), jax.ShapeDtypeStruct((t, D_FF), bf16)]
        + [jax.ShapeDtypeStruct(sh, bf16) for sh in shard_shapes] + extra.out_shape(),
        compiler_params=_params(("arbitrary", "arbitrary")),
    )(order, x, *own, *extra.arrs)
    return res


def _ffn_bwd(dpre, hg, hu, wg, wu, wd, ln_in, *, tm, name, host=None):
    t = dpre.shape[0]
    nj = N_DEV
    with_ln = ln_in is not None

    def body(*refs):
        if with_ln:
            (dp_ref, hg_ref, hu_ref, wg_ref, wu_ref, wd_ref, xh_ref, rs_ref, g_ref,
             dx_ref, gg_ref, gb_ref, dhg_ref, dhu_ref, a_ref, dfb, acc) = refs
        else:
            (dp_ref, hg_ref, hu_ref, wg_ref, wu_ref, wd_ref,
             dx_ref, dhg_ref, dhu_ref, a_ref, dfb, acc) = refs
        i = pl.program_id(0)
        j = pl.program_id(1)

        @pl.when(j == 0)
        def _():
            dfb[...] = (0.5 * dp_ref[...]).astype(bf16)
            acc[...] = jnp.zeros_like(acc)

        da = lax.dot_general(dfb[...], wd_ref[...], _NT, preferred_element_type=f32)
        hgv = hg_ref[...].astype(f32)
        huv = hu_ref[...].astype(f32)
        sg = _sigmoid_tanh(hgv)
        silu = hgv * sg
        a_ref[...] = (silu * huv).astype(bf16)
        dhu = (da * silu).astype(bf16)
        dhg = (da * huv * (sg * (1.0 + hgv * (1.0 - sg)))).astype(bf16)
        dhg_ref[...] = dhg
        dhu_ref[...] = dhu
        acc[...] += (lax.dot_general(dhg, wg_ref[...], _NT, preferred_element_type=f32)
                     + lax.dot_general(dhu, wu_ref[...], _NT, preferred_element_type=f32))

        @pl.when(j == nj - 1)
        def _():
            dx = ALPHA * dp_ref[...] + acc[...]
            if with_ln:
                dprev, gg, gb = _ln_bwd_tile(dx, xh_ref[...], rs_ref[...], g_ref[...])
                dx_ref[...] = dprev

                @pl.when(i == 0)
                def _():
                    gg_ref[...] = gg
                    gb_ref[...] = gb

                @pl.when(i > 0)
                def _():
                    gg_ref[...] += gg
                    gb_ref[...] += gb
            else:
                dx_ref[...] = dx

    tok = pl.BlockSpec((tm, D_MODEL), lambda i, j: (i, 0), pipeline_mode=pl.Buffered(1))
    row = pl.BlockSpec((1, D_MODEL), lambda i, j: (0, 0))
    hid = pl.BlockSpec((tm, FF_TILE), lambda i, j: (i, j))
    in_specs = [tok, hid, hid,
                pl.BlockSpec((None, D_MODEL, FF_TILE), lambda i, j: (j, 0, 0)),
                pl.BlockSpec((None, D_MODEL, FF_TILE), lambda i, j: (j, 0, 0)),
                pl.BlockSpec((None, FF_TILE, D_MODEL), lambda i, j: (j, 0, 0))]
    args = [dpre, hg, hu, wg, wu, wd]
    out_specs = [tok]
    out_shape = [jax.ShapeDtypeStruct((t, D_MODEL), f32)]
    if with_ln:
        in_specs += [tok, pl.BlockSpec((tm, 1), lambda i, j: (i, 0)), row]
        args += list(ln_in)
        out_specs += [row, row]
        out_shape += [jax.ShapeDtypeStruct((1, D_MODEL), f32)] * 2
    out_specs += [hid, hid, hid]
    out_shape += [jax.ShapeDtypeStruct((t, D_FF), bf16)] * 3
    return _hosted_call(
        host, body, name=name, grid=(t // tm, nj), in_specs=in_specs, out_specs=out_specs, out_shape=out_shape,
        scratch_shapes=[pltpu.VMEM((tm, D_MODEL), bf16), pltpu.VMEM((tm, D_MODEL), f32)],
        compiler_params=_params(("arbitrary", "arbitrary")),
    )(*args)


def _ffn_bwd_act(dpre, hg, hu, wd, *, tm, name, host=None):
    t = dpre.shape[0]

    def body(dp_ref, hg_ref, hu_ref, wd_ref, dhg_ref, dhu_ref, a_ref, dfb):
        @pl.when(pl.program_id(1) == 0)
        def _():
            dfb[...] = (0.5 * dp_ref[...]).astype(bf16)

        da = lax.dot_general(dfb[...], wd_ref[...], _NT, preferred_element_type=f32)
        hgv = hg_ref[...].astype(f32)
        huv = hu_ref[...].astype(f32)
        sg = _sigmoid_tanh(hgv)
        silu = hgv * sg
        a_ref[...] = (silu * huv).astype(bf16)
        dhu_ref[...] = (da * silu).astype(bf16)
        dhg_ref[...] = (da * huv * (sg * (1.0 + hgv * (1.0 - sg)))).astype(bf16)

    hid = pl.BlockSpec((tm, FF_TILE), lambda i, j: (i, j))
    return _hosted_call(
        host, body, name=name, grid=(t // tm, N_DEV),
        in_specs=[pl.BlockSpec((tm, D_MODEL), lambda i, j: (i, 0)), hid, hid,
                  pl.BlockSpec((None, FF_TILE, D_MODEL), lambda i, j: (j, 0, 0))],
        out_specs=[hid, hid, hid], out_shape=[jax.ShapeDtypeStruct((t, D_FF), bf16)] * 3,
        scratch_shapes=[pltpu.VMEM((tm, D_MODEL), bf16)],
        compiler_params=_params(("arbitrary", "arbitrary")),
    )(dpre, hg, hu, wd)


def _ffn_bwd_dx(dpre, dhg, dhu, wg, wu, *, tm, name, host=None):
    t = dpre.shape[0]
    nj = N_DEV

    def body(dp_ref, dhg_ref, dhu_ref, wg_ref, wu_ref, dx_ref, acc):
        j = pl.program_id(1)

        @pl.when(j == 0)
        def _():
            acc[...] = jnp.zeros_like(acc)

        acc[...] += (lax.dot_general(dhg_ref[...], wg_ref[...], _NT, preferred_element_type=f32)
                     + lax.dot_general(dhu_ref[...], wu_ref[...], _NT, preferred_element_type=f32))

        @pl.when(j == nj - 1)
        def _():
            dx_ref[...] = ALPHA * dp_ref[...] + acc[...]

    tok = pl.BlockSpec((tm, D_MODEL), lambda i, j: (i, 0))
    hid = pl.BlockSpec((tm, FF_TILE), lambda i, j: (i, j))
    wspec = pl.BlockSpec((None, D_MODEL, FF_TILE), lambda i, j: (j, 0, 0))
    return _hosted_call(
        host, body, name=name, grid=(t // tm, nj), in_specs=[tok, hid, hid, wspec, wspec],
        out_specs=[tok], out_shape=[jax.ShapeDtypeStruct((t, D_MODEL), f32)],
        scratch_shapes=[pltpu.VMEM((tm, D_MODEL), f32)],
        compiler_params=_params(("arbitrary", "arbitrary")),
    )(dpre, dhg, dhu, wg, wu)


def _mm(a, b, *, mode, out_dtype, tm, tn, tk, name, affine=None, a_cols=None, b_cols=None,
        b_blocked=False, out_blocked=False, out_scale=None):
    if mode == "nn":
        m_full, k_full = a.shape
        m_dim, k_dim = (m_full, a_cols[1]) if a_cols else (m_full, k_full)
    else:
        k_dim, m_full = a.shape
        m_dim = a_cols[1] if a_cols else m_full
    a_off = a_cols[0] if a_cols else 0
    if b_blocked:
        n_dim = b.shape[0] * b.shape[2]
        assert b.shape[2] == tn
    else:
        n_dim = b_cols[1] if b_cols else b.shape[1]
    b_off = b_cols[0] if b_cols else 0
    assert m_dim % tm == 0 and n_dim % tn == 0 and k_dim % tk == 0, (name, m_dim, n_dim, k_dim)
    nk = k_dim // tk

    def body(*refs):
        if affine is not None:
            a_ref, g_ref, s_ref, b_ref, o_ref, acc = refs
        else:
            a_ref, b_ref, o_ref, acc = refs
        k = pl.program_id(2)

        @pl.when(k == 0)
        def _():
            acc[...] = jnp.zeros_like(acc)

        av = a_ref[...]
        if affine is not None:
            av = av * g_ref[...] + s_ref[...]
        av = av.astype(bf16)
        bv = b_ref[...].astype(bf16)
        if mode == "nn":
            acc[...] += jnp.dot(av, bv, preferred_element_type=f32)
        else:
            acc[...] += lax.dot_general(av, bv, _TN, preferred_element_type=f32)

        @pl.when(k == nk - 1)
        def _():
            res = acc[...] if out_scale is None else acc[...] * out_scale
            o_ref[...] = res.astype(out_dtype)

    if mode == "nn":
        a_spec = pl.BlockSpec((tm, tk), lambda i, j, k: (i, k + a_off))
        aff_spec = pl.BlockSpec((1, tk), lambda i, j, k: (0, k + a_off))
    else:
        a_spec = pl.BlockSpec((tk, tm), lambda i, j, k: (k, i + a_off))
        aff_spec = pl.BlockSpec((1, tm), lambda i, j, k: (0, i + a_off))
    if b_blocked:
        b_spec = pl.BlockSpec((None, tk, tn), lambda i, j, k: (j, k, 0))
    else:
        b_spec = pl.BlockSpec((tk, tn), lambda i, j, k: (k, j + b_off))
    if out_blocked:
        o_spec = pl.BlockSpec((None, tm, tn), lambda i, j, k: (j, i, 0))
        o_shape = jax.ShapeDtypeStruct((n_dim // tn, m_dim, tn), out_dtype)
    else:
        o_spec = pl.BlockSpec((tm, tn), lambda i, j, k: (i, j))
        o_shape = jax.ShapeDtypeStruct((m_dim, n_dim), out_dtype)
    in_specs = [a_spec] + ([aff_spec, aff_spec] if affine is not None else []) + [b_spec]
    args = [a] + (list(affine) if affine is not None else []) + [b]
    return pl.pallas_call(
        body, name=name, grid=(m_dim // tm, n_dim // tn, nk), in_specs=in_specs, out_specs=o_spec,
        out_shape=o_shape, scratch_shapes=[pltpu.VMEM((tm, tn), f32)],
        compiler_params=_params(("arbitrary", "arbitrary", "arbitrary")),
    )(*args)


def _mm_tn(a, b, *, out_dtype, tm, mb, tn, nb, tk, name, affine=None, out_blocked=False, out_scale=None,
           pair=False, host=None):
    k_dim, m_dim = a.shape
    multi_b = isinstance(b, (list, tuple))
    b_list = list(b) if multi_b else [b]
    n_dim = nb * tn if multi_b else b.shape[1]
    assert m_dim % (mb * tm) == 0 and n_dim % (nb * tn) == 0 and k_dim % tk == 0, (name, m_dim, n_dim, k_dim)
    nk = k_dim // tk
    grid = (m_dim // (mb * tm), n_dim // (nb * tn), nk)
    if pair:
        assert mb * nb == 4 and grid[0] * grid[1] == 2 and out_dtype == bf16, name

    def body(*refs):
        if pair:
            refs, (acc, send_buf, recv_buf, keep, send_sems, recv_sems) = refs[:-6], refs[-6:]
        else:
            refs, acc = refs[:-1], refs[-1]
        a_ref, o_ref = refs[0], refs[-1]
        if affine is not None:
            g_ref, s_ref = refs[1:3]
        b_refs = refs[3 if affine is not None else 1:-1]
        k = pl.program_id(2)

        @pl.when(k == 0)
        def _():
            acc[...] = jnp.zeros_like(acc)

        av = a_ref[...]
        if affine is not None:
            av = av * g_ref[...] + s_ref[...]
        av = av.astype(bf16)
        if multi_b:
            pieces = [r[...].astype(bf16) for r in b_refs]
        else:
            bv = b_refs[0][...].astype(bf16)
            pieces = [bv[:, jn * tn:(jn + 1) * tn] for jn in range(nb)]
        for im in range(mb):
            a_t = av[:, im * tm:(im + 1) * tm].T
            for jn in range(nb):
                acc[im * nb + jn] += jnp.dot(a_t, pieces[jn], preferred_element_type=f32)

        def scaled(v):
            return v if out_scale is None else v * out_scale

        @pl.when(k == nk - 1)
        def _():
            if pair:
                x, y, c = lax.axis_index("x"), lax.axis_index("y"), lax.axis_index("c")
                window = pl.program_id(0) + pl.program_id(1)

                def swap(w, cc):
                    return pltpu.make_async_remote_copy(
                        src_ref=send_buf.at[w, cc], dst_ref=recv_buf.at[w, cc],
                        send_sem=send_sems.at[2 * w + cc], recv_sem=recv_sems.at[2 * w + cc],
                        device_id=(x, y, 1 - c), device_id_type=MESH_T)

                for w in range(2):
                    @pl.when(window == w)
                    def _(w=w):
                        for cc in range(2):
                            send_buf[w, cc] = scaled(acc[2 * cc + 1 - c]).astype(bf16)
                            swap(w, cc).start()
                            if w == 0:
                                keep[cc] = scaled(acc[2 * cc + c])

                @pl.when(window == 1)
                def _():
                    for w in range(2):
                        for cc in range(2):
                            swap(w, cc).wait_recv()
                            mine = keep[cc] if w == 0 else scaled(acc[2 * cc + c])
                            o_ref[2 * w + cc] = (mine + recv_buf[w, cc].astype(f32)).astype(bf16)
                    for w in range(2):
                        for cc in range(2):
                            swap(w, cc).wait_send()
                return
            for im in range(mb):
                for jn in range(nb):
                    res = scaled(acc[im * nb + jn])
                    if out_blocked:
                        o_ref[jn, im * tm:(im + 1) * tm, :] = res.astype(out_dtype)
                    else:
                        o_ref[im * tm:(im + 1) * tm, jn * tn:(jn + 1) * tn] = res.astype(out_dtype)

    a_spec = pl.BlockSpec((tk, mb * tm), lambda i, j, k: (k, i))
    aff_spec = pl.BlockSpec((1, mb * tm), lambda i, j, k: (0, i))
    if multi_b:
        b_specs = [pl.BlockSpec((tk, tn), lambda i, j, k: (k, 0))] * nb
    else:
        b_specs = [pl.BlockSpec((tk, nb * tn), lambda i, j, k: (k, j))]
    scratch = [pltpu.VMEM((mb * nb, tm, tn), f32)]
    if pair:
        o_spec = pl.BlockSpec((4, tm, tn), lambda i, j, k: (0, 0, 0))
        o_shape = jax.ShapeDtypeStruct((4, tm, tn), out_dtype)
        scratch += [pltpu.VMEM((2, 2, tm, tn), bf16), pltpu.VMEM((2, 2, tm, tn), bf16), pltpu.VMEM((2, tm, tn), f32),
                    pltpu.SemaphoreType.DMA((4,)), pltpu.SemaphoreType.DMA((4,))]
    elif out_blocked:
        o_spec = pl.BlockSpec((nb, mb * tm, tn), lambda i, j, k: (j, i, 0))
        o_shape = jax.ShapeDtypeStruct((n_dim // tn, m_dim, tn), out_dtype)
    else:
        o_spec = pl.BlockSpec((mb * tm, nb * tn), lambda i, j, k: (i, j))
        o_shape = jax.ShapeDtypeStruct((m_dim, n_dim), out_dtype)
    in_specs = [a_spec] + ([aff_spec, aff_spec] if affine is not None else []) + b_specs
    args = [a] + (list(affine) if affine is not None else []) + b_list
    res = _hosted_call(
        host, body, name=name, grid=grid, in_specs=in_specs, out_specs=o_spec, out_shape=o_shape,
        scratch_shapes=scratch, compiler_params=_params(("arbitrary", "arbitrary", "arbitrary")),
    )(*args)
    return res[0] if host is None else res


def _in_proj(xhat, g, b, w_in, *, tm, name):
    t = xhat.shape[0]
    n_qkv, n_l = 3 * FOX_W, 2 * LRU_W

    def body(x_ref, g_ref, b_ref, w_ref, qkv_ref, zl_ref, zfg_ref):
        xb = (x_ref[...] * g_ref[...] + b_ref[...]).astype(bf16)
        qkv_ref[...] = jnp.dot(xb, w_ref[:, :n_qkv], preferred_element_type=f32).astype(bf16)
        zl_ref[...] = jnp.dot(xb, w_ref[:, n_qkv:n_qkv + n_l], preferred_element_type=f32)
        zfg_ref[...] = jnp.dot(xb, w_ref[:, n_qkv + n_l:], preferred_element_type=f32)

    row = pl.BlockSpec((1, D_MODEL), lambda i: (0, 0))
    return pl.pallas_call(
        body, name=name, grid=(t // tm,),
        in_specs=[pl.BlockSpec((tm, D_MODEL), lambda i: (i, 0)), row, row,
                  pl.BlockSpec(w_in.shape, lambda i: (0, 0))],
        out_specs=[pl.BlockSpec((tm, n_qkv), lambda i: (i, 0)), pl.BlockSpec((tm, n_l), lambda i: (i, 0)),
                   pl.BlockSpec((tm, LANES), lambda i: (i, 0))],
        out_shape=[jax.ShapeDtypeStruct((t, n_qkv), bf16), jax.ShapeDtypeStruct((t, n_l), f32),
                   jax.ShapeDtypeStruct((t, LANES), f32)],
        compiler_params=_params(("arbitrary",)),
    )(xhat, g, b, w_in)


def _mmln(pairs, *, tm, name, resid=None, resid_scale=1.0, epi=None, ln=None, n_out=D_MODEL):
    t = pairs[0][0].shape[0]
    n_pairs = len(pairs)
    n_resid = 0 if resid is None else len(resid) - 1

    def body(*refs):
        pos = 0
        val = None
        for p in range(n_pairs):
            a_ref, b_ref = refs[pos], refs[pos + 1]
            pos += 2
            av = a_ref[...].astype(bf16)
            bv = b_ref[...].astype(bf16)
            if pairs[p][6] == "nn":
                term = jnp.dot(av, bv, preferred_element_type=f32)
            else:
                term = lax.dot_general(av, bv, _NT, preferred_element_type=f32)
            val = term if val is None else val + term
        if resid is not None:
            if resid[0] == "plain":
                r = refs[pos][...]
            else:
                r = refs[pos][...] * refs[pos + 1][...] + refs[pos + 2][...]
            pos += n_resid
            val = val + resid_scale * r
        if epi is None:
            o_ref = refs[pos]
            o_ref[...] = val.astype(o_ref.dtype)
        elif epi == "ln_fwd":
            xo, rstd = _ln_fwd_tile(val)
            refs[pos][...] = xo
            refs[pos + 1][...] = rstd
        else:
            xh_ref, rs_ref, g_ref, dx_ref, gg_ref, gb_ref = refs[pos:pos + 6]
            dprev, gg, gb = _ln_bwd_tile(val, xh_ref[...], rs_ref[...], g_ref[...])
            dx_ref[...] = dprev
            i = pl.program_id(0)

            @pl.when(i == 0)
            def _():
                gg_ref[...] = gg
                gb_ref[...] = gb

            @pl.when(i > 0)
            def _():
                gg_ref[...] += gg
                gb_ref[...] += gb

    in_specs, args = [], []
    for (a, acb, aw, b, bcb, bw, mode) in pairs:
        in_specs.append(pl.BlockSpec((tm, aw), lambda i, acb=acb: (i, acb)))
        args.append(a)
        if mode == "nn":
            in_specs.append(pl.BlockSpec((aw, n_out), lambda i, bcb=bcb: (bcb, 0)))
        else:
            in_specs.append(pl.BlockSpec((n_out, bw), lambda i, bcb=bcb: (0, bcb)))
        args.append(b)
    tok = pl.BlockSpec((tm, n_out), lambda i: (i, 0))
    row = pl.BlockSpec((1, n_out), lambda i: (0, 0))
    col = pl.BlockSpec((tm, 1), lambda i: (i, 0))
    if resid is not None:
        in_specs += [tok] if resid[0] == "plain" else [tok, row, row]
        args += list(resid[1:])
    if epi is None:
        out_specs, out_shape = tok, jax.ShapeDtypeStruct((t, n_out), f32)
    elif epi == "ln_fwd":
        out_specs = [tok, col]
        out_shape = [jax.ShapeDtypeStruct((t, n_out), f32), jax.ShapeDtypeStruct((t, 1), f32)]
    else:
        in_specs += [tok, col, row]
        args += list(ln)
        out_specs = [tok, row, row]
        out_shape = [jax.ShapeDtypeStruct((t, n_out), f32)] + [jax.ShapeDtypeStruct((1, n_out), f32)] * 2
    return pl.pallas_call(
        body, name=name, grid=(t // tm,), in_specs=in_specs, out_specs=out_specs, out_shape=out_shape,
        compiler_params=_params(("arbitrary",)),
    )(*args)


CUM_TILE = 256


def _tri(n, lower):
    r = lax.broadcasted_iota(jnp.int32, (n, n), 0)
    c = lax.broadcasted_iota(jnp.int32, (n, n), 1)
    return jnp.where((r >= c) if lower else (r <= c), 1.0, 0.0).astype(f32)


def _cum_fwd(zfg, bfg, *, name):
    t = zfg.shape[0]

    def body(z_ref, b_ref, o_ref, carry):
        @pl.when(pl.program_id(0) == 0)
        def _():
            carry[...] = jnp.zeros_like(carry)

        ls = -_softplus(-(z_ref[...] + b_ref[...]))
        c = jnp.dot(_tri(CUM_TILE, True), ls, preferred_element_type=f32,
                    precision=lax.Precision.HIGHEST) + carry[...]
        o_ref[...] = c
        carry[...] = c[CUM_TILE - 1:CUM_TILE, :]

    blk = pl.BlockSpec((CUM_TILE, LANES), lambda i: (i, 0))
    return pl.pallas_call(
        body, name=name, grid=(t // CUM_TILE,),
        in_specs=[blk, pl.BlockSpec((1, LANES), lambda i: (0, 0))], out_specs=blk,
        out_shape=jax.ShapeDtypeStruct((t, LANES), f32), scratch_shapes=[pltpu.VMEM((1, LANES), f32)],
        compiler_params=_params(("arbitrary",)),
    )(zfg, bfg)


def _cum_bwd(dcum_q, dcum_k, zfg, bfg, *, name):
    t = zfg.shape[0]
    n = t // CUM_TILE

    def body(d_ref, d2_ref, z_ref, b_ref, o_ref, s_ref, carry):
        i = pl.program_id(0)

        @pl.when(i == 0)
        def _():
            carry[...] = jnp.zeros_like(carry)

        dls = jnp.dot(_tri(CUM_TILE, False), d_ref[...] + d2_ref[...], preferred_element_type=f32,
                      precision=lax.Precision.HIGHEST) + carry[...]
        carry[...] = dls[0:1, :]
        lane = lax.broadcasted_iota(jnp.int32, (CUM_TILE, LANES), 1)
        dfg = jnp.where(lane < HEADS, dls * _sigmoid(-(z_ref[...] + b_ref[...])), 0.0)
        o_ref[...] = dfg
        tot = jnp.sum(dfg, axis=0, keepdims=True)

        @pl.when(i == 0)
        def _():
            s_ref[...] = tot

        @pl.when(i > 0)
        def _():
            s_ref[...] += tot

    blk = pl.BlockSpec((CUM_TILE, LANES), lambda i: (n - 1 - i, 0))
    row = pl.BlockSpec((1, LANES), lambda i: (0, 0))
    return pl.pallas_call(
        body, name=name, grid=(n,), in_specs=[blk, blk, blk, row], out_specs=[blk, row],
        out_shape=[jax.ShapeDtypeStruct((t, LANES), f32), jax.ShapeDtypeStruct((1, LANES), f32)],
        scratch_shapes=[pltpu.VMEM((1, LANES), f32)],
        compiler_params=_params(("arbitrary",)),
    )(dcum_q, dcum_k, zfg, bfg)


ATT_TILE = 512


def _causal(i, j, transposed):
    r = lax.broadcasted_iota(jnp.int32, (ATT_TILE, ATT_TILE), 0)
    c = lax.broadcasted_iota(jnp.int32, (ATT_TILE, ATT_TILE), 1)
    if transposed:
        return (c + i * ATT_TILE) >= (r + j * ATT_TILE)
    return (r + i * ATT_TILE) >= (c + j * ATT_TILE)


ATT_W = HEADS * LANES


def _data_lane(h):
    return HEAD_D * (h % 2)


def _extra_lane(h):
    return HEAD_D - _data_lane(h)


def _split3(x):
    hi = x.astype(bf16)
    rest = x - hi.astype(f32)
    mid = rest.astype(bf16)
    lo = (rest - mid.astype(f32)).astype(bf16)
    return hi, mid, lo


def _three_pieces(x):
    hi, mid, lo = (p.astype(f32) for p in _split3(x))
    return (hi + pltpu.roll(mid, HEADS, axis=1) + pltpu.roll(lo, 2 * HEADS, axis=1)).astype(bf16)


def _move(h, first):
    r = lax.broadcasted_iota(jnp.int32, (LANES, LANES), 0)
    c = lax.broadcasted_iota(jnp.int32, (LANES, LANES), 1)
    hit = functools.reduce(jnp.logical_or, [jnp.logical_and(r == HEADS * q + h, c == first + q) for q in range(3)])
    return jnp.where(hit, 1.0, 0.0).astype(bf16)


def _ones_from(first, rows):
    lane = lax.broadcasted_iota(jnp.int32, (rows, LANES), 1)
    return jnp.where(jnp.logical_and(lane >= first, lane < first + 3), 1.0, 0.0)


def _own_lanes(h, rows):
    lane = lax.broadcasted_iota(jnp.int32, (rows, LANES), 1)
    return (lane < HEAD_D) if h % 2 == 0 else (lane >= HEAD_D)


def _head_values(x):
    lane = lax.broadcasted_iota(jnp.int32, x.shape, 1)
    return jnp.where(lane < HEADS, x, 0.0)


def _attn_prep_fwd(qkv, cum, *, tm, name):
    t = qkv.shape[0]

    def body(q_ref, k_ref, v_ref, c_ref, qa_ref, ka_ref, va_ref):
        c3 = _three_pieces(_head_values(c_ref[...]))
        ones = jnp.ones((tm, LANES), bf16)
        for h in range(HEADS):
            pair = slice(LANES * (h // 2), LANES * (h // 2 + 1))
            hs = slice(LANES * h, LANES * (h + 1))
            base, own = _extra_lane(h), _own_lanes(h, tm)
            eq = jnp.dot(c3, _move(h, base), preferred_element_type=f32) + _ones_from(base + 3, tm)
            ek = _ones_from(base, tm) - jnp.dot(c3, _move(h, base + 3), preferred_element_type=f32)
            qa_ref[:, hs] = jnp.where(own, q_ref[:, pair] * ATT_SCALE, eq.astype(bf16))
            ka_ref[:, hs] = jnp.where(own, k_ref[:, pair], ek.astype(bf16))
            va_ref[:, hs] = jnp.where(own, v_ref[:, pair], ones)

    wide = pl.BlockSpec((tm, ATT_W), lambda i: (i, 0))
    out = jax.ShapeDtypeStruct((t, ATT_W), bf16)
    return pl.pallas_call(
        body, name=name, grid=(t // tm,),
        in_specs=[pl.BlockSpec((tm, FOX_W), lambda i: (i, 0)), pl.BlockSpec((tm, FOX_W), lambda i: (i, 1)),
                  pl.BlockSpec((tm, FOX_W), lambda i: (i, 2)), pl.BlockSpec((tm, LANES), lambda i: (i, 0))],
        out_specs=[wide] * 3, out_shape=[out] * 3, compiler_params=_params(("arbitrary",)),
    )(qkv, qkv, qkv, cum)


def _attn_prep_bwd(qkv, cum, lse, dmix, o, *, tm, name):
    t = qkv.shape[0]

    def body(q_ref, c_ref, l_ref, do_ref, o_ref, qa_ref, da_ref):
        b3 = _three_pieces(_head_values(c_ref[...] - l_ref[...]))
        r = lax.broadcasted_iota(jnp.int32, (FOX_W, LANES), 0)
        c = lax.broadcasted_iota(jnp.int32, (FOX_W, LANES), 1)
        per_head = jnp.where(r // HEAD_D == c, 1.0, 0.0).astype(bf16)
        delta = sum(jnp.dot(p, per_head, preferred_element_type=f32) for p in _split3(do_ref[...] * o_ref[...]))
        d3 = _three_pieces(delta)
        for h in range(HEADS):
            pair = slice(LANES * (h // 2), LANES * (h // 2 + 1))
            hs = slice(LANES * h, LANES * (h + 1))
            base, own = _extra_lane(h), _own_lanes(h, tm)
            eq = jnp.dot(b3, _move(h, base), preferred_element_type=f32) + _ones_from(base + 3, tm)
            ed = -jnp.dot(d3, _move(h, base), preferred_element_type=f32)
            qa_ref[:, hs] = jnp.where(own, q_ref[:, pair] * ATT_SCALE, eq.astype(bf16))
            da_ref[:, hs] = jnp.where(own, do_ref[:, pair].astype(bf16), ed.astype(bf16))

    wide = pl.BlockSpec((tm, ATT_W), lambda i: (i, 0))
    half = pl.BlockSpec((tm, FOX_W), lambda i: (i, 0))
    col = pl.BlockSpec((tm, LANES), lambda i: (i, 0))
    out = jax.ShapeDtypeStruct((t, ATT_W), bf16)
    return pl.pallas_call(
        body, name=name, grid=(t // tm,), in_specs=[half, col, col, half, half],
        out_specs=[wide] * 2, out_shape=[out] * 2, compiler_params=_params(("arbitrary",)),
    )(qkv, cum, lse, dmix, o)


def _attn_fwd2(q_aug, k_aug, v_aug, *, name, host=None):
    t = q_aug.shape[0]
    n = t // ATT_TILE
    tq = ATT_TILE

    def body(q_ref, k_ref, v_ref, o_ref, lse_ref, acc, m_s):
        i = pl.program_id(0)
        j = pl.program_id(1)

        @pl.when(j == 0)
        def _():
            acc[...] = jnp.zeros_like(acc)
            m_s[...] = jnp.full_like(m_s, NEG_BIG)

        def block(masked):
            mask = _causal(i, j, False) if masked else None
            for h in range(HEADS):
                hs = slice(LANES * h, LANES * (h + 1))
                s = lax.dot_general(q_ref[:, hs], k_ref[:, hs], _NT, preferred_element_type=f32)
                if masked:
                    s = jnp.where(mask, s, NEG_BIG)
                blocks = [s[:, LANES * b:LANES * (b + 1)] for b in range(tq // LANES)]
                m_old = m_s[h]
                m_new = jnp.maximum(m_old, jnp.broadcast_to(
                    jnp.max(functools.reduce(jnp.maximum, blocks), axis=-1, keepdims=True), (tq, LANES)))
                p = jnp.concatenate([jnp.exp(b - m_new) for b in blocks], axis=1).astype(bf16)
                acc[h] = jnp.exp(m_old - m_new) * acc[h] + jnp.dot(p, v_ref[:, hs], preferred_element_type=f32)
                m_s[h] = m_new

        @pl.when(j < i)
        def _():
            block(False)

        @pl.when(j == i)
        def _():
            block(True)
            lse_ref[...] = jnp.zeros_like(lse_ref)
            for h in range(HEADS):
                a = acc[h]
                l = a[:, _extra_lane(h):_extra_lane(h) + 1]
                o_ref[:, HEAD_D * h:HEAD_D * (h + 1)] = a[:, _data_lane(h):_data_lane(h) + HEAD_D] / l
                lse_ref[:, h:h + 1] = m_s[h][:, 0:1] + jnp.log(l)

    kv = pl.BlockSpec((tq, ATT_W), lambda i, j: (jnp.minimum(i, j), 0))
    return _hosted_call(
        host, body, name=name, grid=(n, n),
        in_specs=[pl.BlockSpec((tq, ATT_W), lambda i, j: (i, 0)), kv, kv],
        out_specs=[pl.BlockSpec((tq, FOX_W), lambda i, j: (i, 0)), pl.BlockSpec((tq, LANES), lambda i, j: (i, 0))],
        out_shape=[jax.ShapeDtypeStruct((t, FOX_W), f32), jax.ShapeDtypeStruct((t, LANES), f32)],
        scratch_shapes=[pltpu.VMEM((HEADS, tq, LANES), f32), pltpu.VMEM((HEADS, tq, LANES), f32)],
        compiler_params=_params(("arbitrary", "arbitrary")),
    )(q_aug, k_aug, v_aug)


def _attn_bwd(qb_aug, k_aug, v_aug, do_aug, *, name, host=None):
    t = qb_aug.shape[0]
    n = t // ATT_TILE
    tk = ATT_TILE

    def body(q_ref, k_ref, v_ref, do_ref, dq_ref, dcq_ref, dk_ref, dv_ref, dck_ref, dk_acc, dv_acc, dq_all):
        j = pl.program_id(0)
        i = pl.program_id(1)

        @pl.when(jnp.logical_and(i == 0, j == 0))
        def _():
            dq_all[...] = jnp.zeros_like(dq_all)

        @pl.when(i == 0)
        def _():
            dk_acc[...] = jnp.zeros_like(dk_acc)
            dv_acc[...] = jnp.zeros_like(dv_acc)

        def block(masked):
            mask = _causal(i, j, True) if masked else None
            for h in range(HEADS):
                hs = slice(LANES * h, LANES * (h + 1))
                qh = q_ref[:, hs]
                doh = do_ref[:, hs]
                kh = k_ref[:, hs]
                s_t = lax.dot_general(kh, qh, _NT, preferred_element_type=f32)
                if masked:
                    s_t = jnp.where(mask, s_t, NEG_BIG)
                p_t = jnp.exp(s_t)
                dv_acc[h] += jnp.dot(p_t.astype(bf16), doh, preferred_element_type=f32)
                dp_t = lax.dot_general(v_ref[:, hs], doh, _NT, preferred_element_type=f32)
                ds_t = (p_t * dp_t).astype(bf16)
                dk_acc[h] += jnp.dot(ds_t, qh, preferred_element_type=f32)
                dq_all[i, h] += lax.dot_general(ds_t, kh, _TN, preferred_element_type=f32)

        @pl.when(i > j)
        def _():
            block(False)

        @pl.when(i == j)
        def _():
            block(True)
            dcq_ref[...] = jnp.zeros_like(dcq_ref)
            for h in range(HEADS):
                a = dq_all[j, h]
                dq_ref[:, HEAD_D * h:HEAD_D * (h + 1)] = (
                    a[:, _data_lane(h):_data_lane(h) + HEAD_D] * ATT_SCALE).astype(bf16)
                dcq_ref[:, h:h + 1] = a[:, _extra_lane(h):_extra_lane(h) + 1]

        @pl.when(i == n - 1)
        def _():
            dck_ref[...] = jnp.zeros_like(dck_ref)
            for h in range(HEADS):
                a = dk_acc[h]
                cols = slice(_data_lane(h), _data_lane(h) + HEAD_D)
                dk_ref[:, HEAD_D * h:HEAD_D * (h + 1)] = a[:, cols].astype(bf16)
                dv_ref[:, HEAD_D * h:HEAD_D * (h + 1)] = dv_acc[h][:, cols].astype(bf16)
                dck_ref[:, h:h + 1] = -a[:, _extra_lane(h) + 3:_extra_lane(h) + 4]

    own = pl.BlockSpec((tk, ATT_W), lambda j, i: (j, 0))
    qs = pl.BlockSpec((tk, ATT_W), lambda j, i: (jnp.maximum(i, j), 0))
    half = pl.BlockSpec((tk, FOX_W), lambda j, i: (j, 0))
    col = pl.BlockSpec((tk, LANES), lambda j, i: (j, 0))
    return _hosted_call(
        host, body, name=name, grid=(n, n), in_specs=[qs, own, own, qs],
        out_specs=[half, col, half, half, col],
        out_shape=[jax.ShapeDtypeStruct((t, FOX_W), bf16), jax.ShapeDtypeStruct((t, LANES), f32),
                   jax.ShapeDtypeStruct((t, FOX_W), bf16), jax.ShapeDtypeStruct((t, FOX_W), bf16),
                   jax.ShapeDtypeStruct((t, LANES), f32)],
        scratch_shapes=[pltpu.VMEM((HEADS, tk, LANES), f32), pltpu.VMEM((HEADS, tk, LANES), f32),
                        pltpu.VMEM((n, HEADS, tk, LANES), f32)],
        compiler_params=_params(("arbitrary", "arbitrary")),
    )(qb_aug, k_aug, v_aug, do_aug)


LRU_CHUNK = 64
LRU_G = 256
SUB = 8


def _row_ids(n):
    return lax.broadcasted_iota(jnp.int32, (n, LRU_G), 0)


def _shift_rows_down(ext, s):
    return pltpu.roll(ext, s, axis=0)[SUB:, :]


def _shift_rows_up(ext, s, n):
    return pltpu.roll(ext, ext.shape[0] - s, axis=0)[:n, :]


def _lru_gates(u, wa_ref, ba_ref, wx_ref, bx_ref, sp):
    ub = u.astype(bf16)
    r = _sigmoid(jnp.dot(ub, wa_ref[...], preferred_element_type=f32) + ba_ref[...])
    gi = _sigmoid(jnp.dot(ub, wx_ref[...], preferred_element_type=f32) + bx_ref[...])
    log_a = -LRU_C * r * sp
    a = jnp.exp(log_a)
    s = jnp.sqrt(_one_minus_exp(2.0 * log_a))
    return r, gi, a, s


def _conv_window(lx_ref, r0, ci):
    cur = lx_ref[pl.ds(r0, LRU_CHUNK), :]
    p0 = pl.multiple_of(jnp.maximum(r0 - SUB, 0), SUB)
    prev = jnp.where(ci > 0, lx_ref[pl.ds(p0, SUB), :], 0.0)
    return cur, jnp.concatenate([prev, cur], axis=0)


def _lru_fwd(zl, conv_w, conv_b, wa, ba, wx, bx, lam, *, name, host=None):
    t = zl.shape[0]
    n_chunk = t // LRU_CHUNK

    def body(lx_ref, lg_ref, cw_ref, cb_ref, wa_ref, ba_ref, wx_ref, bx_ref, lam_ref, u_ref, h_ref, y_ref):
        sp = _softplus(-lam_ref[...])
        rows = _row_ids(SUB)

        def chunk(ci, hc):
            r0 = pl.multiple_of(ci * LRU_CHUNK, LRU_CHUNK)
            cur, ext = _conv_window(lx_ref, r0, ci)
            u = cb_ref[...] + cw_ref[3:4, :] * cur
            for k in range(3):
                u = u + cw_ref[k:k + 1, :] * _shift_rows_down(ext, 3 - k)
            r, gi, a, s = _lru_gates(u, wa_ref, ba_ref, wx_ref, bx_ref, sp)
            b = s * (gi * u)
            tiles = []
            for q in range(LRU_CHUNK // SUB):
                ta = a[SUB * q:SUB * (q + 1), :]
                tb = b[SUB * q:SUB * (q + 1), :]
                for d in (1, 2, 4):
                    a_sh = jnp.where(rows >= d, pltpu.roll(ta, d, axis=0), 1.0)
                    b_sh = jnp.where(rows >= d, pltpu.roll(tb, d, axis=0), 0.0)
                    tb = ta * b_sh + tb
                    ta = ta * a_sh
                hq = tb + ta * hc
                hc = hq[SUB - 1:SUB, :]
                tiles.append(hq)
            h = jnp.concatenate(tiles, axis=0)
            u_ref[pl.ds(r0, LRU_CHUNK), :] = u
            h_ref[pl.ds(r0, LRU_CHUNK), :] = h
            gel, _ = _gelu_and_grad(lg_ref[pl.ds(r0, LRU_CHUNK), :])
            y_ref[pl.ds(r0, LRU_CHUNK), :] = gel * h
            return hc

        lax.fori_loop(0, n_chunk, chunk, jnp.zeros((1, LRU_G), f32))

    seq = lambda cb: pl.BlockSpec((t, LRU_G), lambda c, cb=cb: (0, c + cb))
    rowc = pl.BlockSpec((1, LRU_G), lambda c: (0, c))
    diag = pl.BlockSpec((LRU_G, LRU_G), lambda c: (c, c))
    out = jax.ShapeDtypeStruct((t, LRU_W), f32)
    return _hosted_call(
        host, body, name=name, grid=(LRU_W // LRU_G,),
        in_specs=[seq(0), seq(LRU_W // LRU_G), pl.BlockSpec((4, LRU_G), lambda c: (0, c)),
                  rowc, diag, rowc, diag, rowc, rowc],
        out_specs=[seq(0)] * 3, out_shape=[out] * 3,
        compiler_params=_params(("arbitrary",)),
    )(zl, zl, conv_w, conv_b, wa, ba, wx, bx, lam)


def _lru_bwd(dmix, zl, u_all, h_all, conv_w, wa, ba, wx, bx, lam, *, name, host=None):
    t = zl.shape[0]
    n_chunk = t // LRU_CHUNK

    def body(dy_ref, lx_ref, lg_ref, u_ref, h_ref, cw_ref, wa_ref, ba_ref, wx_ref, bx_ref, lam_ref,
             dlx_ref, dlg_ref, dcw_ref, dcb_ref, dba_ref, dbx_ref, dlam_ref, dwa_ref, dwx_ref, dpr_s, dpx_s):
        lam_v = lam_ref[...]
        sp = _softplus(-lam_v)
        rows = _row_ids(SUB)
        rows_c = _row_ids(LRU_CHUNK)
        zero_row = jnp.zeros((1, LRU_G), f32)

        def chunk(step, carry):
            dh_c, a_next0, du_next, dsp, dba, dbx, dcb, dw0, dw1, dw2, dw3 = carry
            ci = n_chunk - 1 - step
            r0 = pl.multiple_of(ci * LRU_CHUNK, LRU_CHUNK)
            sl = pl.ds(r0, LRU_CHUNK)
            u = u_ref[sl, :]
            r, gi, a, s = _lru_gates(u, wa_ref, ba_ref, wx_ref, bx_ref, sp)
            h = h_ref[sl, :]
            p0 = pl.multiple_of(jnp.maximum(r0 - SUB, 0), SUB)
            h_before = jnp.where(ci > 0, h_ref[pl.ds(p0, SUB), :], 0.0)[SUB - 1:SUB, :]
            h_prev = jnp.where(rows_c == 0, h_before, pltpu.roll(h, 1, axis=0))
            gel, dgel = _gelu_and_grad(lg_ref[sl, :])
            dy = dy_ref[sl, :]
            dlg_ref[sl, :] = (dy * h * dgel).astype(bf16)
            g_in = dy * gel
            a_next = jnp.where(rows_c == LRU_CHUNK - 1, a_next0, pltpu.roll(a, LRU_CHUNK - 1, axis=0))
            tiles = [None] * (LRU_CHUNK // SUB)
            for q in reversed(range(LRU_CHUNK // SUB)):
                ta = a_next[SUB * q:SUB * (q + 1), :]
                tb = g_in[SUB * q:SUB * (q + 1), :]
                for d in (1, 2, 4):
                    a_sh = jnp.where(rows < SUB - d, pltpu.roll(ta, SUB - d, axis=0), 1.0)
                    b_sh = jnp.where(rows < SUB - d, pltpu.roll(tb, SUB - d, axis=0), 0.0)
                    tb = ta * b_sh + tb
                    ta = ta * a_sh
                dhq = tb + ta * dh_c
                dh_c = dhq[0:1, :]
                tiles[q] = dhq
            dh = jnp.concatenate(tiles, axis=0)
            da = dh * h_prev
            ds = dh * gi * u
            dgi = dh * s * u
            du = dh * s * gi
            dlog_a = da * a - ds * (a * a) / s
            dr = dlog_a * (-LRU_C * sp)
            dsp = dsp + jnp.sum(dlog_a * (-LRU_C * r), axis=0, keepdims=True)
            dpr = dr * r * (1.0 - r)
            dpx = dgi * gi * (1.0 - gi)
            dprb = dpr.astype(bf16)
            dpxb = dpx.astype(bf16)
            dpr_s[sl, :] = dprb
            dpx_s[sl, :] = dpxb
            du = du + (lax.dot_general(dprb, wa_ref[...], _NT, preferred_element_type=f32)
                       + lax.dot_general(dpxb, wx_ref[...], _NT, preferred_element_type=f32))
            dba = dba + jnp.sum(dpr, axis=0, keepdims=True)
            dbx = dbx + jnp.sum(dpx, axis=0, keepdims=True)
            dcb = dcb + jnp.sum(du, axis=0, keepdims=True)
            du_ext = jnp.concatenate([du, du_next], axis=0)
            dlx = cw_ref[3:4, :] * du
            for k in range(3):
                dlx = dlx + cw_ref[k:k + 1, :] * _shift_rows_up(du_ext, 3 - k, LRU_CHUNK)
            dlx_ref[sl, :] = dlx.astype(bf16)
            cur, ext = _conv_window(lx_ref, r0, ci)
            dws = [dw0, dw1, dw2, dw3 + jnp.sum(du * cur, axis=0, keepdims=True)]
            for k in range(3):
                dws[k] = dws[k] + jnp.sum(du * _shift_rows_down(ext, 3 - k), axis=0, keepdims=True)
            return (dh_c, a[0:1, :], du[0:SUB, :], dsp, dba, dbx, dcb, dws[0], dws[1], dws[2], dws[3])

        init = (zero_row, zero_row, jnp.zeros((SUB, LRU_G), f32)) + (zero_row,) * 8
        out = lax.fori_loop(0, n_chunk, chunk, init)
        _, _, _, dsp, dba, dbx, dcb, dw0, dw1, dw2, dw3 = out
        dlam_ref[...] = dsp * (-_sigmoid(-lam_v))
        dba_ref[...] = dba
        dbx_ref[...] = dbx
        dcb_ref[...] = dcb
        dcw_ref[...] = jnp.concatenate([dw0, dw1, dw2, dw3], axis=0)
        ub = u_ref[...].astype(bf16)
        dwa_ref[...] = lax.dot_general(ub, dpr_s[...], _TN, preferred_element_type=f32)
        dwx_ref[...] = lax.dot_general(ub, dpx_s[...], _TN, preferred_element_type=f32)

    seq = lambda cb: pl.BlockSpec((t, LRU_G), lambda c, cb=cb: (0, c + cb))
    rowc = pl.BlockSpec((1, LRU_G), lambda c: (0, c))
    diag = pl.BlockSpec((LRU_G, LRU_G), lambda c: (c, c))
    gate_out = pl.BlockSpec((None, LRU_G, LRU_G), lambda c: (c, 0, 0))
    row_shape = jax.ShapeDtypeStruct((1, LRU_W), f32)
    return _hosted_call(
        host, body, name=name, grid=(LRU_W // LRU_G,),
        in_specs=[seq(LRU_W // LRU_G), seq(0), seq(LRU_W // LRU_G), seq(0), seq(0),
                  pl.BlockSpec((4, LRU_G), lambda c: (0, c)),
                  diag, rowc, diag, rowc, rowc],
        out_specs=[seq(0), seq(0), pl.BlockSpec((4, LRU_G), lambda c: (0, c)), rowc, rowc, rowc, rowc,
                   gate_out, gate_out],
        out_shape=[jax.ShapeDtypeStruct((t, LRU_W), bf16)] * 2
        + [jax.ShapeDtypeStruct((4, LRU_W), f32)] + [row_shape] * 4
        + [jax.ShapeDtypeStruct((LRU_W // LRU_G, LRU_G, LRU_G), f32)] * 2,
        scratch_shapes=[pltpu.VMEM((t, LRU_G), bf16), pltpu.VMEM((t, LRU_G), bf16)],
        compiler_params=_params(("arbitrary",)),
    )(dmix, zl, zl, u_all, h_all, conv_w, wa, ba, wx, bx, lam)


def _pack_rows(a):
    flat = a.reshape(-1)
    rows = -(-flat.shape[0] // LANES)
    return jnp.pad(flat, (0, rows * LANES - flat.shape[0])).reshape(rows, LANES)


W_IN_PAD = 21 * LANES


def _w_in_join(blocks, *, name):
    tm = 256

    def body(b_ref, o_ref):
        o_ref[:, IN_COLS:] = jnp.zeros((tm, W_IN_PAD - IN_COLS), bf16)
        for q in range(N_DEV):
            o_ref[:, IN_SHARD * q:IN_SHARD * (q + 1)] = b_ref[q]

    return pl.pallas_call(
        body, name=name, grid=(D_MODEL // tm,),
        in_specs=[pl.BlockSpec((N_DEV, tm, IN_SHARD), lambda i: (0, i, 0))],
        out_specs=pl.BlockSpec((tm, W_IN_PAD), lambda i: (i, 0)),
        out_shape=jax.ShapeDtypeStruct((D_MODEL, W_IN_PAD), bf16), compiler_params=_params(("arbitrary",)),
    )(blocks)


def _w_in_split(main, fg, *, name):
    tm = 256
    n_main = main.shape[0]

    def body(m_ref, f_ref, o_ref):
        full = jnp.concatenate([m_ref[n] for n in range(n_main)] + [f_ref[...]], axis=1)
        for q in range(N_DEV):
            o_ref[q] = full[:, IN_SHARD * q:IN_SHARD * (q + 1)]

    return pl.pallas_call(
        body, name=name, grid=(D_MODEL // tm,),
        in_specs=[pl.BlockSpec((n_main, tm, 512), lambda i: (0, i, 0)), pl.BlockSpec((tm, LANES), lambda i: (i, 0))],
        out_specs=pl.BlockSpec((N_DEV, tm, IN_SHARD), lambda i: (0, i, 0)),
        out_shape=jax.ShapeDtypeStruct((N_DEV, D_MODEL, IN_SHARD), bf16), compiler_params=_params(("arbitrary",)),
    )(main, fg)


def _block_diag(w):
    eye = jnp.eye(HEADS, dtype=w.dtype)
    return jnp.einsum("hij,hk->hikj", w, eye).reshape(LRU_W, LRU_W)


def _diag_blocks(dw):
    per = dw.shape[1] // HEAD_D
    blocks = [dw[:, HEAD_D * b:HEAD_D * (b + 1), HEAD_D * b:HEAD_D * (b + 1)] for b in range(per)]
    return jnp.stack(blocks, axis=1).reshape(HEADS, HEAD_D, HEAD_D)


def _local_step(x, target, sent, small, *, tm=512, tm_ffn=1024):
    ln1 = (small["ln1_g"], small["ln1_b"])
    ln2 = (small["ln2_g"], small["ln2_b"])
    ln3 = (small["ln3_g"], small["ln3_b"])

    xh1, rs1, hg1, hu1, wg1, wu1, wd1, w_in_g, w_out_g, conv_w_g = _ffn1_fwd_gathering(
        x, (sent["ffn1_w_gate"], sent["ffn1_w_up"], sent["ffn1_w_down"]),
        _Exchange([sent["w_in"], sent["w_out"], sent["conv_w"]], gather=True), tm=tm_ffn, name="ffn1_fwd")
    w_in = _w_in_join(w_in_g, name="w_in_join")
    w_out = w_out_g.reshape(D_MODEL, D_MODEL)
    conv_w = conv_w_g.transpose(1, 0, 2).reshape(4, LRU_W)
    qkv, zl, zfg = _in_proj(xh1, ln1[0], ln1[1], w_in, tm=tm, name="in_proj")
    bfg = jnp.pad(small["b_forget"], ((0, 0), (0, LANES - HEADS)))
    cum = _cum_fwd(zfg, bfg, name="cum_fwd")
    q_aug, k_aug, v_aug = _attn_prep_fwd(qkv, cum, tm=tm, name="attn_prep_fwd")
    o, lse, wg2, wu2 = _attn_fwd2(
        q_aug, k_aug, v_aug, name="attn_fwd",
        host=_Hosts(_Exchange([sent["ffn2_w_gate"]], gather=True),
                    _Exchange([sent["ffn2_w_up"]], gather=True, direct=True)))
    wa_bd = _block_diag(small["rg_wa"]).astype(bf16)
    wx_bd = _block_diag(small["rg_wx"]).astype(bf16)
    ba = small["rg_ba"].reshape(1, LRU_W)
    bx = small["rg_bx"].reshape(1, LRU_W)
    u, h, lru, wd2 = _lru_fwd(zl, conv_w, small["conv_b"], wa_bd, ba, wx_bd, bx, small["lru_lambda"],
                              name="lru_fwd", host=_Exchange([sent["ffn2_w_down"]], gather=True))
    xh2, rs2 = _mmln([(o, 0, FOX_W, w_out, 0, D_MODEL, "nn"), (lru, 0, LRU_W, w_out, 1, D_MODEL, "nn")],
                     tm=tm, name="mix_fwd", resid=("affine", xh1) + ln1, resid_scale=ALPHA, epi="ln_fwd")
    dpre3, sq_rows, g_ln3g, g_ln3b, hg2, hu2 = _ffn_fwd_loss(
        xh2, ln2[0], ln2[1], wg2, wu2, wd2, ln3[0], ln3[1], target, tm=tm_ffn, name="ffn2_fwd_loss")

    dpre2, g_ln2g, g_ln2b, dhg2, dhu2, a2 = _ffn_bwd(dpre3, hg2, hu2, wg2, wu2, wd2,
                                                     (xh2, rs2, ln2[0]), tm=tm_ffn, name="ffn2_bwd")
    wgrad = dict(out_dtype=bf16, tm=D_MODEL, mb=1, tn=FF_TILE, nb=4, tk=512, pair=True)
    wdgrad = dict(out_dtype=bf16, tm=512, mb=4, tn=D_MODEL, nb=1, tk=512, out_scale=0.5, pair=True)
    between_chips = functools.partial(_Exchange, gather=False, chips=True)
    g_wg2 = _mm_tn(xh2, dhg2, name="g_wg2", affine=ln2, **wgrad)
    g_wu2 = _mm_tn(xh2, dhu2, name="g_wu2", affine=ln2, **wgrad)
    g_wd2 = _mm_tn(a2, dpre3, name="g_wd2", **wdgrad)

    dmix = _mmln([(dpre2, 0, D_MODEL, w_out, 0, D_MODEL, "nt")], tm=tm, name="dmix_bwd")
    g_wout_a = _mm(o, dpre2, mode="tn", out_dtype=bf16, tm=512, tn=D_MODEL, tk=512, name="g_wout_fox")
    g_wout_b = _mm(lru, dpre2, mode="tn", out_dtype=bf16, tm=512, tn=D_MODEL, tk=512, name="g_wout_lru")
    g_wout_blocked = jnp.concatenate([g_wout_a, g_wout_b], axis=0).reshape(N_DEV, D_MODEL // N_DEV, D_MODEL)
    dlx, dlg, g_cw, g_cb, g_ba, g_bx, g_lam, g_wa4, g_wx4, p_wg2, p_wout = _lru_bwd(
        dmix, zl, u, h, conv_w, wa_bd, ba, wx_bd, bx, small["lru_lambda"], name="lru_bwd",
        host=_Hosts(between_chips([g_wg2]), _Exchange([g_wout_blocked], gather=False)))
    qb_aug, do_aug = _attn_prep_bwd(qkv, cum, lse, dmix, o, tm=tm, name="attn_prep_bwd")
    dq, dcum_q, dk, dv, dcum_k, p_wu2, p_wd2 = _attn_bwd(qb_aug, k_aug, v_aug, do_aug, name="attn_bwd",
                                                         host=between_chips([g_wu2, g_wd2]))
    dfg, g_bf = _cum_bwd(dcum_q, dcum_k, zfg, bfg, name="cum_bwd")

    dz = [(dq, 0, 512), (dk, 1, 512), (dv, 2, 512), (dlx, 3, 512), (dlg, 4, 512), (dfg, 20, LANES)]
    dpre1, g_ln1g, g_ln1b = _mmln(
        [(arr, 0, w, w_in, cb, w, "nt") for (arr, cb, w) in dz],
        tm=tm, name="dx1_bwd", resid=("plain", dpre2), resid_scale=ALPHA, epi="ln_bwd", ln=(xh1, rs1, ln1[0]))
    g_win_main = _mm_tn(xh1, [arr for arr, _, _ in dz[:5]], out_dtype=bf16, tm=D_MODEL, mb=1, tn=512, nb=5, tk=512,
                        name="g_win", affine=ln1, out_blocked=True)
    g_win_fg = _mm(xh1, dfg, mode="tn", out_dtype=bf16, tm=D_MODEL, tn=LANES, tk=512, name="g_win_fg", affine=ln1)
    g_win_blocked = _w_in_split(g_win_main, g_win_fg, name="w_in_split")
    dhg1, dhu1, a1, p_win = _ffn_bwd_act(dpre1, hg1, hu1, wd1, tm=tm_ffn, name="ffn1_bwd_act",
                                         host=_Exchange([g_win_blocked], gather=False))
    small_g = {
        "ln1_g": g_ln1g, "ln1_b": g_ln1b, "b_forget": g_bf[:, :HEADS], "conv_w": g_cw, "conv_b": g_cb,
        "rg_wa": _diag_blocks(g_wa4), "rg_ba": g_ba.reshape(HEADS, HEAD_D),
        "rg_wx": _diag_blocks(g_wx4), "rg_bx": g_bx.reshape(HEADS, HEAD_D), "lru_lambda": g_lam,
        "ln2_g": g_ln2g, "ln2_b": g_ln2b, "ln3_g": g_ln3g, "ln3_b": g_ln3b,
    }
    small_g["loss"] = (0.5 / D_MODEL) * jnp.sum(sq_rows, keepdims=True)
    pieces = [_pack_rows(small_g[n]) for n in PACKED]
    packed = jnp.concatenate(pieces + [jnp.zeros((PACK_ROWS - sum(p.shape[0] for p in pieces), LANES), f32)])
    g_wg1, all_packed = _mm_tn(x, dhg1, name="g_wg1", host=_Exchange([packed], gather=True), **wgrad)
    g_wu1, p_wg1 = _mm_tn(x, dhu1, name="g_wu1", host=between_chips([g_wg1]), **wgrad)
    g_wd1, p_wu1 = _mm_tn(a1, dpre1, name="g_wd1", host=between_chips([g_wu1]), **wdgrad)
    grad_x, p_wd1 = _ffn_bwd_dx(dpre1, dhg1, dhu1, wg1, wu1, tm=tm_ffn, name="ffn1_bwd_dx",
                                host=between_chips([g_wd1]))
    parts = {
        "ffn1_w_gate": p_wg1, "ffn1_w_up": p_wu1, "ffn1_w_down": p_wd1, "w_in": p_win, "w_out": p_wout,
        "ffn2_w_gate": p_wg2, "ffn2_w_up": p_wu2, "ffn2_w_down": p_wd2,
    }
    return grad_x, parts, all_packed, {n: small_g[n].shape for n in PACKED}


def _adam_math(w, g, m, v):
    m2 = ADAM_B1 * m + (1.0 - ADAM_B1) * g
    v2 = ADAM_B2 * v + (1.0 - ADAM_B2) * (g * g)
    m_hat = m2 / (1.0 - ADAM_B1 ** ADAM_STEP)
    v_hat = v2 / (1.0 - ADAM_B2 ** ADAM_STEP)
    delta = -ADAM_LR * (m_hat / (jnp.sqrt(v_hat) + ADAM_EPS) + ADAM_WD * w)
    return delta, m2, v2


ADAM_TILE_ELEMS = 128 * 1024


def _adamw_big(items, *, name):
    _, r, c = items[0][1].shape
    n_parts = items[0][0].shape[0]
    n_items = len(items)
    assert all(it[1].shape == (1, r, c) and it[0].shape == (n_parts, r, c) for it in items), name
    tr = max(d for d in range(8, r + 1, 8) if r % d == 0 and d * c <= ADAM_TILE_ELEMS)

    def body(*refs):
        ins, outs = refs[:4 * n_items], refs[4 * n_items:]
        for k in range(n_items):
            p_ref, w_ref, m_ref, v_ref = ins[4 * k:4 * k + 4]
            g = p_ref[0].astype(f32)
            for q in range(1, n_parts):
                g = g + p_ref[q].astype(f32)
            d, m2, v2 = _adam_math(w_ref[...], g, m_ref[...], v_ref[...])
            for o_ref, val in zip(outs[4 * k:4 * k + 4], (g, d, m2, v2)):
                o_ref[...] = val

    blk = pl.BlockSpec((None, tr, c), lambda i: (0, i, 0))
    res = pl.pallas_call(
        body, name=name, grid=(r // tr,),
        in_specs=[pl.BlockSpec((n_parts, tr, c), lambda i: (0, i, 0)), blk, blk, blk] * n_items,
        out_specs=[blk] * (4 * n_items), out_shape=[jax.ShapeDtypeStruct((1, r, c), f32)] * (4 * n_items),
        compiler_params=_params(("arbitrary",)),
    )(*[a for it in items for a in it])
    return [res[4 * k:4 * k + 4] for k in range(n_items)]


def _adamw_small(items, *, name):
    n = len(items)

    def body(*refs):
        ins, outs = refs[:4 * n], refs[4 * n:]
        for k in range(n):
            g, w, m, v = (ins[4 * k + q][...] for q in range(4))
            d, m2, v2 = _adam_math(w, g, m, v)
            outs[3 * k][...] = d
            outs[3 * k + 1][...] = m2
            outs[3 * k + 2][...] = v2

    vm = pl.BlockSpec(memory_space=pltpu.VMEM)
    flat = [a for item in items for a in item]
    out_shape = [jax.ShapeDtypeStruct(item[1].shape, f32) for item in items for _ in range(3)]
    return pl.pallas_call(
        body, name=name, in_specs=[vm] * (4 * n), out_specs=[vm] * (3 * n), out_shape=out_shape,
    )(*flat)


def _sum_parts(parts, *, name):
    def body(p_ref, o_ref):
        acc = p_ref[0]
        for q in range(1, N_DEV):
            acc = acc + p_ref[q]
        o_ref[...] = acc

    vm = pl.BlockSpec(memory_space=pltpu.VMEM)
    return pl.pallas_call(
        body, name=name, in_specs=[vm], out_specs=vm, out_shape=jax.ShapeDtypeStruct(parts.shape[1:], f32),
    )(parts)


WEIGHTS = ["ffn1_w_gate", "ffn1_w_up", "ffn1_w_down", "ln1_g", "ln1_b", "w_in", "b_forget", "conv_w", "conv_b",
           "rg_wa", "rg_ba", "rg_wx", "rg_bx", "lru_lambda", "w_out", "ln2_g", "ln2_b",
           "ffn2_w_gate", "ffn2_w_up", "ffn2_w_down", "ln3_g", "ln3_b"]
BIG = ["ffn1_w_gate", "ffn1_w_up", "ffn1_w_down", "w_in", "w_out", "ffn2_w_gate", "ffn2_w_up", "ffn2_w_down"]
PACKED = ["ln1_g", "ln1_b", "ln2_g", "ln2_b", "ln3_g", "ln3_b", "conv_b", "rg_ba", "rg_bx", "lru_lambda",
          "conv_w", "rg_wa", "rg_wx", "b_forget", "loss"]
PACK_ROWS = 600


def _two_d(a):
    return a.reshape((-1, a.shape[-1]))


def _transport(a):
    return _two_d(a)


def kernel(x, ffn1_w_gate, ffn1_w_up, ffn1_w_down, ln1_g, ln1_b, w_in, b_forget, conv_w, conv_b, rg_wa, rg_ba, rg_wx, rg_bx, lru_lambda, w_out, ln2_g, ln2_b, ffn2_w_gate, ffn2_w_up, ffn2_w_down, ln3_g, ln3_b, loss_target, m_ffn1_w_gate, m_ffn1_w_up, m_ffn1_w_down, m_ln1_g, m_ln1_b, m_w_in, m_b_forget, m_conv_w, m_conv_b, m_rg_wa, m_rg_ba, m_rg_wx, m_rg_bx, m_lru_lambda, m_w_out, m_ln2_g, m_ln2_b, m_ffn2_w_gate, m_ffn2_w_up, m_ffn2_w_down, m_ln3_g, m_ln3_b, v_ffn1_w_gate, v_ffn1_w_up, v_ffn1_w_down, v_ln1_g, v_ln1_b, v_w_in, v_b_forget, v_conv_w, v_conv_b, v_rg_wa, v_rg_ba, v_rg_wx, v_rg_bx, v_lru_lambda, v_w_out, v_ln2_g, v_ln2_b, v_ffn2_w_gate, v_ffn2_w_up, v_ffn2_w_down, v_ln3_g, v_ln3_b):
    w_args = (ffn1_w_gate, ffn1_w_up, ffn1_w_down, ln1_g, ln1_b, w_in, b_forget, conv_w, conv_b, rg_wa, rg_ba, rg_wx, rg_bx, lru_lambda, w_out, ln2_g, ln2_b, ffn2_w_gate, ffn2_w_up, ffn2_w_down, ln3_g, ln3_b)
    m_args = (m_ffn1_w_gate, m_ffn1_w_up, m_ffn1_w_down, m_ln1_g, m_ln1_b, m_w_in, m_b_forget, m_conv_w, m_conv_b, m_rg_wa, m_rg_ba, m_rg_wx, m_rg_bx, m_lru_lambda, m_w_out, m_ln2_g, m_ln2_b, m_ffn2_w_gate, m_ffn2_w_up, m_ffn2_w_down, m_ln3_g, m_ln3_b)
    v_args = (v_ffn1_w_gate, v_ffn1_w_up, v_ffn1_w_down, v_ln1_g, v_ln1_b, v_w_in, v_b_forget, v_conv_w, v_conv_b, v_rg_wa, v_rg_ba, v_rg_wx, v_rg_bx, v_lru_lambda, v_w_out, v_ln2_g, v_ln2_b, v_ffn2_w_gate, v_ffn2_w_up, v_ffn2_w_down, v_ln3_g, v_ln3_b)
    w = dict(zip(WEIGHTS, w_args))
    m = dict(zip(WEIGHTS, m_args))
    v = dict(zip(WEIGHTS, v_args))
    me = 4 * lax.axis_index("x") + 2 * lax.axis_index("y") + lax.axis_index("c")

    sent = {n: _transport(w[n]).astype(bf16) for n in BIG}
    sent["conv_w"] = _two_d(w["conv_w"])
    small = {n: w[n] for n in ("ln1_g", "ln1_b", "ln2_g", "ln2_b", "ln3_g", "ln3_b", "b_forget", "conv_b",
                               "lru_lambda")}
    small.update({n: w[n][0] for n in ("rg_wa", "rg_ba", "rg_wx", "rg_bx")})

    grad_x, parts, all_packed, small_shapes = _local_step(x[0], loss_target[0], sent, small)

    total = _sum_parts(all_packed, name="sum_small_grads")
    grads, off = {}, 0
    for n in PACKED:
        size = math.prod(small_shapes[n])
        rows = -(-size // LANES)
        grads[n] = total[off:off + rows].reshape(-1)[:size].reshape(small_shapes[n])
        off += rows
    loss = grads.pop("loss").reshape(())
    grads["conv_w"] = lax.dynamic_slice_in_dim(grads["conv_w"], me * (LRU_W // N_DEV), LRU_W // N_DEV, axis=1)

    delta, new_m, new_v = {}, {}, {}
    for group in (("ffn1_w_gate", "ffn1_w_up", "ffn2_w_gate", "ffn2_w_up"), ("ffn1_w_down", "ffn2_w_down"),
                  ("w_in",), ("w_out",)):
        done = _adamw_big([(parts[n], w[n], m[n], v[n]) for n in group], name="adamw_" + group[0])
        for n, (g, d, m2, v2) in zip(group, done):
            grads[n], delta[n], new_m[n], new_v[n] = g, d, m2, v2
    small_names = [n for n in WEIGHTS if n not in BIG]
    outs = _adamw_small([(_two_d(grads[n]), _two_d(w[n]), _two_d(m[n]), _two_d(v[n])) for n in small_names],
                        name="adamw_small")
    for k, n in enumerate(small_names):
        delta[n], new_m[n], new_v[n] = outs[3 * k], outs[3 * k + 1], outs[3 * k + 2]

    def shaped(d):
        return [d[n].reshape(w[n].shape) for n in WEIGHTS]

    return (loss, grad_x[None], *shaped(grads), *shaped(delta), *shaped(new_m), *shaped(new_v))
```

```python
import functools
import math

import jax
import jax.numpy as jnp
from jax import lax
from jax.experimental import pallas as pl
from jax.experimental.pallas import tpu as pltpu

f32 = jnp.float32
bf16 = jnp.bfloat16

N_DEV = 8
D_MODEL = 1024
D_FF = 4096
FF_TILE = D_FF // N_DEV
FOX_W = 512
LRU_W = 512
HEADS = 8
HEAD_D = 64
IN_COLS = 2568
IN_SHARD = IN_COLS // N_DEV
LANES = 128
LN_EPS = 1e-5
ALPHA = 2.0 ** 0.25
ATT_SCALE = 1.0 / math.sqrt(HEAD_D)
LRU_C = 8.0
NEG_BIG = -1e30

ADAM_LR = 0.001
ADAM_B1 = 0.9
ADAM_B2 = 0.999
ADAM_EPS = 1e-08
ADAM_WD = 0.01
ADAM_STEP = 10

VMEM_LIMIT = 56 * 1024 * 1024
MESH_T = pl.DeviceIdType.MESH


def _params(sem, **kw):
    return pltpu.CompilerParams(dimension_semantics=sem, vmem_limit_bytes=VMEM_LIMIT, **kw)


def _sigmoid(x):
    return 1.0 / (1.0 + jnp.exp(-x))


def _sigmoid_tanh(x):
    return 0.5 * jnp.tanh(0.5 * x) + 0.5


def _softplus(x):
    return jnp.maximum(x, 0.0) + jnp.log(1.0 + jnp.exp(-jnp.abs(x)))


def _one_minus_exp(x):
    series = -x * (1.0 + x * (0.5 + x * (1.0 / 6 + x * (1.0 / 24 + x * (1.0 / 120 + x * (1.0 / 720))))))
    return jnp.where(x > -0.125, series, 1.0 - jnp.exp(x))


_GELU_C = math.sqrt(2.0 / math.pi)


def _gelu_and_grad(x):
    inner = _GELU_C * (x + 0.044715 * x * x * x)
    t = jnp.tanh(inner)
    g = 0.5 * x * (1.0 + t)
    dg = 0.5 * (1.0 + t) + 0.5 * x * (1.0 - t * t) * _GELU_C * (1.0 + 3 * 0.044715 * x * x)
    return g, dg


def _ln_fwd_tile(pre):
    mu = jnp.mean(pre, axis=-1, keepdims=True)
    xc = pre - mu
    var = jnp.mean(xc * xc, axis=-1, keepdims=True)
    rstd = lax.rsqrt(var + LN_EPS)
    return xc * rstd, rstd


def _ln_bwd_tile(dy, xhat, rstd, g):
    dyg = dy * g
    m1 = jnp.mean(dyg, axis=-1, keepdims=True)
    m2 = jnp.mean(dyg * xhat, axis=-1, keepdims=True)
    dpre = rstd * (dyg - m1 - xhat * m2)
    return dpre, jnp.sum(dy * xhat, axis=0, keepdims=True), jnp.sum(dy, axis=0, keepdims=True)


_NT = (((1,), (1,)), ((), ()))
_TN = (((0,), (0,)), ((), ()))


class _Exchange:
    def __init__(self, arrs, gather, chips=False, direct=False):
        self.arrs, self.gather, self.n, self.chips = list(arrs), gather, len(arrs), chips
        self.relays = gather and not direct

    def out_shape(self):
        return [jax.ShapeDtypeStruct(((N_DEV,) + a.shape) if self.gather else a.shape, a.dtype) for a in self.arrs]

    def scratch(self):
        n_remote = self.n * (N_DEV - 1)
        return [pltpu.SemaphoreType.DMA((n_remote,)), pltpu.SemaphoreType.DMA((n_remote,)),
                pltpu.SemaphoreType.DMA((self.n,))]

    def copies(self, ins, outs, sems):
        send_sems, recv_sems, local_sems = sems
        x, y, c = lax.axis_index("x"), lax.axis_index("y"), lax.axis_index("c")
        me = 2 * x + y if self.chips else 4 * x + 2 * y + c
        out = []
        for k in range(self.n):
            for d in (range(2, N_DEV, 2) if self.chips else range(1, N_DEV)):
                px = 1 - x if d & 4 else x
                py = 1 - y if d & 2 else y
                pc = 1 - c if d & 1 else c
                sem = k * (N_DEV - 1) + d - 1
                out.append(pltpu.make_async_remote_copy(
                    src_ref=ins[k] if self.gather else ins[k].at[2 * px + py if self.chips else 4 * px + 2 * py + pc],
                    dst_ref=outs[k].at[me],
                    send_sem=send_sems.at[sem], recv_sem=recv_sems.at[sem],
                    device_id=(px, py, pc), device_id_type=MESH_T))
            out.append(pltpu.make_async_copy(ins[k] if self.gather else ins[k].at[me], outs[k].at[me],
                                             local_sems.at[k]))
        return out

    def gather_copies(self, ins, outs, sems):
        send_sems, recv_sems, local_sems = sems
        x, y, c = lax.axis_index("x"), lax.axis_index("y"), lax.axis_index("c")
        sibling = (x, y, 1 - c)
        chips = [(1 - x, y), (x, 1 - y), (1 - x, 1 - y)]
        out = []
        for k in range(self.n):
            def copy(s, block, to, src=None, k=k):
                rows = outs[k].at[4 * block[0] + 2 * block[1] + block[2]]
                sem = k * (N_DEV - 1) + s
                return pltpu.make_async_remote_copy(
                    src_ref=rows if src is None else src, dst_ref=rows, send_sem=send_sems.at[sem],
                    recv_sem=recv_sems.at[sem], device_id=to, device_id_type=MESH_T)

            first = [copy(0, (x, y, c), sibling, src=ins[k])]
            first += [copy(1 + q, (x, y, c), (*chip, c), src=ins[k]) for q, chip in enumerate(chips)]
            passed = [copy(4 + q, (*chip, c), sibling) for q, chip in enumerate(chips)]
            own = pltpu.make_async_copy(ins[k], outs[k].at[4 * x + 2 * y + c], local_sems.at[k])
            out.append((first, passed, own, copy))
        return out, sibling, chips, (x, y, c)

    def start(self, ins, outs, sems):
        if not self.relays:
            for cp in self.copies(ins, outs, sems):
                cp.start()
            return
        per_array, _, _, _ = self.gather_copies(ins, outs, sems)
        for first, _, own, _ in per_array:
            own.start()
            for cp in first:
                cp.start()

    def relay(self, ins, outs, sems):
        per_array, sibling, chips, (x, y, c) = self.gather_copies(ins, outs, sems)
        for first, passed, own, copy in per_array:
            for q, chip in enumerate(chips):
                copy(1 + q, (*chip, c), (x, y, c)).wait_recv()
                passed[q].start()

    def wait(self, ins, outs, sems, relayed=False):
        if not self.relays:
            for cp in self.copies(ins, outs, sems):
                cp.wait()
            return
        if not relayed:
            self.relay(ins, outs, sems)
        per_array, sibling, chips, (x, y, c) = self.gather_copies(ins, outs, sems)
        for first, passed, own, copy in per_array:
            copy(0, sibling, (x, y, c)).wait_recv()
            for q, chip in enumerate(chips):
                copy(4 + q, (*chip, 1 - c), (x, y, c)).wait_recv()
            for cp in first + passed:
                cp.wait_send()
            own.wait()


class _Hosts:
    def __init__(self, *hosts):
        self.hosts = hosts
        self.relays = any(h.relays for h in hosts)
        self.n = sum(h.n for h in hosts)
        self.arrs = [a for h in hosts for a in h.arrs]

    def out_shape(self):
        return [sh for h in self.hosts for sh in h.out_shape()]

    def scratch(self):
        return [sc for h in self.hosts for sc in h.scratch()]

    def _each(self, ins, outs, sems):
        at = 0
        for k, h in enumerate(self.hosts):
            yield h, ins[at:at + h.n], outs[at:at + h.n], sems[3 * k:3 * k + 3]
            at += h.n

    def start(self, ins, outs, sems):
        for h, h_in, h_out, h_sems in self._each(ins, outs, sems):
            h.start(h_in, h_out, h_sems)

    def relay(self, ins, outs, sems):
        for h, h_in, h_out, h_sems in self._each(ins, outs, sems):
            if h.relays:
                h.relay(h_in, h_out, h_sems)

    def wait(self, ins, outs, sems, relayed=False):
        for h, h_in, h_out, h_sems in self._each(ins, outs, sems):
            h.wait(h_in, h_out, h_sems, relayed=relayed and h.relays)


def _hosted_call(host, body, *, name, grid, in_specs, out_specs, out_shape, scratch_shapes=(), compiler_params):
    out_specs = list(out_specs) if isinstance(out_specs, (list, tuple)) else [out_specs]
    out_shape = list(out_shape) if isinstance(out_shape, (list, tuple)) else [out_shape]
    if host is None:
        return pl.pallas_call(body, name=name, grid=grid, in_specs=in_specs, out_specs=out_specs,
                              out_shape=out_shape, scratch_shapes=list(scratch_shapes),
                              compiler_params=compiler_params)
    n_in, n_out, n_scr, k = len(in_specs), len(out_shape), len(scratch_shapes), host.n

    def wrapped(*refs):
        ins, h_in = refs[:n_in], refs[n_in:n_in + k]
        outs, h_out = refs[n_in + k:n_in + k + n_out], refs[n_in + k + n_out:n_in + 2 * k + n_out]
        scr, sems = refs[n_in + 2 * k + n_out:n_in + 2 * k + n_out + n_scr], refs[n_in + 2 * k + n_out + n_scr:]
        ids = [pl.program_id(a) for a in range(len(grid))]
        first = functools.reduce(jnp.logical_and, [i == 0 for i in ids])
        last = functools.reduce(jnp.logical_and, [i == g - 1 for i, g in zip(ids, grid)])
        steps = math.prod(grid)
        relay_at = (3 * steps) // 4 if host.relays and steps >= 8 else None

        @pl.when(first)
        def _():
            host.start(h_in, h_out, sems)

        if relay_at is not None:
            coords, rest = [], relay_at
            for g in reversed(grid):
                coords.append(rest % g)
                rest //= g

            @pl.when(functools.reduce(jnp.logical_and, [i == cd for i, cd in zip(ids, reversed(coords))]))
            def _():
                host.relay(h_in, h_out, sems)

        body(*ins, *outs, *scr)

        @pl.when(last)
        def _():
            host.wait(h_in, h_out, sems, relayed=relay_at is not None)

    hbm = pl.BlockSpec(memory_space=pl.ANY)
    call = pl.pallas_call(
        wrapped, name=name, grid=grid, in_specs=list(in_specs) + [hbm] * k, out_specs=out_specs + [hbm] * k,
        out_shape=out_shape + host.out_shape(), scratch_shapes=list(scratch_shapes) + host.scratch(),
        compiler_params=compiler_params)
    return lambda *args: call(*args, *host.arrs)


def _ffn_fwd_loss(xhat, g_in, b_in, wg, wu, wd, g_out, b_out, target, *, tm, name):
    t = xhat.shape[0]
    nj = N_DEV

    def body(x_ref, g_ref, b_ref, wg_ref, wu_ref, wd_ref, go_ref, bo_ref, tg_ref,
             dx_ref, sq_ref, gg_ref, gb_ref, hg_ref, hu_ref, xb, acc):
        i = pl.program_id(0)
        j = pl.program_id(1)

        @pl.when(j == 0)
        def _():
            xb[...] = (x_ref[...] * g_ref[...] + b_ref[...]).astype(bf16)
            acc[...] = jnp.zeros_like(acc)

        hg = jnp.dot(xb[...], wg_ref[...], preferred_element_type=f32)
        hu = jnp.dot(xb[...], wu_ref[...], preferred_element_type=f32)
        hg_ref[...] = hg.astype(bf16)
        hu_ref[...] = hu.astype(bf16)
        a = hg * _sigmoid_tanh(hg) * hu
        acc[...] += jnp.dot(a.astype(bf16), wd_ref[...], preferred_element_type=f32)

        @pl.when(j == nj - 1)
        def _():
            x = x_ref[...] * g_ref[...] + b_ref[...]
            xo, rstd = _ln_fwd_tile(ALPHA * x + 0.5 * acc[...])
            diff = xo * go_ref[...] + bo_ref[...] - tg_ref[...]
            sq = jnp.sum(diff * diff, axis=0, keepdims=True)
            dprev, gg, gb = _ln_bwd_tile(diff * (1.0 / D_MODEL), xo, rstd, go_ref[...])
            dx_ref[...] = dprev

            @pl.when(i == 0)
            def _():
                sq_ref[...] = sq
                gg_ref[...] = gg
                gb_ref[...] = gb

            @pl.when(i > 0)
            def _():
                sq_ref[...] += sq
                gg_ref[...] += gg
                gb_ref[...] += gb

    tok = pl.BlockSpec((tm, D_MODEL), lambda i, j: (i, 0))
    row = pl.BlockSpec((1, D_MODEL), lambda i, j: (0, 0))
    hid = pl.BlockSpec((tm, FF_TILE), lambda i, j: (i, j))
    return pl.pallas_call(
        body, name=name, grid=(t // tm, nj),
        in_specs=[tok, row, row,
                  pl.BlockSpec((None, D_MODEL, FF_TILE), lambda i, j: (j, 0, 0)),
                  pl.BlockSpec((None, D_MODEL, FF_TILE), lambda i, j: (j, 0, 0)),
                  pl.BlockSpec((None, FF_TILE, D_MODEL), lambda i, j: (j, 0, 0)), row, row, tok],
        out_specs=[tok, row, row, row, hid, hid],
        out_shape=[jax.ShapeDtypeStruct((t, D_MODEL), f32)] + [jax.ShapeDtypeStruct((1, D_MODEL), f32)] * 3
        + [jax.ShapeDtypeStruct((t, D_FF), bf16)] * 2,
        scratch_shapes=[pltpu.VMEM((tm, D_MODEL), bf16), pltpu.VMEM((tm, D_MODEL), f32)],
        compiler_params=_params(("arbitrary", "arbitrary")),
    )(xhat, g_in, b_in, wg, wu, wd, g_out, b_out, target)


def _ffn1_fwd_gathering(x, own, extra, *, tm, name):
    t = x.shape[0]
    n_i = t // tm
    n_arr = 3
    k_extra = extra.n
    ex = _Exchange(list(own), gather=True)
    ax, ay, ac = lax.axis_index("x"), lax.axis_index("y"), lax.axis_index("c")
    order = jnp.stack([4 * px + 2 * py + pc for px, py in ((ax, ay), (1 - ax, ay), (ax, 1 - ay), (1 - ax, 1 - ay))
                       for pc in (ac, 1 - ac)]).astype(jnp.int32)
    arrival = [None, (0, None), (1, 0), (4, None), (2, 1), (5, None), (3, 2), (6, None)]

    def body(order_ref, x_ref, *refs):
        w_in, e_in = refs[:n_arr], refs[n_arr:n_arr + k_extra]
        refs = refs[n_arr + k_extra:]
        xo_ref, rstd_ref, hg_ref, hu_ref = refs[:4]
        w_all, e_out = refs[4:4 + n_arr], refs[4 + n_arr:4 + n_arr + k_extra]
        acc, wgb, wub, wdb, fetch_sems, send_sems, recv_sems, local_sems = refs[4 + n_arr + k_extra:12 + n_arr + k_extra]
        e_sems = refs[12 + n_arr + k_extra:]
        bufs = (wgb, wub, wdb)
        s = pl.program_id(0)
        i = pl.program_id(1)
        per_array, sibling, chips, (x_, y_, c_) = ex.gather_copies(w_in, w_all, (send_sems, recv_sems, local_sems))

        def fetch(pos, slot):
            return [pltpu.make_async_copy(w_in[a] if pos == 0 else w_all[a].at[order_ref[pos]],
                                          bufs[a].at[slot], fetch_sems.at[n_arr * slot + a]) for a in range(n_arr)]

        def source_of(pos):
            chip = (x_, y_) if pos < 2 else chips[(pos - 2) // 2]
            return (*chip, c_ if pos % 2 == 0 else 1 - c_)

        @pl.when(jnp.logical_and(s == 0, i == 0))
        def _():
            for q in range(4):
                for first, _, own_copy, _ in per_array:
                    if q == 0:
                        own_copy.start()
                    first[q].start()
            for cp in fetch(0, 0):
                cp.start()
            for cp in fetch(0, 0):
                cp.wait()

        @pl.when(jnp.logical_and(s == N_DEV // 2, i == 0))
        def _():
            extra.start(e_in, e_out, e_sems)

        for pos in range(1, N_DEV):
            @pl.when(jnp.logical_and(s == pos - 1, i == n_i - 1))
            def _(pos=pos):
                sem, passes = arrival[pos]
                for _, passed, _, copy in per_array:
                    copy(sem, source_of(pos), (x_, y_, c_)).wait_recv()
                    if passes is not None:
                        passed[passes].start()
                for cp in fetch(pos, pos % 2):
                    cp.start()

            @pl.when(jnp.logical_and(s == pos, i == 0))
            def _(pos=pos):
                for cp in fetch(pos, pos % 2):
                    cp.wait()

        slot = s % 2
        xb = x_ref[...].astype(bf16)
        hg = jnp.dot(xb, wgb[slot], preferred_element_type=f32)
        hu = jnp.dot(xb, wub[slot], preferred_element_type=f32)
        hg_ref[...] = hg.astype(bf16)
        hu_ref[...] = hu.astype(bf16)
        a = hg * _sigmoid_tanh(hg) * hu
        part = jnp.dot(a.astype(bf16), wdb[slot], preferred_element_type=f32)

        @pl.when(s == 0)
        def _():
            acc[i] = part

        @pl.when(s > 0)
        def _():
            acc[i] += part

        @pl.when(s == N_DEV - 1)
        def _():
            xo, rstd = _ln_fwd_tile(ALPHA * x_ref[...] + 0.5 * acc[i])
            xo_ref[...] = xo
            rstd_ref[...] = rstd

        @pl.when(jnp.logical_and(s == N_DEV - 1, i == n_i - 1))
        def _():
            for first, passed, own_copy, _ in per_array:
                for cp in first + passed:
                    cp.wait_send()
                own_copy.wait()
            extra.wait(e_in, e_out, e_sems)

    hbm = pl.BlockSpec(memory_space=pl.ANY)
    last = N_DEV - 1
    tok_out = pl.BlockSpec((tm, D_MODEL), lambda s, i, o: (jnp.where(s == last, i, 0), 0))
    col_out = pl.BlockSpec((tm, 1), lambda s, i, o: (jnp.where(s == last, i, 0), 0))
    hid = pl.BlockSpec((tm, FF_TILE), lambda s, i, o: (i, o[s]))
    shard_shapes = [(N_DEV,) + w.shape for w in own]
    grid_spec = pltpu.PrefetchScalarGridSpec(
        num_scalar_prefetch=1, grid=(N_DEV, n_i),
        in_specs=[pl.BlockSpec((tm, D_MODEL), lambda s, i, o: (i, 0))] + [hbm] * (n_arr + k_extra),
        out_specs=[tok_out, col_out, hid, hid] + [hbm] * (n_arr + k_extra),
        scratch_shapes=[pltpu.VMEM((n_i, tm, D_MODEL), f32)]
        + [pltpu.VMEM((2,) + w.shape, bf16) for w in own]
        + [pltpu.SemaphoreType.DMA((2 * n_arr,))] + ex.scratch() + extra.scratch())
    res = pl.pallas_call(
        body, name=name, grid_spec=grid_spec,
        out_shape=[jax.ShapeDtypeStruct((t, D_MODEL), f32), jax.ShapeDtypeStruct((t, 1), f32),
                   jax.ShapeDtypeStruct((t, D_FF), bf16), jax.ShapeDtypeStruct((t, D_FF), bf16)]
        + [jax.ShapeDtypeStruct(sh, bf16) for sh in shard_shapes] + extra.out_shape(),
        compiler_params=_params(("arbitrary", "arbitrary")),
    )(order, x, *own, *extra.arrs)
    return res


def _ffn_bwd(dpre, hg, hu, wg, wu, wd, ln_in, *, tm, name, host=None):
    t = dpre.shape[0]
    nj = N_DEV
    with_ln = ln_in is not None

    def body(*refs):
        if with_ln:
            (dp_ref, hg_ref, hu_ref, wg_ref, wu_ref, wd_ref, xh_ref, rs_ref, g_ref,
             dx_ref, gg_ref, gb_ref, dhg_ref, dhu_ref, a_ref, dfb, acc) = refs
        else:
            (dp_ref, hg_ref, hu_ref, wg_ref, wu_ref, wd_ref,
             dx_ref, dhg_ref, dhu_ref, a_ref, dfb, acc) = refs
        i = pl.program_id(0)
        j = pl.program_id(1)

        @pl.when(j == 0)
        def _():
            dfb[...] = (0.5 * dp_ref[...]).astype(bf16)
            acc[...] = jnp.zeros_like(acc)

        da = lax.dot_general(dfb[...], wd_ref[...], _NT, preferred_element_type=f32)
        hgv = hg_ref[...].astype(f32)
        huv = hu_ref[...].astype(f32)
        sg = _sigmoid_tanh(hgv)
        silu = hgv * sg
        a_ref[...] = (silu * huv).astype(bf16)
        dhu = (da * silu).astype(bf16)
        dhg = (da * huv * (sg * (1.0 + hgv * (1.0 - sg)))).astype(bf16)
        dhg_ref[...] = dhg
        dhu_ref[...] = dhu
        acc[...] += (lax.dot_general(dhg, wg_ref[...], _NT, preferred_element_type=f32)
                     + lax.dot_general(dhu, wu_ref[...], _NT, preferred_element_type=f32))

        @pl.when(j == nj - 1)
        def _():
            dx = ALPHA * dp_ref[...] + acc[...]
            if with_ln:
                dprev, gg, gb = _ln_bwd_tile(dx, xh_ref[...], rs_ref[...], g_ref[...])
                dx_ref[...] = dprev

                @pl.when(i == 0)
                def _():
                    gg_ref[...] = gg
                    gb_ref[...] = gb

                @pl.when(i > 0)
                def _():
                    gg_ref[...] += gg
                    gb_ref[...] += gb
            else:
                dx_ref[...] = dx

    tok = pl.BlockSpec((tm, D_MODEL), lambda i, j: (i, 0), pipeline_mode=pl.Buffered(1))
    row = pl.BlockSpec((1, D_MODEL), lambda i, j: (0, 0))
    hid = pl.BlockSpec((tm, FF_TILE), lambda i, j: (i, j))
    in_specs = [tok, hid, hid,
                pl.BlockSpec((None, D_MODEL, FF_TILE), lambda i, j: (j, 0, 0)),
                pl.BlockSpec((None, D_MODEL, FF_TILE), lambda i, j: (j, 0, 0)),
                pl.BlockSpec((None, FF_TILE, D_MODEL), lambda i, j: (j, 0, 0))]
    args = [dpre, hg, hu, wg, wu, wd]
    out_specs = [tok]
    out_shape = [jax.ShapeDtypeStruct((t, D_MODEL), f32)]
    if with_ln:
        in_specs += [tok, pl.BlockSpec((tm, 1), lambda i, j: (i, 0)), row]
        args += list(ln_in)
        out_specs += [row, row]
        out_shape += [jax.ShapeDtypeStruct((1, D_MODEL), f32)] * 2
    out_specs += [hid, hid, hid]
    out_shape += [jax.ShapeDtypeStruct((t, D_FF), bf16)] * 3
    return _hosted_call(
        host, body, name=name, grid=(t // tm, nj), in_specs=in_specs, out_specs=out_specs, out_shape=out_shape,
        scratch_shapes=[pltpu.VMEM((tm, D_MODEL), bf16), pltpu.VMEM((tm, D_MODEL), f32)],
        compiler_params=_params(("arbitrary", "arbitrary")),
    )(*args)


def _ffn_bwd_act(dpre, hg, hu, wd, *, tm, name, host=None):
    t = dpre.shape[0]

    def body(dp_ref, hg_ref, hu_ref, wd_ref, dhg_ref, dhu_ref, a_ref, dfb):
        @pl.when(pl.program_id(1) == 0)
        def _():
            dfb[...] = (0.5 * dp_ref[...]).astype(bf16)

        da = lax.dot_general(dfb[...], wd_ref[...], _NT, preferred_element_type=f32)
        hgv = hg_ref[...].astype(f32)
        huv = hu_ref[...].astype(f32)
        sg = _sigmoid_tanh(hgv)
        silu = hgv * sg
        a_ref[...] = (silu * huv).astype(bf16)
        dhu_ref[...] = (da * silu).astype(bf16)
        dhg_ref[...] = (da * huv * (sg * (1.0 + hgv * (1.0 - sg)))).astype(bf16)

    hid = pl.BlockSpec((tm, FF_TILE), lambda i, j: (i, j))
    return _hosted_call(
        host, body, name=name, grid=(t // tm, N_DEV),
        in_specs=[pl.BlockSpec((tm, D_MODEL), lambda i, j: (i, 0)), hid, hid,
                  pl.BlockSpec((None, FF_TILE, D_MODEL), lambda i, j: (j, 0, 0))],
        out_specs=[hid, hid, hid], out_shape=[jax.ShapeDtypeStruct((t, D_FF), bf16)] * 3,
        scratch_shapes=[pltpu.VMEM((tm, D_MODEL), bf16)],
        compiler_params=_params(("arbitrary", "arbitrary")),
    )(dpre, hg, hu, wd)


def _ffn_bwd_dx(dpre, dhg, dhu, wg, wu, *, tm, name, host=None):
    t = dpre.shape[0]
    nj = N_DEV

    def body(dp_ref, dhg_ref, dhu_ref, wg_ref, wu_ref, dx_ref, acc):
        j = pl.program_id(1)

        @pl.when(j == 0)
        def _():
            acc[...] = jnp.zeros_like(acc)

        acc[...] += (lax.dot_general(dhg_ref[...], wg_ref[...], _NT, preferred_element_type=f32)
                     + lax.dot_general(dhu_ref[...], wu_ref[...], _NT, preferred_element_type=f32))

        @pl.when(j == nj - 1)
        def _():
            dx_ref[...] = ALPHA * dp_ref[...] + acc[...]

    tok = pl.BlockSpec((tm, D_MODEL), lambda i, j: (i, 0))
    hid = pl.BlockSpec((tm, FF_TILE), lambda i, j: (i, j))
    wspec = pl.BlockSpec((None, D_MODEL, FF_TILE), lambda i, j: (j, 0, 0))
    return _hosted_call(
        host, body, name=name, grid=(t // tm, nj), in_specs=[tok, hid, hid, wspec, wspec],
        out_specs=[tok], out_shape=[jax.ShapeDtypeStruct((t, D_MODEL), f32)],
        scratch_shapes=[pltpu.VMEM((tm, D_MODEL), f32)],
        compiler_params=_params(("arbitrary", "arbitrary")),
    )(dpre, dhg, dhu, wg, wu)


def _mm(a, b, *, mode, out_dtype, tm, tn, tk, name, affine=None, a_cols=None, b_cols=None,
        b_blocked=False, out_blocked=False, out_scale=None):
    if mode == "nn":
        m_full, k_full = a.shape
        m_dim, k_dim = (m_full, a_cols[1]) if a_cols else (m_full, k_full)
    else:
        k_dim, m_full = a.shape
        m_dim = a_cols[1] if a_cols else m_full
    a_off = a_cols[0] if a_cols else 0
    if b_blocked:
        n_dim = b.shape[0] * b.shape[2]
        assert b.shape[2] == tn
    else:
        n_dim = b_cols[1] if b_cols else b.shape[1]
    b_off = b_cols[0] if b_cols else 0
    assert m_dim % tm == 0 and n_dim % tn == 0 and k_dim % tk == 0, (name, m_dim, n_dim, k_dim)
    nk = k_dim // tk

    def body(*refs):
        if affine is not None:
            a_ref, g_ref, s_ref, b_ref, o_ref, acc = refs
        else:
            a_ref, b_ref, o_ref, acc = refs
        k = pl.program_id(2)

        @pl.when(k == 0)
        def _():
            acc[...] = jnp.zeros_like(acc)

        av = a_ref[...]
        if affine is not None:
            av = av * g_ref[...] + s_ref[...]
        av = av.astype(bf16)
        bv = b_ref[...].astype(bf16)
        if mode == "nn":
            acc[...] += jnp.dot(av, bv, preferred_element_type=f32)
        else:
            acc[...] += lax.dot_general(av, bv, _TN, preferred_element_type=f32)

        @pl.when(k == nk - 1)
        def _():
            res = acc[...] if out_scale is None else acc[...] * out_scale
            o_ref[...] = res.astype(out_dtype)

    if mode == "nn":
        a_spec = pl.BlockSpec((tm, tk), lambda i, j, k: (i, k + a_off))
        aff_spec = pl.BlockSpec((1, tk), lambda i, j, k: (0, k + a_off))
    else:
        a_spec = pl.BlockSpec((tk, tm), lambda i, j, k: (k, i + a_off))
        aff_spec = pl.BlockSpec((1, tm), lambda i, j, k: (0, i + a_off))
    if b_blocked:
        b_spec = pl.BlockSpec((None, tk, tn), lambda i, j, k: (j, k, 0))
    else:
        b_spec = pl.BlockSpec((tk, tn), lambda i, j, k: (k, j + b_off))
    if out_blocked:
        o_spec = pl.BlockSpec((None, tm, tn), lambda i, j, k: (j, i, 0))
        o_shape = jax.ShapeDtypeStruct((n_dim // tn, m_dim, tn), out_dtype)
    else:
        o_spec = pl.BlockSpec((tm, tn), lambda i, j, k: (i, j))
        o_shape = jax.ShapeDtypeStruct((m_dim, n_dim), out_dtype)
    in_specs = [a_spec] + ([aff_spec, aff_spec] if affine is not None else []) + [b_spec]
    args = [a] + (list(affine) if affine is not None else []) + [b]
    return pl.pallas_call(
        body, name=name, grid=(m_dim // tm, n_dim // tn, nk), in_specs=in_specs, out_specs=o_spec,
        out_shape=o_shape, scratch_shapes=[pltpu.VMEM((tm, tn), f32)],
        compiler_params=_params(("arbitrary", "arbitrary", "arbitrary")),
    )(*args)


def _mm_tn(a, b, *, out_dtype, tm, mb, tn, nb, tk, name, affine=None, out_blocked=False, out_scale=None,
           pair=False, host=None):
    k_dim, m_dim = a.shape
    multi_b = isinstance(b, (list, tuple))
    b_list = list(b) if multi_b else [b]
    n_dim = nb * tn if multi_b else b.shape[1]
    assert m_dim % (mb * tm) == 0 and n_dim % (nb * tn) == 0 and k_dim % tk == 0, (name, m_dim, n_dim, k_dim)
    nk = k_dim // tk
    grid = (m_dim // (mb * tm), n_dim // (nb * tn), nk)
    if pair:
        assert mb * nb == 4 and grid[0] * grid[1] == 2 and out_dtype == bf16, name

    def body(*refs):
        if pair:
            refs, (acc, send_buf, recv_buf, keep, send_sems, recv_sems) = refs[:-6], refs[-6:]
        else:
            refs, acc = refs[:-1], refs[-1]
        a_ref, o_ref = refs[0], refs[-1]
        if affine is not None:
            g_ref, s_ref = refs[1:3]
        b_refs = refs[3 if affine is not None else 1:-1]
        k = pl.program_id(2)

        @pl.when(k == 0)
        def _():
            acc[...] = jnp.zeros_like(acc)

        av = a_ref[...]
        if affine is not None:
            av = av * g_ref[...] + s_ref[...]
        av = av.astype(bf16)
        if multi_b:
            pieces = [r[...].astype(bf16) for r in b_refs]
        else:
            bv = b_refs[0][...].astype(bf16)
            pieces = [bv[:, jn * tn:(jn + 1) * tn] for jn in range(nb)]
        for im in range(mb):
            a_t = av[:, im * tm:(im + 1) * tm].T
            for jn in range(nb):
                acc[im * nb + jn] += jnp.dot(a_t, pieces[jn], preferred_element_type=f32)

        def scaled(v):
            return v if out_scale is None else v * out_scale

        @pl.when(k == nk - 1)
        def _():
            if pair:
                x, y, c = lax.axis_index("x"), lax.axis_index("y"), lax.axis_index("c")
                window = pl.program_id(0) + pl.program_id(1)

                def swap(w, cc):
                    return pltpu.make_async_remote_copy(
                        src_ref=send_buf.at[w, cc], dst_ref=recv_buf.at[w, cc],
                        send_sem=send_sems.at[2 * w + cc], recv_sem=recv_sems.at[2 * w + cc],
                        device_id=(x, y, 1 - c), device_id_type=MESH_T)

                for w in range(2):
                    @pl.when(window == w)
                    def _(w=w):
                        for cc in range(2):
                            send_buf[w, cc] = scaled(acc[2 * cc + 1 - c]).astype(bf16)
                            swap(w, cc).start()
                            if w == 0:
                                keep[cc] = scaled(acc[2 * cc + c])

                @pl.when(window == 1)
                def _():
                    for w in range(2):
                        for cc in range(2):
                            swap(w, cc).wait_recv()
                            mine = keep[cc] if w == 0 else scaled(acc[2 * cc + c])
                            o_ref[2 * w + cc] = (mine + recv_buf[w, cc].astype(f32)).astype(bf16)
                    for w in range(2):
                        for cc in range(2):
                            swap(w, cc).wait_send()
                return
            for im in range(mb):
                for jn in range(nb):
                    res = scaled(acc[im * nb + jn])
                    if out_blocked:
                        o_ref[jn, im * tm:(im + 1) * tm, :] = res.astype(out_dtype)
                    else:
                        o_ref[im * tm:(im + 1) * tm, jn * tn:(jn + 1) * tn] = res.astype(out_dtype)

    a_spec = pl.BlockSpec((tk, mb * tm), lambda i, j, k: (k, i))
    aff_spec = pl.BlockSpec((1, mb * tm), lambda i, j, k: (0, i))
    if multi_b:
        b_specs = [pl.BlockSpec((tk, tn), lambda i, j, k: (k, 0))] * nb
    else:
        b_specs = [pl.BlockSpec((tk, nb * tn), lambda i, j, k: (k, j))]
    scratch = [pltpu.VMEM((mb * nb, tm, tn), f32)]
    if pair:
        o_spec = pl.BlockSpec((4, tm, tn), lambda i, j, k: (0, 0, 0))
        o_shape = jax.ShapeDtypeStruct((4, tm, tn), out_dtype)
        scratch += [pltpu.VMEM((2, 2, tm, tn), bf16), pltpu.VMEM((2, 2, tm, tn), bf16), pltpu.VMEM((2, tm, tn), f32),
                    pltpu.SemaphoreType.DMA((4,)), pltpu.SemaphoreType.DMA((4,))]
    elif out_blocked:
        o_spec = pl.BlockSpec((nb, mb * tm, tn), lambda i, j, k: (j, i, 0))
        o_shape = jax.ShapeDtypeStruct((n_dim // tn, m_dim, tn), out_dtype)
    else:
        o_spec = pl.BlockSpec((mb * tm, nb * tn), lambda i, j, k: (i, j))
        o_shape = jax.ShapeDtypeStruct((m_dim, n_dim), out_dtype)
    in_specs = [a_spec] + ([aff_spec, aff_spec] if affine is not None else []) + b_specs
    args = [a] + (list(affine) if affine is not None else []) + b_list
    res = _hosted_call(
        host, body, name=name, grid=grid, in_specs=in_specs, out_specs=o_spec, out_shape=o_shape,
        scratch_shapes=scratch, compiler_params=_params(("arbitrary", "arbitrary", "arbitrary")),
    )(*args)
    return res[0] if host is None else res


def _in_proj(xhat, g, b, w_in, *, tm, name):
    t = xhat.shape[0]
    n_qkv, n_l = 3 * FOX_W, 2 * LRU_W

    def body(x_ref, g_ref, b_ref, w_ref, qkv_ref, zl_ref, zfg_ref):
        xb = (x_ref[...] * g_ref[...] + b_ref[...]).astype(bf16)
        qkv_ref[...] = jnp.dot(xb, w_ref[:, :n_qkv], preferred_element_type=f32).astype(bf16)
        zl_ref[...] = jnp.dot(xb, w_ref[:, n_qkv:n_qkv + n_l], preferred_element_type=f32)
        zfg_ref[...] = jnp.dot(xb, w_ref[:, n_qkv + n_l:], preferred_element_type=f32)

    row = pl.BlockSpec((1, D_MODEL), lambda i: (0, 0))
    return pl.pallas_call(
        body, name=name, grid=(t // tm,),
        in_specs=[pl.BlockSpec((tm, D_MODEL), lambda i: (i, 0)), row, row,
                  pl.BlockSpec(w_in.shape, lambda i: (0, 0))],
        out_specs=[pl.BlockSpec((tm, n_qkv), lambda i: (i, 0)), pl.BlockSpec((tm, n_l), lambda i: (i, 0)),
                   pl.BlockSpec((tm, LANES), lambda i: (i, 0))],
        out_shape=[jax.ShapeDtypeStruct((t, n_qkv), bf16), jax.ShapeDtypeStruct((t, n_l), f32),
                   jax.ShapeDtypeStruct((t, LANES), f32)],
        compiler_params=_params(("arbitrary",)),
    )(xhat, g, b, w_in)


def _mmln(pairs, *, tm, name, resid=None, resid_scale=1.0, epi=None, ln=None, n_out=D_MODEL):
    t = pairs[0][0].shape[0]
    n_pairs = len(pairs)
    n_resid = 0 if resid is None else len(resid) - 1

    def body(*refs):
        pos = 0
        val = None
        for p in range(n_pairs):
            a_ref, b_ref = refs[pos], refs[pos + 1]
            pos += 2
            av = a_ref[...].astype(bf16)
            bv = b_ref[...].astype(bf16)
            if pairs[p][6] == "nn":
                term = jnp.dot(av, bv, preferred_element_type=f32)
            else:
                term = lax.dot_general(av, bv, _NT, preferred_element_type=f32)
            val = term if val is None else val + term
        if resid is not None:
            if resid[0] == "plain":
                r = refs[pos][...]
            else:
                r = refs[pos][...] * refs[pos + 1][...] + refs[pos + 2][...]
            pos += n_resid
            val = val + resid_scale * r
        if epi is None:
            o_ref = refs[pos]
            o_ref[...] = val.astype(o_ref.dtype)
        elif epi == "ln_fwd":
            xo, rstd = _ln_fwd_tile(val)
            refs[pos][...] = xo
            refs[pos + 1][...] = rstd
        else:
            xh_ref, rs_ref, g_ref, dx_ref, gg_ref, gb_ref = refs[pos:pos + 6]
            dprev, gg, gb = _ln_bwd_tile(val, xh_ref[...], rs_ref[...], g_ref[...])
            dx_ref[...] = dprev
            i = pl.program_id(0)

            @pl.when(i == 0)
            def _():
                gg_ref[...] = gg
                gb_ref[...] = gb

            @pl.when(i > 0)
            def _():
                gg_ref[...] += gg
                gb_ref[...] += gb

    in_specs, args = [], []
    for (a, acb, aw, b, bcb, bw, mode) in pairs:
        in_specs.append(pl.BlockSpec((tm, aw), lambda i, acb=acb: (i, acb)))
        args.append(a)
        if mode == "nn":
            in_specs.append(pl.BlockSpec((aw, n_out), lambda i, bcb=bcb: (bcb, 0)))
        else:
            in_specs.append(pl.BlockSpec((n_out, bw), lambda i, bcb=bcb: (0, bcb)))
        args.append(b)
    tok = pl.BlockSpec((tm, n_out), lambda i: (i, 0))
    row = pl.BlockSpec((1, n_out), lambda i: (0, 0))
    col = pl.BlockSpec((tm, 1), lambda i: (i, 0))
    if resid is not None:
        in_specs += [tok] if resid[0] == "plain" else [tok, row, row]
        args += list(resid[1:])
    if epi is None:
        out_specs, out_shape = tok, jax.ShapeDtypeStruct((t, n_out), f32)
    elif epi == "ln_fwd":
        out_specs = [tok, col]
        out_shape = [jax.ShapeDtypeStruct((t, n_out), f32), jax.ShapeDtypeStruct((t, 1), f32)]
    else:
        in_specs += [tok, col, row]
        args += list(ln)
        out_specs = [tok, row, row]
        out_shape = [jax.ShapeDtypeStruct((t, n_out), f32)] + [jax.ShapeDtypeStruct((1, n_out), f32)] * 2
    return pl.pallas_call(
        body, name=name, grid=(t // tm,), in_specs=in_specs, out_specs=out_specs, out_shape=out_shape,
        compiler_params=_params(("arbitrary",)),
    )(*args)


CUM_TILE = 512


def _tri(n, lower):
    r = lax.broadcasted_iota(jnp.int32, (n, n), 0)
    c = lax.broadcasted_iota(jnp.int32, (n, n), 1)
    return jnp.where((r >= c) if lower else (r <= c), 1.0, 0.0).astype(f32)


def _cum_fwd(zfg, bfg, *, name):
    t = zfg.shape[0]

    def body(z_ref, b_ref, o_ref, carry):
        @pl.when(pl.program_id(0) == 0)
        def _():
            carry[...] = jnp.zeros_like(carry)

        ls = -_softplus(-(z_ref[...] + b_ref[...]))
        c = jnp.dot(_tri(CUM_TILE, True), ls, preferred_element_type=f32,
                    precision=lax.Precision.HIGHEST) + carry[...]
        o_ref[...] = c
        carry[...] = c[CUM_TILE - 1:CUM_TILE, :]

    blk = pl.BlockSpec((CUM_TILE, LANES), lambda i: (i, 0))
    return pl.pallas_call(
        body, name=name, grid=(t // CUM_TILE,),
        in_specs=[blk, pl.BlockSpec((1, LANES), lambda i: (0, 0))], out_specs=blk,
        out_shape=jax.ShapeDtypeStruct((t, LANES), f32), scratch_shapes=[pltpu.VMEM((1, LANES), f32)],
        compiler_params=_params(("arbitrary",)),
    )(zfg, bfg)


def _cum_bwd(dcum_q, dcum_k, zfg, bfg, *, name):
    t = zfg.shape[0]
    n = t // CUM_TILE

    def body(d_ref, d2_ref, z_ref, b_ref, o_ref, s_ref, carry):
        i = pl.program_id(0)

        @pl.when(i == 0)
        def _():
            carry[...] = jnp.zeros_like(carry)

        dls = jnp.dot(_tri(CUM_TILE, False), d_ref[...] + d2_ref[...], preferred_element_type=f32,
                      precision=lax.Precision.HIGHEST) + carry[...]
        carry[...] = dls[0:1, :]
        lane = lax.broadcasted_iota(jnp.int32, (CUM_TILE, LANES), 1)
        dfg = jnp.where(lane < HEADS, dls * _sigmoid(-(z_ref[...] + b_ref[...])), 0.0)
        o_ref[...] = dfg
        tot = jnp.sum(dfg, axis=0, keepdims=True)

        @pl.when(i == 0)
        def _():
            s_ref[...] = tot

        @pl.when(i > 0)
        def _():
            s_ref[...] += tot

    blk = pl.BlockSpec((CUM_TILE, LANES), lambda i: (n - 1 - i, 0))
    row = pl.BlockSpec((1, LANES), lambda i: (0, 0))
    return pl.pallas_call(
        body, name=name, grid=(n,), in_specs=[blk, blk, blk, row], out_specs=[blk, row],
        out_shape=[jax.ShapeDtypeStruct((t, LANES), f32), jax.ShapeDtypeStruct((1, LANES), f32)],
        scratch_shapes=[pltpu.VMEM((1, LANES), f32)],
        compiler_params=_params(("arbitrary",)),
    )(dcum_q, dcum_k, zfg, bfg)


ATT_TILE = 512


def _causal(i, j, transposed):
    r = lax.broadcasted_iota(jnp.int32, (ATT_TILE, ATT_TILE), 0)
    c = lax.broadcasted_iota(jnp.int32, (ATT_TILE, ATT_TILE), 1)
    if transposed:
        return (c + i * ATT_TILE) >= (r + j * ATT_TILE)
    return (r + i * ATT_TILE) >= (c + j * ATT_TILE)


ATT_W = HEADS * LANES


def _data_lane(h):
    return HEAD_D * (h % 2)


def _extra_lane(h):
    return HEAD_D - _data_lane(h)


def _split3(x):
    hi = x.astype(bf16)
    rest = x - hi.astype(f32)
    mid = rest.astype(bf16)
    lo = (rest - mid.astype(f32)).astype(bf16)
    return hi, mid, lo


def _three_pieces(x):
    hi, mid, lo = (p.astype(f32) for p in _split3(x))
    return (hi + pltpu.roll(mid, HEADS, axis=1) + pltpu.roll(lo, 2 * HEADS, axis=1)).astype(bf16)


def _move(h, first):
    r = lax.broadcasted_iota(jnp.int32, (LANES, LANES), 0)
    c = lax.broadcasted_iota(jnp.int32, (LANES, LANES), 1)
    hit = functools.reduce(jnp.logical_or, [jnp.logical_and(r == HEADS * q + h, c == first + q) for q in range(3)])
    return jnp.where(hit, 1.0, 0.0).astype(bf16)


def _ones_from(first, rows):
    lane = lax.broadcasted_iota(jnp.int32, (rows, LANES), 1)
    return jnp.where(jnp.logical_and(lane >= first, lane < first + 3), 1.0, 0.0)


def _own_lanes(h, rows):
    lane = lax.broadcasted_iota(jnp.int32, (rows, LANES), 1)
    return (lane < HEAD_D) if h % 2 == 0 else (lane >= HEAD_D)


def _head_values(x):
    lane = lax.broadcasted_iota(jnp.int32, x.shape, 1)
    return jnp.where(lane < HEADS, x, 0.0)


def _attn_prep_fwd(qkv, cum, *, tm, name):
    t = qkv.shape[0]

    def body(q_ref, k_ref, v_ref, c_ref, qa_ref, ka_ref, va_ref):
        c3 = _three_pieces(_head_values(c_ref[...]))
        ones = jnp.ones((tm, LANES), bf16)
        for h in range(HEADS):
            pair = slice(LANES * (h // 2), LANES * (h // 2 + 1))
            hs = slice(LANES * h, LANES * (h + 1))
            base, own = _extra_lane(h), _own_lanes(h, tm)
            eq = jnp.dot(c3, _move(h, base), preferred_element_type=f32) + _ones_from(base + 3, tm)
            ek = _ones_from(base, tm) - jnp.dot(c3, _move(h, base + 3), preferred_element_type=f32)
            qa_ref[:, hs] = jnp.where(own, q_ref[:, pair] * ATT_SCALE, eq.astype(bf16))
            ka_ref[:, hs] = jnp.where(own, k_ref[:, pair], ek.astype(bf16))
            va_ref[:, hs] = jnp.where(own, v_ref[:, pair], ones)

    wide = pl.BlockSpec((tm, ATT_W), lambda i: (i, 0))
    out = jax.ShapeDtypeStruct((t, ATT_W), bf16)
    return pl.pallas_call(
        body, name=name, grid=(t // tm,),
        in_specs=[pl.BlockSpec((tm, FOX_W), lambda i: (i, 0)), pl.BlockSpec((tm, FOX_W), lambda i: (i, 1)),
                  pl.BlockSpec((tm, FOX_W), lambda i: (i, 2)), pl.BlockSpec((tm, LANES), lambda i: (i, 0))],
        out_specs=[wide] * 3, out_shape=[out] * 3, compiler_params=_params(("arbitrary",)),
    )(qkv, qkv, qkv, cum)


def _attn_prep_bwd(qkv, cum, lse, dmix, o, *, tm, name):
    t = qkv.shape[0]

    def body(q_ref, c_ref, l_ref, do_ref, o_ref, qa_ref, da_ref):
        b3 = _three_pieces(_head_values(c_ref[...] - l_ref[...]))
        r = lax.broadcasted_iota(jnp.int32, (FOX_W, LANES), 0)
        c = lax.broadcasted_iota(jnp.int32, (FOX_W, LANES), 1)
        per_head = jnp.where(r // HEAD_D == c, 1.0, 0.0).astype(bf16)
        delta = sum(jnp.dot(p, per_head, preferred_element_type=f32) for p in _split3(do_ref[...] * o_ref[...]))
        d3 = _three_pieces(delta)
        for h in range(HEADS):
            pair = slice(LANES * (h // 2), LANES * (h // 2 + 1))
            hs = slice(LANES * h, LANES * (h + 1))
            base, own = _extra_lane(h), _own_lanes(h, tm)
            eq = jnp.dot(b3, _move(h, base), preferred_element_type=f32) + _ones_from(base + 3, tm)
            ed = -jnp.dot(d3, _move(h, base), preferred_element_type=f32)
            qa_ref[:, hs] = jnp.where(own, q_ref[:, pair] * ATT_SCALE, eq.astype(bf16))
            da_ref[:, hs] = jnp.where(own, do_ref[:, pair].astype(bf16), ed.astype(bf16))

    wide = pl.BlockSpec((tm, ATT_W), lambda i: (i, 0))
    half = pl.BlockSpec((tm, FOX_W), lambda i: (i, 0))
    col = pl.BlockSpec((tm, LANES), lambda i: (i, 0))
    out = jax.ShapeDtypeStruct((t, ATT_W), bf16)
    return pl.pallas_call(
        body, name=name, grid=(t // tm,), in_specs=[half, col, col, half, half],
        out_specs=[wide] * 2, out_shape=[out] * 2, compiler_params=_params(("arbitrary",)),
    )(qkv, cum, lse, dmix, o)


def _attn_fwd2(q_aug, k_aug, v_aug, *, name, host=None):
    t = q_aug.shape[0]
    n = t // ATT_TILE
    tq = ATT_TILE

    def body(q_ref, k_ref, v_ref, o_ref, lse_ref, acc, m_s):
        i = pl.program_id(0)
        j = pl.program_id(1)

        @pl.when(j == 0)
        def _():
            acc[...] = jnp.zeros_like(acc)
            m_s[...] = jnp.full_like(m_s, NEG_BIG)

        def block(masked):
            mask = _causal(i, j, False) if masked else None
            for h in range(HEADS):
                hs = slice(LANES * h, LANES * (h + 1))
                s = lax.dot_general(q_ref[:, hs], k_ref[:, hs], _NT, preferred_element_type=f32)
                if masked:
                    s = jnp.where(mask, s, NEG_BIG)
                blocks = [s[:, LANES * b:LANES * (b + 1)] for b in range(tq // LANES)]
                m_old = m_s[h]
                m_new = jnp.maximum(m_old, jnp.broadcast_to(
                    jnp.max(functools.reduce(jnp.maximum, blocks), axis=-1, keepdims=True), (tq, LANES)))
                p = jnp.concatenate([jnp.exp(b - m_new) for b in blocks], axis=1).astype(bf16)
                acc[h] = jnp.exp(m_old - m_new) * acc[h] + jnp.dot(p, v_ref[:, hs], preferred_element_type=f32)
                m_s[h] = m_new

        @pl.when(j < i)
        def _():
            block(False)

        @pl.when(j == i)
        def _():
            block(True)
            lse_ref[...] = jnp.zeros_like(lse_ref)
            for h in range(HEADS):
                a = acc[h]
                l = a[:, _extra_lane(h):_extra_lane(h) + 1]
                o_ref[:, HEAD_D * h:HEAD_D * (h + 1)] = a[:, _data_lane(h):_data_lane(h) + HEAD_D] / l
                lse_ref[:, h:h + 1] = m_s[h][:, 0:1] + jnp.log(l)

    kv = pl.BlockSpec((tq, ATT_W), lambda i, j: (jnp.minimum(i, j), 0))
    return _hosted_call(
        host, body, name=name, grid=(n, n),
        in_specs=[pl.BlockSpec((tq, ATT_W), lambda i, j: (i, 0)), kv, kv],
        out_specs=[pl.BlockSpec((tq, FOX_W), lambda i, j: (i, 0)), pl.BlockSpec((tq, LANES), lambda i, j: (i, 0))],
        out_shape=[jax.ShapeDtypeStruct((t, FOX_W), f32), jax.ShapeDtypeStruct((t, LANES), f32)],
        scratch_shapes=[pltpu.VMEM((HEADS, tq, LANES), f32), pltpu.VMEM((HEADS, tq, LANES), f32)],
        compiler_params=_params(("arbitrary", "arbitrary")),
    )(q_aug, k_aug, v_aug)


def _attn_bwd(qb_aug, k_aug, v_aug, do_aug, *, name, host=None):
    t = qb_aug.shape[0]
    n = t // ATT_TILE
    tk = ATT_TILE

    def body(q_ref, k_ref, v_ref, do_ref, dq_ref, dcq_ref, dk_ref, dv_ref, dck_ref, dk_acc, dv_acc, dq_all):
        j = pl.program_id(0)
        i = pl.program_id(1)

        @pl.when(jnp.logical_and(i == 0, j == 0))
        def _():
            dq_all[...] = jnp.zeros_like(dq_all)

        @pl.when(i == 0)
        def _():
            dk_acc[...] = jnp.zeros_like(dk_acc)
            dv_acc[...] = jnp.zeros_like(dv_acc)

        def block(masked):
            mask = _causal(i, j, True) if masked else None
            for h in range(HEADS):
                hs = slice(LANES * h, LANES * (h + 1))
                qh = q_ref[:, hs]
                doh = do_ref[:, hs]
                kh = k_ref[:, hs]
                s_t = lax.dot_general(kh, qh, _NT, preferred_element_type=f32)
                if masked:
                    s_t = jnp.where(mask, s_t, NEG_BIG)
                p_t = jnp.exp(s_t)
                dv_acc[h] += jnp.dot(p_t.astype(bf16), doh, preferred_element_type=f32)
                dp_t = lax.dot_general(v_ref[:, hs], doh, _NT, preferred_element_type=f32)
                ds_t = (p_t * dp_t).astype(bf16)
                dk_acc[h] += jnp.dot(ds_t, qh, preferred_element_type=f32)
                dq_all[i, h] += lax.dot_general(ds_t, kh, _TN, preferred_element_type=f32)

        @pl.when(i > j)
        def _():
            block(False)

        @pl.when(i == j)
        def _():
            block(True)
            dcq_ref[...] = jnp.zeros_like(dcq_ref)
            for h in range(HEADS):
                a = dq_all[j, h]
                dq_ref[:, HEAD_D * h:HEAD_D * (h + 1)] = (
                    a[:, _data_lane(h):_data_lane(h) + HEAD_D] * ATT_SCALE).astype(bf16)
                dcq_ref[:, h:h + 1] = a[:, _extra_lane(h):_extra_lane(h) + 1]

        @pl.when(i == n - 1)
        def _():
            dck_ref[...] = jnp.zeros_like(dck_ref)
            for h in range(HEADS):
                a = dk_acc[h]
                cols = slice(_data_lane(h), _data_lane(h) + HEAD_D)
                dk_ref[:, HEAD_D * h:HEAD_D * (h + 1)] = a[:, cols].astype(bf16)
                dv_ref[:, HEAD_D * h:HEAD_D * (h + 1)] = dv_acc[h][:, cols].astype(bf16)
                dck_ref[:, h:h + 1] = -a[:, _extra_lane(h) + 3:_extra_lane(h) + 4]

    own = pl.BlockSpec((tk, ATT_W), lambda j, i: (j, 0))
    qs = pl.BlockSpec((tk, ATT_W), lambda j, i: (jnp.maximum(i, j), 0))
    half = pl.BlockSpec((tk, FOX_W), lambda j, i: (j, 0))
    col = pl.BlockSpec((tk, LANES), lambda j, i: (j, 0))
    return _hosted_call(
        host, body, name=name, grid=(n, n), in_specs=[qs, own, own, qs],
        out_specs=[half, col, half, half, col],
        out_shape=[jax.ShapeDtypeStruct((t, FOX_W), bf16), jax.ShapeDtypeStruct((t, LANES), f32),
                   jax.ShapeDtypeStruct((t, FOX_W), bf16), jax.ShapeDtypeStruct((t, FOX_W), bf16),
                   jax.ShapeDtypeStruct((t, LANES), f32)],
        scratch_shapes=[pltpu.VMEM((HEADS, tk, LANES), f32), pltpu.VMEM((HEADS, tk, LANES), f32),
                        pltpu.VMEM((n, HEADS, tk, LANES), f32)],
        compiler_params=_params(("arbitrary", "arbitrary")),
    )(qb_aug, k_aug, v_aug, do_aug)


LRU_CHUNK = 64
LRU_G = 256
SUB = 8


def _row_ids(n):
    return lax.broadcasted_iota(jnp.int32, (n, LRU_G), 0)


def _shift_rows_down(ext, s):
    return pltpu.roll(ext, s, axis=0)[SUB:, :]


def _shift_rows_up(ext, s, n):
    return pltpu.roll(ext, ext.shape[0] - s, axis=0)[:n, :]


def _lru_gates(u, wa_ref, ba_ref, wx_ref, bx_ref, sp):
    ub = u.astype(bf16)
    r = _sigmoid(jnp.dot(ub, wa_ref[...], preferred_element_type=f32) + ba_ref[...])
    gi = _sigmoid(jnp.dot(ub, wx_ref[...], preferred_element_type=f32) + bx_ref[...])
    log_a = -LRU_C * r * sp
    a = jnp.exp(log_a)
    s = jnp.sqrt(_one_minus_exp(2.0 * log_a))
    return r, gi, a, s


def _conv_window(lx_ref, r0, ci):
    cur = lx_ref[pl.ds(r0, LRU_CHUNK), :]
    p0 = pl.multiple_of(jnp.maximum(r0 - SUB, 0), SUB)
    prev = jnp.where(ci > 0, lx_ref[pl.ds(p0, SUB), :], 0.0)
    return cur, jnp.concatenate([prev, cur], axis=0)


def _lru_fwd(zl, conv_w, conv_b, wa, ba, wx, bx, lam, *, name, host=None):
    t = zl.shape[0]
    n_chunk = t // LRU_CHUNK

    def body(lx_ref, lg_ref, cw_ref, cb_ref, wa_ref, ba_ref, wx_ref, bx_ref, lam_ref, u_ref, h_ref, y_ref):
        sp = _softplus(-lam_ref[...])
        rows = _row_ids(SUB)

        def chunk(ci, hc):
            r0 = pl.multiple_of(ci * LRU_CHUNK, LRU_CHUNK)
            cur, ext = _conv_window(lx_ref, r0, ci)
            u = cb_ref[...] + cw_ref[3:4, :] * cur
            for k in range(3):
                u = u + cw_ref[k:k + 1, :] * _shift_rows_down(ext, 3 - k)
            r, gi, a, s = _lru_gates(u, wa_ref, ba_ref, wx_ref, bx_ref, sp)
            b = s * (gi * u)
            tiles = []
            for q in range(LRU_CHUNK // SUB):
                ta = a[SUB * q:SUB * (q + 1), :]
                tb = b[SUB * q:SUB * (q + 1), :]
                for d in (1, 2, 4):
                    a_sh = jnp.where(rows >= d, pltpu.roll(ta, d, axis=0), 1.0)
                    b_sh = jnp.where(rows >= d, pltpu.roll(tb, d, axis=0), 0.0)
                    tb = ta * b_sh + tb
                    ta = ta * a_sh
                hq = tb + ta * hc
                hc = hq[SUB - 1:SUB, :]
                tiles.append(hq)
            h = jnp.concatenate(tiles, axis=0)
            u_ref[pl.ds(r0, LRU_CHUNK), :] = u
            h_ref[pl.ds(r0, LRU_CHUNK), :] = h
            gel, _ = _gelu_and_grad(lg_ref[pl.ds(r0, LRU_CHUNK), :])
            y_ref[pl.ds(r0, LRU_CHUNK), :] = gel * h
            return hc

        lax.fori_loop(0, n_chunk, chunk, jnp.zeros((1, LRU_G), f32))

    seq = lambda cb: pl.BlockSpec((t, LRU_G), lambda c, cb=cb: (0, c + cb))
    rowc = pl.BlockSpec((1, LRU_G), lambda c: (0, c))
    diag = pl.BlockSpec((LRU_G, LRU_G), lambda c: (c, c))
    out = jax.ShapeDtypeStruct((t, LRU_W), f32)
    return _hosted_call(
        host, body, name=name, grid=(LRU_W // LRU_G,),
        in_specs=[seq(0), seq(LRU_W // LRU_G), pl.BlockSpec((4, LRU_G), lambda c: (0, c)),
                  rowc, diag, rowc, diag, rowc, rowc],
        out_specs=[seq(0)] * 3, out_shape=[out] * 3,
        compiler_params=_params(("arbitrary",)),
    )(zl, zl, conv_w, conv_b, wa, ba, wx, bx, lam)


def _lru_bwd(dmix, zl, u_all, h_all, conv_w, wa, ba, wx, bx, lam, *, name, host=None):
    t = zl.shape[0]
    n_chunk = t // LRU_CHUNK

    def body(dy_ref, lx_ref, lg_ref, u_ref, h_ref, cw_ref, wa_ref, ba_ref, wx_ref, bx_ref, lam_ref,
             dlx_ref, dlg_ref, dcw_ref, dcb_ref, dba_ref, dbx_ref, dlam_ref, dwa_ref, dwx_ref, dpr_s, dpx_s):
        lam_v = lam_ref[...]
        sp = _softplus(-lam_v)
        rows = _row_ids(SUB)
        rows_c = _row_ids(LRU_CHUNK)
        zero_row = jnp.zeros((1, LRU_G), f32)

        def chunk(step, carry):
            dh_c, a_next0, du_next, dsp, dba, dbx, dcb, dw0, dw1, dw2, dw3 = carry
            ci = n_chunk - 1 - step
            r0 = pl.multiple_of(ci * LRU_CHUNK, LRU_CHUNK)
            sl = pl.ds(r0, LRU_CHUNK)
            u = u_ref[sl, :]
            r, gi, a, s = _lru_gates(u, wa_ref, ba_ref, wx_ref, bx_ref, sp)
            h = h_ref[sl, :]
            p0 = pl.multiple_of(jnp.maximum(r0 - SUB, 0), SUB)
            h_before = jnp.where(ci > 0, h_ref[pl.ds(p0, SUB), :], 0.0)[SUB - 1:SUB, :]
            h_prev = jnp.where(rows_c == 0, h_before, pltpu.roll(h, 1, axis=0))
            gel, dgel = _gelu_and_grad(lg_ref[sl, :])
            dy = dy_ref[sl, :]
            dlg_ref[sl, :] = (dy * h * dgel).astype(bf16)
            g_in = dy * gel
            a_next = jnp.where(rows_c == LRU_CHUNK - 1, a_next0, pltpu.roll(a, LRU_CHUNK - 1, axis=0))
            tiles = [None] * (LRU_CHUNK // SUB)
            for q in reversed(range(LRU_CHUNK // SUB)):
                ta = a_next[SUB * q:SUB * (q + 1), :]
                tb = g_in[SUB * q:SUB * (q + 1), :]
                for d in (1, 2, 4):
                    a_sh = jnp.where(rows < SUB - d, pltpu.roll(ta, SUB - d, axis=0), 1.0)
                    b_sh = jnp.where(rows < SUB - d, pltpu.roll(tb, SUB - d, axis=0), 0.0)
                    tb = ta * b_sh + tb
                    ta = ta * a_sh
                dhq = tb + ta * dh_c
                dh_c = dhq[0:1, :]
                tiles[q] = dhq
            dh = jnp.concatenate(tiles, axis=0)
            da = dh * h_prev
            ds = dh * gi * u
            dgi = dh * s * u
            du = dh * s * gi
            dlog_a = da * a - ds * (a * a) / s
            dr = dlog_a * (-LRU_C * sp)
            dsp = dsp + jnp.sum(dlog_a * (-LRU_C * r), axis=0, keepdims=True)
            dpr = dr * r * (1.0 - r)
            dpx = dgi * gi * (1.0 - gi)
            dprb = dpr.astype(bf16)
            dpxb = dpx.astype(bf16)
            dpr_s[sl, :] = dprb
            dpx_s[sl, :] = dpxb
            du = du + (lax.dot_general(dprb, wa_ref[...], _NT, preferred_element_type=f32)
                       + lax.dot_general(dpxb, wx_ref[...], _NT, preferred_element_type=f32))
            dba = dba + jnp.sum(dpr, axis=0, keepdims=True)
            dbx = dbx + jnp.sum(dpx, axis=0, keepdims=True)
            dcb = dcb + jnp.sum(du, axis=0, keepdims=True)
            du_ext = jnp.concatenate([du, du_next], axis=0)
            dlx = cw_ref[3:4, :] * du
            for k in range(3):
                dlx = dlx + cw_ref[k:k + 1, :] * _shift_rows_up(du_ext, 3 - k, LRU_CHUNK)
            dlx_ref[sl, :] = dlx.astype(bf16)
            cur, ext = _conv_window(lx_ref, r0, ci)
            dws = [dw0, dw1, dw2, dw3 + jnp.sum(du * cur, axis=0, keepdims=True)]
            for k in range(3):
                dws[k] = dws[k] + jnp.sum(du * _shift_rows_down(ext, 3 - k), axis=0, keepdims=True)
            return (dh_c, a[0:1, :], du[0:SUB, :], dsp, dba, dbx, dcb, dws[0], dws[1], dws[2], dws[3])

        init = (zero_row, zero_row, jnp.zeros((SUB, LRU_G), f32)) + (zero_row,) * 8
        out = lax.fori_loop(0, n_chunk, chunk, init)
        _, _, _, dsp, dba, dbx, dcb, dw0, dw1, dw2, dw3 = out
        dlam_ref[...] = dsp * (-_sigmoid(-lam_v))
        dba_ref[...] = dba
        dbx_ref[...] = dbx
        dcb_ref[...] = dcb
        dcw_ref[...] = jnp.concatenate([dw0, dw1, dw2, dw3], axis=0)
        ub = u_ref[...].astype(bf16)
        dwa_ref[...] = lax.dot_general(ub, dpr_s[...], _TN, preferred_element_type=f32)
        dwx_ref[...] = lax.dot_general(ub, dpx_s[...], _TN, preferred_element_type=f32)

    seq = lambda cb: pl.BlockSpec((t, LRU_G), lambda c, cb=cb: (0, c + cb))
    rowc = pl.BlockSpec((1, LRU_G), lambda c: (0, c))
    diag = pl.BlockSpec((LRU_G, LRU_G), lambda c: (c, c))
    gate_out = pl.BlockSpec((None, LRU_G, LRU_G), lambda c: (c, 0, 0))
    row_shape = jax.ShapeDtypeStruct((1, LRU_W), f32)
    return _hosted_call(
        host, body, name=name, grid=(LRU_W // LRU_G,),
        in_specs=[seq(LRU_W // LRU_G), seq(0), seq(LRU_W // LRU_G), seq(0), seq(0),
                  pl.BlockSpec((4, LRU_G), lambda c: (0, c)),
                  diag, rowc, diag, rowc, rowc],
        out_specs=[seq(0), seq(0), pl.BlockSpec((4, LRU_G), lambda c: (0, c)), rowc, rowc, rowc, rowc,
                   gate_out, gate_out],
        out_shape=[jax.ShapeDtypeStruct((t, LRU_W), bf16)] * 2
        + [jax.ShapeDtypeStruct((4, LRU_W), f32)] + [row_shape] * 4
        + [jax.ShapeDtypeStruct((LRU_W // LRU_G, LRU_G, LRU_G), f32)] * 2,
        scratch_shapes=[pltpu.VMEM((t, LRU_G), bf16), pltpu.VMEM((t, LRU_G), bf16)],
        compiler_params=_params(("arbitrary",)),
    )(dmix, zl, zl, u_all, h_all, conv_w, wa, ba, wx, bx, lam)


def _pack_rows(a):
    flat = a.reshape(-1)
    rows = -(-flat.shape[0] // LANES)
    return jnp.pad(flat, (0, rows * LANES - flat.shape[0])).reshape(rows, LANES)


W_IN_PAD = 21 * LANES


def _w_in_join(blocks, *, name):
    tm = 256

    def body(b_ref, o_ref):
        o_ref[:, IN_COLS:] = jnp.zeros((tm, W_IN_PAD - IN_COLS), bf16)
        for q in range(N_DEV):
            o_ref[:, IN_SHARD * q:IN_SHARD * (q + 1)] = b_ref[q]

    return pl.pallas_call(
        body, name=name, grid=(D_MODEL // tm,),
        in_specs=[pl.BlockSpec((N_DEV, tm, IN_SHARD), lambda i: (0, i, 0))],
        out_specs=pl.BlockSpec((tm, W_IN_PAD), lambda i: (i, 0)),
        out_shape=jax.ShapeDtypeStruct((D_MODEL, W_IN_PAD), bf16), compiler_params=_params(("arbitrary",)),
    )(blocks)


def _w_in_split(main, fg, *, name):
    tm = 256
    n_main = main.shape[0]

    def body(m_ref, f_ref, o_ref):
        full = jnp.concatenate([m_ref[n] for n in range(n_main)] + [f_ref[...]], axis=1)
        for q in range(N_DEV):
            o_ref[q] = full[:, IN_SHARD * q:IN_SHARD * (q + 1)]

    return pl.pallas_call(
        body, name=name, grid=(D_MODEL // tm,),
        in_specs=[pl.BlockSpec((n_main, tm, 512), lambda i: (0, i, 0)), pl.BlockSpec((tm, LANES), lambda i: (i, 0))],
        out_specs=pl.BlockSpec((N_DEV, tm, IN_SHARD), lambda i: (0, i, 0)),
        out_shape=jax.ShapeDtypeStruct((N_DEV, D_MODEL, IN_SHARD), bf16), compiler_params=_params(("arbitrary",)),
    )(main, fg)


def _block_diag(w):
    eye = jnp.eye(HEADS, dtype=w.dtype)
    return jnp.einsum("hij,hk->hikj", w, eye).reshape(LRU_W, LRU_W)


def _diag_blocks(dw):
    per = dw.shape[1] // HEAD_D
    blocks = [dw[:, HEAD_D * b:HEAD_D * (b + 1), HEAD_D * b:HEAD_D * (b + 1)] for b in range(per)]
    return jnp.stack(blocks, axis=1).reshape(HEADS, HEAD_D, HEAD_D)


def _local_step(x, target, sent, small, *, tm=512, tm_ffn=1024):
    ln1 = (small["ln1_g"], small["ln1_b"])
    ln2 = (small["ln2_g"], small["ln2_b"])
    ln3 = (small["ln3_g"], small["ln3_b"])

    xh1, rs1, hg1, hu1, wg1, wu1, wd1, w_in_g, w_out_g, conv_w_g = _ffn1_fwd_gathering(
        x, (sent["ffn1_w_gate"], sent["ffn1_w_up"], sent["ffn1_w_down"]),
        _Exchange([sent["w_in"], sent["w_out"], sent["conv_w"]], gather=True), tm=tm_ffn, name="ffn1_fwd")
    w_in = _w_in_join(w_in_g, name="w_in_join")
    w_out = w_out_g.reshape(D_MODEL, D_MODEL)
    conv_w = conv_w_g.transpose(1, 0, 2).reshape(4, LRU_W)
    qkv, zl, zfg = _in_proj(xh1, ln1[0], ln1[1], w_in, tm=tm, name="in_proj")
    bfg = jnp.pad(small["b_forget"], ((0, 0), (0, LANES - HEADS)))
    cum = _cum_fwd(zfg, bfg, name="cum_fwd")
    q_aug, k_aug, v_aug = _attn_prep_fwd(qkv, cum, tm=tm, name="attn_prep_fwd")
    o, lse, wg2, wu2 = _attn_fwd2(
        q_aug, k_aug, v_aug, name="attn_fwd",
        host=_Hosts(_Exchange([sent["ffn2_w_gate"]], gather=True),
                    _Exchange([sent["ffn2_w_up"]], gather=True, direct=True)))
    wa_bd = _block_diag(small["rg_wa"]).astype(bf16)
    wx_bd = _block_diag(small["rg_wx"]).astype(bf16)
    ba = small["rg_ba"].reshape(1, LRU_W)
    bx = small["rg_bx"].reshape(1, LRU_W)
    u, h, lru, wd2 = _lru_fwd(zl, conv_w, small["conv_b"], wa_bd, ba, wx_bd, bx, small["lru_lambda"],
                              name="lru_fwd", host=_Exchange([sent["ffn2_w_down"]], gather=True))
    xh2, rs2 = _mmln([(o, 0, FOX_W, w_out, 0, D_MODEL, "nn"), (lru, 0, LRU_W, w_out, 1, D_MODEL, "nn")],
                     tm=tm, name="mix_fwd", resid=("affine", xh1) + ln1, resid_scale=ALPHA, epi="ln_fwd")
    dpre3, sq_rows, g_ln3g, g_ln3b, hg2, hu2 = _ffn_fwd_loss(
        xh2, ln2[0], ln2[1], wg2, wu2, wd2, ln3[0], ln3[1], target, tm=tm_ffn, name="ffn2_fwd_loss")

    dpre2, g_ln2g, g_ln2b, dhg2, dhu2, a2 = _ffn_bwd(dpre3, hg2, hu2, wg2, wu2, wd2,
                                                     (xh2, rs2, ln2[0]), tm=tm_ffn, name="ffn2_bwd")
    wgrad = dict(out_dtype=bf16, tm=D_MODEL, mb=1, tn=FF_TILE, nb=4, tk=512, pair=True)
    wdgrad = dict(out_dtype=bf16, tm=512, mb=4, tn=D_MODEL, nb=1, tk=512, out_scale=0.5, pair=True)
    between_chips = functools.partial(_Exchange, gather=False, chips=True)
    g_wg2 = _mm_tn(xh2, dhg2, name="g_wg2", affine=ln2, **wgrad)
    g_wu2 = _mm_tn(xh2, dhu2, name="g_wu2", affine=ln2, **wgrad)
    g_wd2 = _mm_tn(a2, dpre3, name="g_wd2", **wdgrad)

    dmix = _mmln([(dpre2, 0, D_MODEL, w_out, 0, D_MODEL, "nt")], tm=tm, name="dmix_bwd")
    g_wout_a = _mm(o, dpre2, mode="tn", out_dtype=bf16, tm=512, tn=D_MODEL, tk=512, name="g_wout_fox")
    g_wout_b = _mm(lru, dpre2, mode="tn", out_dtype=bf16, tm=512, tn=D_MODEL, tk=512, name="g_wout_lru")
    g_wout_blocked = jnp.concatenate([g_wout_a, g_wout_b], axis=0).reshape(N_DEV, D_MODEL // N_DEV, D_MODEL)
    dlx, dlg, g_cw, g_cb, g_ba, g_bx, g_lam, g_wa4, g_wx4, p_wg2, p_wout = _lru_bwd(
        dmix, zl, u, h, conv_w, wa_bd, ba, wx_bd, bx, small["lru_lambda"], name="lru_bwd",
        host=_Hosts(between_chips([g_wg2]), _Exchange([g_wout_blocked], gather=False)))
    qb_aug, do_aug = _attn_prep_bwd(qkv, cum, lse, dmix, o, tm=tm, name="attn_prep_bwd")
    dq, dcum_q, dk, dv, dcum_k, p_wu2, p_wd2 = _attn_bwd(qb_aug, k_aug, v_aug, do_aug, name="attn_bwd",
                                                         host=between_chips([g_wu2, g_wd2]))
    dfg, g_bf = _cum_bwd(dcum_q, dcum_k, zfg, bfg, name="cum_bwd")

    dz = [(dq, 0, 512), (dk, 1, 512), (dv, 2, 512), (dlx, 3, 512), (dlg, 4, 512), (dfg, 20, LANES)]
    dpre1, g_ln1g, g_ln1b = _mmln(
        [(arr, 0, w, w_in, cb, w, "nt") for (arr, cb, w) in dz],
        tm=tm, name="dx1_bwd", resid=("plain", dpre2), resid_scale=ALPHA, epi="ln_bwd", ln=(xh1, rs1, ln1[0]))
    g_win_main = _mm_tn(xh1, [arr for arr, _, _ in dz[:5]], out_dtype=bf16, tm=D_MODEL, mb=1, tn=512, nb=5, tk=512,
                        name="g_win", affine=ln1, out_blocked=True)
    g_win_fg = _mm(xh1, dfg, mode="tn", out_dtype=bf16, tm=D_MODEL, tn=LANES, tk=512, name="g_win_fg", affine=ln1)
    g_win_blocked = _w_in_split(g_win_main, g_win_fg, name="w_in_split")
    dhg1, dhu1, a1, p_win = _ffn_bwd_act(dpre1, hg1, hu1, wd1, tm=tm_ffn, name="ffn1_bwd_act",
                                         host=_Exchange([g_win_blocked], gather=False))
    small_g = {
        "ln1_g": g_ln1g, "ln1_b": g_ln1b, "b_forget": g_bf[:, :HEADS], "conv_w": g_cw, "conv_b": g_cb,
        "rg_wa": _diag_blocks(g_wa4), "rg_ba": g_ba.reshape(HEADS, HEAD_D),
        "rg_wx": _diag_blocks(g_wx4), "rg_bx": g_bx.reshape(HEADS, HEAD_D), "lru_lambda": g_lam,
        "ln2_g": g_ln2g, "ln2_b": g_ln2b, "ln3_g": g_ln3g, "ln3_b": g_ln3b,
    }
    small_g["loss"] = (0.5 / D_MODEL) * jnp.sum(sq_rows, keepdims=True)
    pieces = [_pack_rows(small_g[n]) for n in PACKED]
    packed = jnp.concatenate(pieces + [jnp.zeros((PACK_ROWS - sum(p.shape[0] for p in pieces), LANES), f32)])
    g_wg1, all_packed = _mm_tn(x, dhg1, name="g_wg1", host=_Exchange([packed], gather=True), **wgrad)
    g_wu1, p_wg1 = _mm_tn(x, dhu1, name="g_wu1", host=between_chips([g_wg1]), **wgrad)
    g_wd1, p_wu1 = _mm_tn(a1, dpre1, name="g_wd1", host=between_chips([g_wu1]), **wdgrad)
    grad_x, p_wd1 = _ffn_bwd_dx(dpre1, dhg1, dhu1, wg1, wu1, tm=tm_ffn, name="ffn1_bwd_dx",
                                host=between_chips([g_wd1]))
    parts = {
        "ffn1_w_gate": p_wg1, "ffn1_w_up": p_wu1, "ffn1_w_down": p_wd1, "w_in": p_win, "w_out": p_wout,
        "ffn2_w_gate": p_wg2, "ffn2_w_up": p_wu2, "ffn2_w_down": p_wd2,
    }
    return grad_x, parts, all_packed, {n: small_g[n].shape for n in PACKED}


def _adam_math(w, g, m, v):
    m2 = ADAM_B1 * m + (1.0 - ADAM_B1) * g
    v2 = ADAM_B2 * v + (1.0 - ADAM_B2) * (g * g)
    m_hat = m2 / (1.0 - ADAM_B1 ** ADAM_STEP)
    v_hat = v2 / (1.0 - ADAM_B2 ** ADAM_STEP)
    delta = -ADAM_LR * (m_hat / (jnp.sqrt(v_hat) + ADAM_EPS) + ADAM_WD * w)
    return delta, m2, v2


ADAM_TILE_ELEMS = 128 * 1024


def _adamw_big(items, *, name):
    _, r, c = items[0][1].shape
    n_parts = items[0][0].shape[0]
    n_items = len(items)
    assert all(it[1].shape == (1, r, c) and it[0].shape == (n_parts, r, c) for it in items), name
    tr = max(d for d in range(8, r + 1, 8) if r % d == 0 and d * c <= ADAM_TILE_ELEMS)

    def body(*refs):
        ins, outs = refs[:4 * n_items], refs[4 * n_items:]
        for k in range(n_items):
            p_ref, w_ref, m_ref, v_ref = ins[4 * k:4 * k + 4]
            g = p_ref[0].astype(f32)
            for q in range(1, n_parts):
                g = g + p_ref[q].astype(f32)
            d, m2, v2 = _adam_math(w_ref[...], g, m_ref[...], v_ref[...])
            for o_ref, val in zip(outs[4 * k:4 * k + 4], (g, d, m2, v2)):
                o_ref[...] = val

    blk = pl.BlockSpec((None, tr, c), lambda i: (0, i, 0))
    res = pl.pallas_call(
        body, name=name, grid=(r // tr,),
        in_specs=[pl.BlockSpec((n_parts, tr, c), lambda i: (0, i, 0)), blk, blk, blk] * n_items,
        out_specs=[blk] * (4 * n_items), out_shape=[jax.ShapeDtypeStruct((1, r, c), f32)] * (4 * n_items),
        compiler_params=_params(("arbitrary",)),
    )(*[a for it in items for a in it])
    return [res[4 * k:4 * k + 4] for k in range(n_items)]


def _adamw_small(items, *, name):
    n = len(items)

    def body(*refs):
        ins, outs = refs[:4 * n], refs[4 * n:]
        for k in range(n):
            g, w, m, v = (ins[4 * k + q][...] for q in range(4))
            d, m2, v2 = _adam_math(w, g, m, v)
            outs[3 * k][...] = d
            outs[3 * k + 1][...] = m2
            outs[3 * k + 2][...] = v2

    vm = pl.BlockSpec(memory_space=pltpu.VMEM)
    flat = [a for item in items for a in item]
    out_shape = [jax.ShapeDtypeStruct(item[1].shape, f32) for item in items for _ in range(3)]
    return pl.pallas_call(
        body, name=name, in_specs=[vm] * (4 * n), out_specs=[vm] * (3 * n), out_shape=out_shape,
    )(*flat)


def _sum_parts(parts, *, name):
    def body(p_ref, o_ref):
        acc = p_ref[0]
        for q in range(1, N_DEV):
            acc = acc + p_ref[q]
        o_ref[...] = acc

    vm = pl.BlockSpec(memory_space=pltpu.VMEM)
    return pl.pallas_call(
        body, name=name, in_specs=[vm], out_specs=vm, out_shape=jax.ShapeDtypeStruct(parts.shape[1:], f32),
    )(parts)


WEIGHTS = ["ffn1_w_gate", "ffn1_w_up", "ffn1_w_down", "ln1_g", "ln1_b", "w_in", "b_forget", "conv_w", "conv_b",
           "rg_wa", "rg_ba", "rg_wx", "rg_bx", "lru_lambda", "w_out", "ln2_g", "ln2_b",
           "ffn2_w_gate", "ffn2_w_up", "ffn2_w_down", "ln3_g", "ln3_b"]
BIG = ["ffn1_w_gate", "ffn1_w_up", "ffn1_w_down", "w_in", "w_out", "ffn2_w_gate", "ffn2_w_up", "ffn2_w_down"]
PACKED = ["ln1_g", "ln1_b", "ln2_g", "ln2_b", "ln3_g", "ln3_b", "conv_b", "rg_ba", "rg_bx", "lru_lambda",
          "conv_w", "rg_wa", "rg_wx", "b_forget", "loss"]
PACK_ROWS = 600


def _two_d(a):
    return a.reshape((-1, a.shape[-1]))


def _transport(a):
    return _two_d(a)


def kernel(x, ffn1_w_gate, ffn1_w_up, ffn1_w_down, ln1_g, ln1_b, w_in, b_forget, conv_w, conv_b, rg_wa, rg_ba, rg_wx, rg_bx, lru_lambda, w_out, ln2_g, ln2_b, ffn2_w_gate, ffn2_w_up, ffn2_w_down, ln3_g, ln3_b, loss_target, m_ffn1_w_gate, m_ffn1_w_up, m_ffn1_w_down, m_ln1_g, m_ln1_b, m_w_in, m_b_forget, m_conv_w, m_conv_b, m_rg_wa, m_rg_ba, m_rg_wx, m_rg_bx, m_lru_lambda, m_w_out, m_ln2_g, m_ln2_b, m_ffn2_w_gate, m_ffn2_w_up, m_ffn2_w_down, m_ln3_g, m_ln3_b, v_ffn1_w_gate, v_ffn1_w_up, v_ffn1_w_down, v_ln1_g, v_ln1_b, v_w_in, v_b_forget, v_conv_w, v_conv_b, v_rg_wa, v_rg_ba, v_rg_wx, v_rg_bx, v_lru_lambda, v_w_out, v_ln2_g, v_ln2_b, v_ffn2_w_gate, v_ffn2_w_up, v_ffn2_w_down, v_ln3_g, v_ln3_b):
    w_args = (ffn1_w_gate, ffn1_w_up, ffn1_w_down, ln1_g, ln1_b, w_in, b_forget, conv_w, conv_b, rg_wa, rg_ba, rg_wx, rg_bx, lru_lambda, w_out, ln2_g, ln2_b, ffn2_w_gate, ffn2_w_up, ffn2_w_down, ln3_g, ln3_b)
    m_args = (m_ffn1_w_gate, m_ffn1_w_up, m_ffn1_w_down, m_ln1_g, m_ln1_b, m_w_in, m_b_forget, m_conv_w, m_conv_b, m_rg_wa, m_rg_ba, m_rg_wx, m_rg_bx, m_lru_lambda, m_w_out, m_ln2_g, m_ln2_b, m_ffn2_w_gate, m_ffn2_w_up, m_ffn2_w_down, m_ln3_g, m_ln3_b)
    v_args = (v_ffn1_w_gate, v_ffn1_w_up, v_ffn1_w_down, v_ln1_g, v_ln1_b, v_w_in, v_b_forget, v_conv_w, v_conv_b, v_rg_wa, v_rg_ba, v_rg_wx, v_rg_bx, v_lru_lambda, v_w_out, v_ln2_g, v_ln2_b, v_ffn2_w_gate, v_ffn2_w_up, v_ffn2_w_down, v_ln3_g, v_ln3_b)
    w = dict(zip(WEIGHTS, w_args))
    m = dict(zip(WEIGHTS, m_args))
    v = dict(zip(WEIGHTS, v_args))
    me = 4 * lax.axis_index("x") + 2 * lax.axis_index("y") + lax.axis_index("c")

    sent = {n: _transport(w[n]).astype(bf16) for n in BIG}
    sent["conv_w"] = _two_d(w["conv_w"])
    small = {n: w[n] for n in ("ln1_g", "ln1_b", "ln2_g", "ln2_b", "ln3_g", "ln3_b", "b_forget", "conv_b",
                               "lru_lambda")}
    small.update({n: w[n][0] for n in ("rg_wa", "rg_ba", "rg_wx", "rg_bx")})

    grad_x, parts, all_packed, small_shapes = _local_step(x[0], loss_target[0], sent, small)

    total = _sum_parts(all_packed, name="sum_small_grads")
    grads, off = {}, 0
    for n in PACKED:
        size = math.prod(small_shapes[n])
        rows = -(-size // LANES)
        grads[n] = total[off:off + rows].reshape(-1)[:size].reshape(small_shapes[n])
        off += rows
    loss = grads.pop("loss").reshape(())
    grads["conv_w"] = lax.dynamic_slice_in_dim(grads["conv_w"], me * (LRU_W // N_DEV), LRU_W // N_DEV, axis=1)

    delta, new_m, new_v = {}, {}, {}
    for group in (("ffn1_w_gate", "ffn1_w_up", "ffn2_w_gate", "ffn2_w_up"), ("ffn1_w_down", "ffn2_w_down"),
                  ("w_in",), ("w_out",)):
        done = _adamw_big([(parts[n], w[n], m[n], v[n]) for n in group], name="adamw_" + group[0])
        for n, (g, d, m2, v2) in zip(group, done):
            grads[n], delta[n], new_m[n], new_v[n] = g, d, m2, v2
    small_names = [n for n in WEIGHTS if n not in BIG]
    outs = _adamw_small([(_two_d(grads[n]), _two_d(w[n]), _two_d(m[n]), _two_d(v[n])) for n in small_names],
                        name="adamw_small")
    for k, n in enumerate(small_names):
        delta[n], new_m[n], new_v[n] = outs[3 * k], outs[3 * k + 1], outs[3 * k + 2]

    def shaped(d):
        return [d[n].reshape(w[n].shape) for n in WEIGHTS]

    return (loss, grad_x[None], *shaped(grads), *shaped(delta), *shaped(new_m), *shaped(new_v))
```

```python
import functools
import math

import jax
import jax.numpy as jnp
from jax import lax
from jax.experimental import pallas as pl
from jax.experimental.pallas import tpu as pltpu

f32 = jnp.float32
bf16 = jnp.bfloat16

N_DEV = 8
D_MODEL = 1024
D_FF = 4096
FF_TILE = D_FF // N_DEV
FOX_W = 512
LRU_W = 512
HEADS = 8
HEAD_D = 64
IN_COLS = 2568
IN_SHARD = IN_COLS // N_DEV
LANES = 128
LN_EPS = 1e-5
ALPHA = 2.0 ** 0.25
ATT_SCALE = 1.0 / math.sqrt(HEAD_D)
LRU_C = 8.0
NEG_BIG = -1e30

ADAM_LR = 0.001
ADAM_B1 = 0.9
ADAM_B2 = 0.999
ADAM_EPS = 1e-08
ADAM_WD = 0.01
ADAM_STEP = 10

VMEM_LIMIT = 56 * 1024 * 1024
MESH_T = pl.DeviceIdType.MESH


def _params(sem, **kw):
    return pltpu.CompilerParams(dimension_semantics=sem, vmem_limit_bytes=VMEM_LIMIT, **kw)


def _sigmoid(x):
    return 1.0 / (1.0 + jnp.exp(-x))


def _sigmoid_tanh(x):
    return 0.5 * jnp.tanh(0.5 * x) + 0.5


def _softplus(x):
    return jnp.maximum(x, 0.0) + jnp.log(1.0 + jnp.exp(-jnp.abs(x)))


def _one_minus_exp(x):
    series = -x * (1.0 + x * (0.5 + x * (1.0 / 6 + x * (1.0 / 24 + x * (1.0 / 120 + x * (1.0 / 720))))))
    return jnp.where(x > -0.125, series, 1.0 - jnp.exp(x))


_GELU_C = math.sqrt(2.0 / math.pi)


def _gelu_and_grad(x):
    inner = _GELU_C * (x + 0.044715 * x * x * x)
    t = jnp.tanh(inner)
    g = 0.5 * x * (1.0 + t)
    dg = 0.5 * (1.0 + t) + 0.5 * x * (1.0 - t * t) * _GELU_C * (1.0 + 3 * 0.044715 * x * x)
    return g, dg


def _ln_fwd_tile(pre):
    mu = jnp.mean(pre, axis=-1, keepdims=True)
    xc = pre - mu
    var = jnp.mean(xc * xc, axis=-1, keepdims=True)
    rstd = lax.rsqrt(var + LN_EPS)
    return xc * rstd, rstd


def _ln_bwd_tile(dy, xhat, rstd, g):
    dyg = dy * g
    m1 = jnp.mean(dyg, axis=-1, keepdims=True)
    m2 = jnp.mean(dyg * xhat, axis=-1, keepdims=True)
    dpre = rstd * (dyg - m1 - xhat * m2)
    return dpre, jnp.sum(dy * xhat, axis=0, keepdims=True), jnp.sum(dy, axis=0, keepdims=True)


_NT = (((1,), (1,)), ((), ()))
_TN = (((0,), (0,)), ((), ()))


class _Exchange:
    def __init__(self, arrs, gather, chips=False, direct=False):
        self.arrs, self.gather, self.n, self.chips = list(arrs), gather, len(arrs), chips
        self.relays = gather and not direct

    def out_shape(self):
        return [jax.ShapeDtypeStruct(((N_DEV,) + a.shape) if self.gather else a.shape, a.dtype) for a in self.arrs]

    def scratch(self):
        n_remote = self.n * (N_DEV - 1)
        return [pltpu.SemaphoreType.DMA((n_remote,)), pltpu.SemaphoreType.DMA((n_remote,)),
                pltpu.SemaphoreType.DMA((self.n,))]

    def copies(self, ins, outs, sems):
        send_sems, recv_sems, local_sems = sems
        x, y, c = lax.axis_index("x"), lax.axis_index("y"), lax.axis_index("c")
        me = 2 * x + y if self.chips else 4 * x + 2 * y + c
        out = []
        for k in range(self.n):
            for d in (range(2, N_DEV, 2) if self.chips else range(1, N_DEV)):
                px = 1 - x if d & 4 else x
                py = 1 - y if d & 2 else y
                pc = 1 - c if d & 1 else c
                sem = k * (N_DEV - 1) + d - 1
                out.append(pltpu.make_async_remote_copy(
                    src_ref=ins[k] if self.gather else ins[k].at[2 * px + py if self.chips else 4 * px + 2 * py + pc],
                    dst_ref=outs[k].at[me],
                    send_sem=send_sems.at[sem], recv_sem=recv_sems.at[sem],
                    device_id=(px, py, pc), device_id_type=MESH_T))
            out.append(pltpu.make_async_copy(ins[k] if self.gather else ins[k].at[me], outs[k].at[me],
                                             local_sems.at[k]))
        return out

    def gather_copies(self, ins, outs, sems):
        send_sems, recv_sems, local_sems = sems
        x, y, c = lax.axis_index("x"), lax.axis_index("y"), lax.axis_index("c")
        sibling = (x, y, 1 - c)
        chips = [(1 - x, y), (x, 1 - y), (1 - x, 1 - y)]
        out = []
        for k in range(self.n):
            def copy(s, block, to, src=None, k=k):
                rows = outs[k].at[4 * block[0] + 2 * block[1] + block[2]]
                sem = k * (N_DEV - 1) + s
                return pltpu.make_async_remote_copy(
                    src_ref=rows if src is None else src, dst_ref=rows, send_sem=send_sems.at[sem],
                    recv_sem=recv_sems.at[sem], device_id=to, device_id_type=MESH_T)

            first = [copy(0, (x, y, c), sibling, src=ins[k])]
            first += [copy(1 + q, (x, y, c), (*chip, c), src=ins[k]) for q, chip in enumerate(chips)]
            passed = [copy(4 + q, (*chip, c), sibling) for q, chip in enumerate(chips)]
            own = pltpu.make_async_copy(ins[k], outs[k].at[4 * x + 2 * y + c], local_sems.at[k])
            out.append((first, passed, own, copy))
        return out, sibling, chips, (x, y, c)

    def start(self, ins, outs, sems):
        if not self.relays:
            for cp in self.copies(ins, outs, sems):
                cp.start()
            return
        per_array, _, _, _ = self.gather_copies(ins, outs, sems)
        for first, _, own, _ in per_array:
            own.start()
            for cp in first:
                cp.start()

    def relay(self, ins, outs, sems):
        per_array, sibling, chips, (x, y, c) = self.gather_copies(ins, outs, sems)
        for first, passed, own, copy in per_array:
            for q, chip in enumerate(chips):
                copy(1 + q, (*chip, c), (x, y, c)).wait_recv()
                passed[q].start()

    def wait(self, ins, outs, sems, relayed=False):
        if not self.relays:
            for cp in self.copies(ins, outs, sems):
                cp.wait()
            return
        if not relayed:
            self.relay(ins, outs, sems)
        per_array, sibling, chips, (x, y, c) = self.gather_copies(ins, outs, sems)
        for first, passed, own, copy in per_array:
            copy(0, sibling, (x, y, c)).wait_recv()
            for q, chip in enumerate(chips):
                copy(4 + q, (*chip, 1 - c), (x, y, c)).wait_recv()
            for cp in first + passed:
                cp.wait_send()
            own.wait()


class _Hosts:
    def __init__(self, *hosts):
        self.hosts = hosts
        self.relays = any(h.relays for h in hosts)
        self.n = sum(h.n for h in hosts)
        self.arrs = [a for h in hosts for a in h.arrs]

    def out_shape(self):
        return [sh for h in self.hosts for sh in h.out_shape()]

    def scratch(self):
        return [sc for h in self.hosts for sc in h.scratch()]

    def _each(self, ins, outs, sems):
        at = 0
        for k, h in enumerate(self.hosts):
            yield h, ins[at:at + h.n], outs[at:at + h.n], sems[3 * k:3 * k + 3]
            at += h.n

    def start(self, ins, outs, sems):
        for h, h_in, h_out, h_sems in self._each(ins, outs, sems):
            h.start(h_in, h_out, h_sems)

    def relay(self, ins, outs, sems):
        for h, h_in, h_out, h_sems in self._each(ins, outs, sems):
            if h.relays:
                h.relay(h_in, h_out, h_sems)

    def wait(self, ins, outs, sems, relayed=False):
        for h, h_in, h_out, h_sems in self._each(ins, outs, sems):
            h.wait(h_in, h_out, h_sems, relayed=relayed and h.relays)


def _hosted_call(host, body, *, name, grid, in_specs, out_specs, out_shape, scratch_shapes=(), compiler_params):
    out_specs = list(out_specs) if isinstance(out_specs, (list, tuple)) else [out_specs]
    out_shape = list(out_shape) if isinstance(out_shape, (list, tuple)) else [out_shape]
    if host is None:
        return pl.pallas_call(body, name=name, grid=grid, in_specs=in_specs, out_specs=out_specs,
                              out_shape=out_shape, scratch_shapes=list(scratch_shapes),
                              compiler_params=compiler_params)
    n_in, n_out, n_scr, k = len(in_specs), len(out_shape), len(scratch_shapes), host.n

    def wrapped(*refs):
        ins, h_in = refs[:n_in], refs[n_in:n_in + k]
        outs, h_out = refs[n_in + k:n_in + k + n_out], refs[n_in + k + n_out:n_in + 2 * k + n_out]
        scr, sems = refs[n_in + 2 * k + n_out:n_in + 2 * k + n_out + n_scr], refs[n_in + 2 * k + n_out + n_scr:]
        ids = [pl.program_id(a) for a in range(len(grid))]
        first = functools.reduce(jnp.logical_and, [i == 0 for i in ids])
        last = functools.reduce(jnp.logical_and, [i == g - 1 for i, g in zip(ids, grid)])
        steps = math.prod(grid)
        relay_at = (3 * steps) // 4 if host.relays and steps >= 8 else None

        @pl.when(first)
        def _():
            host.start(h_in, h_out, sems)

        if relay_at is not None:
            coords, rest = [], relay_at
            for g in reversed(grid):
                coords.append(rest % g)
                rest //= g

            @pl.when(functools.reduce(jnp.logical_and, [i == cd for i, cd in zip(ids, reversed(coords))]))
            def _():
                host.relay(h_in, h_out, sems)

        body(*ins, *outs, *scr)

        @pl.when(last)
        def _():
            host.wait(h_in, h_out, sems, relayed=relay_at is not None)

    hbm = pl.BlockSpec(memory_space=pl.ANY)
    call = pl.pallas_call(
        wrapped, name=name, grid=grid, in_specs=list(in_specs) + [hbm] * k, out_specs=out_specs + [hbm] * k,
        out_shape=out_shape + host.out_shape(), scratch_shapes=list(scratch_shapes) + host.scratch(),
        compiler_params=compiler_params)
    return lambda *args: call(*args, *host.arrs)


def _ffn_fwd_loss(xhat, g_in, b_in, wg, wu, wd, g_out, b_out, target, *, tm, name):
    t = xhat.shape[0]
    nj = N_DEV

    def body(x_ref, g_ref, b_ref, wg_ref, wu_ref, wd_ref, go_ref, bo_ref, tg_ref,
             dx_ref, sq_ref, gg_ref, gb_ref, hg_ref, hu_ref, xb, acc):
        i = pl.program_id(0)
        j = pl.program_id(1)

        @pl.when(j == 0)
        def _():
            xb[...] = (x_ref[...] * g_ref[...] + b_ref[...]).astype(bf16)
            acc[...] = jnp.zeros_like(acc)

        hg = jnp.dot(xb[...], wg_ref[...], preferred_element_type=f32)
        hu = jnp.dot(xb[...], wu_ref[...], preferred_element_type=f32)
        hg_ref[...] = hg.astype(bf16)
        hu_ref[...] = hu.astype(bf16)
        a = hg * _sigmoid_tanh(hg) * hu
        acc[...] += jnp.dot(a.astype(bf16), wd_ref[...], preferred_element_type=f32)

        @pl.when(j == nj - 1)
        def _():
            x = x_ref[...] * g_ref[...] + b_ref[...]
            xo, rstd = _ln_fwd_tile(ALPHA * x + 0.5 * acc[...])
            diff = xo * go_ref[...] + bo_ref[...] - tg_ref[...]
            sq = jnp.sum(diff * diff, axis=0, keepdims=True)
            dprev, gg, gb = _ln_bwd_tile(diff * (1.0 / D_MODEL), xo, rstd, go_ref[...])
            dx_ref[...] = dprev

            @pl.when(i == 0)
            def _():
                sq_ref[...] = sq
                gg_ref[...] = gg
                gb_ref[...] = gb

            @pl.when(i > 0)
            def _():
                sq_ref[...] += sq
                gg_ref[...] += gg
                gb_ref[...] += gb

    tok = pl.BlockSpec((tm, D_MODEL), lambda i, j: (i, 0))
    row = pl.BlockSpec((1, D_MODEL), lambda i, j: (0, 0))
    hid = pl.BlockSpec((tm, FF_TILE), lambda i, j: (i, j))
    return pl.pallas_call(
        body, name=name, grid=(t // tm, nj),
        in_specs=[tok, row, row,
                  pl.BlockSpec((None, D_MODEL, FF_TILE), lambda i, j: (j, 0, 0)),
                  pl.BlockSpec((None, D_MODEL, FF_TILE), lambda i, j: (j, 0, 0)),
                  pl.BlockSpec((None, FF_TILE, D_MODEL), lambda i, j: (j, 0, 0)), row, row, tok],
        out_specs=[tok, row, row, row, hid, hid],
        out_shape=[jax.ShapeDtypeStruct((t, D_MODEL), f32)] + [jax.ShapeDtypeStruct((1, D_MODEL), f32)] * 3
        + [jax.ShapeDtypeStruct((t, D_FF), bf16)] * 2,
        scratch_shapes=[pltpu.VMEM((tm, D_MODEL), bf16), pltpu.VMEM((tm, D_MODEL), f32)],
        compiler_params=_params(("arbitrary", "arbitrary")),
    )(xhat, g_in, b_in, wg, wu, wd, g_out, b_out, target)


def _ffn1_fwd_gathering(x, own, extra, *, tm, name):
    t = x.shape[0]
    n_i = t // tm
    n_arr = 3
    k_extra = extra.n
    ex = _Exchange(list(own), gather=True)
    ax, ay, ac = lax.axis_index("x"), lax.axis_index("y"), lax.axis_index("c")
    order = jnp.stack([4 * px + 2 * py + pc for px, py in ((ax, ay), (1 - ax, ay), (ax, 1 - ay), (1 - ax, 1 - ay))
                       for pc in (ac, 1 - ac)]).astype(jnp.int32)
    arrival = [None, (0, None), (1, 0), (4, None), (2, 1), (5, None), (3, 2), (6, None)]

    def body(order_ref, x_ref, *refs):
        w_in, e_in = refs[:n_arr], refs[n_arr:n_arr + k_extra]
        refs = refs[n_arr + k_extra:]
        xo_ref, rstd_ref, hg_ref, hu_ref = refs[:4]
        w_all, e_out = refs[4:4 + n_arr], refs[4 + n_arr:4 + n_arr + k_extra]
        acc, wgb, wub, wdb, fetch_sems, send_sems, recv_sems, local_sems = refs[4 + n_arr + k_extra:12 + n_arr + k_extra]
        e_sems = refs[12 + n_arr + k_extra:]
        bufs = (wgb, wub, wdb)
        s = pl.program_id(0)
        i = pl.program_id(1)
        per_array, sibling, chips, (x_, y_, c_) = ex.gather_copies(w_in, w_all, (send_sems, recv_sems, local_sems))

        def fetch(pos, slot):
            return [pltpu.make_async_copy(w_in[a] if pos == 0 else w_all[a].at[order_ref[pos]],
                                          bufs[a].at[slot], fetch_sems.at[n_arr * slot + a]) for a in range(n_arr)]

        def source_of(pos):
            chip = (x_, y_) if pos < 2 else chips[(pos - 2) // 2]
            return (*chip, c_ if pos % 2 == 0 else 1 - c_)

        @pl.when(jnp.logical_and(s == 0, i == 0))
        def _():
            for q in range(4):
                for first, _, own_copy, _ in per_array:
                    if q == 0:
                        own_copy.start()
                    first[q].start()
            for cp in fetch(0, 0):
                cp.start()
            for cp in fetch(0, 0):
                cp.wait()

        @pl.when(jnp.logical_and(s == N_DEV // 2, i == 0))
        def _():
            extra.start(e_in, e_out, e_sems)

        for pos in range(1, N_DEV):
            @pl.when(jnp.logical_and(s == pos - 1, i == n_i - 1))
            def _(pos=pos):
                sem, passes = arrival[pos]
                for _, passed, _, copy in per_array:
                    copy(sem, source_of(pos), (x_, y_, c_)).wait_recv()
                    if passes is not None:
                        passed[passes].start()
                for cp in fetch(pos, pos % 2):
                    cp.start()

            @pl.when(jnp.logical_and(s == pos, i == 0))
            def _(pos=pos):
                for cp in fetch(pos, pos % 2):
                    cp.wait()

        slot = s % 2
        xb = x_ref[...].astype(bf16)
        hg = jnp.dot(xb, wgb[slot], preferred_element_type=f32)
        hu = jnp.dot(xb, wub[slot], preferred_element_type=f32)
        hg_ref[...] = hg.astype(bf16)
        hu_ref[...] = hu.astype(bf16)
        a = hg * _sigmoid_tanh(hg) * hu
        part = jnp.dot(a.astype(bf16), wdb[slot], preferred_element_type=f32)

        @pl.when(s == 0)
        def _():
            acc[i] = part

        @pl.when(s > 0)
        def _():
            acc[i] += part

        @pl.when(s == N_DEV - 1)
        def _():
            xo, rstd = _ln_fwd_tile(ALPHA * x_ref[...] + 0.5 * acc[i])
            xo_ref[...] = xo
            rstd_ref[...] = rstd

        @pl.when(jnp.logical_and(s == N_DEV - 1, i == n_i - 1))
        def _():
            for first, passed, own_copy, _ in per_array:
                for cp in first + passed:
                    cp.wait_send()
                own_copy.wait()
            extra.wait(e_in, e_out, e_sems)

    hbm = pl.BlockSpec(memory_space=pl.ANY)
    last = N_DEV - 1
    tok_out = pl.BlockSpec((tm, D_MODEL), lambda s, i, o: (jnp.where(s == last, i, 0), 0))
    col_out = pl.BlockSpec((tm, 1), lambda s, i, o: (jnp.where(s == last, i, 0), 0))
    hid = pl.BlockSpec((tm, FF_TILE), lambda s, i, o: (i, o[s]))
    shard_shapes = [(N_DEV,) + w.shape for w in own]
    grid_spec = pltpu.PrefetchScalarGridSpec(
        num_scalar_prefetch=1, grid=(N_DEV, n_i),
        in_specs=[pl.BlockSpec((tm, D_MODEL), lambda s, i, o: (i, 0))] + [hbm] * (n_arr + k_extra),
        out_specs=[tok_out, col_out, hid, hid] + [hbm] * (n_arr + k_extra),
        scratch_shapes=[pltpu.VMEM((n_i, tm, D_MODEL), f32)]
        + [pltpu.VMEM((2,) + w.shape, bf16) for w in own]
        + [pltpu.SemaphoreType.DMA((2 * n_arr,))] + ex.scratch() + extra.scratch())
    res = pl.pallas_call(
        body, name=name, grid_spec=grid_spec,
        out_shape=[jax.ShapeDtypeStruct((t, D_MODEL), f32), jax.ShapeDtypeStruct((t, 1), f32),
                   jax.ShapeDtypeStruct((t, D_FF), bf16), jax.ShapeDtypeStruct((t, D_FF), bf16)]
        + [jax.ShapeDtypeStruct(sh, bf16) for sh in shard_shapes] + extra.out_shape(),
        compiler_params=_params(("arbitrary", "arbitrary")),
    )(order, x, *own, *extra.arrs)
    return res


def _ffn_bwd(dpre, hg, hu, wg, wu, wd, ln_in, *, tm, name, host=None):
    t = dpre.shape[0]
    nj = N_DEV
    with_ln = ln_in is not None

    def body(*refs):
        if with_ln:
            (dp_ref, hg_ref, hu_ref, wg_ref, wu_ref, wd_ref, xh_ref, rs_ref, g_ref,
             dx_ref, gg_ref, gb_ref, dhg_ref, dhu_ref, a_ref, dfb, acc) = refs
        else:
            (dp_ref, hg_ref, hu_ref, wg_ref, wu_ref, wd_ref,
             dx_ref, dhg_ref, dhu_ref, a_ref, dfb, acc) = refs
        i = pl.program_id(0)
        j = pl.program_id(1)

        @pl.when(j == 0)
        def _():
            dfb[...] = (0.5 * dp_ref[...]).astype(bf16)
            acc[...] = jnp.zeros_like(acc)

        da = lax.dot_general(dfb[...], wd_ref[...], _NT, preferred_element_type=f32)
        hgv = hg_ref[...].astype(f32)
        huv = hu_ref[...].astype(f32)
        sg = _sigmoid_tanh(hgv)
        silu = hgv * sg
        a_ref[...] = (silu * huv).astype(bf16)
        dhu = (da * silu).astype(bf16)
        dhg = (da * huv * (sg * (1.0 + hgv * (1.0 - sg)))).astype(bf16)
        dhg_ref[...] = dhg
        dhu_ref[...] = dhu
        acc[...] += (lax.dot_general(dhg, wg_ref[...], _NT, preferred_element_type=f32)
                     + lax.dot_general(dhu, wu_ref[...], _NT, preferred_element_type=f32))

        @pl.when(j == nj - 1)
        def _():
            dx = ALPHA * dp_ref[...] + acc[...]
            if with_ln:
                dprev, gg, gb = _ln_bwd_tile(dx, xh_ref[...], rs_ref[...], g_ref[...])
                dx_ref[...] = dprev

                @pl.when(i == 0)
                def _():
                    gg_ref[...] = gg
                    gb_ref[...] = gb

                @pl.when(i > 0)
                def _():
                    gg_ref[...] += gg
                    gb_ref[...] += gb
            else:
                dx_ref[...] = dx

    tok = pl.BlockSpec((tm, D_MODEL), lambda i, j: (i, 0), pipeline_mode=pl.Buffered(1))
    row = pl.BlockSpec((1, D_MODEL), lambda i, j: (0, 0))
    hid = pl.BlockSpec((tm, FF_TILE), lambda i, j: (i, j))
    in_specs = [tok, hid, hid,
                pl.BlockSpec((None, D_MODEL, FF_TILE), lambda i, j: (j, 0, 0)),
                pl.BlockSpec((None, D_MODEL, FF_TILE), lambda i, j: (j, 0, 0)),
                pl.BlockSpec((None, FF_TILE, D_MODEL), lambda i, j: (j, 0, 0))]
    args = [dpre, hg, hu, wg, wu, wd]
    out_specs = [tok]
    out_shape = [jax.ShapeDtypeStruct((t, D_MODEL), f32)]
    if with_ln:
        in_specs += [tok, pl.BlockSpec((tm, 1), lambda i, j: (i, 0)), row]
        args += list(ln_in)
        out_specs += [row, row]
        out_shape += [jax.ShapeDtypeStruct((1, D_MODEL), f32)] * 2
    out_specs += [hid, hid, hid]
    out_shape += [jax.ShapeDtypeStruct((t, D_FF), bf16)] * 3
    return _hosted_call(
        host, body, name=name, grid=(t // tm, nj), in_specs=in_specs, out_specs=out_specs, out_shape=out_shape,
        scratch_shapes=[pltpu.VMEM((tm, D_MODEL), bf16), pltpu.VMEM((tm, D_MODEL), f32)],
        compiler_params=_params(("arbitrary", "arbitrary")),
    )(*args)


def _ffn_bwd_act(dpre, hg, hu, wd, *, tm, name, host=None):
    t = dpre.shape[0]

    def body(dp_ref, hg_ref, hu_ref, wd_ref, dhg_ref, dhu_ref, a_ref, dfb):
        @pl.when(pl.program_id(1) == 0)
        def _():
            dfb[...] = (0.5 * dp_ref[...]).astype(bf16)

        da = lax.dot_general(dfb[...], wd_ref[...], _NT, preferred_element_type=f32)
        hgv = hg_ref[...].astype(f32)
        huv = hu_ref[...].astype(f32)
        sg = _sigmoid_tanh(hgv)
        silu = hgv * sg
        a_ref[...] = (silu * huv).astype(bf16)
        dhu_ref[...] = (da * silu).astype(bf16)
        dhg_ref[...] = (da * huv * (sg * (1.0 + hgv * (1.0 - sg)))).astype(bf16)

    hid = pl.BlockSpec((tm, FF_TILE), lambda i, j: (i, j))
    return _hosted_call(
        host, body, name=name, grid=(t // tm, N_DEV),
        in_specs=[pl.BlockSpec((tm, D_MODEL), lambda i, j: (i, 0)), hid, hid,
                  pl.BlockSpec((None, FF_TILE, D_MODEL), lambda i, j: (j, 0, 0))],
        out_specs=[hid, hid, hid], out_shape=[jax.ShapeDtypeStruct((t, D_FF), bf16)] * 3,
        scratch_shapes=[pltpu.VMEM((tm, D_MODEL), bf16)],
        compiler_params=_params(("arbitrary", "arbitrary")),
    )(dpre, hg, hu, wd)


def _ffn_bwd_dx(dpre, dhg, dhu, wg, wu, *, tm, name, host=None):
    t = dpre.shape[0]
    nj = N_DEV

    def body(dp_ref, dhg_ref, dhu_ref, wg_ref, wu_ref, dx_ref, acc):
        j = pl.program_id(1)

        @pl.when(j == 0)
        def _():
            acc[...] = jnp.zeros_like(acc)

        acc[...] += (lax.dot_general(dhg_ref[...], wg_ref[...], _NT, preferred_element_type=f32)
                     + lax.dot_general(dhu_ref[...], wu_ref[...], _NT, preferred_element_type=f32))

        @pl.when(j == nj - 1)
        def _():
            dx_ref[...] = ALPHA * dp_ref[...] + acc[...]

    tok = pl.BlockSpec((tm, D_MODEL), lambda i, j: (i, 0))
    hid = pl.BlockSpec((tm, FF_TILE), lambda i, j: (i, j))
    wspec = pl.BlockSpec((None, D_MODEL, FF_TILE), lambda i, j: (j, 0, 0))
    return _hosted_call(
        host, body, name=name, grid=(t // tm, nj), in_specs=[tok, hid, hid, wspec, wspec],
        out_specs=[tok], out_shape=[jax.ShapeDtypeStruct((t, D_MODEL), f32)],
        scratch_shapes=[pltpu.VMEM((tm, D_MODEL), f32)],
        compiler_params=_params(("arbitrary", "arbitrary")),
    )(dpre, dhg, dhu, wg, wu)


def _mm(a, b, *, mode, out_dtype, tm, tn, tk, name, affine=None, a_cols=None, b_cols=None,
        b_blocked=False, out_blocked=False, out_scale=None):
    if mode == "nn":
        m_full, k_full = a.shape
        m_dim, k_dim = (m_full, a_cols[1]) if a_cols else (m_full, k_full)
    else:
        k_dim, m_full = a.shape
        m_dim = a_cols[1] if a_cols else m_full
    a_off = a_cols[0] if a_cols else 0
    if b_blocked:
        n_dim = b.shape[0] * b.shape[2]
        assert b.shape[2] == tn
    else:
        n_dim = b_cols[1] if b_cols else b.shape[1]
    b_off = b_cols[0] if b_cols else 0
    assert m_dim % tm == 0 and n_dim % tn == 0 and k_dim % tk == 0, (name, m_dim, n_dim, k_dim)
    nk = k_dim // tk

    def body(*refs):
        if affine is not None:
            a_ref, g_ref, s_ref, b_ref, o_ref, acc = refs
        else:
            a_ref, b_ref, o_ref, acc = refs
        k = pl.program_id(2)

        @pl.when(k == 0)
        def _():
            acc[...] = jnp.zeros_like(acc)

        av = a_ref[...]
        if affine is not None:
            av = av * g_ref[...] + s_ref[...]
        av = av.astype(bf16)
        bv = b_ref[...].astype(bf16)
        if mode == "nn":
            acc[...] += jnp.dot(av, bv, preferred_element_type=f32)
        else:
            acc[...] += lax.dot_general(av, bv, _TN, preferred_element_type=f32)

        @pl.when(k == nk - 1)
        def _():
            res = acc[...] if out_scale is None else acc[...] * out_scale
            o_ref[...] = res.astype(out_dtype)

    if mode == "nn":
        a_spec = pl.BlockSpec((tm, tk), lambda i, j, k: (i, k + a_off))
        aff_spec = pl.BlockSpec((1, tk), lambda i, j, k: (0, k + a_off))
    else:
        a_spec = pl.BlockSpec((tk, tm), lambda i, j, k: (k, i + a_off))
        aff_spec = pl.BlockSpec((1, tm), lambda i, j, k: (0, i + a_off))
    if b_blocked:
        b_spec = pl.BlockSpec((None, tk, tn), lambda i, j, k: (j, k, 0))
    else:
        b_spec = pl.BlockSpec((tk, tn), lambda i, j, k: (k, j + b_off))
    if out_blocked:
        o_spec = pl.BlockSpec((None, tm, tn), lambda i, j, k: (j, i, 0))
        o_shape = jax.ShapeDtypeStruct((n_dim // tn, m_dim, tn), out_dtype)
    else:
        o_spec = pl.BlockSpec((tm, tn), lambda i, j, k: (i, j))
        o_shape = jax.ShapeDtypeStruct((m_dim, n_dim), out_dtype)
    in_specs = [a_spec] + ([aff_spec, aff_spec] if affine is not None else []) + [b_spec]
    args = [a] + (list(affine) if affine is not None else []) + [b]
    return pl.pallas_call(
        body, name=name, grid=(m_dim // tm, n_dim // tn, nk), in_specs=in_specs, out_specs=o_spec,
        out_shape=o_shape, scratch_shapes=[pltpu.VMEM((tm, tn), f32)],
        compiler_params=_params(("arbitrary", "arbitrary", "arbitrary")),
    )(*args)


def _mm_tn(a, b, *, out_dtype, tm, mb, tn, nb, tk, name, affine=None, out_blocked=False, out_scale=None,
           pair=False, host=None):
    k_dim, m_dim = a.shape
    multi_b = isinstance(b, (list, tuple))
    b_list = list(b) if multi_b else [b]
    n_dim = nb * tn if multi_b else b.shape[1]
    assert m_dim % (mb * tm) == 0 and n_dim % (nb * tn) == 0 and k_dim % tk == 0, (name, m_dim, n_dim, k_dim)
    nk = k_dim // tk
    grid = (m_dim // (mb * tm), n_dim // (nb * tn), nk)
    if pair:
        assert mb * nb == 4 and grid[0] * grid[1] == 2 and out_dtype == bf16, name

    def body(*refs):
        if pair:
            refs, (acc, send_buf, recv_buf, keep, send_sems, recv_sems) = refs[:-6], refs[-6:]
        else:
            refs, acc = refs[:-1], refs[-1]
        a_ref, o_ref = refs[0], refs[-1]
        if affine is not None:
            g_ref, s_ref = refs[1:3]
        b_refs = refs[3 if affine is not None else 1:-1]
        k = pl.program_id(2)

        @pl.when(k == 0)
        def _():
            acc[...] = jnp.zeros_like(acc)

        av = a_ref[...]
        if affine is not None:
            av = av * g_ref[...] + s_ref[...]
        av = av.astype(bf16)
        if multi_b:
            pieces = [r[...].astype(bf16) for r in b_refs]
        else:
            bv = b_refs[0][...].astype(bf16)
            pieces = [bv[:, jn * tn:(jn + 1) * tn] for jn in range(nb)]
        for im in range(mb):
            a_t = av[:, im * tm:(im + 1) * tm].T
            for jn in range(nb):
                acc[im * nb + jn] += jnp.dot(a_t, pieces[jn], preferred_element_type=f32)

        def scaled(v):
            return v if out_scale is None else v * out_scale

        @pl.when(k == nk - 1)
        def _():
            if pair:
                x, y, c = lax.axis_index("x"), lax.axis_index("y"), lax.axis_index("c")
                window = pl.program_id(0) + pl.program_id(1)

                def swap(w, cc):
                    return pltpu.make_async_remote_copy(
                        src_ref=send_buf.at[w, cc], dst_ref=recv_buf.at[w, cc],
                        send_sem=send_sems.at[2 * w + cc], recv_sem=recv_sems.at[2 * w + cc],
                        device_id=(x, y, 1 - c), device_id_type=MESH_T)

                for w in range(2):
                    @pl.when(window == w)
                    def _(w=w):
                        for cc in range(2):
                            send_buf[w, cc] = scaled(acc[2 * cc + 1 - c]).astype(bf16)
                            swap(w, cc).start()
                            if w == 0:
                                keep[cc] = scaled(acc[2 * cc + c])

                @pl.when(window == 1)
                def _():
                    for w in range(2):
                        for cc in range(2):
                            swap(w, cc).wait_recv()
                            mine = keep[cc] if w == 0 else scaled(acc[2 * cc + c])
                            o_ref[2 * w + cc] = (mine + recv_buf[w, cc].astype(f32)).astype(bf16)
                    for w in range(2):
                        for cc in range(2):
                            swap(w, cc).wait_send()
                return
            for im in range(mb):
                for jn in range(nb):
                    res = scaled(acc[im * nb + jn])
                    if out_blocked:
                        o_ref[jn, im * tm:(im + 1) * tm, :] = res.astype(out_dtype)
                    else:
                        o_ref[im * tm:(im + 1) * tm, jn * tn:(jn + 1) * tn] = res.astype(out_dtype)

    a_spec = pl.BlockSpec((tk, mb * tm), lambda i, j, k: (k, i))
    aff_spec = pl.BlockSpec((1, mb * tm), lambda i, j, k: (0, i))
    if multi_b:
        b_specs = [pl.BlockSpec((tk, tn), lambda i, j, k: (k, 0))] * nb
    else:
        b_specs = [pl.BlockSpec((tk, nb * tn), lambda i, j, k: (k, j))]
    scratch = [pltpu.VMEM((mb * nb, tm, tn), f32)]
    if pair:
        o_spec = pl.BlockSpec((4, tm, tn), lambda i, j, k: (0, 0, 0))
        o_shape = jax.ShapeDtypeStruct((4, tm, tn), out_dtype)
        scratch += [pltpu.VMEM((2, 2, tm, tn), bf16), pltpu.VMEM((2, 2, tm, tn), bf16), pltpu.VMEM((2, tm, tn), f32),
                    pltpu.SemaphoreType.DMA((4,)), pltpu.SemaphoreType.DMA((4,))]
    elif out_blocked:
        o_spec = pl.BlockSpec((nb, mb * tm, tn), lambda i, j, k: (j, i, 0))
        o_shape = jax.ShapeDtypeStruct((n_dim // tn, m_dim, tn), out_dtype)
    else:
        o_spec = pl.BlockSpec((mb * tm, nb * tn), lambda i, j, k: (i, j))
        o_shape = jax.ShapeDtypeStruct((m_dim, n_dim), out_dtype)
    in_specs = [a_spec] + ([aff_spec, aff_spec] if affine is not None else []) + b_specs
    args = [a] + (list(affine) if affine is not None else []) + b_list
    res = _hosted_call(
        host, body, name=name, grid=grid, in_specs=in_specs, out_specs=o_spec, out_shape=o_shape,
        scratch_shapes=scratch, compiler_params=_params(("arbitrary", "arbitrary", "arbitrary")),
    )(*args)
    return res[0] if host is None else res


def _in_proj(xhat, g, b, w_in, *, tm, name):
    t = xhat.shape[0]
    n_qkv, n_l = 3 * FOX_W, 2 * LRU_W

    def body(x_ref, g_ref, b_ref, w_ref, qkv_ref, zl_ref, zfg_ref):
        xb = (x_ref[...] * g_ref[...] + b_ref[...]).astype(bf16)
        qkv_ref[...] = jnp.dot(xb, w_ref[:, :n_qkv], preferred_element_type=f32).astype(bf16)
        zl_ref[...] = jnp.dot(xb, w_ref[:, n_qkv:n_qkv + n_l], preferred_element_type=f32)
        zfg_ref[...] = jnp.dot(xb, w_ref[:, n_qkv + n_l:], preferred_element_type=f32)

    row = pl.BlockSpec((1, D_MODEL), lambda i: (0, 0))
    return pl.pallas_call(
        body, name=name, grid=(t // tm,),
        in_specs=[pl.BlockSpec((tm, D_MODEL), lambda i: (i, 0)), row, row,
                  pl.BlockSpec(w_in.shape, lambda i: (0, 0))],
        out_specs=[pl.BlockSpec((tm, n_qkv), lambda i: (i, 0)), pl.BlockSpec((tm, n_l), lambda i: (i, 0)),
                   pl.BlockSpec((tm, LANES), lambda i: (i, 0))],
        out_shape=[jax.ShapeDtypeStruct((t, n_qkv), bf16), jax.ShapeDtypeStruct((t, n_l), f32),
                   jax.ShapeDtypeStruct((t, LANES), f32)],
        compiler_params=_params(("arbitrary",)),
    )(xhat, g, b, w_in)


def _mmln(pairs, *, tm, name, resid=None, resid_scale=1.0, epi=None, ln=None, n_out=D_MODEL):
    t = pairs[0][0].shape[0]
    n_pairs = len(pairs)
    n_resid = 0 if resid is None else len(resid) - 1

    def body(*refs):
        pos = 0
        val = None
        for p in range(n_pairs):
            a_ref, b_ref = refs[pos], refs[pos + 1]
            pos += 2
            av = a_ref[...].astype(bf16)
            bv = b_ref[...].astype(bf16)
            if pairs[p][6] == "nn":
                term = jnp.dot(av, bv, preferred_element_type=f32)
            else:
                term = lax.dot_general(av, bv, _NT, preferred_element_type=f32)
            val = term if val is None else val + term
        if resid is not None:
            if resid[0] == "plain":
                r = refs[pos][...]
            else:
                r = refs[pos][...] * refs[pos + 1][...] + refs[pos + 2][...]
            pos += n_resid
            val = val + resid_scale * r
        if epi is None:
            o_ref = refs[pos]
            o_ref[...] = val.astype(o_ref.dtype)
        elif epi == "ln_fwd":
            xo, rstd = _ln_fwd_tile(val)
            refs[pos][...] = xo
            refs[pos + 1][...] = rstd
        else:
            xh_ref, rs_ref, g_ref, dx_ref, gg_ref, gb_ref = refs[pos:pos + 6]
            dprev, gg, gb = _ln_bwd_tile(val, xh_ref[...], rs_ref[...], g_ref[...])
            dx_ref[...] = dprev
            i = pl.program_id(0)

            @pl.when(i == 0)
            def _():
                gg_ref[...] = gg
                gb_ref[...] = gb

            @pl.when(i > 0)
            def _():
                gg_ref[...] += gg
                gb_ref[...] += gb

    in_specs, args = [], []
    for (a, acb, aw, b, bcb, bw, mode) in pairs:
        in_specs.append(pl.BlockSpec((tm, aw), lambda i, acb=acb: (i, acb)))
        args.append(a)
        if mode == "nn":
            in_specs.append(pl.BlockSpec((aw, n_out), lambda i, bcb=bcb: (bcb, 0)))
        else:
            in_specs.append(pl.BlockSpec((n_out, bw), lambda i, bcb=bcb: (0, bcb)))
        args.append(b)
    tok = pl.BlockSpec((tm, n_out), lambda i: (i, 0))
    row = pl.BlockSpec((1, n_out), lambda i: (0, 0))
    col = pl.BlockSpec((tm, 1), lambda i: (i, 0))
    if resid is not None:
        in_specs += [tok] if resid[0] == "plain" else [tok, row, row]
        args += list(resid[1:])
    if epi is None:
        out_specs, out_shape = tok, jax.ShapeDtypeStruct((t, n_out), f32)
    elif epi == "ln_fwd":
        out_specs = [tok, col]
        out_shape = [jax.ShapeDtypeStruct((t, n_out), f32), jax.ShapeDtypeStruct((t, 1), f32)]
    else:
        in_specs += [tok, col, row]
        args += list(ln)
        out_specs = [tok, row, row]
        out_shape = [jax.ShapeDtypeStruct((t, n_out), f32)] + [jax.ShapeDtypeStruct((1, n_out), f32)] * 2
    return pl.pallas_call(
        body, name=name, grid=(t // tm,), in_specs=in_specs, out_specs=out_specs, out_shape=out_shape,
        compiler_params=_params(("arbitrary",)),
    )(*args)


CUM_TILE = 512


def _tri(n, lower):
    r = lax.broadcasted_iota(jnp.int32, (n, n), 0)
    c = lax.broadcasted_iota(jnp.int32, (n, n), 1)
    return jnp.where((r >= c) if lower else (r <= c), 1.0, 0.0).astype(f32)


def _cum_fwd(zfg, bfg, *, name):
    t = zfg.shape[0]

    def body(z_ref, b_ref, o_ref, carry):
        @pl.when(pl.program_id(0) == 0)
        def _():
            carry[...] = jnp.zeros_like(carry)

        ls = -_softplus(-(z_ref[...] + b_ref[...]))
        c = jnp.dot(_tri(CUM_TILE, True), ls, preferred_element_type=f32,
                    precision=lax.Precision.HIGHEST) + carry[...]
        o_ref[...] = c
        carry[...] = c[CUM_TILE - 1:CUM_TILE, :]

    blk = pl.BlockSpec((CUM_TILE, LANES), lambda i: (i, 0))
    return pl.pallas_call(
        body, name=name, grid=(t // CUM_TILE,),
        in_specs=[blk, pl.BlockSpec((1, LANES), lambda i: (0, 0))], out_specs=blk,
        out_shape=jax.ShapeDtypeStruct((t, LANES), f32), scratch_shapes=[pltpu.VMEM((1, LANES), f32)],
        compiler_params=_params(("arbitrary",)),
    )(zfg, bfg)


def _cum_bwd(dcum_q, dcum_k, zfg, bfg, *, name):
    t = zfg.shape[0]
    n = t // CUM_TILE

    def body(d_ref, d2_ref, z_ref, b_ref, o_ref, s_ref, carry):
        i = pl.program_id(0)

        @pl.when(i == 0)
        def _():
            carry[...] = jnp.zeros_like(carry)

        dls = jnp.dot(_tri(CUM_TILE, False), d_ref[...] + d2_ref[...], preferred_element_type=f32,
                      precision=lax.Precision.HIGHEST) + carry[...]
        carry[...] = dls[0:1, :]
        lane = lax.broadcasted_iota(jnp.int32, (CUM_TILE, LANES), 1)
        dfg = jnp.where(lane < HEADS, dls * _sigmoid(-(z_ref[...] + b_ref[...])), 0.0)
        o_ref[...] = dfg
        tot = jnp.sum(dfg, axis=0, keepdims=True)

        @pl.when(i == 0)
        def _():
            s_ref[...] = tot

        @pl.when(i > 0)
        def _():
            s_ref[...] += tot

    blk = pl.BlockSpec((CUM_TILE, LANES), lambda i: (n - 1 - i, 0))
    row = pl.BlockSpec((1, LANES), lambda i: (0, 0))
    return pl.pallas_call(
        body, name=name, grid=(n,), in_specs=[blk, blk, blk, row], out_specs=[blk, row],
        out_shape=[jax.ShapeDtypeStruct((t, LANES), f32), jax.ShapeDtypeStruct((1, LANES), f32)],
        scratch_shapes=[pltpu.VMEM((1, LANES), f32)],
        compiler_params=_params(("arbitrary",)),
    )(dcum_q, dcum_k, zfg, bfg)


ATT_TILE = 512


def _causal(i, j, transposed):
    r = lax.broadcasted_iota(jnp.int32, (ATT_TILE, ATT_TILE), 0)
    c = lax.broadcasted_iota(jnp.int32, (ATT_TILE, ATT_TILE), 1)
    if transposed:
        return (c + i * ATT_TILE) >= (r + j * ATT_TILE)
    return (r + i * ATT_TILE) >= (c + j * ATT_TILE)


ATT_W = HEADS * LANES


def _data_lane(h):
    return HEAD_D * (h % 2)


def _extra_lane(h):
    return HEAD_D - _data_lane(h)


def _split3(x):
    hi = x.astype(bf16)
    rest = x - hi.astype(f32)
    mid = rest.astype(bf16)
    lo = (rest - mid.astype(f32)).astype(bf16)
    return hi, mid, lo


def _three_pieces(x):
    hi, mid, lo = (p.astype(f32) for p in _split3(x))
    return (hi + pltpu.roll(mid, HEADS, axis=1) + pltpu.roll(lo, 2 * HEADS, axis=1)).astype(bf16)


def _move(h, first):
    r = lax.broadcasted_iota(jnp.int32, (LANES, LANES), 0)
    c = lax.broadcasted_iota(jnp.int32, (LANES, LANES), 1)
    hit = functools.reduce(jnp.logical_or, [jnp.logical_and(r == HEADS * q + h, c == first + q) for q in range(3)])
    return jnp.where(hit, 1.0, 0.0).astype(bf16)


def _ones_from(first, rows):
    lane = lax.broadcasted_iota(jnp.int32, (rows, LANES), 1)
    return jnp.where(jnp.logical_and(lane >= first, lane < first + 3), 1.0, 0.0)


def _own_lanes(h, rows):
    lane = lax.broadcasted_iota(jnp.int32, (rows, LANES), 1)
    return (lane < HEAD_D) if h % 2 == 0 else (lane >= HEAD_D)


def _head_values(x):
    lane = lax.broadcasted_iota(jnp.int32, x.shape, 1)
    return jnp.where(lane < HEADS, x, 0.0)


def _attn_prep_fwd(qkv, cum, *, tm, name):
    t = qkv.shape[0]

    def body(q_ref, k_ref, v_ref, c_ref, qa_ref, ka_ref, va_ref):
        c3 = _three_pieces(_head_values(c_ref[...]))
        ones = jnp.ones((tm, LANES), bf16)
        for h in range(HEADS):
            pair = slice(LANES * (h // 2), LANES * (h // 2 + 1))
            hs = slice(LANES * h, LANES * (h + 1))
            base, own = _extra_lane(h), _own_lanes(h, tm)
            eq = jnp.dot(c3, _move(h, base), preferred_element_type=f32) + _ones_from(base + 3, tm)
            ek = _ones_from(base, tm) - jnp.dot(c3, _move(h, base + 3), preferred_element_type=f32)
            qa_ref[:, hs] = jnp.where(own, q_ref[:, pair] * ATT_SCALE, eq.astype(bf16))
            ka_ref[:, hs] = jnp.where(own, k_ref[:, pair], ek.astype(bf16))
            va_ref[:, hs] = jnp.where(own, v_ref[:, pair], ones)

    wide = pl.BlockSpec((tm, ATT_W), lambda i: (i, 0))
    out = jax.ShapeDtypeStruct((t, ATT_W), bf16)
    return pl.pallas_call(
        body, name=name, grid=(t // tm,),
        in_specs=[pl.BlockSpec((tm, FOX_W), lambda i: (i, 0)), pl.BlockSpec((tm, FOX_W), lambda i: (i, 1)),
                  pl.BlockSpec((tm, FOX_W), lambda i: (i, 2)), pl.BlockSpec((tm, LANES), lambda i: (i, 0))],
        out_specs=[wide] * 3, out_shape=[out] * 3, compiler_params=_params(("arbitrary",)),
    )(qkv, qkv, qkv, cum)


def _attn_prep_bwd(qkv, cum, lse, dmix, o, *, tm, name):
    t = qkv.shape[0]

    def body(q_ref, c_ref, l_ref, do_ref, o_ref, qa_ref, da_ref):
        b3 = _three_pieces(_head_values(c_ref[...] - l_ref[...]))
        r = lax.broadcasted_iota(jnp.int32, (FOX_W, LANES), 0)
        c = lax.broadcasted_iota(jnp.int32, (FOX_W, LANES), 1)
        per_head = jnp.where(r // HEAD_D == c, 1.0, 0.0).astype(bf16)
        delta = sum(jnp.dot(p, per_head, preferred_element_type=f32) for p in _split3(do_ref[...] * o_ref[...]))
        d3 = _three_pieces(delta)
        for h in range(HEADS):
            pair = slice(LANES * (h // 2), LANES * (h // 2 + 1))
            hs = slice(LANES * h, LANES * (h + 1))
            base, own = _extra_lane(h), _own_lanes(h, tm)
            eq = jnp.dot(b3, _move(h, base), preferred_element_type=f32) + _ones_from(base + 3, tm)
            ed = -jnp.dot(d3, _move(h, base), preferred_element_type=f32)
            qa_ref[:, hs] = jnp.where(own, q_ref[:, pair] * ATT_SCALE, eq.astype(bf16))
            da_ref[:, hs] = jnp.where(own, do_ref[:, pair].astype(bf16), ed.astype(bf16))

    wide = pl.BlockSpec((tm, ATT_W), lambda i: (i, 0))
    half = pl.BlockSpec((tm, FOX_W), lambda i: (i, 0))
    col = pl.BlockSpec((tm, LANES), lambda i: (i, 0))
    out = jax.ShapeDtypeStruct((t, ATT_W), bf16)
    return pl.pallas_call(
        body, name=name, grid=(t // tm,), in_specs=[half, col, col, half, half],
        out_specs=[wide] * 2, out_shape=[out] * 2, compiler_params=_params(("arbitrary",)),
    )(qkv, cum, lse, dmix, o)


def _attn_fwd2(q_aug, k_aug, v_aug, *, name, host=None):
    t = q_aug.shape[0]
    n = t // ATT_TILE
    tq = ATT_TILE

    def body(q_ref, k_ref, v_ref, o_ref, lse_ref, acc, m_s):
        i = pl.program_id(0)
        j = pl.program_id(1)

        @pl.when(j == 0)
        def _():
            acc[...] = jnp.zeros_like(acc)
            m_s[...] = jnp.full_like(m_s, NEG_BIG)

        def block(masked):
            mask = _causal(i, j, False) if masked else None
            for h in range(HEADS):
                hs = slice(LANES * h, LANES * (h + 1))
                s = lax.dot_general(q_ref[:, hs], k_ref[:, hs], _NT, preferred_element_type=f32)
                if masked:
                    s = jnp.where(mask, s, NEG_BIG)
                blocks = [s[:, LANES * b:LANES * (b + 1)] for b in range(tq // LANES)]
                m_old = m_s[h]
                m_new = jnp.maximum(m_old, jnp.broadcast_to(
                    jnp.max(functools.reduce(jnp.maximum, blocks), axis=-1, keepdims=True), (tq, LANES)))
                p = jnp.concatenate([jnp.exp(b - m_new) for b in blocks], axis=1).astype(bf16)
                acc[h] = jnp.exp(m_old - m_new) * acc[h] + jnp.dot(p, v_ref[:, hs], preferred_element_type=f32)
                m_s[h] = m_new

        @pl.when(j < i)
        def _():
            block(False)

        @pl.when(j == i)
        def _():
            block(True)
            lse_ref[...] = jnp.zeros_like(lse_ref)
            for h in range(HEADS):
                a = acc[h]
                l = a[:, _extra_lane(h):_extra_lane(h) + 1]
                o_ref[:, HEAD_D * h:HEAD_D * (h + 1)] = a[:, _data_lane(h):_data_lane(h) + HEAD_D] / l
                lse_ref[:, h:h + 1] = m_s[h][:, 0:1] + jnp.log(l)

    kv = pl.BlockSpec((tq, ATT_W), lambda i, j: (jnp.minimum(i, j), 0))
    return _hosted_call(
        host, body, name=name, grid=(n, n),
        in_specs=[pl.BlockSpec((tq, ATT_W), lambda i, j: (i, 0)), kv, kv],
        out_specs=[pl.BlockSpec((tq, FOX_W), lambda i, j: (i, 0)), pl.BlockSpec((tq, LANES), lambda i, j: (i, 0))],
        out_shape=[jax.ShapeDtypeStruct((t, FOX_W), f32), jax.ShapeDtypeStruct((t, LANES), f32)],
        scratch_shapes=[pltpu.VMEM((HEADS, tq, LANES), f32), pltpu.VMEM((HEADS, tq, LANES), f32)],
        compiler_params=_params(("arbitrary", "arbitrary")),
    )(q_aug, k_aug, v_aug)


def _attn_bwd(qb_aug, k_aug, v_aug, do_aug, *, name, host=None):
    t = qb_aug.shape[0]
    n = t // ATT_TILE
    tk = ATT_TILE

    def body(q_ref, k_ref, v_ref, do_ref, dq_ref, dcq_ref, dk_ref, dv_ref, dck_ref, dk_acc, dv_acc, dq_all):
        j = pl.program_id(0)
        i = pl.program_id(1)

        @pl.when(jnp.logical_and(i == 0, j == 0))
        def _():
            dq_all[...] = jnp.zeros_like(dq_all)

        @pl.when(i == 0)
        def _():
            dk_acc[...] = jnp.zeros_like(dk_acc)
            dv_acc[...] = jnp.zeros_like(dv_acc)

        def block(masked):
            mask = _causal(i, j, True) if masked else None
            for h in range(HEADS):
                hs = slice(LANES * h, LANES * (h + 1))
                qh = q_ref[:, hs]
                doh = do_ref[:, hs]
                kh = k_ref[:, hs]
                s_t = lax.dot_general(kh, qh, _NT, preferred_element_type=f32)
                if masked:
                    s_t = jnp.where(mask, s_t, NEG_BIG)
                p_t = jnp.exp(s_t)
                dv_acc[h] += jnp.dot(p_t.astype(bf16), doh, preferred_element_type=f32)
                dp_t = lax.dot_general(v_ref[:, hs], doh, _NT, preferred_element_type=f32)
                ds_t = (p_t * dp_t).astype(bf16)
                dk_acc[h] += jnp.dot(ds_t, qh, preferred_element_type=f32)
                dq_all[i, h] += lax.dot_general(ds_t, kh, _TN, preferred_element_type=f32)

        @pl.when(i > j)
        def _():
            block(False)

        @pl.when(i == j)
        def _():
            block(True)
            dcq_ref[...] = jnp.zeros_like(dcq_ref)
            for h in range(HEADS):
                a = dq_all[j, h]
                dq_ref[:, HEAD_D * h:HEAD_D * (h + 1)] = (
                    a[:, _data_lane(h):_data_lane(h) + HEAD_D] * ATT_SCALE).astype(bf16)
                dcq_ref[:, h:h + 1] = a[:, _extra_lane(h):_extra_lane(h) + 1]

        @pl.when(i == n - 1)
        def _():
            dck_ref[...] = jnp.zeros_like(dck_ref)
            for h in range(HEADS):
                a = dk_acc[h]
                cols = slice(_data_lane(h), _data_lane(h) + HEAD_D)
                dk_ref[:, HEAD_D * h:HEAD_D * (h + 1)] = a[:, cols].astype(bf16)
                dv_ref[:, HEAD_D * h:HEAD_D * (h + 1)] = dv_acc[h][:, cols].astype(bf16)
                dck_ref[:, h:h + 1] = -a[:, _extra_lane(h) + 3:_extra_lane(h) + 4]

    own = pl.BlockSpec((tk, ATT_W), lambda j, i: (j, 0))
    qs = pl.BlockSpec((tk, ATT_W), lambda j, i: (jnp.maximum(i, j), 0))
    half = pl.BlockSpec((tk, FOX_W), lambda j, i: (j, 0))
    col = pl.BlockSpec((tk, LANES), lambda j, i: (j, 0))
    return _hosted_call(
        host, body, name=name, grid=(n, n), in_specs=[qs, own, own, qs],
        out_specs=[half, col, half, half, col],
        out_shape=[jax.ShapeDtypeStruct((t, FOX_W), bf16), jax.ShapeDtypeStruct((t, LANES), f32),
                   jax.ShapeDtypeStruct((t, FOX_W), bf16), jax.ShapeDtypeStruct((t, FOX_W), bf16),
                   jax.ShapeDtypeStruct((t, LANES), f32)],
        scratch_shapes=[pltpu.VMEM((HEADS, tk, LANES), f32), pltpu.VMEM((HEADS, tk, LANES), f32),
                        pltpu.VMEM((n, HEADS, tk, LANES), f32)],
        compiler_params=_params(("arbitrary", "arbitrary")),
    )(qb_aug, k_aug, v_aug, do_aug)


LRU_CHUNK = 64
LRU_G = 256
SUB = 8


def _row_ids(n):
    return lax.broadcasted_iota(jnp.int32, (n, LRU_G), 0)


def _shift_rows_down(ext, s):
    return pltpu.roll(ext, s, axis=0)[SUB:, :]


def _shift_rows_up(ext, s, n):
    return pltpu.roll(ext, ext.shape[0] - s, axis=0)[:n, :]


def _lru_gates(u, wa_ref, ba_ref, wx_ref, bx_ref, sp):
    ub = u.astype(bf16)
    r = _sigmoid(jnp.dot(ub, wa_ref[...], preferred_element_type=f32) + ba_ref[...])
    gi = _sigmoid(jnp.dot(ub, wx_ref[...], preferred_element_type=f32) + bx_ref[...])
    log_a = -LRU_C * r * sp
    a = jnp.exp(log_a)
    s = jnp.sqrt(_one_minus_exp(2.0 * log_a))
    return r, gi, a, s


def _conv_window(lx_ref, r0, ci):
    cur = lx_ref[pl.ds(r0, LRU_CHUNK), :]
    p0 = pl.multiple_of(jnp.maximum(r0 - SUB, 0), SUB)
    prev = jnp.where(ci > 0, lx_ref[pl.ds(p0, SUB), :], 0.0)
    return cur, jnp.concatenate([prev, cur], axis=0)


def _lru_fwd(zl, conv_w, conv_b, wa, ba, wx, bx, lam, *, name, host=None):
    t = zl.shape[0]
    n_chunk = t // LRU_CHUNK

    def body(lx_ref, lg_ref, cw_ref, cb_ref, wa_ref, ba_ref, wx_ref, bx_ref, lam_ref, u_ref, h_ref, y_ref):
        sp = _softplus(-lam_ref[...])
        rows = _row_ids(SUB)

        def chunk(ci, hc):
            r0 = pl.multiple_of(ci * LRU_CHUNK, LRU_CHUNK)
            cur, ext = _conv_window(lx_ref, r0, ci)
            u = cb_ref[...] + cw_ref[3:4, :] * cur
            for k in range(3):
                u = u + cw_ref[k:k + 1, :] * _shift_rows_down(ext, 3 - k)
            r, gi, a, s = _lru_gates(u, wa_ref, ba_ref, wx_ref, bx_ref, sp)
            b = s * (gi * u)
            tiles = []
            for q in range(LRU_CHUNK // SUB):
                ta = a[SUB * q:SUB * (q + 1), :]
                tb = b[SUB * q:SUB * (q + 1), :]
                for d in (1, 2, 4):
                    a_sh = jnp.where(rows >= d, pltpu.roll(ta, d, axis=0), 1.0)
                    b_sh = jnp.where(rows >= d, pltpu.roll(tb, d, axis=0), 0.0)
                    tb = ta * b_sh + tb
                    ta = ta * a_sh
                hq = tb + ta * hc
                hc = hq[SUB - 1:SUB, :]
                tiles.append(hq)
            h = jnp.concatenate(tiles, axis=0)
            u_ref[pl.ds(r0, LRU_CHUNK), :] = u
            h_ref[pl.ds(r0, LRU_CHUNK), :] = h
            gel, _ = _gelu_and_grad(lg_ref[pl.ds(r0, LRU_CHUNK), :])
            y_ref[pl.ds(r0, LRU_CHUNK), :] = gel * h
            return hc

        lax.fori_loop(0, n_chunk, chunk, jnp.zeros((1, LRU_G), f32))

    seq = lambda cb: pl.BlockSpec((t, LRU_G), lambda c, cb=cb: (0, c + cb))
    rowc = pl.BlockSpec((1, LRU_G), lambda c: (0, c))
    diag = pl.BlockSpec((LRU_G, LRU_G), lambda c: (c, c))
    out = jax.ShapeDtypeStruct((t, LRU_W), f32)
    return _hosted_call(
        host, body, name=name, grid=(LRU_W // LRU_G,),
        in_specs=[seq(0), seq(LRU_W // LRU_G), pl.BlockSpec((4, LRU_G), lambda c: (0, c)),
                  rowc, diag, rowc, diag, rowc, rowc],
        out_specs=[seq(0)] * 3, out_shape=[out] * 3,
        compiler_params=_params(("arbitrary",)),
    )(zl, zl, conv_w, conv_b, wa, ba, wx, bx, lam)


def _lru_bwd(dmix, zl, u_all, h_all, conv_w, wa, ba, wx, bx, lam, *, name, host=None):
    t = zl.shape[0]
    n_chunk = t // LRU_CHUNK

    def body(dy_ref, lx_ref, lg_ref, u_ref, h_ref, cw_ref, wa_ref, ba_ref, wx_ref, bx_ref, lam_ref,
             dlx_ref, dlg_ref, dcw_ref, dcb_ref, dba_ref, dbx_ref, dlam_ref, dwa_ref, dwx_ref, dpr_s, dpx_s):
        lam_v = lam_ref[...]
        sp = _softplus(-lam_v)
        rows = _row_ids(SUB)
        rows_c = _row_ids(LRU_CHUNK)
        zero_row = jnp.zeros((1, LRU_G), f32)

        def chunk(step, carry):
            dh_c, a_next0, du_next, dsp, dba, dbx, dcb, dw0, dw1, dw2, dw3 = carry
            ci = n_chunk - 1 - step
            r0 = pl.multiple_of(ci * LRU_CHUNK, LRU_CHUNK)
            sl = pl.ds(r0, LRU_CHUNK)
            u = u_ref[sl, :]
            r, gi, a, s = _lru_gates(u, wa_ref, ba_ref, wx_ref, bx_ref, sp)
            h = h_ref[sl, :]
            p0 = pl.multiple_of(jnp.maximum(r0 - SUB, 0), SUB)
            h_before = jnp.where(ci > 0, h_ref[pl.ds(p0, SUB), :], 0.0)[SUB - 1:SUB, :]
            h_prev = jnp.where(rows_c == 0, h_before, pltpu.roll(h, 1, axis=0))
            gel, dgel = _gelu_and_grad(lg_ref[sl, :])
            dy = dy_ref[sl, :]
            dlg_ref[sl, :] = (dy * h * dgel).astype(bf16)
            g_in = dy * gel
            a_next = jnp.where(rows_c == LRU_CHUNK - 1, a_next0, pltpu.roll(a, LRU_CHUNK - 1, axis=0))
            tiles = [None] * (LRU_CHUNK // SUB)
            for q in reversed(range(LRU_CHUNK // SUB)):
                ta = a_next[SUB * q:SUB * (q + 1), :]
                tb = g_in[SUB * q:SUB * (q + 1), :]
                for d in (1, 2, 4):
                    a_sh = jnp.where(rows < SUB - d, pltpu.roll(ta, SUB - d, axis=0), 1.0)
                    b_sh = jnp.where(rows < SUB - d, pltpu.roll(tb, SUB - d, axis=0), 0.0)
                    tb = ta * b_sh + tb
                    ta = ta * a_sh
                dhq = tb + ta * dh_c
                dh_c = dhq[0:1, :]
                tiles[q] = dhq
            dh = jnp.concatenate(tiles, axis=0)
            da = dh * h_prev
            ds = dh * gi * u
            dgi = dh * s * u
            du = dh * s * gi
            dlog_a = da * a - ds * (a * a) / s
            dr = dlog_a * (-LRU_C * sp)
            dsp = dsp + jnp.sum(dlog_a * (-LRU_C * r), axis=0, keepdims=True)
            dpr = dr * r * (1.0 - r)
            dpx = dgi * gi * (1.0 - gi)
            dprb = dpr.astype(bf16)
            dpxb = dpx.astype(bf16)
            dpr_s[sl, :] = dprb
            dpx_s[sl, :] = dpxb
            du = du + (lax.dot_general(dprb, wa_ref[...], _NT, preferred_element_type=f32)
                       + lax.dot_general(dpxb, wx_ref[...], _NT, preferred_element_type=f32))
            dba = dba + jnp.sum(dpr, axis=0, keepdims=True)
            dbx = dbx + jnp.sum(dpx, axis=0, keepdims=True)
            dcb = dcb + jnp.sum(du, axis=0, keepdims=True)
            du_ext = jnp.concatenate([du, du_next], axis=0)
            dlx = cw_ref[3:4, :] * du
            for k in range(3):
                dlx = dlx + cw_ref[k:k + 1, :] * _shift_rows_up(du_ext, 3 - k, LRU_CHUNK)
            dlx_ref[sl, :] = dlx.astype(bf16)
            cur, ext = _conv_window(lx_ref, r0, ci)
            dws = [dw0, dw1, dw2, dw3 + jnp.sum(du * cur, axis=0, keepdims=True)]
            for k in range(3):
                dws[k] = dws[k] + jnp.sum(du * _shift_rows_down(ext, 3 - k), axis=0, keepdims=True)
            return (dh_c, a[0:1, :], du[0:SUB, :], dsp, dba, dbx, dcb, dws[0], dws[1], dws[2], dws[3])

        init = (zero_row, zero_row, jnp.zeros((SUB, LRU_G), f32)) + (zero_row,) * 8
        out = lax.fori_loop(0, n_chunk, chunk, init)
        _, _, _, dsp, dba, dbx, dcb, dw0, dw1, dw2, dw3 = out
        dlam_ref[...] = dsp * (-_sigmoid(-lam_v))
        dba_ref[...] = dba
        dbx_ref[...] = dbx
        dcb_ref[...] = dcb
        dcw_ref[...] = jnp.concatenate([dw0, dw1, dw2, dw3], axis=0)
        ub = u_ref[...].astype(bf16)
        dwa_ref[...] = lax.dot_general(ub, dpr_s[...], _TN, preferred_element_type=f32)
        dwx_ref[...] = lax.dot_general(ub, dpx_s[...], _TN, preferred_element_type=f32)

    seq = lambda cb: pl.BlockSpec((t, LRU_G), lambda c, cb=cb: (0, c + cb))
    rowc = pl.BlockSpec((1, LRU_G), lambda c: (0, c))
    diag = pl.BlockSpec((LRU_G, LRU_G), lambda c: (c, c))
    gate_out = pl.BlockSpec((None, LRU_G, LRU_G), lambda c: (c, 0, 0))
    row_shape = jax.ShapeDtypeStruct((1, LRU_W), f32)
    return _hosted_call(
        host, body, name=name, grid=(LRU_W // LRU_G,),
        in_specs=[seq(LRU_W // LRU_G), seq(0), seq(LRU_W // LRU_G), seq(0), seq(0),
                  pl.BlockSpec((4, LRU_G), lambda c: (0, c)),
                  diag, rowc, diag, rowc, rowc],
        out_specs=[seq(0), seq(0), pl.BlockSpec((4, LRU_G), lambda c: (0, c)), rowc, rowc, rowc, rowc,
                   gate_out, gate_out],
        out_shape=[jax.ShapeDtypeStruct((t, LRU_W), bf16)] * 2
        + [jax.ShapeDtypeStruct((4, LRU_W), f32)] + [row_shape] * 4
        + [jax.ShapeDtypeStruct((LRU_W // LRU_G, LRU_G, LRU_G), f32)] * 2,
        scratch_shapes=[pltpu.VMEM((t, LRU_G), bf16), pltpu.VMEM((t, LRU_G), bf16)],
        compiler_params=_params(("arbitrary",)),
    )(dmix, zl, zl, u_all, h_all, conv_w, wa, ba, wx, bx, lam)


def _pack_rows(a):
    flat = a.reshape(-1)
    rows = -(-flat.shape[0] // LANES)
    return jnp.pad(flat, (0, rows * LANES - flat.shape[0])).reshape(rows, LANES)


W_IN_PAD = 21 * LANES


def _w_in_join(blocks, *, name):
    tm = 256

    def body(b_ref, o_ref):
        o_ref[:, IN_COLS:] = jnp.zeros((tm, W_IN_PAD - IN_COLS), bf16)
        for q in range(N_DEV):
            o_ref[:, IN_SHARD * q:IN_SHARD * (q + 1)] = b_ref[q]

    return pl.pallas_call(
        body, name=name, grid=(D_MODEL // tm,),
        in_specs=[pl.BlockSpec((N_DEV, tm, IN_SHARD), lambda i: (0, i, 0))],
        out_specs=pl.BlockSpec((tm, W_IN_PAD), lambda i: (i, 0)),
        out_shape=jax.ShapeDtypeStruct((D_MODEL, W_IN_PAD), bf16), compiler_params=_params(("arbitrary",)),
    )(blocks)


def _w_in_split(main, fg, *, name):
    tm = 256
    n_main = main.shape[0]

    def body(m_ref, f_ref, o_ref):
        full = jnp.concatenate([m_ref[n] for n in range(n_main)] + [f_ref[...]], axis=1)
        for q in range(N_DEV):
            o_ref[q] = full[:, IN_SHARD * q:IN_SHARD * (q + 1)]

    return pl.pallas_call(
        body, name=name, grid=(D_MODEL // tm,),
        in_specs=[pl.BlockSpec((n_main, tm, 512), lambda i: (0, i, 0)), pl.BlockSpec((tm, LANES), lambda i: (i, 0))],
        out_specs=pl.BlockSpec((N_DEV, tm, IN_SHARD), lambda i: (0, i, 0)),
        out_shape=jax.ShapeDtypeStruct((N_DEV, D_MODEL, IN_SHARD), bf16), compiler_params=_params(("arbitrary",)),
    )(main, fg)


def _block_diag(w):
    eye = jnp.eye(HEADS, dtype=w.dtype)
    return jnp.einsum("hij,hk->hikj", w, eye).reshape(LRU_W, LRU_W)


def _diag_blocks(dw):
    per = dw.shape[1] // HEAD_D
    blocks = [dw[:, HEAD_D * b:HEAD_D * (b + 1), HEAD_D * b:HEAD_D * (b + 1)] for b in range(per)]
    return jnp.stack(blocks, axis=1).reshape(HEADS, HEAD_D, HEAD_D)


def _local_step(x, target, sent, small, *, tm=512, tm_ffn=1024):
    ln1 = (small["ln1_g"], small["ln1_b"])
    ln2 = (small["ln2_g"], small["ln2_b"])
    ln3 = (small["ln3_g"], small["ln3_b"])

    xh1, rs1, hg1, hu1, wg1, wu1, wd1, w_in_g, w_out_g, conv_w_g = _ffn1_fwd_gathering(
        x, (sent["ffn1_w_gate"], sent["ffn1_w_up"], sent["ffn1_w_down"]),
        _Exchange([sent["w_in"], sent["w_out"], sent["conv_w"]], gather=True), tm=tm_ffn, name="ffn1_fwd")
    w_in = _w_in_join(w_in_g, name="w_in_join")
    w_out = w_out_g.reshape(D_MODEL, D_MODEL)
    conv_w = conv_w_g.transpose(1, 0, 2).reshape(4, LRU_W)
    qkv, zl, zfg = _in_proj(xh1, ln1[0], ln1[1], w_in, tm=tm_ffn, name="in_proj")
    bfg = jnp.pad(small["b_forget"], ((0, 0), (0, LANES - HEADS)))
    cum = _cum_fwd(zfg, bfg, name="cum_fwd")
    q_aug, k_aug, v_aug = _attn_prep_fwd(qkv, cum, tm=tm, name="attn_prep_fwd")
    o, lse, wg2, wu2 = _attn_fwd2(
        q_aug, k_aug, v_aug, name="attn_fwd",
        host=_Hosts(_Exchange([sent["ffn2_w_gate"]], gather=True),
                    _Exchange([sent["ffn2_w_up"]], gather=True, direct=True)))
    wa_bd = _block_diag(small["rg_wa"]).astype(bf16)
    wx_bd = _block_diag(small["rg_wx"]).astype(bf16)
    ba = small["rg_ba"].reshape(1, LRU_W)
    bx = small["rg_bx"].reshape(1, LRU_W)
    u, h, lru, wd2 = _lru_fwd(zl, conv_w, small["conv_b"], wa_bd, ba, wx_bd, bx, small["lru_lambda"],
                              name="lru_fwd", host=_Exchange([sent["ffn2_w_down"]], gather=True))
    xh2, rs2 = _mmln([(o, 0, FOX_W, w_out, 0, D_MODEL, "nn"), (lru, 0, LRU_W, w_out, 1, D_MODEL, "nn")],
                     tm=tm_ffn, name="mix_fwd", resid=("affine", xh1) + ln1, resid_scale=ALPHA, epi="ln_fwd")
    dpre3, sq_rows, g_ln3g, g_ln3b, hg2, hu2 = _ffn_fwd_loss(
        xh2, ln2[0], ln2[1], wg2, wu2, wd2, ln3[0], ln3[1], target, tm=tm_ffn, name="ffn2_fwd_loss")

    dpre2, g_ln2g, g_ln2b, dhg2, dhu2, a2 = _ffn_bwd(dpre3, hg2, hu2, wg2, wu2, wd2,
                                                     (xh2, rs2, ln2[0]), tm=tm_ffn, name="ffn2_bwd")
    wgrad = dict(out_dtype=bf16, tm=D_MODEL, mb=1, tn=FF_TILE, nb=4, tk=512, pair=True)
    wdgrad = dict(out_dtype=bf16, tm=512, mb=4, tn=D_MODEL, nb=1, tk=512, out_scale=0.5, pair=True)
    between_chips = functools.partial(_Exchange, gather=False, chips=True)
    g_wg2 = _mm_tn(xh2, dhg2, name="g_wg2", affine=ln2, **wgrad)
    g_wu2 = _mm_tn(xh2, dhu2, name="g_wu2", affine=ln2, **wgrad)
    g_wd2 = _mm_tn(a2, dpre3, name="g_wd2", **wdgrad)

    dmix = _mmln([(dpre2, 0, D_MODEL, w_out, 0, D_MODEL, "nt")], tm=tm_ffn, name="dmix_bwd")
    g_wout_a = _mm(o, dpre2, mode="tn", out_dtype=bf16, tm=512, tn=D_MODEL, tk=512, name="g_wout_fox")
    g_wout_b = _mm(lru, dpre2, mode="tn", out_dtype=bf16, tm=512, tn=D_MODEL, tk=512, name="g_wout_lru")
    g_wout_blocked = jnp.concatenate([g_wout_a, g_wout_b], axis=0).reshape(N_DEV, D_MODEL // N_DEV, D_MODEL)
    dlx, dlg, g_cw, g_cb, g_ba, g_bx, g_lam, g_wa4, g_wx4, p_wg2, p_wout = _lru_bwd(
        dmix, zl, u, h, conv_w, wa_bd, ba, wx_bd, bx, small["lru_lambda"], name="lru_bwd",
        host=_Hosts(between_chips([g_wg2]), _Exchange([g_wout_blocked], gather=False)))
    qb_aug, do_aug = _attn_prep_bwd(qkv, cum, lse, dmix, o, tm=tm, name="attn_prep_bwd")
    dq, dcum_q, dk, dv, dcum_k, p_wu2, p_wd2 = _attn_bwd(qb_aug, k_aug, v_aug, do_aug, name="attn_bwd",
                                                         host=between_chips([g_wu2, g_wd2]))
    dfg, g_bf = _cum_bwd(dcum_q, dcum_k, zfg, bfg, name="cum_bwd")

    dz = [(dq, 0, 512), (dk, 1, 512), (dv, 2, 512), (dlx, 3, 512), (dlg, 4, 512), (dfg, 20, LANES)]
    dpre1, g_ln1g, g_ln1b = _mmln(
        [(arr, 0, w, w_in, cb, w, "nt") for (arr, cb, w) in dz],
        tm=tm, name="dx1_bwd", resid=("plain", dpre2), resid_scale=ALPHA, epi="ln_bwd", ln=(xh1, rs1, ln1[0]))
    g_win_main = _mm_tn(xh1, [arr for arr, _, _ in dz[:5]], out_dtype=bf16, tm=D_MODEL, mb=1, tn=512, nb=5, tk=512,
                        name="g_win", affine=ln1, out_blocked=True)
    g_win_fg = _mm(xh1, dfg, mode="tn", out_dtype=bf16, tm=D_MODEL, tn=LANES, tk=512, name="g_win_fg", affine=ln1)
    g_win_blocked = _w_in_split(g_win_main, g_win_fg, name="w_in_split")
    dhg1, dhu1, a1, p_win = _ffn_bwd_act(dpre1, hg1, hu1, wd1, tm=tm_ffn, name="ffn1_bwd_act",
                                         host=_Exchange([g_win_blocked], gather=False))
    small_g = {
        "ln1_g": g_ln1g, "ln1_b": g_ln1b, "b_forget": g_bf[:, :HEADS], "conv_w": g_cw, "conv_b": g_cb,
        "rg_wa": _diag_blocks(g_wa4), "rg_ba": g_ba.reshape(HEADS, HEAD_D),
        "rg_wx": _diag_blocks(g_wx4), "rg_bx": g_bx.reshape(HEADS, HEAD_D), "lru_lambda": g_lam,
        "ln2_g": g_ln2g, "ln2_b": g_ln2b, "ln3_g": g_ln3g, "ln3_b": g_ln3b,
    }
    small_g["loss"] = (0.5 / D_MODEL) * jnp.sum(sq_rows, keepdims=True)
    pieces = [_pack_rows(small_g[n]) for n in PACKED]
    packed = jnp.concatenate(pieces + [jnp.zeros((PACK_ROWS - sum(p.shape[0] for p in pieces), LANES), f32)])
    g_wg1, all_packed = _mm_tn(x, dhg1, name="g_wg1", host=_Exchange([packed], gather=True), **wgrad)
    g_wu1, p_wg1 = _mm_tn(x, dhu1, name="g_wu1", host=between_chips([g_wg1]), **wgrad)
    g_wd1, p_wu1 = _mm_tn(a1, dpre1, name="g_wd1", host=between_chips([g_wu1]), **wdgrad)
    grad_x, p_wd1 = _ffn_bwd_dx(dpre1, dhg1, dhu1, wg1, wu1, tm=tm_ffn, name="ffn1_bwd_dx",
                                host=between_chips([g_wd1]))
    parts = {
        "ffn1_w_gate": p_wg1, "ffn1_w_up": p_wu1, "ffn1_w_down": p_wd1, "w_in": p_win, "w_out": p_wout,
        "ffn2_w_gate": p_wg2, "ffn2_w_up": p_wu2, "ffn2_w_down": p_wd2,
    }
    return grad_x, parts, all_packed, {n: small_g[n].shape for n in PACKED}


def _adam_math(w, g, m, v):
    m2 = ADAM_B1 * m + (1.0 - ADAM_B1) * g
    v2 = ADAM_B2 * v + (1.0 - ADAM_B2) * (g * g)
    m_hat = m2 / (1.0 - ADAM_B1 ** ADAM_STEP)
    v_hat = v2 / (1.0 - ADAM_B2 ** ADAM_STEP)
    delta = -ADAM_LR * (m_hat / (jnp.sqrt(v_hat) + ADAM_EPS) + ADAM_WD * w)
    return delta, m2, v2


ADAM_TILE_ELEMS = 128 * 1024


def _adamw_big(items, *, name):
    _, r, c = items[0][1].shape
    n_parts = items[0][0].shape[0]
    n_items = len(items)
    assert all(it[1].shape == (1, r, c) and it[0].shape == (n_parts, r, c) for it in items), name
    tr = max(d for d in range(8, r + 1, 8) if r % d == 0 and d * c <= ADAM_TILE_ELEMS)

    def body(*refs):
        ins, outs = refs[:4 * n_items], refs[4 * n_items:]
        for k in range(n_items):
            p_ref, w_ref, m_ref, v_ref = ins[4 * k:4 * k + 4]
            g = p_ref[0].astype(f32)
            for q in range(1, n_parts):
                g = g + p_ref[q].astype(f32)
            d, m2, v2 = _adam_math(w_ref[...], g, m_ref[...], v_ref[...])
            for o_ref, val in zip(outs[4 * k:4 * k + 4], (g, d, m2, v2)):
                o_ref[...] = val

    blk = pl.BlockSpec((None, tr, c), lambda i: (0, i, 0))
    res = pl.pallas_call(
        body, name=name, grid=(r // tr,),
        in_specs=[pl.BlockSpec((n_parts, tr, c), lambda i: (0, i, 0)), blk, blk, blk] * n_items,
        out_specs=[blk] * (4 * n_items), out_shape=[jax.ShapeDtypeStruct((1, r, c), f32)] * (4 * n_items),
        compiler_params=_params(("arbitrary",)),
    )(*[a for it in items for a in it])
    return [res[4 * k:4 * k + 4] for k in range(n_items)]


def _adamw_small(items, *, name):
    n = len(items)

    def body(*refs):
        ins, outs = refs[:4 * n], refs[4 * n:]
        for k in range(n):
            g, w, m, v = (ins[4 * k + q][...] for q in range(4))
            d, m2, v2 = _adam_math(w, g, m, v)
            outs[3 * k][...] = d
            outs[3 * k + 1][...] = m2
            outs[3 * k + 2][...] = v2

    vm = pl.BlockSpec(memory_space=pltpu.VMEM)
    flat = [a for item in items for a in item]
    out_shape = [jax.ShapeDtypeStruct(item[1].shape, f32) for item in items for _ in range(3)]
    return pl.pallas_call(
        body, name=name, in_specs=[vm] * (4 * n), out_specs=[vm] * (3 * n), out_shape=out_shape,
    )(*flat)


def _sum_parts(parts, *, name):
    def body(p_ref, o_ref):
        acc = p_ref[0]
        for q in range(1, N_DEV):
            acc = acc + p_ref[q]
        o_ref[...] = acc

    vm = pl.BlockSpec(memory_space=pltpu.VMEM)
    return pl.pallas_call(
        body, name=name, in_specs=[vm], out_specs=vm, out_shape=jax.ShapeDtypeStruct(parts.shape[1:], f32),
    )(parts)


WEIGHTS = ["ffn1_w_gate", "ffn1_w_up", "ffn1_w_down", "ln1_g", "ln1_b", "w_in", "b_forget", "conv_w", "conv_b",
           "rg_wa", "rg_ba", "rg_wx", "rg_bx", "lru_lambda", "w_out", "ln2_g", "ln2_b",
           "ffn2_w_gate", "ffn2_w_up", "ffn2_w_down", "ln3_g", "ln3_b"]
BIG = ["ffn1_w_gate", "ffn1_w_up", "ffn1_w_down", "w_in", "w_out", "ffn2_w_gate", "ffn2_w_up", "ffn2_w_down"]
PACKED = ["ln1_g", "ln1_b", "ln2_g", "ln2_b", "ln3_g", "ln3_b", "conv_b", "rg_ba", "rg_bx", "lru_lambda",
          "conv_w", "rg_wa", "rg_wx", "b_forget", "loss"]
PACK_ROWS = 600


def _two_d(a):
    return a.reshape((-1, a.shape[-1]))


def _transport(a):
    return _two_d(a)


def kernel(x, ffn1_w_gate, ffn1_w_up, ffn1_w_down, ln1_g, ln1_b, w_in, b_forget, conv_w, conv_b, rg_wa, rg_ba, rg_wx, rg_bx, lru_lambda, w_out, ln2_g, ln2_b, ffn2_w_gate, ffn2_w_up, ffn2_w_down, ln3_g, ln3_b, loss_target, m_ffn1_w_gate, m_ffn1_w_up, m_ffn1_w_down, m_ln1_g, m_ln1_b, m_w_in, m_b_forget, m_conv_w, m_conv_b, m_rg_wa, m_rg_ba, m_rg_wx, m_rg_bx, m_lru_lambda, m_w_out, m_ln2_g, m_ln2_b, m_ffn2_w_gate, m_ffn2_w_up, m_ffn2_w_down, m_ln3_g, m_ln3_b, v_ffn1_w_gate, v_ffn1_w_up, v_ffn1_w_down, v_ln1_g, v_ln1_b, v_w_in, v_b_forget, v_conv_w, v_conv_b, v_rg_wa, v_rg_ba, v_rg_wx, v_rg_bx, v_lru_lambda, v_w_out, v_ln2_g, v_ln2_b, v_ffn2_w_gate, v_ffn2_w_up, v_ffn2_w_down, v_ln3_g, v_ln3_b):
    w_args = (ffn1_w_gate, ffn1_w_up, ffn1_w_down, ln1_g, ln1_b, w_in, b_forget, conv_w, conv_b, rg_wa, rg_ba, rg_wx, rg_bx, lru_lambda, w_out, ln2_g, ln2_b, ffn2_w_gate, ffn2_w_up, ffn2_w_down, ln3_g, ln3_b)
    m_args = (m_ffn1_w_gate, m_ffn1_w_up, m_ffn1_w_down, m_ln1_g, m_ln1_b, m_w_in, m_b_forget, m_conv_w, m_conv_b, m_rg_wa, m_rg_ba, m_rg_wx, m_rg_bx, m_lru_lambda, m_w_out, m_ln2_g, m_ln2_b, m_ffn2_w_gate, m_ffn2_w_up, m_ffn2_w_down, m_ln3_g, m_ln3_b)
    v_args = (v_ffn1_w_gate, v_ffn1_w_up, v_ffn1_w_down, v_ln1_g, v_ln1_b, v_w_in, v_b_forget, v_conv_w, v_conv_b, v_rg_wa, v_rg_ba, v_rg_wx, v_rg_bx, v_lru_lambda, v_w_out, v_ln2_g, v_ln2_b, v_ffn2_w_gate, v_ffn2_w_up, v_ffn2_w_down, v_ln3_g, v_ln3_b)
    w = dict(zip(WEIGHTS, w_args))
    m = dict(zip(WEIGHTS, m_args))
    v = dict(zip(WEIGHTS, v_args))
    me = 4 * lax.axis_index("x") + 2 * lax.axis_index("y") + lax.axis_index("c")

    sent = {n: _transport(w[n]).astype(bf16) for n in BIG}
    sent["conv_w"] = _two_d(w["conv_w"])
    small = {n: w[n] for n in ("ln1_g", "ln1_b", "ln2_g", "ln2_b", "ln3_g", "ln3_b", "b_forget", "conv_b",
                               "lru_lambda")}
    small.update({n: w[n][0] for n in ("rg_wa", "rg_ba", "rg_wx", "rg_bx")})

    grad_x, parts, all_packed, small_shapes = _local_step(x[0], loss_target[0], sent, small)

    total = _sum_parts(all_packed, name="sum_small_grads")
    grads, off = {}, 0
    for n in PACKED:
        size = math.prod(small_shapes[n])
        rows = -(-size // LANES)
        grads[n] = total[off:off + rows].reshape(-1)[:size].reshape(small_shapes[n])
        off += rows
    loss = grads.pop("loss").reshape(())
    grads["conv_w"] = lax.dynamic_slice_in_dim(grads["conv_w"], me * (LRU_W // N_DEV), LRU_W // N_DEV, axis=1)

    delta, new_m, new_v = {}, {}, {}
    for group in (("ffn1_w_gate", "ffn1_w_up", "ffn2_w_gate", "ffn2_w_up"), ("ffn1_w_down", "ffn2_w_down"),
                  ("w_in",), ("w_out",)):
        done = _adamw_big([(parts[n], w[n], m[n], v[n]) for n in group], name="adamw_" + group[0])
        for n, (g, d, m2, v2) in zip(group, done):
            grads[n], delta[n], new_m[n], new_v[n] = g, d, m2, v2
    small_names = [n for n in WEIGHTS if n not in BIG]
    outs = _adamw_small([(_two_d(grads[n]), _two_d(w[n]), _two_d(m[n]), _two_d(v[n])) for n in small_names],
                        name="adamw_small")
    for k, n in enumerate(small_names):
        delta[n], new_m[n], new_v[n] = outs[3 * k], outs[3 * k + 1], outs[3 * k + 2]

    def shaped(d):
        return [d[n].reshape(w[n].shape) for n in WEIGHTS]

    return (loss, grad_x[None], *shaped(grads), *shaped(delta), *shaped(new_m), *shaped(new_v))
```

```python
import functools
import math

import jax
import jax.numpy as jnp
from jax import lax
from jax.experimental import pallas as pl
from jax.experimental.pallas import tpu as pltpu

f32 = jnp.float32
bf16 = jnp.bfloat16

N_DEV = 8
D_MODEL = 1024
D_FF = 4096
FF_TILE = D_FF // N_DEV
FOX_W = 512
LRU_W = 512
HEADS = 8
HEAD_D = 64
IN_COLS = 2568
IN_SHARD = IN_COLS // N_DEV
LANES = 128
LN_EPS = 1e-5
ALPHA = 2.0 ** 0.25
ATT_SCALE = 1.0 / math.sqrt(HEAD_D)
LRU_C = 8.0
NEG_BIG = -1e30

ADAM_LR = 0.001
ADAM_B1 = 0.9
ADAM_B2 = 0.999
ADAM_EPS = 1e-08
ADAM_WD = 0.01
ADAM_STEP = 10

VMEM_LIMIT = 56 * 1024 * 1024
MESH_T = pl.DeviceIdType.MESH


def _params(sem, **kw):
    return pltpu.CompilerParams(dimension_semantics=sem, vmem_limit_bytes=VMEM_LIMIT, **kw)


def _sigmoid(x):
    return 1.0 / (1.0 + jnp.exp(-x))


def _sigmoid_tanh(x):
    return 0.5 * jnp.tanh(0.5 * x) + 0.5


def _softplus(x):
    return jnp.maximum(x, 0.0) + jnp.log(1.0 + jnp.exp(-jnp.abs(x)))


def _one_minus_exp(x):
    series = -x * (1.0 + x * (0.5 + x * (1.0 / 6 + x * (1.0 / 24 + x * (1.0 / 120 + x * (1.0 / 720))))))
    return jnp.where(x > -0.125, series, 1.0 - jnp.exp(x))


_GELU_C = math.sqrt(2.0 / math.pi)


def _gelu_and_grad(x):
    inner = _GELU_C * (x + 0.044715 * x * x * x)
    t = jnp.tanh(inner)
    g = 0.5 * x * (1.0 + t)
    dg = 0.5 * (1.0 + t) + 0.5 * x * (1.0 - t * t) * _GELU_C * (1.0 + 3 * 0.044715 * x * x)
    return g, dg


def _ln_fwd_tile(pre):
    mu = jnp.mean(pre, axis=-1, keepdims=True)
    xc = pre - mu
    var = jnp.mean(xc * xc, axis=-1, keepdims=True)
    rstd = lax.rsqrt(var + LN_EPS)
    return xc * rstd, rstd


def _ln_bwd_tile(dy, xhat, rstd, g):
    dyg = dy * g
    m1 = jnp.mean(dyg, axis=-1, keepdims=True)
    m2 = jnp.mean(dyg * xhat, axis=-1, keepdims=True)
    dpre = rstd * (dyg - m1 - xhat * m2)
    return dpre, jnp.sum(dy * xhat, axis=0, keepdims=True), jnp.sum(dy, axis=0, keepdims=True)


_NT = (((1,), (1,)), ((), ()))
_TN = (((0,), (0,)), ((), ()))


class _Exchange:
    def __init__(self, arrs, gather, chips=False, direct=False):
        self.arrs, self.gather, self.n, self.chips = list(arrs), gather, len(arrs), chips
        self.relays = gather and not direct

    def out_shape(self):
        return [jax.ShapeDtypeStruct(((N_DEV,) + a.shape) if self.gather else a.shape, a.dtype) for a in self.arrs]

    def scratch(self):
        n_remote = self.n * (N_DEV - 1)
        return [pltpu.SemaphoreType.DMA((n_remote,)), pltpu.SemaphoreType.DMA((n_remote,)),
                pltpu.SemaphoreType.DMA((self.n,))]

    def copies(self, ins, outs, sems):
        send_sems, recv_sems, local_sems = sems
        x, y, c = lax.axis_index("x"), lax.axis_index("y"), lax.axis_index("c")
        me = 2 * x + y if self.chips else 4 * x + 2 * y + c
        out = []
        for k in range(self.n):
            for d in (range(2, N_DEV, 2) if self.chips else range(1, N_DEV)):
                px = 1 - x if d & 4 else x
                py = 1 - y if d & 2 else y
                pc = 1 - c if d & 1 else c
                sem = k * (N_DEV - 1) + d - 1
                out.append(pltpu.make_async_remote_copy(
                    src_ref=ins[k] if self.gather else ins[k].at[2 * px + py if self.chips else 4 * px + 2 * py + pc],
                    dst_ref=outs[k].at[me],
                    send_sem=send_sems.at[sem], recv_sem=recv_sems.at[sem],
                    device_id=(px, py, pc), device_id_type=MESH_T))
            out.append(pltpu.make_async_copy(ins[k] if self.gather else ins[k].at[me], outs[k].at[me],
                                             local_sems.at[k]))
        return out

    def gather_copies(self, ins, outs, sems):
        send_sems, recv_sems, local_sems = sems
        x, y, c = lax.axis_index("x"), lax.axis_index("y"), lax.axis_index("c")
        sibling = (x, y, 1 - c)
        chips = [(1 - x, y), (x, 1 - y), (1 - x, 1 - y)]
        out = []
        for k in range(self.n):
            def copy(s, block, to, src=None, k=k):
                rows = outs[k].at[4 * block[0] + 2 * block[1] + block[2]]
                sem = k * (N_DEV - 1) + s
                return pltpu.make_async_remote_copy(
                    src_ref=rows if src is None else src, dst_ref=rows, send_sem=send_sems.at[sem],
                    recv_sem=recv_sems.at[sem], device_id=to, device_id_type=MESH_T)

            first = [copy(0, (x, y, c), sibling, src=ins[k])]
            first += [copy(1 + q, (x, y, c), (*chip, c), src=ins[k]) for q, chip in enumerate(chips)]
            passed = [copy(4 + q, (*chip, c), sibling) for q, chip in enumerate(chips)]
            own = pltpu.make_async_copy(ins[k], outs[k].at[4 * x + 2 * y + c], local_sems.at[k])
            out.append((first, passed, own, copy))
        return out, sibling, chips, (x, y, c)

    def start(self, ins, outs, sems):
        if not self.relays:
            for cp in self.copies(ins, outs, sems):
                cp.start()
            return
        per_array, _, _, _ = self.gather_copies(ins, outs, sems)
        for first, _, own, _ in per_array:
            own.start()
            for cp in first:
                cp.start()

    def relay(self, ins, outs, sems):
        per_array, sibling, chips, (x, y, c) = self.gather_copies(ins, outs, sems)
        for first, passed, own, copy in per_array:
            for q, chip in enumerate(chips):
                copy(1 + q, (*chip, c), (x, y, c)).wait_recv()
                passed[q].start()

    def wait(self, ins, outs, sems, relayed=False):
        if not self.relays:
            for cp in self.copies(ins, outs, sems):
                cp.wait()
            return
        if not relayed:
            self.relay(ins, outs, sems)
        per_array, sibling, chips, (x, y, c) = self.gather_copies(ins, outs, sems)
        for first, passed, own, copy in per_array:
            copy(0, sibling, (x, y, c)).wait_recv()
            for q, chip in enumerate(chips):
                copy(4 + q, (*chip, 1 - c), (x, y, c)).wait_recv()
            for cp in first + passed:
                cp.wait_send()
            own.wait()


class _Hosts:
    def __init__(self, *hosts):
        self.hosts = hosts
        self.relays = any(h.relays for h in hosts)
        self.n = sum(h.n for h in hosts)
        self.arrs = [a for h in hosts for a in h.arrs]

    def out_shape(self):
        return [sh for h in self.hosts for sh in h.out_shape()]

    def scratch(self):
        return [sc for h in self.hosts for sc in h.scratch()]

    def _each(self, ins, outs, sems):
        at = 0
        for k, h in enumerate(self.hosts):
            yield h, ins[at:at + h.n], outs[at:at + h.n], sems[3 * k:3 * k + 3]
            at += h.n

    def start(self, ins, outs, sems):
        for h, h_in, h_out, h_sems in self._each(ins, outs, sems):
            h.start(h_in, h_out, h_sems)

    def relay(self, ins, outs, sems):
        for h, h_in, h_out, h_sems in self._each(ins, outs, sems):
            if h.relays:
                h.relay(h_in, h_out, h_sems)

    def wait(self, ins, outs, sems, relayed=False):
        for h, h_in, h_out, h_sems in self._each(ins, outs, sems):
            h.wait(h_in, h_out, h_sems, relayed=relayed and h.relays)


def _hosted_call(host, body, *, name, grid, in_specs, out_specs, out_shape, scratch_shapes=(), compiler_params):
    out_specs = list(out_specs) if isinstance(out_specs, (list, tuple)) else [out_specs]
    out_shape = list(out_shape) if isinstance(out_shape, (list, tuple)) else [out_shape]
    if host is None:
        return pl.pallas_call(body, name=name, grid=grid, in_specs=in_specs, out_specs=out_specs,
                              out_shape=out_shape, scratch_shapes=list(scratch_shapes),
                              compiler_params=compiler_params)
    n_in, n_out, n_scr, k = len(in_specs), len(out_shape), len(scratch_shapes), host.n

    def wrapped(*refs):
        ins, h_in = refs[:n_in], refs[n_in:n_in + k]
        outs, h_out = refs[n_in + k:n_in + k + n_out], refs[n_in + k + n_out:n_in + 2 * k + n_out]
        scr, sems = refs[n_in + 2 * k + n_out:n_in + 2 * k + n_out + n_scr], refs[n_in + 2 * k + n_out + n_scr:]
        ids = [pl.program_id(a) for a in range(len(grid))]
        first = functools.reduce(jnp.logical_and, [i == 0 for i in ids])
        last = functools.reduce(jnp.logical_and, [i == g - 1 for i, g in zip(ids, grid)])
        steps = math.prod(grid)
        relay_at = (3 * steps) // 4 if host.relays and steps >= 8 else None

        @pl.when(first)
        def _():
            host.start(h_in, h_out, sems)

        if relay_at is not None:
            coords, rest = [], relay_at
            for g in reversed(grid):
                coords.append(rest % g)
                rest //= g

            @pl.when(functools.reduce(jnp.logical_and, [i == cd for i, cd in zip(ids, reversed(coords))]))
            def _():
                host.relay(h_in, h_out, sems)

        body(*ins, *outs, *scr)

        @pl.when(last)
        def _():
            host.wait(h_in, h_out, sems, relayed=relay_at is not None)

    hbm = pl.BlockSpec(memory_space=pl.ANY)
    call = pl.pallas_call(
        wrapped, name=name, grid=grid, in_specs=list(in_specs) + [hbm] * k, out_specs=out_specs + [hbm] * k,
        out_shape=out_shape + host.out_shape(), scratch_shapes=list(scratch_shapes) + host.scratch(),
        compiler_params=compiler_params)
    return lambda *args: call(*args, *host.arrs)


def _ffn_fwd_loss(xhat, g_in, b_in, wg, wu, wd, g_out, b_out, target, *, tm, name):
    t = xhat.shape[0]
    nj = N_DEV

    def body(x_ref, g_ref, b_ref, wg_ref, wu_ref, wd_ref, go_ref, bo_ref, tg_ref,
             dx_ref, sq_ref, gg_ref, gb_ref, hg_ref, hu_ref, xb, acc):
        i = pl.program_id(0)
        j = pl.program_id(1)

        @pl.when(j == 0)
        def _():
            xb[...] = (x_ref[...] * g_ref[...] + b_ref[...]).astype(bf16)
            acc[...] = jnp.zeros_like(acc)

        hg = jnp.dot(xb[...], wg_ref[...], preferred_element_type=f32)
        hu = jnp.dot(xb[...], wu_ref[...], preferred_element_type=f32)
        hg_ref[...] = hg.astype(bf16)
        hu_ref[...] = hu.astype(bf16)
        a = hg * _sigmoid_tanh(hg) * hu
        acc[...] += jnp.dot(a.astype(bf16), wd_ref[...], preferred_element_type=f32)

        @pl.when(j == nj - 1)
        def _():
            x = x_ref[...] * g_ref[...] + b_ref[...]
            xo, rstd = _ln_fwd_tile(ALPHA * x + 0.5 * acc[...])
            diff = xo * go_ref[...] + bo_ref[...] - tg_ref[...]
            sq = jnp.sum(diff * diff, axis=0, keepdims=True)
            dprev, gg, gb = _ln_bwd_tile(diff * (1.0 / D_MODEL), xo, rstd, go_ref[...])
            dx_ref[...] = dprev

            @pl.when(i == 0)
            def _():
                sq_ref[...] = sq
                gg_ref[...] = gg
                gb_ref[...] = gb

            @pl.when(i > 0)
            def _():
                sq_ref[...] += sq
                gg_ref[...] += gg
                gb_ref[...] += gb

    tok = pl.BlockSpec((tm, D_MODEL), lambda i, j: (i, 0))
    row = pl.BlockSpec((1, D_MODEL), lambda i, j: (0, 0))
    hid = pl.BlockSpec((tm, FF_TILE), lambda i, j: (i, j))
    return pl.pallas_call(
        body, name=name, grid=(t // tm, nj),
        in_specs=[tok, row, row,
                  pl.BlockSpec((None, D_MODEL, FF_TILE), lambda i, j: (j, 0, 0)),
                  pl.BlockSpec((None, D_MODEL, FF_TILE), lambda i, j: (j, 0, 0)),
                  pl.BlockSpec((None, FF_TILE, D_MODEL), lambda i, j: (j, 0, 0)), row, row, tok],
        out_specs=[tok, row, row, row, hid, hid],
        out_shape=[jax.ShapeDtypeStruct((t, D_MODEL), f32)] + [jax.ShapeDtypeStruct((1, D_MODEL), f32)] * 3
        + [jax.ShapeDtypeStruct((t, D_FF), bf16)] * 2,
        scratch_shapes=[pltpu.VMEM((tm, D_MODEL), bf16), pltpu.VMEM((tm, D_MODEL), f32)],
        compiler_params=_params(("arbitrary", "arbitrary")),
    )(xhat, g_in, b_in, wg, wu, wd, g_out, b_out, target)


def _ffn1_fwd_gathering(x, own, extra, *, tm, name):
    t = x.shape[0]
    n_i = t // tm
    n_arr = 3
    k_extra = extra.n
    ex = _Exchange(list(own), gather=True)
    ax, ay, ac = lax.axis_index("x"), lax.axis_index("y"), lax.axis_index("c")
    order = jnp.stack([4 * px + 2 * py + pc for px, py in ((ax, ay), (1 - ax, ay), (ax, 1 - ay), (1 - ax, 1 - ay))
                       for pc in (ac, 1 - ac)]).astype(jnp.int32)
    arrival = [None, (0, None), (1, 0), (4, None), (2, 1), (5, None), (3, 2), (6, None)]

    def body(order_ref, x_ref, *refs):
        w_in, e_in = refs[:n_arr], refs[n_arr:n_arr + k_extra]
        refs = refs[n_arr + k_extra:]
        xo_ref, rstd_ref, hg_ref, hu_ref = refs[:4]
        w_all, e_out = refs[4:4 + n_arr], refs[4 + n_arr:4 + n_arr + k_extra]
        acc, wgb, wub, wdb, fetch_sems, send_sems, recv_sems, local_sems = refs[4 + n_arr + k_extra:12 + n_arr + k_extra]
        e_sems = refs[12 + n_arr + k_extra:]
        bufs = (wgb, wub, wdb)
        s = pl.program_id(0)
        i = pl.program_id(1)
        per_array, sibling, chips, (x_, y_, c_) = ex.gather_copies(w_in, w_all, (send_sems, recv_sems, local_sems))

        def fetch(pos, slot):
            return [pltpu.make_async_copy(w_in[a] if pos == 0 else w_all[a].at[order_ref[pos]],
                                          bufs[a].at[slot], fetch_sems.at[n_arr * slot + a]) for a in range(n_arr)]

        def source_of(pos):
            chip = (x_, y_) if pos < 2 else chips[(pos - 2) // 2]
            return (*chip, c_ if pos % 2 == 0 else 1 - c_)

        @pl.when(jnp.logical_and(s == 0, i == 0))
        def _():
            for q in range(4):
                for first, _, own_copy, _ in per_array:
                    if q == 0:
                        own_copy.start()
                    first[q].start()
            for cp in fetch(0, 0):
                cp.start()
            for cp in fetch(0, 0):
                cp.wait()

        @pl.when(jnp.logical_and(s == N_DEV // 2, i == 0))
        def _():
            extra.start(e_in, e_out, e_sems)

        for pos in range(1, N_DEV):
            @pl.when(jnp.logical_and(s == pos - 1, i == n_i - 1))
            def _(pos=pos):
                sem, passes = arrival[pos]
                for _, passed, _, copy in per_array:
                    copy(sem, source_of(pos), (x_, y_, c_)).wait_recv()
                    if passes is not None:
                        passed[passes].start()
                for cp in fetch(pos, pos % 2):
                    cp.start()

            @pl.when(jnp.logical_and(s == pos, i == 0))
            def _(pos=pos):
                for cp in fetch(pos, pos % 2):
                    cp.wait()

        slot = s % 2
        xb = x_ref[...].astype(bf16)
        hg = jnp.dot(xb, wgb[slot], preferred_element_type=f32)
        hu = jnp.dot(xb, wub[slot], preferred_element_type=f32)
        hg_ref[...] = hg.astype(bf16)
        hu_ref[...] = hu.astype(bf16)
        a = hg * _sigmoid_tanh(hg) * hu
        part = jnp.dot(a.astype(bf16), wdb[slot], preferred_element_type=f32)

        @pl.when(s == 0)
        def _():
            acc[i] = part

        @pl.when(s > 0)
        def _():
            acc[i] += part

        @pl.when(s == N_DEV - 1)
        def _():
            xo, rstd = _ln_fwd_tile(ALPHA * x_ref[...] + 0.5 * acc[i])
            xo_ref[...] = xo
            rstd_ref[...] = rstd

        @pl.when(jnp.logical_and(s == N_DEV - 1, i == n_i - 1))
        def _():
            for first, passed, own_copy, _ in per_array:
                for cp in first + passed:
                    cp.wait_send()
                own_copy.wait()
            extra.wait(e_in, e_out, e_sems)

    hbm = pl.BlockSpec(memory_space=pl.ANY)
    last = N_DEV - 1
    tok_out = pl.BlockSpec((tm, D_MODEL), lambda s, i, o: (jnp.where(s == last, i, 0), 0))
    col_out = pl.BlockSpec((tm, 1), lambda s, i, o: (jnp.where(s == last, i, 0), 0))
    hid = pl.BlockSpec((tm, FF_TILE), lambda s, i, o: (i, o[s]))
    shard_shapes = [(N_DEV,) + w.shape for w in own]
    grid_spec = pltpu.PrefetchScalarGridSpec(
        num_scalar_prefetch=1, grid=(N_DEV, n_i),
        in_specs=[pl.BlockSpec((tm, D_MODEL), lambda s, i, o: (i, 0))] + [hbm] * (n_arr + k_extra),
        out_specs=[tok_out, col_out, hid, hid] + [hbm] * (n_arr + k_extra),
        scratch_shapes=[pltpu.VMEM((n_i, tm, D_MODEL), f32)]
        + [pltpu.VMEM((2,) + w.shape, bf16) for w in own]
        + [pltpu.SemaphoreType.DMA((2 * n_arr,))] + ex.scratch() + extra.scratch())
    res = pl.pallas_call(
        body, name=name, grid_spec=grid_spec,
        out_shape=[jax.ShapeDtypeStruct((t, D_MODEL), f32), jax.ShapeDtypeStruct((t, 1), f32),
                   jax.ShapeDtypeStruct((t, D_FF), bf16), jax.ShapeDtypeStruct((t, D_FF), bf16)]
        + [jax.ShapeDtypeStruct(sh, bf16) for sh in shard_shapes] + extra.out_shape(),
        compiler_params=_params(("arbitrary", "arbitrary")),
    )(order, x, *own, *extra.arrs)
    return res


def _ffn_bwd(dpre, hg, hu, wg, wu, wd, ln_in, *, tm, name, host=None):
    t = dpre.shape[0]
    nj = N_DEV
    with_ln = ln_in is not None

    def body(*refs):
        if with_ln:
            (dp_ref, hg_ref, hu_ref, wg_ref, wu_ref, wd_ref, xh_ref, rs_ref, g_ref,
             dx_ref, gg_ref, gb_ref, dhg_ref, dhu_ref, a_ref, dfb, acc) = refs
        else:
            (dp_ref, hg_ref, hu_ref, wg_ref, wu_ref, wd_ref,
             dx_ref, dhg_ref, dhu_ref, a_ref, dfb, acc) = refs
        i = pl.program_id(0)
        j = pl.program_id(1)

        @pl.when(j == 0)
        def _():
            dfb[...] = (0.5 * dp_ref[...]).astype(bf16)
            acc[...] = jnp.zeros_like(acc)

        da = lax.dot_general(dfb[...], wd_ref[...], _NT, preferred_element_type=f32)
        hgv = hg_ref[...].astype(f32)
        huv = hu_ref[...].astype(f32)
        sg = _sigmoid_tanh(hgv)
        silu = hgv * sg
        a_ref[...] = (silu * huv).astype(bf16)
        dhu = (da * silu).astype(bf16)
        dhg = (da * huv * (sg * (1.0 + hgv * (1.0 - sg)))).astype(bf16)
        dhg_ref[...] = dhg
        dhu_ref[...] = dhu
        acc[...] += (lax.dot_general(dhg, wg_ref[...], _NT, preferred_element_type=f32)
                     + lax.dot_general(dhu, wu_ref[...], _NT, preferred_element_type=f32))

        @pl.when(j == nj - 1)
        def _():
            dx = ALPHA * dp_ref[...] + acc[...]
            if with_ln:
                dprev, gg, gb = _ln_bwd_tile(dx, xh_ref[...], rs_ref[...], g_ref[...])
                dx_ref[...] = dprev

                @pl.when(i == 0)
                def _():
                    gg_ref[...] = gg
                    gb_ref[...] = gb

                @pl.when(i > 0)
                def _():
                    gg_ref[...] += gg
                    gb_ref[...] += gb
            else:
                dx_ref[...] = dx

    tok = pl.BlockSpec((tm, D_MODEL), lambda i, j: (i, 0), pipeline_mode=pl.Buffered(1))
    row = pl.BlockSpec((1, D_MODEL), lambda i, j: (0, 0))
    hid = pl.BlockSpec((tm, FF_TILE), lambda i, j: (i, j))
    in_specs = [tok, hid, hid,
                pl.BlockSpec((None, D_MODEL, FF_TILE), lambda i, j: (j, 0, 0)),
                pl.BlockSpec((None, D_MODEL, FF_TILE), lambda i, j: (j, 0, 0)),
                pl.BlockSpec((None, FF_TILE, D_MODEL), lambda i, j: (j, 0, 0))]
    args = [dpre, hg, hu, wg, wu, wd]
    out_specs = [tok]
    out_shape = [jax.ShapeDtypeStruct((t, D_MODEL), f32)]
    if with_ln:
        in_specs += [tok, pl.BlockSpec((tm, 1), lambda i, j: (i, 0)), row]
        args += list(ln_in)
        out_specs += [row, row]
        out_shape += [jax.ShapeDtypeStruct((1, D_MODEL), f32)] * 2
    out_specs += [hid, hid, hid]
    out_shape += [jax.ShapeDtypeStruct((t, D_FF), bf16)] * 3
    return _hosted_call(
        host, body, name=name, grid=(t // tm, nj), in_specs=in_specs, out_specs=out_specs, out_shape=out_shape,
        scratch_shapes=[pltpu.VMEM((tm, D_MODEL), bf16), pltpu.VMEM((tm, D_MODEL), f32)],
        compiler_params=_params(("arbitrary", "arbitrary")),
    )(*args)


def _ffn_bwd_act(dpre, hg, hu, wd, *, tm, name, host=None):
    t = dpre.shape[0]

    def body(dp_ref, hg_ref, hu_ref, wd_ref, dhg_ref, dhu_ref, a_ref, dfb):
        @pl.when(pl.program_id(1) == 0)
        def _():
            dfb[...] = (0.5 * dp_ref[...]).astype(bf16)

        da = lax.dot_general(dfb[...], wd_ref[...], _NT, preferred_element_type=f32)
        hgv = hg_ref[...].astype(f32)
        huv = hu_ref[...].astype(f32)
        sg = _sigmoid_tanh(hgv)
        silu = hgv * sg
        a_ref[...] = (silu * huv).astype(bf16)
        dhu_ref[...] = (da * silu).astype(bf16)
        dhg_ref[...] = (da * huv * (sg * (1.0 + hgv * (1.0 - sg)))).astype(bf16)

    hid = pl.BlockSpec((tm, FF_TILE), lambda i, j: (i, j))
    return _hosted_call(
        host, body, name=name, grid=(t // tm, N_DEV),
        in_specs=[pl.BlockSpec((tm, D_MODEL), lambda i, j: (i, 0)), hid, hid,
                  pl.BlockSpec((None, FF_TILE, D_MODEL), lambda i, j: (j, 0, 0))],
        out_specs=[hid, hid, hid], out_shape=[jax.ShapeDtypeStruct((t, D_FF), bf16)] * 3,
        scratch_shapes=[pltpu.VMEM((tm, D_MODEL), bf16)],
        compiler_params=_params(("arbitrary", "arbitrary")),
    )(dpre, hg, hu, wd)


def _ffn_bwd_dx(dpre, dhg, dhu, wg, wu, *, tm, name, host=None):
    t = dpre.shape[0]
    nj = N_DEV

    def body(dp_ref, dhg_ref, dhu_ref, wg_ref, wu_ref, dx_ref, acc):
        j = pl.program_id(1)

        @pl.when(j == 0)
        def _():
            acc[...] = jnp.zeros_like(acc)

        acc[...] += (lax.dot_general(dhg_ref[...], wg_ref[...], _NT, preferred_element_type=f32)
                     + lax.dot_general(dhu_ref[...], wu_ref[...], _NT, preferred_element_type=f32))

        @pl.when(j == nj - 1)
        def _():
            dx_ref[...] = ALPHA * dp_ref[...] + acc[...]

    tok = pl.BlockSpec((tm, D_MODEL), lambda i, j: (i, 0))
    hid = pl.BlockSpec((tm, FF_TILE), lambda i, j: (i, j))
    wspec = pl.BlockSpec((None, D_MODEL, FF_TILE), lambda i, j: (j, 0, 0))
    return _hosted_call(
        host, body, name=name, grid=(t // tm, nj), in_specs=[tok, hid, hid, wspec, wspec],
        out_specs=[tok], out_shape=[jax.ShapeDtypeStruct((t, D_MODEL), f32)],
        scratch_shapes=[pltpu.VMEM((tm, D_MODEL), f32)],
        compiler_params=_params(("arbitrary", "arbitrary")),
    )(dpre, dhg, dhu, wg, wu)


def _mm(a, b, *, mode, out_dtype, tm, tn, tk, name, affine=None, a_cols=None, b_cols=None,
        b_blocked=False, out_blocked=False, out_scale=None):
    if mode == "nn":
        m_full, k_full = a.shape
        m_dim, k_dim = (m_full, a_cols[1]) if a_cols else (m_full, k_full)
    else:
        k_dim, m_full = a.shape
        m_dim = a_cols[1] if a_cols else m_full
    a_off = a_cols[0] if a_cols else 0
    if b_blocked:
        n_dim = b.shape[0] * b.shape[2]
        assert b.shape[2] == tn
    else:
        n_dim = b_cols[1] if b_cols else b.shape[1]
    b_off = b_cols[0] if b_cols else 0
    assert m_dim % tm == 0 and n_dim % tn == 0 and k_dim % tk == 0, (name, m_dim, n_dim, k_dim)
    nk = k_dim // tk

    def body(*refs):
        if affine is not None:
            a_ref, g_ref, s_ref, b_ref, o_ref, acc = refs
        else:
            a_ref, b_ref, o_ref, acc = refs
        k = pl.program_id(2)

        @pl.when(k == 0)
        def _():
            acc[...] = jnp.zeros_like(acc)

        av = a_ref[...]
        if affine is not None:
            av = av * g_ref[...] + s_ref[...]
        av = av.astype(bf16)
        bv = b_ref[...].astype(bf16)
        if mode == "nn":
            acc[...] += jnp.dot(av, bv, preferred_element_type=f32)
        else:
            acc[...] += lax.dot_general(av, bv, _TN, preferred_element_type=f32)

        @pl.when(k == nk - 1)
        def _():
            res = acc[...] if out_scale is None else acc[...] * out_scale
            o_ref[...] = res.astype(out_dtype)

    if mode == "nn":
        a_spec = pl.BlockSpec((tm, tk), lambda i, j, k: (i, k + a_off))
        aff_spec = pl.BlockSpec((1, tk), lambda i, j, k: (0, k + a_off))
    else:
        a_spec = pl.BlockSpec((tk, tm), lambda i, j, k: (k, i + a_off))
        aff_spec = pl.BlockSpec((1, tm), lambda i, j, k: (0, i + a_off))
    if b_blocked:
        b_spec = pl.BlockSpec((None, tk, tn), lambda i, j, k: (j, k, 0))
    else:
        b_spec = pl.BlockSpec((tk, tn), lambda i, j, k: (k, j + b_off))
    if out_blocked:
        o_spec = pl.BlockSpec((None, tm, tn), lambda i, j, k: (j, i, 0))
        o_shape = jax.ShapeDtypeStruct((n_dim // tn, m_dim, tn), out_dtype)
    else:
        o_spec = pl.BlockSpec((tm, tn), lambda i, j, k: (i, j))
        o_shape = jax.ShapeDtypeStruct((m_dim, n_dim), out_dtype)
    in_specs = [a_spec] + ([aff_spec, aff_spec] if affine is not None else []) + [b_spec]
    args = [a] + (list(affine) if affine is not None else []) + [b]
    return pl.pallas_call(
        body, name=name, grid=(m_dim // tm, n_dim // tn, nk), in_specs=in_specs, out_specs=o_spec,
        out_shape=o_shape, scratch_shapes=[pltpu.VMEM((tm, tn), f32)],
        compiler_params=_params(("arbitrary", "arbitrary", "arbitrary")),
    )(*args)


def _mm_tn(a, b, *, out_dtype, tm, mb, tn, nb, tk, name, affine=None, out_blocked=False, out_scale=None,
           pair=False, host=None):
    k_dim, m_dim = a.shape
    multi_b = isinstance(b, (list, tuple))
    b_list = list(b) if multi_b else [b]
    n_dim = nb * tn if multi_b else b.shape[1]
    assert m_dim % (mb * tm) == 0 and n_dim % (nb * tn) == 0 and k_dim % tk == 0, (name, m_dim, n_dim, k_dim)
    nk = k_dim // tk
    grid = (m_dim // (mb * tm), n_dim // (nb * tn), nk)
    if pair:
        assert mb * nb == 4 and grid[0] * grid[1] == 2 and out_dtype == bf16, name

    def body(*refs):
        if pair:
            refs, (acc, send_buf, recv_buf, keep, send_sems, recv_sems) = refs[:-6], refs[-6:]
        else:
            refs, acc = refs[:-1], refs[-1]
        a_ref, o_ref = refs[0], refs[-1]
        if affine is not None:
            g_ref, s_ref = refs[1:3]
        b_refs = refs[3 if affine is not None else 1:-1]
        k = pl.program_id(2)

        @pl.when(k == 0)
        def _():
            acc[...] = jnp.zeros_like(acc)

        av = a_ref[...]
        if affine is not None:
            av = av * g_ref[...] + s_ref[...]
        av = av.astype(bf16)
        if multi_b:
            pieces = [r[...].astype(bf16) for r in b_refs]
        else:
            bv = b_refs[0][...].astype(bf16)
            pieces = [bv[:, jn * tn:(jn + 1) * tn] for jn in range(nb)]
        for im in range(mb):
            a_t = av[:, im * tm:(im + 1) * tm].T
            for jn in range(nb):
                acc[im * nb + jn] += jnp.dot(a_t, pieces[jn], preferred_element_type=f32)

        def scaled(v):
            return v if out_scale is None else v * out_scale

        @pl.when(k == nk - 1)
        def _():
            if pair:
                x, y, c = lax.axis_index("x"), lax.axis_index("y"), lax.axis_index("c")
                window = pl.program_id(0) + pl.program_id(1)

                def swap(w, cc):
                    return pltpu.make_async_remote_copy(
                        src_ref=send_buf.at[w, cc], dst_ref=recv_buf.at[w, cc],
                        send_sem=send_sems.at[2 * w + cc], recv_sem=recv_sems.at[2 * w + cc],
                        device_id=(x, y, 1 - c), device_id_type=MESH_T)

                for w in range(2):
                    @pl.when(window == w)
                    def _(w=w):
                        for cc in range(2):
                            send_buf[w, cc] = scaled(acc[2 * cc + 1 - c]).astype(bf16)
                            swap(w, cc).start()
                            if w == 0:
                                keep[cc] = scaled(acc[2 * cc + c])

                @pl.when(window == 1)
                def _():
                    for w in range(2):
                        for cc in range(2):
                            swap(w, cc).wait_recv()
                            mine = keep[cc] if w == 0 else scaled(acc[2 * cc + c])
                            o_ref[2 * w + cc] = (mine + recv_buf[w, cc].astype(f32)).astype(bf16)
                    for w in range(2):
                        for cc in range(2):
                            swap(w, cc).wait_send()
                return
            for im in range(mb):
                for jn in range(nb):
                    res = scaled(acc[im * nb + jn])
                    if out_blocked:
                        o_ref[jn, im * tm:(im + 1) * tm, :] = res.astype(out_dtype)
                    else:
                        o_ref[im * tm:(im + 1) * tm, jn * tn:(jn + 1) * tn] = res.astype(out_dtype)

    a_spec = pl.BlockSpec((tk, mb * tm), lambda i, j, k: (k, i))
    aff_spec = pl.BlockSpec((1, mb * tm), lambda i, j, k: (0, i))
    if multi_b:
        b_specs = [pl.BlockSpec((tk, tn), lambda i, j, k: (k, 0))] * nb
    else:
        b_specs = [pl.BlockSpec((tk, nb * tn), lambda i, j, k: (k, j))]
    scratch = [pltpu.VMEM((mb * nb, tm, tn), f32)]
    if pair:
        o_spec = pl.BlockSpec((4, tm, tn), lambda i, j, k: (0, 0, 0))
        o_shape = jax.ShapeDtypeStruct((4, tm, tn), out_dtype)
        scratch += [pltpu.VMEM((2, 2, tm, tn), bf16), pltpu.VMEM((2, 2, tm, tn), bf16), pltpu.VMEM((2, tm, tn), f32),
                    pltpu.SemaphoreType.DMA((4,)), pltpu.SemaphoreType.DMA((4,))]
    elif out_blocked:
        o_spec = pl.BlockSpec((nb, mb * tm, tn), lambda i, j, k: (j, i, 0))
        o_shape = jax.ShapeDtypeStruct((n_dim // tn, m_dim, tn), out_dtype)
    else:
        o_spec = pl.BlockSpec((mb * tm, nb * tn), lambda i, j, k: (i, j))
        o_shape = jax.ShapeDtypeStruct((m_dim, n_dim), out_dtype)
    in_specs = [a_spec] + ([aff_spec, aff_spec] if affine is not None else []) + b_specs
    args = [a] + (list(affine) if affine is not None else []) + b_list
    res = _hosted_call(
        host, body, name=name, grid=grid, in_specs=in_specs, out_specs=o_spec, out_shape=o_shape,
        scratch_shapes=scratch, compiler_params=_params(("arbitrary", "arbitrary", "arbitrary")),
    )(*args)
    return res[0] if host is None else res


def _in_proj(xhat, g, b, w_in, *, tm, name):
    t = xhat.shape[0]
    n_qkv, n_l = 3 * FOX_W, 2 * LRU_W

    def body(x_ref, g_ref, b_ref, w_ref, qkv_ref, zl_ref, zfg_ref):
        xb = (x_ref[...] * g_ref[...] + b_ref[...]).astype(bf16)
        qkv_ref[...] = jnp.dot(xb, w_ref[:, :n_qkv], preferred_element_type=f32).astype(bf16)
        zl_ref[...] = jnp.dot(xb, w_ref[:, n_qkv:n_qkv + n_l], preferred_element_type=f32)
        zfg_ref[...] = jnp.dot(xb, w_ref[:, n_qkv + n_l:], preferred_element_type=f32)

    row = pl.BlockSpec((1, D_MODEL), lambda i: (0, 0))
    return pl.pallas_call(
        body, name=name, grid=(t // tm,),
        in_specs=[pl.BlockSpec((tm, D_MODEL), lambda i: (i, 0)), row, row,
                  pl.BlockSpec(w_in.shape, lambda i: (0, 0))],
        out_specs=[pl.BlockSpec((tm, n_qkv), lambda i: (i, 0)), pl.BlockSpec((tm, n_l), lambda i: (i, 0)),
                   pl.BlockSpec((tm, LANES), lambda i: (i, 0))],
        out_shape=[jax.ShapeDtypeStruct((t, n_qkv), bf16), jax.ShapeDtypeStruct((t, n_l), f32),
                   jax.ShapeDtypeStruct((t, LANES), f32)],
        compiler_params=_params(("arbitrary",)),
    )(xhat, g, b, w_in)


def _mmln(pairs, *, tm, name, resid=None, resid_scale=1.0, epi=None, ln=None, n_out=D_MODEL):
    t = pairs[0][0].shape[0]
    n_pairs = len(pairs)
    n_resid = 0 if resid is None else len(resid) - 1

    def body(*refs):
        pos = 0
        val = None
        for p in range(n_pairs):
            a_ref, b_ref = refs[pos], refs[pos + 1]
            pos += 2
            av = a_ref[...].astype(bf16)
            bv = b_ref[...].astype(bf16)
            if pairs[p][6] == "nn":
                term = jnp.dot(av, bv, preferred_element_type=f32)
            else:
                term = lax.dot_general(av, bv, _NT, preferred_element_type=f32)
            val = term if val is None else val + term
        if resid is not None:
            if resid[0] == "plain":
                r = refs[pos][...]
            else:
                r = refs[pos][...] * refs[pos + 1][...] + refs[pos + 2][...]
            pos += n_resid
            val = val + resid_scale * r
        if epi is None:
            o_ref = refs[pos]
            o_ref[...] = val.astype(o_ref.dtype)
        elif epi == "ln_fwd":
            xo, rstd = _ln_fwd_tile(val)
            refs[pos][...] = xo
            refs[pos + 1][...] = rstd
        else:
            xh_ref, rs_ref, g_ref, dx_ref, gg_ref, gb_ref = refs[pos:pos + 6]
            dprev, gg, gb = _ln_bwd_tile(val, xh_ref[...], rs_ref[...], g_ref[...])
            dx_ref[...] = dprev
            i = pl.program_id(0)

            @pl.when(i == 0)
            def _():
                gg_ref[...] = gg
                gb_ref[...] = gb

            @pl.when(i > 0)
            def _():
                gg_ref[...] += gg
                gb_ref[...] += gb

    in_specs, args = [], []
    for (a, acb, aw, b, bcb, bw, mode) in pairs:
        in_specs.append(pl.BlockSpec((tm, aw), lambda i, acb=acb: (i, acb)))
        args.append(a)
        if mode == "nn":
            in_specs.append(pl.BlockSpec((aw, n_out), lambda i, bcb=bcb: (bcb, 0)))
        else:
            in_specs.append(pl.BlockSpec((n_out, bw), lambda i, bcb=bcb: (0, bcb)))
        args.append(b)
    tok = pl.BlockSpec((tm, n_out), lambda i: (i, 0))
    row = pl.BlockSpec((1, n_out), lambda i: (0, 0))
    col = pl.BlockSpec((tm, 1), lambda i: (i, 0))
    if resid is not None:
        in_specs += [tok] if resid[0] == "plain" else [tok, row, row]
        args += list(resid[1:])
    if epi is None:
        out_specs, out_shape = tok, jax.ShapeDtypeStruct((t, n_out), f32)
    elif epi == "ln_fwd":
        out_specs = [tok, col]
        out_shape = [jax.ShapeDtypeStruct((t, n_out), f32), jax.ShapeDtypeStruct((t, 1), f32)]
    else:
        in_specs += [tok, col, row]
        args += list(ln)
        out_specs = [tok, row, row]
        out_shape = [jax.ShapeDtypeStruct((t, n_out), f32)] + [jax.ShapeDtypeStruct((1, n_out), f32)] * 2
    return pl.pallas_call(
        body, name=name, grid=(t // tm,), in_specs=in_specs, out_specs=out_specs, out_shape=out_shape,
        compiler_params=_params(("arbitrary",)),
    )(*args)


CUM_TILE = 512


def _tri(n, lower):
    r = lax.broadcasted_iota(jnp.int32, (n, n), 0)
    c = lax.broadcasted_iota(jnp.int32, (n, n), 1)
    return jnp.where((r >= c) if lower else (r <= c), 1.0, 0.0).astype(f32)


def _cum_fwd(zfg, bfg, *, name):
    t = zfg.shape[0]

    def body(z_ref, b_ref, o_ref, carry):
        @pl.when(pl.program_id(0) == 0)
        def _():
            carry[...] = jnp.zeros_like(carry)

        ls = -_softplus(-(z_ref[...] + b_ref[...]))
        c = jnp.dot(_tri(CUM_TILE, True), ls, preferred_element_type=f32,
                    precision=lax.Precision.HIGHEST) + carry[...]
        o_ref[...] = c
        carry[...] = c[CUM_TILE - 1:CUM_TILE, :]

    blk = pl.BlockSpec((CUM_TILE, LANES), lambda i: (i, 0))
    return pl.pallas_call(
        body, name=name, grid=(t // CUM_TILE,),
        in_specs=[blk, pl.BlockSpec((1, LANES), lambda i: (0, 0))], out_specs=blk,
        out_shape=jax.ShapeDtypeStruct((t, LANES), f32), scratch_shapes=[pltpu.VMEM((1, LANES), f32)],
        compiler_params=_params(("arbitrary",)),
    )(zfg, bfg)


def _cum_bwd(dcum_q, dcum_k, zfg, bfg, *, name):
    t = zfg.shape[0]
    n = t // CUM_TILE

    def body(d_ref, d2_ref, z_ref, b_ref, o_ref, s_ref, carry):
        i = pl.program_id(0)

        @pl.when(i == 0)
        def _():
            carry[...] = jnp.zeros_like(carry)

        dls = jnp.dot(_tri(CUM_TILE, False), d_ref[...] + d2_ref[...], preferred_element_type=f32,
                      precision=lax.Precision.HIGHEST) + carry[...]
        carry[...] = dls[0:1, :]
        lane = lax.broadcasted_iota(jnp.int32, (CUM_TILE, LANES), 1)
        dfg = jnp.where(lane < HEADS, dls * _sigmoid(-(z_ref[...] + b_ref[...])), 0.0)
        o_ref[...] = dfg
        tot = jnp.sum(dfg, axis=0, keepdims=True)

        @pl.when(i == 0)
        def _():
            s_ref[...] = tot

        @pl.when(i > 0)
        def _():
            s_ref[...] += tot

    blk = pl.BlockSpec((CUM_TILE, LANES), lambda i: (n - 1 - i, 0))
    row = pl.BlockSpec((1, LANES), lambda i: (0, 0))
    return pl.pallas_call(
        body, name=name, grid=(n,), in_specs=[blk, blk, blk, row], out_specs=[blk, row],
        out_shape=[jax.ShapeDtypeStruct((t, LANES), f32), jax.ShapeDtypeStruct((1, LANES), f32)],
        scratch_shapes=[pltpu.VMEM((1, LANES), f32)],
        compiler_params=_params(("arbitrary",)),
    )(dcum_q, dcum_k, zfg, bfg)


ATT_TILE = 512


def _causal(i, j, transposed):
    r = lax.broadcasted_iota(jnp.int32, (ATT_TILE, ATT_TILE), 0)
    c = lax.broadcasted_iota(jnp.int32, (ATT_TILE, ATT_TILE), 1)
    if transposed:
        return (c + i * ATT_TILE) >= (r + j * ATT_TILE)
    return (r + i * ATT_TILE) >= (c + j * ATT_TILE)


ATT_W = HEADS * LANES


def _data_lane(h):
    return HEAD_D * (h % 2)


def _extra_lane(h):
    return HEAD_D - _data_lane(h)


def _split3(x):
    hi = x.astype(bf16)
    rest = x - hi.astype(f32)
    mid = rest.astype(bf16)
    lo = (rest - mid.astype(f32)).astype(bf16)
    return hi, mid, lo


def _three_pieces(x):
    hi, mid, lo = (p.astype(f32) for p in _split3(x))
    return (hi + pltpu.roll(mid, HEADS, axis=1) + pltpu.roll(lo, 2 * HEADS, axis=1)).astype(bf16)


def _move(h, first):
    r = lax.broadcasted_iota(jnp.int32, (LANES, LANES), 0)
    c = lax.broadcasted_iota(jnp.int32, (LANES, LANES), 1)
    hit = functools.reduce(jnp.logical_or, [jnp.logical_and(r == HEADS * q + h, c == first + q) for q in range(3)])
    return jnp.where(hit, 1.0, 0.0).astype(bf16)


def _ones_from(first, rows):
    lane = lax.broadcasted_iota(jnp.int32, (rows, LANES), 1)
    return jnp.where(jnp.logical_and(lane >= first, lane < first + 3), 1.0, 0.0)


def _own_lanes(h, rows):
    lane = lax.broadcasted_iota(jnp.int32, (rows, LANES), 1)
    return (lane < HEAD_D) if h % 2 == 0 else (lane >= HEAD_D)


def _head_values(x):
    lane = lax.broadcasted_iota(jnp.int32, x.shape, 1)
    return jnp.where(lane < HEADS, x, 0.0)


def _attn_prep_fwd(qkv, cum, *, tm, name):
    t = qkv.shape[0]

    def body(q_ref, k_ref, v_ref, c_ref, qa_ref, ka_ref, va_ref):
        c3 = _three_pieces(_head_values(c_ref[...]))
        ones = jnp.ones((tm, LANES), bf16)
        for h in range(HEADS):
            pair = slice(LANES * (h // 2), LANES * (h // 2 + 1))
            hs = slice(LANES * h, LANES * (h + 1))
            base, own = _extra_lane(h), _own_lanes(h, tm)
            eq = jnp.dot(c3, _move(h, base), preferred_element_type=f32) + _ones_from(base + 3, tm)
            ek = _ones_from(base, tm) - jnp.dot(c3, _move(h, base + 3), preferred_element_type=f32)
            qa_ref[:, hs] = jnp.where(own, q_ref[:, pair] * ATT_SCALE, eq.astype(bf16))
            ka_ref[:, hs] = jnp.where(own, k_ref[:, pair], ek.astype(bf16))
            va_ref[:, hs] = jnp.where(own, v_ref[:, pair], ones)

    wide = pl.BlockSpec((tm, ATT_W), lambda i: (i, 0))
    out = jax.ShapeDtypeStruct((t, ATT_W), bf16)
    return pl.pallas_call(
        body, name=name, grid=(t // tm,),
        in_specs=[pl.BlockSpec((tm, FOX_W), lambda i: (i, 0)), pl.BlockSpec((tm, FOX_W), lambda i: (i, 1)),
                  pl.BlockSpec((tm, FOX_W), lambda i: (i, 2)), pl.BlockSpec((tm, LANES), lambda i: (i, 0))],
        out_specs=[wide] * 3, out_shape=[out] * 3, compiler_params=_params(("arbitrary",)),
    )(qkv, qkv, qkv, cum)


def _attn_prep_bwd(qkv, cum, lse, dmix, o, *, tm, name):
    t = qkv.shape[0]

    def body(q_ref, c_ref, l_ref, do_ref, o_ref, qa_ref, da_ref):
        b3 = _three_pieces(_head_values(c_ref[...] - l_ref[...]))
        r = lax.broadcasted_iota(jnp.int32, (FOX_W, LANES), 0)
        c = lax.broadcasted_iota(jnp.int32, (FOX_W, LANES), 1)
        per_head = jnp.where(r // HEAD_D == c, 1.0, 0.0).astype(bf16)
        delta = sum(jnp.dot(p, per_head, preferred_element_type=f32) for p in _split3(do_ref[...] * o_ref[...]))
        d3 = _three_pieces(delta)
        for h in range(HEADS):
            pair = slice(LANES * (h // 2), LANES * (h // 2 + 1))
            hs = slice(LANES * h, LANES * (h + 1))
            base, own = _extra_lane(h), _own_lanes(h, tm)
            eq = jnp.dot(b3, _move(h, base), preferred_element_type=f32) + _ones_from(base + 3, tm)
            ed = -jnp.dot(d3, _move(h, base), preferred_element_type=f32)
            qa_ref[:, hs] = jnp.where(own, q_ref[:, pair] * ATT_SCALE, eq.astype(bf16))
            da_ref[:, hs] = jnp.where(own, do_ref[:, pair].astype(bf16), ed.astype(bf16))

    wide = pl.BlockSpec((tm, ATT_W), lambda i: (i, 0))
    half = pl.BlockSpec((tm, FOX_W), lambda i: (i, 0))
    col = pl.BlockSpec((tm, LANES), lambda i: (i, 0))
    out = jax.ShapeDtypeStruct((t, ATT_W), bf16)
    return pl.pallas_call(
        body, name=name, grid=(t // tm,), in_specs=[half, col, col, half, half],
        out_specs=[wide] * 2, out_shape=[out] * 2, compiler_params=_params(("arbitrary",)),
    )(qkv, cum, lse, dmix, o)


def _attn_fwd2(q_aug, k_aug, v_aug, *, name, host=None):
    t = q_aug.shape[0]
    n = t // ATT_TILE
    tq = ATT_TILE

    def body(q_ref, k_ref, v_ref, o_ref, lse_ref, acc, m_s):
        i = pl.program_id(0)
        j = pl.program_id(1)

        @pl.when(j == 0)
        def _():
            acc[...] = jnp.zeros_like(acc)
            m_s[...] = jnp.full_like(m_s, NEG_BIG)

        def block(masked):
            mask = _causal(i, j, False) if masked else None
            for h in range(HEADS):
                hs = slice(LANES * h, LANES * (h + 1))
                s = lax.dot_general(q_ref[:, hs], k_ref[:, hs], _NT, preferred_element_type=f32)
                if masked:
                    s = jnp.where(mask, s, NEG_BIG)
                blocks = [s[:, LANES * b:LANES * (b + 1)] for b in range(tq // LANES)]
                m_old = m_s[h]
                m_new = jnp.maximum(m_old, jnp.broadcast_to(
                    jnp.max(functools.reduce(jnp.maximum, blocks), axis=-1, keepdims=True), (tq, LANES)))
                p = jnp.concatenate([jnp.exp(b - m_new) for b in blocks], axis=1).astype(bf16)
                acc[h] = jnp.exp(m_old - m_new) * acc[h] + jnp.dot(p, v_ref[:, hs], preferred_element_type=f32)
                m_s[h] = m_new

        @pl.when(j < i)
        def _():
            block(False)

        @pl.when(j == i)
        def _():
            block(True)
            lse_ref[...] = jnp.zeros_like(lse_ref)
            for h in range(HEADS):
                a = acc[h]
                l = a[:, _extra_lane(h):_extra_lane(h) + 1]
                o_ref[:, HEAD_D * h:HEAD_D * (h + 1)] = a[:, _data_lane(h):_data_lane(h) + HEAD_D] / l
                lse_ref[:, h:h + 1] = m_s[h][:, 0:1] + jnp.log(l)

    kv = pl.BlockSpec((tq, ATT_W), lambda i, j: (jnp.minimum(i, j), 0))
    return _hosted_call(
        host, body, name=name, grid=(n, n),
        in_specs=[pl.BlockSpec((tq, ATT_W), lambda i, j: (i, 0)), kv, kv],
        out_specs=[pl.BlockSpec((tq, FOX_W), lambda i, j: (i, 0)), pl.BlockSpec((tq, LANES), lambda i, j: (i, 0))],
        out_shape=[jax.ShapeDtypeStruct((t, FOX_W), f32), jax.ShapeDtypeStruct((t, LANES), f32)],
        scratch_shapes=[pltpu.VMEM((HEADS, tq, LANES), f32), pltpu.VMEM((HEADS, tq, LANES), f32)],
        compiler_params=_params(("arbitrary", "arbitrary")),
    )(q_aug, k_aug, v_aug)


def _attn_bwd(qb_aug, k_aug, v_aug, do_aug, *, name, host=None):
    t = qb_aug.shape[0]
    n = t // ATT_TILE
    tk = ATT_TILE

    def body(q_ref, k_ref, v_ref, do_ref, dq_ref, dcq_ref, dk_ref, dv_ref, dck_ref, dk_acc, dv_acc, dq_all):
        j = pl.program_id(0)
        i = pl.program_id(1)

        @pl.when(jnp.logical_and(i == 0, j == 0))
        def _():
            dq_all[...] = jnp.zeros_like(dq_all)

        @pl.when(i == 0)
        def _():
            dk_acc[...] = jnp.zeros_like(dk_acc)
            dv_acc[...] = jnp.zeros_like(dv_acc)

        def block(masked):
            mask = _causal(i, j, True) if masked else None
            for h in range(HEADS):
                hs = slice(LANES * h, LANES * (h + 1))
                qh = q_ref[:, hs]
                doh = do_ref[:, hs]
                kh = k_ref[:, hs]
                s_t = lax.dot_general(kh, qh, _NT, preferred_element_type=f32)
                if masked:
                    s_t = jnp.where(mask, s_t, NEG_BIG)
                p_t = jnp.exp(s_t)
                dv_acc[h] += jnp.dot(p_t.astype(bf16), doh, preferred_element_type=f32)
                dp_t = lax.dot_general(v_ref[:, hs], doh, _NT, preferred_element_type=f32)
                ds_t = (p_t * dp_t).astype(bf16)
                dk_acc[h] += jnp.dot(ds_t, qh, preferred_element_type=f32)
                dq_all[i, h] += lax.dot_general(ds_t, kh, _TN, preferred_element_type=f32)

        @pl.when(i > j)
        def _():
            block(False)

        @pl.when(i == j)
        def _():
            block(True)
            dcq_ref[...] = jnp.zeros_like(dcq_ref)
            for h in range(HEADS):
                a = dq_all[j, h]
                dq_ref[:, HEAD_D * h:HEAD_D * (h + 1)] = (
                    a[:, _data_lane(h):_data_lane(h) + HEAD_D] * ATT_SCALE).astype(bf16)
                dcq_ref[:, h:h + 1] = a[:, _extra_lane(h):_extra_lane(h) + 1]

        @pl.when(i == n - 1)
        def _():
            dck_ref[...] = jnp.zeros_like(dck_ref)
            for h in range(HEADS):
                a = dk_acc[h]
                cols = slice(_data_lane(h), _data_lane(h) + HEAD_D)
                dk_ref[:, HEAD_D * h:HEAD_D * (h + 1)] = a[:, cols].astype(bf16)
                dv_ref[:, HEAD_D * h:HEAD_D * (h + 1)] = dv_acc[h][:, cols].astype(bf16)
                dck_ref[:, h:h + 1] = -a[:, _extra_lane(h) + 3:_extra_lane(h) + 4]

    own = pl.BlockSpec((tk, ATT_W), lambda j, i: (j, 0))
    qs = pl.BlockSpec((tk, ATT_W), lambda j, i: (jnp.maximum(i, j), 0))
    half = pl.BlockSpec((tk, FOX_W), lambda j, i: (j, 0))
    col = pl.BlockSpec((tk, LANES), lambda j, i: (j, 0))
    return _hosted_call(
        host, body, name=name, grid=(n, n), in_specs=[qs, own, own, qs],
        out_specs=[half, col, half, half, col],
        out_shape=[jax.ShapeDtypeStruct((t, FOX_W), bf16), jax.ShapeDtypeStruct((t, LANES), f32),
                   jax.ShapeDtypeStruct((t, FOX_W), bf16), jax.ShapeDtypeStruct((t, FOX_W), bf16),
                   jax.ShapeDtypeStruct((t, LANES), f32)],
        scratch_shapes=[pltpu.VMEM((HEADS, tk, LANES), f32), pltpu.VMEM((HEADS, tk, LANES), f32),
                        pltpu.VMEM((n, HEADS, tk, LANES), f32)],
        compiler_params=_params(("arbitrary", "arbitrary")),
    )(qb_aug, k_aug, v_aug, do_aug)


LRU_CHUNK = 64
LRU_G = 256
SUB = 8


def _row_ids(n):
    return lax.broadcasted_iota(jnp.int32, (n, LRU_G), 0)


def _shift_rows_down(ext, s):
    return pltpu.roll(ext, s, axis=0)[SUB:, :]


def _shift_rows_up(ext, s, n):
    return pltpu.roll(ext, ext.shape[0] - s, axis=0)[:n, :]


def _lru_gates(u, wa_ref, ba_ref, wx_ref, bx_ref, sp):
    ub = u.astype(bf16)
    r = _sigmoid(jnp.dot(ub, wa_ref[...], preferred_element_type=f32) + ba_ref[...])
    gi = _sigmoid(jnp.dot(ub, wx_ref[...], preferred_element_type=f32) + bx_ref[...])
    log_a = -LRU_C * r * sp
    a = jnp.exp(log_a)
    s = jnp.sqrt(_one_minus_exp(2.0 * log_a))
    return r, gi, a, s


def _conv_window(lx_ref, r0, ci):
    cur = lx_ref[pl.ds(r0, LRU_CHUNK), :]
    p0 = pl.multiple_of(jnp.maximum(r0 - SUB, 0), SUB)
    prev = jnp.where(ci > 0, lx_ref[pl.ds(p0, SUB), :], 0.0)
    return cur, jnp.concatenate([prev, cur], axis=0)


def _lru_fwd(zl, conv_w, conv_b, wa, ba, wx, bx, lam, *, name, host=None):
    t = zl.shape[0]
    n_chunk = t // LRU_CHUNK

    def body(lx_ref, lg_ref, cw_ref, cb_ref, wa_ref, ba_ref, wx_ref, bx_ref, lam_ref, u_ref, h_ref, y_ref):
        sp = _softplus(-lam_ref[...])
        rows = _row_ids(SUB)

        def chunk(ci, hc):
            r0 = pl.multiple_of(ci * LRU_CHUNK, LRU_CHUNK)
            cur, ext = _conv_window(lx_ref, r0, ci)
            u = cb_ref[...] + cw_ref[3:4, :] * cur
            for k in range(3):
                u = u + cw_ref[k:k + 1, :] * _shift_rows_down(ext, 3 - k)
            r, gi, a, s = _lru_gates(u, wa_ref, ba_ref, wx_ref, bx_ref, sp)
            b = s * (gi * u)
            tiles = []
            for q in range(LRU_CHUNK // SUB):
                ta = a[SUB * q:SUB * (q + 1), :]
                tb = b[SUB * q:SUB * (q + 1), :]
                for d in (1, 2, 4):
                    a_sh = jnp.where(rows >= d, pltpu.roll(ta, d, axis=0), 1.0)
                    b_sh = jnp.where(rows >= d, pltpu.roll(tb, d, axis=0), 0.0)
                    tb = ta * b_sh + tb
                    ta = ta * a_sh
                hq = tb + ta * hc
                hc = hq[SUB - 1:SUB, :]
                tiles.append(hq)
            h = jnp.concatenate(tiles, axis=0)
            u_ref[pl.ds(r0, LRU_CHUNK), :] = u
            h_ref[pl.ds(r0, LRU_CHUNK), :] = h
            gel, _ = _gelu_and_grad(lg_ref[pl.ds(r0, LRU_CHUNK), :])
            y_ref[pl.ds(r0, LRU_CHUNK), :] = gel * h
            return hc

        lax.fori_loop(0, n_chunk, chunk, jnp.zeros((1, LRU_G), f32))

    seq = lambda cb: pl.BlockSpec((t, LRU_G), lambda c, cb=cb: (0, c + cb))
    rowc = pl.BlockSpec((1, LRU_G), lambda c: (0, c))
    diag = pl.BlockSpec((LRU_G, LRU_G), lambda c: (c, c))
    out = jax.ShapeDtypeStruct((t, LRU_W), f32)
    return _hosted_call(
        host, body, name=name, grid=(LRU_W // LRU_G,),
        in_specs=[seq(0), seq(LRU_W // LRU_G), pl.BlockSpec((4, LRU_G), lambda c: (0, c)),
                  rowc, diag, rowc, diag, rowc, rowc],
        out_specs=[seq(0)] * 3, out_shape=[out] * 3,
        compiler_params=_params(("arbitrary",)),
    )(zl, zl, conv_w, conv_b, wa, ba, wx, bx, lam)


def _lru_bwd(dmix, zl, u_all, h_all, conv_w, wa, ba, wx, bx, lam, *, name, host=None):
    t = zl.shape[0]
    n_chunk = t // LRU_CHUNK

    def body(dy_ref, lx_ref, lg_ref, u_ref, h_ref, cw_ref, wa_ref, ba_ref, wx_ref, bx_ref, lam_ref,
             dlx_ref, dlg_ref, dcw_ref, dcb_ref, dba_ref, dbx_ref, dlam_ref, dwa_ref, dwx_ref, dpr_s, dpx_s):
        lam_v = lam_ref[...]
        sp = _softplus(-lam_v)
        rows = _row_ids(SUB)
        rows_c = _row_ids(LRU_CHUNK)
        zero_row = jnp.zeros((1, LRU_G), f32)

        def chunk(step, carry):
            dh_c, a_next0, du_next, dsp, dba, dbx, dcb, dw0, dw1, dw2, dw3 = carry
            ci = n_chunk - 1 - step
            r0 = pl.multiple_of(ci * LRU_CHUNK, LRU_CHUNK)
            sl = pl.ds(r0, LRU_CHUNK)
            u = u_ref[sl, :]
            r, gi, a, s = _lru_gates(u, wa_ref, ba_ref, wx_ref, bx_ref, sp)
            h = h_ref[sl, :]
            p0 = pl.multiple_of(jnp.maximum(r0 - SUB, 0), SUB)
            h_before = jnp.where(ci > 0, h_ref[pl.ds(p0, SUB), :], 0.0)[SUB - 1:SUB, :]
            h_prev = jnp.where(rows_c == 0, h_before, pltpu.roll(h, 1, axis=0))
            gel, dgel = _gelu_and_grad(lg_ref[sl, :])
            dy = dy_ref[sl, :]
            dlg_ref[sl, :] = (dy * h * dgel).astype(bf16)
            g_in = dy * gel
            a_next = jnp.where(rows_c == LRU_CHUNK - 1, a_next0, pltpu.roll(a, LRU_CHUNK - 1, axis=0))
            tiles = [None] * (LRU_CHUNK // SUB)
            for q in reversed(range(LRU_CHUNK // SUB)):
                ta = a_next[SUB * q:SUB * (q + 1), :]
                tb = g_in[SUB * q:SUB * (q + 1), :]
                for d in (1, 2, 4):
                    a_sh = jnp.where(rows < SUB - d, pltpu.roll(ta, SUB - d, axis=0), 1.0)
                    b_sh = jnp.where(rows < SUB - d, pltpu.roll(tb, SUB - d, axis=0), 0.0)
                    tb = ta * b_sh + tb
                    ta = ta * a_sh
                dhq = tb + ta * dh_c
                dh_c = dhq[0:1, :]
                tiles[q] = dhq
            dh = jnp.concatenate(tiles, axis=0)
            da = dh * h_prev
            ds = dh * gi * u
            dgi = dh * s * u
            du = dh * s * gi
            dlog_a = da * a - ds * (a * a) / s
            dr = dlog_a * (-LRU_C * sp)
            dsp = dsp + jnp.sum(dlog_a * (-LRU_C * r), axis=0, keepdims=True)
            dpr = dr * r * (1.0 - r)
            dpx = dgi * gi * (1.0 - gi)
            dprb = dpr.astype(bf16)
            dpxb = dpx.astype(bf16)
            dpr_s[sl, :] = dprb
            dpx_s[sl, :] = dpxb
            du = du + (lax.dot_general(dprb, wa_ref[...], _NT, preferred_element_type=f32)
                       + lax.dot_general(dpxb, wx_ref[...], _NT, preferred_element_type=f32))
            dba = dba + jnp.sum(dpr, axis=0, keepdims=True)
            dbx = dbx + jnp.sum(dpx, axis=0, keepdims=True)
            dcb = dcb + jnp.sum(du, axis=0, keepdims=True)
            du_ext = jnp.concatenate([du, du_next], axis=0)
            dlx = cw_ref[3:4, :] * du
            for k in range(3):
                dlx = dlx + cw_ref[k:k + 1, :] * _shift_rows_up(du_ext, 3 - k, LRU_CHUNK)
            dlx_ref[sl, :] = dlx.astype(bf16)
            cur, ext = _conv_window(lx_ref, r0, ci)
            dws = [dw0, dw1, dw2, dw3 + jnp.sum(du * cur, axis=0, keepdims=True)]
            for k in range(3):
                dws[k] = dws[k] + jnp.sum(du * _shift_rows_down(ext, 3 - k), axis=0, keepdims=True)
            return (dh_c, a[0:1, :], du[0:SUB, :], dsp, dba, dbx, dcb, dws[0], dws[1], dws[2], dws[3])

        init = (zero_row, zero_row, jnp.zeros((SUB, LRU_G), f32)) + (zero_row,) * 8
        out = lax.fori_loop(0, n_chunk, chunk, init)
        _, _, _, dsp, dba, dbx, dcb, dw0, dw1, dw2, dw3 = out
        dlam_ref[...] = dsp * (-_sigmoid(-lam_v))
        dba_ref[...] = dba
        dbx_ref[...] = dbx
        dcb_ref[...] = dcb
        dcw_ref[...] = jnp.concatenate([dw0, dw1, dw2, dw3], axis=0)
        ub = u_ref[...].astype(bf16)
        dwa_ref[...] = lax.dot_general(ub, dpr_s[...], _TN, preferred_element_type=f32)
        dwx_ref[...] = lax.dot_general(ub, dpx_s[...], _TN, preferred_element_type=f32)

    seq = lambda cb: pl.BlockSpec((t, LRU_G), lambda c, cb=cb: (0, c + cb))
    rowc = pl.BlockSpec((1, LRU_G), lambda c: (0, c))
    diag = pl.BlockSpec((LRU_G, LRU_G), lambda c: (c, c))
    gate_out = pl.BlockSpec((None, LRU_G, LRU_G), lambda c: (c, 0, 0))
    row_shape = jax.ShapeDtypeStruct((1, LRU_W), f32)
    return _hosted_call(
        host, body, name=name, grid=(LRU_W // LRU_G,),
        in_specs=[seq(LRU_W // LRU_G), seq(0), seq(LRU_W // LRU_G), seq(0), seq(0),
                  pl.BlockSpec((4, LRU_G), lambda c: (0, c)),
                  diag, rowc, diag, rowc, rowc],
        out_specs=[seq(0), seq(0), pl.BlockSpec((4, LRU_G), lambda c: (0, c)), rowc, rowc, rowc, rowc,
                   gate_out, gate_out],
        out_shape=[jax.ShapeDtypeStruct((t, LRU_W), bf16)] * 2
        + [jax.ShapeDtypeStruct((4, LRU_W), f32)] + [row_shape] * 4
        + [jax.ShapeDtypeStruct((LRU_W // LRU_G, LRU_G, LRU_G), f32)] * 2,
        scratch_shapes=[pltpu.VMEM((t, LRU_G), bf16), pltpu.VMEM((t, LRU_G), bf16)],
        compiler_params=_params(("arbitrary",)),
    )(dmix, zl, zl, u_all, h_all, conv_w, wa, ba, wx, bx, lam)


def _pack_rows(a):
    flat = a.reshape(-1)
    rows = -(-flat.shape[0] // LANES)
    return jnp.pad(flat, (0, rows * LANES - flat.shape[0])).reshape(rows, LANES)


W_IN_PAD = 21 * LANES


def _w_in_join(blocks, *, name):
    tm = 256

    def body(b_ref, o_ref):
        o_ref[:, IN_COLS:] = jnp.zeros((tm, W_IN_PAD - IN_COLS), bf16)
        for q in range(N_DEV):
            o_ref[:, IN_SHARD * q:IN_SHARD * (q + 1)] = b_ref[q]

    return pl.pallas_call(
        body, name=name, grid=(D_MODEL // tm,),
        in_specs=[pl.BlockSpec((N_DEV, tm, IN_SHARD), lambda i: (0, i, 0))],
        out_specs=pl.BlockSpec((tm, W_IN_PAD), lambda i: (i, 0)),
        out_shape=jax.ShapeDtypeStruct((D_MODEL, W_IN_PAD), bf16), compiler_params=_params(("arbitrary",)),
    )(blocks)


def _w_in_split(main, fg, *, name):
    tm = 256
    n_main = main.shape[0]

    def body(m_ref, f_ref, o_ref):
        full = jnp.concatenate([m_ref[n] for n in range(n_main)] + [f_ref[...]], axis=1)
        for q in range(N_DEV):
            o_ref[q] = full[:, IN_SHARD * q:IN_SHARD * (q + 1)]

    return pl.pallas_call(
        body, name=name, grid=(D_MODEL // tm,),
        in_specs=[pl.BlockSpec((n_main, tm, 512), lambda i: (0, i, 0)), pl.BlockSpec((tm, LANES), lambda i: (i, 0))],
        out_specs=pl.BlockSpec((N_DEV, tm, IN_SHARD), lambda i: (0, i, 0)),
        out_shape=jax.ShapeDtypeStruct((N_DEV, D_MODEL, IN_SHARD), bf16), compiler_params=_params(("arbitrary",)),
    )(main, fg)


def _block_diag(w):
    eye = jnp.eye(HEADS, dtype=w.dtype)
    return jnp.einsum("hij,hk->hikj", w, eye).reshape(LRU_W, LRU_W)


def _diag_blocks(dw):
    per = dw.shape[1] // HEAD_D
    blocks = [dw[:, HEAD_D * b:HEAD_D * (b + 1), HEAD_D * b:HEAD_D * (b + 1)] for b in range(per)]
    return jnp.stack(blocks, axis=1).reshape(HEADS, HEAD_D, HEAD_D)


def _local_step(x, target, sent, small, *, tm=512, tm_ffn=1024):
    ln1 = (small["ln1_g"], small["ln1_b"])
    ln2 = (small["ln2_g"], small["ln2_b"])
    ln3 = (small["ln3_g"], small["ln3_b"])

    xh1, rs1, hg1, hu1, wg1, wu1, wd1, w_in_g, w_out_g, conv_w_g = _ffn1_fwd_gathering(
        x, (sent["ffn1_w_gate"], sent["ffn1_w_up"], sent["ffn1_w_down"]),
        _Exchange([sent["w_in"], sent["w_out"], sent["conv_w"]], gather=True), tm=tm_ffn, name="ffn1_fwd")
    w_in = _w_in_join(w_in_g, name="w_in_join")
    w_out = w_out_g.reshape(D_MODEL, D_MODEL)
    conv_w = conv_w_g.transpose(1, 0, 2).reshape(4, LRU_W)
    qkv, zl, zfg = _in_proj(xh1, ln1[0], ln1[1], w_in, tm=tm_ffn, name="in_proj")
    bfg = jnp.pad(small["b_forget"], ((0, 0), (0, LANES - HEADS)))
    cum = _cum_fwd(zfg, bfg, name="cum_fwd")
    q_aug, k_aug, v_aug = _attn_prep_fwd(qkv, cum, tm=tm_ffn, name="attn_prep_fwd")
    o, lse, wg2, wu2 = _attn_fwd2(
        q_aug, k_aug, v_aug, name="attn_fwd",
        host=_Hosts(_Exchange([sent["ffn2_w_gate"]], gather=True),
                    _Exchange([sent["ffn2_w_up"]], gather=True, direct=True)))
    wa_bd = _block_diag(small["rg_wa"]).astype(bf16)
    wx_bd = _block_diag(small["rg_wx"]).astype(bf16)
    ba = small["rg_ba"].reshape(1, LRU_W)
    bx = small["rg_bx"].reshape(1, LRU_W)
    u, h, lru, wd2 = _lru_fwd(zl, conv_w, small["conv_b"], wa_bd, ba, wx_bd, bx, small["lru_lambda"],
                              name="lru_fwd", host=_Exchange([sent["ffn2_w_down"]], gather=True))
    xh2, rs2 = _mmln([(o, 0, FOX_W, w_out, 0, D_MODEL, "nn"), (lru, 0, LRU_W, w_out, 1, D_MODEL, "nn")],
                     tm=tm_ffn, name="mix_fwd", resid=("affine", xh1) + ln1, resid_scale=ALPHA, epi="ln_fwd")
    dpre3, sq_rows, g_ln3g, g_ln3b, hg2, hu2 = _ffn_fwd_loss(
        xh2, ln2[0], ln2[1], wg2, wu2, wd2, ln3[0], ln3[1], target, tm=tm_ffn, name="ffn2_fwd_loss")

    dpre2, g_ln2g, g_ln2b, dhg2, dhu2, a2 = _ffn_bwd(dpre3, hg2, hu2, wg2, wu2, wd2,
                                                     (xh2, rs2, ln2[0]), tm=tm_ffn, name="ffn2_bwd")
    wgrad = dict(out_dtype=bf16, tm=D_MODEL, mb=1, tn=FF_TILE, nb=4, tk=512, pair=True)
    wdgrad = dict(out_dtype=bf16, tm=512, mb=4, tn=D_MODEL, nb=1, tk=512, out_scale=0.5, pair=True)
    between_chips = functools.partial(_Exchange, gather=False, chips=True)
    g_wg2 = _mm_tn(xh2, dhg2, name="g_wg2", affine=ln2, **wgrad)
    g_wu2 = _mm_tn(xh2, dhu2, name="g_wu2", affine=ln2, **wgrad)
    g_wd2 = _mm_tn(a2, dpre3, name="g_wd2", **wdgrad)

    dmix = _mmln([(dpre2, 0, D_MODEL, w_out, 0, D_MODEL, "nt")], tm=tm_ffn, name="dmix_bwd")
    g_wout_a = _mm(o, dpre2, mode="tn", out_dtype=bf16, tm=512, tn=D_MODEL, tk=512, name="g_wout_fox")
    g_wout_b = _mm(lru, dpre2, mode="tn", out_dtype=bf16, tm=512, tn=D_MODEL, tk=512, name="g_wout_lru")
    g_wout_blocked = jnp.concatenate([g_wout_a, g_wout_b], axis=0).reshape(N_DEV, D_MODEL // N_DEV, D_MODEL)
    dlx, dlg, g_cw, g_cb, g_ba, g_bx, g_lam, g_wa4, g_wx4, p_wg2, p_wout = _lru_bwd(
        dmix, zl, u, h, conv_w, wa_bd, ba, wx_bd, bx, small["lru_lambda"], name="lru_bwd",
        host=_Hosts(between_chips([g_wg2]), _Exchange([g_wout_blocked], gather=False)))
    qb_aug, do_aug = _attn_prep_bwd(qkv, cum, lse, dmix, o, tm=tm_ffn, name="attn_prep_bwd")
    dq, dcum_q, dk, dv, dcum_k, p_wu2, p_wd2 = _attn_bwd(qb_aug, k_aug, v_aug, do_aug, name="attn_bwd",
                                                         host=between_chips([g_wu2, g_wd2]))
    dfg, g_bf = _cum_bwd(dcum_q, dcum_k, zfg, bfg, name="cum_bwd")

    dz = [(dq, 0, 512), (dk, 1, 512), (dv, 2, 512), (dlx, 3, 512), (dlg, 4, 512), (dfg, 20, LANES)]
    dpre1, g_ln1g, g_ln1b = _mmln(
        [(arr, 0, w, w_in, cb, w, "nt") for (arr, cb, w) in dz],
        tm=tm, name="dx1_bwd", resid=("plain", dpre2), resid_scale=ALPHA, epi="ln_bwd", ln=(xh1, rs1, ln1[0]))
    g_win_main = _mm_tn(xh1, [arr for arr, _, _ in dz[:5]], out_dtype=bf16, tm=D_MODEL, mb=1, tn=512, nb=5, tk=512,
                        name="g_win", affine=ln1, out_blocked=True)
    g_win_fg = _mm(xh1, dfg, mode="tn", out_dtype=bf16, tm=D_MODEL, tn=LANES, tk=512, name="g_win_fg", affine=ln1)
    g_win_blocked = _w_in_split(g_win_main, g_win_fg, name="w_in_split")
    dhg1, dhu1, a1, p_win = _ffn_bwd_act(dpre1, hg1, hu1, wd1, tm=tm_ffn, name="ffn1_bwd_act",
                                         host=_Exchange([g_win_blocked], gather=False))
    small_g = {
        "ln1_g": g_ln1g, "ln1_b": g_ln1b, "b_forget": g_bf[:, :HEADS], "conv_w": g_cw, "conv_b": g_cb,
        "rg_wa": _diag_blocks(g_wa4), "rg_ba": g_ba.reshape(HEADS, HEAD_D),
        "rg_wx": _diag_blocks(g_wx4), "rg_bx": g_bx.reshape(HEADS, HEAD_D), "lru_lambda": g_lam,
        "ln2_g": g_ln2g, "ln2_b": g_ln2b, "ln3_g": g_ln3g, "ln3_b": g_ln3b,
    }
    small_g["loss"] = (0.5 / D_MODEL) * jnp.sum(sq_rows, keepdims=True)
    pieces = [_pack_rows(small_g[n]) for n in PACKED]
    packed = jnp.concatenate(pieces + [jnp.zeros((PACK_ROWS - sum(p.shape[0] for p in pieces), LANES), f32)])
    g_wg1, all_packed = _mm_tn(x, dhg1, name="g_wg1", host=_Exchange([packed], gather=True), **wgrad)
    g_wu1, p_wg1 = _mm_tn(x, dhu1, name="g_wu1", host=between_chips([g_wg1]), **wgrad)
    g_wd1, p_wu1 = _mm_tn(a1, dpre1, name="g_wd1", host=between_chips([g_wu1]), **wdgrad)
    grad_x, p_wd1 = _ffn_bwd_dx(dpre1, dhg1, dhu1, wg1, wu1, tm=tm_ffn, name="ffn1_bwd_dx",
                                host=between_chips([g_wd1]))
    parts = {
        "ffn1_w_gate": p_wg1, "ffn1_w_up": p_wu1, "ffn1_w_down": p_wd1, "w_in": p_win, "w_out": p_wout,
        "ffn2_w_gate": p_wg2, "ffn2_w_up": p_wu2, "ffn2_w_down": p_wd2,
    }
    return grad_x, parts, all_packed, {n: small_g[n].shape for n in PACKED}


def _adam_math(w, g, m, v):
    m2 = ADAM_B1 * m + (1.0 - ADAM_B1) * g
    v2 = ADAM_B2 * v + (1.0 - ADAM_B2) * (g * g)
    m_hat = m2 / (1.0 - ADAM_B1 ** ADAM_STEP)
    v_hat = v2 / (1.0 - ADAM_B2 ** ADAM_STEP)
    delta = -ADAM_LR * (m_hat / (jnp.sqrt(v_hat) + ADAM_EPS) + ADAM_WD * w)
    return delta, m2, v2


ADAM_TILE_ELEMS = 128 * 1024


def _adamw_big(items, *, name):
    _, r, c = items[0][1].shape
    n_parts = items[0][0].shape[0]
    n_items = len(items)
    assert all(it[1].shape == (1, r, c) and it[0].shape == (n_parts, r, c) for it in items), name
    tr = max(d for d in range(8, r + 1, 8) if r % d == 0 and d * c <= ADAM_TILE_ELEMS)

    def body(*refs):
        ins, outs = refs[:4 * n_items], refs[4 * n_items:]
        for k in range(n_items):
            p_ref, w_ref, m_ref, v_ref = ins[4 * k:4 * k + 4]
            g = p_ref[0].astype(f32)
            for q in range(1, n_parts):
                g = g + p_ref[q].astype(f32)
            d, m2, v2 = _adam_math(w_ref[...], g, m_ref[...], v_ref[...])
            for o_ref, val in zip(outs[4 * k:4 * k + 4], (g, d, m2, v2)):
                o_ref[...] = val

    blk = pl.BlockSpec((None, tr, c), lambda i: (0, i, 0))
    res = pl.pallas_call(
        body, name=name, grid=(r // tr,),
        in_specs=[pl.BlockSpec((n_parts, tr, c), lambda i: (0, i, 0)), blk, blk, blk] * n_items,
        out_specs=[blk] * (4 * n_items), out_shape=[jax.ShapeDtypeStruct((1, r, c), f32)] * (4 * n_items),
        compiler_params=_params(("arbitrary",)),
    )(*[a for it in items for a in it])
    return [res[4 * k:4 * k + 4] for k in range(n_items)]


def _adamw_small(items, *, name):
    n = len(items)

    def body(*refs):
        ins, outs = refs[:4 * n], refs[4 * n:]
        for k in range(n):
            g, w, m, v = (ins[4 * k + q][...] for q in range(4))
            d, m2, v2 = _adam_math(w, g, m, v)
            outs[3 * k][...] = d
            outs[3 * k + 1][...] = m2
            outs[3 * k + 2][...] = v2

    vm = pl.BlockSpec(memory_space=pltpu.VMEM)
    flat = [a for item in items for a in item]
    out_shape = [jax.ShapeDtypeStruct(item[1].shape, f32) for item in items for _ in range(3)]
    return pl.pallas_call(
        body, name=name, in_specs=[vm] * (4 * n), out_specs=[vm] * (3 * n), out_shape=out_shape,
    )(*flat)


def _sum_parts(parts, *, name):
    def body(p_ref, o_ref):
        acc = p_ref[0]
        for q in range(1, N_DEV):
            acc = acc + p_ref[q]
        o_ref[...] = acc

    vm = pl.BlockSpec(memory_space=pltpu.VMEM)
    return pl.pallas_call(
        body, name=name, in_specs=[vm], out_specs=vm, out_shape=jax.ShapeDtypeStruct(parts.shape[1:], f32),
    )(parts)


WEIGHTS = ["ffn1_w_gate", "ffn1_w_up", "ffn1_w_down", "ln1_g", "ln1_b", "w_in", "b_forget", "conv_w", "conv_b",
           "rg_wa", "rg_ba", "rg_wx", "rg_bx", "lru_lambda", "w_out", "ln2_g", "ln2_b",
           "ffn2_w_gate", "ffn2_w_up", "ffn2_w_down", "ln3_g", "ln3_b"]
BIG = ["ffn1_w_gate", "ffn1_w_up", "ffn1_w_down", "w_in", "w_out", "ffn2_w_gate", "ffn2_w_up", "ffn2_w_down"]
PACKED = ["ln1_g", "ln1_b", "ln2_g", "ln2_b", "ln3_g", "ln3_b", "conv_b", "rg_ba", "rg_bx", "lru_lambda",
          "conv_w", "rg_wa", "rg_wx", "b_forget", "loss"]
PACK_ROWS = 600


def _two_d(a):
    return a.reshape((-1, a.shape[-1]))


def _transport(a):
    return _two_d(a)


def kernel(x, ffn1_w_gate, ffn1_w_up, ffn1_w_down, ln1_g, ln1_b, w_in, b_forget, conv_w, conv_b, rg_wa, rg_ba, rg_wx, rg_bx, lru_lambda, w_out, ln2_g, ln2_b, ffn2_w_gate, ffn2_w_up, ffn2_w_down, ln3_g, ln3_b, loss_target, m_ffn1_w_gate, m_ffn1_w_up, m_ffn1_w_down, m_ln1_g, m_ln1_b, m_w_in, m_b_forget, m_conv_w, m_conv_b, m_rg_wa, m_rg_ba, m_rg_wx, m_rg_bx, m_lru_lambda, m_w_out, m_ln2_g, m_ln2_b, m_ffn2_w_gate, m_ffn2_w_up, m_ffn2_w_down, m_ln3_g, m_ln3_b, v_ffn1_w_gate, v_ffn1_w_up, v_ffn1_w_down, v_ln1_g, v_ln1_b, v_w_in, v_b_forget, v_conv_w, v_conv_b, v_rg_wa, v_rg_ba, v_rg_wx, v_rg_bx, v_lru_lambda, v_w_out, v_ln2_g, v_ln2_b, v_ffn2_w_gate, v_ffn2_w_up, v_ffn2_w_down, v_ln3_g, v_ln3_b):
    w_args = (ffn1_w_gate, ffn1_w_up, ffn1_w_down, ln1_g, ln1_b, w_in, b_forget, conv_w, conv_b, rg_wa, rg_ba, rg_wx, rg_bx, lru_lambda, w_out, ln2_g, ln2_b, ffn2_w_gate, ffn2_w_up, ffn2_w_down, ln3_g, ln3_b)
    m_args = (m_ffn1_w_gate, m_ffn1_w_up, m_ffn1_w_down, m_ln1_g, m_ln1_b, m_w_in, m_b_forget, m_conv_w, m_conv_b, m_rg_wa, m_rg_ba, m_rg_wx, m_rg_bx, m_lru_lambda, m_w_out, m_ln2_g, m_ln2_b, m_ffn2_w_gate, m_ffn2_w_up, m_ffn2_w_down, m_ln3_g, m_ln3_b)
    v_args = (v_ffn1_w_gate, v_ffn1_w_up, v_ffn1_w_down, v_ln1_g, v_ln1_b, v_w_in, v_b_forget, v_conv_w, v_conv_b, v_rg_wa, v_rg_ba, v_rg_wx, v_rg_bx, v_lru_lambda, v_w_out, v_ln2_g, v_ln2_b, v_ffn2_w_gate, v_ffn2_w_up, v_ffn2_w_down, v_ln3_g, v_ln3_b)
    w = dict(zip(WEIGHTS, w_args))
    m = dict(zip(WEIGHTS, m_args))
    v = dict(zip(WEIGHTS, v_args))
    me = 4 * lax.axis_index("x") + 2 * lax.axis_index("y") + lax.axis_index("c")

    sent = {n: _transport(w[n]).astype(bf16) for n in BIG}
    sent["conv_w"] = _two_d(w["conv_w"])
    small = {n: w[n] for n in ("ln1_g", "ln1_b", "ln2_g", "ln2_b", "ln3_g", "ln3_b", "b_forget", "conv_b",
                               "lru_lambda")}
    small.update({n: w[n][0] for n in ("rg_wa", "rg_ba", "rg_wx", "rg_bx")})

    grad_x, parts, all_packed, small_shapes = _local_step(x[0], loss_target[0], sent, small)

    total = _sum_parts(all_packed, name="sum_small_grads")
    grads, off = {}, 0
    for n in PACKED:
        size = math.prod(small_shapes[n])
        rows = -(-size // LANES)
        grads[n] = total[off:off + rows].reshape(-1)[:size].reshape(small_shapes[n])
        off += rows
    loss = grads.pop("loss").reshape(())
    grads["conv_w"] = lax.dynamic_slice_in_dim(grads["conv_w"], me * (LRU_W // N_DEV), LRU_W // N_DEV, axis=1)

    delta, new_m, new_v = {}, {}, {}
    for group in (("ffn1_w_gate", "ffn1_w_up", "ffn2_w_gate", "ffn2_w_up"), ("ffn1_w_down", "ffn2_w_down"),
                  ("w_in",), ("w_out",)):
        done = _adamw_big([(parts[n], w[n], m[n], v[n]) for n in group], name="adamw_" + group[0])
        for n, (g, d, m2, v2) in zip(group, done):
            grads[n], delta[n], new_m[n], new_v[n] = g, d, m2, v2
    small_names = [n for n in WEIGHTS if n not in BIG]
    outs = _adamw_small([(_two_d(grads[n]), _two_d(w[n]), _two_d(m[n]), _two_d(v[n])) for n in small_names],
                        name="adamw_small")
    for k, n in enumerate(small_names):
        delta[n], new_m[n], new_v[n] = outs[3 * k], outs[3 * k + 1], outs[3 * k + 2]

    def shaped(d):
        return [d[n].reshape(w[n].shape) for n in WEIGHTS]

    return (loss, grad_x[None], *shaped(grads), *shaped(delta), *shaped(new_m), *shaped(new_v))
```

```python
import functools
import math

import jax
import jax.numpy as jnp
from jax import lax
from jax.experimental import pallas as pl
from jax.experimental.pallas import tpu as pltpu

f32 = jnp.float32
bf16 = jnp.bfloat16

N_DEV = 8
D_MODEL = 1024
D_FF = 4096
FF_TILE = D_FF // N_DEV
FOX_W = 512
LRU_W = 512
HEADS = 8
HEAD_D = 64
IN_COLS = 2568
IN_SHARD = IN_COLS // N_DEV
LANES = 128
LN_EPS = 1e-5
ALPHA = 2.0 ** 0.25
ATT_SCALE = 1.0 / math.sqrt(HEAD_D)
LRU_C = 8.0
NEG_BIG = -1e30

ADAM_LR = 0.001
ADAM_B1 = 0.9
ADAM_B2 = 0.999
ADAM_EPS = 1e-08
ADAM_WD = 0.01
ADAM_STEP = 10

VMEM_LIMIT = 56 * 1024 * 1024
MESH_T = pl.DeviceIdType.MESH


def _params(sem, **kw):
    return pltpu.CompilerParams(dimension_semantics=sem, vmem_limit_bytes=VMEM_LIMIT, **kw)


def _sigmoid(x):
    return 1.0 / (1.0 + jnp.exp(-x))


def _sigmoid_tanh(x):
    return 0.5 * jnp.tanh(0.5 * x) + 0.5


def _softplus(x):
    return jnp.maximum(x, 0.0) + jnp.log(1.0 + jnp.exp(-jnp.abs(x)))


def _one_minus_exp(x):
    series = -x * (1.0 + x * (0.5 + x * (1.0 / 6 + x * (1.0 / 24 + x * (1.0 / 120 + x * (1.0 / 720))))))
    return jnp.where(x > -0.125, series, 1.0 - jnp.exp(x))


_GELU_C = math.sqrt(2.0 / math.pi)


def _gelu_and_grad(x):
    inner = _GELU_C * (x + 0.044715 * x * x * x)
    t = jnp.tanh(inner)
    g = 0.5 * x * (1.0 + t)
    dg = 0.5 * (1.0 + t) + 0.5 * x * (1.0 - t * t) * _GELU_C * (1.0 + 3 * 0.044715 * x * x)
    return g, dg


def _ln_fwd_tile(pre):
    mu = jnp.mean(pre, axis=-1, keepdims=True)
    xc = pre - mu
    var = jnp.mean(xc * xc, axis=-1, keepdims=True)
    rstd = lax.rsqrt(var + LN_EPS)
    return xc * rstd, rstd


def _ln_bwd_tile(dy, xhat, rstd, g):
    dyg = dy * g
    m1 = jnp.mean(dyg, axis=-1, keepdims=True)
    m2 = jnp.mean(dyg * xhat, axis=-1, keepdims=True)
    dpre = rstd * (dyg - m1 - xhat * m2)
    return dpre, jnp.sum(dy * xhat, axis=0, keepdims=True), jnp.sum(dy, axis=0, keepdims=True)


_NT = (((1,), (1,)), ((), ()))
_TN = (((0,), (0,)), ((), ()))


class _Exchange:
    def __init__(self, arrs, gather, chips=False, direct=False):
        self.arrs, self.gather, self.n, self.chips = list(arrs), gather, len(arrs), chips
        self.relays = gather and not direct

    def out_shape(self):
        return [jax.ShapeDtypeStruct(((N_DEV,) + a.shape) if self.gather else a.shape, a.dtype) for a in self.arrs]

    def scratch(self):
        n_remote = self.n * (N_DEV - 1)
        return [pltpu.SemaphoreType.DMA((n_remote,)), pltpu.SemaphoreType.DMA((n_remote,)),
                pltpu.SemaphoreType.DMA((self.n,))]

    def copies(self, ins, outs, sems):
        send_sems, recv_sems, local_sems = sems
        x, y, c = lax.axis_index("x"), lax.axis_index("y"), lax.axis_index("c")
        me = 2 * x + y if self.chips else 4 * x + 2 * y + c
        out = []
        for k in range(self.n):
            for d in (range(2, N_DEV, 2) if self.chips else range(1, N_DEV)):
                px = 1 - x if d & 4 else x
                py = 1 - y if d & 2 else y
                pc = 1 - c if d & 1 else c
                sem = k * (N_DEV - 1) + d - 1
                out.append(pltpu.make_async_remote_copy(
                    src_ref=ins[k] if self.gather else ins[k].at[2 * px + py if self.chips else 4 * px + 2 * py + pc],
                    dst_ref=outs[k].at[me],
                    send_sem=send_sems.at[sem], recv_sem=recv_sems.at[sem],
                    device_id=(px, py, pc), device_id_type=MESH_T))
            out.append(pltpu.make_async_copy(ins[k] if self.gather else ins[k].at[me], outs[k].at[me],
                                             local_sems.at[k]))
        return out

    def gather_copies(self, ins, outs, sems):
        send_sems, recv_sems, local_sems = sems
        x, y, c = lax.axis_index("x"), lax.axis_index("y"), lax.axis_index("c")
        sibling = (x, y, 1 - c)
        chips = [(1 - x, y), (x, 1 - y), (1 - x, 1 - y)]
        out = []
        for k in range(self.n):
            def copy(s, block, to, src=None, k=k):
                rows = outs[k].at[4 * block[0] + 2 * block[1] + block[2]]
                sem = k * (N_DEV - 1) + s
                return pltpu.make_async_remote_copy(
                    src_ref=rows if src is None else src, dst_ref=rows, send_sem=send_sems.at[sem],
                    recv_sem=recv_sems.at[sem], device_id=to, device_id_type=MESH_T)

            first = [copy(0, (x, y, c), sibling, src=ins[k])]
            first += [copy(1 + q, (x, y, c), (*chip, c), src=ins[k]) for q, chip in enumerate(chips)]
            passed = [copy(4 + q, (*chip, c), sibling) for q, chip in enumerate(chips)]
            own = pltpu.make_async_copy(ins[k], outs[k].at[4 * x + 2 * y + c], local_sems.at[k])
            out.append((first, passed, own, copy))
        return out, sibling, chips, (x, y, c)

    def start(self, ins, outs, sems):
        if not self.relays:
            for cp in self.copies(ins, outs, sems):
                cp.start()
            return
        per_array, _, _, _ = self.gather_copies(ins, outs, sems)
        for first, _, own, _ in per_array:
            own.start()
            for cp in first:
                cp.start()

    def relay(self, ins, outs, sems):
        per_array, sibling, chips, (x, y, c) = self.gather_copies(ins, outs, sems)
        for first, passed, own, copy in per_array:
            for q, chip in enumerate(chips):
                copy(1 + q, (*chip, c), (x, y, c)).wait_recv()
                passed[q].start()

    def wait(self, ins, outs, sems, relayed=False):
        if not self.relays:
            for cp in self.copies(ins, outs, sems):
                cp.wait()
            return
        if not relayed:
            self.relay(ins, outs, sems)
        per_array, sibling, chips, (x, y, c) = self.gather_copies(ins, outs, sems)
        for first, passed, own, copy in per_array:
            copy(0, sibling, (x, y, c)).wait_recv()
            for q, chip in enumerate(chips):
                copy(4 + q, (*chip, 1 - c), (x, y, c)).wait_recv()
            for cp in first + passed:
                cp.wait_send()
            own.wait()


class _Hosts:
    def __init__(self, *hosts):
        self.hosts = hosts
        self.relays = any(h.relays for h in hosts)
        self.n = sum(h.n for h in hosts)
        self.arrs = [a for h in hosts for a in h.arrs]

    def out_shape(self):
        return [sh for h in self.hosts for sh in h.out_shape()]

    def scratch(self):
        return [sc for h in self.hosts for sc in h.scratch()]

    def _each(self, ins, outs, sems):
        at = 0
        for k, h in enumerate(self.hosts):
            yield h, ins[at:at + h.n], outs[at:at + h.n], sems[3 * k:3 * k + 3]
            at += h.n

    def start(self, ins, outs, sems):
        for h, h_in, h_out, h_sems in self._each(ins, outs, sems):
            h.start(h_in, h_out, h_sems)

    def relay(self, ins, outs, sems):
        for h, h_in, h_out, h_sems in self._each(ins, outs, sems):
            if h.relays:
                h.relay(h_in, h_out, h_sems)

    def wait(self, ins, outs, sems, relayed=False):
        for h, h_in, h_out, h_sems in self._each(ins, outs, sems):
            h.wait(h_in, h_out, h_sems, relayed=relayed and h.relays)


def _hosted_call(host, body, *, name, grid, in_specs, out_specs, out_shape, scratch_shapes=(), compiler_params):
    out_specs = list(out_specs) if isinstance(out_specs, (list, tuple)) else [out_specs]
    out_shape = list(out_shape) if isinstance(out_shape, (list, tuple)) else [out_shape]
    if host is None:
        return pl.pallas_call(body, name=name, grid=grid, in_specs=in_specs, out_specs=out_specs,
                              out_shape=out_shape, scratch_shapes=list(scratch_shapes),
                              compiler_params=compiler_params)
    n_in, n_out, n_scr, k = len(in_specs), len(out_shape), len(scratch_shapes), host.n

    def wrapped(*refs):
        ins, h_in = refs[:n_in], refs[n_in:n_in + k]
        outs, h_out = refs[n_in + k:n_in + k + n_out], refs[n_in + k + n_out:n_in + 2 * k + n_out]
        scr, sems = refs[n_in + 2 * k + n_out:n_in + 2 * k + n_out + n_scr], refs[n_in + 2 * k + n_out + n_scr:]
        ids = [pl.program_id(a) for a in range(len(grid))]
        first = functools.reduce(jnp.logical_and, [i == 0 for i in ids])
        last = functools.reduce(jnp.logical_and, [i == g - 1 for i, g in zip(ids, grid)])
        steps = math.prod(grid)
        relay_at = (3 * steps) // 4 if host.relays and steps >= 8 else None

        @pl.when(first)
        def _():
            host.start(h_in, h_out, sems)

        if relay_at is not None:
            coords, rest = [], relay_at
            for g in reversed(grid):
                coords.append(rest % g)
                rest //= g

            @pl.when(functools.reduce(jnp.logical_and, [i == cd for i, cd in zip(ids, reversed(coords))]))
            def _():
                host.relay(h_in, h_out, sems)

        body(*ins, *outs, *scr)

        @pl.when(last)
        def _():
            host.wait(h_in, h_out, sems, relayed=relay_at is not None)

    hbm = pl.BlockSpec(memory_space=pl.ANY)
    call = pl.pallas_call(
        wrapped, name=name, grid=grid, in_specs=list(in_specs) + [hbm] * k, out_specs=out_specs + [hbm] * k,
        out_shape=out_shape + host.out_shape(), scratch_shapes=list(scratch_shapes) + host.scratch(),
        compiler_params=compiler_params)
    return lambda *args: call(*args, *host.arrs)


def _ffn_fwd_loss(xhat, g_in, b_in, wg, wu, wd, g_out, b_out, target, *, tm, name):
    t = xhat.shape[0]
    nj = N_DEV

    def body(x_ref, g_ref, b_ref, wg_ref, wu_ref, wd_ref, go_ref, bo_ref, tg_ref,
             dx_ref, sq_ref, gg_ref, gb_ref, hg_ref, hu_ref, xb, acc):
        i = pl.program_id(0)
        j = pl.program_id(1)

        @pl.when(j == 0)
        def _():
            xb[...] = (x_ref[...] * g_ref[...] + b_ref[...]).astype(bf16)
            acc[...] = jnp.zeros_like(acc)

        hg = jnp.dot(xb[...], wg_ref[...], preferred_element_type=f32)
        hu = jnp.dot(xb[...], wu_ref[...], preferred_element_type=f32)
        hg_ref[...] = hg.astype(bf16)
        hu_ref[...] = hu.astype(bf16)
        a = hg * _sigmoid_tanh(hg) * hu
        acc[...] += jnp.dot(a.astype(bf16), wd_ref[...], preferred_element_type=f32)

        @pl.when(j == nj - 1)
        def _():
            x = x_ref[...] * g_ref[...] + b_ref[...]
            xo, rstd = _ln_fwd_tile(ALPHA * x + 0.5 * acc[...])
            diff = xo * go_ref[...] + bo_ref[...] - tg_ref[...]
            sq = jnp.sum(diff * diff, axis=0, keepdims=True)
            dprev, gg, gb = _ln_bwd_tile(diff * (1.0 / D_MODEL), xo, rstd, go_ref[...])
            dx_ref[...] = dprev

            @pl.when(i == 0)
            def _():
                sq_ref[...] = sq
                gg_ref[...] = gg
                gb_ref[...] = gb

            @pl.when(i > 0)
            def _():
                sq_ref[...] += sq
                gg_ref[...] += gg
                gb_ref[...] += gb

    tok = pl.BlockSpec((tm, D_MODEL), lambda i, j: (i, 0))
    row = pl.BlockSpec((1, D_MODEL), lambda i, j: (0, 0))
    hid = pl.BlockSpec((tm, FF_TILE), lambda i, j: (i, j))
    return pl.pallas_call(
        body, name=name, grid=(t // tm, nj),
        in_specs=[tok, row, row,
                  pl.BlockSpec((None, D_MODEL, FF_TILE), lambda i, j: (j, 0, 0)),
                  pl.BlockSpec((None, D_MODEL, FF_TILE), lambda i, j: (j, 0, 0)),
                  pl.BlockSpec((None, FF_TILE, D_MODEL), lambda i, j: (j, 0, 0)), row, row, tok],
        out_specs=[tok, row, row, row, hid, hid],
        out_shape=[jax.ShapeDtypeStruct((t, D_MODEL), f32)] + [jax.ShapeDtypeStruct((1, D_MODEL), f32)] * 3
        + [jax.ShapeDtypeStruct((t, D_FF), bf16)] * 2,
        scratch_shapes=[pltpu.VMEM((tm, D_MODEL), bf16), pltpu.VMEM((tm, D_MODEL), f32)],
        compiler_params=_params(("arbitrary", "arbitrary")),
    )(xhat, g_in, b_in, wg, wu, wd, g_out, b_out, target)


def _ffn1_fwd_gathering(x, own, extra, *, tm, name):
    t = x.shape[0]
    n_i = t // tm
    n_arr = 3
    k_extra = extra.n
    ex = _Exchange(list(own), gather=True)
    ax, ay, ac = lax.axis_index("x"), lax.axis_index("y"), lax.axis_index("c")
    order = jnp.stack([4 * px + 2 * py + pc for px, py in ((ax, ay), (1 - ax, ay), (ax, 1 - ay), (1 - ax, 1 - ay))
                       for pc in (ac, 1 - ac)]).astype(jnp.int32)
    arrival = [None, (0, None), (1, 0), (4, None), (2, 1), (5, None), (3, 2), (6, None)]

    def body(order_ref, x_ref, *refs):
        w_in, e_in = refs[:n_arr], refs[n_arr:n_arr + k_extra]
        refs = refs[n_arr + k_extra:]
        xo_ref, rstd_ref, hg_ref, hu_ref = refs[:4]
        w_all, e_out = refs[4:4 + n_arr], refs[4 + n_arr:4 + n_arr + k_extra]
        acc, wgb, wub, wdb, fetch_sems, send_sems, recv_sems, local_sems = refs[4 + n_arr + k_extra:12 + n_arr + k_extra]
        e_sems = refs[12 + n_arr + k_extra:]
        bufs = (wgb, wub, wdb)
        s = pl.program_id(0)
        i = pl.program_id(1)
        per_array, sibling, chips, (x_, y_, c_) = ex.gather_copies(w_in, w_all, (send_sems, recv_sems, local_sems))

        def fetch(pos, slot):
            return [pltpu.make_async_copy(w_in[a] if pos == 0 else w_all[a].at[order_ref[pos]],
                                          bufs[a].at[slot], fetch_sems.at[n_arr * slot + a]) for a in range(n_arr)]

        def source_of(pos):
            chip = (x_, y_) if pos < 2 else chips[(pos - 2) // 2]
            return (*chip, c_ if pos % 2 == 0 else 1 - c_)

        @pl.when(jnp.logical_and(s == 0, i == 0))
        def _():
            for q in range(4):
                for first, _, own_copy, _ in per_array:
                    if q == 0:
                        own_copy.start()
                    first[q].start()
            for cp in fetch(0, 0):
                cp.start()
            for cp in fetch(0, 0):
                cp.wait()

        @pl.when(jnp.logical_and(s == N_DEV // 2, i == 0))
        def _():
            extra.start(e_in, e_out, e_sems)

        for pos in range(1, N_DEV):
            @pl.when(jnp.logical_and(s == pos - 1, i == n_i - 1))
            def _(pos=pos):
                sem, passes = arrival[pos]
                for _, passed, _, copy in per_array:
                    copy(sem, source_of(pos), (x_, y_, c_)).wait_recv()
                    if passes is not None:
                        passed[passes].start()
                for cp in fetch(pos, pos % 2):
                    cp.start()

            @pl.when(jnp.logical_and(s == pos, i == 0))
            def _(pos=pos):
                for cp in fetch(pos, pos % 2):
                    cp.wait()

        slot = s % 2
        xb = x_ref[...].astype(bf16)
        hg = jnp.dot(xb, wgb[slot], preferred_element_type=f32)
        hu = jnp.dot(xb, wub[slot], preferred_element_type=f32)
        hg_ref[...] = hg.astype(bf16)
        hu_ref[...] = hu.astype(bf16)
        a = hg * _sigmoid_tanh(hg) * hu
        part = jnp.dot(a.astype(bf16), wdb[slot], preferred_element_type=f32)

        @pl.when(s == 0)
        def _():
            acc[i] = part

        @pl.when(s > 0)
        def _():
            acc[i] += part

        @pl.when(s == N_DEV - 1)
        def _():
            xo, rstd = _ln_fwd_tile(ALPHA * x_ref[...] + 0.5 * acc[i])
            xo_ref[...] = xo
            rstd_ref[...] = rstd

        @pl.when(jnp.logical_and(s == N_DEV - 1, i == n_i - 1))
        def _():
            for first, passed, own_copy, _ in per_array:
                for cp in first + passed:
                    cp.wait_send()
                own_copy.wait()
            extra.wait(e_in, e_out, e_sems)

    hbm = pl.BlockSpec(memory_space=pl.ANY)
    last = N_DEV - 1
    tok_out = pl.BlockSpec((tm, D_MODEL), lambda s, i, o: (jnp.where(s == last, i, 0), 0))
    col_out = pl.BlockSpec((tm, 1), lambda s, i, o: (jnp.where(s == last, i, 0), 0))
    hid = pl.BlockSpec((tm, FF_TILE), lambda s, i, o: (i, o[s]))
    shard_shapes = [(N_DEV,) + w.shape for w in own]
    grid_spec = pltpu.PrefetchScalarGridSpec(
        num_scalar_prefetch=1, grid=(N_DEV, n_i),
        in_specs=[pl.BlockSpec((tm, D_MODEL), lambda s, i, o: (i, 0))] + [hbm] * (n_arr + k_extra),
        out_specs=[tok_out, col_out, hid, hid] + [hbm] * (n_arr + k_extra),
        scratch_shapes=[pltpu.VMEM((n_i, tm, D_MODEL), f32)]
        + [pltpu.VMEM((2,) + w.shape, bf16) for w in own]
        + [pltpu.SemaphoreType.DMA((2 * n_arr,))] + ex.scratch() + extra.scratch())
    res = pl.pallas_call(
        body, name=name, grid_spec=grid_spec,
        out_shape=[jax.ShapeDtypeStruct((t, D_MODEL), f32), jax.ShapeDtypeStruct((t, 1), f32),
                   jax.ShapeDtypeStruct((t, D_FF), bf16), jax.ShapeDtypeStruct((t, D_FF), bf16)]
        + [jax.ShapeDtypeStruct(sh, bf16) for sh in shard_shapes] + extra.out_shape(),
        compiler_params=_params(("arbitrary", "arbitrary")),
    )(order, x, *own, *extra.arrs)
    return res


def _ffn_bwd(dpre, hg, hu, wg, wu, wd, ln_in, *, tm, name, host=None):
    t = dpre.shape[0]
    nj = N_DEV
    with_ln = ln_in is not None

    def body(*refs):
        if with_ln:
            (dp_ref, hg_ref, hu_ref, wg_ref, wu_ref, wd_ref, xh_ref, rs_ref, g_ref,
             dx_ref, gg_ref, gb_ref, dhg_ref, dhu_ref, a_ref, dfb, acc) = refs
        else:
            (dp_ref, hg_ref, hu_ref, wg_ref, wu_ref, wd_ref,
             dx_ref, dhg_ref, dhu_ref, a_ref, dfb, acc) = refs
        i = pl.program_id(0)
        j = pl.program_id(1)

        @pl.when(j == 0)
        def _():
            dfb[...] = (0.5 * dp_ref[...]).astype(bf16)
            acc[...] = jnp.zeros_like(acc)

        da = lax.dot_general(dfb[...], wd_ref[...], _NT, preferred_element_type=f32)
        hgv = hg_ref[...].astype(f32)
        huv = hu_ref[...].astype(f32)
        sg = _sigmoid_tanh(hgv)
        silu = hgv * sg
        a_ref[...] = (silu * huv).astype(bf16)
        dhu = (da * silu).astype(bf16)
        dhg = (da * huv * (sg * (1.0 + hgv * (1.0 - sg)))).astype(bf16)
        dhg_ref[...] = dhg
        dhu_ref[...] = dhu
        acc[...] += (lax.dot_general(dhg, wg_ref[...], _NT, preferred_element_type=f32)
                     + lax.dot_general(dhu, wu_ref[...], _NT, preferred_element_type=f32))

        @pl.when(j == nj - 1)
        def _():
            dx = ALPHA * dp_ref[...] + acc[...]
            if with_ln:
                dprev, gg, gb = _ln_bwd_tile(dx, xh_ref[...], rs_ref[...], g_ref[...])
                dx_ref[...] = dprev

                @pl.when(i == 0)
                def _():
                    gg_ref[...] = gg
                    gb_ref[...] = gb

                @pl.when(i > 0)
                def _():
                    gg_ref[...] += gg
                    gb_ref[...] += gb
            else:
                dx_ref[...] = dx

    tok = pl.BlockSpec((tm, D_MODEL), lambda i, j: (i, 0), pipeline_mode=pl.Buffered(1))
    row = pl.BlockSpec((1, D_MODEL), lambda i, j: (0, 0))
    hid = pl.BlockSpec((tm, FF_TILE), lambda i, j: (i, j))
    in_specs = [tok, hid, hid,
                pl.BlockSpec((None, D_MODEL, FF_TILE), lambda i, j: (j, 0, 0)),
                pl.BlockSpec((None, D_MODEL, FF_TILE), lambda i, j: (j, 0, 0)),
                pl.BlockSpec((None, FF_TILE, D_MODEL), lambda i, j: (j, 0, 0))]
    args = [dpre, hg, hu, wg, wu, wd]
    out_specs = [tok]
    out_shape = [jax.ShapeDtypeStruct((t, D_MODEL), f32)]
    if with_ln:
        in_specs += [tok, pl.BlockSpec((tm, 1), lambda i, j: (i, 0)), row]
        args += list(ln_in)
        out_specs += [row, row]
        out_shape += [jax.ShapeDtypeStruct((1, D_MODEL), f32)] * 2
    out_specs += [hid, hid, hid]
    out_shape += [jax.ShapeDtypeStruct((t, D_FF), bf16)] * 3
    return _hosted_call(
        host, body, name=name, grid=(t // tm, nj), in_specs=in_specs, out_specs=out_specs, out_shape=out_shape,
        scratch_shapes=[pltpu.VMEM((tm, D_MODEL), bf16), pltpu.VMEM((tm, D_MODEL), f32)],
        compiler_params=_params(("arbitrary", "arbitrary")),
    )(*args)


def _ffn_bwd_act(dpre, hg, hu, wd, *, tm, name, host=None):
    t = dpre.shape[0]

    def body(dp_ref, hg_ref, hu_ref, wd_ref, dhg_ref, dhu_ref, a_ref, dfb):
        @pl.when(pl.program_id(1) == 0)
        def _():
            dfb[...] = (0.5 * dp_ref[...]).astype(bf16)

        da = lax.dot_general(dfb[...], wd_ref[...], _NT, preferred_element_type=f32)
        hgv = hg_ref[...].astype(f32)
        huv = hu_ref[...].astype(f32)
        sg = _sigmoid_tanh(hgv)
        silu = hgv * sg
        a_ref[...] = (silu * huv).astype(bf16)
        dhu_ref[...] = (da * silu).astype(bf16)
        dhg_ref[...] = (da * huv * (sg * (1.0 + hgv * (1.0 - sg)))).astype(bf16)

    hid = pl.BlockSpec((tm, FF_TILE), lambda i, j: (i, j))
    return _hosted_call(
        host, body, name=name, grid=(t // tm, N_DEV),
        in_specs=[pl.BlockSpec((tm, D_MODEL), lambda i, j: (i, 0)), hid, hid,
                  pl.BlockSpec((None, FF_TILE, D_MODEL), lambda i, j: (j, 0, 0))],
        out_specs=[hid, hid, hid], out_shape=[jax.ShapeDtypeStruct((t, D_FF), bf16)] * 3,
        scratch_shapes=[pltpu.VMEM((tm, D_MODEL), bf16)],
        compiler_params=_params(("arbitrary", "arbitrary")),
    )(dpre, hg, hu, wd)


def _ffn_bwd_dx(dpre, dhg, dhu, wg, wu, *, tm, name, host=None):
    t = dpre.shape[0]
    nj = N_DEV

    def body(dp_ref, dhg_ref, dhu_ref, wg_ref, wu_ref, dx_ref, acc):
        j = pl.program_id(1)

        @pl.when(j == 0)
        def _():
            acc[...] = jnp.zeros_like(acc)

        acc[...] += (lax.dot_general(dhg_ref[...], wg_ref[...], _NT, preferred_element_type=f32)
                     + lax.dot_general(dhu_ref[...], wu_ref[...], _NT, preferred_element_type=f32))

        @pl.when(j == nj - 1)
        def _():
            dx_ref[...] = ALPHA * dp_ref[...] + acc[...]

    tok = pl.BlockSpec((tm, D_MODEL), lambda i, j: (i, 0))
    hid = pl.BlockSpec((tm, FF_TILE), lambda i, j: (i, j))
    wspec = pl.BlockSpec((None, D_MODEL, FF_TILE), lambda i, j: (j, 0, 0))
    return _hosted_call(
        host, body, name=name, grid=(t // tm, nj), in_specs=[tok, hid, hid, wspec, wspec],
        out_specs=[tok], out_shape=[jax.ShapeDtypeStruct((t, D_MODEL), f32)],
        scratch_shapes=[pltpu.VMEM((tm, D_MODEL), f32)],
        compiler_params=_params(("arbitrary", "arbitrary")),
    )(dpre, dhg, dhu, wg, wu)


def _mm(a, b, *, mode, out_dtype, tm, tn, tk, name, affine=None, a_cols=None, b_cols=None,
        b_blocked=False, out_blocked=False, out_scale=None):
    if mode == "nn":
        m_full, k_full = a.shape
        m_dim, k_dim = (m_full, a_cols[1]) if a_cols else (m_full, k_full)
    else:
        k_dim, m_full = a.shape
        m_dim = a_cols[1] if a_cols else m_full
    a_off = a_cols[0] if a_cols else 0
    if b_blocked:
        n_dim = b.shape[0] * b.shape[2]
        assert b.shape[2] == tn
    else:
        n_dim = b_cols[1] if b_cols else b.shape[1]
    b_off = b_cols[0] if b_cols else 0
    assert m_dim % tm == 0 and n_dim % tn == 0 and k_dim % tk == 0, (name, m_dim, n_dim, k_dim)
    nk = k_dim // tk

    def body(*refs):
        if affine is not None:
            a_ref, g_ref, s_ref, b_ref, o_ref, acc = refs
        else:
            a_ref, b_ref, o_ref, acc = refs
        k = pl.program_id(2)

        @pl.when(k == 0)
        def _():
            acc[...] = jnp.zeros_like(acc)

        av = a_ref[...]
        if affine is not None:
            av = av * g_ref[...] + s_ref[...]
        av = av.astype(bf16)
        bv = b_ref[...].astype(bf16)
        if mode == "nn":
            acc[...] += jnp.dot(av, bv, preferred_element_type=f32)
        else:
            acc[...] += lax.dot_general(av, bv, _TN, preferred_element_type=f32)

        @pl.when(k == nk - 1)
        def _():
            res = acc[...] if out_scale is None else acc[...] * out_scale
            o_ref[...] = res.astype(out_dtype)

    if mode == "nn":
        a_spec = pl.BlockSpec((tm, tk), lambda i, j, k: (i, k + a_off))
        aff_spec = pl.BlockSpec((1, tk), lambda i, j, k: (0, k + a_off))
    else:
        a_spec = pl.BlockSpec((tk, tm), lambda i, j, k: (k, i + a_off))
        aff_spec = pl.BlockSpec((1, tm), lambda i, j, k: (0, i + a_off))
    if b_blocked:
        b_spec = pl.BlockSpec((None, tk, tn), lambda i, j, k: (j, k, 0))
    else:
        b_spec = pl.BlockSpec((tk, tn), lambda i, j, k: (k, j + b_off))
    if out_blocked:
        o_spec = pl.BlockSpec((None, tm, tn), lambda i, j, k: (j, i, 0))
        o_shape = jax.ShapeDtypeStruct((n_dim // tn, m_dim, tn), out_dtype)
    else:
        o_spec = pl.BlockSpec((tm, tn), lambda i, j, k: (i, j))
        o_shape = jax.ShapeDtypeStruct((m_dim, n_dim), out_dtype)
    in_specs = [a_spec] + ([aff_spec, aff_spec] if affine is not None else []) + [b_spec]
    args = [a] + (list(affine) if affine is not None else []) + [b]
    return pl.pallas_call(
        body, name=name, grid=(m_dim // tm, n_dim // tn, nk), in_specs=in_specs, out_specs=o_spec,
        out_shape=o_shape, scratch_shapes=[pltpu.VMEM((tm, tn), f32)],
        compiler_params=_params(("arbitrary", "arbitrary", "arbitrary")),
    )(*args)


def _mm_tn(a, b, *, out_dtype, tm, mb, tn, nb, tk, name, affine=None, out_blocked=False, out_scale=None,
           pair=False, host=None):
    k_dim, m_dim = a.shape
    multi_b = isinstance(b, (list, tuple))
    b_list = list(b) if multi_b else [b]
    n_dim = nb * tn if multi_b else b.shape[1]
    assert m_dim % (mb * tm) == 0 and n_dim % (nb * tn) == 0 and k_dim % tk == 0, (name, m_dim, n_dim, k_dim)
    nk = k_dim // tk
    grid = (m_dim // (mb * tm), n_dim // (nb * tn), nk)
    if pair:
        assert mb * nb == 4 and grid[0] * grid[1] == 2 and out_dtype == bf16, name

    def body(*refs):
        if pair:
            refs, (acc, send_buf, recv_buf, keep, send_sems, recv_sems) = refs[:-6], refs[-6:]
        else:
            refs, acc = refs[:-1], refs[-1]
        a_ref, o_ref = refs[0], refs[-1]
        if affine is not None:
            g_ref, s_ref = refs[1:3]
        b_refs = refs[3 if affine is not None else 1:-1]
        k = pl.program_id(2)

        @pl.when(k == 0)
        def _():
            acc[...] = jnp.zeros_like(acc)

        av = a_ref[...]
        if affine is not None:
            av = av * g_ref[...] + s_ref[...]
        av = av.astype(bf16)
        if multi_b:
            pieces = [r[...].astype(bf16) for r in b_refs]
        else:
            bv = b_refs[0][...].astype(bf16)
            pieces = [bv[:, jn * tn:(jn + 1) * tn] for jn in range(nb)]
        for im in range(mb):
            a_t = av[:, im * tm:(im + 1) * tm].T
            for jn in range(nb):
                acc[im * nb + jn] += jnp.dot(a_t, pieces[jn], preferred_element_type=f32)

        def scaled(v):
            return v if out_scale is None else v * out_scale

        @pl.when(k == nk - 1)
        def _():
            if pair:
                x, y, c = lax.axis_index("x"), lax.axis_index("y"), lax.axis_index("c")
                window = pl.program_id(0) + pl.program_id(1)

                def swap(w, cc):
                    return pltpu.make_async_remote_copy(
                        src_ref=send_buf.at[w, cc], dst_ref=recv_buf.at[w, cc],
                        send_sem=send_sems.at[2 * w + cc], recv_sem=recv_sems.at[2 * w + cc],
                        device_id=(x, y, 1 - c), device_id_type=MESH_T)

                for w in range(2):
                    @pl.when(window == w)
                    def _(w=w):
                        for cc in range(2):
                            send_buf[w, cc] = scaled(acc[2 * cc + 1 - c]).astype(bf16)
                            swap(w, cc).start()
                            if w == 0:
                                keep[cc] = scaled(acc[2 * cc + c])

                @pl.when(window == 1)
                def _():
                    for w in range(2):
                        for cc in range(2):
                            swap(w, cc).wait_recv()
                            mine = keep[cc] if w == 0 else scaled(acc[2 * cc + c])
                            o_ref[2 * w + cc] = (mine + recv_buf[w, cc].astype(f32)).astype(bf16)
                    for w in range(2):
                        for cc in range(2):
                            swap(w, cc).wait_send()
                return
            for im in range(mb):
                for jn in range(nb):
                    res = scaled(acc[im * nb + jn])
                    if out_blocked:
                        o_ref[jn, im * tm:(im + 1) * tm, :] = res.astype(out_dtype)
                    else:
                        o_ref[im * tm:(im + 1) * tm, jn * tn:(jn + 1) * tn] = res.astype(out_dtype)

    a_spec = pl.BlockSpec((tk, mb * tm), lambda i, j, k: (k, i))
    aff_spec = pl.BlockSpec((1, mb * tm), lambda i, j, k: (0, i))
    if multi_b:
        b_specs = [pl.BlockSpec((tk, tn), lambda i, j, k: (k, 0))] * nb
    else:
        b_specs = [pl.BlockSpec((tk, nb * tn), lambda i, j, k: (k, j))]
    scratch = [pltpu.VMEM((mb * nb, tm, tn), f32)]
    if pair:
        o_spec = pl.BlockSpec((4, tm, tn), lambda i, j, k: (0, 0, 0))
        o_shape = jax.ShapeDtypeStruct((4, tm, tn), out_dtype)
        scratch += [pltpu.VMEM((2, 2, tm, tn), bf16), pltpu.VMEM((2, 2, tm, tn), bf16), pltpu.VMEM((2, tm, tn), f32),
                    pltpu.SemaphoreType.DMA((4,)), pltpu.SemaphoreType.DMA((4,))]
    elif out_blocked:
        o_spec = pl.BlockSpec((nb, mb * tm, tn), lambda i, j, k: (j, i, 0))
        o_shape = jax.ShapeDtypeStruct((n_dim // tn, m_dim, tn), out_dtype)
    else:
        o_spec = pl.BlockSpec((mb * tm, nb * tn), lambda i, j, k: (i, j))
        o_shape = jax.ShapeDtypeStruct((m_dim, n_dim), out_dtype)
    in_specs = [a_spec] + ([aff_spec, aff_spec] if affine is not None else []) + b_specs
    args = [a] + (list(affine) if affine is not None else []) + b_list
    res = _hosted_call(
        host, body, name=name, grid=grid, in_specs=in_specs, out_specs=o_spec, out_shape=o_shape,
        scratch_shapes=scratch, compiler_params=_params(("arbitrary", "arbitrary", "arbitrary")),
    )(*args)
    return res[0] if host is None else res


def _in_proj(xhat, g, b, w_in, *, tm, name):
    t = xhat.shape[0]
    n_qkv, n_l = 3 * FOX_W, 2 * LRU_W

    def body(x_ref, g_ref, b_ref, w_ref, qkv_ref, zl_ref, zfg_ref):
        xb = (x_ref[...] * g_ref[...] + b_ref[...]).astype(bf16)
        qkv_ref[...] = jnp.dot(xb, w_ref[:, :n_qkv], preferred_element_type=f32).astype(bf16)
        zl_ref[...] = jnp.dot(xb, w_ref[:, n_qkv:n_qkv + n_l], preferred_element_type=f32)
        zfg_ref[...] = jnp.dot(xb, w_ref[:, n_qkv + n_l:], preferred_element_type=f32)

    row = pl.BlockSpec((1, D_MODEL), lambda i: (0, 0))
    return pl.pallas_call(
        body, name=name, grid=(t // tm,),
        in_specs=[pl.BlockSpec((tm, D_MODEL), lambda i: (i, 0)), row, row,
                  pl.BlockSpec(w_in.shape, lambda i: (0, 0))],
        out_specs=[pl.BlockSpec((tm, n_qkv), lambda i: (i, 0)), pl.BlockSpec((tm, n_l), lambda i: (i, 0)),
                   pl.BlockSpec((tm, LANES), lambda i: (i, 0))],
        out_shape=[jax.ShapeDtypeStruct((t, n_qkv), bf16), jax.ShapeDtypeStruct((t, n_l), f32),
                   jax.ShapeDtypeStruct((t, LANES), f32)],
        compiler_params=_params(("arbitrary",)),
    )(xhat, g, b, w_in)


def _mmln(pairs, *, tm, name, resid=None, resid_scale=1.0, epi=None, ln=None, n_out=D_MODEL):
    t = pairs[0][0].shape[0]
    n_pairs = len(pairs)
    n_resid = 0 if resid is None else len(resid) - 1

    def body(*refs):
        pos = 0
        val = None
        for p in range(n_pairs):
            a_ref, b_ref = refs[pos], refs[pos + 1]
            pos += 2
            av = a_ref[...].astype(bf16)
            bv = b_ref[...].astype(bf16)
            if pairs[p][6] == "nn":
                term = jnp.dot(av, bv, preferred_element_type=f32)
            else:
                term = lax.dot_general(av, bv, _NT, preferred_element_type=f32)
            val = term if val is None else val + term
        if resid is not None:
            if resid[0] == "plain":
                r = refs[pos][...]
            else:
                r = refs[pos][...] * refs[pos + 1][...] + refs[pos + 2][...]
            pos += n_resid
            val = val + resid_scale * r
        if epi is None:
            o_ref = refs[pos]
            o_ref[...] = val.astype(o_ref.dtype)
        elif epi == "ln_fwd":
            xo, rstd = _ln_fwd_tile(val)
            refs[pos][...] = xo
            refs[pos + 1][...] = rstd
        else:
            xh_ref, rs_ref, g_ref, dx_ref, gg_ref, gb_ref = refs[pos:pos + 6]
            dprev, gg, gb = _ln_bwd_tile(val, xh_ref[...], rs_ref[...], g_ref[...])
            dx_ref[...] = dprev
            i = pl.program_id(0)

            @pl.when(i == 0)
            def _():
                gg_ref[...] = gg
                gb_ref[...] = gb

            @pl.when(i > 0)
            def _():
                gg_ref[...] += gg
                gb_ref[...] += gb

    in_specs, args = [], []
    for (a, acb, aw, b, bcb, bw, mode) in pairs:
        in_specs.append(pl.BlockSpec((tm, aw), lambda i, acb=acb: (i, acb)))
        args.append(a)
        if mode == "nn":
            in_specs.append(pl.BlockSpec((aw, n_out), lambda i, bcb=bcb: (bcb, 0)))
        else:
            in_specs.append(pl.BlockSpec((n_out, bw), lambda i, bcb=bcb: (0, bcb)))
        args.append(b)
    tok = pl.BlockSpec((tm, n_out), lambda i: (i, 0))
    row = pl.BlockSpec((1, n_out), lambda i: (0, 0))
    col = pl.BlockSpec((tm, 1), lambda i: (i, 0))
    if resid is not None:
        in_specs += [tok] if resid[0] == "plain" else [tok, row, row]
        args += list(resid[1:])
    if epi is None:
        out_specs, out_shape = tok, jax.ShapeDtypeStruct((t, n_out), f32)
    elif epi == "ln_fwd":
        out_specs = [tok, col]
        out_shape = [jax.ShapeDtypeStruct((t, n_out), f32), jax.ShapeDtypeStruct((t, 1), f32)]
    else:
        in_specs += [tok, col, row]
        args += list(ln)
        out_specs = [tok, row, row]
        out_shape = [jax.ShapeDtypeStruct((t, n_out), f32)] + [jax.ShapeDtypeStruct((1, n_out), f32)] * 2
    return pl.pallas_call(
        body, name=name, grid=(t // tm,), in_specs=in_specs, out_specs=out_specs, out_shape=out_shape,
        compiler_params=_params(("arbitrary",)),
    )(*args)


CUM_TILE = 512


def _tri(n, lower):
    r = lax.broadcasted_iota(jnp.int32, (n, n), 0)
    c = lax.broadcasted_iota(jnp.int32, (n, n), 1)
    return jnp.where((r >= c) if lower else (r <= c), 1.0, 0.0).astype(f32)


def _cum_fwd(zfg, bfg, *, name):
    t = zfg.shape[0]

    def body(z_ref, b_ref, o_ref, carry):
        @pl.when(pl.program_id(0) == 0)
        def _():
            carry[...] = jnp.zeros_like(carry)

        ls = -_softplus(-(z_ref[...] + b_ref[...]))
        c = jnp.dot(_tri(CUM_TILE, True), ls, preferred_element_type=f32,
                    precision=lax.Precision.HIGHEST) + carry[...]
        o_ref[...] = c
        carry[...] = c[CUM_TILE - 1:CUM_TILE, :]

    blk = pl.BlockSpec((CUM_TILE, LANES), lambda i: (i, 0))
    return pl.pallas_call(
        body, name=name, grid=(t // CUM_TILE,),
        in_specs=[blk, pl.BlockSpec((1, LANES), lambda i: (0, 0))], out_specs=blk,
        out_shape=jax.ShapeDtypeStruct((t, LANES), f32), scratch_shapes=[pltpu.VMEM((1, LANES), f32)],
        compiler_params=_params(("arbitrary",)),
    )(zfg, bfg)


def _cum_bwd(dcum_q, dcum_k, zfg, bfg, *, name):
    t = zfg.shape[0]
    n = t // CUM_TILE

    def body(d_ref, d2_ref, z_ref, b_ref, o_ref, s_ref, carry):
        i = pl.program_id(0)

        @pl.when(i == 0)
        def _():
            carry[...] = jnp.zeros_like(carry)

        dls = jnp.dot(_tri(CUM_TILE, False), d_ref[...] + d2_ref[...], preferred_element_type=f32,
                      precision=lax.Precision.HIGHEST) + carry[...]
        carry[...] = dls[0:1, :]
        lane = lax.broadcasted_iota(jnp.int32, (CUM_TILE, LANES), 1)
        dfg = jnp.where(lane < HEADS, dls * _sigmoid(-(z_ref[...] + b_ref[...])), 0.0)
        o_ref[...] = dfg
        tot = jnp.sum(dfg, axis=0, keepdims=True)

        @pl.when(i == 0)
        def _():
            s_ref[...] = tot

        @pl.when(i > 0)
        def _():
            s_ref[...] += tot

    blk = pl.BlockSpec((CUM_TILE, LANES), lambda i: (n - 1 - i, 0))
    row = pl.BlockSpec((1, LANES), lambda i: (0, 0))
    return pl.pallas_call(
        body, name=name, grid=(n,), in_specs=[blk, blk, blk, row], out_specs=[blk, row],
        out_shape=[jax.ShapeDtypeStruct((t, LANES), f32), jax.ShapeDtypeStruct((1, LANES), f32)],
        scratch_shapes=[pltpu.VMEM((1, LANES), f32)],
        compiler_params=_params(("arbitrary",)),
    )(dcum_q, dcum_k, zfg, bfg)


ATT_TILE = 512


def _causal(i, j, transposed):
    r = lax.broadcasted_iota(jnp.int32, (ATT_TILE, ATT_TILE), 0)
    c = lax.broadcasted_iota(jnp.int32, (ATT_TILE, ATT_TILE), 1)
    if transposed:
        return (c + i * ATT_TILE) >= (r + j * ATT_TILE)
    return (r + i * ATT_TILE) >= (c + j * ATT_TILE)


ATT_W = HEADS * LANES


def _data_lane(h):
    return HEAD_D * (h % 2)


def _extra_lane(h):
    return HEAD_D - _data_lane(h)


def _split3(x):
    hi = x.astype(bf16)
    rest = x - hi.astype(f32)
    mid = rest.astype(bf16)
    lo = (rest - mid.astype(f32)).astype(bf16)
    return hi, mid, lo


def _three_pieces(x):
    hi, mid, lo = (p.astype(f32) for p in _split3(x))
    return (hi + pltpu.roll(mid, HEADS, axis=1) + pltpu.roll(lo, 2 * HEADS, axis=1)).astype(bf16)


def _move(h, first):
    r = lax.broadcasted_iota(jnp.int32, (LANES, LANES), 0)
    c = lax.broadcasted_iota(jnp.int32, (LANES, LANES), 1)
    hit = functools.reduce(jnp.logical_or, [jnp.logical_and(r == HEADS * q + h, c == first + q) for q in range(3)])
    return jnp.where(hit, 1.0, 0.0).astype(bf16)


def _ones_from(first, rows):
    lane = lax.broadcasted_iota(jnp.int32, (rows, LANES), 1)
    return jnp.where(jnp.logical_and(lane >= first, lane < first + 3), 1.0, 0.0)


def _own_lanes(h, rows):
    lane = lax.broadcasted_iota(jnp.int32, (rows, LANES), 1)
    return (lane < HEAD_D) if h % 2 == 0 else (lane >= HEAD_D)


def _head_values(x):
    lane = lax.broadcasted_iota(jnp.int32, x.shape, 1)
    return jnp.where(lane < HEADS, x, 0.0)


def _attn_prep_fwd(qkv, cum, *, tm, name):
    t = qkv.shape[0]

    def body(q_ref, k_ref, v_ref, c_ref, qa_ref, ka_ref, va_ref):
        c3 = _three_pieces(_head_values(c_ref[...]))
        ones = jnp.ones((tm, LANES), bf16)
        for h in range(HEADS):
            pair = slice(LANES * (h // 2), LANES * (h // 2 + 1))
            hs = slice(LANES * h, LANES * (h + 1))
            base, own = _extra_lane(h), _own_lanes(h, tm)
            eq = jnp.dot(c3, _move(h, base), preferred_element_type=f32) + _ones_from(base + 3, tm)
            ek = _ones_from(base, tm) - jnp.dot(c3, _move(h, base + 3), preferred_element_type=f32)
            qa_ref[:, hs] = jnp.where(own, q_ref[:, pair] * ATT_SCALE, eq.astype(bf16))
            ka_ref[:, hs] = jnp.where(own, k_ref[:, pair], ek.astype(bf16))
            va_ref[:, hs] = jnp.where(own, v_ref[:, pair], ones)

    wide = pl.BlockSpec((tm, ATT_W), lambda i: (i, 0))
    out = jax.ShapeDtypeStruct((t, ATT_W), bf16)
    return pl.pallas_call(
        body, name=name, grid=(t // tm,),
        in_specs=[pl.BlockSpec((tm, FOX_W), lambda i: (i, 0)), pl.BlockSpec((tm, FOX_W), lambda i: (i, 1)),
                  pl.BlockSpec((tm, FOX_W), lambda i: (i, 2)), pl.BlockSpec((tm, LANES), lambda i: (i, 0))],
        out_specs=[wide] * 3, out_shape=[out] * 3, compiler_params=_params(("arbitrary",)),
    )(qkv, qkv, qkv, cum)


def _attn_prep_bwd(qkv, cum, lse, dmix, o, *, tm, name):
    t = qkv.shape[0]

    def body(q_ref, c_ref, l_ref, do_ref, o_ref, qa_ref, da_ref):
        b3 = _three_pieces(_head_values(c_ref[...] - l_ref[...]))
        r = lax.broadcasted_iota(jnp.int32, (FOX_W, LANES), 0)
        c = lax.broadcasted_iota(jnp.int32, (FOX_W, LANES), 1)
        per_head = jnp.where(r // HEAD_D == c, 1.0, 0.0).astype(bf16)
        delta = sum(jnp.dot(p, per_head, preferred_element_type=f32) for p in _split3(do_ref[...] * o_ref[...]))
        d3 = _three_pieces(delta)
        for h in range(HEADS):
            pair = slice(LANES * (h // 2), LANES * (h // 2 + 1))
            hs = slice(LANES * h, LANES * (h + 1))
            base, own = _extra_lane(h), _own_lanes(h, tm)
            eq = jnp.dot(b3, _move(h, base), preferred_element_type=f32) + _ones_from(base + 3, tm)
            ed = -jnp.dot(d3, _move(h, base), preferred_element_type=f32)
            qa_ref[:, hs] = jnp.where(own, q_ref[:, pair] * ATT_SCALE, eq.astype(bf16))
            da_ref[:, hs] = jnp.where(own, do_ref[:, pair].astype(bf16), ed.astype(bf16))

    wide = pl.BlockSpec((tm, ATT_W), lambda i: (i, 0))
    half = pl.BlockSpec((tm, FOX_W), lambda i: (i, 0))
    col = pl.BlockSpec((tm, LANES), lambda i: (i, 0))
    out = jax.ShapeDtypeStruct((t, ATT_W), bf16)
    return pl.pallas_call(
        body, name=name, grid=(t // tm,), in_specs=[half, col, col, half, half],
        out_specs=[wide] * 2, out_shape=[out] * 2, compiler_params=_params(("arbitrary",)),
    )(qkv, cum, lse, dmix, o)


def _attn_fwd2(q_aug, k_aug, v_aug, *, name, host=None):
    t = q_aug.shape[0]
    n = t // ATT_TILE
    tq = ATT_TILE

    def body(q_ref, k_ref, v_ref, o_ref, lse_ref, acc, m_s):
        i = pl.program_id(0)
        j = pl.program_id(1)

        @pl.when(j == 0)
        def _():
            acc[...] = jnp.zeros_like(acc)
            m_s[...] = jnp.full_like(m_s, NEG_BIG)

        def block(masked):
            mask = _causal(i, j, False) if masked else None
            for h in range(HEADS):
                hs = slice(LANES * h, LANES * (h + 1))
                s = lax.dot_general(q_ref[:, hs], k_ref[:, hs], _NT, preferred_element_type=f32)
                if masked:
                    s = jnp.where(mask, s, NEG_BIG)
                blocks = [s[:, LANES * b:LANES * (b + 1)] for b in range(tq // LANES)]
                m_old = m_s[h]
                m_new = jnp.maximum(m_old, jnp.broadcast_to(
                    jnp.max(functools.reduce(jnp.maximum, blocks), axis=-1, keepdims=True), (tq, LANES)))
                p = jnp.concatenate([jnp.exp(b - m_new) for b in blocks], axis=1).astype(bf16)
                acc[h] = jnp.exp(m_old - m_new) * acc[h] + jnp.dot(p, v_ref[:, hs], preferred_element_type=f32)
                m_s[h] = m_new

        @pl.when(j < i)
        def _():
            block(False)

        @pl.when(j == i)
        def _():
            block(True)
            lse_ref[...] = jnp.zeros_like(lse_ref)
            for h in range(HEADS):
                a = acc[h]
                l = a[:, _extra_lane(h):_extra_lane(h) + 1]
                o_ref[:, HEAD_D * h:HEAD_D * (h + 1)] = a[:, _data_lane(h):_data_lane(h) + HEAD_D] / l
                lse_ref[:, h:h + 1] = m_s[h][:, 0:1] + jnp.log(l)

    kv = pl.BlockSpec((tq, ATT_W), lambda i, j: (jnp.minimum(i, j), 0))
    return _hosted_call(
        host, body, name=name, grid=(n, n),
        in_specs=[pl.BlockSpec((tq, ATT_W), lambda i, j: (i, 0)), kv, kv],
        out_specs=[pl.BlockSpec((tq, FOX_W), lambda i, j: (i, 0)), pl.BlockSpec((tq, LANES), lambda i, j: (i, 0))],
        out_shape=[jax.ShapeDtypeStruct((t, FOX_W), f32), jax.ShapeDtypeStruct((t, LANES), f32)],
        scratch_shapes=[pltpu.VMEM((HEADS, tq, LANES), f32), pltpu.VMEM((HEADS, tq, LANES), f32)],
        compiler_params=_params(("arbitrary", "arbitrary")),
    )(q_aug, k_aug, v_aug)


def _attn_bwd(qb_aug, k_aug, v_aug, do_aug, *, name, host=None):
    t = qb_aug.shape[0]
    n = t // ATT_TILE
    tk = ATT_TILE

    def body(q_ref, k_ref, v_ref, do_ref, dq_ref, dcq_ref, dk_ref, dv_ref, dck_ref, dk_acc, dv_acc, dq_all):
        j = pl.program_id(0)
        i = pl.program_id(1)

        @pl.when(jnp.logical_and(i == 0, j == 0))
        def _():
            dq_all[...] = jnp.zeros_like(dq_all)

        @pl.when(i == 0)
        def _():
            dk_acc[...] = jnp.zeros_like(dk_acc)
            dv_acc[...] = jnp.zeros_like(dv_acc)

        def block(masked):
            mask = _causal(i, j, True) if masked else None
            for h in range(HEADS):
                hs = slice(LANES * h, LANES * (h + 1))
                qh = q_ref[:, hs]
                doh = do_ref[:, hs]
                kh = k_ref[:, hs]
                s_t = lax.dot_general(kh, qh, _NT, preferred_element_type=f32)
                if masked:
                    s_t = jnp.where(mask, s_t, NEG_BIG)
                p_t = jnp.exp(s_t)
                dv_acc[h] += jnp.dot(p_t.astype(bf16), doh, preferred_element_type=f32)
                dp_t = lax.dot_general(v_ref[:, hs], doh, _NT, preferred_element_type=f32)
                ds_t = (p_t * dp_t).astype(bf16)
                dk_acc[h] += jnp.dot(ds_t, qh, preferred_element_type=f32)
                dq_all[i, h] += lax.dot_general(ds_t, kh, _TN, preferred_element_type=f32)

        @pl.when(i > j)
        def _():
            block(False)

        @pl.when(i == j)
        def _():
            block(True)
            dcq_ref[...] = jnp.zeros_like(dcq_ref)
            for h in range(HEADS):
                a = dq_all[j, h]
                dq_ref[:, HEAD_D * h:HEAD_D * (h + 1)] = (
                    a[:, _data_lane(h):_data_lane(h) + HEAD_D] * ATT_SCALE).astype(bf16)
                dcq_ref[:, h:h + 1] = a[:, _extra_lane(h):_extra_lane(h) + 1]

        @pl.when(i == n - 1)
        def _():
            dck_ref[...] = jnp.zeros_like(dck_ref)
            for h in range(HEADS):
                a = dk_acc[h]
                cols = slice(_data_lane(h), _data_lane(h) + HEAD_D)
                dk_ref[:, HEAD_D * h:HEAD_D * (h + 1)] = a[:, cols].astype(bf16)
                dv_ref[:, HEAD_D * h:HEAD_D * (h + 1)] = dv_acc[h][:, cols].astype(bf16)
                dck_ref[:, h:h + 1] = -a[:, _extra_lane(h) + 3:_extra_lane(h) + 4]

    own = pl.BlockSpec((tk, ATT_W), lambda j, i: (j, 0))
    qs = pl.BlockSpec((tk, ATT_W), lambda j, i: (jnp.maximum(i, j), 0))
    half = pl.BlockSpec((tk, FOX_W), lambda j, i: (j, 0))
    col = pl.BlockSpec((tk, LANES), lambda j, i: (j, 0))
    return _hosted_call(
        host, body, name=name, grid=(n, n), in_specs=[qs, own, own, qs],
        out_specs=[half, col, half, half, col],
        out_shape=[jax.ShapeDtypeStruct((t, FOX_W), bf16), jax.ShapeDtypeStruct((t, LANES), f32),
                   jax.ShapeDtypeStruct((t, FOX_W), bf16), jax.ShapeDtypeStruct((t, FOX_W), bf16),
                   jax.ShapeDtypeStruct((t, LANES), f32)],
        scratch_shapes=[pltpu.VMEM((HEADS, tk, LANES), f32), pltpu.VMEM((HEADS, tk, LANES), f32),
                        pltpu.VMEM((n, HEADS, tk, LANES), f32)],
        compiler_params=_params(("arbitrary", "arbitrary")),
    )(qb_aug, k_aug, v_aug, do_aug)


LRU_CHUNK = 64
LRU_G = 256
SUB = 8


def _row_ids(n):
    return lax.broadcasted_iota(jnp.int32, (n, LRU_G), 0)


def _shift_rows_down(ext, s):
    return pltpu.roll(ext, s, axis=0)[SUB:, :]


def _shift_rows_up(ext, s, n):
    return pltpu.roll(ext, ext.shape[0] - s, axis=0)[:n, :]


def _lru_gates(u, wa_ref, ba_ref, wx_ref, bx_ref, sp):
    ub = u.astype(bf16)
    r = _sigmoid(jnp.dot(ub, wa_ref[...], preferred_element_type=f32) + ba_ref[...])
    gi = _sigmoid(jnp.dot(ub, wx_ref[...], preferred_element_type=f32) + bx_ref[...])
    log_a = -LRU_C * r * sp
    a = jnp.exp(log_a)
    s = jnp.sqrt(_one_minus_exp(2.0 * log_a))
    return r, gi, a, s


def _conv_window(lx_ref, r0, ci):
    cur = lx_ref[pl.ds(r0, LRU_CHUNK), :]
    p0 = pl.multiple_of(jnp.maximum(r0 - SUB, 0), SUB)
    prev = jnp.where(ci > 0, lx_ref[pl.ds(p0, SUB), :], 0.0)
    return cur, jnp.concatenate([prev, cur], axis=0)


def _lru_fwd(zl, conv_w, conv_b, wa, ba, wx, bx, lam, *, name, host=None):
    t = zl.shape[0]
    n_chunk = t // LRU_CHUNK

    def body(lx_ref, lg_ref, cw_ref, cb_ref, wa_ref, ba_ref, wx_ref, bx_ref, lam_ref, u_ref, h_ref, y_ref):
        sp = _softplus(-lam_ref[...])
        rows = _row_ids(SUB)

        def chunk(ci, hc):
            r0 = pl.multiple_of(ci * LRU_CHUNK, LRU_CHUNK)
            cur, ext = _conv_window(lx_ref, r0, ci)
            u = cb_ref[...] + cw_ref[3:4, :] * cur
            for k in range(3):
                u = u + cw_ref[k:k + 1, :] * _shift_rows_down(ext, 3 - k)
            r, gi, a, s = _lru_gates(u, wa_ref, ba_ref, wx_ref, bx_ref, sp)
            b = s * (gi * u)
            tiles = []
            for q in range(LRU_CHUNK // SUB):
                ta = a[SUB * q:SUB * (q + 1), :]
                tb = b[SUB * q:SUB * (q + 1), :]
                for d in (1, 2, 4):
                    a_sh = jnp.where(rows >= d, pltpu.roll(ta, d, axis=0), 1.0)
                    b_sh = jnp.where(rows >= d, pltpu.roll(tb, d, axis=0), 0.0)
                    tb = ta * b_sh + tb
                    ta = ta * a_sh
                hq = tb + ta * hc
                hc = hq[SUB - 1:SUB, :]
                tiles.append(hq)
            h = jnp.concatenate(tiles, axis=0)
            u_ref[pl.ds(r0, LRU_CHUNK), :] = u
            h_ref[pl.ds(r0, LRU_CHUNK), :] = h
            gel, _ = _gelu_and_grad(lg_ref[pl.ds(r0, LRU_CHUNK), :])
            y_ref[pl.ds(r0, LRU_CHUNK), :] = gel * h
            return hc

        lax.fori_loop(0, n_chunk, chunk, jnp.zeros((1, LRU_G), f32))

    seq = lambda cb: pl.BlockSpec((t, LRU_G), lambda c, cb=cb: (0, c + cb))
    rowc = pl.BlockSpec((1, LRU_G), lambda c: (0, c))
    diag = pl.BlockSpec((LRU_G, LRU_G), lambda c: (c, c))
    out = jax.ShapeDtypeStruct((t, LRU_W), f32)
    return _hosted_call(
        host, body, name=name, grid=(LRU_W // LRU_G,),
        in_specs=[seq(0), seq(LRU_W // LRU_G), pl.BlockSpec((4, LRU_G), lambda c: (0, c)),
                  rowc, diag, rowc, diag, rowc, rowc],
        out_specs=[seq(0)] * 3, out_shape=[out] * 3,
        compiler_params=_params(("arbitrary",)),
    )(zl, zl, conv_w, conv_b, wa, ba, wx, bx, lam)


def _lru_bwd(dmix, zl, u_all, h_all, conv_w, wa, ba, wx, bx, lam, *, name, host=None):
    t = zl.shape[0]
    n_chunk = t // LRU_CHUNK

    def body(dy_ref, lx_ref, lg_ref, u_ref, h_ref, cw_ref, wa_ref, ba_ref, wx_ref, bx_ref, lam_ref,
             dlx_ref, dlg_ref, dcw_ref, dcb_ref, dba_ref, dbx_ref, dlam_ref, dwa_ref, dwx_ref, dpr_s, dpx_s):
        lam_v = lam_ref[...]
        sp = _softplus(-lam_v)
        rows = _row_ids(SUB)
        rows_c = _row_ids(LRU_CHUNK)
        zero_row = jnp.zeros((1, LRU_G), f32)

        def chunk(step, carry):
            dh_c, a_next0, du_next, dsp, dba, dbx, dcb, dw0, dw1, dw2, dw3 = carry
            ci = n_chunk - 1 - step
            r0 = pl.multiple_of(ci * LRU_CHUNK, LRU_CHUNK)
            sl = pl.ds(r0, LRU_CHUNK)
            u = u_ref[sl, :]
            r, gi, a, s = _lru_gates(u, wa_ref, ba_ref, wx_ref, bx_ref, sp)
            h = h_ref[sl, :]
            p0 = pl.multiple_of(jnp.maximum(r0 - SUB, 0), SUB)
            h_before = jnp.where(ci > 0, h_ref[pl.ds(p0, SUB), :], 0.0)[SUB - 1:SUB, :]
            h_prev = jnp.where(rows_c == 0, h_before, pltpu.roll(h, 1, axis=0))
            gel, dgel = _gelu_and_grad(lg_ref[sl, :])
            dy = dy_ref[sl, :]
            dlg_ref[sl, :] = (dy * h * dgel).astype(bf16)
            g_in = dy * gel
            a_next = jnp.where(rows_c == LRU_CHUNK - 1, a_next0, pltpu.roll(a, LRU_CHUNK - 1, axis=0))
            tiles = [None] * (LRU_CHUNK // SUB)
            for q in reversed(range(LRU_CHUNK // SUB)):
                ta = a_next[SUB * q:SUB * (q + 1), :]
                tb = g_in[SUB * q:SUB * (q + 1), :]
                for d in (1, 2, 4):
                    a_sh = jnp.where(rows < SUB - d, pltpu.roll(ta, SUB - d, axis=0), 1.0)
                    b_sh = jnp.where(rows < SUB - d, pltpu.roll(tb, SUB - d, axis=0), 0.0)
                    tb = ta * b_sh + tb
                    ta = ta * a_sh
                dhq = tb + ta * dh_c
                dh_c = dhq[0:1, :]
                tiles[q] = dhq
            dh = jnp.concatenate(tiles, axis=0)
            da = dh * h_prev
            ds = dh * gi * u
            dgi = dh * s * u
            du = dh * s * gi
            dlog_a = da * a - ds * (a * a) / s
            dr = dlog_a * (-LRU_C * sp)
            dsp = dsp + jnp.sum(dlog_a * (-LRU_C * r), axis=0, keepdims=True)
            dpr = dr * r * (1.0 - r)
            dpx = dgi * gi * (1.0 - gi)
            dprb = dpr.astype(bf16)
            dpxb = dpx.astype(bf16)
            dpr_s[sl, :] = dprb
            dpx_s[sl, :] = dpxb
            du = du + (lax.dot_general(dprb, wa_ref[...], _NT, preferred_element_type=f32)
                       + lax.dot_general(dpxb, wx_ref[...], _NT, preferred_element_type=f32))
            dba = dba + jnp.sum(dpr, axis=0, keepdims=True)
            dbx = dbx + jnp.sum(dpx, axis=0, keepdims=True)
            dcb = dcb + jnp.sum(du, axis=0, keepdims=True)
            du_ext = jnp.concatenate([du, du_next], axis=0)
            dlx = cw_ref[3:4, :] * du
            for k in range(3):
                dlx = dlx + cw_ref[k:k + 1, :] * _shift_rows_up(du_ext, 3 - k, LRU_CHUNK)
            dlx_ref[sl, :] = dlx.astype(bf16)
            cur, ext = _conv_window(lx_ref, r0, ci)
            dws = [dw0, dw1, dw2, dw3 + jnp.sum(du * cur, axis=0, keepdims=True)]
            for k in range(3):
                dws[k] = dws[k] + jnp.sum(du * _shift_rows_down(ext, 3 - k), axis=0, keepdims=True)
            return (dh_c, a[0:1, :], du[0:SUB, :], dsp, dba, dbx, dcb, dws[0], dws[1], dws[2], dws[3])

        init = (zero_row, zero_row, jnp.zeros((SUB, LRU_G), f32)) + (zero_row,) * 8
        out = lax.fori_loop(0, n_chunk, chunk, init)
        _, _, _, dsp, dba, dbx, dcb, dw0, dw1, dw2, dw3 = out
        dlam_ref[...] = dsp * (-_sigmoid(-lam_v))
        dba_ref[...] = dba
        dbx_ref[...] = dbx
        dcb_ref[...] = dcb
        dcw_ref[...] = jnp.concatenate([dw0, dw1, dw2, dw3], axis=0)
        ub = u_ref[...].astype(bf16)
        dwa_ref[...] = lax.dot_general(ub, dpr_s[...], _TN, preferred_element_type=f32)
        dwx_ref[...] = lax.dot_general(ub, dpx_s[...], _TN, preferred_element_type=f32)

    seq = lambda cb: pl.BlockSpec((t, LRU_G), lambda c, cb=cb: (0, c + cb))
    rowc = pl.BlockSpec((1, LRU_G), lambda c: (0, c))
    diag = pl.BlockSpec((LRU_G, LRU_G), lambda c: (c, c))
    gate_out = pl.BlockSpec((None, LRU_G, LRU_G), lambda c: (c, 0, 0))
    row_shape = jax.ShapeDtypeStruct((1, LRU_W), f32)
    return _hosted_call(
        host, body, name=name, grid=(LRU_W // LRU_G,),
        in_specs=[seq(LRU_W // LRU_G), seq(0), seq(LRU_W // LRU_G), seq(0), seq(0),
                  pl.BlockSpec((4, LRU_G), lambda c: (0, c)),
                  diag, rowc, diag, rowc, rowc],
        out_specs=[seq(0), seq(0), pl.BlockSpec((4, LRU_G), lambda c: (0, c)), rowc, rowc, rowc, rowc,
                   gate_out, gate_out],
        out_shape=[jax.ShapeDtypeStruct((t, LRU_W), bf16)] * 2
        + [jax.ShapeDtypeStruct((4, LRU_W), f32)] + [row_shape] * 4
        + [jax.ShapeDtypeStruct((LRU_W // LRU_G, LRU_G, LRU_G), f32)] * 2,
        scratch_shapes=[pltpu.VMEM((t, LRU_G), bf16), pltpu.VMEM((t, LRU_G), bf16)],
        compiler_params=_params(("arbitrary",)),
    )(dmix, zl, zl, u_all, h_all, conv_w, wa, ba, wx, bx, lam)


def _pack_rows(a):
    flat = a.reshape(-1)
    rows = -(-flat.shape[0] // LANES)
    return jnp.pad(flat, (0, rows * LANES - flat.shape[0])).reshape(rows, LANES)


W_IN_PAD = 21 * LANES


def _w_in_join(blocks, *, name):
    tm = 256

    def body(b_ref, o_ref):
        o_ref[:, IN_COLS:] = jnp.zeros((tm, W_IN_PAD - IN_COLS), bf16)
        for q in range(N_DEV):
            o_ref[:, IN_SHARD * q:IN_SHARD * (q + 1)] = b_ref[q]

    return pl.pallas_call(
        body, name=name, grid=(D_MODEL // tm,),
        in_specs=[pl.BlockSpec((N_DEV, tm, IN_SHARD), lambda i: (0, i, 0))],
        out_specs=pl.BlockSpec((tm, W_IN_PAD), lambda i: (i, 0)),
        out_shape=jax.ShapeDtypeStruct((D_MODEL, W_IN_PAD), bf16), compiler_params=_params(("arbitrary",)),
    )(blocks)


def _w_in_split(main, fg, *, name):
    tm = 256
    n_main = main.shape[0]

    def body(m_ref, f_ref, o_ref):
        full = jnp.concatenate([m_ref[n] for n in range(n_main)] + [f_ref[...]], axis=1)
        for q in range(N_DEV):
            o_ref[q] = full[:, IN_SHARD * q:IN_SHARD * (q + 1)]

    return pl.pallas_call(
        body, name=name, grid=(D_MODEL // tm,),
        in_specs=[pl.BlockSpec((n_main, tm, 512), lambda i: (0, i, 0)), pl.BlockSpec((tm, LANES), lambda i: (i, 0))],
        out_specs=pl.BlockSpec((N_DEV, tm, IN_SHARD), lambda i: (0, i, 0)),
        out_shape=jax.ShapeDtypeStruct((N_DEV, D_MODEL, IN_SHARD), bf16), compiler_params=_params(("arbitrary",)),
    )(main, fg)


def _block_diag(w):
    eye = jnp.eye(HEADS, dtype=w.dtype)
    return jnp.einsum("hij,hk->hikj", w, eye).reshape(LRU_W, LRU_W)


def _diag_blocks(dw):
    per = dw.shape[1] // HEAD_D
    blocks = [dw[:, HEAD_D * b:HEAD_D * (b + 1), HEAD_D * b:HEAD_D * (b + 1)] for b in range(per)]
    return jnp.stack(blocks, axis=1).reshape(HEADS, HEAD_D, HEAD_D)


def _local_step(x, target, sent, small, *, tm=512, tm_ffn=1024):
    ln1 = (small["ln1_g"], small["ln1_b"])
    ln2 = (small["ln2_g"], small["ln2_b"])
    ln3 = (small["ln3_g"], small["ln3_b"])

    xh1, rs1, hg1, hu1, wg1, wu1, wd1, w_in_g, w_out_g, conv_w_g = _ffn1_fwd_gathering(
        x, (sent["ffn1_w_gate"], sent["ffn1_w_up"], sent["ffn1_w_down"]),
        _Exchange([sent["w_in"], sent["w_out"], sent["conv_w"]], gather=True), tm=tm_ffn, name="ffn1_fwd")
    w_in = _w_in_join(w_in_g, name="w_in_join")
    w_out = w_out_g.reshape(D_MODEL, D_MODEL)
    conv_w = conv_w_g.transpose(1, 0, 2).reshape(4, LRU_W)
    qkv, zl, zfg = _in_proj(xh1, ln1[0], ln1[1], w_in, tm=tm_ffn, name="in_proj")
    bfg = jnp.pad(small["b_forget"], ((0, 0), (0, LANES - HEADS)))
    cum = _cum_fwd(zfg, bfg, name="cum_fwd")
    q_aug, k_aug, v_aug = _attn_prep_fwd(qkv, cum, tm=tm_ffn, name="attn_prep_fwd")
    o, lse, wg2, wu2 = _attn_fwd2(
        q_aug, k_aug, v_aug, name="attn_fwd",
        host=_Hosts(_Exchange([sent["ffn2_w_gate"]], gather=True),
                    _Exchange([sent["ffn2_w_up"]], gather=True, direct=True)))
    wa_bd = _block_diag(small["rg_wa"]).astype(bf16)
    wx_bd = _block_diag(small["rg_wx"]).astype(bf16)
    ba = small["rg_ba"].reshape(1, LRU_W)
    bx = small["rg_bx"].reshape(1, LRU_W)
    u, h, lru, wd2 = _lru_fwd(zl, conv_w, small["conv_b"], wa_bd, ba, wx_bd, bx, small["lru_lambda"],
                              name="lru_fwd", host=_Exchange([sent["ffn2_w_down"]], gather=True))
    xh2, rs2 = _mmln([(o, 0, FOX_W, w_out, 0, D_MODEL, "nn"), (lru, 0, LRU_W, w_out, 1, D_MODEL, "nn")],
                     tm=tm_ffn, name="mix_fwd", resid=("affine", xh1) + ln1, resid_scale=ALPHA, epi="ln_fwd")
    dpre3, sq_rows, g_ln3g, g_ln3b, hg2, hu2 = _ffn_fwd_loss(
        xh2, ln2[0], ln2[1], wg2, wu2, wd2, ln3[0], ln3[1], target, tm=tm_ffn, name="ffn2_fwd_loss")

    dpre2, g_ln2g, g_ln2b, dhg2, dhu2, a2 = _ffn_bwd(dpre3, hg2, hu2, wg2, wu2, wd2,
                                                     (xh2, rs2, ln2[0]), tm=tm_ffn, name="ffn2_bwd")
    wgrad = dict(out_dtype=bf16, tm=D_MODEL, mb=1, tn=FF_TILE, nb=4, tk=512, pair=True)
    wdgrad = dict(out_dtype=bf16, tm=512, mb=4, tn=D_MODEL, nb=1, tk=512, out_scale=0.5, pair=True)
    between_chips = functools.partial(_Exchange, gather=False, chips=True)
    g_wg2 = _mm_tn(xh2, dhg2, name="g_wg2", affine=ln2, **wgrad)
    g_wu2 = _mm_tn(xh2, dhu2, name="g_wu2", affine=ln2, **wgrad)
    g_wd2 = _mm_tn(a2, dpre3, name="g_wd2", **wdgrad)

    dmix = _mmln([(dpre2, 0, D_MODEL, w_out, 0, D_MODEL, "nt")], tm=tm_ffn, name="dmix_bwd")
    g_wout_a = _mm(o, dpre2, mode="tn", out_dtype=bf16, tm=512, tn=D_MODEL, tk=512, name="g_wout_fox")
    g_wout_b = _mm(lru, dpre2, mode="tn", out_dtype=bf16, tm=512, tn=D_MODEL, tk=512, name="g_wout_lru")
    g_wout_blocked = jnp.concatenate([g_wout_a, g_wout_b], axis=0).reshape(N_DEV, D_MODEL // N_DEV, D_MODEL)
    dlx, dlg, g_cw, g_cb, g_ba, g_bx, g_lam, g_wa4, g_wx4, p_wg2, p_wout = _lru_bwd(
        dmix, zl, u, h, conv_w, wa_bd, ba, wx_bd, bx, small["lru_lambda"], name="lru_bwd",
        host=_Hosts(between_chips([g_wg2]), _Exchange([g_wout_blocked], gather=False)))
    qb_aug, do_aug = _attn_prep_bwd(qkv, cum, lse, dmix, o, tm=tm_ffn, name="attn_prep_bwd")
    dq, dcum_q, dk, dv, dcum_k, p_wu2, p_wd2 = _attn_bwd(qb_aug, k_aug, v_aug, do_aug, name="attn_bwd",
                                                         host=between_chips([g_wu2, g_wd2]))
    dfg, g_bf = _cum_bwd(dcum_q, dcum_k, zfg, bfg, name="cum_bwd")

    dz = [(dq, 0, 512), (dk, 1, 512), (dv, 2, 512), (dlx, 3, 512), (dlg, 4, 512), (dfg, 20, LANES)]
    dpre1, g_ln1g, g_ln1b = _mmln(
        [(arr, 0, w, w_in, cb, w, "nt") for (arr, cb, w) in dz],
        tm=tm_ffn, name="dx1_bwd", resid=("plain", dpre2), resid_scale=ALPHA, epi="ln_bwd", ln=(xh1, rs1, ln1[0]))
    g_win_main = _mm_tn(xh1, [arr for arr, _, _ in dz[:5]], out_dtype=bf16, tm=D_MODEL, mb=1, tn=512, nb=5, tk=512,
                        name="g_win", affine=ln1, out_blocked=True)
    g_win_fg = _mm(xh1, dfg, mode="tn", out_dtype=bf16, tm=D_MODEL, tn=LANES, tk=512, name="g_win_fg", affine=ln1)
    g_win_blocked = _w_in_split(g_win_main, g_win_fg, name="w_in_split")
    dhg1, dhu1, a1, p_win = _ffn_bwd_act(dpre1, hg1, hu1, wd1, tm=tm_ffn, name="ffn1_bwd_act",
                                         host=_Exchange([g_win_blocked], gather=False))
    small_g = {
        "ln1_g": g_ln1g, "ln1_b": g_ln1b, "b_forget": g_bf[:, :HEADS], "conv_w": g_cw, "conv_b": g_cb,
        "rg_wa": _diag_blocks(g_wa4), "rg_ba": g_ba.reshape(HEADS, HEAD_D),
        "rg_wx": _diag_blocks(g_wx4), "rg_bx": g_bx.reshape(HEADS, HEAD_D), "lru_lambda": g_lam,
        "ln2_g": g_ln2g, "ln2_b": g_ln2b, "ln3_g": g_ln3g, "ln3_b": g_ln3b,
    }
    small_g["loss"] = (0.5 / D_MODEL) * jnp.sum(sq_rows, keepdims=True)
    pieces = [_pack_rows(small_g[n]) for n in PACKED]
    packed = jnp.concatenate(pieces + [jnp.zeros((PACK_ROWS - sum(p.shape[0] for p in pieces), LANES), f32)])
    g_wg1, all_packed = _mm_tn(x, dhg1, name="g_wg1", host=_Exchange([packed], gather=True), **wgrad)
    g_wu1, p_wg1 = _mm_tn(x, dhu1, name="g_wu1", host=between_chips([g_wg1]), **wgrad)
    g_wd1, p_wu1 = _mm_tn(a1, dpre1, name="g_wd1", host=between_chips([g_wu1]), **wdgrad)
    grad_x, p_wd1 = _ffn_bwd_dx(dpre1, dhg1, dhu1, wg1, wu1, tm=tm_ffn, name="ffn1_bwd_dx",
                                host=between_chips([g_wd1]))
    parts = {
        "ffn1_w_gate": p_wg1, "ffn1_w_up": p_wu1, "ffn1_w_down": p_wd1, "w_in": p_win, "w_out": p_wout,
        "ffn2_w_gate": p_wg2, "ffn2_w_up": p_wu2, "ffn2_w_down": p_wd2,
    }
    return grad_x, parts, all_packed, {n: small_g[n].shape for n in PACKED}


def _adam_math(w, g, m, v):
    m2 = ADAM_B1 * m + (1.0 - ADAM_B1) * g
    v2 = ADAM_B2 * v + (1.0 - ADAM_B2) * (g * g)
    m_hat = m2 / (1.0 - ADAM_B1 ** ADAM_STEP)
    v_hat = v2 / (1.0 - ADAM_B2 ** ADAM_STEP)
    delta = -ADAM_LR * (m_hat / (jnp.sqrt(v_hat) + ADAM_EPS) + ADAM_WD * w)
    return delta, m2, v2


ADAM_TILE_ELEMS = 128 * 1024


def _adamw_big(items, *, name):
    _, r, c = items[0][1].shape
    n_parts = items[0][0].shape[0]
    n_items = len(items)
    assert all(it[1].shape == (1, r, c) and it[0].shape == (n_parts, r, c) for it in items), name
    tr = max(d for d in range(8, r + 1, 8) if r % d == 0 and d * c <= ADAM_TILE_ELEMS)

    def body(*refs):
        ins, outs = refs[:4 * n_items], refs[4 * n_items:]
        for k in range(n_items):
            p_ref, w_ref, m_ref, v_ref = ins[4 * k:4 * k + 4]
            g = p_ref[0].astype(f32)
            for q in range(1, n_parts):
                g = g + p_ref[q].astype(f32)
            d, m2, v2 = _adam_math(w_ref[...], g, m_ref[...], v_ref[...])
            for o_ref, val in zip(outs[4 * k:4 * k + 4], (g, d, m2, v2)):
                o_ref[...] = val

    blk = pl.BlockSpec((None, tr, c), lambda i: (0, i, 0))
    res = pl.pallas_call(
        body, name=name, grid=(r // tr,),
        in_specs=[pl.BlockSpec((n_parts, tr, c), lambda i: (0, i, 0)), blk, blk, blk] * n_items,
        out_specs=[blk] * (4 * n_items), out_shape=[jax.ShapeDtypeStruct((1, r, c), f32)] * (4 * n_items),
        compiler_params=_params(("arbitrary",)),
    )(*[a for it in items for a in it])
    return [res[4 * k:4 * k + 4] for k in range(n_items)]


def _adamw_small(items, *, name):
    n = len(items)

    def body(*refs):
        ins, outs = refs[:4 * n], refs[4 * n:]
        for k in range(n):
            g, w, m, v = (ins[4 * k + q][...] for q in range(4))
            d, m2, v2 = _adam_math(w, g, m, v)
            outs[3 * k][...] = d
            outs[3 * k + 1][...] = m2
            outs[3 * k + 2][...] = v2

    vm = pl.BlockSpec(memory_space=pltpu.VMEM)
    flat = [a for item in items for a in item]
    out_shape = [jax.ShapeDtypeStruct(item[1].shape, f32) for item in items for _ in range(3)]
    return pl.pallas_call(
        body, name=name, in_specs=[vm] * (4 * n), out_specs=[vm] * (3 * n), out_shape=out_shape,
    )(*flat)


def _sum_parts(parts, *, name):
    def body(p_ref, o_ref):
        acc = p_ref[0]
        for q in range(1, N_DEV):
            acc = acc + p_ref[q]
        o_ref[...] = acc

    vm = pl.BlockSpec(memory_space=pltpu.VMEM)
    return pl.pallas_call(
        body, name=name, in_specs=[vm], out_specs=vm, out_shape=jax.ShapeDtypeStruct(parts.shape[1:], f32),
    )(parts)


WEIGHTS = ["ffn1_w_gate", "ffn1_w_up", "ffn1_w_down", "ln1_g", "ln1_b", "w_in", "b_forget", "conv_w", "conv_b",
           "rg_wa", "rg_ba", "rg_wx", "rg_bx", "lru_lambda", "w_out", "ln2_g", "ln2_b",
           "ffn2_w_gate", "ffn2_w_up", "ffn2_w_down", "ln3_g", "ln3_b"]
BIG = ["ffn1_w_gate", "ffn1_w_up", "ffn1_w_down", "w_in", "w_out", "ffn2_w_gate", "ffn2_w_up", "ffn2_w_down"]
PACKED = ["ln1_g", "ln1_b", "ln2_g", "ln2_b", "ln3_g", "ln3_b", "conv_b", "rg_ba", "rg_bx", "lru_lambda",
          "conv_w", "rg_wa", "rg_wx", "b_forget", "loss"]
PACK_ROWS = 600


def _two_d(a):
    return a.reshape((-1, a.shape[-1]))


def _transport(a):
    return _two_d(a)


def kernel(x, ffn1_w_gate, ffn1_w_up, ffn1_w_down, ln1_g, ln1_b, w_in, b_forget, conv_w, conv_b, rg_wa, rg_ba, rg_wx, rg_bx, lru_lambda, w_out, ln2_g, ln2_b, ffn2_w_gate, ffn2_w_up, ffn2_w_down, ln3_g, ln3_b, loss_target, m_ffn1_w_gate, m_ffn1_w_up, m_ffn1_w_down, m_ln1_g, m_ln1_b, m_w_in, m_b_forget, m_conv_w, m_conv_b, m_rg_wa, m_rg_ba, m_rg_wx, m_rg_bx, m_lru_lambda, m_w_out, m_ln2_g, m_ln2_b, m_ffn2_w_gate, m_ffn2_w_up, m_ffn2_w_down, m_ln3_g, m_ln3_b, v_ffn1_w_gate, v_ffn1_w_up, v_ffn1_w_down, v_ln1_g, v_ln1_b, v_w_in, v_b_forget, v_conv_w, v_conv_b, v_rg_wa, v_rg_ba, v_rg_wx, v_rg_bx, v_lru_lambda, v_w_out, v_ln2_g, v_ln2_b, v_ffn2_w_gate, v_ffn2_w_up, v_ffn2_w_down, v_ln3_g, v_ln3_b):
    w_args = (ffn1_w_gate, ffn1_w_up, ffn1_w_down, ln1_g, ln1_b, w_in, b_forget, conv_w, conv_b, rg_wa, rg_ba, rg_wx, rg_bx, lru_lambda, w_out, ln2_g, ln2_b, ffn2_w_gate, ffn2_w_up, ffn2_w_down, ln3_g, ln3_b)
    m_args = (m_ffn1_w_gate, m_ffn1_w_up, m_ffn1_w_down, m_ln1_g, m_ln1_b, m_w_in, m_b_forget, m_conv_w, m_conv_b, m_rg_wa, m_rg_ba, m_rg_wx, m_rg_bx, m_lru_lambda, m_w_out, m_ln2_g, m_ln2_b, m_ffn2_w_gate, m_ffn2_w_up, m_ffn2_w_down, m_ln3_g, m_ln3_b)
    v_args = (v_ffn1_w_gate, v_ffn1_w_up, v_ffn1_w_down, v_ln1_g, v_ln1_b, v_w_in, v_b_forget, v_conv_w, v_conv_b, v_rg_wa, v_rg_ba, v_rg_wx, v_rg_bx, v_lru_lambda, v_w_out, v_ln2_g, v_ln2_b, v_ffn2_w_gate, v_ffn2_w_up, v_ffn2_w_down, v_ln3_g, v_ln3_b)
    w = dict(zip(WEIGHTS, w_args))
    m = dict(zip(WEIGHTS, m_args))
    v = dict(zip(WEIGHTS, v_args))
    me = 4 * lax.axis_index("x") + 2 * lax.axis_index("y") + lax.axis_index("c")

    sent = {n: _transport(w[n]).astype(bf16) for n in BIG}
    sent["conv_w"] = _two_d(w["conv_w"])
    small = {n: w[n] for n in ("ln1_g", "ln1_b", "ln2_g", "ln2_b", "ln3_g", "ln3_b", "b_forget", "conv_b",
                               "lru_lambda")}
    small.update({n: w[n][0] for n in ("rg_wa", "rg_ba", "rg_wx", "rg_bx")})

    grad_x, parts, all_packed, small_shapes = _local_step(x[0], loss_target[0], sent, small)

    total = _sum_parts(all_packed, name="sum_small_grads")
    grads, off = {}, 0
    for n in PACKED:
        size = math.prod(small_shapes[n])
        rows = -(-size // LANES)
        grads[n] = total[off:off + rows].reshape(-1)[:size].reshape(small_shapes[n])
        off += rows
    loss = grads.pop("loss").reshape(())
    grads["conv_w"] = lax.dynamic_slice_in_dim(grads["conv_w"], me * (LRU_W // N_DEV), LRU_W // N_DEV, axis=1)

    delta, new_m, new_v = {}, {}, {}
    for group in (("ffn1_w_gate", "ffn1_w_up", "ffn2_w_gate", "ffn2_w_up"), ("ffn1_w_down", "ffn2_w_down"),
                  ("w_in",), ("w_out",)):
        done = _adamw_big([(parts[n], w[n], m[n], v[n]) for n in group], name="adamw_" + group[0])
        for n, (g, d, m2, v2) in zip(group, done):
            grads[n], delta[n], new_m[n], new_v[n] = g, d, m2, v2
    small_names = [n for n in WEIGHTS if n not in BIG]
    outs = _adamw_small([(_two_d(grads[n]), _two_d(w[n]), _two_d(m[n]), _two_d(v[n])) for n in small_names],
                        name="adamw_small")
    for k, n in enumerate(small_names):
        delta[n], new_m[n], new_v[n] = outs[3 * k], outs[3 * k + 1], outs[3 * k + 2]

    def shaped(d):
        return [d[n].reshape(w[n].shape) for n in WEIGHTS]

    return (loss, grad_x[None], *shaped(grads), *shaped(delta), *shaped(new_m), *shaped(new_v))
```

```python
import functools
import math

import jax
import jax.numpy as jnp
from jax import lax
from jax.experimental import pallas as pl
from jax.experimental.pallas import tpu as pltpu

f32 = jnp.float32
bf16 = jnp.bfloat16

N_DEV = 8
D_MODEL = 1024
D_FF = 4096
FF_TILE = D_FF // N_DEV
FOX_W = 512
LRU_W = 512
HEADS = 8
HEAD_D = 64
IN_COLS = 2568
IN_SHARD = IN_COLS // N_DEV
LANES = 128
LN_EPS = 1e-5
ALPHA = 2.0 ** 0.25
ATT_SCALE = 1.0 / math.sqrt(HEAD_D)
LRU_C = 8.0
NEG_BIG = -1e30

ADAM_LR = 0.001
ADAM_B1 = 0.9
ADAM_B2 = 0.999
ADAM_EPS = 1e-08
ADAM_WD = 0.01
ADAM_STEP = 10

VMEM_LIMIT = 56 * 1024 * 1024
MESH_T = pl.DeviceIdType.MESH
LOCAL_COPY_PRIORITY = 1


def _params(sem, **kw):
    return pltpu.CompilerParams(dimension_semantics=sem, vmem_limit_bytes=VMEM_LIMIT, **kw)


def _sigmoid(x):
    return 1.0 / (1.0 + jnp.exp(-x))


def _sigmoid_tanh(x):
    return 0.5 * jnp.tanh(0.5 * x) + 0.5


def _softplus(x):
    return jnp.maximum(x, 0.0) + jnp.log(1.0 + jnp.exp(-jnp.abs(x)))


def _one_minus_exp(x):
    series = -x * (1.0 + x * (0.5 + x * (1.0 / 6 + x * (1.0 / 24 + x * (1.0 / 120 + x * (1.0 / 720))))))
    return jnp.where(x > -0.125, series, 1.0 - jnp.exp(x))


_GELU_C = math.sqrt(2.0 / math.pi)


def _gelu_and_grad(x):
    inner = _GELU_C * (x + 0.044715 * x * x * x)
    t = jnp.tanh(inner)
    g = 0.5 * x * (1.0 + t)
    dg = 0.5 * (1.0 + t) + 0.5 * x * (1.0 - t * t) * _GELU_C * (1.0 + 3 * 0.044715 * x * x)
    return g, dg


def _ln_fwd_tile(pre):
    mu = jnp.mean(pre, axis=-1, keepdims=True)
    xc = pre - mu
    var = jnp.mean(xc * xc, axis=-1, keepdims=True)
    rstd = lax.rsqrt(var + LN_EPS)
    return xc * rstd, rstd


def _ln_bwd_tile(dy, xhat, rstd, g):
    dyg = dy * g
    m1 = jnp.mean(dyg, axis=-1, keepdims=True)
    m2 = jnp.mean(dyg * xhat, axis=-1, keepdims=True)
    dpre = rstd * (dyg - m1 - xhat * m2)
    return dpre, jnp.sum(dy * xhat, axis=0, keepdims=True), jnp.sum(dy, axis=0, keepdims=True)


_NT = (((1,), (1,)), ((), ()))
_TN = (((0,), (0,)), ((), ()))


class _Exchange:
    def __init__(self, arrs, gather, chips=False, direct=False):
        self.arrs, self.gather, self.n, self.chips = list(arrs), gather, len(arrs), chips
        self.relays = gather and not direct

    def out_shape(self):
        return [jax.ShapeDtypeStruct(((N_DEV,) + a.shape) if self.gather else a.shape, a.dtype) for a in self.arrs]

    def scratch(self):
        n_remote = self.n * (N_DEV - 1)
        return [pltpu.SemaphoreType.DMA((n_remote,)), pltpu.SemaphoreType.DMA((n_remote,)),
                pltpu.SemaphoreType.DMA((self.n,))]

    def copies(self, ins, outs, sems):
        send_sems, recv_sems, local_sems = sems
        x, y, c = lax.axis_index("x"), lax.axis_index("y"), lax.axis_index("c")
        me = 2 * x + y if self.chips else 4 * x + 2 * y + c
        out = []
        for k in range(self.n):
            for d in (range(2, N_DEV, 2) if self.chips else range(1, N_DEV)):
                px = 1 - x if d & 4 else x
                py = 1 - y if d & 2 else y
                pc = 1 - c if d & 1 else c
                sem = k * (N_DEV - 1) + d - 1
                out.append(pltpu.make_async_remote_copy(
                    src_ref=ins[k] if self.gather else ins[k].at[2 * px + py if self.chips else 4 * px + 2 * py + pc],
                    dst_ref=outs[k].at[me],
                    send_sem=send_sems.at[sem], recv_sem=recv_sems.at[sem],
                    device_id=(px, py, pc), device_id_type=MESH_T))
            out.append(pltpu.make_async_copy(ins[k] if self.gather else ins[k].at[me], outs[k].at[me],
                                             local_sems.at[k]))
        return out

    def gather_copies(self, ins, outs, sems):
        send_sems, recv_sems, local_sems = sems
        x, y, c = lax.axis_index("x"), lax.axis_index("y"), lax.axis_index("c")
        sibling = (x, y, 1 - c)
        chips = [(1 - x, y), (x, 1 - y), (1 - x, 1 - y)]
        out = []
        for k in range(self.n):
            def copy(s, block, to, src=None, k=k):
                rows = outs[k].at[4 * block[0] + 2 * block[1] + block[2]]
                sem = k * (N_DEV - 1) + s
                return pltpu.make_async_remote_copy(
                    src_ref=rows if src is None else src, dst_ref=rows, send_sem=send_sems.at[sem],
                    recv_sem=recv_sems.at[sem], device_id=to, device_id_type=MESH_T)

            first = [copy(0, (x, y, c), sibling, src=ins[k])]
            first += [copy(1 + q, (x, y, c), (*chip, c), src=ins[k]) for q, chip in enumerate(chips)]
            passed = [copy(4 + q, (*chip, c), sibling) for q, chip in enumerate(chips)]
            own = pltpu.make_async_copy(ins[k], outs[k].at[4 * x + 2 * y + c], local_sems.at[k])
            out.append((first, passed, own, copy))
        return out, sibling, chips, (x, y, c)

    def start(self, ins, outs, sems):
        if not self.relays:
            for cp in self.copies(ins, outs, sems):
                cp.start()
            return
        per_array, _, _, _ = self.gather_copies(ins, outs, sems)
        for first, _, own, _ in per_array:
            own.start(priority=LOCAL_COPY_PRIORITY)
            for cp in first:
                cp.start()

    def relay(self, ins, outs, sems):
        per_array, sibling, chips, (x, y, c) = self.gather_copies(ins, outs, sems)
        for first, passed, own, copy in per_array:
            for q, chip in enumerate(chips):
                copy(1 + q, (*chip, c), (x, y, c)).wait_recv()
                passed[q].start()

    def wait(self, ins, outs, sems, relayed=False):
        if not self.relays:
            for cp in self.copies(ins, outs, sems):
                cp.wait()
            return
        if not relayed:
            self.relay(ins, outs, sems)
        per_array, sibling, chips, (x, y, c) = self.gather_copies(ins, outs, sems)
        for first, passed, own, copy in per_array:
            copy(0, sibling, (x, y, c)).wait_recv()
            for q, chip in enumerate(chips):
                copy(4 + q, (*chip, 1 - c), (x, y, c)).wait_recv()
            for cp in first + passed:
                cp.wait_send()
            own.wait()


class _Hosts:
    def __init__(self, *hosts):
        self.hosts = hosts
        self.relays = any(h.relays for h in hosts)
        self.n = sum(h.n for h in hosts)
        self.arrs = [a for h in hosts for a in h.arrs]

    def out_shape(self):
        return [sh for h in self.hosts for sh in h.out_shape()]

    def scratch(self):
        return [sc for h in self.hosts for sc in h.scratch()]

    def _each(self, ins, outs, sems):
        at = 0
        for k, h in enumerate(self.hosts):
            yield h, ins[at:at + h.n], outs[at:at + h.n], sems[3 * k:3 * k + 3]
            at += h.n

    def start(self, ins, outs, sems):
        for h, h_in, h_out, h_sems in self._each(ins, outs, sems):
            h.start(h_in, h_out, h_sems)

    def relay(self, ins, outs, sems):
        for h, h_in, h_out, h_sems in self._each(ins, outs, sems):
            if h.relays:
                h.relay(h_in, h_out, h_sems)

    def wait(self, ins, outs, sems, relayed=False):
        for h, h_in, h_out, h_sems in self._each(ins, outs, sems):
            h.wait(h_in, h_out, h_sems, relayed=relayed and h.relays)


def _hosted_call(host, body, *, name, grid, in_specs, out_specs, out_shape, scratch_shapes=(), compiler_params):
    out_specs = list(out_specs) if isinstance(out_specs, (list, tuple)) else [out_specs]
    out_shape = list(out_shape) if isinstance(out_shape, (list, tuple)) else [out_shape]
    if host is None:
        return pl.pallas_call(body, name=name, grid=grid, in_specs=in_specs, out_specs=out_specs,
                              out_shape=out_shape, scratch_shapes=list(scratch_shapes),
                              compiler_params=compiler_params)
    n_in, n_out, n_scr, k = len(in_specs), len(out_shape), len(scratch_shapes), host.n

    def wrapped(*refs):
        ins, h_in = refs[:n_in], refs[n_in:n_in + k]
        outs, h_out = refs[n_in + k:n_in + k + n_out], refs[n_in + k + n_out:n_in + 2 * k + n_out]
        scr, sems = refs[n_in + 2 * k + n_out:n_in + 2 * k + n_out + n_scr], refs[n_in + 2 * k + n_out + n_scr:]
        ids = [pl.program_id(a) for a in range(len(grid))]
        first = functools.reduce(jnp.logical_and, [i == 0 for i in ids])
        last = functools.reduce(jnp.logical_and, [i == g - 1 for i, g in zip(ids, grid)])
        steps = math.prod(grid)
        relay_at = (3 * steps) // 4 if host.relays and steps >= 8 else None

        @pl.when(first)
        def _():
            host.start(h_in, h_out, sems)

        if relay_at is not None:
            coords, rest = [], relay_at
            for g in reversed(grid):
                coords.append(rest % g)
                rest //= g

            @pl.when(functools.reduce(jnp.logical_and, [i == cd for i, cd in zip(ids, reversed(coords))]))
            def _():
                host.relay(h_in, h_out, sems)

        body(*ins, *outs, *scr)

        @pl.when(last)
        def _():
            host.wait(h_in, h_out, sems, relayed=relay_at is not None)

    hbm = pl.BlockSpec(memory_space=pl.ANY)
    call = pl.pallas_call(
        wrapped, name=name, grid=grid, in_specs=list(in_specs) + [hbm] * k, out_specs=out_specs + [hbm] * k,
        out_shape=out_shape + host.out_shape(), scratch_shapes=list(scratch_shapes) + host.scratch(),
        compiler_params=compiler_params)
    return lambda *args: call(*args, *host.arrs)


def _ffn_fwd_loss(xhat, g_in, b_in, wg, wu, wd, g_out, b_out, target, *, tm, name):
    t = xhat.shape[0]
    nj = N_DEV

    def body(x_ref, g_ref, b_ref, wg_ref, wu_ref, wd_ref, go_ref, bo_ref, tg_ref,
             dx_ref, sq_ref, gg_ref, gb_ref, hg_ref, hu_ref, xb, acc):
        i = pl.program_id(0)
        j = pl.program_id(1)

        @pl.when(j == 0)
        def _():
            xb[...] = (x_ref[...] * g_ref[...] + b_ref[...]).astype(bf16)
            acc[...] = jnp.zeros_like(acc)

        hg = jnp.dot(xb[...], wg_ref[...], preferred_element_type=f32)
        hu = jnp.dot(xb[...], wu_ref[...], preferred_element_type=f32)
        hg_ref[...] = hg.astype(bf16)
        hu_ref[...] = hu.astype(bf16)
        a = hg * _sigmoid_tanh(hg) * hu
        acc[...] += jnp.dot(a.astype(bf16), wd_ref[...], preferred_element_type=f32)

        @pl.when(j == nj - 1)
        def _():
            x = x_ref[...] * g_ref[...] + b_ref[...]
            xo, rstd = _ln_fwd_tile(ALPHA * x + 0.5 * acc[...])
            diff = xo * go_ref[...] + bo_ref[...] - tg_ref[...]
            sq = jnp.sum(diff * diff, axis=0, keepdims=True)
            dprev, gg, gb = _ln_bwd_tile(diff * (1.0 / D_MODEL), xo, rstd, go_ref[...])
            dx_ref[...] = dprev

            @pl.when(i == 0)
            def _():
                sq_ref[...] = sq
                gg_ref[...] = gg
                gb_ref[...] = gb

            @pl.when(i > 0)
            def _():
                sq_ref[...] += sq
                gg_ref[...] += gg
                gb_ref[...] += gb

    tok = pl.BlockSpec((tm, D_MODEL), lambda i, j: (i, 0))
    row = pl.BlockSpec((1, D_MODEL), lambda i, j: (0, 0))
    hid = pl.BlockSpec((tm, FF_TILE), lambda i, j: (i, j))
    return pl.pallas_call(
        body, name=name, grid=(t // tm, nj),
        in_specs=[tok, row, row,
                  pl.BlockSpec((None, D_MODEL, FF_TILE), lambda i, j: (j, 0, 0)),
                  pl.BlockSpec((None, D_MODEL, FF_TILE), lambda i, j: (j, 0, 0)),
                  pl.BlockSpec((None, FF_TILE, D_MODEL), lambda i, j: (j, 0, 0)), row, row, tok],
        out_specs=[tok, row, row, row, hid, hid],
        out_shape=[jax.ShapeDtypeStruct((t, D_MODEL), f32)] + [jax.ShapeDtypeStruct((1, D_MODEL), f32)] * 3
        + [jax.ShapeDtypeStruct((t, D_FF), bf16)] * 2,
        scratch_shapes=[pltpu.VMEM((tm, D_MODEL), bf16), pltpu.VMEM((tm, D_MODEL), f32)],
        compiler_params=_params(("arbitrary", "arbitrary")),
    )(xhat, g_in, b_in, wg, wu, wd, g_out, b_out, target)


def _ffn1_fwd_gathering(x, own, extra, *, tm, name):
    t = x.shape[0]
    n_i = t // tm
    n_arr = 3
    k_extra = extra.n
    ex = _Exchange(list(own), gather=True)
    ax, ay, ac = lax.axis_index("x"), lax.axis_index("y"), lax.axis_index("c")
    order = jnp.stack([4 * px + 2 * py + pc for px, py in ((ax, ay), (1 - ax, ay), (ax, 1 - ay), (1 - ax, 1 - ay))
                       for pc in (ac, 1 - ac)]).astype(jnp.int32)
    arrival = [None, (0, None), (1, 0), (4, None), (2, 1), (5, None), (3, 2), (6, None)]

    def body(order_ref, x_ref, *refs):
        w_in, e_in = refs[:n_arr], refs[n_arr:n_arr + k_extra]
        refs = refs[n_arr + k_extra:]
        xo_ref, rstd_ref, hg_ref, hu_ref = refs[:4]
        w_all, e_out = refs[4:4 + n_arr], refs[4 + n_arr:4 + n_arr + k_extra]
        acc, wgb, wub, wdb, fetch_sems, send_sems, recv_sems, local_sems = refs[4 + n_arr + k_extra:12 + n_arr + k_extra]
        e_sems = refs[12 + n_arr + k_extra:]
        bufs = (wgb, wub, wdb)
        s = pl.program_id(0)
        i = pl.program_id(1)
        per_array, sibling, chips, (x_, y_, c_) = ex.gather_copies(w_in, w_all, (send_sems, recv_sems, local_sems))

        def fetch(pos, slot):
            return [pltpu.make_async_copy(w_in[a] if pos == 0 else w_all[a].at[order_ref[pos]],
                                          bufs[a].at[slot], fetch_sems.at[n_arr * slot + a]) for a in range(n_arr)]

        def source_of(pos):
            chip = (x_, y_) if pos < 2 else chips[(pos - 2) // 2]
            return (*chip, c_ if pos % 2 == 0 else 1 - c_)

        @pl.when(jnp.logical_and(s == 0, i == 0))
        def _():
            for q in range(4):
                for first, _, own_copy, _ in per_array:
                    if q == 0:
                        own_copy.start(priority=LOCAL_COPY_PRIORITY)
                    first[q].start()
            for cp in fetch(0, 0):
                cp.start(priority=LOCAL_COPY_PRIORITY)
            for cp in fetch(0, 0):
                cp.wait()

        @pl.when(jnp.logical_and(s == N_DEV // 2, i == 0))
        def _():
            extra.start(e_in, e_out, e_sems)

        for pos in range(1, N_DEV):
            @pl.when(jnp.logical_and(s == pos - 1, i == n_i - 1))
            def _(pos=pos):
                sem, passes = arrival[pos]
                for _, passed, _, copy in per_array:
                    copy(sem, source_of(pos), (x_, y_, c_)).wait_recv()
                    if passes is not None:
                        passed[passes].start()
                for cp in fetch(pos, pos % 2):
                    cp.start(priority=LOCAL_COPY_PRIORITY)

            @pl.when(jnp.logical_and(s == pos, i == 0))
            def _(pos=pos):
                for cp in fetch(pos, pos % 2):
                    cp.wait()

        slot = s % 2
        xb = x_ref[...].astype(bf16)
        hg = jnp.dot(xb, wgb[slot], preferred_element_type=f32)
        hu = jnp.dot(xb, wub[slot], preferred_element_type=f32)
        hg_ref[...] = hg.astype(bf16)
        hu_ref[...] = hu.astype(bf16)
        a = hg * _sigmoid_tanh(hg) * hu
        part = jnp.dot(a.astype(bf16), wdb[slot], preferred_element_type=f32)

        @pl.when(s == 0)
        def _():
            acc[i] = part

        @pl.when(s > 0)
        def _():
            acc[i] += part

        @pl.when(s == N_DEV - 1)
        def _():
            xo, rstd = _ln_fwd_tile(ALPHA * x_ref[...] + 0.5 * acc[i])
            xo_ref[...] = xo
            rstd_ref[...] = rstd

        @pl.when(jnp.logical_and(s == N_DEV - 1, i == n_i - 1))
        def _():
            for first, passed, own_copy, _ in per_array:
                for cp in first + passed:
                    cp.wait_send()
                own_copy.wait()
            extra.wait(e_in, e_out, e_sems)

    hbm = pl.BlockSpec(memory_space=pl.ANY)
    last = N_DEV - 1
    tok_out = pl.BlockSpec((tm, D_MODEL), lambda s, i, o: (jnp.where(s == last, i, 0), 0))
    col_out = pl.BlockSpec((tm, 1), lambda s, i, o: (jnp.where(s == last, i, 0), 0))
    hid = pl.BlockSpec((tm, FF_TILE), lambda s, i, o: (i, o[s]))
    shard_shapes = [(N_DEV,) + w.shape for w in own]
    grid_spec = pltpu.PrefetchScalarGridSpec(
        num_scalar_prefetch=1, grid=(N_DEV, n_i),
        in_specs=[pl.BlockSpec((tm, D_MODEL), lambda s, i, o: (i, 0))] + [hbm] * (n_arr + k_extra),
        out_specs=[tok_out, col_out, hid, hid] + [hbm] * (n_arr + k_extra),
        scratch_shapes=[pltpu.VMEM((n_i, tm, D_MODEL), f32)]
        + [pltpu.VMEM((2,) + w.shape, bf16) for w in own]
        + [pltpu.SemaphoreType.DMA((2 * n_arr,))] + ex.scratch() + extra.scratch())
    res = pl.pallas_call(
        body, name=name, grid_spec=grid_spec,
        out_shape=[jax.ShapeDtypeStruct((t, D_MODEL), f32), jax.ShapeDtypeStruct((t, 1), f32),
                   jax.ShapeDtypeStruct((t, D_FF), bf16), jax.ShapeDtypeStruct((t, D_FF), bf16)]
        + [jax.ShapeDtypeStruct(sh, bf16) for sh in shard_shapes] + extra.out_shape(),
        compiler_params=_params(("arbitrary", "arbitrary")),
    )(order, x, *own, *extra.arrs)
    return res


def _ffn_bwd(dpre, hg, hu, wg, wu, wd, ln_in, *, tm, name, host=None):
    t = dpre.shape[0]
    nj = N_DEV
    with_ln = ln_in is not None

    def body(*refs):
        if with_ln:
            (dp_ref, hg_ref, hu_ref, wg_ref, wu_ref, wd_ref, xh_ref, rs_ref, g_ref,
             dx_ref, gg_ref, gb_ref, dhg_ref, dhu_ref, a_ref, dfb, acc) = refs
        else:
            (dp_ref, hg_ref, hu_ref, wg_ref, wu_ref, wd_ref,
             dx_ref, dhg_ref, dhu_ref, a_ref, dfb, acc) = refs
        i = pl.program_id(0)
        j = pl.program_id(1)

        @pl.when(j == 0)
        def _():
            dfb[...] = (0.5 * dp_ref[...]).astype(bf16)
            acc[...] = jnp.zeros_like(acc)

        da = lax.dot_general(dfb[...], wd_ref[...], _NT, preferred_element_type=f32)
        hgv = hg_ref[...].astype(f32)
        huv = hu_ref[...].astype(f32)
        sg = _sigmoid_tanh(hgv)
        silu = hgv * sg
        a_ref[...] = (silu * huv).astype(bf16)
        dhu = (da * silu).astype(bf16)
        dhg = (da * huv * (sg * (1.0 + hgv * (1.0 - sg)))).astype(bf16)
        dhg_ref[...] = dhg
        dhu_ref[...] = dhu
        acc[...] += (lax.dot_general(dhg, wg_ref[...], _NT, preferred_element_type=f32)
                     + lax.dot_general(dhu, wu_ref[...], _NT, preferred_element_type=f32))

        @pl.when(j == nj - 1)
        def _():
            dx = ALPHA * dp_ref[...] + acc[...]
            if with_ln:
                dprev, gg, gb = _ln_bwd_tile(dx, xh_ref[...], rs_ref[...], g_ref[...])
                dx_ref[...] = dprev

                @pl.when(i == 0)
                def _():
                    gg_ref[...] = gg
                    gb_ref[...] = gb

                @pl.when(i > 0)
                def _():
                    gg_ref[...] += gg
                    gb_ref[...] += gb
            else:
                dx_ref[...] = dx

    tok = pl.BlockSpec((tm, D_MODEL), lambda i, j: (i, 0), pipeline_mode=pl.Buffered(1))
    row = pl.BlockSpec((1, D_MODEL), lambda i, j: (0, 0))
    hid = pl.BlockSpec((tm, FF_TILE), lambda i, j: (i, j))
    in_specs = [tok, hid, hid,
                pl.BlockSpec((None, D_MODEL, FF_TILE), lambda i, j: (j, 0, 0)),
                pl.BlockSpec((None, D_MODEL, FF_TILE), lambda i, j: (j, 0, 0)),
                pl.BlockSpec((None, FF_TILE, D_MODEL), lambda i, j: (j, 0, 0))]
    args = [dpre, hg, hu, wg, wu, wd]
    out_specs = [tok]
    out_shape = [jax.ShapeDtypeStruct((t, D_MODEL), f32)]
    if with_ln:
        in_specs += [tok, pl.BlockSpec((tm, 1), lambda i, j: (i, 0)), row]
        args += list(ln_in)
        out_specs += [row, row]
        out_shape += [jax.ShapeDtypeStruct((1, D_MODEL), f32)] * 2
    out_specs += [hid, hid, hid]
    out_shape += [jax.ShapeDtypeStruct((t, D_FF), bf16)] * 3
    return _hosted_call(
        host, body, name=name, grid=(t // tm, nj), in_specs=in_specs, out_specs=out_specs, out_shape=out_shape,
        scratch_shapes=[pltpu.VMEM((tm, D_MODEL), bf16), pltpu.VMEM((tm, D_MODEL), f32)],
        compiler_params=_params(("arbitrary", "arbitrary")),
    )(*args)


def _ffn_bwd_act(dpre, hg, hu, wd, *, tm, name, host=None):
    t = dpre.shape[0]

    def body(dp_ref, hg_ref, hu_ref, wd_ref, dhg_ref, dhu_ref, a_ref, dfb):
        @pl.when(pl.program_id(1) == 0)
        def _():
            dfb[...] = (0.5 * dp_ref[...]).astype(bf16)

        da = lax.dot_general(dfb[...], wd_ref[...], _NT, preferred_element_type=f32)
        hgv = hg_ref[...].astype(f32)
        huv = hu_ref[...].astype(f32)
        sg = _sigmoid_tanh(hgv)
        silu = hgv * sg
        a_ref[...] = (silu * huv).astype(bf16)
        dhu_ref[...] = (da * silu).astype(bf16)
        dhg_ref[...] = (da * huv * (sg * (1.0 + hgv * (1.0 - sg)))).astype(bf16)

    hid = pl.BlockSpec((tm, FF_TILE), lambda i, j: (i, j))
    return _hosted_call(
        host, body, name=name, grid=(t // tm, N_DEV),
        in_specs=[pl.BlockSpec((tm, D_MODEL), lambda i, j: (i, 0)), hid, hid,
                  pl.BlockSpec((None, FF_TILE, D_MODEL), lambda i, j: (j, 0, 0))],
        out_specs=[hid, hid, hid], out_shape=[jax.ShapeDtypeStruct((t, D_FF), bf16)] * 3,
        scratch_shapes=[pltpu.VMEM((tm, D_MODEL), bf16)],
        compiler_params=_params(("arbitrary", "arbitrary")),
    )(dpre, hg, hu, wd)


def _ffn_bwd_dx(dpre, dhg, dhu, wg, wu, *, tm, name, host=None):
    t = dpre.shape[0]
    nj = N_DEV

    def body(dp_ref, dhg_ref, dhu_ref, wg_ref, wu_ref, dx_ref, acc):
        j = pl.program_id(1)

        @pl.when(j == 0)
        def _():
            acc[...] = jnp.zeros_like(acc)

        acc[...] += (lax.dot_general(dhg_ref[...], wg_ref[...], _NT, preferred_element_type=f32)
                     + lax.dot_general(dhu_ref[...], wu_ref[...], _NT, preferred_element_type=f32))

        @pl.when(j == nj - 1)
        def _():
            dx_ref[...] = ALPHA * dp_ref[...] + acc[...]

    tok = pl.BlockSpec((tm, D_MODEL), lambda i, j: (i, 0))
    hid = pl.BlockSpec((tm, FF_TILE), lambda i, j: (i, j))
    wspec = pl.BlockSpec((None, D_MODEL, FF_TILE), lambda i, j: (j, 0, 0))
    return _hosted_call(
        host, body, name=name, grid=(t // tm, nj), in_specs=[tok, hid, hid, wspec, wspec],
        out_specs=[tok], out_shape=[jax.ShapeDtypeStruct((t, D_MODEL), f32)],
        scratch_shapes=[pltpu.VMEM((tm, D_MODEL), f32)],
        compiler_params=_params(("arbitrary", "arbitrary")),
    )(dpre, dhg, dhu, wg, wu)


def _mm(a, b, *, mode, out_dtype, tm, tn, tk, name, affine=None, a_cols=None, b_cols=None,
        b_blocked=False, out_blocked=False, out_scale=None):
    if mode == "nn":
        m_full, k_full = a.shape
        m_dim, k_dim = (m_full, a_cols[1]) if a_cols else (m_full, k_full)
    else:
        k_dim, m_full = a.shape
        m_dim = a_cols[1] if a_cols else m_full
    a_off = a_cols[0] if a_cols else 0
    if b_blocked:
        n_dim = b.shape[0] * b.shape[2]
        assert b.shape[2] == tn
    else:
        n_dim = b_cols[1] if b_cols else b.shape[1]
    b_off = b_cols[0] if b_cols else 0
    assert m_dim % tm == 0 and n_dim % tn == 0 and k_dim % tk == 0, (name, m_dim, n_dim, k_dim)
    nk = k_dim // tk

    def body(*refs):
        if affine is not None:
            a_ref, g_ref, s_ref, b_ref, o_ref, acc = refs
        else:
            a_ref, b_ref, o_ref, acc = refs
        k = pl.program_id(2)

        @pl.when(k == 0)
        def _():
            acc[...] = jnp.zeros_like(acc)

        av = a_ref[...]
        if affine is not None:
            av = av * g_ref[...] + s_ref[...]
        av = av.astype(bf16)
        bv = b_ref[...].astype(bf16)
        if mode == "nn":
            acc[...] += jnp.dot(av, bv, preferred_element_type=f32)
        else:
            acc[...] += lax.dot_general(av, bv, _TN, preferred_element_type=f32)

        @pl.when(k == nk - 1)
        def _():
            res = acc[...] if out_scale is None else acc[...] * out_scale
            o_ref[...] = res.astype(out_dtype)

    if mode == "nn":
        a_spec = pl.BlockSpec((tm, tk), lambda i, j, k: (i, k + a_off))
        aff_spec = pl.BlockSpec((1, tk), lambda i, j, k: (0, k + a_off))
    else:
        a_spec = pl.BlockSpec((tk, tm), lambda i, j, k: (k, i + a_off))
        aff_spec = pl.BlockSpec((1, tm), lambda i, j, k: (0, i + a_off))
    if b_blocked:
        b_spec = pl.BlockSpec((None, tk, tn), lambda i, j, k: (j, k, 0))
    else:
        b_spec = pl.BlockSpec((tk, tn), lambda i, j, k: (k, j + b_off))
    if out_blocked:
        o_spec = pl.BlockSpec((None, tm, tn), lambda i, j, k: (j, i, 0))
        o_shape = jax.ShapeDtypeStruct((n_dim // tn, m_dim, tn), out_dtype)
    else:
        o_spec = pl.BlockSpec((tm, tn), lambda i, j, k: (i, j))
        o_shape = jax.ShapeDtypeStruct((m_dim, n_dim), out_dtype)
    in_specs = [a_spec] + ([aff_spec, aff_spec] if affine is not None else []) + [b_spec]
    args = [a] + (list(affine) if affine is not None else []) + [b]
    return pl.pallas_call(
        body, name=name, grid=(m_dim // tm, n_dim // tn, nk), in_specs=in_specs, out_specs=o_spec,
        out_shape=o_shape, scratch_shapes=[pltpu.VMEM((tm, tn), f32)],
        compiler_params=_params(("arbitrary", "arbitrary", "arbitrary")),
    )(*args)


def _mm_tn(a, b, *, out_dtype, tm, mb, tn, nb, tk, name, affine=None, out_blocked=False, out_scale=None,
           pair=False, host=None):
    k_dim, m_dim = a.shape
    multi_b = isinstance(b, (list, tuple))
    b_list = list(b) if multi_b else [b]
    n_dim = nb * tn if multi_b else b.shape[1]
    assert m_dim % (mb * tm) == 0 and n_dim % (nb * tn) == 0 and k_dim % tk == 0, (name, m_dim, n_dim, k_dim)
    nk = k_dim // tk
    grid = (m_dim // (mb * tm), n_dim // (nb * tn), nk)
    if pair:
        assert mb * nb == 4 and grid[0] * grid[1] == 2 and out_dtype == bf16, name

    def body(*refs):
        if pair:
            refs, (acc, send_buf, recv_buf, keep, send_sems, recv_sems) = refs[:-6], refs[-6:]
        else:
            refs, acc = refs[:-1], refs[-1]
        a_ref, o_ref = refs[0], refs[-1]
        if affine is not None:
            g_ref, s_ref = refs[1:3]
        b_refs = refs[3 if affine is not None else 1:-1]
        k = pl.program_id(2)

        @pl.when(k == 0)
        def _():
            acc[...] = jnp.zeros_like(acc)

        av = a_ref[...]
        if affine is not None:
            av = av * g_ref[...] + s_ref[...]
        av = av.astype(bf16)
        if multi_b:
            pieces = [r[...].astype(bf16) for r in b_refs]
        else:
            bv = b_refs[0][...].astype(bf16)
            pieces = [bv[:, jn * tn:(jn + 1) * tn] for jn in range(nb)]
        for im in range(mb):
            a_t = av[:, im * tm:(im + 1) * tm].T
            for jn in range(nb):
                acc[im * nb + jn] += jnp.dot(a_t, pieces[jn], preferred_element_type=f32)

        def scaled(v):
            return v if out_scale is None else v * out_scale

        @pl.when(k == nk - 1)
        def _():
            if pair:
                x, y, c = lax.axis_index("x"), lax.axis_index("y"), lax.axis_index("c")
                window = pl.program_id(0) + pl.program_id(1)

                def swap(w, cc):
                    return pltpu.make_async_remote_copy(
                        src_ref=send_buf.at[w, cc], dst_ref=recv_buf.at[w, cc],
                        send_sem=send_sems.at[2 * w + cc], recv_sem=recv_sems.at[2 * w + cc],
                        device_id=(x, y, 1 - c), device_id_type=MESH_T)

                for w in range(2):
                    @pl.when(window == w)
                    def _(w=w):
                        for cc in range(2):
                            send_buf[w, cc] = scaled(acc[2 * cc + 1 - c]).astype(bf16)
                            swap(w, cc).start()
                            if w == 0:
                                keep[cc] = scaled(acc[2 * cc + c])

                @pl.when(window == 1)
                def _():
                    for w in range(2):
                        for cc in range(2):
                            swap(w, cc).wait_recv()
                            mine = keep[cc] if w == 0 else scaled(acc[2 * cc + c])
                            o_ref[2 * w + cc] = (mine + recv_buf[w, cc].astype(f32)).astype(bf16)
                    for w in range(2):
                        for cc in range(2):
                            swap(w, cc).wait_send()
                return
            for im in range(mb):
                for jn in range(nb):
                    res = scaled(acc[im * nb + jn])
                    if out_blocked:
                        o_ref[jn, im * tm:(im + 1) * tm, :] = res.astype(out_dtype)
                    else:
                        o_ref[im * tm:(im + 1) * tm, jn * tn:(jn + 1) * tn] = res.astype(out_dtype)

    a_spec = pl.BlockSpec((tk, mb * tm), lambda i, j, k: (k, i))
    aff_spec = pl.BlockSpec((1, mb * tm), lambda i, j, k: (0, i))
    if multi_b:
        b_specs = [pl.BlockSpec((tk, tn), lambda i, j, k: (k, 0))] * nb
    else:
        b_specs = [pl.BlockSpec((tk, nb * tn), lambda i, j, k: (k, j))]
    scratch = [pltpu.VMEM((mb * nb, tm, tn), f32)]
    if pair:
        o_spec = pl.BlockSpec((4, tm, tn), lambda i, j, k: (0, 0, 0))
        o_shape = jax.ShapeDtypeStruct((4, tm, tn), out_dtype)
        scratch += [pltpu.VMEM((2, 2, tm, tn), bf16), pltpu.VMEM((2, 2, tm, tn), bf16), pltpu.VMEM((2, tm, tn), f32),
                    pltpu.SemaphoreType.DMA((4,)), pltpu.SemaphoreType.DMA((4,))]
    elif out_blocked:
        o_spec = pl.BlockSpec((nb, mb * tm, tn), lambda i, j, k: (j, i, 0))
        o_shape = jax.ShapeDtypeStruct((n_dim // tn, m_dim, tn), out_dtype)
    else:
        o_spec = pl.BlockSpec((mb * tm, nb * tn), lambda i, j, k: (i, j))
        o_shape = jax.ShapeDtypeStruct((m_dim, n_dim), out_dtype)
    in_specs = [a_spec] + ([aff_spec, aff_spec] if affine is not None else []) + b_specs
    args = [a] + (list(affine) if affine is not None else []) + b_list
    res = _hosted_call(
        host, body, name=name, grid=grid, in_specs=in_specs, out_specs=o_spec, out_shape=o_shape,
        scratch_shapes=scratch, compiler_params=_params(("arbitrary", "arbitrary", "arbitrary")),
    )(*args)
    return res[0] if host is None else res


def _in_proj(xhat, g, b, w_in, *, tm, name):
    t = xhat.shape[0]
    n_qkv, n_l = 3 * FOX_W, 2 * LRU_W

    def body(x_ref, g_ref, b_ref, w_ref, qkv_ref, zl_ref, zfg_ref):
        xb = (x_ref[...] * g_ref[...] + b_ref[...]).astype(bf16)
        qkv_ref[...] = jnp.dot(xb, w_ref[:, :n_qkv], preferred_element_type=f32).astype(bf16)
        zl_ref[...] = jnp.dot(xb, w_ref[:, n_qkv:n_qkv + n_l], preferred_element_type=f32)
        zfg_ref[...] = jnp.dot(xb, w_ref[:, n_qkv + n_l:], preferred_element_type=f32)

    row = pl.BlockSpec((1, D_MODEL), lambda i: (0, 0))
    return pl.pallas_call(
        body, name=name, grid=(t // tm,),
        in_specs=[pl.BlockSpec((tm, D_MODEL), lambda i: (i, 0)), row, row,
                  pl.BlockSpec(w_in.shape, lambda i: (0, 0))],
        out_specs=[pl.BlockSpec((tm, n_qkv), lambda i: (i, 0)), pl.BlockSpec((tm, n_l), lambda i: (i, 0)),
                   pl.BlockSpec((tm, LANES), lambda i: (i, 0))],
        out_shape=[jax.ShapeDtypeStruct((t, n_qkv), bf16), jax.ShapeDtypeStruct((t, n_l), f32),
                   jax.ShapeDtypeStruct((t, LANES), f32)],
        compiler_params=_params(("arbitrary",)),
    )(xhat, g, b, w_in)


def _mmln(pairs, *, tm, name, resid=None, resid_scale=1.0, epi=None, ln=None, n_out=D_MODEL):
    t = pairs[0][0].shape[0]
    n_pairs = len(pairs)
    n_resid = 0 if resid is None else len(resid) - 1

    def body(*refs):
        pos = 0
        val = None
        for p in range(n_pairs):
            a_ref, b_ref = refs[pos], refs[pos + 1]
            pos += 2
            av = a_ref[...].astype(bf16)
            bv = b_ref[...].astype(bf16)
            if pairs[p][6] == "nn":
                term = jnp.dot(av, bv, preferred_element_type=f32)
            else:
                term = lax.dot_general(av, bv, _NT, preferred_element_type=f32)
            val = term if val is None else val + term
        if resid is not None:
            if resid[0] == "plain":
                r = refs[pos][...]
            else:
                r = refs[pos][...] * refs[pos + 1][...] + refs[pos + 2][...]
            pos += n_resid
            val = val + resid_scale * r
        if epi is None:
            o_ref = refs[pos]
            o_ref[...] = val.astype(o_ref.dtype)
        elif epi == "ln_fwd":
            xo, rstd = _ln_fwd_tile(val)
            refs[pos][...] = xo
            refs[pos + 1][...] = rstd
        else:
            xh_ref, rs_ref, g_ref, dx_ref, gg_ref, gb_ref = refs[pos:pos + 6]
            dprev, gg, gb = _ln_bwd_tile(val, xh_ref[...], rs_ref[...], g_ref[...])
            dx_ref[...] = dprev
            i = pl.program_id(0)

            @pl.when(i == 0)
            def _():
                gg_ref[...] = gg
                gb_ref[...] = gb

            @pl.when(i > 0)
            def _():
                gg_ref[...] += gg
                gb_ref[...] += gb

    in_specs, args = [], []
    for (a, acb, aw, b, bcb, bw, mode) in pairs:
        in_specs.append(pl.BlockSpec((tm, aw), lambda i, acb=acb: (i, acb)))
        args.append(a)
        if mode == "nn":
            in_specs.append(pl.BlockSpec((aw, n_out), lambda i, bcb=bcb: (bcb, 0)))
        else:
            in_specs.append(pl.BlockSpec((n_out, bw), lambda i, bcb=bcb: (0, bcb)))
        args.append(b)
    tok = pl.BlockSpec((tm, n_out), lambda i: (i, 0))
    row = pl.BlockSpec((1, n_out), lambda i: (0, 0))
    col = pl.BlockSpec((tm, 1), lambda i: (i, 0))
    if resid is not None:
        in_specs += [tok] if resid[0] == "plain" else [tok, row, row]
        args += list(resid[1:])
    if epi is None:
        out_specs, out_shape = tok, jax.ShapeDtypeStruct((t, n_out), f32)
    elif epi == "ln_fwd":
        out_specs = [tok, col]
        out_shape = [jax.ShapeDtypeStruct((t, n_out), f32), jax.ShapeDtypeStruct((t, 1), f32)]
    else:
        in_specs += [tok, col, row]
        args += list(ln)
        out_specs = [tok, row, row]
        out_shape = [jax.ShapeDtypeStruct((t, n_out), f32)] + [jax.ShapeDtypeStruct((1, n_out), f32)] * 2
    return pl.pallas_call(
        body, name=name, grid=(t // tm,), in_specs=in_specs, out_specs=out_specs, out_shape=out_shape,
        compiler_params=_params(("arbitrary",)),
    )(*args)


CUM_TILE = 512


def _tri(n, lower):
    r = lax.broadcasted_iota(jnp.int32, (n, n), 0)
    c = lax.broadcasted_iota(jnp.int32, (n, n), 1)
    return jnp.where((r >= c) if lower else (r <= c), 1.0, 0.0).astype(f32)


def _cum_fwd(zfg, bfg, *, name):
    t = zfg.shape[0]

    def body(z_ref, b_ref, o_ref, carry):
        @pl.when(pl.program_id(0) == 0)
        def _():
            carry[...] = jnp.zeros_like(carry)

        ls = -_softplus(-(z_ref[...] + b_ref[...]))
        c = jnp.dot(_tri(CUM_TILE, True), ls, preferred_element_type=f32,
                    precision=lax.Precision.HIGHEST) + carry[...]
        o_ref[...] = c
        carry[...] = c[CUM_TILE - 1:CUM_TILE, :]

    blk = pl.BlockSpec((CUM_TILE, LANES), lambda i: (i, 0))
    return pl.pallas_call(
        body, name=name, grid=(t // CUM_TILE,),
        in_specs=[blk, pl.BlockSpec((1, LANES), lambda i: (0, 0))], out_specs=blk,
        out_shape=jax.ShapeDtypeStruct((t, LANES), f32), scratch_shapes=[pltpu.VMEM((1, LANES), f32)],
        compiler_params=_params(("arbitrary",)),
    )(zfg, bfg)


def _cum_bwd(dcum_q, dcum_k, zfg, bfg, *, name):
    t = zfg.shape[0]
    n = t // CUM_TILE

    def body(d_ref, d2_ref, z_ref, b_ref, o_ref, s_ref, carry):
        i = pl.program_id(0)

        @pl.when(i == 0)
        def _():
            carry[...] = jnp.zeros_like(carry)

        dls = jnp.dot(_tri(CUM_TILE, False), d_ref[...] + d2_ref[...], preferred_element_type=f32,
                      precision=lax.Precision.HIGHEST) + carry[...]
        carry[...] = dls[0:1, :]
        lane = lax.broadcasted_iota(jnp.int32, (CUM_TILE, LANES), 1)
        dfg = jnp.where(lane < HEADS, dls * _sigmoid(-(z_ref[...] + b_ref[...])), 0.0)
        o_ref[...] = dfg
        tot = jnp.sum(dfg, axis=0, keepdims=True)

        @pl.when(i == 0)
        def _():
            s_ref[...] = tot

        @pl.when(i > 0)
        def _():
            s_ref[...] += tot

    blk = pl.BlockSpec((CUM_TILE, LANES), lambda i: (n - 1 - i, 0))
    row = pl.BlockSpec((1, LANES), lambda i: (0, 0))
    return pl.pallas_call(
        body, name=name, grid=(n,), in_specs=[blk, blk, blk, row], out_specs=[blk, row],
        out_shape=[jax.ShapeDtypeStruct((t, LANES), f32), jax.ShapeDtypeStruct((1, LANES), f32)],
        scratch_shapes=[pltpu.VMEM((1, LANES), f32)],
        compiler_params=_params(("arbitrary",)),
    )(dcum_q, dcum_k, zfg, bfg)


ATT_TILE = 512


def _causal(i, j, transposed):
    r = lax.broadcasted_iota(jnp.int32, (ATT_TILE, ATT_TILE), 0)
    c = lax.broadcasted_iota(jnp.int32, (ATT_TILE, ATT_TILE), 1)
    if transposed:
        return (c + i * ATT_TILE) >= (r + j * ATT_TILE)
    return (r + i * ATT_TILE) >= (c + j * ATT_TILE)


ATT_W = HEADS * LANES


def _data_lane(h):
    return HEAD_D * (h % 2)


def _extra_lane(h):
    return HEAD_D - _data_lane(h)


def _split3(x):
    hi = x.astype(bf16)
    rest = x - hi.astype(f32)
    mid = rest.astype(bf16)
    lo = (rest - mid.astype(f32)).astype(bf16)
    return hi, mid, lo


def _three_pieces(x):
    hi, mid, lo = (p.astype(f32) for p in _split3(x))
    return (hi + pltpu.roll(mid, HEADS, axis=1) + pltpu.roll(lo, 2 * HEADS, axis=1)).astype(bf16)


def _move(h, first):
    r = lax.broadcasted_iota(jnp.int32, (LANES, LANES), 0)
    c = lax.broadcasted_iota(jnp.int32, (LANES, LANES), 1)
    hit = functools.reduce(jnp.logical_or, [jnp.logical_and(r == HEADS * q + h, c == first + q) for q in range(3)])
    return jnp.where(hit, 1.0, 0.0).astype(bf16)


def _ones_from(first, rows):
    lane = lax.broadcasted_iota(jnp.int32, (rows, LANES), 1)
    return jnp.where(jnp.logical_and(lane >= first, lane < first + 3), 1.0, 0.0)


def _own_lanes(h, rows):
    lane = lax.broadcasted_iota(jnp.int32, (rows, LANES), 1)
    return (lane < HEAD_D) if h % 2 == 0 else (lane >= HEAD_D)


def _head_values(x):
    lane = lax.broadcasted_iota(jnp.int32, x.shape, 1)
    return jnp.where(lane < HEADS, x, 0.0)


def _attn_prep_fwd(qkv, cum, *, tm, name):
    t = qkv.shape[0]

    def body(q_ref, k_ref, v_ref, c_ref, qa_ref, ka_ref, va_ref):
        c3 = _three_pieces(_head_values(c_ref[...]))
        ones = jnp.ones((tm, LANES), bf16)
        for h in range(HEADS):
            pair = slice(LANES * (h // 2), LANES * (h // 2 + 1))
            hs = slice(LANES * h, LANES * (h + 1))
            base, own = _extra_lane(h), _own_lanes(h, tm)
            eq = jnp.dot(c3, _move(h, base), preferred_element_type=f32) + _ones_from(base + 3, tm)
            ek = _ones_from(base, tm) - jnp.dot(c3, _move(h, base + 3), preferred_element_type=f32)
            qa_ref[:, hs] = jnp.where(own, q_ref[:, pair] * ATT_SCALE, eq.astype(bf16))
            ka_ref[:, hs] = jnp.where(own, k_ref[:, pair], ek.astype(bf16))
            va_ref[:, hs] = jnp.where(own, v_ref[:, pair], ones)

    wide = pl.BlockSpec((tm, ATT_W), lambda i: (i, 0))
    out = jax.ShapeDtypeStruct((t, ATT_W), bf16)
    return pl.pallas_call(
        body, name=name, grid=(t // tm,),
        in_specs=[pl.BlockSpec((tm, FOX_W), lambda i: (i, 0)), pl.BlockSpec((tm, FOX_W), lambda i: (i, 1)),
                  pl.BlockSpec((tm, FOX_W), lambda i: (i, 2)), pl.BlockSpec((tm, LANES), lambda i: (i, 0))],
        out_specs=[wide] * 3, out_shape=[out] * 3, compiler_params=_params(("arbitrary",)),
    )(qkv, qkv, qkv, cum)


def _attn_prep_bwd(qkv, cum, lse, dmix, o, *, tm, name):
    t = qkv.shape[0]

    def body(q_ref, c_ref, l_ref, do_ref, o_ref, qa_ref, da_ref):
        b3 = _three_pieces(_head_values(c_ref[...] - l_ref[...]))
        r = lax.broadcasted_iota(jnp.int32, (FOX_W, LANES), 0)
        c = lax.broadcasted_iota(jnp.int32, (FOX_W, LANES), 1)
        per_head = jnp.where(r // HEAD_D == c, 1.0, 0.0).astype(bf16)
        delta = sum(jnp.dot(p, per_head, preferred_element_type=f32) for p in _split3(do_ref[...] * o_ref[...]))
        d3 = _three_pieces(delta)
        for h in range(HEADS):
            pair = slice(LANES * (h // 2), LANES * (h // 2 + 1))
            hs = slice(LANES * h, LANES * (h + 1))
            base, own = _extra_lane(h), _own_lanes(h, tm)
            eq = jnp.dot(b3, _move(h, base), preferred_element_type=f32) + _ones_from(base + 3, tm)
            ed = -jnp.dot(d3, _move(h, base), preferred_element_type=f32)
            qa_ref[:, hs] = jnp.where(own, q_ref[:, pair] * ATT_SCALE, eq.astype(bf16))
            da_ref[:, hs] = jnp.where(own, do_ref[:, pair].astype(bf16), ed.astype(bf16))

    wide = pl.BlockSpec((tm, ATT_W), lambda i: (i, 0))
    half = pl.BlockSpec((tm, FOX_W), lambda i: (i, 0))
    col = pl.BlockSpec((tm, LANES), lambda i: (i, 0))
    out = jax.ShapeDtypeStruct((t, ATT_W), bf16)
    return pl.pallas_call(
        body, name=name, grid=(t // tm,), in_specs=[half, col, col, half, half],
        out_specs=[wide] * 2, out_shape=[out] * 2, compiler_params=_params(("arbitrary",)),
    )(qkv, cum, lse, dmix, o)


def _attn_fwd2(q_aug, k_aug, v_aug, *, name, host=None):
    t = q_aug.shape[0]
    n = t // ATT_TILE
    tq = ATT_TILE

    def body(q_ref, k_ref, v_ref, o_ref, lse_ref, acc, m_s):
        i = pl.program_id(0)
        j = pl.program_id(1)

        @pl.when(j == 0)
        def _():
            acc[...] = jnp.zeros_like(acc)
            m_s[...] = jnp.full_like(m_s, NEG_BIG)

        def block(masked):
            mask = _causal(i, j, False) if masked else None
            for h in range(HEADS):
                hs = slice(LANES * h, LANES * (h + 1))
                s = lax.dot_general(q_ref[:, hs], k_ref[:, hs], _NT, preferred_element_type=f32)
                if masked:
                    s = jnp.where(mask, s, NEG_BIG)
                blocks = [s[:, LANES * b:LANES * (b + 1)] for b in range(tq // LANES)]
                m_old = m_s[h]
                m_new = jnp.maximum(m_old, jnp.broadcast_to(
                    jnp.max(functools.reduce(jnp.maximum, blocks), axis=-1, keepdims=True), (tq, LANES)))
                p = jnp.concatenate([jnp.exp(b - m_new) for b in blocks], axis=1).astype(bf16)
                acc[h] = jnp.exp(m_old - m_new) * acc[h] + jnp.dot(p, v_ref[:, hs], preferred_element_type=f32)
                m_s[h] = m_new

        @pl.when(j < i)
        def _():
            block(False)

        @pl.when(j == i)
        def _():
            block(True)
            lse_ref[...] = jnp.zeros_like(lse_ref)
            for h in range(HEADS):
                a = acc[h]
                l = a[:, _extra_lane(h):_extra_lane(h) + 1]
                o_ref[:, HEAD_D * h:HEAD_D * (h + 1)] = a[:, _data_lane(h):_data_lane(h) + HEAD_D] / l
                lse_ref[:, h:h + 1] = m_s[h][:, 0:1] + jnp.log(l)

    kv = pl.BlockSpec((tq, ATT_W), lambda i, j: (jnp.minimum(i, j), 0))
    return _hosted_call(
        host, body, name=name, grid=(n, n),
        in_specs=[pl.BlockSpec((tq, ATT_W), lambda i, j: (i, 0)), kv, kv],
        out_specs=[pl.BlockSpec((tq, FOX_W), lambda i, j: (i, 0)), pl.BlockSpec((tq, LANES), lambda i, j: (i, 0))],
        out_shape=[jax.ShapeDtypeStruct((t, FOX_W), f32), jax.ShapeDtypeStruct((t, LANES), f32)],
        scratch_shapes=[pltpu.VMEM((HEADS, tq, LANES), f32), pltpu.VMEM((HEADS, tq, LANES), f32)],
        compiler_params=_params(("arbitrary", "arbitrary")),
    )(q_aug, k_aug, v_aug)


def _attn_bwd(qb_aug, k_aug, v_aug, do_aug, *, name, host=None):
    t = qb_aug.shape[0]
    n = t // ATT_TILE
    tk = ATT_TILE

    def body(q_ref, k_ref, v_ref, do_ref, dq_ref, dcq_ref, dk_ref, dv_ref, dck_ref, dk_acc, dv_acc, dq_all):
        j = pl.program_id(0)
        i = pl.program_id(1)

        @pl.when(jnp.logical_and(i == 0, j == 0))
        def _():
            dq_all[...] = jnp.zeros_like(dq_all)

        @pl.when(i == 0)
        def _():
            dk_acc[...] = jnp.zeros_like(dk_acc)
            dv_acc[...] = jnp.zeros_like(dv_acc)

        def block(masked):
            mask = _causal(i, j, True) if masked else None
            for h in range(HEADS):
                hs = slice(LANES * h, LANES * (h + 1))
                qh = q_ref[:, hs]
                doh = do_ref[:, hs]
                kh = k_ref[:, hs]
                s_t = lax.dot_general(kh, qh, _NT, preferred_element_type=f32)
                if masked:
                    s_t = jnp.where(mask, s_t, NEG_BIG)
                p_t = jnp.exp(s_t)
                dv_acc[h] += jnp.dot(p_t.astype(bf16), doh, preferred_element_type=f32)
                dp_t = lax.dot_general(v_ref[:, hs], doh, _NT, preferred_element_type=f32)
                ds_t = (p_t * dp_t).astype(bf16)
                dk_acc[h] += jnp.dot(ds_t, qh, preferred_element_type=f32)
                dq_all[i, h] += lax.dot_general(ds_t, kh, _TN, preferred_element_type=f32)

        @pl.when(i > j)
        def _():
            block(False)

        @pl.when(i == j)
        def _():
            block(True)
            dcq_ref[...] = jnp.zeros_like(dcq_ref)
            for h in range(HEADS):
                a = dq_all[j, h]
                dq_ref[:, HEAD_D * h:HEAD_D * (h + 1)] = (
                    a[:, _data_lane(h):_data_lane(h) + HEAD_D] * ATT_SCALE).astype(bf16)
                dcq_ref[:, h:h + 1] = a[:, _extra_lane(h):_extra_lane(h) + 1]

        @pl.when(i == n - 1)
        def _():
            dck_ref[...] = jnp.zeros_like(dck_ref)
            for h in range(HEADS):
                a = dk_acc[h]
                cols = slice(_data_lane(h), _data_lane(h) + HEAD_D)
                dk_ref[:, HEAD_D * h:HEAD_D * (h + 1)] = a[:, cols].astype(bf16)
                dv_ref[:, HEAD_D * h:HEAD_D * (h + 1)] = dv_acc[h][:, cols].astype(bf16)
                dck_ref[:, h:h + 1] = -a[:, _extra_lane(h) + 3:_extra_lane(h) + 4]

    own = pl.BlockSpec((tk, ATT_W), lambda j, i: (j, 0))
    qs = pl.BlockSpec((tk, ATT_W), lambda j, i: (jnp.maximum(i, j), 0))
    half = pl.BlockSpec((tk, FOX_W), lambda j, i: (j, 0))
    col = pl.BlockSpec((tk, LANES), lambda j, i: (j, 0))
    return _hosted_call(
        host, body, name=name, grid=(n, n), in_specs=[qs, own, own, qs],
        out_specs=[half, col, half, half, col],
        out_shape=[jax.ShapeDtypeStruct((t, FOX_W), bf16), jax.ShapeDtypeStruct((t, LANES), f32),
                   jax.ShapeDtypeStruct((t, FOX_W), bf16), jax.ShapeDtypeStruct((t, FOX_W), bf16),
                   jax.ShapeDtypeStruct((t, LANES), f32)],
        scratch_shapes=[pltpu.VMEM((HEADS, tk, LANES), f32), pltpu.VMEM((HEADS, tk, LANES), f32),
                        pltpu.VMEM((n, HEADS, tk, LANES), f32)],
        compiler_params=_params(("arbitrary", "arbitrary")),
    )(qb_aug, k_aug, v_aug, do_aug)


LRU_CHUNK = 64
LRU_G = 256
SUB = 8


def _row_ids(n):
    return lax.broadcasted_iota(jnp.int32, (n, LRU_G), 0)


def _shift_rows_down(ext, s):
    return pltpu.roll(ext, s, axis=0)[SUB:, :]


def _shift_rows_up(ext, s, n):
    return pltpu.roll(ext, ext.shape[0] - s, axis=0)[:n, :]


def _lru_gates(u, wa_ref, ba_ref, wx_ref, bx_ref, sp):
    ub = u.astype(bf16)
    r = _sigmoid(jnp.dot(ub, wa_ref[...], preferred_element_type=f32) + ba_ref[...])
    gi = _sigmoid(jnp.dot(ub, wx_ref[...], preferred_element_type=f32) + bx_ref[...])
    log_a = -LRU_C * r * sp
    a = jnp.exp(log_a)
    s = jnp.sqrt(_one_minus_exp(2.0 * log_a))
    return r, gi, a, s


def _conv_window(lx_ref, r0, ci):
    cur = lx_ref[pl.ds(r0, LRU_CHUNK), :]
    p0 = pl.multiple_of(jnp.maximum(r0 - SUB, 0), SUB)
    prev = jnp.where(ci > 0, lx_ref[pl.ds(p0, SUB), :], 0.0)
    return cur, jnp.concatenate([prev, cur], axis=0)


def _lru_fwd(zl, conv_w, conv_b, wa, ba, wx, bx, lam, *, name, host=None):
    t = zl.shape[0]
    n_chunk = t // LRU_CHUNK

    def body(lx_ref, lg_ref, cw_ref, cb_ref, wa_ref, ba_ref, wx_ref, bx_ref, lam_ref, u_ref, h_ref, y_ref):
        sp = _softplus(-lam_ref[...])
        rows = _row_ids(SUB)

        def chunk(ci, hc):
            r0 = pl.multiple_of(ci * LRU_CHUNK, LRU_CHUNK)
            cur, ext = _conv_window(lx_ref, r0, ci)
            u = cb_ref[...] + cw_ref[3:4, :] * cur
            for k in range(3):
                u = u + cw_ref[k:k + 1, :] * _shift_rows_down(ext, 3 - k)
            r, gi, a, s = _lru_gates(u, wa_ref, ba_ref, wx_ref, bx_ref, sp)
            b = s * (gi * u)
            tiles = []
            for q in range(LRU_CHUNK // SUB):
                ta = a[SUB * q:SUB * (q + 1), :]
                tb = b[SUB * q:SUB * (q + 1), :]
                for d in (1, 2, 4):
                    a_sh = jnp.where(rows >= d, pltpu.roll(ta, d, axis=0), 1.0)
                    b_sh = jnp.where(rows >= d, pltpu.roll(tb, d, axis=0), 0.0)
                    tb = ta * b_sh + tb
                    ta = ta * a_sh
                hq = tb + ta * hc
                hc = hq[SUB - 1:SUB, :]
                tiles.append(hq)
            h = jnp.concatenate(tiles, axis=0)
            u_ref[pl.ds(r0, LRU_CHUNK), :] = u
            h_ref[pl.ds(r0, LRU_CHUNK), :] = h
            gel, _ = _gelu_and_grad(lg_ref[pl.ds(r0, LRU_CHUNK), :])
            y_ref[pl.ds(r0, LRU_CHUNK), :] = gel * h
            return hc

        lax.fori_loop(0, n_chunk, chunk, jnp.zeros((1, LRU_G), f32))

    seq = lambda cb: pl.BlockSpec((t, LRU_G), lambda c, cb=cb: (0, c + cb))
    rowc = pl.BlockSpec((1, LRU_G), lambda c: (0, c))
    diag = pl.BlockSpec((LRU_G, LRU_G), lambda c: (c, c))
    out = jax.ShapeDtypeStruct((t, LRU_W), f32)
    return _hosted_call(
        host, body, name=name, grid=(LRU_W // LRU_G,),
        in_specs=[seq(0), seq(LRU_W // LRU_G), pl.BlockSpec((4, LRU_G), lambda c: (0, c)),
                  rowc, diag, rowc, diag, rowc, rowc],
        out_specs=[seq(0)] * 3, out_shape=[out] * 3,
        compiler_params=_params(("arbitrary",)),
    )(zl, zl, conv_w, conv_b, wa, ba, wx, bx, lam)


def _lru_bwd(dmix, zl, u_all, h_all, conv_w, wa, ba, wx, bx, lam, *, name, host=None):
    t = zl.shape[0]
    n_chunk = t // LRU_CHUNK

    def body(dy_ref, lx_ref, lg_ref, u_ref, h_ref, cw_ref, wa_ref, ba_ref, wx_ref, bx_ref, lam_ref,
             dlx_ref, dlg_ref, dcw_ref, dcb_ref, dba_ref, dbx_ref, dlam_ref, dwa_ref, dwx_ref, dpr_s, dpx_s):
        lam_v = lam_ref[...]
        sp = _softplus(-lam_v)
        rows = _row_ids(SUB)
        rows_c = _row_ids(LRU_CHUNK)
        zero_row = jnp.zeros((1, LRU_G), f32)

        def chunk(step, carry):
            dh_c, a_next0, du_next, dsp, dba, dbx, dcb, dw0, dw1, dw2, dw3 = carry
            ci = n_chunk - 1 - step
            r0 = pl.multiple_of(ci * LRU_CHUNK, LRU_CHUNK)
            sl = pl.ds(r0, LRU_CHUNK)
            u = u_ref[sl, :]
            r, gi, a, s = _lru_gates(u, wa_ref, ba_ref, wx_ref, bx_ref, sp)
            h = h_ref[sl, :]
            p0 = pl.multiple_of(jnp.maximum(r0 - SUB, 0), SUB)
            h_before = jnp.where(ci > 0, h_ref[pl.ds(p0, SUB), :], 0.0)[SUB - 1:SUB, :]
            h_prev = jnp.where(rows_c == 0, h_before, pltpu.roll(h, 1, axis=0))
            gel, dgel = _gelu_and_grad(lg_ref[sl, :])
            dy = dy_ref[sl, :]
            dlg_ref[sl, :] = (dy * h * dgel).astype(bf16)
            g_in = dy * gel
            a_next = jnp.where(rows_c == LRU_CHUNK - 1, a_next0, pltpu.roll(a, LRU_CHUNK - 1, axis=0))
            tiles = [None] * (LRU_CHUNK // SUB)
            for q in reversed(range(LRU_CHUNK // SUB)):
                ta = a_next[SUB * q:SUB * (q + 1), :]
                tb = g_in[SUB * q:SUB * (q + 1), :]
                for d in (1, 2, 4):
                    a_sh = jnp.where(rows < SUB - d, pltpu.roll(ta, SUB - d, axis=0), 1.0)
                    b_sh = jnp.where(rows < SUB - d, pltpu.roll(tb, SUB - d, axis=0), 0.0)
                    tb = ta * b_sh + tb
                    ta = ta * a_sh
                dhq = tb + ta * dh_c
                dh_c = dhq[0:1, :]
                tiles[q] = dhq
            dh = jnp.concatenate(tiles, axis=0)
            da = dh * h_prev
            ds = dh * gi * u
            dgi = dh * s * u
            du = dh * s * gi
            dlog_a = da * a - ds * (a * a) / s
            dr = dlog_a * (-LRU_C * sp)
            dsp = dsp + jnp.sum(dlog_a * (-LRU_C * r), axis=0, keepdims=True)
            dpr = dr * r * (1.0 - r)
            dpx = dgi * gi * (1.0 - gi)
            dprb = dpr.astype(bf16)
            dpxb = dpx.astype(bf16)
            dpr_s[sl, :] = dprb
            dpx_s[sl, :] = dpxb
            du = du + (lax.dot_general(dprb, wa_ref[...], _NT, preferred_element_type=f32)
                       + lax.dot_general(dpxb, wx_ref[...], _NT, preferred_element_type=f32))
            dba = dba + jnp.sum(dpr, axis=0, keepdims=True)
            dbx = dbx + jnp.sum(dpx, axis=0, keepdims=True)
            dcb = dcb + jnp.sum(du, axis=0, keepdims=True)
            du_ext = jnp.concatenate([du, du_next], axis=0)
            dlx = cw_ref[3:4, :] * du
            for k in range(3):
                dlx = dlx + cw_ref[k:k + 1, :] * _shift_rows_up(du_ext, 3 - k, LRU_CHUNK)
            dlx_ref[sl, :] = dlx.astype(bf16)
            cur, ext = _conv_window(lx_ref, r0, ci)
            dws = [dw0, dw1, dw2, dw3 + jnp.sum(du * cur, axis=0, keepdims=True)]
            for k in range(3):
                dws[k] = dws[k] + jnp.sum(du * _shift_rows_down(ext, 3 - k), axis=0, keepdims=True)
            return (dh_c, a[0:1, :], du[0:SUB, :], dsp, dba, dbx, dcb, dws[0], dws[1], dws[2], dws[3])

        init = (zero_row, zero_row, jnp.zeros((SUB, LRU_G), f32)) + (zero_row,) * 8
        out = lax.fori_loop(0, n_chunk, chunk, init)
        _, _, _, dsp, dba, dbx, dcb, dw0, dw1, dw2, dw3 = out
        dlam_ref[...] = dsp * (-_sigmoid(-lam_v))
        dba_ref[...] = dba
        dbx_ref[...] = dbx
        dcb_ref[...] = dcb
        dcw_ref[...] = jnp.concatenate([dw0, dw1, dw2, dw3], axis=0)
        ub = u_ref[...].astype(bf16)
        dwa_ref[...] = lax.dot_general(ub, dpr_s[...], _TN, preferred_element_type=f32)
        dwx_ref[...] = lax.dot_general(ub, dpx_s[...], _TN, preferred_element_type=f32)

    seq = lambda cb: pl.BlockSpec((t, LRU_G), lambda c, cb=cb: (0, c + cb))
    rowc = pl.BlockSpec((1, LRU_G), lambda c: (0, c))
    diag = pl.BlockSpec((LRU_G, LRU_G), lambda c: (c, c))
    gate_out = pl.BlockSpec((None, LRU_G, LRU_G), lambda c: (c, 0, 0))
    row_shape = jax.ShapeDtypeStruct((1, LRU_W), f32)
    return _hosted_call(
        host, body, name=name, grid=(LRU_W // LRU_G,),
        in_specs=[seq(LRU_W // LRU_G), seq(0), seq(LRU_W // LRU_G), seq(0), seq(0),
                  pl.BlockSpec((4, LRU_G), lambda c: (0, c)),
                  diag, rowc, diag, rowc, rowc],
        out_specs=[seq(0), seq(0), pl.BlockSpec((4, LRU_G), lambda c: (0, c)), rowc, rowc, rowc, rowc,
                   gate_out, gate_out],
        out_shape=[jax.ShapeDtypeStruct((t, LRU_W), bf16)] * 2
        + [jax.ShapeDtypeStruct((4, LRU_W), f32)] + [row_shape] * 4
        + [jax.ShapeDtypeStruct((LRU_W // LRU_G, LRU_G, LRU_G), f32)] * 2,
        scratch_shapes=[pltpu.VMEM((t, LRU_G), bf16), pltpu.VMEM((t, LRU_G), bf16)],
        compiler_params=_params(("arbitrary",)),
    )(dmix, zl, zl, u_all, h_all, conv_w, wa, ba, wx, bx, lam)


def _pack_rows(a):
    flat = a.reshape(-1)
    rows = -(-flat.shape[0] // LANES)
    return jnp.pad(flat, (0, rows * LANES - flat.shape[0])).reshape(rows, LANES)


W_IN_PAD = 21 * LANES


def _w_in_join(blocks, *, name):
    tm = 256

    def body(b_ref, o_ref):
        o_ref[:, IN_COLS:] = jnp.zeros((tm, W_IN_PAD - IN_COLS), bf16)
        for q in range(N_DEV):
            o_ref[:, IN_SHARD * q:IN_SHARD * (q + 1)] = b_ref[q]

    return pl.pallas_call(
        body, name=name, grid=(D_MODEL // tm,),
        in_specs=[pl.BlockSpec((N_DEV, tm, IN_SHARD), lambda i: (0, i, 0))],
        out_specs=pl.BlockSpec((tm, W_IN_PAD), lambda i: (i, 0)),
        out_shape=jax.ShapeDtypeStruct((D_MODEL, W_IN_PAD), bf16), compiler_params=_params(("arbitrary",)),
    )(blocks)


def _w_in_split(main, fg, *, name):
    tm = 256
    n_main = main.shape[0]

    def body(m_ref, f_ref, o_ref):
        full = jnp.concatenate([m_ref[n] for n in range(n_main)] + [f_ref[...]], axis=1)
        for q in range(N_DEV):
            o_ref[q] = full[:, IN_SHARD * q:IN_SHARD * (q + 1)]

    return pl.pallas_call(
        body, name=name, grid=(D_MODEL // tm,),
        in_specs=[pl.BlockSpec((n_main, tm, 512), lambda i: (0, i, 0)), pl.BlockSpec((tm, LANES), lambda i: (i, 0))],
        out_specs=pl.BlockSpec((N_DEV, tm, IN_SHARD), lambda i: (0, i, 0)),
        out_shape=jax.ShapeDtypeStruct((N_DEV, D_MODEL, IN_SHARD), bf16), compiler_params=_params(("arbitrary",)),
    )(main, fg)


def _block_diag(w):
    eye = jnp.eye(HEADS, dtype=w.dtype)
    return jnp.einsum("hij,hk->hikj", w, eye).reshape(LRU_W, LRU_W)


def _diag_blocks(dw):
    per = dw.shape[1] // HEAD_D
    blocks = [dw[:, HEAD_D * b:HEAD_D * (b + 1), HEAD_D * b:HEAD_D * (b + 1)] for b in range(per)]
    return jnp.stack(blocks, axis=1).reshape(HEADS, HEAD_D, HEAD_D)


def _local_step(x, target, sent, small, *, tm=512, tm_ffn=1024):
    ln1 = (small["ln1_g"], small["ln1_b"])
    ln2 = (small["ln2_g"], small["ln2_b"])
    ln3 = (small["ln3_g"], small["ln3_b"])

    xh1, rs1, hg1, hu1, wg1, wu1, wd1, w_in_g, w_out_g, conv_w_g = _ffn1_fwd_gathering(
        x, (sent["ffn1_w_gate"], sent["ffn1_w_up"], sent["ffn1_w_down"]),
        _Exchange([sent["w_in"], sent["w_out"], sent["conv_w"]], gather=True), tm=tm_ffn, name="ffn1_fwd")
    w_in = _w_in_join(w_in_g, name="w_in_join")
    w_out = w_out_g.reshape(D_MODEL, D_MODEL)
    conv_w = conv_w_g.transpose(1, 0, 2).reshape(4, LRU_W)
    qkv, zl, zfg = _in_proj(xh1, ln1[0], ln1[1], w_in, tm=tm_ffn, name="in_proj")
    bfg = jnp.pad(small["b_forget"], ((0, 0), (0, LANES - HEADS)))
    cum = _cum_fwd(zfg, bfg, name="cum_fwd")
    q_aug, k_aug, v_aug = _attn_prep_fwd(qkv, cum, tm=tm_ffn, name="attn_prep_fwd")
    o, lse, wg2, wu2 = _attn_fwd2(
        q_aug, k_aug, v_aug, name="attn_fwd",
        host=_Hosts(_Exchange([sent["ffn2_w_gate"]], gather=True),
                    _Exchange([sent["ffn2_w_up"]], gather=True, direct=True)))
    wa_bd = _block_diag(small["rg_wa"]).astype(bf16)
    wx_bd = _block_diag(small["rg_wx"]).astype(bf16)
    ba = small["rg_ba"].reshape(1, LRU_W)
    bx = small["rg_bx"].reshape(1, LRU_W)
    u, h, lru, wd2 = _lru_fwd(zl, conv_w, small["conv_b"], wa_bd, ba, wx_bd, bx, small["lru_lambda"],
                              name="lru_fwd", host=_Exchange([sent["ffn2_w_down"]], gather=True))
    xh2, rs2 = _mmln([(o, 0, FOX_W, w_out, 0, D_MODEL, "nn"), (lru, 0, LRU_W, w_out, 1, D_MODEL, "nn")],
                     tm=tm_ffn, name="mix_fwd", resid=("affine", xh1) + ln1, resid_scale=ALPHA, epi="ln_fwd")
    dpre3, sq_rows, g_ln3g, g_ln3b, hg2, hu2 = _ffn_fwd_loss(
        xh2, ln2[0], ln2[1], wg2, wu2, wd2, ln3[0], ln3[1], target, tm=tm_ffn, name="ffn2_fwd_loss")

    dpre2, g_ln2g, g_ln2b, dhg2, dhu2, a2 = _ffn_bwd(dpre3, hg2, hu2, wg2, wu2, wd2,
                                                     (xh2, rs2, ln2[0]), tm=tm_ffn, name="ffn2_bwd")
    wgrad = dict(out_dtype=bf16, tm=D_MODEL, mb=1, tn=FF_TILE, nb=4, tk=512, pair=True)
    wdgrad = dict(out_dtype=bf16, tm=512, mb=4, tn=D_MODEL, nb=1, tk=512, out_scale=0.5, pair=True)
    between_chips = functools.partial(_Exchange, gather=False, chips=True)
    g_wg2 = _mm_tn(xh2, dhg2, name="g_wg2", affine=ln2, **wgrad)
    g_wu2 = _mm_tn(xh2, dhu2, name="g_wu2", affine=ln2, **wgrad)
    g_wd2 = _mm_tn(a2, dpre3, name="g_wd2", **wdgrad)

    dmix = _mmln([(dpre2, 0, D_MODEL, w_out, 0, D_MODEL, "nt")], tm=tm_ffn, name="dmix_bwd")
    g_wout_a = _mm(o, dpre2, mode="tn", out_dtype=bf16, tm=512, tn=D_MODEL, tk=512, name="g_wout_fox")
    g_wout_b = _mm(lru, dpre2, mode="tn", out_dtype=bf16, tm=512, tn=D_MODEL, tk=512, name="g_wout_lru")
    g_wout_blocked = jnp.concatenate([g_wout_a, g_wout_b], axis=0).reshape(N_DEV, D_MODEL // N_DEV, D_MODEL)
    dlx, dlg, g_cw, g_cb, g_ba, g_bx, g_lam, g_wa4, g_wx4, p_wg2, p_wout = _lru_bwd(
        dmix, zl, u, h, conv_w, wa_bd, ba, wx_bd, bx, small["lru_lambda"], name="lru_bwd",
        host=_Hosts(between_chips([g_wg2]), _Exchange([g_wout_blocked], gather=False)))
    qb_aug, do_aug = _attn_prep_bwd(qkv, cum, lse, dmix, o, tm=tm_ffn, name="attn_prep_bwd")
    dq, dcum_q, dk, dv, dcum_k, p_wu2, p_wd2 = _attn_bwd(qb_aug, k_aug, v_aug, do_aug, name="attn_bwd",
                                                         host=between_chips([g_wu2, g_wd2]))
    dfg, g_bf = _cum_bwd(dcum_q, dcum_k, zfg, bfg, name="cum_bwd")

    dz = [(dq, 0, 512), (dk, 1, 512), (dv, 2, 512), (dlx, 3, 512), (dlg, 4, 512), (dfg, 20, LANES)]
    dpre1, g_ln1g, g_ln1b = _mmln(
        [(arr, 0, w, w_in, cb, w, "nt") for (arr, cb, w) in dz],
        tm=tm, name="dx1_bwd", resid=("plain", dpre2), resid_scale=ALPHA, epi="ln_bwd", ln=(xh1, rs1, ln1[0]))
    g_win_main = _mm_tn(xh1, [arr for arr, _, _ in dz[:5]], out_dtype=bf16, tm=D_MODEL, mb=1, tn=512, nb=5, tk=512,
                        name="g_win", affine=ln1, out_blocked=True)
    g_win_fg = _mm(xh1, dfg, mode="tn", out_dtype=bf16, tm=D_MODEL, tn=LANES, tk=512, name="g_win_fg", affine=ln1)
    g_win_blocked = _w_in_split(g_win_main, g_win_fg, name="w_in_split")
    dhg1, dhu1, a1, p_win = _ffn_bwd_act(dpre1, hg1, hu1, wd1, tm=tm_ffn, name="ffn1_bwd_act",
                                         host=_Exchange([g_win_blocked], gather=False))
    small_g = {
        "ln1_g": g_ln1g, "ln1_b": g_ln1b, "b_forget": g_bf[:, :HEADS], "conv_w": g_cw, "conv_b": g_cb,
        "rg_wa": _diag_blocks(g_wa4), "rg_ba": g_ba.reshape(HEADS, HEAD_D),
        "rg_wx": _diag_blocks(g_wx4), "rg_bx": g_bx.reshape(HEADS, HEAD_D), "lru_lambda": g_lam,
        "ln2_g": g_ln2g, "ln2_b": g_ln2b, "ln3_g": g_ln3g, "ln3_b": g_ln3b,
    }
    small_g["loss"] = (0.5 / D_MODEL) * jnp.sum(sq_rows, keepdims=True)
    pieces = [_pack_rows(small_g[n]) for n in PACKED]
    packed = jnp.concatenate(pieces + [jnp.zeros((PACK_ROWS - sum(p.shape[0] for p in pieces), LANES), f32)])
    g_wg1, all_packed = _mm_tn(x, dhg1, name="g_wg1", host=_Exchange([packed], gather=True), **wgrad)
    g_wu1, p_wg1 = _mm_tn(x, dhu1, name="g_wu1", host=between_chips([g_wg1]), **wgrad)
    g_wd1, p_wu1 = _mm_tn(a1, dpre1, name="g_wd1", host=between_chips([g_wu1]), **wdgrad)
    grad_x, p_wd1 = _ffn_bwd_dx(dpre1, dhg1, dhu1, wg1, wu1, tm=tm_ffn, name="ffn1_bwd_dx",
                                host=between_chips([g_wd1]))
    parts = {
        "ffn1_w_gate": p_wg1, "ffn1_w_up": p_wu1, "ffn1_w_down": p_wd1, "w_in": p_win, "w_out": p_wout,
        "ffn2_w_gate": p_wg2, "ffn2_w_up": p_wu2, "ffn2_w_down": p_wd2,
    }
    return grad_x, parts, all_packed, {n: small_g[n].shape for n in PACKED}


def _adam_math(w, g, m, v):
    m2 = ADAM_B1 * m + (1.0 - ADAM_B1) * g
    v2 = ADAM_B2 * v + (1.0 - ADAM_B2) * (g * g)
    m_hat = m2 / (1.0 - ADAM_B1 ** ADAM_STEP)
    v_hat = v2 / (1.0 - ADAM_B2 ** ADAM_STEP)
    delta = -ADAM_LR * (m_hat / (jnp.sqrt(v_hat) + ADAM_EPS) + ADAM_WD * w)
    return delta, m2, v2


ADAM_TILE_ELEMS = 128 * 1024


def _adamw_big(items, *, name):
    _, r, c = items[0][1].shape
    n_parts = items[0][0].shape[0]
    n_items = len(items)
    assert all(it[1].shape == (1, r, c) and it[0].shape == (n_parts, r, c) for it in items), name
    tr = max(d for d in range(8, r + 1, 8) if r % d == 0 and d * c <= ADAM_TILE_ELEMS)

    def body(*refs):
        ins, outs = refs[:4 * n_items], refs[4 * n_items:]
        for k in range(n_items):
            p_ref, w_ref, m_ref, v_ref = ins[4 * k:4 * k + 4]
            g = p_ref[0].astype(f32)
            for q in range(1, n_parts):
                g = g + p_ref[q].astype(f32)
            d, m2, v2 = _adam_math(w_ref[...], g, m_ref[...], v_ref[...])
            for o_ref, val in zip(outs[4 * k:4 * k + 4], (g, d, m2, v2)):
                o_ref[...] = val

    blk = pl.BlockSpec((None, tr, c), lambda i: (0, i, 0))
    res = pl.pallas_call(
        body, name=name, grid=(r // tr,),
        in_specs=[pl.BlockSpec((n_parts, tr, c), lambda i: (0, i, 0)), blk, blk, blk] * n_items,
        out_specs=[blk] * (4 * n_items), out_shape=[jax.ShapeDtypeStruct((1, r, c), f32)] * (4 * n_items),
        compiler_params=_params(("arbitrary",)),
    )(*[a for it in items for a in it])
    return [res[4 * k:4 * k + 4] for k in range(n_items)]


def _adamw_small(items, *, name):
    n = len(items)

    def body(*refs):
        ins, outs = refs[:4 * n], refs[4 * n:]
        for k in range(n):
            g, w, m, v = (ins[4 * k + q][...] for q in range(4))
            d, m2, v2 = _adam_math(w, g, m, v)
            outs[3 * k][...] = d
            outs[3 * k + 1][...] = m2
            outs[3 * k + 2][...] = v2

    vm = pl.BlockSpec(memory_space=pltpu.VMEM)
    flat = [a for item in items for a in item]
    out_shape = [jax.ShapeDtypeStruct(item[1].shape, f32) for item in items for _ in range(3)]
    return pl.pallas_call(
        body, name=name, in_specs=[vm] * (4 * n), out_specs=[vm] * (3 * n), out_shape=out_shape,
    )(*flat)


def _sum_parts(parts, *, name):
    def body(p_ref, o_ref):
        acc = p_ref[0]
        for q in range(1, N_DEV):
            acc = acc + p_ref[q]
        o_ref[...] = acc

    vm = pl.BlockSpec(memory_space=pltpu.VMEM)
    return pl.pallas_call(
        body, name=name, in_specs=[vm], out_specs=vm, out_shape=jax.ShapeDtypeStruct(parts.shape[1:], f32),
    )(parts)


WEIGHTS = ["ffn1_w_gate", "ffn1_w_up", "ffn1_w_down", "ln1_g", "ln1_b", "w_in", "b_forget", "conv_w", "conv_b",
           "rg_wa", "rg_ba", "rg_wx", "rg_bx", "lru_lambda", "w_out", "ln2_g", "ln2_b",
           "ffn2_w_gate", "ffn2_w_up", "ffn2_w_down", "ln3_g", "ln3_b"]
BIG = ["ffn1_w_gate", "ffn1_w_up", "ffn1_w_down", "w_in", "w_out", "ffn2_w_gate", "ffn2_w_up", "ffn2_w_down"]
PACKED = ["ln1_g", "ln1_b", "ln2_g", "ln2_b", "ln3_g", "ln3_b", "conv_b", "rg_ba", "rg_bx", "lru_lambda",
          "conv_w", "rg_wa", "rg_wx", "b_forget", "loss"]
PACK_ROWS = 600


def _two_d(a):
    return a.reshape((-1, a.shape[-1]))


def _transport(a):
    return _two_d(a)


def kernel(x, ffn1_w_gate, ffn1_w_up, ffn1_w_down, ln1_g, ln1_b, w_in, b_forget, conv_w, conv_b, rg_wa, rg_ba, rg_wx, rg_bx, lru_lambda, w_out, ln2_g, ln2_b, ffn2_w_gate, ffn2_w_up, ffn2_w_down, ln3_g, ln3_b, loss_target, m_ffn1_w_gate, m_ffn1_w_up, m_ffn1_w_down, m_ln1_g, m_ln1_b, m_w_in, m_b_forget, m_conv_w, m_conv_b, m_rg_wa, m_rg_ba, m_rg_wx, m_rg_bx, m_lru_lambda, m_w_out, m_ln2_g, m_ln2_b, m_ffn2_w_gate, m_ffn2_w_up, m_ffn2_w_down, m_ln3_g, m_ln3_b, v_ffn1_w_gate, v_ffn1_w_up, v_ffn1_w_down, v_ln1_g, v_ln1_b, v_w_in, v_b_forget, v_conv_w, v_conv_b, v_rg_wa, v_rg_ba, v_rg_wx, v_rg_bx, v_lru_lambda, v_w_out, v_ln2_g, v_ln2_b, v_ffn2_w_gate, v_ffn2_w_up, v_ffn2_w_down, v_ln3_g, v_ln3_b):
    w_args = (ffn1_w_gate, ffn1_w_up, ffn1_w_down, ln1_g, ln1_b, w_in, b_forget, conv_w, conv_b, rg_wa, rg_ba, rg_wx, rg_bx, lru_lambda, w_out, ln2_g, ln2_b, ffn2_w_gate, ffn2_w_up, ffn2_w_down, ln3_g, ln3_b)
    m_args = (m_ffn1_w_gate, m_ffn1_w_up, m_ffn1_w_down, m_ln1_g, m_ln1_b, m_w_in, m_b_forget, m_conv_w, m_conv_b, m_rg_wa, m_rg_ba, m_rg_wx, m_rg_bx, m_lru_lambda, m_w_out, m_ln2_g, m_ln2_b, m_ffn2_w_gate, m_ffn2_w_up, m_ffn2_w_down, m_ln3_g, m_ln3_b)
    v_args = (v_ffn1_w_gate, v_ffn1_w_up, v_ffn1_w_down, v_ln1_g, v_ln1_b, v_w_in, v_b_forget, v_conv_w, v_conv_b, v_rg_wa, v_rg_ba, v_rg_wx, v_rg_bx, v_lru_lambda, v_w_out, v_ln2_g, v_ln2_b, v_ffn2_w_gate, v_ffn2_w_up, v_ffn2_w_down, v_ln3_g, v_ln3_b)
    w = dict(zip(WEIGHTS, w_args))
    m = dict(zip(WEIGHTS, m_args))
    v = dict(zip(WEIGHTS, v_args))
    me = 4 * lax.axis_index("x") + 2 * lax.axis_index("y") + lax.axis_index("c")

    sent = {n: _transport(w[n]).astype(bf16) for n in BIG}
    sent["conv_w"] = _two_d(w["conv_w"])
    small = {n: w[n] for n in ("ln1_g", "ln1_b", "ln2_g", "ln2_b", "ln3_g", "ln3_b", "b_forget", "conv_b",
                               "lru_lambda")}
    small.update({n: w[n][0] for n in ("rg_wa", "rg_ba", "rg_wx", "rg_bx")})

    grad_x, parts, all_packed, small_shapes = _local_step(x[0], loss_target[0], sent, small)

    total = _sum_parts(all_packed, name="sum_small_grads")
    grads, off = {}, 0
    for n in PACKED:
        size = math.prod(small_shapes[n])
        rows = -(-size // LANES)
        grads[n] = total[off:off + rows].reshape(-1)[:size].reshape(small_shapes[n])
        off += rows
    loss = grads.pop("loss").reshape(())
    grads["conv_w"] = lax.dynamic_slice_in_dim(grads["conv_w"], me * (LRU_W // N_DEV), LRU_W // N_DEV, axis=1)

    delta, new_m, new_v = {}, {}, {}
    for group in (("ffn1_w_gate", "ffn1_w_up", "ffn2_w_gate", "ffn2_w_up"), ("ffn1_w_down", "ffn2_w_down"),
                  ("w_in",), ("w_out",)):
        done = _adamw_big([(parts[n], w[n], m[n], v[n]) for n in group], name="adamw_" + group[0])
        for n, (g, d, m2, v2) in zip(group, done):
            grads[n], delta[n], new_m[n], new_v[n] = g, d, m2, v2
    small_names = [n for n in WEIGHTS if n not in BIG]
    outs = _adamw_small([(_two_d(grads[n]), _two_d(w[n]), _two_d(m[n]), _two_d(v[n])) for n in small_names],
                        name="adamw_small")
    for k, n in enumerate(small_names):
        delta[n], new_m[n], new_v[n] = outs[3 * k], outs[3 * k + 1], outs[3 * k + 2]

    def shaped(d):
        return [d[n].reshape(w[n].shape) for n in WEIGHTS]

    return (loss, grad_x[None], *shaped(grads), *shaped(delta), *shaped(new_m), *shaped(new_v))
```
